```python
import math
import jax
import jax.numpy as jnp
from jax import lax
import numpy as np

D_MODEL = 1024
BATCH = 8
SEQ = 4096
DEPTH = 2

CHUNK = 64
Q_BLOCK = 128

N_MIXERS = 4
HEAD_DIM = 64
GROUP_WIDTH = D_MODEL // N_MIXERS
N_HEADS_GROUP = GROUP_WIDTH // HEAD_DIM
D_MIX = N_MIXERS * GROUP_WIDTH

MLA_Q_RANK = D_MODEL // 4
MLA_KV_RANK = D_MODEL // 8
MLA_NOPE_DIM = HEAD_DIM
MLA_ROPE_DIM = HEAD_DIM // 2
MLA_V_DIM = HEAD_DIM

RET_DECAY_OFFSET = 5.0

D_FF = 4 * D_MODEL

ROPE_BASE = 10000.0
EPS = 1e-6
FORGET_BIAS = 4.0

D_IN_PROJ = 10 * GROUP_WIDTH + N_HEADS_GROUP + MLA_Q_RANK + MLA_KV_RANK + MLA_ROPE_DIM

kernel_name = 'hybrid_fox_mla_retention_stickbreaking_trunk'

F32 = jnp.float32


def rms_norm(x, gain):
    xf = x.astype(F32)
    y = xf * lax.rsqrt(jnp.mean(xf * xf, axis=-1, keepdims=True) + EPS)
    return (y * gain.astype(F32)).astype(x.dtype)


def head_group_norm(o, gain):
    mu = jnp.mean(o, axis=-1, keepdims=True)
    var = jnp.mean(jnp.square(o - mu), axis=-1, keepdims=True)
    y = (o - mu) * lax.rsqrt(var + EPS)
    return y * gain.astype(F32).reshape(o.shape[2], o.shape[3])


def apply_rope(x, positions):
    half = x.shape[-1] // 2
    inv_freq = ROPE_BASE ** (-jnp.arange(half, dtype=F32) / half)
    ang = positions.astype(F32)[:, :, None, None] * inv_freq
    cos, sin = jnp.cos(ang), jnp.sin(ang)
    xf = x.astype(F32)
    x1, x2 = xf[..., :half], xf[..., half:]
    return jnp.concatenate([x1 * cos - x2 * sin, x1 * sin + x2 * cos], axis=-1).astype(x.dtype)


def to_blocks(t):
    b, s = t.shape[:2]
    return jnp.moveaxis(t.reshape((b, s // Q_BLOCK, Q_BLOCK) + t.shape[2:]), 1, 0)


def from_blocks(t):
    nb, b, qb = t.shape[:3]
    return jnp.moveaxis(t, 0, 1).reshape((b, nb * qb) + t.shape[3:])


def forgetting_attention(q, k, v, f_logit):
    seq = q.shape[1]
    scale = q.shape[-1] ** -0.5
    cum = jnp.cumsum(jax.nn.log_sigmoid(f_logit.astype(F32)), axis=1)
    cum_k = jnp.swapaxes(cum, 1, 2)
    k_pos = jnp.arange(seq)
    q_pos = k_pos.reshape(-1, Q_BLOCK)

    def block(xs):
        qb, cqb, qp = xs
        s = jnp.einsum('bqhd,bkhd->bhqk', qb, k, preferred_element_type=F32) * scale
        s = s + jnp.swapaxes(cqb, 1, 2)[..., None] - cum_k[:, :, None, :]
        s = jnp.where(k_pos[None, :] <= qp[:, None], s, -jnp.inf)
        p = jax.nn.softmax(s, axis=-1)
        return jnp.einsum('bhqk,bkhd->bqhd', p.astype(v.dtype), v)

    return from_blocks(lax.map(block, (to_blocks(q), to_blocks(cum), q_pos)))


def chunk_causal_softmax_attention(q, k, v):
    seq = q.shape[1]
    scale = q.shape[-1] ** -0.5
    k_chunk = jnp.arange(seq) // CHUNK
    q_pos = jnp.arange(seq).reshape(-1, Q_BLOCK)

    def block(xs):
        qb, qp = xs
        s = jnp.einsum('bqhd,bkhd->bhqk', qb, k, preferred_element_type=F32) * scale
        s = jnp.where(k_chunk[None, :] <= (qp // CHUNK)[:, None], s, -jnp.inf)
        p = jax.nn.softmax(s, axis=-1)
        return jnp.einsum('bhqk,bkhd->bqhd', p.astype(v.dtype), v)

    return from_blocks(lax.map(block, (to_blocks(q), q_pos)))


def stick_breaking_attention(q, k, v):
    seq = q.shape[1]
    scale = q.shape[-1] ** -0.5
    k_pos = jnp.arange(seq)
    q_pos = k_pos.reshape(-1, Q_BLOCK)

    def block(xs):
        qb, qp = xs
        z = jnp.einsum('bqhd,bkhd->bhqk', qb, k, preferred_element_type=F32) * scale
        visible = k_pos[None, :] < qp[:, None]
        log_stay = jnp.where(visible, jax.nn.log_sigmoid(-z), 0.0)
        later = lax.cumsum(log_stay, axis=3, reverse=True) - log_stay
        w = jnp.where(visible, jnp.exp(jax.nn.log_sigmoid(z) + later), 0.0)
        return jnp.einsum('bhqk,bkhd->bqhd', w.astype(v.dtype), v)

    return from_blocks(lax.map(block, (to_blocks(q), q_pos)))


def chunkwise_retention(q, k, v, positions):
    b, seq, h, d = q.shape
    n = seq // CHUNK
    qf = apply_rope(q, positions).astype(F32)
    kf = apply_rope(k, positions).astype(F32) * (d ** -0.5)
    vf = v.astype(F32)
    log_gamma = jnp.log1p(-jnp.power(2.0, -RET_DECAY_OFFSET - jnp.arange(h, dtype=F32)))
    idx = jnp.arange(CHUNK, dtype=F32)
    qc = qf.reshape(b, n, CHUNK, h, d)
    kc = kf.reshape(b, n, CHUNK, h, d)
    vc = vf.reshape(b, n, CHUNK, h, d)
    intra_decay = jnp.exp(log_gamma[:, None, None] * jnp.abs(idx[:, None] - idx[None, :]))
    scores = jnp.einsum('bncht,bnmht->bnhcm', qc, kc) * intra_decay
    intra = jnp.einsum('bnhcm,bnmhe->bnche', scores, vc)
    k_tail = kc * jnp.exp(log_gamma[None, :] * (CHUNK - 1 - idx)[:, None])[None, None, :, :, None]
    chunk_kv = jnp.einsum('bnmht,bnmhe->nbhte', k_tail, vc)
    chunk_decay = jnp.exp(log_gamma * CHUNK)[None, :, None, None]

    def step(state, kv):
        return state * chunk_decay + kv, state

    _, prev_state = lax.scan(step, jnp.zeros((b, h, d, d), F32), chunk_kv)
    q_head = qc * jnp.exp(log_gamma[None, :] * (idx + 1.0)[:, None])[None, None, :, :, None]
    inter = jnp.einsum('bncht,nbhte->bnche', q_head, prev_state)
    return (intra + inter).reshape(b, seq, h, d)


def split_columns(proj):
    sizes = [GROUP_WIDTH, GROUP_WIDTH, GROUP_WIDTH, N_HEADS_GROUP,
             MLA_Q_RANK, MLA_KV_RANK, MLA_ROPE_DIM,
             GROUP_WIDTH, GROUP_WIDTH, GROUP_WIDTH, GROUP_WIDTH,
             GROUP_WIDTH, GROUP_WIDTH, GROUP_WIDTH]
    offsets = [int(o) for o in np.cumsum(sizes)[:-1]]
    return jnp.split(proj, offsets, axis=-1)


def hybrid_mixer(h, positions, w_in, b_forget, g_q_lora, w_q_up, g_kv_lora, w_kv_up, g_mix_out, w_out):
    b, seq, _ = h.shape
    heads = lambda t: t.reshape(b, seq, N_HEADS_GROUP, -1)
    proj = jnp.einsum('bsd,dn->bsn', h, w_in)
    (fq, fk, fv, ff, cq, ckv, kr, rq, rk, rv, rg, sq, sk, sv) = split_columns(proj)

    out_a = forgetting_attention(heads(fq), heads(fk), heads(fv), ff + b_forget)
    out_a = rms_norm(out_a.reshape(b, seq, GROUP_WIDTH), g_mix_out[0:GROUP_WIDTH])

    q = jnp.einsum('bsr,rn->bsn', rms_norm(cq, g_q_lora), w_q_up).reshape(b, seq, N_HEADS_GROUP, MLA_NOPE_DIM + MLA_ROPE_DIM)
    q = jnp.concatenate([q[..., :MLA_NOPE_DIM], apply_rope(q[..., MLA_NOPE_DIM:], positions)], axis=-1)
    kv = jnp.einsum('bsr,rn->bsn', rms_norm(ckv, g_kv_lora), w_kv_up).reshape(b, seq, N_HEADS_GROUP, MLA_NOPE_DIM + MLA_V_DIM)
    k_rope = apply_rope(kr[:, :, None, :], positions)
    k = jnp.concatenate([kv[..., :MLA_NOPE_DIM], jnp.broadcast_to(k_rope, (b, seq, N_HEADS_GROUP, MLA_ROPE_DIM))], axis=-1)
    out_b = chunk_causal_softmax_attention(q, k, kv[..., MLA_NOPE_DIM:])
    out_b = rms_norm(out_b.reshape(b, seq, GROUP_WIDTH), g_mix_out[GROUP_WIDTH:2 * GROUP_WIDTH])

    ret = chunkwise_retention(heads(rq), heads(rk), heads(rv), positions)
    ret = head_group_norm(ret, g_mix_out[2 * GROUP_WIDTH:3 * GROUP_WIDTH]).reshape(b, seq, GROUP_WIDTH)
    out_c = (ret * jax.nn.silu(rg.astype(F32))).astype(h.dtype)

    out_d = stick_breaking_attention(heads(sq), heads(sk), heads(sv))
    out_d = rms_norm(out_d.reshape(b, seq, GROUP_WIDTH), g_mix_out[3 * GROUP_WIDTH:])

    mixed = jnp.concatenate([out_a, out_b, out_c, out_d], axis=-1)
    return jnp.einsum('bsn,nd->bsd', mixed, w_out)


def squared_relu_mlp(h, w_up, w_down):
    u = jnp.square(jax.nn.relu(jnp.einsum('bsd,df->bsf', h, w_up)))
    return jnp.einsum('bsf,fd->bsd', u, w_down)


def _fwd_setup_inputs(seed: int = 0) -> dict:
    key = jax.random.key(seed)
    ks = jax.random.split(key, 20)
    nrm = lambda k, shape, fan_in: jax.random.normal(k, shape, F32) * (fan_in ** -0.5)
    gain = lambda k, shape: 1.0 + 0.05 * jax.random.normal(k, shape, F32)
    x = jax.random.normal(ks[0], (BATCH, SEQ, D_MODEL), F32)
    start = jax.random.randint(ks[1], (BATCH, 1), 0, 1024, dtype=jnp.int32)
    positions = start + jnp.arange(SEQ, dtype=jnp.int32)[None, :]
    return {
        'x': x,
        'positions': positions,
        'g_mix_pre': gain(ks[2], (DEPTH, D_MODEL)),
        'w_in': nrm(ks[3], (DEPTH, D_MODEL, D_IN_PROJ), D_MODEL),
        'b_forget': FORGET_BIAS + 0.5 * jax.random.normal(ks[4], (DEPTH, N_HEADS_GROUP), F32),
        'g_q_lora': gain(ks[5], (DEPTH, MLA_Q_RANK)),
        'w_q_up': nrm(ks[6], (DEPTH, MLA_Q_RANK, N_HEADS_GROUP * (MLA_NOPE_DIM + MLA_ROPE_DIM)), MLA_Q_RANK),
        'g_kv_lora': gain(ks[7], (DEPTH, MLA_KV_RANK)),
        'w_kv_up': nrm(ks[8], (DEPTH, MLA_KV_RANK, N_HEADS_GROUP * (MLA_NOPE_DIM + MLA_V_DIM)), MLA_KV_RANK),
        'g_mix_out': gain(ks[9], (DEPTH, D_MIX)),
        'w_out': nrm(ks[10], (DEPTH, D_MIX, D_MODEL), D_MIX),
        'g_mix_post': gain(ks[11], (DEPTH, D_MODEL)),
        'g_ffn_pre': gain(ks[12], (DEPTH, D_MODEL)),
        'w_ffn_up': nrm(ks[13], (DEPTH, D_MODEL, D_FF), D_MODEL),
        'w_ffn_down': nrm(ks[14], (DEPTH, D_FF, D_MODEL), D_FF),
        'g_ffn_post': gain(ks[15], (DEPTH, D_MODEL)),
    }


def _fwd_reference(x, positions, g_mix_pre, w_in, b_forget, g_q_lora, w_q_up, g_kv_lora, w_kv_up,
              g_mix_out, w_out, g_mix_post, g_ffn_pre, w_ffn_up, w_ffn_down, g_ffn_post):
    for layer in range(DEPTH):
        h = rms_norm(x, g_mix_pre[layer])
        mix = hybrid_mixer(h, positions, w_in[layer], b_forget[layer], g_q_lora[layer], w_q_up[layer],
                           g_kv_lora[layer], w_kv_up[layer], g_mix_out[layer], w_out[layer])
        x = x + rms_norm(mix, g_mix_post[layer])
        h = rms_norm(x, g_ffn_pre[layer])
        x = x + rms_norm(squared_relu_mlp(h, w_ffn_up[layer], w_ffn_down[layer]), g_ffn_post[layer])
    return x


import jax as _jax
import jax.numpy as _jnp

TWIN_FORMAT = 'train_step'
FWD_PARAMS = ['x', 'positions', 'g_mix_pre', 'w_in', 'b_forget', 'g_q_lora', 'w_q_up', 'g_kv_lora', 'w_kv_up', 'g_mix_out', 'w_out', 'g_mix_post', 'g_ffn_pre', 'w_ffn_up', 'w_ffn_down', 'g_ffn_post']
TWIN_WEIGHTS = ['g_mix_pre', 'w_in', 'b_forget', 'g_q_lora', 'w_q_up', 'g_kv_lora', 'w_kv_up', 'g_mix_out', 'w_out', 'g_mix_post', 'g_ffn_pre', 'w_ffn_up', 'w_ffn_down', 'g_ffn_post']
TWIN_DIFF_INPUT = 'x'
TWIN_INPUTS = ['x', 'positions', 'g_mix_pre', 'w_in', 'b_forget', 'g_q_lora', 'w_q_up', 'g_kv_lora', 'w_kv_up', 'g_mix_out', 'w_out', 'g_mix_post', 'g_ffn_pre', 'w_ffn_up', 'w_ffn_down', 'g_ffn_post', 'loss_target', 'm_g_mix_pre', 'm_w_in', 'm_b_forget', 'm_g_q_lora', 'm_w_q_up', 'm_g_kv_lora', 'm_w_kv_up', 'm_g_mix_out', 'm_w_out', 'm_g_mix_post', 'm_g_ffn_pre', 'm_w_ffn_up', 'm_w_ffn_down', 'm_g_ffn_post', 'v_g_mix_pre', 'v_w_in', 'v_b_forget', 'v_g_q_lora', 'v_w_q_up', 'v_g_kv_lora', 'v_w_kv_up', 'v_g_mix_out', 'v_w_out', 'v_g_mix_post', 'v_g_ffn_pre', 'v_w_ffn_up', 'v_w_ffn_down', 'v_g_ffn_post']
TWIN_OUTPUTS = ['loss', 'grad_x', 'grad_g_mix_pre', 'grad_w_in', 'grad_b_forget', 'grad_g_q_lora', 'grad_w_q_up', 'grad_g_kv_lora', 'grad_w_kv_up', 'grad_g_mix_out', 'grad_w_out', 'grad_g_mix_post', 'grad_g_ffn_pre', 'grad_w_ffn_up', 'grad_w_ffn_down', 'grad_g_ffn_post', 'delta_g_mix_pre', 'delta_w_in', 'delta_b_forget', 'delta_g_q_lora', 'delta_w_q_up', 'delta_g_kv_lora', 'delta_w_kv_up', 'delta_g_mix_out', 'delta_w_out', 'delta_g_mix_post', 'delta_g_ffn_pre', 'delta_w_ffn_up', 'delta_w_ffn_down', 'delta_g_ffn_post', 'new_m_g_mix_pre', 'new_m_w_in', 'new_m_b_forget', 'new_m_g_q_lora', 'new_m_w_q_up', 'new_m_g_kv_lora', 'new_m_w_kv_up', 'new_m_g_mix_out', 'new_m_w_out', 'new_m_g_mix_post', 'new_m_g_ffn_pre', 'new_m_w_ffn_up', 'new_m_w_ffn_down', 'new_m_g_ffn_post', 'new_v_g_mix_pre', 'new_v_w_in', 'new_v_b_forget', 'new_v_g_q_lora', 'new_v_w_q_up', 'new_v_g_kv_lora', 'new_v_w_kv_up', 'new_v_g_mix_out', 'new_v_w_out', 'new_v_g_mix_post', 'new_v_g_ffn_pre', 'new_v_w_ffn_up', 'new_v_w_ffn_down', 'new_v_g_ffn_post']
TWIN_LEAF_KINDS = {'loss': 'loss', 'grad_x': 'grad_x', 'grad_g_mix_pre': 'grad_w', 'grad_w_in': 'grad_w', 'grad_b_forget': 'grad_w', 'grad_g_q_lora': 'grad_w', 'grad_w_q_up': 'grad_w', 'grad_g_kv_lora': 'grad_w', 'grad_w_kv_up': 'grad_w', 'grad_g_mix_out': 'grad_w', 'grad_w_out': 'grad_w', 'grad_g_mix_post': 'grad_w', 'grad_g_ffn_pre': 'grad_w', 'grad_w_ffn_up': 'grad_w', 'grad_w_ffn_down': 'grad_w', 'grad_g_ffn_post': 'grad_w', 'delta_g_mix_pre': 'delta_w', 'delta_w_in': 'delta_w', 'delta_b_forget': 'delta_w', 'delta_g_q_lora': 'delta_w', 'delta_w_q_up': 'delta_w', 'delta_g_kv_lora': 'delta_w', 'delta_w_kv_up': 'delta_w', 'delta_g_mix_out': 'delta_w', 'delta_w_out': 'delta_w', 'delta_g_mix_post': 'delta_w', 'delta_g_ffn_pre': 'delta_w', 'delta_w_ffn_up': 'delta_w', 'delta_w_ffn_down': 'delta_w', 'delta_g_ffn_post': 'delta_w', 'new_m_g_mix_pre': 'new_m', 'new_m_w_in': 'new_m', 'new_m_b_forget': 'new_m', 'new_m_g_q_lora': 'new_m', 'new_m_w_q_up': 'new_m', 'new_m_g_kv_lora': 'new_m', 'new_m_w_kv_up': 'new_m', 'new_m_g_mix_out': 'new_m', 'new_m_w_out': 'new_m', 'new_m_g_mix_post': 'new_m', 'new_m_g_ffn_pre': 'new_m', 'new_m_w_ffn_up': 'new_m', 'new_m_w_ffn_down': 'new_m', 'new_m_g_ffn_post': 'new_m', 'new_v_g_mix_pre': 'new_v', 'new_v_w_in': 'new_v', 'new_v_b_forget': 'new_v', 'new_v_g_q_lora': 'new_v', 'new_v_w_q_up': 'new_v', 'new_v_g_kv_lora': 'new_v', 'new_v_w_kv_up': 'new_v', 'new_v_g_mix_out': 'new_v', 'new_v_w_out': 'new_v', 'new_v_g_mix_post': 'new_v', 'new_v_g_ffn_pre': 'new_v', 'new_v_w_ffn_up': 'new_v', 'new_v_w_ffn_down': 'new_v', 'new_v_g_ffn_post': 'new_v'}


def _forward(args):
    return _fwd_reference(*[args[k] for k in FWD_PARAMS])


def _output_shape():
    out = _jax.eval_shape(lambda: _forward(_fwd_setup_inputs(0)))
    return out.shape, out.dtype

N_MICROBATCH = 1
ADAM_LR = 0.001
ADAM_B1 = 0.9
ADAM_B2 = 0.999
ADAM_EPS = 1e-08
ADAM_WD = 0.01
ADAM_STEP = 10
PER_EXAMPLE_BATCH_AXIS = {'x': 0, 'positions': 0, 'loss_target': 0}
SHARED_INPUTS = []
_WEIGHT_DTYPES = {'g_mix_pre': _jnp.float32, 'w_in': _jnp.float32, 'b_forget': _jnp.float32, 'g_q_lora': _jnp.float32, 'w_q_up': _jnp.float32, 'g_kv_lora': _jnp.float32, 'w_kv_up': _jnp.float32, 'g_mix_out': _jnp.float32, 'w_out': _jnp.float32, 'g_mix_post': _jnp.float32, 'g_ffn_pre': _jnp.float32, 'w_ffn_up': _jnp.float32, 'w_ffn_down': _jnp.float32, 'g_ffn_post': _jnp.float32}
MOMENT_SCALE = {'g_mix_pre': 7.112039e+00, 'w_in': 4.563958e+00, 'b_forget': 5.529498e+00, 'g_q_lora': 1.137465e+00, 'w_q_up': 8.639294e-01, 'g_kv_lora': 1.429796e+01, 'w_kv_up': 7.935051e+00, 'g_mix_out': 6.871205e+00, 'w_out': 8.165999e+00, 'g_mix_post': 3.442473e+01, 'g_ffn_pre': 4.523304e+00, 'w_ffn_up': 2.038759e+00, 'w_ffn_down': 1.337688e+01, 'g_ffn_post': 3.439906e+01}


def _to_microbatches(a, axis):
    t = _jnp.moveaxis(a, axis, 0)
    t = t.reshape((N_MICROBATCH, t.shape[0] // N_MICROBATCH) + t.shape[1:])
    return _jnp.moveaxis(t, 1, axis + 1)


def setup_inputs(seed: int = 0) -> dict:
    inp = _fwd_setup_inputs(seed)
    key = _jax.random.fold_in(_jax.random.key(seed), 7919)
    shape, _ = _output_shape()
    out = dict(inp)
    out["loss_target"] = _jax.random.normal(_jax.random.fold_in(key, 0), shape, _jnp.float32)
    for i, name in enumerate(TWIN_WEIGHTS):
        w = inp[name].astype(_jnp.float32)
        if MOMENT_SCALE is None:
            s = _jnp.sqrt(_jnp.mean(_jnp.square(w)) + 1e-30)
        else:
            s = MOMENT_SCALE[name]
        km, kv = _jax.random.split(_jax.random.fold_in(key, i + 1))
        out[name] = w
        out["m_" + name] = s * _jax.random.normal(km, w.shape, _jnp.float32)
        out["v_" + name] = (s * s) * _jax.random.uniform(kv, w.shape, _jnp.float32, 0.5, 1.5)
    if N_MICROBATCH > 1:
        for name, axis in PER_EXAMPLE_BATCH_AXIS.items():
            out[name] = _to_microbatches(out[name], axis)
    return {'x': out['x'], 'positions': out['positions'], 'g_mix_pre': out['g_mix_pre'], 'w_in': out['w_in'], 'b_forget': out['b_forget'], 'g_q_lora': out['g_q_lora'], 'w_q_up': out['w_q_up'], 'g_kv_lora': out['g_kv_lora'], 'w_kv_up': out['w_kv_up'], 'g_mix_out': out['g_mix_out'], 'w_out': out['w_out'], 'g_mix_post': out['g_mix_post'], 'g_ffn_pre': out['g_ffn_pre'], 'w_ffn_up': out['w_ffn_up'], 'w_ffn_down': out['w_ffn_down'], 'g_ffn_post': out['g_ffn_post'], 'loss_target': out['loss_target'], 'm_g_mix_pre': out['m_g_mix_pre'], 'm_w_in': out['m_w_in'], 'm_b_forget': out['m_b_forget'], 'm_g_q_lora': out['m_g_q_lora'], 'm_w_q_up': out['m_w_q_up'], 'm_g_kv_lora': out['m_g_kv_lora'], 'm_w_kv_up': out['m_w_kv_up'], 'm_g_mix_out': out['m_g_mix_out'], 'm_w_out': out['m_w_out'], 'm_g_mix_post': out['m_g_mix_post'], 'm_g_ffn_pre': out['m_g_ffn_pre'], 'm_w_ffn_up': out['m_w_ffn_up'], 'm_w_ffn_down': out['m_w_ffn_down'], 'm_g_ffn_post': out['m_g_ffn_post'], 'v_g_mix_pre': out['v_g_mix_pre'], 'v_w_in': out['v_w_in'], 'v_b_forget': out['v_b_forget'], 'v_g_q_lora': out['v_g_q_lora'], 'v_w_q_up': out['v_w_q_up'], 'v_g_kv_lora': out['v_g_kv_lora'], 'v_w_kv_up': out['v_w_kv_up'], 'v_g_mix_out': out['v_g_mix_out'], 'v_w_out': out['v_w_out'], 'v_g_mix_post': out['v_g_mix_post'], 'v_g_ffn_pre': out['v_g_ffn_pre'], 'v_w_ffn_up': out['v_w_ffn_up'], 'v_w_ffn_down': out['v_w_ffn_down'], 'v_g_ffn_post': out['v_g_ffn_post']}


def _loss(weights, diff, rest, loss_target):
    with _jax.named_scope("forward"):
        args = {**rest, TWIN_DIFF_INPUT: diff, **{k: w.astype(_WEIGHT_DTYPES[k]) for k, w in weights.items()}}
        y = _forward(args)
    with _jax.named_scope("loss_head"):
        err = _jnp.square(y.astype(_jnp.float32) - loss_target)
        return 0.5 * _jnp.sum(_jnp.mean(err, axis=-1)) if err.ndim else 0.5 * err


def _adamw(w, g, m, v):
    m = ADAM_B1 * m + (1.0 - ADAM_B1) * g
    v = ADAM_B2 * v + (1.0 - ADAM_B2) * _jnp.square(g)
    m_hat = m / (1.0 - ADAM_B1 ** ADAM_STEP)
    v_hat = v / (1.0 - ADAM_B2 ** ADAM_STEP)
    delta = -ADAM_LR * (m_hat / (_jnp.sqrt(v_hat) + ADAM_EPS) + ADAM_WD * w)
    return delta, m, v


def reference(x, positions, g_mix_pre, w_in, b_forget, g_q_lora, w_q_up, g_kv_lora, w_kv_up, g_mix_out, w_out, g_mix_post, g_ffn_pre, w_ffn_up, w_ffn_down, g_ffn_post, loss_target, m_g_mix_pre, m_w_in, m_b_forget, m_g_q_lora, m_w_q_up, m_g_kv_lora, m_w_kv_up, m_g_mix_out, m_w_out, m_g_mix_post, m_g_ffn_pre, m_w_ffn_up, m_w_ffn_down, m_g_ffn_post, v_g_mix_pre, v_w_in, v_b_forget, v_g_q_lora, v_w_q_up, v_g_kv_lora, v_w_kv_up, v_g_mix_out, v_w_out, v_g_mix_post, v_g_ffn_pre, v_w_ffn_up, v_w_ffn_down, v_g_ffn_post):
    given = dict(x=x, positions=positions, g_mix_pre=g_mix_pre, w_in=w_in, b_forget=b_forget, g_q_lora=g_q_lora, w_q_up=w_q_up, g_kv_lora=g_kv_lora, w_kv_up=w_kv_up, g_mix_out=g_mix_out, w_out=w_out, g_mix_post=g_mix_post, g_ffn_pre=g_ffn_pre, w_ffn_up=w_ffn_up, w_ffn_down=w_ffn_down, g_ffn_post=g_ffn_post, loss_target=loss_target, m_g_mix_pre=m_g_mix_pre, m_w_in=m_w_in, m_b_forget=m_b_forget, m_g_q_lora=m_g_q_lora, m_w_q_up=m_w_q_up, m_g_kv_lora=m_g_kv_lora, m_w_kv_up=m_w_kv_up, m_g_mix_out=m_g_mix_out, m_w_out=m_w_out, m_g_mix_post=m_g_mix_post, m_g_ffn_pre=m_g_ffn_pre, m_w_ffn_up=m_w_ffn_up, m_w_ffn_down=m_w_ffn_down, m_g_ffn_post=m_g_ffn_post, v_g_mix_pre=v_g_mix_pre, v_w_in=v_w_in, v_b_forget=v_b_forget, v_g_q_lora=v_g_q_lora, v_w_q_up=v_w_q_up, v_g_kv_lora=v_g_kv_lora, v_w_kv_up=v_w_kv_up, v_g_mix_out=v_g_mix_out, v_w_out=v_w_out, v_g_mix_post=v_g_mix_post, v_g_ffn_pre=v_g_ffn_pre, v_w_ffn_up=v_w_ffn_up, v_w_ffn_down=v_w_ffn_down, v_g_ffn_post=v_g_ffn_post)
    weights = {n: given[n] for n in TWIN_WEIGHTS}
    shared = {n: given[n] for n in SHARED_INPUTS}
    per_example = {n: given[n] for n in ['x', 'positions']}
    grad_fn = _jax.value_and_grad(_loss, argnums=(0, 1))

    def one_microbatch(ex, loss_target):
        ex = dict(ex)
        diff = ex.pop(TWIN_DIFF_INPUT)
        return grad_fn(weights, diff, {**shared, **ex}, loss_target)

    if N_MICROBATCH == 1:
        loss, (grad_w, grad_x) = one_microbatch(per_example, given["loss_target"])
    else:
        def body(carry, xs):
            loss_sum, grad_sum = carry
            l_k, (gw_k, gx_k) = one_microbatch(xs[0], xs[1])
            with _jax.named_scope("update"):
                return (loss_sum + l_k, _jax.tree.map(_jnp.add, grad_sum, gw_k)), gx_k

        init = (_jnp.zeros((), _jnp.float32), _jax.tree.map(_jnp.zeros_like, weights))
        (loss, grad_w), grad_x = _jax.lax.scan(body, init, (per_example, given["loss_target"]))
    with _jax.named_scope("update"):
        delta_w, new_m, new_v = {}, {}, {}
        for n in TWIN_WEIGHTS:
            delta_w[n], new_m[n], new_v[n] = _adamw(weights[n], grad_w[n], given["m_" + n], given["v_" + n])
    return (loss, grad_x, *[grad_w[n] for n in TWIN_WEIGHTS], *[delta_w[n] for n in TWIN_WEIGHTS],
            *[new_m[n] for n in TWIN_WEIGHTS], *[new_v[n] for n in TWIN_WEIGHTS])
```

```python
import functools
import math

import numpy as np
import jax
import jax.numpy as jnp
from jax import lax
from jax.experimental import pallas as pl
from jax.experimental.pallas import tpu as pltpu

F32 = jnp.float32
BF16 = jnp.bfloat16

D_MODEL = 1024
DEPTH = 2
N_DEV = 8
GROUP = 256
HEADS = 4
HEAD_DIM = 64
LANES = 128
HP = HEADS * LANES
QKV = 3 * HP
Q_RANK = 256
KV_RANK = 128
ROPE_MLA = 32
D_FF = 4096
D_IN = 2980
CHUNK_SHIFT = 6
EPS = 1e-6
ROPE_BASE = 10000.0
NEG = -1e30

OFF_FOX, OFF_RET, OFF_SB = 0, QKV, 2 * QKV
OFF_RG = 3 * QKV
OFF_CQ = OFF_RG + HP
OFF_CKV = OFF_CQ + Q_RANK
OFF_KR = OFF_CKV + LANES
OFF_FF = OFF_KR + LANES
NP_IN = 6144

BQ = 256
TKS = 128
TM = 256
VMEM_LIMIT = 48 * 1024 * 1024

ADAM_LR, ADAM_B1, ADAM_B2, ADAM_EPS, ADAM_WD, ADAM_STEP = 0.001, 0.9, 0.999, 1e-08, 0.01, 10
ADAM_C1 = 1.0 - ADAM_B1 ** ADAM_STEP
ADAM_C2 = 1.0 - ADAM_B2 ** ADAM_STEP

SMALL_ROWS = 44

NT = (((1,), (1,)), ((), ()))
TN = (((0,), (0,)), ((), ()))


def _cp(*sem):
    return pltpu.CompilerParams(dimension_semantics=sem if sem else None, vmem_limit_bytes=VMEM_LIMIT)


def _bdot(a, b, dn=None):
    if dn is None:
        return jnp.dot(a, b, preferred_element_type=F32)
    return lax.dot_general(a, b, dn, preferred_element_type=F32)


def _split2(x):
    hi = x.astype(BF16)
    lo = (x - hi.astype(F32)).astype(BF16)
    return hi, lo


def _mm(a, b, *, name, ta=False, tb=False, out_dtype=F32, a_fn=None, epi=None, epi_in=None,
        tm=512, tn=1024, tk=1024):
    m, k = (a.shape[1], a.shape[0]) if ta else a.shape
    n = b.shape[0] if tb else b.shape[1]
    tm, tn, tk = min(tm, m), min(tn, n), min(tk, k)
    assert m % tm == 0 and n % tn == 0 and k % tk == 0, (name, m, n, k)
    nk = k // tk
    dn = (((0 if ta else 1,), (1 if tb else 0,)), ((), ()))

    def body(*refs):
        if epi is None:
            a_ref, b_ref, o_ref = refs[:3]
            e_ref = None
            rest = refs[3:]
        else:
            a_ref, b_ref, e_ref, o_ref = refs[:4]
            rest = refs[4:]
        av = a_ref[...]
        if a_fn is not None:
            av = a_fn(av)
        part = lax.dot_general(av.astype(BF16), b_ref[...].astype(BF16), dn, preferred_element_type=F32)

        def finish(r):
            if epi is not None:
                r = epi(r, e_ref[...])
            o_ref[...] = r.astype(out_dtype)

        if nk == 1:
            finish(part)
        else:
            acc_ref = rest[0]
            kk = pl.program_id(2)

            @pl.when(kk == 0)
            def _():
                acc_ref[...] = part

            @pl.when(kk > 0)
            def _():
                acc_ref[...] += part

            @pl.when(kk == nk - 1)
            def _():
                finish(acc_ref[...])

    a_spec = pl.BlockSpec((tk, tm), lambda i, j, kk: (kk, i)) if ta else pl.BlockSpec((tm, tk), lambda i, j, kk: (i, kk))
    b_spec = pl.BlockSpec((tn, tk), lambda i, j, kk: (j, kk)) if tb else pl.BlockSpec((tk, tn), lambda i, j, kk: (kk, j))
    o_spec = pl.BlockSpec((tm, tn), lambda i, j, kk: (i, j))
    in_specs = [a_spec, b_spec]
    args = [a, b]
    if epi is not None:
        in_specs.append(o_spec)
        args.append(epi_in)
    return pl.pallas_call(
        body, name=name, grid=(m // tm, n // tn, nk),
        in_specs=in_specs, out_specs=o_spec,
        out_shape=jax.ShapeDtypeStruct((m, n), out_dtype),
        scratch_shapes=[pltpu.VMEM((tm, tn), F32)] if nk > 1 else [],
        compiler_params=_cp("parallel", "parallel", "arbitrary"),
    )(*args)


def _relu2(v):
    r = jnp.maximum(v, 0.0)
    return r * r


def _drelu2(du, av):
    return du * (2.0 * jnp.maximum(av, 0.0))


def _rms(v, g):
    r = lax.rsqrt(jnp.mean(v * v, axis=-1, keepdims=True) + EPS)
    return v * r * g


def _row_spec(w):
    return pl.BlockSpec((TM, w), lambda i: (i, 0))


def _vec_spec(w):
    return pl.BlockSpec((1, w), lambda i: (0, 0))


def _rms_fwd(x, g, name):
    t, d = x.shape

    def body(x_ref, g_ref, h_ref):
        h_ref[...] = _rms(x_ref[...], g_ref[...]).astype(BF16)

    return pl.pallas_call(
        body, name=name, grid=(t // TM,), in_specs=[_row_spec(d), _vec_spec(d)], out_specs=_row_spec(d),
        out_shape=jax.ShapeDtypeStruct((t, d), BF16), compiler_params=_cp("parallel"))(x, g)


def _add_rms_fwd(x, y, g1, g2, name):
    t, d = x.shape

    def body(x_ref, y_ref, g1_ref, g2_ref, xn_ref, h_ref):
        xn = x_ref[...] + _rms(y_ref[...], g1_ref[...])
        xn_ref[...] = xn
        h_ref[...] = _rms(xn, g2_ref[...]).astype(BF16)

    return pl.pallas_call(
        body, name=name, grid=(t // TM,),
        in_specs=[_row_spec(d), _row_spec(d), _vec_spec(d), _vec_spec(d)],
        out_specs=[_row_spec(d), _row_spec(d)],
        out_shape=[jax.ShapeDtypeStruct((t, d), F32), jax.ShapeDtypeStruct((t, d), BF16)],
        compiler_params=_cp("parallel"))(x, y, g1, g2)


def _final_loss(x, y, g, tgt, name):
    t, d = x.shape

    def body(x_ref, y_ref, g_ref, t_ref, l_ref, dx_ref):
        @pl.when(pl.program_id(0) == 0)
        def _():
            l_ref[...] = jnp.zeros_like(l_ref)

        err = x_ref[...] + _rms(y_ref[...], g_ref[...]) - t_ref[...]
        dx_ref[...] = err * (1.0 / d)
        l_ref[...] += jnp.sum(jnp.sum(err * err, axis=1, keepdims=True), axis=0, keepdims=True) * (0.5 / d)

    return pl.pallas_call(
        body, name=name, grid=(t // TM,),
        in_specs=[_row_spec(d), _row_spec(d), _vec_spec(d), _row_spec(d)],
        out_specs=[pl.BlockSpec((1, LANES), lambda i: (0, 0)), _row_spec(d)],
        out_shape=[jax.ShapeDtypeStruct((1, LANES), F32), jax.ShapeDtypeStruct((t, d), F32)],
        compiler_params=_cp("arbitrary"))(x, y, g, tgt)


def _rms_bwd_vals(dn, v, g):
    w = v.shape[-1]
    r = lax.rsqrt(jnp.mean(v * v, axis=-1, keepdims=True) + EPS)
    vh = v * r
    dgp = jnp.sum(dn * vh, axis=0, keepdims=True)
    dvh = dn * g
    dv = r * (dvh - vh * (jnp.sum(dvh * vh, axis=-1, keepdims=True) * (1.0 / w)))
    return dv, dgp


def _norm_bwd(dn, v, g, resid, out_dtype, name):
    t, d = v.shape
    has_res = resid is not None

    def body(*refs):
        if has_res:
            dn_ref, v_ref, g_ref, r_ref, dv_ref, dg_ref = refs
        else:
            dn_ref, v_ref, g_ref, dv_ref, dg_ref = refs

        @pl.when(pl.program_id(0) == 0)
        def _():
            dg_ref[...] = jnp.zeros_like(dg_ref)

        dv, dgp = _rms_bwd_vals(dn_ref[...].astype(F32), v_ref[...], g_ref[...])
        if has_res:
            dv = dv + r_ref[...]
        dv_ref[...] = dv.astype(out_dtype)
        dg_ref[...] += dgp

    in_specs = [_row_spec(d), _row_spec(d), _vec_spec(d)] + ([_row_spec(d)] if has_res else [])
    args = [dn, v, g] + ([resid] if has_res else [])
    return pl.pallas_call(
        body, name=name, grid=(t // TM,), in_specs=in_specs,
        out_specs=[_row_spec(d), _vec_spec(d)],
        out_shape=[jax.ShapeDtypeStruct((t, d), out_dtype), jax.ShapeDtypeStruct((1, d), F32)],
        compiler_params=_cp("arbitrary"))(*args)


def _rope_tables(pos, invf, lo, half):
    w = invf.shape[-1]
    lane = lax.broadcasted_iota(jnp.int32, (pos.shape[0], w), 1) & (LANES - 1)
    first = (lane >= lo) & (lane < lo + half)
    second = (lane >= lo + half) & (lane < lo + 2 * half)
    ang = pos * invf
    active = first | second
    cos = jnp.where(active, jnp.cos(ang), 1.0)
    sin = jnp.where(active, jnp.sin(ang), 0.0)
    return cos, sin, first, second


def _rope_apply(v, cos, sin, first, second, half, sign):
    w = v.shape[-1]
    up = pltpu.roll(v, w - half, 1)
    dn = pltpu.roll(v, half, 1)
    rot = jnp.where(first, -up, jnp.where(second, dn, 0.0))
    return v * cos + rot * (sin * sign)


def _prep_fwd(proj, pos, invf_ret, invf_mla, lg_lanes, b_pad, g_q, g_kv, wq_pad, wkv_pad, name):
    t = proj.shape[0]

    def body(fox_ref, ret_ref, sb_ref, cq_ref, ckv_ref, kr_ref, ff_ref, pos_ref, ifr_ref, ifm_ref, lg_ref,
             b_ref, gq_ref, gkv_ref, wq_ref, wkv_ref,
             ofox_ref, oret_ref, osb_ref, omla_ref, olsf_ref):
        pos_v = pos_ref[...]
        for src, dst in ((fox_ref, ofox_ref), (sb_ref, osb_ref)):
            dst[:, 0:HP] = (src[:, 0:HP] * 0.125).astype(BF16)
            dst[:, HP:QKV] = src[:, HP:QKV].astype(BF16)
        f = ff_ref[...] + b_ref[...]
        olsf_ref[...] = -(jnp.maximum(-f, 0.0) + jnp.log(1.0 + jnp.exp(-jnp.abs(f))))
        cos, sin, first, second = _rope_tables(pos_v, ifr_ref[...], 0, HEAD_DIM // 2)
        nloc = lax.broadcasted_iota(jnp.int32, (TM, 1), 0).astype(F32)
        dec = lg_ref[...] * nloc
        rq = _rope_apply(ret_ref[:, 0:HP], cos, sin, first, second, HEAD_DIM // 2, 1.0)
        rk = _rope_apply(ret_ref[:, HP:2 * HP], cos, sin, first, second, HEAD_DIM // 2, 1.0)
        oret_ref[:, 0:HP] = (rq * jnp.exp(dec)).astype(BF16)
        oret_ref[:, HP:2 * HP] = (rk * 0.125 * jnp.exp(-dec)).astype(BF16)
        oret_ref[:, 2 * HP:QKV] = ret_ref[:, 2 * HP:QKV].astype(BF16)
        cosm, sinm, firstm, secondm = _rope_tables(pos_v, ifm_ref[...], HEAD_DIM, ROPE_MLA // 2)
        cqn = _rms(cq_ref[...], gq_ref[...]).astype(BF16)
        qm = _bdot(cqn, wq_ref[...])
        omla_ref[:, 0:HP] = _rope_apply(qm, cosm, sinm, firstm, secondm, ROPE_MLA // 2, 1.0).astype(BF16)
        ckvn = _rms(ckv_ref[...], gkv_ref[...]).astype(BF16)
        kv = _bdot(ckvn, wkv_ref[...])
        krr = _rope_apply(kr_ref[...], cosm[:, 0:LANES], sinm[:, 0:LANES], firstm[:, 0:LANES],
                          secondm[:, 0:LANES], ROPE_MLA // 2, 1.0)
        omla_ref[:, HP:2 * HP] = (kv[:, 0:HP] + jnp.concatenate([krr] * HEADS, axis=1)).astype(BF16)
        omla_ref[:, 2 * HP:QKV] = kv[:, HP:2 * HP].astype(BF16)

    def seg(off, w):
        return pl.BlockSpec((TM, w), lambda i, o=off // w: (i, o))

    def full(shape):
        return pl.BlockSpec(shape, lambda i: (0,) * len(shape))

    in_specs = [seg(OFF_FOX, QKV), seg(OFF_RET, QKV), seg(OFF_SB, QKV), seg(OFF_CQ, Q_RANK), seg(OFF_CKV, LANES),
                seg(OFF_KR, LANES), seg(OFF_FF, LANES), pl.BlockSpec((TM, 1), lambda i: (i, 0)),
                full((1, HP)), full((1, HP)), full((1, HP)), full((1, LANES)), full((1, Q_RANK)), full((1, KV_RANK)),
                full((Q_RANK, HP)), full((KV_RANK, 2 * HP))]
    out_specs = [_row_spec(QKV)] * 4 + [_row_spec(LANES)]
    out_shape = [jax.ShapeDtypeStruct((t, QKV), BF16)] * 4 + [jax.ShapeDtypeStruct((t, LANES), F32)]
    return pl.pallas_call(
        body, name=name, grid=(t // TM,), in_specs=in_specs, out_specs=out_specs, out_shape=out_shape,
        compiler_params=_cp("parallel"))(proj, proj, proj, proj, proj, proj, proj, pos, invf_ret, invf_mla, lg_lanes,
                                         b_pad, g_q, g_kv, wq_pad, wkv_pad)


def _prep_bwd(dfox, dret, dsb, dmla, drg, dlsf, proj, pos, invf_ret, invf_mla, lg_lanes, b_pad, g_q, g_kv,
              wq_pad, wkv_pad, name):
    t = proj.shape[0]

    def body(dfq, dfk, dfv, drq, drk, drv, dsq, dsk, dsv, dmq, dmk, dmv, drg_ref, dlsf_ref,
             cq_ref, ckv_ref, ff_ref, pos_ref, ifr_ref, ifm_ref, lg_ref, b_ref, gq_ref, gkv_ref, wq_ref, wkv_ref,
             dp_ref, dwq_ref, dwkv_ref, dgq_ref, dgkv_ref, dbf_ref):
        @pl.when(pl.program_id(0) == 0)
        def _():
            dwq_ref[...] = jnp.zeros_like(dwq_ref)
            dwkv_ref[...] = jnp.zeros_like(dwkv_ref)
            dgq_ref[...] = jnp.zeros_like(dgq_ref)
            dgkv_ref[...] = jnp.zeros_like(dgkv_ref)
            dbf_ref[...] = jnp.zeros_like(dbf_ref)

        pos_v = pos_ref[...]
        for off, (dq, dk, dv) in ((OFF_FOX, (dfq, dfk, dfv)), (OFF_SB, (dsq, dsk, dsv))):
            dp_ref[:, off:off + HP] = (dq[...] * 0.125).astype(BF16)
            dp_ref[:, off + HP:off + 2 * HP] = dk[...].astype(BF16)
            dp_ref[:, off + 2 * HP:off + QKV] = dv[...].astype(BF16)
        cos, sin, first, second = _rope_tables(pos_v, ifr_ref[...], 0, HEAD_DIM // 2)
        nloc = lax.broadcasted_iota(jnp.int32, (TM, 1), 0).astype(F32)
        dec = lg_ref[...] * nloc
        dq = _rope_apply(drq[...] * jnp.exp(dec), cos, sin, first, second, HEAD_DIM // 2, -1.0)
        dk = _rope_apply(drk[...] * (0.125 * jnp.exp(-dec)), cos, sin, first, second, HEAD_DIM // 2, -1.0)
        dp_ref[:, OFF_RET:OFF_RET + HP] = dq.astype(BF16)
        dp_ref[:, OFF_RET + HP:OFF_RET + 2 * HP] = dk.astype(BF16)
        dp_ref[:, OFF_RET + 2 * HP:OFF_RET + QKV] = drv[...].astype(BF16)
        dp_ref[:, OFF_RG:OFF_RG + HP] = drg_ref[...].astype(BF16)
        cosm, sinm, firstm, secondm = _rope_tables(pos_v, ifm_ref[...], HEAD_DIM, ROPE_MLA // 2)
        dql = _rope_apply(dmq[...], cosm, sinm, firstm, secondm, ROPE_MLA // 2, -1.0).astype(BF16)
        cq = cq_ref[...]
        cqn = _rms(cq, gq_ref[...]).astype(BF16)
        dwq_ref[...] += _bdot(cqn, dql, TN)
        dcqn = _bdot(dql, wq_ref[...], NT)
        dcq, dgq = _rms_bwd_vals(dcqn, cq, gq_ref[...])
        dgq_ref[...] += dgq
        dp_ref[:, OFF_CQ:OFF_CQ + Q_RANK] = dcq.astype(BF16)
        dkm = dmk[...]
        dkv = jnp.concatenate([dkm, dmv[...]], axis=1).astype(BF16)
        ckv = ckv_ref[...]
        ckvn = _rms(ckv, gkv_ref[...]).astype(BF16)
        dwkv_ref[...] += _bdot(ckvn, dkv, TN)
        dckvn = _bdot(dkv, wkv_ref[...], NT)
        dckv, dgkv = _rms_bwd_vals(dckvn, ckv, gkv_ref[...])
        dgkv_ref[...] += dgkv
        dp_ref[:, OFF_CKV:OFF_CKV + LANES] = dckv.astype(BF16)
        dkr = dkm[:, 0:LANES] + dkm[:, LANES:2 * LANES] + dkm[:, 2 * LANES:3 * LANES] + dkm[:, 3 * LANES:HP]
        act = firstm[:, 0:LANES] | secondm[:, 0:LANES]
        dkr = jnp.where(act, dkr, 0.0)
        dkr = _rope_apply(dkr, cosm[:, 0:LANES], sinm[:, 0:LANES], firstm[:, 0:LANES], secondm[:, 0:LANES],
                          ROPE_MLA // 2, -1.0)
        dp_ref[:, OFF_KR:OFF_KR + LANES] = dkr.astype(BF16)
        f = ff_ref[...] + b_ref[...]
        dff = dlsf_ref[...] / (1.0 + jnp.exp(f))
        dbf_ref[...] += jnp.sum(dff, axis=0, keepdims=True)
        dp_ref[:, OFF_FF:OFF_FF + LANES] = dff.astype(BF16)
        dp_ref[:, OFF_FF + LANES:NP_IN] = jnp.zeros((TM, NP_IN - OFF_FF - LANES), BF16)

    def seg(off, w):
        return pl.BlockSpec((TM, w), lambda i, o=off // w: (i, o))

    def full(shape):
        return pl.BlockSpec(shape, lambda i: (0,) * len(shape))

    hp_spec = _row_spec(HP)
    in_specs = [hp_spec] * 13 + [_row_spec(LANES), seg(OFF_CQ, Q_RANK), seg(OFF_CKV, LANES), seg(OFF_FF, LANES),
                                 pl.BlockSpec((TM, 1), lambda i: (i, 0)),
                                 full((1, HP)), full((1, HP)), full((1, HP)), full((1, LANES)), full((1, Q_RANK)),
                                 full((1, KV_RANK)), full((Q_RANK, HP)), full((KV_RANK, 2 * HP))]
    out_specs = [_row_spec(NP_IN), full((Q_RANK, HP)), full((KV_RANK, 2 * HP)), full((1, Q_RANK)),
                 full((1, KV_RANK)), full((1, LANES))]
    out_shape = [jax.ShapeDtypeStruct((t, NP_IN), BF16), jax.ShapeDtypeStruct((Q_RANK, HP), F32),
                 jax.ShapeDtypeStruct((KV_RANK, 2 * HP), F32), jax.ShapeDtypeStruct((1, Q_RANK), F32),
                 jax.ShapeDtypeStruct((1, KV_RANK), F32), jax.ShapeDtypeStruct((1, LANES), F32)]
    return pl.pallas_call(
        body, name=name, grid=(t // TM,), in_specs=in_specs, out_specs=out_specs, out_shape=out_shape,
        compiler_params=_cp("arbitrary"))(*dfox, *dret, *dsb, *dmla, drg, dlsf, proj, proj, proj, pos, invf_ret,
                                          invf_mla, lg_lanes, b_pad, g_q, g_kv, wq_pad, wkv_pad)


def _cumsum(xs, reverse, name):
    t, w = xs[0].shape
    n = t // TM

    def body(*refs):
        x_refs, o_ref, carry = refs[:len(xs)], refs[len(xs)], refs[len(xs) + 1]

        @pl.when(pl.program_id(0) == 0)
        def _():
            carry[...] = jnp.zeros_like(carry)

        r = lax.broadcasted_iota(jnp.int32, (TM, TM), 0)
        c = lax.broadcasted_iota(jnp.int32, (TM, TM), 1)
        tri = jnp.where((r <= c) if reverse else (r >= c), 1.0, 0.0).astype(BF16)
        v = x_refs[0][...]
        for x_ref in x_refs[1:]:
            v = v + x_ref[...]
        hi = v.astype(BF16)
        r1 = v - hi.astype(F32)
        mid = r1.astype(BF16)
        lo = (r1 - mid.astype(F32)).astype(BF16)
        cs = _bdot(tri, hi) + _bdot(tri, mid) + _bdot(tri, lo) + carry[...]
        o_ref[...] = cs
        carry[...] = cs[0:1, :] if reverse else cs[TM - 1:TM, :]

    imap = (lambda i: (n - 1 - i, 0)) if reverse else (lambda i: (i, 0))
    return pl.pallas_call(
        body, name=name, grid=(n,), in_specs=[pl.BlockSpec((TM, w), imap)] * len(xs),
        out_specs=pl.BlockSpec((TM, w), imap),
        out_shape=jax.ShapeDtypeStruct((t, w), F32), scratch_shapes=[pltpu.VMEM((1, w), F32)],
        compiler_params=_cp("arbitrary"))(*xs)


def _q_spec():
    return pl.BlockSpec((BQ, LANES), lambda h, i: (i, h))


def _kv_spec(t, which):
    return pl.BlockSpec((t, LANES), lambda h, i, w=which: (0, w * HEADS + h))


def _acc_spec(t):
    return pl.BlockSpec((t, LANES), lambda h, i: (0, h))


def _rowvec_spec(nkv):
    return pl.BlockSpec((None, nkv, 1, BQ), lambda h, i: (h, 0, 0, 0))


def _tile_iota(rows, cols):
    return (lax.broadcasted_iota(jnp.int32, (rows, cols), 0), lax.broadcasted_iota(jnp.int32, (rows, cols), 1))


def _softmax_fwd(qkv, cum_bc, cum_row, *, bias, chunk_mask, scale, name):
    t = qkv.shape[0]
    nq = t // BQ

    def body(*refs):
        if bias:
            q_ref, k_ref, v_ref, cb_ref, cr_ref, o_ref, lse_ref, m_sc, l_sc, acc_sc = refs
        else:
            q_ref, k_ref, v_ref, o_ref, lse_ref, m_sc, l_sc, acc_sc = refs
        i = pl.program_id(1)
        q = q_ref[...]
        m_sc[...] = jnp.full((BQ, 1), NEG, F32)
        l_sc[...] = jnp.zeros((BQ, 1), F32)
        acc_sc[...] = jnp.zeros((BQ, LANES), F32)

        def tile(j, masked):
            off = pl.multiple_of(j * BQ, BQ)
            kb = k_ref[pl.ds(off, BQ), :]
            vb = v_ref[pl.ds(off, BQ), :]
            s = _bdot(q, kb, NT)
            if scale != 1.0:
                s = s * scale
            if bias:
                s = s + cb_ref[:, 0:1] - cr_ref[j]
            if masked:
                r, c = _tile_iota(BQ, BQ)
                vis = ((c >> CHUNK_SHIFT) <= (r >> CHUNK_SHIFT)) if chunk_mask else (c <= r)
                s = jnp.where(vis, s, NEG)
            m_old = m_sc[...]
            m_new = jnp.maximum(m_old, jnp.max(s, axis=1, keepdims=True))
            alpha = jnp.exp(m_old - m_new)
            p = jnp.exp(s - m_new)
            l_sc[...] = alpha * l_sc[...] + jnp.sum(p, axis=1, keepdims=True)
            acc_sc[...] = alpha * acc_sc[...] + _bdot(p.astype(BF16), vb)
            m_sc[...] = m_new

        def loop(j, carry):
            tile(j, False)
            return carry

        lax.fori_loop(0, i, loop, 0)
        tile(i, True)
        l = l_sc[...]
        o_ref[...] = acc_sc[...] / l
        lse_ref[...] = jnp.broadcast_to(m_sc[...] + jnp.log(l), (BQ, LANES))

    in_specs = [_q_spec(), _kv_spec(t, 1), _kv_spec(t, 2)]
    args = [qkv, qkv, qkv]
    if bias:
        in_specs += [_q_spec(), _rowvec_spec(nq)]
        args += [cum_bc, cum_row]
    return pl.pallas_call(
        body, name=name, grid=(HEADS, nq), in_specs=in_specs, out_specs=[_q_spec(), _q_spec()],
        out_shape=[jax.ShapeDtypeStruct((t, HP), F32), jax.ShapeDtypeStruct((t, HP), F32)],
        scratch_shapes=[pltpu.VMEM((BQ, 1), F32), pltpu.VMEM((BQ, 1), F32), pltpu.VMEM((BQ, LANES), F32)],
        compiler_params=_cp("parallel", "arbitrary"))(*args)


def _softmax_bwd(qkv, cum_bc, cum_row, do, lse, delta, *, bias, chunk_mask, scale, name):
    t = qkv.shape[0]
    nq = t // BQ

    def body(*refs):
        if bias:
            (q_ref, k_ref, v_ref, cb_ref, cr_ref, do_ref, lse_ref, dl_ref, dq_ref, dk_ref, dv_ref, dc_ref, dr_ref,
             dq_sc, dr_sc) = refs
            dr_sc[...] = jnp.zeros((BQ, 1), F32)
        else:
            q_ref, k_ref, v_ref, do_ref, lse_ref, dl_ref, dq_ref, dk_ref, dv_ref, dq_sc = refs
        i = pl.program_id(1)

        @pl.when(i == 0)
        def _():
            dk_ref[...] = jnp.zeros_like(dk_ref)
            dv_ref[...] = jnp.zeros_like(dv_ref)
            if bias:
                dc_ref[...] = jnp.zeros_like(dc_ref)

        q = q_ref[...]
        dob = do_ref[...].astype(BF16)
        lse_c = lse_ref[:, 0:1]
        dl_c = dl_ref[:, 0:1]
        dq_sc[...] = jnp.zeros((BQ, LANES), F32)

        def tile(j, masked):
            off = pl.multiple_of(j * BQ, BQ)
            kb = k_ref[pl.ds(off, BQ), :]
            vb = v_ref[pl.ds(off, BQ), :]
            s = _bdot(q, kb, NT)
            if scale != 1.0:
                s = s * scale
            if bias:
                s = s + cb_ref[:, 0:1] - cr_ref[j]
            p = jnp.exp(s - lse_c)
            if masked:
                r, c = _tile_iota(BQ, BQ)
                vis = ((c >> CHUNK_SHIFT) <= (r >> CHUNK_SHIFT)) if chunk_mask else (c <= r)
                p = jnp.where(vis, p, 0.0)
            dp = _bdot(dob, vb, NT)
            ds = p * (dp - dl_c)
            if bias:
                dc_ref[j] = dc_ref[j] - jnp.sum(ds, axis=0, keepdims=True)
                dr_sc[...] += jnp.sum(ds, axis=1, keepdims=True)
            if scale != 1.0:
                ds = ds * scale
            dsb = ds.astype(BF16)
            dv_ref[pl.ds(off, BQ), :] += _bdot(p.astype(BF16), dob, TN)
            dk_ref[pl.ds(off, BQ), :] += _bdot(dsb, q, TN)
            dq_sc[...] += _bdot(dsb, kb)

        def loop(j, carry):
            tile(j, False)
            return carry

        lax.fori_loop(0, i, loop, 0)
        tile(i, True)
        dq_ref[...] = dq_sc[...]
        if bias:
            dr_ref[...] = jnp.broadcast_to(dr_sc[...], (BQ, LANES))

    in_specs = [_q_spec(), _kv_spec(t, 1), _kv_spec(t, 2)]
    args = [qkv, qkv, qkv]
    if bias:
        in_specs += [_q_spec(), _rowvec_spec(nq)]
        args += [cum_bc, cum_row]
    in_specs += [_q_spec(), _q_spec(), _q_spec()]
    args += [do, lse, delta]
    out_specs = [_q_spec(), _acc_spec(t), _acc_spec(t)]
    out_shape = [jax.ShapeDtypeStruct((t, HP), F32)] * 3
    scratch = [pltpu.VMEM((BQ, LANES), F32)]
    if bias:
        out_specs += [_rowvec_spec(nq), _q_spec()]
        out_shape += [jax.ShapeDtypeStruct((HEADS, nq, 1, BQ), F32), jax.ShapeDtypeStruct((t, HP), F32)]
        scratch.append(pltpu.VMEM((BQ, 1), F32))
    return pl.pallas_call(
        body, name=name, grid=(HEADS, nq), in_specs=in_specs, out_specs=out_specs, out_shape=out_shape,
        scratch_shapes=scratch,
        compiler_params=_cp("parallel", "arbitrary"))(*args)


def _ret_diag_decay(lg1):
    r, c = _tile_iota(BQ, BQ)
    dd = jnp.where(c > r, jnp.exp((2.0 * lg1) * (c - r).astype(F32)), 1.0)
    return jnp.where((c >> CHUNK_SHIFT) <= (r >> CHUNK_SHIFT), dd, 0.0)


def _lg_spec():
    return pl.BlockSpec((None, 1, LANES), lambda h, i: (h, 0, 0))


def _ret_fwd(qkv, lg_heads, name):
    t = qkv.shape[0]
    nq = t // BQ

    def body(lg_ref, q_ref, k_ref, v_ref, o_ref, acc_sc):
        i = pl.program_id(1)
        lg1 = lg_ref[:, 0:1]
        q = q_ref[...]
        acc_sc[...] = jnp.zeros((BQ, LANES), F32)

        def loop(j, carry):
            off = pl.multiple_of(j * BQ, BQ)
            a = _bdot(q, k_ref[pl.ds(off, BQ), :], NT)
            cf = jnp.exp(lg1 * ((i - j) * BQ).astype(F32))
            acc_sc[...] += _bdot((a * cf).astype(BF16), v_ref[pl.ds(off, BQ), :])
            return carry

        lax.fori_loop(0, i, loop, 0)
        off = pl.multiple_of(i * BQ, BQ)
        a = _bdot(q, k_ref[pl.ds(off, BQ), :], NT) * _ret_diag_decay(lg1)
        o_ref[...] = acc_sc[...] + _bdot(a.astype(BF16), v_ref[pl.ds(off, BQ), :])

    return pl.pallas_call(
        body, name=name, grid=(HEADS, nq), in_specs=[_lg_spec(), _q_spec(), _kv_spec(t, 1), _kv_spec(t, 2)],
        out_specs=_q_spec(), out_shape=jax.ShapeDtypeStruct((t, HP), F32),
        scratch_shapes=[pltpu.VMEM((BQ, LANES), F32)],
        compiler_params=_cp("parallel", "arbitrary"))(lg_heads, qkv, qkv, qkv)


def _ret_bwd(qkv, lg_heads, do, name):
    t = qkv.shape[0]
    nq = t // BQ

    def body(lg_ref, q_ref, k_ref, v_ref, do_ref, dq_ref, dk_ref, dv_ref, dq_sc):
        i = pl.program_id(1)

        @pl.when(i == 0)
        def _():
            dk_ref[...] = jnp.zeros_like(dk_ref)
            dv_ref[...] = jnp.zeros_like(dv_ref)

        lg1 = lg_ref[:, 0:1]
        q = q_ref[...]
        dob = do_ref[...].astype(BF16)
        dq_sc[...] = jnp.zeros((BQ, LANES), F32)

        def tile(j, fac):
            off = pl.multiple_of(j * BQ, BQ)
            kb = k_ref[pl.ds(off, BQ), :]
            vb = v_ref[pl.ds(off, BQ), :]
            ab = (_bdot(q, kb, NT) * fac).astype(BF16)
            dab = (_bdot(dob, vb, NT) * fac).astype(BF16)
            dv_ref[pl.ds(off, BQ), :] += _bdot(ab, dob, TN)
            dk_ref[pl.ds(off, BQ), :] += _bdot(dab, q, TN)
            dq_sc[...] += _bdot(dab, kb)

        def loop(j, carry):
            tile(j, jnp.exp(lg1 * ((i - j) * BQ).astype(F32)))
            return carry

        lax.fori_loop(0, i, loop, 0)
        tile(i, _ret_diag_decay(lg1))
        dq_ref[...] = dq_sc[...]

    return pl.pallas_call(
        body, name=name, grid=(HEADS, nq),
        in_specs=[_lg_spec(), _q_spec(), _kv_spec(t, 1), _kv_spec(t, 2), _q_spec()],
        out_specs=[_q_spec(), _acc_spec(t), _acc_spec(t)],
        out_shape=[jax.ShapeDtypeStruct((t, HP), F32)] * 3,
        scratch_shapes=[pltpu.VMEM((BQ, LANES), F32)],
        compiler_params=_cp("parallel", "arbitrary"))(lg_heads, qkv, qkv, qkv, do)


def _sb_tile_logs(q, kb, vis):
    z = _bdot(q, kb, NT)
    ls = -(jnp.maximum(z, 0.0) + jnp.log(1.0 + jnp.exp(-jnp.abs(z))))
    if vis is not None:
        ls = jnp.where(vis, ls, 0.0)
    return z, ls


def _sb_vis(i, j):
    r, c = _tile_iota(BQ, TKS)
    return (j * TKS + c) < (i * BQ + r)


def _sb_fwd(qkv, name):
    t = qkv.shape[0]
    nq = t // BQ
    per = BQ // TKS

    def body(q_ref, k_ref, v_ref, o_ref, acc_sc, r_sc):
        i = pl.program_id(1)
        q = q_ref[...]
        acc_sc[...] = jnp.zeros((BQ, LANES), F32)
        r_sc[...] = jnp.zeros((BQ, 1), F32)
        mr, mc = _tile_iota(TKS, TKS)
        after = jnp.where(mr > mc, 1.0, 0.0).astype(BF16)

        def tile(j, masked):
            off = pl.multiple_of(j * TKS, TKS)
            kb = k_ref[pl.ds(off, TKS), :]
            vb = v_ref[pl.ds(off, TKS), :]
            vis = _sb_vis(i, j) if masked else None
            z, ls = _sb_tile_logs(q, kb, vis)
            hi, lo = _split2(ls)
            later = _bdot(hi, after) + _bdot(lo, after)
            w = jnp.exp(z + ls + later + r_sc[...])
            if masked:
                w = jnp.where(vis, w, 0.0)
            acc_sc[...] += _bdot(w.astype(BF16), vb)
            r_sc[...] += later[:, 0:1] + ls[:, 0:1]

        for d in range(per):
            tile(per * i + (per - 1 - d), True)

        def loop(jj, carry):
            tile(per * i - 1 - jj, False)
            return carry

        lax.fori_loop(0, per * i, loop, 0)
        o_ref[...] = acc_sc[...]

    return pl.pallas_call(
        body, name=name, grid=(HEADS, nq), in_specs=[_q_spec(), _kv_spec(t, 1), _kv_spec(t, 2)],
        out_specs=_q_spec(), out_shape=jax.ShapeDtypeStruct((t, HP), F32),
        scratch_shapes=[pltpu.VMEM((BQ, LANES), F32), pltpu.VMEM((BQ, 1), F32)],
        compiler_params=_cp("parallel", "arbitrary"))(qkv, qkv, qkv)


def _sb_bwd(qkv, do, name):
    t = qkv.shape[0]
    nq = t // BQ
    per = BQ // TKS
    nkv = t // TKS

    def body(q_ref, k_ref, v_ref, do_ref, dq_ref, dk_ref, dv_ref, dq_sc, r_sc, g_sc, rs_sc):
        i = pl.program_id(1)

        @pl.when(i == 0)
        def _():
            dk_ref[...] = jnp.zeros_like(dk_ref)
            dv_ref[...] = jnp.zeros_like(dv_ref)

        q = q_ref[...]
        dob = do_ref[...].astype(BF16)
        dq_sc[...] = jnp.zeros((BQ, LANES), F32)
        r_sc[...] = jnp.zeros((BQ, 1), F32)
        g_sc[...] = jnp.zeros((BQ, 1), F32)
        mr, mc = _tile_iota(TKS, TKS)
        after = jnp.where(mr > mc, 1.0, 0.0).astype(BF16)
        before = jnp.where(mr < mc, 1.0, 0.0).astype(BF16)

        def tile1(j, masked):
            off = pl.multiple_of(j * TKS, TKS)
            vis = _sb_vis(i, j) if masked else None
            _, ls = _sb_tile_logs(q, k_ref[pl.ds(off, TKS), :], vis)
            rs_sc[j] = r_sc[...]
            r_sc[...] += jnp.sum(ls, axis=1, keepdims=True)

        for d in range(per):
            tile1(per * i + (per - 1 - d), True)

        def loop1(jj, carry):
            tile1(per * i - 1 - jj, False)
            return carry

        lax.fori_loop(0, per * i, loop1, 0)

        def tile2(j, masked):
            off = pl.multiple_of(j * TKS, TKS)
            kb = k_ref[pl.ds(off, TKS), :]
            vb = v_ref[pl.ds(off, TKS), :]
            vis = _sb_vis(i, j) if masked else None
            z, ls = _sb_tile_logs(q, kb, vis)
            hi, lo = _split2(ls)
            later = _bdot(hi, after) + _bdot(lo, after)
            w = jnp.exp(z + ls + later + rs_sc[j])
            if masked:
                w = jnp.where(vis, w, 0.0)
            g = _bdot(dob, vb, NT) * w
            ghi, glo = _split2(g)
            gin = _bdot(ghi, before) + _bdot(glo, before)
            stay = jnp.exp(ls)
            dz = g * stay - (1.0 - stay) * (gin + g_sc[...])
            if masked:
                dz = jnp.where(vis, dz, 0.0)
            dzb = dz.astype(BF16)
            dv_ref[pl.ds(off, TKS), :] += _bdot(w.astype(BF16), dob, TN)
            dk_ref[pl.ds(off, TKS), :] += _bdot(dzb, q, TN)
            dq_sc[...] += _bdot(dzb, kb)
            g_sc[...] += gin[:, TKS - 1:TKS] + g[:, TKS - 1:TKS]

        def loop2(j, carry):
            tile2(j, False)
            return carry

        lax.fori_loop(0, per * i, loop2, 0)
        for d in range(per):
            tile2(per * i + d, True)
        dq_ref[...] = dq_sc[...]

    return pl.pallas_call(
        body, name=name, grid=(HEADS, nq), in_specs=[_q_spec(), _kv_spec(t, 1), _kv_spec(t, 2), _q_spec()],
        out_specs=[_q_spec(), _acc_spec(t), _acc_spec(t)],
        out_shape=[jax.ShapeDtypeStruct((t, HP), F32)] * 3,
        scratch_shapes=[pltpu.VMEM((BQ, LANES), F32), pltpu.VMEM((BQ, 1), F32), pltpu.VMEM((BQ, 1), F32),
                        pltpu.VMEM((nkv, BQ, 1), F32)],
        compiler_params=_cp("parallel", "arbitrary"))(qkv, qkv, qkv, do)


def _sigmoid(v):
    return 1.0 / (1.0 + jnp.exp(-v))


def _post_fwd(oa, ob, oc, od, proj, g_pad, name):
    t = oa.shape[0]

    def body(oa_ref, ob_ref, oc_ref, od_ref, rg_ref, g_ref, mx_ref):
        g = g_ref[...]

        def group(o, gg):
            r = lax.rsqrt(jnp.sum(o * o, axis=-1, keepdims=True) * (1.0 / GROUP) + EPS)
            return (o * r * gg).astype(BF16)

        mx_ref[:, 0:HP] = group(oa_ref[...], g[:, 0:HP])
        mx_ref[:, HP:2 * HP] = group(ob_ref[...], g[:, HP:2 * HP])
        mx_ref[:, 3 * HP:4 * HP] = group(od_ref[...], g[:, 3 * HP:4 * HP])
        real = lax.broadcasted_iota(jnp.int32, (TM, LANES), 1) < HEAD_DIM
        for hb in range(HEADS):
            sl = slice(hb * LANES, (hb + 1) * LANES)
            o = oc_ref[:, sl]
            mu = jnp.sum(o, axis=-1, keepdims=True) * (1.0 / HEAD_DIM)
            dlt = jnp.where(real, o - mu, 0.0)
            var = jnp.sum(dlt * dlt, axis=-1, keepdims=True) * (1.0 / HEAD_DIM)
            yn = dlt * lax.rsqrt(var + EPS) * g[:, 2 * HP + hb * LANES:2 * HP + (hb + 1) * LANES]
            x = rg_ref[:, sl]
            mx_ref[:, 2 * HP + hb * LANES:2 * HP + (hb + 1) * LANES] = (yn * (x * _sigmoid(x))).astype(BF16)

    rg_spec = pl.BlockSpec((TM, HP), lambda i: (i, OFF_RG // HP))
    return pl.pallas_call(
        body, name=name, grid=(t // TM,),
        in_specs=[_row_spec(HP)] * 4 + [rg_spec, _vec_spec(4 * HP)], out_specs=_row_spec(4 * HP),
        out_shape=jax.ShapeDtypeStruct((t, 4 * HP), BF16), compiler_params=_cp("parallel"))(oa, ob, oc, od, proj, g_pad)


def _post_bwd(dmx, oa, ob, oc, od, proj, g_pad, name):
    t = oa.shape[0]

    def body(dm_ref, oa_ref, ob_ref, oc_ref, od_ref, rg_ref, g_ref,
             doa_ref, dob_ref, doc_ref, dod_ref, dla_ref, dlb_ref, drg_ref, dg_ref):
        @pl.when(pl.program_id(0) == 0)
        def _():
            dg_ref[...] = jnp.zeros_like(dg_ref)

        g = g_ref[...]

        def group_bwd(dm, o, gg):
            r = lax.rsqrt(jnp.sum(o * o, axis=-1, keepdims=True) * (1.0 / GROUP) + EPS)
            oh = o * r
            dgp = jnp.sum(dm * oh, axis=0, keepdims=True)
            dyh = dm * gg
            do = r * (dyh - oh * (jnp.sum(dyh * oh, axis=-1, keepdims=True) * (1.0 / GROUP)))
            return do, dgp

        def delta_bc(do, o):
            prod = do * o
            cols = [jnp.broadcast_to(jnp.sum(prod[:, hb * LANES:(hb + 1) * LANES], axis=-1, keepdims=True), (TM, LANES))
                    for hb in range(HEADS)]
            return jnp.concatenate(cols, axis=1)

        oa = oa_ref[...]
        do_a, dga = group_bwd(dm_ref[:, 0:HP], oa, g[:, 0:HP])
        doa_ref[...] = do_a
        dla_ref[...] = delta_bc(do_a, oa)
        dg_ref[:, 0:HP] += dga
        ob = ob_ref[...]
        do_b, dgb = group_bwd(dm_ref[:, HP:2 * HP], ob, g[:, HP:2 * HP])
        dob_ref[...] = do_b
        dlb_ref[...] = delta_bc(do_b, ob)
        dg_ref[:, HP:2 * HP] += dgb
        do_d, dgd = group_bwd(dm_ref[:, 3 * HP:4 * HP], od_ref[...], g[:, 3 * HP:4 * HP])
        dod_ref[...] = do_d
        dg_ref[:, 3 * HP:4 * HP] += dgd
        real = lax.broadcasted_iota(jnp.int32, (TM, LANES), 1) < HEAD_DIM
        for hb in range(HEADS):
            sl = slice(hb * LANES, (hb + 1) * LANES)
            gsl = slice(2 * HP + hb * LANES, 2 * HP + (hb + 1) * LANES)
            o = oc_ref[:, sl]
            mu = jnp.sum(o, axis=-1, keepdims=True) * (1.0 / HEAD_DIM)
            dlt = jnp.where(real, o - mu, 0.0)
            var = jnp.sum(dlt * dlt, axis=-1, keepdims=True) * (1.0 / HEAD_DIM)
            rstd = lax.rsqrt(var + EPS)
            dhat = dlt * rstd
            gc = g[:, gsl]
            x = rg_ref[:, sl]
            sg = _sigmoid(x)
            dm = dm_ref[:, gsl]
            drg_ref[:, sl] = dm * (dhat * gc) * (sg * (1.0 + x * (1.0 - sg)))
            dyn = dm * (x * sg)
            dg_ref[:, gsl] += jnp.sum(dyn * dhat, axis=0, keepdims=True)
            ddh = dyn * gc
            m1 = jnp.sum(ddh, axis=-1, keepdims=True) * (1.0 / HEAD_DIM)
            m2 = jnp.sum(ddh * dhat, axis=-1, keepdims=True) * (1.0 / HEAD_DIM)
            doc_ref[:, sl] = jnp.where(real, rstd * (ddh - m1 - dhat * m2), 0.0)

    rg_spec = pl.BlockSpec((TM, HP), lambda i: (i, OFF_RG // HP))
    hp = _row_spec(HP)
    return pl.pallas_call(
        body, name=name, grid=(t // TM,),
        in_specs=[_row_spec(4 * HP), hp, hp, hp, hp, rg_spec, _vec_spec(4 * HP)],
        out_specs=[hp] * 7 + [_vec_spec(4 * HP)],
        out_shape=[jax.ShapeDtypeStruct((t, HP), F32)] * 7 + [jax.ShapeDtypeStruct((1, 4 * HP), F32)],
        compiler_params=_cp("arbitrary"))(dmx, oa, ob, oc, od, proj, g_pad)


def _mesh_pos():
    return lax.axis_index("x"), lax.axis_index("y"), lax.axis_index("c")


def _peer(pos, k):
    x, y, c = pos
    px = 1 - x if (k >> 2) & 1 else x
    py = 1 - y if (k >> 1) & 1 else y
    pc = 1 - c if k & 1 else c
    return (px, py, pc), 4 * px + 2 * py + pc


def _exchange(arrs, gather, name):
    n = len(arrs)

    def body(*refs):
        ins, outs = refs[:n], refs[n:2 * n]
        send_sems, recv_sems, loc_sems = refs[2 * n:]
        pos = _mesh_pos()
        me = 4 * pos[0] + 2 * pos[1] + pos[2]
        local = []
        for a in range(n):
            src = ins[a] if gather else ins[a].at[me]
            cp = pltpu.make_async_copy(src, outs[a].at[me], loc_sems.at[a])
            cp.start()
            local.append(cp)
        sends, recvs = [], []
        for k in range(1, N_DEV):
            peer, pid = _peer(pos, k)
            for a in range(n):
                s = a * (N_DEV - 1) + k - 1
                src = ins[a] if gather else ins[a].at[pid]
                cp = pltpu.make_async_remote_copy(
                    src_ref=src, dst_ref=outs[a].at[me], send_sem=send_sems.at[s], recv_sem=recv_sems.at[s],
                    device_id=peer, device_id_type=pl.DeviceIdType.MESH)
                cp.start()
                sends.append(cp)
                recvs.append(pltpu.make_async_remote_copy(
                    src_ref=src, dst_ref=outs[a].at[pid], send_sem=send_sems.at[s], recv_sem=recv_sems.at[s],
                    device_id=peer, device_id_type=pl.DeviceIdType.MESH))
        for cp in recvs:
            cp.wait_recv()
        for cp in sends:
            cp.wait_send()
        for cp in local:
            cp.wait()

    any_spec = pl.BlockSpec(memory_space=pl.ANY)
    out_shape = [jax.ShapeDtypeStruct((N_DEV,) + tuple(a.shape) if gather else tuple(a.shape), a.dtype) for a in arrs]
    return pl.pallas_call(
        body, name=name, in_specs=[any_spec] * n, out_specs=[any_spec] * n, out_shape=out_shape,
        scratch_shapes=[pltpu.SemaphoreType.DMA((n * (N_DEV - 1),)), pltpu.SemaphoreType.DMA((n * (N_DEV - 1),)),
                        pltpu.SemaphoreType.DMA((n,))],
        compiler_params=pltpu.CompilerParams(has_side_effects=True))(*arrs)


def _adam_vals(w, g, m, v):
    m = ADAM_B1 * m + (1.0 - ADAM_B1) * g
    v = ADAM_B2 * v + (1.0 - ADAM_B2) * (g * g)
    m_hat = m / ADAM_C1
    v_hat = v / ADAM_C2
    delta = -ADAM_LR * (m_hat / (jnp.sqrt(v_hat) + ADAM_EPS) + ADAM_WD * w)
    return delta, m, v


def _small_allreduce_adam(part, w, m, v, name):
    rows = part.shape[0]

    def body(p_ref, w_ref, m_ref, v_ref, g_ref, d_ref, nm_ref, nv_ref, gath, send_sems, recv_sems):
        pos = _mesh_pos()
        me = 4 * pos[0] + 2 * pos[1] + pos[2]
        gath[me] = p_ref[...]
        sends, recvs = [], []
        for k in range(1, N_DEV):
            peer, pid = _peer(pos, k)
            cp = pltpu.make_async_remote_copy(
                src_ref=p_ref, dst_ref=gath.at[me], send_sem=send_sems.at[k - 1], recv_sem=recv_sems.at[k - 1],
                device_id=peer, device_id_type=pl.DeviceIdType.MESH)
            cp.start()
            sends.append(cp)
            recvs.append(pltpu.make_async_remote_copy(
                src_ref=p_ref, dst_ref=gath.at[pid], send_sem=send_sems.at[k - 1], recv_sem=recv_sems.at[k - 1],
                device_id=peer, device_id_type=pl.DeviceIdType.MESH))
        for cp in recvs:
            cp.wait_recv()
        for cp in sends:
            cp.wait_send()
        g = gath[0]
        for p in range(1, N_DEV):
            g = g + gath[p]
        g_ref[...] = g
        d, nm, nv = _adam_vals(w_ref[...], g, m_ref[...], v_ref[...])
        d_ref[...] = d
        nm_ref[...] = nm
        nv_ref[...] = nv

    vm = pl.BlockSpec(memory_space=pltpu.VMEM)
    sds = jax.ShapeDtypeStruct((rows, LANES), F32)
    return pl.pallas_call(
        body, name=name, in_specs=[vm] * 4, out_specs=[vm] * 4, out_shape=[sds] * 4,
        scratch_shapes=[pltpu.VMEM((N_DEV, rows, LANES), F32), pltpu.SemaphoreType.DMA((N_DEV - 1,)),
                        pltpu.SemaphoreType.DMA((N_DEV - 1,))],
        compiler_params=pltpu.CompilerParams(has_side_effects=True))(part, w, m, v)


def _reduce_adam(recv, w, m, v, name):
    shape = w.shape
    c = shape[-1]
    r = int(np.prod(shape[:-1]))
    recv2, w2, m2, v2 = recv.reshape(N_DEV, r, c), w.reshape(r, c), m.reshape(r, c), v.reshape(r, c)
    tr = r
    while tr * c * 4 > (1 << 20) and tr % 16 == 0:
        tr //= 2

    def body(r_ref, w_ref, m_ref, v_ref, g_ref, d_ref, nm_ref, nv_ref):
        g = r_ref[0].astype(F32)
        for p in range(1, N_DEV):
            g = g + r_ref[p].astype(F32)
        g_ref[...] = g
        d, nm, nv = _adam_vals(w_ref[...], g, m_ref[...], v_ref[...])
        d_ref[...] = d
        nm_ref[...] = nm
        nv_ref[...] = nv

    spec = pl.BlockSpec((tr, c), lambda i: (i, 0))
    sds = jax.ShapeDtypeStruct((r, c), F32)
    outs = pl.pallas_call(
        body, name=name, grid=(r // tr,),
        in_specs=[pl.BlockSpec((N_DEV, tr, c), lambda i: (0, i, 0)), spec, spec, spec],
        out_specs=[spec] * 4, out_shape=[sds] * 4, compiler_params=_cp("parallel"))(recv2, w2, m2, v2)
    return [o.reshape(shape) for o in outs]


def _pad_heads(w, real=HEAD_DIM):
    lead = w.shape[:-1]
    w = w.reshape(lead + (HEADS, real))
    w = jnp.pad(w, [(0, 0)] * len(lead) + [(0, 0), (0, LANES - real)])
    return w.reshape(lead + (HP,))


def _unpad_heads(w, real=HEAD_DIM):
    lead = w.shape[:-1]
    return w.reshape(lead + (HEADS, LANES))[..., :real].reshape(lead + (HEADS * real,))


_IN_SEGS = (("fq", 0, 256), ("fk", 256, 512), ("fv", 512, 768), ("ff", 768, 772), ("cq", 772, 1028),
            ("ckv", 1028, 1156), ("kr", 1156, 1188), ("rq", 1188, 1444), ("rk", 1444, 1700), ("rv", 1700, 1956),
            ("rg", 1956, 2212), ("sq", 2212, 2468), ("sk", 2468, 2724), ("sv", 2724, 2980))


def _pad_w_in(w):
    s = {n: w[:, a:b] for n, a, b in _IN_SEGS}
    rows = w.shape[0]
    z = lambda n: jnp.zeros((rows, n), w.dtype)
    parts = [_pad_heads(s[n]) for n in ("fq", "fk", "fv", "rq", "rk", "rv", "sq", "sk", "sv", "rg")]
    parts += [s["cq"], s["ckv"], z(HEAD_DIM), s["kr"], z(LANES - HEAD_DIM - ROPE_MLA), s["ff"], z(LANES - HEADS),
              z(NP_IN - OFF_FF - LANES)]
    return jnp.concatenate(parts, axis=1)


def _unpad_w_in(wp):
    def heads(off):
        return _unpad_heads(wp[:, off:off + HP])

    parts = [heads(OFF_FOX), heads(OFF_FOX + HP), heads(OFF_FOX + 2 * HP), wp[:, OFF_FF:OFF_FF + HEADS],
             wp[:, OFF_CQ:OFF_CQ + Q_RANK], wp[:, OFF_CKV:OFF_CKV + KV_RANK],
             wp[:, OFF_KR + HEAD_DIM:OFF_KR + HEAD_DIM + ROPE_MLA],
             heads(OFF_RET), heads(OFF_RET + HP), heads(OFF_RET + 2 * HP), heads(OFF_RG),
             heads(OFF_SB), heads(OFF_SB + HP), heads(OFF_SB + 2 * HP)]
    return jnp.concatenate(parts, axis=1)


def _pad_w_kv(w):
    w4 = w.reshape(KV_RANK, HEADS, 2 * HEAD_DIM)
    k = w4[:, :, :HEAD_DIM].reshape(KV_RANK, GROUP)
    v = w4[:, :, HEAD_DIM:].reshape(KV_RANK, GROUP)
    return jnp.concatenate([_pad_heads(k), _pad_heads(v)], axis=1)


def _unpad_w_kv(wp):
    k = _unpad_heads(wp[:, :HP]).reshape(KV_RANK, HEADS, HEAD_DIM)
    v = _unpad_heads(wp[:, HP:]).reshape(KV_RANK, HEADS, HEAD_DIM)
    return jnp.concatenate([k, v], axis=-1).reshape(KV_RANK, HEADS * 2 * HEAD_DIM)


def _pad_rows_out(w):
    w = w.reshape(4 * HEADS, HEAD_DIM, D_MODEL)
    w = jnp.pad(w, ((0, 0), (0, LANES - HEAD_DIM), (0, 0)))
    return w.reshape(4 * HP, D_MODEL)


def _unpad_rows_out(wp):
    return wp.reshape(4 * HEADS, LANES, D_MODEL)[:, :HEAD_DIM, :].reshape(D_MODEL, D_MODEL)


def _pad_gain_out(g):
    g = jnp.pad(g.reshape(4 * HEADS, HEAD_DIM), ((0, 0), (0, LANES - HEAD_DIM)))
    return g.reshape(1, 4 * HP)


def _unpad_gain_out(gp):
    return gp.reshape(4 * HEADS, LANES)[:, :HEAD_DIM].reshape(D_MODEL)


_SMALL = (("g_mix_pre", 1024), ("g_mix_post", 1024), ("g_ffn_pre", 1024), ("g_ffn_post", 1024), ("g_mix_out", 1024),
          ("g_q_lora", 256), ("g_kv_lora", 128), ("b_forget", 4))


def _pack_small(vals):
    parts = []
    for name, n in _SMALL:
        a = vals[name].astype(F32)
        if n < LANES:
            a = jnp.pad(a, ((0, 0), (0, LANES - n)))
        parts.append(a)
    return jnp.concatenate(parts, axis=1).reshape(DEPTH * SMALL_ROWS, LANES)


def _unpack_small(packed):
    flat = packed.reshape(DEPTH, SMALL_ROWS * LANES)
    out, off = {}, 0
    for name, n in _SMALL:
        out[name] = flat[:, off:off + n]
        off += max(n, LANES)
    return out


def kernel(x, positions, g_mix_pre, w_in, b_forget, g_q_lora, w_q_up, g_kv_lora, w_kv_up, g_mix_out, w_out, g_mix_post, g_ffn_pre, w_ffn_up, w_ffn_down, g_ffn_post, loss_target, m_g_mix_pre, m_w_in, m_b_forget, m_g_q_lora, m_w_q_up, m_g_kv_lora, m_w_kv_up, m_g_mix_out, m_w_out, m_g_mix_post, m_g_ffn_pre, m_w_ffn_up, m_w_ffn_down, m_g_ffn_post, v_g_mix_pre, v_w_in, v_b_forget, v_g_q_lora, v_w_q_up, v_g_kv_lora, v_w_kv_up, v_g_mix_out, v_w_out, v_g_mix_post, v_g_ffn_pre, v_w_ffn_up, v_w_ffn_down, v_g_ffn_post):
    t = x.shape[1]
    nq = t // BQ
    x0 = x[0]
    tgt = loss_target[0]
    pos = positions[0].astype(F32).reshape(t, 1)

    half_r, half_m = HEAD_DIM // 2, ROPE_MLA // 2
    invf_r = ROPE_BASE ** (-jnp.arange(half_r, dtype=F32) / half_r)
    invf_m = ROPE_BASE ** (-jnp.arange(half_m, dtype=F32) / half_m)
    blk_r = jnp.concatenate([invf_r, invf_r, jnp.zeros((LANES - HEAD_DIM,), F32)])
    blk_m = jnp.concatenate([jnp.zeros((HEAD_DIM,), F32), invf_m, invf_m, jnp.zeros((LANES - HEAD_DIM - ROPE_MLA,), F32)])
    invf_ret = jnp.tile(blk_r, HEADS).reshape(1, HP)
    invf_mla = jnp.tile(blk_m, HEADS).reshape(1, HP)
    log_gamma = jnp.log1p(-jnp.power(2.0, -5.0 - jnp.arange(HEADS, dtype=F32)))
    lg_lanes = jnp.repeat(log_gamma, LANES).reshape(1, HP)
    lg_heads = jnp.broadcast_to(log_gamma[:, None, None], (HEADS, 1, LANES))

    big = [w_in, w_q_up, w_kv_up, w_out, w_ffn_up, w_ffn_down]
    gat = _exchange([w.astype(BF16) for w in big], True, "weights_all_gather")
    win_g = gat[0].transpose(1, 0, 2, 3).reshape(DEPTH, D_MODEL, D_IN)
    wq_g = gat[1].transpose(1, 2, 0, 3).reshape(DEPTH, Q_RANK, 384)
    wkv_g = gat[2].transpose(1, 2, 0, 3).reshape(DEPTH, KV_RANK, 512)
    wout_g = gat[3].transpose(1, 0, 2, 3).reshape(DEPTH, D_MODEL, D_MODEL)
    wup_g = gat[4].transpose(1, 2, 0, 3).reshape(DEPTH, D_MODEL, D_FF)
    wdn_g = gat[5].transpose(1, 0, 2, 3).reshape(DEPTH, D_FF, D_MODEL)

    row = lambda g: g.reshape(1, -1)
    layers = []
    for l in range(DEPTH):
        layers.append(dict(
            win=_pad_w_in(win_g[l]), wq=_pad_heads(wq_g[l], 96), wkv=_pad_w_kv(wkv_g[l]),
            wout=_pad_rows_out(wout_g[l]), wup=wup_g[l], wdn=wdn_g[l],
            g_pre=row(g_mix_pre[l]), g_post=row(g_mix_post[l]), g_fpre=row(g_ffn_pre[l]), g_fpost=row(g_ffn_post[l]),
            g_out=_pad_gain_out(g_mix_out[l]), g_q=row(g_q_lora[l]), g_kv=row(g_kv_lora[l]),
            b_pad=jnp.pad(b_forget[l], (0, LANES - HEADS)).reshape(1, LANES)))

    saved = []
    xin = x0
    h = _rms_fwd(xin, layers[0]["g_pre"], "rms_pre_0")
    loss_row = dx = None
    for l, p in enumerate(layers):
        s = dict(x=xin, h=h)
        proj = _mm(h, p["win"], name=f"in_proj_{l}")
        fox, ret, sb, mla, lsf = _prep_fwd(proj, pos, invf_ret, invf_mla, lg_lanes, p["b_pad"], p["g_q"], p["g_kv"],
                                           p["wq"], p["wkv"], f"prep_fwd_{l}")
        cum = _cumsum([lsf], False, f"forget_cumsum_{l}")[:, :HEADS]
        cum_bc = jnp.broadcast_to(cum[:, :, None], (t, HEADS, LANES)).reshape(t, HP)
        cum_row = cum.T.reshape(HEADS, nq, 1, BQ)
        oa, lse_a = _softmax_fwd(fox, cum_bc, cum_row, bias=True, chunk_mask=False, scale=1.0, name=f"fox_fwd_{l}")
        ob, lse_b = _softmax_fwd(mla, None, None, bias=False, chunk_mask=True, scale=96.0 ** -0.5, name=f"mla_fwd_{l}")
        oc = _ret_fwd(ret, lg_heads, f"ret_fwd_{l}")
        od = _sb_fwd(sb, f"sb_fwd_{l}")
        mixed = _post_fwd(oa, ob, oc, od, proj, p["g_out"], f"post_fwd_{l}")
        mix = _mm(mixed, p["wout"], name=f"out_proj_{l}", tk=2048)
        x1, h2 = _add_rms_fwd(xin, mix, p["g_post"], p["g_fpre"], f"mix_residual_{l}")
        a = _mm(h2, p["wup"], name=f"ffn_up_{l}")
        y = _mm(a, p["wdn"], name=f"ffn_down_{l}", a_fn=_relu2)
        s.update(proj=proj, fox=fox, ret=ret, sb=sb, mla=mla, cum_bc=cum_bc, cum_row=cum_row, oa=oa, ob=ob, oc=oc,
                 od=od, lse_a=lse_a, lse_b=lse_b, mixed=mixed, mix=mix, x1=x1, h2=h2, a=a, y=y)
        saved.append(s)
        if l + 1 < DEPTH:
            xin, h = _add_rms_fwd(x1, y, p["g_fpost"], layers[l + 1]["g_pre"], f"ffn_residual_{l}")
        else:
            loss_row, dx = _final_loss(x1, y, p["g_fpost"], tgt, "loss")
    loss = lax.psum(loss_row[0, 0], ("x", "y", "c"))

    small_g = {n: [None] * DEPTH for n, _ in _SMALL}
    big_g = [[None] * DEPTH for _ in range(6)]
    for l in reversed(range(DEPTH)):
        p, s = layers[l], saved[l]
        dy, dg = _norm_bwd(dx, s["y"], p["g_fpost"], None, BF16, f"ffn_post_bwd_{l}")
        small_g["g_ffn_post"][l] = dg
        da = _mm(dy, p["wdn"], name=f"ffn_down_dx_{l}", tb=True, out_dtype=BF16, epi=_drelu2, epi_in=s["a"])
        big_g[5][l] = _mm(s["a"], dy, name=f"ffn_down_dw_{l}", ta=True, a_fn=_relu2, tm=1024, tk=512)
        big_g[4][l] = _mm(s["h2"], da, name=f"ffn_up_dw_{l}", ta=True, tm=1024, tk=512)
        dh2 = _mm(da, p["wup"], name=f"ffn_up_dx_{l}", tb=True)
        dx1, dg = _norm_bwd(dh2, s["x1"], p["g_fpre"], dx, F32, f"ffn_pre_bwd_{l}")
        small_g["g_ffn_pre"][l] = dg
        dmix, dg = _norm_bwd(dx1, s["mix"], p["g_post"], None, BF16, f"mix_post_bwd_{l}")
        small_g["g_mix_post"][l] = dg
        dmixed = _mm(dmix, p["wout"], name=f"out_proj_dx_{l}", tb=True)
        big_g[3][l] = _unpad_rows_out(_mm(s["mixed"], dmix, name=f"out_proj_dw_{l}", ta=True, tm=1024, tk=512))
        doa, dob, doc, dod, dla, dlb, drg, dgo = _post_bwd(dmixed, s["oa"], s["ob"], s["oc"], s["od"], s["proj"],
                                                           p["g_out"], f"post_bwd_{l}")
        small_g["g_mix_out"][l] = _unpad_gain_out(dgo).reshape(1, D_MODEL)
        dfq, dfk, dfv, dcum, dcum_q = _softmax_bwd(s["fox"], s["cum_bc"], s["cum_row"], doa, s["lse_a"], dla, bias=True,
                                           chunk_mask=False, scale=1.0, name=f"fox_bwd_{l}")
        dmq, dmk, dmv = _softmax_bwd(s["mla"], None, None, dob, s["lse_b"], dlb, bias=False, chunk_mask=True,
                                     scale=96.0 ** -0.5, name=f"mla_bwd_{l}")
        drq, drk, drv = _ret_bwd(s["ret"], lg_heads, doc, f"ret_bwd_{l}")
        dsq, dsk, dsv = _sb_bwd(s["sb"], dod, f"sb_bwd_{l}")
        pad_heads = lambda a: jnp.pad(a, ((0, 0), (0, LANES - HEADS)))
        dcum_k = pad_heads(dcum.reshape(HEADS, t).T)
        dcum_q = pad_heads(dcum_q.reshape(t, HEADS, LANES)[:, :, 0])
        dlsf = _cumsum([dcum_k, dcum_q], True, f"forget_cumsum_bwd_{l}")
        dproj, dwq, dwkv, dgq, dgkv, dbf = _prep_bwd(
            (dfq, dfk, dfv), (drq, drk, drv), (dsq, dsk, dsv), (dmq, dmk, dmv), drg, dlsf, s["proj"], pos, invf_ret,
            invf_mla, lg_lanes, p["b_pad"], p["g_q"], p["g_kv"], p["wq"], p["wkv"], f"prep_bwd_{l}")
        small_g["g_q_lora"][l] = dgq
        small_g["g_kv_lora"][l] = dgkv
        small_g["b_forget"][l] = dbf[:, :HEADS]
        big_g[1][l] = _unpad_heads(dwq, 96)
        big_g[2][l] = _unpad_w_kv(dwkv)
        big_g[0][l] = _unpad_w_in(_mm(s["h"], dproj, name=f"in_proj_dw_{l}", ta=True, tm=1024, tk=512))
        dh = _mm(dproj, p["win"], name=f"in_proj_dx_{l}", tb=True)
        dx, dg = _norm_bwd(dh, s["x"], p["g_pre"], dx1, F32, f"mix_pre_bwd_{l}")
        small_g["g_mix_pre"][l] = dg
    grad_x = dx.reshape(1, t, D_MODEL)

    st = lambda k: jnp.stack(big_g[k])
    send = [
        st(0).reshape(DEPTH, N_DEV, D_MODEL // N_DEV, D_IN).transpose(1, 0, 2, 3),
        st(1).reshape(DEPTH, Q_RANK, N_DEV, 384 // N_DEV).transpose(2, 0, 1, 3),
        st(2).reshape(DEPTH, KV_RANK, N_DEV, 512 // N_DEV).transpose(2, 0, 1, 3),
        st(3).reshape(DEPTH, N_DEV, D_MODEL // N_DEV, D_MODEL).transpose(1, 0, 2, 3),
        st(4).reshape(DEPTH, D_MODEL, N_DEV, D_FF // N_DEV).transpose(2, 0, 1, 3),
        st(5).reshape(DEPTH, N_DEV, D_FF // N_DEV, D_MODEL).transpose(1, 0, 2, 3),
    ]
    recv = _exchange([a.astype(BF16) for a in send], False, "grads_all_to_all")
    ms = [m_w_in, m_w_q_up, m_w_kv_up, m_w_out, m_w_ffn_up, m_w_ffn_down]
    vs = [v_w_in, v_w_q_up, v_w_kv_up, v_w_out, v_w_ffn_up, v_w_ffn_down]
    names = ["w_in", "w_q_up", "w_kv_up", "w_out", "w_ffn_up", "w_ffn_down"]
    res = {}
    for k, n in enumerate(names):
        res[n] = _reduce_adam(recv[k], big[k], ms[k], vs[k], f"adamw_{n}")

    small_w = dict(g_mix_pre=g_mix_pre, g_mix_post=g_mix_post, g_ffn_pre=g_ffn_pre, g_ffn_post=g_ffn_post,
                   g_mix_out=g_mix_out, g_q_lora=g_q_lora, g_kv_lora=g_kv_lora, b_forget=b_forget)
    small_m = dict(g_mix_pre=m_g_mix_pre, g_mix_post=m_g_mix_post, g_ffn_pre=m_g_ffn_pre, g_ffn_post=m_g_ffn_post,
                   g_mix_out=m_g_mix_out, g_q_lora=m_g_q_lora, g_kv_lora=m_g_kv_lora, b_forget=m_b_forget)
    small_v = dict(g_mix_pre=v_g_mix_pre, g_mix_post=v_g_mix_post, g_ffn_pre=v_g_ffn_pre, g_ffn_post=v_g_ffn_post,
                   g_mix_out=v_g_mix_out, g_q_lora=v_g_q_lora, g_kv_lora=v_g_kv_lora, b_forget=v_b_forget)
    part = _pack_small({n: jnp.concatenate(small_g[n], axis=0) for n, _ in _SMALL})
    sres = _small_allreduce_adam(part, _pack_small(small_w), _pack_small(small_m), _pack_small(small_v),
                                 "small_allreduce_adamw")
    sg, sd, sm, sv = [_unpack_small(a) for a in sres]
    for n, _ in _SMALL:
        res[n] = [sg[n], sd[n], sm[n], sv[n]]

    order = ["g_mix_pre", "w_in", "b_forget", "g_q_lora", "w_q_up", "g_kv_lora", "w_kv_up", "g_mix_out", "w_out",
             "g_mix_post", "g_ffn_pre", "w_ffn_up", "w_ffn_down", "g_ffn_post"]
    outs = [loss, grad_x]
    for idx in range(4):
        outs += [res[n][idx] for n in order]
    return tuple(outs)
```

```python
import functools
import math

import numpy as np
import jax
import jax.numpy as jnp
from jax import lax
from jax.experimental import pallas as pl
from jax.experimental.pallas import tpu as pltpu

F32 = jnp.float32
BF16 = jnp.bfloat16

D_MODEL = 1024
DEPTH = 2
N_DEV = 8
GROUP = 256
HEADS = 4
HEAD_DIM = 64
LANES = 128
HP = HEADS * LANES
QKV = 3 * HP
Q_RANK = 256
KV_RANK = 128
ROPE_MLA = 32
D_FF = 4096
D_IN = 2980
CHUNK_SHIFT = 6
EPS = 1e-6
ROPE_BASE = 10000.0
NEG = -1e30

OFF_FOX, OFF_RET, OFF_SB = 0, QKV, 2 * QKV
OFF_RG = 3 * QKV
OFF_CQ = OFF_RG + HP
OFF_CKV = OFF_CQ + Q_RANK
OFF_KR = OFF_CKV + LANES
OFF_FF = OFF_KR + LANES
NP_IN = 6144

BQ = 256
TKS = 128
TM = 256
VMEM_LIMIT = 48 * 1024 * 1024

ADAM_LR, ADAM_B1, ADAM_B2, ADAM_EPS, ADAM_WD, ADAM_STEP = 0.001, 0.9, 0.999, 1e-08, 0.01, 10
ADAM_C1 = 1.0 - ADAM_B1 ** ADAM_STEP
ADAM_C2 = 1.0 - ADAM_B2 ** ADAM_STEP

SMALL_ROWS = 44

NT = (((1,), (1,)), ((), ()))
TN = (((0,), (0,)), ((), ()))


def _cp(*sem):
    return pltpu.CompilerParams(dimension_semantics=sem if sem else None, vmem_limit_bytes=VMEM_LIMIT)


def _bdot(a, b, dn=None):
    if dn is None:
        return jnp.dot(a, b, preferred_element_type=F32)
    return lax.dot_general(a, b, dn, preferred_element_type=F32)


def _split2(x):
    hi = x.astype(BF16)
    lo = (x - hi.astype(F32)).astype(BF16)
    return hi, lo


def _mm(a, b, *, name, ta=False, tb=False, out_dtype=F32, a_fn=None, epi=None, epi_in=None,
        tm=512, tn=1024, tk=1024):
    m, k = (a.shape[1], a.shape[0]) if ta else a.shape
    n = b.shape[0] if tb else b.shape[1]
    tm, tn, tk = min(tm, m), min(tn, n), min(tk, k)
    assert m % tm == 0 and n % tn == 0 and k % tk == 0, (name, m, n, k)
    nk = k // tk
    dn = (((0 if ta else 1,), (1 if tb else 0,)), ((), ()))

    def body(*refs):
        if epi is None:
            a_ref, b_ref, o_ref = refs[:3]
            e_ref = None
            rest = refs[3:]
        else:
            a_ref, b_ref, e_ref, o_ref = refs[:4]
            rest = refs[4:]
        av = a_ref[...]
        if a_fn is not None:
            av = a_fn(av)
        part = lax.dot_general(av.astype(BF16), b_ref[...].astype(BF16), dn, preferred_element_type=F32)

        def finish(r):
            if epi is not None:
                r = epi(r, e_ref[...])
            o_ref[...] = r.astype(out_dtype)

        if nk == 1:
            finish(part)
        else:
            acc_ref = rest[0]
            kk = pl.program_id(2)

            @pl.when(kk == 0)
            def _():
                acc_ref[...] = part

            @pl.when(kk > 0)
            def _():
                acc_ref[...] += part

            @pl.when(kk == nk - 1)
            def _():
                finish(acc_ref[...])

    a_spec = pl.BlockSpec((tk, tm), lambda i, j, kk: (kk, i)) if ta else pl.BlockSpec((tm, tk), lambda i, j, kk: (i, kk))
    b_spec = pl.BlockSpec((tn, tk), lambda i, j, kk: (j, kk)) if tb else pl.BlockSpec((tk, tn), lambda i, j, kk: (kk, j))
    o_spec = pl.BlockSpec((tm, tn), lambda i, j, kk: (i, j))
    in_specs = [a_spec, b_spec]
    args = [a, b]
    if epi is not None:
        in_specs.append(o_spec)
        args.append(epi_in)
    return pl.pallas_call(
        body, name=name, grid=(m // tm, n // tn, nk),
        in_specs=in_specs, out_specs=o_spec,
        out_shape=jax.ShapeDtypeStruct((m, n), out_dtype),
        scratch_shapes=[pltpu.VMEM((tm, tn), F32)] if nk > 1 else [],
        compiler_params=_cp("parallel", "parallel", "arbitrary"),
    )(*args)


def _relu2(v):
    r = jnp.maximum(v, 0.0)
    return r * r


def _drelu2(du, av):
    return du * (2.0 * jnp.maximum(av, 0.0))


def _rms(v, g):
    r = lax.rsqrt(jnp.mean(v * v, axis=-1, keepdims=True) + EPS)
    return v * r * g


def _row_spec(w):
    return pl.BlockSpec((TM, w), lambda i: (i, 0))


def _vec_spec(w):
    return pl.BlockSpec((1, w), lambda i: (0, 0))


def _rms_fwd(x, g, name):
    t, d = x.shape

    def body(x_ref, g_ref, h_ref):
        h_ref[...] = _rms(x_ref[...], g_ref[...]).astype(BF16)

    return pl.pallas_call(
        body, name=name, grid=(t // TM,), in_specs=[_row_spec(d), _vec_spec(d)], out_specs=_row_spec(d),
        out_shape=jax.ShapeDtypeStruct((t, d), BF16), compiler_params=_cp("parallel"))(x, g)


def _add_rms_fwd(x, y, g1, g2, name):
    t, d = x.shape

    def body(x_ref, y_ref, g1_ref, g2_ref, xn_ref, h_ref):
        xn = x_ref[...] + _rms(y_ref[...], g1_ref[...])
        xn_ref[...] = xn
        h_ref[...] = _rms(xn, g2_ref[...]).astype(BF16)

    return pl.pallas_call(
        body, name=name, grid=(t // TM,),
        in_specs=[_row_spec(d), _row_spec(d), _vec_spec(d), _vec_spec(d)],
        out_specs=[_row_spec(d), _row_spec(d)],
        out_shape=[jax.ShapeDtypeStruct((t, d), F32), jax.ShapeDtypeStruct((t, d), BF16)],
        compiler_params=_cp("parallel"))(x, y, g1, g2)


def _final_loss(x, y, g, tgt, name):
    t, d = x.shape

    def body(x_ref, y_ref, g_ref, t_ref, l_ref, dx_ref):
        @pl.when(pl.program_id(0) == 0)
        def _():
            l_ref[...] = jnp.zeros_like(l_ref)

        err = x_ref[...] + _rms(y_ref[...], g_ref[...]) - t_ref[...]
        dx_ref[...] = err * (1.0 / d)
        l_ref[...] += jnp.sum(jnp.sum(err * err, axis=1, keepdims=True), axis=0, keepdims=True) * (0.5 / d)

    return pl.pallas_call(
        body, name=name, grid=(t // TM,),
        in_specs=[_row_spec(d), _row_spec(d), _vec_spec(d), _row_spec(d)],
        out_specs=[pl.BlockSpec((1, LANES), lambda i: (0, 0)), _row_spec(d)],
        out_shape=[jax.ShapeDtypeStruct((1, LANES), F32), jax.ShapeDtypeStruct((t, d), F32)],
        compiler_params=_cp("arbitrary"))(x, y, g, tgt)


def _rms_bwd_vals(dn, v, g):
    w = v.shape[-1]
    r = lax.rsqrt(jnp.mean(v * v, axis=-1, keepdims=True) + EPS)
    vh = v * r
    dgp = jnp.sum(dn * vh, axis=0, keepdims=True)
    dvh = dn * g
    dv = r * (dvh - vh * (jnp.sum(dvh * vh, axis=-1, keepdims=True) * (1.0 / w)))
    return dv, dgp


def _norm_bwd(dn, v, g, resid, out_dtype, name):
    t, d = v.shape
    has_res = resid is not None

    def body(*refs):
        if has_res:
            dn_ref, v_ref, g_ref, r_ref, dv_ref, dg_ref = refs
        else:
            dn_ref, v_ref, g_ref, dv_ref, dg_ref = refs

        @pl.when(pl.program_id(0) == 0)
        def _():
            dg_ref[...] = jnp.zeros_like(dg_ref)

        dv, dgp = _rms_bwd_vals(dn_ref[...].astype(F32), v_ref[...], g_ref[...])
        if has_res:
            dv = dv + r_ref[...]
        dv_ref[...] = dv.astype(out_dtype)
        dg_ref[...] += dgp

    in_specs = [_row_spec(d), _row_spec(d), _vec_spec(d)] + ([_row_spec(d)] if has_res else [])
    args = [dn, v, g] + ([resid] if has_res else [])
    return pl.pallas_call(
        body, name=name, grid=(t // TM,), in_specs=in_specs,
        out_specs=[_row_spec(d), _vec_spec(d)],
        out_shape=[jax.ShapeDtypeStruct((t, d), out_dtype), jax.ShapeDtypeStruct((1, d), F32)],
        compiler_params=_cp("arbitrary"))(*args)


def _rope_tables(pos, invf, lo, half):
    w = invf.shape[-1]
    lane = lax.broadcasted_iota(jnp.int32, (pos.shape[0], w), 1) & (LANES - 1)
    first = (lane >= lo) & (lane < lo + half)
    second = (lane >= lo + half) & (lane < lo + 2 * half)
    ang = pos * invf
    active = first | second
    cos = jnp.where(active, jnp.cos(ang), 1.0)
    sin = jnp.where(active, jnp.sin(ang), 0.0)
    return cos, sin, first, second


def _rope_apply(v, cos, sin, first, second, half, sign):
    w = v.shape[-1]
    up = pltpu.roll(v, w - half, 1)
    dn = pltpu.roll(v, half, 1)
    rot = jnp.where(first, -up, jnp.where(second, dn, 0.0))
    return v * cos + rot * (sin * sign)


def _forget_lsf(proj, b_pad, name):
    t = proj.shape[0]

    def body(ff_ref, b_ref, o_ref):
        f = ff_ref[...] + b_ref[...]
        o_ref[...] = -(jnp.maximum(-f, 0.0) + jnp.log(1.0 + jnp.exp(-jnp.abs(f))))

    return pl.pallas_call(
        body, name=name, grid=(t // TM,),
        in_specs=[pl.BlockSpec((TM, LANES), lambda i: (i, OFF_FF // LANES)), _vec_spec(LANES)],
        out_specs=_row_spec(LANES), out_shape=jax.ShapeDtypeStruct((t, LANES), F32),
        compiler_params=_cp("parallel"))(proj, b_pad)


def _split3(c):
    hi = c.astype(BF16).astype(F32)
    mid = (c - hi).astype(BF16).astype(F32)
    return hi, mid, (c - hi) - mid


def _transposed_tiles(dst, row0, blocks, width):
    for b, blk in enumerate(blocks):
        bt = blk.T.astype(BF16)
        rows = bt.shape[0]
        for w in range(TM // width):
            dst[w, row0 + b * rows:row0 + (b + 1) * rows, :] = bt[:, w * width:(w + 1) * width]


def _prep_fwd(proj, cum, pos, invf_ret, invf_mla, lg_lanes, g_q, g_kv, wq_pad, wkv_pad, name):
    t = proj.shape[0]

    def body(fox_ref, ret_ref, sb_ref, cq_ref, ckv_ref, kr_ref, cum_ref, pos_ref, ifr_ref, ifm_ref, lg_ref,
             gq_ref, gkv_ref, wq_ref, wkv_ref,
             ofox_ref, oret_ref, osb_ref, omla_ref, ofoxt_ref, osbt_ref, omlat_ref):
        pos_v = pos_ref[...]
        osb_ref[:, 0:HP] = (sb_ref[:, 0:HP] * 0.125).astype(BF16)
        osb_ref[:, HP:QKV] = sb_ref[:, HP:QKV].astype(BF16)
        _transposed_tiles(osbt_ref, 0, [sb_ref[:, HP:2 * HP], sb_ref[:, 2 * HP:QKV]], TKS)
        lane = lax.broadcasted_iota(jnp.int32, (TM, LANES), 1)
        cumv = cum_ref[...]
        fq, fk = [], []
        for hb in range(HEADS):
            hi, mid, lo = _split3(cumv[:, hb:hb + 1])
            q = fox_ref[:, hb * LANES:(hb + 1) * LANES] * 0.125
            k = fox_ref[:, HP + hb * LANES:HP + (hb + 1) * LANES]
            ones_q = (lane >= HEAD_DIM) & (lane < HEAD_DIM + 3)
            ones_k = (lane >= HEAD_DIM + 3) & (lane < HEAD_DIM + 6)
            q = jnp.where(ones_q, 1.0, q)
            k = jnp.where(ones_k, 1.0, k)
            for n, part in enumerate((hi, mid, lo)):
                q = jnp.where(lane == HEAD_DIM + 3 + n, part, q)
                k = jnp.where(lane == HEAD_DIM + n, -part, k)
            fq.append(q)
            fk.append(k)
        fk = jnp.concatenate(fk, axis=1)
        ofox_ref[:, 0:HP] = jnp.concatenate(fq, axis=1).astype(BF16)
        ofox_ref[:, HP:2 * HP] = fk.astype(BF16)
        ofox_ref[:, 2 * HP:QKV] = fox_ref[:, 2 * HP:QKV].astype(BF16)
        _transposed_tiles(ofoxt_ref, 0, [fk, fox_ref[:, 2 * HP:QKV]], BQ)
        cos, sin, first, second = _rope_tables(pos_v, ifr_ref[...], 0, HEAD_DIM // 2)
        nloc = lax.broadcasted_iota(jnp.int32, (TM, 1), 0).astype(F32)
        dec = lg_ref[...] * nloc
        rq = _rope_apply(ret_ref[:, 0:HP], cos, sin, first, second, HEAD_DIM // 2, 1.0)
        rk = _rope_apply(ret_ref[:, HP:2 * HP], cos, sin, first, second, HEAD_DIM // 2, 1.0)
        oret_ref[:, 0:HP] = (rq * jnp.exp(dec)).astype(BF16)
        oret_ref[:, HP:2 * HP] = (rk * 0.125 * jnp.exp(-dec)).astype(BF16)
        oret_ref[:, 2 * HP:QKV] = ret_ref[:, 2 * HP:QKV].astype(BF16)
        cosm, sinm, firstm, secondm = _rope_tables(pos_v, ifm_ref[...], HEAD_DIM, ROPE_MLA // 2)
        cqn = _rms(cq_ref[...], gq_ref[...]).astype(BF16)
        qm = _bdot(cqn, wq_ref[...])
        omla_ref[:, 0:HP] = _rope_apply(qm, cosm, sinm, firstm, secondm, ROPE_MLA // 2, 1.0).astype(BF16)
        ckvn = _rms(ckv_ref[...], gkv_ref[...]).astype(BF16)
        kv = _bdot(ckvn, wkv_ref[...])
        krr = _rope_apply(kr_ref[...], cosm[:, 0:LANES], sinm[:, 0:LANES], firstm[:, 0:LANES],
                          secondm[:, 0:LANES], ROPE_MLA // 2, 1.0)
        mk = kv[:, 0:HP] + jnp.concatenate([krr] * HEADS, axis=1)
        omla_ref[:, HP:2 * HP] = mk.astype(BF16)
        omla_ref[:, 2 * HP:QKV] = kv[:, HP:2 * HP].astype(BF16)
        _transposed_tiles(omlat_ref, 0, [mk, kv[:, HP:2 * HP]], BQ)

    def seg(off, w):
        return pl.BlockSpec((TM, w), lambda i, o=off // w: (i, o))

    def full(shape):
        return pl.BlockSpec(shape, lambda i: (0,) * len(shape))

    def tiles(width):
        return pl.BlockSpec((TM // width, 2 * HP, width), lambda i: (i, 0, 0))

    in_specs = [seg(OFF_FOX, QKV), seg(OFF_RET, QKV), seg(OFF_SB, QKV), seg(OFF_CQ, Q_RANK), seg(OFF_CKV, LANES),
                seg(OFF_KR, LANES), _row_spec(LANES), pl.BlockSpec((TM, 1), lambda i: (i, 0)),
                full((1, HP)), full((1, HP)), full((1, HP)), full((1, Q_RANK)), full((1, KV_RANK)),
                full((Q_RANK, HP)), full((KV_RANK, 2 * HP))]
    out_specs = [_row_spec(QKV)] * 4 + [tiles(BQ), tiles(TKS), tiles(BQ)]
    out_shape = [jax.ShapeDtypeStruct((t, QKV), BF16)] * 4 + [
        jax.ShapeDtypeStruct((t // BQ, 2 * HP, BQ), BF16), jax.ShapeDtypeStruct((t // TKS, 2 * HP, TKS), BF16),
        jax.ShapeDtypeStruct((t // BQ, 2 * HP, BQ), BF16)]
    return pl.pallas_call(
        body, name=name, grid=(t // TM,), in_specs=in_specs, out_specs=out_specs, out_shape=out_shape,
        compiler_params=_cp("parallel"))(proj, proj, proj, proj, proj, proj, cum, pos, invf_ret, invf_mla, lg_lanes,
                                         g_q, g_kv, wq_pad, wkv_pad)


def _prep_bwd(dfox, dret, dsb, dmla, drg, dlsf, proj, pos, invf_ret, invf_mla, lg_lanes, b_pad, g_q, g_kv,
              wq_pad, wkv_pad, name):
    t = proj.shape[0]

    def body(dfq, dfk, dfv, drq, drk, drv, dsq, dsk, dsv, dmq, dmk, dmv, drg_ref, dlsf_ref,
             cq_ref, ckv_ref, ff_ref, pos_ref, ifr_ref, ifm_ref, lg_ref, b_ref, gq_ref, gkv_ref, wq_ref, wkv_ref,
             dp_ref, dwq_ref, dwkv_ref, dgq_ref, dgkv_ref, dbf_ref):
        @pl.when(pl.program_id(0) == 0)
        def _():
            dwq_ref[...] = jnp.zeros_like(dwq_ref)
            dwkv_ref[...] = jnp.zeros_like(dwkv_ref)
            dgq_ref[...] = jnp.zeros_like(dgq_ref)
            dgkv_ref[...] = jnp.zeros_like(dgkv_ref)
            dbf_ref[...] = jnp.zeros_like(dbf_ref)

        pos_v = pos_ref[...]
        for off, (dq, dk, dv) in ((OFF_FOX, (dfq, dfk, dfv)), (OFF_SB, (dsq, dsk, dsv))):
            dp_ref[:, off:off + HP] = (dq[...] * 0.125).astype(BF16)
            dp_ref[:, off + HP:off + 2 * HP] = dk[...].astype(BF16)
            dp_ref[:, off + 2 * HP:off + QKV] = dv[...].astype(BF16)
        cos, sin, first, second = _rope_tables(pos_v, ifr_ref[...], 0, HEAD_DIM // 2)
        nloc = lax.broadcasted_iota(jnp.int32, (TM, 1), 0).astype(F32)
        dec = lg_ref[...] * nloc
        dq = _rope_apply(drq[...] * jnp.exp(dec), cos, sin, first, second, HEAD_DIM // 2, -1.0)
        dk = _rope_apply(drk[...] * (0.125 * jnp.exp(-dec)), cos, sin, first, second, HEAD_DIM // 2, -1.0)
        dp_ref[:, OFF_RET:OFF_RET + HP] = dq.astype(BF16)
        dp_ref[:, OFF_RET + HP:OFF_RET + 2 * HP] = dk.astype(BF16)
        dp_ref[:, OFF_RET + 2 * HP:OFF_RET + QKV] = drv[...].astype(BF16)
        dp_ref[:, OFF_RG:OFF_RG + HP] = drg_ref[...].astype(BF16)
        cosm, sinm, firstm, secondm = _rope_tables(pos_v, ifm_ref[...], HEAD_DIM, ROPE_MLA // 2)
        dql = _rope_apply(dmq[...], cosm, sinm, firstm, secondm, ROPE_MLA // 2, -1.0).astype(BF16)
        cq = cq_ref[...]
        cqn = _rms(cq, gq_ref[...]).astype(BF16)
        dwq_ref[...] += _bdot(cqn, dql, TN)
        dcqn = _bdot(dql, wq_ref[...], NT)
        dcq, dgq = _rms_bwd_vals(dcqn, cq, gq_ref[...])
        dgq_ref[...] += dgq
        dp_ref[:, OFF_CQ:OFF_CQ + Q_RANK] = dcq.astype(BF16)
        dkm = dmk[...]
        dkv = jnp.concatenate([dkm, dmv[...]], axis=1).astype(BF16)
        ckv = ckv_ref[...]
        ckvn = _rms(ckv, gkv_ref[...]).astype(BF16)
        dwkv_ref[...] += _bdot(ckvn, dkv, TN)
        dckvn = _bdot(dkv, wkv_ref[...], NT)
        dckv, dgkv = _rms_bwd_vals(dckvn, ckv, gkv_ref[...])
        dgkv_ref[...] += dgkv
        dp_ref[:, OFF_CKV:OFF_CKV + LANES] = dckv.astype(BF16)
        dkr = dkm[:, 0:LANES] + dkm[:, LANES:2 * LANES] + dkm[:, 2 * LANES:3 * LANES] + dkm[:, 3 * LANES:HP]
        act = firstm[:, 0:LANES] | secondm[:, 0:LANES]
        dkr = jnp.where(act, dkr, 0.0)
        dkr = _rope_apply(dkr, cosm[:, 0:LANES], sinm[:, 0:LANES], firstm[:, 0:LANES], secondm[:, 0:LANES],
                          ROPE_MLA // 2, -1.0)
        dp_ref[:, OFF_KR:OFF_KR + LANES] = dkr.astype(BF16)
        f = ff_ref[...] + b_ref[...]
        dff = dlsf_ref[...] / (1.0 + jnp.exp(f))
        dbf_ref[...] += jnp.sum(dff, axis=0, keepdims=True)
        dp_ref[:, OFF_FF:OFF_FF + LANES] = dff.astype(BF16)
        dp_ref[:, OFF_FF + LANES:NP_IN] = jnp.zeros((TM, NP_IN - OFF_FF - LANES), BF16)

    def seg(off, w):
        return pl.BlockSpec((TM, w), lambda i, o=off // w: (i, o))

    def full(shape):
        return pl.BlockSpec(shape, lambda i: (0,) * len(shape))

    hp_spec = _row_spec(HP)
    in_specs = [hp_spec] * 13 + [_row_spec(LANES), seg(OFF_CQ, Q_RANK), seg(OFF_CKV, LANES), seg(OFF_FF, LANES),
                                 pl.BlockSpec((TM, 1), lambda i: (i, 0)),
                                 full((1, HP)), full((1, HP)), full((1, HP)), full((1, LANES)), full((1, Q_RANK)),
                                 full((1, KV_RANK)), full((Q_RANK, HP)), full((KV_RANK, 2 * HP))]
    out_specs = [_row_spec(NP_IN), full((Q_RANK, HP)), full((KV_RANK, 2 * HP)), full((1, Q_RANK)),
                 full((1, KV_RANK)), full((1, LANES))]
    out_shape = [jax.ShapeDtypeStruct((t, NP_IN), BF16), jax.ShapeDtypeStruct((Q_RANK, HP), F32),
                 jax.ShapeDtypeStruct((KV_RANK, 2 * HP), F32), jax.ShapeDtypeStruct((1, Q_RANK), F32),
                 jax.ShapeDtypeStruct((1, KV_RANK), F32), jax.ShapeDtypeStruct((1, LANES), F32)]
    return pl.pallas_call(
        body, name=name, grid=(t // TM,), in_specs=in_specs, out_specs=out_specs, out_shape=out_shape,
        compiler_params=_cp("arbitrary"))(*dfox, *dret, *dsb, *dmla, drg, dlsf, proj, proj, proj, pos, invf_ret,
                                          invf_mla, lg_lanes, b_pad, g_q, g_kv, wq_pad, wkv_pad)


def _cumsum(x, reverse, name, partials=None):
    t, w = x.shape
    n = t // TM
    xs = [x] if partials is None else [x, partials]

    def body(*refs):
        x_refs, o_ref, carry = refs[:len(xs)], refs[len(xs)], refs[len(xs) + 1]

        @pl.when(pl.program_id(0) == 0)
        def _():
            carry[...] = jnp.zeros_like(carry)

        r = lax.broadcasted_iota(jnp.int32, (TM, TM), 0)
        c = lax.broadcasted_iota(jnp.int32, (TM, TM), 1)
        tri = jnp.where((r <= c) if reverse else (r >= c), 1.0, 0.0).astype(BF16)
        v = x_refs[0][...]
        if partials is not None:
            lane = lax.broadcasted_iota(jnp.int32, (TM, LANES), 1)
            for hb in range(HEADS):
                v = v + jnp.where(lane == hb, jnp.sum(x_refs[1][:, _hs(hb)], axis=1, keepdims=True), 0.0)
        hi = v.astype(BF16)
        r1 = v - hi.astype(F32)
        mid = r1.astype(BF16)
        lo = (r1 - mid.astype(F32)).astype(BF16)
        cs = _bdot(tri, hi) + _bdot(tri, mid) + _bdot(tri, lo) + carry[...]
        o_ref[...] = cs
        carry[...] = cs[0:1, :] if reverse else cs[TM - 1:TM, :]

    imap = (lambda i: (n - 1 - i, 0)) if reverse else (lambda i: (i, 0))
    return pl.pallas_call(
        body, name=name, grid=(n,), in_specs=[pl.BlockSpec((TM, a.shape[1]), imap) for a in xs],
        out_specs=pl.BlockSpec((TM, w), imap),
        out_shape=jax.ShapeDtypeStruct((t, w), F32), scratch_shapes=[pltpu.VMEM((1, w), F32)],
        compiler_params=_cp("arbitrary"))(*xs)


HB_FWD = 4
HB_BWD = 2


def _q_spec(hb):
    return pl.BlockSpec((BQ, hb * LANES), lambda g, i: (i, g))


def _kv_spec(t, which, hb):
    return pl.BlockSpec((t, hb * LANES), lambda g, i, w=which: (0, w * (HEADS // hb) + g))


def _acc_spec(t, hb):
    return pl.BlockSpec((t, hb * LANES), lambda g, i: (0, g))


def _rowvec_spec(nkv, hb):
    return pl.BlockSpec((hb, nkv, 1, BQ), lambda g, i: (g, 0, 0, 0))


def _hs(hh):
    return slice(hh * LANES, (hh + 1) * LANES)


def _tile_iota(rows, cols):
    return (lax.broadcasted_iota(jnp.int32, (rows, cols), 0), lax.broadcasted_iota(jnp.int32, (rows, cols), 1))


def _kvt_spec(nkv, width, which, hb):
    return pl.BlockSpec((nkv, hb * LANES, width), lambda g, i, w=which: (0, w * (HEADS // hb) + g, 0))


def _qrow_spec(hb):
    return pl.BlockSpec((hb, 1, 1, BQ), lambda g, i: (g, i, 0, 0))


def _vis_t(chunk_mask):
    r, c = _tile_iota(BQ, BQ)
    return ((r >> CHUNK_SHIFT) <= (c >> CHUNK_SHIFT)) if chunk_mask else (r <= c)


def _softmax_fwd(qkv, kvt, *, chunk_mask, scale, name):
    t = qkv.shape[0]
    nq = t // BQ
    hb = HB_FWD

    def body(q_ref, k_ref, vt_ref, o_ref, lse_ref, m_sc, l_sc, acc_sc):
        i = pl.program_id(1)
        m_sc[...] = jnp.full((hb, 1, BQ), NEG, F32)
        l_sc[...] = jnp.zeros((hb, 1, BQ), F32)
        acc_sc[...] = jnp.zeros((hb, LANES, BQ), F32)

        def tile(j, masked):
            off = pl.multiple_of(j * BQ, BQ)
            vis = _vis_t(chunk_mask) if masked else None
            for hh in range(hb):
                sl = _hs(hh)
                s = _bdot(k_ref[pl.ds(off, BQ), sl], q_ref[:, sl], NT)
                if scale != 1.0:
                    s = s * scale
                if masked:
                    s = jnp.where(vis, s, NEG)
                m_old = m_sc[hh]
                m_new = jnp.maximum(m_old, jnp.max(s, axis=0, keepdims=True))
                alpha = jnp.exp(m_old - m_new)
                p = jnp.exp(s - m_new)
                l_sc[hh] = alpha * l_sc[hh] + jnp.sum(p, axis=0, keepdims=True)
                acc_sc[hh] = alpha * acc_sc[hh] + _bdot(vt_ref[j, sl, :], p.astype(BF16))
                m_sc[hh] = m_new

        def loop(j, carry):
            tile(j, False)
            return carry

        lax.fori_loop(0, i, loop, 0)
        tile(i, True)
        for hh in range(hb):
            l = l_sc[hh]
            o_ref[:, _hs(hh)] = (acc_sc[hh] / l).T
            lse_ref[hh, 0] = m_sc[hh] + jnp.log(l)

    return pl.pallas_call(
        body, name=name, grid=(HEADS // hb, nq),
        in_specs=[_q_spec(hb), _kv_spec(t, 1, hb), _kvt_spec(nq, BQ, 1, hb)],
        out_specs=[_q_spec(hb), _qrow_spec(hb)],
        out_shape=[jax.ShapeDtypeStruct((t, HP), F32), jax.ShapeDtypeStruct((HEADS, nq, 1, BQ), F32)],
        scratch_shapes=[pltpu.VMEM((hb, 1, BQ), F32), pltpu.VMEM((hb, 1, BQ), F32), pltpu.VMEM((hb, LANES, BQ), F32)],
        compiler_params=_cp("parallel", "arbitrary"))(qkv, qkv, kvt)


def _softmax_bwd(qkv, kvt, do, lse, delta, *, bias, chunk_mask, scale, name):
    t = qkv.shape[0]
    nq = t // BQ
    hb = HB_BWD

    def body(*refs):
        if bias:
            (q_ref, k_ref, v_ref, kt_ref, do_ref, lse_ref, dl_ref, dq_ref, dk_ref, dv_ref, dck_ref, dcq_ref,
             dq_sc, dcq_sc) = refs
            dcq_sc[...] = jnp.zeros((hb, 1, BQ), F32)
        else:
            q_ref, k_ref, v_ref, kt_ref, do_ref, lse_ref, dl_ref, dq_ref, dk_ref, dv_ref, dq_sc = refs
        i = pl.program_id(1)

        @pl.when(i == 0)
        def _():
            dk_ref[...] = jnp.zeros_like(dk_ref)
            dv_ref[...] = jnp.zeros_like(dv_ref)
            if bias:
                dck_ref[...] = jnp.zeros_like(dck_ref)

        dq_sc[...] = jnp.zeros((hb, LANES, BQ), F32)

        def tile(j, masked):
            off = pl.multiple_of(j * BQ, BQ)
            vis = _vis_t(chunk_mask) if masked else None
            for hh in range(hb):
                sl = _hs(hh)
                q = q_ref[:, sl]
                dob = do_ref[:, sl].astype(BF16)
                s = _bdot(k_ref[pl.ds(off, BQ), sl], q, NT)
                if scale != 1.0:
                    s = s * scale
                p = jnp.exp(s - lse_ref[hh, 0])
                if masked:
                    p = jnp.where(vis, p, 0.0)
                dp = _bdot(v_ref[pl.ds(off, BQ), sl], dob, NT)
                ds = p * (dp - dl_ref[hh, 0])
                if bias:
                    dck_ref[pl.ds(off, BQ), sl] -= ds[:, 0:LANES] + ds[:, LANES:BQ]
                    dcq_sc[hh] += jnp.sum(ds, axis=0, keepdims=True)
                if scale != 1.0:
                    ds = ds * scale
                dsb = ds.astype(BF16)
                dv_ref[pl.ds(off, BQ), sl] += _bdot(p.astype(BF16), dob)
                dk_ref[pl.ds(off, BQ), sl] += _bdot(dsb, q)
                dq_sc[hh] += _bdot(kt_ref[j, sl, :], dsb)

        def loop(j, carry):
            tile(j, False)
            return carry

        lax.fori_loop(0, i, loop, 0)
        tile(i, True)
        for hh in range(hb):
            dq_ref[:, _hs(hh)] = dq_sc[hh].T
            if bias:
                dcq_ref[hh, 0] = dcq_sc[hh]

    in_specs = [_q_spec(hb), _kv_spec(t, 1, hb), _kv_spec(t, 2, hb), _kvt_spec(nq, BQ, 0, hb), _q_spec(hb),
                _qrow_spec(hb), _qrow_spec(hb)]
    out_specs = [_q_spec(hb), _acc_spec(t, hb), _acc_spec(t, hb)]
    out_shape = [jax.ShapeDtypeStruct((t, HP), F32)] * 3
    scratch = [pltpu.VMEM((hb, LANES, BQ), F32)]
    if bias:
        out_specs += [_acc_spec(t, hb), _qrow_spec(hb)]
        out_shape += [jax.ShapeDtypeStruct((t, HP), F32), jax.ShapeDtypeStruct((HEADS, nq, 1, BQ), F32)]
        scratch.append(pltpu.VMEM((hb, 1, BQ), F32))
    return pl.pallas_call(
        body, name=name, grid=(HEADS // hb, nq), in_specs=in_specs, out_specs=out_specs, out_shape=out_shape,
        scratch_shapes=scratch,
        compiler_params=_cp("parallel", "arbitrary"))(qkv, qkv, qkv, kvt, do, lse, delta)


def _ret_diag_decay(lg1):
    r, c = _tile_iota(BQ, BQ)
    dd = jnp.where(c > r, jnp.exp((2.0 * lg1) * (c - r).astype(F32)), 1.0)
    return jnp.where((c >> CHUNK_SHIFT) <= (r >> CHUNK_SHIFT), dd, 0.0)


def _lg_spec(hb):
    return pl.BlockSpec((hb, 1, LANES), lambda g, i: (g, 0, 0))


def _ret_fwd(qkv, lg_heads, name):
    t = qkv.shape[0]
    nq = t // BQ
    hb = HB_FWD

    def body(lg_ref, q_ref, k_ref, v_ref, o_ref, acc_sc):
        i = pl.program_id(1)
        acc_sc[...] = jnp.zeros((hb, BQ, LANES), F32)

        def loop(j, carry):
            off = pl.multiple_of(j * BQ, BQ)
            span = ((i - j) * BQ).astype(F32)
            for hh in range(hb):
                sl = _hs(hh)
                a = _bdot(q_ref[:, sl], k_ref[pl.ds(off, BQ), sl], NT)
                cf = jnp.exp(lg_ref[hh][:, 0:1] * span)
                acc_sc[hh] += cf * _bdot(a.astype(BF16), v_ref[pl.ds(off, BQ), sl])
            return carry

        lax.fori_loop(0, i, loop, 0)
        off = pl.multiple_of(i * BQ, BQ)
        for hh in range(hb):
            sl = _hs(hh)
            a = _bdot(q_ref[:, sl], k_ref[pl.ds(off, BQ), sl], NT) * _ret_diag_decay(lg_ref[hh][:, 0:1])
            o_ref[:, sl] = acc_sc[hh] + _bdot(a.astype(BF16), v_ref[pl.ds(off, BQ), sl])

    return pl.pallas_call(
        body, name=name, grid=(HEADS // hb, nq),
        in_specs=[_lg_spec(hb), _q_spec(hb), _kv_spec(t, 1, hb), _kv_spec(t, 2, hb)],
        out_specs=_q_spec(hb), out_shape=jax.ShapeDtypeStruct((t, HP), F32),
        scratch_shapes=[pltpu.VMEM((hb, BQ, LANES), F32)],
        compiler_params=_cp("parallel", "arbitrary"))(lg_heads, qkv, qkv, qkv)


def _ret_bwd(qkv, lg_heads, do, name):
    t = qkv.shape[0]
    nq = t // BQ
    hb = HB_BWD

    def body(lg_ref, q_ref, k_ref, v_ref, do_ref, dq_ref, dk_ref, dv_ref, dq_sc):
        i = pl.program_id(1)

        @pl.when(i == 0)
        def _():
            dk_ref[...] = jnp.zeros_like(dk_ref)
            dv_ref[...] = jnp.zeros_like(dv_ref)

        dq_sc[...] = jnp.zeros((hb, BQ, LANES), F32)

        def tile(j, diag):
            off = pl.multiple_of(j * BQ, BQ)
            for hh in range(hb):
                sl = _hs(hh)
                lg1 = lg_ref[hh][:, 0:1]
                q = q_ref[:, sl]
                dob = do_ref[:, sl].astype(BF16)
                kb = k_ref[pl.ds(off, BQ), sl]
                vb = v_ref[pl.ds(off, BQ), sl]
                a = _bdot(q, kb, NT)
                da = _bdot(dob, vb, NT)
                if diag:
                    dd = _ret_diag_decay(lg1)
                    a, da, cf = a * dd, da * dd, 1.0
                else:
                    cf = jnp.exp(lg1 * ((i - j) * BQ).astype(F32))
                ab, dab = a.astype(BF16), da.astype(BF16)
                dv_ref[pl.ds(off, BQ), sl] += cf * _bdot(ab, dob, TN)
                dk_ref[pl.ds(off, BQ), sl] += cf * _bdot(dab, q, TN)
                dq_sc[hh] += cf * _bdot(dab, kb)

        def loop(j, carry):
            tile(j, False)
            return carry

        lax.fori_loop(0, i, loop, 0)
        tile(i, True)
        for hh in range(hb):
            dq_ref[:, _hs(hh)] = dq_sc[hh]

    return pl.pallas_call(
        body, name=name, grid=(HEADS // hb, nq),
        in_specs=[_lg_spec(hb), _q_spec(hb), _kv_spec(t, 1, hb), _kv_spec(t, 2, hb), _q_spec(hb)],
        out_specs=[_q_spec(hb), _acc_spec(t, hb), _acc_spec(t, hb)],
        out_shape=[jax.ShapeDtypeStruct((t, HP), F32)] * 3,
        scratch_shapes=[pltpu.VMEM((hb, BQ, LANES), F32)],
        compiler_params=_cp("parallel", "arbitrary"))(lg_heads, qkv, qkv, qkv, do)


def _sb_tile_logs(q, kb, vis):
    z = _bdot(kb, q, NT)
    ls = -(jnp.maximum(z, 0.0) + jnp.log(1.0 + jnp.exp(-jnp.abs(z))))
    if vis is not None:
        ls = jnp.where(vis, ls, 0.0)
    return z, ls


def _sb_vis(i, j):
    r, c = _tile_iota(TKS, BQ)
    return (j * TKS + r) < (i * BQ + c)


def _sb_later(ls, after):
    hi, lo = _split2(ls)
    return _bdot(after, hi) + _bdot(after, lo)


def _sb_fwd(qkv, kvt, name):
    t = qkv.shape[0]
    nq = t // BQ
    per = BQ // TKS
    hb = HB_FWD

    def body(q_ref, k_ref, vt_ref, o_ref, tot_ref, acc_sc, r_sc):
        i = pl.program_id(1)
        acc_sc[...] = jnp.zeros((hb, LANES, BQ), F32)
        r_sc[...] = jnp.zeros((hb, 1, BQ), F32)
        mr, mc = _tile_iota(TKS, TKS)
        after = jnp.where(mc > mr, 1.0, 0.0).astype(BF16)

        def tile(j, masked):
            off = pl.multiple_of(j * TKS, TKS)
            vis = _sb_vis(i, j) if masked else None
            for hh in range(hb):
                sl = _hs(hh)
                z, ls = _sb_tile_logs(q_ref[:, sl], k_ref[pl.ds(off, TKS), sl], vis)
                w = jnp.exp(z + ls + _sb_later(ls, after) + r_sc[hh])
                if masked:
                    w = jnp.where(vis, w, 0.0)
                acc_sc[hh] += _bdot(vt_ref[j, sl, :], w.astype(BF16))
                r_sc[hh] += jnp.sum(ls, axis=0, keepdims=True)

        for d in range(per):
            tile(per * i + (per - 1 - d), True)

        def loop(jj, carry):
            tile(per * i - 1 - jj, False)
            return carry

        lax.fori_loop(0, per * i, loop, 0)
        for hh in range(hb):
            o_ref[:, _hs(hh)] = acc_sc[hh].T
            tot_ref[hh, 0] = r_sc[hh]

    return pl.pallas_call(
        body, name=name, grid=(HEADS // hb, nq),
        in_specs=[_q_spec(hb), _kv_spec(t, 1, hb), _kvt_spec(t // TKS, TKS, 1, hb)],
        out_specs=[_q_spec(hb), _qrow_spec(hb)],
        out_shape=[jax.ShapeDtypeStruct((t, HP), F32), jax.ShapeDtypeStruct((HEADS, nq, 1, BQ), F32)],
        scratch_shapes=[pltpu.VMEM((hb, LANES, BQ), F32), pltpu.VMEM((hb, 1, BQ), F32)],
        compiler_params=_cp("parallel", "arbitrary"))(qkv, qkv, kvt)


def _sb_bwd(qkv, kvt, do, tot, name):
    t = qkv.shape[0]
    nq = t // BQ
    per = BQ // TKS
    hb = HB_BWD

    def body(q_ref, k_ref, v_ref, kt_ref, do_ref, tot_ref, dq_ref, dk_ref, dv_ref, dq_sc, p_sc, g_sc):
        i = pl.program_id(1)

        @pl.when(i == 0)
        def _():
            dk_ref[...] = jnp.zeros_like(dk_ref)
            dv_ref[...] = jnp.zeros_like(dv_ref)

        dq_sc[...] = jnp.zeros((hb, LANES, BQ), F32)
        p_sc[...] = jnp.zeros((hb, 1, BQ), F32)
        g_sc[...] = jnp.zeros((hb, 1, BQ), F32)
        mr, mc = _tile_iota(TKS, TKS)
        after = jnp.where(mc > mr, 1.0, 0.0).astype(BF16)
        before = jnp.where(mc < mr, 1.0, 0.0).astype(BF16)

        def tile(j, masked):
            off = pl.multiple_of(j * TKS, TKS)
            vis = _sb_vis(i, j) if masked else None
            for hh in range(hb):
                sl = _hs(hh)
                q = q_ref[:, sl]
                dob = do_ref[:, sl].astype(BF16)
                z, ls = _sb_tile_logs(q, k_ref[pl.ds(off, TKS), sl], vis)
                own = jnp.sum(ls, axis=0, keepdims=True)
                rest = tot_ref[hh, 0] - p_sc[hh] - own
                w = jnp.exp(z + ls + _sb_later(ls, after) + rest)
                if masked:
                    w = jnp.where(vis, w, 0.0)
                g = _bdot(v_ref[pl.ds(off, TKS), sl], dob, NT) * w
                ghi, glo = _split2(g)
                gin = _bdot(before, ghi) + _bdot(before, glo)
                stay = jnp.exp(ls)
                dz = g * stay - (1.0 - stay) * (gin + g_sc[hh])
                if masked:
                    dz = jnp.where(vis, dz, 0.0)
                dzb = dz.astype(BF16)
                dv_ref[pl.ds(off, TKS), sl] += _bdot(w.astype(BF16), dob)
                dk_ref[pl.ds(off, TKS), sl] += _bdot(dzb, q)
                dq_sc[hh] += _bdot(kt_ref[j, sl, :], dzb)
                g_sc[hh] += jnp.sum(g, axis=0, keepdims=True)
                p_sc[hh] += own

        def loop(j, carry):
            tile(j, False)
            return carry

        lax.fori_loop(0, per * i, loop, 0)
        for d in range(per):
            tile(per * i + d, True)
        for hh in range(hb):
            dq_ref[:, _hs(hh)] = dq_sc[hh].T

    return pl.pallas_call(
        body, name=name, grid=(HEADS // hb, nq),
        in_specs=[_q_spec(hb), _kv_spec(t, 1, hb), _kv_spec(t, 2, hb), _kvt_spec(t // TKS, TKS, 0, hb), _q_spec(hb),
                  _qrow_spec(hb)],
        out_specs=[_q_spec(hb), _acc_spec(t, hb), _acc_spec(t, hb)],
        out_shape=[jax.ShapeDtypeStruct((t, HP), F32)] * 3,
        scratch_shapes=[pltpu.VMEM((hb, LANES, BQ), F32), pltpu.VMEM((hb, 1, BQ), F32), pltpu.VMEM((hb, 1, BQ), F32)],
        compiler_params=_cp("parallel", "arbitrary"))(qkv, qkv, qkv, kvt, do, tot)


def _sigmoid(v):
    return 1.0 / (1.0 + jnp.exp(-v))


def _post_fwd(oa, ob, oc, od, proj, g_pad, name):
    t = oa.shape[0]

    def body(oa_ref, ob_ref, oc_ref, od_ref, rg_ref, g_ref, mx_ref):
        g = g_ref[...]

        def group(o, gg):
            r = lax.rsqrt(jnp.sum(o * o, axis=-1, keepdims=True) * (1.0 / GROUP) + EPS)
            return (o * r * gg).astype(BF16)

        mx_ref[:, 0:HP] = group(oa_ref[...], g[:, 0:HP])
        mx_ref[:, HP:2 * HP] = group(ob_ref[...], g[:, HP:2 * HP])
        mx_ref[:, 3 * HP:4 * HP] = group(od_ref[...], g[:, 3 * HP:4 * HP])
        real = lax.broadcasted_iota(jnp.int32, (TM, LANES), 1) < HEAD_DIM
        for hb in range(HEADS):
            sl = slice(hb * LANES, (hb + 1) * LANES)
            o = oc_ref[:, sl]
            mu = jnp.sum(o, axis=-1, keepdims=True) * (1.0 / HEAD_DIM)
            dlt = jnp.where(real, o - mu, 0.0)
            var = jnp.sum(dlt * dlt, axis=-1, keepdims=True) * (1.0 / HEAD_DIM)
            yn = dlt * lax.rsqrt(var + EPS) * g[:, 2 * HP + hb * LANES:2 * HP + (hb + 1) * LANES]
            x = rg_ref[:, sl]
            mx_ref[:, 2 * HP + hb * LANES:2 * HP + (hb + 1) * LANES] = (yn * (x * _sigmoid(x))).astype(BF16)

    rg_spec = pl.BlockSpec((TM, HP), lambda i: (i, OFF_RG // HP))
    return pl.pallas_call(
        body, name=name, grid=(t // TM,),
        in_specs=[_row_spec(HP)] * 4 + [rg_spec, _vec_spec(4 * HP)], out_specs=_row_spec(4 * HP),
        out_shape=jax.ShapeDtypeStruct((t, 4 * HP), BF16), compiler_params=_cp("parallel"))(oa, ob, oc, od, proj, g_pad)


def _post_bwd(dmx, oa, ob, oc, od, proj, g_pad, name):
    t = oa.shape[0]

    def body(dm_ref, oa_ref, ob_ref, oc_ref, od_ref, rg_ref, g_ref,
             doa_ref, dob_ref, doc_ref, dod_ref, dla_ref, dlb_ref, drg_ref, dg_ref):
        @pl.when(pl.program_id(0) == 0)
        def _():
            dg_ref[...] = jnp.zeros_like(dg_ref)

        g = g_ref[...]

        def group_bwd(dm, o, gg):
            r = lax.rsqrt(jnp.sum(o * o, axis=-1, keepdims=True) * (1.0 / GROUP) + EPS)
            oh = o * r
            dgp = jnp.sum(dm * oh, axis=0, keepdims=True)
            dyh = dm * gg
            do = r * (dyh - oh * (jnp.sum(dyh * oh, axis=-1, keepdims=True) * (1.0 / GROUP)))
            return do, dgp

        def delta_bc(do, o):
            prod = do * o
            lane = lax.broadcasted_iota(jnp.int32, (TM, LANES), 1)
            out = jnp.zeros((TM, LANES), F32)
            for hb in range(HEADS):
                out = jnp.where(lane == hb, jnp.sum(prod[:, hb * LANES:(hb + 1) * LANES], axis=-1, keepdims=True), out)
            return out

        oa = oa_ref[...]
        do_a, dga = group_bwd(dm_ref[:, 0:HP], oa, g[:, 0:HP])
        doa_ref[...] = do_a
        dla_ref[...] = delta_bc(do_a, oa)
        dg_ref[:, 0:HP] += dga
        ob = ob_ref[...]
        do_b, dgb = group_bwd(dm_ref[:, HP:2 * HP], ob, g[:, HP:2 * HP])
        dob_ref[...] = do_b
        dlb_ref[...] = delta_bc(do_b, ob)
        dg_ref[:, HP:2 * HP] += dgb
        do_d, dgd = group_bwd(dm_ref[:, 3 * HP:4 * HP], od_ref[...], g[:, 3 * HP:4 * HP])
        dod_ref[...] = do_d
        dg_ref[:, 3 * HP:4 * HP] += dgd
        real = lax.broadcasted_iota(jnp.int32, (TM, LANES), 1) < HEAD_DIM
        for hb in range(HEADS):
            sl = slice(hb * LANES, (hb + 1) * LANES)
            gsl = slice(2 * HP + hb * LANES, 2 * HP + (hb + 1) * LANES)
            o = oc_ref[:, sl]
            mu = jnp.sum(o, axis=-1, keepdims=True) * (1.0 / HEAD_DIM)
            dlt = jnp.where(real, o - mu, 0.0)
            var = jnp.sum(dlt * dlt, axis=-1, keepdims=True) * (1.0 / HEAD_DIM)
            rstd = lax.rsqrt(var + EPS)
            dhat = dlt * rstd
            gc = g[:, gsl]
            x = rg_ref[:, sl]
            sg = _sigmoid(x)
            dm = dm_ref[:, gsl]
            drg_ref[:, sl] = dm * (dhat * gc) * (sg * (1.0 + x * (1.0 - sg)))
            dyn = dm * (x * sg)
            dg_ref[:, gsl] += jnp.sum(dyn * dhat, axis=0, keepdims=True)
            ddh = dyn * gc
            m1 = jnp.sum(ddh, axis=-1, keepdims=True) * (1.0 / HEAD_DIM)
            m2 = jnp.sum(ddh * dhat, axis=-1, keepdims=True) * (1.0 / HEAD_DIM)
            doc_ref[:, sl] = jnp.where(real, rstd * (ddh - m1 - dhat * m2), 0.0)

    rg_spec = pl.BlockSpec((TM, HP), lambda i: (i, OFF_RG // HP))
    hp = _row_spec(HP)
    return pl.pallas_call(
        body, name=name, grid=(t // TM,),
        in_specs=[_row_spec(4 * HP), hp, hp, hp, hp, rg_spec, _vec_spec(4 * HP)],
        out_specs=[hp] * 4 + [_row_spec(LANES)] * 2 + [hp, _vec_spec(4 * HP)],
        out_shape=[jax.ShapeDtypeStruct((t, HP), F32)] * 4 + [jax.ShapeDtypeStruct((t, LANES), F32)] * 2
        + [jax.ShapeDtypeStruct((t, HP), F32), jax.ShapeDtypeStruct((1, 4 * HP), F32)],
        compiler_params=_cp("arbitrary"))(dmx, oa, ob, oc, od, proj, g_pad)


def _mesh_pos():
    return lax.axis_index("x"), lax.axis_index("y"), lax.axis_index("c")


def _peer(pos, k):
    x, y, c = pos
    px = 1 - x if (k >> 2) & 1 else x
    py = 1 - y if (k >> 1) & 1 else y
    pc = 1 - c if k & 1 else c
    return (px, py, pc), 4 * px + 2 * py + pc


def _exchange(arrs, gather, name):
    n = len(arrs)

    def body(*refs):
        ins, outs = refs[:n], refs[n:2 * n]
        send_sems, recv_sems, loc_sems = refs[2 * n:]
        pos = _mesh_pos()
        me = 4 * pos[0] + 2 * pos[1] + pos[2]
        local = []
        for a in range(n):
            src = ins[a] if gather else ins[a].at[me]
            cp = pltpu.make_async_copy(src, outs[a].at[me], loc_sems.at[a])
            cp.start()
            local.append(cp)
        sends, recvs = [], []
        for k in range(1, N_DEV):
            peer, pid = _peer(pos, k)
            for a in range(n):
                s = a * (N_DEV - 1) + k - 1
                src = ins[a] if gather else ins[a].at[pid]
                cp = pltpu.make_async_remote_copy(
                    src_ref=src, dst_ref=outs[a].at[me], send_sem=send_sems.at[s], recv_sem=recv_sems.at[s],
                    device_id=peer, device_id_type=pl.DeviceIdType.MESH)
                cp.start()
                sends.append(cp)
                recvs.append(pltpu.make_async_remote_copy(
                    src_ref=src, dst_ref=outs[a].at[pid], send_sem=send_sems.at[s], recv_sem=recv_sems.at[s],
                    device_id=peer, device_id_type=pl.DeviceIdType.MESH))
        for cp in recvs:
            cp.wait_recv()
        for cp in sends:
            cp.wait_send()
        for cp in local:
            cp.wait()

    any_spec = pl.BlockSpec(memory_space=pl.ANY)
    out_shape = [jax.ShapeDtypeStruct((N_DEV,) + tuple(a.shape) if gather else tuple(a.shape), a.dtype) for a in arrs]
    return pl.pallas_call(
        body, name=name, in_specs=[any_spec] * n, out_specs=[any_spec] * n, out_shape=out_shape,
        scratch_shapes=[pltpu.SemaphoreType.DMA((n * (N_DEV - 1),)), pltpu.SemaphoreType.DMA((n * (N_DEV - 1),)),
                        pltpu.SemaphoreType.DMA((n,))],
        compiler_params=pltpu.CompilerParams(has_side_effects=True))(*arrs)


def _adam_vals(w, g, m, v):
    m = ADAM_B1 * m + (1.0 - ADAM_B1) * g
    v = ADAM_B2 * v + (1.0 - ADAM_B2) * (g * g)
    m_hat = m / ADAM_C1
    v_hat = v / ADAM_C2
    delta = -ADAM_LR * (m_hat / (jnp.sqrt(v_hat) + ADAM_EPS) + ADAM_WD * w)
    return delta, m, v


def _small_allreduce_adam(part, w, m, v, name):
    rows = part.shape[0]

    def body(p_ref, w_ref, m_ref, v_ref, g_ref, d_ref, nm_ref, nv_ref, gath, send_sems, recv_sems):
        pos = _mesh_pos()
        me = 4 * pos[0] + 2 * pos[1] + pos[2]
        gath[me] = p_ref[...]
        sends, recvs = [], []
        for k in range(1, N_DEV):
            peer, pid = _peer(pos, k)
            cp = pltpu.make_async_remote_copy(
                src_ref=p_ref, dst_ref=gath.at[me], send_sem=send_sems.at[k - 1], recv_sem=recv_sems.at[k - 1],
                device_id=peer, device_id_type=pl.DeviceIdType.MESH)
            cp.start()
            sends.append(cp)
            recvs.append(pltpu.make_async_remote_copy(
                src_ref=p_ref, dst_ref=gath.at[pid], send_sem=send_sems.at[k - 1], recv_sem=recv_sems.at[k - 1],
                device_id=peer, device_id_type=pl.DeviceIdType.MESH))
        for cp in recvs:
            cp.wait_recv()
        for cp in sends:
            cp.wait_send()
        g = gath[0]
        for p in range(1, N_DEV):
            g = g + gath[p]
        g_ref[...] = g
        d, nm, nv = _adam_vals(w_ref[...], g, m_ref[...], v_ref[...])
        d_ref[...] = d
        nm_ref[...] = nm
        nv_ref[...] = nv

    vm = pl.BlockSpec(memory_space=pltpu.VMEM)
    sds = jax.ShapeDtypeStruct((rows, LANES), F32)
    return pl.pallas_call(
        body, name=name, in_specs=[vm] * 4, out_specs=[vm] * 4, out_shape=[sds] * 4,
        scratch_shapes=[pltpu.VMEM((N_DEV, rows, LANES), F32), pltpu.SemaphoreType.DMA((N_DEV - 1,)),
                        pltpu.SemaphoreType.DMA((N_DEV - 1,))],
        compiler_params=pltpu.CompilerParams(has_side_effects=True))(part, w, m, v)


def _reduce_adam(recv, w, m, v, name):
    shape = w.shape
    c = shape[-1]
    r = int(np.prod(shape[:-1]))
    recv2, w2, m2, v2 = recv.reshape(N_DEV, r, c), w.reshape(r, c), m.reshape(r, c), v.reshape(r, c)
    tr = r
    while tr * c * 4 > (1 << 20) and tr % 16 == 0:
        tr //= 2

    def body(r_ref, w_ref, m_ref, v_ref, g_ref, d_ref, nm_ref, nv_ref):
        g = r_ref[0].astype(F32)
        for p in range(1, N_DEV):
            g = g + r_ref[p].astype(F32)
        g_ref[...] = g
        d, nm, nv = _adam_vals(w_ref[...], g, m_ref[...], v_ref[...])
        d_ref[...] = d
        nm_ref[...] = nm
        nv_ref[...] = nv

    spec = pl.BlockSpec((tr, c), lambda i: (i, 0))
    sds = jax.ShapeDtypeStruct((r, c), F32)
    outs = pl.pallas_call(
        body, name=name, grid=(r // tr,),
        in_specs=[pl.BlockSpec((N_DEV, tr, c), lambda i: (0, i, 0)), spec, spec, spec],
        out_specs=[spec] * 4, out_shape=[sds] * 4, compiler_params=_cp("parallel"))(recv2, w2, m2, v2)
    return [o.reshape(shape) for o in outs]


def _pad_heads(w, real=HEAD_DIM):
    lead = w.shape[:-1]
    w = w.reshape(lead + (HEADS, real))
    w = jnp.pad(w, [(0, 0)] * len(lead) + [(0, 0), (0, LANES - real)])
    return w.reshape(lead + (HP,))


def _unpad_heads(w, real=HEAD_DIM):
    lead = w.shape[:-1]
    return w.reshape(lead + (HEADS, LANES))[..., :real].reshape(lead + (HEADS * real,))


_IN_SEGS = (("fq", 0, 256), ("fk", 256, 512), ("fv", 512, 768), ("ff", 768, 772), ("cq", 772, 1028),
            ("ckv", 1028, 1156), ("kr", 1156, 1188), ("rq", 1188, 1444), ("rk", 1444, 1700), ("rv", 1700, 1956),
            ("rg", 1956, 2212), ("sq", 2212, 2468), ("sk", 2468, 2724), ("sv", 2724, 2980))


def _pad_w_in(w):
    s = {n: w[:, a:b] for n, a, b in _IN_SEGS}
    rows = w.shape[0]
    z = lambda n: jnp.zeros((rows, n), w.dtype)
    parts = [_pad_heads(s[n]) for n in ("fq", "fk", "fv", "rq", "rk", "rv", "sq", "sk", "sv", "rg")]
    parts += [s["cq"], s["ckv"], z(HEAD_DIM), s["kr"], z(LANES - HEAD_DIM - ROPE_MLA), s["ff"], z(LANES - HEADS),
              z(NP_IN - OFF_FF - LANES)]
    return jnp.concatenate(parts, axis=1)


def _unpad_w_in(wp):
    def heads(off):
        return _unpad_heads(wp[:, off:off + HP])

    parts = [heads(OFF_FOX), heads(OFF_FOX + HP), heads(OFF_FOX + 2 * HP), wp[:, OFF_FF:OFF_FF + HEADS],
             wp[:, OFF_CQ:OFF_CQ + Q_RANK], wp[:, OFF_CKV:OFF_CKV + KV_RANK],
             wp[:, OFF_KR + HEAD_DIM:OFF_KR + HEAD_DIM + ROPE_MLA],
             heads(OFF_RET), heads(OFF_RET + HP), heads(OFF_RET + 2 * HP), heads(OFF_RG),
             heads(OFF_SB), heads(OFF_SB + HP), heads(OFF_SB + 2 * HP)]
    return jnp.concatenate(parts, axis=1)


def _pad_w_kv(w):
    w4 = w.reshape(KV_RANK, HEADS, 2 * HEAD_DIM)
    k = w4[:, :, :HEAD_DIM].reshape(KV_RANK, GROUP)
    v = w4[:, :, HEAD_DIM:].reshape(KV_RANK, GROUP)
    return jnp.concatenate([_pad_heads(k), _pad_heads(v)], axis=1)


def _unpad_w_kv(wp):
    k = _unpad_heads(wp[:, :HP]).reshape(KV_RANK, HEADS, HEAD_DIM)
    v = _unpad_heads(wp[:, HP:]).reshape(KV_RANK, HEADS, HEAD_DIM)
    return jnp.concatenate([k, v], axis=-1).reshape(KV_RANK, HEADS * 2 * HEAD_DIM)


def _pad_rows_out(w):
    w = w.reshape(4 * HEADS, HEAD_DIM, D_MODEL)
    w = jnp.pad(w, ((0, 0), (0, LANES - HEAD_DIM), (0, 0)))
    return w.reshape(4 * HP, D_MODEL)


def _unpad_rows_out(wp):
    return wp.reshape(4 * HEADS, LANES, D_MODEL)[:, :HEAD_DIM, :].reshape(D_MODEL, D_MODEL)


def _pad_gain_out(g):
    g = jnp.pad(g.reshape(4 * HEADS, HEAD_DIM), ((0, 0), (0, LANES - HEAD_DIM)))
    return g.reshape(1, 4 * HP)


def _unpad_gain_out(gp):
    return gp.reshape(4 * HEADS, LANES)[:, :HEAD_DIM].reshape(D_MODEL)


_SMALL = (("g_mix_pre", 1024), ("g_mix_post", 1024), ("g_ffn_pre", 1024), ("g_ffn_post", 1024), ("g_mix_out", 1024),
          ("g_q_lora", 256), ("g_kv_lora", 128), ("b_forget", 4))


def _pack_small(vals):
    parts = []
    for name, n in _SMALL:
        a = vals[name].astype(F32)
        if n < LANES:
            a = jnp.pad(a, ((0, 0), (0, LANES - n)))
        parts.append(a)
    return jnp.concatenate(parts, axis=1).reshape(DEPTH * SMALL_ROWS, LANES)


def _unpack_small(packed):
    flat = packed.reshape(DEPTH, SMALL_ROWS * LANES)
    out, off = {}, 0
    for name, n in _SMALL:
        out[name] = flat[:, off:off + n]
        off += max(n, LANES)
    return out


def kernel(x, positions, g_mix_pre, w_in, b_forget, g_q_lora, w_q_up, g_kv_lora, w_kv_up, g_mix_out, w_out, g_mix_post, g_ffn_pre, w_ffn_up, w_ffn_down, g_ffn_post, loss_target, m_g_mix_pre, m_w_in, m_b_forget, m_g_q_lora, m_w_q_up, m_g_kv_lora, m_w_kv_up, m_g_mix_out, m_w_out, m_g_mix_post, m_g_ffn_pre, m_w_ffn_up, m_w_ffn_down, m_g_ffn_post, v_g_mix_pre, v_w_in, v_b_forget, v_g_q_lora, v_w_q_up, v_g_kv_lora, v_w_kv_up, v_g_mix_out, v_w_out, v_g_mix_post, v_g_ffn_pre, v_w_ffn_up, v_w_ffn_down, v_g_ffn_post):
    t = x.shape[1]
    nq = t // BQ
    x0 = x[0]
    tgt = loss_target[0]
    pos = positions[0].astype(F32).reshape(t, 1)

    half_r, half_m = HEAD_DIM // 2, ROPE_MLA // 2
    invf_r = ROPE_BASE ** (-jnp.arange(half_r, dtype=F32) / half_r)
    invf_m = ROPE_BASE ** (-jnp.arange(half_m, dtype=F32) / half_m)
    blk_r = jnp.concatenate([invf_r, invf_r, jnp.zeros((LANES - HEAD_DIM,), F32)])
    blk_m = jnp.concatenate([jnp.zeros((HEAD_DIM,), F32), invf_m, invf_m, jnp.zeros((LANES - HEAD_DIM - ROPE_MLA,), F32)])
    invf_ret = jnp.tile(blk_r, HEADS).reshape(1, HP)
    invf_mla = jnp.tile(blk_m, HEADS).reshape(1, HP)
    log_gamma = jnp.log1p(-jnp.power(2.0, -5.0 - jnp.arange(HEADS, dtype=F32)))
    lg_lanes = jnp.repeat(log_gamma, LANES).reshape(1, HP)
    lg_heads = jnp.broadcast_to(log_gamma[:, None, None], (HEADS, 1, LANES))

    big = [w_in, w_q_up, w_kv_up, w_out, w_ffn_up, w_ffn_down]
    gat = _exchange([w.astype(BF16) for w in big], True, "weights_all_gather")
    win_g = gat[0].transpose(1, 0, 2, 3).reshape(DEPTH, D_MODEL, D_IN)
    wq_g = gat[1].transpose(1, 2, 0, 3).reshape(DEPTH, Q_RANK, 384)
    wkv_g = gat[2].transpose(1, 2, 0, 3).reshape(DEPTH, KV_RANK, 512)
    wout_g = gat[3].transpose(1, 0, 2, 3).reshape(DEPTH, D_MODEL, D_MODEL)
    wup_g = gat[4].transpose(1, 2, 0, 3).reshape(DEPTH, D_MODEL, D_FF)
    wdn_g = gat[5].transpose(1, 0, 2, 3).reshape(DEPTH, D_FF, D_MODEL)

    row = lambda g: g.reshape(1, -1)
    layers = []
    for l in range(DEPTH):
        layers.append(dict(
            win=_pad_w_in(win_g[l]), wq=_pad_heads(wq_g[l], 96), wkv=_pad_w_kv(wkv_g[l]),
            wout=_pad_rows_out(wout_g[l]), wup=wup_g[l], wdn=wdn_g[l],
            g_pre=row(g_mix_pre[l]), g_post=row(g_mix_post[l]), g_fpre=row(g_ffn_pre[l]), g_fpost=row(g_ffn_post[l]),
            g_out=_pad_gain_out(g_mix_out[l]), g_q=row(g_q_lora[l]), g_kv=row(g_kv_lora[l]),
            b_pad=jnp.pad(b_forget[l], (0, LANES - HEADS)).reshape(1, LANES)))

    saved = []
    xin = x0
    h = _rms_fwd(xin, layers[0]["g_pre"], "rms_pre_0")
    loss_row = dx = None
    for l, p in enumerate(layers):
        s = dict(x=xin, h=h)
        proj = _mm(h, p["win"], name=f"in_proj_{l}")
        cum = _cumsum(_forget_lsf(proj, p["b_pad"], f"forget_lsf_{l}"), False, f"forget_cumsum_{l}")
        fox, ret, sb, mla, fox_t, sb_t, mla_t = _prep_fwd(proj, cum, pos, invf_ret, invf_mla, lg_lanes, p["g_q"],
                                                          p["g_kv"], p["wq"], p["wkv"], f"prep_fwd_{l}")
        oa, lse_a = _softmax_fwd(fox, fox_t, chunk_mask=False, scale=1.0, name=f"fox_fwd_{l}")
        ob, lse_b = _softmax_fwd(mla, mla_t, chunk_mask=True, scale=96.0 ** -0.5, name=f"mla_fwd_{l}")
        oc = _ret_fwd(ret, lg_heads, f"ret_fwd_{l}")
        od, sb_tot = _sb_fwd(sb, sb_t, f"sb_fwd_{l}")
        mixed = _post_fwd(oa, ob, oc, od, proj, p["g_out"], f"post_fwd_{l}")
        mix = _mm(mixed, p["wout"], name=f"out_proj_{l}", tk=2048)
        x1, h2 = _add_rms_fwd(xin, mix, p["g_post"], p["g_fpre"], f"mix_residual_{l}")
        a = _mm(h2, p["wup"], name=f"ffn_up_{l}")
        y = _mm(a, p["wdn"], name=f"ffn_down_{l}", a_fn=_relu2)
        s.update(proj=proj, fox=fox, ret=ret, sb=sb, mla=mla, fox_t=fox_t, sb_t=sb_t, mla_t=mla_t, oa=oa, ob=ob, oc=oc,
                 od=od, sb_tot=sb_tot, lse_a=lse_a, lse_b=lse_b, mixed=mixed, mix=mix, x1=x1, h2=h2, a=a, y=y)
        saved.append(s)
        if l + 1 < DEPTH:
            xin, h = _add_rms_fwd(x1, y, p["g_fpost"], layers[l + 1]["g_pre"], f"ffn_residual_{l}")
        else:
            loss_row, dx = _final_loss(x1, y, p["g_fpost"], tgt, "loss")
    loss = lax.psum(loss_row[0, 0], ("x", "y", "c"))

    small_g = {n: [None] * DEPTH for n, _ in _SMALL}
    big_g = [[None] * DEPTH for _ in range(6)]
    for l in reversed(range(DEPTH)):
        p, s = layers[l], saved[l]
        dy, dg = _norm_bwd(dx, s["y"], p["g_fpost"], None, BF16, f"ffn_post_bwd_{l}")
        small_g["g_ffn_post"][l] = dg
        da = _mm(dy, p["wdn"], name=f"ffn_down_dx_{l}", tb=True, out_dtype=BF16, epi=_drelu2, epi_in=s["a"])
        big_g[5][l] = _mm(s["a"], dy, name=f"ffn_down_dw_{l}", ta=True, a_fn=_relu2, tm=1024, tk=512)
        big_g[4][l] = _mm(s["h2"], da, name=f"ffn_up_dw_{l}", ta=True, tm=1024, tk=512)
        dh2 = _mm(da, p["wup"], name=f"ffn_up_dx_{l}", tb=True)
        dx1, dg = _norm_bwd(dh2, s["x1"], p["g_fpre"], dx, F32, f"ffn_pre_bwd_{l}")
        small_g["g_ffn_pre"][l] = dg
        dmix, dg = _norm_bwd(dx1, s["mix"], p["g_post"], None, BF16, f"mix_post_bwd_{l}")
        small_g["g_mix_post"][l] = dg
        dmixed = _mm(dmix, p["wout"], name=f"out_proj_dx_{l}", tb=True)
        big_g[3][l] = _unpad_rows_out(_mm(s["mixed"], dmix, name=f"out_proj_dw_{l}", ta=True, tm=1024, tk=512))
        doa, dob, doc, dod, dla, dlb, drg, dgo = _post_bwd(dmixed, s["oa"], s["ob"], s["oc"], s["od"], s["proj"],
                                                           p["g_out"], f"post_bwd_{l}")
        small_g["g_mix_out"][l] = _unpad_gain_out(dgo).reshape(1, D_MODEL)
        as_rows = lambda a: a[:, :HEADS].T.reshape(HEADS, nq, 1, BQ)
        dfq, dfk, dfv, dcum_k, dcum_q = _softmax_bwd(s["fox"], s["fox_t"], doa, s["lse_a"], as_rows(dla), bias=True,
                                                     chunk_mask=False, scale=1.0, name=f"fox_bwd_{l}")
        dmq, dmk, dmv = _softmax_bwd(s["mla"], s["mla_t"], dob, s["lse_b"], as_rows(dlb), bias=False, chunk_mask=True,
                                     scale=96.0 ** -0.5, name=f"mla_bwd_{l}")
        drq, drk, drv = _ret_bwd(s["ret"], lg_heads, doc, f"ret_bwd_{l}")
        dsq, dsk, dsv = _sb_bwd(s["sb"], s["sb_t"], dod, s["sb_tot"], f"sb_bwd_{l}")
        dcum_q = jnp.pad(dcum_q.reshape(HEADS, t).T, ((0, 0), (0, LANES - HEADS)))
        dlsf = _cumsum(dcum_q, True, f"forget_cumsum_bwd_{l}", partials=dcum_k)
        dproj, dwq, dwkv, dgq, dgkv, dbf = _prep_bwd(
            (dfq, dfk, dfv), (drq, drk, drv), (dsq, dsk, dsv), (dmq, dmk, dmv), drg, dlsf, s["proj"], pos, invf_ret,
            invf_mla, lg_lanes, p["b_pad"], p["g_q"], p["g_kv"], p["wq"], p["wkv"], f"prep_bwd_{l}")
        small_g["g_q_lora"][l] = dgq
        small_g["g_kv_lora"][l] = dgkv
        small_g["b_forget"][l] = dbf[:, :HEADS]
        big_g[1][l] = _unpad_heads(dwq, 96)
        big_g[2][l] = _unpad_w_kv(dwkv)
        big_g[0][l] = _unpad_w_in(_mm(s["h"], dproj, name=f"in_proj_dw_{l}", ta=True, tm=1024, tk=512))
        dh = _mm(dproj, p["win"], name=f"in_proj_dx_{l}", tb=True)
        dx, dg = _norm_bwd(dh, s["x"], p["g_pre"], dx1, F32, f"mix_pre_bwd_{l}")
        small_g["g_mix_pre"][l] = dg
    grad_x = dx.reshape(1, t, D_MODEL)

    st = lambda k: jnp.stack(big_g[k])
    send = [
        st(0).reshape(DEPTH, N_DEV, D_MODEL // N_DEV, D_IN).transpose(1, 0, 2, 3),
        st(1).reshape(DEPTH, Q_RANK, N_DEV, 384 // N_DEV).transpose(2, 0, 1, 3),
        st(2).reshape(DEPTH, KV_RANK, N_DEV, 512 // N_DEV).transpose(2, 0, 1, 3),
        st(3).reshape(DEPTH, N_DEV, D_MODEL // N_DEV, D_MODEL).transpose(1, 0, 2, 3),
        st(4).reshape(DEPTH, D_MODEL, N_DEV, D_FF // N_DEV).transpose(2, 0, 1, 3),
        st(5).reshape(DEPTH, N_DEV, D_FF // N_DEV, D_MODEL).transpose(1, 0, 2, 3),
    ]
    recv = _exchange([a.astype(BF16) for a in send], False, "grads_all_to_all")
    ms = [m_w_in, m_w_q_up, m_w_kv_up, m_w_out, m_w_ffn_up, m_w_ffn_down]
    vs = [v_w_in, v_w_q_up, v_w_kv_up, v_w_out, v_w_ffn_up, v_w_ffn_down]
    names = ["w_in", "w_q_up", "w_kv_up", "w_out", "w_ffn_up", "w_ffn_down"]
    res = {}
    for k, n in enumerate(names):
        res[n] = _reduce_adam(recv[k], big[k], ms[k], vs[k], f"adamw_{n}")

    small_w = dict(g_mix_pre=g_mix_pre, g_mix_post=g_mix_post, g_ffn_pre=g_ffn_pre, g_ffn_post=g_ffn_post,
                   g_mix_out=g_mix_out, g_q_lora=g_q_lora, g_kv_lora=g_kv_lora, b_forget=b_forget)
    small_m = dict(g_mix_pre=m_g_mix_pre, g_mix_post=m_g_mix_post, g_ffn_pre=m_g_ffn_pre, g_ffn_post=m_g_ffn_post,
                   g_mix_out=m_g_mix_out, g_q_lora=m_g_q_lora, g_kv_lora=m_g_kv_lora, b_forget=m_b_forget)
    small_v = dict(g_mix_pre=v_g_mix_pre, g_mix_post=v_g_mix_post, g_ffn_pre=v_g_ffn_pre, g_ffn_post=v_g_ffn_post,
                   g_mix_out=v_g_mix_out, g_q_lora=v_g_q_lora, g_kv_lora=v_g_kv_lora, b_forget=v_b_forget)
    part = _pack_small({n: jnp.concatenate(small_g[n], axis=0) for n, _ in _SMALL})
    sres = _small_allreduce_adam(part, _pack_small(small_w), _pack_small(small_m), _pack_small(small_v),
                                 "small_allreduce_adamw")
    sg, sd, sm, sv = [_unpack_small(a) for a in sres]
    for n, _ in _SMALL:
        res[n] = [sg[n], sd[n], sm[n], sv[n]]

    order = ["g_mix_pre", "w_in", "b_forget", "g_q_lora", "w_q_up", "g_kv_lora", "w_kv_up", "g_mix_out", "w_out",
             "g_mix_post", "g_ffn_pre", "w_ffn_up", "w_ffn_down", "g_ffn_post"]
    outs = [loss, grad_x]
    for idx in range(4):
        outs += [res[n][idx] for n in order]
    return tuple(outs)
```

```python
import functools
import math

import numpy as np
import jax
import jax.numpy as jnp
from jax import lax
from jax.experimental import pallas as pl
from jax.experimental.pallas import tpu as pltpu

F32 = jnp.float32
BF16 = jnp.bfloat16

D_MODEL = 1024
DEPTH = 2
N_DEV = 8
GROUP = 256
HEADS = 4
HEAD_DIM = 64
LANES = 128
HP = HEADS * LANES
QKV = 3 * HP
Q_RANK = 256
KV_RANK = 128
ROPE_MLA = 32
D_FF = 4096
D_IN = 2980
CHUNK_SHIFT = 6
EPS = 1e-6
ROPE_BASE = 10000.0
NEG = -1e30

OFF_FOX, OFF_RET, OFF_SB = 0, QKV, 2 * QKV
OFF_RG = 3 * QKV
OFF_CQ = OFF_RG + HP
OFF_CKV = OFF_CQ + Q_RANK
OFF_KR = OFF_CKV + LANES
OFF_FF = OFF_KR + LANES
NP_IN = 6144

BQ = 256
TKS = 128
TM = 256
VMEM_LIMIT = 48 * 1024 * 1024

ADAM_LR, ADAM_B1, ADAM_B2, ADAM_EPS, ADAM_WD, ADAM_STEP = 0.001, 0.9, 0.999, 1e-08, 0.01, 10
ADAM_C1 = 1.0 - ADAM_B1 ** ADAM_STEP
ADAM_C2 = 1.0 - ADAM_B2 ** ADAM_STEP

SMALL_ROWS = 44

NT = (((1,), (1,)), ((), ()))
TN = (((0,), (0,)), ((), ()))


def _cp(*sem):
    return pltpu.CompilerParams(dimension_semantics=sem if sem else None, vmem_limit_bytes=VMEM_LIMIT)


def _bdot(a, b, dn=None):
    if dn is None:
        return jnp.dot(a, b, preferred_element_type=F32)
    return lax.dot_general(a, b, dn, preferred_element_type=F32)


def _split2(x):
    hi = x.astype(BF16)
    lo = (x - hi.astype(F32)).astype(BF16)
    return hi, lo


def _mm(a, b, *, name, ta=False, tb=False, out_dtype=F32, a_fn=None, epi=None, epi_in=None,
        tm=512, tn=1024, tk=1024):
    m, k = (a.shape[1], a.shape[0]) if ta else a.shape
    n = b.shape[0] if tb else b.shape[1]
    tm, tn, tk = min(tm, m), min(tn, n), min(tk, k)
    assert m % tm == 0 and n % tn == 0 and k % tk == 0, (name, m, n, k)
    nk = k // tk
    dn = (((0 if ta else 1,), (1 if tb else 0,)), ((), ()))

    def body(*refs):
        if epi is None:
            a_ref, b_ref, o_ref = refs[:3]
            e_ref = None
            rest = refs[3:]
        else:
            a_ref, b_ref, e_ref, o_ref = refs[:4]
            rest = refs[4:]
        av = a_ref[...]
        if a_fn is not None:
            av = a_fn(av)
        part = lax.dot_general(av.astype(BF16), b_ref[...].astype(BF16), dn, preferred_element_type=F32)

        def finish(r):
            if epi is not None:
                r = epi(r, e_ref[...])
            o_ref[...] = r.astype(out_dtype)

        if nk == 1:
            finish(part)
        else:
            acc_ref = rest[0]
            kk = pl.program_id(2)

            @pl.when(kk == 0)
            def _():
                acc_ref[...] = part

            @pl.when(kk > 0)
            def _():
                acc_ref[...] += part

            @pl.when(kk == nk - 1)
            def _():
                finish(acc_ref[...])

    a_spec = pl.BlockSpec((tk, tm), lambda i, j, kk: (kk, i)) if ta else pl.BlockSpec((tm, tk), lambda i, j, kk: (i, kk))
    b_spec = pl.BlockSpec((tn, tk), lambda i, j, kk: (j, kk)) if tb else pl.BlockSpec((tk, tn), lambda i, j, kk: (kk, j))
    o_spec = pl.BlockSpec((tm, tn), lambda i, j, kk: (i, j))
    in_specs = [a_spec, b_spec]
    args = [a, b]
    if epi is not None:
        in_specs.append(o_spec)
        args.append(epi_in)
    return pl.pallas_call(
        body, name=name, grid=(m // tm, n // tn, nk),
        in_specs=in_specs, out_specs=o_spec,
        out_shape=jax.ShapeDtypeStruct((m, n), out_dtype),
        scratch_shapes=[pltpu.VMEM((tm, tn), F32)] if nk > 1 else [],
        compiler_params=_cp("parallel", "parallel", "arbitrary"),
    )(*args)


def _relu2(v):
    r = jnp.maximum(v, 0.0)
    return r * r


def _drelu2(du, av):
    return du * (2.0 * jnp.maximum(av, 0.0))


def _rms(v, g):
    r = lax.rsqrt(jnp.mean(v * v, axis=-1, keepdims=True) + EPS)
    return v * r * g


def _row_spec(w):
    return pl.BlockSpec((TM, w), lambda i: (i, 0))


def _vec_spec(w):
    return pl.BlockSpec((1, w), lambda i: (0, 0))


def _rms_fwd(x, g, name):
    t, d = x.shape

    def body(x_ref, g_ref, h_ref):
        h_ref[...] = _rms(x_ref[...], g_ref[...]).astype(BF16)

    return pl.pallas_call(
        body, name=name, grid=(t // TM,), in_specs=[_row_spec(d), _vec_spec(d)], out_specs=_row_spec(d),
        out_shape=jax.ShapeDtypeStruct((t, d), BF16), compiler_params=_cp("parallel"))(x, g)


def _add_rms_fwd(x, y, g1, g2, name):
    t, d = x.shape

    def body(x_ref, y_ref, g1_ref, g2_ref, xn_ref, h_ref):
        xn = x_ref[...] + _rms(y_ref[...], g1_ref[...])
        xn_ref[...] = xn
        h_ref[...] = _rms(xn, g2_ref[...]).astype(BF16)

    return pl.pallas_call(
        body, name=name, grid=(t // TM,),
        in_specs=[_row_spec(d), _row_spec(d), _vec_spec(d), _vec_spec(d)],
        out_specs=[_row_spec(d), _row_spec(d)],
        out_shape=[jax.ShapeDtypeStruct((t, d), F32), jax.ShapeDtypeStruct((t, d), BF16)],
        compiler_params=_cp("parallel"))(x, y, g1, g2)


def _final_loss(x, y, g, tgt, name):
    t, d = x.shape

    def body(x_ref, y_ref, g_ref, t_ref, l_ref, dx_ref):
        @pl.when(pl.program_id(0) == 0)
        def _():
            l_ref[...] = jnp.zeros_like(l_ref)

        err = x_ref[...] + _rms(y_ref[...], g_ref[...]) - t_ref[...]
        dx_ref[...] = err * (1.0 / d)
        l_ref[...] += jnp.sum(jnp.sum(err * err, axis=1, keepdims=True), axis=0, keepdims=True) * (0.5 / d)

    return pl.pallas_call(
        body, name=name, grid=(t // TM,),
        in_specs=[_row_spec(d), _row_spec(d), _vec_spec(d), _row_spec(d)],
        out_specs=[pl.BlockSpec((1, LANES), lambda i: (0, 0)), _row_spec(d)],
        out_shape=[jax.ShapeDtypeStruct((1, LANES), F32), jax.ShapeDtypeStruct((t, d), F32)],
        compiler_params=_cp("arbitrary"))(x, y, g, tgt)


def _rms_bwd_vals(dn, v, g):
    w = v.shape[-1]
    r = lax.rsqrt(jnp.mean(v * v, axis=-1, keepdims=True) + EPS)
    vh = v * r
    dgp = jnp.sum(dn * vh, axis=0, keepdims=True)
    dvh = dn * g
    dv = r * (dvh - vh * (jnp.sum(dvh * vh, axis=-1, keepdims=True) * (1.0 / w)))
    return dv, dgp


def _norm_bwd(dn, v, g, resid, out_dtype, name):
    t, d = v.shape
    has_res = resid is not None

    def body(*refs):
        if has_res:
            dn_ref, v_ref, g_ref, r_ref, dv_ref, dg_ref = refs
        else:
            dn_ref, v_ref, g_ref, dv_ref, dg_ref = refs

        @pl.when(pl.program_id(0) == 0)
        def _():
            dg_ref[...] = jnp.zeros_like(dg_ref)

        dv, dgp = _rms_bwd_vals(dn_ref[...].astype(F32), v_ref[...], g_ref[...])
        if has_res:
            dv = dv + r_ref[...]
        dv_ref[...] = dv.astype(out_dtype)
        dg_ref[...] += dgp

    in_specs = [_row_spec(d), _row_spec(d), _vec_spec(d)] + ([_row_spec(d)] if has_res else [])
    args = [dn, v, g] + ([resid] if has_res else [])
    return pl.pallas_call(
        body, name=name, grid=(t // TM,), in_specs=in_specs,
        out_specs=[_row_spec(d), _vec_spec(d)],
        out_shape=[jax.ShapeDtypeStruct((t, d), out_dtype), jax.ShapeDtypeStruct((1, d), F32)],
        compiler_params=_cp("arbitrary"))(*args)


def _rope_tables(pos, invf, lo, half):
    w = invf.shape[-1]
    lane = lax.broadcasted_iota(jnp.int32, (pos.shape[0], w), 1) & (LANES - 1)
    first = (lane >= lo) & (lane < lo + half)
    second = (lane >= lo + half) & (lane < lo + 2 * half)
    ang = pos * invf
    active = first | second
    cos = jnp.where(active, jnp.cos(ang), 1.0)
    sin = jnp.where(active, jnp.sin(ang), 0.0)
    return cos, sin, first, second


def _rope_apply(v, cos, sin, first, second, half, sign):
    w = v.shape[-1]
    up = pltpu.roll(v, w - half, 1)
    dn = pltpu.roll(v, half, 1)
    rot = jnp.where(first, -up, jnp.where(second, dn, 0.0))
    return v * cos + rot * (sin * sign)


def _forget_lsf(proj, b_pad, name):
    t = proj.shape[0]

    def body(ff_ref, b_ref, o_ref):
        f = ff_ref[...] + b_ref[...]
        o_ref[...] = -(jnp.maximum(-f, 0.0) + jnp.log(1.0 + jnp.exp(-jnp.abs(f))))

    return pl.pallas_call(
        body, name=name, grid=(t // TM,),
        in_specs=[pl.BlockSpec((TM, LANES), lambda i: (i, OFF_FF // LANES)), _vec_spec(LANES)],
        out_specs=_row_spec(LANES), out_shape=jax.ShapeDtypeStruct((t, LANES), F32),
        compiler_params=_cp("parallel"))(proj, b_pad)


def _split3(c):
    hi = c.astype(BF16).astype(F32)
    mid = (c - hi).astype(BF16).astype(F32)
    return hi, mid, (c - hi) - mid


def _transposed_tiles(dst, row0, blocks, width):
    for b, blk in enumerate(blocks):
        bt = blk.T.astype(BF16)
        rows = bt.shape[0]
        for w in range(TM // width):
            dst[w, row0 + b * rows:row0 + (b + 1) * rows, :] = bt[:, w * width:(w + 1) * width]


def _prep_fwd(proj, cum, pos, invf_ret, invf_mla, lg_lanes, g_q, g_kv, wq_pad, wkv_pad, name):
    t = proj.shape[0]

    def body(fox_ref, ret_ref, sb_ref, cq_ref, ckv_ref, kr_ref, cum_ref, pos_ref, ifr_ref, ifm_ref, lg_ref,
             gq_ref, gkv_ref, wq_ref, wkv_ref,
             ofox_ref, oret_ref, osb_ref, omla_ref, ofoxt_ref, osbt_ref, omlat_ref, orett_ref):
        pos_v = pos_ref[...]
        osb_ref[:, 0:HP] = (sb_ref[:, 0:HP] * 0.125).astype(BF16)
        osb_ref[:, HP:QKV] = sb_ref[:, HP:QKV].astype(BF16)
        _transposed_tiles(osbt_ref, 0, [sb_ref[:, HP:2 * HP], sb_ref[:, 2 * HP:QKV]], TKS)
        lane = lax.broadcasted_iota(jnp.int32, (TM, LANES), 1)
        cumv = cum_ref[...]
        fq, fk = [], []
        for hb in range(HEADS):
            hi, mid, lo = _split3(cumv[:, hb:hb + 1])
            q = fox_ref[:, hb * LANES:(hb + 1) * LANES] * 0.125
            k = fox_ref[:, HP + hb * LANES:HP + (hb + 1) * LANES]
            ones_q = (lane >= HEAD_DIM) & (lane < HEAD_DIM + 3)
            ones_k = (lane >= HEAD_DIM + 3) & (lane < HEAD_DIM + 6)
            q = jnp.where(ones_q, 1.0, q)
            k = jnp.where(ones_k, 1.0, k)
            for n, part in enumerate((hi, mid, lo)):
                q = jnp.where(lane == HEAD_DIM + 3 + n, part, q)
                k = jnp.where(lane == HEAD_DIM + n, -part, k)
            fq.append(q)
            fk.append(k)
        fk = jnp.concatenate(fk, axis=1)
        ofox_ref[:, 0:HP] = jnp.concatenate(fq, axis=1).astype(BF16)
        ofox_ref[:, HP:2 * HP] = fk.astype(BF16)
        ofox_ref[:, 2 * HP:QKV] = fox_ref[:, 2 * HP:QKV].astype(BF16)
        _transposed_tiles(ofoxt_ref, 0, [fk, fox_ref[:, 2 * HP:QKV]], BQ)
        cos, sin, first, second = _rope_tables(pos_v, ifr_ref[...], 0, HEAD_DIM // 2)
        nloc = lax.broadcasted_iota(jnp.int32, (TM, 1), 0).astype(F32)
        dec = lg_ref[...] * nloc
        rq = _rope_apply(ret_ref[:, 0:HP], cos, sin, first, second, HEAD_DIM // 2, 1.0)
        rk = _rope_apply(ret_ref[:, HP:2 * HP], cos, sin, first, second, HEAD_DIM // 2, 1.0)
        oret_ref[:, 0:HP] = (rq * jnp.exp(dec)).astype(BF16)
        rk = rk * 0.125 * jnp.exp(-dec)
        oret_ref[:, HP:2 * HP] = rk.astype(BF16)
        _transposed_tiles(orett_ref, 0, [rk], BQ)
        oret_ref[:, 2 * HP:QKV] = ret_ref[:, 2 * HP:QKV].astype(BF16)
        cosm, sinm, firstm, secondm = _rope_tables(pos_v, ifm_ref[...], HEAD_DIM, ROPE_MLA // 2)
        cqn = _rms(cq_ref[...], gq_ref[...]).astype(BF16)
        qm = _bdot(cqn, wq_ref[...])
        omla_ref[:, 0:HP] = _rope_apply(qm, cosm, sinm, firstm, secondm, ROPE_MLA // 2, 1.0).astype(BF16)
        ckvn = _rms(ckv_ref[...], gkv_ref[...]).astype(BF16)
        kv = _bdot(ckvn, wkv_ref[...])
        krr = _rope_apply(kr_ref[...], cosm[:, 0:LANES], sinm[:, 0:LANES], firstm[:, 0:LANES],
                          secondm[:, 0:LANES], ROPE_MLA // 2, 1.0)
        mk = kv[:, 0:HP] + jnp.concatenate([krr] * HEADS, axis=1)
        omla_ref[:, HP:2 * HP] = mk.astype(BF16)
        omla_ref[:, 2 * HP:QKV] = kv[:, HP:2 * HP].astype(BF16)
        _transposed_tiles(omlat_ref, 0, [mk, kv[:, HP:2 * HP]], BQ)

    def seg(off, w):
        return pl.BlockSpec((TM, w), lambda i, o=off // w: (i, o))

    def full(shape):
        return pl.BlockSpec(shape, lambda i: (0,) * len(shape))

    def tiles(width):
        return pl.BlockSpec((TM // width, 2 * HP, width), lambda i: (i, 0, 0))

    in_specs = [seg(OFF_FOX, QKV), seg(OFF_RET, QKV), seg(OFF_SB, QKV), seg(OFF_CQ, Q_RANK), seg(OFF_CKV, LANES),
                seg(OFF_KR, LANES), _row_spec(LANES), pl.BlockSpec((TM, 1), lambda i: (i, 0)),
                full((1, HP)), full((1, HP)), full((1, HP)), full((1, Q_RANK)), full((1, KV_RANK)),
                full((Q_RANK, HP)), full((KV_RANK, 2 * HP))]
    out_specs = [_row_spec(QKV)] * 4 + [tiles(BQ), tiles(TKS), tiles(BQ),
                                        pl.BlockSpec((1, HP, BQ), lambda i: (i, 0, 0))]
    out_shape = [jax.ShapeDtypeStruct((t, QKV), BF16)] * 4 + [
        jax.ShapeDtypeStruct((t // BQ, 2 * HP, BQ), BF16), jax.ShapeDtypeStruct((t // TKS, 2 * HP, TKS), BF16),
        jax.ShapeDtypeStruct((t // BQ, 2 * HP, BQ), BF16), jax.ShapeDtypeStruct((t // BQ, HP, BQ), BF16)]
    return pl.pallas_call(
        body, name=name, grid=(t // TM,), in_specs=in_specs, out_specs=out_specs, out_shape=out_shape,
        compiler_params=_cp("parallel"))(proj, proj, proj, proj, proj, proj, cum, pos, invf_ret, invf_mla, lg_lanes,
                                         g_q, g_kv, wq_pad, wkv_pad)


def _prep_bwd(dfox, dret, dsb, dmla, drg, dlsf, proj, pos, invf_ret, invf_mla, lg_lanes, b_pad, g_q, g_kv,
              wq_pad, wkv_pad, name):
    t = proj.shape[0]

    def body(dfq, dfk, dfv, drq, drk, drv, dsq, dsk, dsv, dmq, dmk, dmv, drg_ref, dlsf_ref,
             cq_ref, ckv_ref, ff_ref, pos_ref, ifr_ref, ifm_ref, lg_ref, b_ref, gq_ref, gkv_ref, wq_ref, wkv_ref,
             dp_ref, dwq_ref, dwkv_ref, dgq_ref, dgkv_ref, dbf_ref):
        @pl.when(pl.program_id(0) == 0)
        def _():
            dwq_ref[...] = jnp.zeros_like(dwq_ref)
            dwkv_ref[...] = jnp.zeros_like(dwkv_ref)
            dgq_ref[...] = jnp.zeros_like(dgq_ref)
            dgkv_ref[...] = jnp.zeros_like(dgkv_ref)
            dbf_ref[...] = jnp.zeros_like(dbf_ref)

        pos_v = pos_ref[...]
        for off, (dq, dk, dv) in ((OFF_FOX, (dfq, dfk, dfv)), (OFF_SB, (dsq, dsk, dsv))):
            dp_ref[:, off:off + HP] = (dq[...] * 0.125).astype(BF16)
            dp_ref[:, off + HP:off + 2 * HP] = dk[...].astype(BF16)
            dp_ref[:, off + 2 * HP:off + QKV] = dv[...].astype(BF16)
        cos, sin, first, second = _rope_tables(pos_v, ifr_ref[...], 0, HEAD_DIM // 2)
        nloc = lax.broadcasted_iota(jnp.int32, (TM, 1), 0).astype(F32)
        dec = lg_ref[...] * nloc
        dq = _rope_apply(drq[...] * jnp.exp(dec), cos, sin, first, second, HEAD_DIM // 2, -1.0)
        dk = _rope_apply(drk[...] * (0.125 * jnp.exp(-dec)), cos, sin, first, second, HEAD_DIM // 2, -1.0)
        dp_ref[:, OFF_RET:OFF_RET + HP] = dq.astype(BF16)
        dp_ref[:, OFF_RET + HP:OFF_RET + 2 * HP] = dk.astype(BF16)
        dp_ref[:, OFF_RET + 2 * HP:OFF_RET + QKV] = drv[...].astype(BF16)
        dp_ref[:, OFF_RG:OFF_RG + HP] = drg_ref[...].astype(BF16)
        cosm, sinm, firstm, secondm = _rope_tables(pos_v, ifm_ref[...], HEAD_DIM, ROPE_MLA // 2)
        dql = _rope_apply(dmq[...], cosm, sinm, firstm, secondm, ROPE_MLA // 2, -1.0).astype(BF16)
        cq = cq_ref[...]
        cqn = _rms(cq, gq_ref[...]).astype(BF16)
        dwq_ref[...] += _bdot(cqn, dql, TN)
        dcqn = _bdot(dql, wq_ref[...], NT)
        dcq, dgq = _rms_bwd_vals(dcqn, cq, gq_ref[...])
        dgq_ref[...] += dgq
        dp_ref[:, OFF_CQ:OFF_CQ + Q_RANK] = dcq.astype(BF16)
        dkm = dmk[...]
        dkv = jnp.concatenate([dkm, dmv[...]], axis=1).astype(BF16)
        ckv = ckv_ref[...]
        ckvn = _rms(ckv, gkv_ref[...]).astype(BF16)
        dwkv_ref[...] += _bdot(ckvn, dkv, TN)
        dckvn = _bdot(dkv, wkv_ref[...], NT)
        dckv, dgkv = _rms_bwd_vals(dckvn, ckv, gkv_ref[...])
        dgkv_ref[...] += dgkv
        dp_ref[:, OFF_CKV:OFF_CKV + LANES] = dckv.astype(BF16)
        dkr = dkm[:, 0:LANES] + dkm[:, LANES:2 * LANES] + dkm[:, 2 * LANES:3 * LANES] + dkm[:, 3 * LANES:HP]
        act = firstm[:, 0:LANES] | secondm[:, 0:LANES]
        dkr = jnp.where(act, dkr, 0.0)
        dkr = _rope_apply(dkr, cosm[:, 0:LANES], sinm[:, 0:LANES], firstm[:, 0:LANES], secondm[:, 0:LANES],
                          ROPE_MLA // 2, -1.0)
        dp_ref[:, OFF_KR:OFF_KR + LANES] = dkr.astype(BF16)
        f = ff_ref[...] + b_ref[...]
        dff = dlsf_ref[...] / (1.0 + jnp.exp(f))
        dbf_ref[...] += jnp.sum(dff, axis=0, keepdims=True)
        dp_ref[:, OFF_FF:OFF_FF + LANES] = dff.astype(BF16)
        dp_ref[:, OFF_FF + LANES:NP_IN] = jnp.zeros((TM, NP_IN - OFF_FF - LANES), BF16)

    def seg(off, w):
        return pl.BlockSpec((TM, w), lambda i, o=off // w: (i, o))

    def full(shape):
        return pl.BlockSpec(shape, lambda i: (0,) * len(shape))

    hp_spec = _row_spec(HP)
    in_specs = [hp_spec] * 13 + [_row_spec(LANES), seg(OFF_CQ, Q_RANK), seg(OFF_CKV, LANES), seg(OFF_FF, LANES),
                                 pl.BlockSpec((TM, 1), lambda i: (i, 0)),
                                 full((1, HP)), full((1, HP)), full((1, HP)), full((1, LANES)), full((1, Q_RANK)),
                                 full((1, KV_RANK)), full((Q_RANK, HP)), full((KV_RANK, 2 * HP))]
    out_specs = [_row_spec(NP_IN), full((Q_RANK, HP)), full((KV_RANK, 2 * HP)), full((1, Q_RANK)),
                 full((1, KV_RANK)), full((1, LANES))]
    out_shape = [jax.ShapeDtypeStruct((t, NP_IN), BF16), jax.ShapeDtypeStruct((Q_RANK, HP), F32),
                 jax.ShapeDtypeStruct((KV_RANK, 2 * HP), F32), jax.ShapeDtypeStruct((1, Q_RANK), F32),
                 jax.ShapeDtypeStruct((1, KV_RANK), F32), jax.ShapeDtypeStruct((1, LANES), F32)]
    return pl.pallas_call(
        body, name=name, grid=(t // TM,), in_specs=in_specs, out_specs=out_specs, out_shape=out_shape,
        compiler_params=_cp("arbitrary"))(*dfox, *dret, *dsb, *dmla, drg, dlsf, proj, proj, proj, pos, invf_ret,
                                          invf_mla, lg_lanes, b_pad, g_q, g_kv, wq_pad, wkv_pad)


def _cumsum(x, reverse, name, partials=None):
    t, w = x.shape
    n = t // TM
    xs = [x] if partials is None else [x, partials]

    def body(*refs):
        x_refs, o_ref, carry = refs[:len(xs)], refs[len(xs)], refs[len(xs) + 1]

        @pl.when(pl.program_id(0) == 0)
        def _():
            carry[...] = jnp.zeros_like(carry)

        r = lax.broadcasted_iota(jnp.int32, (TM, TM), 0)
        c = lax.broadcasted_iota(jnp.int32, (TM, TM), 1)
        tri = jnp.where((r <= c) if reverse else (r >= c), 1.0, 0.0).astype(BF16)
        v = x_refs[0][...]
        if partials is not None:
            lane = lax.broadcasted_iota(jnp.int32, (TM, LANES), 1)
            for hb in range(HEADS):
                v = v + jnp.where(lane == hb, jnp.sum(x_refs[1][:, _hs(hb)], axis=1, keepdims=True), 0.0)
        hi = v.astype(BF16)
        r1 = v - hi.astype(F32)
        mid = r1.astype(BF16)
        lo = (r1 - mid.astype(F32)).astype(BF16)
        cs = _bdot(tri, hi) + _bdot(tri, mid) + _bdot(tri, lo) + carry[...]
        o_ref[...] = cs
        carry[...] = cs[0:1, :] if reverse else cs[TM - 1:TM, :]

    imap = (lambda i: (n - 1 - i, 0)) if reverse else (lambda i: (i, 0))
    return pl.pallas_call(
        body, name=name, grid=(n,), in_specs=[pl.BlockSpec((TM, a.shape[1]), imap) for a in xs],
        out_specs=pl.BlockSpec((TM, w), imap),
        out_shape=jax.ShapeDtypeStruct((t, w), F32), scratch_shapes=[pltpu.VMEM((1, w), F32)],
        compiler_params=_cp("arbitrary"))(*xs)


HB_FWD = 4
HB_BWD = 2


def _q_spec(hb):
    return pl.BlockSpec((BQ, hb * LANES), lambda g, i: (i, g))


def _kv_spec(t, which, hb):
    return pl.BlockSpec((t, hb * LANES), lambda g, i, w=which: (0, w * (HEADS // hb) + g))


def _acc_spec(t, hb):
    return pl.BlockSpec((t, hb * LANES), lambda g, i: (0, g))


def _hs(hh):
    return slice(hh * LANES, (hh + 1) * LANES)


def _tile_iota(rows, cols):
    return (lax.broadcasted_iota(jnp.int32, (rows, cols), 0), lax.broadcasted_iota(jnp.int32, (rows, cols), 1))


def _kvt_spec(nkv, width, which, hb):
    return pl.BlockSpec((nkv, hb * LANES, width), lambda g, i, w=which: (0, w * (HEADS // hb) + g, 0))


def _qrow_spec(hb):
    return pl.BlockSpec((hb, 1, 1, BQ), lambda g, i: (g, i, 0, 0))


def _vis_t(chunk_mask):
    r, c = _tile_iota(BQ, BQ)
    return ((r >> CHUNK_SHIFT) <= (c >> CHUNK_SHIFT)) if chunk_mask else (r <= c)


def _softmax_fwd(qkv, kvt, *, chunk_mask, scale, name):
    t = qkv.shape[0]
    nq = t // BQ
    hb = HB_FWD

    def body(q_ref, k_ref, vt_ref, o_ref, lse_ref, m_sc, l_sc, acc_sc):
        i = pl.program_id(1)
        m_sc[...] = jnp.full((hb, 1, BQ), NEG, F32)
        l_sc[...] = jnp.zeros((hb, 1, BQ), F32)
        acc_sc[...] = jnp.zeros((hb, LANES, BQ), F32)

        def tile(j, masked):
            off = pl.multiple_of(j * BQ, BQ)
            vis = _vis_t(chunk_mask) if masked else None
            ss = [_bdot(k_ref[pl.ds(off, BQ), _hs(hh)], q_ref[:, _hs(hh)], NT) for hh in range(hb)]
            ps, alphas = [], []
            for hh in range(hb):
                s = ss[hh]
                if scale != 1.0:
                    s = s * scale
                if masked:
                    s = jnp.where(vis, s, NEG)
                m_old = m_sc[hh]
                m_new = jnp.maximum(m_old, jnp.max(s, axis=0, keepdims=True))
                alpha = jnp.exp(m_old - m_new)
                p = jnp.exp(s - m_new)
                l_sc[hh] = alpha * l_sc[hh] + jnp.sum(p, axis=0, keepdims=True)
                m_sc[hh] = m_new
                ps.append(p.astype(BF16))
                alphas.append(alpha)
            for hh in range(hb):
                acc_sc[hh] = alphas[hh] * acc_sc[hh] + _bdot(vt_ref[j, _hs(hh), :], ps[hh])

        def loop(j, carry):
            tile(j, False)
            return carry

        lax.fori_loop(0, i, loop, 0)
        tile(i, True)
        for hh in range(hb):
            l = l_sc[hh]
            o_ref[:, _hs(hh)] = (acc_sc[hh] / l).T
            lse_ref[hh, 0] = m_sc[hh] + jnp.log(l)

    return pl.pallas_call(
        body, name=name, grid=(HEADS // hb, nq),
        in_specs=[_q_spec(hb), _kv_spec(t, 1, hb), _kvt_spec(nq, BQ, 1, hb)],
        out_specs=[_q_spec(hb), _qrow_spec(hb)],
        out_shape=[jax.ShapeDtypeStruct((t, HP), F32), jax.ShapeDtypeStruct((HEADS, nq, 1, BQ), F32)],
        scratch_shapes=[pltpu.VMEM((hb, 1, BQ), F32), pltpu.VMEM((hb, 1, BQ), F32), pltpu.VMEM((hb, LANES, BQ), F32)],
        compiler_params=_cp("parallel", "arbitrary"))(qkv, qkv, kvt)


def _softmax_bwd(qkv, kvt, do, lse, delta, *, bias, chunk_mask, scale, name):
    t = qkv.shape[0]
    nq = t // BQ
    hb = HB_BWD

    def body(*refs):
        if bias:
            (q_ref, k_ref, v_ref, kt_ref, do_ref, lse_ref, dl_ref, dq_ref, dk_ref, dv_ref, dck_ref, dcq_ref,
             dq_sc, dcq_sc) = refs
            dcq_sc[...] = jnp.zeros((hb, 1, BQ), F32)
        else:
            q_ref, k_ref, v_ref, kt_ref, do_ref, lse_ref, dl_ref, dq_ref, dk_ref, dv_ref, dq_sc = refs
        i = pl.program_id(1)

        @pl.when(i == 0)
        def _():
            dk_ref[...] = jnp.zeros_like(dk_ref)
            dv_ref[...] = jnp.zeros_like(dv_ref)
            if bias:
                dck_ref[...] = jnp.zeros_like(dck_ref)

        dq_sc[...] = jnp.zeros((hb, LANES, BQ), F32)

        def tile(j, masked):
            off = pl.multiple_of(j * BQ, BQ)
            vis = _vis_t(chunk_mask) if masked else None
            qs = [q_ref[:, _hs(hh)] for hh in range(hb)]
            dobs = [do_ref[:, _hs(hh)].astype(BF16) for hh in range(hb)]
            ss = [_bdot(k_ref[pl.ds(off, BQ), _hs(hh)], qs[hh], NT) for hh in range(hb)]
            dps = [_bdot(v_ref[pl.ds(off, BQ), _hs(hh)], dobs[hh], NT) for hh in range(hb)]
            pbs, dsbs = [], []
            for hh in range(hb):
                s = ss[hh]
                if scale != 1.0:
                    s = s * scale
                p = jnp.exp(s - lse_ref[hh, 0])
                if masked:
                    p = jnp.where(vis, p, 0.0)
                ds = p * (dps[hh] - dl_ref[hh, 0])
                if bias:
                    dck_ref[pl.ds(off, BQ), _hs(hh)] -= ds[:, 0:LANES] + ds[:, LANES:BQ]
                    dcq_sc[hh] += jnp.sum(ds, axis=0, keepdims=True)
                if scale != 1.0:
                    ds = ds * scale
                pbs.append(p.astype(BF16))
                dsbs.append(ds.astype(BF16))
            for hh in range(hb):
                sl = _hs(hh)
                dv_ref[pl.ds(off, BQ), sl] += _bdot(pbs[hh], dobs[hh])
                dk_ref[pl.ds(off, BQ), sl] += _bdot(dsbs[hh], qs[hh])
                dq_sc[hh] += _bdot(kt_ref[j, sl, :], dsbs[hh])

        def loop(j, carry):
            tile(j, False)
            return carry

        lax.fori_loop(0, i, loop, 0)
        tile(i, True)
        for hh in range(hb):
            dq_ref[:, _hs(hh)] = dq_sc[hh].T
            if bias:
                dcq_ref[hh, 0] = dcq_sc[hh]

    in_specs = [_q_spec(hb), _kv_spec(t, 1, hb), _kv_spec(t, 2, hb), _kvt_spec(nq, BQ, 0, hb), _q_spec(hb),
                _qrow_spec(hb), _qrow_spec(hb)]
    out_specs = [_q_spec(hb), _acc_spec(t, hb), _acc_spec(t, hb)]
    out_shape = [jax.ShapeDtypeStruct((t, HP), F32)] * 3
    scratch = [pltpu.VMEM((hb, LANES, BQ), F32)]
    if bias:
        out_specs += [_acc_spec(t, hb), _qrow_spec(hb)]
        out_shape += [jax.ShapeDtypeStruct((t, HP), F32), jax.ShapeDtypeStruct((HEADS, nq, 1, BQ), F32)]
        scratch.append(pltpu.VMEM((hb, 1, BQ), F32))
    return pl.pallas_call(
        body, name=name, grid=(HEADS // hb, nq), in_specs=in_specs, out_specs=out_specs, out_shape=out_shape,
        scratch_shapes=scratch,
        compiler_params=_cp("parallel", "arbitrary"))(qkv, qkv, qkv, kvt, do, lse, delta)


def _ret_diag_decay(lg1, keys_on_rows=False):
    r, c = _tile_iota(BQ, BQ)
    qn, km = (c, r) if keys_on_rows else (r, c)
    dd = jnp.where(km > qn, jnp.exp((2.0 * lg1) * (km - qn).astype(F32)), 1.0)
    return jnp.where((km >> CHUNK_SHIFT) <= (qn >> CHUNK_SHIFT), dd, 0.0)


def _lg_spec(hb):
    return pl.BlockSpec((hb, 1, LANES), lambda g, i: (g, 0, 0))


def _ret_fwd(qkv, lg_heads, name):
    t = qkv.shape[0]
    nq = t // BQ
    hb = HB_FWD

    def body(lg_ref, q_ref, k_ref, v_ref, o_ref, acc_sc):
        i = pl.program_id(1)
        acc_sc[...] = jnp.zeros((hb, BQ, LANES), F32)

        def loop(j, carry):
            off = pl.multiple_of(j * BQ, BQ)
            span = ((i - j) * BQ).astype(F32)
            aa = [_bdot(q_ref[:, _hs(hh)], k_ref[pl.ds(off, BQ), _hs(hh)], NT).astype(BF16) for hh in range(hb)]
            for hh in range(hb):
                cf = jnp.exp(lg_ref[hh][:, 0:1] * span)
                acc_sc[hh] += cf * _bdot(aa[hh], v_ref[pl.ds(off, BQ), _hs(hh)])
            return carry

        lax.fori_loop(0, i, loop, 0)
        off = pl.multiple_of(i * BQ, BQ)
        for hh in range(hb):
            sl = _hs(hh)
            a = _bdot(q_ref[:, sl], k_ref[pl.ds(off, BQ), sl], NT) * _ret_diag_decay(lg_ref[hh][:, 0:1])
            o_ref[:, sl] = acc_sc[hh] + _bdot(a.astype(BF16), v_ref[pl.ds(off, BQ), sl])

    return pl.pallas_call(
        body, name=name, grid=(HEADS // hb, nq),
        in_specs=[_lg_spec(hb), _q_spec(hb), _kv_spec(t, 1, hb), _kv_spec(t, 2, hb)],
        out_specs=_q_spec(hb), out_shape=jax.ShapeDtypeStruct((t, HP), F32),
        scratch_shapes=[pltpu.VMEM((hb, BQ, LANES), F32)],
        compiler_params=_cp("parallel", "arbitrary"))(lg_heads, qkv, qkv, qkv)


def _ret_bwd(qkv, kt, lg_heads, do, name):
    t = qkv.shape[0]
    nq = t // BQ
    hb = HB_BWD

    def body(lg_ref, q_ref, k_ref, v_ref, kt_ref, do_ref, dq_ref, dk_ref, dv_ref, dq_sc):
        i = pl.program_id(1)

        @pl.when(i == 0)
        def _():
            dk_ref[...] = jnp.zeros_like(dk_ref)
            dv_ref[...] = jnp.zeros_like(dv_ref)

        dq_sc[...] = jnp.zeros((hb, LANES, BQ), F32)

        def tile(j, diag):
            off = pl.multiple_of(j * BQ, BQ)
            qs = [q_ref[:, _hs(hh)] for hh in range(hb)]
            dobs = [do_ref[:, _hs(hh)].astype(BF16) for hh in range(hb)]
            aa = [_bdot(k_ref[pl.ds(off, BQ), _hs(hh)], qs[hh], NT) for hh in range(hb)]
            das = [_bdot(v_ref[pl.ds(off, BQ), _hs(hh)], dobs[hh], NT) for hh in range(hb)]
            abs_, dabs, cfs = [], [], []
            for hh in range(hb):
                lg1 = lg_ref[hh][:, 0:1]
                if diag:
                    dd = _ret_diag_decay(lg1, keys_on_rows=True)
                    abs_.append((aa[hh] * dd).astype(BF16))
                    dabs.append((das[hh] * dd).astype(BF16))
                    cfs.append(1.0)
                else:
                    abs_.append(aa[hh].astype(BF16))
                    dabs.append(das[hh].astype(BF16))
                    cfs.append(jnp.exp(lg1 * ((i - j) * BQ).astype(F32)))
            for hh in range(hb):
                sl = _hs(hh)
                dv_ref[pl.ds(off, BQ), sl] += cfs[hh] * _bdot(abs_[hh], dobs[hh])
                dk_ref[pl.ds(off, BQ), sl] += cfs[hh] * _bdot(dabs[hh], qs[hh])
                dq_sc[hh] += cfs[hh] * _bdot(kt_ref[j, sl, :], dabs[hh])

        def loop(j, carry):
            tile(j, False)
            return carry

        lax.fori_loop(0, i, loop, 0)
        tile(i, True)
        for hh in range(hb):
            dq_ref[:, _hs(hh)] = dq_sc[hh].T

    return pl.pallas_call(
        body, name=name, grid=(HEADS // hb, nq),
        in_specs=[_lg_spec(hb), _q_spec(hb), _kv_spec(t, 1, hb), _kv_spec(t, 2, hb), _kvt_spec(nq, BQ, 0, hb),
                  _q_spec(hb)],
        out_specs=[_q_spec(hb), _acc_spec(t, hb), _acc_spec(t, hb)],
        out_shape=[jax.ShapeDtypeStruct((t, HP), F32)] * 3,
        scratch_shapes=[pltpu.VMEM((hb, LANES, BQ), F32)],
        compiler_params=_cp("parallel", "arbitrary"))(lg_heads, qkv, qkv, qkv, kt, do)


def _sb_tile_logs(q, kb, vis):
    z = _bdot(kb, q, NT)
    ls = -(jnp.maximum(z, 0.0) + jnp.log(1.0 + jnp.exp(-jnp.abs(z))))
    if vis is not None:
        ls = jnp.where(vis, ls, 0.0)
    return z, ls


def _sb_vis(i, j):
    r, c = _tile_iota(TKS, BQ)
    return (j * TKS + r) < (i * BQ + c)


def _sb_later(ls, after):
    hi, lo = _split2(ls)
    return _bdot(after, hi) + _bdot(after, lo)


def _sb_fwd(qkv, kvt, name):
    t = qkv.shape[0]
    nq = t // BQ
    per = BQ // TKS
    hb = HB_FWD

    def body(q_ref, k_ref, vt_ref, o_ref, tot_ref, acc_sc, r_sc):
        i = pl.program_id(1)
        acc_sc[...] = jnp.zeros((hb, LANES, BQ), F32)
        r_sc[...] = jnp.zeros((hb, 1, BQ), F32)
        mr, mc = _tile_iota(TKS, TKS)
        after = jnp.where(mc > mr, 1.0, 0.0).astype(BF16)

        def tile(j, masked):
            off = pl.multiple_of(j * TKS, TKS)
            vis = _sb_vis(i, j) if masked else None
            zl = [_sb_tile_logs(q_ref[:, _hs(hh)], k_ref[pl.ds(off, TKS), _hs(hh)], vis) for hh in range(hb)]
            laters = [_sb_later(zl[hh][1], after) for hh in range(hb)]
            ws = []
            for hh in range(hb):
                z, ls = zl[hh]
                w = jnp.exp(z + ls + laters[hh] + r_sc[hh])
                if masked:
                    w = jnp.where(vis, w, 0.0)
                ws.append(w.astype(BF16))
                r_sc[hh] += jnp.sum(ls, axis=0, keepdims=True)
            for hh in range(hb):
                acc_sc[hh] += _bdot(vt_ref[j, _hs(hh), :], ws[hh])

        for d in range(per):
            tile(per * i + (per - 1 - d), True)

        def loop(jj, carry):
            tile(per * i - 1 - jj, False)
            return carry

        lax.fori_loop(0, per * i, loop, 0)
        for hh in range(hb):
            o_ref[:, _hs(hh)] = acc_sc[hh].T
            tot_ref[hh, 0] = r_sc[hh]

    return pl.pallas_call(
        body, name=name, grid=(HEADS // hb, nq),
        in_specs=[_q_spec(hb), _kv_spec(t, 1, hb), _kvt_spec(t // TKS, TKS, 1, hb)],
        out_specs=[_q_spec(hb), _qrow_spec(hb)],
        out_shape=[jax.ShapeDtypeStruct((t, HP), F32), jax.ShapeDtypeStruct((HEADS, nq, 1, BQ), F32)],
        scratch_shapes=[pltpu.VMEM((hb, LANES, BQ), F32), pltpu.VMEM((hb, 1, BQ), F32)],
        compiler_params=_cp("parallel", "arbitrary"))(qkv, qkv, kvt)


def _sb_bwd(qkv, kvt, do, tot, name):
    t = qkv.shape[0]
    nq = t // BQ
    per = BQ // TKS
    hb = HB_BWD

    def body(q_ref, k_ref, v_ref, kt_ref, do_ref, tot_ref, dq_ref, dk_ref, dv_ref, dq_sc, p_sc, g_sc):
        i = pl.program_id(1)

        @pl.when(i == 0)
        def _():
            dk_ref[...] = jnp.zeros_like(dk_ref)
            dv_ref[...] = jnp.zeros_like(dv_ref)

        dq_sc[...] = jnp.zeros((hb, LANES, BQ), F32)
        p_sc[...] = jnp.zeros((hb, 1, BQ), F32)
        g_sc[...] = jnp.zeros((hb, 1, BQ), F32)
        mr, mc = _tile_iota(TKS, TKS)
        after = jnp.where(mc > mr, 1.0, 0.0).astype(BF16)
        before = jnp.where(mc < mr, 1.0, 0.0).astype(BF16)

        def tile(j, masked):
            off = pl.multiple_of(j * TKS, TKS)
            vis = _sb_vis(i, j) if masked else None
            qs = [q_ref[:, _hs(hh)] for hh in range(hb)]
            dobs = [do_ref[:, _hs(hh)].astype(BF16) for hh in range(hb)]
            zl = [_sb_tile_logs(qs[hh], k_ref[pl.ds(off, TKS), _hs(hh)], vis) for hh in range(hb)]
            dws = [_bdot(v_ref[pl.ds(off, TKS), _hs(hh)], dobs[hh], NT) for hh in range(hb)]
            laters = [_sb_later(zl[hh][1], after) for hh in range(hb)]
            ws, gs = [], []
            for hh in range(hb):
                z, ls = zl[hh]
                own = jnp.sum(ls, axis=0, keepdims=True)
                rest = tot_ref[hh, 0] - p_sc[hh] - own
                w = jnp.exp(z + ls + laters[hh] + rest)
                if masked:
                    w = jnp.where(vis, w, 0.0)
                p_sc[hh] += own
                ws.append(w.astype(BF16))
                gs.append(dws[hh] * w)
            gins = []
            for hh in range(hb):
                ghi, glo = _split2(gs[hh])
                gins.append(_bdot(before, ghi) + _bdot(before, glo))
            dzbs = []
            for hh in range(hb):
                g = gs[hh]
                stay = jnp.exp(zl[hh][1])
                dz = g * stay - (1.0 - stay) * (gins[hh] + g_sc[hh])
                if masked:
                    dz = jnp.where(vis, dz, 0.0)
                g_sc[hh] += jnp.sum(g, axis=0, keepdims=True)
                dzbs.append(dz.astype(BF16))
            for hh in range(hb):
                sl = _hs(hh)
                dv_ref[pl.ds(off, TKS), sl] += _bdot(ws[hh], dobs[hh])
                dk_ref[pl.ds(off, TKS), sl] += _bdot(dzbs[hh], qs[hh])
                dq_sc[hh] += _bdot(kt_ref[j, sl, :], dzbs[hh])

        def loop(j, carry):
            tile(j, False)
            return carry

        lax.fori_loop(0, per * i, loop, 0)
        for d in range(per):
            tile(per * i + d, True)
        for hh in range(hb):
            dq_ref[:, _hs(hh)] = dq_sc[hh].T

    return pl.pallas_call(
        body, name=name, grid=(HEADS // hb, nq),
        in_specs=[_q_spec(hb), _kv_spec(t, 1, hb), _kv_spec(t, 2, hb), _kvt_spec(t // TKS, TKS, 0, hb), _q_spec(hb),
                  _qrow_spec(hb)],
        out_specs=[_q_spec(hb), _acc_spec(t, hb), _acc_spec(t, hb)],
        out_shape=[jax.ShapeDtypeStruct((t, HP), F32)] * 3,
        scratch_shapes=[pltpu.VMEM((hb, LANES, BQ), F32), pltpu.VMEM((hb, 1, BQ), F32), pltpu.VMEM((hb, 1, BQ), F32)],
        compiler_params=_cp("parallel", "arbitrary"))(qkv, qkv, qkv, kvt, do, tot)


def _sigmoid(v):
    return 1.0 / (1.0 + jnp.exp(-v))


def _post_fwd(oa, ob, oc, od, proj, g_pad, name):
    t = oa.shape[0]

    def body(oa_ref, ob_ref, oc_ref, od_ref, rg_ref, g_ref, mx_ref):
        g = g_ref[...]

        def group(o, gg):
            r = lax.rsqrt(jnp.sum(o * o, axis=-1, keepdims=True) * (1.0 / GROUP) + EPS)
            return (o * r * gg).astype(BF16)

        mx_ref[:, 0:HP] = group(oa_ref[...], g[:, 0:HP])
        mx_ref[:, HP:2 * HP] = group(ob_ref[...], g[:, HP:2 * HP])
        mx_ref[:, 3 * HP:4 * HP] = group(od_ref[...], g[:, 3 * HP:4 * HP])
        real = lax.broadcasted_iota(jnp.int32, (TM, LANES), 1) < HEAD_DIM
        for hb in range(HEADS):
            sl = slice(hb * LANES, (hb + 1) * LANES)
            o = oc_ref[:, sl]
            mu = jnp.sum(o, axis=-1, keepdims=True) * (1.0 / HEAD_DIM)
            dlt = jnp.where(real, o - mu, 0.0)
            var = jnp.sum(dlt * dlt, axis=-1, keepdims=True) * (1.0 / HEAD_DIM)
            yn = dlt * lax.rsqrt(var + EPS) * g[:, 2 * HP + hb * LANES:2 * HP + (hb + 1) * LANES]
            x = rg_ref[:, sl]
            mx_ref[:, 2 * HP + hb * LANES:2 * HP + (hb + 1) * LANES] = (yn * (x * _sigmoid(x))).astype(BF16)

    rg_spec = pl.BlockSpec((TM, HP), lambda i: (i, OFF_RG // HP))
    return pl.pallas_call(
        body, name=name, grid=(t // TM,),
        in_specs=[_row_spec(HP)] * 4 + [rg_spec, _vec_spec(4 * HP)], out_specs=_row_spec(4 * HP),
        out_shape=jax.ShapeDtypeStruct((t, 4 * HP), BF16), compiler_params=_cp("parallel"))(oa, ob, oc, od, proj, g_pad)


def _post_bwd(dmx, oa, ob, oc, od, proj, g_pad, name):
    t = oa.shape[0]

    def body(dm_ref, oa_ref, ob_ref, oc_ref, od_ref, rg_ref, g_ref,
             doa_ref, dob_ref, doc_ref, dod_ref, dla_ref, dlb_ref, drg_ref, dg_ref):
        @pl.when(pl.program_id(0) == 0)
        def _():
            dg_ref[...] = jnp.zeros_like(dg_ref)

        g = g_ref[...]

        def group_bwd(dm, o, gg):
            r = lax.rsqrt(jnp.sum(o * o, axis=-1, keepdims=True) * (1.0 / GROUP) + EPS)
            oh = o * r
            dgp = jnp.sum(dm * oh, axis=0, keepdims=True)
            dyh = dm * gg
            do = r * (dyh - oh * (jnp.sum(dyh * oh, axis=-1, keepdims=True) * (1.0 / GROUP)))
            return do, dgp

        def delta_bc(do, o):
            prod = do * o
            lane = lax.broadcasted_iota(jnp.int32, (TM, LANES), 1)
            out = jnp.zeros((TM, LANES), F32)
            for hb in range(HEADS):
                out = jnp.where(lane == hb, jnp.sum(prod[:, hb * LANES:(hb + 1) * LANES], axis=-1, keepdims=True), out)
            return out

        oa = oa_ref[...]
        do_a, dga = group_bwd(dm_ref[:, 0:HP], oa, g[:, 0:HP])
        doa_ref[...] = do_a
        dla_ref[...] = delta_bc(do_a, oa)
        dg_ref[:, 0:HP] += dga
        ob = ob_ref[...]
        do_b, dgb = group_bwd(dm_ref[:, HP:2 * HP], ob, g[:, HP:2 * HP])
        dob_ref[...] = do_b
        dlb_ref[...] = delta_bc(do_b, ob)
        dg_ref[:, HP:2 * HP] += dgb
        do_d, dgd = group_bwd(dm_ref[:, 3 * HP:4 * HP], od_ref[...], g[:, 3 * HP:4 * HP])
        dod_ref[...] = do_d
        dg_ref[:, 3 * HP:4 * HP] += dgd
        real = lax.broadcasted_iota(jnp.int32, (TM, LANES), 1) < HEAD_DIM
        for hb in range(HEADS):
            sl = slice(hb * LANES, (hb + 1) * LANES)
            gsl = slice(2 * HP + hb * LANES, 2 * HP + (hb + 1) * LANES)
            o = oc_ref[:, sl]
            mu = jnp.sum(o, axis=-1, keepdims=True) * (1.0 / HEAD_DIM)
            dlt = jnp.where(real, o - mu, 0.0)
            var = jnp.sum(dlt * dlt, axis=-1, keepdims=True) * (1.0 / HEAD_DIM)
            rstd = lax.rsqrt(var + EPS)
            dhat = dlt * rstd
            gc = g[:, gsl]
            x = rg_ref[:, sl]
            sg = _sigmoid(x)
            dm = dm_ref[:, gsl]
            drg_ref[:, sl] = dm * (dhat * gc) * (sg * (1.0 + x * (1.0 - sg)))
            dyn = dm * (x * sg)
            dg_ref[:, gsl] += jnp.sum(dyn * dhat, axis=0, keepdims=True)
            ddh = dyn * gc
            m1 = jnp.sum(ddh, axis=-1, keepdims=True) * (1.0 / HEAD_DIM)
            m2 = jnp.sum(ddh * dhat, axis=-1, keepdims=True) * (1.0 / HEAD_DIM)
            doc_ref[:, sl] = jnp.where(real, rstd * (ddh - m1 - dhat * m2), 0.0)

    rg_spec = pl.BlockSpec((TM, HP), lambda i: (i, OFF_RG // HP))
    hp = _row_spec(HP)
    return pl.pallas_call(
        body, name=name, grid=(t // TM,),
        in_specs=[_row_spec(4 * HP), hp, hp, hp, hp, rg_spec, _vec_spec(4 * HP)],
        out_specs=[hp] * 4 + [_row_spec(LANES)] * 2 + [hp, _vec_spec(4 * HP)],
        out_shape=[jax.ShapeDtypeStruct((t, HP), F32)] * 4 + [jax.ShapeDtypeStruct((t, LANES), F32)] * 2
        + [jax.ShapeDtypeStruct((t, HP), F32), jax.ShapeDtypeStruct((1, 4 * HP), F32)],
        compiler_params=_cp("arbitrary"))(dmx, oa, ob, oc, od, proj, g_pad)


def _mesh_pos():
    return lax.axis_index("x"), lax.axis_index("y"), lax.axis_index("c")


def _peer(pos, k):
    x, y, c = pos
    px = 1 - x if (k >> 2) & 1 else x
    py = 1 - y if (k >> 1) & 1 else y
    pc = 1 - c if k & 1 else c
    return (px, py, pc), 4 * px + 2 * py + pc


def _exchange(arrs, gather, name):
    n = len(arrs)

    def body(*refs):
        ins, outs = refs[:n], refs[n:2 * n]
        send_sems, recv_sems, loc_sems = refs[2 * n:]
        pos = _mesh_pos()
        me = 4 * pos[0] + 2 * pos[1] + pos[2]
        local = []
        for a in range(n):
            src = ins[a] if gather else ins[a].at[me]
            cp = pltpu.make_async_copy(src, outs[a].at[me], loc_sems.at[a])
            cp.start()
            local.append(cp)
        sends, recvs = [], []
        for k in range(1, N_DEV):
            peer, pid = _peer(pos, k)
            for a in range(n):
                s = a * (N_DEV - 1) + k - 1
                src = ins[a] if gather else ins[a].at[pid]
                cp = pltpu.make_async_remote_copy(
                    src_ref=src, dst_ref=outs[a].at[me], send_sem=send_sems.at[s], recv_sem=recv_sems.at[s],
                    device_id=peer, device_id_type=pl.DeviceIdType.MESH)
                cp.start()
                sends.append(cp)
                recvs.append(pltpu.make_async_remote_copy(
                    src_ref=src, dst_ref=outs[a].at[pid], send_sem=send_sems.at[s], recv_sem=recv_sems.at[s],
                    device_id=peer, device_id_type=pl.DeviceIdType.MESH))
        for cp in recvs:
            cp.wait_recv()
        for cp in sends:
            cp.wait_send()
        for cp in local:
            cp.wait()

    any_spec = pl.BlockSpec(memory_space=pl.ANY)
    out_shape = [jax.ShapeDtypeStruct((N_DEV,) + tuple(a.shape) if gather else tuple(a.shape), a.dtype) for a in arrs]
    return pl.pallas_call(
        body, name=name, in_specs=[any_spec] * n, out_specs=[any_spec] * n, out_shape=out_shape,
        scratch_shapes=[pltpu.SemaphoreType.DMA((n * (N_DEV - 1),)), pltpu.SemaphoreType.DMA((n * (N_DEV - 1),)),
                        pltpu.SemaphoreType.DMA((n,))],
        compiler_params=pltpu.CompilerParams(has_side_effects=True))(*arrs)


def _adam_vals(w, g, m, v):
    m = ADAM_B1 * m + (1.0 - ADAM_B1) * g
    v = ADAM_B2 * v + (1.0 - ADAM_B2) * (g * g)
    m_hat = m / ADAM_C1
    v_hat = v / ADAM_C2
    delta = -ADAM_LR * (m_hat / (jnp.sqrt(v_hat) + ADAM_EPS) + ADAM_WD * w)
    return delta, m, v


def _small_allreduce_adam(part, w, m, v, name):
    rows = part.shape[0]

    def body(p_ref, w_ref, m_ref, v_ref, g_ref, d_ref, nm_ref, nv_ref, gath, send_sems, recv_sems):
        pos = _mesh_pos()
        me = 4 * pos[0] + 2 * pos[1] + pos[2]
        gath[me] = p_ref[...]
        sends, recvs = [], []
        for k in range(1, N_DEV):
            peer, pid = _peer(pos, k)
            cp = pltpu.make_async_remote_copy(
                src_ref=p_ref, dst_ref=gath.at[me], send_sem=send_sems.at[k - 1], recv_sem=recv_sems.at[k - 1],
                device_id=peer, device_id_type=pl.DeviceIdType.MESH)
            cp.start()
            sends.append(cp)
            recvs.append(pltpu.make_async_remote_copy(
                src_ref=p_ref, dst_ref=gath.at[pid], send_sem=send_sems.at[k - 1], recv_sem=recv_sems.at[k - 1],
                device_id=peer, device_id_type=pl.DeviceIdType.MESH))
        for cp in recvs:
            cp.wait_recv()
        for cp in sends:
            cp.wait_send()
        g = gath[0]
        for p in range(1, N_DEV):
            g = g + gath[p]
        g_ref[...] = g
        d, nm, nv = _adam_vals(w_ref[...], g, m_ref[...], v_ref[...])
        d_ref[...] = d
        nm_ref[...] = nm
        nv_ref[...] = nv

    vm = pl.BlockSpec(memory_space=pltpu.VMEM)
    sds = jax.ShapeDtypeStruct((rows, LANES), F32)
    return pl.pallas_call(
        body, name=name, in_specs=[vm] * 4, out_specs=[vm] * 4, out_shape=[sds] * 4,
        scratch_shapes=[pltpu.VMEM((N_DEV, rows, LANES), F32), pltpu.SemaphoreType.DMA((N_DEV - 1,)),
                        pltpu.SemaphoreType.DMA((N_DEV - 1,))],
        compiler_params=pltpu.CompilerParams(has_side_effects=True))(part, w, m, v)


def _reduce_adam(recv, w, m, v, name):
    shape = w.shape
    c = shape[-1]
    r = int(np.prod(shape[:-1]))
    recv2, w2, m2, v2 = recv.reshape(N_DEV, r, c), w.reshape(r, c), m.reshape(r, c), v.reshape(r, c)
    tr = r
    while tr * c * 4 > (1 << 20) and tr % 16 == 0:
        tr //= 2

    def body(r_ref, w_ref, m_ref, v_ref, g_ref, d_ref, nm_ref, nv_ref):
        g = r_ref[0].astype(F32)
        for p in range(1, N_DEV):
            g = g + r_ref[p].astype(F32)
        g_ref[...] = g
        d, nm, nv = _adam_vals(w_ref[...], g, m_ref[...], v_ref[...])
        d_ref[...] = d
        nm_ref[...] = nm
        nv_ref[...] = nv

    spec = pl.BlockSpec((tr, c), lambda i: (i, 0))
    sds = jax.ShapeDtypeStruct((r, c), F32)
    outs = pl.pallas_call(
        body, name=name, grid=(r // tr,),
        in_specs=[pl.BlockSpec((N_DEV, tr, c), lambda i: (0, i, 0)), spec, spec, spec],
        out_specs=[spec] * 4, out_shape=[sds] * 4, compiler_params=_cp("parallel"))(recv2, w2, m2, v2)
    return [o.reshape(shape) for o in outs]


def _pad_heads(w, real=HEAD_DIM):
    lead = w.shape[:-1]
    w = w.reshape(lead + (HEADS, real))
    w = jnp.pad(w, [(0, 0)] * len(lead) + [(0, 0), (0, LANES - real)])
    return w.reshape(lead + (HP,))


def _unpad_heads(w, real=HEAD_DIM):
    lead = w.shape[:-1]
    return w.reshape(lead + (HEADS, LANES))[..., :real].reshape(lead + (HEADS * real,))


_IN_SEGS = (("fq", 0, 256), ("fk", 256, 512), ("fv", 512, 768), ("ff", 768, 772), ("cq", 772, 1028),
            ("ckv", 1028, 1156), ("kr", 1156, 1188), ("rq", 1188, 1444), ("rk", 1444, 1700), ("rv", 1700, 1956),
            ("rg", 1956, 2212), ("sq", 2212, 2468), ("sk", 2468, 2724), ("sv", 2724, 2980))


def _pad_w_in(w):
    s = {n: w[:, a:b] for n, a, b in _IN_SEGS}
    rows = w.shape[0]
    z = lambda n: jnp.zeros((rows, n), w.dtype)
    parts = [_pad_heads(s[n]) for n in ("fq", "fk", "fv", "rq", "rk", "rv", "sq", "sk", "sv", "rg")]
    parts += [s["cq"], s["ckv"], z(HEAD_DIM), s["kr"], z(LANES - HEAD_DIM - ROPE_MLA), s["ff"], z(LANES - HEADS),
              z(NP_IN - OFF_FF - LANES)]
    return jnp.concatenate(parts, axis=1)


def _unpad_w_in(wp):
    def heads(off):
        return _unpad_heads(wp[:, off:off + HP])

    parts = [heads(OFF_FOX), heads(OFF_FOX + HP), heads(OFF_FOX + 2 * HP), wp[:, OFF_FF:OFF_FF + HEADS],
             wp[:, OFF_CQ:OFF_CQ + Q_RANK], wp[:, OFF_CKV:OFF_CKV + KV_RANK],
             wp[:, OFF_KR + HEAD_DIM:OFF_KR + HEAD_DIM + ROPE_MLA],
             heads(OFF_RET), heads(OFF_RET + HP), heads(OFF_RET + 2 * HP), heads(OFF_RG),
             heads(OFF_SB), heads(OFF_SB + HP), heads(OFF_SB + 2 * HP)]
    return jnp.concatenate(parts, axis=1)


def _pad_w_kv(w):
    w4 = w.reshape(KV_RANK, HEADS, 2 * HEAD_DIM)
    k = w4[:, :, :HEAD_DIM].reshape(KV_RANK, GROUP)
    v = w4[:, :, HEAD_DIM:].reshape(KV_RANK, GROUP)
    return jnp.concatenate([_pad_heads(k), _pad_heads(v)], axis=1)


def _unpad_w_kv(wp):
    k = _unpad_heads(wp[:, :HP]).reshape(KV_RANK, HEADS, HEAD_DIM)
    v = _unpad_heads(wp[:, HP:]).reshape(KV_RANK, HEADS, HEAD_DIM)
    return jnp.concatenate([k, v], axis=-1).reshape(KV_RANK, HEADS * 2 * HEAD_DIM)


def _pad_rows_out(w):
    w = w.reshape(4 * HEADS, HEAD_DIM, D_MODEL)
    w = jnp.pad(w, ((0, 0), (0, LANES - HEAD_DIM), (0, 0)))
    return w.reshape(4 * HP, D_MODEL)


def _unpad_rows_out(wp):
    return wp.reshape(4 * HEADS, LANES, D_MODEL)[:, :HEAD_DIM, :].reshape(D_MODEL, D_MODEL)


def _pad_gain_out(g):
    g = jnp.pad(g.reshape(4 * HEADS, HEAD_DIM), ((0, 0), (0, LANES - HEAD_DIM)))
    return g.reshape(1, 4 * HP)


def _unpad_gain_out(gp):
    return gp.reshape(4 * HEADS, LANES)[:, :HEAD_DIM].reshape(D_MODEL)


_SMALL = (("g_mix_pre", 1024), ("g_mix_post", 1024), ("g_ffn_pre", 1024), ("g_ffn_post", 1024), ("g_mix_out", 1024),
          ("g_q_lora", 256), ("g_kv_lora", 128), ("b_forget", 4))


def _pack_small(vals):
    parts = []
    for name, n in _SMALL:
        a = vals[name].astype(F32)
        if n < LANES:
            a = jnp.pad(a, ((0, 0), (0, LANES - n)))
        parts.append(a)
    return jnp.concatenate(parts, axis=1).reshape(DEPTH * SMALL_ROWS, LANES)


def _unpack_small(packed):
    flat = packed.reshape(DEPTH, SMALL_ROWS * LANES)
    out, off = {}, 0
    for name, n in _SMALL:
        out[name] = flat[:, off:off + n]
        off += max(n, LANES)
    return out


def kernel(x, positions, g_mix_pre, w_in, b_forget, g_q_lora, w_q_up, g_kv_lora, w_kv_up, g_mix_out, w_out, g_mix_post, g_ffn_pre, w_ffn_up, w_ffn_down, g_ffn_post, loss_target, m_g_mix_pre, m_w_in, m_b_forget, m_g_q_lora, m_w_q_up, m_g_kv_lora, m_w_kv_up, m_g_mix_out, m_w_out, m_g_mix_post, m_g_ffn_pre, m_w_ffn_up, m_w_ffn_down, m_g_ffn_post, v_g_mix_pre, v_w_in, v_b_forget, v_g_q_lora, v_w_q_up, v_g_kv_lora, v_w_kv_up, v_g_mix_out, v_w_out, v_g_mix_post, v_g_ffn_pre, v_w_ffn_up, v_w_ffn_down, v_g_ffn_post):
    t = x.shape[1]
    nq = t // BQ
    x0 = x[0]
    tgt = loss_target[0]
    pos = positions[0].astype(F32).reshape(t, 1)

    half_r, half_m = HEAD_DIM // 2, ROPE_MLA // 2
    invf_r = ROPE_BASE ** (-jnp.arange(half_r, dtype=F32) / half_r)
    invf_m = ROPE_BASE ** (-jnp.arange(half_m, dtype=F32) / half_m)
    blk_r = jnp.concatenate([invf_r, invf_r, jnp.zeros((LANES - HEAD_DIM,), F32)])
    blk_m = jnp.concatenate([jnp.zeros((HEAD_DIM,), F32), invf_m, invf_m, jnp.zeros((LANES - HEAD_DIM - ROPE_MLA,), F32)])
    invf_ret = jnp.tile(blk_r, HEADS).reshape(1, HP)
    invf_mla = jnp.tile(blk_m, HEADS).reshape(1, HP)
    log_gamma = jnp.log1p(-jnp.power(2.0, -5.0 - jnp.arange(HEADS, dtype=F32)))
    lg_lanes = jnp.repeat(log_gamma, LANES).reshape(1, HP)
    lg_heads = jnp.broadcast_to(log_gamma[:, None, None], (HEADS, 1, LANES))

    big = [w_in, w_q_up, w_kv_up, w_out, w_ffn_up, w_ffn_down]
    gat = _exchange([w.astype(BF16) for w in big], True, "weights_all_gather")
    win_g = gat[0].transpose(1, 0, 2, 3).reshape(DEPTH, D_MODEL, D_IN)
    wq_g = gat[1].transpose(1, 2, 0, 3).reshape(DEPTH, Q_RANK, 384)
    wkv_g = gat[2].transpose(1, 2, 0, 3).reshape(DEPTH, KV_RANK, 512)
    wout_g = gat[3].transpose(1, 0, 2, 3).reshape(DEPTH, D_MODEL, D_MODEL)
    wup_g = gat[4].transpose(1, 2, 0, 3).reshape(DEPTH, D_MODEL, D_FF)
    wdn_g = gat[5].transpose(1, 0, 2, 3).reshape(DEPTH, D_FF, D_MODEL)

    row = lambda g: g.reshape(1, -1)
    layers = []
    for l in range(DEPTH):
        layers.append(dict(
            win=_pad_w_in(win_g[l]), wq=_pad_heads(wq_g[l], 96), wkv=_pad_w_kv(wkv_g[l]),
            wout=_pad_rows_out(wout_g[l]), wup=wup_g[l], wdn=wdn_g[l],
            g_pre=row(g_mix_pre[l]), g_post=row(g_mix_post[l]), g_fpre=row(g_ffn_pre[l]), g_fpost=row(g_ffn_post[l]),
            g_out=_pad_gain_out(g_mix_out[l]), g_q=row(g_q_lora[l]), g_kv=row(g_kv_lora[l]),
            b_pad=jnp.pad(b_forget[l], (0, LANES - HEADS)).reshape(1, LANES)))

    saved = []
    xin = x0
    h = _rms_fwd(xin, layers[0]["g_pre"], "rms_pre_0")
    loss_row = dx = None
    for l, p in enumerate(layers):
        s = dict(x=xin, h=h)
        proj = _mm(h, p["win"], name=f"in_proj_{l}")
        cum = _cumsum(_forget_lsf(proj, p["b_pad"], f"forget_lsf_{l}"), False, f"forget_cumsum_{l}")
        fox, ret, sb, mla, fox_t, sb_t, mla_t, ret_t = _prep_fwd(proj, cum, pos, invf_ret, invf_mla, lg_lanes, p["g_q"],
                                                          p["g_kv"], p["wq"], p["wkv"], f"prep_fwd_{l}")
        oa, lse_a = _softmax_fwd(fox, fox_t, chunk_mask=False, scale=1.0, name=f"fox_fwd_{l}")
        ob, lse_b = _softmax_fwd(mla, mla_t, chunk_mask=True, scale=96.0 ** -0.5, name=f"mla_fwd_{l}")
        oc = _ret_fwd(ret, lg_heads, f"ret_fwd_{l}")
        od, sb_tot = _sb_fwd(sb, sb_t, f"sb_fwd_{l}")
        mixed = _post_fwd(oa, ob, oc, od, proj, p["g_out"], f"post_fwd_{l}")
        mix = _mm(mixed, p["wout"], name=f"out_proj_{l}", tk=2048)
        x1, h2 = _add_rms_fwd(xin, mix, p["g_post"], p["g_fpre"], f"mix_residual_{l}")
        a = _mm(h2, p["wup"], name=f"ffn_up_{l}")
        y = _mm(a, p["wdn"], name=f"ffn_down_{l}", a_fn=_relu2)
        s.update(proj=proj, fox=fox, ret=ret, sb=sb, mla=mla, fox_t=fox_t, sb_t=sb_t, mla_t=mla_t, ret_t=ret_t, oa=oa, ob=ob, oc=oc,
                 od=od, sb_tot=sb_tot, lse_a=lse_a, lse_b=lse_b, mixed=mixed, mix=mix, x1=x1, h2=h2, a=a, y=y)
        saved.append(s)
        if l + 1 < DEPTH:
            xin, h = _add_rms_fwd(x1, y, p["g_fpost"], layers[l + 1]["g_pre"], f"ffn_residual_{l}")
        else:
            loss_row, dx = _final_loss(x1, y, p["g_fpost"], tgt, "loss")
    loss = lax.psum(loss_row[0, 0], ("x", "y", "c"))

    small_g = {n: [None] * DEPTH for n, _ in _SMALL}
    big_g = [[None] * DEPTH for _ in range(6)]
    for l in reversed(range(DEPTH)):
        p, s = layers[l], saved[l]
        dy, dg = _norm_bwd(dx, s["y"], p["g_fpost"], None, BF16, f"ffn_post_bwd_{l}")
        small_g["g_ffn_post"][l] = dg
        da = _mm(dy, p["wdn"], name=f"ffn_down_dx_{l}", tb=True, out_dtype=BF16, epi=_drelu2, epi_in=s["a"])
        big_g[5][l] = _mm(s["a"], dy, name=f"ffn_down_dw_{l}", ta=True, a_fn=_relu2, tm=1024, tk=512)
        big_g[4][l] = _mm(s["h2"], da, name=f"ffn_up_dw_{l}", ta=True, tm=1024, tk=512)
        dh2 = _mm(da, p["wup"], name=f"ffn_up_dx_{l}", tb=True)
        dx1, dg = _norm_bwd(dh2, s["x1"], p["g_fpre"], dx, F32, f"ffn_pre_bwd_{l}")
        small_g["g_ffn_pre"][l] = dg
        dmix, dg = _norm_bwd(dx1, s["mix"], p["g_post"], None, BF16, f"mix_post_bwd_{l}")
        small_g["g_mix_post"][l] = dg
        dmixed = _mm(dmix, p["wout"], name=f"out_proj_dx_{l}", tb=True)
        big_g[3][l] = _unpad_rows_out(_mm(s["mixed"], dmix, name=f"out_proj_dw_{l}", ta=True, tm=1024, tk=512))
        doa, dob, doc, dod, dla, dlb, drg, dgo = _post_bwd(dmixed, s["oa"], s["ob"], s["oc"], s["od"], s["proj"],
                                                           p["g_out"], f"post_bwd_{l}")
        small_g["g_mix_out"][l] = _unpad_gain_out(dgo).reshape(1, D_MODEL)
        as_rows = lambda a: a[:, :HEADS].T.reshape(HEADS, nq, 1, BQ)
        dfq, dfk, dfv, dcum_k, dcum_q = _softmax_bwd(s["fox"], s["fox_t"], doa, s["lse_a"], as_rows(dla), bias=True,
                                                     chunk_mask=False, scale=1.0, name=f"fox_bwd_{l}")
        dmq, dmk, dmv = _softmax_bwd(s["mla"], s["mla_t"], dob, s["lse_b"], as_rows(dlb), bias=False, chunk_mask=True,
                                     scale=96.0 ** -0.5, name=f"mla_bwd_{l}")
        drq, drk, drv = _ret_bwd(s["ret"], s["ret_t"], lg_heads, doc, f"ret_bwd_{l}")
        dsq, dsk, dsv = _sb_bwd(s["sb"], s["sb_t"], dod, s["sb_tot"], f"sb_bwd_{l}")
        dcum_q = jnp.pad(dcum_q.reshape(HEADS, t).T, ((0, 0), (0, LANES - HEADS)))
        dlsf = _cumsum(dcum_q, True, f"forget_cumsum_bwd_{l}", partials=dcum_k)
        dproj, dwq, dwkv, dgq, dgkv, dbf = _prep_bwd(
            (dfq, dfk, dfv), (drq, drk, drv), (dsq, dsk, dsv), (dmq, dmk, dmv), drg, dlsf, s["proj"], pos, invf_ret,
            invf_mla, lg_lanes, p["b_pad"], p["g_q"], p["g_kv"], p["wq"], p["wkv"], f"prep_bwd_{l}")
        small_g["g_q_lora"][l] = dgq
        small_g["g_kv_lora"][l] = dgkv
        small_g["b_forget"][l] = dbf[:, :HEADS]
        big_g[1][l] = _unpad_heads(dwq, 96)
        big_g[2][l] = _unpad_w_kv(dwkv)
        big_g[0][l] = _unpad_w_in(_mm(s["h"], dproj, name=f"in_proj_dw_{l}", ta=True, tm=1024, tk=512))
        dh = _mm(dproj, p["win"], name=f"in_proj_dx_{l}", tb=True)
        dx, dg = _norm_bwd(dh, s["x"], p["g_pre"], dx1, F32, f"mix_pre_bwd_{l}")
        small_g["g_mix_pre"][l] = dg
    grad_x = dx.reshape(1, t, D_MODEL)

    st = lambda k: jnp.stack(big_g[k])
    send = [
        st(0).reshape(DEPTH, N_DEV, D_MODEL // N_DEV, D_IN).transpose(1, 0, 2, 3),
        st(1).reshape(DEPTH, Q_RANK, N_DEV, 384 // N_DEV).transpose(2, 0, 1, 3),
        st(2).reshape(DEPTH, KV_RANK, N_DEV, 512 // N_DEV).transpose(2, 0, 1, 3),
        st(3).reshape(DEPTH, N_DEV, D_MODEL // N_DEV, D_MODEL).transpose(1, 0, 2, 3),
        st(4).reshape(DEPTH, D_MODEL, N_DEV, D_FF // N_DEV).transpose(2, 0, 1, 3),
        st(5).reshape(DEPTH, N_DEV, D_FF // N_DEV, D_MODEL).transpose(1, 0, 2, 3),
    ]
    recv = _exchange([a.astype(BF16) for a in send], False, "grads_all_to_all")
    ms = [m_w_in, m_w_q_up, m_w_kv_up, m_w_out, m_w_ffn_up, m_w_ffn_down]
    vs = [v_w_in, v_w_q_up, v_w_kv_up, v_w_out, v_w_ffn_up, v_w_ffn_down]
    names = ["w_in", "w_q_up", "w_kv_up", "w_out", "w_ffn_up", "w_ffn_down"]
    res = {}
    for k, n in enumerate(names):
        res[n] = _reduce_adam(recv[k], big[k], ms[k], vs[k], f"adamw_{n}")

    small_w = dict(g_mix_pre=g_mix_pre, g_mix_post=g_mix_post, g_ffn_pre=g_ffn_pre, g_ffn_post=g_ffn_post,
                   g_mix_out=g_mix_out, g_q_lora=g_q_lora, g_kv_lora=g_kv_lora, b_forget=b_forget)
    small_m = dict(g_mix_pre=m_g_mix_pre, g_mix_post=m_g_mix_post, g_ffn_pre=m_g_ffn_pre, g_ffn_post=m_g_ffn_post,
                   g_mix_out=m_g_mix_out, g_q_lora=m_g_q_lora, g_kv_lora=m_g_kv_lora, b_forget=m_b_forget)
    small_v = dict(g_mix_pre=v_g_mix_pre, g_mix_post=v_g_mix_post, g_ffn_pre=v_g_ffn_pre, g_ffn_post=v_g_ffn_post,
                   g_mix_out=v_g_mix_out, g_q_lora=v_g_q_lora, g_kv_lora=v_g_kv_lora, b_forget=v_b_forget)
    part = _pack_small({n: jnp.concatenate(small_g[n], axis=0) for n, _ in _SMALL})
    sres = _small_allreduce_adam(part, _pack_small(small_w), _pack_small(small_m), _pack_small(small_v),
                                 "small_allreduce_adamw")
    sg, sd, sm, sv = [_unpack_small(a) for a in sres]
    for n, _ in _SMALL:
        res[n] = [sg[n], sd[n], sm[n], sv[n]]

    order = ["g_mix_pre", "w_in", "b_forget", "g_q_lora", "w_q_up", "g_kv_lora", "w_kv_up", "g_mix_out", "w_out",
             "g_mix_post", "g_ffn_pre", "w_ffn_up", "w_ffn_down", "g_ffn_post"]
    outs = [loss, grad_x]
    for idx in range(4):
        outs += [res[n][idx] for n in order]
    return tuple(outs)
```

```python
import functools
import math

import numpy as np
import jax
import jax.numpy as jnp
from jax import lax
from jax.experimental import pallas as pl
from jax.experimental.pallas import tpu as pltpu

F32 = jnp.float32
BF16 = jnp.bfloat16

D_MODEL = 1024
DEPTH = 2
N_DEV = 8
GROUP = 256
HEADS = 4
HEAD_DIM = 64
LANES = 128
HP = HEADS * LANES
QKV = 3 * HP
Q_RANK = 256
KV_RANK = 128
ROPE_MLA = 32
D_FF = 4096
D_IN = 2980
CHUNK_SHIFT = 6
EPS = 1e-6
ROPE_BASE = 10000.0
NEG = -1e30

OFF_FOX, OFF_RET, OFF_SB = 0, QKV, 2 * QKV
OFF_RG = 3 * QKV
OFF_CQ = OFF_RG + HP
OFF_CKV = OFF_CQ + Q_RANK
OFF_KR = OFF_CKV + LANES
OFF_FF = OFF_KR + LANES
NP_IN = 6144

BQ = 256
TKS = 128
TM = 256
VMEM_LIMIT = 48 * 1024 * 1024

ADAM_LR, ADAM_B1, ADAM_B2, ADAM_EPS, ADAM_WD, ADAM_STEP = 0.001, 0.9, 0.999, 1e-08, 0.01, 10
ADAM_C1 = 1.0 - ADAM_B1 ** ADAM_STEP
ADAM_C2 = 1.0 - ADAM_B2 ** ADAM_STEP

SMALL_ROWS = 44

NT = (((1,), (1,)), ((), ()))
TN = (((0,), (0,)), ((), ()))


def _cp(*sem):
    return pltpu.CompilerParams(dimension_semantics=sem if sem else None, vmem_limit_bytes=VMEM_LIMIT)


def _bdot(a, b, dn=None):
    if dn is None:
        return jnp.dot(a, b, preferred_element_type=F32)
    return lax.dot_general(a, b, dn, preferred_element_type=F32)


def _split2(x):
    hi = x.astype(BF16)
    lo = (x - hi.astype(F32)).astype(BF16)
    return hi, lo


def _mm(a, b, *, name, ta=False, tb=False, out_dtype=F32, a_fn=None, epi=None, epi_in=None,
        tm=1024, tn=1024, tk=1024):
    m, k = (a.shape[1], a.shape[0]) if ta else a.shape
    n = b.shape[0] if tb else b.shape[1]
    tm, tn, tk = min(tm, m), min(tn, n), min(tk, k)
    assert m % tm == 0 and n % tn == 0 and k % tk == 0, (name, m, n, k)
    nk = k // tk
    dn = (((0 if ta else 1,), (1 if tb else 0,)), ((), ()))

    def body(*refs):
        if epi is None:
            a_ref, b_ref, o_ref = refs[:3]
            e_ref = None
            rest = refs[3:]
        else:
            a_ref, b_ref, e_ref, o_ref = refs[:4]
            rest = refs[4:]
        av = a_ref[...]
        if a_fn is not None:
            av = a_fn(av)
        part = lax.dot_general(av.astype(BF16), b_ref[...].astype(BF16), dn, preferred_element_type=F32)

        def finish(r):
            if epi is not None:
                r = epi(r, e_ref[...])
            o_ref[...] = r.astype(out_dtype)

        if nk == 1:
            finish(part)
        else:
            acc_ref = rest[0]
            kk = pl.program_id(2)

            @pl.when(kk == 0)
            def _():
                acc_ref[...] = part

            @pl.when(kk > 0)
            def _():
                acc_ref[...] += part

            @pl.when(kk == nk - 1)
            def _():
                finish(acc_ref[...])

    a_spec = pl.BlockSpec((tk, tm), lambda i, j, kk: (kk, i)) if ta else pl.BlockSpec((tm, tk), lambda i, j, kk: (i, kk))
    b_spec = pl.BlockSpec((tn, tk), lambda i, j, kk: (j, kk)) if tb else pl.BlockSpec((tk, tn), lambda i, j, kk: (kk, j))
    o_spec = pl.BlockSpec((tm, tn), lambda i, j, kk: (i, j))
    in_specs = [a_spec, b_spec]
    args = [a, b]
    if epi is not None:
        in_specs.append(o_spec)
        args.append(epi_in)
    return pl.pallas_call(
        body, name=name, grid=(m // tm, n // tn, nk),
        in_specs=in_specs, out_specs=o_spec,
        out_shape=jax.ShapeDtypeStruct((m, n), out_dtype),
        scratch_shapes=[pltpu.VMEM((tm, tn), F32)] if nk > 1 else [],
        compiler_params=_cp("parallel", "parallel", "arbitrary"),
    )(*args)


def _relu2(v):
    r = jnp.maximum(v, 0.0)
    return r * r


def _drelu2(du, av):
    return du * (2.0 * jnp.maximum(av, 0.0))


def _rms(v, g):
    r = lax.rsqrt(jnp.mean(v * v, axis=-1, keepdims=True) + EPS)
    return v * r * g


def _row_spec(w):
    return pl.BlockSpec((TM, w), lambda i: (i, 0))


def _vec_spec(w):
    return pl.BlockSpec((1, w), lambda i: (0, 0))


def _rms_fwd(x, g, name):
    t, d = x.shape

    def body(x_ref, g_ref, h_ref):
        h_ref[...] = _rms(x_ref[...], g_ref[...]).astype(BF16)

    return pl.pallas_call(
        body, name=name, grid=(t // TM,), in_specs=[_row_spec(d), _vec_spec(d)], out_specs=_row_spec(d),
        out_shape=jax.ShapeDtypeStruct((t, d), BF16), compiler_params=_cp("parallel"))(x, g)


def _add_rms_fwd(x, y, g1, g2, name):
    t, d = x.shape

    def body(x_ref, y_ref, g1_ref, g2_ref, xn_ref, h_ref):
        xn = x_ref[...] + _rms(y_ref[...], g1_ref[...])
        xn_ref[...] = xn
        h_ref[...] = _rms(xn, g2_ref[...]).astype(BF16)

    return pl.pallas_call(
        body, name=name, grid=(t // TM,),
        in_specs=[_row_spec(d), _row_spec(d), _vec_spec(d), _vec_spec(d)],
        out_specs=[_row_spec(d), _row_spec(d)],
        out_shape=[jax.ShapeDtypeStruct((t, d), F32), jax.ShapeDtypeStruct((t, d), BF16)],
        compiler_params=_cp("parallel"))(x, y, g1, g2)


def _final_loss(x, y, g, tgt, name):
    t, d = x.shape

    def body(x_ref, y_ref, g_ref, t_ref, l_ref, dx_ref):
        @pl.when(pl.program_id(0) == 0)
        def _():
            l_ref[...] = jnp.zeros_like(l_ref)

        err = x_ref[...] + _rms(y_ref[...], g_ref[...]) - t_ref[...]
        dx_ref[...] = err * (1.0 / d)
        l_ref[...] += jnp.sum(jnp.sum(err * err, axis=1, keepdims=True), axis=0, keepdims=True) * (0.5 / d)

    return pl.pallas_call(
        body, name=name, grid=(t // TM,),
        in_specs=[_row_spec(d), _row_spec(d), _vec_spec(d), _row_spec(d)],
        out_specs=[pl.BlockSpec((1, LANES), lambda i: (0, 0)), _row_spec(d)],
        out_shape=[jax.ShapeDtypeStruct((1, LANES), F32), jax.ShapeDtypeStruct((t, d), F32)],
        compiler_params=_cp("arbitrary"))(x, y, g, tgt)


def _rms_bwd_vals(dn, v, g):
    w = v.shape[-1]
    r = lax.rsqrt(jnp.mean(v * v, axis=-1, keepdims=True) + EPS)
    vh = v * r
    dgp = jnp.sum(dn * vh, axis=0, keepdims=True)
    dvh = dn * g
    dv = r * (dvh - vh * (jnp.sum(dvh * vh, axis=-1, keepdims=True) * (1.0 / w)))
    return dv, dgp


def _norm_bwd(dn, v, g, resid, out_dtype, name):
    t, d = v.shape
    has_res = resid is not None

    def body(*refs):
        if has_res:
            dn_ref, v_ref, g_ref, r_ref, dv_ref, dg_ref = refs
        else:
            dn_ref, v_ref, g_ref, dv_ref, dg_ref = refs

        @pl.when(pl.program_id(0) == 0)
        def _():
            dg_ref[...] = jnp.zeros_like(dg_ref)

        dv, dgp = _rms_bwd_vals(dn_ref[...].astype(F32), v_ref[...], g_ref[...])
        if has_res:
            dv = dv + r_ref[...]
        dv_ref[...] = dv.astype(out_dtype)
        dg_ref[...] += dgp

    in_specs = [_row_spec(d), _row_spec(d), _vec_spec(d)] + ([_row_spec(d)] if has_res else [])
    args = [dn, v, g] + ([resid] if has_res else [])
    return pl.pallas_call(
        body, name=name, grid=(t // TM,), in_specs=in_specs,
        out_specs=[_row_spec(d), _vec_spec(d)],
        out_shape=[jax.ShapeDtypeStruct((t, d), out_dtype), jax.ShapeDtypeStruct((1, d), F32)],
        compiler_params=_cp("arbitrary"))(*args)


def _rope_tables(pos, invf, lo, half):
    w = invf.shape[-1]
    lane = lax.broadcasted_iota(jnp.int32, (pos.shape[0], w), 1) & (LANES - 1)
    first = (lane >= lo) & (lane < lo + half)
    second = (lane >= lo + half) & (lane < lo + 2 * half)
    ang = pos * invf
    active = first | second
    cos = jnp.where(active, jnp.cos(ang), 1.0)
    sin = jnp.where(active, jnp.sin(ang), 0.0)
    return cos, sin, first, second


def _rope_apply(v, cos, sin, first, second, half, sign):
    w = v.shape[-1]
    up = pltpu.roll(v, w - half, 1)
    dn = pltpu.roll(v, half, 1)
    rot = jnp.where(first, -up, jnp.where(second, dn, 0.0))
    return v * cos + rot * (sin * sign)


def _forget_lsf(proj, b_pad, name):
    t = proj.shape[0]

    def body(ff_ref, b_ref, o_ref):
        f = ff_ref[...] + b_ref[...]
        o_ref[...] = -(jnp.maximum(-f, 0.0) + jnp.log(1.0 + jnp.exp(-jnp.abs(f))))

    return pl.pallas_call(
        body, name=name, grid=(t // TM,),
        in_specs=[pl.BlockSpec((TM, LANES), lambda i: (i, OFF_FF // LANES)), _vec_spec(LANES)],
        out_specs=_row_spec(LANES), out_shape=jax.ShapeDtypeStruct((t, LANES), F32),
        compiler_params=_cp("parallel"))(proj, b_pad)


def _split3(c):
    hi = c.astype(BF16).astype(F32)
    mid = (c - hi).astype(BF16).astype(F32)
    return hi, mid, (c - hi) - mid


def _transposed_tiles(dst, row0, blocks, width):
    for b, blk in enumerate(blocks):
        bt = blk.T.astype(BF16)
        rows = bt.shape[0]
        for w in range(TM // width):
            dst[w, row0 + b * rows:row0 + (b + 1) * rows, :] = bt[:, w * width:(w + 1) * width]


def _prep_fwd(proj, cum, pos, invf_ret, invf_mla, lg_lanes, g_q, g_kv, wq_pad, wkv_pad, name):
    t = proj.shape[0]

    def body(fox_ref, ret_ref, sb_ref, cq_ref, ckv_ref, kr_ref, cum_ref, pos_ref, ifr_ref, ifm_ref, lg_ref,
             gq_ref, gkv_ref, wq_ref, wkv_ref,
             ofox_ref, oret_ref, osb_ref, omla_ref, ofoxt_ref, osbt_ref, omlat_ref, orett_ref):
        pos_v = pos_ref[...]
        osb_ref[:, 0:HP] = (sb_ref[:, 0:HP] * 0.125).astype(BF16)
        osb_ref[:, HP:QKV] = sb_ref[:, HP:QKV].astype(BF16)
        _transposed_tiles(osbt_ref, 0, [sb_ref[:, HP:2 * HP], sb_ref[:, 2 * HP:QKV]], TKS)
        lane = lax.broadcasted_iota(jnp.int32, (TM, LANES), 1)
        cumv = cum_ref[...]
        fq, fk = [], []
        for hb in range(HEADS):
            hi, mid, lo = _split3(cumv[:, hb:hb + 1])
            q = fox_ref[:, hb * LANES:(hb + 1) * LANES] * 0.125
            k = fox_ref[:, HP + hb * LANES:HP + (hb + 1) * LANES]
            ones_q = (lane >= HEAD_DIM) & (lane < HEAD_DIM + 3)
            ones_k = (lane >= HEAD_DIM + 3) & (lane < HEAD_DIM + 6)
            q = jnp.where(ones_q, 1.0, q)
            k = jnp.where(ones_k, 1.0, k)
            for n, part in enumerate((hi, mid, lo)):
                q = jnp.where(lane == HEAD_DIM + 3 + n, part, q)
                k = jnp.where(lane == HEAD_DIM + n, -part, k)
            fq.append(q)
            fk.append(k)
        fk = jnp.concatenate(fk, axis=1)
        ofox_ref[:, 0:HP] = jnp.concatenate(fq, axis=1).astype(BF16)
        ofox_ref[:, HP:2 * HP] = fk.astype(BF16)
        ofox_ref[:, 2 * HP:QKV] = fox_ref[:, 2 * HP:QKV].astype(BF16)
        _transposed_tiles(ofoxt_ref, 0, [fk, fox_ref[:, 2 * HP:QKV]], BQ)
        cos, sin, first, second = _rope_tables(pos_v, ifr_ref[...], 0, HEAD_DIM // 2)
        nloc = lax.broadcasted_iota(jnp.int32, (TM, 1), 0).astype(F32)
        dec = lg_ref[...] * nloc
        rq = _rope_apply(ret_ref[:, 0:HP], cos, sin, first, second, HEAD_DIM // 2, 1.0)
        rk = _rope_apply(ret_ref[:, HP:2 * HP], cos, sin, first, second, HEAD_DIM // 2, 1.0)
        oret_ref[:, 0:HP] = (rq * jnp.exp(dec)).astype(BF16)
        rk = rk * 0.125 * jnp.exp(-dec)
        oret_ref[:, HP:2 * HP] = rk.astype(BF16)
        _transposed_tiles(orett_ref, 0, [rk], BQ)
        oret_ref[:, 2 * HP:QKV] = ret_ref[:, 2 * HP:QKV].astype(BF16)
        cosm, sinm, firstm, secondm = _rope_tables(pos_v, ifm_ref[...], HEAD_DIM, ROPE_MLA // 2)
        cqn = _rms(cq_ref[...], gq_ref[...]).astype(BF16)
        qm = _bdot(cqn, wq_ref[...])
        omla_ref[:, 0:HP] = _rope_apply(qm, cosm, sinm, firstm, secondm, ROPE_MLA // 2, 1.0).astype(BF16)
        ckvn = _rms(ckv_ref[...], gkv_ref[...]).astype(BF16)
        kv = _bdot(ckvn, wkv_ref[...])
        krr = _rope_apply(kr_ref[...], cosm[:, 0:LANES], sinm[:, 0:LANES], firstm[:, 0:LANES],
                          secondm[:, 0:LANES], ROPE_MLA // 2, 1.0)
        mk = kv[:, 0:HP] + jnp.concatenate([krr] * HEADS, axis=1)
        omla_ref[:, HP:2 * HP] = mk.astype(BF16)
        omla_ref[:, 2 * HP:QKV] = kv[:, HP:2 * HP].astype(BF16)
        _transposed_tiles(omlat_ref, 0, [mk, kv[:, HP:2 * HP]], BQ)

    def seg(off, w):
        return pl.BlockSpec((TM, w), lambda i, o=off // w: (i, o))

    def full(shape):
        return pl.BlockSpec(shape, lambda i: (0,) * len(shape))

    def tiles(width):
        return pl.BlockSpec((TM // width, 2 * HP, width), lambda i: (i, 0, 0))

    in_specs = [seg(OFF_FOX, QKV), seg(OFF_RET, QKV), seg(OFF_SB, QKV), seg(OFF_CQ, Q_RANK), seg(OFF_CKV, LANES),
                seg(OFF_KR, LANES), _row_spec(LANES), pl.BlockSpec((TM, 1), lambda i: (i, 0)),
                full((1, HP)), full((1, HP)), full((1, HP)), full((1, Q_RANK)), full((1, KV_RANK)),
                full((Q_RANK, HP)), full((KV_RANK, 2 * HP))]
    out_specs = [_row_spec(QKV)] * 4 + [tiles(BQ), tiles(TKS), tiles(BQ),
                                        pl.BlockSpec((1, HP, BQ), lambda i: (i, 0, 0))]
    out_shape = [jax.ShapeDtypeStruct((t, QKV), BF16)] * 4 + [
        jax.ShapeDtypeStruct((t // BQ, 2 * HP, BQ), BF16), jax.ShapeDtypeStruct((t // TKS, 2 * HP, TKS), BF16),
        jax.ShapeDtypeStruct((t // BQ, 2 * HP, BQ), BF16), jax.ShapeDtypeStruct((t // BQ, HP, BQ), BF16)]
    return pl.pallas_call(
        body, name=name, grid=(t // TM,), in_specs=in_specs, out_specs=out_specs, out_shape=out_shape,
        compiler_params=_cp("parallel"))(proj, proj, proj, proj, proj, proj, cum, pos, invf_ret, invf_mla, lg_lanes,
                                         g_q, g_kv, wq_pad, wkv_pad)


def _prep_bwd(dfox, dret, dsb, dmla, drg, dlsf, proj, pos, invf_ret, invf_mla, lg_lanes, b_pad, g_q, g_kv,
              wq_pad, wkv_pad, name):
    t = proj.shape[0]

    def body(dfq, dfk, dfv, drq, drk, drv, dsq, dsk, dsv, dmq, dmk, dmv, drg_ref, dlsf_ref,
             cq_ref, ckv_ref, ff_ref, pos_ref, ifr_ref, ifm_ref, lg_ref, b_ref, gq_ref, gkv_ref, wq_ref, wkv_ref,
             dp_ref, dwq_ref, dwkv_ref, dgq_ref, dgkv_ref, dbf_ref):
        @pl.when(pl.program_id(0) == 0)
        def _():
            dwq_ref[...] = jnp.zeros_like(dwq_ref)
            dwkv_ref[...] = jnp.zeros_like(dwkv_ref)
            dgq_ref[...] = jnp.zeros_like(dgq_ref)
            dgkv_ref[...] = jnp.zeros_like(dgkv_ref)
            dbf_ref[...] = jnp.zeros_like(dbf_ref)

        pos_v = pos_ref[...]
        for off, (dq, dk, dv) in ((OFF_FOX, (dfq, dfk, dfv)), (OFF_SB, (dsq, dsk, dsv))):
            dp_ref[:, off:off + HP] = (dq[...] * 0.125).astype(BF16)
            dp_ref[:, off + HP:off + 2 * HP] = dk[...].astype(BF16)
            dp_ref[:, off + 2 * HP:off + QKV] = dv[...].astype(BF16)
        cos, sin, first, second = _rope_tables(pos_v, ifr_ref[...], 0, HEAD_DIM // 2)
        nloc = lax.broadcasted_iota(jnp.int32, (TM, 1), 0).astype(F32)
        dec = lg_ref[...] * nloc
        dq = _rope_apply(drq[...] * jnp.exp(dec), cos, sin, first, second, HEAD_DIM // 2, -1.0)
        dk = _rope_apply(drk[...] * (0.125 * jnp.exp(-dec)), cos, sin, first, second, HEAD_DIM // 2, -1.0)
        dp_ref[:, OFF_RET:OFF_RET + HP] = dq.astype(BF16)
        dp_ref[:, OFF_RET + HP:OFF_RET + 2 * HP] = dk.astype(BF16)
        dp_ref[:, OFF_RET + 2 * HP:OFF_RET + QKV] = drv[...].astype(BF16)
        dp_ref[:, OFF_RG:OFF_RG + HP] = drg_ref[...].astype(BF16)
        cosm, sinm, firstm, secondm = _rope_tables(pos_v, ifm_ref[...], HEAD_DIM, ROPE_MLA // 2)
        dql = _rope_apply(dmq[...], cosm, sinm, firstm, secondm, ROPE_MLA // 2, -1.0).astype(BF16)
        cq = cq_ref[...]
        cqn = _rms(cq, gq_ref[...]).astype(BF16)
        dwq_ref[...] += _bdot(cqn, dql, TN)
        dcqn = _bdot(dql, wq_ref[...], NT)
        dcq, dgq = _rms_bwd_vals(dcqn, cq, gq_ref[...])
        dgq_ref[...] += dgq
        dp_ref[:, OFF_CQ:OFF_CQ + Q_RANK] = dcq.astype(BF16)
        dkm = dmk[...]
        dkv = jnp.concatenate([dkm, dmv[...]], axis=1).astype(BF16)
        ckv = ckv_ref[...]
        ckvn = _rms(ckv, gkv_ref[...]).astype(BF16)
        dwkv_ref[...] += _bdot(ckvn, dkv, TN)
        dckvn = _bdot(dkv, wkv_ref[...], NT)
        dckv, dgkv = _rms_bwd_vals(dckvn, ckv, gkv_ref[...])
        dgkv_ref[...] += dgkv
        dp_ref[:, OFF_CKV:OFF_CKV + LANES] = dckv.astype(BF16)
        dkr = dkm[:, 0:LANES] + dkm[:, LANES:2 * LANES] + dkm[:, 2 * LANES:3 * LANES] + dkm[:, 3 * LANES:HP]
        act = firstm[:, 0:LANES] | secondm[:, 0:LANES]
        dkr = jnp.where(act, dkr, 0.0)
        dkr = _rope_apply(dkr, cosm[:, 0:LANES], sinm[:, 0:LANES], firstm[:, 0:LANES], secondm[:, 0:LANES],
                          ROPE_MLA // 2, -1.0)
        dp_ref[:, OFF_KR:OFF_KR + LANES] = dkr.astype(BF16)
        f = ff_ref[...] + b_ref[...]
        dff = dlsf_ref[...] / (1.0 + jnp.exp(f))
        dbf_ref[...] += jnp.sum(dff, axis=0, keepdims=True)
        dp_ref[:, OFF_FF:OFF_FF + LANES] = dff.astype(BF16)
        dp_ref[:, OFF_FF + LANES:NP_IN] = jnp.zeros((TM, NP_IN - OFF_FF - LANES), BF16)

    def seg(off, w):
        return pl.BlockSpec((TM, w), lambda i, o=off // w: (i, o))

    def full(shape):
        return pl.BlockSpec(shape, lambda i: (0,) * len(shape))

    hp_spec = _row_spec(HP)
    in_specs = [hp_spec] * 13 + [_row_spec(LANES), seg(OFF_CQ, Q_RANK), seg(OFF_CKV, LANES), seg(OFF_FF, LANES),
                                 pl.BlockSpec((TM, 1), lambda i: (i, 0)),
                                 full((1, HP)), full((1, HP)), full((1, HP)), full((1, LANES)), full((1, Q_RANK)),
                                 full((1, KV_RANK)), full((Q_RANK, HP)), full((KV_RANK, 2 * HP))]
    out_specs = [_row_spec(NP_IN), full((Q_RANK, HP)), full((KV_RANK, 2 * HP)), full((1, Q_RANK)),
                 full((1, KV_RANK)), full((1, LANES))]
    out_shape = [jax.ShapeDtypeStruct((t, NP_IN), BF16), jax.ShapeDtypeStruct((Q_RANK, HP), F32),
                 jax.ShapeDtypeStruct((KV_RANK, 2 * HP), F32), jax.ShapeDtypeStruct((1, Q_RANK), F32),
                 jax.ShapeDtypeStruct((1, KV_RANK), F32), jax.ShapeDtypeStruct((1, LANES), F32)]
    return pl.pallas_call(
        body, name=name, grid=(t // TM,), in_specs=in_specs, out_specs=out_specs, out_shape=out_shape,
        compiler_params=_cp("arbitrary"))(*dfox, *dret, *dsb, *dmla, drg, dlsf, proj, proj, proj, pos, invf_ret,
                                          invf_mla, lg_lanes, b_pad, g_q, g_kv, wq_pad, wkv_pad)


def _cumsum(x, reverse, name, partials=None):
    t, w = x.shape
    n = t // TM
    xs = [x] if partials is None else [x, partials]

    def body(*refs):
        x_refs, o_ref, carry = refs[:len(xs)], refs[len(xs)], refs[len(xs) + 1]

        @pl.when(pl.program_id(0) == 0)
        def _():
            carry[...] = jnp.zeros_like(carry)

        r = lax.broadcasted_iota(jnp.int32, (TM, TM), 0)
        c = lax.broadcasted_iota(jnp.int32, (TM, TM), 1)
        tri = jnp.where((r <= c) if reverse else (r >= c), 1.0, 0.0).astype(BF16)
        v = x_refs[0][...]
        if partials is not None:
            lane = lax.broadcasted_iota(jnp.int32, (TM, LANES), 1)
            for hb in range(HEADS):
                v = v + jnp.where(lane == hb, jnp.sum(x_refs[1][:, _hs(hb)], axis=1, keepdims=True), 0.0)
        hi = v.astype(BF16)
        r1 = v - hi.astype(F32)
        mid = r1.astype(BF16)
        lo = (r1 - mid.astype(F32)).astype(BF16)
        cs = _bdot(tri, hi) + _bdot(tri, mid) + _bdot(tri, lo) + carry[...]
        o_ref[...] = cs
        carry[...] = cs[0:1, :] if reverse else cs[TM - 1:TM, :]

    imap = (lambda i: (n - 1 - i, 0)) if reverse else (lambda i: (i, 0))
    return pl.pallas_call(
        body, name=name, grid=(n,), in_specs=[pl.BlockSpec((TM, a.shape[1]), imap) for a in xs],
        out_specs=pl.BlockSpec((TM, w), imap),
        out_shape=jax.ShapeDtypeStruct((t, w), F32), scratch_shapes=[pltpu.VMEM((1, w), F32)],
        compiler_params=_cp("arbitrary"))(*xs)


HB_FWD = 4
HB_BWD = 2


def _q_spec(hb):
    return pl.BlockSpec((BQ, hb * LANES), lambda g, i: (i, g))


def _kv_spec(t, which, hb):
    return pl.BlockSpec((t, hb * LANES), lambda g, i, w=which: (0, w * (HEADS // hb) + g))


def _acc_spec(t, hb):
    return pl.BlockSpec((t, hb * LANES), lambda g, i: (0, g))


def _hs(hh):
    return slice(hh * LANES, (hh + 1) * LANES)


def _tile_iota(rows, cols):
    return (lax.broadcasted_iota(jnp.int32, (rows, cols), 0), lax.broadcasted_iota(jnp.int32, (rows, cols), 1))


def _kvt_spec(nkv, width, which, hb):
    return pl.BlockSpec((nkv, hb * LANES, width), lambda g, i, w=which: (0, w * (HEADS // hb) + g, 0))


def _qrow_spec(hb):
    return pl.BlockSpec((hb, 1, 1, BQ), lambda g, i: (g, i, 0, 0))


def _vis_t(chunk_mask):
    r, c = _tile_iota(BQ, BQ)
    return ((r >> CHUNK_SHIFT) <= (c >> CHUNK_SHIFT)) if chunk_mask else (r <= c)


def _softmax_fwd(qkv, kvt, *, chunk_mask, scale, name):
    t = qkv.shape[0]
    nq = t // BQ
    hb = HB_FWD

    def body(q_ref, k_ref, vt_ref, o_ref, lse_ref, m_sc, l_sc, acc_sc):
        i = pl.program_id(1)
        m_sc[...] = jnp.full((hb, 1, BQ), NEG, F32)
        l_sc[...] = jnp.zeros((hb, 1, BQ), F32)
        acc_sc[...] = jnp.zeros((hb, LANES, BQ), F32)

        def tile(j, masked):
            off = pl.multiple_of(j * BQ, BQ)
            vis = _vis_t(chunk_mask) if masked else None
            ss = [_bdot(k_ref[pl.ds(off, BQ), _hs(hh)], q_ref[:, _hs(hh)], NT) for hh in range(hb)]
            ps, alphas = [], []
            for hh in range(hb):
                s = ss[hh]
                if scale != 1.0:
                    s = s * scale
                if masked:
                    s = jnp.where(vis, s, NEG)
                m_old = m_sc[hh]
                m_new = jnp.maximum(m_old, jnp.max(s, axis=0, keepdims=True))
                alpha = jnp.exp(m_old - m_new)
                p = jnp.exp(s - m_new)
                l_sc[hh] = alpha * l_sc[hh] + jnp.sum(p, axis=0, keepdims=True)
                m_sc[hh] = m_new
                ps.append(p.astype(BF16))
                alphas.append(alpha)
            for hh in range(hb):
                acc_sc[hh] = alphas[hh] * acc_sc[hh] + _bdot(vt_ref[j, _hs(hh), :], ps[hh])

        def loop(j, carry):
            tile(j, False)
            return carry

        lax.fori_loop(0, i, loop, 0)
        tile(i, True)
        for hh in range(hb):
            l = l_sc[hh]
            o_ref[:, _hs(hh)] = (acc_sc[hh] / l).T
            lse_ref[hh, 0] = m_sc[hh] + jnp.log(l)

    return pl.pallas_call(
        body, name=name, grid=(HEADS // hb, nq),
        in_specs=[_q_spec(hb), _kv_spec(t, 1, hb), _kvt_spec(nq, BQ, 1, hb)],
        out_specs=[_q_spec(hb), _qrow_spec(hb)],
        out_shape=[jax.ShapeDtypeStruct((t, HP), F32), jax.ShapeDtypeStruct((HEADS, nq, 1, BQ), F32)],
        scratch_shapes=[pltpu.VMEM((hb, 1, BQ), F32), pltpu.VMEM((hb, 1, BQ), F32), pltpu.VMEM((hb, LANES, BQ), F32)],
        compiler_params=_cp("parallel", "arbitrary"))(qkv, qkv, kvt)


def _softmax_bwd(qkv, kvt, do, lse, delta, *, bias, chunk_mask, scale, name):
    t = qkv.shape[0]
    nq = t // BQ
    hb = HB_BWD

    def body(*refs):
        if bias:
            (q_ref, k_ref, v_ref, kt_ref, do_ref, lse_ref, dl_ref, dq_ref, dk_ref, dv_ref, dck_ref, dcq_ref,
             dq_sc, dcq_sc) = refs
            dcq_sc[...] = jnp.zeros((hb, 1, BQ), F32)
        else:
            q_ref, k_ref, v_ref, kt_ref, do_ref, lse_ref, dl_ref, dq_ref, dk_ref, dv_ref, dq_sc = refs
        i = pl.program_id(1)

        @pl.when(i == 0)
        def _():
            dk_ref[...] = jnp.zeros_like(dk_ref)
            dv_ref[...] = jnp.zeros_like(dv_ref)
            if bias:
                dck_ref[...] = jnp.zeros_like(dck_ref)

        dq_sc[...] = jnp.zeros((hb, LANES, BQ), F32)

        def tile(j, masked):
            off = pl.multiple_of(j * BQ, BQ)
            vis = _vis_t(chunk_mask) if masked else None
            qs = [q_ref[:, _hs(hh)] for hh in range(hb)]
            dobs = [do_ref[:, _hs(hh)].astype(BF16) for hh in range(hb)]
            ss = [_bdot(k_ref[pl.ds(off, BQ), _hs(hh)], qs[hh], NT) for hh in range(hb)]
            dps = [_bdot(v_ref[pl.ds(off, BQ), _hs(hh)], dobs[hh], NT) for hh in range(hb)]
            pbs, dsbs = [], []
            for hh in range(hb):
                s = ss[hh]
                if scale != 1.0:
                    s = s * scale
                p = jnp.exp(s - lse_ref[hh, 0])
                if masked:
                    p = jnp.where(vis, p, 0.0)
                ds = p * (dps[hh] - dl_ref[hh, 0])
                if bias:
                    dck_ref[pl.ds(off, BQ), _hs(hh)] -= ds[:, 0:LANES] + ds[:, LANES:BQ]
                    dcq_sc[hh] += jnp.sum(ds, axis=0, keepdims=True)
                if scale != 1.0:
                    ds = ds * scale
                pbs.append(p.astype(BF16))
                dsbs.append(ds.astype(BF16))
            for hh in range(hb):
                sl = _hs(hh)
                dv_ref[pl.ds(off, BQ), sl] += _bdot(pbs[hh], dobs[hh])
                dk_ref[pl.ds(off, BQ), sl] += _bdot(dsbs[hh], qs[hh])
                dq_sc[hh] += _bdot(kt_ref[j, sl, :], dsbs[hh])

        def loop(j, carry):
            tile(j, False)
            return carry

        lax.fori_loop(0, i, loop, 0)
        tile(i, True)
        for hh in range(hb):
            dq_ref[:, _hs(hh)] = dq_sc[hh].T
            if bias:
                dcq_ref[hh, 0] = dcq_sc[hh]

    in_specs = [_q_spec(hb), _kv_spec(t, 1, hb), _kv_spec(t, 2, hb), _kvt_spec(nq, BQ, 0, hb), _q_spec(hb),
                _qrow_spec(hb), _qrow_spec(hb)]
    out_specs = [_q_spec(hb), _acc_spec(t, hb), _acc_spec(t, hb)]
    out_shape = [jax.ShapeDtypeStruct((t, HP), F32)] * 3
    scratch = [pltpu.VMEM((hb, LANES, BQ), F32)]
    if bias:
        out_specs += [_acc_spec(t, hb), _qrow_spec(hb)]
        out_shape += [jax.ShapeDtypeStruct((t, HP), F32), jax.ShapeDtypeStruct((HEADS, nq, 1, BQ), F32)]
        scratch.append(pltpu.VMEM((hb, 1, BQ), F32))
    return pl.pallas_call(
        body, name=name, grid=(HEADS // hb, nq), in_specs=in_specs, out_specs=out_specs, out_shape=out_shape,
        scratch_shapes=scratch,
        compiler_params=_cp("parallel", "arbitrary"))(qkv, qkv, qkv, kvt, do, lse, delta)


def _ret_diag_decay(lg1, keys_on_rows=False):
    r, c = _tile_iota(BQ, BQ)
    qn, km = (c, r) if keys_on_rows else (r, c)
    dd = jnp.where(km > qn, jnp.exp((2.0 * lg1) * (km - qn).astype(F32)), 1.0)
    return jnp.where((km >> CHUNK_SHIFT) <= (qn >> CHUNK_SHIFT), dd, 0.0)


def _lg_spec(hb):
    return pl.BlockSpec((hb, 1, LANES), lambda g, i: (g, 0, 0))


def _ret_fwd(qkv, lg_heads, name):
    t = qkv.shape[0]
    nq = t // BQ
    hb = HB_FWD

    def body(lg_ref, q_ref, k_ref, v_ref, o_ref, acc_sc):
        i = pl.program_id(1)
        acc_sc[...] = jnp.zeros((hb, BQ, LANES), F32)

        def loop(j, carry):
            off = pl.multiple_of(j * BQ, BQ)
            span = ((i - j) * BQ).astype(F32)
            aa = [_bdot(q_ref[:, _hs(hh)], k_ref[pl.ds(off, BQ), _hs(hh)], NT).astype(BF16) for hh in range(hb)]
            for hh in range(hb):
                cf = jnp.exp(lg_ref[hh][:, 0:1] * span)
                acc_sc[hh] += cf * _bdot(aa[hh], v_ref[pl.ds(off, BQ), _hs(hh)])
            return carry

        lax.fori_loop(0, i, loop, 0)
        off = pl.multiple_of(i * BQ, BQ)
        for hh in range(hb):
            sl = _hs(hh)
            a = _bdot(q_ref[:, sl], k_ref[pl.ds(off, BQ), sl], NT) * _ret_diag_decay(lg_ref[hh][:, 0:1])
            o_ref[:, sl] = acc_sc[hh] + _bdot(a.astype(BF16), v_ref[pl.ds(off, BQ), sl])

    return pl.pallas_call(
        body, name=name, grid=(HEADS // hb, nq),
        in_specs=[_lg_spec(hb), _q_spec(hb), _kv_spec(t, 1, hb), _kv_spec(t, 2, hb)],
        out_specs=_q_spec(hb), out_shape=jax.ShapeDtypeStruct((t, HP), F32),
        scratch_shapes=[pltpu.VMEM((hb, BQ, LANES), F32)],
        compiler_params=_cp("parallel", "arbitrary"))(lg_heads, qkv, qkv, qkv)


def _ret_bwd(qkv, kt, lg_heads, do, name):
    t = qkv.shape[0]
    nq = t // BQ
    hb = HB_BWD

    def body(lg_ref, q_ref, k_ref, v_ref, kt_ref, do_ref, dq_ref, dk_ref, dv_ref, dq_sc):
        i = pl.program_id(1)

        @pl.when(i == 0)
        def _():
            dk_ref[...] = jnp.zeros_like(dk_ref)
            dv_ref[...] = jnp.zeros_like(dv_ref)

        dq_sc[...] = jnp.zeros((hb, LANES, BQ), F32)

        def tile(j, diag):
            off = pl.multiple_of(j * BQ, BQ)
            qs = [q_ref[:, _hs(hh)] for hh in range(hb)]
            dobs = [do_ref[:, _hs(hh)].astype(BF16) for hh in range(hb)]
            aa = [_bdot(k_ref[pl.ds(off, BQ), _hs(hh)], qs[hh], NT) for hh in range(hb)]
            das = [_bdot(v_ref[pl.ds(off, BQ), _hs(hh)], dobs[hh], NT) for hh in range(hb)]
            abs_, dabs, cfs = [], [], []
            for hh in range(hb):
                lg1 = lg_ref[hh][:, 0:1]
                if diag:
                    dd = _ret_diag_decay(lg1, keys_on_rows=True)
                    abs_.append((aa[hh] * dd).astype(BF16))
                    dabs.append((das[hh] * dd).astype(BF16))
                    cfs.append(1.0)
                else:
                    abs_.append(aa[hh].astype(BF16))
                    dabs.append(das[hh].astype(BF16))
                    cfs.append(jnp.exp(lg1 * ((i - j) * BQ).astype(F32)))
            for hh in range(hb):
                sl = _hs(hh)
                dv_ref[pl.ds(off, BQ), sl] += cfs[hh] * _bdot(abs_[hh], dobs[hh])
                dk_ref[pl.ds(off, BQ), sl] += cfs[hh] * _bdot(dabs[hh], qs[hh])
                dq_sc[hh] += cfs[hh] * _bdot(kt_ref[j, sl, :], dabs[hh])

        def loop(j, carry):
            tile(j, False)
            return carry

        lax.fori_loop(0, i, loop, 0)
        tile(i, True)
        for hh in range(hb):
            dq_ref[:, _hs(hh)] = dq_sc[hh].T

    return pl.pallas_call(
        body, name=name, grid=(HEADS // hb, nq),
        in_specs=[_lg_spec(hb), _q_spec(hb), _kv_spec(t, 1, hb), _kv_spec(t, 2, hb), _kvt_spec(nq, BQ, 0, hb),
                  _q_spec(hb)],
        out_specs=[_q_spec(hb), _acc_spec(t, hb), _acc_spec(t, hb)],
        out_shape=[jax.ShapeDtypeStruct((t, HP), F32)] * 3,
        scratch_shapes=[pltpu.VMEM((hb, LANES, BQ), F32)],
        compiler_params=_cp("parallel", "arbitrary"))(lg_heads, qkv, qkv, qkv, kt, do)


def _sb_tile_logs(q, kb, vis):
    z = _bdot(kb, q, NT)
    ls = -(jnp.maximum(z, 0.0) + jnp.log(1.0 + jnp.exp(-jnp.abs(z))))
    if vis is not None:
        ls = jnp.where(vis, ls, 0.0)
    return z, ls


def _sb_vis(i, j):
    r, c = _tile_iota(TKS, BQ)
    return (j * TKS + r) < (i * BQ + c)


def _sb_later(ls, after):
    hi, lo = _split2(ls)
    return _bdot(after, hi) + _bdot(after, lo)


def _sb_fwd(qkv, kvt, name):
    t = qkv.shape[0]
    nq = t // BQ
    per = BQ // TKS
    hb = HB_FWD

    def body(q_ref, k_ref, vt_ref, o_ref, tot_ref, acc_sc, r_sc):
        i = pl.program_id(1)
        acc_sc[...] = jnp.zeros((hb, LANES, BQ), F32)
        r_sc[...] = jnp.zeros((hb, 1, BQ), F32)
        mr, mc = _tile_iota(TKS, TKS)
        after = jnp.where(mc > mr, 1.0, 0.0).astype(BF16)

        def tile(j, masked):
            off = pl.multiple_of(j * TKS, TKS)
            vis = _sb_vis(i, j) if masked else None
            zl = [_sb_tile_logs(q_ref[:, _hs(hh)], k_ref[pl.ds(off, TKS), _hs(hh)], vis) for hh in range(hb)]
            laters = [_sb_later(zl[hh][1], after) for hh in range(hb)]
            ws = []
            for hh in range(hb):
                z, ls = zl[hh]
                w = jnp.exp(z + ls + laters[hh] + r_sc[hh])
                if masked:
                    w = jnp.where(vis, w, 0.0)
                ws.append(w.astype(BF16))
                r_sc[hh] += jnp.sum(ls, axis=0, keepdims=True)
            for hh in range(hb):
                acc_sc[hh] += _bdot(vt_ref[j, _hs(hh), :], ws[hh])

        for d in range(per):
            tile(per * i + (per - 1 - d), True)

        def loop(jj, carry):
            tile(per * i - 1 - jj, False)
            return carry

        lax.fori_loop(0, per * i, loop, 0)
        for hh in range(hb):
            o_ref[:, _hs(hh)] = acc_sc[hh].T
            tot_ref[hh, 0] = r_sc[hh]

    return pl.pallas_call(
        body, name=name, grid=(HEADS // hb, nq),
        in_specs=[_q_spec(hb), _kv_spec(t, 1, hb), _kvt_spec(t // TKS, TKS, 1, hb)],
        out_specs=[_q_spec(hb), _qrow_spec(hb)],
        out_shape=[jax.ShapeDtypeStruct((t, HP), F32), jax.ShapeDtypeStruct((HEADS, nq, 1, BQ), F32)],
        scratch_shapes=[pltpu.VMEM((hb, LANES, BQ), F32), pltpu.VMEM((hb, 1, BQ), F32)],
        compiler_params=_cp("parallel", "arbitrary"))(qkv, qkv, kvt)


def _sb_bwd(qkv, kvt, do, tot, name):
    t = qkv.shape[0]
    nq = t // BQ
    per = BQ // TKS
    hb = HB_BWD

    def body(q_ref, k_ref, v_ref, kt_ref, do_ref, tot_ref, dq_ref, dk_ref, dv_ref, dq_sc, p_sc, g_sc):
        i = pl.program_id(1)

        @pl.when(i == 0)
        def _():
            dk_ref[...] = jnp.zeros_like(dk_ref)
            dv_ref[...] = jnp.zeros_like(dv_ref)

        dq_sc[...] = jnp.zeros((hb, LANES, BQ), F32)
        p_sc[...] = jnp.zeros((hb, 1, BQ), F32)
        g_sc[...] = jnp.zeros((hb, 1, BQ), F32)
        mr, mc = _tile_iota(TKS, TKS)
        after = jnp.where(mc > mr, 1.0, 0.0).astype(BF16)
        before = jnp.where(mc < mr, 1.0, 0.0).astype(BF16)

        def tile(j, masked):
            off = pl.multiple_of(j * TKS, TKS)
            vis = _sb_vis(i, j) if masked else None
            qs = [q_ref[:, _hs(hh)] for hh in range(hb)]
            dobs = [do_ref[:, _hs(hh)].astype(BF16) for hh in range(hb)]
            zl = [_sb_tile_logs(qs[hh], k_ref[pl.ds(off, TKS), _hs(hh)], vis) for hh in range(hb)]
            dws = [_bdot(v_ref[pl.ds(off, TKS), _hs(hh)], dobs[hh], NT) for hh in range(hb)]
            laters = [_sb_later(zl[hh][1], after) for hh in range(hb)]
            ws, gs = [], []
            for hh in range(hb):
                z, ls = zl[hh]
                own = jnp.sum(ls, axis=0, keepdims=True)
                rest = tot_ref[hh, 0] - p_sc[hh] - own
                w = jnp.exp(z + ls + laters[hh] + rest)
                if masked:
                    w = jnp.where(vis, w, 0.0)
                p_sc[hh] += own
                ws.append(w.astype(BF16))
                gs.append(dws[hh] * w)
            gins = []
            for hh in range(hb):
                ghi, glo = _split2(gs[hh])
                gins.append(_bdot(before, ghi) + _bdot(before, glo))
            dzbs = []
            for hh in range(hb):
                g = gs[hh]
                stay = jnp.exp(zl[hh][1])
                dz = g * stay - (1.0 - stay) * (gins[hh] + g_sc[hh])
                if masked:
                    dz = jnp.where(vis, dz, 0.0)
                g_sc[hh] += jnp.sum(g, axis=0, keepdims=True)
                dzbs.append(dz.astype(BF16))
            for hh in range(hb):
                sl = _hs(hh)
                dv_ref[pl.ds(off, TKS), sl] += _bdot(ws[hh], dobs[hh])
                dk_ref[pl.ds(off, TKS), sl] += _bdot(dzbs[hh], qs[hh])
                dq_sc[hh] += _bdot(kt_ref[j, sl, :], dzbs[hh])

        def loop(j, carry):
            tile(j, False)
            return carry

        lax.fori_loop(0, per * i, loop, 0)
        for d in range(per):
            tile(per * i + d, True)
        for hh in range(hb):
            dq_ref[:, _hs(hh)] = dq_sc[hh].T

    return pl.pallas_call(
        body, name=name, grid=(HEADS // hb, nq),
        in_specs=[_q_spec(hb), _kv_spec(t, 1, hb), _kv_spec(t, 2, hb), _kvt_spec(t // TKS, TKS, 0, hb), _q_spec(hb),
                  _qrow_spec(hb)],
        out_specs=[_q_spec(hb), _acc_spec(t, hb), _acc_spec(t, hb)],
        out_shape=[jax.ShapeDtypeStruct((t, HP), F32)] * 3,
        scratch_shapes=[pltpu.VMEM((hb, LANES, BQ), F32), pltpu.VMEM((hb, 1, BQ), F32), pltpu.VMEM((hb, 1, BQ), F32)],
        compiler_params=_cp("parallel", "arbitrary"))(qkv, qkv, qkv, kvt, do, tot)


def _sigmoid(v):
    return 1.0 / (1.0 + jnp.exp(-v))


def _post_fwd(oa, ob, oc, od, proj, g_pad, name):
    t = oa.shape[0]

    def body(oa_ref, ob_ref, oc_ref, od_ref, rg_ref, g_ref, mx_ref):
        g = g_ref[...]

        def group(o, gg):
            r = lax.rsqrt(jnp.sum(o * o, axis=-1, keepdims=True) * (1.0 / GROUP) + EPS)
            return (o * r * gg).astype(BF16)

        mx_ref[:, 0:HP] = group(oa_ref[...], g[:, 0:HP])
        mx_ref[:, HP:2 * HP] = group(ob_ref[...], g[:, HP:2 * HP])
        mx_ref[:, 3 * HP:4 * HP] = group(od_ref[...], g[:, 3 * HP:4 * HP])
        real = lax.broadcasted_iota(jnp.int32, (TM, LANES), 1) < HEAD_DIM
        for hb in range(HEADS):
            sl = slice(hb * LANES, (hb + 1) * LANES)
            o = oc_ref[:, sl]
            mu = jnp.sum(o, axis=-1, keepdims=True) * (1.0 / HEAD_DIM)
            dlt = jnp.where(real, o - mu, 0.0)
            var = jnp.sum(dlt * dlt, axis=-1, keepdims=True) * (1.0 / HEAD_DIM)
            yn = dlt * lax.rsqrt(var + EPS) * g[:, 2 * HP + hb * LANES:2 * HP + (hb + 1) * LANES]
            x = rg_ref[:, sl]
            mx_ref[:, 2 * HP + hb * LANES:2 * HP + (hb + 1) * LANES] = (yn * (x * _sigmoid(x))).astype(BF16)

    rg_spec = pl.BlockSpec((TM, HP), lambda i: (i, OFF_RG // HP))
    return pl.pallas_call(
        body, name=name, grid=(t // TM,),
        in_specs=[_row_spec(HP)] * 4 + [rg_spec, _vec_spec(4 * HP)], out_specs=_row_spec(4 * HP),
        out_shape=jax.ShapeDtypeStruct((t, 4 * HP), BF16), compiler_params=_cp("parallel"))(oa, ob, oc, od, proj, g_pad)


def _post_bwd(dmx, oa, ob, oc, od, proj, g_pad, name):
    t = oa.shape[0]

    def body(dm_ref, oa_ref, ob_ref, oc_ref, od_ref, rg_ref, g_ref,
             doa_ref, dob_ref, doc_ref, dod_ref, dla_ref, dlb_ref, drg_ref, dg_ref):
        @pl.when(pl.program_id(0) == 0)
        def _():
            dg_ref[...] = jnp.zeros_like(dg_ref)

        g = g_ref[...]

        def group_bwd(dm, o, gg):
            r = lax.rsqrt(jnp.sum(o * o, axis=-1, keepdims=True) * (1.0 / GROUP) + EPS)
            oh = o * r
            dgp = jnp.sum(dm * oh, axis=0, keepdims=True)
            dyh = dm * gg
            do = r * (dyh - oh * (jnp.sum(dyh * oh, axis=-1, keepdims=True) * (1.0 / GROUP)))
            return do, dgp

        def delta_bc(do, o):
            prod = do * o
            lane = lax.broadcasted_iota(jnp.int32, (TM, LANES), 1)
            out = jnp.zeros((TM, LANES), F32)
            for hb in range(HEADS):
                out = jnp.where(lane == hb, jnp.sum(prod[:, hb * LANES:(hb + 1) * LANES], axis=-1, keepdims=True), out)
            return out

        oa = oa_ref[...]
        do_a, dga = group_bwd(dm_ref[:, 0:HP], oa, g[:, 0:HP])
        doa_ref[...] = do_a
        dla_ref[...] = delta_bc(do_a, oa)
        dg_ref[:, 0:HP] += dga
        ob = ob_ref[...]
        do_b, dgb = group_bwd(dm_ref[:, HP:2 * HP], ob, g[:, HP:2 * HP])
        dob_ref[...] = do_b
        dlb_ref[...] = delta_bc(do_b, ob)
        dg_ref[:, HP:2 * HP] += dgb
        do_d, dgd = group_bwd(dm_ref[:, 3 * HP:4 * HP], od_ref[...], g[:, 3 * HP:4 * HP])
        dod_ref[...] = do_d
        dg_ref[:, 3 * HP:4 * HP] += dgd
        real = lax.broadcasted_iota(jnp.int32, (TM, LANES), 1) < HEAD_DIM
        for hb in range(HEADS):
            sl = slice(hb * LANES, (hb + 1) * LANES)
            gsl = slice(2 * HP + hb * LANES, 2 * HP + (hb + 1) * LANES)
            o = oc_ref[:, sl]
            mu = jnp.sum(o, axis=-1, keepdims=True) * (1.0 / HEAD_DIM)
            dlt = jnp.where(real, o - mu, 0.0)
            var = jnp.sum(dlt * dlt, axis=-1, keepdims=True) * (1.0 / HEAD_DIM)
            rstd = lax.rsqrt(var + EPS)
            dhat = dlt * rstd
            gc = g[:, gsl]
            x = rg_ref[:, sl]
            sg = _sigmoid(x)
            dm = dm_ref[:, gsl]
            drg_ref[:, sl] = dm * (dhat * gc) * (sg * (1.0 + x * (1.0 - sg)))
            dyn = dm * (x * sg)
            dg_ref[:, gsl] += jnp.sum(dyn * dhat, axis=0, keepdims=True)
            ddh = dyn * gc
            m1 = jnp.sum(ddh, axis=-1, keepdims=True) * (1.0 / HEAD_DIM)
            m2 = jnp.sum(ddh * dhat, axis=-1, keepdims=True) * (1.0 / HEAD_DIM)
            doc_ref[:, sl] = jnp.where(real, rstd * (ddh - m1 - dhat * m2), 0.0)

    rg_spec = pl.BlockSpec((TM, HP), lambda i: (i, OFF_RG // HP))
    hp = _row_spec(HP)
    return pl.pallas_call(
        body, name=name, grid=(t // TM,),
        in_specs=[_row_spec(4 * HP), hp, hp, hp, hp, rg_spec, _vec_spec(4 * HP)],
        out_specs=[hp] * 4 + [_row_spec(LANES)] * 2 + [hp, _vec_spec(4 * HP)],
        out_shape=[jax.ShapeDtypeStruct((t, HP), F32)] * 4 + [jax.ShapeDtypeStruct((t, LANES), F32)] * 2
        + [jax.ShapeDtypeStruct((t, HP), F32), jax.ShapeDtypeStruct((1, 4 * HP), F32)],
        compiler_params=_cp("arbitrary"))(dmx, oa, ob, oc, od, proj, g_pad)


def _mesh_pos():
    return lax.axis_index("x"), lax.axis_index("y"), lax.axis_index("c")


def _peer(pos, k):
    x, y, c = pos
    px = 1 - x if (k >> 2) & 1 else x
    py = 1 - y if (k >> 1) & 1 else y
    pc = 1 - c if k & 1 else c
    return (px, py, pc), 4 * px + 2 * py + pc


def _exchange(arrs, gather, name):
    n = len(arrs)

    def body(*refs):
        ins, outs = refs[:n], refs[n:2 * n]
        send_sems, recv_sems, loc_sems = refs[2 * n:]
        pos = _mesh_pos()
        me = 4 * pos[0] + 2 * pos[1] + pos[2]
        local = []
        for a in range(n):
            src = ins[a] if gather else ins[a].at[me]
            cp = pltpu.make_async_copy(src, outs[a].at[me], loc_sems.at[a])
            cp.start()
            local.append(cp)
        sends, recvs = [], []
        for k in range(1, N_DEV):
            peer, pid = _peer(pos, k)
            for a in range(n):
                s = a * (N_DEV - 1) + k - 1
                src = ins[a] if gather else ins[a].at[pid]
                cp = pltpu.make_async_remote_copy(
                    src_ref=src, dst_ref=outs[a].at[me], send_sem=send_sems.at[s], recv_sem=recv_sems.at[s],
                    device_id=peer, device_id_type=pl.DeviceIdType.MESH)
                cp.start()
                sends.append(cp)
                recvs.append(pltpu.make_async_remote_copy(
                    src_ref=src, dst_ref=outs[a].at[pid], send_sem=send_sems.at[s], recv_sem=recv_sems.at[s],
                    device_id=peer, device_id_type=pl.DeviceIdType.MESH))
        for cp in recvs:
            cp.wait_recv()
        for cp in sends:
            cp.wait_send()
        for cp in local:
            cp.wait()

    any_spec = pl.BlockSpec(memory_space=pl.ANY)
    out_shape = [jax.ShapeDtypeStruct((N_DEV,) + tuple(a.shape) if gather else tuple(a.shape), a.dtype) for a in arrs]
    return pl.pallas_call(
        body, name=name, in_specs=[any_spec] * n, out_specs=[any_spec] * n, out_shape=out_shape,
        scratch_shapes=[pltpu.SemaphoreType.DMA((n * (N_DEV - 1),)), pltpu.SemaphoreType.DMA((n * (N_DEV - 1),)),
                        pltpu.SemaphoreType.DMA((n,))],
        compiler_params=pltpu.CompilerParams(has_side_effects=True))(*arrs)


def _device_index():
    x, y, c = _mesh_pos()
    return 4 * x + 2 * y + c


def _landing(srcs, gather):
    me = _device_index()
    lands = []
    for a in srcs:
        own = a[None] if gather else lax.dynamic_slice_in_dim(a, me, 1, axis=0)
        shape = (N_DEV,) + tuple(a.shape) if gather else tuple(a.shape)
        lands.append(lax.dynamic_update_slice_in_dim(jnp.zeros(shape, a.dtype), own, me, axis=0))
    return lands


def _exchange_copies(ins, lands, send_sems, recv_sems, gather):
    pos = _mesh_pos()
    me = 4 * pos[0] + 2 * pos[1] + pos[2]
    sends, recvs = [], []
    for k in range(1, N_DEV):
        peer, pid = _peer(pos, k)
        for a in range(len(ins)):
            s = a * (N_DEV - 1) + k - 1
            src = ins[a] if gather else ins[a].at[pid]
            sends.append(pltpu.make_async_remote_copy(
                src_ref=src, dst_ref=lands[a].at[me], send_sem=send_sems.at[s], recv_sem=recv_sems.at[s],
                device_id=peer, device_id_type=pl.DeviceIdType.MESH))
            recvs.append(pltpu.make_async_remote_copy(
                src_ref=src, dst_ref=lands[a].at[pid], send_sem=send_sems.at[s], recv_sem=recv_sems.at[s],
                device_id=peer, device_id_type=pl.DeviceIdType.MESH))
    return sends, recvs


def _exchange_start(srcs, gather, name):
    n = len(srcs)
    lands = _landing(srcs, gather)
    nsem = n * (N_DEV - 1)

    def body(*refs):
        ins, lnd = refs[:n], refs[n:2 * n]
        send_sems, recv_sems = refs[2 * n], refs[2 * n + 1]
        token = refs[-1]
        sends, _ = _exchange_copies(ins, lnd, send_sems, recv_sems, gather)
        for cp in sends:
            cp.start()
        token[...] = jnp.zeros_like(token)

    hbm = pl.BlockSpec(memory_space=pltpu.HBM)
    sem = pl.BlockSpec(memory_space=pltpu.SEMAPHORE)
    bufs = list(srcs) + lands
    out_shape = ([pltpu.SemaphoreType.DMA((nsem,)), pltpu.SemaphoreType.DMA((nsem,))]
                 + [pltpu.HBM(b.shape, b.dtype) for b in bufs] + [jax.ShapeDtypeStruct((8, LANES), F32)])
    outs = pl.pallas_call(
        body, name=name, in_specs=[hbm] * (2 * n),
        out_specs=[sem, sem] + [hbm] * (2 * n) + [pl.BlockSpec(memory_space=pltpu.VMEM)], out_shape=out_shape,
        input_output_aliases={i: 2 + i for i in range(2 * n)},
        compiler_params=pltpu.CompilerParams(has_side_effects=pltpu.SideEffectType.DATAFLOW_SIDE_EFFECTING),
    )(*[pltpu.with_memory_space_constraint(b, pltpu.HBM) for b in bufs])
    return (outs[0], outs[1], outs[2:2 + n], outs[2 + n:2 + 2 * n]), outs[-1]


def _exchange_wait(state, after, gather, name):
    send_sems, recv_sems, srcs, lands = state
    n = len(srcs)

    def body(*refs):
        ins, lnd = refs[:n], refs[n:2 * n]
        s_sems, r_sems = refs[2 * n], refs[2 * n + 1]
        sends, recvs = _exchange_copies(ins, lnd, s_sems, r_sems, gather)
        for cp in sends:
            cp.wait_send()
        for cp in recvs:
            cp.wait_recv()

    hbm = pl.BlockSpec(memory_space=pltpu.HBM)
    sem = pl.BlockSpec(memory_space=pltpu.SEMAPHORE)
    bufs = list(srcs) + list(lands)
    outs = pl.pallas_call(
        body, name=name, in_specs=[hbm] * (2 * n) + [sem, sem, pl.BlockSpec(memory_space=pl.ANY)],
        out_specs=[hbm] * (2 * n), out_shape=[pltpu.HBM(b.shape, b.dtype) for b in bufs],
        input_output_aliases={i: i for i in range(2 * n)},
        compiler_params=pltpu.CompilerParams(has_side_effects=pltpu.SideEffectType.DATAFLOW_SIDE_EFFECTING),
    )(*bufs, send_sems, recv_sems, after)
    return outs[n:]


def _adam_vals(w, g, m, v):
    m = ADAM_B1 * m + (1.0 - ADAM_B1) * g
    v = ADAM_B2 * v + (1.0 - ADAM_B2) * (g * g)
    m_hat = m / ADAM_C1
    v_hat = v / ADAM_C2
    delta = -ADAM_LR * (m_hat / (jnp.sqrt(v_hat) + ADAM_EPS) + ADAM_WD * w)
    return delta, m, v


def _small_allreduce_adam(part, w, m, v, name):
    rows = part.shape[0]

    def body(p_ref, w_ref, m_ref, v_ref, g_ref, d_ref, nm_ref, nv_ref, gath, send_sems, recv_sems):
        pos = _mesh_pos()
        me = 4 * pos[0] + 2 * pos[1] + pos[2]
        gath[me] = p_ref[...]
        sends, recvs = [], []
        for k in range(1, N_DEV):
            peer, pid = _peer(pos, k)
            cp = pltpu.make_async_remote_copy(
                src_ref=p_ref, dst_ref=gath.at[me], send_sem=send_sems.at[k - 1], recv_sem=recv_sems.at[k - 1],
                device_id=peer, device_id_type=pl.DeviceIdType.MESH)
            cp.start()
            sends.append(cp)
            recvs.append(pltpu.make_async_remote_copy(
                src_ref=p_ref, dst_ref=gath.at[pid], send_sem=send_sems.at[k - 1], recv_sem=recv_sems.at[k - 1],
                device_id=peer, device_id_type=pl.DeviceIdType.MESH))
        for cp in recvs:
            cp.wait_recv()
        for cp in sends:
            cp.wait_send()
        g = gath[0]
        for p in range(1, N_DEV):
            g = g + gath[p]
        g_ref[...] = g
        d, nm, nv = _adam_vals(w_ref[...], g, m_ref[...], v_ref[...])
        d_ref[...] = d
        nm_ref[...] = nm
        nv_ref[...] = nv

    vm = pl.BlockSpec(memory_space=pltpu.VMEM)
    sds = jax.ShapeDtypeStruct((rows, LANES), F32)
    return pl.pallas_call(
        body, name=name, in_specs=[vm] * 4, out_specs=[vm] * 4, out_shape=[sds] * 4,
        scratch_shapes=[pltpu.VMEM((N_DEV, rows, LANES), F32), pltpu.SemaphoreType.DMA((N_DEV - 1,)),
                        pltpu.SemaphoreType.DMA((N_DEV - 1,))],
        compiler_params=pltpu.CompilerParams(has_side_effects=True))(part, w, m, v)


def _reduce_adam(recv, w, m, v, name):
    shape = w.shape
    c = shape[-1]
    r = int(np.prod(shape[:-1]))
    recv2, w2, m2, v2 = recv.reshape(N_DEV, r, c), w.reshape(r, c), m.reshape(r, c), v.reshape(r, c)
    tr = r
    while tr * c * 4 > (1 << 20) and tr % 16 == 0:
        tr //= 2

    def body(r_ref, w_ref, m_ref, v_ref, g_ref, d_ref, nm_ref, nv_ref):
        g = r_ref[0].astype(F32)
        for p in range(1, N_DEV):
            g = g + r_ref[p].astype(F32)
        g_ref[...] = g
        d, nm, nv = _adam_vals(w_ref[...], g, m_ref[...], v_ref[...])
        d_ref[...] = d
        nm_ref[...] = nm
        nv_ref[...] = nv

    spec = pl.BlockSpec((tr, c), lambda i: (i, 0))
    sds = jax.ShapeDtypeStruct((r, c), F32)
    outs = pl.pallas_call(
        body, name=name, grid=(r // tr,),
        in_specs=[pl.BlockSpec((N_DEV, tr, c), lambda i: (0, i, 0)), spec, spec, spec],
        out_specs=[spec] * 4, out_shape=[sds] * 4, compiler_params=_cp("parallel"))(recv2, w2, m2, v2)
    return [o.reshape(shape) for o in outs]


def _pad_heads(w, real=HEAD_DIM):
    lead = w.shape[:-1]
    w = w.reshape(lead + (HEADS, real))
    w = jnp.pad(w, [(0, 0)] * len(lead) + [(0, 0), (0, LANES - real)])
    return w.reshape(lead + (HP,))


def _unpad_heads(w, real=HEAD_DIM):
    lead = w.shape[:-1]
    return w.reshape(lead + (HEADS, LANES))[..., :real].reshape(lead + (HEADS * real,))


_IN_SEGS = (("fq", 0, 256), ("fk", 256, 512), ("fv", 512, 768), ("ff", 768, 772), ("cq", 772, 1028),
            ("ckv", 1028, 1156), ("kr", 1156, 1188), ("rq", 1188, 1444), ("rk", 1444, 1700), ("rv", 1700, 1956),
            ("rg", 1956, 2212), ("sq", 2212, 2468), ("sk", 2468, 2724), ("sv", 2724, 2980))


def _pad_w_in(w):
    s = {n: w[:, a:b] for n, a, b in _IN_SEGS}
    rows = w.shape[0]
    z = lambda n: jnp.zeros((rows, n), w.dtype)
    parts = [_pad_heads(s[n]) for n in ("fq", "fk", "fv", "rq", "rk", "rv", "sq", "sk", "sv", "rg")]
    parts += [s["cq"], s["ckv"], z(HEAD_DIM), s["kr"], z(LANES - HEAD_DIM - ROPE_MLA), s["ff"], z(LANES - HEADS),
              z(NP_IN - OFF_FF - LANES)]
    return jnp.concatenate(parts, axis=1)


def _unpad_w_in(wp):
    def heads(off):
        return _unpad_heads(wp[:, off:off + HP])

    parts = [heads(OFF_FOX), heads(OFF_FOX + HP), heads(OFF_FOX + 2 * HP), wp[:, OFF_FF:OFF_FF + HEADS],
             wp[:, OFF_CQ:OFF_CQ + Q_RANK], wp[:, OFF_CKV:OFF_CKV + KV_RANK],
             wp[:, OFF_KR + HEAD_DIM:OFF_KR + HEAD_DIM + ROPE_MLA],
             heads(OFF_RET), heads(OFF_RET + HP), heads(OFF_RET + 2 * HP), heads(OFF_RG),
             heads(OFF_SB), heads(OFF_SB + HP), heads(OFF_SB + 2 * HP)]
    return jnp.concatenate(parts, axis=1)


def _pad_w_kv(w):
    w4 = w.reshape(KV_RANK, HEADS, 2 * HEAD_DIM)
    k = w4[:, :, :HEAD_DIM].reshape(KV_RANK, GROUP)
    v = w4[:, :, HEAD_DIM:].reshape(KV_RANK, GROUP)
    return jnp.concatenate([_pad_heads(k), _pad_heads(v)], axis=1)


def _unpad_w_kv(wp):
    k = _unpad_heads(wp[:, :HP]).reshape(KV_RANK, HEADS, HEAD_DIM)
    v = _unpad_heads(wp[:, HP:]).reshape(KV_RANK, HEADS, HEAD_DIM)
    return jnp.concatenate([k, v], axis=-1).reshape(KV_RANK, HEADS * 2 * HEAD_DIM)


def _pad_rows_out(w):
    w = w.reshape(4 * HEADS, HEAD_DIM, D_MODEL)
    w = jnp.pad(w, ((0, 0), (0, LANES - HEAD_DIM), (0, 0)))
    return w.reshape(4 * HP, D_MODEL)


def _unpad_rows_out(wp):
    return wp.reshape(4 * HEADS, LANES, D_MODEL)[:, :HEAD_DIM, :].reshape(D_MODEL, D_MODEL)


def _pad_gain_out(g):
    g = jnp.pad(g.reshape(4 * HEADS, HEAD_DIM), ((0, 0), (0, LANES - HEAD_DIM)))
    return g.reshape(1, 4 * HP)


def _unpad_gain_out(gp):
    return gp.reshape(4 * HEADS, LANES)[:, :HEAD_DIM].reshape(D_MODEL)


_SMALL = (("g_mix_pre", 1024), ("g_mix_post", 1024), ("g_ffn_pre", 1024), ("g_ffn_post", 1024), ("g_mix_out", 1024),
          ("g_q_lora", 256), ("g_kv_lora", 128), ("b_forget", 4))


def _pack_small(vals):
    parts = []
    for name, n in _SMALL:
        a = vals[name].astype(F32)
        if n < LANES:
            a = jnp.pad(a, ((0, 0), (0, LANES - n)))
        parts.append(a)
    return jnp.concatenate(parts, axis=1).reshape(DEPTH * SMALL_ROWS, LANES)


def _unpack_small(packed):
    flat = packed.reshape(DEPTH, SMALL_ROWS * LANES)
    out, off = {}, 0
    for name, n in _SMALL:
        out[name] = flat[:, off:off + n]
        off += max(n, LANES)
    return out


def kernel(x, positions, g_mix_pre, w_in, b_forget, g_q_lora, w_q_up, g_kv_lora, w_kv_up, g_mix_out, w_out, g_mix_post, g_ffn_pre, w_ffn_up, w_ffn_down, g_ffn_post, loss_target, m_g_mix_pre, m_w_in, m_b_forget, m_g_q_lora, m_w_q_up, m_g_kv_lora, m_w_kv_up, m_g_mix_out, m_w_out, m_g_mix_post, m_g_ffn_pre, m_w_ffn_up, m_w_ffn_down, m_g_ffn_post, v_g_mix_pre, v_w_in, v_b_forget, v_g_q_lora, v_w_q_up, v_g_kv_lora, v_w_kv_up, v_g_mix_out, v_w_out, v_g_mix_post, v_g_ffn_pre, v_w_ffn_up, v_w_ffn_down, v_g_ffn_post):
    t = x.shape[1]
    nq = t // BQ
    x0 = x[0]
    tgt = loss_target[0]
    pos = positions[0].astype(F32).reshape(t, 1)

    half_r, half_m = HEAD_DIM // 2, ROPE_MLA // 2
    invf_r = ROPE_BASE ** (-jnp.arange(half_r, dtype=F32) / half_r)
    invf_m = ROPE_BASE ** (-jnp.arange(half_m, dtype=F32) / half_m)
    blk_r = jnp.concatenate([invf_r, invf_r, jnp.zeros((LANES - HEAD_DIM,), F32)])
    blk_m = jnp.concatenate([jnp.zeros((HEAD_DIM,), F32), invf_m, invf_m, jnp.zeros((LANES - HEAD_DIM - ROPE_MLA,), F32)])
    invf_ret = jnp.tile(blk_r, HEADS).reshape(1, HP)
    invf_mla = jnp.tile(blk_m, HEADS).reshape(1, HP)
    log_gamma = jnp.log1p(-jnp.power(2.0, -5.0 - jnp.arange(HEADS, dtype=F32)))
    lg_lanes = jnp.repeat(log_gamma, LANES).reshape(1, HP)
    lg_heads = jnp.broadcast_to(log_gamma[:, None, None], (HEADS, 1, LANES))

    big = [w_in, w_q_up, w_kv_up, w_out, w_ffn_up, w_ffn_down]
    bf = lambda w: w.astype(BF16)
    first = _exchange([bf(w_in[0]), bf(w_q_up), bf(w_kv_up)], True, "weights_gather_first")
    rest_state, rest_token = _exchange_start([bf(w_in[1]), bf(w_out), bf(w_ffn_up), bf(w_ffn_down)], True,
                                             "weights_gather_start")
    wq_g = first[1].transpose(1, 2, 0, 3).reshape(DEPTH, Q_RANK, 384)
    wkv_g = first[2].transpose(1, 2, 0, 3).reshape(DEPTH, KV_RANK, 512)

    row = lambda g: g.reshape(1, -1)
    layers = []
    for l in range(DEPTH):
        layers.append(dict(
            wq=_pad_heads(wq_g[l], 96), wkv=_pad_w_kv(wkv_g[l]),
            g_pre=row(g_mix_pre[l]), g_post=row(g_mix_post[l]), g_fpre=row(g_ffn_pre[l]), g_fpost=row(g_ffn_post[l]),
            g_out=_pad_gain_out(g_mix_out[l]), g_q=row(g_q_lora[l]), g_kv=row(g_kv_lora[l]),
            b_pad=jnp.pad(b_forget[l], (0, LANES - HEADS)).reshape(1, LANES)))
    layers[0]["win"] = _pad_w_in(first[0].reshape(D_MODEL, D_IN))

    saved = []
    xin = x0
    h = _rms_fwd(xin, layers[0]["g_pre"] + rest_token[0:1, 0:1], "rms_pre_0")
    loss_row = dx = None
    for l, p in enumerate(layers):
        s = dict(x=xin, h=h)
        proj = _mm(h, p["win"], name=f"in_proj_{l}")
        cum = _cumsum(_forget_lsf(proj, p["b_pad"], f"forget_lsf_{l}"), False, f"forget_cumsum_{l}")
        fox, ret, sb, mla, fox_t, sb_t, mla_t, ret_t = _prep_fwd(proj, cum, pos, invf_ret, invf_mla, lg_lanes, p["g_q"],
                                                          p["g_kv"], p["wq"], p["wkv"], f"prep_fwd_{l}")
        oa, lse_a = _softmax_fwd(fox, fox_t, chunk_mask=False, scale=1.0, name=f"fox_fwd_{l}")
        ob, lse_b = _softmax_fwd(mla, mla_t, chunk_mask=True, scale=96.0 ** -0.5, name=f"mla_fwd_{l}")
        oc = _ret_fwd(ret, lg_heads, f"ret_fwd_{l}")
        od, sb_tot = _sb_fwd(sb, sb_t, f"sb_fwd_{l}")
        if l == 0:
            rest = _exchange_wait(rest_state, od, True, "weights_gather_wait")
            wout_g = rest[1].transpose(1, 0, 2, 3).reshape(DEPTH, D_MODEL, D_MODEL)
            wup_g = rest[2].transpose(1, 2, 0, 3).reshape(DEPTH, D_MODEL, D_FF)
            wdn_g = rest[3].transpose(1, 0, 2, 3).reshape(DEPTH, D_FF, D_MODEL)
            layers[1]["win"] = _pad_w_in(rest[0].reshape(D_MODEL, D_IN))
            for ll in range(DEPTH):
                layers[ll].update(wout=_pad_rows_out(wout_g[ll]), wup=wup_g[ll], wdn=wdn_g[ll])
        mixed = _post_fwd(oa, ob, oc, od, proj, p["g_out"], f"post_fwd_{l}")
        mix = _mm(mixed, p["wout"], name=f"out_proj_{l}", tk=2048)
        x1, h2 = _add_rms_fwd(xin, mix, p["g_post"], p["g_fpre"], f"mix_residual_{l}")
        a = _mm(h2, p["wup"], name=f"ffn_up_{l}")
        y = _mm(a, p["wdn"], name=f"ffn_down_{l}", a_fn=_relu2)
        s.update(proj=proj, fox=fox, ret=ret, sb=sb, mla=mla, fox_t=fox_t, sb_t=sb_t, mla_t=mla_t, ret_t=ret_t, oa=oa, ob=ob, oc=oc,
                 od=od, sb_tot=sb_tot, lse_a=lse_a, lse_b=lse_b, mixed=mixed, mix=mix, x1=x1, h2=h2, a=a, y=y)
        saved.append(s)
        if l + 1 < DEPTH:
            xin, h = _add_rms_fwd(x1, y, p["g_fpost"], layers[l + 1]["g_pre"], f"ffn_residual_{l}")
        else:
            loss_row, dx = _final_loss(x1, y, p["g_fpost"], tgt, "loss")
    loss = lax.psum(loss_row[0, 0], ("x", "y", "c"))

    small_g = {n: [None] * DEPTH for n, _ in _SMALL}
    big_g = [[None] * DEPTH for _ in range(6)]
    to_send = [
        lambda g: g.reshape(N_DEV, 1, D_MODEL // N_DEV, D_IN),
        lambda g: g.reshape(Q_RANK, N_DEV, 384 // N_DEV).transpose(1, 0, 2)[:, None],
        lambda g: g.reshape(KV_RANK, N_DEV, 512 // N_DEV).transpose(1, 0, 2)[:, None],
        lambda g: g.reshape(N_DEV, 1, D_MODEL // N_DEV, D_MODEL),
        lambda g: g.reshape(D_MODEL, N_DEV, D_FF // N_DEV).transpose(1, 0, 2)[:, None],
        lambda g: g.reshape(N_DEV, 1, D_FF // N_DEV, D_MODEL),
    ]
    send_of = lambda ks, l: [to_send[k](big_g[k][l]).astype(BF16) for k in ks]
    late_state = early_state = None
    order_token = jnp.zeros((1, 1), F32)
    for l in reversed(range(DEPTH)):
        p, s = layers[l], saved[l]
        dy, dg = _norm_bwd(dx, s["y"], p["g_fpost"] + order_token, None, BF16, f"ffn_post_bwd_{l}")
        small_g["g_ffn_post"][l] = dg
        da = _mm(dy, p["wdn"], name=f"ffn_down_dx_{l}", tb=True, out_dtype=BF16, epi=_drelu2, epi_in=s["a"])
        big_g[5][l] = _mm(s["a"], dy, name=f"ffn_down_dw_{l}", ta=True, a_fn=_relu2, tk=2048)
        big_g[4][l] = _mm(s["h2"], da, name=f"ffn_up_dw_{l}", ta=True, tk=2048)
        dh2 = _mm(da, p["wup"], name=f"ffn_up_dx_{l}", tb=True)
        dx1, dg = _norm_bwd(dh2, s["x1"], p["g_fpre"], dx, F32, f"ffn_pre_bwd_{l}")
        small_g["g_ffn_pre"][l] = dg
        dmix, dg = _norm_bwd(dx1, s["mix"], p["g_post"], None, BF16, f"mix_post_bwd_{l}")
        small_g["g_mix_post"][l] = dg
        dmixed = _mm(dmix, p["wout"], name=f"out_proj_dx_{l}", tb=True)
        big_g[3][l] = _unpad_rows_out(_mm(s["mixed"], dmix, name=f"out_proj_dw_{l}", ta=True, tk=2048))
        g_out = p["g_out"]
        if l == 0:
            early_state, early_token = _exchange_start(send_of((3, 4, 5), 0), False, "grads_layer0_early_start")
            g_out = g_out + early_token[0:1, 0:1]
        doa, dob, doc, dod, dla, dlb, drg, dgo = _post_bwd(dmixed, s["oa"], s["ob"], s["oc"], s["od"], s["proj"],
                                                           g_out, f"post_bwd_{l}")
        small_g["g_mix_out"][l] = _unpad_gain_out(dgo).reshape(1, D_MODEL)
        as_rows = lambda a: a[:, :HEADS].T.reshape(HEADS, nq, 1, BQ)
        dfq, dfk, dfv, dcum_k, dcum_q = _softmax_bwd(s["fox"], s["fox_t"], doa, s["lse_a"], as_rows(dla), bias=True,
                                                     chunk_mask=False, scale=1.0, name=f"fox_bwd_{l}")
        dmq, dmk, dmv = _softmax_bwd(s["mla"], s["mla_t"], dob, s["lse_b"], as_rows(dlb), bias=False, chunk_mask=True,
                                     scale=96.0 ** -0.5, name=f"mla_bwd_{l}")
        drq, drk, drv = _ret_bwd(s["ret"], s["ret_t"], lg_heads, doc, f"ret_bwd_{l}")
        dsq, dsk, dsv = _sb_bwd(s["sb"], s["sb_t"], dod, s["sb_tot"], f"sb_bwd_{l}")
        dcum_q = jnp.pad(dcum_q.reshape(HEADS, t).T, ((0, 0), (0, LANES - HEADS)))
        dlsf = _cumsum(dcum_q, True, f"forget_cumsum_bwd_{l}", partials=dcum_k)
        dproj, dwq, dwkv, dgq, dgkv, dbf = _prep_bwd(
            (dfq, dfk, dfv), (drq, drk, drv), (dsq, dsk, dsv), (dmq, dmk, dmv), drg, dlsf, s["proj"], pos, invf_ret,
            invf_mla, lg_lanes, p["b_pad"], p["g_q"], p["g_kv"], p["wq"], p["wkv"], f"prep_bwd_{l}")
        small_g["g_q_lora"][l] = dgq
        small_g["g_kv_lora"][l] = dgkv
        small_g["b_forget"][l] = dbf[:, :HEADS]
        big_g[1][l] = _unpad_heads(dwq, 96)
        big_g[2][l] = _unpad_w_kv(dwkv)
        big_g[0][l] = _unpad_w_in(_mm(s["h"], dproj, name=f"in_proj_dw_{l}", ta=True, tk=2048))
        dh = _mm(dproj, p["win"], name=f"in_proj_dx_{l}", tb=True)
        dx, dg = _norm_bwd(dh, s["x"], p["g_pre"], dx1, F32, f"mix_pre_bwd_{l}")
        small_g["g_mix_pre"][l] = dg
        if l == DEPTH - 1:
            late_state, late_token = _exchange_start(send_of(range(6), l), False, "grads_layer1_start")
            order_token = late_token[0:1, 0:1]
    grad_x = dx.reshape(1, t, D_MODEL)

    last = _exchange(send_of((0, 1, 2), 0), False, "grads_layer0_rest")
    late = _exchange_wait(late_state, dx, False, "grads_layer1_wait")
    early = _exchange_wait(early_state, dx, False, "grads_layer0_early_wait")
    recv = [jnp.concatenate([(last[k] if k < 3 else early[k - 3]), late[k]], axis=1) for k in range(6)]
    ms = [m_w_in, m_w_q_up, m_w_kv_up, m_w_out, m_w_ffn_up, m_w_ffn_down]
    vs = [v_w_in, v_w_q_up, v_w_kv_up, v_w_out, v_w_ffn_up, v_w_ffn_down]
    names = ["w_in", "w_q_up", "w_kv_up", "w_out", "w_ffn_up", "w_ffn_down"]
    res = {}
    for k, n in enumerate(names):
        res[n] = _reduce_adam(recv[k], big[k], ms[k], vs[k], f"adamw_{n}")

    small_w = dict(g_mix_pre=g_mix_pre, g_mix_post=g_mix_post, g_ffn_pre=g_ffn_pre, g_ffn_post=g_ffn_post,
                   g_mix_out=g_mix_out, g_q_lora=g_q_lora, g_kv_lora=g_kv_lora, b_forget=b_forget)
    small_m = dict(g_mix_pre=m_g_mix_pre, g_mix_post=m_g_mix_post, g_ffn_pre=m_g_ffn_pre, g_ffn_post=m_g_ffn_post,
                   g_mix_out=m_g_mix_out, g_q_lora=m_g_q_lora, g_kv_lora=m_g_kv_lora, b_forget=m_b_forget)
    small_v = dict(g_mix_pre=v_g_mix_pre, g_mix_post=v_g_mix_post, g_ffn_pre=v_g_ffn_pre, g_ffn_post=v_g_ffn_post,
                   g_mix_out=v_g_mix_out, g_q_lora=v_g_q_lora, g_kv_lora=v_g_kv_lora, b_forget=v_b_forget)
    part = _pack_small({n: jnp.concatenate(small_g[n], axis=0) for n, _ in _SMALL})
    sres = _small_allreduce_adam(part, _pack_small(small_w), _pack_small(small_m), _pack_small(small_v),
                                 "small_allreduce_adamw")
    sg, sd, sm, sv = [_unpack_small(a) for a in sres]
    for n, _ in _SMALL:
        res[n] = [sg[n], sd[n], sm[n], sv[n]]

    order = ["g_mix_pre", "w_in", "b_forget", "g_q_lora", "w_q_up", "g_kv_lora", "w_kv_up", "g_mix_out", "w_out",
             "g_mix_post", "g_ffn_pre", "w_ffn_up", "w_ffn_down", "g_ffn_post"]
    outs = [loss, grad_x]
    for idx in range(4):
        outs += [res[n][idx] for n in order]
    return tuple(outs)
```

```python
import functools
import math

import numpy as np
import jax
import jax.numpy as jnp
from jax import lax
from jax.experimental import pallas as pl
from jax.experimental.pallas import tpu as pltpu

F32 = jnp.float32
BF16 = jnp.bfloat16

D_MODEL = 1024
DEPTH = 2
N_DEV = 8
GROUP = 256
HEADS = 4
HEAD_DIM = 64
LANES = 128
HP = HEADS * LANES
QKV = 3 * HP
Q_RANK = 256
KV_RANK = 128
ROPE_MLA = 32
D_FF = 4096
D_IN = 2980
CHUNK_SHIFT = 6
EPS = 1e-6
ROPE_BASE = 10000.0
NEG = -1e30

OFF_FOX, OFF_RET, OFF_SB = 0, QKV, 2 * QKV
OFF_RG = 3 * QKV
OFF_CQ = OFF_RG + HP
OFF_CKV = OFF_CQ + Q_RANK
OFF_KR = OFF_CKV + LANES
OFF_FF = OFF_KR + LANES
NP_IN = 6144

BQ = 256
TKS = 128
TM = 256
VMEM_LIMIT = 48 * 1024 * 1024

ADAM_LR, ADAM_B1, ADAM_B2, ADAM_EPS, ADAM_WD, ADAM_STEP = 0.001, 0.9, 0.999, 1e-08, 0.01, 10
ADAM_C1 = 1.0 - ADAM_B1 ** ADAM_STEP
ADAM_C2 = 1.0 - ADAM_B2 ** ADAM_STEP

SMALL_ROWS = 44

NT = (((1,), (1,)), ((), ()))
TN = (((0,), (0,)), ((), ()))


def _cp(*sem):
    return pltpu.CompilerParams(dimension_semantics=sem if sem else None, vmem_limit_bytes=VMEM_LIMIT)


def _bdot(a, b, dn=None):
    if dn is None:
        return jnp.dot(a, b, preferred_element_type=F32)
    return lax.dot_general(a, b, dn, preferred_element_type=F32)


def _split2(x):
    hi = x.astype(BF16)
    lo = (x - hi.astype(F32)).astype(BF16)
    return hi, lo


def _mm(a, b, *, name, ta=False, tb=False, out_dtype=F32, a_fn=None, epi=None, epi_in=None,
        tm=1024, tn=1024, tk=1024):
    m, k = (a.shape[1], a.shape[0]) if ta else a.shape
    n = b.shape[0] if tb else b.shape[1]
    tm, tn, tk = min(tm, m), min(tn, n), min(tk, k)
    assert m % tm == 0 and n % tn == 0 and k % tk == 0, (name, m, n, k)
    nk = k // tk
    dn = (((0 if ta else 1,), (1 if tb else 0,)), ((), ()))

    def body(*refs):
        if epi is None:
            a_ref, b_ref, o_ref = refs[:3]
            e_ref = None
            rest = refs[3:]
        else:
            a_ref, b_ref, e_ref, o_ref = refs[:4]
            rest = refs[4:]
        av = a_ref[...]
        if a_fn is not None:
            av = a_fn(av)
        part = lax.dot_general(av.astype(BF16), b_ref[...].astype(BF16), dn, preferred_element_type=F32)

        def finish(r):
            if epi is not None:
                r = epi(r, e_ref[...])
            o_ref[...] = r.astype(out_dtype)

        if nk == 1:
            finish(part)
        else:
            acc_ref = rest[0]
            kk = pl.program_id(2)

            @pl.when(kk == 0)
            def _():
                acc_ref[...] = part

            @pl.when(kk > 0)
            def _():
                acc_ref[...] += part

            @pl.when(kk == nk - 1)
            def _():
                finish(acc_ref[...])

    a_spec = pl.BlockSpec((tk, tm), lambda i, j, kk: (kk, i)) if ta else pl.BlockSpec((tm, tk), lambda i, j, kk: (i, kk))
    b_spec = pl.BlockSpec((tn, tk), lambda i, j, kk: (j, kk)) if tb else pl.BlockSpec((tk, tn), lambda i, j, kk: (kk, j))
    o_spec = pl.BlockSpec((tm, tn), lambda i, j, kk: (i, j))
    in_specs = [a_spec, b_spec]
    args = [a, b]
    if epi is not None:
        in_specs.append(o_spec)
        args.append(epi_in)
    return pl.pallas_call(
        body, name=name, grid=(m // tm, n // tn, nk),
        in_specs=in_specs, out_specs=o_spec,
        out_shape=jax.ShapeDtypeStruct((m, n), out_dtype),
        scratch_shapes=[pltpu.VMEM((tm, tn), F32)] if nk > 1 else [],
        compiler_params=_cp("parallel", "parallel", "arbitrary"),
    )(*args)


def _relu2(v):
    r = jnp.maximum(v, 0.0)
    return r * r


def _drelu2(du, av):
    return du * (2.0 * jnp.maximum(av, 0.0))


def _rms(v, g):
    r = lax.rsqrt(jnp.mean(v * v, axis=-1, keepdims=True) + EPS)
    return v * r * g


def _row_spec(w):
    return pl.BlockSpec((TM, w), lambda i: (i, 0))


def _vec_spec(w):
    return pl.BlockSpec((1, w), lambda i: (0, 0))


def _rms_fwd(x, g, name):
    t, d = x.shape

    def body(x_ref, g_ref, h_ref):
        h_ref[...] = _rms(x_ref[...], g_ref[...]).astype(BF16)

    return pl.pallas_call(
        body, name=name, grid=(t // TM,), in_specs=[_row_spec(d), _vec_spec(d)], out_specs=_row_spec(d),
        out_shape=jax.ShapeDtypeStruct((t, d), BF16), compiler_params=_cp("parallel"))(x, g)


def _add_rms_fwd(x, y, g1, g2, name):
    t, d = x.shape

    def body(x_ref, y_ref, g1_ref, g2_ref, xn_ref, h_ref):
        xn = x_ref[...] + _rms(y_ref[...], g1_ref[...])
        xn_ref[...] = xn
        h_ref[...] = _rms(xn, g2_ref[...]).astype(BF16)

    return pl.pallas_call(
        body, name=name, grid=(t // TM,),
        in_specs=[_row_spec(d), _row_spec(d), _vec_spec(d), _vec_spec(d)],
        out_specs=[_row_spec(d), _row_spec(d)],
        out_shape=[jax.ShapeDtypeStruct((t, d), F32), jax.ShapeDtypeStruct((t, d), BF16)],
        compiler_params=_cp("parallel"))(x, y, g1, g2)


def _final_loss(x, y, g, tgt, name):
    t, d = x.shape

    def body(x_ref, y_ref, g_ref, t_ref, l_ref, dx_ref):
        @pl.when(pl.program_id(0) == 0)
        def _():
            l_ref[...] = jnp.zeros_like(l_ref)

        err = x_ref[...] + _rms(y_ref[...], g_ref[...]) - t_ref[...]
        dx_ref[...] = err * (1.0 / d)
        l_ref[...] += jnp.sum(jnp.sum(err * err, axis=1, keepdims=True), axis=0, keepdims=True) * (0.5 / d)

    return pl.pallas_call(
        body, name=name, grid=(t // TM,),
        in_specs=[_row_spec(d), _row_spec(d), _vec_spec(d), _row_spec(d)],
        out_specs=[pl.BlockSpec((1, LANES), lambda i: (0, 0)), _row_spec(d)],
        out_shape=[jax.ShapeDtypeStruct((1, LANES), F32), jax.ShapeDtypeStruct((t, d), F32)],
        compiler_params=_cp("arbitrary"))(x, y, g, tgt)


def _rms_bwd_vals(dn, v, g):
    w = v.shape[-1]
    r = lax.rsqrt(jnp.mean(v * v, axis=-1, keepdims=True) + EPS)
    vh = v * r
    dgp = jnp.sum(dn * vh, axis=0, keepdims=True)
    dvh = dn * g
    dv = r * (dvh - vh * (jnp.sum(dvh * vh, axis=-1, keepdims=True) * (1.0 / w)))
    return dv, dgp


def _norm_bwd(dn, v, g, resid, out_dtype, name):
    t, d = v.shape
    has_res = resid is not None

    def body(*refs):
        if has_res:
            dn_ref, v_ref, g_ref, r_ref, dv_ref, dg_ref = refs
        else:
            dn_ref, v_ref, g_ref, dv_ref, dg_ref = refs

        @pl.when(pl.program_id(0) == 0)
        def _():
            dg_ref[...] = jnp.zeros_like(dg_ref)

        dv, dgp = _rms_bwd_vals(dn_ref[...].astype(F32), v_ref[...], g_ref[...])
        if has_res:
            dv = dv + r_ref[...]
        dv_ref[...] = dv.astype(out_dtype)
        dg_ref[...] += dgp

    in_specs = [_row_spec(d), _row_spec(d), _vec_spec(d)] + ([_row_spec(d)] if has_res else [])
    args = [dn, v, g] + ([resid] if has_res else [])
    return pl.pallas_call(
        body, name=name, grid=(t // TM,), in_specs=in_specs,
        out_specs=[_row_spec(d), _vec_spec(d)],
        out_shape=[jax.ShapeDtypeStruct((t, d), out_dtype), jax.ShapeDtypeStruct((1, d), F32)],
        compiler_params=_cp("arbitrary"))(*args)


def _rope_tables(pos, invf, lo, half):
    w = invf.shape[-1]
    lane = lax.broadcasted_iota(jnp.int32, (pos.shape[0], w), 1) & (LANES - 1)
    first = (lane >= lo) & (lane < lo + half)
    second = (lane >= lo + half) & (lane < lo + 2 * half)
    ang = pos * invf
    active = first | second
    cos = jnp.where(active, jnp.cos(ang), 1.0)
    sin = jnp.where(active, jnp.sin(ang), 0.0)
    return cos, sin, first, second


def _rope_apply(v, cos, sin, first, second, half, sign):
    w = v.shape[-1]
    up = pltpu.roll(v, w - half, 1)
    dn = pltpu.roll(v, half, 1)
    rot = jnp.where(first, -up, jnp.where(second, dn, 0.0))
    return v * cos + rot * (sin * sign)


def _forget_lsf(proj, b_pad, name):
    t = proj.shape[0]

    def body(ff_ref, b_ref, o_ref):
        f = ff_ref[...] + b_ref[...]
        o_ref[...] = -(jnp.maximum(-f, 0.0) + jnp.log(1.0 + jnp.exp(-jnp.abs(f))))

    return pl.pallas_call(
        body, name=name, grid=(t // TM,),
        in_specs=[pl.BlockSpec((TM, LANES), lambda i: (i, OFF_FF // LANES)), _vec_spec(LANES)],
        out_specs=_row_spec(LANES), out_shape=jax.ShapeDtypeStruct((t, LANES), F32),
        compiler_params=_cp("parallel"))(proj, b_pad)


def _split3(c):
    hi = c.astype(BF16).astype(F32)
    mid = (c - hi).astype(BF16).astype(F32)
    return hi, mid, (c - hi) - mid


def _transposed_tiles(dst, row0, blocks, width):
    for b, blk in enumerate(blocks):
        bt = blk.T.astype(BF16)
        rows = bt.shape[0]
        for w in range(TM // width):
            dst[w, row0 + b * rows:row0 + (b + 1) * rows, :] = bt[:, w * width:(w + 1) * width]


def _prep_fwd(proj, cum, pos, invf_ret, invf_mla, lg_lanes, g_q, g_kv, wq_pad, wkv_pad, name):
    t = proj.shape[0]

    def body(fox_ref, ret_ref, sb_ref, cq_ref, ckv_ref, kr_ref, cum_ref, pos_ref, ifr_ref, ifm_ref, lg_ref,
             gq_ref, gkv_ref, wq_ref, wkv_ref,
             ofox_ref, oret_ref, osb_ref, omla_ref, ofoxt_ref, osbt_ref, omlat_ref, orett_ref):
        pos_v = pos_ref[...]
        osb_ref[:, 0:HP] = (sb_ref[:, 0:HP] * 0.125).astype(BF16)
        osb_ref[:, HP:QKV] = sb_ref[:, HP:QKV].astype(BF16)
        _transposed_tiles(osbt_ref, 0, [sb_ref[:, HP:2 * HP], sb_ref[:, 2 * HP:QKV]], TKS)
        lane = lax.broadcasted_iota(jnp.int32, (TM, LANES), 1)
        cumv = cum_ref[...]
        fq, fk = [], []
        for hb in range(HEADS):
            hi, mid, lo = _split3(cumv[:, hb:hb + 1])
            q = fox_ref[:, hb * LANES:(hb + 1) * LANES] * 0.125
            k = fox_ref[:, HP + hb * LANES:HP + (hb + 1) * LANES]
            ones_q = (lane >= HEAD_DIM) & (lane < HEAD_DIM + 3)
            ones_k = (lane >= HEAD_DIM + 3) & (lane < HEAD_DIM + 6)
            q = jnp.where(ones_q, 1.0, q)
            k = jnp.where(ones_k, 1.0, k)
            for n, part in enumerate((hi, mid, lo)):
                q = jnp.where(lane == HEAD_DIM + 3 + n, part, q)
                k = jnp.where(lane == HEAD_DIM + n, -part, k)
            fq.append(q)
            fk.append(k)
        fk = jnp.concatenate(fk, axis=1)
        ofox_ref[:, 0:HP] = jnp.concatenate(fq, axis=1).astype(BF16)
        ofox_ref[:, HP:2 * HP] = fk.astype(BF16)
        ofox_ref[:, 2 * HP:QKV] = fox_ref[:, 2 * HP:QKV].astype(BF16)
        _transposed_tiles(ofoxt_ref, 0, [fk, fox_ref[:, 2 * HP:QKV]], BQ)
        cos, sin, first, second = _rope_tables(pos_v, ifr_ref[...], 0, HEAD_DIM // 2)
        nloc = lax.broadcasted_iota(jnp.int32, (TM, 1), 0).astype(F32)
        dec = lg_ref[...] * nloc
        rq = _rope_apply(ret_ref[:, 0:HP], cos, sin, first, second, HEAD_DIM // 2, 1.0)
        rk = _rope_apply(ret_ref[:, HP:2 * HP], cos, sin, first, second, HEAD_DIM // 2, 1.0)
        oret_ref[:, 0:HP] = (rq * jnp.exp(dec)).astype(BF16)
        rk = rk * 0.125 * jnp.exp(-dec)
        oret_ref[:, HP:2 * HP] = rk.astype(BF16)
        _transposed_tiles(orett_ref, 0, [rk], BQ)
        oret_ref[:, 2 * HP:QKV] = ret_ref[:, 2 * HP:QKV].astype(BF16)
        cosm, sinm, firstm, secondm = _rope_tables(pos_v, ifm_ref[...], HEAD_DIM, ROPE_MLA // 2)
        cqn = _rms(cq_ref[...], gq_ref[...]).astype(BF16)
        qm = _bdot(cqn, wq_ref[...])
        omla_ref[:, 0:HP] = _rope_apply(qm, cosm, sinm, firstm, secondm, ROPE_MLA // 2, 1.0).astype(BF16)
        ckvn = _rms(ckv_ref[...], gkv_ref[...]).astype(BF16)
        kv = _bdot(ckvn, wkv_ref[...])
        krr = _rope_apply(kr_ref[...], cosm[:, 0:LANES], sinm[:, 0:LANES], firstm[:, 0:LANES],
                          secondm[:, 0:LANES], ROPE_MLA // 2, 1.0)
        mk = kv[:, 0:HP] + jnp.concatenate([krr] * HEADS, axis=1)
        omla_ref[:, HP:2 * HP] = mk.astype(BF16)
        omla_ref[:, 2 * HP:QKV] = kv[:, HP:2 * HP].astype(BF16)
        _transposed_tiles(omlat_ref, 0, [mk, kv[:, HP:2 * HP]], BQ)

    def seg(off, w):
        return pl.BlockSpec((TM, w), lambda i, o=off // w: (i, o))

    def full(shape):
        return pl.BlockSpec(shape, lambda i: (0,) * len(shape))

    def tiles(width):
        return pl.BlockSpec((TM // width, 2 * HP, width), lambda i: (i, 0, 0))

    in_specs = [seg(OFF_FOX, QKV), seg(OFF_RET, QKV), seg(OFF_SB, QKV), seg(OFF_CQ, Q_RANK), seg(OFF_CKV, LANES),
                seg(OFF_KR, LANES), _row_spec(LANES), pl.BlockSpec((TM, 1), lambda i: (i, 0)),
                full((1, HP)), full((1, HP)), full((1, HP)), full((1, Q_RANK)), full((1, KV_RANK)),
                full((Q_RANK, HP)), full((KV_RANK, 2 * HP))]
    out_specs = [_row_spec(QKV)] * 4 + [tiles(BQ), tiles(TKS), tiles(BQ),
                                        pl.BlockSpec((1, HP, BQ), lambda i: (i, 0, 0))]
    out_shape = [jax.ShapeDtypeStruct((t, QKV), BF16)] * 4 + [
        jax.ShapeDtypeStruct((t // BQ, 2 * HP, BQ), BF16), jax.ShapeDtypeStruct((t // TKS, 2 * HP, TKS), BF16),
        jax.ShapeDtypeStruct((t // BQ, 2 * HP, BQ), BF16), jax.ShapeDtypeStruct((t // BQ, HP, BQ), BF16)]
    return pl.pallas_call(
        body, name=name, grid=(t // TM,), in_specs=in_specs, out_specs=out_specs, out_shape=out_shape,
        compiler_params=_cp("parallel"))(proj, proj, proj, proj, proj, proj, cum, pos, invf_ret, invf_mla, lg_lanes,
                                         g_q, g_kv, wq_pad, wkv_pad)


def _prep_bwd(dfox, dret, dsb, dmla, drg, dlsf, proj, pos, invf_ret, invf_mla, lg_lanes, b_pad, g_q, g_kv,
              wq_pad, wkv_pad, name):
    t = proj.shape[0]

    def body(dfq, dfk, dfv, drq, drk, drv, dsq, dsk, dsv, dmq, dmk, dmv, drg_ref, dlsf_ref,
             cq_ref, ckv_ref, ff_ref, pos_ref, ifr_ref, ifm_ref, lg_ref, b_ref, gq_ref, gkv_ref, wq_ref, wkv_ref,
             dp_ref, dwq_ref, dwkv_ref, dgq_ref, dgkv_ref, dbf_ref):
        @pl.when(pl.program_id(0) == 0)
        def _():
            dwq_ref[...] = jnp.zeros_like(dwq_ref)
            dwkv_ref[...] = jnp.zeros_like(dwkv_ref)
            dgq_ref[...] = jnp.zeros_like(dgq_ref)
            dgkv_ref[...] = jnp.zeros_like(dgkv_ref)
            dbf_ref[...] = jnp.zeros_like(dbf_ref)

        pos_v = pos_ref[...]
        for off, (dq, dk, dv) in ((OFF_FOX, (dfq, dfk, dfv)), (OFF_SB, (dsq, dsk, dsv))):
            dp_ref[:, off:off + HP] = (dq[...] * 0.125).astype(BF16)
            dp_ref[:, off + HP:off + 2 * HP] = dk[...].astype(BF16)
            dp_ref[:, off + 2 * HP:off + QKV] = dv[...].astype(BF16)
        cos, sin, first, second = _rope_tables(pos_v, ifr_ref[...], 0, HEAD_DIM // 2)
        nloc = lax.broadcasted_iota(jnp.int32, (TM, 1), 0).astype(F32)
        dec = lg_ref[...] * nloc
        dq = _rope_apply(drq[...] * jnp.exp(dec), cos, sin, first, second, HEAD_DIM // 2, -1.0)
        dk = _rope_apply(drk[...] * (0.125 * jnp.exp(-dec)), cos, sin, first, second, HEAD_DIM // 2, -1.0)
        dp_ref[:, OFF_RET:OFF_RET + HP] = dq.astype(BF16)
        dp_ref[:, OFF_RET + HP:OFF_RET + 2 * HP] = dk.astype(BF16)
        dp_ref[:, OFF_RET + 2 * HP:OFF_RET + QKV] = drv[...].astype(BF16)
        dp_ref[:, OFF_RG:OFF_RG + HP] = drg_ref[...].astype(BF16)
        cosm, sinm, firstm, secondm = _rope_tables(pos_v, ifm_ref[...], HEAD_DIM, ROPE_MLA // 2)
        dql = _rope_apply(dmq[...], cosm, sinm, firstm, secondm, ROPE_MLA // 2, -1.0).astype(BF16)
        cq = cq_ref[...]
        cqn = _rms(cq, gq_ref[...]).astype(BF16)
        dwq_ref[...] += _bdot(cqn, dql, TN)
        dcqn = _bdot(dql, wq_ref[...], NT)
        dcq, dgq = _rms_bwd_vals(dcqn, cq, gq_ref[...])
        dgq_ref[...] += dgq
        dp_ref[:, OFF_CQ:OFF_CQ + Q_RANK] = dcq.astype(BF16)
        dkm = dmk[...]
        dkv = jnp.concatenate([dkm, dmv[...]], axis=1).astype(BF16)
        ckv = ckv_ref[...]
        ckvn = _rms(ckv, gkv_ref[...]).astype(BF16)
        dwkv_ref[...] += _bdot(ckvn, dkv, TN)
        dckvn = _bdot(dkv, wkv_ref[...], NT)
        dckv, dgkv = _rms_bwd_vals(dckvn, ckv, gkv_ref[...])
        dgkv_ref[...] += dgkv
        dp_ref[:, OFF_CKV:OFF_CKV + LANES] = dckv.astype(BF16)
        dkr = dkm[:, 0:LANES] + dkm[:, LANES:2 * LANES] + dkm[:, 2 * LANES:3 * LANES] + dkm[:, 3 * LANES:HP]
        act = firstm[:, 0:LANES] | secondm[:, 0:LANES]
        dkr = jnp.where(act, dkr, 0.0)
        dkr = _rope_apply(dkr, cosm[:, 0:LANES], sinm[:, 0:LANES], firstm[:, 0:LANES], secondm[:, 0:LANES],
                          ROPE_MLA // 2, -1.0)
        dp_ref[:, OFF_KR:OFF_KR + LANES] = dkr.astype(BF16)
        f = ff_ref[...] + b_ref[...]
        dff = dlsf_ref[...] / (1.0 + jnp.exp(f))
        dbf_ref[...] += jnp.sum(dff, axis=0, keepdims=True)
        dp_ref[:, OFF_FF:OFF_FF + LANES] = dff.astype(BF16)
        dp_ref[:, OFF_FF + LANES:NP_IN] = jnp.zeros((TM, NP_IN - OFF_FF - LANES), BF16)

    def seg(off, w):
        return pl.BlockSpec((TM, w), lambda i, o=off // w: (i, o))

    def full(shape):
        return pl.BlockSpec(shape, lambda i: (0,) * len(shape))

    hp_spec = _row_spec(HP)
    in_specs = [hp_spec] * 13 + [_row_spec(LANES), seg(OFF_CQ, Q_RANK), seg(OFF_CKV, LANES), seg(OFF_FF, LANES),
                                 pl.BlockSpec((TM, 1), lambda i: (i, 0)),
                                 full((1, HP)), full((1, HP)), full((1, HP)), full((1, LANES)), full((1, Q_RANK)),
                                 full((1, KV_RANK)), full((Q_RANK, HP)), full((KV_RANK, 2 * HP))]
    out_specs = [_row_spec(NP_IN), full((Q_RANK, HP)), full((KV_RANK, 2 * HP)), full((1, Q_RANK)),
                 full((1, KV_RANK)), full((1, LANES))]
    out_shape = [jax.ShapeDtypeStruct((t, NP_IN), BF16), jax.ShapeDtypeStruct((Q_RANK, HP), F32),
                 jax.ShapeDtypeStruct((KV_RANK, 2 * HP), F32), jax.ShapeDtypeStruct((1, Q_RANK), F32),
                 jax.ShapeDtypeStruct((1, KV_RANK), F32), jax.ShapeDtypeStruct((1, LANES), F32)]
    return pl.pallas_call(
        body, name=name, grid=(t // TM,), in_specs=in_specs, out_specs=out_specs, out_shape=out_shape,
        compiler_params=_cp("arbitrary"))(*dfox, *dret, *dsb, *dmla, drg, dlsf, proj, proj, proj, pos, invf_ret,
                                          invf_mla, lg_lanes, b_pad, g_q, g_kv, wq_pad, wkv_pad)


def _cumsum(x, reverse, name, partials=None):
    t, w = x.shape
    n = t // TM
    xs = [x] if partials is None else [x, partials]

    def body(*refs):
        x_refs, o_ref, carry = refs[:len(xs)], refs[len(xs)], refs[len(xs) + 1]

        @pl.when(pl.program_id(0) == 0)
        def _():
            carry[...] = jnp.zeros_like(carry)

        r = lax.broadcasted_iota(jnp.int32, (TM, TM), 0)
        c = lax.broadcasted_iota(jnp.int32, (TM, TM), 1)
        tri = jnp.where((r <= c) if reverse else (r >= c), 1.0, 0.0).astype(BF16)
        v = x_refs[0][...]
        if partials is not None:
            lane = lax.broadcasted_iota(jnp.int32, (TM, LANES), 1)
            for hb in range(HEADS):
                v = v + jnp.where(lane == hb, jnp.sum(x_refs[1][:, _hs(hb)], axis=1, keepdims=True), 0.0)
        hi = v.astype(BF16)
        r1 = v - hi.astype(F32)
        mid = r1.astype(BF16)
        lo = (r1 - mid.astype(F32)).astype(BF16)
        cs = _bdot(tri, hi) + _bdot(tri, mid) + _bdot(tri, lo) + carry[...]
        o_ref[...] = cs
        carry[...] = cs[0:1, :] if reverse else cs[TM - 1:TM, :]

    imap = (lambda i: (n - 1 - i, 0)) if reverse else (lambda i: (i, 0))
    return pl.pallas_call(
        body, name=name, grid=(n,), in_specs=[pl.BlockSpec((TM, a.shape[1]), imap) for a in xs],
        out_specs=pl.BlockSpec((TM, w), imap),
        out_shape=jax.ShapeDtypeStruct((t, w), F32), scratch_shapes=[pltpu.VMEM((1, w), F32)],
        compiler_params=_cp("arbitrary"))(*xs)


HB_FWD = 4
HB_BWD = 2
HB_SB_FWD = 2
BQS = 512


def _q_spec(hb, bq=BQ):
    return pl.BlockSpec((bq, hb * LANES), lambda g, i: (i, g))


def _kv_spec(t, which, hb):
    return pl.BlockSpec((t, hb * LANES), lambda g, i, w=which: (0, w * (HEADS // hb) + g))


def _acc_spec(t, hb):
    return pl.BlockSpec((t, hb * LANES), lambda g, i: (0, g))


def _hs(hh):
    return slice(hh * LANES, (hh + 1) * LANES)


def _tile_iota(rows, cols):
    return (lax.broadcasted_iota(jnp.int32, (rows, cols), 0), lax.broadcasted_iota(jnp.int32, (rows, cols), 1))


def _kvt_spec(nkv, width, which, hb):
    return pl.BlockSpec((nkv, hb * LANES, width), lambda g, i, w=which: (0, w * (HEADS // hb) + g, 0))


def _qrow_spec(hb, bq=BQ):
    return pl.BlockSpec((hb, 1, 1, bq), lambda g, i: (g, i, 0, 0))


def _vis_t(chunk_mask):
    r, c = _tile_iota(BQ, BQ)
    return ((r >> CHUNK_SHIFT) <= (c >> CHUNK_SHIFT)) if chunk_mask else (r <= c)


def _softmax_fwd(qkv, kvt, *, chunk_mask, scale, name):
    t = qkv.shape[0]
    nq = t // BQ
    hb = HB_FWD

    def body(q_ref, k_ref, vt_ref, o_ref, lse_ref, m_sc, l_sc, acc_sc):
        i = pl.program_id(1)
        m_sc[...] = jnp.full((hb, 1, BQ), NEG, F32)
        l_sc[...] = jnp.zeros((hb, 1, BQ), F32)
        acc_sc[...] = jnp.zeros((hb, LANES, BQ), F32)

        def tile(j, masked):
            off = pl.multiple_of(j * BQ, BQ)
            vis = _vis_t(chunk_mask) if masked else None
            ss = [_bdot(k_ref[pl.ds(off, BQ), _hs(hh)], q_ref[:, _hs(hh)], NT) for hh in range(hb)]
            ps, alphas = [], []
            for hh in range(hb):
                s = ss[hh]
                if scale != 1.0:
                    s = s * scale
                if masked:
                    s = jnp.where(vis, s, NEG)
                m_old = m_sc[hh]
                m_new = jnp.maximum(m_old, jnp.max(s, axis=0, keepdims=True))
                alpha = jnp.exp(m_old - m_new)
                p = jnp.exp(s - m_new)
                l_sc[hh] = alpha * l_sc[hh] + jnp.sum(p, axis=0, keepdims=True)
                m_sc[hh] = m_new
                ps.append(p.astype(BF16))
                alphas.append(alpha)
            for hh in range(hb):
                acc_sc[hh] = alphas[hh] * acc_sc[hh] + _bdot(vt_ref[j, _hs(hh), :], ps[hh])

        def loop(j, carry):
            tile(j, False)
            return carry

        lax.fori_loop(0, i, loop, 0)
        tile(i, True)
        for hh in range(hb):
            l = l_sc[hh]
            o_ref[:, _hs(hh)] = (acc_sc[hh] / l).T
            lse_ref[hh, 0] = m_sc[hh] + jnp.log(l)

    return pl.pallas_call(
        body, name=name, grid=(HEADS // hb, nq),
        in_specs=[_q_spec(hb), _kv_spec(t, 1, hb), _kvt_spec(nq, BQ, 1, hb)],
        out_specs=[_q_spec(hb), _qrow_spec(hb)],
        out_shape=[jax.ShapeDtypeStruct((t, HP), F32), jax.ShapeDtypeStruct((HEADS, nq, 1, BQ), F32)],
        scratch_shapes=[pltpu.VMEM((hb, 1, BQ), F32), pltpu.VMEM((hb, 1, BQ), F32), pltpu.VMEM((hb, LANES, BQ), F32)],
        compiler_params=_cp("parallel", "arbitrary"))(qkv, qkv, kvt)


def _softmax_bwd(qkv, kvt, do, lse, delta, *, bias, chunk_mask, scale, name):
    t = qkv.shape[0]
    nq = t // BQ
    hb = HB_BWD

    def body(*refs):
        if bias:
            (q_ref, k_ref, v_ref, kt_ref, do_ref, lse_ref, dl_ref, dq_ref, dk_ref, dv_ref, dck_ref, dcq_ref,
             dq_sc, dcq_sc) = refs
            dcq_sc[...] = jnp.zeros((hb, 1, BQ), F32)
        else:
            q_ref, k_ref, v_ref, kt_ref, do_ref, lse_ref, dl_ref, dq_ref, dk_ref, dv_ref, dq_sc = refs
        i = pl.program_id(1)

        @pl.when(i == 0)
        def _():
            dk_ref[...] = jnp.zeros_like(dk_ref)
            dv_ref[...] = jnp.zeros_like(dv_ref)
            if bias:
                dck_ref[...] = jnp.zeros_like(dck_ref)

        dq_sc[...] = jnp.zeros((hb, LANES, BQ), F32)

        def tile(j, masked):
            off = pl.multiple_of(j * BQ, BQ)
            vis = _vis_t(chunk_mask) if masked else None
            qs = [q_ref[:, _hs(hh)] for hh in range(hb)]
            dobs = [do_ref[:, _hs(hh)].astype(BF16) for hh in range(hb)]
            ss = [_bdot(k_ref[pl.ds(off, BQ), _hs(hh)], qs[hh], NT) for hh in range(hb)]
            dps = [_bdot(v_ref[pl.ds(off, BQ), _hs(hh)], dobs[hh], NT) for hh in range(hb)]
            pbs, dsbs = [], []
            for hh in range(hb):
                s = ss[hh]
                if scale != 1.0:
                    s = s * scale
                p = jnp.exp(s - lse_ref[hh, 0])
                if masked:
                    p = jnp.where(vis, p, 0.0)
                ds = p * (dps[hh] - dl_ref[hh, 0])
                if bias:
                    dck_ref[pl.ds(off, BQ), _hs(hh)] -= ds[:, 0:LANES] + ds[:, LANES:BQ]
                    dcq_sc[hh] += jnp.sum(ds, axis=0, keepdims=True)
                if scale != 1.0:
                    ds = ds * scale
                pbs.append(p.astype(BF16))
                dsbs.append(ds.astype(BF16))
            for hh in range(hb):
                sl = _hs(hh)
                dv_ref[pl.ds(off, BQ), sl] += _bdot(pbs[hh], dobs[hh])
                dk_ref[pl.ds(off, BQ), sl] += _bdot(dsbs[hh], qs[hh])
                dq_sc[hh] += _bdot(kt_ref[j, sl, :], dsbs[hh])

        def loop(j, carry):
            tile(j, False)
            return carry

        lax.fori_loop(0, i, loop, 0)
        tile(i, True)
        for hh in range(hb):
            dq_ref[:, _hs(hh)] = dq_sc[hh].T
            if bias:
                dcq_ref[hh, 0] = dcq_sc[hh]

    in_specs = [_q_spec(hb), _kv_spec(t, 1, hb), _kv_spec(t, 2, hb), _kvt_spec(nq, BQ, 0, hb), _q_spec(hb),
                _qrow_spec(hb), _qrow_spec(hb)]
    out_specs = [_q_spec(hb), _acc_spec(t, hb), _acc_spec(t, hb)]
    out_shape = [jax.ShapeDtypeStruct((t, HP), F32)] * 3
    scratch = [pltpu.VMEM((hb, LANES, BQ), F32)]
    if bias:
        out_specs += [_acc_spec(t, hb), _qrow_spec(hb)]
        out_shape += [jax.ShapeDtypeStruct((t, HP), F32), jax.ShapeDtypeStruct((HEADS, nq, 1, BQ), F32)]
        scratch.append(pltpu.VMEM((hb, 1, BQ), F32))
    return pl.pallas_call(
        body, name=name, grid=(HEADS // hb, nq), in_specs=in_specs, out_specs=out_specs, out_shape=out_shape,
        scratch_shapes=scratch,
        compiler_params=_cp("parallel", "arbitrary"))(qkv, qkv, qkv, kvt, do, lse, delta)


def _ret_diag_decay(lg1, keys_on_rows=False):
    r, c = _tile_iota(BQ, BQ)
    qn, km = (c, r) if keys_on_rows else (r, c)
    dd = jnp.where(km > qn, jnp.exp((2.0 * lg1) * (km - qn).astype(F32)), 1.0)
    return jnp.where((km >> CHUNK_SHIFT) <= (qn >> CHUNK_SHIFT), dd, 0.0)


def _lg_spec(hb):
    return pl.BlockSpec((hb, 1, LANES), lambda g, i: (g, 0, 0))


def _ret_fwd(qkv, lg_heads, name):
    t = qkv.shape[0]
    nq = t // BQ
    hb = HB_FWD

    def body(lg_ref, q_ref, k_ref, v_ref, o_ref, acc_sc):
        i = pl.program_id(1)
        acc_sc[...] = jnp.zeros((hb, BQ, LANES), F32)

        def loop(j, carry):
            off = pl.multiple_of(j * BQ, BQ)
            span = ((i - j) * BQ).astype(F32)
            aa = [_bdot(q_ref[:, _hs(hh)], k_ref[pl.ds(off, BQ), _hs(hh)], NT).astype(BF16) for hh in range(hb)]
            for hh in range(hb):
                cf = jnp.exp(lg_ref[hh][:, 0:1] * span)
                acc_sc[hh] += cf * _bdot(aa[hh], v_ref[pl.ds(off, BQ), _hs(hh)])
            return carry

        lax.fori_loop(0, i, loop, 0)
        off = pl.multiple_of(i * BQ, BQ)
        for hh in range(hb):
            sl = _hs(hh)
            a = _bdot(q_ref[:, sl], k_ref[pl.ds(off, BQ), sl], NT) * _ret_diag_decay(lg_ref[hh][:, 0:1])
            o_ref[:, sl] = acc_sc[hh] + _bdot(a.astype(BF16), v_ref[pl.ds(off, BQ), sl])

    return pl.pallas_call(
        body, name=name, grid=(HEADS // hb, nq),
        in_specs=[_lg_spec(hb), _q_spec(hb), _kv_spec(t, 1, hb), _kv_spec(t, 2, hb)],
        out_specs=_q_spec(hb), out_shape=jax.ShapeDtypeStruct((t, HP), F32),
        scratch_shapes=[pltpu.VMEM((hb, BQ, LANES), F32)],
        compiler_params=_cp("parallel", "arbitrary"))(lg_heads, qkv, qkv, qkv)


def _ret_bwd(qkv, kt, lg_heads, do, name):
    t = qkv.shape[0]
    nq = t // BQ
    hb = HB_BWD

    def body(lg_ref, q_ref, k_ref, v_ref, kt_ref, do_ref, dq_ref, dk_ref, dv_ref, dq_sc):
        i = pl.program_id(1)

        @pl.when(i == 0)
        def _():
            dk_ref[...] = jnp.zeros_like(dk_ref)
            dv_ref[...] = jnp.zeros_like(dv_ref)

        dq_sc[...] = jnp.zeros((hb, LANES, BQ), F32)

        def tile(j, diag):
            off = pl.multiple_of(j * BQ, BQ)
            qs = [q_ref[:, _hs(hh)] for hh in range(hb)]
            dobs = [do_ref[:, _hs(hh)].astype(BF16) for hh in range(hb)]
            aa = [_bdot(k_ref[pl.ds(off, BQ), _hs(hh)], qs[hh], NT) for hh in range(hb)]
            das = [_bdot(v_ref[pl.ds(off, BQ), _hs(hh)], dobs[hh], NT) for hh in range(hb)]
            abs_, dabs, cfs = [], [], []
            for hh in range(hb):
                lg1 = lg_ref[hh][:, 0:1]
                if diag:
                    dd = _ret_diag_decay(lg1, keys_on_rows=True)
                    abs_.append((aa[hh] * dd).astype(BF16))
                    dabs.append((das[hh] * dd).astype(BF16))
                    cfs.append(1.0)
                else:
                    abs_.append(aa[hh].astype(BF16))
                    dabs.append(das[hh].astype(BF16))
                    cfs.append(jnp.exp(lg1 * ((i - j) * BQ).astype(F32)))
            for hh in range(hb):
                sl = _hs(hh)
                dv_ref[pl.ds(off, BQ), sl] += cfs[hh] * _bdot(abs_[hh], dobs[hh])
                dk_ref[pl.ds(off, BQ), sl] += cfs[hh] * _bdot(dabs[hh], qs[hh])
                dq_sc[hh] += cfs[hh] * _bdot(kt_ref[j, sl, :], dabs[hh])

        def loop(j, carry):
            tile(j, False)
            return carry

        lax.fori_loop(0, i, loop, 0)
        tile(i, True)
        for hh in range(hb):
            dq_ref[:, _hs(hh)] = dq_sc[hh].T

    return pl.pallas_call(
        body, name=name, grid=(HEADS // hb, nq),
        in_specs=[_lg_spec(hb), _q_spec(hb), _kv_spec(t, 1, hb), _kv_spec(t, 2, hb), _kvt_spec(nq, BQ, 0, hb),
                  _q_spec(hb)],
        out_specs=[_q_spec(hb), _acc_spec(t, hb), _acc_spec(t, hb)],
        out_shape=[jax.ShapeDtypeStruct((t, HP), F32)] * 3,
        scratch_shapes=[pltpu.VMEM((hb, LANES, BQ), F32)],
        compiler_params=_cp("parallel", "arbitrary"))(lg_heads, qkv, qkv, qkv, kt, do)


def _sb_tile_logs(q, kb, vis):
    z = _bdot(kb, q, NT)
    ls = -(jnp.maximum(z, 0.0) + jnp.log(1.0 + jnp.exp(-jnp.abs(z))))
    if vis is not None:
        ls = jnp.where(vis, ls, 0.0)
    return z, ls


def _sb_vis(i, j):
    r, c = _tile_iota(TKS, BQS)
    return (j * TKS + r) < (i * BQS + c)


def _sb_later(ls, after):
    hi, lo = _split2(ls)
    return _bdot(after, hi) + _bdot(after, lo)


def _sb_fwd(qkv, kvt, name):
    t = qkv.shape[0]
    nq = t // BQS
    per = BQS // TKS
    hb = HB_SB_FWD

    def body(q_ref, k_ref, vt_ref, o_ref, tot_ref, acc_sc, r_sc):
        i = pl.program_id(1)
        acc_sc[...] = jnp.zeros((hb, LANES, BQS), F32)
        r_sc[...] = jnp.zeros((hb, 1, BQS), F32)
        mr, mc = _tile_iota(TKS, TKS)
        after = jnp.where(mc > mr, 1.0, 0.0).astype(BF16)

        def tile(j, masked):
            off = pl.multiple_of(j * TKS, TKS)
            vis = _sb_vis(i, j) if masked else None
            zl = [_sb_tile_logs(q_ref[:, _hs(hh)], k_ref[pl.ds(off, TKS), _hs(hh)], vis) for hh in range(hb)]
            laters = [_sb_later(zl[hh][1], after) for hh in range(hb)]
            ws = []
            for hh in range(hb):
                z, ls = zl[hh]
                w = jnp.exp(z + ls + laters[hh] + r_sc[hh])
                if masked:
                    w = jnp.where(vis, w, 0.0)
                ws.append(w.astype(BF16))
                r_sc[hh] += jnp.sum(ls, axis=0, keepdims=True)
            for hh in range(hb):
                acc_sc[hh] += _bdot(vt_ref[j, _hs(hh), :], ws[hh])

        for d in range(per):
            tile(per * i + (per - 1 - d), True)

        def loop(jj, carry):
            tile(per * i - 1 - jj, False)
            return carry

        lax.fori_loop(0, per * i, loop, 0)
        for hh in range(hb):
            o_ref[:, _hs(hh)] = acc_sc[hh].T
            tot_ref[hh, 0] = r_sc[hh]

    return pl.pallas_call(
        body, name=name, grid=(HEADS // hb, nq),
        in_specs=[_q_spec(hb, BQS), _kv_spec(t, 1, hb), _kvt_spec(t // TKS, TKS, 1, hb)],
        out_specs=[_q_spec(hb, BQS), _qrow_spec(hb, BQS)],
        out_shape=[jax.ShapeDtypeStruct((t, HP), F32), jax.ShapeDtypeStruct((HEADS, nq, 1, BQS), F32)],
        scratch_shapes=[pltpu.VMEM((hb, LANES, BQS), F32), pltpu.VMEM((hb, 1, BQS), F32)],
        compiler_params=_cp("parallel", "arbitrary"))(qkv, qkv, kvt)


def _sb_bwd(qkv, kvt, do, tot, name):
    t = qkv.shape[0]
    nq = t // BQS
    per = BQS // TKS
    hb = HB_BWD

    def body(q_ref, k_ref, v_ref, kt_ref, do_ref, tot_ref, dq_ref, dk_ref, dv_ref, dq_sc, p_sc, g_sc):
        i = pl.program_id(1)

        @pl.when(i == 0)
        def _():
            dk_ref[...] = jnp.zeros_like(dk_ref)
            dv_ref[...] = jnp.zeros_like(dv_ref)

        dq_sc[...] = jnp.zeros((hb, LANES, BQS), F32)
        p_sc[...] = jnp.zeros((hb, 1, BQS), F32)
        g_sc[...] = jnp.zeros((hb, 1, BQS), F32)
        mr, mc = _tile_iota(TKS, TKS)
        after = jnp.where(mc > mr, 1.0, 0.0).astype(BF16)
        before = jnp.where(mc < mr, 1.0, 0.0).astype(BF16)

        def tile(j, masked):
            off = pl.multiple_of(j * TKS, TKS)
            vis = _sb_vis(i, j) if masked else None
            qs = [q_ref[:, _hs(hh)] for hh in range(hb)]
            dobs = [do_ref[:, _hs(hh)].astype(BF16) for hh in range(hb)]
            zl = [_sb_tile_logs(qs[hh], k_ref[pl.ds(off, TKS), _hs(hh)], vis) for hh in range(hb)]
            dws = [_bdot(v_ref[pl.ds(off, TKS), _hs(hh)], dobs[hh], NT) for hh in range(hb)]
            laters = [_sb_later(zl[hh][1], after) for hh in range(hb)]
            ws, gs = [], []
            for hh in range(hb):
                z, ls = zl[hh]
                own = jnp.sum(ls, axis=0, keepdims=True)
                rest = tot_ref[hh, 0] - p_sc[hh] - own
                w = jnp.exp(z + ls + laters[hh] + rest)
                if masked:
                    w = jnp.where(vis, w, 0.0)
                p_sc[hh] += own
                ws.append(w.astype(BF16))
                gs.append(dws[hh] * w)
            gins = []
            for hh in range(hb):
                ghi, glo = _split2(gs[hh])
                gins.append(_bdot(before, ghi) + _bdot(before, glo))
            dzbs = []
            for hh in range(hb):
                g = gs[hh]
                stay = jnp.exp(zl[hh][1])
                dz = g * stay - (1.0 - stay) * (gins[hh] + g_sc[hh])
                if masked:
                    dz = jnp.where(vis, dz, 0.0)
                g_sc[hh] += jnp.sum(g, axis=0, keepdims=True)
                dzbs.append(dz.astype(BF16))
            for hh in range(hb):
                sl = _hs(hh)
                dv_ref[pl.ds(off, TKS), sl] += _bdot(ws[hh], dobs[hh])
                dk_ref[pl.ds(off, TKS), sl] += _bdot(dzbs[hh], qs[hh])
                dq_sc[hh] += _bdot(kt_ref[j, sl, :], dzbs[hh])

        def loop(j, carry):
            tile(j, False)
            return carry

        lax.fori_loop(0, per * i, loop, 0)
        for d in range(per):
            tile(per * i + d, True)
        for hh in range(hb):
            dq_ref[:, _hs(hh)] = dq_sc[hh].T

    return pl.pallas_call(
        body, name=name, grid=(HEADS // hb, nq),
        in_specs=[_q_spec(hb, BQS), _kv_spec(t, 1, hb), _kv_spec(t, 2, hb), _kvt_spec(t // TKS, TKS, 0, hb),
                  _q_spec(hb, BQS), _qrow_spec(hb, BQS)],
        out_specs=[_q_spec(hb, BQS), _acc_spec(t, hb), _acc_spec(t, hb)],
        out_shape=[jax.ShapeDtypeStruct((t, HP), F32)] * 3,
        scratch_shapes=[pltpu.VMEM((hb, LANES, BQS), F32), pltpu.VMEM((hb, 1, BQS), F32),
                        pltpu.VMEM((hb, 1, BQS), F32)],
        compiler_params=_cp("parallel", "arbitrary"))(qkv, qkv, qkv, kvt, do, tot)


def _sigmoid(v):
    return 1.0 / (1.0 + jnp.exp(-v))


def _post_fwd(oa, ob, oc, od, proj, g_pad, name):
    t = oa.shape[0]

    def body(oa_ref, ob_ref, oc_ref, od_ref, rg_ref, g_ref, mx_ref):
        g = g_ref[...]

        def group(o, gg):
            r = lax.rsqrt(jnp.sum(o * o, axis=-1, keepdims=True) * (1.0 / GROUP) + EPS)
            return (o * r * gg).astype(BF16)

        mx_ref[:, 0:HP] = group(oa_ref[...], g[:, 0:HP])
        mx_ref[:, HP:2 * HP] = group(ob_ref[...], g[:, HP:2 * HP])
        mx_ref[:, 3 * HP:4 * HP] = group(od_ref[...], g[:, 3 * HP:4 * HP])
        real = lax.broadcasted_iota(jnp.int32, (TM, LANES), 1) < HEAD_DIM
        for hb in range(HEADS):
            sl = slice(hb * LANES, (hb + 1) * LANES)
            o = oc_ref[:, sl]
            mu = jnp.sum(o, axis=-1, keepdims=True) * (1.0 / HEAD_DIM)
            dlt = jnp.where(real, o - mu, 0.0)
            var = jnp.sum(dlt * dlt, axis=-1, keepdims=True) * (1.0 / HEAD_DIM)
            yn = dlt * lax.rsqrt(var + EPS) * g[:, 2 * HP + hb * LANES:2 * HP + (hb + 1) * LANES]
            x = rg_ref[:, sl]
            mx_ref[:, 2 * HP + hb * LANES:2 * HP + (hb + 1) * LANES] = (yn * (x * _sigmoid(x))).astype(BF16)

    rg_spec = pl.BlockSpec((TM, HP), lambda i: (i, OFF_RG // HP))
    return pl.pallas_call(
        body, name=name, grid=(t // TM,),
        in_specs=[_row_spec(HP)] * 4 + [rg_spec, _vec_spec(4 * HP)], out_specs=_row_spec(4 * HP),
        out_shape=jax.ShapeDtypeStruct((t, 4 * HP), BF16), compiler_params=_cp("parallel"))(oa, ob, oc, od, proj, g_pad)


def _post_bwd(dmx, oa, ob, oc, od, proj, g_pad, name):
    t = oa.shape[0]

    def body(dm_ref, oa_ref, ob_ref, oc_ref, od_ref, rg_ref, g_ref,
             doa_ref, dob_ref, doc_ref, dod_ref, dla_ref, dlb_ref, drg_ref, dg_ref):
        @pl.when(pl.program_id(0) == 0)
        def _():
            dg_ref[...] = jnp.zeros_like(dg_ref)

        g = g_ref[...]

        def group_bwd(dm, o, gg):
            r = lax.rsqrt(jnp.sum(o * o, axis=-1, keepdims=True) * (1.0 / GROUP) + EPS)
            oh = o * r
            dgp = jnp.sum(dm * oh, axis=0, keepdims=True)
            dyh = dm * gg
            do = r * (dyh - oh * (jnp.sum(dyh * oh, axis=-1, keepdims=True) * (1.0 / GROUP)))
            return do, dgp

        def delta_bc(do, o):
            prod = do * o
            lane = lax.broadcasted_iota(jnp.int32, (TM, LANES), 1)
            out = jnp.zeros((TM, LANES), F32)
            for hb in range(HEADS):
                out = jnp.where(lane == hb, jnp.sum(prod[:, hb * LANES:(hb + 1) * LANES], axis=-1, keepdims=True), out)
            return out

        oa = oa_ref[...]
        do_a, dga = group_bwd(dm_ref[:, 0:HP], oa, g[:, 0:HP])
        doa_ref[...] = do_a
        dla_ref[...] = delta_bc(do_a, oa)
        dg_ref[:, 0:HP] += dga
        ob = ob_ref[...]
        do_b, dgb = group_bwd(dm_ref[:, HP:2 * HP], ob, g[:, HP:2 * HP])
        dob_ref[...] = do_b
        dlb_ref[...] = delta_bc(do_b, ob)
        dg_ref[:, HP:2 * HP] += dgb
        do_d, dgd = group_bwd(dm_ref[:, 3 * HP:4 * HP], od_ref[...], g[:, 3 * HP:4 * HP])
        dod_ref[...] = do_d
        dg_ref[:, 3 * HP:4 * HP] += dgd
        real = lax.broadcasted_iota(jnp.int32, (TM, LANES), 1) < HEAD_DIM
        for hb in range(HEADS):
            sl = slice(hb * LANES, (hb + 1) * LANES)
            gsl = slice(2 * HP + hb * LANES, 2 * HP + (hb + 1) * LANES)
            o = oc_ref[:, sl]
            mu = jnp.sum(o, axis=-1, keepdims=True) * (1.0 / HEAD_DIM)
            dlt = jnp.where(real, o - mu, 0.0)
            var = jnp.sum(dlt * dlt, axis=-1, keepdims=True) * (1.0 / HEAD_DIM)
            rstd = lax.rsqrt(var + EPS)
            dhat = dlt * rstd
            gc = g[:, gsl]
            x = rg_ref[:, sl]
            sg = _sigmoid(x)
            dm = dm_ref[:, gsl]
            drg_ref[:, sl] = dm * (dhat * gc) * (sg * (1.0 + x * (1.0 - sg)))
            dyn = dm * (x * sg)
            dg_ref[:, gsl] += jnp.sum(dyn * dhat, axis=0, keepdims=True)
            ddh = dyn * gc
            m1 = jnp.sum(ddh, axis=-1, keepdims=True) * (1.0 / HEAD_DIM)
            m2 = jnp.sum(ddh * dhat, axis=-1, keepdims=True) * (1.0 / HEAD_DIM)
            doc_ref[:, sl] = jnp.where(real, rstd * (ddh - m1 - dhat * m2), 0.0)

    rg_spec = pl.BlockSpec((TM, HP), lambda i: (i, OFF_RG // HP))
    hp = _row_spec(HP)
    return pl.pallas_call(
        body, name=name, grid=(t // TM,),
        in_specs=[_row_spec(4 * HP), hp, hp, hp, hp, rg_spec, _vec_spec(4 * HP)],
        out_specs=[hp] * 4 + [_row_spec(LANES)] * 2 + [hp, _vec_spec(4 * HP)],
        out_shape=[jax.ShapeDtypeStruct((t, HP), F32)] * 4 + [jax.ShapeDtypeStruct((t, LANES), F32)] * 2
        + [jax.ShapeDtypeStruct((t, HP), F32), jax.ShapeDtypeStruct((1, 4 * HP), F32)],
        compiler_params=_cp("arbitrary"))(dmx, oa, ob, oc, od, proj, g_pad)


def _mesh_pos():
    return lax.axis_index("x"), lax.axis_index("y"), lax.axis_index("c")


def _peer(pos, k):
    x, y, c = pos
    px = 1 - x if (k >> 2) & 1 else x
    py = 1 - y if (k >> 1) & 1 else y
    pc = 1 - c if k & 1 else c
    return (px, py, pc), 4 * px + 2 * py + pc


def _exchange(arrs, gather, name):
    n = len(arrs)

    def body(*refs):
        ins, outs = refs[:n], refs[n:2 * n]
        send_sems, recv_sems, loc_sems = refs[2 * n:]
        pos = _mesh_pos()
        me = 4 * pos[0] + 2 * pos[1] + pos[2]
        local = []
        for a in range(n):
            src = ins[a] if gather else ins[a].at[me]
            cp = pltpu.make_async_copy(src, outs[a].at[me], loc_sems.at[a])
            cp.start()
            local.append(cp)
        sends, recvs = [], []
        for k in range(1, N_DEV):
            peer, pid = _peer(pos, k)
            for a in range(n):
                s = a * (N_DEV - 1) + k - 1
                src = ins[a] if gather else ins[a].at[pid]
                cp = pltpu.make_async_remote_copy(
                    src_ref=src, dst_ref=outs[a].at[me], send_sem=send_sems.at[s], recv_sem=recv_sems.at[s],
                    device_id=peer, device_id_type=pl.DeviceIdType.MESH)
                cp.start()
                sends.append(cp)
                recvs.append(pltpu.make_async_remote_copy(
                    src_ref=src, dst_ref=outs[a].at[pid], send_sem=send_sems.at[s], recv_sem=recv_sems.at[s],
                    device_id=peer, device_id_type=pl.DeviceIdType.MESH))
        for cp in recvs:
            cp.wait_recv()
        for cp in sends:
            cp.wait_send()
        for cp in local:
            cp.wait()

    any_spec = pl.BlockSpec(memory_space=pl.ANY)
    out_shape = [jax.ShapeDtypeStruct((N_DEV,) + tuple(a.shape) if gather else tuple(a.shape), a.dtype) for a in arrs]
    return pl.pallas_call(
        body, name=name, in_specs=[any_spec] * n, out_specs=[any_spec] * n, out_shape=out_shape,
        scratch_shapes=[pltpu.SemaphoreType.DMA((n * (N_DEV - 1),)), pltpu.SemaphoreType.DMA((n * (N_DEV - 1),)),
                        pltpu.SemaphoreType.DMA((n,))],
        compiler_params=pltpu.CompilerParams(has_side_effects=True))(*arrs)


def _device_index():
    x, y, c = _mesh_pos()
    return 4 * x + 2 * y + c


def _landing(srcs, gather):
    me = _device_index()
    lands = []
    for a in srcs:
        own = a[None] if gather else lax.dynamic_slice_in_dim(a, me, 1, axis=0)
        shape = (N_DEV,) + tuple(a.shape) if gather else tuple(a.shape)
        lands.append(lax.dynamic_update_slice_in_dim(jnp.zeros(shape, a.dtype), own, me, axis=0))
    return lands


def _exchange_copies(ins, lands, send_sems, recv_sems, gather):
    pos = _mesh_pos()
    me = 4 * pos[0] + 2 * pos[1] + pos[2]
    sends, recvs = [], []
    for k in range(1, N_DEV):
        peer, pid = _peer(pos, k)
        for a in range(len(ins)):
            s = a * (N_DEV - 1) + k - 1
            src = ins[a] if gather else ins[a].at[pid]
            sends.append(pltpu.make_async_remote_copy(
                src_ref=src, dst_ref=lands[a].at[me], send_sem=send_sems.at[s], recv_sem=recv_sems.at[s],
                device_id=peer, device_id_type=pl.DeviceIdType.MESH))
            recvs.append(pltpu.make_async_remote_copy(
                src_ref=src, dst_ref=lands[a].at[pid], send_sem=send_sems.at[s], recv_sem=recv_sems.at[s],
                device_id=peer, device_id_type=pl.DeviceIdType.MESH))
    return sends, recvs


def _exchange_start(srcs, gather, name):
    n = len(srcs)
    lands = _landing(srcs, gather)
    nsem = n * (N_DEV - 1)

    def body(*refs):
        ins, lnd = refs[:n], refs[n:2 * n]
        send_sems, recv_sems = refs[2 * n], refs[2 * n + 1]
        token = refs[-1]
        sends, _ = _exchange_copies(ins, lnd, send_sems, recv_sems, gather)
        for cp in sends:
            cp.start()
        token[...] = jnp.zeros_like(token)

    hbm = pl.BlockSpec(memory_space=pltpu.HBM)
    sem = pl.BlockSpec(memory_space=pltpu.SEMAPHORE)
    bufs = list(srcs) + lands
    out_shape = ([pltpu.SemaphoreType.DMA((nsem,)), pltpu.SemaphoreType.DMA((nsem,))]
                 + [pltpu.HBM(b.shape, b.dtype) for b in bufs] + [jax.ShapeDtypeStruct((8, LANES), F32)])
    outs = pl.pallas_call(
        body, name=name, in_specs=[hbm] * (2 * n),
        out_specs=[sem, sem] + [hbm] * (2 * n) + [pl.BlockSpec(memory_space=pltpu.VMEM)], out_shape=out_shape,
        input_output_aliases={i: 2 + i for i in range(2 * n)},
        compiler_params=pltpu.CompilerParams(has_side_effects=pltpu.SideEffectType.DATAFLOW_SIDE_EFFECTING),
    )(*[pltpu.with_memory_space_constraint(b, pltpu.HBM) for b in bufs])
    return (outs[0], outs[1], outs[2:2 + n], outs[2 + n:2 + 2 * n]), outs[-1]


def _exchange_wait(state, after, gather, name):
    send_sems, recv_sems, srcs, lands = state
    n = len(srcs)

    def body(*refs):
        ins, lnd = refs[:n], refs[n:2 * n]
        s_sems, r_sems = refs[2 * n], refs[2 * n + 1]
        sends, recvs = _exchange_copies(ins, lnd, s_sems, r_sems, gather)
        for cp in sends:
            cp.wait_send()
        for cp in recvs:
            cp.wait_recv()

    hbm = pl.BlockSpec(memory_space=pltpu.HBM)
    sem = pl.BlockSpec(memory_space=pltpu.SEMAPHORE)
    bufs = list(srcs) + list(lands)
    outs = pl.pallas_call(
        body, name=name, in_specs=[hbm] * (2 * n) + [sem, sem, pl.BlockSpec(memory_space=pl.ANY)],
        out_specs=[hbm] * (2 * n), out_shape=[pltpu.HBM(b.shape, b.dtype) for b in bufs],
        input_output_aliases={i: i for i in range(2 * n)},
        compiler_params=pltpu.CompilerParams(has_side_effects=pltpu.SideEffectType.DATAFLOW_SIDE_EFFECTING),
    )(*bufs, send_sems, recv_sems, after)
    return outs[n:]


def _adam_vals(w, g, m, v):
    m = ADAM_B1 * m + (1.0 - ADAM_B1) * g
    v = ADAM_B2 * v + (1.0 - ADAM_B2) * (g * g)
    m_hat = m / ADAM_C1
    v_hat = v / ADAM_C2
    delta = -ADAM_LR * (m_hat / (jnp.sqrt(v_hat) + ADAM_EPS) + ADAM_WD * w)
    return delta, m, v


def _small_allreduce_adam(part, w, m, v, name):
    rows = part.shape[0]

    def body(p_ref, w_ref, m_ref, v_ref, g_ref, d_ref, nm_ref, nv_ref, gath, send_sems, recv_sems):
        pos = _mesh_pos()
        me = 4 * pos[0] + 2 * pos[1] + pos[2]
        gath[me] = p_ref[...]
        sends, recvs = [], []
        for k in range(1, N_DEV):
            peer, pid = _peer(pos, k)
            cp = pltpu.make_async_remote_copy(
                src_ref=p_ref, dst_ref=gath.at[me], send_sem=send_sems.at[k - 1], recv_sem=recv_sems.at[k - 1],
                device_id=peer, device_id_type=pl.DeviceIdType.MESH)
            cp.start()
            sends.append(cp)
            recvs.append(pltpu.make_async_remote_copy(
                src_ref=p_ref, dst_ref=gath.at[pid], send_sem=send_sems.at[k - 1], recv_sem=recv_sems.at[k - 1],
                device_id=peer, device_id_type=pl.DeviceIdType.MESH))
        for cp in recvs:
            cp.wait_recv()
        for cp in sends:
            cp.wait_send()
        g = gath[0]
        for p in range(1, N_DEV):
            g = g + gath[p]
        g_ref[...] = g
        d, nm, nv = _adam_vals(w_ref[...], g, m_ref[...], v_ref[...])
        d_ref[...] = d
        nm_ref[...] = nm
        nv_ref[...] = nv

    vm = pl.BlockSpec(memory_space=pltpu.VMEM)
    sds = jax.ShapeDtypeStruct((rows, LANES), F32)
    return pl.pallas_call(
        body, name=name, in_specs=[vm] * 4, out_specs=[vm] * 4, out_shape=[sds] * 4,
        scratch_shapes=[pltpu.VMEM((N_DEV, rows, LANES), F32), pltpu.SemaphoreType.DMA((N_DEV - 1,)),
                        pltpu.SemaphoreType.DMA((N_DEV - 1,))],
        compiler_params=pltpu.CompilerParams(has_side_effects=True))(part, w, m, v)


def _reduce_adam(recv, w, m, v, name):
    shape = w.shape
    c = shape[-1]
    r = int(np.prod(shape[:-1]))
    recv2, w2, m2, v2 = recv.reshape(N_DEV, r, c), w.reshape(r, c), m.reshape(r, c), v.reshape(r, c)
    tr = r
    while tr * c * 4 > (1 << 20) and tr % 16 == 0:
        tr //= 2

    def body(r_ref, w_ref, m_ref, v_ref, g_ref, d_ref, nm_ref, nv_ref):
        g = r_ref[0].astype(F32)
        for p in range(1, N_DEV):
            g = g + r_ref[p].astype(F32)
        g_ref[...] = g
        d, nm, nv = _adam_vals(w_ref[...], g, m_ref[...], v_ref[...])
        d_ref[...] = d
        nm_ref[...] = nm
        nv_ref[...] = nv

    spec = pl.BlockSpec((tr, c), lambda i: (i, 0))
    sds = jax.ShapeDtypeStruct((r, c), F32)
    outs = pl.pallas_call(
        body, name=name, grid=(r // tr,),
        in_specs=[pl.BlockSpec((N_DEV, tr, c), lambda i: (0, i, 0)), spec, spec, spec],
        out_specs=[spec] * 4, out_shape=[sds] * 4, compiler_params=_cp("parallel"))(recv2, w2, m2, v2)
    return [o.reshape(shape) for o in outs]


def _pad_heads(w, real=HEAD_DIM):
    lead = w.shape[:-1]
    w = w.reshape(lead + (HEADS, real))
    w = jnp.pad(w, [(0, 0)] * len(lead) + [(0, 0), (0, LANES - real)])
    return w.reshape(lead + (HP,))


def _unpad_heads(w, real=HEAD_DIM):
    lead = w.shape[:-1]
    return w.reshape(lead + (HEADS, LANES))[..., :real].reshape(lead + (HEADS * real,))


_IN_SEGS = (("fq", 0, 256), ("fk", 256, 512), ("fv", 512, 768), ("ff", 768, 772), ("cq", 772, 1028),
            ("ckv", 1028, 1156), ("kr", 1156, 1188), ("rq", 1188, 1444), ("rk", 1444, 1700), ("rv", 1700, 1956),
            ("rg", 1956, 2212), ("sq", 2212, 2468), ("sk", 2468, 2724), ("sv", 2724, 2980))


def _pad_w_in(w):
    s = {n: w[:, a:b] for n, a, b in _IN_SEGS}
    rows = w.shape[0]
    z = lambda n: jnp.zeros((rows, n), w.dtype)
    parts = [_pad_heads(s[n]) for n in ("fq", "fk", "fv", "rq", "rk", "rv", "sq", "sk", "sv", "rg")]
    parts += [s["cq"], s["ckv"], z(HEAD_DIM), s["kr"], z(LANES - HEAD_DIM - ROPE_MLA), s["ff"], z(LANES - HEADS),
              z(NP_IN - OFF_FF - LANES)]
    return jnp.concatenate(parts, axis=1)


def _unpad_w_in(wp):
    def heads(off):
        return _unpad_heads(wp[:, off:off + HP])

    parts = [heads(OFF_FOX), heads(OFF_FOX + HP), heads(OFF_FOX + 2 * HP), wp[:, OFF_FF:OFF_FF + HEADS],
             wp[:, OFF_CQ:OFF_CQ + Q_RANK], wp[:, OFF_CKV:OFF_CKV + KV_RANK],
             wp[:, OFF_KR + HEAD_DIM:OFF_KR + HEAD_DIM + ROPE_MLA],
             heads(OFF_RET), heads(OFF_RET + HP), heads(OFF_RET + 2 * HP), heads(OFF_RG),
             heads(OFF_SB), heads(OFF_SB + HP), heads(OFF_SB + 2 * HP)]
    return jnp.concatenate(parts, axis=1)


def _pad_w_kv(w):
    w4 = w.reshape(KV_RANK, HEADS, 2 * HEAD_DIM)
    k = w4[:, :, :HEAD_DIM].reshape(KV_RANK, GROUP)
    v = w4[:, :, HEAD_DIM:].reshape(KV_RANK, GROUP)
    return jnp.concatenate([_pad_heads(k), _pad_heads(v)], axis=1)


def _unpad_w_kv(wp):
    k = _unpad_heads(wp[:, :HP]).reshape(KV_RANK, HEADS, HEAD_DIM)
    v = _unpad_heads(wp[:, HP:]).reshape(KV_RANK, HEADS, HEAD_DIM)
    return jnp.concatenate([k, v], axis=-1).reshape(KV_RANK, HEADS * 2 * HEAD_DIM)


def _pad_rows_out(w):
    w = w.reshape(4 * HEADS, HEAD_DIM, D_MODEL)
    w = jnp.pad(w, ((0, 0), (0, LANES - HEAD_DIM), (0, 0)))
    return w.reshape(4 * HP, D_MODEL)


def _unpad_rows_out(wp):
    return wp.reshape(4 * HEADS, LANES, D_MODEL)[:, :HEAD_DIM, :].reshape(D_MODEL, D_MODEL)


def _pad_gain_out(g):
    g = jnp.pad(g.reshape(4 * HEADS, HEAD_DIM), ((0, 0), (0, LANES - HEAD_DIM)))
    return g.reshape(1, 4 * HP)


def _unpad_gain_out(gp):
    return gp.reshape(4 * HEADS, LANES)[:, :HEAD_DIM].reshape(D_MODEL)


_SMALL = (("g_mix_pre", 1024), ("g_mix_post", 1024), ("g_ffn_pre", 1024), ("g_ffn_post", 1024), ("g_mix_out", 1024),
          ("g_q_lora", 256), ("g_kv_lora", 128), ("b_forget", 4))


def _pack_small(vals):
    parts = []
    for name, n in _SMALL:
        a = vals[name].astype(F32)
        if n < LANES:
            a = jnp.pad(a, ((0, 0), (0, LANES - n)))
        parts.append(a)
    return jnp.concatenate(parts, axis=1).reshape(DEPTH * SMALL_ROWS, LANES)


def _unpack_small(packed):
    flat = packed.reshape(DEPTH, SMALL_ROWS * LANES)
    out, off = {}, 0
    for name, n in _SMALL:
        out[name] = flat[:, off:off + n]
        off += max(n, LANES)
    return out


def kernel(x, positions, g_mix_pre, w_in, b_forget, g_q_lora, w_q_up, g_kv_lora, w_kv_up, g_mix_out, w_out, g_mix_post, g_ffn_pre, w_ffn_up, w_ffn_down, g_ffn_post, loss_target, m_g_mix_pre, m_w_in, m_b_forget, m_g_q_lora, m_w_q_up, m_g_kv_lora, m_w_kv_up, m_g_mix_out, m_w_out, m_g_mix_post, m_g_ffn_pre, m_w_ffn_up, m_w_ffn_down, m_g_ffn_post, v_g_mix_pre, v_w_in, v_b_forget, v_g_q_lora, v_w_q_up, v_g_kv_lora, v_w_kv_up, v_g_mix_out, v_w_out, v_g_mix_post, v_g_ffn_pre, v_w_ffn_up, v_w_ffn_down, v_g_ffn_post):
    t = x.shape[1]
    nq = t // BQ
    x0 = x[0]
    tgt = loss_target[0]
    pos = positions[0].astype(F32).reshape(t, 1)

    half_r, half_m = HEAD_DIM // 2, ROPE_MLA // 2
    invf_r = ROPE_BASE ** (-jnp.arange(half_r, dtype=F32) / half_r)
    invf_m = ROPE_BASE ** (-jnp.arange(half_m, dtype=F32) / half_m)
    blk_r = jnp.concatenate([invf_r, invf_r, jnp.zeros((LANES - HEAD_DIM,), F32)])
    blk_m = jnp.concatenate([jnp.zeros((HEAD_DIM,), F32), invf_m, invf_m, jnp.zeros((LANES - HEAD_DIM - ROPE_MLA,), F32)])
    invf_ret = jnp.tile(blk_r, HEADS).reshape(1, HP)
    invf_mla = jnp.tile(blk_m, HEADS).reshape(1, HP)
    log_gamma = jnp.log1p(-jnp.power(2.0, -5.0 - jnp.arange(HEADS, dtype=F32)))
    lg_lanes = jnp.repeat(log_gamma, LANES).reshape(1, HP)
    lg_heads = jnp.broadcast_to(log_gamma[:, None, None], (HEADS, 1, LANES))

    big = [w_in, w_q_up, w_kv_up, w_out, w_ffn_up, w_ffn_down]
    bf = lambda w: w.astype(BF16)
    first = _exchange([bf(w_in[0]), bf(w_q_up), bf(w_kv_up)], True, "weights_gather_first")
    rest_state, rest_token = _exchange_start([bf(w_in[1]), bf(w_out), bf(w_ffn_up), bf(w_ffn_down)], True,
                                             "weights_gather_start")
    wq_g = first[1].transpose(1, 2, 0, 3).reshape(DEPTH, Q_RANK, 384)
    wkv_g = first[2].transpose(1, 2, 0, 3).reshape(DEPTH, KV_RANK, 512)

    row = lambda g: g.reshape(1, -1)
    layers = []
    for l in range(DEPTH):
        layers.append(dict(
            wq=_pad_heads(wq_g[l], 96), wkv=_pad_w_kv(wkv_g[l]),
            g_pre=row(g_mix_pre[l]), g_post=row(g_mix_post[l]), g_fpre=row(g_ffn_pre[l]), g_fpost=row(g_ffn_post[l]),
            g_out=_pad_gain_out(g_mix_out[l]), g_q=row(g_q_lora[l]), g_kv=row(g_kv_lora[l]),
            b_pad=jnp.pad(b_forget[l], (0, LANES - HEADS)).reshape(1, LANES)))
    layers[0]["win"] = _pad_w_in(first[0].reshape(D_MODEL, D_IN))

    saved = []
    xin = x0
    h = _rms_fwd(xin, layers[0]["g_pre"] + rest_token[0:1, 0:1], "rms_pre_0")
    loss_row = dx = None
    for l, p in enumerate(layers):
        s = dict(x=xin, h=h)
        proj = _mm(h, p["win"], name=f"in_proj_{l}")
        cum = _cumsum(_forget_lsf(proj, p["b_pad"], f"forget_lsf_{l}"), False, f"forget_cumsum_{l}")
        fox, ret, sb, mla, fox_t, sb_t, mla_t, ret_t = _prep_fwd(proj, cum, pos, invf_ret, invf_mla, lg_lanes, p["g_q"],
                                                          p["g_kv"], p["wq"], p["wkv"], f"prep_fwd_{l}")
        oa, lse_a = _softmax_fwd(fox, fox_t, chunk_mask=False, scale=1.0, name=f"fox_fwd_{l}")
        ob, lse_b = _softmax_fwd(mla, mla_t, chunk_mask=True, scale=96.0 ** -0.5, name=f"mla_fwd_{l}")
        oc = _ret_fwd(ret, lg_heads, f"ret_fwd_{l}")
        od, sb_tot = _sb_fwd(sb, sb_t, f"sb_fwd_{l}")
        if l == 0:
            rest = _exchange_wait(rest_state, od, True, "weights_gather_wait")
            wout_g = rest[1].transpose(1, 0, 2, 3).reshape(DEPTH, D_MODEL, D_MODEL)
            wup_g = rest[2].transpose(1, 2, 0, 3).reshape(DEPTH, D_MODEL, D_FF)
            wdn_g = rest[3].transpose(1, 0, 2, 3).reshape(DEPTH, D_FF, D_MODEL)
            layers[1]["win"] = _pad_w_in(rest[0].reshape(D_MODEL, D_IN))
            for ll in range(DEPTH):
                layers[ll].update(wout=_pad_rows_out(wout_g[ll]), wup=wup_g[ll], wdn=wdn_g[ll])
        mixed = _post_fwd(oa, ob, oc, od, proj, p["g_out"], f"post_fwd_{l}")
        mix = _mm(mixed, p["wout"], name=f"out_proj_{l}", tk=2048)
        x1, h2 = _add_rms_fwd(xin, mix, p["g_post"], p["g_fpre"], f"mix_residual_{l}")
        a = _mm(h2, p["wup"], name=f"ffn_up_{l}")
        y = _mm(a, p["wdn"], name=f"ffn_down_{l}", a_fn=_relu2)
        s.update(proj=proj, fox=fox, ret=ret, sb=sb, mla=mla, fox_t=fox_t, sb_t=sb_t, mla_t=mla_t, ret_t=ret_t, oa=oa, ob=ob, oc=oc,
                 od=od, sb_tot=sb_tot, lse_a=lse_a, lse_b=lse_b, mixed=mixed, mix=mix, x1=x1, h2=h2, a=a, y=y)
        saved.append(s)
        if l + 1 < DEPTH:
            xin, h = _add_rms_fwd(x1, y, p["g_fpost"], layers[l + 1]["g_pre"], f"ffn_residual_{l}")
        else:
            loss_row, dx = _final_loss(x1, y, p["g_fpost"], tgt, "loss")
    loss = lax.psum(loss_row[0, 0], ("x", "y", "c"))

    small_g = {n: [None] * DEPTH for n, _ in _SMALL}
    big_g = [[None] * DEPTH for _ in range(6)]
    to_send = [
        lambda g: g.reshape(N_DEV, 1, D_MODEL // N_DEV, D_IN),
        lambda g: g.reshape(Q_RANK, N_DEV, 384 // N_DEV).transpose(1, 0, 2)[:, None],
        lambda g: g.reshape(KV_RANK, N_DEV, 512 // N_DEV).transpose(1, 0, 2)[:, None],
        lambda g: g.reshape(N_DEV, 1, D_MODEL // N_DEV, D_MODEL),
        lambda g: g.reshape(D_MODEL, N_DEV, D_FF // N_DEV).transpose(1, 0, 2)[:, None],
        lambda g: g.reshape(N_DEV, 1, D_FF // N_DEV, D_MODEL),
    ]
    send_of = lambda ks, l: [to_send[k](big_g[k][l]).astype(BF16) for k in ks]
    late_state = early_state = None
    order_token = jnp.zeros((1, 1), F32)
    for l in reversed(range(DEPTH)):
        p, s = layers[l], saved[l]
        dy, dg = _norm_bwd(dx, s["y"], p["g_fpost"] + order_token, None, BF16, f"ffn_post_bwd_{l}")
        small_g["g_ffn_post"][l] = dg
        da = _mm(dy, p["wdn"], name=f"ffn_down_dx_{l}", tb=True, out_dtype=BF16, epi=_drelu2, epi_in=s["a"])
        big_g[5][l] = _mm(s["a"], dy, name=f"ffn_down_dw_{l}", ta=True, a_fn=_relu2, tk=2048)
        big_g[4][l] = _mm(s["h2"], da, name=f"ffn_up_dw_{l}", ta=True, tk=2048)
        dh2 = _mm(da, p["wup"], name=f"ffn_up_dx_{l}", tb=True)
        dx1, dg = _norm_bwd(dh2, s["x1"], p["g_fpre"], dx, F32, f"ffn_pre_bwd_{l}")
        small_g["g_ffn_pre"][l] = dg
        dmix, dg = _norm_bwd(dx1, s["mix"], p["g_post"], None, BF16, f"mix_post_bwd_{l}")
        small_g["g_mix_post"][l] = dg
        dmixed = _mm(dmix, p["wout"], name=f"out_proj_dx_{l}", tb=True)
        big_g[3][l] = _unpad_rows_out(_mm(s["mixed"], dmix, name=f"out_proj_dw_{l}", ta=True, tk=2048))
        g_out = p["g_out"]
        if l == 0:
            early_state, early_token = _exchange_start(send_of((3, 4, 5), 0), False, "grads_layer0_early_start")
            g_out = g_out + early_token[0:1, 0:1]
        doa, dob, doc, dod, dla, dlb, drg, dgo = _post_bwd(dmixed, s["oa"], s["ob"], s["oc"], s["od"], s["proj"],
                                                           g_out, f"post_bwd_{l}")
        small_g["g_mix_out"][l] = _unpad_gain_out(dgo).reshape(1, D_MODEL)
        as_rows = lambda a: a[:, :HEADS].T.reshape(HEADS, nq, 1, BQ)
        dfq, dfk, dfv, dcum_k, dcum_q = _softmax_bwd(s["fox"], s["fox_t"], doa, s["lse_a"], as_rows(dla), bias=True,
                                                     chunk_mask=False, scale=1.0, name=f"fox_bwd_{l}")
        dmq, dmk, dmv = _softmax_bwd(s["mla"], s["mla_t"], dob, s["lse_b"], as_rows(dlb), bias=False, chunk_mask=True,
                                     scale=96.0 ** -0.5, name=f"mla_bwd_{l}")
        drq, drk, drv = _ret_bwd(s["ret"], s["ret_t"], lg_heads, doc, f"ret_bwd_{l}")
        dsq, dsk, dsv = _sb_bwd(s["sb"], s["sb_t"], dod, s["sb_tot"], f"sb_bwd_{l}")
        dcum_q = jnp.pad(dcum_q.reshape(HEADS, t).T, ((0, 0), (0, LANES - HEADS)))
        dlsf = _cumsum(dcum_q, True, f"forget_cumsum_bwd_{l}", partials=dcum_k)
        dproj, dwq, dwkv, dgq, dgkv, dbf = _prep_bwd(
            (dfq, dfk, dfv), (drq, drk, drv), (dsq, dsk, dsv), (dmq, dmk, dmv), drg, dlsf, s["proj"], pos, invf_ret,
            invf_mla, lg_lanes, p["b_pad"], p["g_q"], p["g_kv"], p["wq"], p["wkv"], f"prep_bwd_{l}")
        small_g["g_q_lora"][l] = dgq
        small_g["g_kv_lora"][l] = dgkv
        small_g["b_forget"][l] = dbf[:, :HEADS]
        big_g[1][l] = _unpad_heads(dwq, 96)
        big_g[2][l] = _unpad_w_kv(dwkv)
        big_g[0][l] = _unpad_w_in(_mm(s["h"], dproj, name=f"in_proj_dw_{l}", ta=True, tk=2048))
        dh = _mm(dproj, p["win"], name=f"in_proj_dx_{l}", tb=True)
        dx, dg = _norm_bwd(dh, s["x"], p["g_pre"], dx1, F32, f"mix_pre_bwd_{l}")
        small_g["g_mix_pre"][l] = dg
        if l == DEPTH - 1:
            late_state, late_token = _exchange_start(send_of(range(6), l), False, "grads_layer1_start")
            order_token = late_token[0:1, 0:1]
    grad_x = dx.reshape(1, t, D_MODEL)

    last = _exchange(send_of((0, 1, 2), 0), False, "grads_layer0_rest")
    late = _exchange_wait(late_state, dx, False, "grads_layer1_wait")
    early = _exchange_wait(early_state, dx, False, "grads_layer0_early_wait")
    recv = [jnp.concatenate([(last[k] if k < 3 else early[k - 3]), late[k]], axis=1) for k in range(6)]
    ms = [m_w_in, m_w_q_up, m_w_kv_up, m_w_out, m_w_ffn_up, m_w_ffn_down]
    vs = [v_w_in, v_w_q_up, v_w_kv_up, v_w_out, v_w_ffn_up, v_w_ffn_down]
    names = ["w_in", "w_q_up", "w_kv_up", "w_out", "w_ffn_up", "w_ffn_down"]
    res = {}
    for k, n in enumerate(names):
        res[n] = _reduce_adam(recv[k], big[k], ms[k], vs[k], f"adamw_{n}")

    small_w = dict(g_mix_pre=g_mix_pre, g_mix_post=g_mix_post, g_ffn_pre=g_ffn_pre, g_ffn_post=g_ffn_post,
                   g_mix_out=g_mix_out, g_q_lora=g_q_lora, g_kv_lora=g_kv_lora, b_forget=b_forget)
    small_m = dict(g_mix_pre=m_g_mix_pre, g_mix_post=m_g_mix_post, g_ffn_pre=m_g_ffn_pre, g_ffn_post=m_g_ffn_post,
                   g_mix_out=m_g_mix_out, g_q_lora=m_g_q_lora, g_kv_lora=m_g_kv_lora, b_forget=m_b_forget)
    small_v = dict(g_mix_pre=v_g_mix_pre, g_mix_post=v_g_mix_post, g_ffn_pre=v_g_ffn_pre, g_ffn_post=v_g_ffn_post,
                   g_mix_out=v_g_mix_out, g_q_lora=v_g_q_lora, g_kv_lora=v_g_kv_lora, b_forget=v_b_forget)
    part = _pack_small({n: jnp.concatenate(small_g[n], axis=0) for n, _ in _SMALL})
    sres = _small_allreduce_adam(part, _pack_small(small_w), _pack_small(small_m), _pack_small(small_v),
                                 "small_allreduce_adamw")
    sg, sd, sm, sv = [_unpack_small(a) for a in sres]
    for n, _ in _SMALL:
        res[n] = [sg[n], sd[n], sm[n], sv[n]]

    order = ["g_mix_pre", "w_in", "b_forget", "g_q_lora", "w_q_up", "g_kv_lora", "w_kv_up", "g_mix_out", "w_out",
             "g_mix_post", "g_ffn_pre", "w_ffn_up", "w_ffn_down", "g_ffn_post"]
    outs = [loss, grad_x]
    for idx in range(4):
        outs += [res[n][idx] for n in order]
    return tuple(outs)
```

```python
import functools
import math

import numpy as np
import jax
import jax.numpy as jnp
from jax import lax
from jax.experimental import pallas as pl
from jax.experimental.pallas import tpu as pltpu

F32 = jnp.float32
BF16 = jnp.bfloat16

D_MODEL = 1024
DEPTH = 2
N_DEV = 8
GROUP = 256
HEADS = 4
HEAD_DIM = 64
LANES = 128
HP = HEADS * LANES
QKV = 3 * HP
Q_RANK = 256
KV_RANK = 128
ROPE_MLA = 32
D_FF = 4096
D_IN = 2980
CHUNK_SHIFT = 6
EPS = 1e-6
ROPE_BASE = 10000.0
NEG = -1e30

OFF_FOX, OFF_RET, OFF_SB = 0, QKV, 2 * QKV
OFF_RG = 3 * QKV
OFF_CQ = OFF_RG + HP
OFF_CKV = OFF_CQ + Q_RANK
OFF_KR = OFF_CKV + LANES
OFF_FF = OFF_KR + LANES
NP_IN = 6144

BQ = 256
TKS = 128
TM = 256
VMEM_LIMIT = 48 * 1024 * 1024

ADAM_LR, ADAM_B1, ADAM_B2, ADAM_EPS, ADAM_WD, ADAM_STEP = 0.001, 0.9, 0.999, 1e-08, 0.01, 10
ADAM_C1 = 1.0 - ADAM_B1 ** ADAM_STEP
ADAM_C2 = 1.0 - ADAM_B2 ** ADAM_STEP

SMALL_ROWS = 44

NT = (((1,), (1,)), ((), ()))
TN = (((0,), (0,)), ((), ()))


def _cp(*sem):
    return pltpu.CompilerParams(dimension_semantics=sem if sem else None, vmem_limit_bytes=VMEM_LIMIT)


def _bdot(a, b, dn=None):
    if dn is None:
        return jnp.dot(a, b, preferred_element_type=F32)
    return lax.dot_general(a, b, dn, preferred_element_type=F32)


def _split2(x):
    hi = x.astype(BF16)
    lo = (x - hi.astype(F32)).astype(BF16)
    return hi, lo


def _mm(a, b, *, name, ta=False, tb=False, out_dtype=F32, a_fn=None, epi=None, epi_in=None,
        tm=1024, tn=1024, tk=1024):
    m, k = (a.shape[1], a.shape[0]) if ta else a.shape
    n = b.shape[0] if tb else b.shape[1]
    tm, tn, tk = min(tm, m), min(tn, n), min(tk, k)
    assert m % tm == 0 and n % tn == 0 and k % tk == 0, (name, m, n, k)
    nk = k // tk
    dn = (((0 if ta else 1,), (1 if tb else 0,)), ((), ()))

    def body(*refs):
        if epi is None:
            a_ref, b_ref, o_ref = refs[:3]
            e_ref = None
            rest = refs[3:]
        else:
            a_ref, b_ref, e_ref, o_ref = refs[:4]
            rest = refs[4:]
        av = a_ref[...]
        if a_fn is not None:
            av = a_fn(av)
        part = lax.dot_general(av.astype(BF16), b_ref[...].astype(BF16), dn, preferred_element_type=F32)

        def finish(r):
            if epi is not None:
                r = epi(r, e_ref[...])
            o_ref[...] = r.astype(out_dtype)

        if nk == 1:
            finish(part)
        else:
            acc_ref = rest[0]
            kk = pl.program_id(2)

            @pl.when(kk == 0)
            def _():
                acc_ref[...] = part

            @pl.when(kk > 0)
            def _():
                acc_ref[...] += part

            @pl.when(kk == nk - 1)
            def _():
                finish(acc_ref[...])

    a_spec = pl.BlockSpec((tk, tm), lambda i, j, kk: (kk, i)) if ta else pl.BlockSpec((tm, tk), lambda i, j, kk: (i, kk))
    b_spec = pl.BlockSpec((tn, tk), lambda i, j, kk: (j, kk)) if tb else pl.BlockSpec((tk, tn), lambda i, j, kk: (kk, j))
    o_spec = pl.BlockSpec((tm, tn), lambda i, j, kk: (i, j))
    in_specs = [a_spec, b_spec]
    args = [a, b]
    if epi is not None:
        in_specs.append(o_spec)
        args.append(epi_in)
    return pl.pallas_call(
        body, name=name, grid=(m // tm, n // tn, nk),
        in_specs=in_specs, out_specs=o_spec,
        out_shape=jax.ShapeDtypeStruct((m, n), out_dtype),
        scratch_shapes=[pltpu.VMEM((tm, tn), F32)] if nk > 1 else [],
        compiler_params=_cp("parallel", "parallel", "arbitrary"),
    )(*args)


def _relu2(v):
    r = jnp.maximum(v, 0.0)
    return r * r


def _drelu2(du, av):
    return du * (2.0 * jnp.maximum(av, 0.0))


def _rms(v, g):
    r = lax.rsqrt(jnp.mean(v * v, axis=-1, keepdims=True) + EPS)
    return v * r * g


def _row_spec(w):
    return pl.BlockSpec((TM, w), lambda i: (i, 0))


def _vec_spec(w):
    return pl.BlockSpec((1, w), lambda i: (0, 0))


def _rms_fwd(x, g, name):
    t, d = x.shape

    def body(x_ref, g_ref, h_ref):
        h_ref[...] = _rms(x_ref[...], g_ref[...]).astype(BF16)

    return pl.pallas_call(
        body, name=name, grid=(t // TM,), in_specs=[_row_spec(d), _vec_spec(d)], out_specs=_row_spec(d),
        out_shape=jax.ShapeDtypeStruct((t, d), BF16), compiler_params=_cp("parallel"))(x, g)


def _add_rms_fwd(x, y, g1, g2, name):
    t, d = x.shape

    def body(x_ref, y_ref, g1_ref, g2_ref, xn_ref, h_ref):
        xn = x_ref[...] + _rms(y_ref[...], g1_ref[...])
        xn_ref[...] = xn
        h_ref[...] = _rms(xn, g2_ref[...]).astype(BF16)

    return pl.pallas_call(
        body, name=name, grid=(t // TM,),
        in_specs=[_row_spec(d), _row_spec(d), _vec_spec(d), _vec_spec(d)],
        out_specs=[_row_spec(d), _row_spec(d)],
        out_shape=[jax.ShapeDtypeStruct((t, d), F32), jax.ShapeDtypeStruct((t, d), BF16)],
        compiler_params=_cp("parallel"))(x, y, g1, g2)


def _final_loss(x, y, g, tgt, name):
    t, d = x.shape

    def body(x_ref, y_ref, g_ref, t_ref, l_ref, dx_ref):
        @pl.when(pl.program_id(0) == 0)
        def _():
            l_ref[...] = jnp.zeros_like(l_ref)

        err = x_ref[...] + _rms(y_ref[...], g_ref[...]) - t_ref[...]
        dx_ref[...] = err * (1.0 / d)
        l_ref[...] += jnp.sum(jnp.sum(err * err, axis=1, keepdims=True), axis=0, keepdims=True) * (0.5 / d)

    return pl.pallas_call(
        body, name=name, grid=(t // TM,),
        in_specs=[_row_spec(d), _row_spec(d), _vec_spec(d), _row_spec(d)],
        out_specs=[pl.BlockSpec((1, LANES), lambda i: (0, 0)), _row_spec(d)],
        out_shape=[jax.ShapeDtypeStruct((1, LANES), F32), jax.ShapeDtypeStruct((t, d), F32)],
        compiler_params=_cp("arbitrary"))(x, y, g, tgt)


def _rms_bwd_vals(dn, v, g):
    w = v.shape[-1]
    r = lax.rsqrt(jnp.mean(v * v, axis=-1, keepdims=True) + EPS)
    vh = v * r
    dgp = jnp.sum(dn * vh, axis=0, keepdims=True)
    dvh = dn * g
    dv = r * (dvh - vh * (jnp.sum(dvh * vh, axis=-1, keepdims=True) * (1.0 / w)))
    return dv, dgp


def _norm_bwd(dn, v, g, resid, out_dtype, name):
    t, d = v.shape
    has_res = resid is not None

    def body(*refs):
        if has_res:
            dn_ref, v_ref, g_ref, r_ref, dv_ref, dg_ref = refs
        else:
            dn_ref, v_ref, g_ref, dv_ref, dg_ref = refs

        @pl.when(pl.program_id(0) == 0)
        def _():
            dg_ref[...] = jnp.zeros_like(dg_ref)

        dv, dgp = _rms_bwd_vals(dn_ref[...].astype(F32), v_ref[...], g_ref[...])
        if has_res:
            dv = dv + r_ref[...]
        dv_ref[...] = dv.astype(out_dtype)
        dg_ref[...] += dgp

    in_specs = [_row_spec(d), _row_spec(d), _vec_spec(d)] + ([_row_spec(d)] if has_res else [])
    args = [dn, v, g] + ([resid] if has_res else [])
    return pl.pallas_call(
        body, name=name, grid=(t // TM,), in_specs=in_specs,
        out_specs=[_row_spec(d), _vec_spec(d)],
        out_shape=[jax.ShapeDtypeStruct((t, d), out_dtype), jax.ShapeDtypeStruct((1, d), F32)],
        compiler_params=_cp("arbitrary"))(*args)


def _rope_trig(pos, invf):
    ang = pos * invf
    return jnp.cos(ang), jnp.sin(ang)


def _rope_tables(trig, w, lo, half):
    c, s = trig
    lane = lax.broadcasted_iota(jnp.int32, c.shape, 1)
    active = (lane >= lo) & (lane < lo + 2 * half)
    cos = jnp.concatenate([jnp.where(active, c, 1.0)] * (w // LANES), axis=1)
    sin = jnp.concatenate([jnp.where(active, s, 0.0)] * (w // LANES), axis=1)
    lanes = lax.broadcasted_iota(jnp.int32, (c.shape[0], w), 1) & (LANES - 1)
    first = (lanes >= lo) & (lanes < lo + half)
    second = (lanes >= lo + half) & (lanes < lo + 2 * half)
    return cos, sin, first, second


def _rope_apply(v, cos, sin, first, second, half, sign):
    w = v.shape[-1]
    up = pltpu.roll(v, w - half, 1)
    dn = pltpu.roll(v, half, 1)
    rot = jnp.where(first, -up, jnp.where(second, dn, 0.0))
    return v * cos + rot * (sin * sign)


def _forget_lsf(proj, b_pad, name):
    t = proj.shape[0]

    def body(ff_ref, b_ref, o_ref):
        f = ff_ref[...] + b_ref[...]
        o_ref[...] = -(jnp.maximum(-f, 0.0) + jnp.log(1.0 + jnp.exp(-jnp.abs(f))))

    return pl.pallas_call(
        body, name=name, grid=(t // TM,),
        in_specs=[pl.BlockSpec((TM, LANES), lambda i: (i, OFF_FF // LANES)), _vec_spec(LANES)],
        out_specs=_row_spec(LANES), out_shape=jax.ShapeDtypeStruct((t, LANES), F32),
        compiler_params=_cp("parallel"))(proj, b_pad)


def _split3(c):
    hi = c.astype(BF16).astype(F32)
    mid = (c - hi).astype(BF16).astype(F32)
    return hi, mid, (c - hi) - mid


def _transposed_tiles(dst, row0, blocks, width):
    for b, blk in enumerate(blocks):
        bt = blk.T.astype(BF16)
        rows = bt.shape[0]
        for w in range(TM // width):
            dst[w, row0 + b * rows:row0 + (b + 1) * rows, :] = bt[:, w * width:(w + 1) * width]


def _prep_fwd(proj, cum, pos, invf, lg_lanes, g_q, g_kv, wq_pad, wkv_pad, name):
    t = proj.shape[0]

    def body(fox_ref, ret_ref, sb_ref, cq_ref, ckv_ref, kr_ref, cum_ref, pos_ref, invf_ref, lg_ref,
             gq_ref, gkv_ref, wq_ref, wkv_ref,
             ofox_ref, oret_ref, osb_ref, omla_ref, ofoxt_ref, osbt_ref, omlat_ref, orett_ref):
        pos_v = pos_ref[...]
        osb_ref[:, 0:HP] = (sb_ref[:, 0:HP] * 0.125).astype(BF16)
        osb_ref[:, HP:QKV] = sb_ref[:, HP:QKV].astype(BF16)
        _transposed_tiles(osbt_ref, 0, [sb_ref[:, HP:2 * HP], sb_ref[:, 2 * HP:QKV]], TKS)
        lane = lax.broadcasted_iota(jnp.int32, (TM, LANES), 1)
        cumv = cum_ref[...]
        fq, fk = [], []
        for hb in range(HEADS):
            hi, mid, lo = _split3(cumv[:, hb:hb + 1])
            q = fox_ref[:, hb * LANES:(hb + 1) * LANES] * 0.125
            k = fox_ref[:, HP + hb * LANES:HP + (hb + 1) * LANES]
            ones_q = (lane >= HEAD_DIM) & (lane < HEAD_DIM + 3)
            ones_k = (lane >= HEAD_DIM + 3) & (lane < HEAD_DIM + 6)
            q = jnp.where(ones_q, 1.0, q)
            k = jnp.where(ones_k, 1.0, k)
            for n, part in enumerate((hi, mid, lo)):
                q = jnp.where(lane == HEAD_DIM + 3 + n, part, q)
                k = jnp.where(lane == HEAD_DIM + n, -part, k)
            fq.append(q)
            fk.append(k)
        fk = jnp.concatenate(fk, axis=1)
        ofox_ref[:, 0:HP] = jnp.concatenate(fq, axis=1).astype(BF16)
        ofox_ref[:, HP:2 * HP] = fk.astype(BF16)
        ofox_ref[:, 2 * HP:QKV] = fox_ref[:, 2 * HP:QKV].astype(BF16)
        _transposed_tiles(ofoxt_ref, 0, [fk, fox_ref[:, 2 * HP:QKV]], BQ)
        trig = _rope_trig(pos_v, invf_ref[...])
        cos, sin, first, second = _rope_tables(trig, HP, 0, HEAD_DIM // 2)
        nloc = lax.broadcasted_iota(jnp.int32, (TM, 1), 0).astype(F32)
        dec = lg_ref[...] * nloc
        rq = _rope_apply(ret_ref[:, 0:HP], cos, sin, first, second, HEAD_DIM // 2, 1.0)
        rk = _rope_apply(ret_ref[:, HP:2 * HP], cos, sin, first, second, HEAD_DIM // 2, 1.0)
        oret_ref[:, 0:HP] = (rq * jnp.exp(dec)).astype(BF16)
        rk = rk * 0.125 * jnp.exp(-dec)
        oret_ref[:, HP:2 * HP] = rk.astype(BF16)
        _transposed_tiles(orett_ref, 0, [rk], BQ)
        oret_ref[:, 2 * HP:QKV] = ret_ref[:, 2 * HP:QKV].astype(BF16)
        cosm, sinm, firstm, secondm = _rope_tables(trig, HP, HEAD_DIM, ROPE_MLA // 2)
        cqn = _rms(cq_ref[...], gq_ref[...]).astype(BF16)
        qm = _bdot(cqn, wq_ref[...])
        omla_ref[:, 0:HP] = _rope_apply(qm, cosm, sinm, firstm, secondm, ROPE_MLA // 2, 1.0).astype(BF16)
        ckvn = _rms(ckv_ref[...], gkv_ref[...]).astype(BF16)
        kv = _bdot(ckvn, wkv_ref[...])
        krr = _rope_apply(kr_ref[...], cosm[:, 0:LANES], sinm[:, 0:LANES], firstm[:, 0:LANES],
                          secondm[:, 0:LANES], ROPE_MLA // 2, 1.0)
        mk = kv[:, 0:HP] + jnp.concatenate([krr] * HEADS, axis=1)
        omla_ref[:, HP:2 * HP] = mk.astype(BF16)
        omla_ref[:, 2 * HP:QKV] = kv[:, HP:2 * HP].astype(BF16)
        _transposed_tiles(omlat_ref, 0, [mk, kv[:, HP:2 * HP]], BQ)

    def seg(off, w):
        return pl.BlockSpec((TM, w), lambda i, o=off // w: (i, o))

    def full(shape):
        return pl.BlockSpec(shape, lambda i: (0,) * len(shape))

    def tiles(width):
        return pl.BlockSpec((TM // width, 2 * HP, width), lambda i: (i, 0, 0))

    in_specs = [seg(OFF_FOX, QKV), seg(OFF_RET, QKV), seg(OFF_SB, QKV), seg(OFF_CQ, Q_RANK), seg(OFF_CKV, LANES),
                seg(OFF_KR, LANES), _row_spec(LANES), pl.BlockSpec((TM, 1), lambda i: (i, 0)),
                full((1, LANES)), full((1, HP)), full((1, Q_RANK)), full((1, KV_RANK)),
                full((Q_RANK, HP)), full((KV_RANK, 2 * HP))]
    out_specs = [_row_spec(QKV)] * 4 + [tiles(BQ), tiles(TKS), tiles(BQ),
                                        pl.BlockSpec((1, HP, BQ), lambda i: (i, 0, 0))]
    out_shape = [jax.ShapeDtypeStruct((t, QKV), BF16)] * 4 + [
        jax.ShapeDtypeStruct((t // BQ, 2 * HP, BQ), BF16), jax.ShapeDtypeStruct((t // TKS, 2 * HP, TKS), BF16),
        jax.ShapeDtypeStruct((t // BQ, 2 * HP, BQ), BF16), jax.ShapeDtypeStruct((t // BQ, HP, BQ), BF16)]
    return pl.pallas_call(
        body, name=name, grid=(t // TM,), in_specs=in_specs, out_specs=out_specs, out_shape=out_shape,
        compiler_params=_cp("parallel"))(proj, proj, proj, proj, proj, proj, cum, pos, invf, lg_lanes,
                                         g_q, g_kv, wq_pad, wkv_pad)


def _prep_bwd(dfox, dret, dsb, dmla, drg, dlsf, proj, pos, invf, lg_lanes, b_pad, g_q, g_kv,
              wq_pad, wkv_pad, name):
    t = proj.shape[0]

    def body(dfq, dfk, dfv, drq, drk, drv, dsq, dsk, dsv, dmq, dmk, dmv, drg_ref, dlsf_ref,
             cq_ref, ckv_ref, ff_ref, pos_ref, invf_ref, lg_ref, b_ref, gq_ref, gkv_ref, wq_ref, wkv_ref,
             dp_ref, dwq_ref, dwkv_ref, dgq_ref, dgkv_ref, dbf_ref):
        @pl.when(pl.program_id(0) == 0)
        def _():
            dwq_ref[...] = jnp.zeros_like(dwq_ref)
            dwkv_ref[...] = jnp.zeros_like(dwkv_ref)
            dgq_ref[...] = jnp.zeros_like(dgq_ref)
            dgkv_ref[...] = jnp.zeros_like(dgkv_ref)
            dbf_ref[...] = jnp.zeros_like(dbf_ref)

        pos_v = pos_ref[...]
        for off, (dq, dk, dv) in ((OFF_FOX, (dfq, dfk, dfv)), (OFF_SB, (dsq, dsk, dsv))):
            dp_ref[:, off:off + HP] = (dq[...] * 0.125).astype(BF16)
            dp_ref[:, off + HP:off + 2 * HP] = dk[...].astype(BF16)
            dp_ref[:, off + 2 * HP:off + QKV] = dv[...].astype(BF16)
        trig = _rope_trig(pos_v, invf_ref[...])
        cos, sin, first, second = _rope_tables(trig, HP, 0, HEAD_DIM // 2)
        nloc = lax.broadcasted_iota(jnp.int32, (TM, 1), 0).astype(F32)
        dec = lg_ref[...] * nloc
        dq = _rope_apply(drq[...] * jnp.exp(dec), cos, sin, first, second, HEAD_DIM // 2, -1.0)
        dk = _rope_apply(drk[...] * (0.125 * jnp.exp(-dec)), cos, sin, first, second, HEAD_DIM // 2, -1.0)
        dp_ref[:, OFF_RET:OFF_RET + HP] = dq.astype(BF16)
        dp_ref[:, OFF_RET + HP:OFF_RET + 2 * HP] = dk.astype(BF16)
        dp_ref[:, OFF_RET + 2 * HP:OFF_RET + QKV] = drv[...].astype(BF16)
        dp_ref[:, OFF_RG:OFF_RG + HP] = drg_ref[...].astype(BF16)
        cosm, sinm, firstm, secondm = _rope_tables(trig, HP, HEAD_DIM, ROPE_MLA // 2)
        dql = _rope_apply(dmq[...], cosm, sinm, firstm, secondm, ROPE_MLA // 2, -1.0).astype(BF16)
        cq = cq_ref[...]
        cqn = _rms(cq, gq_ref[...]).astype(BF16)
        dwq_ref[...] += _bdot(cqn, dql, TN)
        dcqn = _bdot(dql, wq_ref[...], NT)
        dcq, dgq = _rms_bwd_vals(dcqn, cq, gq_ref[...])
        dgq_ref[...] += dgq
        dp_ref[:, OFF_CQ:OFF_CQ + Q_RANK] = dcq.astype(BF16)
        dkm = dmk[...]
        dkv = jnp.concatenate([dkm, dmv[...]], axis=1).astype(BF16)
        ckv = ckv_ref[...]
        ckvn = _rms(ckv, gkv_ref[...]).astype(BF16)
        dwkv_ref[...] += _bdot(ckvn, dkv, TN)
        dckvn = _bdot(dkv, wkv_ref[...], NT)
        dckv, dgkv = _rms_bwd_vals(dckvn, ckv, gkv_ref[...])
        dgkv_ref[...] += dgkv
        dp_ref[:, OFF_CKV:OFF_CKV + LANES] = dckv.astype(BF16)
        dkr = dkm[:, 0:LANES] + dkm[:, LANES:2 * LANES] + dkm[:, 2 * LANES:3 * LANES] + dkm[:, 3 * LANES:HP]
        act = firstm[:, 0:LANES] | secondm[:, 0:LANES]
        dkr = jnp.where(act, dkr, 0.0)
        dkr = _rope_apply(dkr, cosm[:, 0:LANES], sinm[:, 0:LANES], firstm[:, 0:LANES], secondm[:, 0:LANES],
                          ROPE_MLA // 2, -1.0)
        dp_ref[:, OFF_KR:OFF_KR + LANES] = dkr.astype(BF16)
        f = ff_ref[...] + b_ref[...]
        dff = dlsf_ref[...] / (1.0 + jnp.exp(f))
        dbf_ref[...] += jnp.sum(dff, axis=0, keepdims=True)
        dp_ref[:, OFF_FF:OFF_FF + LANES] = dff.astype(BF16)
        dp_ref[:, OFF_FF + LANES:NP_IN] = jnp.zeros((TM, NP_IN - OFF_FF - LANES), BF16)

    def seg(off, w):
        return pl.BlockSpec((TM, w), lambda i, o=off // w: (i, o))

    def full(shape):
        return pl.BlockSpec(shape, lambda i: (0,) * len(shape))

    hp_spec = _row_spec(HP)
    in_specs = [hp_spec] * 13 + [_row_spec(LANES), seg(OFF_CQ, Q_RANK), seg(OFF_CKV, LANES), seg(OFF_FF, LANES),
                                 pl.BlockSpec((TM, 1), lambda i: (i, 0)),
                                 full((1, LANES)), full((1, HP)), full((1, LANES)), full((1, Q_RANK)),
                                 full((1, KV_RANK)), full((Q_RANK, HP)), full((KV_RANK, 2 * HP))]
    out_specs = [_row_spec(NP_IN), full((Q_RANK, HP)), full((KV_RANK, 2 * HP)), full((1, Q_RANK)),
                 full((1, KV_RANK)), full((1, LANES))]
    out_shape = [jax.ShapeDtypeStruct((t, NP_IN), BF16), jax.ShapeDtypeStruct((Q_RANK, HP), F32),
                 jax.ShapeDtypeStruct((KV_RANK, 2 * HP), F32), jax.ShapeDtypeStruct((1, Q_RANK), F32),
                 jax.ShapeDtypeStruct((1, KV_RANK), F32), jax.ShapeDtypeStruct((1, LANES), F32)]
    return pl.pallas_call(
        body, name=name, grid=(t // TM,), in_specs=in_specs, out_specs=out_specs, out_shape=out_shape,
        compiler_params=_cp("arbitrary"))(*dfox, *dret, *dsb, *dmla, drg, dlsf, proj, proj, proj, pos, invf,
                                          lg_lanes, b_pad, g_q, g_kv, wq_pad, wkv_pad)


def _cumsum(x, reverse, name, partials=None):
    t, w = x.shape
    n = t // TM
    xs = [x] if partials is None else [x, partials]

    def body(*refs):
        x_refs, o_ref, carry = refs[:len(xs)], refs[len(xs)], refs[len(xs) + 1]

        @pl.when(pl.program_id(0) == 0)
        def _():
            carry[...] = jnp.zeros_like(carry)

        r = lax.broadcasted_iota(jnp.int32, (TM, TM), 0)
        c = lax.broadcasted_iota(jnp.int32, (TM, TM), 1)
        tri = jnp.where((r <= c) if reverse else (r >= c), 1.0, 0.0).astype(BF16)
        v = x_refs[0][...]
        if partials is not None:
            lane = lax.broadcasted_iota(jnp.int32, (TM, LANES), 1)
            for hb in range(HEADS):
                v = v + jnp.where(lane == hb, jnp.sum(x_refs[1][:, _hs(hb)], axis=1, keepdims=True), 0.0)
        hi = v.astype(BF16)
        r1 = v - hi.astype(F32)
        mid = r1.astype(BF16)
        lo = (r1 - mid.astype(F32)).astype(BF16)
        cs = _bdot(tri, hi) + _bdot(tri, mid) + _bdot(tri, lo) + carry[...]
        o_ref[...] = cs
        carry[...] = cs[0:1, :] if reverse else cs[TM - 1:TM, :]

    imap = (lambda i: (n - 1 - i, 0)) if reverse else (lambda i: (i, 0))
    return pl.pallas_call(
        body, name=name, grid=(n,), in_specs=[pl.BlockSpec((TM, a.shape[1]), imap) for a in xs],
        out_specs=pl.BlockSpec((TM, w), imap),
        out_shape=jax.ShapeDtypeStruct((t, w), F32), scratch_shapes=[pltpu.VMEM((1, w), F32)],
        compiler_params=_cp("arbitrary"))(*xs)


HB_FWD = 4
HB_BWD = 2
HB_SB_FWD = 4
BQS = 512


def _q_spec(hb, bq=BQ):
    return pl.BlockSpec((bq, hb * LANES), lambda g, i: (i, g))


def _kv_spec(t, which, hb):
    return pl.BlockSpec((t, hb * LANES), lambda g, i, w=which: (0, w * (HEADS // hb) + g))


def _acc_spec(t, hb):
    return pl.BlockSpec((t, hb * LANES), lambda g, i: (0, g))


def _hs(hh):
    return slice(hh * LANES, (hh + 1) * LANES)


def _tile_iota(rows, cols):
    return (lax.broadcasted_iota(jnp.int32, (rows, cols), 0), lax.broadcasted_iota(jnp.int32, (rows, cols), 1))


def _kvt_spec(nkv, width, which, hb):
    return pl.BlockSpec((nkv, hb * LANES, width), lambda g, i, w=which: (0, w * (HEADS // hb) + g, 0))


def _qrow_spec(hb, bq=BQ):
    return pl.BlockSpec((hb, 1, 1, bq), lambda g, i: (g, i, 0, 0))


def _vis_t(chunk_mask):
    r, c = _tile_iota(BQ, BQ)
    return ((r >> CHUNK_SHIFT) <= (c >> CHUNK_SHIFT)) if chunk_mask else (r <= c)


def _softmax_fwd(qkv, kvt, *, chunk_mask, scale, name):
    t = qkv.shape[0]
    nq = t // BQ
    hb = HB_FWD

    def body(q_ref, k_ref, vt_ref, o_ref, lse_ref, m_sc, l_sc, acc_sc):
        i = pl.program_id(1)
        m_sc[...] = jnp.full((hb, 1, BQ), NEG, F32)
        l_sc[...] = jnp.zeros((hb, 1, BQ), F32)
        acc_sc[...] = jnp.zeros((hb, LANES, BQ), F32)

        def tile(j, masked):
            off = pl.multiple_of(j * BQ, BQ)
            vis = _vis_t(chunk_mask) if masked else None
            ss = [_bdot(k_ref[pl.ds(off, BQ), _hs(hh)], q_ref[:, _hs(hh)], NT) for hh in range(hb)]
            ps, alphas = [], []
            for hh in range(hb):
                s = ss[hh]
                if scale != 1.0:
                    s = s * scale
                if masked:
                    s = jnp.where(vis, s, NEG)
                m_old = m_sc[hh]
                m_new = jnp.maximum(m_old, jnp.max(s, axis=0, keepdims=True))
                alpha = jnp.exp(m_old - m_new)
                p = jnp.exp(s - m_new)
                l_sc[hh] = alpha * l_sc[hh] + jnp.sum(p, axis=0, keepdims=True)
                m_sc[hh] = m_new
                ps.append(p.astype(BF16))
                alphas.append(alpha)
            for hh in range(hb):
                acc_sc[hh] = alphas[hh] * acc_sc[hh] + _bdot(vt_ref[j, _hs(hh), :], ps[hh])

        def loop(j, carry):
            tile(j, False)
            return carry

        lax.fori_loop(0, i, loop, 0)
        tile(i, True)
        for hh in range(hb):
            l = l_sc[hh]
            o_ref[:, _hs(hh)] = (acc_sc[hh] / l).T
            lse_ref[hh, 0] = m_sc[hh] + jnp.log(l)

    return pl.pallas_call(
        body, name=name, grid=(HEADS // hb, nq),
        in_specs=[_q_spec(hb), _kv_spec(t, 1, hb), _kvt_spec(nq, BQ, 1, hb)],
        out_specs=[_q_spec(hb), _qrow_spec(hb)],
        out_shape=[jax.ShapeDtypeStruct((t, HP), F32), jax.ShapeDtypeStruct((HEADS, nq, 1, BQ), F32)],
        scratch_shapes=[pltpu.VMEM((hb, 1, BQ), F32), pltpu.VMEM((hb, 1, BQ), F32), pltpu.VMEM((hb, LANES, BQ), F32)],
        compiler_params=_cp("parallel", "arbitrary"))(qkv, qkv, kvt)


def _softmax_bwd(qkv, kvt, do, lse, delta, *, bias, chunk_mask, scale, name):
    t = qkv.shape[0]
    nq = t // BQ
    hb = HB_BWD

    def body(*refs):
        if bias:
            (q_ref, k_ref, v_ref, kt_ref, do_ref, lse_ref, dl_ref, dq_ref, dk_ref, dv_ref, dck_ref, dcq_ref,
             dq_sc, dcq_sc) = refs
            dcq_sc[...] = jnp.zeros((hb, 1, BQ), F32)
        else:
            q_ref, k_ref, v_ref, kt_ref, do_ref, lse_ref, dl_ref, dq_ref, dk_ref, dv_ref, dq_sc = refs
        i = pl.program_id(1)

        @pl.when(i == 0)
        def _():
            dk_ref[...] = jnp.zeros_like(dk_ref)
            dv_ref[...] = jnp.zeros_like(dv_ref)
            if bias:
                dck_ref[...] = jnp.zeros_like(dck_ref)

        dq_sc[...] = jnp.zeros((hb, LANES, BQ), F32)

        def tile(j, masked):
            off = pl.multiple_of(j * BQ, BQ)
            vis = _vis_t(chunk_mask) if masked else None
            qs = [q_ref[:, _hs(hh)] for hh in range(hb)]
            dobs = [do_ref[:, _hs(hh)].astype(BF16) for hh in range(hb)]
            ss = [_bdot(k_ref[pl.ds(off, BQ), _hs(hh)], qs[hh], NT) for hh in range(hb)]
            dps = [_bdot(v_ref[pl.ds(off, BQ), _hs(hh)], dobs[hh], NT) for hh in range(hb)]
            pbs, dsbs = [], []
            for hh in range(hb):
                s = ss[hh]
                if scale != 1.0:
                    s = s * scale
                p = jnp.exp(s - lse_ref[hh, 0])
                if masked:
                    p = jnp.where(vis, p, 0.0)
                ds = p * (dps[hh] - dl_ref[hh, 0])
                if bias:
                    dck_ref[pl.ds(off, BQ), _hs(hh)] -= ds[:, 0:LANES] + ds[:, LANES:BQ]
                    dcq_sc[hh] += jnp.sum(ds, axis=0, keepdims=True)
                if scale != 1.0:
                    ds = ds * scale
                pbs.append(p.astype(BF16))
                dsbs.append(ds.astype(BF16))
            for hh in range(hb):
                sl = _hs(hh)
                dv_ref[pl.ds(off, BQ), sl] += _bdot(pbs[hh], dobs[hh])
                dk_ref[pl.ds(off, BQ), sl] += _bdot(dsbs[hh], qs[hh])
                dq_sc[hh] += _bdot(kt_ref[j, sl, :], dsbs[hh])

        def loop(j, carry):
            tile(j, False)
            return carry

        lax.fori_loop(0, i, loop, 0)
        tile(i, True)
        for hh in range(hb):
            dq_ref[:, _hs(hh)] = dq_sc[hh].T
            if bias:
                dcq_ref[hh, 0] = dcq_sc[hh]

    in_specs = [_q_spec(hb), _kv_spec(t, 1, hb), _kv_spec(t, 2, hb), _kvt_spec(nq, BQ, 0, hb), _q_spec(hb),
                _qrow_spec(hb), _qrow_spec(hb)]
    out_specs = [_q_spec(hb), _acc_spec(t, hb), _acc_spec(t, hb)]
    out_shape = [jax.ShapeDtypeStruct((t, HP), F32)] * 3
    scratch = [pltpu.VMEM((hb, LANES, BQ), F32)]
    if bias:
        out_specs += [_acc_spec(t, hb), _qrow_spec(hb)]
        out_shape += [jax.ShapeDtypeStruct((t, HP), F32), jax.ShapeDtypeStruct((HEADS, nq, 1, BQ), F32)]
        scratch.append(pltpu.VMEM((hb, 1, BQ), F32))
    return pl.pallas_call(
        body, name=name, grid=(HEADS // hb, nq), in_specs=in_specs, out_specs=out_specs, out_shape=out_shape,
        scratch_shapes=scratch,
        compiler_params=_cp("parallel", "arbitrary"))(qkv, qkv, qkv, kvt, do, lse, delta)


def _ret_diag_decay(lg1, keys_on_rows=False):
    r, c = _tile_iota(BQ, BQ)
    qn, km = (c, r) if keys_on_rows else (r, c)
    dd = jnp.where(km > qn, jnp.exp((2.0 * lg1) * (km - qn).astype(F32)), 1.0)
    return jnp.where((km >> CHUNK_SHIFT) <= (qn >> CHUNK_SHIFT), dd, 0.0)


def _lg_spec(hb):
    return pl.BlockSpec((hb, 1, LANES), lambda g, i: (g, 0, 0))


def _ret_fwd(qkv, lg_heads, name):
    t = qkv.shape[0]
    nq = t // BQ
    hb = HB_FWD

    def body(lg_ref, q_ref, k_ref, v_ref, o_ref, acc_sc):
        i = pl.program_id(1)
        acc_sc[...] = jnp.zeros((hb, BQ, LANES), F32)

        def loop(j, carry):
            off = pl.multiple_of(j * BQ, BQ)
            span = ((i - j) * BQ).astype(F32)
            aa = [_bdot(q_ref[:, _hs(hh)], k_ref[pl.ds(off, BQ), _hs(hh)], NT).astype(BF16) for hh in range(hb)]
            for hh in range(hb):
                cf = jnp.exp(lg_ref[hh][:, 0:1] * span)
                acc_sc[hh] += cf * _bdot(aa[hh], v_ref[pl.ds(off, BQ), _hs(hh)])
            return carry

        lax.fori_loop(0, i, loop, 0)
        off = pl.multiple_of(i * BQ, BQ)
        for hh in range(hb):
            sl = _hs(hh)
            a = _bdot(q_ref[:, sl], k_ref[pl.ds(off, BQ), sl], NT) * _ret_diag_decay(lg_ref[hh][:, 0:1])
            o_ref[:, sl] = acc_sc[hh] + _bdot(a.astype(BF16), v_ref[pl.ds(off, BQ), sl])

    return pl.pallas_call(
        body, name=name, grid=(HEADS // hb, nq),
        in_specs=[_lg_spec(hb), _q_spec(hb), _kv_spec(t, 1, hb), _kv_spec(t, 2, hb)],
        out_specs=_q_spec(hb), out_shape=jax.ShapeDtypeStruct((t, HP), F32),
        scratch_shapes=[pltpu.VMEM((hb, BQ, LANES), F32)],
        compiler_params=_cp("parallel", "arbitrary"))(lg_heads, qkv, qkv, qkv)


def _ret_bwd(qkv, kt, lg_heads, do, name):
    t = qkv.shape[0]
    nq = t // BQ
    hb = HB_BWD

    def body(lg_ref, q_ref, k_ref, v_ref, kt_ref, do_ref, dq_ref, dk_ref, dv_ref, dq_sc):
        i = pl.program_id(1)

        @pl.when(i == 0)
        def _():
            dk_ref[...] = jnp.zeros_like(dk_ref)
            dv_ref[...] = jnp.zeros_like(dv_ref)

        dq_sc[...] = jnp.zeros((hb, LANES, BQ), F32)

        def tile(j, diag):
            off = pl.multiple_of(j * BQ, BQ)
            qs = [q_ref[:, _hs(hh)] for hh in range(hb)]
            dobs = [do_ref[:, _hs(hh)].astype(BF16) for hh in range(hb)]
            aa = [_bdot(k_ref[pl.ds(off, BQ), _hs(hh)], qs[hh], NT) for hh in range(hb)]
            das = [_bdot(v_ref[pl.ds(off, BQ), _hs(hh)], dobs[hh], NT) for hh in range(hb)]
            abs_, dabs, cfs = [], [], []
            for hh in range(hb):
                lg1 = lg_ref[hh][:, 0:1]
                if diag:
                    dd = _ret_diag_decay(lg1, keys_on_rows=True)
                    abs_.append((aa[hh] * dd).astype(BF16))
                    dabs.append((das[hh] * dd).astype(BF16))
                    cfs.append(1.0)
                else:
                    abs_.append(aa[hh].astype(BF16))
                    dabs.append(das[hh].astype(BF16))
                    cfs.append(jnp.exp(lg1 * ((i - j) * BQ).astype(F32)))
            for hh in range(hb):
                sl = _hs(hh)
                dv_ref[pl.ds(off, BQ), sl] += cfs[hh] * _bdot(abs_[hh], dobs[hh])
                dk_ref[pl.ds(off, BQ), sl] += cfs[hh] * _bdot(dabs[hh], qs[hh])
                dq_sc[hh] += cfs[hh] * _bdot(kt_ref[j, sl, :], dabs[hh])

        def loop(j, carry):
            tile(j, False)
            return carry

        lax.fori_loop(0, i, loop, 0)
        tile(i, True)
        for hh in range(hb):
            dq_ref[:, _hs(hh)] = dq_sc[hh].T

    return pl.pallas_call(
        body, name=name, grid=(HEADS // hb, nq),
        in_specs=[_lg_spec(hb), _q_spec(hb), _kv_spec(t, 1, hb), _kv_spec(t, 2, hb), _kvt_spec(nq, BQ, 0, hb),
                  _q_spec(hb)],
        out_specs=[_q_spec(hb), _acc_spec(t, hb), _acc_spec(t, hb)],
        out_shape=[jax.ShapeDtypeStruct((t, HP), F32)] * 3,
        scratch_shapes=[pltpu.VMEM((hb, LANES, BQ), F32)],
        compiler_params=_cp("parallel", "arbitrary"))(lg_heads, qkv, qkv, qkv, kt, do)


def _sb_tile_logs(q, kb, vis):
    z = _bdot(kb, q, NT)
    ls = -(jnp.maximum(z, 0.0) + jnp.log(1.0 + jnp.exp(-jnp.abs(z))))
    if vis is not None:
        ls = jnp.where(vis, ls, 0.0)
    return z, ls


def _sb_vis(i, j):
    r, c = _tile_iota(TKS, BQS)
    return (j * TKS + r) < (i * BQS + c)


def _sb_later(ls, after):
    hi, lo = _split2(ls)
    return _bdot(after, hi) + _bdot(after, lo)


def _sb_fwd(qkv, kvt, name):
    t = qkv.shape[0]
    nq = t // BQS
    per = BQS // TKS
    hb = HB_SB_FWD

    def body(q_ref, k_ref, vt_ref, o_ref, tot_ref, acc_sc, r_sc):
        i = pl.program_id(1)
        acc_sc[...] = jnp.zeros((hb, LANES, BQS), F32)
        r_sc[...] = jnp.zeros((hb, 1, BQS), F32)
        mr, mc = _tile_iota(TKS, TKS)
        after = jnp.where(mc > mr, 1.0, 0.0).astype(BF16)

        def tile(j, masked):
            off = pl.multiple_of(j * TKS, TKS)
            vis = _sb_vis(i, j) if masked else None
            zl = [_sb_tile_logs(q_ref[:, _hs(hh)], k_ref[pl.ds(off, TKS), _hs(hh)], vis) for hh in range(hb)]
            laters = [_sb_later(zl[hh][1], after) for hh in range(hb)]
            ws = []
            for hh in range(hb):
                z, ls = zl[hh]
                w = jnp.exp(z + ls + laters[hh] + r_sc[hh])
                if masked:
                    w = jnp.where(vis, w, 0.0)
                ws.append(w.astype(BF16))
                r_sc[hh] += jnp.sum(ls, axis=0, keepdims=True)
            for hh in range(hb):
                acc_sc[hh] += _bdot(vt_ref[j, _hs(hh), :], ws[hh])

        for d in range(per):
            tile(per * i + (per - 1 - d), True)

        def loop(jj, carry):
            tile(per * i - 1 - jj, False)
            return carry

        lax.fori_loop(0, per * i, loop, 0)
        for hh in range(hb):
            o_ref[:, _hs(hh)] = acc_sc[hh].T
            tot_ref[hh, 0] = r_sc[hh]

    return pl.pallas_call(
        body, name=name, grid=(HEADS // hb, nq),
        in_specs=[_q_spec(hb, BQS), _kv_spec(t, 1, hb), _kvt_spec(t // TKS, TKS, 1, hb)],
        out_specs=[_q_spec(hb, BQS), _qrow_spec(hb, BQS)],
        out_shape=[jax.ShapeDtypeStruct((t, HP), F32), jax.ShapeDtypeStruct((HEADS, nq, 1, BQS), F32)],
        scratch_shapes=[pltpu.VMEM((hb, LANES, BQS), F32), pltpu.VMEM((hb, 1, BQS), F32)],
        compiler_params=_cp("parallel", "arbitrary"))(qkv, qkv, kvt)


def _sb_bwd(qkv, kvt, do, tot, name):
    t = qkv.shape[0]
    nq = t // BQS
    per = BQS // TKS
    hb = HB_BWD

    def body(q_ref, k_ref, v_ref, kt_ref, do_ref, tot_ref, dq_ref, dk_ref, dv_ref, dq_sc, p_sc, g_sc):
        i = pl.program_id(1)

        @pl.when(i == 0)
        def _():
            dk_ref[...] = jnp.zeros_like(dk_ref)
            dv_ref[...] = jnp.zeros_like(dv_ref)

        dq_sc[...] = jnp.zeros((hb, LANES, BQS), F32)
        p_sc[...] = jnp.zeros((hb, 1, BQS), F32)
        g_sc[...] = jnp.zeros((hb, 1, BQS), F32)
        mr, mc = _tile_iota(TKS, TKS)
        after = jnp.where(mc > mr, 1.0, 0.0).astype(BF16)
        before = jnp.where(mc < mr, 1.0, 0.0).astype(BF16)

        def tile(j, masked):
            off = pl.multiple_of(j * TKS, TKS)
            vis = _sb_vis(i, j) if masked else None
            qs = [q_ref[:, _hs(hh)] for hh in range(hb)]
            dobs = [do_ref[:, _hs(hh)].astype(BF16) for hh in range(hb)]
            zl = [_sb_tile_logs(qs[hh], k_ref[pl.ds(off, TKS), _hs(hh)], vis) for hh in range(hb)]
            dws = [_bdot(v_ref[pl.ds(off, TKS), _hs(hh)], dobs[hh], NT) for hh in range(hb)]
            laters = [_sb_later(zl[hh][1], after) for hh in range(hb)]
            ws, gs = [], []
            for hh in range(hb):
                z, ls = zl[hh]
                own = jnp.sum(ls, axis=0, keepdims=True)
                rest = tot_ref[hh, 0] - p_sc[hh] - own
                w = jnp.exp(z + ls + laters[hh] + rest)
                if masked:
                    w = jnp.where(vis, w, 0.0)
                p_sc[hh] += own
                ws.append(w.astype(BF16))
                gs.append(dws[hh] * w)
            gins = []
            for hh in range(hb):
                ghi, glo = _split2(gs[hh])
                gins.append(_bdot(before, ghi) + _bdot(before, glo))
            dzbs = []
            for hh in range(hb):
                g = gs[hh]
                stay = jnp.exp(zl[hh][1])
                dz = g * stay - (1.0 - stay) * (gins[hh] + g_sc[hh])
                if masked:
                    dz = jnp.where(vis, dz, 0.0)
                g_sc[hh] += jnp.sum(g, axis=0, keepdims=True)
                dzbs.append(dz.astype(BF16))
            for hh in range(hb):
                sl = _hs(hh)
                dv_ref[pl.ds(off, TKS), sl] += _bdot(ws[hh], dobs[hh])
                dk_ref[pl.ds(off, TKS), sl] += _bdot(dzbs[hh], qs[hh])
                dq_sc[hh] += _bdot(kt_ref[j, sl, :], dzbs[hh])

        def loop(j, carry):
            tile(j, False)
            return carry

        lax.fori_loop(0, per * i, loop, 0)
        for d in range(per):
            tile(per * i + d, True)
        for hh in range(hb):
            dq_ref[:, _hs(hh)] = dq_sc[hh].T

    return pl.pallas_call(
        body, name=name, grid=(HEADS // hb, nq),
        in_specs=[_q_spec(hb, BQS), _kv_spec(t, 1, hb), _kv_spec(t, 2, hb), _kvt_spec(t // TKS, TKS, 0, hb),
                  _q_spec(hb, BQS), _qrow_spec(hb, BQS)],
        out_specs=[_q_spec(hb, BQS), _acc_spec(t, hb), _acc_spec(t, hb)],
        out_shape=[jax.ShapeDtypeStruct((t, HP), F32)] * 3,
        scratch_shapes=[pltpu.VMEM((hb, LANES, BQS), F32), pltpu.VMEM((hb, 1, BQS), F32),
                        pltpu.VMEM((hb, 1, BQS), F32)],
        compiler_params=_cp("parallel", "arbitrary"))(qkv, qkv, qkv, kvt, do, tot)


def _sigmoid(v):
    return 1.0 / (1.0 + jnp.exp(-v))


def _post_fwd(oa, ob, oc, od, proj, g_pad, name):
    t = oa.shape[0]

    def body(oa_ref, ob_ref, oc_ref, od_ref, rg_ref, g_ref, mx_ref):
        g = g_ref[...]

        def group(o, gg):
            r = lax.rsqrt(jnp.sum(o * o, axis=-1, keepdims=True) * (1.0 / GROUP) + EPS)
            return (o * r * gg).astype(BF16)

        mx_ref[:, 0:HP] = group(oa_ref[...], g[:, 0:HP])
        mx_ref[:, HP:2 * HP] = group(ob_ref[...], g[:, HP:2 * HP])
        mx_ref[:, 3 * HP:4 * HP] = group(od_ref[...], g[:, 3 * HP:4 * HP])
        real = lax.broadcasted_iota(jnp.int32, (TM, LANES), 1) < HEAD_DIM
        for hb in range(HEADS):
            sl = slice(hb * LANES, (hb + 1) * LANES)
            o = oc_ref[:, sl]
            mu = jnp.sum(o, axis=-1, keepdims=True) * (1.0 / HEAD_DIM)
            dlt = jnp.where(real, o - mu, 0.0)
            var = jnp.sum(dlt * dlt, axis=-1, keepdims=True) * (1.0 / HEAD_DIM)
            yn = dlt * lax.rsqrt(var + EPS) * g[:, 2 * HP + hb * LANES:2 * HP + (hb + 1) * LANES]
            x = rg_ref[:, sl]
            mx_ref[:, 2 * HP + hb * LANES:2 * HP + (hb + 1) * LANES] = (yn * (x * _sigmoid(x))).astype(BF16)

    rg_spec = pl.BlockSpec((TM, HP), lambda i: (i, OFF_RG // HP))
    return pl.pallas_call(
        body, name=name, grid=(t // TM,),
        in_specs=[_row_spec(HP)] * 4 + [rg_spec, _vec_spec(4 * HP)], out_specs=_row_spec(4 * HP),
        out_shape=jax.ShapeDtypeStruct((t, 4 * HP), BF16), compiler_params=_cp("parallel"))(oa, ob, oc, od, proj, g_pad)


def _post_bwd(dmx, oa, ob, oc, od, proj, g_pad, name):
    t = oa.shape[0]

    def body(dm_ref, oa_ref, ob_ref, oc_ref, od_ref, rg_ref, g_ref,
             doa_ref, dob_ref, doc_ref, dod_ref, dla_ref, dlb_ref, drg_ref, dg_ref):
        @pl.when(pl.program_id(0) == 0)
        def _():
            dg_ref[...] = jnp.zeros_like(dg_ref)

        g = g_ref[...]

        def group_bwd(dm, o, gg):
            r = lax.rsqrt(jnp.sum(o * o, axis=-1, keepdims=True) * (1.0 / GROUP) + EPS)
            oh = o * r
            dgp = jnp.sum(dm * oh, axis=0, keepdims=True)
            dyh = dm * gg
            do = r * (dyh - oh * (jnp.sum(dyh * oh, axis=-1, keepdims=True) * (1.0 / GROUP)))
            return do, dgp

        def delta_bc(do, o):
            prod = do * o
            lane = lax.broadcasted_iota(jnp.int32, (TM, LANES), 1)
            out = jnp.zeros((TM, LANES), F32)
            for hb in range(HEADS):
                out = jnp.where(lane == hb, jnp.sum(prod[:, hb * LANES:(hb + 1) * LANES], axis=-1, keepdims=True), out)
            return out

        oa = oa_ref[...]
        do_a, dga = group_bwd(dm_ref[:, 0:HP], oa, g[:, 0:HP])
        doa_ref[...] = do_a
        dla_ref[...] = delta_bc(do_a, oa)
        dg_ref[:, 0:HP] += dga
        ob = ob_ref[...]
        do_b, dgb = group_bwd(dm_ref[:, HP:2 * HP], ob, g[:, HP:2 * HP])
        dob_ref[...] = do_b
        dlb_ref[...] = delta_bc(do_b, ob)
        dg_ref[:, HP:2 * HP] += dgb
        do_d, dgd = group_bwd(dm_ref[:, 3 * HP:4 * HP], od_ref[...], g[:, 3 * HP:4 * HP])
        dod_ref[...] = do_d
        dg_ref[:, 3 * HP:4 * HP] += dgd
        real = lax.broadcasted_iota(jnp.int32, (TM, LANES), 1) < HEAD_DIM
        for hb in range(HEADS):
            sl = slice(hb * LANES, (hb + 1) * LANES)
            gsl = slice(2 * HP + hb * LANES, 2 * HP + (hb + 1) * LANES)
            o = oc_ref[:, sl]
            mu = jnp.sum(o, axis=-1, keepdims=True) * (1.0 / HEAD_DIM)
            dlt = jnp.where(real, o - mu, 0.0)
            var = jnp.sum(dlt * dlt, axis=-1, keepdims=True) * (1.0 / HEAD_DIM)
            rstd = lax.rsqrt(var + EPS)
            dhat = dlt * rstd
            gc = g[:, gsl]
            x = rg_ref[:, sl]
            sg = _sigmoid(x)
            dm = dm_ref[:, gsl]
            drg_ref[:, sl] = dm * (dhat * gc) * (sg * (1.0 + x * (1.0 - sg)))
            dyn = dm * (x * sg)
            dg_ref[:, gsl] += jnp.sum(dyn * dhat, axis=0, keepdims=True)
            ddh = dyn * gc
            m1 = jnp.sum(ddh, axis=-1, keepdims=True) * (1.0 / HEAD_DIM)
            m2 = jnp.sum(ddh * dhat, axis=-1, keepdims=True) * (1.0 / HEAD_DIM)
            doc_ref[:, sl] = jnp.where(real, rstd * (ddh - m1 - dhat * m2), 0.0)

    rg_spec = pl.BlockSpec((TM, HP), lambda i: (i, OFF_RG // HP))
    hp = _row_spec(HP)
    return pl.pallas_call(
        body, name=name, grid=(t // TM,),
        in_specs=[_row_spec(4 * HP), hp, hp, hp, hp, rg_spec, _vec_spec(4 * HP)],
        out_specs=[hp] * 4 + [_row_spec(LANES)] * 2 + [hp, _vec_spec(4 * HP)],
        out_shape=[jax.ShapeDtypeStruct((t, HP), F32)] * 4 + [jax.ShapeDtypeStruct((t, LANES), F32)] * 2
        + [jax.ShapeDtypeStruct((t, HP), F32), jax.ShapeDtypeStruct((1, 4 * HP), F32)],
        compiler_params=_cp("arbitrary"))(dmx, oa, ob, oc, od, proj, g_pad)


def _mesh_pos():
    return lax.axis_index("x"), lax.axis_index("y"), lax.axis_index("c")


def _peer(pos, k):
    x, y, c = pos
    px = 1 - x if (k >> 2) & 1 else x
    py = 1 - y if (k >> 1) & 1 else y
    pc = 1 - c if k & 1 else c
    return (px, py, pc), 4 * px + 2 * py + pc


def _exchange(arrs, gather, name):
    n = len(arrs)

    def body(*refs):
        ins, outs = refs[:n], refs[n:2 * n]
        send_sems, recv_sems, loc_sems = refs[2 * n:]
        pos = _mesh_pos()
        me = 4 * pos[0] + 2 * pos[1] + pos[2]
        local = []
        for a in range(n):
            src = ins[a] if gather else ins[a].at[me]
            cp = pltpu.make_async_copy(src, outs[a].at[me], loc_sems.at[a])
            cp.start()
            local.append(cp)
        sends, recvs = [], []
        for k in range(1, N_DEV):
            peer, pid = _peer(pos, k)
            for a in range(n):
                s = a * (N_DEV - 1) + k - 1
                src = ins[a] if gather else ins[a].at[pid]
                cp = pltpu.make_async_remote_copy(
                    src_ref=src, dst_ref=outs[a].at[me], send_sem=send_sems.at[s], recv_sem=recv_sems.at[s],
                    device_id=peer, device_id_type=pl.DeviceIdType.MESH)
                cp.start()
                sends.append(cp)
                recvs.append(pltpu.make_async_remote_copy(
                    src_ref=src, dst_ref=outs[a].at[pid], send_sem=send_sems.at[s], recv_sem=recv_sems.at[s],
                    device_id=peer, device_id_type=pl.DeviceIdType.MESH))
        for cp in recvs:
            cp.wait_recv()
        for cp in sends:
            cp.wait_send()
        for cp in local:
            cp.wait()

    any_spec = pl.BlockSpec(memory_space=pl.ANY)
    out_shape = [jax.ShapeDtypeStruct((N_DEV,) + tuple(a.shape) if gather else tuple(a.shape), a.dtype) for a in arrs]
    return pl.pallas_call(
        body, name=name, in_specs=[any_spec] * n, out_specs=[any_spec] * n, out_shape=out_shape,
        scratch_shapes=[pltpu.SemaphoreType.DMA((n * (N_DEV - 1),)), pltpu.SemaphoreType.DMA((n * (N_DEV - 1),)),
                        pltpu.SemaphoreType.DMA((n,))],
        compiler_params=pltpu.CompilerParams(has_side_effects=True))(*arrs)


def _device_index():
    x, y, c = _mesh_pos()
    return 4 * x + 2 * y + c


def _landing(srcs, gather):
    me = _device_index()
    lands = []
    for a in srcs:
        own = a[None] if gather else lax.dynamic_slice_in_dim(a, me, 1, axis=0)
        shape = (N_DEV,) + tuple(a.shape) if gather else tuple(a.shape)
        lands.append(lax.dynamic_update_slice_in_dim(jnp.zeros(shape, a.dtype), own, me, axis=0))
    return lands


def _exchange_copies(ins, lands, send_sems, recv_sems, gather):
    pos = _mesh_pos()
    me = 4 * pos[0] + 2 * pos[1] + pos[2]
    sends, recvs = [], []
    for k in range(1, N_DEV):
        peer, pid = _peer(pos, k)
        for a in range(len(ins)):
            s = a * (N_DEV - 1) + k - 1
            src = ins[a] if gather else ins[a].at[pid]
            sends.append(pltpu.make_async_remote_copy(
                src_ref=src, dst_ref=lands[a].at[me], send_sem=send_sems.at[s], recv_sem=recv_sems.at[s],
                device_id=peer, device_id_type=pl.DeviceIdType.MESH))
            recvs.append(pltpu.make_async_remote_copy(
                src_ref=src, dst_ref=lands[a].at[pid], send_sem=send_sems.at[s], recv_sem=recv_sems.at[s],
                device_id=peer, device_id_type=pl.DeviceIdType.MESH))
    return sends, recvs


def _exchange_start(srcs, gather, name):
    n = len(srcs)
    lands = _landing(srcs, gather)
    nsem = n * (N_DEV - 1)

    def body(*refs):
        ins, lnd = refs[:n], refs[n:2 * n]
        send_sems, recv_sems = refs[2 * n], refs[2 * n + 1]
        token = refs[-1]
        sends, _ = _exchange_copies(ins, lnd, send_sems, recv_sems, gather)
        for cp in sends:
            cp.start()
        token[...] = jnp.zeros_like(token)

    hbm = pl.BlockSpec(memory_space=pltpu.HBM)
    sem = pl.BlockSpec(memory_space=pltpu.SEMAPHORE)
    bufs = list(srcs) + lands
    out_shape = ([pltpu.SemaphoreType.DMA((nsem,)), pltpu.SemaphoreType.DMA((nsem,))]
                 + [pltpu.HBM(b.shape, b.dtype) for b in bufs] + [jax.ShapeDtypeStruct((8, LANES), F32)])
    outs = pl.pallas_call(
        body, name=name, in_specs=[hbm] * (2 * n),
        out_specs=[sem, sem] + [hbm] * (2 * n) + [pl.BlockSpec(memory_space=pltpu.VMEM)], out_shape=out_shape,
        input_output_aliases={i: 2 + i for i in range(2 * n)},
        compiler_params=pltpu.CompilerParams(has_side_effects=pltpu.SideEffectType.DATAFLOW_SIDE_EFFECTING),
    )(*[pltpu.with_memory_space_constraint(b, pltpu.HBM) for b in bufs])
    return (outs[0], outs[1], outs[2:2 + n], outs[2 + n:2 + 2 * n]), outs[-1]


def _exchange_wait(state, after, gather, name):
    send_sems, recv_sems, srcs, lands = state
    n = len(srcs)

    def body(*refs):
        ins, lnd = refs[:n], refs[n:2 * n]
        s_sems, r_sems = refs[2 * n], refs[2 * n + 1]
        sends, recvs = _exchange_copies(ins, lnd, s_sems, r_sems, gather)
        for cp in sends:
            cp.wait_send()
        for cp in recvs:
            cp.wait_recv()

    hbm = pl.BlockSpec(memory_space=pltpu.HBM)
    sem = pl.BlockSpec(memory_space=pltpu.SEMAPHORE)
    bufs = list(srcs) + list(lands)
    outs = pl.pallas_call(
        body, name=name, in_specs=[hbm] * (2 * n) + [sem, sem, pl.BlockSpec(memory_space=pl.ANY)],
        out_specs=[hbm] * (2 * n), out_shape=[pltpu.HBM(b.shape, b.dtype) for b in bufs],
        input_output_aliases={i: i for i in range(2 * n)},
        compiler_params=pltpu.CompilerParams(has_side_effects=pltpu.SideEffectType.DATAFLOW_SIDE_EFFECTING),
    )(*bufs, send_sems, recv_sems, after)
    return outs[n:]


def _adam_vals(w, g, m, v):
    m = ADAM_B1 * m + (1.0 - ADAM_B1) * g
    v = ADAM_B2 * v + (1.0 - ADAM_B2) * (g * g)
    m_hat = m / ADAM_C1
    v_hat = v / ADAM_C2
    delta = -ADAM_LR * (m_hat / (jnp.sqrt(v_hat) + ADAM_EPS) + ADAM_WD * w)
    return delta, m, v


def _small_allreduce_adam(part, w, m, v, name):
    rows = part.shape[0]

    def body(p_ref, w_ref, m_ref, v_ref, g_ref, d_ref, nm_ref, nv_ref, gath, send_sems, recv_sems):
        pos = _mesh_pos()
        me = 4 * pos[0] + 2 * pos[1] + pos[2]
        gath[me] = p_ref[...]
        sends, recvs = [], []
        for k in range(1, N_DEV):
            peer, pid = _peer(pos, k)
            cp = pltpu.make_async_remote_copy(
                src_ref=p_ref, dst_ref=gath.at[me], send_sem=send_sems.at[k - 1], recv_sem=recv_sems.at[k - 1],
                device_id=peer, device_id_type=pl.DeviceIdType.MESH)
            cp.start()
            sends.append(cp)
            recvs.append(pltpu.make_async_remote_copy(
                src_ref=p_ref, dst_ref=gath.at[pid], send_sem=send_sems.at[k - 1], recv_sem=recv_sems.at[k - 1],
                device_id=peer, device_id_type=pl.DeviceIdType.MESH))
        for cp in recvs:
            cp.wait_recv()
        for cp in sends:
            cp.wait_send()
        g = gath[0]
        for p in range(1, N_DEV):
            g = g + gath[p]
        g_ref[...] = g
        d, nm, nv = _adam_vals(w_ref[...], g, m_ref[...], v_ref[...])
        d_ref[...] = d
        nm_ref[...] = nm
        nv_ref[...] = nv

    vm = pl.BlockSpec(memory_space=pltpu.VMEM)
    sds = jax.ShapeDtypeStruct((rows, LANES), F32)
    return pl.pallas_call(
        body, name=name, in_specs=[vm] * 4, out_specs=[vm] * 4, out_shape=[sds] * 4,
        scratch_shapes=[pltpu.VMEM((N_DEV, rows, LANES), F32), pltpu.SemaphoreType.DMA((N_DEV - 1,)),
                        pltpu.SemaphoreType.DMA((N_DEV - 1,))],
        compiler_params=pltpu.CompilerParams(has_side_effects=True))(part, w, m, v)


def _reduce_adam(recv, w, m, v, name):
    shape = w.shape
    c = shape[-1]
    r = int(np.prod(shape[:-1]))
    recv2, w2, m2, v2 = recv.reshape(N_DEV, r, c), w.reshape(r, c), m.reshape(r, c), v.reshape(r, c)
    tr = r
    while tr * c * 4 > (1 << 20) and tr % 16 == 0:
        tr //= 2

    def body(r_ref, w_ref, m_ref, v_ref, g_ref, d_ref, nm_ref, nv_ref):
        g = r_ref[0].astype(F32)
        for p in range(1, N_DEV):
            g = g + r_ref[p].astype(F32)
        g_ref[...] = g
        d, nm, nv = _adam_vals(w_ref[...], g, m_ref[...], v_ref[...])
        d_ref[...] = d
        nm_ref[...] = nm
        nv_ref[...] = nv

    spec = pl.BlockSpec((tr, c), lambda i: (i, 0))
    sds = jax.ShapeDtypeStruct((r, c), F32)
    outs = pl.pallas_call(
        body, name=name, grid=(r // tr,),
        in_specs=[pl.BlockSpec((N_DEV, tr, c), lambda i: (0, i, 0)), spec, spec, spec],
        out_specs=[spec] * 4, out_shape=[sds] * 4, compiler_params=_cp("parallel"))(recv2, w2, m2, v2)
    return [o.reshape(shape) for o in outs]


def _pad_heads(w, real=HEAD_DIM):
    lead = w.shape[:-1]
    w = w.reshape(lead + (HEADS, real))
    w = jnp.pad(w, [(0, 0)] * len(lead) + [(0, 0), (0, LANES - real)])
    return w.reshape(lead + (HP,))


def _unpad_heads(w, real=HEAD_DIM):
    lead = w.shape[:-1]
    return w.reshape(lead + (HEADS, LANES))[..., :real].reshape(lead + (HEADS * real,))


_IN_SEGS = (("fq", 0, 256), ("fk", 256, 512), ("fv", 512, 768), ("ff", 768, 772), ("cq", 772, 1028),
            ("ckv", 1028, 1156), ("kr", 1156, 1188), ("rq", 1188, 1444), ("rk", 1444, 1700), ("rv", 1700, 1956),
            ("rg", 1956, 2212), ("sq", 2212, 2468), ("sk", 2468, 2724), ("sv", 2724, 2980))


def _pad_w_in(w):
    s = {n: w[:, a:b] for n, a, b in _IN_SEGS}
    rows = w.shape[0]
    z = lambda n: jnp.zeros((rows, n), w.dtype)
    parts = [_pad_heads(s[n]) for n in ("fq", "fk", "fv", "rq", "rk", "rv", "sq", "sk", "sv", "rg")]
    parts += [s["cq"], s["ckv"], z(HEAD_DIM), s["kr"], z(LANES - HEAD_DIM - ROPE_MLA), s["ff"], z(LANES - HEADS),
              z(NP_IN - OFF_FF - LANES)]
    return jnp.concatenate(parts, axis=1)


def _unpad_w_in(wp):
    def heads(off):
        return _unpad_heads(wp[:, off:off + HP])

    parts = [heads(OFF_FOX), heads(OFF_FOX + HP), heads(OFF_FOX + 2 * HP), wp[:, OFF_FF:OFF_FF + HEADS],
             wp[:, OFF_CQ:OFF_CQ + Q_RANK], wp[:, OFF_CKV:OFF_CKV + KV_RANK],
             wp[:, OFF_KR + HEAD_DIM:OFF_KR + HEAD_DIM + ROPE_MLA],
             heads(OFF_RET), heads(OFF_RET + HP), heads(OFF_RET + 2 * HP), heads(OFF_RG),
             heads(OFF_SB), heads(OFF_SB + HP), heads(OFF_SB + 2 * HP)]
    return jnp.concatenate(parts, axis=1)


def _pad_w_kv(w):
    w4 = w.reshape(KV_RANK, HEADS, 2 * HEAD_DIM)
    k = w4[:, :, :HEAD_DIM].reshape(KV_RANK, GROUP)
    v = w4[:, :, HEAD_DIM:].reshape(KV_RANK, GROUP)
    return jnp.concatenate([_pad_heads(k), _pad_heads(v)], axis=1)


def _unpad_w_kv(wp):
    k = _unpad_heads(wp[:, :HP]).reshape(KV_RANK, HEADS, HEAD_DIM)
    v = _unpad_heads(wp[:, HP:]).reshape(KV_RANK, HEADS, HEAD_DIM)
    return jnp.concatenate([k, v], axis=-1).reshape(KV_RANK, HEADS * 2 * HEAD_DIM)


def _pad_rows_out(w):
    w = w.reshape(4 * HEADS, HEAD_DIM, D_MODEL)
    w = jnp.pad(w, ((0, 0), (0, LANES - HEAD_DIM), (0, 0)))
    return w.reshape(4 * HP, D_MODEL)


def _unpad_rows_out(wp):
    return wp.reshape(4 * HEADS, LANES, D_MODEL)[:, :HEAD_DIM, :].reshape(D_MODEL, D_MODEL)


def _pad_gain_out(g):
    g = jnp.pad(g.reshape(4 * HEADS, HEAD_DIM), ((0, 0), (0, LANES - HEAD_DIM)))
    return g.reshape(1, 4 * HP)


def _unpad_gain_out(gp):
    return gp.reshape(4 * HEADS, LANES)[:, :HEAD_DIM].reshape(D_MODEL)


_SMALL = (("g_mix_pre", 1024), ("g_mix_post", 1024), ("g_ffn_pre", 1024), ("g_ffn_post", 1024), ("g_mix_out", 1024),
          ("g_q_lora", 256), ("g_kv_lora", 128), ("b_forget", 4))


def _pack_small(vals):
    parts = []
    for name, n in _SMALL:
        a = vals[name].astype(F32)
        if n < LANES:
            a = jnp.pad(a, ((0, 0), (0, LANES - n)))
        parts.append(a)
    return jnp.concatenate(parts, axis=1).reshape(DEPTH * SMALL_ROWS, LANES)


def _unpack_small(packed):
    flat = packed.reshape(DEPTH, SMALL_ROWS * LANES)
    out, off = {}, 0
    for name, n in _SMALL:
        out[name] = flat[:, off:off + n]
        off += max(n, LANES)
    return out


def kernel(x, positions, g_mix_pre, w_in, b_forget, g_q_lora, w_q_up, g_kv_lora, w_kv_up, g_mix_out, w_out, g_mix_post, g_ffn_pre, w_ffn_up, w_ffn_down, g_ffn_post, loss_target, m_g_mix_pre, m_w_in, m_b_forget, m_g_q_lora, m_w_q_up, m_g_kv_lora, m_w_kv_up, m_g_mix_out, m_w_out, m_g_mix_post, m_g_ffn_pre, m_w_ffn_up, m_w_ffn_down, m_g_ffn_post, v_g_mix_pre, v_w_in, v_b_forget, v_g_q_lora, v_w_q_up, v_g_kv_lora, v_w_kv_up, v_g_mix_out, v_w_out, v_g_mix_post, v_g_ffn_pre, v_w_ffn_up, v_w_ffn_down, v_g_ffn_post):
    t = x.shape[1]
    nq = t // BQ
    x0 = x[0]
    tgt = loss_target[0]
    pos = positions[0].astype(F32).reshape(t, 1)

    half_r, half_m = HEAD_DIM // 2, ROPE_MLA // 2
    invf_r = ROPE_BASE ** (-jnp.arange(half_r, dtype=F32) / half_r)
    invf_m = ROPE_BASE ** (-jnp.arange(half_m, dtype=F32) / half_m)
    invf = jnp.concatenate([invf_r, invf_r, invf_m, invf_m,
                            jnp.zeros((LANES - HEAD_DIM - ROPE_MLA,), F32)]).reshape(1, LANES)
    log_gamma = jnp.log1p(-jnp.power(2.0, -5.0 - jnp.arange(HEADS, dtype=F32)))
    lg_lanes = jnp.repeat(log_gamma, LANES).reshape(1, HP)
    lg_heads = jnp.broadcast_to(log_gamma[:, None, None], (HEADS, 1, LANES))

    big = [w_in, w_q_up, w_kv_up, w_out, w_ffn_up, w_ffn_down]
    bf = lambda w: w.astype(BF16)
    first = _exchange([bf(w_in[0]), bf(w_q_up), bf(w_kv_up)], True, "weights_gather_first")
    rest_state, rest_token = _exchange_start([bf(w_in[1]), bf(w_out), bf(w_ffn_up), bf(w_ffn_down)], True,
                                             "weights_gather_start")
    wq_g = first[1].transpose(1, 2, 0, 3).reshape(DEPTH, Q_RANK, 384)
    wkv_g = first[2].transpose(1, 2, 0, 3).reshape(DEPTH, KV_RANK, 512)

    row = lambda g: g.reshape(1, -1)
    layers = []
    for l in range(DEPTH):
        layers.append(dict(
            wq=_pad_heads(wq_g[l], 96), wkv=_pad_w_kv(wkv_g[l]),
            g_pre=row(g_mix_pre[l]), g_post=row(g_mix_post[l]), g_fpre=row(g_ffn_pre[l]), g_fpost=row(g_ffn_post[l]),
            g_out=_pad_gain_out(g_mix_out[l]), g_q=row(g_q_lora[l]), g_kv=row(g_kv_lora[l]),
            b_pad=jnp.pad(b_forget[l], (0, LANES - HEADS)).reshape(1, LANES)))
    layers[0]["win"] = _pad_w_in(first[0].reshape(D_MODEL, D_IN))

    saved = []
    xin = x0
    h = _rms_fwd(xin, layers[0]["g_pre"] + rest_token[0:1, 0:1], "rms_pre_0")
    loss_row = dx = None
    for l, p in enumerate(layers):
        s = dict(x=xin, h=h)
        proj = _mm(h, p["win"], name=f"in_proj_{l}")
        cum = _cumsum(_forget_lsf(proj, p["b_pad"], f"forget_lsf_{l}"), False, f"forget_cumsum_{l}")
        fox, ret, sb, mla, fox_t, sb_t, mla_t, ret_t = _prep_fwd(proj, cum, pos, invf, lg_lanes, p["g_q"],
                                                          p["g_kv"], p["wq"], p["wkv"], f"prep_fwd_{l}")
        oa, lse_a = _softmax_fwd(fox, fox_t, chunk_mask=False, scale=1.0, name=f"fox_fwd_{l}")
        ob, lse_b = _softmax_fwd(mla, mla_t, chunk_mask=True, scale=96.0 ** -0.5, name=f"mla_fwd_{l}")
        oc = _ret_fwd(ret, lg_heads, f"ret_fwd_{l}")
        od, sb_tot = _sb_fwd(sb, sb_t, f"sb_fwd_{l}")
        if l == 0:
            rest = _exchange_wait(rest_state, od, True, "weights_gather_wait")
            wout_g = rest[1].transpose(1, 0, 2, 3).reshape(DEPTH, D_MODEL, D_MODEL)
            wup_g = rest[2].transpose(1, 2, 0, 3).reshape(DEPTH, D_MODEL, D_FF)
            wdn_g = rest[3].transpose(1, 0, 2, 3).reshape(DEPTH, D_FF, D_MODEL)
            layers[1]["win"] = _pad_w_in(rest[0].reshape(D_MODEL, D_IN))
            for ll in range(DEPTH):
                layers[ll].update(wout=_pad_rows_out(wout_g[ll]), wup=wup_g[ll], wdn=wdn_g[ll])
        mixed = _post_fwd(oa, ob, oc, od, proj, p["g_out"], f"post_fwd_{l}")
        mix = _mm(mixed, p["wout"], name=f"out_proj_{l}", tk=2048)
        x1, h2 = _add_rms_fwd(xin, mix, p["g_post"], p["g_fpre"], f"mix_residual_{l}")
        a = _mm(h2, p["wup"], name=f"ffn_up_{l}")
        y = _mm(a, p["wdn"], name=f"ffn_down_{l}", a_fn=_relu2)
        s.update(proj=proj, fox=fox, ret=ret, sb=sb, mla=mla, fox_t=fox_t, sb_t=sb_t, mla_t=mla_t, ret_t=ret_t, oa=oa, ob=ob, oc=oc,
                 od=od, sb_tot=sb_tot, lse_a=lse_a, lse_b=lse_b, mixed=mixed, mix=mix, x1=x1, h2=h2, a=a, y=y)
        saved.append(s)
        if l + 1 < DEPTH:
            xin, h = _add_rms_fwd(x1, y, p["g_fpost"], layers[l + 1]["g_pre"], f"ffn_residual_{l}")
        else:
            loss_row, dx = _final_loss(x1, y, p["g_fpost"], tgt, "loss")

    small_g = {n: [None] * DEPTH for n, _ in _SMALL}
    big_g = [[None] * DEPTH for _ in range(6)]
    to_send = [
        lambda g: g.reshape(N_DEV, 1, D_MODEL // N_DEV, D_IN),
        lambda g: g.reshape(Q_RANK, N_DEV, 384 // N_DEV).transpose(1, 0, 2)[:, None],
        lambda g: g.reshape(KV_RANK, N_DEV, 512 // N_DEV).transpose(1, 0, 2)[:, None],
        lambda g: g.reshape(N_DEV, 1, D_MODEL // N_DEV, D_MODEL),
        lambda g: g.reshape(D_MODEL, N_DEV, D_FF // N_DEV).transpose(1, 0, 2)[:, None],
        lambda g: g.reshape(N_DEV, 1, D_FF // N_DEV, D_MODEL),
    ]
    send_of = lambda ks, l: [to_send[k](big_g[k][l]).astype(BF16) for k in ks]
    late_state = early_state = None
    order_token = jnp.zeros((1, 1), F32)
    for l in reversed(range(DEPTH)):
        p, s = layers[l], saved[l]
        dy, dg = _norm_bwd(dx, s["y"], p["g_fpost"] + order_token, None, BF16, f"ffn_post_bwd_{l}")
        small_g["g_ffn_post"][l] = dg
        da = _mm(dy, p["wdn"], name=f"ffn_down_dx_{l}", tb=True, out_dtype=BF16, epi=_drelu2, epi_in=s["a"])
        big_g[5][l] = _mm(s["a"], dy, name=f"ffn_down_dw_{l}", ta=True, a_fn=_relu2, tk=2048)
        big_g[4][l] = _mm(s["h2"], da, name=f"ffn_up_dw_{l}", ta=True, tk=2048)
        dh2 = _mm(da, p["wup"], name=f"ffn_up_dx_{l}", tb=True)
        dx1, dg = _norm_bwd(dh2, s["x1"], p["g_fpre"], dx, F32, f"ffn_pre_bwd_{l}")
        small_g["g_ffn_pre"][l] = dg
        dmix, dg = _norm_bwd(dx1, s["mix"], p["g_post"], None, BF16, f"mix_post_bwd_{l}")
        small_g["g_mix_post"][l] = dg
        dmixed = _mm(dmix, p["wout"], name=f"out_proj_dx_{l}", tb=True)
        big_g[3][l] = _unpad_rows_out(_mm(s["mixed"], dmix, name=f"out_proj_dw_{l}", ta=True, tk=2048))
        g_out = p["g_out"]
        if l == 0:
            early_state, early_token = _exchange_start(send_of((3, 4, 5), 0), False, "grads_layer0_early_start")
            g_out = g_out + early_token[0:1, 0:1]
        doa, dob, doc, dod, dla, dlb, drg, dgo = _post_bwd(dmixed, s["oa"], s["ob"], s["oc"], s["od"], s["proj"],
                                                           g_out, f"post_bwd_{l}")
        small_g["g_mix_out"][l] = _unpad_gain_out(dgo).reshape(1, D_MODEL)
        as_rows = lambda a: a[:, :HEADS].T.reshape(HEADS, nq, 1, BQ)
        dfq, dfk, dfv, dcum_k, dcum_q = _softmax_bwd(s["fox"], s["fox_t"], doa, s["lse_a"], as_rows(dla), bias=True,
                                                     chunk_mask=False, scale=1.0, name=f"fox_bwd_{l}")
        dmq, dmk, dmv = _softmax_bwd(s["mla"], s["mla_t"], dob, s["lse_b"], as_rows(dlb), bias=False, chunk_mask=True,
                                     scale=96.0 ** -0.5, name=f"mla_bwd_{l}")
        drq, drk, drv = _ret_bwd(s["ret"], s["ret_t"], lg_heads, doc, f"ret_bwd_{l}")
        dsq, dsk, dsv = _sb_bwd(s["sb"], s["sb_t"], dod, s["sb_tot"], f"sb_bwd_{l}")
        dcum_q = jnp.pad(dcum_q.reshape(HEADS, t).T, ((0, 0), (0, LANES - HEADS)))
        dlsf = _cumsum(dcum_q, True, f"forget_cumsum_bwd_{l}", partials=dcum_k)
        dproj, dwq, dwkv, dgq, dgkv, dbf = _prep_bwd(
            (dfq, dfk, dfv), (drq, drk, drv), (dsq, dsk, dsv), (dmq, dmk, dmv), drg, dlsf, s["proj"], pos, invf,
            lg_lanes, p["b_pad"], p["g_q"], p["g_kv"], p["wq"], p["wkv"], f"prep_bwd_{l}")
        small_g["g_q_lora"][l] = dgq
        small_g["g_kv_lora"][l] = dgkv
        small_g["b_forget"][l] = dbf[:, :HEADS]
        big_g[1][l] = _unpad_heads(dwq, 96)
        big_g[2][l] = _unpad_w_kv(dwkv)
        big_g[0][l] = _unpad_w_in(_mm(s["h"], dproj, name=f"in_proj_dw_{l}", ta=True, tk=2048))
        dh = _mm(dproj, p["win"], name=f"in_proj_dx_{l}", tb=True)
        dx, dg = _norm_bwd(dh, s["x"], p["g_pre"], dx1, F32, f"mix_pre_bwd_{l}")
        small_g["g_mix_pre"][l] = dg
        if l == DEPTH - 1:
            late_state, late_token = _exchange_start(send_of(range(6), l), False, "grads_layer1_start")
            order_token = late_token[0:1, 0:1]
    grad_x = dx.reshape(1, t, D_MODEL)

    last = _exchange(send_of((0, 1, 2), 0), False, "grads_layer0_rest")
    late = _exchange_wait(late_state, dx, False, "grads_layer1_wait")
    early = _exchange_wait(early_state, dx, False, "grads_layer0_early_wait")
    recv = [jnp.concatenate([(last[k] if k < 3 else early[k - 3]), late[k]], axis=1) for k in range(6)]
    ms = [m_w_in, m_w_q_up, m_w_kv_up, m_w_out, m_w_ffn_up, m_w_ffn_down]
    vs = [v_w_in, v_w_q_up, v_w_kv_up, v_w_out, v_w_ffn_up, v_w_ffn_down]
    names = ["w_in", "w_q_up", "w_kv_up", "w_out", "w_ffn_up", "w_ffn_down"]
    res = {}
    for k, n in enumerate(names):
        res[n] = _reduce_adam(recv[k], big[k], ms[k], vs[k], f"adamw_{n}")

    small_w = dict(g_mix_pre=g_mix_pre, g_mix_post=g_mix_post, g_ffn_pre=g_ffn_pre, g_ffn_post=g_ffn_post,
                   g_mix_out=g_mix_out, g_q_lora=g_q_lora, g_kv_lora=g_kv_lora, b_forget=b_forget)
    small_m = dict(g_mix_pre=m_g_mix_pre, g_mix_post=m_g_mix_post, g_ffn_pre=m_g_ffn_pre, g_ffn_post=m_g_ffn_post,
                   g_mix_out=m_g_mix_out, g_q_lora=m_g_q_lora, g_kv_lora=m_g_kv_lora, b_forget=m_b_forget)
    small_v = dict(g_mix_pre=v_g_mix_pre, g_mix_post=v_g_mix_post, g_ffn_pre=v_g_ffn_pre, g_ffn_post=v_g_ffn_post,
                   g_mix_out=v_g_mix_out, g_q_lora=v_g_q_lora, g_kv_lora=v_g_kv_lora, b_forget=v_b_forget)
    n_small = DEPTH * SMALL_ROWS
    extra = lambda a: jnp.concatenate([a, jnp.zeros((8, LANES), F32)], axis=0)
    part = jnp.concatenate([_pack_small({n: jnp.concatenate(small_g[n], axis=0) for n, _ in _SMALL}),
                            jnp.broadcast_to(loss_row, (8, LANES))], axis=0)
    sres = _small_allreduce_adam(part, extra(_pack_small(small_w)), extra(_pack_small(small_m)),
                                 extra(_pack_small(small_v)), "small_allreduce_adamw")
    loss = sres[0][n_small, 0]
    sg, sd, sm, sv = [_unpack_small(a[:n_small]) for a in sres]
    for n, _ in _SMALL:
        res[n] = [sg[n], sd[n], sm[n], sv[n]]

    order = ["g_mix_pre", "w_in", "b_forget", "g_q_lora", "w_q_up", "g_kv_lora", "w_kv_up", "g_mix_out", "w_out",
             "g_mix_post", "g_ffn_pre", "w_ffn_up", "w_ffn_down", "g_ffn_post"]
    outs = [loss, grad_x]
    for idx in range(4):
        outs += [res[n][idx] for n in order]
    return tuple(outs)
```

```python
import functools
import math

import numpy as np
import jax
import jax.numpy as jnp
from jax import lax
from jax.experimental import pallas as pl
from jax.experimental.pallas import tpu as pltpu

F32 = jnp.float32
BF16 = jnp.bfloat16

D_MODEL = 1024
DEPTH = 2
N_DEV = 8
GROUP = 256
HEADS = 4
HEAD_DIM = 64
LANES = 128
HP = HEADS * LANES
QKV = 3 * HP
Q_RANK = 256
KV_RANK = 128
ROPE_MLA = 32
D_FF = 4096
D_IN = 2980
CHUNK_SHIFT = 6
EPS = 1e-6
ROPE_BASE = 10000.0
NEG = -1e30

OFF_FOX, OFF_RET, OFF_SB = 0, QKV, 2 * QKV
OFF_RG = 3 * QKV
OFF_CQ = OFF_RG + HP
OFF_CKV = OFF_CQ + Q_RANK
OFF_KR = OFF_CKV + LANES
OFF_FF = OFF_KR + LANES
NP_IN = 6144

BQ = 256
TKS = 128
TM = 256
VMEM_LIMIT = 48 * 1024 * 1024

ADAM_LR, ADAM_B1, ADAM_B2, ADAM_EPS, ADAM_WD, ADAM_STEP = 0.001, 0.9, 0.999, 1e-08, 0.01, 10
ADAM_C1 = 1.0 - ADAM_B1 ** ADAM_STEP
ADAM_C2 = 1.0 - ADAM_B2 ** ADAM_STEP

SMALL_ROWS = 44

NT = (((1,), (1,)), ((), ()))
TN = (((0,), (0,)), ((), ()))


def _cp(*sem):
    return pltpu.CompilerParams(dimension_semantics=sem if sem else None, vmem_limit_bytes=VMEM_LIMIT)


def _bdot(a, b, dn=None):
    if dn is None:
        return jnp.dot(a, b, preferred_element_type=F32)
    return lax.dot_general(a, b, dn, preferred_element_type=F32)


def _split2(x):
    hi = x.astype(BF16)
    lo = (x - hi.astype(F32)).astype(BF16)
    return hi, lo


def _mm(a, b, *, name, ta=False, tb=False, out_dtype=F32, a_fn=None, epi=None, epi_in=None,
        tm=1024, tn=1024, tk=1024):
    m, k = (a.shape[1], a.shape[0]) if ta else a.shape
    n = b.shape[0] if tb else b.shape[1]
    tm, tn, tk = min(tm, m), min(tn, n), min(tk, k)
    assert m % tm == 0 and n % tn == 0 and k % tk == 0, (name, m, n, k)
    nk = k // tk
    dn = (((0 if ta else 1,), (1 if tb else 0,)), ((), ()))

    def body(*refs):
        if epi is None:
            a_ref, b_ref, o_ref = refs[:3]
            e_ref = None
            rest = refs[3:]
        else:
            a_ref, b_ref, e_ref, o_ref = refs[:4]
            rest = refs[4:]
        av = a_ref[...]
        if a_fn is not None:
            av = a_fn(av)
        part = lax.dot_general(av.astype(BF16), b_ref[...].astype(BF16), dn, preferred_element_type=F32)

        def finish(r):
            if epi is not None:
                r = epi(r, e_ref[...])
            o_ref[...] = r.astype(out_dtype)

        if nk == 1:
            finish(part)
        else:
            acc_ref = rest[0]
            kk = pl.program_id(2)

            @pl.when(kk == 0)
            def _():
                acc_ref[...] = part

            @pl.when(kk > 0)
            def _():
                acc_ref[...] += part

            @pl.when(kk == nk - 1)
            def _():
                finish(acc_ref[...])

    a_spec = pl.BlockSpec((tk, tm), lambda i, j, kk: (kk, i)) if ta else pl.BlockSpec((tm, tk), lambda i, j, kk: (i, kk))
    b_spec = pl.BlockSpec((tn, tk), lambda i, j, kk: (j, kk)) if tb else pl.BlockSpec((tk, tn), lambda i, j, kk: (kk, j))
    o_spec = pl.BlockSpec((tm, tn), lambda i, j, kk: (i, j))
    in_specs = [a_spec, b_spec]
    args = [a, b]
    if epi is not None:
        in_specs.append(o_spec)
        args.append(epi_in)
    return pl.pallas_call(
        body, name=name, grid=(m // tm, n // tn, nk),
        in_specs=in_specs, out_specs=o_spec,
        out_shape=jax.ShapeDtypeStruct((m, n), out_dtype),
        scratch_shapes=[pltpu.VMEM((tm, tn), F32)] if nk > 1 else [],
        compiler_params=_cp("parallel", "parallel", "arbitrary"),
    )(*args)


def _relu2(v):
    r = jnp.maximum(v, 0.0)
    return r * r


def _drelu2(du, av):
    return du * (2.0 * jnp.maximum(av, 0.0))


def _rms(v, g):
    r = lax.rsqrt(jnp.mean(v * v, axis=-1, keepdims=True) + EPS)
    return v * r * g


def _row_spec(w):
    return pl.BlockSpec((TM, w), lambda i: (i, 0))


def _vec_spec(w):
    return pl.BlockSpec((1, w), lambda i: (0, 0))


def _rms_fwd(x, g, name):
    t, d = x.shape

    def body(x_ref, g_ref, h_ref):
        h_ref[...] = _rms(x_ref[...], g_ref[...]).astype(BF16)

    return pl.pallas_call(
        body, name=name, grid=(t // TM,), in_specs=[_row_spec(d), _vec_spec(d)], out_specs=_row_spec(d),
        out_shape=jax.ShapeDtypeStruct((t, d), BF16), compiler_params=_cp("parallel"))(x, g)


def _add_rms_fwd(x, y, g1, g2, name):
    t, d = x.shape

    def body(x_ref, y_ref, g1_ref, g2_ref, xn_ref, h_ref):
        xn = x_ref[...] + _rms(y_ref[...], g1_ref[...])
        xn_ref[...] = xn
        h_ref[...] = _rms(xn, g2_ref[...]).astype(BF16)

    return pl.pallas_call(
        body, name=name, grid=(t // TM,),
        in_specs=[_row_spec(d), _row_spec(d), _vec_spec(d), _vec_spec(d)],
        out_specs=[_row_spec(d), _row_spec(d)],
        out_shape=[jax.ShapeDtypeStruct((t, d), F32), jax.ShapeDtypeStruct((t, d), BF16)],
        compiler_params=_cp("parallel"))(x, y, g1, g2)


def _final_loss(x, y, g, tgt, name):
    t, d = x.shape

    def body(x_ref, y_ref, g_ref, t_ref, l_ref, dx_ref):
        @pl.when(pl.program_id(0) == 0)
        def _():
            l_ref[...] = jnp.zeros_like(l_ref)

        err = x_ref[...] + _rms(y_ref[...], g_ref[...]) - t_ref[...]
        dx_ref[...] = err * (1.0 / d)
        l_ref[...] += jnp.sum(jnp.sum(err * err, axis=1, keepdims=True), axis=0, keepdims=True) * (0.5 / d)

    return pl.pallas_call(
        body, name=name, grid=(t // TM,),
        in_specs=[_row_spec(d), _row_spec(d), _vec_spec(d), _row_spec(d)],
        out_specs=[pl.BlockSpec((1, LANES), lambda i: (0, 0)), _row_spec(d)],
        out_shape=[jax.ShapeDtypeStruct((1, LANES), F32), jax.ShapeDtypeStruct((t, d), F32)],
        compiler_params=_cp("arbitrary"))(x, y, g, tgt)


def _rms_bwd_vals(dn, v, g):
    w = v.shape[-1]
    r = lax.rsqrt(jnp.mean(v * v, axis=-1, keepdims=True) + EPS)
    vh = v * r
    dgp = jnp.sum(dn * vh, axis=0, keepdims=True)
    dvh = dn * g
    dv = r * (dvh - vh * (jnp.sum(dvh * vh, axis=-1, keepdims=True) * (1.0 / w)))
    return dv, dgp


def _norm_bwd(dn, v, g, resid, out_dtype, name):
    t, d = v.shape
    has_res = resid is not None

    def body(*refs):
        if has_res:
            dn_ref, v_ref, g_ref, r_ref, dv_ref, dg_ref = refs
        else:
            dn_ref, v_ref, g_ref, dv_ref, dg_ref = refs

        @pl.when(pl.program_id(0) == 0)
        def _():
            dg_ref[...] = jnp.zeros_like(dg_ref)

        dv, dgp = _rms_bwd_vals(dn_ref[...].astype(F32), v_ref[...], g_ref[...])
        if has_res:
            dv = dv + r_ref[...]
        dv_ref[...] = dv.astype(out_dtype)
        dg_ref[...] += dgp

    in_specs = [_row_spec(d), _row_spec(d), _vec_spec(d)] + ([_row_spec(d)] if has_res else [])
    args = [dn, v, g] + ([resid] if has_res else [])
    return pl.pallas_call(
        body, name=name, grid=(t // TM,), in_specs=in_specs,
        out_specs=[_row_spec(d), _vec_spec(d)],
        out_shape=[jax.ShapeDtypeStruct((t, d), out_dtype), jax.ShapeDtypeStruct((1, d), F32)],
        compiler_params=_cp("arbitrary"))(*args)


def _rope_trig(pos, invf):
    ang = pos * invf
    return jnp.cos(ang), jnp.sin(ang)


def _rope_tables(trig, w, lo, half):
    c, s = trig
    lane = lax.broadcasted_iota(jnp.int32, c.shape, 1)
    active = (lane >= lo) & (lane < lo + 2 * half)
    cos = jnp.concatenate([jnp.where(active, c, 1.0)] * (w // LANES), axis=1)
    sin = jnp.concatenate([jnp.where(active, s, 0.0)] * (w // LANES), axis=1)
    lanes = lax.broadcasted_iota(jnp.int32, (c.shape[0], w), 1) & (LANES - 1)
    first = (lanes >= lo) & (lanes < lo + half)
    second = (lanes >= lo + half) & (lanes < lo + 2 * half)
    return cos, sin, first, second


def _rope_apply(v, cos, sin, first, second, half, sign):
    w = v.shape[-1]
    up = pltpu.roll(v, w - half, 1)
    dn = pltpu.roll(v, half, 1)
    rot = jnp.where(first, -up, jnp.where(second, dn, 0.0))
    return v * cos + rot * (sin * sign)


def _forget_lsf(proj, b_pad, name):
    t = proj.shape[0]

    def body(ff_ref, b_ref, o_ref):
        f = ff_ref[...] + b_ref[...]
        o_ref[...] = -(jnp.maximum(-f, 0.0) + jnp.log(1.0 + jnp.exp(-jnp.abs(f))))

    return pl.pallas_call(
        body, name=name, grid=(t // TM,),
        in_specs=[pl.BlockSpec((TM, LANES), lambda i: (i, OFF_FF // LANES)), _vec_spec(LANES)],
        out_specs=_row_spec(LANES), out_shape=jax.ShapeDtypeStruct((t, LANES), F32),
        compiler_params=_cp("parallel"))(proj, b_pad)


def _split3(c):
    hi = c.astype(BF16).astype(F32)
    mid = (c - hi).astype(BF16).astype(F32)
    return hi, mid, (c - hi) - mid


def _transposed_tiles(dst, row0, blocks, width):
    for b, blk in enumerate(blocks):
        bt = blk.T.astype(BF16)
        rows = bt.shape[0]
        for w in range(TM // width):
            dst[w, row0 + b * rows:row0 + (b + 1) * rows, :] = bt[:, w * width:(w + 1) * width]


def _prep_fwd(proj, cum, pos, invf, lg_lanes, g_q, g_kv, wq_pad, wkv_pad, name):
    t = proj.shape[0]

    def body(fox_ref, ret_ref, sb_ref, cq_ref, ckv_ref, kr_ref, cum_ref, pos_ref, invf_ref, lg_ref,
             gq_ref, gkv_ref, wq_ref, wkv_ref,
             ofox_ref, oret_ref, osb_ref, omla_ref, ofoxt_ref, osbt_ref, omlat_ref, orett_ref):
        pos_v = pos_ref[...]
        osb_ref[:, 0:HP] = (sb_ref[:, 0:HP] * 0.125).astype(BF16)
        osb_ref[:, HP:QKV] = sb_ref[:, HP:QKV].astype(BF16)
        _transposed_tiles(osbt_ref, 0, [sb_ref[:, HP:2 * HP], sb_ref[:, 2 * HP:QKV]], TKS)
        lane = lax.broadcasted_iota(jnp.int32, (TM, LANES), 1)
        cumv = cum_ref[...]
        fq, fk = [], []
        for hb in range(HEADS):
            hi, mid, lo = _split3(cumv[:, hb:hb + 1])
            q = fox_ref[:, hb * LANES:(hb + 1) * LANES] * 0.125
            k = fox_ref[:, HP + hb * LANES:HP + (hb + 1) * LANES]
            ones_q = (lane >= HEAD_DIM) & (lane < HEAD_DIM + 3)
            ones_k = (lane >= HEAD_DIM + 3) & (lane < HEAD_DIM + 6)
            q = jnp.where(ones_q, 1.0, q)
            k = jnp.where(ones_k, 1.0, k)
            for n, part in enumerate((hi, mid, lo)):
                q = jnp.where(lane == HEAD_DIM + 3 + n, part, q)
                k = jnp.where(lane == HEAD_DIM + n, -part, k)
            fq.append(q)
            fk.append(k)
        fk = jnp.concatenate(fk, axis=1)
        ofox_ref[:, 0:HP] = jnp.concatenate(fq, axis=1).astype(BF16)
        ofox_ref[:, HP:2 * HP] = fk.astype(BF16)
        ofox_ref[:, 2 * HP:QKV] = fox_ref[:, 2 * HP:QKV].astype(BF16)
        _transposed_tiles(ofoxt_ref, 0, [fk, fox_ref[:, 2 * HP:QKV]], BQ)
        trig = _rope_trig(pos_v, invf_ref[...])
        cos, sin, first, second = _rope_tables(trig, HP, 0, HEAD_DIM // 2)
        nloc = lax.broadcasted_iota(jnp.int32, (TM, 1), 0).astype(F32)
        dec = lg_ref[...] * nloc
        rq = _rope_apply(ret_ref[:, 0:HP], cos, sin, first, second, HEAD_DIM // 2, 1.0)
        rk = _rope_apply(ret_ref[:, HP:2 * HP], cos, sin, first, second, HEAD_DIM // 2, 1.0)
        oret_ref[:, 0:HP] = (rq * jnp.exp(dec)).astype(BF16)
        rk = rk * 0.125 * jnp.exp(-dec)
        oret_ref[:, HP:2 * HP] = rk.astype(BF16)
        _transposed_tiles(orett_ref, 0, [rk], BQ)
        oret_ref[:, 2 * HP:QKV] = ret_ref[:, 2 * HP:QKV].astype(BF16)
        cosm, sinm, firstm, secondm = _rope_tables(trig, HP, HEAD_DIM, ROPE_MLA // 2)
        cqn = _rms(cq_ref[...], gq_ref[...]).astype(BF16)
        qm = _bdot(cqn, wq_ref[...])
        omla_ref[:, 0:HP] = _rope_apply(qm, cosm, sinm, firstm, secondm, ROPE_MLA // 2, 1.0).astype(BF16)
        ckvn = _rms(ckv_ref[...], gkv_ref[...]).astype(BF16)
        kv = _bdot(ckvn, wkv_ref[...])
        krr = _rope_apply(kr_ref[...], cosm[:, 0:LANES], sinm[:, 0:LANES], firstm[:, 0:LANES],
                          secondm[:, 0:LANES], ROPE_MLA // 2, 1.0)
        mk = kv[:, 0:HP] + jnp.concatenate([krr] * HEADS, axis=1)
        omla_ref[:, HP:2 * HP] = mk.astype(BF16)
        omla_ref[:, 2 * HP:QKV] = kv[:, HP:2 * HP].astype(BF16)
        _transposed_tiles(omlat_ref, 0, [mk, kv[:, HP:2 * HP]], BQ)

    def seg(off, w):
        return pl.BlockSpec((TM, w), lambda i, o=off // w: (i, o))

    def full(shape):
        return pl.BlockSpec(shape, lambda i: (0,) * len(shape))

    def tiles(width):
        return pl.BlockSpec((TM // width, 2 * HP, width), lambda i: (i, 0, 0))

    in_specs = [seg(OFF_FOX, QKV), seg(OFF_RET, QKV), seg(OFF_SB, QKV), seg(OFF_CQ, Q_RANK), seg(OFF_CKV, LANES),
                seg(OFF_KR, LANES), _row_spec(LANES), pl.BlockSpec((TM, 1), lambda i: (i, 0)),
                full((1, LANES)), full((1, HP)), full((1, Q_RANK)), full((1, KV_RANK)),
                full((Q_RANK, HP)), full((KV_RANK, 2 * HP))]
    out_specs = [_row_spec(QKV)] * 4 + [tiles(BQ), tiles(TKS), tiles(BQ),
                                        pl.BlockSpec((1, HP, BQ), lambda i: (i, 0, 0))]
    out_shape = [jax.ShapeDtypeStruct((t, QKV), BF16)] * 4 + [
        jax.ShapeDtypeStruct((t // BQ, 2 * HP, BQ), BF16), jax.ShapeDtypeStruct((t // TKS, 2 * HP, TKS), BF16),
        jax.ShapeDtypeStruct((t // BQ, 2 * HP, BQ), BF16), jax.ShapeDtypeStruct((t // BQ, HP, BQ), BF16)]
    return pl.pallas_call(
        body, name=name, grid=(t // TM,), in_specs=in_specs, out_specs=out_specs, out_shape=out_shape,
        compiler_params=_cp("parallel"))(proj, proj, proj, proj, proj, proj, cum, pos, invf, lg_lanes,
                                         g_q, g_kv, wq_pad, wkv_pad)


def _prep_bwd(dfox, dret, dsb, dmla, drg, dlsf, proj, pos, invf, lg_lanes, b_pad, g_q, g_kv,
              wq_pad, wkv_pad, name):
    t = proj.shape[0]

    def body(dfq, dfk, dfv, drq, drk, drv, dsq, dsk, dsv, dmq, dmk, dmv, drg_ref, dlsf_ref,
             cq_ref, ckv_ref, ff_ref, pos_ref, invf_ref, lg_ref, b_ref, gq_ref, gkv_ref, wq_ref, wkv_ref,
             dp_ref, dwq_ref, dwkv_ref, dgq_ref, dgkv_ref, dbf_ref):
        @pl.when(pl.program_id(0) == 0)
        def _():
            dwq_ref[...] = jnp.zeros_like(dwq_ref)
            dwkv_ref[...] = jnp.zeros_like(dwkv_ref)
            dgq_ref[...] = jnp.zeros_like(dgq_ref)
            dgkv_ref[...] = jnp.zeros_like(dgkv_ref)
            dbf_ref[...] = jnp.zeros_like(dbf_ref)

        pos_v = pos_ref[...]
        for off, (dq, dk, dv) in ((OFF_FOX, (dfq, dfk, dfv)), (OFF_SB, (dsq, dsk, dsv))):
            dp_ref[:, off:off + HP] = (dq[...] * 0.125).astype(BF16)
            dp_ref[:, off + HP:off + 2 * HP] = dk[...].astype(BF16)
            dp_ref[:, off + 2 * HP:off + QKV] = dv[...].astype(BF16)
        trig = _rope_trig(pos_v, invf_ref[...])
        cos, sin, first, second = _rope_tables(trig, HP, 0, HEAD_DIM // 2)
        nloc = lax.broadcasted_iota(jnp.int32, (TM, 1), 0).astype(F32)
        dec = lg_ref[...] * nloc
        dq = _rope_apply(drq[...] * jnp.exp(dec), cos, sin, first, second, HEAD_DIM // 2, -1.0)
        dk = _rope_apply(drk[...] * (0.125 * jnp.exp(-dec)), cos, sin, first, second, HEAD_DIM // 2, -1.0)
        dp_ref[:, OFF_RET:OFF_RET + HP] = dq.astype(BF16)
        dp_ref[:, OFF_RET + HP:OFF_RET + 2 * HP] = dk.astype(BF16)
        dp_ref[:, OFF_RET + 2 * HP:OFF_RET + QKV] = drv[...].astype(BF16)
        dp_ref[:, OFF_RG:OFF_RG + HP] = drg_ref[...].astype(BF16)
        cosm, sinm, firstm, secondm = _rope_tables(trig, HP, HEAD_DIM, ROPE_MLA // 2)
        dql = _rope_apply(dmq[...], cosm, sinm, firstm, secondm, ROPE_MLA // 2, -1.0).astype(BF16)
        cq = cq_ref[...]
        cqn = _rms(cq, gq_ref[...]).astype(BF16)
        dwq_ref[...] += _bdot(cqn, dql, TN)
        dcqn = _bdot(dql, wq_ref[...], NT)
        dcq, dgq = _rms_bwd_vals(dcqn, cq, gq_ref[...])
        dgq_ref[...] += dgq
        dp_ref[:, OFF_CQ:OFF_CQ + Q_RANK] = dcq.astype(BF16)
        dkm = dmk[...]
        dkv = jnp.concatenate([dkm, dmv[...]], axis=1).astype(BF16)
        ckv = ckv_ref[...]
        ckvn = _rms(ckv, gkv_ref[...]).astype(BF16)
        dwkv_ref[...] += _bdot(ckvn, dkv, TN)
        dckvn = _bdot(dkv, wkv_ref[...], NT)
        dckv, dgkv = _rms_bwd_vals(dckvn, ckv, gkv_ref[...])
        dgkv_ref[...] += dgkv
        dp_ref[:, OFF_CKV:OFF_CKV + LANES] = dckv.astype(BF16)
        dkr = dkm[:, 0:LANES] + dkm[:, LANES:2 * LANES] + dkm[:, 2 * LANES:3 * LANES] + dkm[:, 3 * LANES:HP]
        act = firstm[:, 0:LANES] | secondm[:, 0:LANES]
        dkr = jnp.where(act, dkr, 0.0)
        dkr = _rope_apply(dkr, cosm[:, 0:LANES], sinm[:, 0:LANES], firstm[:, 0:LANES], secondm[:, 0:LANES],
                          ROPE_MLA // 2, -1.0)
        dp_ref[:, OFF_KR:OFF_KR + LANES] = dkr.astype(BF16)
        f = ff_ref[...] + b_ref[...]
        dff = dlsf_ref[...] / (1.0 + jnp.exp(f))
        dbf_ref[...] += jnp.sum(dff, axis=0, keepdims=True)
        dp_ref[:, OFF_FF:OFF_FF + LANES] = dff.astype(BF16)
        dp_ref[:, OFF_FF + LANES:NP_IN] = jnp.zeros((TM, NP_IN - OFF_FF - LANES), BF16)

    def seg(off, w):
        return pl.BlockSpec((TM, w), lambda i, o=off // w: (i, o))

    def full(shape):
        return pl.BlockSpec(shape, lambda i: (0,) * len(shape))

    hp_spec = _row_spec(HP)
    in_specs = [hp_spec] * 13 + [_row_spec(LANES), seg(OFF_CQ, Q_RANK), seg(OFF_CKV, LANES), seg(OFF_FF, LANES),
                                 pl.BlockSpec((TM, 1), lambda i: (i, 0)),
                                 full((1, LANES)), full((1, HP)), full((1, LANES)), full((1, Q_RANK)),
                                 full((1, KV_RANK)), full((Q_RANK, HP)), full((KV_RANK, 2 * HP))]
    out_specs = [_row_spec(NP_IN), full((Q_RANK, HP)), full((KV_RANK, 2 * HP)), full((1, Q_RANK)),
                 full((1, KV_RANK)), full((1, LANES))]
    out_shape = [jax.ShapeDtypeStruct((t, NP_IN), BF16), jax.ShapeDtypeStruct((Q_RANK, HP), F32),
                 jax.ShapeDtypeStruct((KV_RANK, 2 * HP), F32), jax.ShapeDtypeStruct((1, Q_RANK), F32),
                 jax.ShapeDtypeStruct((1, KV_RANK), F32), jax.ShapeDtypeStruct((1, LANES), F32)]
    return pl.pallas_call(
        body, name=name, grid=(t // TM,), in_specs=in_specs, out_specs=out_specs, out_shape=out_shape,
        compiler_params=_cp("arbitrary"))(*dfox, *dret, *dsb, *dmla, drg, dlsf, proj, proj, proj, pos, invf,
                                          lg_lanes, b_pad, g_q, g_kv, wq_pad, wkv_pad)


def _cumsum(x, reverse, name, partials=None):
    t, w = x.shape
    n = t // TM
    xs = [x] if partials is None else [x, partials]

    def body(*refs):
        x_refs, o_ref, carry = refs[:len(xs)], refs[len(xs)], refs[len(xs) + 1]

        @pl.when(pl.program_id(0) == 0)
        def _():
            carry[...] = jnp.zeros_like(carry)

        r = lax.broadcasted_iota(jnp.int32, (TM, TM), 0)
        c = lax.broadcasted_iota(jnp.int32, (TM, TM), 1)
        tri = jnp.where((r <= c) if reverse else (r >= c), 1.0, 0.0).astype(BF16)
        v = x_refs[0][...]
        if partials is not None:
            lane = lax.broadcasted_iota(jnp.int32, (TM, LANES), 1)
            for hb in range(HEADS):
                v = v + jnp.where(lane == hb, jnp.sum(x_refs[1][:, _hs(hb)], axis=1, keepdims=True), 0.0)
        hi = v.astype(BF16)
        r1 = v - hi.astype(F32)
        mid = r1.astype(BF16)
        lo = (r1 - mid.astype(F32)).astype(BF16)
        cs = _bdot(tri, hi) + _bdot(tri, mid) + _bdot(tri, lo) + carry[...]
        o_ref[...] = cs
        carry[...] = cs[0:1, :] if reverse else cs[TM - 1:TM, :]

    imap = (lambda i: (n - 1 - i, 0)) if reverse else (lambda i: (i, 0))
    return pl.pallas_call(
        body, name=name, grid=(n,), in_specs=[pl.BlockSpec((TM, a.shape[1]), imap) for a in xs],
        out_specs=pl.BlockSpec((TM, w), imap),
        out_shape=jax.ShapeDtypeStruct((t, w), F32), scratch_shapes=[pltpu.VMEM((1, w), F32)],
        compiler_params=_cp("arbitrary"))(*xs)


HB_FWD = 4
HB_BWD = 2
HB_SB_FWD = 4
BQS = 512


def _q_spec(hb, bq=BQ):
    return pl.BlockSpec((bq, hb * LANES), lambda g, i: (i, g))


def _kv_spec(t, which, hb):
    return pl.BlockSpec((t, hb * LANES), lambda g, i, w=which: (0, w * (HEADS // hb) + g))


def _acc_spec(t, hb):
    return pl.BlockSpec((t, hb * LANES), lambda g, i: (0, g))


def _hs(hh):
    return slice(hh * LANES, (hh + 1) * LANES)


def _tile_iota(rows, cols):
    return (lax.broadcasted_iota(jnp.int32, (rows, cols), 0), lax.broadcasted_iota(jnp.int32, (rows, cols), 1))


def _kvt_spec(nkv, width, which, hb):
    return pl.BlockSpec((nkv, hb * LANES, width), lambda g, i, w=which: (0, w * (HEADS // hb) + g, 0))


def _qrow_spec(hb, bq=BQ):
    return pl.BlockSpec((hb, 1, 1, bq), lambda g, i: (g, i, 0, 0))


def _vis_t(chunk_mask):
    r, c = _tile_iota(BQ, BQ)
    return ((r >> CHUNK_SHIFT) <= (c >> CHUNK_SHIFT)) if chunk_mask else (r <= c)


def _softmax_fwd(qkv, kvt, *, chunk_mask, scale, name):
    t = qkv.shape[0]
    nq = t // BQ
    hb = HB_FWD

    def body(q_ref, k_ref, vt_ref, o_ref, lse_ref, m_sc, l_sc, acc_sc):
        i = pl.program_id(1)
        m_sc[...] = jnp.full((hb, 1, BQ), NEG, F32)
        l_sc[...] = jnp.zeros((hb, 1, BQ), F32)
        acc_sc[...] = jnp.zeros((hb, LANES, BQ), F32)

        def tile(j, masked):
            off = pl.multiple_of(j * BQ, BQ)
            vis = _vis_t(chunk_mask) if masked else None
            ss = [_bdot(k_ref[pl.ds(off, BQ), _hs(hh)], q_ref[:, _hs(hh)], NT) for hh in range(hb)]
            ps, alphas = [], []
            for hh in range(hb):
                s = ss[hh]
                if scale != 1.0:
                    s = s * scale
                if masked:
                    s = jnp.where(vis, s, NEG)
                m_old = m_sc[hh]
                m_new = jnp.maximum(m_old, jnp.max(s, axis=0, keepdims=True))
                alpha = jnp.exp(m_old - m_new)
                p = jnp.exp(s - m_new)
                l_sc[hh] = alpha * l_sc[hh] + jnp.sum(p, axis=0, keepdims=True)
                m_sc[hh] = m_new
                ps.append(p.astype(BF16))
                alphas.append(alpha)
            for hh in range(hb):
                acc_sc[hh] = alphas[hh] * acc_sc[hh] + _bdot(vt_ref[j, _hs(hh), :], ps[hh])

        def loop(j, carry):
            tile(j, False)
            return carry

        lax.fori_loop(0, i, loop, 0)
        tile(i, True)
        for hh in range(hb):
            l = l_sc[hh]
            o_ref[:, _hs(hh)] = (acc_sc[hh] / l).T
            lse_ref[hh, 0] = m_sc[hh] + jnp.log(l)

    return pl.pallas_call(
        body, name=name, grid=(HEADS // hb, nq),
        in_specs=[_q_spec(hb), _kv_spec(t, 1, hb), _kvt_spec(nq, BQ, 1, hb)],
        out_specs=[_q_spec(hb), _qrow_spec(hb)],
        out_shape=[jax.ShapeDtypeStruct((t, HP), F32), jax.ShapeDtypeStruct((HEADS, nq, 1, BQ), F32)],
        scratch_shapes=[pltpu.VMEM((hb, 1, BQ), F32), pltpu.VMEM((hb, 1, BQ), F32), pltpu.VMEM((hb, LANES, BQ), F32)],
        compiler_params=_cp("parallel", "arbitrary"))(qkv, qkv, kvt)


def _softmax_bwd(qkv, kvt, do, lse, delta, *, bias, chunk_mask, scale, name):
    t = qkv.shape[0]
    nq = t // BQ
    hb = HB_BWD

    def body(*refs):
        if bias:
            (q_ref, k_ref, v_ref, kt_ref, do_ref, lse_ref, dl_ref, dq_ref, dk_ref, dv_ref, dck_ref, dcq_ref,
             dq_sc, dcq_sc) = refs
            dcq_sc[...] = jnp.zeros((hb, 1, BQ), F32)
        else:
            q_ref, k_ref, v_ref, kt_ref, do_ref, lse_ref, dl_ref, dq_ref, dk_ref, dv_ref, dq_sc = refs
        i = pl.program_id(1)

        @pl.when(i == 0)
        def _():
            dk_ref[...] = jnp.zeros_like(dk_ref)
            dv_ref[...] = jnp.zeros_like(dv_ref)
            if bias:
                dck_ref[...] = jnp.zeros_like(dck_ref)

        dq_sc[...] = jnp.zeros((hb, LANES, BQ), F32)

        def tile(j, masked):
            off = pl.multiple_of(j * BQ, BQ)
            vis = _vis_t(chunk_mask) if masked else None
            qs = [q_ref[:, _hs(hh)] for hh in range(hb)]
            dobs = [do_ref[:, _hs(hh)].astype(BF16) for hh in range(hb)]
            ss = [_bdot(k_ref[pl.ds(off, BQ), _hs(hh)], qs[hh], NT) for hh in range(hb)]
            dps = [_bdot(v_ref[pl.ds(off, BQ), _hs(hh)], dobs[hh], NT) for hh in range(hb)]
            pbs, dsbs = [], []
            for hh in range(hb):
                s = ss[hh]
                if scale != 1.0:
                    s = s * scale
                p = jnp.exp(s - lse_ref[hh, 0])
                if masked:
                    p = jnp.where(vis, p, 0.0)
                ds = p * (dps[hh] - dl_ref[hh, 0])
                if bias:
                    dck_ref[pl.ds(off, BQ), _hs(hh)] -= ds[:, 0:LANES] + ds[:, LANES:BQ]
                    dcq_sc[hh] += jnp.sum(ds, axis=0, keepdims=True)
                if scale != 1.0:
                    ds = ds * scale
                pbs.append(p.astype(BF16))
                dsbs.append(ds.astype(BF16))
            for hh in range(hb):
                sl = _hs(hh)
                dv_ref[pl.ds(off, BQ), sl] += _bdot(pbs[hh], dobs[hh])
                dk_ref[pl.ds(off, BQ), sl] += _bdot(dsbs[hh], qs[hh])
                dq_sc[hh] += _bdot(kt_ref[j, sl, :], dsbs[hh])

        def loop(j, carry):
            tile(j, False)
            return carry

        lax.fori_loop(0, i, loop, 0)
        tile(i, True)
        for hh in range(hb):
            dq_ref[:, _hs(hh)] = dq_sc[hh].T
            if bias:
                dcq_ref[hh, 0] = dcq_sc[hh]

    in_specs = [_q_spec(hb), _kv_spec(t, 1, hb), _kv_spec(t, 2, hb), _kvt_spec(nq, BQ, 0, hb), _q_spec(hb),
                _qrow_spec(hb), _qrow_spec(hb)]
    out_specs = [_q_spec(hb), _acc_spec(t, hb), _acc_spec(t, hb)]
    out_shape = [jax.ShapeDtypeStruct((t, HP), F32)] * 3
    scratch = [pltpu.VMEM((hb, LANES, BQ), F32)]
    if bias:
        out_specs += [_acc_spec(t, hb), _qrow_spec(hb)]
        out_shape += [jax.ShapeDtypeStruct((t, HP), F32), jax.ShapeDtypeStruct((HEADS, nq, 1, BQ), F32)]
        scratch.append(pltpu.VMEM((hb, 1, BQ), F32))
    return pl.pallas_call(
        body, name=name, grid=(HEADS // hb, nq), in_specs=in_specs, out_specs=out_specs, out_shape=out_shape,
        scratch_shapes=scratch,
        compiler_params=_cp("parallel", "arbitrary"))(qkv, qkv, qkv, kvt, do, lse, delta)


def _ret_diag_decay(lg1, keys_on_rows=False):
    r, c = _tile_iota(BQ, BQ)
    qn, km = (c, r) if keys_on_rows else (r, c)
    dd = jnp.where(km > qn, jnp.exp((2.0 * lg1) * (km - qn).astype(F32)), 1.0)
    return jnp.where((km >> CHUNK_SHIFT) <= (qn >> CHUNK_SHIFT), dd, 0.0)


def _lg_spec(hb):
    return pl.BlockSpec((hb, 1, LANES), lambda g, i: (g, 0, 0))


def _ret_specs(nq, hb, reverse):
    tile = (lambda i: nq - 1 - i) if reverse else (lambda i: i)
    qkv = [pl.BlockSpec((BQ, hb * LANES), lambda g, i, w=w: (tile(i), w * (HEADS // hb) + g)) for w in range(3)]
    kt = pl.BlockSpec((1, hb * LANES, BQ), lambda g, i: (tile(i), g, 0))
    st = pl.BlockSpec((1, hb * LANES, LANES), lambda g, i: (tile(i), g, 0))
    return qkv, kt, st


def _ret_fwd(qkv, kt, lg_heads, name):
    t = qkv.shape[0]
    nq = t // BQ
    hb = HB_FWD

    def body(lg_ref, q_ref, k_ref, v_ref, kt_ref, o_ref, st_ref, s_sc):
        @pl.when(pl.program_id(1) == 0)
        def _():
            s_sc[...] = jnp.zeros_like(s_sc)

        qs = [q_ref[:, _hs(hh)] for hh in range(hb)]
        vs = [v_ref[:, _hs(hh)] for hh in range(hb)]
        aa = [_bdot(qs[hh], k_ref[:, _hs(hh)], NT) for hh in range(hb)]
        kv = [_bdot(kt_ref[0, _hs(hh), :], vs[hh]) for hh in range(hb)]
        for hh in range(hb):
            sl = _hs(hh)
            lg1 = lg_ref[hh][:, 0:1]
            s = s_sc[hh]
            st_ref[0, sl, :] = s
            shi, slo = _split2(s)
            a = (aa[hh] * _ret_diag_decay(lg1)).astype(BF16)
            o_ref[:, sl] = _bdot(a, vs[hh]) + _bdot(qs[hh], shi) + _bdot(qs[hh], slo)
            s_sc[hh] = jnp.exp(lg1 * float(BQ)) * (s + kv[hh])

    qkv_specs, kt_spec, st_spec = _ret_specs(nq, hb, False)
    return pl.pallas_call(
        body, name=name, grid=(HEADS // hb, nq), in_specs=[_lg_spec(hb)] + qkv_specs + [kt_spec],
        out_specs=[_q_spec(hb), st_spec],
        out_shape=[jax.ShapeDtypeStruct((t, HP), F32), jax.ShapeDtypeStruct((nq, HP, LANES), F32)],
        scratch_shapes=[pltpu.VMEM((hb, LANES, LANES), F32)],
        compiler_params=_cp("parallel", "arbitrary"))(lg_heads, qkv, qkv, qkv, kt)


def _ret_bwd(qkv, states, lg_heads, do, name):
    t = qkv.shape[0]
    nq = t // BQ
    hb = HB_FWD

    def body(lg_ref, q_ref, k_ref, v_ref, st_ref, do_ref, dq_ref, dk_ref, dv_ref, g_sc):
        @pl.when(pl.program_id(1) == 0)
        def _():
            g_sc[...] = jnp.zeros_like(g_sc)

        qs = [q_ref[:, _hs(hh)] for hh in range(hb)]
        ks = [k_ref[:, _hs(hh)] for hh in range(hb)]
        vs = [v_ref[:, _hs(hh)] for hh in range(hb)]
        dobs = [do_ref[:, _hs(hh)].astype(BF16) for hh in range(hb)]
        aa = [_bdot(ks[hh], qs[hh], NT) for hh in range(hb)]
        das = [_bdot(vs[hh], dobs[hh], NT) for hh in range(hb)]
        qdo = [_bdot(qs[hh], dobs[hh], TN) for hh in range(hb)]
        for hh in range(hb):
            sl = _hs(hh)
            lg1 = lg_ref[hh][:, 0:1]
            dd = _ret_diag_decay(lg1, keys_on_rows=True)
            at = (aa[hh] * dd).astype(BF16)
            dat = (das[hh] * dd).astype(BF16)
            h = jnp.exp(lg1 * float(BQ)) * g_sc[hh]
            hhi, hlo = _split2(h)
            shi, slo = _split2(st_ref[0, sl, :])
            dv_ref[:, sl] = _bdot(at, dobs[hh]) + _bdot(ks[hh], hhi) + _bdot(ks[hh], hlo)
            dk_ref[:, sl] = _bdot(dat, qs[hh]) + _bdot(vs[hh], hhi, NT) + _bdot(vs[hh], hlo, NT)
            dq_ref[:, sl] = _bdot(dat, ks[hh], TN) + _bdot(dobs[hh], shi, NT) + _bdot(dobs[hh], slo, NT)
            g_sc[hh] = qdo[hh] + h

    qkv_specs, _, st_spec = _ret_specs(nq, hb, True)
    tile_spec = pl.BlockSpec((BQ, hb * LANES), lambda g, i: (nq - 1 - i, g))
    return pl.pallas_call(
        body, name=name, grid=(HEADS // hb, nq), in_specs=[_lg_spec(hb)] + qkv_specs + [st_spec, tile_spec],
        out_specs=[tile_spec] * 3, out_shape=[jax.ShapeDtypeStruct((t, HP), F32)] * 3,
        scratch_shapes=[pltpu.VMEM((hb, LANES, LANES), F32)],
        compiler_params=_cp("parallel", "arbitrary"))(lg_heads, qkv, qkv, qkv, states, do)


def _sb_tile_logs(q, kb, vis):
    z = _bdot(kb, q, NT)
    ls = -(jnp.maximum(z, 0.0) + jnp.log(1.0 + jnp.exp(-jnp.abs(z))))
    if vis is not None:
        ls = jnp.where(vis, ls, 0.0)
    return z, ls


def _sb_vis(i, j):
    r, c = _tile_iota(TKS, BQS)
    return (j * TKS + r) < (i * BQS + c)


def _sb_later(ls, after):
    hi, lo = _split2(ls)
    return _bdot(after, hi) + _bdot(after, lo)


def _sb_fwd(qkv, kvt, name):
    t = qkv.shape[0]
    nq = t // BQS
    per = BQS // TKS
    hb = HB_SB_FWD

    def body(q_ref, k_ref, vt_ref, o_ref, tot_ref, acc_sc, r_sc):
        i = pl.program_id(1)
        acc_sc[...] = jnp.zeros((hb, LANES, BQS), F32)
        r_sc[...] = jnp.zeros((hb, 1, BQS), F32)
        mr, mc = _tile_iota(TKS, TKS)
        after = jnp.where(mc > mr, 1.0, 0.0).astype(BF16)

        def tile(j, masked):
            off = pl.multiple_of(j * TKS, TKS)
            vis = _sb_vis(i, j) if masked else None
            zl = [_sb_tile_logs(q_ref[:, _hs(hh)], k_ref[pl.ds(off, TKS), _hs(hh)], vis) for hh in range(hb)]
            laters = [_sb_later(zl[hh][1], after) for hh in range(hb)]
            ws = []
            for hh in range(hb):
                z, ls = zl[hh]
                w = jnp.exp(z + ls + laters[hh] + r_sc[hh])
                if masked:
                    w = jnp.where(vis, w, 0.0)
                ws.append(w.astype(BF16))
                r_sc[hh] += jnp.sum(ls, axis=0, keepdims=True)
            for hh in range(hb):
                acc_sc[hh] += _bdot(vt_ref[j, _hs(hh), :], ws[hh])

        for d in range(per):
            tile(per * i + (per - 1 - d), True)

        def loop(jj, carry):
            tile(per * i - 1 - jj, False)
            return carry

        lax.fori_loop(0, per * i, loop, 0)
        for hh in range(hb):
            o_ref[:, _hs(hh)] = acc_sc[hh].T
            tot_ref[hh, 0] = r_sc[hh]

    return pl.pallas_call(
        body, name=name, grid=(HEADS // hb, nq),
        in_specs=[_q_spec(hb, BQS), _kv_spec(t, 1, hb), _kvt_spec(t // TKS, TKS, 1, hb)],
        out_specs=[_q_spec(hb, BQS), _qrow_spec(hb, BQS)],
        out_shape=[jax.ShapeDtypeStruct((t, HP), F32), jax.ShapeDtypeStruct((HEADS, nq, 1, BQS), F32)],
        scratch_shapes=[pltpu.VMEM((hb, LANES, BQS), F32), pltpu.VMEM((hb, 1, BQS), F32)],
        compiler_params=_cp("parallel", "arbitrary"))(qkv, qkv, kvt)


def _sb_bwd(qkv, kvt, do, tot, name):
    t = qkv.shape[0]
    nq = t // BQS
    per = BQS // TKS
    hb = HB_BWD

    def body(q_ref, k_ref, v_ref, kt_ref, do_ref, tot_ref, dq_ref, dk_ref, dv_ref, dq_sc, p_sc, g_sc):
        i = pl.program_id(1)

        @pl.when(i == 0)
        def _():
            dk_ref[...] = jnp.zeros_like(dk_ref)
            dv_ref[...] = jnp.zeros_like(dv_ref)

        dq_sc[...] = jnp.zeros((hb, LANES, BQS), F32)
        p_sc[...] = jnp.zeros((hb, 1, BQS), F32)
        g_sc[...] = jnp.zeros((hb, 1, BQS), F32)
        mr, mc = _tile_iota(TKS, TKS)
        after = jnp.where(mc > mr, 1.0, 0.0).astype(BF16)
        before = jnp.where(mc < mr, 1.0, 0.0).astype(BF16)

        def tile(j, masked):
            off = pl.multiple_of(j * TKS, TKS)
            vis = _sb_vis(i, j) if masked else None
            qs = [q_ref[:, _hs(hh)] for hh in range(hb)]
            dobs = [do_ref[:, _hs(hh)].astype(BF16) for hh in range(hb)]
            zl = [_sb_tile_logs(qs[hh], k_ref[pl.ds(off, TKS), _hs(hh)], vis) for hh in range(hb)]
            dws = [_bdot(v_ref[pl.ds(off, TKS), _hs(hh)], dobs[hh], NT) for hh in range(hb)]
            laters = [_sb_later(zl[hh][1], after) for hh in range(hb)]
            ws, gs = [], []
            for hh in range(hb):
                z, ls = zl[hh]
                own = jnp.sum(ls, axis=0, keepdims=True)
                rest = tot_ref[hh, 0] - p_sc[hh] - own
                w = jnp.exp(z + ls + laters[hh] + rest)
                if masked:
                    w = jnp.where(vis, w, 0.0)
                p_sc[hh] += own
                ws.append(w.astype(BF16))
                gs.append(dws[hh] * w)
            gins = []
            for hh in range(hb):
                ghi, glo = _split2(gs[hh])
                gins.append(_bdot(before, ghi) + _bdot(before, glo))
            dzbs = []
            for hh in range(hb):
                g = gs[hh]
                stay = jnp.exp(zl[hh][1])
                dz = g * stay - (1.0 - stay) * (gins[hh] + g_sc[hh])
                if masked:
                    dz = jnp.where(vis, dz, 0.0)
                g_sc[hh] += jnp.sum(g, axis=0, keepdims=True)
                dzbs.append(dz.astype(BF16))
            for hh in range(hb):
                sl = _hs(hh)
                dv_ref[pl.ds(off, TKS), sl] += _bdot(ws[hh], dobs[hh])
                dk_ref[pl.ds(off, TKS), sl] += _bdot(dzbs[hh], qs[hh])
                dq_sc[hh] += _bdot(kt_ref[j, sl, :], dzbs[hh])

        def loop(j, carry):
            tile(j, False)
            return carry

        lax.fori_loop(0, per * i, loop, 0)
        for d in range(per):
            tile(per * i + d, True)
        for hh in range(hb):
            dq_ref[:, _hs(hh)] = dq_sc[hh].T

    return pl.pallas_call(
        body, name=name, grid=(HEADS // hb, nq),
        in_specs=[_q_spec(hb, BQS), _kv_spec(t, 1, hb), _kv_spec(t, 2, hb), _kvt_spec(t // TKS, TKS, 0, hb),
                  _q_spec(hb, BQS), _qrow_spec(hb, BQS)],
        out_specs=[_q_spec(hb, BQS), _acc_spec(t, hb), _acc_spec(t, hb)],
        out_shape=[jax.ShapeDtypeStruct((t, HP), F32)] * 3,
        scratch_shapes=[pltpu.VMEM((hb, LANES, BQS), F32), pltpu.VMEM((hb, 1, BQS), F32),
                        pltpu.VMEM((hb, 1, BQS), F32)],
        compiler_params=_cp("parallel", "arbitrary"))(qkv, qkv, qkv, kvt, do, tot)


def _sigmoid(v):
    return 1.0 / (1.0 + jnp.exp(-v))


def _post_fwd(oa, ob, oc, od, proj, g_pad, name):
    t = oa.shape[0]

    def body(oa_ref, ob_ref, oc_ref, od_ref, rg_ref, g_ref, mx_ref):
        g = g_ref[...]

        def group(o, gg):
            r = lax.rsqrt(jnp.sum(o * o, axis=-1, keepdims=True) * (1.0 / GROUP) + EPS)
            return (o * r * gg).astype(BF16)

        mx_ref[:, 0:HP] = group(oa_ref[...], g[:, 0:HP])
        mx_ref[:, HP:2 * HP] = group(ob_ref[...], g[:, HP:2 * HP])
        mx_ref[:, 3 * HP:4 * HP] = group(od_ref[...], g[:, 3 * HP:4 * HP])
        real = lax.broadcasted_iota(jnp.int32, (TM, LANES), 1) < HEAD_DIM
        for hb in range(HEADS):
            sl = slice(hb * LANES, (hb + 1) * LANES)
            o = oc_ref[:, sl]
            mu = jnp.sum(o, axis=-1, keepdims=True) * (1.0 / HEAD_DIM)
            dlt = jnp.where(real, o - mu, 0.0)
            var = jnp.sum(dlt * dlt, axis=-1, keepdims=True) * (1.0 / HEAD_DIM)
            yn = dlt * lax.rsqrt(var + EPS) * g[:, 2 * HP + hb * LANES:2 * HP + (hb + 1) * LANES]
            x = rg_ref[:, sl]
            mx_ref[:, 2 * HP + hb * LANES:2 * HP + (hb + 1) * LANES] = (yn * (x * _sigmoid(x))).astype(BF16)

    rg_spec = pl.BlockSpec((TM, HP), lambda i: (i, OFF_RG // HP))
    return pl.pallas_call(
        body, name=name, grid=(t // TM,),
        in_specs=[_row_spec(HP)] * 4 + [rg_spec, _vec_spec(4 * HP)], out_specs=_row_spec(4 * HP),
        out_shape=jax.ShapeDtypeStruct((t, 4 * HP), BF16), compiler_params=_cp("parallel"))(oa, ob, oc, od, proj, g_pad)


def _post_bwd(dmx, oa, ob, oc, od, proj, g_pad, name):
    t = oa.shape[0]

    def body(dm_ref, oa_ref, ob_ref, oc_ref, od_ref, rg_ref, g_ref,
             doa_ref, dob_ref, doc_ref, dod_ref, dla_ref, dlb_ref, drg_ref, dg_ref):
        @pl.when(pl.program_id(0) == 0)
        def _():
            dg_ref[...] = jnp.zeros_like(dg_ref)

        g = g_ref[...]

        def group_bwd(dm, o, gg):
            r = lax.rsqrt(jnp.sum(o * o, axis=-1, keepdims=True) * (1.0 / GROUP) + EPS)
            oh = o * r
            dgp = jnp.sum(dm * oh, axis=0, keepdims=True)
            dyh = dm * gg
            do = r * (dyh - oh * (jnp.sum(dyh * oh, axis=-1, keepdims=True) * (1.0 / GROUP)))
            return do, dgp

        def delta_bc(do, o):
            prod = do * o
            lane = lax.broadcasted_iota(jnp.int32, (TM, LANES), 1)
            out = jnp.zeros((TM, LANES), F32)
            for hb in range(HEADS):
                out = jnp.where(lane == hb, jnp.sum(prod[:, hb * LANES:(hb + 1) * LANES], axis=-1, keepdims=True), out)
            return out

        oa = oa_ref[...]
        do_a, dga = group_bwd(dm_ref[:, 0:HP], oa, g[:, 0:HP])
        doa_ref[...] = do_a
        dla_ref[...] = delta_bc(do_a, oa)
        dg_ref[:, 0:HP] += dga
        ob = ob_ref[...]
        do_b, dgb = group_bwd(dm_ref[:, HP:2 * HP], ob, g[:, HP:2 * HP])
        dob_ref[...] = do_b
        dlb_ref[...] = delta_bc(do_b, ob)
        dg_ref[:, HP:2 * HP] += dgb
        do_d, dgd = group_bwd(dm_ref[:, 3 * HP:4 * HP], od_ref[...], g[:, 3 * HP:4 * HP])
        dod_ref[...] = do_d
        dg_ref[:, 3 * HP:4 * HP] += dgd
        real = lax.broadcasted_iota(jnp.int32, (TM, LANES), 1) < HEAD_DIM
        for hb in range(HEADS):
            sl = slice(hb * LANES, (hb + 1) * LANES)
            gsl = slice(2 * HP + hb * LANES, 2 * HP + (hb + 1) * LANES)
            o = oc_ref[:, sl]
            mu = jnp.sum(o, axis=-1, keepdims=True) * (1.0 / HEAD_DIM)
            dlt = jnp.where(real, o - mu, 0.0)
            var = jnp.sum(dlt * dlt, axis=-1, keepdims=True) * (1.0 / HEAD_DIM)
            rstd = lax.rsqrt(var + EPS)
            dhat = dlt * rstd
            gc = g[:, gsl]
            x = rg_ref[:, sl]
            sg = _sigmoid(x)
            dm = dm_ref[:, gsl]
            drg_ref[:, sl] = dm * (dhat * gc) * (sg * (1.0 + x * (1.0 - sg)))
            dyn = dm * (x * sg)
            dg_ref[:, gsl] += jnp.sum(dyn * dhat, axis=0, keepdims=True)
            ddh = dyn * gc
            m1 = jnp.sum(ddh, axis=-1, keepdims=True) * (1.0 / HEAD_DIM)
            m2 = jnp.sum(ddh * dhat, axis=-1, keepdims=True) * (1.0 / HEAD_DIM)
            doc_ref[:, sl] = jnp.where(real, rstd * (ddh - m1 - dhat * m2), 0.0)

    rg_spec = pl.BlockSpec((TM, HP), lambda i: (i, OFF_RG // HP))
    hp = _row_spec(HP)
    return pl.pallas_call(
        body, name=name, grid=(t // TM,),
        in_specs=[_row_spec(4 * HP), hp, hp, hp, hp, rg_spec, _vec_spec(4 * HP)],
        out_specs=[hp] * 4 + [_row_spec(LANES)] * 2 + [hp, _vec_spec(4 * HP)],
        out_shape=[jax.ShapeDtypeStruct((t, HP), F32)] * 4 + [jax.ShapeDtypeStruct((t, LANES), F32)] * 2
        + [jax.ShapeDtypeStruct((t, HP), F32), jax.ShapeDtypeStruct((1, 4 * HP), F32)],
        compiler_params=_cp("arbitrary"))(dmx, oa, ob, oc, od, proj, g_pad)


def _mesh_pos():
    return lax.axis_index("x"), lax.axis_index("y"), lax.axis_index("c")


def _peer(pos, k):
    x, y, c = pos
    px = 1 - x if (k >> 2) & 1 else x
    py = 1 - y if (k >> 1) & 1 else y
    pc = 1 - c if k & 1 else c
    return (px, py, pc), 4 * px + 2 * py + pc


def _exchange(arrs, gather, name):
    n = len(arrs)

    def body(*refs):
        ins, outs = refs[:n], refs[n:2 * n]
        send_sems, recv_sems, loc_sems = refs[2 * n:]
        pos = _mesh_pos()
        me = 4 * pos[0] + 2 * pos[1] + pos[2]
        local = []
        for a in range(n):
            src = ins[a] if gather else ins[a].at[me]
            cp = pltpu.make_async_copy(src, outs[a].at[me], loc_sems.at[a])
            cp.start()
            local.append(cp)
        sends, recvs = [], []
        for k in range(1, N_DEV):
            peer, pid = _peer(pos, k)
            for a in range(n):
                s = a * (N_DEV - 1) + k - 1
                src = ins[a] if gather else ins[a].at[pid]
                cp = pltpu.make_async_remote_copy(
                    src_ref=src, dst_ref=outs[a].at[me], send_sem=send_sems.at[s], recv_sem=recv_sems.at[s],
                    device_id=peer, device_id_type=pl.DeviceIdType.MESH)
                cp.start()
                sends.append(cp)
                recvs.append(pltpu.make_async_remote_copy(
                    src_ref=src, dst_ref=outs[a].at[pid], send_sem=send_sems.at[s], recv_sem=recv_sems.at[s],
                    device_id=peer, device_id_type=pl.DeviceIdType.MESH))
        for cp in recvs:
            cp.wait_recv()
        for cp in sends:
            cp.wait_send()
        for cp in local:
            cp.wait()

    any_spec = pl.BlockSpec(memory_space=pl.ANY)
    out_shape = [jax.ShapeDtypeStruct((N_DEV,) + tuple(a.shape) if gather else tuple(a.shape), a.dtype) for a in arrs]
    return pl.pallas_call(
        body, name=name, in_specs=[any_spec] * n, out_specs=[any_spec] * n, out_shape=out_shape,
        scratch_shapes=[pltpu.SemaphoreType.DMA((n * (N_DEV - 1),)), pltpu.SemaphoreType.DMA((n * (N_DEV - 1),)),
                        pltpu.SemaphoreType.DMA((n,))],
        compiler_params=pltpu.CompilerParams(has_side_effects=True))(*arrs)


def _device_index():
    x, y, c = _mesh_pos()
    return 4 * x + 2 * y + c


def _landing(srcs, gather):
    me = _device_index()
    lands = []
    for a in srcs:
        own = a[None] if gather else lax.dynamic_slice_in_dim(a, me, 1, axis=0)
        shape = (N_DEV,) + tuple(a.shape) if gather else tuple(a.shape)
        lands.append(lax.dynamic_update_slice_in_dim(jnp.zeros(shape, a.dtype), own, me, axis=0))
    return lands


def _exchange_copies(ins, lands, send_sems, recv_sems, gather):
    pos = _mesh_pos()
    me = 4 * pos[0] + 2 * pos[1] + pos[2]
    sends, recvs = [], []
    for k in range(1, N_DEV):
        peer, pid = _peer(pos, k)
        for a in range(len(ins)):
            s = a * (N_DEV - 1) + k - 1
            src = ins[a] if gather else ins[a].at[pid]
            sends.append(pltpu.make_async_remote_copy(
                src_ref=src, dst_ref=lands[a].at[me], send_sem=send_sems.at[s], recv_sem=recv_sems.at[s],
                device_id=peer, device_id_type=pl.DeviceIdType.MESH))
            recvs.append(pltpu.make_async_remote_copy(
                src_ref=src, dst_ref=lands[a].at[pid], send_sem=send_sems.at[s], recv_sem=recv_sems.at[s],
                device_id=peer, device_id_type=pl.DeviceIdType.MESH))
    return sends, recvs


def _exchange_start(srcs, gather, name):
    n = len(srcs)
    lands = _landing(srcs, gather)
    nsem = n * (N_DEV - 1)

    def body(*refs):
        ins, lnd = refs[:n], refs[n:2 * n]
        send_sems, recv_sems = refs[2 * n], refs[2 * n + 1]
        token = refs[-1]
        sends, _ = _exchange_copies(ins, lnd, send_sems, recv_sems, gather)
        for cp in sends:
            cp.start()
        token[...] = jnp.zeros_like(token)

    hbm = pl.BlockSpec(memory_space=pltpu.HBM)
    sem = pl.BlockSpec(memory_space=pltpu.SEMAPHORE)
    bufs = list(srcs) + lands
    out_shape = ([pltpu.SemaphoreType.DMA((nsem,)), pltpu.SemaphoreType.DMA((nsem,))]
                 + [pltpu.HBM(b.shape, b.dtype) for b in bufs] + [jax.ShapeDtypeStruct((8, LANES), F32)])
    outs = pl.pallas_call(
        body, name=name, in_specs=[hbm] * (2 * n),
        out_specs=[sem, sem] + [hbm] * (2 * n) + [pl.BlockSpec(memory_space=pltpu.VMEM)], out_shape=out_shape,
        input_output_aliases={i: 2 + i for i in range(2 * n)},
        compiler_params=pltpu.CompilerParams(has_side_effects=pltpu.SideEffectType.DATAFLOW_SIDE_EFFECTING),
    )(*[pltpu.with_memory_space_constraint(b, pltpu.HBM) for b in bufs])
    return (outs[0], outs[1], outs[2:2 + n], outs[2 + n:2 + 2 * n]), outs[-1]


def _exchange_wait(state, after, gather, name):
    send_sems, recv_sems, srcs, lands = state
    n = len(srcs)

    def body(*refs):
        ins, lnd = refs[:n], refs[n:2 * n]
        s_sems, r_sems = refs[2 * n], refs[2 * n + 1]
        sends, recvs = _exchange_copies(ins, lnd, s_sems, r_sems, gather)
        for cp in sends:
            cp.wait_send()
        for cp in recvs:
            cp.wait_recv()

    hbm = pl.BlockSpec(memory_space=pltpu.HBM)
    sem = pl.BlockSpec(memory_space=pltpu.SEMAPHORE)
    bufs = list(srcs) + list(lands)
    outs = pl.pallas_call(
        body, name=name, in_specs=[hbm] * (2 * n) + [sem, sem, pl.BlockSpec(memory_space=pl.ANY)],
        out_specs=[hbm] * (2 * n), out_shape=[pltpu.HBM(b.shape, b.dtype) for b in bufs],
        input_output_aliases={i: i for i in range(2 * n)},
        compiler_params=pltpu.CompilerParams(has_side_effects=pltpu.SideEffectType.DATAFLOW_SIDE_EFFECTING),
    )(*bufs, send_sems, recv_sems, after)
    return outs[n:]


def _adam_vals(w, g, m, v):
    m = ADAM_B1 * m + (1.0 - ADAM_B1) * g
    v = ADAM_B2 * v + (1.0 - ADAM_B2) * (g * g)
    m_hat = m / ADAM_C1
    v_hat = v / ADAM_C2
    delta = -ADAM_LR * (m_hat / (jnp.sqrt(v_hat) + ADAM_EPS) + ADAM_WD * w)
    return delta, m, v


def _small_allreduce_adam(part, w, m, v, name):
    rows = part.shape[0]

    def body(p_ref, w_ref, m_ref, v_ref, g_ref, d_ref, nm_ref, nv_ref, gath, send_sems, recv_sems):
        pos = _mesh_pos()
        me = 4 * pos[0] + 2 * pos[1] + pos[2]
        gath[me] = p_ref[...]
        sends, recvs = [], []
        for k in range(1, N_DEV):
            peer, pid = _peer(pos, k)
            cp = pltpu.make_async_remote_copy(
                src_ref=p_ref, dst_ref=gath.at[me], send_sem=send_sems.at[k - 1], recv_sem=recv_sems.at[k - 1],
                device_id=peer, device_id_type=pl.DeviceIdType.MESH)
            cp.start()
            sends.append(cp)
            recvs.append(pltpu.make_async_remote_copy(
                src_ref=p_ref, dst_ref=gath.at[pid], send_sem=send_sems.at[k - 1], recv_sem=recv_sems.at[k - 1],
                device_id=peer, device_id_type=pl.DeviceIdType.MESH))
        for cp in recvs:
            cp.wait_recv()
        for cp in sends:
            cp.wait_send()
        g = gath[0]
        for p in range(1, N_DEV):
            g = g + gath[p]
        g_ref[...] = g
        d, nm, nv = _adam_vals(w_ref[...], g, m_ref[...], v_ref[...])
        d_ref[...] = d
        nm_ref[...] = nm
        nv_ref[...] = nv

    vm = pl.BlockSpec(memory_space=pltpu.VMEM)
    sds = jax.ShapeDtypeStruct((rows, LANES), F32)
    return pl.pallas_call(
        body, name=name, in_specs=[vm] * 4, out_specs=[vm] * 4, out_shape=[sds] * 4,
        scratch_shapes=[pltpu.VMEM((N_DEV, rows, LANES), F32), pltpu.SemaphoreType.DMA((N_DEV - 1,)),
                        pltpu.SemaphoreType.DMA((N_DEV - 1,))],
        compiler_params=pltpu.CompilerParams(has_side_effects=True))(part, w, m, v)


def _reduce_adam(recv, w, m, v, name):
    shape = w.shape
    c = shape[-1]
    r = int(np.prod(shape[:-1]))
    recv2, w2, m2, v2 = recv.reshape(N_DEV, r, c), w.reshape(r, c), m.reshape(r, c), v.reshape(r, c)
    tr = r
    while tr * c * 4 > (1 << 20) and tr % 16 == 0:
        tr //= 2

    def body(r_ref, w_ref, m_ref, v_ref, g_ref, d_ref, nm_ref, nv_ref):
        g = r_ref[0].astype(F32)
        for p in range(1, N_DEV):
            g = g + r_ref[p].astype(F32)
        g_ref[...] = g
        d, nm, nv = _adam_vals(w_ref[...], g, m_ref[...], v_ref[...])
        d_ref[...] = d
        nm_ref[...] = nm
        nv_ref[...] = nv

    spec = pl.BlockSpec((tr, c), lambda i: (i, 0))
    sds = jax.ShapeDtypeStruct((r, c), F32)
    outs = pl.pallas_call(
        body, name=name, grid=(r // tr,),
        in_specs=[pl.BlockSpec((N_DEV, tr, c), lambda i: (0, i, 0)), spec, spec, spec],
        out_specs=[spec] * 4, out_shape=[sds] * 4, compiler_params=_cp("parallel"))(recv2, w2, m2, v2)
    return [o.reshape(shape) for o in outs]


def _pad_heads(w, real=HEAD_DIM):
    lead = w.shape[:-1]
    w = w.reshape(lead + (HEADS, real))
    w = jnp.pad(w, [(0, 0)] * len(lead) + [(0, 0), (0, LANES - real)])
    return w.reshape(lead + (HP,))


def _unpad_heads(w, real=HEAD_DIM):
    lead = w.shape[:-1]
    return w.reshape(lead + (HEADS, LANES))[..., :real].reshape(lead + (HEADS * real,))


_IN_SEGS = (("fq", 0, 256), ("fk", 256, 512), ("fv", 512, 768), ("ff", 768, 772), ("cq", 772, 1028),
            ("ckv", 1028, 1156), ("kr", 1156, 1188), ("rq", 1188, 1444), ("rk", 1444, 1700), ("rv", 1700, 1956),
            ("rg", 1956, 2212), ("sq", 2212, 2468), ("sk", 2468, 2724), ("sv", 2724, 2980))


def _pad_w_in(w):
    s = {n: w[:, a:b] for n, a, b in _IN_SEGS}
    rows = w.shape[0]
    z = lambda n: jnp.zeros((rows, n), w.dtype)
    parts = [_pad_heads(s[n]) for n in ("fq", "fk", "fv", "rq", "rk", "rv", "sq", "sk", "sv", "rg")]
    parts += [s["cq"], s["ckv"], z(HEAD_DIM), s["kr"], z(LANES - HEAD_DIM - ROPE_MLA), s["ff"], z(LANES - HEADS),
              z(NP_IN - OFF_FF - LANES)]
    return jnp.concatenate(parts, axis=1)


def _unpad_w_in(wp):
    def heads(off):
        return _unpad_heads(wp[:, off:off + HP])

    parts = [heads(OFF_FOX), heads(OFF_FOX + HP), heads(OFF_FOX + 2 * HP), wp[:, OFF_FF:OFF_FF + HEADS],
             wp[:, OFF_CQ:OFF_CQ + Q_RANK], wp[:, OFF_CKV:OFF_CKV + KV_RANK],
             wp[:, OFF_KR + HEAD_DIM:OFF_KR + HEAD_DIM + ROPE_MLA],
             heads(OFF_RET), heads(OFF_RET + HP), heads(OFF_RET + 2 * HP), heads(OFF_RG),
             heads(OFF_SB), heads(OFF_SB + HP), heads(OFF_SB + 2 * HP)]
    return jnp.concatenate(parts, axis=1)


def _pad_w_kv(w):
    w4 = w.reshape(KV_RANK, HEADS, 2 * HEAD_DIM)
    k = w4[:, :, :HEAD_DIM].reshape(KV_RANK, GROUP)
    v = w4[:, :, HEAD_DIM:].reshape(KV_RANK, GROUP)
    return jnp.concatenate([_pad_heads(k), _pad_heads(v)], axis=1)


def _unpad_w_kv(wp):
    k = _unpad_heads(wp[:, :HP]).reshape(KV_RANK, HEADS, HEAD_DIM)
    v = _unpad_heads(wp[:, HP:]).reshape(KV_RANK, HEADS, HEAD_DIM)
    return jnp.concatenate([k, v], axis=-1).reshape(KV_RANK, HEADS * 2 * HEAD_DIM)


def _pad_rows_out(w):
    w = w.reshape(4 * HEADS, HEAD_DIM, D_MODEL)
    w = jnp.pad(w, ((0, 0), (0, LANES - HEAD_DIM), (0, 0)))
    return w.reshape(4 * HP, D_MODEL)


def _unpad_rows_out(wp):
    return wp.reshape(4 * HEADS, LANES, D_MODEL)[:, :HEAD_DIM, :].reshape(D_MODEL, D_MODEL)


def _pad_gain_out(g):
    g = jnp.pad(g.reshape(4 * HEADS, HEAD_DIM), ((0, 0), (0, LANES - HEAD_DIM)))
    return g.reshape(1, 4 * HP)


def _unpad_gain_out(gp):
    return gp.reshape(4 * HEADS, LANES)[:, :HEAD_DIM].reshape(D_MODEL)


_SMALL = (("g_mix_pre", 1024), ("g_mix_post", 1024), ("g_ffn_pre", 1024), ("g_ffn_post", 1024), ("g_mix_out", 1024),
          ("g_q_lora", 256), ("g_kv_lora", 128), ("b_forget", 4))


def _pack_small(vals):
    parts = []
    for name, n in _SMALL:
        a = vals[name].astype(F32)
        if n < LANES:
            a = jnp.pad(a, ((0, 0), (0, LANES - n)))
        parts.append(a)
    return jnp.concatenate(parts, axis=1).reshape(DEPTH * SMALL_ROWS, LANES)


def _unpack_small(packed):
    flat = packed.reshape(DEPTH, SMALL_ROWS * LANES)
    out, off = {}, 0
    for name, n in _SMALL:
        out[name] = flat[:, off:off + n]
        off += max(n, LANES)
    return out


def kernel(x, positions, g_mix_pre, w_in, b_forget, g_q_lora, w_q_up, g_kv_lora, w_kv_up, g_mix_out, w_out, g_mix_post, g_ffn_pre, w_ffn_up, w_ffn_down, g_ffn_post, loss_target, m_g_mix_pre, m_w_in, m_b_forget, m_g_q_lora, m_w_q_up, m_g_kv_lora, m_w_kv_up, m_g_mix_out, m_w_out, m_g_mix_post, m_g_ffn_pre, m_w_ffn_up, m_w_ffn_down, m_g_ffn_post, v_g_mix_pre, v_w_in, v_b_forget, v_g_q_lora, v_w_q_up, v_g_kv_lora, v_w_kv_up, v_g_mix_out, v_w_out, v_g_mix_post, v_g_ffn_pre, v_w_ffn_up, v_w_ffn_down, v_g_ffn_post):
    t = x.shape[1]
    nq = t // BQ
    x0 = x[0]
    tgt = loss_target[0]
    pos = positions[0].astype(F32).reshape(t, 1)

    half_r, half_m = HEAD_DIM // 2, ROPE_MLA // 2
    invf_r = ROPE_BASE ** (-jnp.arange(half_r, dtype=F32) / half_r)
    invf_m = ROPE_BASE ** (-jnp.arange(half_m, dtype=F32) / half_m)
    invf = jnp.concatenate([invf_r, invf_r, invf_m, invf_m,
                            jnp.zeros((LANES - HEAD_DIM - ROPE_MLA,), F32)]).reshape(1, LANES)
    log_gamma = jnp.log1p(-jnp.power(2.0, -5.0 - jnp.arange(HEADS, dtype=F32)))
    lg_lanes = jnp.repeat(log_gamma, LANES).reshape(1, HP)
    lg_heads = jnp.broadcast_to(log_gamma[:, None, None], (HEADS, 1, LANES))

    big = [w_in, w_q_up, w_kv_up, w_out, w_ffn_up, w_ffn_down]
    bf = lambda w: w.astype(BF16)
    first = _exchange([bf(w_in[0]), bf(w_q_up), bf(w_kv_up)], True, "weights_gather_first")
    rest_state, rest_token = _exchange_start([bf(w_in[1]), bf(w_out), bf(w_ffn_up), bf(w_ffn_down)], True,
                                             "weights_gather_start")
    wq_g = first[1].transpose(1, 2, 0, 3).reshape(DEPTH, Q_RANK, 384)
    wkv_g = first[2].transpose(1, 2, 0, 3).reshape(DEPTH, KV_RANK, 512)

    row = lambda g: g.reshape(1, -1)
    layers = []
    for l in range(DEPTH):
        layers.append(dict(
            wq=_pad_heads(wq_g[l], 96), wkv=_pad_w_kv(wkv_g[l]),
            g_pre=row(g_mix_pre[l]), g_post=row(g_mix_post[l]), g_fpre=row(g_ffn_pre[l]), g_fpost=row(g_ffn_post[l]),
            g_out=_pad_gain_out(g_mix_out[l]), g_q=row(g_q_lora[l]), g_kv=row(g_kv_lora[l]),
            b_pad=jnp.pad(b_forget[l], (0, LANES - HEADS)).reshape(1, LANES)))
    layers[0]["win"] = _pad_w_in(first[0].reshape(D_MODEL, D_IN))

    saved = []
    xin = x0
    h = _rms_fwd(xin, layers[0]["g_pre"] + rest_token[0:1, 0:1], "rms_pre_0")
    loss_row = dx = None
    for l, p in enumerate(layers):
        s = dict(x=xin, h=h)
        proj = _mm(h, p["win"], name=f"in_proj_{l}")
        cum = _cumsum(_forget_lsf(proj, p["b_pad"], f"forget_lsf_{l}"), False, f"forget_cumsum_{l}")
        fox, ret, sb, mla, fox_t, sb_t, mla_t, ret_t = _prep_fwd(proj, cum, pos, invf, lg_lanes, p["g_q"],
                                                          p["g_kv"], p["wq"], p["wkv"], f"prep_fwd_{l}")
        oa, lse_a = _softmax_fwd(fox, fox_t, chunk_mask=False, scale=1.0, name=f"fox_fwd_{l}")
        ob, lse_b = _softmax_fwd(mla, mla_t, chunk_mask=True, scale=96.0 ** -0.5, name=f"mla_fwd_{l}")
        oc, ret_st = _ret_fwd(ret, ret_t, lg_heads, f"ret_fwd_{l}")
        od, sb_tot = _sb_fwd(sb, sb_t, f"sb_fwd_{l}")
        if l == 0:
            rest = _exchange_wait(rest_state, od, True, "weights_gather_wait")
            wout_g = rest[1].transpose(1, 0, 2, 3).reshape(DEPTH, D_MODEL, D_MODEL)
            wup_g = rest[2].transpose(1, 2, 0, 3).reshape(DEPTH, D_MODEL, D_FF)
            wdn_g = rest[3].transpose(1, 0, 2, 3).reshape(DEPTH, D_FF, D_MODEL)
            layers[1]["win"] = _pad_w_in(rest[0].reshape(D_MODEL, D_IN))
            for ll in range(DEPTH):
                layers[ll].update(wout=_pad_rows_out(wout_g[ll]), wup=wup_g[ll], wdn=wdn_g[ll])
        mixed = _post_fwd(oa, ob, oc, od, proj, p["g_out"], f"post_fwd_{l}")
        mix = _mm(mixed, p["wout"], name=f"out_proj_{l}", tk=2048)
        x1, h2 = _add_rms_fwd(xin, mix, p["g_post"], p["g_fpre"], f"mix_residual_{l}")
        a = _mm(h2, p["wup"], name=f"ffn_up_{l}")
        y = _mm(a, p["wdn"], name=f"ffn_down_{l}", a_fn=_relu2)
        s.update(proj=proj, fox=fox, ret=ret, sb=sb, mla=mla, fox_t=fox_t, sb_t=sb_t, mla_t=mla_t, ret_st=ret_st, oa=oa, ob=ob, oc=oc,
                 od=od, sb_tot=sb_tot, lse_a=lse_a, lse_b=lse_b, mixed=mixed, mix=mix, x1=x1, h2=h2, a=a, y=y)
        saved.append(s)
        if l + 1 < DEPTH:
            xin, h = _add_rms_fwd(x1, y, p["g_fpost"], layers[l + 1]["g_pre"], f"ffn_residual_{l}")
        else:
            loss_row, dx = _final_loss(x1, y, p["g_fpost"], tgt, "loss")

    small_g = {n: [None] * DEPTH for n, _ in _SMALL}
    big_g = [[None] * DEPTH for _ in range(6)]
    to_send = [
        lambda g: g.reshape(N_DEV, 1, D_MODEL // N_DEV, D_IN),
        lambda g: g.reshape(Q_RANK, N_DEV, 384 // N_DEV).transpose(1, 0, 2)[:, None],
        lambda g: g.reshape(KV_RANK, N_DEV, 512 // N_DEV).transpose(1, 0, 2)[:, None],
        lambda g: g.reshape(N_DEV, 1, D_MODEL // N_DEV, D_MODEL),
        lambda g: g.reshape(D_MODEL, N_DEV, D_FF // N_DEV).transpose(1, 0, 2)[:, None],
        lambda g: g.reshape(N_DEV, 1, D_FF // N_DEV, D_MODEL),
    ]
    send_of = lambda ks, l: [to_send[k](big_g[k][l]).astype(BF16) for k in ks]
    late_state = early_state = None
    order_token = jnp.zeros((1, 1), F32)
    for l in reversed(range(DEPTH)):
        p, s = layers[l], saved[l]
        dy, dg = _norm_bwd(dx, s["y"], p["g_fpost"] + order_token, None, BF16, f"ffn_post_bwd_{l}")
        small_g["g_ffn_post"][l] = dg
        da = _mm(dy, p["wdn"], name=f"ffn_down_dx_{l}", tb=True, out_dtype=BF16, epi=_drelu2, epi_in=s["a"])
        big_g[5][l] = _mm(s["a"], dy, name=f"ffn_down_dw_{l}", ta=True, a_fn=_relu2, tk=2048)
        big_g[4][l] = _mm(s["h2"], da, name=f"ffn_up_dw_{l}", ta=True, tk=2048)
        dh2 = _mm(da, p["wup"], name=f"ffn_up_dx_{l}", tb=True)
        dx1, dg = _norm_bwd(dh2, s["x1"], p["g_fpre"], dx, F32, f"ffn_pre_bwd_{l}")
        small_g["g_ffn_pre"][l] = dg
        dmix, dg = _norm_bwd(dx1, s["mix"], p["g_post"], None, BF16, f"mix_post_bwd_{l}")
        small_g["g_mix_post"][l] = dg
        dmixed = _mm(dmix, p["wout"], name=f"out_proj_dx_{l}", tb=True)
        big_g[3][l] = _unpad_rows_out(_mm(s["mixed"], dmix, name=f"out_proj_dw_{l}", ta=True, tk=2048))
        g_out = p["g_out"]
        if l == 0:
            early_state, early_token = _exchange_start(send_of((3, 4, 5), 0), False, "grads_layer0_early_start")
            g_out = g_out + early_token[0:1, 0:1]
        doa, dob, doc, dod, dla, dlb, drg, dgo = _post_bwd(dmixed, s["oa"], s["ob"], s["oc"], s["od"], s["proj"],
                                                           g_out, f"post_bwd_{l}")
        small_g["g_mix_out"][l] = _unpad_gain_out(dgo).reshape(1, D_MODEL)
        as_rows = lambda a: a[:, :HEADS].T.reshape(HEADS, nq, 1, BQ)
        dfq, dfk, dfv, dcum_k, dcum_q = _softmax_bwd(s["fox"], s["fox_t"], doa, s["lse_a"], as_rows(dla), bias=True,
                                                     chunk_mask=False, scale=1.0, name=f"fox_bwd_{l}")
        dmq, dmk, dmv = _softmax_bwd(s["mla"], s["mla_t"], dob, s["lse_b"], as_rows(dlb), bias=False, chunk_mask=True,
                                     scale=96.0 ** -0.5, name=f"mla_bwd_{l}")
        drq, drk, drv = _ret_bwd(s["ret"], s["ret_st"], lg_heads, doc, f"ret_bwd_{l}")
        dsq, dsk, dsv = _sb_bwd(s["sb"], s["sb_t"], dod, s["sb_tot"], f"sb_bwd_{l}")
        dcum_q = jnp.pad(dcum_q.reshape(HEADS, t).T, ((0, 0), (0, LANES - HEADS)))
        dlsf = _cumsum(dcum_q, True, f"forget_cumsum_bwd_{l}", partials=dcum_k)
        dproj, dwq, dwkv, dgq, dgkv, dbf = _prep_bwd(
            (dfq, dfk, dfv), (drq, drk, drv), (dsq, dsk, dsv), (dmq, dmk, dmv), drg, dlsf, s["proj"], pos, invf,
            lg_lanes, p["b_pad"], p["g_q"], p["g_kv"], p["wq"], p["wkv"], f"prep_bwd_{l}")
        small_g["g_q_lora"][l] = dgq
        small_g["g_kv_lora"][l] = dgkv
        small_g["b_forget"][l] = dbf[:, :HEADS]
        big_g[1][l] = _unpad_heads(dwq, 96)
        big_g[2][l] = _unpad_w_kv(dwkv)
        big_g[0][l] = _unpad_w_in(_mm(s["h"], dproj, name=f"in_proj_dw_{l}", ta=True, tk=2048))
        dh = _mm(dproj, p["win"], name=f"in_proj_dx_{l}", tb=True)
        dx, dg = _norm_bwd(dh, s["x"], p["g_pre"], dx1, F32, f"mix_pre_bwd_{l}")
        small_g["g_mix_pre"][l] = dg
        if l == DEPTH - 1:
            late_state, late_token = _exchange_start(send_of(range(6), l), False, "grads_layer1_start")
            order_token = late_token[0:1, 0:1]
    grad_x = dx.reshape(1, t, D_MODEL)

    last = _exchange(send_of((0, 1, 2), 0), False, "grads_layer0_rest")
    late = _exchange_wait(late_state, dx, False, "grads_layer1_wait")
    early = _exchange_wait(early_state, dx, False, "grads_layer0_early_wait")
    recv = [jnp.concatenate([(last[k] if k < 3 else early[k - 3]), late[k]], axis=1) for k in range(6)]
    ms = [m_w_in, m_w_q_up, m_w_kv_up, m_w_out, m_w_ffn_up, m_w_ffn_down]
    vs = [v_w_in, v_w_q_up, v_w_kv_up, v_w_out, v_w_ffn_up, v_w_ffn_down]
    names = ["w_in", "w_q_up", "w_kv_up", "w_out", "w_ffn_up", "w_ffn_down"]
    res = {}
    for k, n in enumerate(names):
        res[n] = _reduce_adam(recv[k], big[k], ms[k], vs[k], f"adamw_{n}")

    small_w = dict(g_mix_pre=g_mix_pre, g_mix_post=g_mix_post, g_ffn_pre=g_ffn_pre, g_ffn_post=g_ffn_post,
                   g_mix_out=g_mix_out, g_q_lora=g_q_lora, g_kv_lora=g_kv_lora, b_forget=b_forget)
    small_m = dict(g_mix_pre=m_g_mix_pre, g_mix_post=m_g_mix_post, g_ffn_pre=m_g_ffn_pre, g_ffn_post=m_g_ffn_post,
                   g_mix_out=m_g_mix_out, g_q_lora=m_g_q_lora, g_kv_lora=m_g_kv_lora, b_forget=m_b_forget)
    small_v = dict(g_mix_pre=v_g_mix_pre, g_mix_post=v_g_mix_post, g_ffn_pre=v_g_ffn_pre, g_ffn_post=v_g_ffn_post,
                   g_mix_out=v_g_mix_out, g_q_lora=v_g_q_lora, g_kv_lora=v_g_kv_lora, b_forget=v_b_forget)
    n_small = DEPTH * SMALL_ROWS
    extra = lambda a: jnp.concatenate([a, jnp.zeros((8, LANES), F32)], axis=0)
    part = jnp.concatenate([_pack_small({n: jnp.concatenate(small_g[n], axis=0) for n, _ in _SMALL}),
                            jnp.broadcast_to(loss_row, (8, LANES))], axis=0)
    sres = _small_allreduce_adam(part, extra(_pack_small(small_w)), extra(_pack_small(small_m)),
                                 extra(_pack_small(small_v)), "small_allreduce_adamw")
    loss = sres[0][n_small, 0]
    sg, sd, sm, sv = [_unpack_small(a[:n_small]) for a in sres]
    for n, _ in _SMALL:
        res[n] = [sg[n], sd[n], sm[n], sv[n]]

    order = ["g_mix_pre", "w_in", "b_forget", "g_q_lora", "w_q_up", "g_kv_lora", "w_kv_up", "g_mix_out", "w_out",
             "g_mix_post", "g_ffn_pre", "w_ffn_up", "w_ffn_down", "g_ffn_post"]
    outs = [loss, grad_x]
    for idx in range(4):
        outs += [res[n][idx] for n in order]
    return tuple(outs)
```

```python
import functools
import math

import numpy as np
import jax
import jax.numpy as jnp
from jax import lax
from jax.experimental import pallas as pl
from jax.experimental.pallas import tpu as pltpu

F32 = jnp.float32
BF16 = jnp.bfloat16

D_MODEL = 1024
DEPTH = 2
N_DEV = 8
GROUP = 256
HEADS = 4
HEAD_DIM = 64
LANES = 128
HP = HEADS * LANES
QKV = 3 * HP
Q_RANK = 256
KV_RANK = 128
ROPE_MLA = 32
D_FF = 4096
D_IN = 2980
CHUNK_SHIFT = 6
EPS = 1e-6
ROPE_BASE = 10000.0
NEG = -1e30

OFF_FOX, OFF_RET, OFF_SB = 0, QKV, 2 * QKV
OFF_RG = 3 * QKV
OFF_CQ = OFF_RG + HP
OFF_CKV = OFF_CQ + Q_RANK
OFF_KR = OFF_CKV + LANES
OFF_FF = OFF_KR + LANES
NP_IN = 6144

BQ = 256
TKS = 128
TM = 256
VMEM_LIMIT = 48 * 1024 * 1024

ADAM_LR, ADAM_B1, ADAM_B2, ADAM_EPS, ADAM_WD, ADAM_STEP = 0.001, 0.9, 0.999, 1e-08, 0.01, 10
ADAM_C1 = 1.0 - ADAM_B1 ** ADAM_STEP
ADAM_C2 = 1.0 - ADAM_B2 ** ADAM_STEP

SMALL_ROWS = 44

NT = (((1,), (1,)), ((), ()))
TN = (((0,), (0,)), ((), ()))


def _cp(*sem):
    return pltpu.CompilerParams(dimension_semantics=sem if sem else None, vmem_limit_bytes=VMEM_LIMIT)


def _bdot(a, b, dn=None):
    if dn is None:
        return jnp.dot(a, b, preferred_element_type=F32)
    return lax.dot_general(a, b, dn, preferred_element_type=F32)


def _split2(x):
    hi = x.astype(BF16)
    lo = (x - hi.astype(F32)).astype(BF16)
    return hi, lo


def _mm(a, b, *, name, ta=False, tb=False, out_dtype=F32, a_fn=None, epi=None, epi_in=None,
        tm=1024, tn=1024, tk=1024):
    m, k = (a.shape[1], a.shape[0]) if ta else a.shape
    n = b.shape[0] if tb else b.shape[1]
    tm, tn, tk = min(tm, m), min(tn, n), min(tk, k)
    assert m % tm == 0 and n % tn == 0 and k % tk == 0, (name, m, n, k)
    nk = k // tk
    dn = (((0 if ta else 1,), (1 if tb else 0,)), ((), ()))

    def body(*refs):
        if epi is None:
            a_ref, b_ref, o_ref = refs[:3]
            e_ref = None
            rest = refs[3:]
        else:
            a_ref, b_ref, e_ref, o_ref = refs[:4]
            rest = refs[4:]
        av = a_ref[...]
        if a_fn is not None:
            av = a_fn(av)
        part = lax.dot_general(av.astype(BF16), b_ref[...].astype(BF16), dn, preferred_element_type=F32)

        def finish(r):
            if epi is not None:
                r = epi(r, e_ref[...])
            o_ref[...] = r.astype(out_dtype)

        if nk == 1:
            finish(part)
        else:
            acc_ref = rest[0]
            kk = pl.program_id(2)

            @pl.when(kk == 0)
            def _():
                acc_ref[...] = part

            @pl.when(kk > 0)
            def _():
                acc_ref[...] += part

            @pl.when(kk == nk - 1)
            def _():
                finish(acc_ref[...])

    a_spec = pl.BlockSpec((tk, tm), lambda i, j, kk: (kk, i)) if ta else pl.BlockSpec((tm, tk), lambda i, j, kk: (i, kk))
    b_spec = pl.BlockSpec((tn, tk), lambda i, j, kk: (j, kk)) if tb else pl.BlockSpec((tk, tn), lambda i, j, kk: (kk, j))
    o_spec = pl.BlockSpec((tm, tn), lambda i, j, kk: (i, j))
    in_specs = [a_spec, b_spec]
    args = [a, b]
    if epi is not None:
        in_specs.append(o_spec)
        args.append(epi_in)
    return pl.pallas_call(
        body, name=name, grid=(m // tm, n // tn, nk),
        in_specs=in_specs, out_specs=o_spec,
        out_shape=jax.ShapeDtypeStruct((m, n), out_dtype),
        scratch_shapes=[pltpu.VMEM((tm, tn), F32)] if nk > 1 else [],
        compiler_params=_cp("parallel", "parallel", "arbitrary"),
    )(*args)


def _relu2(v):
    r = jnp.maximum(v, 0.0)
    return r * r


def _drelu2(du, av):
    return du * (2.0 * jnp.maximum(av, 0.0))


def _rms(v, g):
    r = lax.rsqrt(jnp.mean(v * v, axis=-1, keepdims=True) + EPS)
    return v * r * g


def _row_spec(w):
    return pl.BlockSpec((TM, w), lambda i: (i, 0))


def _vec_spec(w):
    return pl.BlockSpec((1, w), lambda i: (0, 0))


def _rms_fwd(x, g, name):
    t, d = x.shape

    def body(x_ref, g_ref, h_ref):
        h_ref[...] = _rms(x_ref[...], g_ref[...]).astype(BF16)

    return pl.pallas_call(
        body, name=name, grid=(t // TM,), in_specs=[_row_spec(d), _vec_spec(d)], out_specs=_row_spec(d),
        out_shape=jax.ShapeDtypeStruct((t, d), BF16), compiler_params=_cp("parallel"))(x, g)


def _add_rms_fwd(x, y, g1, g2, name):
    t, d = x.shape

    def body(x_ref, y_ref, g1_ref, g2_ref, xn_ref, h_ref):
        xn = x_ref[...] + _rms(y_ref[...], g1_ref[...])
        xn_ref[...] = xn
        h_ref[...] = _rms(xn, g2_ref[...]).astype(BF16)

    return pl.pallas_call(
        body, name=name, grid=(t // TM,),
        in_specs=[_row_spec(d), _row_spec(d), _vec_spec(d), _vec_spec(d)],
        out_specs=[_row_spec(d), _row_spec(d)],
        out_shape=[jax.ShapeDtypeStruct((t, d), F32), jax.ShapeDtypeStruct((t, d), BF16)],
        compiler_params=_cp("parallel"))(x, y, g1, g2)


def _final_loss(x, y, g, tgt, name):
    t, d = x.shape

    def body(x_ref, y_ref, g_ref, t_ref, l_ref, dx_ref):
        @pl.when(pl.program_id(0) == 0)
        def _():
            l_ref[...] = jnp.zeros_like(l_ref)

        err = x_ref[...] + _rms(y_ref[...], g_ref[...]) - t_ref[...]
        dx_ref[...] = err * (1.0 / d)
        l_ref[...] += jnp.sum(jnp.sum(err * err, axis=1, keepdims=True), axis=0, keepdims=True) * (0.5 / d)

    return pl.pallas_call(
        body, name=name, grid=(t // TM,),
        in_specs=[_row_spec(d), _row_spec(d), _vec_spec(d), _row_spec(d)],
        out_specs=[pl.BlockSpec((1, LANES), lambda i: (0, 0)), _row_spec(d)],
        out_shape=[jax.ShapeDtypeStruct((1, LANES), F32), jax.ShapeDtypeStruct((t, d), F32)],
        compiler_params=_cp("arbitrary"))(x, y, g, tgt)


def _rms_bwd_vals(dn, v, g):
    w = v.shape[-1]
    r = lax.rsqrt(jnp.mean(v * v, axis=-1, keepdims=True) + EPS)
    vh = v * r
    dgp = jnp.sum(dn * vh, axis=0, keepdims=True)
    dvh = dn * g
    dv = r * (dvh - vh * (jnp.sum(dvh * vh, axis=-1, keepdims=True) * (1.0 / w)))
    return dv, dgp


def _norm_bwd(dn, v, g, resid, out_dtype, name):
    t, d = v.shape
    has_res = resid is not None

    def body(*refs):
        if has_res:
            dn_ref, v_ref, g_ref, r_ref, dv_ref, dg_ref = refs
        else:
            dn_ref, v_ref, g_ref, dv_ref, dg_ref = refs

        @pl.when(pl.program_id(0) == 0)
        def _():
            dg_ref[...] = jnp.zeros_like(dg_ref)

        dv, dgp = _rms_bwd_vals(dn_ref[...].astype(F32), v_ref[...], g_ref[...])
        if has_res:
            dv = dv + r_ref[...]
        dv_ref[...] = dv.astype(out_dtype)
        dg_ref[...] += dgp

    in_specs = [_row_spec(d), _row_spec(d), _vec_spec(d)] + ([_row_spec(d)] if has_res else [])
    args = [dn, v, g] + ([resid] if has_res else [])
    return pl.pallas_call(
        body, name=name, grid=(t // TM,), in_specs=in_specs,
        out_specs=[_row_spec(d), _vec_spec(d)],
        out_shape=[jax.ShapeDtypeStruct((t, d), out_dtype), jax.ShapeDtypeStruct((1, d), F32)],
        compiler_params=_cp("arbitrary"))(*args)


def _rope_trig(pos, invf):
    ang = pos * invf
    return jnp.cos(ang), jnp.sin(ang)


def _rope_tables(trig, w, lo, half):
    c, s = trig
    lane = lax.broadcasted_iota(jnp.int32, c.shape, 1)
    active = (lane >= lo) & (lane < lo + 2 * half)
    cos = jnp.concatenate([jnp.where(active, c, 1.0)] * (w // LANES), axis=1)
    sin = jnp.concatenate([jnp.where(active, s, 0.0)] * (w // LANES), axis=1)
    lanes = lax.broadcasted_iota(jnp.int32, (c.shape[0], w), 1) & (LANES - 1)
    first = (lanes >= lo) & (lanes < lo + half)
    second = (lanes >= lo + half) & (lanes < lo + 2 * half)
    return cos, sin, first, second


def _rope_apply(v, cos, sin, first, second, half, sign):
    w = v.shape[-1]
    up = pltpu.roll(v, w - half, 1)
    dn = pltpu.roll(v, half, 1)
    rot = jnp.where(first, -up, jnp.where(second, dn, 0.0))
    return v * cos + rot * (sin * sign)


def _forget_lsf(proj, b_pad, name):
    t = proj.shape[0]

    def body(ff_ref, b_ref, o_ref):
        f = ff_ref[...] + b_ref[...]
        o_ref[...] = -(jnp.maximum(-f, 0.0) + jnp.log(1.0 + jnp.exp(-jnp.abs(f))))

    return pl.pallas_call(
        body, name=name, grid=(t // TM,),
        in_specs=[pl.BlockSpec((TM, LANES), lambda i: (i, OFF_FF // LANES)), _vec_spec(LANES)],
        out_specs=_row_spec(LANES), out_shape=jax.ShapeDtypeStruct((t, LANES), F32),
        compiler_params=_cp("parallel"))(proj, b_pad)


def _split3(c):
    hi = c.astype(BF16).astype(F32)
    mid = (c - hi).astype(BF16).astype(F32)
    return hi, mid, (c - hi) - mid


def _transposed_tiles(dst, row0, blocks, width):
    for b, blk in enumerate(blocks):
        bt = blk.T.astype(BF16)
        rows = bt.shape[0]
        for w in range(TM // width):
            dst[w, row0 + b * rows:row0 + (b + 1) * rows, :] = bt[:, w * width:(w + 1) * width]


def _prep_fwd(proj, cum, pos, invf, lg_lanes, g_q, g_kv, wq_pad, wkv_pad, name):
    t = proj.shape[0]

    def body(fox_ref, ret_ref, sb_ref, cq_ref, ckv_ref, kr_ref, cum_ref, pos_ref, invf_ref, lg_ref,
             gq_ref, gkv_ref, wq_ref, wkv_ref,
             ofox_ref, oret_ref, osb_ref, omla_ref, ofoxt_ref, osbt_ref, omlat_ref, orett_ref):
        pos_v = pos_ref[...]
        osb_ref[:, 0:HP] = (sb_ref[:, 0:HP] * 0.125).astype(BF16)
        osb_ref[:, HP:QKV] = sb_ref[:, HP:QKV].astype(BF16)
        _transposed_tiles(osbt_ref, 0, [sb_ref[:, HP:2 * HP], sb_ref[:, 2 * HP:QKV]], TKS)
        lane = lax.broadcasted_iota(jnp.int32, (TM, LANES), 1)
        cumv = cum_ref[...]
        fq, fk = [], []
        for hb in range(HEADS):
            hi, mid, lo = _split3(cumv[:, hb:hb + 1])
            q = fox_ref[:, hb * LANES:(hb + 1) * LANES] * 0.125
            k = fox_ref[:, HP + hb * LANES:HP + (hb + 1) * LANES]
            ones_q = (lane >= HEAD_DIM) & (lane < HEAD_DIM + 3)
            ones_k = (lane >= HEAD_DIM + 3) & (lane < HEAD_DIM + 6)
            q = jnp.where(ones_q, 1.0, q)
            k = jnp.where(ones_k, 1.0, k)
            for n, part in enumerate((hi, mid, lo)):
                q = jnp.where(lane == HEAD_DIM + 3 + n, part, q)
                k = jnp.where(lane == HEAD_DIM + n, -part, k)
            fq.append(q)
            fk.append(k)
        fk = jnp.concatenate(fk, axis=1)
        ofox_ref[:, 0:HP] = jnp.concatenate(fq, axis=1).astype(BF16)
        ofox_ref[:, HP:2 * HP] = fk.astype(BF16)
        ofox_ref[:, 2 * HP:QKV] = fox_ref[:, 2 * HP:QKV].astype(BF16)
        _transposed_tiles(ofoxt_ref, 0, [fk, fox_ref[:, 2 * HP:QKV]], BQ)
        trig = _rope_trig(pos_v, invf_ref[...])
        cos, sin, first, second = _rope_tables(trig, HP, 0, HEAD_DIM // 2)
        nloc = lax.broadcasted_iota(jnp.int32, (TM, 1), 0).astype(F32)
        dec = lg_ref[...] * nloc
        rq = _rope_apply(ret_ref[:, 0:HP], cos, sin, first, second, HEAD_DIM // 2, 1.0)
        rk = _rope_apply(ret_ref[:, HP:2 * HP], cos, sin, first, second, HEAD_DIM // 2, 1.0)
        oret_ref[:, 0:HP] = (rq * jnp.exp(dec)).astype(BF16)
        rk = rk * 0.125 * jnp.exp(-dec)
        oret_ref[:, HP:2 * HP] = rk.astype(BF16)
        _transposed_tiles(orett_ref, 0, [rk], BQ)
        oret_ref[:, 2 * HP:QKV] = ret_ref[:, 2 * HP:QKV].astype(BF16)
        cosm, sinm, firstm, secondm = _rope_tables(trig, HP, HEAD_DIM, ROPE_MLA // 2)
        cqn = _rms(cq_ref[...], gq_ref[...]).astype(BF16)
        qm = _bdot(cqn, wq_ref[...])
        omla_ref[:, 0:HP] = _rope_apply(qm, cosm, sinm, firstm, secondm, ROPE_MLA // 2, 1.0).astype(BF16)
        ckvn = _rms(ckv_ref[...], gkv_ref[...]).astype(BF16)
        kv = _bdot(ckvn, wkv_ref[...])
        krr = _rope_apply(kr_ref[...], cosm[:, 0:LANES], sinm[:, 0:LANES], firstm[:, 0:LANES],
                          secondm[:, 0:LANES], ROPE_MLA // 2, 1.0)
        mk = kv[:, 0:HP] + jnp.concatenate([krr] * HEADS, axis=1)
        omla_ref[:, HP:2 * HP] = mk.astype(BF16)
        omla_ref[:, 2 * HP:QKV] = kv[:, HP:2 * HP].astype(BF16)
        _transposed_tiles(omlat_ref, 0, [mk, kv[:, HP:2 * HP]], BQ)

    def seg(off, w):
        return pl.BlockSpec((TM, w), lambda i, o=off // w: (i, o))

    def full(shape):
        return pl.BlockSpec(shape, lambda i: (0,) * len(shape))

    def tiles(width):
        return pl.BlockSpec((TM // width, 2 * HP, width), lambda i: (i, 0, 0))

    in_specs = [seg(OFF_FOX, QKV), seg(OFF_RET, QKV), seg(OFF_SB, QKV), seg(OFF_CQ, Q_RANK), seg(OFF_CKV, LANES),
                seg(OFF_KR, LANES), _row_spec(LANES), pl.BlockSpec((TM, 1), lambda i: (i, 0)),
                full((1, LANES)), full((1, HP)), full((1, Q_RANK)), full((1, KV_RANK)),
                full((Q_RANK, HP)), full((KV_RANK, 2 * HP))]
    out_specs = [_row_spec(QKV)] * 4 + [tiles(BQ), tiles(TKS), tiles(BQ),
                                        pl.BlockSpec((1, HP, BQ), lambda i: (i, 0, 0))]
    out_shape = [jax.ShapeDtypeStruct((t, QKV), BF16)] * 4 + [
        jax.ShapeDtypeStruct((t // BQ, 2 * HP, BQ), BF16), jax.ShapeDtypeStruct((t // TKS, 2 * HP, TKS), BF16),
        jax.ShapeDtypeStruct((t // BQ, 2 * HP, BQ), BF16), jax.ShapeDtypeStruct((t // BQ, HP, BQ), BF16)]
    return pl.pallas_call(
        body, name=name, grid=(t // TM,), in_specs=in_specs, out_specs=out_specs, out_shape=out_shape,
        compiler_params=_cp("parallel"))(proj, proj, proj, proj, proj, proj, cum, pos, invf, lg_lanes,
                                         g_q, g_kv, wq_pad, wkv_pad)


def _prep_bwd(dfox, dret, dsb, dmla, drg, dlsf, proj, pos, invf, lg_lanes, b_pad, g_q, g_kv,
              wq_pad, wkv_pad, name):
    t = proj.shape[0]

    def body(dfq, dfk, dfv, drq, drk, drv, dsq, dsk, dsv, dmq, dmk, dmv, drg_ref, dlsf_ref,
             cq_ref, ckv_ref, ff_ref, pos_ref, invf_ref, lg_ref, b_ref, gq_ref, gkv_ref, wq_ref, wkv_ref,
             dp_ref, dwq_ref, dwkv_ref, dgq_ref, dgkv_ref, dbf_ref):
        @pl.when(pl.program_id(0) == 0)
        def _():
            dwq_ref[...] = jnp.zeros_like(dwq_ref)
            dwkv_ref[...] = jnp.zeros_like(dwkv_ref)
            dgq_ref[...] = jnp.zeros_like(dgq_ref)
            dgkv_ref[...] = jnp.zeros_like(dgkv_ref)
            dbf_ref[...] = jnp.zeros_like(dbf_ref)

        pos_v = pos_ref[...]
        for off, (dq, dk, dv) in ((OFF_FOX, (dfq, dfk, dfv)), (OFF_SB, (dsq, dsk, dsv))):
            dp_ref[:, off:off + HP] = (dq[...] * 0.125).astype(BF16)
            dp_ref[:, off + HP:off + 2 * HP] = dk[...].astype(BF16)
            dp_ref[:, off + 2 * HP:off + QKV] = dv[...].astype(BF16)
        trig = _rope_trig(pos_v, invf_ref[...])
        cos, sin, first, second = _rope_tables(trig, HP, 0, HEAD_DIM // 2)
        nloc = lax.broadcasted_iota(jnp.int32, (TM, 1), 0).astype(F32)
        dec = lg_ref[...] * nloc
        dq = _rope_apply(drq[...] * jnp.exp(dec), cos, sin, first, second, HEAD_DIM // 2, -1.0)
        dk = _rope_apply(drk[...] * (0.125 * jnp.exp(-dec)), cos, sin, first, second, HEAD_DIM // 2, -1.0)
        dp_ref[:, OFF_RET:OFF_RET + HP] = dq.astype(BF16)
        dp_ref[:, OFF_RET + HP:OFF_RET + 2 * HP] = dk.astype(BF16)
        dp_ref[:, OFF_RET + 2 * HP:OFF_RET + QKV] = drv[...].astype(BF16)
        dp_ref[:, OFF_RG:OFF_RG + HP] = drg_ref[...].astype(BF16)
        cosm, sinm, firstm, secondm = _rope_tables(trig, HP, HEAD_DIM, ROPE_MLA // 2)
        dql = _rope_apply(dmq[...], cosm, sinm, firstm, secondm, ROPE_MLA // 2, -1.0).astype(BF16)
        cq = cq_ref[...]
        cqn = _rms(cq, gq_ref[...]).astype(BF16)
        dwq_ref[...] += _bdot(cqn, dql, TN)
        dcqn = _bdot(dql, wq_ref[...], NT)
        dcq, dgq = _rms_bwd_vals(dcqn, cq, gq_ref[...])
        dgq_ref[...] += dgq
        dp_ref[:, OFF_CQ:OFF_CQ + Q_RANK] = dcq.astype(BF16)
        dkm = dmk[...]
        dkv = jnp.concatenate([dkm, dmv[...]], axis=1).astype(BF16)
        ckv = ckv_ref[...]
        ckvn = _rms(ckv, gkv_ref[...]).astype(BF16)
        dwkv_ref[...] += _bdot(ckvn, dkv, TN)
        dckvn = _bdot(dkv, wkv_ref[...], NT)
        dckv, dgkv = _rms_bwd_vals(dckvn, ckv, gkv_ref[...])
        dgkv_ref[...] += dgkv
        dp_ref[:, OFF_CKV:OFF_CKV + LANES] = dckv.astype(BF16)
        dkr = dkm[:, 0:LANES] + dkm[:, LANES:2 * LANES] + dkm[:, 2 * LANES:3 * LANES] + dkm[:, 3 * LANES:HP]
        act = firstm[:, 0:LANES] | secondm[:, 0:LANES]
        dkr = jnp.where(act, dkr, 0.0)
        dkr = _rope_apply(dkr, cosm[:, 0:LANES], sinm[:, 0:LANES], firstm[:, 0:LANES], secondm[:, 0:LANES],
                          ROPE_MLA // 2, -1.0)
        dp_ref[:, OFF_KR:OFF_KR + LANES] = dkr.astype(BF16)
        f = ff_ref[...] + b_ref[...]
        dff = dlsf_ref[...] / (1.0 + jnp.exp(f))
        dbf_ref[...] += jnp.sum(dff, axis=0, keepdims=True)
        dp_ref[:, OFF_FF:OFF_FF + LANES] = dff.astype(BF16)
        dp_ref[:, OFF_FF + LANES:NP_IN] = jnp.zeros((TM, NP_IN - OFF_FF - LANES), BF16)

    def seg(off, w):
        return pl.BlockSpec((TM, w), lambda i, o=off // w: (i, o))

    def full(shape):
        return pl.BlockSpec(shape, lambda i: (0,) * len(shape))

    hp_spec = _row_spec(HP)
    in_specs = [hp_spec] * 13 + [_row_spec(LANES), seg(OFF_CQ, Q_RANK), seg(OFF_CKV, LANES), seg(OFF_FF, LANES),
                                 pl.BlockSpec((TM, 1), lambda i: (i, 0)),
                                 full((1, LANES)), full((1, HP)), full((1, LANES)), full((1, Q_RANK)),
                                 full((1, KV_RANK)), full((Q_RANK, HP)), full((KV_RANK, 2 * HP))]
    out_specs = [_row_spec(NP_IN), full((Q_RANK, HP)), full((KV_RANK, 2 * HP)), full((1, Q_RANK)),
                 full((1, KV_RANK)), full((1, LANES))]
    out_shape = [jax.ShapeDtypeStruct((t, NP_IN), BF16), jax.ShapeDtypeStruct((Q_RANK, HP), F32),
                 jax.ShapeDtypeStruct((KV_RANK, 2 * HP), F32), jax.ShapeDtypeStruct((1, Q_RANK), F32),
                 jax.ShapeDtypeStruct((1, KV_RANK), F32), jax.ShapeDtypeStruct((1, LANES), F32)]
    return pl.pallas_call(
        body, name=name, grid=(t // TM,), in_specs=in_specs, out_specs=out_specs, out_shape=out_shape,
        compiler_params=_cp("arbitrary"))(*dfox, *dret, *dsb, *dmla, drg, dlsf, proj, proj, proj, pos, invf,
                                          lg_lanes, b_pad, g_q, g_kv, wq_pad, wkv_pad)


def _cumsum(x, reverse, name, partials=None):
    t, w = x.shape
    n = t // TM
    xs = [x] if partials is None else [x, partials]

    def body(*refs):
        x_refs, o_ref, carry = refs[:len(xs)], refs[len(xs)], refs[len(xs) + 1]

        @pl.when(pl.program_id(0) == 0)
        def _():
            carry[...] = jnp.zeros_like(carry)

        r = lax.broadcasted_iota(jnp.int32, (TM, TM), 0)
        c = lax.broadcasted_iota(jnp.int32, (TM, TM), 1)
        tri = jnp.where((r <= c) if reverse else (r >= c), 1.0, 0.0).astype(BF16)
        v = x_refs[0][...]
        if partials is not None:
            lane = lax.broadcasted_iota(jnp.int32, (TM, LANES), 1)
            for hb in range(HEADS):
                v = v + jnp.where(lane == hb, jnp.sum(x_refs[1][:, _hs(hb)], axis=1, keepdims=True), 0.0)
        hi = v.astype(BF16)
        r1 = v - hi.astype(F32)
        mid = r1.astype(BF16)
        lo = (r1 - mid.astype(F32)).astype(BF16)
        cs = _bdot(tri, hi) + _bdot(tri, mid) + _bdot(tri, lo) + carry[...]
        o_ref[...] = cs
        carry[...] = cs[0:1, :] if reverse else cs[TM - 1:TM, :]

    imap = (lambda i: (n - 1 - i, 0)) if reverse else (lambda i: (i, 0))
    return pl.pallas_call(
        body, name=name, grid=(n,), in_specs=[pl.BlockSpec((TM, a.shape[1]), imap) for a in xs],
        out_specs=pl.BlockSpec((TM, w), imap),
        out_shape=jax.ShapeDtypeStruct((t, w), F32), scratch_shapes=[pltpu.VMEM((1, w), F32)],
        compiler_params=_cp("arbitrary"))(*xs)


HB_FWD = 4
HB_BWD = 2
HB_SB_FWD = 4
BQS = 512
BQA = 512


def _q_spec(hb, bq=BQ):
    return pl.BlockSpec((bq, hb * LANES), lambda g, i: (i, g))


def _kv_spec(t, which, hb):
    return pl.BlockSpec((t, hb * LANES), lambda g, i, w=which: (0, w * (HEADS // hb) + g))


def _acc_spec(t, hb):
    return pl.BlockSpec((t, hb * LANES), lambda g, i: (0, g))


def _hs(hh):
    return slice(hh * LANES, (hh + 1) * LANES)


def _tile_iota(rows, cols):
    return (lax.broadcasted_iota(jnp.int32, (rows, cols), 0), lax.broadcasted_iota(jnp.int32, (rows, cols), 1))


def _kvt_spec(nkv, width, which, hb):
    return pl.BlockSpec((nkv, hb * LANES, width), lambda g, i, w=which: (0, w * (HEADS // hb) + g, 0))


def _qrow_spec(hb, bq=BQ):
    return pl.BlockSpec((hb, 1, 1, bq), lambda g, i: (g, i, 0, 0))


def _vis(key0, query0, rows, cols, kind):
    r, c = _tile_iota(rows, cols)
    k, q = key0 + r, query0 + c
    if kind == "chunk":
        return (k >> CHUNK_SHIFT) <= (q >> CHUNK_SHIFT)
    return (k < q) if kind == "strict" else (k <= q)


def _softmax_fwd(qkv, kvt, *, chunk_mask, scale, name):
    t = qkv.shape[0]
    nq = t // BQA
    per = BQA // BQ
    hb = HB_FWD
    kind = "chunk" if chunk_mask else "causal"

    def body(q_ref, k_ref, vt_ref, o_ref, lse_ref, m_sc, l_sc, acc_sc):
        i = pl.program_id(1)
        m_sc[...] = jnp.full((hb, 1, BQA), NEG, F32)
        l_sc[...] = jnp.zeros((hb, 1, BQA), F32)
        acc_sc[...] = jnp.zeros((hb, LANES, BQA), F32)

        def tile(j, qoff):
            off = pl.multiple_of(j * BQ, BQ)
            lo = 0 if qoff is None else qoff
            qs = slice(lo, BQA)
            vis = None if qoff is None else _vis(off, i * BQA + lo, BQ, BQA - lo, kind)
            ss = [_bdot(k_ref[pl.ds(off, BQ), _hs(hh)], q_ref[qs, _hs(hh)], NT) for hh in range(hb)]
            ps, alphas = [], []
            for hh in range(hb):
                s = ss[hh]
                if scale != 1.0:
                    s = s * scale
                if vis is not None:
                    s = jnp.where(vis, s, NEG)
                m_old = m_sc[hh, :, qs]
                m_new = jnp.maximum(m_old, jnp.max(s, axis=0, keepdims=True))
                alpha = jnp.exp(m_old - m_new)
                p = jnp.exp(s - m_new)
                l_sc[hh, :, qs] = alpha * l_sc[hh, :, qs] + jnp.sum(p, axis=0, keepdims=True)
                m_sc[hh, :, qs] = m_new
                ps.append(p.astype(BF16))
                alphas.append(alpha)
            for hh in range(hb):
                acc_sc[hh, :, qs] = alphas[hh] * acc_sc[hh, :, qs] + _bdot(vt_ref[j, _hs(hh), :], ps[hh])

        def loop(j, carry):
            tile(j, None)
            return carry

        lax.fori_loop(0, per * i, loop, 0)
        for d in range(per):
            tile(per * i + d, d * BQ)
        for hh in range(hb):
            l = l_sc[hh]
            o_ref[:, _hs(hh)] = (acc_sc[hh] / l).T
            lse_ref[hh, 0] = m_sc[hh] + jnp.log(l)

    return pl.pallas_call(
        body, name=name, grid=(HEADS // hb, nq),
        in_specs=[_q_spec(hb, BQA), _kv_spec(t, 1, hb), _kvt_spec(t // BQ, BQ, 1, hb)],
        out_specs=[_q_spec(hb, BQA), _qrow_spec(hb, BQA)],
        out_shape=[jax.ShapeDtypeStruct((t, HP), F32), jax.ShapeDtypeStruct((HEADS, nq, 1, BQA), F32)],
        scratch_shapes=[pltpu.VMEM((hb, 1, BQA), F32), pltpu.VMEM((hb, 1, BQA), F32),
                        pltpu.VMEM((hb, LANES, BQA), F32)],
        compiler_params=_cp("parallel", "arbitrary"))(qkv, qkv, kvt)


def _softmax_bwd(qkv, kvt, do, lse, delta, *, bias, chunk_mask, scale, name):
    t = qkv.shape[0]
    nq = t // BQA
    per = BQA // BQ
    hb = HB_BWD
    kind = "chunk" if chunk_mask else "causal"

    def body(*refs):
        if bias:
            (q_ref, k_ref, v_ref, kt_ref, do_ref, lse_ref, dl_ref, dq_ref, dk_ref, dv_ref, dck_ref, dcq_ref,
             dq_sc, dcq_sc) = refs
            dcq_sc[...] = jnp.zeros((hb, 1, BQA), F32)
        else:
            q_ref, k_ref, v_ref, kt_ref, do_ref, lse_ref, dl_ref, dq_ref, dk_ref, dv_ref, dq_sc = refs
        i = pl.program_id(1)

        @pl.when(i == 0)
        def _():
            dk_ref[...] = jnp.zeros_like(dk_ref)
            dv_ref[...] = jnp.zeros_like(dv_ref)
            if bias:
                dck_ref[...] = jnp.zeros_like(dck_ref)

        dq_sc[...] = jnp.zeros((hb, LANES, BQA), F32)

        def tile(j, qoff):
            off = pl.multiple_of(j * BQ, BQ)
            lo = 0 if qoff is None else qoff
            qsl = slice(lo, BQA)
            vis = None if qoff is None else _vis(off, i * BQA + lo, BQ, BQA - lo, kind)
            qs = [q_ref[qsl, _hs(hh)] for hh in range(hb)]
            dobs = [do_ref[qsl, _hs(hh)].astype(BF16) for hh in range(hb)]
            ss = [_bdot(k_ref[pl.ds(off, BQ), _hs(hh)], qs[hh], NT) for hh in range(hb)]
            dps = [_bdot(v_ref[pl.ds(off, BQ), _hs(hh)], dobs[hh], NT) for hh in range(hb)]
            pbs, dsbs = [], []
            for hh in range(hb):
                s = ss[hh]
                if scale != 1.0:
                    s = s * scale
                p = jnp.exp(s - lse_ref[hh, 0, :, qsl])
                if vis is not None:
                    p = jnp.where(vis, p, 0.0)
                ds = p * (dps[hh] - dl_ref[hh, 0, :, qsl])
                if bias:
                    part = ds[:, 0:LANES]
                    for b in range(1, (BQA - lo) // LANES):
                        part = part + ds[:, b * LANES:(b + 1) * LANES]
                    dck_ref[pl.ds(off, BQ), _hs(hh)] -= part
                    dcq_sc[hh, :, qsl] += jnp.sum(ds, axis=0, keepdims=True)
                if scale != 1.0:
                    ds = ds * scale
                pbs.append(p.astype(BF16))
                dsbs.append(ds.astype(BF16))
            for hh in range(hb):
                sl = _hs(hh)
                dv_ref[pl.ds(off, BQ), sl] += _bdot(pbs[hh], dobs[hh])
                dk_ref[pl.ds(off, BQ), sl] += _bdot(dsbs[hh], qs[hh])
                dq_sc[hh, :, qsl] += _bdot(kt_ref[j, sl, :], dsbs[hh])

        def loop(j, carry):
            tile(j, None)
            return carry

        lax.fori_loop(0, per * i, loop, 0)
        for d in range(per):
            tile(per * i + d, d * BQ)
        for hh in range(hb):
            dq_ref[:, _hs(hh)] = dq_sc[hh].T
            if bias:
                dcq_ref[hh, 0] = dcq_sc[hh]

    in_specs = [_q_spec(hb, BQA), _kv_spec(t, 1, hb), _kv_spec(t, 2, hb), _kvt_spec(t // BQ, BQ, 0, hb),
                _q_spec(hb, BQA), _qrow_spec(hb, BQA), _qrow_spec(hb, BQA)]
    out_specs = [_q_spec(hb, BQA), _acc_spec(t, hb), _acc_spec(t, hb)]
    out_shape = [jax.ShapeDtypeStruct((t, HP), F32)] * 3
    scratch = [pltpu.VMEM((hb, LANES, BQA), F32)]
    if bias:
        out_specs += [_acc_spec(t, hb), _qrow_spec(hb, BQA)]
        out_shape += [jax.ShapeDtypeStruct((t, HP), F32), jax.ShapeDtypeStruct((HEADS, nq, 1, BQA), F32)]
        scratch.append(pltpu.VMEM((hb, 1, BQA), F32))
    return pl.pallas_call(
        body, name=name, grid=(HEADS // hb, nq), in_specs=in_specs, out_specs=out_specs, out_shape=out_shape,
        scratch_shapes=scratch,
        compiler_params=_cp("parallel", "arbitrary"))(qkv, qkv, qkv, kvt, do, lse, delta)


def _ret_diag_decay(lg1, keys_on_rows=False):
    r, c = _tile_iota(BQ, BQ)
    qn, km = (c, r) if keys_on_rows else (r, c)
    dd = jnp.where(km > qn, jnp.exp((2.0 * lg1) * (km - qn).astype(F32)), 1.0)
    return jnp.where((km >> CHUNK_SHIFT) <= (qn >> CHUNK_SHIFT), dd, 0.0)


def _lg_spec(hb):
    return pl.BlockSpec((hb, 1, LANES), lambda g, i: (g, 0, 0))


def _ret_specs(nq, hb, reverse):
    tile = (lambda i: nq - 1 - i) if reverse else (lambda i: i)
    qkv = [pl.BlockSpec((BQ, hb * LANES), lambda g, i, w=w: (tile(i), w * (HEADS // hb) + g)) for w in range(3)]
    kt = pl.BlockSpec((1, hb * LANES, BQ), lambda g, i: (tile(i), g, 0))
    st = pl.BlockSpec((1, hb * LANES, LANES), lambda g, i: (tile(i), g, 0))
    return qkv, kt, st


def _ret_fwd(qkv, kt, lg_heads, name):
    t = qkv.shape[0]
    nq = t // BQ
    hb = HB_FWD

    def body(lg_ref, q_ref, k_ref, v_ref, kt_ref, o_ref, st_ref, s_sc):
        @pl.when(pl.program_id(1) == 0)
        def _():
            s_sc[...] = jnp.zeros_like(s_sc)

        qs = [q_ref[:, _hs(hh)] for hh in range(hb)]
        vs = [v_ref[:, _hs(hh)] for hh in range(hb)]
        aa = [_bdot(qs[hh], k_ref[:, _hs(hh)], NT) for hh in range(hb)]
        kv = [_bdot(kt_ref[0, _hs(hh), :], vs[hh]) for hh in range(hb)]
        for hh in range(hb):
            sl = _hs(hh)
            lg1 = lg_ref[hh][:, 0:1]
            s = s_sc[hh]
            st_ref[0, sl, :] = s
            shi, slo = _split2(s)
            a = (aa[hh] * _ret_diag_decay(lg1)).astype(BF16)
            o_ref[:, sl] = _bdot(a, vs[hh]) + _bdot(qs[hh], shi) + _bdot(qs[hh], slo)
            s_sc[hh] = jnp.exp(lg1 * float(BQ)) * (s + kv[hh])

    qkv_specs, kt_spec, st_spec = _ret_specs(nq, hb, False)
    return pl.pallas_call(
        body, name=name, grid=(HEADS // hb, nq), in_specs=[_lg_spec(hb)] + qkv_specs + [kt_spec],
        out_specs=[_q_spec(hb), st_spec],
        out_shape=[jax.ShapeDtypeStruct((t, HP), F32), jax.ShapeDtypeStruct((nq, HP, LANES), F32)],
        scratch_shapes=[pltpu.VMEM((hb, LANES, LANES), F32)],
        compiler_params=_cp("parallel", "arbitrary"))(lg_heads, qkv, qkv, qkv, kt)


def _ret_bwd(qkv, states, lg_heads, do, name):
    t = qkv.shape[0]
    nq = t // BQ
    hb = HB_FWD

    def body(lg_ref, q_ref, k_ref, v_ref, st_ref, do_ref, dq_ref, dk_ref, dv_ref, g_sc):
        @pl.when(pl.program_id(1) == 0)
        def _():
            g_sc[...] = jnp.zeros_like(g_sc)

        qs = [q_ref[:, _hs(hh)] for hh in range(hb)]
        ks = [k_ref[:, _hs(hh)] for hh in range(hb)]
        vs = [v_ref[:, _hs(hh)] for hh in range(hb)]
        dobs = [do_ref[:, _hs(hh)].astype(BF16) for hh in range(hb)]
        aa = [_bdot(ks[hh], qs[hh], NT) for hh in range(hb)]
        das = [_bdot(vs[hh], dobs[hh], NT) for hh in range(hb)]
        qdo = [_bdot(qs[hh], dobs[hh], TN) for hh in range(hb)]
        for hh in range(hb):
            sl = _hs(hh)
            lg1 = lg_ref[hh][:, 0:1]
            dd = _ret_diag_decay(lg1, keys_on_rows=True)
            at = (aa[hh] * dd).astype(BF16)
            dat = (das[hh] * dd).astype(BF16)
            h = jnp.exp(lg1 * float(BQ)) * g_sc[hh]
            hhi, hlo = _split2(h)
            shi, slo = _split2(st_ref[0, sl, :])
            dv_ref[:, sl] = _bdot(at, dobs[hh]) + _bdot(ks[hh], hhi) + _bdot(ks[hh], hlo)
            dk_ref[:, sl] = _bdot(dat, qs[hh]) + _bdot(vs[hh], hhi, NT) + _bdot(vs[hh], hlo, NT)
            dq_ref[:, sl] = _bdot(dat, ks[hh], TN) + _bdot(dobs[hh], shi, NT) + _bdot(dobs[hh], slo, NT)
            g_sc[hh] = qdo[hh] + h

    qkv_specs, _, st_spec = _ret_specs(nq, hb, True)
    tile_spec = pl.BlockSpec((BQ, hb * LANES), lambda g, i: (nq - 1 - i, g))
    return pl.pallas_call(
        body, name=name, grid=(HEADS // hb, nq), in_specs=[_lg_spec(hb)] + qkv_specs + [st_spec, tile_spec],
        out_specs=[tile_spec] * 3, out_shape=[jax.ShapeDtypeStruct((t, HP), F32)] * 3,
        scratch_shapes=[pltpu.VMEM((hb, LANES, LANES), F32)],
        compiler_params=_cp("parallel", "arbitrary"))(lg_heads, qkv, qkv, qkv, states, do)


def _sb_tile_logs(q, kb, vis):
    z = _bdot(kb, q, NT)
    ls = -(jnp.maximum(z, 0.0) + jnp.log(1.0 + jnp.exp(-jnp.abs(z))))
    if vis is not None:
        ls = jnp.where(vis, ls, 0.0)
    return z, ls


def _sb_later(ls, after):
    hi, lo = _split2(ls)
    return _bdot(after, hi) + _bdot(after, lo)


def _sb_fwd(qkv, kvt, name):
    t = qkv.shape[0]
    nq = t // BQS
    per = BQS // TKS
    hb = HB_SB_FWD

    def body(q_ref, k_ref, vt_ref, o_ref, tot_ref, acc_sc, r_sc):
        i = pl.program_id(1)
        acc_sc[...] = jnp.zeros((hb, LANES, BQS), F32)
        r_sc[...] = jnp.zeros((hb, 1, BQS), F32)
        mr, mc = _tile_iota(TKS, TKS)
        after = jnp.where(mc > mr, 1.0, 0.0).astype(BF16)

        def tile(j, qoff):
            off = pl.multiple_of(j * TKS, TKS)
            lo = 0 if qoff is None else qoff
            qsl = slice(lo, BQS)
            vis = None if qoff is None else _vis(off, i * BQS + lo, TKS, BQS - lo, "strict")
            zl = [_sb_tile_logs(q_ref[qsl, _hs(hh)], k_ref[pl.ds(off, TKS), _hs(hh)], vis) for hh in range(hb)]
            laters = [_sb_later(zl[hh][1], after) for hh in range(hb)]
            ws = []
            for hh in range(hb):
                z, ls = zl[hh]
                w = jnp.exp(z + ls + laters[hh] + r_sc[hh, :, qsl])
                if vis is not None:
                    w = jnp.where(vis, w, 0.0)
                ws.append(w.astype(BF16))
                r_sc[hh, :, qsl] += jnp.sum(ls, axis=0, keepdims=True)
            for hh in range(hb):
                acc_sc[hh, :, qsl] += _bdot(vt_ref[j, _hs(hh), :], ws[hh])

        for d in reversed(range(per)):
            tile(per * i + d, d * TKS)

        def loop(jj, carry):
            tile(per * i - 1 - jj, None)
            return carry

        lax.fori_loop(0, per * i, loop, 0)
        for hh in range(hb):
            o_ref[:, _hs(hh)] = acc_sc[hh].T
            tot_ref[hh, 0] = r_sc[hh]

    return pl.pallas_call(
        body, name=name, grid=(HEADS // hb, nq),
        in_specs=[_q_spec(hb, BQS), _kv_spec(t, 1, hb), _kvt_spec(t // TKS, TKS, 1, hb)],
        out_specs=[_q_spec(hb, BQS), _qrow_spec(hb, BQS)],
        out_shape=[jax.ShapeDtypeStruct((t, HP), F32), jax.ShapeDtypeStruct((HEADS, nq, 1, BQS), F32)],
        scratch_shapes=[pltpu.VMEM((hb, LANES, BQS), F32), pltpu.VMEM((hb, 1, BQS), F32)],
        compiler_params=_cp("parallel", "arbitrary"))(qkv, qkv, kvt)


def _sb_bwd(qkv, kvt, do, tot, name):
    t = qkv.shape[0]
    nq = t // BQS
    per = BQS // TKS
    hb = HB_BWD

    def body(q_ref, k_ref, v_ref, kt_ref, do_ref, tot_ref, dq_ref, dk_ref, dv_ref, dq_sc, p_sc, g_sc):
        i = pl.program_id(1)

        @pl.when(i == 0)
        def _():
            dk_ref[...] = jnp.zeros_like(dk_ref)
            dv_ref[...] = jnp.zeros_like(dv_ref)

        dq_sc[...] = jnp.zeros((hb, LANES, BQS), F32)
        p_sc[...] = jnp.zeros((hb, 1, BQS), F32)
        g_sc[...] = jnp.zeros((hb, 1, BQS), F32)
        mr, mc = _tile_iota(TKS, TKS)
        after = jnp.where(mc > mr, 1.0, 0.0).astype(BF16)
        before = jnp.where(mc < mr, 1.0, 0.0).astype(BF16)

        def tile(j, qoff):
            off = pl.multiple_of(j * TKS, TKS)
            lo = 0 if qoff is None else qoff
            qsl = slice(lo, BQS)
            vis = None if qoff is None else _vis(off, i * BQS + lo, TKS, BQS - lo, "strict")
            qs = [q_ref[qsl, _hs(hh)] for hh in range(hb)]
            dobs = [do_ref[qsl, _hs(hh)].astype(BF16) for hh in range(hb)]
            zl = [_sb_tile_logs(qs[hh], k_ref[pl.ds(off, TKS), _hs(hh)], vis) for hh in range(hb)]
            dws = [_bdot(v_ref[pl.ds(off, TKS), _hs(hh)], dobs[hh], NT) for hh in range(hb)]
            laters = [_sb_later(zl[hh][1], after) for hh in range(hb)]
            ws, gs = [], []
            for hh in range(hb):
                z, ls = zl[hh]
                own = jnp.sum(ls, axis=0, keepdims=True)
                rest = tot_ref[hh, 0, :, qsl] - p_sc[hh, :, qsl] - own
                w = jnp.exp(z + ls + laters[hh] + rest)
                if vis is not None:
                    w = jnp.where(vis, w, 0.0)
                p_sc[hh, :, qsl] += own
                ws.append(w.astype(BF16))
                gs.append(dws[hh] * w)
            gins = []
            for hh in range(hb):
                ghi, glo = _split2(gs[hh])
                gins.append(_bdot(before, ghi) + _bdot(before, glo))
            dzbs = []
            for hh in range(hb):
                g = gs[hh]
                stay = jnp.exp(zl[hh][1])
                dz = g * stay - (1.0 - stay) * (gins[hh] + g_sc[hh, :, qsl])
                if vis is not None:
                    dz = jnp.where(vis, dz, 0.0)
                g_sc[hh, :, qsl] += jnp.sum(g, axis=0, keepdims=True)
                dzbs.append(dz.astype(BF16))
            for hh in range(hb):
                sl = _hs(hh)
                dv_ref[pl.ds(off, TKS), sl] += _bdot(ws[hh], dobs[hh])
                dk_ref[pl.ds(off, TKS), sl] += _bdot(dzbs[hh], qs[hh])
                dq_sc[hh, :, qsl] += _bdot(kt_ref[j, sl, :], dzbs[hh])

        def loop(j, carry):
            tile(j, None)
            return carry

        lax.fori_loop(0, per * i, loop, 0)
        for d in range(per):
            tile(per * i + d, d * TKS)
        for hh in range(hb):
            dq_ref[:, _hs(hh)] = dq_sc[hh].T

    return pl.pallas_call(
        body, name=name, grid=(HEADS // hb, nq),
        in_specs=[_q_spec(hb, BQS), _kv_spec(t, 1, hb), _kv_spec(t, 2, hb), _kvt_spec(t // TKS, TKS, 0, hb),
                  _q_spec(hb, BQS), _qrow_spec(hb, BQS)],
        out_specs=[_q_spec(hb, BQS), _acc_spec(t, hb), _acc_spec(t, hb)],
        out_shape=[jax.ShapeDtypeStruct((t, HP), F32)] * 3,
        scratch_shapes=[pltpu.VMEM((hb, LANES, BQS), F32), pltpu.VMEM((hb, 1, BQS), F32),
                        pltpu.VMEM((hb, 1, BQS), F32)],
        compiler_params=_cp("parallel", "arbitrary"))(qkv, qkv, qkv, kvt, do, tot)


def _sigmoid(v):
    return 1.0 / (1.0 + jnp.exp(-v))


def _post_fwd(oa, ob, oc, od, proj, g_pad, name):
    t = oa.shape[0]

    def body(oa_ref, ob_ref, oc_ref, od_ref, rg_ref, g_ref, mx_ref):
        g = g_ref[...]

        def group(o, gg):
            r = lax.rsqrt(jnp.sum(o * o, axis=-1, keepdims=True) * (1.0 / GROUP) + EPS)
            return (o * r * gg).astype(BF16)

        mx_ref[:, 0:HP] = group(oa_ref[...], g[:, 0:HP])
        mx_ref[:, HP:2 * HP] = group(ob_ref[...], g[:, HP:2 * HP])
        mx_ref[:, 3 * HP:4 * HP] = group(od_ref[...], g[:, 3 * HP:4 * HP])
        real = lax.broadcasted_iota(jnp.int32, (TM, LANES), 1) < HEAD_DIM
        for hb in range(HEADS):
            sl = slice(hb * LANES, (hb + 1) * LANES)
            o = oc_ref[:, sl]
            mu = jnp.sum(o, axis=-1, keepdims=True) * (1.0 / HEAD_DIM)
            dlt = jnp.where(real, o - mu, 0.0)
            var = jnp.sum(dlt * dlt, axis=-1, keepdims=True) * (1.0 / HEAD_DIM)
            yn = dlt * lax.rsqrt(var + EPS) * g[:, 2 * HP + hb * LANES:2 * HP + (hb + 1) * LANES]
            x = rg_ref[:, sl]
            mx_ref[:, 2 * HP + hb * LANES:2 * HP + (hb + 1) * LANES] = (yn * (x * _sigmoid(x))).astype(BF16)

    rg_spec = pl.BlockSpec((TM, HP), lambda i: (i, OFF_RG // HP))
    return pl.pallas_call(
        body, name=name, grid=(t // TM,),
        in_specs=[_row_spec(HP)] * 4 + [rg_spec, _vec_spec(4 * HP)], out_specs=_row_spec(4 * HP),
        out_shape=jax.ShapeDtypeStruct((t, 4 * HP), BF16), compiler_params=_cp("parallel"))(oa, ob, oc, od, proj, g_pad)


def _post_bwd(dmx, oa, ob, oc, od, proj, g_pad, name):
    t = oa.shape[0]

    def body(dm_ref, oa_ref, ob_ref, oc_ref, od_ref, rg_ref, g_ref,
             doa_ref, dob_ref, doc_ref, dod_ref, dla_ref, dlb_ref, drg_ref, dg_ref):
        @pl.when(pl.program_id(0) == 0)
        def _():
            dg_ref[...] = jnp.zeros_like(dg_ref)

        g = g_ref[...]

        def group_bwd(dm, o, gg):
            r = lax.rsqrt(jnp.sum(o * o, axis=-1, keepdims=True) * (1.0 / GROUP) + EPS)
            oh = o * r
            dgp = jnp.sum(dm * oh, axis=0, keepdims=True)
            dyh = dm * gg
            do = r * (dyh - oh * (jnp.sum(dyh * oh, axis=-1, keepdims=True) * (1.0 / GROUP)))
            return do, dgp

        def delta_bc(do, o):
            prod = do * o
            lane = lax.broadcasted_iota(jnp.int32, (TM, LANES), 1)
            out = jnp.zeros((TM, LANES), F32)
            for hb in range(HEADS):
                out = jnp.where(lane == hb, jnp.sum(prod[:, hb * LANES:(hb + 1) * LANES], axis=-1, keepdims=True), out)
            return out

        oa = oa_ref[...]
        do_a, dga = group_bwd(dm_ref[:, 0:HP], oa, g[:, 0:HP])
        doa_ref[...] = do_a
        dla_ref[...] = delta_bc(do_a, oa)
        dg_ref[:, 0:HP] += dga
        ob = ob_ref[...]
        do_b, dgb = group_bwd(dm_ref[:, HP:2 * HP], ob, g[:, HP:2 * HP])
        dob_ref[...] = do_b
        dlb_ref[...] = delta_bc(do_b, ob)
        dg_ref[:, HP:2 * HP] += dgb
        do_d, dgd = group_bwd(dm_ref[:, 3 * HP:4 * HP], od_ref[...], g[:, 3 * HP:4 * HP])
        dod_ref[...] = do_d
        dg_ref[:, 3 * HP:4 * HP] += dgd
        real = lax.broadcasted_iota(jnp.int32, (TM, LANES), 1) < HEAD_DIM
        for hb in range(HEADS):
            sl = slice(hb * LANES, (hb + 1) * LANES)
            gsl = slice(2 * HP + hb * LANES, 2 * HP + (hb + 1) * LANES)
            o = oc_ref[:, sl]
            mu = jnp.sum(o, axis=-1, keepdims=True) * (1.0 / HEAD_DIM)
            dlt = jnp.where(real, o - mu, 0.0)
            var = jnp.sum(dlt * dlt, axis=-1, keepdims=True) * (1.0 / HEAD_DIM)
            rstd = lax.rsqrt(var + EPS)
            dhat = dlt * rstd
            gc = g[:, gsl]
            x = rg_ref[:, sl]
            sg = _sigmoid(x)
            dm = dm_ref[:, gsl]
            drg_ref[:, sl] = dm * (dhat * gc) * (sg * (1.0 + x * (1.0 - sg)))
            dyn = dm * (x * sg)
            dg_ref[:, gsl] += jnp.sum(dyn * dhat, axis=0, keepdims=True)
            ddh = dyn * gc
            m1 = jnp.sum(ddh, axis=-1, keepdims=True) * (1.0 / HEAD_DIM)
            m2 = jnp.sum(ddh * dhat, axis=-1, keepdims=True) * (1.0 / HEAD_DIM)
            doc_ref[:, sl] = jnp.where(real, rstd * (ddh - m1 - dhat * m2), 0.0)

    rg_spec = pl.BlockSpec((TM, HP), lambda i: (i, OFF_RG // HP))
    hp = _row_spec(HP)
    return pl.pallas_call(
        body, name=name, grid=(t // TM,),
        in_specs=[_row_spec(4 * HP), hp, hp, hp, hp, rg_spec, _vec_spec(4 * HP)],
        out_specs=[hp] * 4 + [_row_spec(LANES)] * 2 + [hp, _vec_spec(4 * HP)],
        out_shape=[jax.ShapeDtypeStruct((t, HP), F32)] * 4 + [jax.ShapeDtypeStruct((t, LANES), F32)] * 2
        + [jax.ShapeDtypeStruct((t, HP), F32), jax.ShapeDtypeStruct((1, 4 * HP), F32)],
        compiler_params=_cp("arbitrary"))(dmx, oa, ob, oc, od, proj, g_pad)


def _mesh_pos():
    return lax.axis_index("x"), lax.axis_index("y"), lax.axis_index("c")


def _peer(pos, k):
    x, y, c = pos
    px = 1 - x if (k >> 2) & 1 else x
    py = 1 - y if (k >> 1) & 1 else y
    pc = 1 - c if k & 1 else c
    return (px, py, pc), 4 * px + 2 * py + pc


def _exchange(arrs, gather, name):
    n = len(arrs)

    def body(*refs):
        ins, outs = refs[:n], refs[n:2 * n]
        send_sems, recv_sems, loc_sems = refs[2 * n:]
        pos = _mesh_pos()
        me = 4 * pos[0] + 2 * pos[1] + pos[2]
        local = []
        for a in range(n):
            src = ins[a] if gather else ins[a].at[me]
            cp = pltpu.make_async_copy(src, outs[a].at[me], loc_sems.at[a])
            cp.start()
            local.append(cp)
        sends, recvs = [], []
        for k in range(1, N_DEV):
            peer, pid = _peer(pos, k)
            for a in range(n):
                s = a * (N_DEV - 1) + k - 1
                src = ins[a] if gather else ins[a].at[pid]
                cp = pltpu.make_async_remote_copy(
                    src_ref=src, dst_ref=outs[a].at[me], send_sem=send_sems.at[s], recv_sem=recv_sems.at[s],
                    device_id=peer, device_id_type=pl.DeviceIdType.MESH)
                cp.start()
                sends.append(cp)
                recvs.append(pltpu.make_async_remote_copy(
                    src_ref=src, dst_ref=outs[a].at[pid], send_sem=send_sems.at[s], recv_sem=recv_sems.at[s],
                    device_id=peer, device_id_type=pl.DeviceIdType.MESH))
        for cp in recvs:
            cp.wait_recv()
        for cp in sends:
            cp.wait_send()
        for cp in local:
            cp.wait()

    any_spec = pl.BlockSpec(memory_space=pl.ANY)
    out_shape = [jax.ShapeDtypeStruct((N_DEV,) + tuple(a.shape) if gather else tuple(a.shape), a.dtype) for a in arrs]
    return pl.pallas_call(
        body, name=name, in_specs=[any_spec] * n, out_specs=[any_spec] * n, out_shape=out_shape,
        scratch_shapes=[pltpu.SemaphoreType.DMA((n * (N_DEV - 1),)), pltpu.SemaphoreType.DMA((n * (N_DEV - 1),)),
                        pltpu.SemaphoreType.DMA((n,))],
        compiler_params=pltpu.CompilerParams(has_side_effects=True))(*arrs)


def _device_index():
    x, y, c = _mesh_pos()
    return 4 * x + 2 * y + c


def _landing(srcs, gather):
    me = _device_index()
    lands = []
    for a in srcs:
        own = a[None] if gather else lax.dynamic_slice_in_dim(a, me, 1, axis=0)
        shape = (N_DEV,) + tuple(a.shape) if gather else tuple(a.shape)
        lands.append(lax.dynamic_update_slice_in_dim(lax.empty(shape, a.dtype), own, me, axis=0))
    return lands


def _exchange_copies(ins, lands, send_sems, recv_sems, gather):
    pos = _mesh_pos()
    me = 4 * pos[0] + 2 * pos[1] + pos[2]
    sends, recvs = [], []
    for k in range(1, N_DEV):
        peer, pid = _peer(pos, k)
        for a in range(len(ins)):
            s = a * (N_DEV - 1) + k - 1
            src = ins[a] if gather else ins[a].at[pid]
            sends.append(pltpu.make_async_remote_copy(
                src_ref=src, dst_ref=lands[a].at[me], send_sem=send_sems.at[s], recv_sem=recv_sems.at[s],
                device_id=peer, device_id_type=pl.DeviceIdType.MESH))
            recvs.append(pltpu.make_async_remote_copy(
                src_ref=src, dst_ref=lands[a].at[pid], send_sem=send_sems.at[s], recv_sem=recv_sems.at[s],
                device_id=peer, device_id_type=pl.DeviceIdType.MESH))
    return sends, recvs


def _exchange_start(srcs, gather, name):
    n = len(srcs)
    lands = _landing(srcs, gather)
    nsem = n * (N_DEV - 1)

    def body(*refs):
        ins, lnd = refs[:n], refs[n:2 * n]
        send_sems, recv_sems = refs[2 * n], refs[2 * n + 1]
        token = refs[-1]
        sends, _ = _exchange_copies(ins, lnd, send_sems, recv_sems, gather)
        for cp in sends:
            cp.start()
        token[...] = jnp.zeros_like(token)

    hbm = pl.BlockSpec(memory_space=pltpu.HBM)
    sem = pl.BlockSpec(memory_space=pltpu.SEMAPHORE)
    bufs = list(srcs) + lands
    out_shape = ([pltpu.SemaphoreType.DMA((nsem,)), pltpu.SemaphoreType.DMA((nsem,))]
                 + [pltpu.HBM(b.shape, b.dtype) for b in bufs] + [jax.ShapeDtypeStruct((8, LANES), F32)])
    outs = pl.pallas_call(
        body, name=name, in_specs=[hbm] * (2 * n),
        out_specs=[sem, sem] + [hbm] * (2 * n) + [pl.BlockSpec(memory_space=pltpu.VMEM)], out_shape=out_shape,
        input_output_aliases={i: 2 + i for i in range(2 * n)},
        compiler_params=pltpu.CompilerParams(has_side_effects=pltpu.SideEffectType.DATAFLOW_SIDE_EFFECTING),
    )(*[pltpu.with_memory_space_constraint(b, pltpu.HBM) for b in bufs])
    return (outs[0], outs[1], outs[2:2 + n], outs[2 + n:2 + 2 * n]), outs[-1]


def _exchange_wait(state, after, gather, name):
    send_sems, recv_sems, srcs, lands = state
    n = len(srcs)

    def body(*refs):
        ins, lnd = refs[:n], refs[n:2 * n]
        s_sems, r_sems = refs[2 * n], refs[2 * n + 1]
        sends, recvs = _exchange_copies(ins, lnd, s_sems, r_sems, gather)
        for cp in sends:
            cp.wait_send()
        for cp in recvs:
            cp.wait_recv()

    hbm = pl.BlockSpec(memory_space=pltpu.HBM)
    sem = pl.BlockSpec(memory_space=pltpu.SEMAPHORE)
    bufs = list(srcs) + list(lands)
    outs = pl.pallas_call(
        body, name=name, in_specs=[hbm] * (2 * n) + [sem, sem, pl.BlockSpec(memory_space=pl.ANY)],
        out_specs=[hbm] * (2 * n), out_shape=[pltpu.HBM(b.shape, b.dtype) for b in bufs],
        input_output_aliases={i: i for i in range(2 * n)},
        compiler_params=pltpu.CompilerParams(has_side_effects=pltpu.SideEffectType.DATAFLOW_SIDE_EFFECTING),
    )(*bufs, send_sems, recv_sems, after)
    return outs[n:]


def _adam_vals(w, g, m, v):
    m = ADAM_B1 * m + (1.0 - ADAM_B1) * g
    v = ADAM_B2 * v + (1.0 - ADAM_B2) * (g * g)
    m_hat = m / ADAM_C1
    v_hat = v / ADAM_C2
    delta = -ADAM_LR * (m_hat / (jnp.sqrt(v_hat) + ADAM_EPS) + ADAM_WD * w)
    return delta, m, v


def _small_allreduce_adam(part, w, m, v, name):
    rows = part.shape[0]

    def body(p_ref, w_ref, m_ref, v_ref, g_ref, d_ref, nm_ref, nv_ref, gath, send_sems, recv_sems):
        pos = _mesh_pos()
        me = 4 * pos[0] + 2 * pos[1] + pos[2]
        gath[me] = p_ref[...]
        sends, recvs = [], []
        for k in range(1, N_DEV):
            peer, pid = _peer(pos, k)
            cp = pltpu.make_async_remote_copy(
                src_ref=p_ref, dst_ref=gath.at[me], send_sem=send_sems.at[k - 1], recv_sem=recv_sems.at[k - 1],
                device_id=peer, device_id_type=pl.DeviceIdType.MESH)
            cp.start()
            sends.append(cp)
            recvs.append(pltpu.make_async_remote_copy(
                src_ref=p_ref, dst_ref=gath.at[pid], send_sem=send_sems.at[k - 1], recv_sem=recv_sems.at[k - 1],
                device_id=peer, device_id_type=pl.DeviceIdType.MESH))
        for cp in recvs:
            cp.wait_recv()
        for cp in sends:
            cp.wait_send()
        g = gath[0]
        for p in range(1, N_DEV):
            g = g + gath[p]
        g_ref[...] = g
        d, nm, nv = _adam_vals(w_ref[...], g, m_ref[...], v_ref[...])
        d_ref[...] = d
        nm_ref[...] = nm
        nv_ref[...] = nv

    vm = pl.BlockSpec(memory_space=pltpu.VMEM)
    sds = jax.ShapeDtypeStruct((rows, LANES), F32)
    return pl.pallas_call(
        body, name=name, in_specs=[vm] * 4, out_specs=[vm] * 4, out_shape=[sds] * 4,
        scratch_shapes=[pltpu.VMEM((N_DEV, rows, LANES), F32), pltpu.SemaphoreType.DMA((N_DEV - 1,)),
                        pltpu.SemaphoreType.DMA((N_DEV - 1,))],
        compiler_params=pltpu.CompilerParams(has_side_effects=True))(part, w, m, v)


def _reduce_adam(recv, w, m, v, name):
    shape = w.shape
    c = shape[-1]
    r = int(np.prod(shape[:-1]))
    recv2, w2, m2, v2 = recv.reshape(N_DEV, r, c), w.reshape(r, c), m.reshape(r, c), v.reshape(r, c)
    tr = r
    while tr * c * 4 > (1 << 20) and tr % 16 == 0:
        tr //= 2

    def body(r_ref, w_ref, m_ref, v_ref, g_ref, d_ref, nm_ref, nv_ref):
        g = r_ref[0].astype(F32)
        for p in range(1, N_DEV):
            g = g + r_ref[p].astype(F32)
        g_ref[...] = g
        d, nm, nv = _adam_vals(w_ref[...], g, m_ref[...], v_ref[...])
        d_ref[...] = d
        nm_ref[...] = nm
        nv_ref[...] = nv

    spec = pl.BlockSpec((tr, c), lambda i: (i, 0))
    sds = jax.ShapeDtypeStruct((r, c), F32)
    outs = pl.pallas_call(
        body, name=name, grid=(r // tr,),
        in_specs=[pl.BlockSpec((N_DEV, tr, c), lambda i: (0, i, 0)), spec, spec, spec],
        out_specs=[spec] * 4, out_shape=[sds] * 4, compiler_params=_cp("parallel"))(recv2, w2, m2, v2)
    return [o.reshape(shape) for o in outs]


def _pad_heads(w, real=HEAD_DIM):
    lead = w.shape[:-1]
    w = w.reshape(lead + (HEADS, real))
    w = jnp.pad(w, [(0, 0)] * len(lead) + [(0, 0), (0, LANES - real)])
    return w.reshape(lead + (HP,))


def _unpad_heads(w, real=HEAD_DIM):
    lead = w.shape[:-1]
    return w.reshape(lead + (HEADS, LANES))[..., :real].reshape(lead + (HEADS * real,))


_IN_SEGS = (("fq", 0, 256), ("fk", 256, 512), ("fv", 512, 768), ("ff", 768, 772), ("cq", 772, 1028),
            ("ckv", 1028, 1156), ("kr", 1156, 1188), ("rq", 1188, 1444), ("rk", 1444, 1700), ("rv", 1700, 1956),
            ("rg", 1956, 2212), ("sq", 2212, 2468), ("sk", 2468, 2724), ("sv", 2724, 2980))


def _pad_w_in(w):
    s = {n: w[:, a:b] for n, a, b in _IN_SEGS}
    rows = w.shape[0]
    z = lambda n: jnp.zeros((rows, n), w.dtype)
    parts = [_pad_heads(s[n]) for n in ("fq", "fk", "fv", "rq", "rk", "rv", "sq", "sk", "sv", "rg")]
    parts += [s["cq"], s["ckv"], z(HEAD_DIM), s["kr"], z(LANES - HEAD_DIM - ROPE_MLA), s["ff"], z(LANES - HEADS),
              z(NP_IN - OFF_FF - LANES)]
    return jnp.concatenate(parts, axis=1)


def _unpad_w_in(wp):
    def heads(off):
        return _unpad_heads(wp[:, off:off + HP])

    parts = [heads(OFF_FOX), heads(OFF_FOX + HP), heads(OFF_FOX + 2 * HP), wp[:, OFF_FF:OFF_FF + HEADS],
             wp[:, OFF_CQ:OFF_CQ + Q_RANK], wp[:, OFF_CKV:OFF_CKV + KV_RANK],
             wp[:, OFF_KR + HEAD_DIM:OFF_KR + HEAD_DIM + ROPE_MLA],
             heads(OFF_RET), heads(OFF_RET + HP), heads(OFF_RET + 2 * HP), heads(OFF_RG),
             heads(OFF_SB), heads(OFF_SB + HP), heads(OFF_SB + 2 * HP)]
    return jnp.concatenate(parts, axis=1)


def _pad_w_kv(w):
    w4 = w.reshape(KV_RANK, HEADS, 2 * HEAD_DIM)
    k = w4[:, :, :HEAD_DIM].reshape(KV_RANK, GROUP)
    v = w4[:, :, HEAD_DIM:].reshape(KV_RANK, GROUP)
    return jnp.concatenate([_pad_heads(k), _pad_heads(v)], axis=1)


def _unpad_w_kv(wp):
    k = _unpad_heads(wp[:, :HP]).reshape(KV_RANK, HEADS, HEAD_DIM)
    v = _unpad_heads(wp[:, HP:]).reshape(KV_RANK, HEADS, HEAD_DIM)
    return jnp.concatenate([k, v], axis=-1).reshape(KV_RANK, HEADS * 2 * HEAD_DIM)


def _pad_rows_out(w):
    w = w.reshape(4 * HEADS, HEAD_DIM, D_MODEL)
    w = jnp.pad(w, ((0, 0), (0, LANES - HEAD_DIM), (0, 0)))
    return w.reshape(4 * HP, D_MODEL)


def _unpad_rows_out(wp):
    return wp.reshape(4 * HEADS, LANES, D_MODEL)[:, :HEAD_DIM, :].reshape(D_MODEL, D_MODEL)


def _pad_gain_out(g):
    g = jnp.pad(g.reshape(4 * HEADS, HEAD_DIM), ((0, 0), (0, LANES - HEAD_DIM)))
    return g.reshape(1, 4 * HP)


def _unpad_gain_out(gp):
    return gp.reshape(4 * HEADS, LANES)[:, :HEAD_DIM].reshape(D_MODEL)


_SMALL = (("g_mix_pre", 1024), ("g_mix_post", 1024), ("g_ffn_pre", 1024), ("g_ffn_post", 1024), ("g_mix_out", 1024),
          ("g_q_lora", 256), ("g_kv_lora", 128), ("b_forget", 4))


def _pack_small(vals):
    parts = []
    for name, n in _SMALL:
        a = vals[name].astype(F32)
        if n < LANES:
            a = jnp.pad(a, ((0, 0), (0, LANES - n)))
        parts.append(a)
    return jnp.concatenate(parts, axis=1).reshape(DEPTH * SMALL_ROWS, LANES)


def _unpack_small(packed):
    flat = packed.reshape(DEPTH, SMALL_ROWS * LANES)
    out, off = {}, 0
    for name, n in _SMALL:
        out[name] = flat[:, off:off + n]
        off += max(n, LANES)
    return out


def kernel(x, positions, g_mix_pre, w_in, b_forget, g_q_lora, w_q_up, g_kv_lora, w_kv_up, g_mix_out, w_out, g_mix_post, g_ffn_pre, w_ffn_up, w_ffn_down, g_ffn_post, loss_target, m_g_mix_pre, m_w_in, m_b_forget, m_g_q_lora, m_w_q_up, m_g_kv_lora, m_w_kv_up, m_g_mix_out, m_w_out, m_g_mix_post, m_g_ffn_pre, m_w_ffn_up, m_w_ffn_down, m_g_ffn_post, v_g_mix_pre, v_w_in, v_b_forget, v_g_q_lora, v_w_q_up, v_g_kv_lora, v_w_kv_up, v_g_mix_out, v_w_out, v_g_mix_post, v_g_ffn_pre, v_w_ffn_up, v_w_ffn_down, v_g_ffn_post):
    t = x.shape[1]
    nq = t // BQ
    x0 = x[0]
    tgt = loss_target[0]
    pos = positions[0].astype(F32).reshape(t, 1)

    half_r, half_m = HEAD_DIM // 2, ROPE_MLA // 2
    invf_r = ROPE_BASE ** (-jnp.arange(half_r, dtype=F32) / half_r)
    invf_m = ROPE_BASE ** (-jnp.arange(half_m, dtype=F32) / half_m)
    invf = jnp.concatenate([invf_r, invf_r, invf_m, invf_m,
                            jnp.zeros((LANES - HEAD_DIM - ROPE_MLA,), F32)]).reshape(1, LANES)
    log_gamma = jnp.log1p(-jnp.power(2.0, -5.0 - jnp.arange(HEADS, dtype=F32)))
    lg_lanes = jnp.repeat(log_gamma, LANES).reshape(1, HP)
    lg_heads = jnp.broadcast_to(log_gamma[:, None, None], (HEADS, 1, LANES))

    big = [w_in, w_q_up, w_kv_up, w_out, w_ffn_up, w_ffn_down]
    bf = lambda w: w.astype(BF16)
    first = _exchange([bf(w_in[0]), bf(w_q_up), bf(w_kv_up)], True, "weights_gather_first")
    rest_state, rest_token = _exchange_start([bf(w_in[1]), bf(w_out), bf(w_ffn_up), bf(w_ffn_down)], True,
                                             "weights_gather_start")
    wq_g = first[1].transpose(1, 2, 0, 3).reshape(DEPTH, Q_RANK, 384)
    wkv_g = first[2].transpose(1, 2, 0, 3).reshape(DEPTH, KV_RANK, 512)

    row = lambda g: g.reshape(1, -1)
    layers = []
    for l in range(DEPTH):
        layers.append(dict(
            wq=_pad_heads(wq_g[l], 96), wkv=_pad_w_kv(wkv_g[l]),
            g_pre=row(g_mix_pre[l]), g_post=row(g_mix_post[l]), g_fpre=row(g_ffn_pre[l]), g_fpost=row(g_ffn_post[l]),
            g_out=_pad_gain_out(g_mix_out[l]), g_q=row(g_q_lora[l]), g_kv=row(g_kv_lora[l]),
            b_pad=jnp.pad(b_forget[l], (0, LANES - HEADS)).reshape(1, LANES)))
    layers[0]["win"] = _pad_w_in(first[0].reshape(D_MODEL, D_IN))

    saved = []
    xin = x0
    h = _rms_fwd(xin, layers[0]["g_pre"] + rest_token[0:1, 0:1], "rms_pre_0")
    loss_row = dx = None
    for l, p in enumerate(layers):
        s = dict(x=xin, h=h)
        proj = _mm(h, p["win"], name=f"in_proj_{l}")
        cum = _cumsum(_forget_lsf(proj, p["b_pad"], f"forget_lsf_{l}"), False, f"forget_cumsum_{l}")
        fox, ret, sb, mla, fox_t, sb_t, mla_t, ret_t = _prep_fwd(proj, cum, pos, invf, lg_lanes, p["g_q"],
                                                          p["g_kv"], p["wq"], p["wkv"], f"prep_fwd_{l}")
        oa, lse_a = _softmax_fwd(fox, fox_t, chunk_mask=False, scale=1.0, name=f"fox_fwd_{l}")
        ob, lse_b = _softmax_fwd(mla, mla_t, chunk_mask=True, scale=96.0 ** -0.5, name=f"mla_fwd_{l}")
        oc, ret_st = _ret_fwd(ret, ret_t, lg_heads, f"ret_fwd_{l}")
        od, sb_tot = _sb_fwd(sb, sb_t, f"sb_fwd_{l}")
        if l == 0:
            rest = _exchange_wait(rest_state, od, True, "weights_gather_wait")
            wout_g = rest[1].transpose(1, 0, 2, 3).reshape(DEPTH, D_MODEL, D_MODEL)
            wup_g = rest[2].transpose(1, 2, 0, 3).reshape(DEPTH, D_MODEL, D_FF)
            wdn_g = rest[3].transpose(1, 0, 2, 3).reshape(DEPTH, D_FF, D_MODEL)
            layers[1]["win"] = _pad_w_in(rest[0].reshape(D_MODEL, D_IN))
            for ll in range(DEPTH):
                layers[ll].update(wout=_pad_rows_out(wout_g[ll]), wup=wup_g[ll], wdn=wdn_g[ll])
        mixed = _post_fwd(oa, ob, oc, od, proj, p["g_out"], f"post_fwd_{l}")
        mix = _mm(mixed, p["wout"], name=f"out_proj_{l}", tk=2048)
        x1, h2 = _add_rms_fwd(xin, mix, p["g_post"], p["g_fpre"], f"mix_residual_{l}")
        a = _mm(h2, p["wup"], name=f"ffn_up_{l}")
        y = _mm(a, p["wdn"], name=f"ffn_down_{l}", a_fn=_relu2)
        s.update(proj=proj, fox=fox, ret=ret, sb=sb, mla=mla, fox_t=fox_t, sb_t=sb_t, mla_t=mla_t, ret_st=ret_st, oa=oa, ob=ob, oc=oc,
                 od=od, sb_tot=sb_tot, lse_a=lse_a, lse_b=lse_b, mixed=mixed, mix=mix, x1=x1, h2=h2, a=a, y=y)
        saved.append(s)
        if l + 1 < DEPTH:
            xin, h = _add_rms_fwd(x1, y, p["g_fpost"], layers[l + 1]["g_pre"], f"ffn_residual_{l}")
        else:
            loss_row, dx = _final_loss(x1, y, p["g_fpost"], tgt, "loss")

    small_g = {n: [None] * DEPTH for n, _ in _SMALL}
    big_g = [[None] * DEPTH for _ in range(6)]
    to_send = [
        lambda g: g.reshape(N_DEV, 1, D_MODEL // N_DEV, D_IN),
        lambda g: g.reshape(Q_RANK, N_DEV, 384 // N_DEV).transpose(1, 0, 2)[:, None],
        lambda g: g.reshape(KV_RANK, N_DEV, 512 // N_DEV).transpose(1, 0, 2)[:, None],
        lambda g: g.reshape(N_DEV, 1, D_MODEL // N_DEV, D_MODEL),
        lambda g: g.reshape(D_MODEL, N_DEV, D_FF // N_DEV).transpose(1, 0, 2)[:, None],
        lambda g: g.reshape(N_DEV, 1, D_FF // N_DEV, D_MODEL),
    ]
    send_of = lambda ks, l: [to_send[k](big_g[k][l]).astype(BF16) for k in ks]
    late_state = early_state = None
    order_token = jnp.zeros((1, 1), F32)
    for l in reversed(range(DEPTH)):
        p, s = layers[l], saved[l]
        dy, dg = _norm_bwd(dx, s["y"], p["g_fpost"] + order_token, None, BF16, f"ffn_post_bwd_{l}")
        small_g["g_ffn_post"][l] = dg
        da = _mm(dy, p["wdn"], name=f"ffn_down_dx_{l}", tb=True, out_dtype=BF16, epi=_drelu2, epi_in=s["a"])
        big_g[5][l] = _mm(s["a"], dy, name=f"ffn_down_dw_{l}", ta=True, a_fn=_relu2, tk=2048)
        big_g[4][l] = _mm(s["h2"], da, name=f"ffn_up_dw_{l}", ta=True, tk=2048)
        dh2 = _mm(da, p["wup"], name=f"ffn_up_dx_{l}", tb=True)
        dx1, dg = _norm_bwd(dh2, s["x1"], p["g_fpre"], dx, F32, f"ffn_pre_bwd_{l}")
        small_g["g_ffn_pre"][l] = dg
        dmix, dg = _norm_bwd(dx1, s["mix"], p["g_post"], None, BF16, f"mix_post_bwd_{l}")
        small_g["g_mix_post"][l] = dg
        dmixed = _mm(dmix, p["wout"], name=f"out_proj_dx_{l}", tb=True)
        big_g[3][l] = _unpad_rows_out(_mm(s["mixed"], dmix, name=f"out_proj_dw_{l}", ta=True, tk=2048))
        g_out = p["g_out"]
        if l == 0:
            early_state, early_token = _exchange_start(send_of((3, 4, 5), 0), False, "grads_layer0_early_start")
            g_out = g_out + early_token[0:1, 0:1]
        doa, dob, doc, dod, dla, dlb, drg, dgo = _post_bwd(dmixed, s["oa"], s["ob"], s["oc"], s["od"], s["proj"],
                                                           g_out, f"post_bwd_{l}")
        small_g["g_mix_out"][l] = _unpad_gain_out(dgo).reshape(1, D_MODEL)
        as_rows = lambda a: a[:, :HEADS].T.reshape(HEADS, t // BQA, 1, BQA)
        dfq, dfk, dfv, dcum_k, dcum_q = _softmax_bwd(s["fox"], s["fox_t"], doa, s["lse_a"], as_rows(dla), bias=True,
                                                     chunk_mask=False, scale=1.0, name=f"fox_bwd_{l}")
        dmq, dmk, dmv = _softmax_bwd(s["mla"], s["mla_t"], dob, s["lse_b"], as_rows(dlb), bias=False, chunk_mask=True,
                                     scale=96.0 ** -0.5, name=f"mla_bwd_{l}")
        drq, drk, drv = _ret_bwd(s["ret"], s["ret_st"], lg_heads, doc, f"ret_bwd_{l}")
        dsq, dsk, dsv = _sb_bwd(s["sb"], s["sb_t"], dod, s["sb_tot"], f"sb_bwd_{l}")
        dcum_q = jnp.pad(dcum_q.reshape(HEADS, t).T, ((0, 0), (0, LANES - HEADS)))
        dlsf = _cumsum(dcum_q, True, f"forget_cumsum_bwd_{l}", partials=dcum_k)
        dproj, dwq, dwkv, dgq, dgkv, dbf = _prep_bwd(
            (dfq, dfk, dfv), (drq, drk, drv), (dsq, dsk, dsv), (dmq, dmk, dmv), drg, dlsf, s["proj"], pos, invf,
            lg_lanes, p["b_pad"], p["g_q"], p["g_kv"], p["wq"], p["wkv"], f"prep_bwd_{l}")
        small_g["g_q_lora"][l] = dgq
        small_g["g_kv_lora"][l] = dgkv
        small_g["b_forget"][l] = dbf[:, :HEADS]
        big_g[1][l] = _unpad_heads(dwq, 96)
        big_g[2][l] = _unpad_w_kv(dwkv)
        big_g[0][l] = _unpad_w_in(_mm(s["h"], dproj, name=f"in_proj_dw_{l}", ta=True, tk=2048))
        dh = _mm(dproj, p["win"], name=f"in_proj_dx_{l}", tb=True)
        dx, dg = _norm_bwd(dh, s["x"], p["g_pre"], dx1, F32, f"mix_pre_bwd_{l}")
        small_g["g_mix_pre"][l] = dg
        if l == DEPTH - 1:
            late_state, late_token = _exchange_start(send_of(range(6), l), False, "grads_layer1_start")
            order_token = late_token[0:1, 0:1]
    grad_x = dx.reshape(1, t, D_MODEL)

    last = _exchange(send_of((0, 1, 2), 0), False, "grads_layer0_rest")
    late = _exchange_wait(late_state, dx, False, "grads_layer1_wait")
    early = _exchange_wait(early_state, dx, False, "grads_layer0_early_wait")
    recv = [jnp.concatenate([(last[k] if k < 3 else early[k - 3]), late[k]], axis=1) for k in range(6)]
    ms = [m_w_in, m_w_q_up, m_w_kv_up, m_w_out, m_w_ffn_up, m_w_ffn_down]
    vs = [v_w_in, v_w_q_up, v_w_kv_up, v_w_out, v_w_ffn_up, v_w_ffn_down]
    names = ["w_in", "w_q_up", "w_kv_up", "w_out", "w_ffn_up", "w_ffn_down"]
    res = {}
    for k, n in enumerate(names):
        res[n] = _reduce_adam(recv[k], big[k], ms[k], vs[k], f"adamw_{n}")

    small_w = dict(g_mix_pre=g_mix_pre, g_mix_post=g_mix_post, g_ffn_pre=g_ffn_pre, g_ffn_post=g_ffn_post,
                   g_mix_out=g_mix_out, g_q_lora=g_q_lora, g_kv_lora=g_kv_lora, b_forget=b_forget)
    small_m = dict(g_mix_pre=m_g_mix_pre, g_mix_post=m_g_mix_post, g_ffn_pre=m_g_ffn_pre, g_ffn_post=m_g_ffn_post,
                   g_mix_out=m_g_mix_out, g_q_lora=m_g_q_lora, g_kv_lora=m_g_kv_lora, b_forget=m_b_forget)
    small_v = dict(g_mix_pre=v_g_mix_pre, g_mix_post=v_g_mix_post, g_ffn_pre=v_g_ffn_pre, g_ffn_post=v_g_ffn_post,
                   g_mix_out=v_g_mix_out, g_q_lora=v_g_q_lora, g_kv_lora=v_g_kv_lora, b_forget=v_b_forget)
    n_small = DEPTH * SMALL_ROWS
    extra = lambda a: jnp.concatenate([a, jnp.zeros((8, LANES), F32)], axis=0)
    part = jnp.concatenate([_pack_small({n: jnp.concatenate(small_g[n], axis=0) for n, _ in _SMALL}),
                            jnp.broadcast_to(loss_row, (8, LANES))], axis=0)
    sres = _small_allreduce_adam(part, extra(_pack_small(small_w)), extra(_pack_small(small_m)),
                                 extra(_pack_small(small_v)), "small_allreduce_adamw")
    loss = sres[0][n_small, 0]
    sg, sd, sm, sv = [_unpack_small(a[:n_small]) for a in sres]
    for n, _ in _SMALL:
        res[n] = [sg[n], sd[n], sm[n], sv[n]]

    order = ["g_mix_pre", "w_in", "b_forget", "g_q_lora", "w_q_up", "g_kv_lora", "w_kv_up", "g_mix_out", "w_out",
             "g_mix_post", "g_ffn_pre", "w_ffn_up", "w_ffn_down", "g_ffn_post"]
    outs = [loss, grad_x]
    for idx in range(4):
        outs += [res[n][idx] for n in order]
    return tuple(outs)
```

```python
import functools
import math

import numpy as np
import jax
import jax.numpy as jnp
from jax import lax
from jax.experimental import pallas as pl
from jax.experimental.pallas import tpu as pltpu

F32 = jnp.float32
BF16 = jnp.bfloat16

D_MODEL = 1024
DEPTH = 2
N_DEV = 8
GROUP = 256
HEADS = 4
HEAD_DIM = 64
LANES = 128
HP = HEADS * LANES
QKV = 3 * HP
Q_RANK = 256
KV_RANK = 128
ROPE_MLA = 32
D_FF = 4096
D_IN = 2980
CHUNK_SHIFT = 6
EPS = 1e-6
ROPE_BASE = 10000.0
NEG = -1e30

QKV_IN = 3 * GROUP
OFF_FOX, OFF_RET, OFF_SB = 0, QKV_IN, 2 * QKV_IN
OFF_RG = 3 * QKV_IN
OFF_CQ = OFF_RG + GROUP
OFF_CKV = OFF_CQ + Q_RANK
OFF_KR = OFF_CKV + LANES
OFF_FF = OFF_KR + LANES
NP_IN = 3328

BQ = 256
TKS = 128
TM = 256
VMEM_LIMIT = 48 * 1024 * 1024

ADAM_LR, ADAM_B1, ADAM_B2, ADAM_EPS, ADAM_WD, ADAM_STEP = 0.001, 0.9, 0.999, 1e-08, 0.01, 10
ADAM_C1 = 1.0 - ADAM_B1 ** ADAM_STEP
ADAM_C2 = 1.0 - ADAM_B2 ** ADAM_STEP

SMALL_ROWS = 44

NT = (((1,), (1,)), ((), ()))
TN = (((0,), (0,)), ((), ()))


def _cp(*sem):
    return pltpu.CompilerParams(dimension_semantics=sem if sem else None, vmem_limit_bytes=VMEM_LIMIT)


def _bdot(a, b, dn=None):
    if dn is None:
        return jnp.dot(a, b, preferred_element_type=F32)
    return lax.dot_general(a, b, dn, preferred_element_type=F32)


def _split2(x):
    hi = x.astype(BF16)
    lo = (x - hi.astype(F32)).astype(BF16)
    return hi, lo


def _mm(a, b, *, name, ta=False, tb=False, out_dtype=F32, a_fn=None, epi=None, epi_in=None,
        tm=1024, tn=1024, tk=1024):
    m, k = (a.shape[1], a.shape[0]) if ta else a.shape
    n = b.shape[0] if tb else b.shape[1]
    tm, tn, tk = min(tm, m), min(tn, n), min(tk, k)
    assert m % tm == 0 and n % tn == 0 and k % tk == 0, (name, m, n, k)
    nk = k // tk
    dn = (((0 if ta else 1,), (1 if tb else 0,)), ((), ()))

    def body(*refs):
        if epi is None:
            a_ref, b_ref, o_ref = refs[:3]
            e_ref = None
            rest = refs[3:]
        else:
            a_ref, b_ref, e_ref, o_ref = refs[:4]
            rest = refs[4:]
        av = a_ref[...]
        if a_fn is not None:
            av = a_fn(av)
        part = lax.dot_general(av.astype(BF16), b_ref[...].astype(BF16), dn, preferred_element_type=F32)

        def finish(r):
            if epi is not None:
                r = epi(r, e_ref[...])
            o_ref[...] = r.astype(out_dtype)

        if nk == 1:
            finish(part)
        else:
            acc_ref = rest[0]
            kk = pl.program_id(2)

            @pl.when(kk == 0)
            def _():
                acc_ref[...] = part

            @pl.when(kk > 0)
            def _():
                acc_ref[...] += part

            @pl.when(kk == nk - 1)
            def _():
                finish(acc_ref[...])

    a_spec = pl.BlockSpec((tk, tm), lambda i, j, kk: (kk, i)) if ta else pl.BlockSpec((tm, tk), lambda i, j, kk: (i, kk))
    b_spec = pl.BlockSpec((tn, tk), lambda i, j, kk: (j, kk)) if tb else pl.BlockSpec((tk, tn), lambda i, j, kk: (kk, j))
    o_spec = pl.BlockSpec((tm, tn), lambda i, j, kk: (i, j))
    in_specs = [a_spec, b_spec]
    args = [a, b]
    if epi is not None:
        in_specs.append(o_spec)
        args.append(epi_in)
    return pl.pallas_call(
        body, name=name, grid=(m // tm, n // tn, nk),
        in_specs=in_specs, out_specs=o_spec,
        out_shape=jax.ShapeDtypeStruct((m, n), out_dtype),
        scratch_shapes=[pltpu.VMEM((tm, tn), F32)] if nk > 1 else [],
        compiler_params=_cp("parallel", "parallel", "arbitrary"),
    )(*args)


def _relu2(v):
    r = jnp.maximum(v, 0.0)
    return r * r


def _drelu2(du, av):
    return du * (2.0 * jnp.maximum(av, 0.0))


def _rms(v, g):
    r = lax.rsqrt(jnp.mean(v * v, axis=-1, keepdims=True) + EPS)
    return v * r * g


def _row_spec(w):
    return pl.BlockSpec((TM, w), lambda i: (i, 0))


def _vec_spec(w):
    return pl.BlockSpec((1, w), lambda i: (0, 0))


def _rms_fwd(x, g, name):
    t, d = x.shape

    def body(x_ref, g_ref, h_ref):
        h_ref[...] = _rms(x_ref[...], g_ref[...]).astype(BF16)

    return pl.pallas_call(
        body, name=name, grid=(t // TM,), in_specs=[_row_spec(d), _vec_spec(d)], out_specs=_row_spec(d),
        out_shape=jax.ShapeDtypeStruct((t, d), BF16), compiler_params=_cp("parallel"))(x, g)


def _add_rms_fwd(x, y, g1, g2, name):
    t, d = x.shape

    def body(x_ref, y_ref, g1_ref, g2_ref, xn_ref, h_ref):
        xn = x_ref[...] + _rms(y_ref[...], g1_ref[...])
        xn_ref[...] = xn
        h_ref[...] = _rms(xn, g2_ref[...]).astype(BF16)

    return pl.pallas_call(
        body, name=name, grid=(t // TM,),
        in_specs=[_row_spec(d), _row_spec(d), _vec_spec(d), _vec_spec(d)],
        out_specs=[_row_spec(d), _row_spec(d)],
        out_shape=[jax.ShapeDtypeStruct((t, d), F32), jax.ShapeDtypeStruct((t, d), BF16)],
        compiler_params=_cp("parallel"))(x, y, g1, g2)


def _final_loss(x, y, g, tgt, name):
    t, d = x.shape

    def body(x_ref, y_ref, g_ref, t_ref, l_ref, dx_ref):
        @pl.when(pl.program_id(0) == 0)
        def _():
            l_ref[...] = jnp.zeros_like(l_ref)

        err = x_ref[...] + _rms(y_ref[...], g_ref[...]) - t_ref[...]
        dx_ref[...] = err * (1.0 / d)
        l_ref[...] += jnp.sum(jnp.sum(err * err, axis=1, keepdims=True), axis=0, keepdims=True) * (0.5 / d)

    return pl.pallas_call(
        body, name=name, grid=(t // TM,),
        in_specs=[_row_spec(d), _row_spec(d), _vec_spec(d), _row_spec(d)],
        out_specs=[pl.BlockSpec((1, LANES), lambda i: (0, 0)), _row_spec(d)],
        out_shape=[jax.ShapeDtypeStruct((1, LANES), F32), jax.ShapeDtypeStruct((t, d), F32)],
        compiler_params=_cp("arbitrary"))(x, y, g, tgt)


def _rms_bwd_vals(dn, v, g):
    w = v.shape[-1]
    r = lax.rsqrt(jnp.mean(v * v, axis=-1, keepdims=True) + EPS)
    vh = v * r
    dgp = jnp.sum(dn * vh, axis=0, keepdims=True)
    dvh = dn * g
    dv = r * (dvh - vh * (jnp.sum(dvh * vh, axis=-1, keepdims=True) * (1.0 / w)))
    return dv, dgp


def _norm_bwd(dn, v, g, resid, out_dtype, name):
    t, d = v.shape
    has_res = resid is not None

    def body(*refs):
        if has_res:
            dn_ref, v_ref, g_ref, r_ref, dv_ref, dg_ref = refs
        else:
            dn_ref, v_ref, g_ref, dv_ref, dg_ref = refs

        @pl.when(pl.program_id(0) == 0)
        def _():
            dg_ref[...] = jnp.zeros_like(dg_ref)

        dv, dgp = _rms_bwd_vals(dn_ref[...].astype(F32), v_ref[...], g_ref[...])
        if has_res:
            dv = dv + r_ref[...]
        dv_ref[...] = dv.astype(out_dtype)
        dg_ref[...] += dgp

    in_specs = [_row_spec(d), _row_spec(d), _vec_spec(d)] + ([_row_spec(d)] if has_res else [])
    args = [dn, v, g] + ([resid] if has_res else [])
    return pl.pallas_call(
        body, name=name, grid=(t // TM,), in_specs=in_specs,
        out_specs=[_row_spec(d), _vec_spec(d)],
        out_shape=[jax.ShapeDtypeStruct((t, d), out_dtype), jax.ShapeDtypeStruct((1, d), F32)],
        compiler_params=_cp("arbitrary"))(*args)


def _rope_trig(pos, invf):
    ang = pos * invf
    return jnp.cos(ang), jnp.sin(ang)


def _rope_tables(trig, w, lo, half):
    c, s = trig
    lane = lax.broadcasted_iota(jnp.int32, c.shape, 1)
    active = (lane >= lo) & (lane < lo + 2 * half)
    cos = jnp.concatenate([jnp.where(active, c, 1.0)] * (w // LANES), axis=1)
    sin = jnp.concatenate([jnp.where(active, s, 0.0)] * (w // LANES), axis=1)
    lanes = lax.broadcasted_iota(jnp.int32, (c.shape[0], w), 1) & (LANES - 1)
    first = (lanes >= lo) & (lanes < lo + half)
    second = (lanes >= lo + half) & (lanes < lo + 2 * half)
    return cos, sin, first, second


def _rope_apply(v, cos, sin, first, second, half, sign):
    w = v.shape[-1]
    up = pltpu.roll(v, w - half, 1)
    dn = pltpu.roll(v, half, 1)
    rot = jnp.where(first, -up, jnp.where(second, dn, 0.0))
    return v * cos + rot * (sin * sign)


def _forget_lsf(proj, b_pad, name):
    t = proj.shape[0]

    def body(ff_ref, b_ref, o_ref):
        f = ff_ref[...] + b_ref[...]
        o_ref[...] = -(jnp.maximum(-f, 0.0) + jnp.log(1.0 + jnp.exp(-jnp.abs(f))))

    return pl.pallas_call(
        body, name=name, grid=(t // TM,),
        in_specs=[pl.BlockSpec((TM, LANES), lambda i: (i, OFF_FF // LANES)), _vec_spec(LANES)],
        out_specs=_row_spec(LANES), out_shape=jax.ShapeDtypeStruct((t, LANES), F32),
        compiler_params=_cp("parallel"))(proj, b_pad)


def _split3(c):
    hi = c.astype(BF16).astype(F32)
    mid = (c - hi).astype(BF16).astype(F32)
    return hi, mid, (c - hi) - mid


def _spread_heads(x):
    low = lax.broadcasted_iota(jnp.int32, (x.shape[0], LANES), 1) < HEAD_DIM
    out = []
    for c in range(x.shape[1] // LANES):
        blk = x[:, c * LANES:(c + 1) * LANES]
        out.append(jnp.where(low, blk, 0.0))
        out.append(jnp.where(low, pltpu.roll(blk, HEAD_DIM, 1), 0.0))
    return jnp.concatenate(out, axis=1)


def _gather_heads(y):
    low = lax.broadcasted_iota(jnp.int32, (y.shape[0], LANES), 1) < HEAD_DIM
    out = []
    for c in range(y.shape[1] // (2 * LANES)):
        a = y[:, 2 * c * LANES:(2 * c + 1) * LANES]
        b = y[:, (2 * c + 1) * LANES:(2 * c + 2) * LANES]
        out.append(jnp.where(low, a, pltpu.roll(b, HEAD_DIM, 1)))
    return jnp.concatenate(out, axis=1)


def _transposed_tiles(dst, row0, blocks, width):
    for b, blk in enumerate(blocks):
        bt = blk.T.astype(BF16)
        rows = bt.shape[0]
        for w in range(TM // width):
            dst[w, row0 + b * rows:row0 + (b + 1) * rows, :] = bt[:, w * width:(w + 1) * width]


def _prep_fwd(proj, cum, pos, invf, lg_lanes, g_q, g_kv, wq_pad, wkv_pad, name):
    t = proj.shape[0]

    def body(fox_ref, ret_ref, sb_ref, cq_ref, ckv_ref, kr_ref, cum_ref, pos_ref, invf_ref, lg_ref,
             gq_ref, gkv_ref, wq_ref, wkv_ref,
             ofox_ref, oret_ref, osb_ref, omla_ref, ofoxt_ref, osbt_ref, omlat_ref, orett_ref):
        pos_v = pos_ref[...]
        foxv, retv, sbv = [[_spread_heads(ref[:, g * GROUP:(g + 1) * GROUP]) for g in range(3)]
                           for ref in (fox_ref, ret_ref, sb_ref)]
        osb_ref[:, 0:HP] = (sbv[0] * 0.125).astype(BF16)
        osb_ref[:, HP:2 * HP] = sbv[1].astype(BF16)
        osb_ref[:, 2 * HP:QKV] = sbv[2].astype(BF16)
        _transposed_tiles(osbt_ref, 0, [sbv[1], sbv[2]], TKS)
        lane = lax.broadcasted_iota(jnp.int32, (TM, LANES), 1)
        cumv = cum_ref[...]
        fq, fk = [], []
        for hb in range(HEADS):
            hi, mid, lo = _split3(cumv[:, hb:hb + 1])
            q = foxv[0][:, hb * LANES:(hb + 1) * LANES] * 0.125
            k = foxv[1][:, hb * LANES:(hb + 1) * LANES]
            ones_q = (lane >= HEAD_DIM) & (lane < HEAD_DIM + 3)
            ones_k = (lane >= HEAD_DIM + 3) & (lane < HEAD_DIM + 6)
            q = jnp.where(ones_q, 1.0, q)
            k = jnp.where(ones_k, 1.0, k)
            for n, part in enumerate((hi, mid, lo)):
                q = jnp.where(lane == HEAD_DIM + 3 + n, part, q)
                k = jnp.where(lane == HEAD_DIM + n, -part, k)
            fq.append(q)
            fk.append(k)
        fk = jnp.concatenate(fk, axis=1)
        ofox_ref[:, 0:HP] = jnp.concatenate(fq, axis=1).astype(BF16)
        ofox_ref[:, HP:2 * HP] = fk.astype(BF16)
        ofox_ref[:, 2 * HP:QKV] = foxv[2].astype(BF16)
        _transposed_tiles(ofoxt_ref, 0, [fk, foxv[2]], BQ)
        trig = _rope_trig(pos_v, invf_ref[...])
        cos, sin, first, second = _rope_tables(trig, HP, 0, HEAD_DIM // 2)
        nloc = lax.broadcasted_iota(jnp.int32, (TM, 1), 0).astype(F32)
        dec = lg_ref[...] * nloc
        rq = _rope_apply(retv[0], cos, sin, first, second, HEAD_DIM // 2, 1.0)
        rk = _rope_apply(retv[1], cos, sin, first, second, HEAD_DIM // 2, 1.0)
        oret_ref[:, 0:HP] = (rq * jnp.exp(dec)).astype(BF16)
        rk = rk * 0.125 * jnp.exp(-dec)
        oret_ref[:, HP:2 * HP] = rk.astype(BF16)
        _transposed_tiles(orett_ref, 0, [rk], BQ)
        oret_ref[:, 2 * HP:QKV] = retv[2].astype(BF16)
        cosm, sinm, firstm, secondm = _rope_tables(trig, HP, HEAD_DIM, ROPE_MLA // 2)
        cqn = _rms(cq_ref[...], gq_ref[...]).astype(BF16)
        qm = _bdot(cqn, wq_ref[...])
        omla_ref[:, 0:HP] = _rope_apply(qm, cosm, sinm, firstm, secondm, ROPE_MLA // 2, 1.0).astype(BF16)
        ckvn = _rms(ckv_ref[...], gkv_ref[...]).astype(BF16)
        kv = _bdot(ckvn, wkv_ref[...])
        krr = _rope_apply(kr_ref[...], cosm[:, 0:LANES], sinm[:, 0:LANES], firstm[:, 0:LANES],
                          secondm[:, 0:LANES], ROPE_MLA // 2, 1.0)
        mk = kv[:, 0:HP] + jnp.concatenate([krr] * HEADS, axis=1)
        omla_ref[:, HP:2 * HP] = mk.astype(BF16)
        omla_ref[:, 2 * HP:QKV] = kv[:, HP:2 * HP].astype(BF16)
        _transposed_tiles(omlat_ref, 0, [mk, kv[:, HP:2 * HP]], BQ)

    def seg(off, w):
        return pl.BlockSpec((TM, w), lambda i, o=off // w: (i, o))

    def full(shape):
        return pl.BlockSpec(shape, lambda i: (0,) * len(shape))

    def tiles(width):
        return pl.BlockSpec((TM // width, 2 * HP, width), lambda i: (i, 0, 0))

    in_specs = [seg(OFF_FOX, QKV_IN), seg(OFF_RET, QKV_IN), seg(OFF_SB, QKV_IN), seg(OFF_CQ, Q_RANK),
                seg(OFF_CKV, LANES), seg(OFF_KR, LANES), _row_spec(LANES), pl.BlockSpec((TM, 1), lambda i: (i, 0)),
                full((1, LANES)), full((1, HP)), full((1, Q_RANK)), full((1, KV_RANK)),
                full((Q_RANK, HP)), full((KV_RANK, 2 * HP))]
    out_specs = [_row_spec(QKV)] * 4 + [tiles(BQ), tiles(TKS), tiles(BQ),
                                        pl.BlockSpec((1, HP, BQ), lambda i: (i, 0, 0))]
    out_shape = [jax.ShapeDtypeStruct((t, QKV), BF16)] * 4 + [
        jax.ShapeDtypeStruct((t // BQ, 2 * HP, BQ), BF16), jax.ShapeDtypeStruct((t // TKS, 2 * HP, TKS), BF16),
        jax.ShapeDtypeStruct((t // BQ, 2 * HP, BQ), BF16), jax.ShapeDtypeStruct((t // BQ, HP, BQ), BF16)]
    return pl.pallas_call(
        body, name=name, grid=(t // TM,), in_specs=in_specs, out_specs=out_specs, out_shape=out_shape,
        compiler_params=_cp("parallel"))(proj, proj, proj, proj, proj, proj, cum, pos, invf, lg_lanes,
                                         g_q, g_kv, wq_pad, wkv_pad)


def _prep_bwd(dfox, dret, dsb, dmla, drg, dlsf, proj, pos, invf, lg_lanes, b_pad, g_q, g_kv,
              wq_pad, wkv_pad, name):
    t = proj.shape[0]

    def body(dfq, dfk, dfv, drq, drk, drv, dsq, dsk, dsv, dmq, dmk, dmv, drg_ref, dlsf_ref,
             cq_ref, ckv_ref, ff_ref, pos_ref, invf_ref, lg_ref, b_ref, gq_ref, gkv_ref, wq_ref, wkv_ref,
             dp_ref, dwq_ref, dwkv_ref, dgq_ref, dgkv_ref, dbf_ref):
        @pl.when(pl.program_id(0) == 0)
        def _():
            dwq_ref[...] = jnp.zeros_like(dwq_ref)
            dwkv_ref[...] = jnp.zeros_like(dwkv_ref)
            dgq_ref[...] = jnp.zeros_like(dgq_ref)
            dgkv_ref[...] = jnp.zeros_like(dgkv_ref)
            dbf_ref[...] = jnp.zeros_like(dbf_ref)

        pos_v = pos_ref[...]
        def put(off, val):
            dp_ref[:, off:off + GROUP] = _gather_heads(val).astype(BF16)

        for off, (dq, dk, dv) in ((OFF_FOX, (dfq, dfk, dfv)), (OFF_SB, (dsq, dsk, dsv))):
            put(off, dq[...] * 0.125)
            put(off + GROUP, dk[...])
            put(off + 2 * GROUP, dv[...])
        trig = _rope_trig(pos_v, invf_ref[...])
        cos, sin, first, second = _rope_tables(trig, HP, 0, HEAD_DIM // 2)
        nloc = lax.broadcasted_iota(jnp.int32, (TM, 1), 0).astype(F32)
        dec = lg_ref[...] * nloc
        dq = _rope_apply(drq[...] * jnp.exp(dec), cos, sin, first, second, HEAD_DIM // 2, -1.0)
        dk = _rope_apply(drk[...] * (0.125 * jnp.exp(-dec)), cos, sin, first, second, HEAD_DIM // 2, -1.0)
        put(OFF_RET, dq)
        put(OFF_RET + GROUP, dk)
        put(OFF_RET + 2 * GROUP, drv[...])
        put(OFF_RG, drg_ref[...])
        cosm, sinm, firstm, secondm = _rope_tables(trig, HP, HEAD_DIM, ROPE_MLA // 2)
        dql = _rope_apply(dmq[...], cosm, sinm, firstm, secondm, ROPE_MLA // 2, -1.0).astype(BF16)
        cq = cq_ref[...]
        cqn = _rms(cq, gq_ref[...]).astype(BF16)
        dwq_ref[...] += _bdot(cqn, dql, TN)
        dcqn = _bdot(dql, wq_ref[...], NT)
        dcq, dgq = _rms_bwd_vals(dcqn, cq, gq_ref[...])
        dgq_ref[...] += dgq
        dp_ref[:, OFF_CQ:OFF_CQ + Q_RANK] = dcq.astype(BF16)
        dkm = dmk[...]
        dkv = jnp.concatenate([dkm, dmv[...]], axis=1).astype(BF16)
        ckv = ckv_ref[...]
        ckvn = _rms(ckv, gkv_ref[...]).astype(BF16)
        dwkv_ref[...] += _bdot(ckvn, dkv, TN)
        dckvn = _bdot(dkv, wkv_ref[...], NT)
        dckv, dgkv = _rms_bwd_vals(dckvn, ckv, gkv_ref[...])
        dgkv_ref[...] += dgkv
        dp_ref[:, OFF_CKV:OFF_CKV + LANES] = dckv.astype(BF16)
        dkr = dkm[:, 0:LANES] + dkm[:, LANES:2 * LANES] + dkm[:, 2 * LANES:3 * LANES] + dkm[:, 3 * LANES:HP]
        act = firstm[:, 0:LANES] | secondm[:, 0:LANES]
        dkr = jnp.where(act, dkr, 0.0)
        dkr = _rope_apply(dkr, cosm[:, 0:LANES], sinm[:, 0:LANES], firstm[:, 0:LANES], secondm[:, 0:LANES],
                          ROPE_MLA // 2, -1.0)
        dp_ref[:, OFF_KR:OFF_KR + LANES] = dkr.astype(BF16)
        f = ff_ref[...] + b_ref[...]
        dff = dlsf_ref[...] / (1.0 + jnp.exp(f))
        dbf_ref[...] += jnp.sum(dff, axis=0, keepdims=True)
        dp_ref[:, OFF_FF:OFF_FF + LANES] = dff.astype(BF16)
        dp_ref[:, OFF_FF + LANES:NP_IN] = jnp.zeros((TM, NP_IN - OFF_FF - LANES), BF16)

    def seg(off, w):
        return pl.BlockSpec((TM, w), lambda i, o=off // w: (i, o))

    def full(shape):
        return pl.BlockSpec(shape, lambda i: (0,) * len(shape))

    hp_spec = _row_spec(HP)
    in_specs = [hp_spec] * 13 + [_row_spec(LANES), seg(OFF_CQ, Q_RANK), seg(OFF_CKV, LANES), seg(OFF_FF, LANES),
                                 pl.BlockSpec((TM, 1), lambda i: (i, 0)),
                                 full((1, LANES)), full((1, HP)), full((1, LANES)), full((1, Q_RANK)),
                                 full((1, KV_RANK)), full((Q_RANK, HP)), full((KV_RANK, 2 * HP))]
    out_specs = [_row_spec(NP_IN), full((Q_RANK, HP)), full((KV_RANK, 2 * HP)), full((1, Q_RANK)),
                 full((1, KV_RANK)), full((1, LANES))]
    out_shape = [jax.ShapeDtypeStruct((t, NP_IN), BF16), jax.ShapeDtypeStruct((Q_RANK, HP), F32),
                 jax.ShapeDtypeStruct((KV_RANK, 2 * HP), F32), jax.ShapeDtypeStruct((1, Q_RANK), F32),
                 jax.ShapeDtypeStruct((1, KV_RANK), F32), jax.ShapeDtypeStruct((1, LANES), F32)]
    return pl.pallas_call(
        body, name=name, grid=(t // TM,), in_specs=in_specs, out_specs=out_specs, out_shape=out_shape,
        compiler_params=_cp("arbitrary"))(*dfox, *dret, *dsb, *dmla, drg, dlsf, proj, proj, proj, pos, invf,
                                          lg_lanes, b_pad, g_q, g_kv, wq_pad, wkv_pad)


def _cumsum(x, reverse, name, partials=None):
    t, w = x.shape
    n = t // TM
    xs = [x] if partials is None else [x, partials]

    def body(*refs):
        x_refs, o_ref, carry = refs[:len(xs)], refs[len(xs)], refs[len(xs) + 1]

        @pl.when(pl.program_id(0) == 0)
        def _():
            carry[...] = jnp.zeros_like(carry)

        r = lax.broadcasted_iota(jnp.int32, (TM, TM), 0)
        c = lax.broadcasted_iota(jnp.int32, (TM, TM), 1)
        tri = jnp.where((r <= c) if reverse else (r >= c), 1.0, 0.0).astype(BF16)
        v = x_refs[0][...]
        if partials is not None:
            lane = lax.broadcasted_iota(jnp.int32, (TM, LANES), 1)
            for hb in range(HEADS):
                v = v + jnp.where(lane == hb, jnp.sum(x_refs[1][:, _hs(hb)], axis=1, keepdims=True), 0.0)
        hi = v.astype(BF16)
        r1 = v - hi.astype(F32)
        mid = r1.astype(BF16)
        lo = (r1 - mid.astype(F32)).astype(BF16)
        cs = _bdot(tri, hi) + _bdot(tri, mid) + _bdot(tri, lo) + carry[...]
        o_ref[...] = cs
        carry[...] = cs[0:1, :] if reverse else cs[TM - 1:TM, :]

    imap = (lambda i: (n - 1 - i, 0)) if reverse else (lambda i: (i, 0))
    return pl.pallas_call(
        body, name=name, grid=(n,), in_specs=[pl.BlockSpec((TM, a.shape[1]), imap) for a in xs],
        out_specs=pl.BlockSpec((TM, w), imap),
        out_shape=jax.ShapeDtypeStruct((t, w), F32), scratch_shapes=[pltpu.VMEM((1, w), F32)],
        compiler_params=_cp("arbitrary"))(*xs)


HB_FWD = 4
HB_BWD = 2
HB_SB_FWD = 4
BQS = 512
BQA = 512


def _q_spec(hb, bq=BQ):
    return pl.BlockSpec((bq, hb * LANES), lambda g, i: (i, g))


def _kv_spec(t, which, hb):
    return pl.BlockSpec((t, hb * LANES), lambda g, i, w=which: (0, w * (HEADS // hb) + g))


def _acc_spec(t, hb):
    return pl.BlockSpec((t, hb * LANES), lambda g, i: (0, g))


def _hs(hh):
    return slice(hh * LANES, (hh + 1) * LANES)


def _tile_iota(rows, cols):
    return (lax.broadcasted_iota(jnp.int32, (rows, cols), 0), lax.broadcasted_iota(jnp.int32, (rows, cols), 1))


def _kvt_spec(nkv, width, which, hb):
    return pl.BlockSpec((nkv, hb * LANES, width), lambda g, i, w=which: (0, w * (HEADS // hb) + g, 0))


def _qrow_spec(hb, bq=BQ):
    return pl.BlockSpec((hb, 1, 1, bq), lambda g, i: (g, i, 0, 0))


def _vis(key0, query0, rows, cols, kind):
    r, c = _tile_iota(rows, cols)
    k, q = key0 + r, query0 + c
    if kind == "chunk":
        return (k >> CHUNK_SHIFT) <= (q >> CHUNK_SHIFT)
    return (k < q) if kind == "strict" else (k <= q)


def _softmax_fwd(qkv, kvt, *, chunk_mask, scale, name):
    t = qkv.shape[0]
    nq = t // BQA
    per = BQA // BQ
    hb = HB_FWD
    kind = "chunk" if chunk_mask else "causal"

    def body(q_ref, k_ref, vt_ref, o_ref, lse_ref, m_sc, l_sc, acc_sc):
        i = pl.program_id(1)
        m_sc[...] = jnp.full((hb, 1, BQA), NEG, F32)
        l_sc[...] = jnp.zeros((hb, 1, BQA), F32)
        acc_sc[...] = jnp.zeros((hb, LANES, BQA), F32)

        def tile(j, qoff):
            off = pl.multiple_of(j * BQ, BQ)
            lo = 0 if qoff is None else qoff
            qs = slice(lo, BQA)
            vis = None if qoff is None else _vis(off, i * BQA + lo, BQ, BQA - lo, kind)
            ss = [_bdot(k_ref[pl.ds(off, BQ), _hs(hh)], q_ref[qs, _hs(hh)], NT) for hh in range(hb)]
            ps, alphas = [], []
            for hh in range(hb):
                s = ss[hh]
                if scale != 1.0:
                    s = s * scale
                if vis is not None:
                    s = jnp.where(vis, s, NEG)
                m_old = m_sc[hh, :, qs]
                m_new = jnp.maximum(m_old, jnp.max(s, axis=0, keepdims=True))
                alpha = jnp.exp(m_old - m_new)
                p = jnp.exp(s - m_new)
                l_sc[hh, :, qs] = alpha * l_sc[hh, :, qs] + jnp.sum(p, axis=0, keepdims=True)
                m_sc[hh, :, qs] = m_new
                ps.append(p.astype(BF16))
                alphas.append(alpha)
            for hh in range(hb):
                acc_sc[hh, :, qs] = alphas[hh] * acc_sc[hh, :, qs] + _bdot(vt_ref[j, _hs(hh), :], ps[hh])

        def loop(j, carry):
            tile(j, None)
            return carry

        lax.fori_loop(0, per * i, loop, 0)
        for d in range(per):
            tile(per * i + d, d * BQ)
        for hh in range(hb):
            l = l_sc[hh]
            o_ref[:, _hs(hh)] = (acc_sc[hh] / l).T
            lse_ref[hh, 0] = m_sc[hh] + jnp.log(l)

    return pl.pallas_call(
        body, name=name, grid=(HEADS // hb, nq),
        in_specs=[_q_spec(hb, BQA), _kv_spec(t, 1, hb), _kvt_spec(t // BQ, BQ, 1, hb)],
        out_specs=[_q_spec(hb, BQA), _qrow_spec(hb, BQA)],
        out_shape=[jax.ShapeDtypeStruct((t, HP), F32), jax.ShapeDtypeStruct((HEADS, nq, 1, BQA), F32)],
        scratch_shapes=[pltpu.VMEM((hb, 1, BQA), F32), pltpu.VMEM((hb, 1, BQA), F32),
                        pltpu.VMEM((hb, LANES, BQA), F32)],
        compiler_params=_cp("parallel", "arbitrary"))(qkv, qkv, kvt)


def _softmax_bwd(qkv, kvt, do, lse, delta, *, bias, chunk_mask, scale, name):
    t = qkv.shape[0]
    nq = t // BQA
    per = BQA // BQ
    hb = HB_BWD
    kind = "chunk" if chunk_mask else "causal"

    def body(*refs):
        if bias:
            (q_ref, k_ref, v_ref, kt_ref, do_ref, lse_ref, dl_ref, dq_ref, dk_ref, dv_ref, dck_ref, dcq_ref,
             dq_sc, dcq_sc) = refs
            dcq_sc[...] = jnp.zeros((hb, 1, BQA), F32)
        else:
            q_ref, k_ref, v_ref, kt_ref, do_ref, lse_ref, dl_ref, dq_ref, dk_ref, dv_ref, dq_sc = refs
        i = pl.program_id(1)

        @pl.when(i == 0)
        def _():
            dk_ref[...] = jnp.zeros_like(dk_ref)
            dv_ref[...] = jnp.zeros_like(dv_ref)
            if bias:
                dck_ref[...] = jnp.zeros_like(dck_ref)

        dq_sc[...] = jnp.zeros((hb, LANES, BQA), F32)

        def tile(j, qoff):
            off = pl.multiple_of(j * BQ, BQ)
            lo = 0 if qoff is None else qoff
            qsl = slice(lo, BQA)
            vis = None if qoff is None else _vis(off, i * BQA + lo, BQ, BQA - lo, kind)
            qs = [q_ref[qsl, _hs(hh)] for hh in range(hb)]
            dobs = [do_ref[qsl, _hs(hh)].astype(BF16) for hh in range(hb)]
            ss = [_bdot(k_ref[pl.ds(off, BQ), _hs(hh)], qs[hh], NT) for hh in range(hb)]
            dps = [_bdot(v_ref[pl.ds(off, BQ), _hs(hh)], dobs[hh], NT) for hh in range(hb)]
            pbs, dsbs = [], []
            for hh in range(hb):
                s = ss[hh]
                if scale != 1.0:
                    s = s * scale
                p = jnp.exp(s - lse_ref[hh, 0, :, qsl])
                if vis is not None:
                    p = jnp.where(vis, p, 0.0)
                ds = p * (dps[hh] - dl_ref[hh, 0, :, qsl])
                if bias:
                    part = ds[:, 0:LANES]
                    for b in range(1, (BQA - lo) // LANES):
                        part = part + ds[:, b * LANES:(b + 1) * LANES]
                    dck_ref[pl.ds(off, BQ), _hs(hh)] -= part
                    dcq_sc[hh, :, qsl] += jnp.sum(ds, axis=0, keepdims=True)
                if scale != 1.0:
                    ds = ds * scale
                pbs.append(p.astype(BF16))
                dsbs.append(ds.astype(BF16))
            for hh in range(hb):
                sl = _hs(hh)
                dv_ref[pl.ds(off, BQ), sl] += _bdot(pbs[hh], dobs[hh])
                dk_ref[pl.ds(off, BQ), sl] += _bdot(dsbs[hh], qs[hh])
                dq_sc[hh, :, qsl] += _bdot(kt_ref[j, sl, :], dsbs[hh])

        def loop(j, carry):
            tile(j, None)
            return carry

        lax.fori_loop(0, per * i, loop, 0)
        for d in range(per):
            tile(per * i + d, d * BQ)
        for hh in range(hb):
            dq_ref[:, _hs(hh)] = dq_sc[hh].T
            if bias:
                dcq_ref[hh, 0] = dcq_sc[hh]

    in_specs = [_q_spec(hb, BQA), _kv_spec(t, 1, hb), _kv_spec(t, 2, hb), _kvt_spec(t // BQ, BQ, 0, hb),
                _q_spec(hb, BQA), _qrow_spec(hb, BQA), _qrow_spec(hb, BQA)]
    out_specs = [_q_spec(hb, BQA), _acc_spec(t, hb), _acc_spec(t, hb)]
    out_shape = [jax.ShapeDtypeStruct((t, HP), F32)] * 3
    scratch = [pltpu.VMEM((hb, LANES, BQA), F32)]
    if bias:
        out_specs += [_acc_spec(t, hb), _qrow_spec(hb, BQA)]
        out_shape += [jax.ShapeDtypeStruct((t, HP), F32), jax.ShapeDtypeStruct((HEADS, nq, 1, BQA), F32)]
        scratch.append(pltpu.VMEM((hb, 1, BQA), F32))
    return pl.pallas_call(
        body, name=name, grid=(HEADS // hb, nq), in_specs=in_specs, out_specs=out_specs, out_shape=out_shape,
        scratch_shapes=scratch,
        compiler_params=_cp("parallel", "arbitrary"))(qkv, qkv, qkv, kvt, do, lse, delta)


def _ret_diag_decay(lg1, keys_on_rows=False):
    r, c = _tile_iota(BQ, BQ)
    qn, km = (c, r) if keys_on_rows else (r, c)
    dd = jnp.where(km > qn, jnp.exp((2.0 * lg1) * (km - qn).astype(F32)), 1.0)
    return jnp.where((km >> CHUNK_SHIFT) <= (qn >> CHUNK_SHIFT), dd, 0.0)


def _lg_spec(hb):
    return pl.BlockSpec((hb, 1, LANES), lambda g, i: (g, 0, 0))


def _ret_specs(nq, hb, reverse):
    tile = (lambda i: nq - 1 - i) if reverse else (lambda i: i)
    qkv = [pl.BlockSpec((BQ, hb * LANES), lambda g, i, w=w: (tile(i), w * (HEADS // hb) + g)) for w in range(3)]
    kt = pl.BlockSpec((1, hb * LANES, BQ), lambda g, i: (tile(i), g, 0))
    st = pl.BlockSpec((1, hb * LANES, LANES), lambda g, i: (tile(i), g, 0))
    return qkv, kt, st


def _ret_fwd(qkv, kt, lg_heads, name):
    t = qkv.shape[0]
    nq = t // BQ
    hb = HB_FWD

    def body(lg_ref, q_ref, k_ref, v_ref, kt_ref, o_ref, st_ref, s_sc):
        @pl.when(pl.program_id(1) == 0)
        def _():
            s_sc[...] = jnp.zeros_like(s_sc)

        qs = [q_ref[:, _hs(hh)] for hh in range(hb)]
        vs = [v_ref[:, _hs(hh)] for hh in range(hb)]
        aa = [_bdot(qs[hh], k_ref[:, _hs(hh)], NT) for hh in range(hb)]
        kv = [_bdot(kt_ref[0, _hs(hh), :], vs[hh]) for hh in range(hb)]
        for hh in range(hb):
            sl = _hs(hh)
            lg1 = lg_ref[hh][:, 0:1]
            s = s_sc[hh]
            st_ref[0, sl, :] = s
            shi, slo = _split2(s)
            a = (aa[hh] * _ret_diag_decay(lg1)).astype(BF16)
            o_ref[:, sl] = _bdot(a, vs[hh]) + _bdot(qs[hh], shi) + _bdot(qs[hh], slo)
            s_sc[hh] = jnp.exp(lg1 * float(BQ)) * (s + kv[hh])

    qkv_specs, kt_spec, st_spec = _ret_specs(nq, hb, False)
    return pl.pallas_call(
        body, name=name, grid=(HEADS // hb, nq), in_specs=[_lg_spec(hb)] + qkv_specs + [kt_spec],
        out_specs=[_q_spec(hb), st_spec],
        out_shape=[jax.ShapeDtypeStruct((t, HP), F32), jax.ShapeDtypeStruct((nq, HP, LANES), F32)],
        scratch_shapes=[pltpu.VMEM((hb, LANES, LANES), F32)],
        compiler_params=_cp("parallel", "arbitrary"))(lg_heads, qkv, qkv, qkv, kt)


def _ret_bwd(qkv, states, lg_heads, do, name):
    t = qkv.shape[0]
    nq = t // BQ
    hb = HB_FWD

    def body(lg_ref, q_ref, k_ref, v_ref, st_ref, do_ref, dq_ref, dk_ref, dv_ref, g_sc):
        @pl.when(pl.program_id(1) == 0)
        def _():
            g_sc[...] = jnp.zeros_like(g_sc)

        qs = [q_ref[:, _hs(hh)] for hh in range(hb)]
        ks = [k_ref[:, _hs(hh)] for hh in range(hb)]
        vs = [v_ref[:, _hs(hh)] for hh in range(hb)]
        dobs = [do_ref[:, _hs(hh)].astype(BF16) for hh in range(hb)]
        aa = [_bdot(ks[hh], qs[hh], NT) for hh in range(hb)]
        das = [_bdot(vs[hh], dobs[hh], NT) for hh in range(hb)]
        qdo = [_bdot(qs[hh], dobs[hh], TN) for hh in range(hb)]
        for hh in range(hb):
            sl = _hs(hh)
            lg1 = lg_ref[hh][:, 0:1]
            dd = _ret_diag_decay(lg1, keys_on_rows=True)
            at = (aa[hh] * dd).astype(BF16)
            dat = (das[hh] * dd).astype(BF16)
            h = jnp.exp(lg1 * float(BQ)) * g_sc[hh]
            hhi, hlo = _split2(h)
            shi, slo = _split2(st_ref[0, sl, :])
            dv_ref[:, sl] = _bdot(at, dobs[hh]) + _bdot(ks[hh], hhi) + _bdot(ks[hh], hlo)
            dk_ref[:, sl] = _bdot(dat, qs[hh]) + _bdot(vs[hh], hhi, NT) + _bdot(vs[hh], hlo, NT)
            dq_ref[:, sl] = _bdot(dat, ks[hh], TN) + _bdot(dobs[hh], shi, NT) + _bdot(dobs[hh], slo, NT)
            g_sc[hh] = qdo[hh] + h

    qkv_specs, _, st_spec = _ret_specs(nq, hb, True)
    tile_spec = pl.BlockSpec((BQ, hb * LANES), lambda g, i: (nq - 1 - i, g))
    return pl.pallas_call(
        body, name=name, grid=(HEADS // hb, nq), in_specs=[_lg_spec(hb)] + qkv_specs + [st_spec, tile_spec],
        out_specs=[tile_spec] * 3, out_shape=[jax.ShapeDtypeStruct((t, HP), F32)] * 3,
        scratch_shapes=[pltpu.VMEM((hb, LANES, LANES), F32)],
        compiler_params=_cp("parallel", "arbitrary"))(lg_heads, qkv, qkv, qkv, states, do)


def _sb_tile_logs(q, kb, vis):
    z = _bdot(kb, q, NT)
    ls = -(jnp.maximum(z, 0.0) + jnp.log(1.0 + jnp.exp(-jnp.abs(z))))
    if vis is not None:
        ls = jnp.where(vis, ls, 0.0)
    return z, ls


def _sb_later(ls, after):
    hi, lo = _split2(ls)
    return _bdot(after, hi) + _bdot(after, lo)


def _sb_fwd(qkv, kvt, name):
    t = qkv.shape[0]
    nq = t // BQS
    per = BQS // TKS
    hb = HB_SB_FWD

    def body(q_ref, k_ref, vt_ref, o_ref, tot_ref, acc_sc, r_sc):
        i = pl.program_id(1)
        acc_sc[...] = jnp.zeros((hb, LANES, BQS), F32)
        r_sc[...] = jnp.zeros((hb, 1, BQS), F32)
        mr, mc = _tile_iota(TKS, TKS)
        after = jnp.where(mc > mr, 1.0, 0.0).astype(BF16)

        def tile(j, qoff):
            off = pl.multiple_of(j * TKS, TKS)
            lo = 0 if qoff is None else qoff
            qsl = slice(lo, BQS)
            vis = None if qoff is None else _vis(off, i * BQS + lo, TKS, BQS - lo, "strict")
            zl = [_sb_tile_logs(q_ref[qsl, _hs(hh)], k_ref[pl.ds(off, TKS), _hs(hh)], vis) for hh in range(hb)]
            laters = [_sb_later(zl[hh][1], after) for hh in range(hb)]
            ws = []
            for hh in range(hb):
                z, ls = zl[hh]
                w = jnp.exp(z + ls + laters[hh] + r_sc[hh, :, qsl])
                if vis is not None:
                    w = jnp.where(vis, w, 0.0)
                ws.append(w.astype(BF16))
                r_sc[hh, :, qsl] += jnp.sum(ls, axis=0, keepdims=True)
            for hh in range(hb):
                acc_sc[hh, :, qsl] += _bdot(vt_ref[j, _hs(hh), :], ws[hh])

        for d in reversed(range(per)):
            tile(per * i + d, d * TKS)

        def loop(jj, carry):
            tile(per * i - 1 - jj, None)
            return carry

        lax.fori_loop(0, per * i, loop, 0)
        for hh in range(hb):
            o_ref[:, _hs(hh)] = acc_sc[hh].T
            tot_ref[hh, 0] = r_sc[hh]

    return pl.pallas_call(
        body, name=name, grid=(HEADS // hb, nq),
        in_specs=[_q_spec(hb, BQS), _kv_spec(t, 1, hb), _kvt_spec(t // TKS, TKS, 1, hb)],
        out_specs=[_q_spec(hb, BQS), _qrow_spec(hb, BQS)],
        out_shape=[jax.ShapeDtypeStruct((t, HP), F32), jax.ShapeDtypeStruct((HEADS, nq, 1, BQS), F32)],
        scratch_shapes=[pltpu.VMEM((hb, LANES, BQS), F32), pltpu.VMEM((hb, 1, BQS), F32)],
        compiler_params=_cp("parallel", "arbitrary"))(qkv, qkv, kvt)


def _sb_bwd(qkv, kvt, do, tot, name):
    t = qkv.shape[0]
    nq = t // BQS
    per = BQS // TKS
    hb = HB_BWD

    def body(q_ref, k_ref, v_ref, kt_ref, do_ref, tot_ref, dq_ref, dk_ref, dv_ref, dq_sc, p_sc, g_sc):
        i = pl.program_id(1)

        @pl.when(i == 0)
        def _():
            dk_ref[...] = jnp.zeros_like(dk_ref)
            dv_ref[...] = jnp.zeros_like(dv_ref)

        dq_sc[...] = jnp.zeros((hb, LANES, BQS), F32)
        p_sc[...] = jnp.zeros((hb, 1, BQS), F32)
        g_sc[...] = jnp.zeros((hb, 1, BQS), F32)
        mr, mc = _tile_iota(TKS, TKS)
        after = jnp.where(mc > mr, 1.0, 0.0).astype(BF16)
        before = jnp.where(mc < mr, 1.0, 0.0).astype(BF16)

        def tile(j, qoff):
            off = pl.multiple_of(j * TKS, TKS)
            lo = 0 if qoff is None else qoff
            qsl = slice(lo, BQS)
            vis = None if qoff is None else _vis(off, i * BQS + lo, TKS, BQS - lo, "strict")
            qs = [q_ref[qsl, _hs(hh)] for hh in range(hb)]
            dobs = [do_ref[qsl, _hs(hh)].astype(BF16) for hh in range(hb)]
            zl = [_sb_tile_logs(qs[hh], k_ref[pl.ds(off, TKS), _hs(hh)], vis) for hh in range(hb)]
            dws = [_bdot(v_ref[pl.ds(off, TKS), _hs(hh)], dobs[hh], NT) for hh in range(hb)]
            laters = [_sb_later(zl[hh][1], after) for hh in range(hb)]
            ws, gs = [], []
            for hh in range(hb):
                z, ls = zl[hh]
                own = jnp.sum(ls, axis=0, keepdims=True)
                rest = tot_ref[hh, 0, :, qsl] - p_sc[hh, :, qsl] - own
                w = jnp.exp(z + ls + laters[hh] + rest)
                if vis is not None:
                    w = jnp.where(vis, w, 0.0)
                p_sc[hh, :, qsl] += own
                ws.append(w.astype(BF16))
                gs.append(dws[hh] * w)
            gins = []
            for hh in range(hb):
                ghi, glo = _split2(gs[hh])
                gins.append(_bdot(before, ghi) + _bdot(before, glo))
            dzbs = []
            for hh in range(hb):
                g = gs[hh]
                stay = jnp.exp(zl[hh][1])
                dz = g * stay - (1.0 - stay) * (gins[hh] + g_sc[hh, :, qsl])
                if vis is not None:
                    dz = jnp.where(vis, dz, 0.0)
                g_sc[hh, :, qsl] += jnp.sum(g, axis=0, keepdims=True)
                dzbs.append(dz.astype(BF16))
            for hh in range(hb):
                sl = _hs(hh)
                dv_ref[pl.ds(off, TKS), sl] += _bdot(ws[hh], dobs[hh])
                dk_ref[pl.ds(off, TKS), sl] += _bdot(dzbs[hh], qs[hh])
                dq_sc[hh, :, qsl] += _bdot(kt_ref[j, sl, :], dzbs[hh])

        def loop(j, carry):
            tile(j, None)
            return carry

        lax.fori_loop(0, per * i, loop, 0)
        for d in range(per):
            tile(per * i + d, d * TKS)
        for hh in range(hb):
            dq_ref[:, _hs(hh)] = dq_sc[hh].T

    return pl.pallas_call(
        body, name=name, grid=(HEADS // hb, nq),
        in_specs=[_q_spec(hb, BQS), _kv_spec(t, 1, hb), _kv_spec(t, 2, hb), _kvt_spec(t // TKS, TKS, 0, hb),
                  _q_spec(hb, BQS), _qrow_spec(hb, BQS)],
        out_specs=[_q_spec(hb, BQS), _acc_spec(t, hb), _acc_spec(t, hb)],
        out_shape=[jax.ShapeDtypeStruct((t, HP), F32)] * 3,
        scratch_shapes=[pltpu.VMEM((hb, LANES, BQS), F32), pltpu.VMEM((hb, 1, BQS), F32),
                        pltpu.VMEM((hb, 1, BQS), F32)],
        compiler_params=_cp("parallel", "arbitrary"))(qkv, qkv, qkv, kvt, do, tot)


def _sigmoid(v):
    return 1.0 / (1.0 + jnp.exp(-v))


def _post_fwd(oa, ob, oc, od, proj, g_pad, name):
    t = oa.shape[0]

    def body(oa_ref, ob_ref, oc_ref, od_ref, rg_ref, g_ref, mx_ref):
        g = g_ref[...]

        def group(o, gg):
            r = lax.rsqrt(jnp.sum(o * o, axis=-1, keepdims=True) * (1.0 / GROUP) + EPS)
            return _gather_heads(o * r * gg).astype(BF16)

        mx_ref[:, 0:GROUP] = group(oa_ref[...], g[:, 0:HP])
        mx_ref[:, GROUP:2 * GROUP] = group(ob_ref[...], g[:, HP:2 * HP])
        mx_ref[:, 3 * GROUP:4 * GROUP] = group(od_ref[...], g[:, 3 * HP:4 * HP])
        real = lax.broadcasted_iota(jnp.int32, (TM, LANES), 1) < HEAD_DIM
        rg = _spread_heads(rg_ref[...])
        gated = []
        for hb in range(HEADS):
            sl = slice(hb * LANES, (hb + 1) * LANES)
            o = oc_ref[:, sl]
            mu = jnp.sum(o, axis=-1, keepdims=True) * (1.0 / HEAD_DIM)
            dlt = jnp.where(real, o - mu, 0.0)
            var = jnp.sum(dlt * dlt, axis=-1, keepdims=True) * (1.0 / HEAD_DIM)
            yn = dlt * lax.rsqrt(var + EPS) * g[:, 2 * HP + hb * LANES:2 * HP + (hb + 1) * LANES]
            x = rg[:, sl]
            gated.append(yn * (x * _sigmoid(x)))
        mx_ref[:, 2 * GROUP:3 * GROUP] = _gather_heads(jnp.concatenate(gated, axis=1)).astype(BF16)

    rg_spec = pl.BlockSpec((TM, GROUP), lambda i: (i, OFF_RG // GROUP))
    return pl.pallas_call(
        body, name=name, grid=(t // TM,),
        in_specs=[_row_spec(HP)] * 4 + [rg_spec, _vec_spec(4 * HP)], out_specs=_row_spec(D_MODEL),
        out_shape=jax.ShapeDtypeStruct((t, D_MODEL), BF16), compiler_params=_cp("parallel"))(oa, ob, oc, od, proj, g_pad)


def _post_bwd(dmx, oa, ob, oc, od, proj, g_pad, name):
    t = oa.shape[0]

    def body(dm_ref, oa_ref, ob_ref, oc_ref, od_ref, rg_ref, g_ref,
             doa_ref, dob_ref, doc_ref, dod_ref, dla_ref, dlb_ref, drg_ref, dg_ref):
        @pl.when(pl.program_id(0) == 0)
        def _():
            dg_ref[...] = jnp.zeros_like(dg_ref)

        g = g_ref[...]

        def group_bwd(dm, o, gg):
            r = lax.rsqrt(jnp.sum(o * o, axis=-1, keepdims=True) * (1.0 / GROUP) + EPS)
            oh = o * r
            dgp = jnp.sum(dm * oh, axis=0, keepdims=True)
            dyh = dm * gg
            do = r * (dyh - oh * (jnp.sum(dyh * oh, axis=-1, keepdims=True) * (1.0 / GROUP)))
            return do, dgp

        def delta_bc(do, o):
            prod = do * o
            lane = lax.broadcasted_iota(jnp.int32, (TM, LANES), 1)
            out = jnp.zeros((TM, LANES), F32)
            for hb in range(HEADS):
                out = jnp.where(lane == hb, jnp.sum(prod[:, hb * LANES:(hb + 1) * LANES], axis=-1, keepdims=True), out)
            return out

        dmp = [_spread_heads(dm_ref[:, gi * GROUP:(gi + 1) * GROUP]) for gi in range(4)]
        rg = _spread_heads(rg_ref[...])
        oa = oa_ref[...]
        do_a, dga = group_bwd(dmp[0], oa, g[:, 0:HP])
        doa_ref[...] = do_a
        dla_ref[...] = delta_bc(do_a, oa)
        dg_ref[:, 0:HP] += dga
        ob = ob_ref[...]
        do_b, dgb = group_bwd(dmp[1], ob, g[:, HP:2 * HP])
        dob_ref[...] = do_b
        dlb_ref[...] = delta_bc(do_b, ob)
        dg_ref[:, HP:2 * HP] += dgb
        do_d, dgd = group_bwd(dmp[3], od_ref[...], g[:, 3 * HP:4 * HP])
        dod_ref[...] = do_d
        dg_ref[:, 3 * HP:4 * HP] += dgd
        real = lax.broadcasted_iota(jnp.int32, (TM, LANES), 1) < HEAD_DIM
        for hb in range(HEADS):
            sl = slice(hb * LANES, (hb + 1) * LANES)
            gsl = slice(2 * HP + hb * LANES, 2 * HP + (hb + 1) * LANES)
            o = oc_ref[:, sl]
            mu = jnp.sum(o, axis=-1, keepdims=True) * (1.0 / HEAD_DIM)
            dlt = jnp.where(real, o - mu, 0.0)
            var = jnp.sum(dlt * dlt, axis=-1, keepdims=True) * (1.0 / HEAD_DIM)
            rstd = lax.rsqrt(var + EPS)
            dhat = dlt * rstd
            gc = g[:, gsl]
            x = rg[:, sl]
            sg = _sigmoid(x)
            dm = dmp[2][:, sl]
            drg_ref[:, sl] = dm * (dhat * gc) * (sg * (1.0 + x * (1.0 - sg)))
            dyn = dm * (x * sg)
            dg_ref[:, gsl] += jnp.sum(dyn * dhat, axis=0, keepdims=True)
            ddh = dyn * gc
            m1 = jnp.sum(ddh, axis=-1, keepdims=True) * (1.0 / HEAD_DIM)
            m2 = jnp.sum(ddh * dhat, axis=-1, keepdims=True) * (1.0 / HEAD_DIM)
            doc_ref[:, sl] = jnp.where(real, rstd * (ddh - m1 - dhat * m2), 0.0)

    rg_spec = pl.BlockSpec((TM, GROUP), lambda i: (i, OFF_RG // GROUP))
    hp = _row_spec(HP)
    return pl.pallas_call(
        body, name=name, grid=(t // TM,),
        in_specs=[_row_spec(D_MODEL), hp, hp, hp, hp, rg_spec, _vec_spec(4 * HP)],
        out_specs=[hp] * 4 + [_row_spec(LANES)] * 2 + [hp, _vec_spec(4 * HP)],
        out_shape=[jax.ShapeDtypeStruct((t, HP), F32)] * 4 + [jax.ShapeDtypeStruct((t, LANES), F32)] * 2
        + [jax.ShapeDtypeStruct((t, HP), F32), jax.ShapeDtypeStruct((1, 4 * HP), F32)],
        compiler_params=_cp("arbitrary"))(dmx, oa, ob, oc, od, proj, g_pad)


def _mesh_pos():
    return lax.axis_index("x"), lax.axis_index("y"), lax.axis_index("c")


def _peer(pos, k):
    x, y, c = pos
    px = 1 - x if (k >> 2) & 1 else x
    py = 1 - y if (k >> 1) & 1 else y
    pc = 1 - c if k & 1 else c
    return (px, py, pc), 4 * px + 2 * py + pc


def _exchange(arrs, gather, name):
    n = len(arrs)

    def body(*refs):
        ins, outs = refs[:n], refs[n:2 * n]
        send_sems, recv_sems, loc_sems = refs[2 * n:]
        pos = _mesh_pos()
        me = 4 * pos[0] + 2 * pos[1] + pos[2]
        local = []
        for a in range(n):
            src = ins[a] if gather else ins[a].at[me]
            cp = pltpu.make_async_copy(src, outs[a].at[me], loc_sems.at[a])
            cp.start()
            local.append(cp)
        sends, recvs = [], []
        for k in range(1, N_DEV):
            peer, pid = _peer(pos, k)
            for a in range(n):
                s = a * (N_DEV - 1) + k - 1
                src = ins[a] if gather else ins[a].at[pid]
                cp = pltpu.make_async_remote_copy(
                    src_ref=src, dst_ref=outs[a].at[me], send_sem=send_sems.at[s], recv_sem=recv_sems.at[s],
                    device_id=peer, device_id_type=pl.DeviceIdType.MESH)
                cp.start()
                sends.append(cp)
                recvs.append(pltpu.make_async_remote_copy(
                    src_ref=src, dst_ref=outs[a].at[pid], send_sem=send_sems.at[s], recv_sem=recv_sems.at[s],
                    device_id=peer, device_id_type=pl.DeviceIdType.MESH))
        for cp in recvs:
            cp.wait_recv()
        for cp in sends:
            cp.wait_send()
        for cp in local:
            cp.wait()

    any_spec = pl.BlockSpec(memory_space=pl.ANY)
    out_shape = [jax.ShapeDtypeStruct((N_DEV,) + tuple(a.shape) if gather else tuple(a.shape), a.dtype) for a in arrs]
    return pl.pallas_call(
        body, name=name, in_specs=[any_spec] * n, out_specs=[any_spec] * n, out_shape=out_shape,
        scratch_shapes=[pltpu.SemaphoreType.DMA((n * (N_DEV - 1),)), pltpu.SemaphoreType.DMA((n * (N_DEV - 1),)),
                        pltpu.SemaphoreType.DMA((n,))],
        compiler_params=pltpu.CompilerParams(has_side_effects=True))(*arrs)


def _device_index():
    x, y, c = _mesh_pos()
    return 4 * x + 2 * y + c


def _landing(srcs, gather):
    me = _device_index()
    lands = []
    for a in srcs:
        own = a[None] if gather else lax.dynamic_slice_in_dim(a, me, 1, axis=0)
        shape = (N_DEV,) + tuple(a.shape) if gather else tuple(a.shape)
        lands.append(lax.dynamic_update_slice_in_dim(lax.empty(shape, a.dtype), own, me, axis=0))
    return lands


def _exchange_copies(ins, lands, send_sems, recv_sems, gather):
    pos = _mesh_pos()
    me = 4 * pos[0] + 2 * pos[1] + pos[2]
    sends, recvs = [], []
    for k in range(1, N_DEV):
        peer, pid = _peer(pos, k)
        for a in range(len(ins)):
            s = a * (N_DEV - 1) + k - 1
            src = ins[a] if gather else ins[a].at[pid]
            sends.append(pltpu.make_async_remote_copy(
                src_ref=src, dst_ref=lands[a].at[me], send_sem=send_sems.at[s], recv_sem=recv_sems.at[s],
                device_id=peer, device_id_type=pl.DeviceIdType.MESH))
            recvs.append(pltpu.make_async_remote_copy(
                src_ref=src, dst_ref=lands[a].at[pid], send_sem=send_sems.at[s], recv_sem=recv_sems.at[s],
                device_id=peer, device_id_type=pl.DeviceIdType.MESH))
    return sends, recvs


def _exchange_start(srcs, gather, name):
    n = len(srcs)
    lands = _landing(srcs, gather)
    nsem = n * (N_DEV - 1)

    def body(*refs):
        ins, lnd = refs[:n], refs[n:2 * n]
        send_sems, recv_sems = refs[2 * n], refs[2 * n + 1]
        token = refs[-1]
        sends, _ = _exchange_copies(ins, lnd, send_sems, recv_sems, gather)
        for cp in sends:
            cp.start()
        token[...] = jnp.zeros_like(token)

    hbm = pl.BlockSpec(memory_space=pltpu.HBM)
    sem = pl.BlockSpec(memory_space=pltpu.SEMAPHORE)
    bufs = list(srcs) + lands
    out_shape = ([pltpu.SemaphoreType.DMA((nsem,)), pltpu.SemaphoreType.DMA((nsem,))]
                 + [pltpu.HBM(b.shape, b.dtype) for b in bufs] + [jax.ShapeDtypeStruct((8, LANES), F32)])
    outs = pl.pallas_call(
        body, name=name, in_specs=[hbm] * (2 * n),
        out_specs=[sem, sem] + [hbm] * (2 * n) + [pl.BlockSpec(memory_space=pltpu.VMEM)], out_shape=out_shape,
        input_output_aliases={i: 2 + i for i in range(2 * n)},
        compiler_params=pltpu.CompilerParams(has_side_effects=pltpu.SideEffectType.DATAFLOW_SIDE_EFFECTING),
    )(*[pltpu.with_memory_space_constraint(b, pltpu.HBM) for b in bufs])
    return (outs[0], outs[1], outs[2:2 + n], outs[2 + n:2 + 2 * n]), outs[-1]


def _exchange_wait(state, after, gather, name):
    send_sems, recv_sems, srcs, lands = state
    n = len(srcs)

    def body(*refs):
        ins, lnd = refs[:n], refs[n:2 * n]
        s_sems, r_sems = refs[2 * n], refs[2 * n + 1]
        sends, recvs = _exchange_copies(ins, lnd, s_sems, r_sems, gather)
        for cp in sends:
            cp.wait_send()
        for cp in recvs:
            cp.wait_recv()

    hbm = pl.BlockSpec(memory_space=pltpu.HBM)
    sem = pl.BlockSpec(memory_space=pltpu.SEMAPHORE)
    bufs = list(srcs) + list(lands)
    outs = pl.pallas_call(
        body, name=name, in_specs=[hbm] * (2 * n) + [sem, sem, pl.BlockSpec(memory_space=pl.ANY)],
        out_specs=[hbm] * (2 * n), out_shape=[pltpu.HBM(b.shape, b.dtype) for b in bufs],
        input_output_aliases={i: i for i in range(2 * n)},
        compiler_params=pltpu.CompilerParams(has_side_effects=pltpu.SideEffectType.DATAFLOW_SIDE_EFFECTING),
    )(*bufs, send_sems, recv_sems, after)
    return outs[n:]


def _adam_vals(w, g, m, v):
    m = ADAM_B1 * m + (1.0 - ADAM_B1) * g
    v = ADAM_B2 * v + (1.0 - ADAM_B2) * (g * g)
    m_hat = m / ADAM_C1
    v_hat = v / ADAM_C2
    delta = -ADAM_LR * (m_hat / (jnp.sqrt(v_hat) + ADAM_EPS) + ADAM_WD * w)
    return delta, m, v


def _small_allreduce_adam(part, w, m, v, name):
    rows = part.shape[0]

    def body(p_ref, w_ref, m_ref, v_ref, g_ref, d_ref, nm_ref, nv_ref, gath, send_sems, recv_sems):
        pos = _mesh_pos()
        me = 4 * pos[0] + 2 * pos[1] + pos[2]
        gath[me] = p_ref[...]
        sends, recvs = [], []
        for k in range(1, N_DEV):
            peer, pid = _peer(pos, k)
            cp = pltpu.make_async_remote_copy(
                src_ref=p_ref, dst_ref=gath.at[me], send_sem=send_sems.at[k - 1], recv_sem=recv_sems.at[k - 1],
                device_id=peer, device_id_type=pl.DeviceIdType.MESH)
            cp.start()
            sends.append(cp)
            recvs.append(pltpu.make_async_remote_copy(
                src_ref=p_ref, dst_ref=gath.at[pid], send_sem=send_sems.at[k - 1], recv_sem=recv_sems.at[k - 1],
                device_id=peer, device_id_type=pl.DeviceIdType.MESH))
        for cp in recvs:
            cp.wait_recv()
        for cp in sends:
            cp.wait_send()
        g = gath[0]
        for p in range(1, N_DEV):
            g = g + gath[p]
        g_ref[...] = g
        d, nm, nv = _adam_vals(w_ref[...], g, m_ref[...], v_ref[...])
        d_ref[...] = d
        nm_ref[...] = nm
        nv_ref[...] = nv

    vm = pl.BlockSpec(memory_space=pltpu.VMEM)
    sds = jax.ShapeDtypeStruct((rows, LANES), F32)
    return pl.pallas_call(
        body, name=name, in_specs=[vm] * 4, out_specs=[vm] * 4, out_shape=[sds] * 4,
        scratch_shapes=[pltpu.VMEM((N_DEV, rows, LANES), F32), pltpu.SemaphoreType.DMA((N_DEV - 1,)),
                        pltpu.SemaphoreType.DMA((N_DEV - 1,))],
        compiler_params=pltpu.CompilerParams(has_side_effects=True))(part, w, m, v)


def _reduce_adam(recv, w, m, v, name):
    shape = w.shape
    c = shape[-1]
    r = int(np.prod(shape[:-1]))
    recv2, w2, m2, v2 = recv.reshape(N_DEV, r, c), w.reshape(r, c), m.reshape(r, c), v.reshape(r, c)
    tr = r
    while tr * c * 4 > (1 << 20) and tr % 16 == 0:
        tr //= 2

    def body(r_ref, w_ref, m_ref, v_ref, g_ref, d_ref, nm_ref, nv_ref):
        g = r_ref[0].astype(F32)
        for p in range(1, N_DEV):
            g = g + r_ref[p].astype(F32)
        g_ref[...] = g
        d, nm, nv = _adam_vals(w_ref[...], g, m_ref[...], v_ref[...])
        d_ref[...] = d
        nm_ref[...] = nm
        nv_ref[...] = nv

    spec = pl.BlockSpec((tr, c), lambda i: (i, 0))
    sds = jax.ShapeDtypeStruct((r, c), F32)
    outs = pl.pallas_call(
        body, name=name, grid=(r // tr,),
        in_specs=[pl.BlockSpec((N_DEV, tr, c), lambda i: (0, i, 0)), spec, spec, spec],
        out_specs=[spec] * 4, out_shape=[sds] * 4, compiler_params=_cp("parallel"))(recv2, w2, m2, v2)
    return [o.reshape(shape) for o in outs]


def _pad_heads(w, real=HEAD_DIM):
    lead = w.shape[:-1]
    w = w.reshape(lead + (HEADS, real))
    w = jnp.pad(w, [(0, 0)] * len(lead) + [(0, 0), (0, LANES - real)])
    return w.reshape(lead + (HP,))


def _unpad_heads(w, real=HEAD_DIM):
    lead = w.shape[:-1]
    return w.reshape(lead + (HEADS, LANES))[..., :real].reshape(lead + (HEADS * real,))


_IN_SEGS = (("fq", 0, 256), ("fk", 256, 512), ("fv", 512, 768), ("ff", 768, 772), ("cq", 772, 1028),
            ("ckv", 1028, 1156), ("kr", 1156, 1188), ("rq", 1188, 1444), ("rk", 1444, 1700), ("rv", 1700, 1956),
            ("rg", 1956, 2212), ("sq", 2212, 2468), ("sk", 2468, 2724), ("sv", 2724, 2980))


def _pad_w_in(w):
    s = {n: w[:, a:b] for n, a, b in _IN_SEGS}
    rows = w.shape[0]
    z = lambda n: jnp.zeros((rows, n), w.dtype)
    parts = [s[n] for n in ("fq", "fk", "fv", "rq", "rk", "rv", "sq", "sk", "sv", "rg", "cq", "ckv")]
    parts += [z(HEAD_DIM), s["kr"], z(LANES - HEAD_DIM - ROPE_MLA), s["ff"], z(LANES - HEADS),
              z(NP_IN - OFF_FF - LANES)]
    return jnp.concatenate(parts, axis=1)


def _unpad_w_in(wp):
    seg = lambda off, n=GROUP: wp[:, off:off + n]
    parts = [seg(OFF_FOX), seg(OFF_FOX + GROUP), seg(OFF_FOX + 2 * GROUP), seg(OFF_FF, HEADS),
             seg(OFF_CQ, Q_RANK), seg(OFF_CKV, KV_RANK), seg(OFF_KR + HEAD_DIM, ROPE_MLA),
             seg(OFF_RET), seg(OFF_RET + GROUP), seg(OFF_RET + 2 * GROUP), seg(OFF_RG),
             seg(OFF_SB), seg(OFF_SB + GROUP), seg(OFF_SB + 2 * GROUP)]
    return jnp.concatenate(parts, axis=1)


def _pad_w_kv(w):
    w4 = w.reshape(KV_RANK, HEADS, 2 * HEAD_DIM)
    k = w4[:, :, :HEAD_DIM].reshape(KV_RANK, GROUP)
    v = w4[:, :, HEAD_DIM:].reshape(KV_RANK, GROUP)
    return jnp.concatenate([_pad_heads(k), _pad_heads(v)], axis=1)


def _unpad_w_kv(wp):
    k = _unpad_heads(wp[:, :HP]).reshape(KV_RANK, HEADS, HEAD_DIM)
    v = _unpad_heads(wp[:, HP:]).reshape(KV_RANK, HEADS, HEAD_DIM)
    return jnp.concatenate([k, v], axis=-1).reshape(KV_RANK, HEADS * 2 * HEAD_DIM)


def _pad_gain_out(g):
    g = jnp.pad(g.reshape(4 * HEADS, HEAD_DIM), ((0, 0), (0, LANES - HEAD_DIM)))
    return g.reshape(1, 4 * HP)


def _unpad_gain_out(gp):
    return gp.reshape(4 * HEADS, LANES)[:, :HEAD_DIM].reshape(D_MODEL)


_SMALL = (("g_mix_pre", 1024), ("g_mix_post", 1024), ("g_ffn_pre", 1024), ("g_ffn_post", 1024), ("g_mix_out", 1024),
          ("g_q_lora", 256), ("g_kv_lora", 128), ("b_forget", 4))


def _pack_small(vals):
    parts = []
    for name, n in _SMALL:
        a = vals[name].astype(F32)
        if n < LANES:
            a = jnp.pad(a, ((0, 0), (0, LANES - n)))
        parts.append(a)
    return jnp.concatenate(parts, axis=1).reshape(DEPTH * SMALL_ROWS, LANES)


def _unpack_small(packed):
    flat = packed.reshape(DEPTH, SMALL_ROWS * LANES)
    out, off = {}, 0
    for name, n in _SMALL:
        out[name] = flat[:, off:off + n]
        off += max(n, LANES)
    return out


def kernel(x, positions, g_mix_pre, w_in, b_forget, g_q_lora, w_q_up, g_kv_lora, w_kv_up, g_mix_out, w_out, g_mix_post, g_ffn_pre, w_ffn_up, w_ffn_down, g_ffn_post, loss_target, m_g_mix_pre, m_w_in, m_b_forget, m_g_q_lora, m_w_q_up, m_g_kv_lora, m_w_kv_up, m_g_mix_out, m_w_out, m_g_mix_post, m_g_ffn_pre, m_w_ffn_up, m_w_ffn_down, m_g_ffn_post, v_g_mix_pre, v_w_in, v_b_forget, v_g_q_lora, v_w_q_up, v_g_kv_lora, v_w_kv_up, v_g_mix_out, v_w_out, v_g_mix_post, v_g_ffn_pre, v_w_ffn_up, v_w_ffn_down, v_g_ffn_post):
    t = x.shape[1]
    nq = t // BQ
    x0 = x[0]
    tgt = loss_target[0]
    pos = positions[0].astype(F32).reshape(t, 1)

    half_r, half_m = HEAD_DIM // 2, ROPE_MLA // 2
    invf_r = ROPE_BASE ** (-jnp.arange(half_r, dtype=F32) / half_r)
    invf_m = ROPE_BASE ** (-jnp.arange(half_m, dtype=F32) / half_m)
    invf = jnp.concatenate([invf_r, invf_r, invf_m, invf_m,
                            jnp.zeros((LANES - HEAD_DIM - ROPE_MLA,), F32)]).reshape(1, LANES)
    log_gamma = jnp.log1p(-jnp.power(2.0, -5.0 - jnp.arange(HEADS, dtype=F32)))
    lg_lanes = jnp.repeat(log_gamma, LANES).reshape(1, HP)
    lg_heads = jnp.broadcast_to(log_gamma[:, None, None], (HEADS, 1, LANES))

    big = [w_in, w_q_up, w_kv_up, w_out, w_ffn_up, w_ffn_down]
    bf = lambda w: w.astype(BF16)
    first = _exchange([bf(w_in[0]), bf(w_q_up), bf(w_kv_up)], True, "weights_gather_first")
    rest_state, rest_token = _exchange_start([bf(w_in[1]), bf(w_out), bf(w_ffn_up), bf(w_ffn_down)], True,
                                             "weights_gather_start")
    wq_g = first[1].transpose(1, 2, 0, 3).reshape(DEPTH, Q_RANK, 384)
    wkv_g = first[2].transpose(1, 2, 0, 3).reshape(DEPTH, KV_RANK, 512)

    row = lambda g: g.reshape(1, -1)
    layers = []
    for l in range(DEPTH):
        layers.append(dict(
            wq=_pad_heads(wq_g[l], 96), wkv=_pad_w_kv(wkv_g[l]),
            g_pre=row(g_mix_pre[l]), g_post=row(g_mix_post[l]), g_fpre=row(g_ffn_pre[l]), g_fpost=row(g_ffn_post[l]),
            g_out=_pad_gain_out(g_mix_out[l]), g_q=row(g_q_lora[l]), g_kv=row(g_kv_lora[l]),
            b_pad=jnp.pad(b_forget[l], (0, LANES - HEADS)).reshape(1, LANES)))
    layers[0]["win"] = _pad_w_in(first[0].reshape(D_MODEL, D_IN))

    saved = []
    xin = x0
    h = _rms_fwd(xin, layers[0]["g_pre"] + rest_token[0:1, 0:1], "rms_pre_0")
    loss_row = dx = None
    for l, p in enumerate(layers):
        s = dict(x=xin, h=h)
        proj = _mm(h, p["win"], name=f"in_proj_{l}", tm=512, tn=NP_IN)
        cum = _cumsum(_forget_lsf(proj, p["b_pad"], f"forget_lsf_{l}"), False, f"forget_cumsum_{l}")
        fox, ret, sb, mla, fox_t, sb_t, mla_t, ret_t = _prep_fwd(proj, cum, pos, invf, lg_lanes, p["g_q"],
                                                          p["g_kv"], p["wq"], p["wkv"], f"prep_fwd_{l}")
        oa, lse_a = _softmax_fwd(fox, fox_t, chunk_mask=False, scale=1.0, name=f"fox_fwd_{l}")
        ob, lse_b = _softmax_fwd(mla, mla_t, chunk_mask=True, scale=96.0 ** -0.5, name=f"mla_fwd_{l}")
        oc, ret_st = _ret_fwd(ret, ret_t, lg_heads, f"ret_fwd_{l}")
        od, sb_tot = _sb_fwd(sb, sb_t, f"sb_fwd_{l}")
        if l == 0:
            rest = _exchange_wait(rest_state, od, True, "weights_gather_wait")
            wout_g = rest[1].transpose(1, 0, 2, 3).reshape(DEPTH, D_MODEL, D_MODEL)
            wup_g = rest[2].transpose(1, 2, 0, 3).reshape(DEPTH, D_MODEL, D_FF)
            wdn_g = rest[3].transpose(1, 0, 2, 3).reshape(DEPTH, D_FF, D_MODEL)
            layers[1]["win"] = _pad_w_in(rest[0].reshape(D_MODEL, D_IN))
            for ll in range(DEPTH):
                layers[ll].update(wout=wout_g[ll], wup=wup_g[ll], wdn=wdn_g[ll])
        mixed = _post_fwd(oa, ob, oc, od, proj, p["g_out"], f"post_fwd_{l}")
        mix = _mm(mixed, p["wout"], name=f"out_proj_{l}")
        x1, h2 = _add_rms_fwd(xin, mix, p["g_post"], p["g_fpre"], f"mix_residual_{l}")
        a = _mm(h2, p["wup"], name=f"ffn_up_{l}")
        y = _mm(a, p["wdn"], name=f"ffn_down_{l}", a_fn=_relu2)
        s.update(proj=proj, fox=fox, ret=ret, sb=sb, mla=mla, fox_t=fox_t, sb_t=sb_t, mla_t=mla_t, ret_st=ret_st, oa=oa, ob=ob, oc=oc,
                 od=od, sb_tot=sb_tot, lse_a=lse_a, lse_b=lse_b, mixed=mixed, mix=mix, x1=x1, h2=h2, a=a, y=y)
        saved.append(s)
        if l + 1 < DEPTH:
            xin, h = _add_rms_fwd(x1, y, p["g_fpost"], layers[l + 1]["g_pre"], f"ffn_residual_{l}")
        else:
            loss_row, dx = _final_loss(x1, y, p["g_fpost"], tgt, "loss")

    small_g = {n: [None] * DEPTH for n, _ in _SMALL}
    big_g = [[None] * DEPTH for _ in range(6)]
    to_send = [
        lambda g: g.reshape(N_DEV, 1, D_MODEL // N_DEV, D_IN),
        lambda g: g.reshape(Q_RANK, N_DEV, 384 // N_DEV).transpose(1, 0, 2)[:, None],
        lambda g: g.reshape(KV_RANK, N_DEV, 512 // N_DEV).transpose(1, 0, 2)[:, None],
        lambda g: g.reshape(N_DEV, 1, D_MODEL // N_DEV, D_MODEL),
        lambda g: g.reshape(D_MODEL, N_DEV, D_FF // N_DEV).transpose(1, 0, 2)[:, None],
        lambda g: g.reshape(N_DEV, 1, D_FF // N_DEV, D_MODEL),
    ]
    send_of = lambda ks, l: [to_send[k](big_g[k][l]).astype(BF16) for k in ks]
    late_state = early_state = None
    order_token = jnp.zeros((1, 1), F32)
    for l in reversed(range(DEPTH)):
        p, s = layers[l], saved[l]
        dy, dg = _norm_bwd(dx, s["y"], p["g_fpost"] + order_token, None, BF16, f"ffn_post_bwd_{l}")
        small_g["g_ffn_post"][l] = dg
        da = _mm(dy, p["wdn"], name=f"ffn_down_dx_{l}", tb=True, out_dtype=BF16, epi=_drelu2, epi_in=s["a"])
        big_g[5][l] = _mm(s["a"], dy, name=f"ffn_down_dw_{l}", ta=True, a_fn=_relu2, tk=2048)
        big_g[4][l] = _mm(s["h2"], da, name=f"ffn_up_dw_{l}", ta=True, tk=2048)
        dh2 = _mm(da, p["wup"], name=f"ffn_up_dx_{l}", tb=True)
        dx1, dg = _norm_bwd(dh2, s["x1"], p["g_fpre"], dx, F32, f"ffn_pre_bwd_{l}")
        small_g["g_ffn_pre"][l] = dg
        dmix, dg = _norm_bwd(dx1, s["mix"], p["g_post"], None, BF16, f"mix_post_bwd_{l}")
        small_g["g_mix_post"][l] = dg
        dmixed = _mm(dmix, p["wout"], name=f"out_proj_dx_{l}", tb=True)
        big_g[3][l] = _mm(s["mixed"], dmix, name=f"out_proj_dw_{l}", ta=True, tk=2048)
        g_out = p["g_out"]
        if l == 0:
            early_state, early_token = _exchange_start(send_of((3, 4, 5), 0), False, "grads_layer0_early_start")
            g_out = g_out + early_token[0:1, 0:1]
        doa, dob, doc, dod, dla, dlb, drg, dgo = _post_bwd(dmixed, s["oa"], s["ob"], s["oc"], s["od"], s["proj"],
                                                           g_out, f"post_bwd_{l}")
        small_g["g_mix_out"][l] = _unpad_gain_out(dgo).reshape(1, D_MODEL)
        as_rows = lambda a: a[:, :HEADS].T.reshape(HEADS, t // BQA, 1, BQA)
        dfq, dfk, dfv, dcum_k, dcum_q = _softmax_bwd(s["fox"], s["fox_t"], doa, s["lse_a"], as_rows(dla), bias=True,
                                                     chunk_mask=False, scale=1.0, name=f"fox_bwd_{l}")
        dmq, dmk, dmv = _softmax_bwd(s["mla"], s["mla_t"], dob, s["lse_b"], as_rows(dlb), bias=False, chunk_mask=True,
                                     scale=96.0 ** -0.5, name=f"mla_bwd_{l}")
        drq, drk, drv = _ret_bwd(s["ret"], s["ret_st"], lg_heads, doc, f"ret_bwd_{l}")
        dsq, dsk, dsv = _sb_bwd(s["sb"], s["sb_t"], dod, s["sb_tot"], f"sb_bwd_{l}")
        dcum_q = jnp.pad(dcum_q.reshape(HEADS, t).T, ((0, 0), (0, LANES - HEADS)))
        dlsf = _cumsum(dcum_q, True, f"forget_cumsum_bwd_{l}", partials=dcum_k)
        dproj, dwq, dwkv, dgq, dgkv, dbf = _prep_bwd(
            (dfq, dfk, dfv), (drq, drk, drv), (dsq, dsk, dsv), (dmq, dmk, dmv), drg, dlsf, s["proj"], pos, invf,
            lg_lanes, p["b_pad"], p["g_q"], p["g_kv"], p["wq"], p["wkv"], f"prep_bwd_{l}")
        small_g["g_q_lora"][l] = dgq
        small_g["g_kv_lora"][l] = dgkv
        small_g["b_forget"][l] = dbf[:, :HEADS]
        big_g[1][l] = _unpad_heads(dwq, 96)
        big_g[2][l] = _unpad_w_kv(dwkv)
        big_g[0][l] = _unpad_w_in(_mm(s["h"], dproj, name=f"in_proj_dw_{l}", ta=True, tn=NP_IN // 2))
        dh = _mm(dproj, p["win"], name=f"in_proj_dx_{l}", tb=True, tk=NP_IN // 2)
        dx, dg = _norm_bwd(dh, s["x"], p["g_pre"], dx1, F32, f"mix_pre_bwd_{l}")
        small_g["g_mix_pre"][l] = dg
        if l == DEPTH - 1:
            late_state, late_token = _exchange_start(send_of(range(6), l), False, "grads_layer1_start")
            order_token = late_token[0:1, 0:1]
    grad_x = dx.reshape(1, t, D_MODEL)

    last = _exchange(send_of((0, 1, 2), 0), False, "grads_layer0_rest")
    late = _exchange_wait(late_state, dx, False, "grads_layer1_wait")
    early = _exchange_wait(early_state, dx, False, "grads_layer0_early_wait")
    recv = [jnp.concatenate([(last[k] if k < 3 else early[k - 3]), late[k]], axis=1) for k in range(6)]
    ms = [m_w_in, m_w_q_up, m_w_kv_up, m_w_out, m_w_ffn_up, m_w_ffn_down]
    vs = [v_w_in, v_w_q_up, v_w_kv_up, v_w_out, v_w_ffn_up, v_w_ffn_down]
    names = ["w_in", "w_q_up", "w_kv_up", "w_out", "w_ffn_up", "w_ffn_down"]
    res = {}
    for k, n in enumerate(names):
        res[n] = _reduce_adam(recv[k], big[k], ms[k], vs[k], f"adamw_{n}")

    small_w = dict(g_mix_pre=g_mix_pre, g_mix_post=g_mix_post, g_ffn_pre=g_ffn_pre, g_ffn_post=g_ffn_post,
                   g_mix_out=g_mix_out, g_q_lora=g_q_lora, g_kv_lora=g_kv_lora, b_forget=b_forget)
    small_m = dict(g_mix_pre=m_g_mix_pre, g_mix_post=m_g_mix_post, g_ffn_pre=m_g_ffn_pre, g_ffn_post=m_g_ffn_post,
                   g_mix_out=m_g_mix_out, g_q_lora=m_g_q_lora, g_kv_lora=m_g_kv_lora, b_forget=m_b_forget)
    small_v = dict(g_mix_pre=v_g_mix_pre, g_mix_post=v_g_mix_post, g_ffn_pre=v_g_ffn_pre, g_ffn_post=v_g_ffn_post,
                   g_mix_out=v_g_mix_out, g_q_lora=v_g_q_lora, g_kv_lora=v_g_kv_lora, b_forget=v_b_forget)
    n_small = DEPTH * SMALL_ROWS
    extra = lambda a: jnp.concatenate([a, jnp.zeros((8, LANES), F32)], axis=0)
    part = jnp.concatenate([_pack_small({n: jnp.concatenate(small_g[n], axis=0) for n, _ in _SMALL}),
                            jnp.broadcast_to(loss_row, (8, LANES))], axis=0)
    sres = _small_allreduce_adam(part, extra(_pack_small(small_w)), extra(_pack_small(small_m)),
                                 extra(_pack_small(small_v)), "small_allreduce_adamw")
    loss = sres[0][n_small, 0]
    sg, sd, sm, sv = [_unpack_small(a[:n_small]) for a in sres]
    for n, _ in _SMALL:
        res[n] = [sg[n], sd[n], sm[n], sv[n]]

    order = ["g_mix_pre", "w_in", "b_forget", "g_q_lora", "w_q_up", "g_kv_lora", "w_kv_up", "g_mix_out", "w_out",
             "g_mix_post", "g_ffn_pre", "w_ffn_up", "w_ffn_down", "g_ffn_post"]
    outs = [loss, grad_x]
    for idx in range(4):
        outs += [res[n][idx] for n in order]
    return tuple(outs)
```

```python
import functools
import math

import numpy as np
import jax
import jax.numpy as jnp
from jax import lax
from jax.experimental import pallas as pl
from jax.experimental.pallas import tpu as pltpu

F32 = jnp.float32
BF16 = jnp.bfloat16

D_MODEL = 1024
DEPTH = 2
N_DEV = 8
GROUP = 256
HEADS = 4
HEAD_DIM = 64
LANES = 128
HP = HEADS * LANES
QKV = 3 * HP
Q_RANK = 256
KV_RANK = 128
ROPE_MLA = 32
D_FF = 4096
D_IN = 2980
CHUNK_SHIFT = 6
EPS = 1e-6
ROPE_BASE = 10000.0
NEG = -1e30

QKV_IN = 3 * GROUP
OFF_FOX, OFF_RET, OFF_SB = 0, QKV_IN, 2 * QKV_IN
OFF_RG = 3 * QKV_IN
OFF_CQ = OFF_RG + GROUP
OFF_CKV = OFF_CQ + Q_RANK
OFF_KR = OFF_CKV + LANES
OFF_FF = OFF_KR + LANES
NP_IN = 3328

BQ = 256
TKS = 128
TM = 256
VMEM_LIMIT = 48 * 1024 * 1024

ADAM_LR, ADAM_B1, ADAM_B2, ADAM_EPS, ADAM_WD, ADAM_STEP = 0.001, 0.9, 0.999, 1e-08, 0.01, 10
ADAM_C1 = 1.0 - ADAM_B1 ** ADAM_STEP
ADAM_C2 = 1.0 - ADAM_B2 ** ADAM_STEP

SMALL_ROWS = 44

NT = (((1,), (1,)), ((), ()))
TN = (((0,), (0,)), ((), ()))


def _cp(*sem):
    return pltpu.CompilerParams(dimension_semantics=sem if sem else None, vmem_limit_bytes=VMEM_LIMIT)


def _bdot(a, b, dn=None):
    if dn is None:
        return jnp.dot(a, b, preferred_element_type=F32)
    return lax.dot_general(a, b, dn, preferred_element_type=F32)


def _split2(x):
    hi = x.astype(BF16)
    lo = (x - hi.astype(F32)).astype(BF16)
    return hi, lo


def _mm(a, b, *, name, ta=False, tb=False, out_dtype=F32, a_fn=None, epi=None, epi_in=None,
        tm=1024, tn=1024, tk=1024):
    m, k = (a.shape[1], a.shape[0]) if ta else a.shape
    n = b.shape[0] if tb else b.shape[1]
    tm, tn, tk = min(tm, m), min(tn, n), min(tk, k)
    assert m % tm == 0 and n % tn == 0 and k % tk == 0, (name, m, n, k)
    nk = k // tk
    dn = (((0 if ta else 1,), (1 if tb else 0,)), ((), ()))

    def body(*refs):
        if epi is None:
            a_ref, b_ref, o_ref = refs[:3]
            e_ref = None
            rest = refs[3:]
        else:
            a_ref, b_ref, e_ref, o_ref = refs[:4]
            rest = refs[4:]
        av = a_ref[...]
        if a_fn is not None:
            av = a_fn(av)
        part = lax.dot_general(av.astype(BF16), b_ref[...].astype(BF16), dn, preferred_element_type=F32)

        def finish(r):
            if epi is not None:
                r = epi(r, e_ref[...])
            o_ref[...] = r.astype(out_dtype)

        if nk == 1:
            finish(part)
        else:
            acc_ref = rest[0]
            kk = pl.program_id(2)

            @pl.when(kk == 0)
            def _():
                acc_ref[...] = part

            @pl.when(kk > 0)
            def _():
                acc_ref[...] += part

            @pl.when(kk == nk - 1)
            def _():
                finish(acc_ref[...])

    a_spec = pl.BlockSpec((tk, tm), lambda i, j, kk: (kk, i)) if ta else pl.BlockSpec((tm, tk), lambda i, j, kk: (i, kk))
    b_spec = pl.BlockSpec((tn, tk), lambda i, j, kk: (j, kk)) if tb else pl.BlockSpec((tk, tn), lambda i, j, kk: (kk, j))
    o_spec = pl.BlockSpec((tm, tn), lambda i, j, kk: (i, j))
    in_specs = [a_spec, b_spec]
    args = [a, b]
    if epi is not None:
        in_specs.append(o_spec)
        args.append(epi_in)
    return pl.pallas_call(
        body, name=name, grid=(m // tm, n // tn, nk),
        in_specs=in_specs, out_specs=o_spec,
        out_shape=jax.ShapeDtypeStruct((m, n), out_dtype),
        scratch_shapes=[pltpu.VMEM((tm, tn), F32)] if nk > 1 else [],
        compiler_params=_cp("parallel", "parallel", "arbitrary"),
    )(*args)


def _relu2(v):
    r = jnp.maximum(v, 0.0)
    return r * r


def _drelu2(du, av):
    return du * (2.0 * jnp.maximum(av, 0.0))


def _rms(v, g):
    r = lax.rsqrt(jnp.mean(v * v, axis=-1, keepdims=True) + EPS)
    return v * r * g


def _row_spec(w):
    return pl.BlockSpec((TM, w), lambda i: (i, 0))


def _vec_spec(w):
    return pl.BlockSpec((1, w), lambda i: (0, 0))


def _rms_fwd(x, g, name):
    t, d = x.shape

    def body(x_ref, g_ref, h_ref):
        h_ref[...] = _rms(x_ref[...], g_ref[...]).astype(BF16)

    return pl.pallas_call(
        body, name=name, grid=(t // TM,), in_specs=[_row_spec(d), _vec_spec(d)], out_specs=_row_spec(d),
        out_shape=jax.ShapeDtypeStruct((t, d), BF16), compiler_params=_cp("parallel"))(x, g)


def _add_rms_fwd(x, y, g1, g2, name):
    t, d = x.shape

    def body(x_ref, y_ref, g1_ref, g2_ref, xn_ref, h_ref):
        xn = x_ref[...] + _rms(y_ref[...], g1_ref[...])
        xn_ref[...] = xn
        h_ref[...] = _rms(xn, g2_ref[...]).astype(BF16)

    return pl.pallas_call(
        body, name=name, grid=(t // TM,),
        in_specs=[_row_spec(d), _row_spec(d), _vec_spec(d), _vec_spec(d)],
        out_specs=[_row_spec(d), _row_spec(d)],
        out_shape=[jax.ShapeDtypeStruct((t, d), F32), jax.ShapeDtypeStruct((t, d), BF16)],
        compiler_params=_cp("parallel"))(x, y, g1, g2)


def _final_loss(x, y, g, tgt, name):
    t, d = x.shape

    def body(x_ref, y_ref, g_ref, t_ref, l_ref, dx_ref):
        @pl.when(pl.program_id(0) == 0)
        def _():
            l_ref[...] = jnp.zeros_like(l_ref)

        err = x_ref[...] + _rms(y_ref[...], g_ref[...]) - t_ref[...]
        dx_ref[...] = err * (1.0 / d)
        l_ref[...] += jnp.sum(jnp.sum(err * err, axis=1, keepdims=True), axis=0, keepdims=True) * (0.5 / d)

    return pl.pallas_call(
        body, name=name, grid=(t // TM,),
        in_specs=[_row_spec(d), _row_spec(d), _vec_spec(d), _row_spec(d)],
        out_specs=[pl.BlockSpec((1, LANES), lambda i: (0, 0)), _row_spec(d)],
        out_shape=[jax.ShapeDtypeStruct((1, LANES), F32), jax.ShapeDtypeStruct((t, d), F32)],
        compiler_params=_cp("arbitrary"))(x, y, g, tgt)


def _rms_bwd_vals(dn, v, g):
    w = v.shape[-1]
    r = lax.rsqrt(jnp.mean(v * v, axis=-1, keepdims=True) + EPS)
    vh = v * r
    dgp = jnp.sum(dn * vh, axis=0, keepdims=True)
    dvh = dn * g
    dv = r * (dvh - vh * (jnp.sum(dvh * vh, axis=-1, keepdims=True) * (1.0 / w)))
    return dv, dgp


def _norm_bwd(dn, v, g, resid, out_dtype, name):
    t, d = v.shape
    has_res = resid is not None

    def body(*refs):
        if has_res:
            dn_ref, v_ref, g_ref, r_ref, dv_ref, dg_ref = refs
        else:
            dn_ref, v_ref, g_ref, dv_ref, dg_ref = refs

        @pl.when(pl.program_id(0) == 0)
        def _():
            dg_ref[...] = jnp.zeros_like(dg_ref)

        dv, dgp = _rms_bwd_vals(dn_ref[...].astype(F32), v_ref[...], g_ref[...])
        if has_res:
            dv = dv + r_ref[...]
        dv_ref[...] = dv.astype(out_dtype)
        dg_ref[...] += dgp

    in_specs = [_row_spec(d), _row_spec(d), _vec_spec(d)] + ([_row_spec(d)] if has_res else [])
    args = [dn, v, g] + ([resid] if has_res else [])
    return pl.pallas_call(
        body, name=name, grid=(t // TM,), in_specs=in_specs,
        out_specs=[_row_spec(d), _vec_spec(d)],
        out_shape=[jax.ShapeDtypeStruct((t, d), out_dtype), jax.ShapeDtypeStruct((1, d), F32)],
        compiler_params=_cp("arbitrary"))(*args)


def _rope_trig(pos, invf):
    ang = pos * invf
    return jnp.cos(ang), jnp.sin(ang)


def _rope_tables(trig, w, lo, half):
    c, s = trig
    lane = lax.broadcasted_iota(jnp.int32, c.shape, 1)
    active = (lane >= lo) & (lane < lo + 2 * half)
    cos = jnp.concatenate([jnp.where(active, c, 1.0)] * (w // LANES), axis=1)
    sin = jnp.concatenate([jnp.where(active, s, 0.0)] * (w // LANES), axis=1)
    lanes = lax.broadcasted_iota(jnp.int32, (c.shape[0], w), 1) & (LANES - 1)
    first = (lanes >= lo) & (lanes < lo + half)
    second = (lanes >= lo + half) & (lanes < lo + 2 * half)
    return cos, sin, first, second


def _rope_apply(v, cos, sin, first, second, half, sign):
    w = v.shape[-1]
    up = pltpu.roll(v, w - half, 1)
    dn = pltpu.roll(v, half, 1)
    rot = jnp.where(first, -up, jnp.where(second, dn, 0.0))
    return v * cos + rot * (sin * sign)


def _forget_lsf(proj, b_pad, name):
    t = proj.shape[0]

    def body(ff_ref, b_ref, o_ref):
        f = ff_ref[...] + b_ref[...]
        o_ref[...] = -(jnp.maximum(-f, 0.0) + jnp.log(1.0 + jnp.exp(-jnp.abs(f))))

    return pl.pallas_call(
        body, name=name, grid=(t // TM,),
        in_specs=[pl.BlockSpec((TM, LANES), lambda i: (i, OFF_FF // LANES)), _vec_spec(LANES)],
        out_specs=_row_spec(LANES), out_shape=jax.ShapeDtypeStruct((t, LANES), F32),
        compiler_params=_cp("parallel"))(proj, b_pad)


def _split3(c):
    hi = c.astype(BF16).astype(F32)
    mid = (c - hi).astype(BF16).astype(F32)
    return hi, mid, (c - hi) - mid


def _spread_heads(x):
    low = lax.broadcasted_iota(jnp.int32, (x.shape[0], LANES), 1) < HEAD_DIM
    out = []
    for c in range(x.shape[1] // LANES):
        blk = x[:, c * LANES:(c + 1) * LANES]
        out.append(jnp.where(low, blk, 0.0))
        out.append(jnp.where(low, pltpu.roll(blk, HEAD_DIM, 1), 0.0))
    return jnp.concatenate(out, axis=1)


def _gather_heads(y):
    low = lax.broadcasted_iota(jnp.int32, (y.shape[0], LANES), 1) < HEAD_DIM
    out = []
    for c in range(y.shape[1] // (2 * LANES)):
        a = y[:, 2 * c * LANES:(2 * c + 1) * LANES]
        b = y[:, (2 * c + 1) * LANES:(2 * c + 2) * LANES]
        out.append(jnp.where(low, a, pltpu.roll(b, HEAD_DIM, 1)))
    return jnp.concatenate(out, axis=1)


def _transposed_tiles(dst, row0, blocks, width):
    for b, blk in enumerate(blocks):
        bt = blk.T.astype(BF16)
        rows = bt.shape[0]
        for w in range(TM // width):
            dst[w, row0 + b * rows:row0 + (b + 1) * rows, :] = bt[:, w * width:(w + 1) * width]


def _prep_fwd(proj, cum, pos, invf, lg_lanes, g_q, g_kv, wq_pad, wkv_pad, name):
    t = proj.shape[0]

    def body(fox_ref, ret_ref, sb_ref, cq_ref, ckv_ref, kr_ref, cum_ref, pos_ref, invf_ref, lg_ref,
             gq_ref, gkv_ref, wq_ref, wkv_ref,
             ofox_ref, oret_ref, osb_ref, omla_ref, ofoxt_ref, osbt_ref, omlat_ref, orett_ref):
        pos_v = pos_ref[...]
        foxv, retv, sbv = [[_spread_heads(ref[:, g * GROUP:(g + 1) * GROUP]) for g in range(3)]
                           for ref in (fox_ref, ret_ref, sb_ref)]
        osb_ref[:, 0:HP] = (sbv[0] * 0.125).astype(BF16)
        osb_ref[:, HP:2 * HP] = sbv[1].astype(BF16)
        osb_ref[:, 2 * HP:QKV] = sbv[2].astype(BF16)
        _transposed_tiles(osbt_ref, 0, [sbv[1], sbv[2]], TKS)
        lane = lax.broadcasted_iota(jnp.int32, (TM, LANES), 1)
        cumv = cum_ref[...]
        fq, fk = [], []
        for hb in range(HEADS):
            hi, mid, lo = _split3(cumv[:, hb:hb + 1])
            q = foxv[0][:, hb * LANES:(hb + 1) * LANES] * 0.125
            k = foxv[1][:, hb * LANES:(hb + 1) * LANES]
            ones_q = (lane >= HEAD_DIM) & (lane < HEAD_DIM + 3)
            ones_k = (lane >= HEAD_DIM + 3) & (lane < HEAD_DIM + 6)
            q = jnp.where(ones_q, 1.0, q)
            k = jnp.where(ones_k, 1.0, k)
            for n, part in enumerate((hi, mid, lo)):
                q = jnp.where(lane == HEAD_DIM + 3 + n, part, q)
                k = jnp.where(lane == HEAD_DIM + n, -part, k)
            fq.append(q)
            fk.append(k)
        fk = jnp.concatenate(fk, axis=1)
        ofox_ref[:, 0:HP] = jnp.concatenate(fq, axis=1).astype(BF16)
        ofox_ref[:, HP:2 * HP] = fk.astype(BF16)
        ofox_ref[:, 2 * HP:QKV] = foxv[2].astype(BF16)
        _transposed_tiles(ofoxt_ref, 0, [fk, foxv[2]], BQ)
        trig = _rope_trig(pos_v, invf_ref[...])
        cos, sin, first, second = _rope_tables(trig, HP, 0, HEAD_DIM // 2)
        nloc = lax.broadcasted_iota(jnp.int32, (TM, 1), 0).astype(F32)
        dec = lg_ref[...] * nloc
        rq = _rope_apply(retv[0], cos, sin, first, second, HEAD_DIM // 2, 1.0)
        rk = _rope_apply(retv[1], cos, sin, first, second, HEAD_DIM // 2, 1.0)
        oret_ref[:, 0:HP] = (rq * jnp.exp(dec)).astype(BF16)
        rk = rk * 0.125 * jnp.exp(-dec)
        oret_ref[:, HP:2 * HP] = rk.astype(BF16)
        _transposed_tiles(orett_ref, 0, [rk], BQ)
        oret_ref[:, 2 * HP:QKV] = retv[2].astype(BF16)
        cosm, sinm, firstm, secondm = _rope_tables(trig, HP, HEAD_DIM, ROPE_MLA // 2)
        cqn = _rms(cq_ref[...], gq_ref[...]).astype(BF16)
        qm = _bdot(cqn, wq_ref[...])
        omla_ref[:, 0:HP] = _rope_apply(qm, cosm, sinm, firstm, secondm, ROPE_MLA // 2, 1.0).astype(BF16)
        ckvn = _rms(ckv_ref[...], gkv_ref[...]).astype(BF16)
        kv = _bdot(ckvn, wkv_ref[...])
        krr = _rope_apply(kr_ref[...], cosm[:, 0:LANES], sinm[:, 0:LANES], firstm[:, 0:LANES],
                          secondm[:, 0:LANES], ROPE_MLA // 2, 1.0)
        mk = kv[:, 0:HP] + jnp.concatenate([krr] * HEADS, axis=1)
        omla_ref[:, HP:2 * HP] = mk.astype(BF16)
        omla_ref[:, 2 * HP:QKV] = kv[:, HP:2 * HP].astype(BF16)
        _transposed_tiles(omlat_ref, 0, [mk, kv[:, HP:2 * HP]], BQ)

    def seg(off, w):
        return pl.BlockSpec((TM, w), lambda i, o=off // w: (i, o))

    def full(shape):
        return pl.BlockSpec(shape, lambda i: (0,) * len(shape))

    def tiles(width):
        return pl.BlockSpec((TM // width, 2 * HP, width), lambda i: (i, 0, 0))

    in_specs = [seg(OFF_FOX, QKV_IN), seg(OFF_RET, QKV_IN), seg(OFF_SB, QKV_IN), seg(OFF_CQ, Q_RANK),
                seg(OFF_CKV, LANES), seg(OFF_KR, LANES), _row_spec(LANES), pl.BlockSpec((TM, 1), lambda i: (i, 0)),
                full((1, LANES)), full((1, HP)), full((1, Q_RANK)), full((1, KV_RANK)),
                full((Q_RANK, HP)), full((KV_RANK, 2 * HP))]
    out_specs = [_row_spec(QKV)] * 4 + [tiles(BQ), tiles(TKS), tiles(BQ),
                                        pl.BlockSpec((1, HP, BQ), lambda i: (i, 0, 0))]
    out_shape = [jax.ShapeDtypeStruct((t, QKV), BF16)] * 4 + [
        jax.ShapeDtypeStruct((t // BQ, 2 * HP, BQ), BF16), jax.ShapeDtypeStruct((t // TKS, 2 * HP, TKS), BF16),
        jax.ShapeDtypeStruct((t // BQ, 2 * HP, BQ), BF16), jax.ShapeDtypeStruct((t // BQ, HP, BQ), BF16)]
    return pl.pallas_call(
        body, name=name, grid=(t // TM,), in_specs=in_specs, out_specs=out_specs, out_shape=out_shape,
        compiler_params=_cp("parallel"))(proj, proj, proj, proj, proj, proj, cum, pos, invf, lg_lanes,
                                         g_q, g_kv, wq_pad, wkv_pad)


def _prep_bwd(dfox, dret, dsb, dmla, drg, dlsf, proj, pos, invf, lg_lanes, b_pad, g_q, g_kv,
              wq_pad, wkv_pad, name):
    t = proj.shape[0]

    def body(dfq, dfk, dfv, drq, drk, drv, dsq, dsk, dsv, dmq, dmk, dmv, drg_ref, dlsf_ref,
             cq_ref, ckv_ref, ff_ref, pos_ref, invf_ref, lg_ref, b_ref, gq_ref, gkv_ref, wq_ref, wkv_ref,
             dp_ref, dwq_ref, dwkv_ref, dgq_ref, dgkv_ref, dbf_ref):
        @pl.when(pl.program_id(0) == 0)
        def _():
            dwq_ref[...] = jnp.zeros_like(dwq_ref)
            dwkv_ref[...] = jnp.zeros_like(dwkv_ref)
            dgq_ref[...] = jnp.zeros_like(dgq_ref)
            dgkv_ref[...] = jnp.zeros_like(dgkv_ref)
            dbf_ref[...] = jnp.zeros_like(dbf_ref)

        pos_v = pos_ref[...]
        def put(off, val):
            dp_ref[:, off:off + GROUP] = _gather_heads(val).astype(BF16)

        for off, (dq, dk, dv) in ((OFF_FOX, (dfq, dfk, dfv)), (OFF_SB, (dsq, dsk, dsv))):
            put(off, dq[...] * 0.125)
            put(off + GROUP, dk[...])
            put(off + 2 * GROUP, dv[...])
        trig = _rope_trig(pos_v, invf_ref[...])
        cos, sin, first, second = _rope_tables(trig, HP, 0, HEAD_DIM // 2)
        nloc = lax.broadcasted_iota(jnp.int32, (TM, 1), 0).astype(F32)
        dec = lg_ref[...] * nloc
        dq = _rope_apply(drq[...] * jnp.exp(dec), cos, sin, first, second, HEAD_DIM // 2, -1.0)
        dk = _rope_apply(drk[...] * (0.125 * jnp.exp(-dec)), cos, sin, first, second, HEAD_DIM // 2, -1.0)
        put(OFF_RET, dq)
        put(OFF_RET + GROUP, dk)
        put(OFF_RET + 2 * GROUP, drv[...])
        put(OFF_RG, drg_ref[...])
        cosm, sinm, firstm, secondm = _rope_tables(trig, HP, HEAD_DIM, ROPE_MLA // 2)
        dql = _rope_apply(dmq[...], cosm, sinm, firstm, secondm, ROPE_MLA // 2, -1.0).astype(BF16)
        cq = cq_ref[...]
        cqn = _rms(cq, gq_ref[...]).astype(BF16)
        dwq_ref[...] += _bdot(cqn, dql, TN)
        dcqn = _bdot(dql, wq_ref[...], NT)
        dcq, dgq = _rms_bwd_vals(dcqn, cq, gq_ref[...])
        dgq_ref[...] += dgq
        dp_ref[:, OFF_CQ:OFF_CQ + Q_RANK] = dcq.astype(BF16)
        dkm = dmk[...]
        dkv = jnp.concatenate([dkm, dmv[...]], axis=1).astype(BF16)
        ckv = ckv_ref[...]
        ckvn = _rms(ckv, gkv_ref[...]).astype(BF16)
        dwkv_ref[...] += _bdot(ckvn, dkv, TN)
        dckvn = _bdot(dkv, wkv_ref[...], NT)
        dckv, dgkv = _rms_bwd_vals(dckvn, ckv, gkv_ref[...])
        dgkv_ref[...] += dgkv
        dp_ref[:, OFF_CKV:OFF_CKV + LANES] = dckv.astype(BF16)
        dkr = dkm[:, 0:LANES] + dkm[:, LANES:2 * LANES] + dkm[:, 2 * LANES:3 * LANES] + dkm[:, 3 * LANES:HP]
        act = firstm[:, 0:LANES] | secondm[:, 0:LANES]
        dkr = jnp.where(act, dkr, 0.0)
        dkr = _rope_apply(dkr, cosm[:, 0:LANES], sinm[:, 0:LANES], firstm[:, 0:LANES], secondm[:, 0:LANES],
                          ROPE_MLA // 2, -1.0)
        dp_ref[:, OFF_KR:OFF_KR + LANES] = dkr.astype(BF16)
        f = ff_ref[...] + b_ref[...]
        dff = dlsf_ref[...] / (1.0 + jnp.exp(f))
        dbf_ref[...] += jnp.sum(dff, axis=0, keepdims=True)
        dp_ref[:, OFF_FF:OFF_FF + LANES] = dff.astype(BF16)
        dp_ref[:, OFF_FF + LANES:NP_IN] = jnp.zeros((TM, NP_IN - OFF_FF - LANES), BF16)

    def seg(off, w):
        return pl.BlockSpec((TM, w), lambda i, o=off // w: (i, o))

    def full(shape):
        return pl.BlockSpec(shape, lambda i: (0,) * len(shape))

    hp_spec = _row_spec(HP)
    in_specs = [hp_spec] * 13 + [_row_spec(LANES), seg(OFF_CQ, Q_RANK), seg(OFF_CKV, LANES), seg(OFF_FF, LANES),
                                 pl.BlockSpec((TM, 1), lambda i: (i, 0)),
                                 full((1, LANES)), full((1, HP)), full((1, LANES)), full((1, Q_RANK)),
                                 full((1, KV_RANK)), full((Q_RANK, HP)), full((KV_RANK, 2 * HP))]
    out_specs = [_row_spec(NP_IN), full((Q_RANK, HP)), full((KV_RANK, 2 * HP)), full((1, Q_RANK)),
                 full((1, KV_RANK)), full((1, LANES))]
    out_shape = [jax.ShapeDtypeStruct((t, NP_IN), BF16), jax.ShapeDtypeStruct((Q_RANK, HP), F32),
                 jax.ShapeDtypeStruct((KV_RANK, 2 * HP), F32), jax.ShapeDtypeStruct((1, Q_RANK), F32),
                 jax.ShapeDtypeStruct((1, KV_RANK), F32), jax.ShapeDtypeStruct((1, LANES), F32)]
    return pl.pallas_call(
        body, name=name, grid=(t // TM,), in_specs=in_specs, out_specs=out_specs, out_shape=out_shape,
        compiler_params=_cp("arbitrary"))(*dfox, *dret, *dsb, *dmla, drg, dlsf, proj, proj, proj, pos, invf,
                                          lg_lanes, b_pad, g_q, g_kv, wq_pad, wkv_pad)


def _cumsum(x, reverse, name, partials=None):
    t, w = x.shape
    n = t // TM
    xs = [x] if partials is None else [x, partials]

    def body(*refs):
        x_refs, o_ref, carry = refs[:len(xs)], refs[len(xs)], refs[len(xs) + 1]

        @pl.when(pl.program_id(0) == 0)
        def _():
            carry[...] = jnp.zeros_like(carry)

        r = lax.broadcasted_iota(jnp.int32, (TM, TM), 0)
        c = lax.broadcasted_iota(jnp.int32, (TM, TM), 1)
        tri = jnp.where((r <= c) if reverse else (r >= c), 1.0, 0.0).astype(BF16)
        v = x_refs[0][...]
        if partials is not None:
            lane = lax.broadcasted_iota(jnp.int32, (TM, LANES), 1)
            for hb in range(HEADS):
                v = v + jnp.where(lane == hb, jnp.sum(x_refs[1][:, _hs(hb)], axis=1, keepdims=True), 0.0)
        hi = v.astype(BF16)
        r1 = v - hi.astype(F32)
        mid = r1.astype(BF16)
        lo = (r1 - mid.astype(F32)).astype(BF16)
        cs = _bdot(tri, hi) + _bdot(tri, mid) + _bdot(tri, lo) + carry[...]
        o_ref[...] = cs
        carry[...] = cs[0:1, :] if reverse else cs[TM - 1:TM, :]

    imap = (lambda i: (n - 1 - i, 0)) if reverse else (lambda i: (i, 0))
    return pl.pallas_call(
        body, name=name, grid=(n,), in_specs=[pl.BlockSpec((TM, a.shape[1]), imap) for a in xs],
        out_specs=pl.BlockSpec((TM, w), imap),
        out_shape=jax.ShapeDtypeStruct((t, w), F32), scratch_shapes=[pltpu.VMEM((1, w), F32)],
        compiler_params=_cp("arbitrary"))(*xs)


HB_FWD = 4
HB_BWD = 2
HB_SB_FWD = 4
BQS = 512
BQA = 512


def _q_spec(hb, bq=BQ):
    return pl.BlockSpec((bq, hb * LANES), lambda g, i: (i, g))


def _kv_spec(t, which, hb):
    return pl.BlockSpec((t, hb * LANES), lambda g, i, w=which: (0, w * (HEADS // hb) + g))


def _acc_spec(t, hb):
    return pl.BlockSpec((t, hb * LANES), lambda g, i: (0, g))


def _hs(hh):
    return slice(hh * LANES, (hh + 1) * LANES)


def _tile_iota(rows, cols):
    return (lax.broadcasted_iota(jnp.int32, (rows, cols), 0), lax.broadcasted_iota(jnp.int32, (rows, cols), 1))


def _kvt_spec(nkv, width, which, hb):
    return pl.BlockSpec((nkv, hb * LANES, width), lambda g, i, w=which: (0, w * (HEADS // hb) + g, 0))


def _qrow_spec(hb, bq=BQ):
    return pl.BlockSpec((hb, 1, 1, bq), lambda g, i: (g, i, 0, 0))


def _vis(key0, query0, rows, cols, kind):
    r, c = _tile_iota(rows, cols)
    k, q = key0 + r, query0 + c
    if kind == "chunk":
        return (k >> CHUNK_SHIFT) <= (q >> CHUNK_SHIFT)
    return (k < q) if kind == "strict" else (k <= q)


def _softmax_fwd(qkv, kvt, *, chunk_mask, scale, name):
    t = qkv.shape[0]
    nq = t // BQA
    per = BQA // BQ
    hb = HB_FWD
    kind = "chunk" if chunk_mask else "causal"

    def body(q_ref, k_ref, vt_ref, o_ref, lse_ref, m_sc, l_sc, acc_sc):
        i = pl.program_id(1)
        m_sc[...] = jnp.full((hb, 1, BQA), NEG, F32)
        l_sc[...] = jnp.zeros((hb, 1, BQA), F32)
        acc_sc[...] = jnp.zeros((hb, LANES, BQA), F32)

        def tile(j, qoff):
            off = pl.multiple_of(j * BQ, BQ)
            lo = 0 if qoff is None else qoff
            qs = slice(lo, BQA)
            vis = None if qoff is None else _vis(off, i * BQA + lo, BQ, BQA - lo, kind)
            ss = [_bdot(k_ref[pl.ds(off, BQ), _hs(hh)], q_ref[qs, _hs(hh)], NT) for hh in range(hb)]
            ps, alphas = [], []
            for hh in range(hb):
                s = ss[hh]
                if scale != 1.0:
                    s = s * scale
                if vis is not None:
                    s = jnp.where(vis, s, NEG)
                m_old = m_sc[hh, :, qs]
                m_new = jnp.maximum(m_old, jnp.max(s, axis=0, keepdims=True))
                alpha = jnp.exp(m_old - m_new)
                p = jnp.exp(s - m_new)
                l_sc[hh, :, qs] = alpha * l_sc[hh, :, qs] + jnp.sum(p, axis=0, keepdims=True)
                m_sc[hh, :, qs] = m_new
                ps.append(p.astype(BF16))
                alphas.append(alpha)
            for hh in range(hb):
                acc_sc[hh, :, qs] = alphas[hh] * acc_sc[hh, :, qs] + _bdot(vt_ref[j, _hs(hh), :], ps[hh])

        def loop(j, carry):
            tile(j, None)
            return carry

        lax.fori_loop(0, per * i, loop, 0)
        for d in range(per):
            tile(per * i + d, d * BQ)
        for hh in range(hb):
            l = l_sc[hh]
            o_ref[:, _hs(hh)] = (acc_sc[hh] / l).T
            lse_ref[hh, 0] = m_sc[hh] + jnp.log(l)

    return pl.pallas_call(
        body, name=name, grid=(HEADS // hb, nq),
        in_specs=[_q_spec(hb, BQA), _kv_spec(t, 1, hb), _kvt_spec(t // BQ, BQ, 1, hb)],
        out_specs=[_q_spec(hb, BQA), _qrow_spec(hb, BQA)],
        out_shape=[jax.ShapeDtypeStruct((t, HP), F32), jax.ShapeDtypeStruct((HEADS, nq, 1, BQA), F32)],
        scratch_shapes=[pltpu.VMEM((hb, 1, BQA), F32), pltpu.VMEM((hb, 1, BQA), F32),
                        pltpu.VMEM((hb, LANES, BQA), F32)],
        compiler_params=_cp("parallel", "arbitrary"))(qkv, qkv, kvt)


def _softmax_bwd(qkv, kvt, do, lse, delta, *, bias, chunk_mask, scale, name):
    t = qkv.shape[0]
    nq = t // BQA
    per = BQA // BQ
    hb = HB_BWD
    kind = "chunk" if chunk_mask else "causal"

    def body(*refs):
        if bias:
            (q_ref, k_ref, v_ref, kt_ref, do_ref, lse_ref, dl_ref, dq_ref, dk_ref, dv_ref, dck_ref, dcq_ref,
             dq_sc, dcq_sc) = refs
            dcq_sc[...] = jnp.zeros((hb, 1, BQA), F32)
        else:
            q_ref, k_ref, v_ref, kt_ref, do_ref, lse_ref, dl_ref, dq_ref, dk_ref, dv_ref, dq_sc = refs
        i = pl.program_id(1)

        @pl.when(i == 0)
        def _():
            dk_ref[...] = jnp.zeros_like(dk_ref)
            dv_ref[...] = jnp.zeros_like(dv_ref)
            if bias:
                dck_ref[...] = jnp.zeros_like(dck_ref)

        dq_sc[...] = jnp.zeros((hb, LANES, BQA), F32)

        def tile(j, qoff):
            off = pl.multiple_of(j * BQ, BQ)
            lo = 0 if qoff is None else qoff
            qsl = slice(lo, BQA)
            vis = None if qoff is None else _vis(off, i * BQA + lo, BQ, BQA - lo, kind)
            qs = [q_ref[qsl, _hs(hh)] for hh in range(hb)]
            dobs = [do_ref[qsl, _hs(hh)].astype(BF16) for hh in range(hb)]
            ss = [_bdot(k_ref[pl.ds(off, BQ), _hs(hh)], qs[hh], NT) for hh in range(hb)]
            dps = [_bdot(v_ref[pl.ds(off, BQ), _hs(hh)], dobs[hh], NT) for hh in range(hb)]
            pbs, dsbs = [], []
            for hh in range(hb):
                s = ss[hh]
                if scale != 1.0:
                    s = s * scale
                p = jnp.exp(s - lse_ref[hh, 0, :, qsl])
                if vis is not None:
                    p = jnp.where(vis, p, 0.0)
                ds = p * (dps[hh] - dl_ref[hh, 0, :, qsl])
                if bias:
                    part = ds[:, 0:LANES]
                    for b in range(1, (BQA - lo) // LANES):
                        part = part + ds[:, b * LANES:(b + 1) * LANES]
                    dck_ref[pl.ds(off, BQ), _hs(hh)] -= part
                    dcq_sc[hh, :, qsl] += jnp.sum(ds, axis=0, keepdims=True)
                if scale != 1.0:
                    ds = ds * scale
                pbs.append(p.astype(BF16))
                dsbs.append(ds.astype(BF16))
            for hh in range(hb):
                sl = _hs(hh)
                dv_ref[pl.ds(off, BQ), sl] += _bdot(pbs[hh], dobs[hh])
                dk_ref[pl.ds(off, BQ), sl] += _bdot(dsbs[hh], qs[hh])
                dq_sc[hh, :, qsl] += _bdot(kt_ref[j, sl, :], dsbs[hh])

        def loop(j, carry):
            tile(j, None)
            return carry

        lax.fori_loop(0, per * i, loop, 0)
        for d in range(per):
            tile(per * i + d, d * BQ)
        for hh in range(hb):
            dq_ref[:, _hs(hh)] = dq_sc[hh].T
            if bias:
                dcq_ref[hh, 0] = dcq_sc[hh]

    in_specs = [_q_spec(hb, BQA), _kv_spec(t, 1, hb), _kv_spec(t, 2, hb), _kvt_spec(t // BQ, BQ, 0, hb),
                _q_spec(hb, BQA), _qrow_spec(hb, BQA), _qrow_spec(hb, BQA)]
    out_specs = [_q_spec(hb, BQA), _acc_spec(t, hb), _acc_spec(t, hb)]
    out_shape = [jax.ShapeDtypeStruct((t, HP), F32)] * 3
    scratch = [pltpu.VMEM((hb, LANES, BQA), F32)]
    if bias:
        out_specs += [_acc_spec(t, hb), _qrow_spec(hb, BQA)]
        out_shape += [jax.ShapeDtypeStruct((t, HP), F32), jax.ShapeDtypeStruct((HEADS, nq, 1, BQA), F32)]
        scratch.append(pltpu.VMEM((hb, 1, BQA), F32))
    return pl.pallas_call(
        body, name=name, grid=(HEADS // hb, nq), in_specs=in_specs, out_specs=out_specs, out_shape=out_shape,
        scratch_shapes=scratch,
        compiler_params=_cp("parallel", "arbitrary"))(qkv, qkv, qkv, kvt, do, lse, delta)


def _ret_diag_decay(lg1, keys_on_rows=False):
    r, c = _tile_iota(BQ, BQ)
    qn, km = (c, r) if keys_on_rows else (r, c)
    dd = jnp.where(km > qn, jnp.exp((2.0 * lg1) * (km - qn).astype(F32)), 1.0)
    return jnp.where((km >> CHUNK_SHIFT) <= (qn >> CHUNK_SHIFT), dd, 0.0)


def _lg_spec(hb):
    return pl.BlockSpec((hb, 1, LANES), lambda g, i: (g, 0, 0))


def _ret_specs(nq, hb, reverse):
    tile = (lambda i: nq - 1 - i) if reverse else (lambda i: i)
    qkv = [pl.BlockSpec((BQ, hb * LANES), lambda g, i, w=w: (tile(i), w * (HEADS // hb) + g)) for w in range(3)]
    kt = pl.BlockSpec((1, hb * LANES, BQ), lambda g, i: (tile(i), g, 0))
    st = pl.BlockSpec((1, hb * LANES, LANES), lambda g, i: (tile(i), g, 0))
    return qkv, kt, st


def _ret_fwd(qkv, kt, lg_heads, name):
    t = qkv.shape[0]
    nq = t // BQ
    hb = HB_FWD

    def body(lg_ref, q_ref, k_ref, v_ref, kt_ref, o_ref, st_ref, s_sc):
        @pl.when(pl.program_id(1) == 0)
        def _():
            s_sc[...] = jnp.zeros_like(s_sc)

        qs = [q_ref[:, _hs(hh)] for hh in range(hb)]
        vs = [v_ref[:, _hs(hh)] for hh in range(hb)]
        aa = [_bdot(qs[hh], k_ref[:, _hs(hh)], NT) for hh in range(hb)]
        kv = [_bdot(kt_ref[0, _hs(hh), :], vs[hh]) for hh in range(hb)]
        for hh in range(hb):
            sl = _hs(hh)
            lg1 = lg_ref[hh][:, 0:1]
            s = s_sc[hh]
            st_ref[0, sl, :] = s
            shi, slo = _split2(s)
            a = (aa[hh] * _ret_diag_decay(lg1)).astype(BF16)
            o_ref[:, sl] = _bdot(a, vs[hh]) + _bdot(qs[hh], shi) + _bdot(qs[hh], slo)
            s_sc[hh] = jnp.exp(lg1 * float(BQ)) * (s + kv[hh])

    qkv_specs, kt_spec, st_spec = _ret_specs(nq, hb, False)
    return pl.pallas_call(
        body, name=name, grid=(HEADS // hb, nq), in_specs=[_lg_spec(hb)] + qkv_specs + [kt_spec],
        out_specs=[_q_spec(hb), st_spec],
        out_shape=[jax.ShapeDtypeStruct((t, HP), F32), jax.ShapeDtypeStruct((nq, HP, LANES), F32)],
        scratch_shapes=[pltpu.VMEM((hb, LANES, LANES), F32)],
        compiler_params=_cp("parallel", "arbitrary"))(lg_heads, qkv, qkv, qkv, kt)


def _ret_bwd(qkv, states, lg_heads, do, name):
    t = qkv.shape[0]
    nq = t // BQ
    hb = HB_FWD

    def body(lg_ref, q_ref, k_ref, v_ref, st_ref, do_ref, dq_ref, dk_ref, dv_ref, g_sc):
        @pl.when(pl.program_id(1) == 0)
        def _():
            g_sc[...] = jnp.zeros_like(g_sc)

        qs = [q_ref[:, _hs(hh)] for hh in range(hb)]
        ks = [k_ref[:, _hs(hh)] for hh in range(hb)]
        vs = [v_ref[:, _hs(hh)] for hh in range(hb)]
        dobs = [do_ref[:, _hs(hh)].astype(BF16) for hh in range(hb)]
        aa = [_bdot(ks[hh], qs[hh], NT) for hh in range(hb)]
        das = [_bdot(vs[hh], dobs[hh], NT) for hh in range(hb)]
        qdo = [_bdot(qs[hh], dobs[hh], TN) for hh in range(hb)]
        for hh in range(hb):
            sl = _hs(hh)
            lg1 = lg_ref[hh][:, 0:1]
            dd = _ret_diag_decay(lg1, keys_on_rows=True)
            at = (aa[hh] * dd).astype(BF16)
            dat = (das[hh] * dd).astype(BF16)
            h = jnp.exp(lg1 * float(BQ)) * g_sc[hh]
            hhi, hlo = _split2(h)
            shi, slo = _split2(st_ref[0, sl, :])
            dv_ref[:, sl] = _bdot(at, dobs[hh]) + _bdot(ks[hh], hhi) + _bdot(ks[hh], hlo)
            dk_ref[:, sl] = _bdot(dat, qs[hh]) + _bdot(vs[hh], hhi, NT) + _bdot(vs[hh], hlo, NT)
            dq_ref[:, sl] = _bdot(dat, ks[hh], TN) + _bdot(dobs[hh], shi, NT) + _bdot(dobs[hh], slo, NT)
            g_sc[hh] = qdo[hh] + h

    qkv_specs, _, st_spec = _ret_specs(nq, hb, True)
    tile_spec = pl.BlockSpec((BQ, hb * LANES), lambda g, i: (nq - 1 - i, g))
    return pl.pallas_call(
        body, name=name, grid=(HEADS // hb, nq), in_specs=[_lg_spec(hb)] + qkv_specs + [st_spec, tile_spec],
        out_specs=[tile_spec] * 3, out_shape=[jax.ShapeDtypeStruct((t, HP), F32)] * 3,
        scratch_shapes=[pltpu.VMEM((hb, LANES, LANES), F32)],
        compiler_params=_cp("parallel", "arbitrary"))(lg_heads, qkv, qkv, qkv, states, do)


def _sb_tile_logs(q, kb, vis):
    z = _bdot(kb, q, NT)
    ls = -(jnp.maximum(z, 0.0) + jnp.log(1.0 + jnp.exp(-jnp.abs(z))))
    if vis is not None:
        ls = jnp.where(vis, ls, 0.0)
    return z, ls


def _sb_later(ls, after):
    hi, lo = _split2(ls)
    return _bdot(after, hi) + _bdot(after, lo)


def _sb_fwd(qkv, kvt, name):
    t = qkv.shape[0]
    nq = t // BQS
    per = BQS // TKS
    hb = HB_SB_FWD

    def body(q_ref, k_ref, vt_ref, o_ref, tot_ref, acc_sc, r_sc):
        i = pl.program_id(1)
        acc_sc[...] = jnp.zeros((hb, LANES, BQS), F32)
        r_sc[...] = jnp.zeros((hb, 1, BQS), F32)
        mr, mc = _tile_iota(TKS, TKS)
        after = jnp.where(mc > mr, 1.0, 0.0).astype(BF16)

        def tile(j, qoff):
            off = pl.multiple_of(j * TKS, TKS)
            lo = 0 if qoff is None else qoff
            qsl = slice(lo, BQS)
            vis = None if qoff is None else _vis(off, i * BQS + lo, TKS, BQS - lo, "strict")
            zl = [_sb_tile_logs(q_ref[qsl, _hs(hh)], k_ref[pl.ds(off, TKS), _hs(hh)], vis) for hh in range(hb)]
            laters = [_sb_later(zl[hh][1], after) for hh in range(hb)]
            ws = []
            for hh in range(hb):
                z, ls = zl[hh]
                w = jnp.exp(z + ls + laters[hh] + r_sc[hh, :, qsl])
                if vis is not None:
                    w = jnp.where(vis, w, 0.0)
                ws.append(w.astype(BF16))
                r_sc[hh, :, qsl] += jnp.sum(ls, axis=0, keepdims=True)
            for hh in range(hb):
                acc_sc[hh, :, qsl] += _bdot(vt_ref[j, _hs(hh), :], ws[hh])

        for d in reversed(range(per)):
            tile(per * i + d, d * TKS)

        def loop(jj, carry):
            tile(per * i - 1 - jj, None)
            return carry

        lax.fori_loop(0, per * i, loop, 0)
        for hh in range(hb):
            o_ref[:, _hs(hh)] = acc_sc[hh].T
            tot_ref[hh, 0] = r_sc[hh]

    return pl.pallas_call(
        body, name=name, grid=(HEADS // hb, nq),
        in_specs=[_q_spec(hb, BQS), _kv_spec(t, 1, hb), _kvt_spec(t // TKS, TKS, 1, hb)],
        out_specs=[_q_spec(hb, BQS), _qrow_spec(hb, BQS)],
        out_shape=[jax.ShapeDtypeStruct((t, HP), F32), jax.ShapeDtypeStruct((HEADS, nq, 1, BQS), F32)],
        scratch_shapes=[pltpu.VMEM((hb, LANES, BQS), F32), pltpu.VMEM((hb, 1, BQS), F32)],
        compiler_params=_cp("parallel", "arbitrary"))(qkv, qkv, kvt)


def _sb_bwd(qkv, kvt, do, tot, name):
    t = qkv.shape[0]
    nq = t // BQS
    per = BQS // TKS
    hb = HB_BWD

    def body(q_ref, k_ref, v_ref, kt_ref, do_ref, tot_ref, dq_ref, dk_ref, dv_ref, dq_sc, p_sc, g_sc):
        i = pl.program_id(1)

        @pl.when(i == 0)
        def _():
            dk_ref[...] = jnp.zeros_like(dk_ref)
            dv_ref[...] = jnp.zeros_like(dv_ref)

        dq_sc[...] = jnp.zeros((hb, LANES, BQS), F32)
        p_sc[...] = jnp.zeros((hb, 1, BQS), F32)
        g_sc[...] = jnp.zeros((hb, 1, BQS), F32)
        mr, mc = _tile_iota(TKS, TKS)
        after = jnp.where(mc > mr, 1.0, 0.0).astype(BF16)
        before = jnp.where(mc < mr, 1.0, 0.0).astype(BF16)

        def tile(j, qoff):
            off = pl.multiple_of(j * TKS, TKS)
            lo = 0 if qoff is None else qoff
            qsl = slice(lo, BQS)
            vis = None if qoff is None else _vis(off, i * BQS + lo, TKS, BQS - lo, "strict")
            qs = [q_ref[qsl, _hs(hh)] for hh in range(hb)]
            dobs = [do_ref[qsl, _hs(hh)].astype(BF16) for hh in range(hb)]
            zl = [_sb_tile_logs(qs[hh], k_ref[pl.ds(off, TKS), _hs(hh)], vis) for hh in range(hb)]
            dws = [_bdot(v_ref[pl.ds(off, TKS), _hs(hh)], dobs[hh], NT) for hh in range(hb)]
            laters = [_sb_later(zl[hh][1], after) for hh in range(hb)]
            ws, gs = [], []
            for hh in range(hb):
                z, ls = zl[hh]
                own = jnp.sum(ls, axis=0, keepdims=True)
                rest = tot_ref[hh, 0, :, qsl] - p_sc[hh, :, qsl] - own
                w = jnp.exp(z + ls + laters[hh] + rest)
                if vis is not None:
                    w = jnp.where(vis, w, 0.0)
                p_sc[hh, :, qsl] += own
                ws.append(w.astype(BF16))
                gs.append(dws[hh] * w)
            gins = []
            for hh in range(hb):
                ghi, glo = _split2(gs[hh])
                gins.append(_bdot(before, ghi) + _bdot(before, glo))
            dzbs = []
            for hh in range(hb):
                g = gs[hh]
                stay = jnp.exp(zl[hh][1])
                dz = g * stay - (1.0 - stay) * (gins[hh] + g_sc[hh, :, qsl])
                if vis is not None:
                    dz = jnp.where(vis, dz, 0.0)
                g_sc[hh, :, qsl] += jnp.sum(g, axis=0, keepdims=True)
                dzbs.append(dz.astype(BF16))
            for hh in range(hb):
                sl = _hs(hh)
                dv_ref[pl.ds(off, TKS), sl] += _bdot(ws[hh], dobs[hh])
                dk_ref[pl.ds(off, TKS), sl] += _bdot(dzbs[hh], qs[hh])
                dq_sc[hh, :, qsl] += _bdot(kt_ref[j, sl, :], dzbs[hh])

        def loop(j, carry):
            tile(j, None)
            return carry

        lax.fori_loop(0, per * i, loop, 0)
        for d in range(per):
            tile(per * i + d, d * TKS)
        for hh in range(hb):
            dq_ref[:, _hs(hh)] = dq_sc[hh].T

    return pl.pallas_call(
        body, name=name, grid=(HEADS // hb, nq),
        in_specs=[_q_spec(hb, BQS), _kv_spec(t, 1, hb), _kv_spec(t, 2, hb), _kvt_spec(t // TKS, TKS, 0, hb),
                  _q_spec(hb, BQS), _qrow_spec(hb, BQS)],
        out_specs=[_q_spec(hb, BQS), _acc_spec(t, hb), _acc_spec(t, hb)],
        out_shape=[jax.ShapeDtypeStruct((t, HP), F32)] * 3,
        scratch_shapes=[pltpu.VMEM((hb, LANES, BQS), F32), pltpu.VMEM((hb, 1, BQS), F32),
                        pltpu.VMEM((hb, 1, BQS), F32)],
        compiler_params=_cp("parallel", "arbitrary"))(qkv, qkv, qkv, kvt, do, tot)


def _sigmoid(v):
    return 1.0 / (1.0 + jnp.exp(-v))


def _post_fwd(oa, ob, oc, od, proj, g_pad, name):
    t = oa.shape[0]

    def body(oa_ref, ob_ref, oc_ref, od_ref, rg_ref, g_ref, mx_ref):
        g = g_ref[...]

        def group(o, gg):
            r = lax.rsqrt(jnp.sum(o * o, axis=-1, keepdims=True) * (1.0 / GROUP) + EPS)
            return _gather_heads(o * r * gg).astype(BF16)

        mx_ref[:, 0:GROUP] = group(oa_ref[...], g[:, 0:HP])
        mx_ref[:, GROUP:2 * GROUP] = group(ob_ref[...], g[:, HP:2 * HP])
        mx_ref[:, 3 * GROUP:4 * GROUP] = group(od_ref[...], g[:, 3 * HP:4 * HP])
        real = lax.broadcasted_iota(jnp.int32, (TM, LANES), 1) < HEAD_DIM
        rg = _spread_heads(rg_ref[...])
        gated = []
        for hb in range(HEADS):
            sl = slice(hb * LANES, (hb + 1) * LANES)
            o = oc_ref[:, sl]
            mu = jnp.sum(o, axis=-1, keepdims=True) * (1.0 / HEAD_DIM)
            dlt = jnp.where(real, o - mu, 0.0)
            var = jnp.sum(dlt * dlt, axis=-1, keepdims=True) * (1.0 / HEAD_DIM)
            yn = dlt * lax.rsqrt(var + EPS) * g[:, 2 * HP + hb * LANES:2 * HP + (hb + 1) * LANES]
            x = rg[:, sl]
            gated.append(yn * (x * _sigmoid(x)))
        mx_ref[:, 2 * GROUP:3 * GROUP] = _gather_heads(jnp.concatenate(gated, axis=1)).astype(BF16)

    rg_spec = pl.BlockSpec((TM, GROUP), lambda i: (i, OFF_RG // GROUP))
    return pl.pallas_call(
        body, name=name, grid=(t // TM,),
        in_specs=[_row_spec(HP)] * 4 + [rg_spec, _vec_spec(4 * HP)], out_specs=_row_spec(D_MODEL),
        out_shape=jax.ShapeDtypeStruct((t, D_MODEL), BF16), compiler_params=_cp("parallel"))(oa, ob, oc, od, proj, g_pad)


def _post_bwd(dmx, oa, ob, oc, od, proj, g_pad, name):
    t = oa.shape[0]

    def body(dm_ref, oa_ref, ob_ref, oc_ref, od_ref, rg_ref, g_ref,
             doa_ref, dob_ref, doc_ref, dod_ref, dla_ref, dlb_ref, drg_ref, dg_ref):
        @pl.when(pl.program_id(0) == 0)
        def _():
            dg_ref[...] = jnp.zeros_like(dg_ref)

        g = g_ref[...]

        def group_bwd(dm, o, gg):
            r = lax.rsqrt(jnp.sum(o * o, axis=-1, keepdims=True) * (1.0 / GROUP) + EPS)
            oh = o * r
            dgp = jnp.sum(dm * oh, axis=0, keepdims=True)
            dyh = dm * gg
            do = r * (dyh - oh * (jnp.sum(dyh * oh, axis=-1, keepdims=True) * (1.0 / GROUP)))
            return do, dgp

        def delta_bc(do, o):
            prod = do * o
            lane = lax.broadcasted_iota(jnp.int32, (TM, LANES), 1)
            out = jnp.zeros((TM, LANES), F32)
            for hb in range(HEADS):
                out = jnp.where(lane == hb, jnp.sum(prod[:, hb * LANES:(hb + 1) * LANES], axis=-1, keepdims=True), out)
            return out

        dmp = [_spread_heads(dm_ref[:, gi * GROUP:(gi + 1) * GROUP]) for gi in range(4)]
        rg = _spread_heads(rg_ref[...])
        oa = oa_ref[...]
        do_a, dga = group_bwd(dmp[0], oa, g[:, 0:HP])
        doa_ref[...] = do_a
        dla_ref[...] = delta_bc(do_a, oa)
        dg_ref[:, 0:HP] += dga
        ob = ob_ref[...]
        do_b, dgb = group_bwd(dmp[1], ob, g[:, HP:2 * HP])
        dob_ref[...] = do_b
        dlb_ref[...] = delta_bc(do_b, ob)
        dg_ref[:, HP:2 * HP] += dgb
        do_d, dgd = group_bwd(dmp[3], od_ref[...], g[:, 3 * HP:4 * HP])
        dod_ref[...] = do_d
        dg_ref[:, 3 * HP:4 * HP] += dgd
        real = lax.broadcasted_iota(jnp.int32, (TM, LANES), 1) < HEAD_DIM
        for hb in range(HEADS):
            sl = slice(hb * LANES, (hb + 1) * LANES)
            gsl = slice(2 * HP + hb * LANES, 2 * HP + (hb + 1) * LANES)
            o = oc_ref[:, sl]
            mu = jnp.sum(o, axis=-1, keepdims=True) * (1.0 / HEAD_DIM)
            dlt = jnp.where(real, o - mu, 0.0)
            var = jnp.sum(dlt * dlt, axis=-1, keepdims=True) * (1.0 / HEAD_DIM)
            rstd = lax.rsqrt(var + EPS)
            dhat = dlt * rstd
            gc = g[:, gsl]
            x = rg[:, sl]
            sg = _sigmoid(x)
            dm = dmp[2][:, sl]
            drg_ref[:, sl] = dm * (dhat * gc) * (sg * (1.0 + x * (1.0 - sg)))
            dyn = dm * (x * sg)
            dg_ref[:, gsl] += jnp.sum(dyn * dhat, axis=0, keepdims=True)
            ddh = dyn * gc
            m1 = jnp.sum(ddh, axis=-1, keepdims=True) * (1.0 / HEAD_DIM)
            m2 = jnp.sum(ddh * dhat, axis=-1, keepdims=True) * (1.0 / HEAD_DIM)
            doc_ref[:, sl] = jnp.where(real, rstd * (ddh - m1 - dhat * m2), 0.0)

    rg_spec = pl.BlockSpec((TM, GROUP), lambda i: (i, OFF_RG // GROUP))
    hp = _row_spec(HP)
    return pl.pallas_call(
        body, name=name, grid=(t // TM,),
        in_specs=[_row_spec(D_MODEL), hp, hp, hp, hp, rg_spec, _vec_spec(4 * HP)],
        out_specs=[hp] * 4 + [_row_spec(LANES)] * 2 + [hp, _vec_spec(4 * HP)],
        out_shape=[jax.ShapeDtypeStruct((t, HP), F32)] * 4 + [jax.ShapeDtypeStruct((t, LANES), F32)] * 2
        + [jax.ShapeDtypeStruct((t, HP), F32), jax.ShapeDtypeStruct((1, 4 * HP), F32)],
        compiler_params=_cp("arbitrary"))(dmx, oa, ob, oc, od, proj, g_pad)


def _mesh_pos():
    return lax.axis_index("x"), lax.axis_index("y"), lax.axis_index("c")


def _peer(pos, k):
    x, y, c = pos
    px = 1 - x if (k >> 2) & 1 else x
    py = 1 - y if (k >> 1) & 1 else y
    pc = 1 - c if k & 1 else c
    return (px, py, pc), 4 * px + 2 * py + pc


def _exchange(arrs, gather, name):
    n = len(arrs)

    def body(*refs):
        ins, outs = refs[:n], refs[n:2 * n]
        send_sems, recv_sems, loc_sems = refs[2 * n:]
        pos = _mesh_pos()
        me = 4 * pos[0] + 2 * pos[1] + pos[2]
        local = []
        for a in range(n):
            src = ins[a] if gather else ins[a].at[me]
            cp = pltpu.make_async_copy(src, outs[a].at[me], loc_sems.at[a])
            cp.start()
            local.append(cp)
        sends, recvs = [], []
        for k in range(1, N_DEV):
            peer, pid = _peer(pos, k)
            for a in range(n):
                s = a * (N_DEV - 1) + k - 1
                src = ins[a] if gather else ins[a].at[pid]
                cp = pltpu.make_async_remote_copy(
                    src_ref=src, dst_ref=outs[a].at[me], send_sem=send_sems.at[s], recv_sem=recv_sems.at[s],
                    device_id=peer, device_id_type=pl.DeviceIdType.MESH)
                cp.start()
                sends.append(cp)
                recvs.append(pltpu.make_async_remote_copy(
                    src_ref=src, dst_ref=outs[a].at[pid], send_sem=send_sems.at[s], recv_sem=recv_sems.at[s],
                    device_id=peer, device_id_type=pl.DeviceIdType.MESH))
        for cp in recvs:
            cp.wait_recv()
        for cp in sends:
            cp.wait_send()
        for cp in local:
            cp.wait()

    any_spec = pl.BlockSpec(memory_space=pl.ANY)
    out_shape = [jax.ShapeDtypeStruct((N_DEV,) + tuple(a.shape) if gather else tuple(a.shape), a.dtype) for a in arrs]
    return pl.pallas_call(
        body, name=name, in_specs=[any_spec] * n, out_specs=[any_spec] * n, out_shape=out_shape,
        scratch_shapes=[pltpu.SemaphoreType.DMA((n * (N_DEV - 1),)), pltpu.SemaphoreType.DMA((n * (N_DEV - 1),)),
                        pltpu.SemaphoreType.DMA((n,))],
        compiler_params=pltpu.CompilerParams(has_side_effects=True))(*arrs)


def _device_index():
    x, y, c = _mesh_pos()
    return 4 * x + 2 * y + c


def _landing(srcs, gather):
    me = _device_index()
    lands = []
    for a in srcs:
        own = a[None] if gather else lax.dynamic_slice_in_dim(a, me, 1, axis=0)
        shape = (N_DEV,) + tuple(a.shape) if gather else tuple(a.shape)
        lands.append(lax.dynamic_update_slice_in_dim(lax.empty(shape, a.dtype), own, me, axis=0))
    return lands


def _exchange_copies(ins, lands, send_sems, recv_sems, gather):
    pos = _mesh_pos()
    me = 4 * pos[0] + 2 * pos[1] + pos[2]
    sends, recvs = [], []
    for k in range(1, N_DEV):
        peer, pid = _peer(pos, k)
        for a in range(len(ins)):
            s = a * (N_DEV - 1) + k - 1
            src = ins[a] if gather else ins[a].at[pid]
            sends.append(pltpu.make_async_remote_copy(
                src_ref=src, dst_ref=lands[a].at[me], send_sem=send_sems.at[s], recv_sem=recv_sems.at[s],
                device_id=peer, device_id_type=pl.DeviceIdType.MESH))
            recvs.append(pltpu.make_async_remote_copy(
                src_ref=src, dst_ref=lands[a].at[pid], send_sem=send_sems.at[s], recv_sem=recv_sems.at[s],
                device_id=peer, device_id_type=pl.DeviceIdType.MESH))
    return sends, recvs


def _exchange_start(srcs, gather, name, after=None):
    n = len(srcs)
    lands = _landing(srcs, gather)
    nsem = n * (N_DEV - 1)
    extra = [] if after is None else [after]

    def body(*refs):
        ins, lnd = refs[:n], refs[n:2 * n]
        send_sems, recv_sems = refs[2 * n + len(extra)], refs[2 * n + len(extra) + 1]
        token = refs[-1]
        sends, _ = _exchange_copies(ins, lnd, send_sems, recv_sems, gather)
        for cp in sends:
            cp.start()
        token[...] = jnp.zeros_like(token)

    hbm = pl.BlockSpec(memory_space=pltpu.HBM)
    sem = pl.BlockSpec(memory_space=pltpu.SEMAPHORE)
    bufs = list(srcs) + lands
    out_shape = ([pltpu.SemaphoreType.DMA((nsem,)), pltpu.SemaphoreType.DMA((nsem,))]
                 + [pltpu.HBM(b.shape, b.dtype) for b in bufs] + [jax.ShapeDtypeStruct((8, LANES), F32)])
    outs = pl.pallas_call(
        body, name=name, in_specs=[hbm] * (2 * n) + [pl.BlockSpec(memory_space=pl.ANY)] * len(extra),
        out_specs=[sem, sem] + [hbm] * (2 * n) + [pl.BlockSpec(memory_space=pltpu.VMEM)], out_shape=out_shape,
        input_output_aliases={i: 2 + i for i in range(2 * n)},
        compiler_params=pltpu.CompilerParams(has_side_effects=pltpu.SideEffectType.DATAFLOW_SIDE_EFFECTING),
    )(*[pltpu.with_memory_space_constraint(b, pltpu.HBM) for b in bufs], *extra)
    return (outs[0], outs[1], outs[2:2 + n], outs[2 + n:2 + 2 * n]), outs[-1]


def _exchange_wait(state, after, gather, name):
    send_sems, recv_sems, srcs, lands = state
    n = len(srcs)

    def body(*refs):
        ins, lnd = refs[:n], refs[n:2 * n]
        s_sems, r_sems = refs[2 * n], refs[2 * n + 1]
        sends, recvs = _exchange_copies(ins, lnd, s_sems, r_sems, gather)
        for cp in sends:
            cp.wait_send()
        for cp in recvs:
            cp.wait_recv()

    hbm = pl.BlockSpec(memory_space=pltpu.HBM)
    sem = pl.BlockSpec(memory_space=pltpu.SEMAPHORE)
    bufs = list(srcs) + list(lands)
    outs = pl.pallas_call(
        body, name=name, in_specs=[hbm] * (2 * n) + [sem, sem, pl.BlockSpec(memory_space=pl.ANY)],
        out_specs=[hbm] * (2 * n), out_shape=[pltpu.HBM(b.shape, b.dtype) for b in bufs],
        input_output_aliases={i: i for i in range(2 * n)},
        compiler_params=pltpu.CompilerParams(has_side_effects=pltpu.SideEffectType.DATAFLOW_SIDE_EFFECTING),
    )(*bufs, send_sems, recv_sems, after)
    return outs[n:]


def _adam_vals(w, g, m, v):
    m = ADAM_B1 * m + (1.0 - ADAM_B1) * g
    v = ADAM_B2 * v + (1.0 - ADAM_B2) * (g * g)
    m_hat = m / ADAM_C1
    v_hat = v / ADAM_C2
    delta = -ADAM_LR * (m_hat / (jnp.sqrt(v_hat) + ADAM_EPS) + ADAM_WD * w)
    return delta, m, v


def _small_allreduce_adam(part, w, m, v, name):
    rows = part.shape[0]

    def body(p_ref, w_ref, m_ref, v_ref, g_ref, d_ref, nm_ref, nv_ref, gath, send_sems, recv_sems):
        pos = _mesh_pos()
        me = 4 * pos[0] + 2 * pos[1] + pos[2]
        gath[me] = p_ref[...]
        sends, recvs = [], []
        for k in range(1, N_DEV):
            peer, pid = _peer(pos, k)
            cp = pltpu.make_async_remote_copy(
                src_ref=p_ref, dst_ref=gath.at[me], send_sem=send_sems.at[k - 1], recv_sem=recv_sems.at[k - 1],
                device_id=peer, device_id_type=pl.DeviceIdType.MESH)
            cp.start()
            sends.append(cp)
            recvs.append(pltpu.make_async_remote_copy(
                src_ref=p_ref, dst_ref=gath.at[pid], send_sem=send_sems.at[k - 1], recv_sem=recv_sems.at[k - 1],
                device_id=peer, device_id_type=pl.DeviceIdType.MESH))
        for cp in recvs:
            cp.wait_recv()
        for cp in sends:
            cp.wait_send()
        g = gath[0]
        for p in range(1, N_DEV):
            g = g + gath[p]
        g_ref[...] = g
        d, nm, nv = _adam_vals(w_ref[...], g, m_ref[...], v_ref[...])
        d_ref[...] = d
        nm_ref[...] = nm
        nv_ref[...] = nv

    vm = pl.BlockSpec(memory_space=pltpu.VMEM)
    sds = jax.ShapeDtypeStruct((rows, LANES), F32)
    return pl.pallas_call(
        body, name=name, in_specs=[vm] * 4, out_specs=[vm] * 4, out_shape=[sds] * 4,
        scratch_shapes=[pltpu.VMEM((N_DEV, rows, LANES), F32), pltpu.SemaphoreType.DMA((N_DEV - 1,)),
                        pltpu.SemaphoreType.DMA((N_DEV - 1,))],
        compiler_params=pltpu.CompilerParams(has_side_effects=True))(part, w, m, v)


def _reduce_adam(recv, w, m, v, name):
    shape = w.shape
    c = shape[-1]
    r = int(np.prod(shape[:-1]))
    recv2, w2, m2, v2 = recv.reshape(N_DEV, r, c), w.reshape(r, c), m.reshape(r, c), v.reshape(r, c)
    tr = r
    while tr * c * 4 > (1 << 20) and tr % 16 == 0:
        tr //= 2

    def body(r_ref, w_ref, m_ref, v_ref, g_ref, d_ref, nm_ref, nv_ref):
        g = r_ref[0].astype(F32)
        for p in range(1, N_DEV):
            g = g + r_ref[p].astype(F32)
        g_ref[...] = g
        d, nm, nv = _adam_vals(w_ref[...], g, m_ref[...], v_ref[...])
        d_ref[...] = d
        nm_ref[...] = nm
        nv_ref[...] = nv

    spec = pl.BlockSpec((tr, c), lambda i: (i, 0))
    sds = jax.ShapeDtypeStruct((r, c), F32)
    outs = pl.pallas_call(
        body, name=name, grid=(r // tr,),
        in_specs=[pl.BlockSpec((N_DEV, tr, c), lambda i: (0, i, 0)), spec, spec, spec],
        out_specs=[spec] * 4, out_shape=[sds] * 4, compiler_params=_cp("parallel"))(recv2, w2, m2, v2)
    return [o.reshape(shape) for o in outs]


def _pad_heads(w, real=HEAD_DIM):
    lead = w.shape[:-1]
    w = w.reshape(lead + (HEADS, real))
    w = jnp.pad(w, [(0, 0)] * len(lead) + [(0, 0), (0, LANES - real)])
    return w.reshape(lead + (HP,))


def _unpad_heads(w, real=HEAD_DIM):
    lead = w.shape[:-1]
    return w.reshape(lead + (HEADS, LANES))[..., :real].reshape(lead + (HEADS * real,))


_IN_SEGS = (("fq", 0, 256), ("fk", 256, 512), ("fv", 512, 768), ("ff", 768, 772), ("cq", 772, 1028),
            ("ckv", 1028, 1156), ("kr", 1156, 1188), ("rq", 1188, 1444), ("rk", 1444, 1700), ("rv", 1700, 1956),
            ("rg", 1956, 2212), ("sq", 2212, 2468), ("sk", 2468, 2724), ("sv", 2724, 2980))


def _pad_w_in(w):
    s = {n: w[:, a:b] for n, a, b in _IN_SEGS}
    rows = w.shape[0]
    z = lambda n: jnp.zeros((rows, n), w.dtype)
    parts = [s[n] for n in ("fq", "fk", "fv", "rq", "rk", "rv", "sq", "sk", "sv", "rg", "cq", "ckv")]
    parts += [z(HEAD_DIM), s["kr"], z(LANES - HEAD_DIM - ROPE_MLA), s["ff"], z(LANES - HEADS),
              z(NP_IN - OFF_FF - LANES)]
    return jnp.concatenate(parts, axis=1)


def _unpad_w_in(wp):
    seg = lambda off, n=GROUP: wp[:, off:off + n]
    parts = [seg(OFF_FOX), seg(OFF_FOX + GROUP), seg(OFF_FOX + 2 * GROUP), seg(OFF_FF, HEADS),
             seg(OFF_CQ, Q_RANK), seg(OFF_CKV, KV_RANK), seg(OFF_KR + HEAD_DIM, ROPE_MLA),
             seg(OFF_RET), seg(OFF_RET + GROUP), seg(OFF_RET + 2 * GROUP), seg(OFF_RG),
             seg(OFF_SB), seg(OFF_SB + GROUP), seg(OFF_SB + 2 * GROUP)]
    return jnp.concatenate(parts, axis=1)


def _pad_w_kv(w):
    w4 = w.reshape(KV_RANK, HEADS, 2 * HEAD_DIM)
    k = w4[:, :, :HEAD_DIM].reshape(KV_RANK, GROUP)
    v = w4[:, :, HEAD_DIM:].reshape(KV_RANK, GROUP)
    return jnp.concatenate([_pad_heads(k), _pad_heads(v)], axis=1)


def _unpad_w_kv(wp):
    k = _unpad_heads(wp[:, :HP]).reshape(KV_RANK, HEADS, HEAD_DIM)
    v = _unpad_heads(wp[:, HP:]).reshape(KV_RANK, HEADS, HEAD_DIM)
    return jnp.concatenate([k, v], axis=-1).reshape(KV_RANK, HEADS * 2 * HEAD_DIM)


def _pad_gain_out(g):
    g = jnp.pad(g.reshape(4 * HEADS, HEAD_DIM), ((0, 0), (0, LANES - HEAD_DIM)))
    return g.reshape(1, 4 * HP)


def _unpad_gain_out(gp):
    return gp.reshape(4 * HEADS, LANES)[:, :HEAD_DIM].reshape(D_MODEL)


_SMALL = (("g_mix_pre", 1024), ("g_mix_post", 1024), ("g_ffn_pre", 1024), ("g_ffn_post", 1024), ("g_mix_out", 1024),
          ("g_q_lora", 256), ("g_kv_lora", 128), ("b_forget", 4))


def _pack_small(vals):
    parts = []
    for name, n in _SMALL:
        a = vals[name].astype(F32)
        if n < LANES:
            a = jnp.pad(a, ((0, 0), (0, LANES - n)))
        parts.append(a)
    return jnp.concatenate(parts, axis=1).reshape(DEPTH * SMALL_ROWS, LANES)


def _unpack_small(packed):
    flat = packed.reshape(DEPTH, SMALL_ROWS * LANES)
    out, off = {}, 0
    for name, n in _SMALL:
        out[name] = flat[:, off:off + n]
        off += max(n, LANES)
    return out


def kernel(x, positions, g_mix_pre, w_in, b_forget, g_q_lora, w_q_up, g_kv_lora, w_kv_up, g_mix_out, w_out, g_mix_post, g_ffn_pre, w_ffn_up, w_ffn_down, g_ffn_post, loss_target, m_g_mix_pre, m_w_in, m_b_forget, m_g_q_lora, m_w_q_up, m_g_kv_lora, m_w_kv_up, m_g_mix_out, m_w_out, m_g_mix_post, m_g_ffn_pre, m_w_ffn_up, m_w_ffn_down, m_g_ffn_post, v_g_mix_pre, v_w_in, v_b_forget, v_g_q_lora, v_w_q_up, v_g_kv_lora, v_w_kv_up, v_g_mix_out, v_w_out, v_g_mix_post, v_g_ffn_pre, v_w_ffn_up, v_w_ffn_down, v_g_ffn_post):
    t = x.shape[1]
    nq = t // BQ
    x0 = x[0]
    tgt = loss_target[0]
    pos = positions[0].astype(F32).reshape(t, 1)

    half_r, half_m = HEAD_DIM // 2, ROPE_MLA // 2
    invf_r = ROPE_BASE ** (-jnp.arange(half_r, dtype=F32) / half_r)
    invf_m = ROPE_BASE ** (-jnp.arange(half_m, dtype=F32) / half_m)
    invf = jnp.concatenate([invf_r, invf_r, invf_m, invf_m,
                            jnp.zeros((LANES - HEAD_DIM - ROPE_MLA,), F32)]).reshape(1, LANES)
    log_gamma = jnp.log1p(-jnp.power(2.0, -5.0 - jnp.arange(HEADS, dtype=F32)))
    lg_lanes = jnp.repeat(log_gamma, LANES).reshape(1, HP)
    lg_heads = jnp.broadcast_to(log_gamma[:, None, None], (HEADS, 1, LANES))

    big = [w_in, w_q_up, w_kv_up, w_out, w_ffn_up, w_ffn_down]
    bf = lambda w: w.astype(BF16)
    first = _exchange([bf(w_in[0]), bf(w_q_up), bf(w_kv_up)], True, "weights_gather_first")
    l0_state, l0_token = _exchange_start([bf(w[0]) for w in (w_out, w_ffn_up, w_ffn_down)], True,
                                         "weights_gather_layer0_start")
    l1_state, rest_token = _exchange_start([bf(w[1]) for w in (w_in, w_out, w_ffn_up, w_ffn_down)], True,
                                           "weights_gather_layer1_start", after=l0_token)
    wq_g = first[1].transpose(1, 2, 0, 3).reshape(DEPTH, Q_RANK, 384)
    wkv_g = first[2].transpose(1, 2, 0, 3).reshape(DEPTH, KV_RANK, 512)

    row = lambda g: g.reshape(1, -1)
    layers = []
    for l in range(DEPTH):
        layers.append(dict(
            wq=_pad_heads(wq_g[l], 96), wkv=_pad_w_kv(wkv_g[l]),
            g_pre=row(g_mix_pre[l]), g_post=row(g_mix_post[l]), g_fpre=row(g_ffn_pre[l]), g_fpost=row(g_ffn_post[l]),
            g_out=_pad_gain_out(g_mix_out[l]), g_q=row(g_q_lora[l]), g_kv=row(g_kv_lora[l]),
            b_pad=jnp.pad(b_forget[l], (0, LANES - HEADS)).reshape(1, LANES)))
    layers[0]["win"] = _pad_w_in(first[0].reshape(D_MODEL, D_IN))

    saved = []
    xin = x0
    h = _rms_fwd(xin, layers[0]["g_pre"] + rest_token[0:1, 0:1], "rms_pre_0")
    loss_row = dx = None
    for l, p in enumerate(layers):
        s = dict(x=xin, h=h)
        proj = _mm(h, p["win"], name=f"in_proj_{l}", tm=512, tn=NP_IN)
        cum = _cumsum(_forget_lsf(proj, p["b_pad"], f"forget_lsf_{l}"), False, f"forget_cumsum_{l}")
        fox, ret, sb, mla, fox_t, sb_t, mla_t, ret_t = _prep_fwd(proj, cum, pos, invf, lg_lanes, p["g_q"],
                                                          p["g_kv"], p["wq"], p["wkv"], f"prep_fwd_{l}")
        oa, lse_a = _softmax_fwd(fox, fox_t, chunk_mask=False, scale=1.0, name=f"fox_fwd_{l}")
        ob, lse_b = _softmax_fwd(mla, mla_t, chunk_mask=True, scale=96.0 ** -0.5, name=f"mla_fwd_{l}")
        oc, ret_st = _ret_fwd(ret, ret_t, lg_heads, f"ret_fwd_{l}")
        od, sb_tot = _sb_fwd(sb, sb_t, f"sb_fwd_{l}")
        if l == 0:
            got = _exchange_wait(l0_state, od, True, "weights_gather_layer0_wait")
            p.update(wout=got[0].reshape(D_MODEL, D_MODEL), wup=got[1].transpose(1, 0, 2).reshape(D_MODEL, D_FF),
                     wdn=got[2].reshape(D_FF, D_MODEL))
        mixed = _post_fwd(oa, ob, oc, od, proj, p["g_out"], f"post_fwd_{l}")
        mix = _mm(mixed, p["wout"], name=f"out_proj_{l}")
        x1, h2 = _add_rms_fwd(xin, mix, p["g_post"], p["g_fpre"], f"mix_residual_{l}")
        a = _mm(h2, p["wup"], name=f"ffn_up_{l}")
        y = _mm(a, p["wdn"], name=f"ffn_down_{l}", a_fn=_relu2)
        s.update(proj=proj, fox=fox, ret=ret, sb=sb, mla=mla, fox_t=fox_t, sb_t=sb_t, mla_t=mla_t, ret_st=ret_st, oa=oa, ob=ob, oc=oc,
                 od=od, sb_tot=sb_tot, lse_a=lse_a, lse_b=lse_b, mixed=mixed, mix=mix, x1=x1, h2=h2, a=a, y=y)
        saved.append(s)
        if l == 0:
            got = _exchange_wait(l1_state, y, True, "weights_gather_layer1_wait")
            layers[1].update(win=_pad_w_in(got[0].reshape(D_MODEL, D_IN)), wout=got[1].reshape(D_MODEL, D_MODEL),
                             wup=got[2].transpose(1, 0, 2).reshape(D_MODEL, D_FF), wdn=got[3].reshape(D_FF, D_MODEL))
        if l + 1 < DEPTH:
            xin, h = _add_rms_fwd(x1, y, p["g_fpost"], layers[l + 1]["g_pre"], f"ffn_residual_{l}")
        else:
            loss_row, dx = _final_loss(x1, y, p["g_fpost"], tgt, "loss")

    small_g = {n: [None] * DEPTH for n, _ in _SMALL}
    big_g = [[None] * DEPTH for _ in range(6)]
    to_send = [
        lambda g: g.reshape(N_DEV, 1, D_MODEL // N_DEV, D_IN),
        lambda g: g.reshape(Q_RANK, N_DEV, 384 // N_DEV).transpose(1, 0, 2)[:, None],
        lambda g: g.reshape(KV_RANK, N_DEV, 512 // N_DEV).transpose(1, 0, 2)[:, None],
        lambda g: g.reshape(N_DEV, 1, D_MODEL // N_DEV, D_MODEL),
        lambda g: g.reshape(D_MODEL, N_DEV, D_FF // N_DEV).transpose(1, 0, 2)[:, None],
        lambda g: g.reshape(N_DEV, 1, D_FF // N_DEV, D_MODEL),
    ]
    send_of = lambda ks, l: [to_send[k](big_g[k][l]).astype(BF16) for k in ks]
    late_state = early_state = None
    order_token = jnp.zeros((1, 1), F32)
    for l in reversed(range(DEPTH)):
        p, s = layers[l], saved[l]
        dy, dg = _norm_bwd(dx, s["y"], p["g_fpost"] + order_token, None, BF16, f"ffn_post_bwd_{l}")
        small_g["g_ffn_post"][l] = dg
        da = _mm(dy, p["wdn"], name=f"ffn_down_dx_{l}", tb=True, out_dtype=BF16, epi=_drelu2, epi_in=s["a"])
        big_g[5][l] = _mm(s["a"], dy, name=f"ffn_down_dw_{l}", ta=True, a_fn=_relu2, tk=2048)
        big_g[4][l] = _mm(s["h2"], da, name=f"ffn_up_dw_{l}", ta=True, tk=2048)
        dh2 = _mm(da, p["wup"], name=f"ffn_up_dx_{l}", tb=True)
        dx1, dg = _norm_bwd(dh2, s["x1"], p["g_fpre"], dx, F32, f"ffn_pre_bwd_{l}")
        small_g["g_ffn_pre"][l] = dg
        dmix, dg = _norm_bwd(dx1, s["mix"], p["g_post"], None, BF16, f"mix_post_bwd_{l}")
        small_g["g_mix_post"][l] = dg
        dmixed = _mm(dmix, p["wout"], name=f"out_proj_dx_{l}", tb=True)
        big_g[3][l] = _mm(s["mixed"], dmix, name=f"out_proj_dw_{l}", ta=True, tk=2048)
        g_out = p["g_out"]
        if l == 0:
            early_state, early_token = _exchange_start(send_of((3, 4, 5), 0), False, "grads_layer0_early_start")
            g_out = g_out + early_token[0:1, 0:1]
        doa, dob, doc, dod, dla, dlb, drg, dgo = _post_bwd(dmixed, s["oa"], s["ob"], s["oc"], s["od"], s["proj"],
                                                           g_out, f"post_bwd_{l}")
        small_g["g_mix_out"][l] = _unpad_gain_out(dgo).reshape(1, D_MODEL)
        as_rows = lambda a: a[:, :HEADS].T.reshape(HEADS, t // BQA, 1, BQA)
        dfq, dfk, dfv, dcum_k, dcum_q = _softmax_bwd(s["fox"], s["fox_t"], doa, s["lse_a"], as_rows(dla), bias=True,
                                                     chunk_mask=False, scale=1.0, name=f"fox_bwd_{l}")
        dmq, dmk, dmv = _softmax_bwd(s["mla"], s["mla_t"], dob, s["lse_b"], as_rows(dlb), bias=False, chunk_mask=True,
                                     scale=96.0 ** -0.5, name=f"mla_bwd_{l}")
        drq, drk, drv = _ret_bwd(s["ret"], s["ret_st"], lg_heads, doc, f"ret_bwd_{l}")
        dsq, dsk, dsv = _sb_bwd(s["sb"], s["sb_t"], dod, s["sb_tot"], f"sb_bwd_{l}")
        dcum_q = jnp.pad(dcum_q.reshape(HEADS, t).T, ((0, 0), (0, LANES - HEADS)))
        dlsf = _cumsum(dcum_q, True, f"forget_cumsum_bwd_{l}", partials=dcum_k)
        dproj, dwq, dwkv, dgq, dgkv, dbf = _prep_bwd(
            (dfq, dfk, dfv), (drq, drk, drv), (dsq, dsk, dsv), (dmq, dmk, dmv), drg, dlsf, s["proj"], pos, invf,
            lg_lanes, p["b_pad"], p["g_q"], p["g_kv"], p["wq"], p["wkv"], f"prep_bwd_{l}")
        small_g["g_q_lora"][l] = dgq
        small_g["g_kv_lora"][l] = dgkv
        small_g["b_forget"][l] = dbf[:, :HEADS]
        big_g[1][l] = _unpad_heads(dwq, 96)
        big_g[2][l] = _unpad_w_kv(dwkv)
        big_g[0][l] = _unpad_w_in(_mm(s["h"], dproj, name=f"in_proj_dw_{l}", ta=True, tn=NP_IN // 2))
        dh = _mm(dproj, p["win"], name=f"in_proj_dx_{l}", tb=True, tk=NP_IN // 2)
        dx, dg = _norm_bwd(dh, s["x"], p["g_pre"], dx1, F32, f"mix_pre_bwd_{l}")
        small_g["g_mix_pre"][l] = dg
        if l == DEPTH - 1:
            late_state, late_token = _exchange_start(send_of(range(6), l), False, "grads_layer1_start")
            order_token = late_token[0:1, 0:1]
    grad_x = dx.reshape(1, t, D_MODEL)

    last_state, _ = _exchange_start(send_of((0, 1, 2), 0), False, "grads_layer0_rest_start")
    late = _exchange_wait(late_state, dx, False, "grads_layer1_wait")
    early = _exchange_wait(early_state, dx, False, "grads_layer0_early_wait")
    ms = [m_w_in, m_w_q_up, m_w_kv_up, m_w_out, m_w_ffn_up, m_w_ffn_down]
    vs = [v_w_in, v_w_q_up, v_w_kv_up, v_w_out, v_w_ffn_up, v_w_ffn_down]
    names = ["w_in", "w_q_up", "w_kv_up", "w_out", "w_ffn_up", "w_ffn_down"]
    res = {}
    for k in (3, 4, 5):
        recv = jnp.concatenate([early[k - 3], late[k]], axis=1)
        res[names[k]] = _reduce_adam(recv, big[k], ms[k], vs[k], f"adamw_{names[k]}")
    last = _exchange_wait(last_state, res["w_ffn_down"][0], False, "grads_layer0_rest_wait")
    for k in (0, 1, 2):
        recv = jnp.concatenate([last[k], late[k]], axis=1)
        res[names[k]] = _reduce_adam(recv, big[k], ms[k], vs[k], f"adamw_{names[k]}")

    small_w = dict(g_mix_pre=g_mix_pre, g_mix_post=g_mix_post, g_ffn_pre=g_ffn_pre, g_ffn_post=g_ffn_post,
                   g_mix_out=g_mix_out, g_q_lora=g_q_lora, g_kv_lora=g_kv_lora, b_forget=b_forget)
    small_m = dict(g_mix_pre=m_g_mix_pre, g_mix_post=m_g_mix_post, g_ffn_pre=m_g_ffn_pre, g_ffn_post=m_g_ffn_post,
                   g_mix_out=m_g_mix_out, g_q_lora=m_g_q_lora, g_kv_lora=m_g_kv_lora, b_forget=m_b_forget)
    small_v = dict(g_mix_pre=v_g_mix_pre, g_mix_post=v_g_mix_post, g_ffn_pre=v_g_ffn_pre, g_ffn_post=v_g_ffn_post,
                   g_mix_out=v_g_mix_out, g_q_lora=v_g_q_lora, g_kv_lora=v_g_kv_lora, b_forget=v_b_forget)
    n_small = DEPTH * SMALL_ROWS
    extra = lambda a: jnp.concatenate([a, jnp.zeros((8, LANES), F32)], axis=0)
    part = jnp.concatenate([_pack_small({n: jnp.concatenate(small_g[n], axis=0) for n, _ in _SMALL}),
                            jnp.broadcast_to(loss_row, (8, LANES))], axis=0)
    sres = _small_allreduce_adam(part, extra(_pack_small(small_w)), extra(_pack_small(small_m)),
                                 extra(_pack_small(small_v)), "small_allreduce_adamw")
    loss = sres[0][n_small, 0]
    sg, sd, sm, sv = [_unpack_small(a[:n_small]) for a in sres]
    for n, _ in _SMALL:
        res[n] = [sg[n], sd[n], sm[n], sv[n]]

    order = ["g_mix_pre", "w_in", "b_forget", "g_q_lora", "w_q_up", "g_kv_lora", "w_kv_up", "g_mix_out", "w_out",
             "g_mix_post", "g_ffn_pre", "w_ffn_up", "w_ffn_down", "g_ffn_post"]
    outs = [loss, grad_x]
    for idx in range(4):
        outs += [res[n][idx] for n in order]
    return tuple(outs)
```

```python
import functools
import math

import numpy as np
import jax
import jax.numpy as jnp
from jax import lax
from jax.experimental import pallas as pl
from jax.experimental.pallas import tpu as pltpu

F32 = jnp.float32
BF16 = jnp.bfloat16

D_MODEL = 1024
DEPTH = 2
N_DEV = 8
GROUP = 256
HEADS = 4
HEAD_DIM = 64
LANES = 128
HP = HEADS * LANES
QKV = 3 * HP
Q_RANK = 256
KV_RANK = 128
ROPE_MLA = 32
D_FF = 4096
D_IN = 2980
CHUNK_SHIFT = 6
EPS = 1e-6
ROPE_BASE = 10000.0
NEG = -1e30

QKV_IN = 3 * GROUP
OFF_FOX, OFF_RET, OFF_SB = 0, QKV_IN, 2 * QKV_IN
OFF_RG = 3 * QKV_IN
OFF_CQ = OFF_RG + GROUP
OFF_CKV = OFF_CQ + Q_RANK
OFF_KR = OFF_CKV + LANES
OFF_FF = OFF_KR + LANES
NP_IN = 3328

BQ = 256
TKS = 128
TM = 256
VMEM_LIMIT = 48 * 1024 * 1024

ADAM_LR, ADAM_B1, ADAM_B2, ADAM_EPS, ADAM_WD, ADAM_STEP = 0.001, 0.9, 0.999, 1e-08, 0.01, 10
ADAM_C1 = 1.0 - ADAM_B1 ** ADAM_STEP
ADAM_C2 = 1.0 - ADAM_B2 ** ADAM_STEP

SMALL_ROWS = 44

NT = (((1,), (1,)), ((), ()))
TN = (((0,), (0,)), ((), ()))


def _cp(*sem):
    return pltpu.CompilerParams(dimension_semantics=sem if sem else None, vmem_limit_bytes=VMEM_LIMIT)


def _bdot(a, b, dn=None):
    if dn is None:
        return jnp.dot(a, b, preferred_element_type=F32)
    return lax.dot_general(a, b, dn, preferred_element_type=F32)


def _split2(x):
    hi = x.astype(BF16)
    lo = (x - hi.astype(F32)).astype(BF16)
    return hi, lo


def _mm(a, b, *, name, ta=False, tb=False, out_dtype=F32, a_fn=None, epi=None, epi_in=None,
        tm=1024, tn=1024, tk=1024):
    m, k = (a.shape[1], a.shape[0]) if ta else a.shape
    n = b.shape[0] if tb else b.shape[1]
    tm, tn, tk = min(tm, m), min(tn, n), min(tk, k)
    assert m % tm == 0 and n % tn == 0 and k % tk == 0, (name, m, n, k)
    nk = k // tk
    dn = (((0 if ta else 1,), (1 if tb else 0,)), ((), ()))

    def body(*refs):
        if epi is None:
            a_ref, b_ref, o_ref = refs[:3]
            e_ref = None
            rest = refs[3:]
        else:
            a_ref, b_ref, e_ref, o_ref = refs[:4]
            rest = refs[4:]
        av = a_ref[...]
        if a_fn is not None:
            av = a_fn(av)
        part = lax.dot_general(av.astype(BF16), b_ref[...].astype(BF16), dn, preferred_element_type=F32)

        def finish(r):
            if epi is not None:
                r = epi(r, e_ref[...])
            o_ref[...] = r.astype(out_dtype)

        if nk == 1:
            finish(part)
        else:
            acc_ref = rest[0]
            kk = pl.program_id(2)

            @pl.when(kk == 0)
            def _():
                acc_ref[...] = part

            @pl.when(kk > 0)
            def _():
                acc_ref[...] += part

            @pl.when(kk == nk - 1)
            def _():
                finish(acc_ref[...])

    a_spec = pl.BlockSpec((tk, tm), lambda i, j, kk: (kk, i)) if ta else pl.BlockSpec((tm, tk), lambda i, j, kk: (i, kk))
    b_spec = pl.BlockSpec((tn, tk), lambda i, j, kk: (j, kk)) if tb else pl.BlockSpec((tk, tn), lambda i, j, kk: (kk, j))
    o_spec = pl.BlockSpec((tm, tn), lambda i, j, kk: (i, j))
    in_specs = [a_spec, b_spec]
    args = [a, b]
    if epi is not None:
        in_specs.append(o_spec)
        args.append(epi_in)
    return pl.pallas_call(
        body, name=name, grid=(m // tm, n // tn, nk),
        in_specs=in_specs, out_specs=o_spec,
        out_shape=jax.ShapeDtypeStruct((m, n), out_dtype),
        scratch_shapes=[pltpu.VMEM((tm, tn), F32)] if nk > 1 else [],
        compiler_params=_cp("parallel", "parallel", "arbitrary"),
    )(*args)


def _relu2(v):
    r = jnp.maximum(v, 0.0)
    return r * r


def _drelu2(du, av):
    return du * (2.0 * jnp.maximum(av, 0.0))


def _rms(v, g):
    r = lax.rsqrt(jnp.mean(v * v, axis=-1, keepdims=True) + EPS)
    return v * r * g


def _row_spec(w):
    return pl.BlockSpec((TM, w), lambda i: (i, 0))


def _vec_spec(w):
    return pl.BlockSpec((1, w), lambda i: (0, 0))


def _rms_fwd(x, g, name):
    t, d = x.shape

    def body(x_ref, g_ref, h_ref):
        h_ref[...] = _rms(x_ref[...], g_ref[...]).astype(BF16)

    return pl.pallas_call(
        body, name=name, grid=(t // TM,), in_specs=[_row_spec(d), _vec_spec(d)], out_specs=_row_spec(d),
        out_shape=jax.ShapeDtypeStruct((t, d), BF16), compiler_params=_cp("parallel"))(x, g)


def _add_rms_fwd(x, y, g1, g2, name):
    t, d = x.shape

    def body(x_ref, y_ref, g1_ref, g2_ref, xn_ref, h_ref):
        xn = x_ref[...] + _rms(y_ref[...], g1_ref[...])
        xn_ref[...] = xn
        h_ref[...] = _rms(xn, g2_ref[...]).astype(BF16)

    return pl.pallas_call(
        body, name=name, grid=(t // TM,),
        in_specs=[_row_spec(d), _row_spec(d), _vec_spec(d), _vec_spec(d)],
        out_specs=[_row_spec(d), _row_spec(d)],
        out_shape=[jax.ShapeDtypeStruct((t, d), F32), jax.ShapeDtypeStruct((t, d), BF16)],
        compiler_params=_cp("parallel"))(x, y, g1, g2)


def _final_loss(x, y, g, tgt, name):
    t, d = x.shape

    def body(x_ref, y_ref, g_ref, t_ref, l_ref, dx_ref):
        @pl.when(pl.program_id(0) == 0)
        def _():
            l_ref[...] = jnp.zeros_like(l_ref)

        err = x_ref[...] + _rms(y_ref[...], g_ref[...]) - t_ref[...]
        dx_ref[...] = err * (1.0 / d)
        l_ref[...] += jnp.sum(jnp.sum(err * err, axis=1, keepdims=True), axis=0, keepdims=True) * (0.5 / d)

    return pl.pallas_call(
        body, name=name, grid=(t // TM,),
        in_specs=[_row_spec(d), _row_spec(d), _vec_spec(d), _row_spec(d)],
        out_specs=[pl.BlockSpec((1, LANES), lambda i: (0, 0)), _row_spec(d)],
        out_shape=[jax.ShapeDtypeStruct((1, LANES), F32), jax.ShapeDtypeStruct((t, d), F32)],
        compiler_params=_cp("arbitrary"))(x, y, g, tgt)


def _rms_bwd_vals(dn, v, g):
    w = v.shape[-1]
    r = lax.rsqrt(jnp.mean(v * v, axis=-1, keepdims=True) + EPS)
    vh = v * r
    dgp = jnp.sum(dn * vh, axis=0, keepdims=True)
    dvh = dn * g
    dv = r * (dvh - vh * (jnp.sum(dvh * vh, axis=-1, keepdims=True) * (1.0 / w)))
    return dv, dgp


def _norm_bwd(dn, v, g, resid, out_dtype, name):
    t, d = v.shape
    has_res = resid is not None

    def body(*refs):
        if has_res:
            dn_ref, v_ref, g_ref, r_ref, dv_ref, dg_ref = refs
        else:
            dn_ref, v_ref, g_ref, dv_ref, dg_ref = refs

        @pl.when(pl.program_id(0) == 0)
        def _():
            dg_ref[...] = jnp.zeros_like(dg_ref)

        dv, dgp = _rms_bwd_vals(dn_ref[...].astype(F32), v_ref[...], g_ref[...])
        if has_res:
            dv = dv + r_ref[...]
        dv_ref[...] = dv.astype(out_dtype)
        dg_ref[...] += dgp

    in_specs = [_row_spec(d), _row_spec(d), _vec_spec(d)] + ([_row_spec(d)] if has_res else [])
    args = [dn, v, g] + ([resid] if has_res else [])
    return pl.pallas_call(
        body, name=name, grid=(t // TM,), in_specs=in_specs,
        out_specs=[_row_spec(d), _vec_spec(d)],
        out_shape=[jax.ShapeDtypeStruct((t, d), out_dtype), jax.ShapeDtypeStruct((1, d), F32)],
        compiler_params=_cp("arbitrary"))(*args)


def _rope_trig(pos, invf):
    ang = pos * invf
    return jnp.cos(ang), jnp.sin(ang)


def _rope_tables(trig, w, lo, half):
    c, s = trig
    lane = lax.broadcasted_iota(jnp.int32, c.shape, 1)
    active = (lane >= lo) & (lane < lo + 2 * half)
    cos = jnp.concatenate([jnp.where(active, c, 1.0)] * (w // LANES), axis=1)
    sin = jnp.concatenate([jnp.where(active, s, 0.0)] * (w // LANES), axis=1)
    lanes = lax.broadcasted_iota(jnp.int32, (c.shape[0], w), 1) & (LANES - 1)
    first = (lanes >= lo) & (lanes < lo + half)
    second = (lanes >= lo + half) & (lanes < lo + 2 * half)
    return cos, sin, first, second


def _rope_apply(v, cos, sin, first, second, half, sign):
    w = v.shape[-1]
    up = pltpu.roll(v, w - half, 1)
    dn = pltpu.roll(v, half, 1)
    rot = jnp.where(first, -up, jnp.where(second, dn, 0.0))
    return v * cos + rot * (sin * sign)


def _forget_lsf(proj, b_pad, name):
    t = proj.shape[0]

    def body(ff_ref, b_ref, o_ref):
        f = ff_ref[...] + b_ref[...]
        o_ref[...] = -(jnp.maximum(-f, 0.0) + jnp.log(1.0 + jnp.exp(-jnp.abs(f))))

    return pl.pallas_call(
        body, name=name, grid=(t // TM,),
        in_specs=[pl.BlockSpec((TM, LANES), lambda i: (i, OFF_FF // LANES)), _vec_spec(LANES)],
        out_specs=_row_spec(LANES), out_shape=jax.ShapeDtypeStruct((t, LANES), F32),
        compiler_params=_cp("parallel"))(proj, b_pad)


def _split3(c):
    hi = c.astype(BF16).astype(F32)
    mid = (c - hi).astype(BF16).astype(F32)
    return hi, mid, (c - hi) - mid


def _spread_heads(x):
    low = lax.broadcasted_iota(jnp.int32, (x.shape[0], LANES), 1) < HEAD_DIM
    out = []
    for c in range(x.shape[1] // LANES):
        blk = x[:, c * LANES:(c + 1) * LANES]
        out.append(jnp.where(low, blk, 0.0))
        out.append(jnp.where(low, pltpu.roll(blk, HEAD_DIM, 1), 0.0))
    return jnp.concatenate(out, axis=1)


def _gather_heads(y):
    low = lax.broadcasted_iota(jnp.int32, (y.shape[0], LANES), 1) < HEAD_DIM
    out = []
    for c in range(y.shape[1] // (2 * LANES)):
        a = y[:, 2 * c * LANES:(2 * c + 1) * LANES]
        b = y[:, (2 * c + 1) * LANES:(2 * c + 2) * LANES]
        out.append(jnp.where(low, a, pltpu.roll(b, HEAD_DIM, 1)))
    return jnp.concatenate(out, axis=1)


def _transposed_tiles(dst, row0, blocks, width):
    for b, blk in enumerate(blocks):
        bt = blk.T.astype(BF16)
        rows = bt.shape[0]
        for w in range(TM // width):
            dst[w, row0 + b * rows:row0 + (b + 1) * rows, :] = bt[:, w * width:(w + 1) * width]


def _prep_fwd(proj, cum, pos, invf, lg_lanes, g_q, g_kv, wq_pad, wkv_pad, name):
    t = proj.shape[0]

    def body(fox_ref, ret_ref, sb_ref, cq_ref, ckv_ref, kr_ref, cum_ref, pos_ref, invf_ref, lg_ref,
             gq_ref, gkv_ref, wq_ref, wkv_ref,
             ofox_ref, oret_ref, osb_ref, omla_ref, ofoxt_ref, osbt_ref, omlat_ref, orett_ref):
        pos_v = pos_ref[...]
        foxv, retv, sbv = [[_spread_heads(ref[:, g * GROUP:(g + 1) * GROUP]) for g in range(3)]
                           for ref in (fox_ref, ret_ref, sb_ref)]
        osb_ref[:, 0:HP] = (sbv[0] * 0.125).astype(BF16)
        osb_ref[:, HP:2 * HP] = sbv[1].astype(BF16)
        osb_ref[:, 2 * HP:QKV] = sbv[2].astype(BF16)
        _transposed_tiles(osbt_ref, 0, [sbv[1], sbv[2]], TKS)
        lane = lax.broadcasted_iota(jnp.int32, (TM, LANES), 1)
        cumv = cum_ref[...]
        fq, fk = [], []
        for hb in range(HEADS):
            hi, mid, lo = _split3(cumv[:, hb:hb + 1])
            q = foxv[0][:, hb * LANES:(hb + 1) * LANES] * 0.125
            k = foxv[1][:, hb * LANES:(hb + 1) * LANES]
            ones_q = (lane >= HEAD_DIM) & (lane < HEAD_DIM + 3)
            ones_k = (lane >= HEAD_DIM + 3) & (lane < HEAD_DIM + 6)
            q = jnp.where(ones_q, 1.0, q)
            k = jnp.where(ones_k, 1.0, k)
            for n, part in enumerate((hi, mid, lo)):
                q = jnp.where(lane == HEAD_DIM + 3 + n, part, q)
                k = jnp.where(lane == HEAD_DIM + n, -part, k)
            fq.append(q)
            fk.append(k)
        fk = jnp.concatenate(fk, axis=1)
        ofox_ref[:, 0:HP] = jnp.concatenate(fq, axis=1).astype(BF16)
        ofox_ref[:, HP:2 * HP] = fk.astype(BF16)
        ofox_ref[:, 2 * HP:QKV] = foxv[2].astype(BF16)
        _transposed_tiles(ofoxt_ref, 0, [fk, foxv[2]], BQ)
        trig = _rope_trig(pos_v, invf_ref[...])
        cos, sin, first, second = _rope_tables(trig, HP, 0, HEAD_DIM // 2)
        nloc = lax.broadcasted_iota(jnp.int32, (TM, 1), 0).astype(F32)
        dec = lg_ref[...] * nloc
        rq = _rope_apply(retv[0], cos, sin, first, second, HEAD_DIM // 2, 1.0)
        rk = _rope_apply(retv[1], cos, sin, first, second, HEAD_DIM // 2, 1.0)
        oret_ref[:, 0:HP] = (rq * jnp.exp(dec)).astype(BF16)
        rk = rk * 0.125 * jnp.exp(-dec)
        oret_ref[:, HP:2 * HP] = rk.astype(BF16)
        _transposed_tiles(orett_ref, 0, [rk], BQ)
        oret_ref[:, 2 * HP:QKV] = retv[2].astype(BF16)
        cosm, sinm, firstm, secondm = _rope_tables(trig, HP, HEAD_DIM, ROPE_MLA // 2)
        cqn = _rms(cq_ref[...], gq_ref[...]).astype(BF16)
        qm = _bdot(cqn, wq_ref[...])
        omla_ref[:, 0:HP] = _rope_apply(qm, cosm, sinm, firstm, secondm, ROPE_MLA // 2, 1.0).astype(BF16)
        ckvn = _rms(ckv_ref[...], gkv_ref[...]).astype(BF16)
        kv = _bdot(ckvn, wkv_ref[...])
        krr = _rope_apply(kr_ref[...], cosm[:, 0:LANES], sinm[:, 0:LANES], firstm[:, 0:LANES],
                          secondm[:, 0:LANES], ROPE_MLA // 2, 1.0)
        mk = kv[:, 0:HP] + jnp.concatenate([krr] * HEADS, axis=1)
        omla_ref[:, HP:2 * HP] = mk.astype(BF16)
        omla_ref[:, 2 * HP:QKV] = kv[:, HP:2 * HP].astype(BF16)
        _transposed_tiles(omlat_ref, 0, [mk, kv[:, HP:2 * HP]], BQ)

    def seg(off, w):
        return pl.BlockSpec((TM, w), lambda i, o=off // w: (i, o))

    def full(shape):
        return pl.BlockSpec(shape, lambda i: (0,) * len(shape))

    def tiles(width):
        return pl.BlockSpec((TM // width, 2 * HP, width), lambda i: (i, 0, 0))

    in_specs = [seg(OFF_FOX, QKV_IN), seg(OFF_RET, QKV_IN), seg(OFF_SB, QKV_IN), seg(OFF_CQ, Q_RANK),
                seg(OFF_CKV, LANES), seg(OFF_KR, LANES), _row_spec(LANES), pl.BlockSpec((TM, 1), lambda i: (i, 0)),
                full((1, LANES)), full((1, HP)), full((1, Q_RANK)), full((1, KV_RANK)),
                full((Q_RANK, HP)), full((KV_RANK, 2 * HP))]
    out_specs = [_row_spec(QKV)] * 4 + [tiles(BQ), tiles(TKS), tiles(BQ),
                                        pl.BlockSpec((1, HP, BQ), lambda i: (i, 0, 0))]
    out_shape = [jax.ShapeDtypeStruct((t, QKV), BF16)] * 4 + [
        jax.ShapeDtypeStruct((t // BQ, 2 * HP, BQ), BF16), jax.ShapeDtypeStruct((t // TKS, 2 * HP, TKS), BF16),
        jax.ShapeDtypeStruct((t // BQ, 2 * HP, BQ), BF16), jax.ShapeDtypeStruct((t // BQ, HP, BQ), BF16)]
    return pl.pallas_call(
        body, name=name, grid=(t // TM,), in_specs=in_specs, out_specs=out_specs, out_shape=out_shape,
        compiler_params=_cp("parallel"))(proj, proj, proj, proj, proj, proj, cum, pos, invf, lg_lanes,
                                         g_q, g_kv, wq_pad, wkv_pad)


def _prep_bwd(dfox, dret, dsb, dmla, drg, dlsf, proj, pos, invf, lg_lanes, b_pad, g_q, g_kv,
              wq_pad, wkv_pad, name):
    t = proj.shape[0]

    def body(dfq, dfk, dfv, drq, drk, drv, dsq, dsk, dsv, dmq, dmk, dmv, drg_ref, dlsf_ref,
             cq_ref, ckv_ref, ff_ref, pos_ref, invf_ref, lg_ref, b_ref, gq_ref, gkv_ref, wq_ref, wkv_ref,
             dp_ref, dwq_ref, dwkv_ref, dgq_ref, dgkv_ref, dbf_ref):
        @pl.when(pl.program_id(0) == 0)
        def _():
            dwq_ref[...] = jnp.zeros_like(dwq_ref)
            dwkv_ref[...] = jnp.zeros_like(dwkv_ref)
            dgq_ref[...] = jnp.zeros_like(dgq_ref)
            dgkv_ref[...] = jnp.zeros_like(dgkv_ref)
            dbf_ref[...] = jnp.zeros_like(dbf_ref)

        pos_v = pos_ref[...]
        def put(off, val):
            dp_ref[:, off:off + GROUP] = _gather_heads(val).astype(BF16)

        for off, (dq, dk, dv) in ((OFF_FOX, (dfq, dfk, dfv)), (OFF_SB, (dsq, dsk, dsv))):
            put(off, dq[...] * 0.125)
            put(off + GROUP, dk[...])
            put(off + 2 * GROUP, dv[...])
        trig = _rope_trig(pos_v, invf_ref[...])
        cos, sin, first, second = _rope_tables(trig, HP, 0, HEAD_DIM // 2)
        nloc = lax.broadcasted_iota(jnp.int32, (TM, 1), 0).astype(F32)
        dec = lg_ref[...] * nloc
        dq = _rope_apply(drq[...] * jnp.exp(dec), cos, sin, first, second, HEAD_DIM // 2, -1.0)
        dk = _rope_apply(drk[...] * (0.125 * jnp.exp(-dec)), cos, sin, first, second, HEAD_DIM // 2, -1.0)
        put(OFF_RET, dq)
        put(OFF_RET + GROUP, dk)
        put(OFF_RET + 2 * GROUP, drv[...])
        put(OFF_RG, drg_ref[...])
        cosm, sinm, firstm, secondm = _rope_tables(trig, HP, HEAD_DIM, ROPE_MLA // 2)
        dql = _rope_apply(dmq[...], cosm, sinm, firstm, secondm, ROPE_MLA // 2, -1.0).astype(BF16)
        cq = cq_ref[...]
        cqn = _rms(cq, gq_ref[...]).astype(BF16)
        dwq_ref[...] += _bdot(cqn, dql, TN)
        dcqn = _bdot(dql, wq_ref[...], NT)
        dcq, dgq = _rms_bwd_vals(dcqn, cq, gq_ref[...])
        dgq_ref[...] += dgq
        dp_ref[:, OFF_CQ:OFF_CQ + Q_RANK] = dcq.astype(BF16)
        dkm = dmk[...]
        dkv = jnp.concatenate([dkm, dmv[...]], axis=1).astype(BF16)
        ckv = ckv_ref[...]
        ckvn = _rms(ckv, gkv_ref[...]).astype(BF16)
        dwkv_ref[...] += _bdot(ckvn, dkv, TN)
        dckvn = _bdot(dkv, wkv_ref[...], NT)
        dckv, dgkv = _rms_bwd_vals(dckvn, ckv, gkv_ref[...])
        dgkv_ref[...] += dgkv
        dp_ref[:, OFF_CKV:OFF_CKV + LANES] = dckv.astype(BF16)
        dkr = dkm[:, 0:LANES] + dkm[:, LANES:2 * LANES] + dkm[:, 2 * LANES:3 * LANES] + dkm[:, 3 * LANES:HP]
        act = firstm[:, 0:LANES] | secondm[:, 0:LANES]
        dkr = jnp.where(act, dkr, 0.0)
        dkr = _rope_apply(dkr, cosm[:, 0:LANES], sinm[:, 0:LANES], firstm[:, 0:LANES], secondm[:, 0:LANES],
                          ROPE_MLA // 2, -1.0)
        dp_ref[:, OFF_KR:OFF_KR + LANES] = dkr.astype(BF16)
        f = ff_ref[...] + b_ref[...]
        dff = dlsf_ref[...] / (1.0 + jnp.exp(f))
        dbf_ref[...] += jnp.sum(dff, axis=0, keepdims=True)
        dp_ref[:, OFF_FF:OFF_FF + LANES] = dff.astype(BF16)
        dp_ref[:, OFF_FF + LANES:NP_IN] = jnp.zeros((TM, NP_IN - OFF_FF - LANES), BF16)

    def seg(off, w):
        return pl.BlockSpec((TM, w), lambda i, o=off // w: (i, o))

    def full(shape):
        return pl.BlockSpec(shape, lambda i: (0,) * len(shape))

    hp_spec = _row_spec(HP)
    in_specs = [hp_spec] * 13 + [_row_spec(LANES), seg(OFF_CQ, Q_RANK), seg(OFF_CKV, LANES), seg(OFF_FF, LANES),
                                 pl.BlockSpec((TM, 1), lambda i: (i, 0)),
                                 full((1, LANES)), full((1, HP)), full((1, LANES)), full((1, Q_RANK)),
                                 full((1, KV_RANK)), full((Q_RANK, HP)), full((KV_RANK, 2 * HP))]
    out_specs = [_row_spec(NP_IN), full((Q_RANK, HP)), full((KV_RANK, 2 * HP)), full((1, Q_RANK)),
                 full((1, KV_RANK)), full((1, LANES))]
    out_shape = [jax.ShapeDtypeStruct((t, NP_IN), BF16), jax.ShapeDtypeStruct((Q_RANK, HP), F32),
                 jax.ShapeDtypeStruct((KV_RANK, 2 * HP), F32), jax.ShapeDtypeStruct((1, Q_RANK), F32),
                 jax.ShapeDtypeStruct((1, KV_RANK), F32), jax.ShapeDtypeStruct((1, LANES), F32)]
    return pl.pallas_call(
        body, name=name, grid=(t // TM,), in_specs=in_specs, out_specs=out_specs, out_shape=out_shape,
        compiler_params=_cp("arbitrary"))(*dfox, *dret, *dsb, *dmla, drg, dlsf, proj, proj, proj, pos, invf,
                                          lg_lanes, b_pad, g_q, g_kv, wq_pad, wkv_pad)


def _cumsum(x, reverse, name, partials=None):
    t, w = x.shape
    n = t // TM
    xs = [x] if partials is None else [x, partials]

    def body(*refs):
        x_refs, o_ref, carry = refs[:len(xs)], refs[len(xs)], refs[len(xs) + 1]

        @pl.when(pl.program_id(0) == 0)
        def _():
            carry[...] = jnp.zeros_like(carry)

        r = lax.broadcasted_iota(jnp.int32, (TM, TM), 0)
        c = lax.broadcasted_iota(jnp.int32, (TM, TM), 1)
        tri = jnp.where((r <= c) if reverse else (r >= c), 1.0, 0.0).astype(BF16)
        v = x_refs[0][...]
        if partials is not None:
            lane = lax.broadcasted_iota(jnp.int32, (TM, LANES), 1)
            for hb in range(HEADS):
                v = v + jnp.where(lane == hb, jnp.sum(x_refs[1][:, _hs(hb)], axis=1, keepdims=True), 0.0)
        hi = v.astype(BF16)
        r1 = v - hi.astype(F32)
        mid = r1.astype(BF16)
        lo = (r1 - mid.astype(F32)).astype(BF16)
        cs = _bdot(tri, hi) + _bdot(tri, mid) + _bdot(tri, lo) + carry[...]
        o_ref[...] = cs
        carry[...] = cs[0:1, :] if reverse else cs[TM - 1:TM, :]

    imap = (lambda i: (n - 1 - i, 0)) if reverse else (lambda i: (i, 0))
    return pl.pallas_call(
        body, name=name, grid=(n,), in_specs=[pl.BlockSpec((TM, a.shape[1]), imap) for a in xs],
        out_specs=pl.BlockSpec((TM, w), imap),
        out_shape=jax.ShapeDtypeStruct((t, w), F32), scratch_shapes=[pltpu.VMEM((1, w), F32)],
        compiler_params=_cp("arbitrary"))(*xs)


HB_FWD = 4
HB_BWD = 2
HB_SB_FWD = 4
BQS = 512
BQA = 512


def _q_spec(hb, bq=BQ):
    return pl.BlockSpec((bq, hb * LANES), lambda g, i: (i, g))


def _kv_spec(t, which, hb):
    return pl.BlockSpec((t, hb * LANES), lambda g, i, w=which: (0, w * (HEADS // hb) + g))


def _acc_spec(t, hb):
    return pl.BlockSpec((t, hb * LANES), lambda g, i: (0, g))


def _hs(hh):
    return slice(hh * LANES, (hh + 1) * LANES)


def _tile_iota(rows, cols):
    return (lax.broadcasted_iota(jnp.int32, (rows, cols), 0), lax.broadcasted_iota(jnp.int32, (rows, cols), 1))


def _kvt_spec(nkv, width, which, hb):
    return pl.BlockSpec((nkv, hb * LANES, width), lambda g, i, w=which: (0, w * (HEADS // hb) + g, 0))


def _qrow_spec(hb, bq=BQ):
    return pl.BlockSpec((hb, 1, 1, bq), lambda g, i: (g, i, 0, 0))


def _vis(key0, query0, rows, cols, kind):
    r, c = _tile_iota(rows, cols)
    k, q = key0 + r, query0 + c
    if kind == "chunk":
        return (k >> CHUNK_SHIFT) <= (q >> CHUNK_SHIFT)
    return (k < q) if kind == "strict" else (k <= q)


def _softmax_fwd(qkv, kvt, *, chunk_mask, scale, name):
    t = qkv.shape[0]
    nq = t // BQA
    per = BQA // BQ
    hb = HB_FWD
    kind = "chunk" if chunk_mask else "causal"

    def body(q_ref, k_ref, vt_ref, o_ref, lse_ref, m_sc, l_sc, acc_sc):
        i = pl.program_id(1)
        m_sc[...] = jnp.full((hb, 1, BQA), NEG, F32)
        l_sc[...] = jnp.zeros((hb, 1, BQA), F32)
        acc_sc[...] = jnp.zeros((hb, LANES, BQA), F32)

        def tile(j, qoff):
            off = pl.multiple_of(j * BQ, BQ)
            lo = 0 if qoff is None else qoff
            qs = slice(lo, BQA)
            vis = None if qoff is None else _vis(off, i * BQA + lo, BQ, BQA - lo, kind)
            ss = [_bdot(k_ref[pl.ds(off, BQ), _hs(hh)], q_ref[qs, _hs(hh)], NT) for hh in range(hb)]
            ps, alphas = [], []
            for hh in range(hb):
                s = ss[hh]
                if scale != 1.0:
                    s = s * scale
                if vis is not None:
                    s = jnp.where(vis, s, NEG)
                m_old = m_sc[hh, :, qs]
                m_new = jnp.maximum(m_old, jnp.max(s, axis=0, keepdims=True))
                alpha = jnp.exp(m_old - m_new)
                p = jnp.exp(s - m_new)
                l_sc[hh, :, qs] = alpha * l_sc[hh, :, qs] + jnp.sum(p, axis=0, keepdims=True)
                m_sc[hh, :, qs] = m_new
                ps.append(p.astype(BF16))
                alphas.append(alpha)
            for hh in range(hb):
                acc_sc[hh, :, qs] = alphas[hh] * acc_sc[hh, :, qs] + _bdot(vt_ref[j, _hs(hh), :], ps[hh])

        def loop(j, carry):
            tile(j, None)
            return carry

        lax.fori_loop(0, per * i, loop, 0)
        for d in range(per):
            tile(per * i + d, d * BQ)
        for hh in range(hb):
            l = l_sc[hh]
            o_ref[:, _hs(hh)] = (acc_sc[hh] / l).T
            lse_ref[hh, 0] = m_sc[hh] + jnp.log(l)

    return pl.pallas_call(
        body, name=name, grid=(HEADS // hb, nq),
        in_specs=[_q_spec(hb, BQA), _kv_spec(t, 1, hb), _kvt_spec(t // BQ, BQ, 1, hb)],
        out_specs=[_q_spec(hb, BQA), _qrow_spec(hb, BQA)],
        out_shape=[jax.ShapeDtypeStruct((t, HP), F32), jax.ShapeDtypeStruct((HEADS, nq, 1, BQA), F32)],
        scratch_shapes=[pltpu.VMEM((hb, 1, BQA), F32), pltpu.VMEM((hb, 1, BQA), F32),
                        pltpu.VMEM((hb, LANES, BQA), F32)],
        compiler_params=_cp("parallel", "arbitrary"))(qkv, qkv, kvt)


def _softmax_bwd(qkv, kvt, do, lse, delta, *, bias, chunk_mask, scale, name):
    t = qkv.shape[0]
    nq = t // BQA
    per = BQA // BQ
    hb = HB_BWD
    kind = "chunk" if chunk_mask else "causal"

    def body(*refs):
        if bias:
            (q_ref, k_ref, v_ref, kt_ref, do_ref, lse_ref, dl_ref, dq_ref, dk_ref, dv_ref, dck_ref, dcq_ref,
             dq_sc, dcq_sc) = refs
            dcq_sc[...] = jnp.zeros((hb, 1, BQA), F32)
        else:
            q_ref, k_ref, v_ref, kt_ref, do_ref, lse_ref, dl_ref, dq_ref, dk_ref, dv_ref, dq_sc = refs
        i = pl.program_id(1)

        @pl.when(i == 0)
        def _():
            dk_ref[...] = jnp.zeros_like(dk_ref)
            dv_ref[...] = jnp.zeros_like(dv_ref)
            if bias:
                dck_ref[...] = jnp.zeros_like(dck_ref)

        dq_sc[...] = jnp.zeros((hb, LANES, BQA), F32)

        def tile(j, qoff):
            off = pl.multiple_of(j * BQ, BQ)
            lo = 0 if qoff is None else qoff
            qsl = slice(lo, BQA)
            vis = None if qoff is None else _vis(off, i * BQA + lo, BQ, BQA - lo, kind)
            qs = [q_ref[qsl, _hs(hh)] for hh in range(hb)]
            dobs = [do_ref[qsl, _hs(hh)].astype(BF16) for hh in range(hb)]
            ss = [_bdot(k_ref[pl.ds(off, BQ), _hs(hh)], qs[hh], NT) for hh in range(hb)]
            dps = [_bdot(v_ref[pl.ds(off, BQ), _hs(hh)], dobs[hh], NT) for hh in range(hb)]
            pbs, dsbs = [], []
            for hh in range(hb):
                s = ss[hh]
                if scale != 1.0:
                    s = s * scale
                p = jnp.exp(s - lse_ref[hh, 0, :, qsl])
                if vis is not None:
                    p = jnp.where(vis, p, 0.0)
                ds = p * (dps[hh] - dl_ref[hh, 0, :, qsl])
                if bias:
                    part = ds[:, 0:LANES]
                    for b in range(1, (BQA - lo) // LANES):
                        part = part + ds[:, b * LANES:(b + 1) * LANES]
                    dck_ref[pl.ds(off, BQ), _hs(hh)] -= part
                    dcq_sc[hh, :, qsl] += jnp.sum(ds, axis=0, keepdims=True)
                if scale != 1.0:
                    ds = ds * scale
                pbs.append(p.astype(BF16))
                dsbs.append(ds.astype(BF16))
            for hh in range(hb):
                sl = _hs(hh)
                dv_ref[pl.ds(off, BQ), sl] += _bdot(pbs[hh], dobs[hh])
                dk_ref[pl.ds(off, BQ), sl] += _bdot(dsbs[hh], qs[hh])
                dq_sc[hh, :, qsl] += _bdot(kt_ref[j, sl, :], dsbs[hh])

        def loop(j, carry):
            tile(j, None)
            return carry

        lax.fori_loop(0, per * i, loop, 0)
        for d in range(per):
            tile(per * i + d, d * BQ)
        for hh in range(hb):
            dq_ref[:, _hs(hh)] = dq_sc[hh].T
            if bias:
                dcq_ref[hh, 0] = dcq_sc[hh]

    in_specs = [_q_spec(hb, BQA), _kv_spec(t, 1, hb), _kv_spec(t, 2, hb), _kvt_spec(t // BQ, BQ, 0, hb),
                _q_spec(hb, BQA), _qrow_spec(hb, BQA), _qrow_spec(hb, BQA)]
    out_specs = [_q_spec(hb, BQA), _acc_spec(t, hb), _acc_spec(t, hb)]
    out_shape = [jax.ShapeDtypeStruct((t, HP), F32)] * 3
    scratch = [pltpu.VMEM((hb, LANES, BQA), F32)]
    if bias:
        out_specs += [_acc_spec(t, hb), _qrow_spec(hb, BQA)]
        out_shape += [jax.ShapeDtypeStruct((t, HP), F32), jax.ShapeDtypeStruct((HEADS, nq, 1, BQA), F32)]
        scratch.append(pltpu.VMEM((hb, 1, BQA), F32))
    return pl.pallas_call(
        body, name=name, grid=(HEADS // hb, nq), in_specs=in_specs, out_specs=out_specs, out_shape=out_shape,
        scratch_shapes=scratch,
        compiler_params=_cp("parallel", "arbitrary"))(qkv, qkv, qkv, kvt, do, lse, delta)


def _ret_diag_decay(lg1, keys_on_rows=False):
    r, c = _tile_iota(BQ, BQ)
    qn, km = (c, r) if keys_on_rows else (r, c)
    dd = jnp.where(km > qn, jnp.exp((2.0 * lg1) * (km - qn).astype(F32)), 1.0)
    return jnp.where((km >> CHUNK_SHIFT) <= (qn >> CHUNK_SHIFT), dd, 0.0)


def _lg_spec(hb):
    return pl.BlockSpec((hb, 1, LANES), lambda g, i: (g, 0, 0))


def _ret_specs(nq, hb, reverse):
    tile = (lambda i: nq - 1 - i) if reverse else (lambda i: i)
    qkv = [pl.BlockSpec((BQ, hb * LANES), lambda g, i, w=w: (tile(i), w * (HEADS // hb) + g)) for w in range(3)]
    kt = pl.BlockSpec((1, hb * LANES, BQ), lambda g, i: (tile(i), g, 0))
    st = pl.BlockSpec((1, hb * LANES, LANES), lambda g, i: (tile(i), g, 0))
    return qkv, kt, st


def _ret_fwd(qkv, kt, lg_heads, name):
    t = qkv.shape[0]
    nq = t // BQ
    hb = HB_FWD

    def body(lg_ref, q_ref, k_ref, v_ref, kt_ref, o_ref, st_ref, s_sc):
        @pl.when(pl.program_id(1) == 0)
        def _():
            s_sc[...] = jnp.zeros_like(s_sc)

        qs = [q_ref[:, _hs(hh)] for hh in range(hb)]
        vs = [v_ref[:, _hs(hh)] for hh in range(hb)]
        aa = [_bdot(qs[hh], k_ref[:, _hs(hh)], NT) for hh in range(hb)]
        kv = [_bdot(kt_ref[0, _hs(hh), :], vs[hh]) for hh in range(hb)]
        for hh in range(hb):
            sl = _hs(hh)
            lg1 = lg_ref[hh][:, 0:1]
            s = s_sc[hh]
            st_ref[0, sl, :] = s
            shi, slo = _split2(s)
            a = (aa[hh] * _ret_diag_decay(lg1)).astype(BF16)
            o_ref[:, sl] = _bdot(a, vs[hh]) + _bdot(qs[hh], shi) + _bdot(qs[hh], slo)
            s_sc[hh] = jnp.exp(lg1 * float(BQ)) * (s + kv[hh])

    qkv_specs, kt_spec, st_spec = _ret_specs(nq, hb, False)
    return pl.pallas_call(
        body, name=name, grid=(HEADS // hb, nq), in_specs=[_lg_spec(hb)] + qkv_specs + [kt_spec],
        out_specs=[_q_spec(hb), st_spec],
        out_shape=[jax.ShapeDtypeStruct((t, HP), F32), jax.ShapeDtypeStruct((nq, HP, LANES), F32)],
        scratch_shapes=[pltpu.VMEM((hb, LANES, LANES), F32)],
        compiler_params=_cp("parallel", "arbitrary"))(lg_heads, qkv, qkv, qkv, kt)


def _ret_bwd(qkv, states, lg_heads, do, name):
    t = qkv.shape[0]
    nq = t // BQ
    hb = HB_FWD

    def body(lg_ref, q_ref, k_ref, v_ref, st_ref, do_ref, dq_ref, dk_ref, dv_ref, g_sc):
        @pl.when(pl.program_id(1) == 0)
        def _():
            g_sc[...] = jnp.zeros_like(g_sc)

        qs = [q_ref[:, _hs(hh)] for hh in range(hb)]
        ks = [k_ref[:, _hs(hh)] for hh in range(hb)]
        vs = [v_ref[:, _hs(hh)] for hh in range(hb)]
        dobs = [do_ref[:, _hs(hh)].astype(BF16) for hh in range(hb)]
        aa = [_bdot(ks[hh], qs[hh], NT) for hh in range(hb)]
        das = [_bdot(vs[hh], dobs[hh], NT) for hh in range(hb)]
        qdo = [_bdot(qs[hh], dobs[hh], TN) for hh in range(hb)]
        for hh in range(hb):
            sl = _hs(hh)
            lg1 = lg_ref[hh][:, 0:1]
            dd = _ret_diag_decay(lg1, keys_on_rows=True)
            at = (aa[hh] * dd).astype(BF16)
            dat = (das[hh] * dd).astype(BF16)
            h = jnp.exp(lg1 * float(BQ)) * g_sc[hh]
            hhi, hlo = _split2(h)
            shi, slo = _split2(st_ref[0, sl, :])
            dv_ref[:, sl] = _bdot(at, dobs[hh]) + _bdot(ks[hh], hhi) + _bdot(ks[hh], hlo)
            dk_ref[:, sl] = _bdot(dat, qs[hh]) + _bdot(vs[hh], hhi, NT) + _bdot(vs[hh], hlo, NT)
            dq_ref[:, sl] = _bdot(dat, ks[hh], TN) + _bdot(dobs[hh], shi, NT) + _bdot(dobs[hh], slo, NT)
            g_sc[hh] = qdo[hh] + h

    qkv_specs, _, st_spec = _ret_specs(nq, hb, True)
    tile_spec = pl.BlockSpec((BQ, hb * LANES), lambda g, i: (nq - 1 - i, g))
    return pl.pallas_call(
        body, name=name, grid=(HEADS // hb, nq), in_specs=[_lg_spec(hb)] + qkv_specs + [st_spec, tile_spec],
        out_specs=[tile_spec] * 3, out_shape=[jax.ShapeDtypeStruct((t, HP), F32)] * 3,
        scratch_shapes=[pltpu.VMEM((hb, LANES, LANES), F32)],
        compiler_params=_cp("parallel", "arbitrary"))(lg_heads, qkv, qkv, qkv, states, do)


def _sb_tile_logs(q, kb, vis):
    z = _bdot(kb, q, NT)
    ls = -(jnp.maximum(z, 0.0) + jnp.log(1.0 + jnp.exp(-jnp.abs(z))))
    if vis is not None:
        ls = jnp.where(vis, ls, 0.0)
    return z, ls


def _sb_later(ls, after):
    hi, lo = _split2(ls)
    return _bdot(after, hi) + _bdot(after, lo)


def _sb_fwd(qkv, kvt, name):
    t = qkv.shape[0]
    nq = t // BQS
    per = BQS // TKS
    hb = HB_SB_FWD

    def body(q_ref, k_ref, vt_ref, o_ref, tot_ref, acc_sc, r_sc):
        i = pl.program_id(1)
        acc_sc[...] = jnp.zeros((hb, LANES, BQS), F32)
        r_sc[...] = jnp.zeros((hb, 1, BQS), F32)
        mr, mc = _tile_iota(TKS, TKS)
        after = jnp.where(mc > mr, 1.0, 0.0).astype(BF16)

        def tile(j, qoff):
            off = pl.multiple_of(j * TKS, TKS)
            lo = 0 if qoff is None else qoff
            qsl = slice(lo, BQS)
            vis = None if qoff is None else _vis(off, i * BQS + lo, TKS, BQS - lo, "strict")
            zl = [_sb_tile_logs(q_ref[qsl, _hs(hh)], k_ref[pl.ds(off, TKS), _hs(hh)], vis) for hh in range(hb)]
            laters = [_sb_later(zl[hh][1], after) for hh in range(hb)]
            ws = []
            for hh in range(hb):
                z, ls = zl[hh]
                w = jnp.exp(z + ls + laters[hh] + r_sc[hh, :, qsl])
                if vis is not None:
                    w = jnp.where(vis, w, 0.0)
                ws.append(w.astype(BF16))
                r_sc[hh, :, qsl] += jnp.sum(ls, axis=0, keepdims=True)
            for hh in range(hb):
                acc_sc[hh, :, qsl] += _bdot(vt_ref[j, _hs(hh), :], ws[hh])

        for d in reversed(range(per)):
            tile(per * i + d, d * TKS)

        def loop(jj, carry):
            tile(per * i - 1 - jj, None)
            return carry

        lax.fori_loop(0, per * i, loop, 0)
        for hh in range(hb):
            o_ref[:, _hs(hh)] = acc_sc[hh].T
            tot_ref[hh, 0] = r_sc[hh]

    return pl.pallas_call(
        body, name=name, grid=(HEADS // hb, nq),
        in_specs=[_q_spec(hb, BQS), _kv_spec(t, 1, hb), _kvt_spec(t // TKS, TKS, 1, hb)],
        out_specs=[_q_spec(hb, BQS), _qrow_spec(hb, BQS)],
        out_shape=[jax.ShapeDtypeStruct((t, HP), F32), jax.ShapeDtypeStruct((HEADS, nq, 1, BQS), F32)],
        scratch_shapes=[pltpu.VMEM((hb, LANES, BQS), F32), pltpu.VMEM((hb, 1, BQS), F32)],
        compiler_params=_cp("parallel", "arbitrary"))(qkv, qkv, kvt)


def _sb_bwd(qkv, kvt, do, tot, name):
    t = qkv.shape[0]
    nq = t // BQS
    per = BQS // TKS
    hb = HB_BWD

    def body(q_ref, k_ref, v_ref, kt_ref, do_ref, tot_ref, dq_ref, dk_ref, dv_ref, dq_sc, p_sc, g_sc):
        i = pl.program_id(1)

        @pl.when(i == 0)
        def _():
            dk_ref[...] = jnp.zeros_like(dk_ref)
            dv_ref[...] = jnp.zeros_like(dv_ref)

        dq_sc[...] = jnp.zeros((hb, LANES, BQS), F32)
        p_sc[...] = jnp.zeros((hb, 1, BQS), F32)
        g_sc[...] = jnp.zeros((hb, 1, BQS), F32)
        mr, mc = _tile_iota(TKS, TKS)
        after = jnp.where(mc > mr, 1.0, 0.0).astype(BF16)
        before = jnp.where(mc < mr, 1.0, 0.0).astype(BF16)

        def tile(j, qoff):
            off = pl.multiple_of(j * TKS, TKS)
            lo = 0 if qoff is None else qoff
            qsl = slice(lo, BQS)
            vis = None if qoff is None else _vis(off, i * BQS + lo, TKS, BQS - lo, "strict")
            qs = [q_ref[qsl, _hs(hh)] for hh in range(hb)]
            dobs = [do_ref[qsl, _hs(hh)].astype(BF16) for hh in range(hb)]
            zl = [_sb_tile_logs(qs[hh], k_ref[pl.ds(off, TKS), _hs(hh)], vis) for hh in range(hb)]
            dws = [_bdot(v_ref[pl.ds(off, TKS), _hs(hh)], dobs[hh], NT) for hh in range(hb)]
            laters = [_sb_later(zl[hh][1], after) for hh in range(hb)]
            ws, gs = [], []
            for hh in range(hb):
                z, ls = zl[hh]
                own = jnp.sum(ls, axis=0, keepdims=True)
                rest = tot_ref[hh, 0, :, qsl] - p_sc[hh, :, qsl] - own
                w = jnp.exp(z + ls + laters[hh] + rest)
                if vis is not None:
                    w = jnp.where(vis, w, 0.0)
                p_sc[hh, :, qsl] += own
                ws.append(w.astype(BF16))
                gs.append(dws[hh] * w)
            gins = []
            for hh in range(hb):
                ghi, glo = _split2(gs[hh])
                gins.append(_bdot(before, ghi) + _bdot(before, glo))
            dzbs = []
            for hh in range(hb):
                g = gs[hh]
                stay = jnp.exp(zl[hh][1])
                dz = g * stay - (1.0 - stay) * (gins[hh] + g_sc[hh, :, qsl])
                if vis is not None:
                    dz = jnp.where(vis, dz, 0.0)
                g_sc[hh, :, qsl] += jnp.sum(g, axis=0, keepdims=True)
                dzbs.append(dz.astype(BF16))
            for hh in range(hb):
                sl = _hs(hh)
                dv_ref[pl.ds(off, TKS), sl] += _bdot(ws[hh], dobs[hh])
                dk_ref[pl.ds(off, TKS), sl] += _bdot(dzbs[hh], qs[hh])
                dq_sc[hh, :, qsl] += _bdot(kt_ref[j, sl, :], dzbs[hh])

        def loop(j, carry):
            tile(j, None)
            return carry

        lax.fori_loop(0, per * i, loop, 0)
        for d in range(per):
            tile(per * i + d, d * TKS)
        for hh in range(hb):
            dq_ref[:, _hs(hh)] = dq_sc[hh].T

    return pl.pallas_call(
        body, name=name, grid=(HEADS // hb, nq),
        in_specs=[_q_spec(hb, BQS), _kv_spec(t, 1, hb), _kv_spec(t, 2, hb), _kvt_spec(t // TKS, TKS, 0, hb),
                  _q_spec(hb, BQS), _qrow_spec(hb, BQS)],
        out_specs=[_q_spec(hb, BQS), _acc_spec(t, hb), _acc_spec(t, hb)],
        out_shape=[jax.ShapeDtypeStruct((t, HP), F32)] * 3,
        scratch_shapes=[pltpu.VMEM((hb, LANES, BQS), F32), pltpu.VMEM((hb, 1, BQS), F32),
                        pltpu.VMEM((hb, 1, BQS), F32)],
        compiler_params=_cp("parallel", "arbitrary"))(qkv, qkv, qkv, kvt, do, tot)


def _sigmoid(v):
    return 1.0 / (1.0 + jnp.exp(-v))


def _post_fwd(oa, ob, oc, od, proj, g_pad, name):
    t = oa.shape[0]

    def body(oa_ref, ob_ref, oc_ref, od_ref, rg_ref, g_ref, mx_ref):
        g = g_ref[...]

        def group(o, gg):
            r = lax.rsqrt(jnp.sum(o * o, axis=-1, keepdims=True) * (1.0 / GROUP) + EPS)
            return _gather_heads(o * r * gg).astype(BF16)

        mx_ref[:, 0:GROUP] = group(oa_ref[...], g[:, 0:HP])
        mx_ref[:, GROUP:2 * GROUP] = group(ob_ref[...], g[:, HP:2 * HP])
        mx_ref[:, 3 * GROUP:4 * GROUP] = group(od_ref[...], g[:, 3 * HP:4 * HP])
        real = lax.broadcasted_iota(jnp.int32, (TM, LANES), 1) < HEAD_DIM
        rg = _spread_heads(rg_ref[...])
        gated = []
        for hb in range(HEADS):
            sl = slice(hb * LANES, (hb + 1) * LANES)
            o = oc_ref[:, sl]
            mu = jnp.sum(o, axis=-1, keepdims=True) * (1.0 / HEAD_DIM)
            dlt = jnp.where(real, o - mu, 0.0)
            var = jnp.sum(dlt * dlt, axis=-1, keepdims=True) * (1.0 / HEAD_DIM)
            yn = dlt * lax.rsqrt(var + EPS) * g[:, 2 * HP + hb * LANES:2 * HP + (hb + 1) * LANES]
            x = rg[:, sl]
            gated.append(yn * (x * _sigmoid(x)))
        mx_ref[:, 2 * GROUP:3 * GROUP] = _gather_heads(jnp.concatenate(gated, axis=1)).astype(BF16)

    rg_spec = pl.BlockSpec((TM, GROUP), lambda i: (i, OFF_RG // GROUP))
    return pl.pallas_call(
        body, name=name, grid=(t // TM,),
        in_specs=[_row_spec(HP)] * 4 + [rg_spec, _vec_spec(4 * HP)], out_specs=_row_spec(D_MODEL),
        out_shape=jax.ShapeDtypeStruct((t, D_MODEL), BF16), compiler_params=_cp("parallel"))(oa, ob, oc, od, proj, g_pad)


def _post_bwd(dmx, oa, ob, oc, od, proj, g_pad, name):
    t = oa.shape[0]

    def body(dm_ref, oa_ref, ob_ref, oc_ref, od_ref, rg_ref, g_ref,
             doa_ref, dob_ref, doc_ref, dod_ref, dla_ref, dlb_ref, drg_ref, dg_ref):
        @pl.when(pl.program_id(0) == 0)
        def _():
            dg_ref[...] = jnp.zeros_like(dg_ref)

        g = g_ref[...]

        def group_bwd(dm, o, gg):
            r = lax.rsqrt(jnp.sum(o * o, axis=-1, keepdims=True) * (1.0 / GROUP) + EPS)
            oh = o * r
            dgp = jnp.sum(dm * oh, axis=0, keepdims=True)
            dyh = dm * gg
            do = r * (dyh - oh * (jnp.sum(dyh * oh, axis=-1, keepdims=True) * (1.0 / GROUP)))
            return do, dgp

        def delta_bc(do, o):
            prod = do * o
            lane = lax.broadcasted_iota(jnp.int32, (TM, LANES), 1)
            out = jnp.zeros((TM, LANES), F32)
            for hb in range(HEADS):
                out = jnp.where(lane == hb, jnp.sum(prod[:, hb * LANES:(hb + 1) * LANES], axis=-1, keepdims=True), out)
            return out

        dmp = [_spread_heads(dm_ref[:, gi * GROUP:(gi + 1) * GROUP]) for gi in range(4)]
        rg = _spread_heads(rg_ref[...])
        oa = oa_ref[...]
        do_a, dga = group_bwd(dmp[0], oa, g[:, 0:HP])
        doa_ref[...] = do_a
        dla_ref[...] = delta_bc(do_a, oa)
        dg_ref[:, 0:HP] += dga
        ob = ob_ref[...]
        do_b, dgb = group_bwd(dmp[1], ob, g[:, HP:2 * HP])
        dob_ref[...] = do_b
        dlb_ref[...] = delta_bc(do_b, ob)
        dg_ref[:, HP:2 * HP] += dgb
        do_d, dgd = group_bwd(dmp[3], od_ref[...], g[:, 3 * HP:4 * HP])
        dod_ref[...] = do_d
        dg_ref[:, 3 * HP:4 * HP] += dgd
        real = lax.broadcasted_iota(jnp.int32, (TM, LANES), 1) < HEAD_DIM
        for hb in range(HEADS):
            sl = slice(hb * LANES, (hb + 1) * LANES)
            gsl = slice(2 * HP + hb * LANES, 2 * HP + (hb + 1) * LANES)
            o = oc_ref[:, sl]
            mu = jnp.sum(o, axis=-1, keepdims=True) * (1.0 / HEAD_DIM)
            dlt = jnp.where(real, o - mu, 0.0)
            var = jnp.sum(dlt * dlt, axis=-1, keepdims=True) * (1.0 / HEAD_DIM)
            rstd = lax.rsqrt(var + EPS)
            dhat = dlt * rstd
            gc = g[:, gsl]
            x = rg[:, sl]
            sg = _sigmoid(x)
            dm = dmp[2][:, sl]
            drg_ref[:, sl] = dm * (dhat * gc) * (sg * (1.0 + x * (1.0 - sg)))
            dyn = dm * (x * sg)
            dg_ref[:, gsl] += jnp.sum(dyn * dhat, axis=0, keepdims=True)
            ddh = dyn * gc
            m1 = jnp.sum(ddh, axis=-1, keepdims=True) * (1.0 / HEAD_DIM)
            m2 = jnp.sum(ddh * dhat, axis=-1, keepdims=True) * (1.0 / HEAD_DIM)
            doc_ref[:, sl] = jnp.where(real, rstd * (ddh - m1 - dhat * m2), 0.0)

    rg_spec = pl.BlockSpec((TM, GROUP), lambda i: (i, OFF_RG // GROUP))
    hp = _row_spec(HP)
    return pl.pallas_call(
        body, name=name, grid=(t // TM,),
        in_specs=[_row_spec(D_MODEL), hp, hp, hp, hp, rg_spec, _vec_spec(4 * HP)],
        out_specs=[hp] * 4 + [_row_spec(LANES)] * 2 + [hp, _vec_spec(4 * HP)],
        out_shape=[jax.ShapeDtypeStruct((t, HP), F32)] * 4 + [jax.ShapeDtypeStruct((t, LANES), F32)] * 2
        + [jax.ShapeDtypeStruct((t, HP), F32), jax.ShapeDtypeStruct((1, 4 * HP), F32)],
        compiler_params=_cp("arbitrary"))(dmx, oa, ob, oc, od, proj, g_pad)


def _mesh_pos():
    return lax.axis_index("x"), lax.axis_index("y"), lax.axis_index("c")


def _peer(pos, k):
    x, y, c = pos
    px = 1 - x if (k >> 2) & 1 else x
    py = 1 - y if (k >> 1) & 1 else y
    pc = 1 - c if k & 1 else c
    return (px, py, pc), 4 * px + 2 * py + pc


def _exchange(arrs, gather, name):
    n = len(arrs)

    def body(*refs):
        ins, outs = refs[:n], refs[n:2 * n]
        send_sems, recv_sems, loc_sems = refs[2 * n:]
        pos = _mesh_pos()
        me = 4 * pos[0] + 2 * pos[1] + pos[2]
        local = []
        for a in range(n):
            src = ins[a] if gather else ins[a].at[me]
            cp = pltpu.make_async_copy(src, outs[a].at[me], loc_sems.at[a])
            cp.start()
            local.append(cp)
        sends, recvs = [], []
        for k in range(1, N_DEV):
            peer, pid = _peer(pos, k)
            for a in range(n):
                s = a * (N_DEV - 1) + k - 1
                src = ins[a] if gather else ins[a].at[pid]
                cp = pltpu.make_async_remote_copy(
                    src_ref=src, dst_ref=outs[a].at[me], send_sem=send_sems.at[s], recv_sem=recv_sems.at[s],
                    device_id=peer, device_id_type=pl.DeviceIdType.MESH)
                cp.start()
                sends.append(cp)
                recvs.append(pltpu.make_async_remote_copy(
                    src_ref=src, dst_ref=outs[a].at[pid], send_sem=send_sems.at[s], recv_sem=recv_sems.at[s],
                    device_id=peer, device_id_type=pl.DeviceIdType.MESH))
        for cp in recvs:
            cp.wait_recv()
        for cp in sends:
            cp.wait_send()
        for cp in local:
            cp.wait()

    any_spec = pl.BlockSpec(memory_space=pl.ANY)
    out_shape = [jax.ShapeDtypeStruct((N_DEV,) + tuple(a.shape) if gather else tuple(a.shape), a.dtype) for a in arrs]
    return pl.pallas_call(
        body, name=name, in_specs=[any_spec] * n, out_specs=[any_spec] * n, out_shape=out_shape,
        scratch_shapes=[pltpu.SemaphoreType.DMA((n * (N_DEV - 1),)), pltpu.SemaphoreType.DMA((n * (N_DEV - 1),)),
                        pltpu.SemaphoreType.DMA((n,))],
        compiler_params=pltpu.CompilerParams(has_side_effects=True))(*arrs)


def _device_index():
    x, y, c = _mesh_pos()
    return 4 * x + 2 * y + c


def _landing(srcs, gather):
    me = _device_index()
    lands = []
    for a in srcs:
        own = a[None] if gather else lax.dynamic_slice_in_dim(a, me, 1, axis=0)
        shape = (N_DEV,) + tuple(a.shape) if gather else tuple(a.shape)
        lands.append(lax.dynamic_update_slice_in_dim(lax.empty(shape, a.dtype), own, me, axis=0))
    return lands


def _exchange_copies(ins, lands, send_sems, recv_sems, gather):
    pos = _mesh_pos()
    me = 4 * pos[0] + 2 * pos[1] + pos[2]
    sends, recvs = [], []
    for k in range(1, N_DEV):
        peer, pid = _peer(pos, k)
        for a in range(len(ins)):
            s = a * (N_DEV - 1) + k - 1
            src = ins[a] if gather else ins[a].at[pid]
            sends.append(pltpu.make_async_remote_copy(
                src_ref=src, dst_ref=lands[a].at[me], send_sem=send_sems.at[s], recv_sem=recv_sems.at[s],
                device_id=peer, device_id_type=pl.DeviceIdType.MESH))
            recvs.append(pltpu.make_async_remote_copy(
                src_ref=src, dst_ref=lands[a].at[pid], send_sem=send_sems.at[s], recv_sem=recv_sems.at[s],
                device_id=peer, device_id_type=pl.DeviceIdType.MESH))
    return sends, recvs


def _exchange_start(srcs, gather, name, after=None):
    n = len(srcs)
    lands = _landing(srcs, gather)
    nsem = n * (N_DEV - 1)
    extra = [] if after is None else [after]

    def body(*refs):
        ins, lnd = refs[:n], refs[n:2 * n]
        send_sems, recv_sems = refs[2 * n + len(extra)], refs[2 * n + len(extra) + 1]
        token = refs[-1]
        sends, _ = _exchange_copies(ins, lnd, send_sems, recv_sems, gather)
        for cp in sends:
            cp.start()
        token[...] = jnp.zeros_like(token)

    hbm = pl.BlockSpec(memory_space=pltpu.HBM)
    sem = pl.BlockSpec(memory_space=pltpu.SEMAPHORE)
    bufs = list(srcs) + lands
    out_shape = ([pltpu.SemaphoreType.DMA((nsem,)), pltpu.SemaphoreType.DMA((nsem,))]
                 + [pltpu.HBM(b.shape, b.dtype) for b in bufs] + [jax.ShapeDtypeStruct((8, LANES), F32)])
    outs = pl.pallas_call(
        body, name=name, in_specs=[hbm] * (2 * n) + [pl.BlockSpec(memory_space=pl.ANY)] * len(extra),
        out_specs=[sem, sem] + [hbm] * (2 * n) + [pl.BlockSpec(memory_space=pltpu.VMEM)], out_shape=out_shape,
        input_output_aliases={i: 2 + i for i in range(2 * n)},
        compiler_params=pltpu.CompilerParams(has_side_effects=pltpu.SideEffectType.DATAFLOW_SIDE_EFFECTING),
    )(*[pltpu.with_memory_space_constraint(b, pltpu.HBM) for b in bufs], *extra)
    return (outs[0], outs[1], outs[2:2 + n], outs[2 + n:2 + 2 * n]), outs[-1]


def _exchange_wait(state, after, gather, name):
    send_sems, recv_sems, srcs, lands = state
    n = len(srcs)
    after = list(after) if isinstance(after, (list, tuple)) else [after]

    def body(*refs):
        ins, lnd = refs[:n], refs[n:2 * n]
        s_sems, r_sems = refs[2 * n], refs[2 * n + 1]
        sends, recvs = _exchange_copies(ins, lnd, s_sems, r_sems, gather)
        for cp in sends:
            cp.wait_send()
        for cp in recvs:
            cp.wait_recv()

    hbm = pl.BlockSpec(memory_space=pltpu.HBM)
    sem = pl.BlockSpec(memory_space=pltpu.SEMAPHORE)
    bufs = list(srcs) + list(lands)
    outs = pl.pallas_call(
        body, name=name, in_specs=[hbm] * (2 * n) + [sem, sem] + [pl.BlockSpec(memory_space=pl.ANY)] * len(after),
        out_specs=[hbm] * (2 * n), out_shape=[pltpu.HBM(b.shape, b.dtype) for b in bufs],
        input_output_aliases={i: i for i in range(2 * n)},
        compiler_params=pltpu.CompilerParams(has_side_effects=pltpu.SideEffectType.DATAFLOW_SIDE_EFFECTING),
    )(*bufs, send_sems, recv_sems, *after)
    return outs[n:]


def _adam_vals(w, g, m, v):
    m = ADAM_B1 * m + (1.0 - ADAM_B1) * g
    v = ADAM_B2 * v + (1.0 - ADAM_B2) * (g * g)
    m_hat = m / ADAM_C1
    v_hat = v / ADAM_C2
    delta = -ADAM_LR * (m_hat / (jnp.sqrt(v_hat) + ADAM_EPS) + ADAM_WD * w)
    return delta, m, v


def _small_allreduce_adam(part, w, m, v, name):
    rows = part.shape[0]

    def body(p_ref, w_ref, m_ref, v_ref, g_ref, d_ref, nm_ref, nv_ref, gath, send_sems, recv_sems):
        pos = _mesh_pos()
        me = 4 * pos[0] + 2 * pos[1] + pos[2]
        gath[me] = p_ref[...]
        sends, recvs = [], []
        for k in range(1, N_DEV):
            peer, pid = _peer(pos, k)
            cp = pltpu.make_async_remote_copy(
                src_ref=p_ref, dst_ref=gath.at[me], send_sem=send_sems.at[k - 1], recv_sem=recv_sems.at[k - 1],
                device_id=peer, device_id_type=pl.DeviceIdType.MESH)
            cp.start()
            sends.append(cp)
            recvs.append(pltpu.make_async_remote_copy(
                src_ref=p_ref, dst_ref=gath.at[pid], send_sem=send_sems.at[k - 1], recv_sem=recv_sems.at[k - 1],
                device_id=peer, device_id_type=pl.DeviceIdType.MESH))
        for cp in recvs:
            cp.wait_recv()
        for cp in sends:
            cp.wait_send()
        g = gath[0]
        for p in range(1, N_DEV):
            g = g + gath[p]
        g_ref[...] = g
        d, nm, nv = _adam_vals(w_ref[...], g, m_ref[...], v_ref[...])
        d_ref[...] = d
        nm_ref[...] = nm
        nv_ref[...] = nv

    vm = pl.BlockSpec(memory_space=pltpu.VMEM)
    sds = jax.ShapeDtypeStruct((rows, LANES), F32)
    return pl.pallas_call(
        body, name=name, in_specs=[vm] * 4, out_specs=[vm] * 4, out_shape=[sds] * 4,
        scratch_shapes=[pltpu.VMEM((N_DEV, rows, LANES), F32), pltpu.SemaphoreType.DMA((N_DEV - 1,)),
                        pltpu.SemaphoreType.DMA((N_DEV - 1,))],
        compiler_params=pltpu.CompilerParams(has_side_effects=True))(part, w, m, v)


def _reduce_adam(recv, w, m, v, name):
    shape = w.shape
    c = shape[-1]
    r = int(np.prod(shape[:-1]))
    recv2, w2, m2, v2 = recv.reshape(N_DEV, r, c), w.reshape(r, c), m.reshape(r, c), v.reshape(r, c)
    tr = r
    while tr * c * 4 > (1 << 20) and tr % 16 == 0:
        tr //= 2

    def body(r_ref, w_ref, m_ref, v_ref, g_ref, d_ref, nm_ref, nv_ref):
        g = r_ref[0].astype(F32)
        for p in range(1, N_DEV):
            g = g + r_ref[p].astype(F32)
        g_ref[...] = g
        d, nm, nv = _adam_vals(w_ref[...], g, m_ref[...], v_ref[...])
        d_ref[...] = d
        nm_ref[...] = nm
        nv_ref[...] = nv

    spec = pl.BlockSpec((tr, c), lambda i: (i, 0))
    sds = jax.ShapeDtypeStruct((r, c), F32)
    outs = pl.pallas_call(
        body, name=name, grid=(r // tr,),
        in_specs=[pl.BlockSpec((N_DEV, tr, c), lambda i: (0, i, 0)), spec, spec, spec],
        out_specs=[spec] * 4, out_shape=[sds] * 4, compiler_params=_cp("parallel"))(recv2, w2, m2, v2)
    return [o.reshape(shape) for o in outs]


def _pad_heads(w, real=HEAD_DIM):
    lead = w.shape[:-1]
    w = w.reshape(lead + (HEADS, real))
    w = jnp.pad(w, [(0, 0)] * len(lead) + [(0, 0), (0, LANES - real)])
    return w.reshape(lead + (HP,))


def _unpad_heads(w, real=HEAD_DIM):
    lead = w.shape[:-1]
    return w.reshape(lead + (HEADS, LANES))[..., :real].reshape(lead + (HEADS * real,))


_IN_SEGS = (("fq", 0, 256), ("fk", 256, 512), ("fv", 512, 768), ("ff", 768, 772), ("cq", 772, 1028),
            ("ckv", 1028, 1156), ("kr", 1156, 1188), ("rq", 1188, 1444), ("rk", 1444, 1700), ("rv", 1700, 1956),
            ("rg", 1956, 2212), ("sq", 2212, 2468), ("sk", 2468, 2724), ("sv", 2724, 2980))


def _pad_w_in(w):
    s = {n: w[:, a:b] for n, a, b in _IN_SEGS}
    rows = w.shape[0]
    z = lambda n: jnp.zeros((rows, n), w.dtype)
    parts = [s[n] for n in ("fq", "fk", "fv", "rq", "rk", "rv", "sq", "sk", "sv", "rg", "cq", "ckv")]
    parts += [z(HEAD_DIM), s["kr"], z(LANES - HEAD_DIM - ROPE_MLA), s["ff"], z(LANES - HEADS),
              z(NP_IN - OFF_FF - LANES)]
    return jnp.concatenate(parts, axis=1)


def _unpad_w_in(wp):
    seg = lambda off, n=GROUP: wp[:, off:off + n]
    parts = [seg(OFF_FOX), seg(OFF_FOX + GROUP), seg(OFF_FOX + 2 * GROUP), seg(OFF_FF, HEADS),
             seg(OFF_CQ, Q_RANK), seg(OFF_CKV, KV_RANK), seg(OFF_KR + HEAD_DIM, ROPE_MLA),
             seg(OFF_RET), seg(OFF_RET + GROUP), seg(OFF_RET + 2 * GROUP), seg(OFF_RG),
             seg(OFF_SB), seg(OFF_SB + GROUP), seg(OFF_SB + 2 * GROUP)]
    return jnp.concatenate(parts, axis=1)


def _pad_w_kv(w):
    w4 = w.reshape(KV_RANK, HEADS, 2 * HEAD_DIM)
    k = w4[:, :, :HEAD_DIM].reshape(KV_RANK, GROUP)
    v = w4[:, :, HEAD_DIM:].reshape(KV_RANK, GROUP)
    return jnp.concatenate([_pad_heads(k), _pad_heads(v)], axis=1)


def _unpad_w_kv(wp):
    k = _unpad_heads(wp[:, :HP]).reshape(KV_RANK, HEADS, HEAD_DIM)
    v = _unpad_heads(wp[:, HP:]).reshape(KV_RANK, HEADS, HEAD_DIM)
    return jnp.concatenate([k, v], axis=-1).reshape(KV_RANK, HEADS * 2 * HEAD_DIM)


def _pad_gain_out(g):
    g = jnp.pad(g.reshape(4 * HEADS, HEAD_DIM), ((0, 0), (0, LANES - HEAD_DIM)))
    return g.reshape(1, 4 * HP)


def _unpad_gain_out(gp):
    return gp.reshape(4 * HEADS, LANES)[:, :HEAD_DIM].reshape(D_MODEL)


_SMALL = (("g_mix_pre", 1024), ("g_mix_post", 1024), ("g_ffn_pre", 1024), ("g_ffn_post", 1024), ("g_mix_out", 1024),
          ("g_q_lora", 256), ("g_kv_lora", 128), ("b_forget", 4))


def _pack_small(vals):
    parts = []
    for name, n in _SMALL:
        a = vals[name].astype(F32)
        if n < LANES:
            a = jnp.pad(a, ((0, 0), (0, LANES - n)))
        parts.append(a)
    return jnp.concatenate(parts, axis=1).reshape(DEPTH * SMALL_ROWS, LANES)


def _unpack_small(packed):
    flat = packed.reshape(DEPTH, SMALL_ROWS * LANES)
    out, off = {}, 0
    for name, n in _SMALL:
        out[name] = flat[:, off:off + n]
        off += max(n, LANES)
    return out


def kernel(x, positions, g_mix_pre, w_in, b_forget, g_q_lora, w_q_up, g_kv_lora, w_kv_up, g_mix_out, w_out, g_mix_post, g_ffn_pre, w_ffn_up, w_ffn_down, g_ffn_post, loss_target, m_g_mix_pre, m_w_in, m_b_forget, m_g_q_lora, m_w_q_up, m_g_kv_lora, m_w_kv_up, m_g_mix_out, m_w_out, m_g_mix_post, m_g_ffn_pre, m_w_ffn_up, m_w_ffn_down, m_g_ffn_post, v_g_mix_pre, v_w_in, v_b_forget, v_g_q_lora, v_w_q_up, v_g_kv_lora, v_w_kv_up, v_g_mix_out, v_w_out, v_g_mix_post, v_g_ffn_pre, v_w_ffn_up, v_w_ffn_down, v_g_ffn_post):
    t = x.shape[1]
    nq = t // BQ
    x0 = x[0]
    tgt = loss_target[0]
    pos = positions[0].astype(F32).reshape(t, 1)

    half_r, half_m = HEAD_DIM // 2, ROPE_MLA // 2
    invf_r = ROPE_BASE ** (-jnp.arange(half_r, dtype=F32) / half_r)
    invf_m = ROPE_BASE ** (-jnp.arange(half_m, dtype=F32) / half_m)
    invf = jnp.concatenate([invf_r, invf_r, invf_m, invf_m,
                            jnp.zeros((LANES - HEAD_DIM - ROPE_MLA,), F32)]).reshape(1, LANES)
    log_gamma = jnp.log1p(-jnp.power(2.0, -5.0 - jnp.arange(HEADS, dtype=F32)))
    lg_lanes = jnp.repeat(log_gamma, LANES).reshape(1, HP)
    lg_heads = jnp.broadcast_to(log_gamma[:, None, None], (HEADS, 1, LANES))

    big = [w_in, w_q_up, w_kv_up, w_out, w_ffn_up, w_ffn_down]
    bf = lambda w: w.astype(BF16)
    first = _exchange([bf(w_in[0]), bf(w_q_up), bf(w_kv_up)], True, "weights_gather_first")
    l0_state, l0_token = _exchange_start([bf(w[0]) for w in (w_out, w_ffn_up, w_ffn_down)], True,
                                         "weights_gather_layer0_start")
    l1_state, rest_token = _exchange_start([bf(w[1]) for w in (w_in, w_out, w_ffn_up, w_ffn_down)], True,
                                           "weights_gather_layer1_start", after=l0_token)
    wq_g = first[1].transpose(1, 2, 0, 3).reshape(DEPTH, Q_RANK, 384)
    wkv_g = first[2].transpose(1, 2, 0, 3).reshape(DEPTH, KV_RANK, 512)

    row = lambda g: g.reshape(1, -1)
    layers = []
    for l in range(DEPTH):
        layers.append(dict(
            wq=_pad_heads(wq_g[l], 96), wkv=_pad_w_kv(wkv_g[l]),
            g_pre=row(g_mix_pre[l]), g_post=row(g_mix_post[l]), g_fpre=row(g_ffn_pre[l]), g_fpost=row(g_ffn_post[l]),
            g_out=_pad_gain_out(g_mix_out[l]), g_q=row(g_q_lora[l]), g_kv=row(g_kv_lora[l]),
            b_pad=jnp.pad(b_forget[l], (0, LANES - HEADS)).reshape(1, LANES)))
    layers[0]["win"] = _pad_w_in(first[0].reshape(D_MODEL, D_IN))

    saved = []
    xin = x0
    h = _rms_fwd(xin, layers[0]["g_pre"] + rest_token[0:1, 0:1], "rms_pre_0")
    loss_row = dx = None
    for l, p in enumerate(layers):
        s = dict(x=xin, h=h)
        proj = _mm(h, p["win"], name=f"in_proj_{l}", tm=512, tn=NP_IN)
        cum = _cumsum(_forget_lsf(proj, p["b_pad"], f"forget_lsf_{l}"), False, f"forget_cumsum_{l}")
        fox, ret, sb, mla, fox_t, sb_t, mla_t, ret_t = _prep_fwd(proj, cum, pos, invf, lg_lanes, p["g_q"],
                                                          p["g_kv"], p["wq"], p["wkv"], f"prep_fwd_{l}")
        oa, lse_a = _softmax_fwd(fox, fox_t, chunk_mask=False, scale=1.0, name=f"fox_fwd_{l}")
        ob, lse_b = _softmax_fwd(mla, mla_t, chunk_mask=True, scale=96.0 ** -0.5, name=f"mla_fwd_{l}")
        oc, ret_st = _ret_fwd(ret, ret_t, lg_heads, f"ret_fwd_{l}")
        od, sb_tot = _sb_fwd(sb, sb_t, f"sb_fwd_{l}")
        if l == 0:
            got = _exchange_wait(l0_state, od, True, "weights_gather_layer0_wait")
            p.update(wout=got[0].reshape(D_MODEL, D_MODEL), wup=got[1].transpose(1, 0, 2).reshape(D_MODEL, D_FF),
                     wdn=got[2].reshape(D_FF, D_MODEL))
        mixed = _post_fwd(oa, ob, oc, od, proj, p["g_out"], f"post_fwd_{l}")
        mix = _mm(mixed, p["wout"], name=f"out_proj_{l}")
        x1, h2 = _add_rms_fwd(xin, mix, p["g_post"], p["g_fpre"], f"mix_residual_{l}")
        a = _mm(h2, p["wup"], name=f"ffn_up_{l}")
        y = _mm(a, p["wdn"], name=f"ffn_down_{l}", a_fn=_relu2)
        s.update(proj=proj, fox=fox, ret=ret, sb=sb, mla=mla, fox_t=fox_t, sb_t=sb_t, mla_t=mla_t, ret_st=ret_st, oa=oa, ob=ob, oc=oc,
                 od=od, sb_tot=sb_tot, lse_a=lse_a, lse_b=lse_b, mixed=mixed, mix=mix, x1=x1, h2=h2, a=a, y=y)
        saved.append(s)
        if l == 0:
            got = _exchange_wait(l1_state, y, True, "weights_gather_layer1_wait")
            layers[1].update(win=_pad_w_in(got[0].reshape(D_MODEL, D_IN)), wout=got[1].reshape(D_MODEL, D_MODEL),
                             wup=got[2].transpose(1, 0, 2).reshape(D_MODEL, D_FF), wdn=got[3].reshape(D_FF, D_MODEL))
        if l + 1 < DEPTH:
            xin, h = _add_rms_fwd(x1, y, p["g_fpost"], layers[l + 1]["g_pre"], f"ffn_residual_{l}")
        else:
            loss_row, dx = _final_loss(x1, y, p["g_fpost"], tgt, "loss")

    small_g = {n: [None] * DEPTH for n, _ in _SMALL}
    big_g = [[None] * DEPTH for _ in range(6)]
    to_send = [
        lambda g: g.reshape(N_DEV, 1, D_MODEL // N_DEV, D_IN),
        lambda g: g.reshape(Q_RANK, N_DEV, 384 // N_DEV).transpose(1, 0, 2)[:, None],
        lambda g: g.reshape(KV_RANK, N_DEV, 512 // N_DEV).transpose(1, 0, 2)[:, None],
        lambda g: g.reshape(N_DEV, 1, D_MODEL // N_DEV, D_MODEL),
        lambda g: g.reshape(D_MODEL, N_DEV, D_FF // N_DEV).transpose(1, 0, 2)[:, None],
        lambda g: g.reshape(N_DEV, 1, D_FF // N_DEV, D_MODEL),
    ]
    send_of = lambda ks, l: [to_send[k](big_g[k][l]).astype(BF16) for k in ks]
    late_state = early_state = None
    order_token = jnp.zeros((1, 1), F32)
    for l in reversed(range(DEPTH)):
        p, s = layers[l], saved[l]
        dy, dg = _norm_bwd(dx, s["y"], p["g_fpost"] + order_token, None, BF16, f"ffn_post_bwd_{l}")
        small_g["g_ffn_post"][l] = dg
        da = _mm(dy, p["wdn"], name=f"ffn_down_dx_{l}", tb=True, out_dtype=BF16, epi=_drelu2, epi_in=s["a"])
        big_g[5][l] = _mm(s["a"], dy, name=f"ffn_down_dw_{l}", ta=True, a_fn=_relu2, tk=2048)
        big_g[4][l] = _mm(s["h2"], da, name=f"ffn_up_dw_{l}", ta=True, tk=2048)
        dh2 = _mm(da, p["wup"], name=f"ffn_up_dx_{l}", tb=True)
        dx1, dg = _norm_bwd(dh2, s["x1"], p["g_fpre"], dx, F32, f"ffn_pre_bwd_{l}")
        small_g["g_ffn_pre"][l] = dg
        dmix, dg = _norm_bwd(dx1, s["mix"], p["g_post"], None, BF16, f"mix_post_bwd_{l}")
        small_g["g_mix_post"][l] = dg
        dmixed = _mm(dmix, p["wout"], name=f"out_proj_dx_{l}", tb=True)
        big_g[3][l] = _mm(s["mixed"], dmix, name=f"out_proj_dw_{l}", ta=True, tk=2048)
        g_out = p["g_out"]
        if l == 0:
            early_state, early_token = _exchange_start(send_of((3, 4, 5), 0), False, "grads_layer0_early_start")
            g_out = g_out + early_token[0:1, 0:1]
        doa, dob, doc, dod, dla, dlb, drg, dgo = _post_bwd(dmixed, s["oa"], s["ob"], s["oc"], s["od"], s["proj"],
                                                           g_out, f"post_bwd_{l}")
        small_g["g_mix_out"][l] = _unpad_gain_out(dgo).reshape(1, D_MODEL)
        as_rows = lambda a: a[:, :HEADS].T.reshape(HEADS, t // BQA, 1, BQA)
        dfq, dfk, dfv, dcum_k, dcum_q = _softmax_bwd(s["fox"], s["fox_t"], doa, s["lse_a"], as_rows(dla), bias=True,
                                                     chunk_mask=False, scale=1.0, name=f"fox_bwd_{l}")
        dmq, dmk, dmv = _softmax_bwd(s["mla"], s["mla_t"], dob, s["lse_b"], as_rows(dlb), bias=False, chunk_mask=True,
                                     scale=96.0 ** -0.5, name=f"mla_bwd_{l}")
        drq, drk, drv = _ret_bwd(s["ret"], s["ret_st"], lg_heads, doc, f"ret_bwd_{l}")
        dsq, dsk, dsv = _sb_bwd(s["sb"], s["sb_t"], dod, s["sb_tot"], f"sb_bwd_{l}")
        dcum_q = jnp.pad(dcum_q.reshape(HEADS, t).T, ((0, 0), (0, LANES - HEADS)))
        dlsf = _cumsum(dcum_q, True, f"forget_cumsum_bwd_{l}", partials=dcum_k)
        dproj, dwq, dwkv, dgq, dgkv, dbf = _prep_bwd(
            (dfq, dfk, dfv), (drq, drk, drv), (dsq, dsk, dsv), (dmq, dmk, dmv), drg, dlsf, s["proj"], pos, invf,
            lg_lanes, p["b_pad"], p["g_q"], p["g_kv"], p["wq"], p["wkv"], f"prep_bwd_{l}")
        small_g["g_q_lora"][l] = dgq
        small_g["g_kv_lora"][l] = dgkv
        small_g["b_forget"][l] = dbf[:, :HEADS]
        big_g[1][l] = _unpad_heads(dwq, 96)
        big_g[2][l] = _unpad_w_kv(dwkv)
        dh = _mm(dproj, p["win"], name=f"in_proj_dx_{l}", tb=True, tk=NP_IN // 2)
        big_g[0][l] = _unpad_w_in(_mm(s["h"], dproj, name=f"in_proj_dw_{l}", ta=True, tn=NP_IN // 2))
        g_pre = p["g_pre"]
        if l == 0:
            last_state, last_token = _exchange_start(send_of((0, 1, 2), 0), False, "grads_layer0_rest_start")
            g_pre = g_pre + last_token[0:1, 0:1]
        dx, dg = _norm_bwd(dh, s["x"], g_pre, dx1, F32, f"mix_pre_bwd_{l}")
        small_g["g_mix_pre"][l] = dg
        if l == DEPTH - 1:
            late_state, late_token = _exchange_start(send_of(range(6), l), False, "grads_layer1_start")
            order_token = late_token[0:1, 0:1]
    grad_x = dx.reshape(1, t, D_MODEL)

    res = {}
    small_w = dict(g_mix_pre=g_mix_pre, g_mix_post=g_mix_post, g_ffn_pre=g_ffn_pre, g_ffn_post=g_ffn_post,
                   g_mix_out=g_mix_out, g_q_lora=g_q_lora, g_kv_lora=g_kv_lora, b_forget=b_forget)
    small_m = dict(g_mix_pre=m_g_mix_pre, g_mix_post=m_g_mix_post, g_ffn_pre=m_g_ffn_pre, g_ffn_post=m_g_ffn_post,
                   g_mix_out=m_g_mix_out, g_q_lora=m_g_q_lora, g_kv_lora=m_g_kv_lora, b_forget=m_b_forget)
    small_v = dict(g_mix_pre=v_g_mix_pre, g_mix_post=v_g_mix_post, g_ffn_pre=v_g_ffn_pre, g_ffn_post=v_g_ffn_post,
                   g_mix_out=v_g_mix_out, g_q_lora=v_g_q_lora, g_kv_lora=v_g_kv_lora, b_forget=v_b_forget)
    n_small = DEPTH * SMALL_ROWS
    extra = lambda a: jnp.concatenate([a, jnp.zeros((8, LANES), F32)], axis=0)
    part = jnp.concatenate([_pack_small({n: jnp.concatenate(small_g[n], axis=0) for n, _ in _SMALL}),
                            jnp.broadcast_to(loss_row, (8, LANES))], axis=0)
    sres = _small_allreduce_adam(part, extra(_pack_small(small_w)), extra(_pack_small(small_m)),
                                 extra(_pack_small(small_v)), "small_allreduce_adamw")
    loss = sres[0][n_small, 0]
    sg, sd, sm, sv = [_unpack_small(a[:n_small]) for a in sres]
    for n, _ in _SMALL:
        res[n] = [sg[n], sd[n], sm[n], sv[n]]

    late = _exchange_wait(late_state, dx, False, "grads_layer1_wait")
    early = _exchange_wait(early_state, dx, False, "grads_layer0_early_wait")
    ms = [m_w_in, m_w_q_up, m_w_kv_up, m_w_out, m_w_ffn_up, m_w_ffn_down]
    vs = [v_w_in, v_w_q_up, v_w_kv_up, v_w_out, v_w_ffn_up, v_w_ffn_down]
    names = ["w_in", "w_q_up", "w_kv_up", "w_out", "w_ffn_up", "w_ffn_down"]
    for k in (3, 4, 5):
        recv = jnp.concatenate([early[k - 3], late[k]], axis=1)
        res[names[k]] = _reduce_adam(recv, big[k], ms[k], vs[k], f"adamw_{names[k]}")
    done = [sres[0]] + [res[names[k]][0] for k in (3, 4, 5)]
    last = _exchange_wait(last_state, done, False, "grads_layer0_rest_wait")
    for k in (0, 1, 2):
        recv = jnp.concatenate([last[k], late[k]], axis=1)
        res[names[k]] = _reduce_adam(recv, big[k], ms[k], vs[k], f"adamw_{names[k]}")

    order = ["g_mix_pre", "w_in", "b_forget", "g_q_lora", "w_q_up", "g_kv_lora", "w_kv_up", "g_mix_out", "w_out",
             "g_mix_post", "g_ffn_pre", "w_ffn_up", "w_ffn_down", "g_ffn_post"]
    outs = [loss, grad_x]
    for idx in range(4):
        outs += [res[n][idx] for n in order]
    return tuple(outs)
```

```python
import functools
import math

import numpy as np
import jax
import jax.numpy as jnp
from jax import lax
from jax.experimental import pallas as pl
from jax.experimental.pallas import tpu as pltpu

F32 = jnp.float32
BF16 = jnp.bfloat16

D_MODEL = 1024
DEPTH = 2
N_DEV = 8
GROUP = 256
HEADS = 4
HEAD_DIM = 64
LANES = 128
HP = HEADS * LANES
QKV = 3 * HP
Q_RANK = 256
KV_RANK = 128
ROPE_MLA = 32
D_FF = 4096
D_IN = 2980
CHUNK_SHIFT = 6
EPS = 1e-6
ROPE_BASE = 10000.0
NEG = -1e30

QKV_IN = 3 * GROUP
OFF_FOX, OFF_RET, OFF_SB = 0, QKV_IN, 2 * QKV_IN
OFF_RG = 3 * QKV_IN
OFF_CQ = OFF_RG + GROUP
OFF_CKV = OFF_CQ + Q_RANK
OFF_KR = OFF_CKV + LANES
OFF_FF = OFF_KR + LANES
NP_IN = 3328

BQ = 256
TKS = 128
TM = 256
VMEM_LIMIT = 58 * 1024 * 1024

ADAM_LR, ADAM_B1, ADAM_B2, ADAM_EPS, ADAM_WD, ADAM_STEP = 0.001, 0.9, 0.999, 1e-08, 0.01, 10
ADAM_C1 = 1.0 - ADAM_B1 ** ADAM_STEP
ADAM_C2 = 1.0 - ADAM_B2 ** ADAM_STEP

SMALL_ROWS = 44

NT = (((1,), (1,)), ((), ()))
TN = (((0,), (0,)), ((), ()))


def _cp(*sem):
    return pltpu.CompilerParams(dimension_semantics=sem if sem else None, vmem_limit_bytes=VMEM_LIMIT)


def _bdot(a, b, dn=None):
    if dn is None:
        return jnp.dot(a, b, preferred_element_type=F32)
    return lax.dot_general(a, b, dn, preferred_element_type=F32)


def _split2(x):
    hi = x.astype(BF16)
    lo = (x - hi.astype(F32)).astype(BF16)
    return hi, lo


def _mm(a, b, *, name, ta=False, tb=False, out_dtype=F32, a_fn=None, epi=None, epi_in=None,
        tm=1024, tn=1024, tk=1024):
    m, k = (a.shape[1], a.shape[0]) if ta else a.shape
    n = b.shape[0] if tb else b.shape[1]
    tm, tn, tk = min(tm, m), min(tn, n), min(tk, k)
    assert m % tm == 0 and n % tn == 0 and k % tk == 0, (name, m, n, k)
    nk = k // tk
    dn = (((0 if ta else 1,), (1 if tb else 0,)), ((), ()))

    def body(*refs):
        if epi is None:
            a_ref, b_ref, o_ref = refs[:3]
            e_ref = None
            rest = refs[3:]
        else:
            a_ref, b_ref, e_ref, o_ref = refs[:4]
            rest = refs[4:]
        av = a_ref[...]
        if a_fn is not None:
            av = a_fn(av)
        part = lax.dot_general(av.astype(BF16), b_ref[...].astype(BF16), dn, preferred_element_type=F32)

        def finish(r):
            if epi is not None:
                r = epi(r, e_ref[...])
            o_ref[...] = r.astype(out_dtype)

        if nk == 1:
            finish(part)
        else:
            acc_ref = rest[0]
            kk = pl.program_id(2)

            @pl.when(kk == 0)
            def _():
                acc_ref[...] = part

            @pl.when(kk > 0)
            def _():
                acc_ref[...] += part

            @pl.when(kk == nk - 1)
            def _():
                finish(acc_ref[...])

    a_spec = pl.BlockSpec((tk, tm), lambda i, j, kk: (kk, i)) if ta else pl.BlockSpec((tm, tk), lambda i, j, kk: (i, kk))
    b_spec = pl.BlockSpec((tn, tk), lambda i, j, kk: (j, kk)) if tb else pl.BlockSpec((tk, tn), lambda i, j, kk: (kk, j))
    o_spec = pl.BlockSpec((tm, tn), lambda i, j, kk: (i, j))
    in_specs = [a_spec, b_spec]
    args = [a, b]
    if epi is not None:
        in_specs.append(o_spec)
        args.append(epi_in)
    return pl.pallas_call(
        body, name=name, grid=(m // tm, n // tn, nk),
        in_specs=in_specs, out_specs=o_spec,
        out_shape=jax.ShapeDtypeStruct((m, n), out_dtype),
        scratch_shapes=[pltpu.VMEM((tm, tn), F32)] if nk > 1 else [],
        compiler_params=_cp("parallel", "parallel", "arbitrary"),
    )(*args)


def _relu2(v):
    r = jnp.maximum(v, 0.0)
    return r * r


def _drelu2(du, av):
    return du * (2.0 * jnp.maximum(av, 0.0))


def _rms(v, g):
    r = lax.rsqrt(jnp.mean(v * v, axis=-1, keepdims=True) + EPS)
    return v * r * g


def _row_spec(w):
    return pl.BlockSpec((TM, w), lambda i: (i, 0))


def _vec_spec(w):
    return pl.BlockSpec((1, w), lambda i: (0, 0))


def _rms_fwd(x, g, name):
    t, d = x.shape

    def body(x_ref, g_ref, h_ref):
        h_ref[...] = _rms(x_ref[...], g_ref[...]).astype(BF16)

    return pl.pallas_call(
        body, name=name, grid=(t // TM,), in_specs=[_row_spec(d), _vec_spec(d)], out_specs=_row_spec(d),
        out_shape=jax.ShapeDtypeStruct((t, d), BF16), compiler_params=_cp("parallel"))(x, g)


def _add_rms_fwd(x, y, g1, g2, name):
    t, d = x.shape

    def body(x_ref, y_ref, g1_ref, g2_ref, xn_ref, h_ref):
        xn = x_ref[...] + _rms(y_ref[...], g1_ref[...])
        xn_ref[...] = xn
        h_ref[...] = _rms(xn, g2_ref[...]).astype(BF16)

    return pl.pallas_call(
        body, name=name, grid=(t // TM,),
        in_specs=[_row_spec(d), _row_spec(d), _vec_spec(d), _vec_spec(d)],
        out_specs=[_row_spec(d), _row_spec(d)],
        out_shape=[jax.ShapeDtypeStruct((t, d), F32), jax.ShapeDtypeStruct((t, d), BF16)],
        compiler_params=_cp("parallel"))(x, y, g1, g2)


def _final_loss(x, y, g, tgt, name):
    t, d = x.shape

    def body(x_ref, y_ref, g_ref, t_ref, l_ref, dx_ref):
        @pl.when(pl.program_id(0) == 0)
        def _():
            l_ref[...] = jnp.zeros_like(l_ref)

        err = x_ref[...] + _rms(y_ref[...], g_ref[...]) - t_ref[...]
        dx_ref[...] = err * (1.0 / d)
        l_ref[...] += jnp.sum(jnp.sum(err * err, axis=1, keepdims=True), axis=0, keepdims=True) * (0.5 / d)

    return pl.pallas_call(
        body, name=name, grid=(t // TM,),
        in_specs=[_row_spec(d), _row_spec(d), _vec_spec(d), _row_spec(d)],
        out_specs=[pl.BlockSpec((1, LANES), lambda i: (0, 0)), _row_spec(d)],
        out_shape=[jax.ShapeDtypeStruct((1, LANES), F32), jax.ShapeDtypeStruct((t, d), F32)],
        compiler_params=_cp("arbitrary"))(x, y, g, tgt)


def _rms_bwd_vals(dn, v, g):
    w = v.shape[-1]
    r = lax.rsqrt(jnp.mean(v * v, axis=-1, keepdims=True) + EPS)
    vh = v * r
    dgp = jnp.sum(dn * vh, axis=0, keepdims=True)
    dvh = dn * g
    dv = r * (dvh - vh * (jnp.sum(dvh * vh, axis=-1, keepdims=True) * (1.0 / w)))
    return dv, dgp


def _norm_bwd(dn, v, g, resid, out_dtype, name):
    t, d = v.shape
    has_res = resid is not None

    def body(*refs):
        if has_res:
            dn_ref, v_ref, g_ref, r_ref, dv_ref, dg_ref = refs
        else:
            dn_ref, v_ref, g_ref, dv_ref, dg_ref = refs

        @pl.when(pl.program_id(0) == 0)
        def _():
            dg_ref[...] = jnp.zeros_like(dg_ref)

        dv, dgp = _rms_bwd_vals(dn_ref[...].astype(F32), v_ref[...], g_ref[...])
        if has_res:
            dv = dv + r_ref[...]
        dv_ref[...] = dv.astype(out_dtype)
        dg_ref[...] += dgp

    in_specs = [_row_spec(d), _row_spec(d), _vec_spec(d)] + ([_row_spec(d)] if has_res else [])
    args = [dn, v, g] + ([resid] if has_res else [])
    return pl.pallas_call(
        body, name=name, grid=(t // TM,), in_specs=in_specs,
        out_specs=[_row_spec(d), _vec_spec(d)],
        out_shape=[jax.ShapeDtypeStruct((t, d), out_dtype), jax.ShapeDtypeStruct((1, d), F32)],
        compiler_params=_cp("arbitrary"))(*args)


def _rope_trig(pos, invf):
    ang = pos * invf
    return jnp.cos(ang), jnp.sin(ang)


def _rope_tables(trig, w, lo, half):
    c, s = trig
    lane = lax.broadcasted_iota(jnp.int32, c.shape, 1)
    active = (lane >= lo) & (lane < lo + 2 * half)
    cos = jnp.concatenate([jnp.where(active, c, 1.0)] * (w // LANES), axis=1)
    sin = jnp.concatenate([jnp.where(active, s, 0.0)] * (w // LANES), axis=1)
    lanes = lax.broadcasted_iota(jnp.int32, (c.shape[0], w), 1) & (LANES - 1)
    first = (lanes >= lo) & (lanes < lo + half)
    second = (lanes >= lo + half) & (lanes < lo + 2 * half)
    return cos, sin, first, second


def _rope_apply(v, cos, sin, first, second, half, sign):
    w = v.shape[-1]
    up = pltpu.roll(v, w - half, 1)
    dn = pltpu.roll(v, half, 1)
    rot = jnp.where(first, -up, jnp.where(second, dn, 0.0))
    return v * cos + rot * (sin * sign)


def _forget_lsf(proj, b_pad, name):
    t = proj.shape[0]

    def body(ff_ref, b_ref, o_ref):
        f = ff_ref[...] + b_ref[...]
        o_ref[...] = -(jnp.maximum(-f, 0.0) + jnp.log(1.0 + jnp.exp(-jnp.abs(f))))

    return pl.pallas_call(
        body, name=name, grid=(t // TM,),
        in_specs=[pl.BlockSpec((TM, LANES), lambda i: (i, OFF_FF // LANES)), _vec_spec(LANES)],
        out_specs=_row_spec(LANES), out_shape=jax.ShapeDtypeStruct((t, LANES), F32),
        compiler_params=_cp("parallel"))(proj, b_pad)


def _split3(c):
    hi = c.astype(BF16).astype(F32)
    mid = (c - hi).astype(BF16).astype(F32)
    return hi, mid, (c - hi) - mid


def _spread_heads(x):
    low = lax.broadcasted_iota(jnp.int32, (x.shape[0], LANES), 1) < HEAD_DIM
    out = []
    for c in range(x.shape[1] // LANES):
        blk = x[:, c * LANES:(c + 1) * LANES]
        out.append(jnp.where(low, blk, 0.0))
        out.append(jnp.where(low, pltpu.roll(blk, HEAD_DIM, 1), 0.0))
    return jnp.concatenate(out, axis=1)


def _gather_heads(y):
    low = lax.broadcasted_iota(jnp.int32, (y.shape[0], LANES), 1) < HEAD_DIM
    out = []
    for c in range(y.shape[1] // (2 * LANES)):
        a = y[:, 2 * c * LANES:(2 * c + 1) * LANES]
        b = y[:, (2 * c + 1) * LANES:(2 * c + 2) * LANES]
        out.append(jnp.where(low, a, pltpu.roll(b, HEAD_DIM, 1)))
    return jnp.concatenate(out, axis=1)


def _transposed_tiles(dst, row0, blocks, width):
    for b, blk in enumerate(blocks):
        bt = blk.T.astype(BF16)
        rows = bt.shape[0]
        for w in range(TM // width):
            dst[w, row0 + b * rows:row0 + (b + 1) * rows, :] = bt[:, w * width:(w + 1) * width]


def _prep_fwd(proj, cum, pos, invf, lg_lanes, g_q, g_kv, wq_pad, wkv_pad, name):
    t = proj.shape[0]

    def body(fox_ref, ret_ref, sb_ref, cq_ref, ckv_ref, kr_ref, cum_ref, pos_ref, invf_ref, lg_ref,
             gq_ref, gkv_ref, wq_ref, wkv_ref,
             ofox_ref, oret_ref, osb_ref, omla_ref, ofoxt_ref, osbt_ref, omlat_ref, orett_ref):
        pos_v = pos_ref[...]
        foxv, retv, sbv = [[_spread_heads(ref[:, g * GROUP:(g + 1) * GROUP]) for g in range(3)]
                           for ref in (fox_ref, ret_ref, sb_ref)]
        osb_ref[:, 0:HP] = (sbv[0] * 0.125).astype(BF16)
        osb_ref[:, HP:2 * HP] = sbv[1].astype(BF16)
        osb_ref[:, 2 * HP:QKV] = sbv[2].astype(BF16)
        _transposed_tiles(osbt_ref, 0, [sbv[1], sbv[2]], TKS)
        lane = lax.broadcasted_iota(jnp.int32, (TM, LANES), 1)
        cumv = cum_ref[...]
        fq, fk = [], []
        for hb in range(HEADS):
            hi, mid, lo = _split3(cumv[:, hb:hb + 1])
            q = foxv[0][:, hb * LANES:(hb + 1) * LANES] * 0.125
            k = foxv[1][:, hb * LANES:(hb + 1) * LANES]
            ones_q = (lane >= HEAD_DIM) & (lane < HEAD_DIM + 3)
            ones_k = (lane >= HEAD_DIM + 3) & (lane < HEAD_DIM + 6)
            q = jnp.where(ones_q, 1.0, q)
            k = jnp.where(ones_k, 1.0, k)
            for n, part in enumerate((hi, mid, lo)):
                q = jnp.where(lane == HEAD_DIM + 3 + n, part, q)
                k = jnp.where(lane == HEAD_DIM + n, -part, k)
            fq.append(q)
            fk.append(k)
        fk = jnp.concatenate(fk, axis=1)
        ofox_ref[:, 0:HP] = jnp.concatenate(fq, axis=1).astype(BF16)
        ofox_ref[:, HP:2 * HP] = fk.astype(BF16)
        ofox_ref[:, 2 * HP:QKV] = foxv[2].astype(BF16)
        _transposed_tiles(ofoxt_ref, 0, [fk, foxv[2]], BQ)
        trig = _rope_trig(pos_v, invf_ref[...])
        cos, sin, first, second = _rope_tables(trig, HP, 0, HEAD_DIM // 2)
        nloc = lax.broadcasted_iota(jnp.int32, (TM, 1), 0).astype(F32)
        dec = lg_ref[...] * nloc
        rq = _rope_apply(retv[0], cos, sin, first, second, HEAD_DIM // 2, 1.0)
        rk = _rope_apply(retv[1], cos, sin, first, second, HEAD_DIM // 2, 1.0)
        oret_ref[:, 0:HP] = (rq * jnp.exp(dec)).astype(BF16)
        rk = rk * 0.125 * jnp.exp(-dec)
        oret_ref[:, HP:2 * HP] = rk.astype(BF16)
        _transposed_tiles(orett_ref, 0, [rk], BQ)
        oret_ref[:, 2 * HP:QKV] = retv[2].astype(BF16)
        cosm, sinm, firstm, secondm = _rope_tables(trig, HP, HEAD_DIM, ROPE_MLA // 2)
        cqn = _rms(cq_ref[...], gq_ref[...]).astype(BF16)
        qm = _bdot(cqn, wq_ref[...])
        omla_ref[:, 0:HP] = _rope_apply(qm, cosm, sinm, firstm, secondm, ROPE_MLA // 2, 1.0).astype(BF16)
        ckvn = _rms(ckv_ref[...], gkv_ref[...]).astype(BF16)
        kv = _bdot(ckvn, wkv_ref[...])
        krr = _rope_apply(kr_ref[...], cosm[:, 0:LANES], sinm[:, 0:LANES], firstm[:, 0:LANES],
                          secondm[:, 0:LANES], ROPE_MLA // 2, 1.0)
        mk = kv[:, 0:HP] + jnp.concatenate([krr] * HEADS, axis=1)
        omla_ref[:, HP:2 * HP] = mk.astype(BF16)
        omla_ref[:, 2 * HP:QKV] = kv[:, HP:2 * HP].astype(BF16)
        _transposed_tiles(omlat_ref, 0, [mk, kv[:, HP:2 * HP]], BQ)

    def seg(off, w):
        return pl.BlockSpec((TM, w), lambda i, o=off // w: (i, o))

    def full(shape):
        return pl.BlockSpec(shape, lambda i: (0,) * len(shape))

    def tiles(width):
        return pl.BlockSpec((TM // width, 2 * HP, width), lambda i: (i, 0, 0))

    in_specs = [seg(OFF_FOX, QKV_IN), seg(OFF_RET, QKV_IN), seg(OFF_SB, QKV_IN), seg(OFF_CQ, Q_RANK),
                seg(OFF_CKV, LANES), seg(OFF_KR, LANES), _row_spec(LANES), pl.BlockSpec((TM, 1), lambda i: (i, 0)),
                full((1, LANES)), full((1, HP)), full((1, Q_RANK)), full((1, KV_RANK)),
                full((Q_RANK, HP)), full((KV_RANK, 2 * HP))]
    out_specs = [_row_spec(QKV)] * 4 + [tiles(BQ), tiles(TKS), tiles(BQ),
                                        pl.BlockSpec((1, HP, BQ), lambda i: (i, 0, 0))]
    out_shape = [jax.ShapeDtypeStruct((t, QKV), BF16)] * 4 + [
        jax.ShapeDtypeStruct((t // BQ, 2 * HP, BQ), BF16), jax.ShapeDtypeStruct((t // TKS, 2 * HP, TKS), BF16),
        jax.ShapeDtypeStruct((t // BQ, 2 * HP, BQ), BF16), jax.ShapeDtypeStruct((t // BQ, HP, BQ), BF16)]
    return pl.pallas_call(
        body, name=name, grid=(t // TM,), in_specs=in_specs, out_specs=out_specs, out_shape=out_shape,
        compiler_params=_cp("parallel"))(proj, proj, proj, proj, proj, proj, cum, pos, invf, lg_lanes,
                                         g_q, g_kv, wq_pad, wkv_pad)


def _prep_bwd(dfox, dret, dsb, dmla, drg, dlsf, proj, pos, invf, lg_lanes, b_pad, g_q, g_kv,
              wq_pad, wkv_pad, name):
    t = proj.shape[0]

    def body(dfq, dfk, dfv, drq, drk, drv, dsq, dsk, dsv, dmq, dmk, dmv, drg_ref, dlsf_ref,
             cq_ref, ckv_ref, ff_ref, pos_ref, invf_ref, lg_ref, b_ref, gq_ref, gkv_ref, wq_ref, wkv_ref,
             dp_ref, dwq_ref, dwkv_ref, dgq_ref, dgkv_ref, dbf_ref):
        @pl.when(pl.program_id(0) == 0)
        def _():
            dwq_ref[...] = jnp.zeros_like(dwq_ref)
            dwkv_ref[...] = jnp.zeros_like(dwkv_ref)
            dgq_ref[...] = jnp.zeros_like(dgq_ref)
            dgkv_ref[...] = jnp.zeros_like(dgkv_ref)
            dbf_ref[...] = jnp.zeros_like(dbf_ref)

        pos_v = pos_ref[...]
        def put(off, val):
            dp_ref[:, off:off + GROUP] = _gather_heads(val).astype(BF16)

        for off, (dq, dk, dv) in ((OFF_FOX, (dfq, dfk, dfv)), (OFF_SB, (dsq, dsk, dsv))):
            put(off, dq[...] * 0.125)
            put(off + GROUP, dk[...])
            put(off + 2 * GROUP, dv[...])
        trig = _rope_trig(pos_v, invf_ref[...])
        cos, sin, first, second = _rope_tables(trig, HP, 0, HEAD_DIM // 2)
        nloc = lax.broadcasted_iota(jnp.int32, (TM, 1), 0).astype(F32)
        dec = lg_ref[...] * nloc
        dq = _rope_apply(drq[...] * jnp.exp(dec), cos, sin, first, second, HEAD_DIM // 2, -1.0)
        dk = _rope_apply(drk[...] * (0.125 * jnp.exp(-dec)), cos, sin, first, second, HEAD_DIM // 2, -1.0)
        put(OFF_RET, dq)
        put(OFF_RET + GROUP, dk)
        put(OFF_RET + 2 * GROUP, drv[...])
        put(OFF_RG, drg_ref[...])
        cosm, sinm, firstm, secondm = _rope_tables(trig, HP, HEAD_DIM, ROPE_MLA // 2)
        dql = _rope_apply(dmq[...], cosm, sinm, firstm, secondm, ROPE_MLA // 2, -1.0).astype(BF16)
        cq = cq_ref[...]
        cqn = _rms(cq, gq_ref[...]).astype(BF16)
        dwq_ref[...] += _bdot(cqn, dql, TN)
        dcqn = _bdot(dql, wq_ref[...], NT)
        dcq, dgq = _rms_bwd_vals(dcqn, cq, gq_ref[...])
        dgq_ref[...] += dgq
        dp_ref[:, OFF_CQ:OFF_CQ + Q_RANK] = dcq.astype(BF16)
        dkm = dmk[...]
        dkv = jnp.concatenate([dkm, dmv[...]], axis=1).astype(BF16)
        ckv = ckv_ref[...]
        ckvn = _rms(ckv, gkv_ref[...]).astype(BF16)
        dwkv_ref[...] += _bdot(ckvn, dkv, TN)
        dckvn = _bdot(dkv, wkv_ref[...], NT)
        dckv, dgkv = _rms_bwd_vals(dckvn, ckv, gkv_ref[...])
        dgkv_ref[...] += dgkv
        dp_ref[:, OFF_CKV:OFF_CKV + LANES] = dckv.astype(BF16)
        dkr = dkm[:, 0:LANES] + dkm[:, LANES:2 * LANES] + dkm[:, 2 * LANES:3 * LANES] + dkm[:, 3 * LANES:HP]
        act = firstm[:, 0:LANES] | secondm[:, 0:LANES]
        dkr = jnp.where(act, dkr, 0.0)
        dkr = _rope_apply(dkr, cosm[:, 0:LANES], sinm[:, 0:LANES], firstm[:, 0:LANES], secondm[:, 0:LANES],
                          ROPE_MLA // 2, -1.0)
        dp_ref[:, OFF_KR:OFF_KR + LANES] = dkr.astype(BF16)
        f = ff_ref[...] + b_ref[...]
        dff = dlsf_ref[...] / (1.0 + jnp.exp(f))
        dbf_ref[...] += jnp.sum(dff, axis=0, keepdims=True)
        dp_ref[:, OFF_FF:OFF_FF + LANES] = dff.astype(BF16)
        dp_ref[:, OFF_FF + LANES:NP_IN] = jnp.zeros((TM, NP_IN - OFF_FF - LANES), BF16)

    def seg(off, w):
        return pl.BlockSpec((TM, w), lambda i, o=off // w: (i, o))

    def full(shape):
        return pl.BlockSpec(shape, lambda i: (0,) * len(shape))

    hp_spec = _row_spec(HP)
    in_specs = [hp_spec] * 13 + [_row_spec(LANES), seg(OFF_CQ, Q_RANK), seg(OFF_CKV, LANES), seg(OFF_FF, LANES),
                                 pl.BlockSpec((TM, 1), lambda i: (i, 0)),
                                 full((1, LANES)), full((1, HP)), full((1, LANES)), full((1, Q_RANK)),
                                 full((1, KV_RANK)), full((Q_RANK, HP)), full((KV_RANK, 2 * HP))]
    out_specs = [_row_spec(NP_IN), full((Q_RANK, HP)), full((KV_RANK, 2 * HP)), full((1, Q_RANK)),
                 full((1, KV_RANK)), full((1, LANES))]
    out_shape = [jax.ShapeDtypeStruct((t, NP_IN), BF16), jax.ShapeDtypeStruct((Q_RANK, HP), F32),
                 jax.ShapeDtypeStruct((KV_RANK, 2 * HP), F32), jax.ShapeDtypeStruct((1, Q_RANK), F32),
                 jax.ShapeDtypeStruct((1, KV_RANK), F32), jax.ShapeDtypeStruct((1, LANES), F32)]
    return pl.pallas_call(
        body, name=name, grid=(t // TM,), in_specs=in_specs, out_specs=out_specs, out_shape=out_shape,
        compiler_params=_cp("arbitrary"))(*dfox, *dret, *dsb, *dmla, drg, dlsf, proj, proj, proj, pos, invf,
                                          lg_lanes, b_pad, g_q, g_kv, wq_pad, wkv_pad)


def _cumsum(x, reverse, name, partials=None):
    t, w = x.shape
    n = t // TM
    xs = [x] if partials is None else [x, partials]

    def body(*refs):
        x_refs, o_ref, carry = refs[:len(xs)], refs[len(xs)], refs[len(xs) + 1]

        @pl.when(pl.program_id(0) == 0)
        def _():
            carry[...] = jnp.zeros_like(carry)

        r = lax.broadcasted_iota(jnp.int32, (TM, TM), 0)
        c = lax.broadcasted_iota(jnp.int32, (TM, TM), 1)
        tri = jnp.where((r <= c) if reverse else (r >= c), 1.0, 0.0).astype(BF16)
        v = x_refs[0][...]
        if partials is not None:
            lane = lax.broadcasted_iota(jnp.int32, (TM, LANES), 1)
            for hb in range(HEADS):
                v = v + jnp.where(lane == hb, jnp.sum(x_refs[1][:, _hs(hb)], axis=1, keepdims=True), 0.0)
        hi = v.astype(BF16)
        r1 = v - hi.astype(F32)
        mid = r1.astype(BF16)
        lo = (r1 - mid.astype(F32)).astype(BF16)
        cs = _bdot(tri, hi) + _bdot(tri, mid) + _bdot(tri, lo) + carry[...]
        o_ref[...] = cs
        carry[...] = cs[0:1, :] if reverse else cs[TM - 1:TM, :]

    imap = (lambda i: (n - 1 - i, 0)) if reverse else (lambda i: (i, 0))
    return pl.pallas_call(
        body, name=name, grid=(n,), in_specs=[pl.BlockSpec((TM, a.shape[1]), imap) for a in xs],
        out_specs=pl.BlockSpec((TM, w), imap),
        out_shape=jax.ShapeDtypeStruct((t, w), F32), scratch_shapes=[pltpu.VMEM((1, w), F32)],
        compiler_params=_cp("arbitrary"))(*xs)


HB_FWD = 4
HB_BWD = 4
HB_SB_FWD = 4
BQS = 512
BQA = 512


def _q_spec(hb, bq=BQ):
    return pl.BlockSpec((bq, hb * LANES), lambda g, i: (i, g))


ONE_BUFFER = pl.Buffered(1)


def _kv_spec(t, which, hb):
    return pl.BlockSpec((t, hb * LANES), lambda g, i, w=which: (0, w * (HEADS // hb) + g), pipeline_mode=ONE_BUFFER)


def _acc_spec(t, hb):
    return pl.BlockSpec((t, hb * LANES), lambda g, i: (0, g), pipeline_mode=ONE_BUFFER)


def _hs(hh):
    return slice(hh * LANES, (hh + 1) * LANES)


def _tile_iota(rows, cols):
    return (lax.broadcasted_iota(jnp.int32, (rows, cols), 0), lax.broadcasted_iota(jnp.int32, (rows, cols), 1))


def _kvt_spec(nkv, width, which, hb):
    return pl.BlockSpec((nkv, hb * LANES, width), lambda g, i, w=which: (0, w * (HEADS // hb) + g, 0),
                        pipeline_mode=ONE_BUFFER)


def _qrow_spec(hb, bq=BQ):
    return pl.BlockSpec((hb, 1, 1, bq), lambda g, i: (g, i, 0, 0))


def _vis(key0, query0, rows, cols, kind):
    r, c = _tile_iota(rows, cols)
    k, q = key0 + r, query0 + c
    if kind == "chunk":
        return (k >> CHUNK_SHIFT) <= (q >> CHUNK_SHIFT)
    return (k < q) if kind == "strict" else (k <= q)


def _softmax_fwd(qkv, kvt, *, chunk_mask, scale, name):
    t = qkv.shape[0]
    nq = t // BQA
    per = BQA // BQ
    hb = HB_FWD
    kind = "chunk" if chunk_mask else "causal"

    def body(q_ref, k_ref, vt_ref, o_ref, lse_ref, m_sc, l_sc, acc_sc):
        i = pl.program_id(1)
        m_sc[...] = jnp.full((hb, 1, BQA), NEG, F32)
        l_sc[...] = jnp.zeros((hb, 1, BQA), F32)
        acc_sc[...] = jnp.zeros((hb, LANES, BQA), F32)

        def tile(j, qoff):
            off = pl.multiple_of(j * BQ, BQ)
            lo = 0 if qoff is None else qoff
            qs = slice(lo, BQA)
            vis = None if qoff is None else _vis(off, i * BQA + lo, BQ, BQA - lo, kind)
            ss = [_bdot(k_ref[pl.ds(off, BQ), _hs(hh)], q_ref[qs, _hs(hh)], NT) for hh in range(hb)]
            ps, alphas = [], []
            for hh in range(hb):
                s = ss[hh]
                if scale != 1.0:
                    s = s * scale
                if vis is not None:
                    s = jnp.where(vis, s, NEG)
                m_old = m_sc[hh, :, qs]
                m_new = jnp.maximum(m_old, jnp.max(s, axis=0, keepdims=True))
                alpha = jnp.exp(m_old - m_new)
                p = jnp.exp(s - m_new)
                l_sc[hh, :, qs] = alpha * l_sc[hh, :, qs] + jnp.sum(p, axis=0, keepdims=True)
                m_sc[hh, :, qs] = m_new
                ps.append(p.astype(BF16))
                alphas.append(alpha)
            for hh in range(hb):
                acc_sc[hh, :, qs] = alphas[hh] * acc_sc[hh, :, qs] + _bdot(vt_ref[j, _hs(hh), :], ps[hh])

        def loop(j, carry):
            tile(j, None)
            return carry

        lax.fori_loop(0, per * i, loop, 0)
        for d in range(per):
            tile(per * i + d, d * BQ)
        for hh in range(hb):
            l = l_sc[hh]
            o_ref[:, _hs(hh)] = (acc_sc[hh] / l).T
            lse_ref[hh, 0] = m_sc[hh] + jnp.log(l)

    return pl.pallas_call(
        body, name=name, grid=(HEADS // hb, nq),
        in_specs=[_q_spec(hb, BQA), _kv_spec(t, 1, hb), _kvt_spec(t // BQ, BQ, 1, hb)],
        out_specs=[_q_spec(hb, BQA), _qrow_spec(hb, BQA)],
        out_shape=[jax.ShapeDtypeStruct((t, HP), F32), jax.ShapeDtypeStruct((HEADS, nq, 1, BQA), F32)],
        scratch_shapes=[pltpu.VMEM((hb, 1, BQA), F32), pltpu.VMEM((hb, 1, BQA), F32),
                        pltpu.VMEM((hb, LANES, BQA), F32)],
        compiler_params=_cp("parallel", "arbitrary"))(qkv, qkv, kvt)


def _softmax_bwd(qkv, kvt, do, lse, delta, *, bias, chunk_mask, scale, name):
    t = qkv.shape[0]
    nq = t // BQA
    per = BQA // BQ
    hb = HB_BWD
    kind = "chunk" if chunk_mask else "causal"

    def body(*refs):
        if bias:
            (q_ref, k_ref, v_ref, kt_ref, do_ref, lse_ref, dl_ref, dq_ref, dk_ref, dv_ref, dck_ref, dcq_ref,
             dq_sc, dcq_sc) = refs
            dcq_sc[...] = jnp.zeros((hb, 1, BQA), F32)
        else:
            q_ref, k_ref, v_ref, kt_ref, do_ref, lse_ref, dl_ref, dq_ref, dk_ref, dv_ref, dq_sc = refs
        i = pl.program_id(1)

        @pl.when(i == 0)
        def _():
            dk_ref[...] = jnp.zeros_like(dk_ref)
            dv_ref[...] = jnp.zeros_like(dv_ref)
            if bias:
                dck_ref[...] = jnp.zeros_like(dck_ref)

        dq_sc[...] = jnp.zeros((hb, LANES, BQA), F32)

        def tile(j, qoff):
            off = pl.multiple_of(j * BQ, BQ)
            lo = 0 if qoff is None else qoff
            qsl = slice(lo, BQA)
            vis = None if qoff is None else _vis(off, i * BQA + lo, BQ, BQA - lo, kind)
            qs = [q_ref[qsl, _hs(hh)] for hh in range(hb)]
            dobs = [do_ref[qsl, _hs(hh)].astype(BF16) for hh in range(hb)]
            ss = [_bdot(k_ref[pl.ds(off, BQ), _hs(hh)], qs[hh], NT) for hh in range(hb)]
            dps = [_bdot(v_ref[pl.ds(off, BQ), _hs(hh)], dobs[hh], NT) for hh in range(hb)]
            pbs, dsbs = [], []
            for hh in range(hb):
                s = ss[hh]
                if scale != 1.0:
                    s = s * scale
                p = jnp.exp(s - lse_ref[hh, 0, :, qsl])
                if vis is not None:
                    p = jnp.where(vis, p, 0.0)
                ds = p * (dps[hh] - dl_ref[hh, 0, :, qsl])
                if bias:
                    part = ds[:, 0:LANES]
                    for b in range(1, (BQA - lo) // LANES):
                        part = part + ds[:, b * LANES:(b + 1) * LANES]
                    dck_ref[pl.ds(off, BQ), _hs(hh)] -= part
                    dcq_sc[hh, :, qsl] += jnp.sum(ds, axis=0, keepdims=True)
                if scale != 1.0:
                    ds = ds * scale
                pbs.append(p.astype(BF16))
                dsbs.append(ds.astype(BF16))
            for hh in range(hb):
                sl = _hs(hh)
                dv_ref[pl.ds(off, BQ), sl] += _bdot(pbs[hh], dobs[hh])
                dk_ref[pl.ds(off, BQ), sl] += _bdot(dsbs[hh], qs[hh])
                dq_sc[hh, :, qsl] += _bdot(kt_ref[j, sl, :], dsbs[hh])

        def loop(j, carry):
            tile(j, None)
            return carry

        lax.fori_loop(0, per * i, loop, 0)
        for d in range(per):
            tile(per * i + d, d * BQ)
        for hh in range(hb):
            dq_ref[:, _hs(hh)] = dq_sc[hh].T
            if bias:
                dcq_ref[hh, 0] = dcq_sc[hh]

    in_specs = [_q_spec(hb, BQA), _kv_spec(t, 1, hb), _kv_spec(t, 2, hb), _kvt_spec(t // BQ, BQ, 0, hb),
                _q_spec(hb, BQA), _qrow_spec(hb, BQA), _qrow_spec(hb, BQA)]
    out_specs = [_q_spec(hb, BQA), _acc_spec(t, hb), _acc_spec(t, hb)]
    out_shape = [jax.ShapeDtypeStruct((t, HP), F32)] * 3
    scratch = [pltpu.VMEM((hb, LANES, BQA), F32)]
    if bias:
        out_specs += [_acc_spec(t, hb), _qrow_spec(hb, BQA)]
        out_shape += [jax.ShapeDtypeStruct((t, HP), F32), jax.ShapeDtypeStruct((HEADS, nq, 1, BQA), F32)]
        scratch.append(pltpu.VMEM((hb, 1, BQA), F32))
    return pl.pallas_call(
        body, name=name, grid=(HEADS // hb, nq), in_specs=in_specs, out_specs=out_specs, out_shape=out_shape,
        scratch_shapes=scratch,
        compiler_params=_cp("parallel", "arbitrary"))(qkv, qkv, qkv, kvt, do, lse, delta)


def _ret_diag_decay(lg1, keys_on_rows=False):
    r, c = _tile_iota(BQ, BQ)
    qn, km = (c, r) if keys_on_rows else (r, c)
    dd = jnp.where(km > qn, jnp.exp((2.0 * lg1) * (km - qn).astype(F32)), 1.0)
    return jnp.where((km >> CHUNK_SHIFT) <= (qn >> CHUNK_SHIFT), dd, 0.0)


def _lg_spec(hb):
    return pl.BlockSpec((hb, 1, LANES), lambda g, i: (g, 0, 0))


def _ret_specs(nq, hb, reverse):
    tile = (lambda i: nq - 1 - i) if reverse else (lambda i: i)
    qkv = [pl.BlockSpec((BQ, hb * LANES), lambda g, i, w=w: (tile(i), w * (HEADS // hb) + g)) for w in range(3)]
    kt = pl.BlockSpec((1, hb * LANES, BQ), lambda g, i: (tile(i), g, 0))
    st = pl.BlockSpec((1, hb * LANES, LANES), lambda g, i: (tile(i), g, 0))
    return qkv, kt, st


def _ret_fwd(qkv, kt, lg_heads, name):
    t = qkv.shape[0]
    nq = t // BQ
    hb = HB_FWD

    def body(lg_ref, q_ref, k_ref, v_ref, kt_ref, o_ref, st_ref, s_sc):
        @pl.when(pl.program_id(1) == 0)
        def _():
            s_sc[...] = jnp.zeros_like(s_sc)

        qs = [q_ref[:, _hs(hh)] for hh in range(hb)]
        vs = [v_ref[:, _hs(hh)] for hh in range(hb)]
        aa = [_bdot(qs[hh], k_ref[:, _hs(hh)], NT) for hh in range(hb)]
        kv = [_bdot(kt_ref[0, _hs(hh), :], vs[hh]) for hh in range(hb)]
        for hh in range(hb):
            sl = _hs(hh)
            lg1 = lg_ref[hh][:, 0:1]
            s = s_sc[hh]
            st_ref[0, sl, :] = s
            shi, slo = _split2(s)
            a = (aa[hh] * _ret_diag_decay(lg1)).astype(BF16)
            o_ref[:, sl] = _bdot(a, vs[hh]) + _bdot(qs[hh], shi) + _bdot(qs[hh], slo)
            s_sc[hh] = jnp.exp(lg1 * float(BQ)) * (s + kv[hh])

    qkv_specs, kt_spec, st_spec = _ret_specs(nq, hb, False)
    return pl.pallas_call(
        body, name=name, grid=(HEADS // hb, nq), in_specs=[_lg_spec(hb)] + qkv_specs + [kt_spec],
        out_specs=[_q_spec(hb), st_spec],
        out_shape=[jax.ShapeDtypeStruct((t, HP), F32), jax.ShapeDtypeStruct((nq, HP, LANES), F32)],
        scratch_shapes=[pltpu.VMEM((hb, LANES, LANES), F32)],
        compiler_params=_cp("parallel", "arbitrary"))(lg_heads, qkv, qkv, qkv, kt)


def _ret_bwd(qkv, states, lg_heads, do, name):
    t = qkv.shape[0]
    nq = t // BQ
    hb = HB_FWD

    def body(lg_ref, q_ref, k_ref, v_ref, st_ref, do_ref, dq_ref, dk_ref, dv_ref, g_sc):
        @pl.when(pl.program_id(1) == 0)
        def _():
            g_sc[...] = jnp.zeros_like(g_sc)

        qs = [q_ref[:, _hs(hh)] for hh in range(hb)]
        ks = [k_ref[:, _hs(hh)] for hh in range(hb)]
        vs = [v_ref[:, _hs(hh)] for hh in range(hb)]
        dobs = [do_ref[:, _hs(hh)].astype(BF16) for hh in range(hb)]
        aa = [_bdot(ks[hh], qs[hh], NT) for hh in range(hb)]
        das = [_bdot(vs[hh], dobs[hh], NT) for hh in range(hb)]
        qdo = [_bdot(qs[hh], dobs[hh], TN) for hh in range(hb)]
        for hh in range(hb):
            sl = _hs(hh)
            lg1 = lg_ref[hh][:, 0:1]
            dd = _ret_diag_decay(lg1, keys_on_rows=True)
            at = (aa[hh] * dd).astype(BF16)
            dat = (das[hh] * dd).astype(BF16)
            h = jnp.exp(lg1 * float(BQ)) * g_sc[hh]
            hhi, hlo = _split2(h)
            shi, slo = _split2(st_ref[0, sl, :])
            dv_ref[:, sl] = _bdot(at, dobs[hh]) + _bdot(ks[hh], hhi) + _bdot(ks[hh], hlo)
            dk_ref[:, sl] = _bdot(dat, qs[hh]) + _bdot(vs[hh], hhi, NT) + _bdot(vs[hh], hlo, NT)
            dq_ref[:, sl] = _bdot(dat, ks[hh], TN) + _bdot(dobs[hh], shi, NT) + _bdot(dobs[hh], slo, NT)
            g_sc[hh] = qdo[hh] + h

    qkv_specs, _, st_spec = _ret_specs(nq, hb, True)
    tile_spec = pl.BlockSpec((BQ, hb * LANES), lambda g, i: (nq - 1 - i, g))
    return pl.pallas_call(
        body, name=name, grid=(HEADS // hb, nq), in_specs=[_lg_spec(hb)] + qkv_specs + [st_spec, tile_spec],
        out_specs=[tile_spec] * 3, out_shape=[jax.ShapeDtypeStruct((t, HP), F32)] * 3,
        scratch_shapes=[pltpu.VMEM((hb, LANES, LANES), F32)],
        compiler_params=_cp("parallel", "arbitrary"))(lg_heads, qkv, qkv, qkv, states, do)


def _sb_tile_logs(q, kb, vis):
    z = _bdot(kb, q, NT)
    ls = -(jnp.maximum(z, 0.0) + jnp.log(1.0 + jnp.exp(-jnp.abs(z))))
    if vis is not None:
        ls = jnp.where(vis, ls, 0.0)
    return z, ls


def _sb_later(ls, after):
    hi, lo = _split2(ls)
    return _bdot(after, hi) + _bdot(after, lo)


def _sb_fwd(qkv, kvt, name):
    t = qkv.shape[0]
    nq = t // BQS
    per = BQS // TKS
    hb = HB_SB_FWD

    def body(q_ref, k_ref, vt_ref, o_ref, tot_ref, acc_sc, r_sc):
        i = pl.program_id(1)
        acc_sc[...] = jnp.zeros((hb, LANES, BQS), F32)
        r_sc[...] = jnp.zeros((hb, 1, BQS), F32)
        mr, mc = _tile_iota(TKS, TKS)
        after = jnp.where(mc > mr, 1.0, 0.0).astype(BF16)

        def tile(j, qoff):
            off = pl.multiple_of(j * TKS, TKS)
            lo = 0 if qoff is None else qoff
            qsl = slice(lo, BQS)
            vis = None if qoff is None else _vis(off, i * BQS + lo, TKS, BQS - lo, "strict")
            zl = [_sb_tile_logs(q_ref[qsl, _hs(hh)], k_ref[pl.ds(off, TKS), _hs(hh)], vis) for hh in range(hb)]
            laters = [_sb_later(zl[hh][1], after) for hh in range(hb)]
            ws = []
            for hh in range(hb):
                z, ls = zl[hh]
                w = jnp.exp(z + ls + laters[hh] + r_sc[hh, :, qsl])
                if vis is not None:
                    w = jnp.where(vis, w, 0.0)
                ws.append(w.astype(BF16))
                r_sc[hh, :, qsl] += jnp.sum(ls, axis=0, keepdims=True)
            for hh in range(hb):
                acc_sc[hh, :, qsl] += _bdot(vt_ref[j, _hs(hh), :], ws[hh])

        for d in reversed(range(per)):
            tile(per * i + d, d * TKS)

        def loop(jj, carry):
            tile(per * i - 1 - jj, None)
            return carry

        lax.fori_loop(0, per * i, loop, 0)
        for hh in range(hb):
            o_ref[:, _hs(hh)] = acc_sc[hh].T
            tot_ref[hh, 0] = r_sc[hh]

    return pl.pallas_call(
        body, name=name, grid=(HEADS // hb, nq),
        in_specs=[_q_spec(hb, BQS), _kv_spec(t, 1, hb), _kvt_spec(t // TKS, TKS, 1, hb)],
        out_specs=[_q_spec(hb, BQS), _qrow_spec(hb, BQS)],
        out_shape=[jax.ShapeDtypeStruct((t, HP), F32), jax.ShapeDtypeStruct((HEADS, nq, 1, BQS), F32)],
        scratch_shapes=[pltpu.VMEM((hb, LANES, BQS), F32), pltpu.VMEM((hb, 1, BQS), F32)],
        compiler_params=_cp("parallel", "arbitrary"))(qkv, qkv, kvt)


def _sb_bwd(qkv, kvt, do, tot, name):
    t = qkv.shape[0]
    nq = t // BQS
    per = BQS // TKS
    hb = HB_BWD

    def body(q_ref, k_ref, v_ref, kt_ref, do_ref, tot_ref, dq_ref, dk_ref, dv_ref, dq_sc, p_sc, g_sc):
        i = pl.program_id(1)

        @pl.when(i == 0)
        def _():
            dk_ref[...] = jnp.zeros_like(dk_ref)
            dv_ref[...] = jnp.zeros_like(dv_ref)

        dq_sc[...] = jnp.zeros((hb, LANES, BQS), F32)
        p_sc[...] = jnp.zeros((hb, 1, BQS), F32)
        g_sc[...] = jnp.zeros((hb, 1, BQS), F32)
        mr, mc = _tile_iota(TKS, TKS)
        after = jnp.where(mc > mr, 1.0, 0.0).astype(BF16)
        before = jnp.where(mc < mr, 1.0, 0.0).astype(BF16)

        def tile(j, qoff):
            off = pl.multiple_of(j * TKS, TKS)
            lo = 0 if qoff is None else qoff
            qsl = slice(lo, BQS)
            vis = None if qoff is None else _vis(off, i * BQS + lo, TKS, BQS - lo, "strict")
            qs = [q_ref[qsl, _hs(hh)] for hh in range(hb)]
            dobs = [do_ref[qsl, _hs(hh)].astype(BF16) for hh in range(hb)]
            zl = [_sb_tile_logs(qs[hh], k_ref[pl.ds(off, TKS), _hs(hh)], vis) for hh in range(hb)]
            dws = [_bdot(v_ref[pl.ds(off, TKS), _hs(hh)], dobs[hh], NT) for hh in range(hb)]
            laters = [_sb_later(zl[hh][1], after) for hh in range(hb)]
            ws, gs = [], []
            for hh in range(hb):
                z, ls = zl[hh]
                own = jnp.sum(ls, axis=0, keepdims=True)
                rest = tot_ref[hh, 0, :, qsl] - p_sc[hh, :, qsl] - own
                w = jnp.exp(z + ls + laters[hh] + rest)
                if vis is not None:
                    w = jnp.where(vis, w, 0.0)
                p_sc[hh, :, qsl] += own
                ws.append(w.astype(BF16))
                gs.append(dws[hh] * w)
            gins = []
            for hh in range(hb):
                ghi, glo = _split2(gs[hh])
                gins.append(_bdot(before, ghi) + _bdot(before, glo))
            dzbs = []
            for hh in range(hb):
                g = gs[hh]
                stay = jnp.exp(zl[hh][1])
                dz = g * stay - (1.0 - stay) * (gins[hh] + g_sc[hh, :, qsl])
                if vis is not None:
                    dz = jnp.where(vis, dz, 0.0)
                g_sc[hh, :, qsl] += jnp.sum(g, axis=0, keepdims=True)
                dzbs.append(dz.astype(BF16))
            for hh in range(hb):
                sl = _hs(hh)
                dv_ref[pl.ds(off, TKS), sl] += _bdot(ws[hh], dobs[hh])
                dk_ref[pl.ds(off, TKS), sl] += _bdot(dzbs[hh], qs[hh])
                dq_sc[hh, :, qsl] += _bdot(kt_ref[j, sl, :], dzbs[hh])

        def loop(j, carry):
            tile(j, None)
            return carry

        lax.fori_loop(0, per * i, loop, 0)
        for d in range(per):
            tile(per * i + d, d * TKS)
        for hh in range(hb):
            dq_ref[:, _hs(hh)] = dq_sc[hh].T

    return pl.pallas_call(
        body, name=name, grid=(HEADS // hb, nq),
        in_specs=[_q_spec(hb, BQS), _kv_spec(t, 1, hb), _kv_spec(t, 2, hb), _kvt_spec(t // TKS, TKS, 0, hb),
                  _q_spec(hb, BQS), _qrow_spec(hb, BQS)],
        out_specs=[_q_spec(hb, BQS), _acc_spec(t, hb), _acc_spec(t, hb)],
        out_shape=[jax.ShapeDtypeStruct((t, HP), F32)] * 3,
        scratch_shapes=[pltpu.VMEM((hb, LANES, BQS), F32), pltpu.VMEM((hb, 1, BQS), F32),
                        pltpu.VMEM((hb, 1, BQS), F32)],
        compiler_params=_cp("parallel", "arbitrary"))(qkv, qkv, qkv, kvt, do, tot)


def _sigmoid(v):
    return 1.0 / (1.0 + jnp.exp(-v))


def _post_fwd(oa, ob, oc, od, proj, g_pad, name):
    t = oa.shape[0]

    def body(oa_ref, ob_ref, oc_ref, od_ref, rg_ref, g_ref, mx_ref):
        g = g_ref[...]

        def group(o, gg):
            r = lax.rsqrt(jnp.sum(o * o, axis=-1, keepdims=True) * (1.0 / GROUP) + EPS)
            return _gather_heads(o * r * gg).astype(BF16)

        mx_ref[:, 0:GROUP] = group(oa_ref[...], g[:, 0:HP])
        mx_ref[:, GROUP:2 * GROUP] = group(ob_ref[...], g[:, HP:2 * HP])
        mx_ref[:, 3 * GROUP:4 * GROUP] = group(od_ref[...], g[:, 3 * HP:4 * HP])
        real = lax.broadcasted_iota(jnp.int32, (TM, LANES), 1) < HEAD_DIM
        rg = _spread_heads(rg_ref[...])
        gated = []
        for hb in range(HEADS):
            sl = slice(hb * LANES, (hb + 1) * LANES)
            o = oc_ref[:, sl]
            mu = jnp.sum(o, axis=-1, keepdims=True) * (1.0 / HEAD_DIM)
            dlt = jnp.where(real, o - mu, 0.0)
            var = jnp.sum(dlt * dlt, axis=-1, keepdims=True) * (1.0 / HEAD_DIM)
            yn = dlt * lax.rsqrt(var + EPS) * g[:, 2 * HP + hb * LANES:2 * HP + (hb + 1) * LANES]
            x = rg[:, sl]
            gated.append(yn * (x * _sigmoid(x)))
        mx_ref[:, 2 * GROUP:3 * GROUP] = _gather_heads(jnp.concatenate(gated, axis=1)).astype(BF16)

    rg_spec = pl.BlockSpec((TM, GROUP), lambda i: (i, OFF_RG // GROUP))
    return pl.pallas_call(
        body, name=name, grid=(t // TM,),
        in_specs=[_row_spec(HP)] * 4 + [rg_spec, _vec_spec(4 * HP)], out_specs=_row_spec(D_MODEL),
        out_shape=jax.ShapeDtypeStruct((t, D_MODEL), BF16), compiler_params=_cp("parallel"))(oa, ob, oc, od, proj, g_pad)


def _post_bwd(dmx, oa, ob, oc, od, proj, g_pad, name):
    t = oa.shape[0]

    def body(dm_ref, oa_ref, ob_ref, oc_ref, od_ref, rg_ref, g_ref,
             doa_ref, dob_ref, doc_ref, dod_ref, dla_ref, dlb_ref, drg_ref, dg_ref):
        @pl.when(pl.program_id(0) == 0)
        def _():
            dg_ref[...] = jnp.zeros_like(dg_ref)

        g = g_ref[...]

        def group_bwd(dm, o, gg):
            r = lax.rsqrt(jnp.sum(o * o, axis=-1, keepdims=True) * (1.0 / GROUP) + EPS)
            oh = o * r
            dgp = jnp.sum(dm * oh, axis=0, keepdims=True)
            dyh = dm * gg
            do = r * (dyh - oh * (jnp.sum(dyh * oh, axis=-1, keepdims=True) * (1.0 / GROUP)))
            return do, dgp

        def delta_bc(do, o):
            prod = do * o
            lane = lax.broadcasted_iota(jnp.int32, (TM, LANES), 1)
            out = jnp.zeros((TM, LANES), F32)
            for hb in range(HEADS):
                out = jnp.where(lane == hb, jnp.sum(prod[:, hb * LANES:(hb + 1) * LANES], axis=-1, keepdims=True), out)
            return out

        dmp = [_spread_heads(dm_ref[:, gi * GROUP:(gi + 1) * GROUP]) for gi in range(4)]
        rg = _spread_heads(rg_ref[...])
        oa = oa_ref[...]
        do_a, dga = group_bwd(dmp[0], oa, g[:, 0:HP])
        doa_ref[...] = do_a
        dla_ref[...] = delta_bc(do_a, oa)
        dg_ref[:, 0:HP] += dga
        ob = ob_ref[...]
        do_b, dgb = group_bwd(dmp[1], ob, g[:, HP:2 * HP])
        dob_ref[...] = do_b
        dlb_ref[...] = delta_bc(do_b, ob)
        dg_ref[:, HP:2 * HP] += dgb
        do_d, dgd = group_bwd(dmp[3], od_ref[...], g[:, 3 * HP:4 * HP])
        dod_ref[...] = do_d
        dg_ref[:, 3 * HP:4 * HP] += dgd
        real = lax.broadcasted_iota(jnp.int32, (TM, LANES), 1) < HEAD_DIM
        for hb in range(HEADS):
            sl = slice(hb * LANES, (hb + 1) * LANES)
            gsl = slice(2 * HP + hb * LANES, 2 * HP + (hb + 1) * LANES)
            o = oc_ref[:, sl]
            mu = jnp.sum(o, axis=-1, keepdims=True) * (1.0 / HEAD_DIM)
            dlt = jnp.where(real, o - mu, 0.0)
            var = jnp.sum(dlt * dlt, axis=-1, keepdims=True) * (1.0 / HEAD_DIM)
            rstd = lax.rsqrt(var + EPS)
            dhat = dlt * rstd
            gc = g[:, gsl]
            x = rg[:, sl]
            sg = _sigmoid(x)
            dm = dmp[2][:, sl]
            drg_ref[:, sl] = dm * (dhat * gc) * (sg * (1.0 + x * (1.0 - sg)))
            dyn = dm * (x * sg)
            dg_ref[:, gsl] += jnp.sum(dyn * dhat, axis=0, keepdims=True)
            ddh = dyn * gc
            m1 = jnp.sum(ddh, axis=-1, keepdims=True) * (1.0 / HEAD_DIM)
            m2 = jnp.sum(ddh * dhat, axis=-1, keepdims=True) * (1.0 / HEAD_DIM)
            doc_ref[:, sl] = jnp.where(real, rstd * (ddh - m1 - dhat * m2), 0.0)

    rg_spec = pl.BlockSpec((TM, GROUP), lambda i: (i, OFF_RG // GROUP))
    hp = _row_spec(HP)
    return pl.pallas_call(
        body, name=name, grid=(t // TM,),
        in_specs=[_row_spec(D_MODEL), hp, hp, hp, hp, rg_spec, _vec_spec(4 * HP)],
        out_specs=[hp] * 4 + [_row_spec(LANES)] * 2 + [hp, _vec_spec(4 * HP)],
        out_shape=[jax.ShapeDtypeStruct((t, HP), F32)] * 4 + [jax.ShapeDtypeStruct((t, LANES), F32)] * 2
        + [jax.ShapeDtypeStruct((t, HP), F32), jax.ShapeDtypeStruct((1, 4 * HP), F32)],
        compiler_params=_cp("arbitrary"))(dmx, oa, ob, oc, od, proj, g_pad)


def _mesh_pos():
    return lax.axis_index("x"), lax.axis_index("y"), lax.axis_index("c")


def _peer(pos, k):
    x, y, c = pos
    px = 1 - x if (k >> 2) & 1 else x
    py = 1 - y if (k >> 1) & 1 else y
    pc = 1 - c if k & 1 else c
    return (px, py, pc), 4 * px + 2 * py + pc


def _exchange(arrs, gather, name):
    n = len(arrs)

    def body(*refs):
        ins, outs = refs[:n], refs[n:2 * n]
        send_sems, recv_sems, loc_sems = refs[2 * n:]
        pos = _mesh_pos()
        me = 4 * pos[0] + 2 * pos[1] + pos[2]
        local = []
        for a in range(n):
            src = ins[a] if gather else ins[a].at[me]
            cp = pltpu.make_async_copy(src, outs[a].at[me], loc_sems.at[a])
            cp.start()
            local.append(cp)
        sends, recvs = [], []
        for k in range(1, N_DEV):
            peer, pid = _peer(pos, k)
            for a in range(n):
                s = a * (N_DEV - 1) + k - 1
                src = ins[a] if gather else ins[a].at[pid]
                cp = pltpu.make_async_remote_copy(
                    src_ref=src, dst_ref=outs[a].at[me], send_sem=send_sems.at[s], recv_sem=recv_sems.at[s],
                    device_id=peer, device_id_type=pl.DeviceIdType.MESH)
                cp.start()
                sends.append(cp)
                recvs.append(pltpu.make_async_remote_copy(
                    src_ref=src, dst_ref=outs[a].at[pid], send_sem=send_sems.at[s], recv_sem=recv_sems.at[s],
                    device_id=peer, device_id_type=pl.DeviceIdType.MESH))
        for cp in recvs:
            cp.wait_recv()
        for cp in sends:
            cp.wait_send()
        for cp in local:
            cp.wait()

    any_spec = pl.BlockSpec(memory_space=pl.ANY)
    out_shape = [jax.ShapeDtypeStruct((N_DEV,) + tuple(a.shape) if gather else tuple(a.shape), a.dtype) for a in arrs]
    return pl.pallas_call(
        body, name=name, in_specs=[any_spec] * n, out_specs=[any_spec] * n, out_shape=out_shape,
        scratch_shapes=[pltpu.SemaphoreType.DMA((n * (N_DEV - 1),)), pltpu.SemaphoreType.DMA((n * (N_DEV - 1),)),
                        pltpu.SemaphoreType.DMA((n,))],
        compiler_params=pltpu.CompilerParams(has_side_effects=True))(*arrs)


def _device_index():
    x, y, c = _mesh_pos()
    return 4 * x + 2 * y + c


def _landing(srcs, gather):
    me = _device_index()
    lands = []
    for a in srcs:
        own = a[None] if gather else lax.dynamic_slice_in_dim(a, me, 1, axis=0)
        shape = (N_DEV,) + tuple(a.shape) if gather else tuple(a.shape)
        lands.append(lax.dynamic_update_slice_in_dim(lax.empty(shape, a.dtype), own, me, axis=0))
    return lands


def _exchange_copies(ins, lands, send_sems, recv_sems, gather):
    pos = _mesh_pos()
    me = 4 * pos[0] + 2 * pos[1] + pos[2]
    sends, recvs = [], []
    for k in range(1, N_DEV):
        peer, pid = _peer(pos, k)
        for a in range(len(ins)):
            s = a * (N_DEV - 1) + k - 1
            src = ins[a] if gather else ins[a].at[pid]
            sends.append(pltpu.make_async_remote_copy(
                src_ref=src, dst_ref=lands[a].at[me], send_sem=send_sems.at[s], recv_sem=recv_sems.at[s],
                device_id=peer, device_id_type=pl.DeviceIdType.MESH))
            recvs.append(pltpu.make_async_remote_copy(
                src_ref=src, dst_ref=lands[a].at[pid], send_sem=send_sems.at[s], recv_sem=recv_sems.at[s],
                device_id=peer, device_id_type=pl.DeviceIdType.MESH))
    return sends, recvs


def _exchange_start(srcs, gather, name, after=None):
    n = len(srcs)
    lands = _landing(srcs, gather)
    nsem = n * (N_DEV - 1)
    extra = [] if after is None else [after]

    def body(*refs):
        ins, lnd = refs[:n], refs[n:2 * n]
        send_sems, recv_sems = refs[2 * n + len(extra)], refs[2 * n + len(extra) + 1]
        token = refs[-1]
        sends, _ = _exchange_copies(ins, lnd, send_sems, recv_sems, gather)
        for cp in sends:
            cp.start()
        token[...] = jnp.zeros_like(token)

    hbm = pl.BlockSpec(memory_space=pltpu.HBM)
    sem = pl.BlockSpec(memory_space=pltpu.SEMAPHORE)
    bufs = list(srcs) + lands
    out_shape = ([pltpu.SemaphoreType.DMA((nsem,)), pltpu.SemaphoreType.DMA((nsem,))]
                 + [pltpu.HBM(b.shape, b.dtype) for b in bufs] + [jax.ShapeDtypeStruct((8, LANES), F32)])
    outs = pl.pallas_call(
        body, name=name, in_specs=[hbm] * (2 * n) + [pl.BlockSpec(memory_space=pl.ANY)] * len(extra),
        out_specs=[sem, sem] + [hbm] * (2 * n) + [pl.BlockSpec(memory_space=pltpu.VMEM)], out_shape=out_shape,
        input_output_aliases={i: 2 + i for i in range(2 * n)},
        compiler_params=pltpu.CompilerParams(has_side_effects=pltpu.SideEffectType.DATAFLOW_SIDE_EFFECTING),
    )(*[pltpu.with_memory_space_constraint(b, pltpu.HBM) for b in bufs], *extra)
    return (outs[0], outs[1], outs[2:2 + n], outs[2 + n:2 + 2 * n]), outs[-1]


def _exchange_wait(state, after, gather, name):
    send_sems, recv_sems, srcs, lands = state
    n = len(srcs)
    after = list(after) if isinstance(after, (list, tuple)) else [after]

    def body(*refs):
        ins, lnd = refs[:n], refs[n:2 * n]
        s_sems, r_sems = refs[2 * n], refs[2 * n + 1]
        sends, recvs = _exchange_copies(ins, lnd, s_sems, r_sems, gather)
        for cp in sends:
            cp.wait_send()
        for cp in recvs:
            cp.wait_recv()

    hbm = pl.BlockSpec(memory_space=pltpu.HBM)
    sem = pl.BlockSpec(memory_space=pltpu.SEMAPHORE)
    bufs = list(srcs) + list(lands)
    outs = pl.pallas_call(
        body, name=name, in_specs=[hbm] * (2 * n) + [sem, sem] + [pl.BlockSpec(memory_space=pl.ANY)] * len(after),
        out_specs=[hbm] * (2 * n), out_shape=[pltpu.HBM(b.shape, b.dtype) for b in bufs],
        input_output_aliases={i: i for i in range(2 * n)},
        compiler_params=pltpu.CompilerParams(has_side_effects=pltpu.SideEffectType.DATAFLOW_SIDE_EFFECTING),
    )(*bufs, send_sems, recv_sems, *after)
    return outs[n:]


def _adam_vals(w, g, m, v):
    m = ADAM_B1 * m + (1.0 - ADAM_B1) * g
    v = ADAM_B2 * v + (1.0 - ADAM_B2) * (g * g)
    m_hat = m / ADAM_C1
    v_hat = v / ADAM_C2
    delta = -ADAM_LR * (m_hat / (jnp.sqrt(v_hat) + ADAM_EPS) + ADAM_WD * w)
    return delta, m, v


def _small_allreduce_adam(part, w, m, v, name):
    rows = part.shape[0]

    def body(p_ref, w_ref, m_ref, v_ref, g_ref, d_ref, nm_ref, nv_ref, gath, send_sems, recv_sems):
        pos = _mesh_pos()
        me = 4 * pos[0] + 2 * pos[1] + pos[2]
        gath[me] = p_ref[...]
        sends, recvs = [], []
        for k in range(1, N_DEV):
            peer, pid = _peer(pos, k)
            cp = pltpu.make_async_remote_copy(
                src_ref=p_ref, dst_ref=gath.at[me], send_sem=send_sems.at[k - 1], recv_sem=recv_sems.at[k - 1],
                device_id=peer, device_id_type=pl.DeviceIdType.MESH)
            cp.start()
            sends.append(cp)
            recvs.append(pltpu.make_async_remote_copy(
                src_ref=p_ref, dst_ref=gath.at[pid], send_sem=send_sems.at[k - 1], recv_sem=recv_sems.at[k - 1],
                device_id=peer, device_id_type=pl.DeviceIdType.MESH))
        for cp in recvs:
            cp.wait_recv()
        for cp in sends:
            cp.wait_send()
        g = gath[0]
        for p in range(1, N_DEV):
            g = g + gath[p]
        g_ref[...] = g
        d, nm, nv = _adam_vals(w_ref[...], g, m_ref[...], v_ref[...])
        d_ref[...] = d
        nm_ref[...] = nm
        nv_ref[...] = nv

    vm = pl.BlockSpec(memory_space=pltpu.VMEM)
    sds = jax.ShapeDtypeStruct((rows, LANES), F32)
    return pl.pallas_call(
        body, name=name, in_specs=[vm] * 4, out_specs=[vm] * 4, out_shape=[sds] * 4,
        scratch_shapes=[pltpu.VMEM((N_DEV, rows, LANES), F32), pltpu.SemaphoreType.DMA((N_DEV - 1,)),
                        pltpu.SemaphoreType.DMA((N_DEV - 1,))],
        compiler_params=pltpu.CompilerParams(has_side_effects=True))(part, w, m, v)


def _reduce_adam(recv, w, m, v, name):
    shape = w.shape
    c = shape[-1]
    r = int(np.prod(shape[:-1]))
    recv2, w2, m2, v2 = recv.reshape(N_DEV, r, c), w.reshape(r, c), m.reshape(r, c), v.reshape(r, c)
    tr = r
    while tr * c * 4 > (1 << 20) and tr % 16 == 0:
        tr //= 2

    def body(r_ref, w_ref, m_ref, v_ref, g_ref, d_ref, nm_ref, nv_ref):
        g = r_ref[0].astype(F32)
        for p in range(1, N_DEV):
            g = g + r_ref[p].astype(F32)
        g_ref[...] = g
        d, nm, nv = _adam_vals(w_ref[...], g, m_ref[...], v_ref[...])
        d_ref[...] = d
        nm_ref[...] = nm
        nv_ref[...] = nv

    spec = pl.BlockSpec((tr, c), lambda i: (i, 0))
    sds = jax.ShapeDtypeStruct((r, c), F32)
    outs = pl.pallas_call(
        body, name=name, grid=(r // tr,),
        in_specs=[pl.BlockSpec((N_DEV, tr, c), lambda i: (0, i, 0)), spec, spec, spec],
        out_specs=[spec] * 4, out_shape=[sds] * 4, compiler_params=_cp("parallel"))(recv2, w2, m2, v2)
    return [o.reshape(shape) for o in outs]


def _pad_heads(w, real=HEAD_DIM):
    lead = w.shape[:-1]
    w = w.reshape(lead + (HEADS, real))
    w = jnp.pad(w, [(0, 0)] * len(lead) + [(0, 0), (0, LANES - real)])
    return w.reshape(lead + (HP,))


def _unpad_heads(w, real=HEAD_DIM):
    lead = w.shape[:-1]
    return w.reshape(lead + (HEADS, LANES))[..., :real].reshape(lead + (HEADS * real,))


_IN_SEGS = (("fq", 0, 256), ("fk", 256, 512), ("fv", 512, 768), ("ff", 768, 772), ("cq", 772, 1028),
            ("ckv", 1028, 1156), ("kr", 1156, 1188), ("rq", 1188, 1444), ("rk", 1444, 1700), ("rv", 1700, 1956),
            ("rg", 1956, 2212), ("sq", 2212, 2468), ("sk", 2468, 2724), ("sv", 2724, 2980))


def _pad_w_in(w):
    s = {n: w[:, a:b] for n, a, b in _IN_SEGS}
    rows = w.shape[0]
    z = lambda n: jnp.zeros((rows, n), w.dtype)
    parts = [s[n] for n in ("fq", "fk", "fv", "rq", "rk", "rv", "sq", "sk", "sv", "rg", "cq", "ckv")]
    parts += [z(HEAD_DIM), s["kr"], z(LANES - HEAD_DIM - ROPE_MLA), s["ff"], z(LANES - HEADS),
              z(NP_IN - OFF_FF - LANES)]
    return jnp.concatenate(parts, axis=1)


def _unpad_w_in(wp):
    seg = lambda off, n=GROUP: wp[:, off:off + n]
    parts = [seg(OFF_FOX), seg(OFF_FOX + GROUP), seg(OFF_FOX + 2 * GROUP), seg(OFF_FF, HEADS),
             seg(OFF_CQ, Q_RANK), seg(OFF_CKV, KV_RANK), seg(OFF_KR + HEAD_DIM, ROPE_MLA),
             seg(OFF_RET), seg(OFF_RET + GROUP), seg(OFF_RET + 2 * GROUP), seg(OFF_RG),
             seg(OFF_SB), seg(OFF_SB + GROUP), seg(OFF_SB + 2 * GROUP)]
    return jnp.concatenate(parts, axis=1)


def _pad_w_kv(w):
    w4 = w.reshape(KV_RANK, HEADS, 2 * HEAD_DIM)
    k = w4[:, :, :HEAD_DIM].reshape(KV_RANK, GROUP)
    v = w4[:, :, HEAD_DIM:].reshape(KV_RANK, GROUP)
    return jnp.concatenate([_pad_heads(k), _pad_heads(v)], axis=1)


def _unpad_w_kv(wp):
    k = _unpad_heads(wp[:, :HP]).reshape(KV_RANK, HEADS, HEAD_DIM)
    v = _unpad_heads(wp[:, HP:]).reshape(KV_RANK, HEADS, HEAD_DIM)
    return jnp.concatenate([k, v], axis=-1).reshape(KV_RANK, HEADS * 2 * HEAD_DIM)


def _pad_gain_out(g):
    g = jnp.pad(g.reshape(4 * HEADS, HEAD_DIM), ((0, 0), (0, LANES - HEAD_DIM)))
    return g.reshape(1, 4 * HP)


def _unpad_gain_out(gp):
    return gp.reshape(4 * HEADS, LANES)[:, :HEAD_DIM].reshape(D_MODEL)


_SMALL = (("g_mix_pre", 1024), ("g_mix_post", 1024), ("g_ffn_pre", 1024), ("g_ffn_post", 1024), ("g_mix_out", 1024),
          ("g_q_lora", 256), ("g_kv_lora", 128), ("b_forget", 4))


def _pack_small(vals):
    parts = []
    for name, n in _SMALL:
        a = vals[name].astype(F32)
        if n < LANES:
            a = jnp.pad(a, ((0, 0), (0, LANES - n)))
        parts.append(a)
    return jnp.concatenate(parts, axis=1).reshape(DEPTH * SMALL_ROWS, LANES)


def _unpack_small(packed):
    flat = packed.reshape(DEPTH, SMALL_ROWS * LANES)
    out, off = {}, 0
    for name, n in _SMALL:
        out[name] = flat[:, off:off + n]
        off += max(n, LANES)
    return out


def kernel(x, positions, g_mix_pre, w_in, b_forget, g_q_lora, w_q_up, g_kv_lora, w_kv_up, g_mix_out, w_out, g_mix_post, g_ffn_pre, w_ffn_up, w_ffn_down, g_ffn_post, loss_target, m_g_mix_pre, m_w_in, m_b_forget, m_g_q_lora, m_w_q_up, m_g_kv_lora, m_w_kv_up, m_g_mix_out, m_w_out, m_g_mix_post, m_g_ffn_pre, m_w_ffn_up, m_w_ffn_down, m_g_ffn_post, v_g_mix_pre, v_w_in, v_b_forget, v_g_q_lora, v_w_q_up, v_g_kv_lora, v_w_kv_up, v_g_mix_out, v_w_out, v_g_mix_post, v_g_ffn_pre, v_w_ffn_up, v_w_ffn_down, v_g_ffn_post):
    t = x.shape[1]
    nq = t // BQ
    x0 = x[0]
    tgt = loss_target[0]
    pos = positions[0].astype(F32).reshape(t, 1)

    half_r, half_m = HEAD_DIM // 2, ROPE_MLA // 2
    invf_r = ROPE_BASE ** (-jnp.arange(half_r, dtype=F32) / half_r)
    invf_m = ROPE_BASE ** (-jnp.arange(half_m, dtype=F32) / half_m)
    invf = jnp.concatenate([invf_r, invf_r, invf_m, invf_m,
                            jnp.zeros((LANES - HEAD_DIM - ROPE_MLA,), F32)]).reshape(1, LANES)
    log_gamma = jnp.log1p(-jnp.power(2.0, -5.0 - jnp.arange(HEADS, dtype=F32)))
    lg_lanes = jnp.repeat(log_gamma, LANES).reshape(1, HP)
    lg_heads = jnp.broadcast_to(log_gamma[:, None, None], (HEADS, 1, LANES))

    big = [w_in, w_q_up, w_kv_up, w_out, w_ffn_up, w_ffn_down]
    bf = lambda w: w.astype(BF16)
    first = _exchange([bf(w_in[0]), bf(w_q_up), bf(w_kv_up)], True, "weights_gather_first")
    l0_state, l0_token = _exchange_start([bf(w[0]) for w in (w_out, w_ffn_up, w_ffn_down)], True,
                                         "weights_gather_layer0_start")
    l1_state, rest_token = _exchange_start([bf(w[1]) for w in (w_in, w_out, w_ffn_up, w_ffn_down)], True,
                                           "weights_gather_layer1_start", after=l0_token)
    wq_g = first[1].transpose(1, 2, 0, 3).reshape(DEPTH, Q_RANK, 384)
    wkv_g = first[2].transpose(1, 2, 0, 3).reshape(DEPTH, KV_RANK, 512)

    row = lambda g: g.reshape(1, -1)
    layers = []
    for l in range(DEPTH):
        layers.append(dict(
            wq=_pad_heads(wq_g[l], 96), wkv=_pad_w_kv(wkv_g[l]),
            g_pre=row(g_mix_pre[l]), g_post=row(g_mix_post[l]), g_fpre=row(g_ffn_pre[l]), g_fpost=row(g_ffn_post[l]),
            g_out=_pad_gain_out(g_mix_out[l]), g_q=row(g_q_lora[l]), g_kv=row(g_kv_lora[l]),
            b_pad=jnp.pad(b_forget[l], (0, LANES - HEADS)).reshape(1, LANES)))
    layers[0]["win"] = _pad_w_in(first[0].reshape(D_MODEL, D_IN))

    saved = []
    xin = x0
    h = _rms_fwd(xin, layers[0]["g_pre"] + rest_token[0:1, 0:1], "rms_pre_0")
    loss_row = dx = None
    for l, p in enumerate(layers):
        s = dict(x=xin, h=h)
        proj = _mm(h, p["win"], name=f"in_proj_{l}", tm=512, tn=NP_IN)
        cum = _cumsum(_forget_lsf(proj, p["b_pad"], f"forget_lsf_{l}"), False, f"forget_cumsum_{l}")
        fox, ret, sb, mla, fox_t, sb_t, mla_t, ret_t = _prep_fwd(proj, cum, pos, invf, lg_lanes, p["g_q"],
                                                          p["g_kv"], p["wq"], p["wkv"], f"prep_fwd_{l}")
        oa, lse_a = _softmax_fwd(fox, fox_t, chunk_mask=False, scale=1.0, name=f"fox_fwd_{l}")
        ob, lse_b = _softmax_fwd(mla, mla_t, chunk_mask=True, scale=96.0 ** -0.5, name=f"mla_fwd_{l}")
        oc, ret_st = _ret_fwd(ret, ret_t, lg_heads, f"ret_fwd_{l}")
        od, sb_tot = _sb_fwd(sb, sb_t, f"sb_fwd_{l}")
        if l == 0:
            got = _exchange_wait(l0_state, od, True, "weights_gather_layer0_wait")
            p.update(wout=got[0].reshape(D_MODEL, D_MODEL), wup=got[1].transpose(1, 0, 2).reshape(D_MODEL, D_FF),
                     wdn=got[2].reshape(D_FF, D_MODEL))
        mixed = _post_fwd(oa, ob, oc, od, proj, p["g_out"], f"post_fwd_{l}")
        mix = _mm(mixed, p["wout"], name=f"out_proj_{l}")
        x1, h2 = _add_rms_fwd(xin, mix, p["g_post"], p["g_fpre"], f"mix_residual_{l}")
        a = _mm(h2, p["wup"], name=f"ffn_up_{l}")
        y = _mm(a, p["wdn"], name=f"ffn_down_{l}", a_fn=_relu2)
        s.update(proj=proj, fox=fox, ret=ret, sb=sb, mla=mla, fox_t=fox_t, sb_t=sb_t, mla_t=mla_t, ret_st=ret_st, oa=oa, ob=ob, oc=oc,
                 od=od, sb_tot=sb_tot, lse_a=lse_a, lse_b=lse_b, mixed=mixed, mix=mix, x1=x1, h2=h2, a=a, y=y)
        saved.append(s)
        if l == 0:
            got = _exchange_wait(l1_state, y, True, "weights_gather_layer1_wait")
            layers[1].update(win=_pad_w_in(got[0].reshape(D_MODEL, D_IN)), wout=got[1].reshape(D_MODEL, D_MODEL),
                             wup=got[2].transpose(1, 0, 2).reshape(D_MODEL, D_FF), wdn=got[3].reshape(D_FF, D_MODEL))
        if l + 1 < DEPTH:
            xin, h = _add_rms_fwd(x1, y, p["g_fpost"], layers[l + 1]["g_pre"], f"ffn_residual_{l}")
        else:
            loss_row, dx = _final_loss(x1, y, p["g_fpost"], tgt, "loss")

    small_g = {n: [None] * DEPTH for n, _ in _SMALL}
    big_g = [[None] * DEPTH for _ in range(6)]
    to_send = [
        lambda g: g.reshape(N_DEV, 1, D_MODEL // N_DEV, D_IN),
        lambda g: g.reshape(Q_RANK, N_DEV, 384 // N_DEV).transpose(1, 0, 2)[:, None],
        lambda g: g.reshape(KV_RANK, N_DEV, 512 // N_DEV).transpose(1, 0, 2)[:, None],
        lambda g: g.reshape(N_DEV, 1, D_MODEL // N_DEV, D_MODEL),
        lambda g: g.reshape(D_MODEL, N_DEV, D_FF // N_DEV).transpose(1, 0, 2)[:, None],
        lambda g: g.reshape(N_DEV, 1, D_FF // N_DEV, D_MODEL),
    ]
    send_of = lambda ks, l: [to_send[k](big_g[k][l]).astype(BF16) for k in ks]
    late_state = early_state = None
    order_token = jnp.zeros((1, 1), F32)
    for l in reversed(range(DEPTH)):
        p, s = layers[l], saved[l]
        dy, dg = _norm_bwd(dx, s["y"], p["g_fpost"] + order_token, None, BF16, f"ffn_post_bwd_{l}")
        small_g["g_ffn_post"][l] = dg
        da = _mm(dy, p["wdn"], name=f"ffn_down_dx_{l}", tb=True, out_dtype=BF16, epi=_drelu2, epi_in=s["a"])
        big_g[5][l] = _mm(s["a"], dy, name=f"ffn_down_dw_{l}", ta=True, a_fn=_relu2, tk=2048)
        big_g[4][l] = _mm(s["h2"], da, name=f"ffn_up_dw_{l}", ta=True, tk=2048)
        dh2 = _mm(da, p["wup"], name=f"ffn_up_dx_{l}", tb=True)
        dx1, dg = _norm_bwd(dh2, s["x1"], p["g_fpre"], dx, F32, f"ffn_pre_bwd_{l}")
        small_g["g_ffn_pre"][l] = dg
        dmix, dg = _norm_bwd(dx1, s["mix"], p["g_post"], None, BF16, f"mix_post_bwd_{l}")
        small_g["g_mix_post"][l] = dg
        dmixed = _mm(dmix, p["wout"], name=f"out_proj_dx_{l}", tb=True)
        big_g[3][l] = _mm(s["mixed"], dmix, name=f"out_proj_dw_{l}", ta=True, tk=2048)
        g_out = p["g_out"]
        if l == 0:
            early_state, early_token = _exchange_start(send_of((3, 4, 5), 0), False, "grads_layer0_early_start")
            g_out = g_out + early_token[0:1, 0:1]
        doa, dob, doc, dod, dla, dlb, drg, dgo = _post_bwd(dmixed, s["oa"], s["ob"], s["oc"], s["od"], s["proj"],
                                                           g_out, f"post_bwd_{l}")
        small_g["g_mix_out"][l] = _unpad_gain_out(dgo).reshape(1, D_MODEL)
        as_rows = lambda a: a[:, :HEADS].T.reshape(HEADS, t // BQA, 1, BQA)
        dfq, dfk, dfv, dcum_k, dcum_q = _softmax_bwd(s["fox"], s["fox_t"], doa, s["lse_a"], as_rows(dla), bias=True,
                                                     chunk_mask=False, scale=1.0, name=f"fox_bwd_{l}")
        dmq, dmk, dmv = _softmax_bwd(s["mla"], s["mla_t"], dob, s["lse_b"], as_rows(dlb), bias=False, chunk_mask=True,
                                     scale=96.0 ** -0.5, name=f"mla_bwd_{l}")
        drq, drk, drv = _ret_bwd(s["ret"], s["ret_st"], lg_heads, doc, f"ret_bwd_{l}")
        dsq, dsk, dsv = _sb_bwd(s["sb"], s["sb_t"], dod, s["sb_tot"], f"sb_bwd_{l}")
        dcum_q = jnp.pad(dcum_q.reshape(HEADS, t).T, ((0, 0), (0, LANES - HEADS)))
        dlsf = _cumsum(dcum_q, True, f"forget_cumsum_bwd_{l}", partials=dcum_k)
        dproj, dwq, dwkv, dgq, dgkv, dbf = _prep_bwd(
            (dfq, dfk, dfv), (drq, drk, drv), (dsq, dsk, dsv), (dmq, dmk, dmv), drg, dlsf, s["proj"], pos, invf,
            lg_lanes, p["b_pad"], p["g_q"], p["g_kv"], p["wq"], p["wkv"], f"prep_bwd_{l}")
        small_g["g_q_lora"][l] = dgq
        small_g["g_kv_lora"][l] = dgkv
        small_g["b_forget"][l] = dbf[:, :HEADS]
        big_g[1][l] = _unpad_heads(dwq, 96)
        big_g[2][l] = _unpad_w_kv(dwkv)
        dh = _mm(dproj, p["win"], name=f"in_proj_dx_{l}", tb=True, tk=NP_IN // 2)
        big_g[0][l] = _unpad_w_in(_mm(s["h"], dproj, name=f"in_proj_dw_{l}", ta=True, tn=NP_IN // 2))
        g_pre = p["g_pre"]
        if l == 0:
            last_state, last_token = _exchange_start(send_of((0, 1, 2), 0), False, "grads_layer0_rest_start")
            g_pre = g_pre + last_token[0:1, 0:1]
        dx, dg = _norm_bwd(dh, s["x"], g_pre, dx1, F32, f"mix_pre_bwd_{l}")
        small_g["g_mix_pre"][l] = dg
        if l == DEPTH - 1:
            late_state, late_token = _exchange_start(send_of(range(6), l), False, "grads_layer1_start")
            order_token = late_token[0:1, 0:1]
    grad_x = dx.reshape(1, t, D_MODEL)

    res = {}
    small_w = dict(g_mix_pre=g_mix_pre, g_mix_post=g_mix_post, g_ffn_pre=g_ffn_pre, g_ffn_post=g_ffn_post,
                   g_mix_out=g_mix_out, g_q_lora=g_q_lora, g_kv_lora=g_kv_lora, b_forget=b_forget)
    small_m = dict(g_mix_pre=m_g_mix_pre, g_mix_post=m_g_mix_post, g_ffn_pre=m_g_ffn_pre, g_ffn_post=m_g_ffn_post,
                   g_mix_out=m_g_mix_out, g_q_lora=m_g_q_lora, g_kv_lora=m_g_kv_lora, b_forget=m_b_forget)
    small_v = dict(g_mix_pre=v_g_mix_pre, g_mix_post=v_g_mix_post, g_ffn_pre=v_g_ffn_pre, g_ffn_post=v_g_ffn_post,
                   g_mix_out=v_g_mix_out, g_q_lora=v_g_q_lora, g_kv_lora=v_g_kv_lora, b_forget=v_b_forget)
    n_small = DEPTH * SMALL_ROWS
    extra = lambda a: jnp.concatenate([a, jnp.zeros((8, LANES), F32)], axis=0)
    part = jnp.concatenate([_pack_small({n: jnp.concatenate(small_g[n], axis=0) for n, _ in _SMALL}),
                            jnp.broadcast_to(loss_row, (8, LANES))], axis=0)
    sres = _small_allreduce_adam(part, extra(_pack_small(small_w)), extra(_pack_small(small_m)),
                                 extra(_pack_small(small_v)), "small_allreduce_adamw")
    loss = sres[0][n_small, 0]
    sg, sd, sm, sv = [_unpack_small(a[:n_small]) for a in sres]
    for n, _ in _SMALL:
        res[n] = [sg[n], sd[n], sm[n], sv[n]]

    late = _exchange_wait(late_state, dx, False, "grads_layer1_wait")
    early = _exchange_wait(early_state, dx, False, "grads_layer0_early_wait")
    ms = [m_w_in, m_w_q_up, m_w_kv_up, m_w_out, m_w_ffn_up, m_w_ffn_down]
    vs = [v_w_in, v_w_q_up, v_w_kv_up, v_w_out, v_w_ffn_up, v_w_ffn_down]
    names = ["w_in", "w_q_up", "w_kv_up", "w_out", "w_ffn_up", "w_ffn_down"]
    for k in (3, 4, 5):
        recv = jnp.concatenate([early[k - 3], late[k]], axis=1)
        res[names[k]] = _reduce_adam(recv, big[k], ms[k], vs[k], f"adamw_{names[k]}")
    done = [sres[0]] + [res[names[k]][0] for k in (3, 4, 5)]
    last = _exchange_wait(last_state, done, False, "grads_layer0_rest_wait")
    for k in (0, 1, 2):
        recv = jnp.concatenate([last[k], late[k]], axis=1)
        res[names[k]] = _reduce_adam(recv, big[k], ms[k], vs[k], f"adamw_{names[k]}")

    order = ["g_mix_pre", "w_in", "b_forget", "g_q_lora", "w_q_up", "g_kv_lora", "w_kv_up", "g_mix_out", "w_out",
             "g_mix_post", "g_ffn_pre", "w_ffn_up", "w_ffn_down", "g_ffn_post"]
    outs = [loss, grad_x]
    for idx in range(4):
        outs += [res[n][idx] for n in order]
    return tuple(outs)
```

```python
import functools
import math

import numpy as np
import jax
import jax.numpy as jnp
from jax import lax
from jax.experimental import pallas as pl
from jax.experimental.pallas import tpu as pltpu

F32 = jnp.float32
BF16 = jnp.bfloat16

D_MODEL = 1024
DEPTH = 2
N_DEV = 8
GROUP = 256
HEADS = 4
HEAD_DIM = 64
LANES = 128
HP = HEADS * LANES
QKV = 3 * HP
Q_RANK = 256
KV_RANK = 128
ROPE_MLA = 32
D_FF = 4096
D_IN = 2980
CHUNK_SHIFT = 6
EPS = 1e-6
ROPE_BASE = 10000.0
NEG = -1e30

QKV_IN = 3 * GROUP
OFF_FOX, OFF_RET, OFF_SB = 0, QKV_IN, 2 * QKV_IN
OFF_RG = 3 * QKV_IN
OFF_CQ = OFF_RG + GROUP
OFF_CKV = OFF_CQ + Q_RANK
OFF_KR = OFF_CKV + LANES
OFF_FF = OFF_KR + LANES
NP_IN = 3328

BQ = 256
TKS = 128
TM = 256
VMEM_LIMIT = 58 * 1024 * 1024

ADAM_LR, ADAM_B1, ADAM_B2, ADAM_EPS, ADAM_WD, ADAM_STEP = 0.001, 0.9, 0.999, 1e-08, 0.01, 10
ADAM_C1 = 1.0 - ADAM_B1 ** ADAM_STEP
ADAM_C2 = 1.0 - ADAM_B2 ** ADAM_STEP

SMALL_ROWS = 44

NT = (((1,), (1,)), ((), ()))
TN = (((0,), (0,)), ((), ()))


def _cp(*sem):
    return pltpu.CompilerParams(dimension_semantics=sem if sem else None, vmem_limit_bytes=VMEM_LIMIT)


def _bdot(a, b, dn=None):
    if dn is None:
        return jnp.dot(a, b, preferred_element_type=F32)
    return lax.dot_general(a, b, dn, preferred_element_type=F32)


def _split2(x):
    hi = x.astype(BF16)
    lo = (x - hi.astype(F32)).astype(BF16)
    return hi, lo


def _mm(a, b, *, name, ta=False, tb=False, out_dtype=F32, a_fn=None, epi=None, epi_in=None,
        tm=1024, tn=1024, tk=1024):
    m, k = (a.shape[1], a.shape[0]) if ta else a.shape
    n = b.shape[0] if tb else b.shape[1]
    tm, tn, tk = min(tm, m), min(tn, n), min(tk, k)
    assert m % tm == 0 and n % tn == 0 and k % tk == 0, (name, m, n, k)
    nk = k // tk
    dn = (((0 if ta else 1,), (1 if tb else 0,)), ((), ()))

    def body(*refs):
        if epi is None:
            a_ref, b_ref, o_ref = refs[:3]
            e_ref = None
            rest = refs[3:]
        else:
            a_ref, b_ref, e_ref, o_ref = refs[:4]
            rest = refs[4:]
        av = a_ref[...]
        if a_fn is not None:
            av = a_fn(av)
        part = lax.dot_general(av.astype(BF16), b_ref[...].astype(BF16), dn, preferred_element_type=F32)

        def finish(r):
            if epi is not None:
                r = epi(r, e_ref[...])
            o_ref[...] = r.astype(out_dtype)

        if nk == 1:
            finish(part)
        else:
            acc_ref = rest[0]
            kk = pl.program_id(2)

            @pl.when(kk == 0)
            def _():
                acc_ref[...] = part

            @pl.when(kk > 0)
            def _():
                acc_ref[...] += part

            @pl.when(kk == nk - 1)
            def _():
                finish(acc_ref[...])

    a_spec = pl.BlockSpec((tk, tm), lambda i, j, kk: (kk, i)) if ta else pl.BlockSpec((tm, tk), lambda i, j, kk: (i, kk))
    b_spec = pl.BlockSpec((tn, tk), lambda i, j, kk: (j, kk)) if tb else pl.BlockSpec((tk, tn), lambda i, j, kk: (kk, j))
    o_spec = pl.BlockSpec((tm, tn), lambda i, j, kk: (i, j))
    in_specs = [a_spec, b_spec]
    args = [a, b]
    if epi is not None:
        in_specs.append(o_spec)
        args.append(epi_in)
    return pl.pallas_call(
        body, name=name, grid=(m // tm, n // tn, nk),
        in_specs=in_specs, out_specs=o_spec,
        out_shape=jax.ShapeDtypeStruct((m, n), out_dtype),
        scratch_shapes=[pltpu.VMEM((tm, tn), F32)] if nk > 1 else [],
        compiler_params=_cp("parallel", "parallel", "arbitrary"),
    )(*args)


def _relu2(v):
    r = jnp.maximum(v, 0.0)
    return r * r


def _drelu2(du, av):
    return du * (2.0 * jnp.maximum(av, 0.0))


def _rms(v, g):
    r = lax.rsqrt(jnp.mean(v * v, axis=-1, keepdims=True) + EPS)
    return v * r * g


def _row_spec(w):
    return pl.BlockSpec((TM, w), lambda i: (i, 0))


def _vec_spec(w):
    return pl.BlockSpec((1, w), lambda i: (0, 0))


def _rms_fwd(x, g, name):
    t, d = x.shape

    def body(x_ref, g_ref, h_ref):
        h_ref[...] = _rms(x_ref[...], g_ref[...]).astype(BF16)

    return pl.pallas_call(
        body, name=name, grid=(t // TM,), in_specs=[_row_spec(d), _vec_spec(d)], out_specs=_row_spec(d),
        out_shape=jax.ShapeDtypeStruct((t, d), BF16), compiler_params=_cp("parallel"))(x, g)


def _add_rms_fwd(x, y, g1, g2, name):
    t, d = x.shape

    def body(x_ref, y_ref, g1_ref, g2_ref, xn_ref, h_ref):
        xn = x_ref[...] + _rms(y_ref[...], g1_ref[...])
        xn_ref[...] = xn
        h_ref[...] = _rms(xn, g2_ref[...]).astype(BF16)

    return pl.pallas_call(
        body, name=name, grid=(t // TM,),
        in_specs=[_row_spec(d), _row_spec(d), _vec_spec(d), _vec_spec(d)],
        out_specs=[_row_spec(d), _row_spec(d)],
        out_shape=[jax.ShapeDtypeStruct((t, d), F32), jax.ShapeDtypeStruct((t, d), BF16)],
        compiler_params=_cp("parallel"))(x, y, g1, g2)


def _final_loss(x, y, g, tgt, name):
    t, d = x.shape

    def body(x_ref, y_ref, g_ref, t_ref, l_ref, dx_ref):
        @pl.when(pl.program_id(0) == 0)
        def _():
            l_ref[...] = jnp.zeros_like(l_ref)

        err = x_ref[...] + _rms(y_ref[...], g_ref[...]) - t_ref[...]
        dx_ref[...] = err * (1.0 / d)
        l_ref[...] += jnp.sum(jnp.sum(err * err, axis=1, keepdims=True), axis=0, keepdims=True) * (0.5 / d)

    return pl.pallas_call(
        body, name=name, grid=(t // TM,),
        in_specs=[_row_spec(d), _row_spec(d), _vec_spec(d), _row_spec(d)],
        out_specs=[pl.BlockSpec((1, LANES), lambda i: (0, 0)), _row_spec(d)],
        out_shape=[jax.ShapeDtypeStruct((1, LANES), F32), jax.ShapeDtypeStruct((t, d), F32)],
        compiler_params=_cp("arbitrary"))(x, y, g, tgt)


def _rms_bwd_vals(dn, v, g):
    w = v.shape[-1]
    r = lax.rsqrt(jnp.mean(v * v, axis=-1, keepdims=True) + EPS)
    vh = v * r
    dgp = jnp.sum(dn * vh, axis=0, keepdims=True)
    dvh = dn * g
    dv = r * (dvh - vh * (jnp.sum(dvh * vh, axis=-1, keepdims=True) * (1.0 / w)))
    return dv, dgp


def _norm_bwd(dn, v, g, resid, out_dtype, name):
    t, d = v.shape
    has_res = resid is not None

    def body(*refs):
        if has_res:
            dn_ref, v_ref, g_ref, r_ref, dv_ref, dg_ref = refs
        else:
            dn_ref, v_ref, g_ref, dv_ref, dg_ref = refs

        @pl.when(pl.program_id(0) == 0)
        def _():
            dg_ref[...] = jnp.zeros_like(dg_ref)

        dv, dgp = _rms_bwd_vals(dn_ref[...].astype(F32), v_ref[...], g_ref[...])
        if has_res:
            dv = dv + r_ref[...]
        dv_ref[...] = dv.astype(out_dtype)
        dg_ref[...] += dgp

    in_specs = [_row_spec(d), _row_spec(d), _vec_spec(d)] + ([_row_spec(d)] if has_res else [])
    args = [dn, v, g] + ([resid] if has_res else [])
    return pl.pallas_call(
        body, name=name, grid=(t // TM,), in_specs=in_specs,
        out_specs=[_row_spec(d), _vec_spec(d)],
        out_shape=[jax.ShapeDtypeStruct((t, d), out_dtype), jax.ShapeDtypeStruct((1, d), F32)],
        compiler_params=_cp("arbitrary"))(*args)


def _rope_trig(pos, invf):
    ang = pos * invf
    return jnp.cos(ang), jnp.sin(ang)


def _rope_tables(trig, w, lo, half):
    c, s = trig
    lane = lax.broadcasted_iota(jnp.int32, c.shape, 1)
    active = (lane >= lo) & (lane < lo + 2 * half)
    cos = jnp.concatenate([jnp.where(active, c, 1.0)] * (w // LANES), axis=1)
    sin = jnp.concatenate([jnp.where(active, s, 0.0)] * (w // LANES), axis=1)
    lanes = lax.broadcasted_iota(jnp.int32, (c.shape[0], w), 1) & (LANES - 1)
    first = (lanes >= lo) & (lanes < lo + half)
    second = (lanes >= lo + half) & (lanes < lo + 2 * half)
    return cos, sin, first, second


def _rope_apply(v, cos, sin, first, second, half, sign):
    w = v.shape[-1]
    up = pltpu.roll(v, w - half, 1)
    dn = pltpu.roll(v, half, 1)
    rot = jnp.where(first, -up, jnp.where(second, dn, 0.0))
    return v * cos + rot * (sin * sign)


def _forget_lsf(proj, b_pad, name):
    t = proj.shape[0]

    def body(ff_ref, b_ref, o_ref):
        f = ff_ref[...] + b_ref[...]
        o_ref[...] = -(jnp.maximum(-f, 0.0) + jnp.log(1.0 + jnp.exp(-jnp.abs(f))))

    return pl.pallas_call(
        body, name=name, grid=(t // TM,),
        in_specs=[pl.BlockSpec((TM, LANES), lambda i: (i, OFF_FF // LANES)), _vec_spec(LANES)],
        out_specs=_row_spec(LANES), out_shape=jax.ShapeDtypeStruct((t, LANES), F32),
        compiler_params=_cp("parallel"))(proj, b_pad)


def _split3(c):
    hi = c.astype(BF16).astype(F32)
    mid = (c - hi).astype(BF16).astype(F32)
    return hi, mid, (c - hi) - mid


def _spread_heads(x):
    low = lax.broadcasted_iota(jnp.int32, (x.shape[0], LANES), 1) < HEAD_DIM
    out = []
    for c in range(x.shape[1] // LANES):
        blk = x[:, c * LANES:(c + 1) * LANES]
        out.append(jnp.where(low, blk, 0.0))
        out.append(jnp.where(low, pltpu.roll(blk, HEAD_DIM, 1), 0.0))
    return jnp.concatenate(out, axis=1)


def _gather_heads(y):
    low = lax.broadcasted_iota(jnp.int32, (y.shape[0], LANES), 1) < HEAD_DIM
    out = []
    for c in range(y.shape[1] // (2 * LANES)):
        a = y[:, 2 * c * LANES:(2 * c + 1) * LANES]
        b = y[:, (2 * c + 1) * LANES:(2 * c + 2) * LANES]
        out.append(jnp.where(low, a, pltpu.roll(b, HEAD_DIM, 1)))
    return jnp.concatenate(out, axis=1)


def _transposed_tiles(dst, row0, blocks, width):
    for b, blk in enumerate(blocks):
        bt = blk.T.astype(BF16)
        rows = bt.shape[0]
        for w in range(TM // width):
            dst[w, row0 + b * rows:row0 + (b + 1) * rows, :] = bt[:, w * width:(w + 1) * width]


def _prep_fwd(proj, cum, pos, invf, lg_lanes, g_q, g_kv, wq_pad, wkv_pad, name):
    t = proj.shape[0]

    def body(fox_ref, ret_ref, sb_ref, cq_ref, ckv_ref, kr_ref, cum_ref, pos_ref, invf_ref, lg_ref,
             gq_ref, gkv_ref, wq_ref, wkv_ref,
             ofox_ref, oret_ref, osb_ref, omla_ref, ofoxt_ref, osbt_ref, omlat_ref, orett_ref):
        pos_v = pos_ref[...]
        foxv, retv, sbv = [[_spread_heads(ref[:, g * GROUP:(g + 1) * GROUP]) for g in range(3)]
                           for ref in (fox_ref, ret_ref, sb_ref)]
        osb_ref[:, 0:HP] = (sbv[0] * 0.125).astype(BF16)
        osb_ref[:, HP:2 * HP] = sbv[1].astype(BF16)
        osb_ref[:, 2 * HP:QKV] = sbv[2].astype(BF16)
        _transposed_tiles(osbt_ref, 0, [sbv[1], sbv[2]], TKS)
        lane = lax.broadcasted_iota(jnp.int32, (TM, LANES), 1)
        cumv = cum_ref[...]
        fq, fk = [], []
        for hb in range(HEADS):
            hi, mid, lo = _split3(cumv[:, hb:hb + 1])
            q = foxv[0][:, hb * LANES:(hb + 1) * LANES] * 0.125
            k = foxv[1][:, hb * LANES:(hb + 1) * LANES]
            ones_q = (lane >= HEAD_DIM) & (lane < HEAD_DIM + 3)
            ones_k = (lane >= HEAD_DIM + 3) & (lane < HEAD_DIM + 6)
            q = jnp.where(ones_q, 1.0, q)
            k = jnp.where(ones_k, 1.0, k)
            for n, part in enumerate((hi, mid, lo)):
                q = jnp.where(lane == HEAD_DIM + 3 + n, part, q)
                k = jnp.where(lane == HEAD_DIM + n, -part, k)
            fq.append(q)
            fk.append(k)
        fk = jnp.concatenate(fk, axis=1)
        ofox_ref[:, 0:HP] = jnp.concatenate(fq, axis=1).astype(BF16)
        ofox_ref[:, HP:2 * HP] = fk.astype(BF16)
        ofox_ref[:, 2 * HP:QKV] = foxv[2].astype(BF16)
        _transposed_tiles(ofoxt_ref, 0, [fk, foxv[2]], BQ)
        trig = _rope_trig(pos_v, invf_ref[...])
        cos, sin, first, second = _rope_tables(trig, HP, 0, HEAD_DIM // 2)
        nloc = lax.broadcasted_iota(jnp.int32, (TM, 1), 0).astype(F32)
        dec = lg_ref[...] * nloc
        rq = _rope_apply(retv[0], cos, sin, first, second, HEAD_DIM // 2, 1.0)
        rk = _rope_apply(retv[1], cos, sin, first, second, HEAD_DIM // 2, 1.0)
        oret_ref[:, 0:HP] = (rq * jnp.exp(dec)).astype(BF16)
        rk = rk * 0.125 * jnp.exp(-dec)
        oret_ref[:, HP:2 * HP] = rk.astype(BF16)
        _transposed_tiles(orett_ref, 0, [rk], BQ)
        oret_ref[:, 2 * HP:QKV] = retv[2].astype(BF16)
        cosm, sinm, firstm, secondm = _rope_tables(trig, HP, HEAD_DIM, ROPE_MLA // 2)
        cqn = _rms(cq_ref[...], gq_ref[...]).astype(BF16)
        qm = _bdot(cqn, wq_ref[...])
        omla_ref[:, 0:HP] = _rope_apply(qm, cosm, sinm, firstm, secondm, ROPE_MLA // 2, 1.0).astype(BF16)
        ckvn = _rms(ckv_ref[...], gkv_ref[...]).astype(BF16)
        kv = _bdot(ckvn, wkv_ref[...])
        krr = _rope_apply(kr_ref[...], cosm[:, 0:LANES], sinm[:, 0:LANES], firstm[:, 0:LANES],
                          secondm[:, 0:LANES], ROPE_MLA // 2, 1.0)
        mk = kv[:, 0:HP] + jnp.concatenate([krr] * HEADS, axis=1)
        omla_ref[:, HP:2 * HP] = mk.astype(BF16)
        omla_ref[:, 2 * HP:QKV] = kv[:, HP:2 * HP].astype(BF16)
        _transposed_tiles(omlat_ref, 0, [mk, kv[:, HP:2 * HP]], BQ)

    def seg(off, w):
        return pl.BlockSpec((TM, w), lambda i, o=off // w: (i, o))

    def full(shape):
        return pl.BlockSpec(shape, lambda i: (0,) * len(shape))

    def tiles(width):
        return pl.BlockSpec((TM // width, 2 * HP, width), lambda i: (i, 0, 0))

    in_specs = [seg(OFF_FOX, QKV_IN), seg(OFF_RET, QKV_IN), seg(OFF_SB, QKV_IN), seg(OFF_CQ, Q_RANK),
                seg(OFF_CKV, LANES), seg(OFF_KR, LANES), _row_spec(LANES), pl.BlockSpec((TM, 1), lambda i: (i, 0)),
                full((1, LANES)), full((1, HP)), full((1, Q_RANK)), full((1, KV_RANK)),
                full((Q_RANK, HP)), full((KV_RANK, 2 * HP))]
    out_specs = [_row_spec(QKV)] * 4 + [tiles(BQ), tiles(TKS), tiles(BQ),
                                        pl.BlockSpec((1, HP, BQ), lambda i: (i, 0, 0))]
    out_shape = [jax.ShapeDtypeStruct((t, QKV), BF16)] * 4 + [
        jax.ShapeDtypeStruct((t // BQ, 2 * HP, BQ), BF16), jax.ShapeDtypeStruct((t // TKS, 2 * HP, TKS), BF16),
        jax.ShapeDtypeStruct((t // BQ, 2 * HP, BQ), BF16), jax.ShapeDtypeStruct((t // BQ, HP, BQ), BF16)]
    return pl.pallas_call(
        body, name=name, grid=(t // TM,), in_specs=in_specs, out_specs=out_specs, out_shape=out_shape,
        compiler_params=_cp("parallel"))(proj, proj, proj, proj, proj, proj, cum, pos, invf, lg_lanes,
                                         g_q, g_kv, wq_pad, wkv_pad)


def _prep_bwd(dfox, dret, dsb, dmla, drg, dlsf, proj, pos, invf, lg_lanes, b_pad, g_q, g_kv,
              wq_pad, wkv_pad, name):
    t = proj.shape[0]

    def body(dfq, dfk, dfv, drq, drk, drv, dsq, dsk, dsv, dmq, dmk, dmv, drg_ref, dlsf_ref,
             cq_ref, ckv_ref, ff_ref, pos_ref, invf_ref, lg_ref, b_ref, gq_ref, gkv_ref, wq_ref, wkv_ref,
             dp_ref, dwq_ref, dwkv_ref, dgq_ref, dgkv_ref, dbf_ref):
        @pl.when(pl.program_id(0) == 0)
        def _():
            dwq_ref[...] = jnp.zeros_like(dwq_ref)
            dwkv_ref[...] = jnp.zeros_like(dwkv_ref)
            dgq_ref[...] = jnp.zeros_like(dgq_ref)
            dgkv_ref[...] = jnp.zeros_like(dgkv_ref)
            dbf_ref[...] = jnp.zeros_like(dbf_ref)

        pos_v = pos_ref[...]
        def put(off, val):
            dp_ref[:, off:off + GROUP] = _gather_heads(val).astype(BF16)

        for off, (dq, dk, dv) in ((OFF_FOX, (dfq, dfk, dfv)), (OFF_SB, (dsq, dsk, dsv))):
            put(off, dq[...] * 0.125)
            put(off + GROUP, dk[...])
            put(off + 2 * GROUP, dv[...])
        trig = _rope_trig(pos_v, invf_ref[...])
        cos, sin, first, second = _rope_tables(trig, HP, 0, HEAD_DIM // 2)
        nloc = lax.broadcasted_iota(jnp.int32, (TM, 1), 0).astype(F32)
        dec = lg_ref[...] * nloc
        dq = _rope_apply(drq[...] * jnp.exp(dec), cos, sin, first, second, HEAD_DIM // 2, -1.0)
        dk = _rope_apply(drk[...] * (0.125 * jnp.exp(-dec)), cos, sin, first, second, HEAD_DIM // 2, -1.0)
        put(OFF_RET, dq)
        put(OFF_RET + GROUP, dk)
        put(OFF_RET + 2 * GROUP, drv[...])
        put(OFF_RG, drg_ref[...])
        cosm, sinm, firstm, secondm = _rope_tables(trig, HP, HEAD_DIM, ROPE_MLA // 2)
        dql = _rope_apply(dmq[...], cosm, sinm, firstm, secondm, ROPE_MLA // 2, -1.0).astype(BF16)
        cq = cq_ref[...]
        cqn = _rms(cq, gq_ref[...]).astype(BF16)
        dwq_ref[...] += _bdot(cqn, dql, TN)
        dcqn = _bdot(dql, wq_ref[...], NT)
        dcq, dgq = _rms_bwd_vals(dcqn, cq, gq_ref[...])
        dgq_ref[...] += dgq
        dp_ref[:, OFF_CQ:OFF_CQ + Q_RANK] = dcq.astype(BF16)
        dkm = dmk[...]
        dkv = jnp.concatenate([dkm, dmv[...]], axis=1).astype(BF16)
        ckv = ckv_ref[...]
        ckvn = _rms(ckv, gkv_ref[...]).astype(BF16)
        dwkv_ref[...] += _bdot(ckvn, dkv, TN)
        dckvn = _bdot(dkv, wkv_ref[...], NT)
        dckv, dgkv = _rms_bwd_vals(dckvn, ckv, gkv_ref[...])
        dgkv_ref[...] += dgkv
        dp_ref[:, OFF_CKV:OFF_CKV + LANES] = dckv.astype(BF16)
        dkr = dkm[:, 0:LANES] + dkm[:, LANES:2 * LANES] + dkm[:, 2 * LANES:3 * LANES] + dkm[:, 3 * LANES:HP]
        act = firstm[:, 0:LANES] | secondm[:, 0:LANES]
        dkr = jnp.where(act, dkr, 0.0)
        dkr = _rope_apply(dkr, cosm[:, 0:LANES], sinm[:, 0:LANES], firstm[:, 0:LANES], secondm[:, 0:LANES],
                          ROPE_MLA // 2, -1.0)
        dp_ref[:, OFF_KR:OFF_KR + LANES] = dkr.astype(BF16)
        f = ff_ref[...] + b_ref[...]
        dff = dlsf_ref[...] / (1.0 + jnp.exp(f))
        dbf_ref[...] += jnp.sum(dff, axis=0, keepdims=True)
        dp_ref[:, OFF_FF:OFF_FF + LANES] = dff.astype(BF16)
        dp_ref[:, OFF_FF + LANES:NP_IN] = jnp.zeros((TM, NP_IN - OFF_FF - LANES), BF16)

    def seg(off, w):
        return pl.BlockSpec((TM, w), lambda i, o=off // w: (i, o))

    def full(shape):
        return pl.BlockSpec(shape, lambda i: (0,) * len(shape))

    hp_spec = _row_spec(HP)
    in_specs = [hp_spec] * 13 + [_row_spec(LANES), seg(OFF_CQ, Q_RANK), seg(OFF_CKV, LANES), seg(OFF_FF, LANES),
                                 pl.BlockSpec((TM, 1), lambda i: (i, 0)),
                                 full((1, LANES)), full((1, HP)), full((1, LANES)), full((1, Q_RANK)),
                                 full((1, KV_RANK)), full((Q_RANK, HP)), full((KV_RANK, 2 * HP))]
    out_specs = [_row_spec(NP_IN), full((Q_RANK, HP)), full((KV_RANK, 2 * HP)), full((1, Q_RANK)),
                 full((1, KV_RANK)), full((1, LANES))]
    out_shape = [jax.ShapeDtypeStruct((t, NP_IN), BF16), jax.ShapeDtypeStruct((Q_RANK, HP), F32),
                 jax.ShapeDtypeStruct((KV_RANK, 2 * HP), F32), jax.ShapeDtypeStruct((1, Q_RANK), F32),
                 jax.ShapeDtypeStruct((1, KV_RANK), F32), jax.ShapeDtypeStruct((1, LANES), F32)]
    return pl.pallas_call(
        body, name=name, grid=(t // TM,), in_specs=in_specs, out_specs=out_specs, out_shape=out_shape,
        compiler_params=_cp("arbitrary"))(*dfox, *dret, *dsb, *dmla, drg, dlsf, proj, proj, proj, pos, invf,
                                          lg_lanes, b_pad, g_q, g_kv, wq_pad, wkv_pad)


def _cumsum(x, reverse, name, partials=None):
    t, w = x.shape
    n = t // TM
    xs = [x] if partials is None else [x, partials]

    def body(*refs):
        x_refs, o_ref, carry = refs[:len(xs)], refs[len(xs)], refs[len(xs) + 1]

        @pl.when(pl.program_id(0) == 0)
        def _():
            carry[...] = jnp.zeros_like(carry)

        r = lax.broadcasted_iota(jnp.int32, (TM, TM), 0)
        c = lax.broadcasted_iota(jnp.int32, (TM, TM), 1)
        tri = jnp.where((r <= c) if reverse else (r >= c), 1.0, 0.0).astype(BF16)
        v = x_refs[0][...]
        if partials is not None:
            lane = lax.broadcasted_iota(jnp.int32, (TM, LANES), 1)
            for hb in range(HEADS):
                v = v + jnp.where(lane == hb, jnp.sum(x_refs[1][:, _hs(hb)], axis=1, keepdims=True), 0.0)
        hi = v.astype(BF16)
        r1 = v - hi.astype(F32)
        mid = r1.astype(BF16)
        lo = (r1 - mid.astype(F32)).astype(BF16)
        cs = _bdot(tri, hi) + _bdot(tri, mid) + _bdot(tri, lo) + carry[...]
        o_ref[...] = cs
        carry[...] = cs[0:1, :] if reverse else cs[TM - 1:TM, :]

    imap = (lambda i: (n - 1 - i, 0)) if reverse else (lambda i: (i, 0))
    return pl.pallas_call(
        body, name=name, grid=(n,), in_specs=[pl.BlockSpec((TM, a.shape[1]), imap) for a in xs],
        out_specs=pl.BlockSpec((TM, w), imap),
        out_shape=jax.ShapeDtypeStruct((t, w), F32), scratch_shapes=[pltpu.VMEM((1, w), F32)],
        compiler_params=_cp("arbitrary"))(*xs)


HB_FWD = 4
HB_BWD = 4
HB_SB_FWD = 4
BQS = 512
BQA = 512


def _q_spec(hb, bq=BQ):
    return pl.BlockSpec((bq, hb * LANES), lambda g, i: (i, g))


ONE_BUFFER = pl.Buffered(1)


def _kv_spec(t, which, hb):
    return pl.BlockSpec((t, hb * LANES), lambda g, i, w=which: (0, w * (HEADS // hb) + g), pipeline_mode=ONE_BUFFER)


def _acc_spec(t, hb):
    return pl.BlockSpec((t, hb * LANES), lambda g, i: (0, g), pipeline_mode=ONE_BUFFER)


def _hs(hh):
    return slice(hh * LANES, (hh + 1) * LANES)


def _tile_iota(rows, cols):
    return (lax.broadcasted_iota(jnp.int32, (rows, cols), 0), lax.broadcasted_iota(jnp.int32, (rows, cols), 1))


def _kvt_spec(nkv, width, which, hb):
    return pl.BlockSpec((nkv, hb * LANES, width), lambda g, i, w=which: (0, w * (HEADS // hb) + g, 0),
                        pipeline_mode=ONE_BUFFER)


def _qrow_spec(hb, bq=BQ):
    return pl.BlockSpec((hb, 1, 1, bq), lambda g, i: (g, i, 0, 0))


def _vis(key0, query0, rows, cols, kind):
    r, c = _tile_iota(rows, cols)
    k, q = key0 + r, query0 + c
    if kind == "chunk":
        return (k >> CHUNK_SHIFT) <= (q >> CHUNK_SHIFT)
    return (k < q) if kind == "strict" else (k <= q)


def _softmax_fwd(qkv, kvt, *, chunk_mask, scale, name):
    t = qkv.shape[0]
    nq = t // BQA
    per = BQA // BQ
    hb = HB_FWD
    kind = "chunk" if chunk_mask else "causal"

    def body(q_ref, k_ref, vt_ref, o_ref, lse_ref, m_sc, l_sc, acc_sc):
        i = pl.program_id(1)
        m_sc[...] = jnp.full((hb, 1, BQA), NEG, F32)
        l_sc[...] = jnp.zeros((hb, 1, BQA), F32)
        acc_sc[...] = jnp.zeros((hb, LANES, BQA), F32)

        def tile(j, qoff):
            off = pl.multiple_of(j * BQ, BQ)
            lo = 0 if qoff is None else qoff
            qs = slice(lo, BQA)
            vis = None if qoff is None else _vis(off, i * BQA + lo, BQ, BQA - lo, kind)
            ss = [_bdot(k_ref[pl.ds(off, BQ), _hs(hh)], q_ref[qs, _hs(hh)], NT) for hh in range(hb)]
            ps, alphas = [], []
            for hh in range(hb):
                s = ss[hh]
                if scale != 1.0:
                    s = s * scale
                if vis is not None:
                    s = jnp.where(vis, s, NEG)
                m_old = m_sc[hh, :, qs]
                m_new = jnp.maximum(m_old, jnp.max(s, axis=0, keepdims=True))
                alpha = jnp.exp(m_old - m_new)
                p = jnp.exp(s - m_new)
                l_sc[hh, :, qs] = alpha * l_sc[hh, :, qs] + jnp.sum(p, axis=0, keepdims=True)
                m_sc[hh, :, qs] = m_new
                ps.append(p.astype(BF16))
                alphas.append(alpha)
            for hh in range(hb):
                acc_sc[hh, :, qs] = alphas[hh] * acc_sc[hh, :, qs] + _bdot(vt_ref[j, _hs(hh), :], ps[hh])

        def loop(j, carry):
            tile(j, None)
            return carry

        lax.fori_loop(0, per * i, loop, 0)
        for d in range(per):
            tile(per * i + d, d * BQ)
        for hh in range(hb):
            l = l_sc[hh]
            o_ref[:, _hs(hh)] = (acc_sc[hh] / l).T
            lse_ref[hh, 0] = m_sc[hh] + jnp.log(l)

    return pl.pallas_call(
        body, name=name, grid=(HEADS // hb, nq),
        in_specs=[_q_spec(hb, BQA), _kv_spec(t, 1, hb), _kvt_spec(t // BQ, BQ, 1, hb)],
        out_specs=[_q_spec(hb, BQA), _qrow_spec(hb, BQA)],
        out_shape=[jax.ShapeDtypeStruct((t, HP), F32), jax.ShapeDtypeStruct((HEADS, nq, 1, BQA), F32)],
        scratch_shapes=[pltpu.VMEM((hb, 1, BQA), F32), pltpu.VMEM((hb, 1, BQA), F32),
                        pltpu.VMEM((hb, LANES, BQA), F32)],
        compiler_params=_cp("parallel", "arbitrary"))(qkv, qkv, kvt)


def _softmax_bwd(qkv, kvt, do, lse, delta, *, bias, chunk_mask, scale, name):
    t = qkv.shape[0]
    nq = t // BQA
    per = BQA // BQ
    hb = HB_BWD
    kind = "chunk" if chunk_mask else "causal"

    def body(*refs):
        if bias:
            (q_ref, k_ref, v_ref, kt_ref, do_ref, lse_ref, dl_ref, dq_ref, dk_ref, dv_ref, dck_ref, dcq_ref,
             dq_sc, dcq_sc) = refs
            dcq_sc[...] = jnp.zeros((hb, 1, BQA), F32)
        else:
            q_ref, k_ref, v_ref, kt_ref, do_ref, lse_ref, dl_ref, dq_ref, dk_ref, dv_ref, dq_sc = refs
        i = pl.program_id(1)

        @pl.when(i == 0)
        def _():
            dk_ref[...] = jnp.zeros_like(dk_ref)
            dv_ref[...] = jnp.zeros_like(dv_ref)
            if bias:
                dck_ref[...] = jnp.zeros_like(dck_ref)

        dq_sc[...] = jnp.zeros((hb, LANES, BQA), F32)

        def tile(j, qoff):
            off = pl.multiple_of(j * BQ, BQ)
            lo = 0 if qoff is None else qoff
            qsl = slice(lo, BQA)
            vis = None if qoff is None else _vis(off, i * BQA + lo, BQ, BQA - lo, kind)
            qs = [q_ref[qsl, _hs(hh)] for hh in range(hb)]
            dobs = [do_ref[qsl, _hs(hh)].astype(BF16) for hh in range(hb)]
            ss = [_bdot(k_ref[pl.ds(off, BQ), _hs(hh)], qs[hh], NT) for hh in range(hb)]
            dps = [_bdot(v_ref[pl.ds(off, BQ), _hs(hh)], dobs[hh], NT) for hh in range(hb)]
            pbs, dsbs = [], []
            for hh in range(hb):
                s = ss[hh]
                if scale != 1.0:
                    s = s * scale
                p = jnp.exp(s - lse_ref[hh, 0, :, qsl])
                if vis is not None:
                    p = jnp.where(vis, p, 0.0)
                ds = p * (dps[hh] - dl_ref[hh, 0, :, qsl])
                if bias:
                    part = ds[:, 0:LANES]
                    for b in range(1, (BQA - lo) // LANES):
                        part = part + ds[:, b * LANES:(b + 1) * LANES]
                    dck_ref[pl.ds(off, BQ), _hs(hh)] -= part
                    dcq_sc[hh, :, qsl] += jnp.sum(ds, axis=0, keepdims=True)
                if scale != 1.0:
                    ds = ds * scale
                pbs.append(p.astype(BF16))
                dsbs.append(ds.astype(BF16))
            for hh in range(hb):
                sl = _hs(hh)
                dv_ref[pl.ds(off, BQ), sl] += _bdot(pbs[hh], dobs[hh])
                dk_ref[pl.ds(off, BQ), sl] += _bdot(dsbs[hh], qs[hh])
                dq_sc[hh, :, qsl] += _bdot(kt_ref[j, sl, :], dsbs[hh])

        def loop(j, carry):
            tile(j, None)
            return carry

        lax.fori_loop(0, per * i, loop, 0)
        for d in range(per):
            tile(per * i + d, d * BQ)
        for hh in range(hb):
            dq_ref[:, _hs(hh)] = dq_sc[hh].T
            if bias:
                dcq_ref[hh, 0] = dcq_sc[hh]

    in_specs = [_q_spec(hb, BQA), _kv_spec(t, 1, hb), _kv_spec(t, 2, hb), _kvt_spec(t // BQ, BQ, 0, hb),
                _q_spec(hb, BQA), _qrow_spec(hb, BQA), _qrow_spec(hb, BQA)]
    out_specs = [_q_spec(hb, BQA), _acc_spec(t, hb), _acc_spec(t, hb)]
    out_shape = [jax.ShapeDtypeStruct((t, HP), F32)] * 3
    scratch = [pltpu.VMEM((hb, LANES, BQA), F32)]
    if bias:
        out_specs += [_acc_spec(t, hb), _qrow_spec(hb, BQA)]
        out_shape += [jax.ShapeDtypeStruct((t, HP), F32), jax.ShapeDtypeStruct((HEADS, nq, 1, BQA), F32)]
        scratch.append(pltpu.VMEM((hb, 1, BQA), F32))
    return pl.pallas_call(
        body, name=name, grid=(HEADS // hb, nq), in_specs=in_specs, out_specs=out_specs, out_shape=out_shape,
        scratch_shapes=scratch,
        compiler_params=_cp("parallel", "arbitrary"))(qkv, qkv, qkv, kvt, do, lse, delta)


def _ret_diag_decay(lg1, keys_on_rows=False):
    r, c = _tile_iota(BQ, BQ)
    qn, km = (c, r) if keys_on_rows else (r, c)
    dd = jnp.where(km > qn, jnp.exp((2.0 * lg1) * (km - qn).astype(F32)), 1.0)
    return jnp.where((km >> CHUNK_SHIFT) <= (qn >> CHUNK_SHIFT), dd, 0.0)


def _lg_spec(hb):
    return pl.BlockSpec((hb, 1, LANES), lambda g, i: (g, 0, 0))


def _ret_specs(nq, hb, reverse):
    tile = (lambda i: nq - 1 - i) if reverse else (lambda i: i)
    qkv = [pl.BlockSpec((BQ, hb * LANES), lambda g, i, w=w: (tile(i), w * (HEADS // hb) + g)) for w in range(3)]
    kt = pl.BlockSpec((1, hb * LANES, BQ), lambda g, i: (tile(i), g, 0))
    st = pl.BlockSpec((1, hb * LANES, LANES), lambda g, i: (tile(i), g, 0))
    return qkv, kt, st


def _ret_fwd(qkv, kt, lg_heads, name):
    t = qkv.shape[0]
    nq = t // BQ
    hb = HB_FWD

    def body(lg_ref, q_ref, k_ref, v_ref, kt_ref, o_ref, st_ref, s_sc):
        @pl.when(pl.program_id(1) == 0)
        def _():
            s_sc[...] = jnp.zeros_like(s_sc)

        qs = [q_ref[:, _hs(hh)] for hh in range(hb)]
        vs = [v_ref[:, _hs(hh)] for hh in range(hb)]
        aa = [_bdot(qs[hh], k_ref[:, _hs(hh)], NT) for hh in range(hb)]
        kv = [_bdot(kt_ref[0, _hs(hh), :], vs[hh]) for hh in range(hb)]
        for hh in range(hb):
            sl = _hs(hh)
            lg1 = lg_ref[hh][:, 0:1]
            s = s_sc[hh]
            st_ref[0, sl, :] = s
            shi, slo = _split2(s)
            a = (aa[hh] * _ret_diag_decay(lg1)).astype(BF16)
            o_ref[:, sl] = _bdot(a, vs[hh]) + _bdot(qs[hh], shi) + _bdot(qs[hh], slo)
            s_sc[hh] = jnp.exp(lg1 * float(BQ)) * (s + kv[hh])

    qkv_specs, kt_spec, st_spec = _ret_specs(nq, hb, False)
    return pl.pallas_call(
        body, name=name, grid=(HEADS // hb, nq), in_specs=[_lg_spec(hb)] + qkv_specs + [kt_spec],
        out_specs=[_q_spec(hb), st_spec],
        out_shape=[jax.ShapeDtypeStruct((t, HP), F32), jax.ShapeDtypeStruct((nq, HP, LANES), F32)],
        scratch_shapes=[pltpu.VMEM((hb, LANES, LANES), F32)],
        compiler_params=_cp("parallel", "arbitrary"))(lg_heads, qkv, qkv, qkv, kt)


def _ret_bwd(qkv, states, lg_heads, do, name):
    t = qkv.shape[0]
    nq = t // BQ
    hb = HB_FWD

    def body(lg_ref, q_ref, k_ref, v_ref, st_ref, do_ref, dq_ref, dk_ref, dv_ref, g_sc):
        @pl.when(pl.program_id(1) == 0)
        def _():
            g_sc[...] = jnp.zeros_like(g_sc)

        qs = [q_ref[:, _hs(hh)] for hh in range(hb)]
        ks = [k_ref[:, _hs(hh)] for hh in range(hb)]
        vs = [v_ref[:, _hs(hh)] for hh in range(hb)]
        dobs = [do_ref[:, _hs(hh)].astype(BF16) for hh in range(hb)]
        aa = [_bdot(ks[hh], qs[hh], NT) for hh in range(hb)]
        das = [_bdot(vs[hh], dobs[hh], NT) for hh in range(hb)]
        qdo = [_bdot(qs[hh], dobs[hh], TN) for hh in range(hb)]
        for hh in range(hb):
            sl = _hs(hh)
            lg1 = lg_ref[hh][:, 0:1]
            dd = _ret_diag_decay(lg1, keys_on_rows=True)
            at = (aa[hh] * dd).astype(BF16)
            dat = (das[hh] * dd).astype(BF16)
            h = jnp.exp(lg1 * float(BQ)) * g_sc[hh]
            hhi, hlo = _split2(h)
            shi, slo = _split2(st_ref[0, sl, :])
            dv_ref[:, sl] = _bdot(at, dobs[hh]) + _bdot(ks[hh], hhi) + _bdot(ks[hh], hlo)
            dk_ref[:, sl] = _bdot(dat, qs[hh]) + _bdot(vs[hh], hhi, NT) + _bdot(vs[hh], hlo, NT)
            dq_ref[:, sl] = _bdot(dat, ks[hh], TN) + _bdot(dobs[hh], shi, NT) + _bdot(dobs[hh], slo, NT)
            g_sc[hh] = qdo[hh] + h

    qkv_specs, _, st_spec = _ret_specs(nq, hb, True)
    tile_spec = pl.BlockSpec((BQ, hb * LANES), lambda g, i: (nq - 1 - i, g))
    return pl.pallas_call(
        body, name=name, grid=(HEADS // hb, nq), in_specs=[_lg_spec(hb)] + qkv_specs + [st_spec, tile_spec],
        out_specs=[tile_spec] * 3, out_shape=[jax.ShapeDtypeStruct((t, HP), F32)] * 3,
        scratch_shapes=[pltpu.VMEM((hb, LANES, LANES), F32)],
        compiler_params=_cp("parallel", "arbitrary"))(lg_heads, qkv, qkv, qkv, states, do)


def _sb_tile_logs(q, kb, vis):
    z = _bdot(kb, q, NT)
    ls = -(jnp.maximum(z, 0.0) + jnp.log(1.0 + jnp.exp(-jnp.abs(z))))
    if vis is not None:
        ls = jnp.where(vis, ls, 0.0)
    return z, ls


def _sb_later(ls, after):
    hi, lo = _split2(ls)
    return _bdot(after, hi) + _bdot(after, lo)


def _sb_fwd(qkv, kvt, name):
    t = qkv.shape[0]
    nq = t // BQS
    per = BQS // TKS
    hb = HB_SB_FWD

    def body(q_ref, k_ref, vt_ref, o_ref, tot_ref, acc_sc, r_sc):
        i = pl.program_id(1)
        acc_sc[...] = jnp.zeros((hb, LANES, BQS), F32)
        r_sc[...] = jnp.zeros((hb, 1, BQS), F32)
        mr, mc = _tile_iota(TKS, TKS)
        after = jnp.where(mc > mr, 1.0, 0.0).astype(BF16)

        def tile(j, qoff):
            off = pl.multiple_of(j * TKS, TKS)
            lo = 0 if qoff is None else qoff
            qsl = slice(lo, BQS)
            vis = None if qoff is None else _vis(off, i * BQS + lo, TKS, BQS - lo, "strict")
            zl = [_sb_tile_logs(q_ref[qsl, _hs(hh)], k_ref[pl.ds(off, TKS), _hs(hh)], vis) for hh in range(hb)]
            laters = [_sb_later(zl[hh][1], after) for hh in range(hb)]
            ws = []
            for hh in range(hb):
                z, ls = zl[hh]
                w = jnp.exp(z + ls + laters[hh] + r_sc[hh, :, qsl])
                if vis is not None:
                    w = jnp.where(vis, w, 0.0)
                ws.append(w.astype(BF16))
                r_sc[hh, :, qsl] += jnp.sum(ls, axis=0, keepdims=True)
            for hh in range(hb):
                acc_sc[hh, :, qsl] += _bdot(vt_ref[j, _hs(hh), :], ws[hh])

        for d in reversed(range(per)):
            tile(per * i + d, d * TKS)

        def loop(jj, carry):
            tile(per * i - 1 - jj, None)
            return carry

        lax.fori_loop(0, per * i, loop, 0)
        for hh in range(hb):
            o_ref[:, _hs(hh)] = acc_sc[hh].T
            tot_ref[hh, 0] = r_sc[hh]

    return pl.pallas_call(
        body, name=name, grid=(HEADS // hb, nq),
        in_specs=[_q_spec(hb, BQS), _kv_spec(t, 1, hb), _kvt_spec(t // TKS, TKS, 1, hb)],
        out_specs=[_q_spec(hb, BQS), _qrow_spec(hb, BQS)],
        out_shape=[jax.ShapeDtypeStruct((t, HP), F32), jax.ShapeDtypeStruct((HEADS, nq, 1, BQS), F32)],
        scratch_shapes=[pltpu.VMEM((hb, LANES, BQS), F32), pltpu.VMEM((hb, 1, BQS), F32)],
        compiler_params=_cp("parallel", "arbitrary"))(qkv, qkv, kvt)


def _sb_bwd(qkv, kvt, do, tot, name):
    t = qkv.shape[0]
    nq = t // BQS
    per = BQS // TKS
    hb = HB_BWD

    def body(q_ref, k_ref, v_ref, kt_ref, do_ref, tot_ref, dq_ref, dk_ref, dv_ref, dq_sc, p_sc, g_sc):
        i = pl.program_id(1)

        @pl.when(i == 0)
        def _():
            dk_ref[...] = jnp.zeros_like(dk_ref)
            dv_ref[...] = jnp.zeros_like(dv_ref)

        dq_sc[...] = jnp.zeros((hb, LANES, BQS), F32)
        p_sc[...] = jnp.zeros((hb, 1, BQS), F32)
        g_sc[...] = jnp.zeros((hb, 1, BQS), F32)
        mr, mc = _tile_iota(TKS, TKS)
        after = jnp.where(mc > mr, 1.0, 0.0).astype(BF16)
        before = jnp.where(mc < mr, 1.0, 0.0).astype(BF16)

        def tile(j, qoff):
            off = pl.multiple_of(j * TKS, TKS)
            lo = 0 if qoff is None else qoff
            qsl = slice(lo, BQS)
            vis = None if qoff is None else _vis(off, i * BQS + lo, TKS, BQS - lo, "strict")
            qs = [q_ref[qsl, _hs(hh)] for hh in range(hb)]
            dobs = [do_ref[qsl, _hs(hh)].astype(BF16) for hh in range(hb)]
            zl = [_sb_tile_logs(qs[hh], k_ref[pl.ds(off, TKS), _hs(hh)], vis) for hh in range(hb)]
            dws = [_bdot(v_ref[pl.ds(off, TKS), _hs(hh)], dobs[hh], NT) for hh in range(hb)]
            laters = [_sb_later(zl[hh][1], after) for hh in range(hb)]
            ws, gs = [], []
            for hh in range(hb):
                z, ls = zl[hh]
                own = jnp.sum(ls, axis=0, keepdims=True)
                rest = tot_ref[hh, 0, :, qsl] - p_sc[hh, :, qsl] - own
                w = jnp.exp(z + ls + laters[hh] + rest)
                if vis is not None:
                    w = jnp.where(vis, w, 0.0)
                p_sc[hh, :, qsl] += own
                ws.append(w.astype(BF16))
                gs.append(dws[hh] * w)
            gins = [_bdot(before, gs[hh].astype(BF16)) for hh in range(hb)]
            dzbs = []
            for hh in range(hb):
                g = gs[hh]
                stay = jnp.exp(zl[hh][1])
                dz = g * stay - (1.0 - stay) * (gins[hh] + g_sc[hh, :, qsl])
                if vis is not None:
                    dz = jnp.where(vis, dz, 0.0)
                g_sc[hh, :, qsl] += jnp.sum(g, axis=0, keepdims=True)
                dzbs.append(dz.astype(BF16))
            for hh in range(hb):
                sl = _hs(hh)
                dv_ref[pl.ds(off, TKS), sl] += _bdot(ws[hh], dobs[hh])
                dk_ref[pl.ds(off, TKS), sl] += _bdot(dzbs[hh], qs[hh])
                dq_sc[hh, :, qsl] += _bdot(kt_ref[j, sl, :], dzbs[hh])

        def loop(j, carry):
            tile(j, None)
            return carry

        lax.fori_loop(0, per * i, loop, 0)
        for d in range(per):
            tile(per * i + d, d * TKS)
        for hh in range(hb):
            dq_ref[:, _hs(hh)] = dq_sc[hh].T

    return pl.pallas_call(
        body, name=name, grid=(HEADS // hb, nq),
        in_specs=[_q_spec(hb, BQS), _kv_spec(t, 1, hb), _kv_spec(t, 2, hb), _kvt_spec(t // TKS, TKS, 0, hb),
                  _q_spec(hb, BQS), _qrow_spec(hb, BQS)],
        out_specs=[_q_spec(hb, BQS), _acc_spec(t, hb), _acc_spec(t, hb)],
        out_shape=[jax.ShapeDtypeStruct((t, HP), F32)] * 3,
        scratch_shapes=[pltpu.VMEM((hb, LANES, BQS), F32), pltpu.VMEM((hb, 1, BQS), F32),
                        pltpu.VMEM((hb, 1, BQS), F32)],
        compiler_params=_cp("parallel", "arbitrary"))(qkv, qkv, qkv, kvt, do, tot)


def _sigmoid(v):
    return 1.0 / (1.0 + jnp.exp(-v))


def _post_fwd(oa, ob, oc, od, proj, g_pad, name):
    t = oa.shape[0]

    def body(oa_ref, ob_ref, oc_ref, od_ref, rg_ref, g_ref, mx_ref):
        g = g_ref[...]

        def group(o, gg):
            r = lax.rsqrt(jnp.sum(o * o, axis=-1, keepdims=True) * (1.0 / GROUP) + EPS)
            return _gather_heads(o * r * gg).astype(BF16)

        mx_ref[:, 0:GROUP] = group(oa_ref[...], g[:, 0:HP])
        mx_ref[:, GROUP:2 * GROUP] = group(ob_ref[...], g[:, HP:2 * HP])
        mx_ref[:, 3 * GROUP:4 * GROUP] = group(od_ref[...], g[:, 3 * HP:4 * HP])
        real = lax.broadcasted_iota(jnp.int32, (TM, LANES), 1) < HEAD_DIM
        rg = _spread_heads(rg_ref[...])
        gated = []
        for hb in range(HEADS):
            sl = slice(hb * LANES, (hb + 1) * LANES)
            o = oc_ref[:, sl]
            mu = jnp.sum(o, axis=-1, keepdims=True) * (1.0 / HEAD_DIM)
            dlt = jnp.where(real, o - mu, 0.0)
            var = jnp.sum(dlt * dlt, axis=-1, keepdims=True) * (1.0 / HEAD_DIM)
            yn = dlt * lax.rsqrt(var + EPS) * g[:, 2 * HP + hb * LANES:2 * HP + (hb + 1) * LANES]
            x = rg[:, sl]
            gated.append(yn * (x * _sigmoid(x)))
        mx_ref[:, 2 * GROUP:3 * GROUP] = _gather_heads(jnp.concatenate(gated, axis=1)).astype(BF16)

    rg_spec = pl.BlockSpec((TM, GROUP), lambda i: (i, OFF_RG // GROUP))
    return pl.pallas_call(
        body, name=name, grid=(t // TM,),
        in_specs=[_row_spec(HP)] * 4 + [rg_spec, _vec_spec(4 * HP)], out_specs=_row_spec(D_MODEL),
        out_shape=jax.ShapeDtypeStruct((t, D_MODEL), BF16), compiler_params=_cp("parallel"))(oa, ob, oc, od, proj, g_pad)


def _post_bwd(dmx, oa, ob, oc, od, proj, g_pad, name):
    t = oa.shape[0]

    def body(dm_ref, oa_ref, ob_ref, oc_ref, od_ref, rg_ref, g_ref,
             doa_ref, dob_ref, doc_ref, dod_ref, dla_ref, dlb_ref, drg_ref, dg_ref):
        @pl.when(pl.program_id(0) == 0)
        def _():
            dg_ref[...] = jnp.zeros_like(dg_ref)

        g = g_ref[...]

        def group_bwd(dm, o, gg):
            r = lax.rsqrt(jnp.sum(o * o, axis=-1, keepdims=True) * (1.0 / GROUP) + EPS)
            oh = o * r
            dgp = jnp.sum(dm * oh, axis=0, keepdims=True)
            dyh = dm * gg
            do = r * (dyh - oh * (jnp.sum(dyh * oh, axis=-1, keepdims=True) * (1.0 / GROUP)))
            return do, dgp

        def delta_bc(do, o):
            prod = do * o
            lane = lax.broadcasted_iota(jnp.int32, (TM, LANES), 1)
            out = jnp.zeros((TM, LANES), F32)
            for hb in range(HEADS):
                out = jnp.where(lane == hb, jnp.sum(prod[:, hb * LANES:(hb + 1) * LANES], axis=-1, keepdims=True), out)
            return out

        dmp = [_spread_heads(dm_ref[:, gi * GROUP:(gi + 1) * GROUP]) for gi in range(4)]
        rg = _spread_heads(rg_ref[...])
        oa = oa_ref[...]
        do_a, dga = group_bwd(dmp[0], oa, g[:, 0:HP])
        doa_ref[...] = do_a
        dla_ref[...] = delta_bc(do_a, oa)
        dg_ref[:, 0:HP] += dga
        ob = ob_ref[...]
        do_b, dgb = group_bwd(dmp[1], ob, g[:, HP:2 * HP])
        dob_ref[...] = do_b
        dlb_ref[...] = delta_bc(do_b, ob)
        dg_ref[:, HP:2 * HP] += dgb
        do_d, dgd = group_bwd(dmp[3], od_ref[...], g[:, 3 * HP:4 * HP])
        dod_ref[...] = do_d
        dg_ref[:, 3 * HP:4 * HP] += dgd
        real = lax.broadcasted_iota(jnp.int32, (TM, LANES), 1) < HEAD_DIM
        for hb in range(HEADS):
            sl = slice(hb * LANES, (hb + 1) * LANES)
            gsl = slice(2 * HP + hb * LANES, 2 * HP + (hb + 1) * LANES)
            o = oc_ref[:, sl]
            mu = jnp.sum(o, axis=-1, keepdims=True) * (1.0 / HEAD_DIM)
            dlt = jnp.where(real, o - mu, 0.0)
            var = jnp.sum(dlt * dlt, axis=-1, keepdims=True) * (1.0 / HEAD_DIM)
            rstd = lax.rsqrt(var + EPS)
            dhat = dlt * rstd
            gc = g[:, gsl]
            x = rg[:, sl]
            sg = _sigmoid(x)
            dm = dmp[2][:, sl]
            drg_ref[:, sl] = dm * (dhat * gc) * (sg * (1.0 + x * (1.0 - sg)))
            dyn = dm * (x * sg)
            dg_ref[:, gsl] += jnp.sum(dyn * dhat, axis=0, keepdims=True)
            ddh = dyn * gc
            m1 = jnp.sum(ddh, axis=-1, keepdims=True) * (1.0 / HEAD_DIM)
            m2 = jnp.sum(ddh * dhat, axis=-1, keepdims=True) * (1.0 / HEAD_DIM)
            doc_ref[:, sl] = jnp.where(real, rstd * (ddh - m1 - dhat * m2), 0.0)

    rg_spec = pl.BlockSpec((TM, GROUP), lambda i: (i, OFF_RG // GROUP))
    hp = _row_spec(HP)
    return pl.pallas_call(
        body, name=name, grid=(t // TM,),
        in_specs=[_row_spec(D_MODEL), hp, hp, hp, hp, rg_spec, _vec_spec(4 * HP)],
        out_specs=[hp] * 4 + [_row_spec(LANES)] * 2 + [hp, _vec_spec(4 * HP)],
        out_shape=[jax.ShapeDtypeStruct((t, HP), F32)] * 4 + [jax.ShapeDtypeStruct((t, LANES), F32)] * 2
        + [jax.ShapeDtypeStruct((t, HP), F32), jax.ShapeDtypeStruct((1, 4 * HP), F32)],
        compiler_params=_cp("arbitrary"))(dmx, oa, ob, oc, od, proj, g_pad)


def _mesh_pos():
    return lax.axis_index("x"), lax.axis_index("y"), lax.axis_index("c")


def _peer(pos, k):
    x, y, c = pos
    px = 1 - x if (k >> 2) & 1 else x
    py = 1 - y if (k >> 1) & 1 else y
    pc = 1 - c if k & 1 else c
    return (px, py, pc), 4 * px + 2 * py + pc


def _exchange(arrs, gather, name):
    n = len(arrs)

    def body(*refs):
        ins, outs = refs[:n], refs[n:2 * n]
        send_sems, recv_sems, loc_sems = refs[2 * n:]
        pos = _mesh_pos()
        me = 4 * pos[0] + 2 * pos[1] + pos[2]
        local = []
        for a in range(n):
            src = ins[a] if gather else ins[a].at[me]
            cp = pltpu.make_async_copy(src, outs[a].at[me], loc_sems.at[a])
            cp.start()
            local.append(cp)
        sends, recvs = [], []
        for k in range(1, N_DEV):
            peer, pid = _peer(pos, k)
            for a in range(n):
                s = a * (N_DEV - 1) + k - 1
                src = ins[a] if gather else ins[a].at[pid]
                cp = pltpu.make_async_remote_copy(
                    src_ref=src, dst_ref=outs[a].at[me], send_sem=send_sems.at[s], recv_sem=recv_sems.at[s],
                    device_id=peer, device_id_type=pl.DeviceIdType.MESH)
                cp.start()
                sends.append(cp)
                recvs.append(pltpu.make_async_remote_copy(
                    src_ref=src, dst_ref=outs[a].at[pid], send_sem=send_sems.at[s], recv_sem=recv_sems.at[s],
                    device_id=peer, device_id_type=pl.DeviceIdType.MESH))
        for cp in recvs:
            cp.wait_recv()
        for cp in sends:
            cp.wait_send()
        for cp in local:
            cp.wait()

    any_spec = pl.BlockSpec(memory_space=pl.ANY)
    out_shape = [jax.ShapeDtypeStruct((N_DEV,) + tuple(a.shape) if gather else tuple(a.shape), a.dtype) for a in arrs]
    return pl.pallas_call(
        body, name=name, in_specs=[any_spec] * n, out_specs=[any_spec] * n, out_shape=out_shape,
        scratch_shapes=[pltpu.SemaphoreType.DMA((n * (N_DEV - 1),)), pltpu.SemaphoreType.DMA((n * (N_DEV - 1),)),
                        pltpu.SemaphoreType.DMA((n,))],
        compiler_params=pltpu.CompilerParams(has_side_effects=True))(*arrs)


def _device_index():
    x, y, c = _mesh_pos()
    return 4 * x + 2 * y + c


def _landing(srcs, gather):
    me = _device_index()
    lands = []
    for a in srcs:
        own = a[None] if gather else lax.dynamic_slice_in_dim(a, me, 1, axis=0)
        shape = (N_DEV,) + tuple(a.shape) if gather else tuple(a.shape)
        lands.append(lax.dynamic_update_slice_in_dim(lax.empty(shape, a.dtype), own, me, axis=0))
    return lands


def _exchange_copies(ins, lands, send_sems, recv_sems, gather):
    pos = _mesh_pos()
    me = 4 * pos[0] + 2 * pos[1] + pos[2]
    sends, recvs = [], []
    for k in range(1, N_DEV):
        peer, pid = _peer(pos, k)
        for a in range(len(ins)):
            s = a * (N_DEV - 1) + k - 1
            src = ins[a] if gather else ins[a].at[pid]
            sends.append(pltpu.make_async_remote_copy(
                src_ref=src, dst_ref=lands[a].at[me], send_sem=send_sems.at[s], recv_sem=recv_sems.at[s],
                device_id=peer, device_id_type=pl.DeviceIdType.MESH))
            recvs.append(pltpu.make_async_remote_copy(
                src_ref=src, dst_ref=lands[a].at[pid], send_sem=send_sems.at[s], recv_sem=recv_sems.at[s],
                device_id=peer, device_id_type=pl.DeviceIdType.MESH))
    return sends, recvs


def _exchange_start(srcs, gather, name, after=None):
    n = len(srcs)
    lands = _landing(srcs, gather)
    nsem = n * (N_DEV - 1)
    extra = [] if after is None else [after]

    def body(*refs):
        ins, lnd = refs[:n], refs[n:2 * n]
        send_sems, recv_sems = refs[2 * n + len(extra)], refs[2 * n + len(extra) + 1]
        token = refs[-1]
        sends, _ = _exchange_copies(ins, lnd, send_sems, recv_sems, gather)
        for cp in sends:
            cp.start()
        token[...] = jnp.zeros_like(token)

    hbm = pl.BlockSpec(memory_space=pltpu.HBM)
    sem = pl.BlockSpec(memory_space=pltpu.SEMAPHORE)
    bufs = list(srcs) + lands
    out_shape = ([pltpu.SemaphoreType.DMA((nsem,)), pltpu.SemaphoreType.DMA((nsem,))]
                 + [pltpu.HBM(b.shape, b.dtype) for b in bufs] + [jax.ShapeDtypeStruct((8, LANES), F32)])
    outs = pl.pallas_call(
        body, name=name, in_specs=[hbm] * (2 * n) + [pl.BlockSpec(memory_space=pl.ANY)] * len(extra),
        out_specs=[sem, sem] + [hbm] * (2 * n) + [pl.BlockSpec(memory_space=pltpu.VMEM)], out_shape=out_shape,
        input_output_aliases={i: 2 + i for i in range(2 * n)},
        compiler_params=pltpu.CompilerParams(has_side_effects=pltpu.SideEffectType.DATAFLOW_SIDE_EFFECTING),
    )(*[pltpu.with_memory_space_constraint(b, pltpu.HBM) for b in bufs], *extra)
    return (outs[0], outs[1], outs[2:2 + n], outs[2 + n:2 + 2 * n]), outs[-1]


def _exchange_wait(state, after, gather, name):
    send_sems, recv_sems, srcs, lands = state
    n = len(srcs)
    after = list(after) if isinstance(after, (list, tuple)) else [after]

    def body(*refs):
        ins, lnd = refs[:n], refs[n:2 * n]
        s_sems, r_sems = refs[2 * n], refs[2 * n + 1]
        sends, recvs = _exchange_copies(ins, lnd, s_sems, r_sems, gather)
        for cp in sends:
            cp.wait_send()
        for cp in recvs:
            cp.wait_recv()

    hbm = pl.BlockSpec(memory_space=pltpu.HBM)
    sem = pl.BlockSpec(memory_space=pltpu.SEMAPHORE)
    bufs = list(srcs) + list(lands)
    outs = pl.pallas_call(
        body, name=name, in_specs=[hbm] * (2 * n) + [sem, sem] + [pl.BlockSpec(memory_space=pl.ANY)] * len(after),
        out_specs=[hbm] * (2 * n), out_shape=[pltpu.HBM(b.shape, b.dtype) for b in bufs],
        input_output_aliases={i: i for i in range(2 * n)},
        compiler_params=pltpu.CompilerParams(has_side_effects=pltpu.SideEffectType.DATAFLOW_SIDE_EFFECTING),
    )(*bufs, send_sems, recv_sems, *after)
    return outs[n:]


def _adam_vals(w, g, m, v):
    m = ADAM_B1 * m + (1.0 - ADAM_B1) * g
    v = ADAM_B2 * v + (1.0 - ADAM_B2) * (g * g)
    m_hat = m / ADAM_C1
    v_hat = v / ADAM_C2
    delta = -ADAM_LR * (m_hat / (jnp.sqrt(v_hat) + ADAM_EPS) + ADAM_WD * w)
    return delta, m, v


def _small_allreduce_adam(part, w, m, v, name):
    rows = part.shape[0]

    def body(p_ref, w_ref, m_ref, v_ref, g_ref, d_ref, nm_ref, nv_ref, gath, send_sems, recv_sems):
        pos = _mesh_pos()
        me = 4 * pos[0] + 2 * pos[1] + pos[2]
        gath[me] = p_ref[...]
        sends, recvs = [], []
        for k in range(1, N_DEV):
            peer, pid = _peer(pos, k)
            cp = pltpu.make_async_remote_copy(
                src_ref=p_ref, dst_ref=gath.at[me], send_sem=send_sems.at[k - 1], recv_sem=recv_sems.at[k - 1],
                device_id=peer, device_id_type=pl.DeviceIdType.MESH)
            cp.start()
            sends.append(cp)
            recvs.append(pltpu.make_async_remote_copy(
                src_ref=p_ref, dst_ref=gath.at[pid], send_sem=send_sems.at[k - 1], recv_sem=recv_sems.at[k - 1],
                device_id=peer, device_id_type=pl.DeviceIdType.MESH))
        for cp in recvs:
            cp.wait_recv()
        for cp in sends:
            cp.wait_send()
        g = gath[0]
        for p in range(1, N_DEV):
            g = g + gath[p]
        g_ref[...] = g
        d, nm, nv = _adam_vals(w_ref[...], g, m_ref[...], v_ref[...])
        d_ref[...] = d
        nm_ref[...] = nm
        nv_ref[...] = nv

    vm = pl.BlockSpec(memory_space=pltpu.VMEM)
    sds = jax.ShapeDtypeStruct((rows, LANES), F32)
    return pl.pallas_call(
        body, name=name, in_specs=[vm] * 4, out_specs=[vm] * 4, out_shape=[sds] * 4,
        scratch_shapes=[pltpu.VMEM((N_DEV, rows, LANES), F32), pltpu.SemaphoreType.DMA((N_DEV - 1,)),
                        pltpu.SemaphoreType.DMA((N_DEV - 1,))],
        compiler_params=pltpu.CompilerParams(has_side_effects=True))(part, w, m, v)


def _reduce_adam(recv, w, m, v, name):
    shape = w.shape
    c = shape[-1]
    r = int(np.prod(shape[:-1]))
    recv2, w2, m2, v2 = recv.reshape(N_DEV, r, c), w.reshape(r, c), m.reshape(r, c), v.reshape(r, c)
    tr = r
    while tr * c * 4 > (1 << 20) and tr % 16 == 0:
        tr //= 2

    def body(r_ref, w_ref, m_ref, v_ref, g_ref, d_ref, nm_ref, nv_ref):
        g = r_ref[0].astype(F32)
        for p in range(1, N_DEV):
            g = g + r_ref[p].astype(F32)
        g_ref[...] = g
        d, nm, nv = _adam_vals(w_ref[...], g, m_ref[...], v_ref[...])
        d_ref[...] = d
        nm_ref[...] = nm
        nv_ref[...] = nv

    spec = pl.BlockSpec((tr, c), lambda i: (i, 0))
    sds = jax.ShapeDtypeStruct((r, c), F32)
    outs = pl.pallas_call(
        body, name=name, grid=(r // tr,),
        in_specs=[pl.BlockSpec((N_DEV, tr, c), lambda i: (0, i, 0)), spec, spec, spec],
        out_specs=[spec] * 4, out_shape=[sds] * 4, compiler_params=_cp("parallel"))(recv2, w2, m2, v2)
    return [o.reshape(shape) for o in outs]


def _pad_heads(w, real=HEAD_DIM):
    lead = w.shape[:-1]
    w = w.reshape(lead + (HEADS, real))
    w = jnp.pad(w, [(0, 0)] * len(lead) + [(0, 0), (0, LANES - real)])
    return w.reshape(lead + (HP,))


def _unpad_heads(w, real=HEAD_DIM):
    lead = w.shape[:-1]
    return w.reshape(lead + (HEADS, LANES))[..., :real].reshape(lead + (HEADS * real,))


_IN_SEGS = (("fq", 0, 256), ("fk", 256, 512), ("fv", 512, 768), ("ff", 768, 772), ("cq", 772, 1028),
            ("ckv", 1028, 1156), ("kr", 1156, 1188), ("rq", 1188, 1444), ("rk", 1444, 1700), ("rv", 1700, 1956),
            ("rg", 1956, 2212), ("sq", 2212, 2468), ("sk", 2468, 2724), ("sv", 2724, 2980))


def _pad_w_in(w):
    s = {n: w[:, a:b] for n, a, b in _IN_SEGS}
    rows = w.shape[0]
    z = lambda n: jnp.zeros((rows, n), w.dtype)
    parts = [s[n] for n in ("fq", "fk", "fv", "rq", "rk", "rv", "sq", "sk", "sv", "rg", "cq", "ckv")]
    parts += [z(HEAD_DIM), s["kr"], z(LANES - HEAD_DIM - ROPE_MLA), s["ff"], z(LANES - HEADS),
              z(NP_IN - OFF_FF - LANES)]
    return jnp.concatenate(parts, axis=1)


def _unpad_w_in(wp):
    seg = lambda off, n=GROUP: wp[:, off:off + n]
    parts = [seg(OFF_FOX), seg(OFF_FOX + GROUP), seg(OFF_FOX + 2 * GROUP), seg(OFF_FF, HEADS),
             seg(OFF_CQ, Q_RANK), seg(OFF_CKV, KV_RANK), seg(OFF_KR + HEAD_DIM, ROPE_MLA),
             seg(OFF_RET), seg(OFF_RET + GROUP), seg(OFF_RET + 2 * GROUP), seg(OFF_RG),
             seg(OFF_SB), seg(OFF_SB + GROUP), seg(OFF_SB + 2 * GROUP)]
    return jnp.concatenate(parts, axis=1)


def _pad_w_kv(w):
    w4 = w.reshape(KV_RANK, HEADS, 2 * HEAD_DIM)
    k = w4[:, :, :HEAD_DIM].reshape(KV_RANK, GROUP)
    v = w4[:, :, HEAD_DIM:].reshape(KV_RANK, GROUP)
    return jnp.concatenate([_pad_heads(k), _pad_heads(v)], axis=1)


def _unpad_w_kv(wp):
    k = _unpad_heads(wp[:, :HP]).reshape(KV_RANK, HEADS, HEAD_DIM)
    v = _unpad_heads(wp[:, HP:]).reshape(KV_RANK, HEADS, HEAD_DIM)
    return jnp.concatenate([k, v], axis=-1).reshape(KV_RANK, HEADS * 2 * HEAD_DIM)


def _pad_gain_out(g):
    g = jnp.pad(g.reshape(4 * HEADS, HEAD_DIM), ((0, 0), (0, LANES - HEAD_DIM)))
    return g.reshape(1, 4 * HP)


def _unpad_gain_out(gp):
    return gp.reshape(4 * HEADS, LANES)[:, :HEAD_DIM].reshape(D_MODEL)


_SMALL = (("g_mix_pre", 1024), ("g_mix_post", 1024), ("g_ffn_pre", 1024), ("g_ffn_post", 1024), ("g_mix_out", 1024),
          ("g_q_lora", 256), ("g_kv_lora", 128), ("b_forget", 4))


def _pack_small(vals):
    parts = []
    for name, n in _SMALL:
        a = vals[name].astype(F32)
        if n < LANES:
            a = jnp.pad(a, ((0, 0), (0, LANES - n)))
        parts.append(a)
    return jnp.concatenate(parts, axis=1).reshape(DEPTH * SMALL_ROWS, LANES)


def _unpack_small(packed):
    flat = packed.reshape(DEPTH, SMALL_ROWS * LANES)
    out, off = {}, 0
    for name, n in _SMALL:
        out[name] = flat[:, off:off + n]
        off += max(n, LANES)
    return out


def kernel(x, positions, g_mix_pre, w_in, b_forget, g_q_lora, w_q_up, g_kv_lora, w_kv_up, g_mix_out, w_out, g_mix_post, g_ffn_pre, w_ffn_up, w_ffn_down, g_ffn_post, loss_target, m_g_mix_pre, m_w_in, m_b_forget, m_g_q_lora, m_w_q_up, m_g_kv_lora, m_w_kv_up, m_g_mix_out, m_w_out, m_g_mix_post, m_g_ffn_pre, m_w_ffn_up, m_w_ffn_down, m_g_ffn_post, v_g_mix_pre, v_w_in, v_b_forget, v_g_q_lora, v_w_q_up, v_g_kv_lora, v_w_kv_up, v_g_mix_out, v_w_out, v_g_mix_post, v_g_ffn_pre, v_w_ffn_up, v_w_ffn_down, v_g_ffn_post):
    t = x.shape[1]
    nq = t // BQ
    x0 = x[0]
    tgt = loss_target[0]
    pos = positions[0].astype(F32).reshape(t, 1)

    half_r, half_m = HEAD_DIM // 2, ROPE_MLA // 2
    invf_r = ROPE_BASE ** (-jnp.arange(half_r, dtype=F32) / half_r)
    invf_m = ROPE_BASE ** (-jnp.arange(half_m, dtype=F32) / half_m)
    invf = jnp.concatenate([invf_r, invf_r, invf_m, invf_m,
                            jnp.zeros((LANES - HEAD_DIM - ROPE_MLA,), F32)]).reshape(1, LANES)
    log_gamma = jnp.log1p(-jnp.power(2.0, -5.0 - jnp.arange(HEADS, dtype=F32)))
    lg_lanes = jnp.repeat(log_gamma, LANES).reshape(1, HP)
    lg_heads = jnp.broadcast_to(log_gamma[:, None, None], (HEADS, 1, LANES))

    big = [w_in, w_q_up, w_kv_up, w_out, w_ffn_up, w_ffn_down]
    bf = lambda w: w.astype(BF16)
    first = _exchange([bf(w_in[0]), bf(w_q_up), bf(w_kv_up)], True, "weights_gather_first")
    l0_state, l0_token = _exchange_start([bf(w[0]) for w in (w_out, w_ffn_up, w_ffn_down)], True,
                                         "weights_gather_layer0_start")
    l1_state, rest_token = _exchange_start([bf(w[1]) for w in (w_in, w_out, w_ffn_up, w_ffn_down)], True,
                                           "weights_gather_layer1_start", after=l0_token)
    wq_g = first[1].transpose(1, 2, 0, 3).reshape(DEPTH, Q_RANK, 384)
    wkv_g = first[2].transpose(1, 2, 0, 3).reshape(DEPTH, KV_RANK, 512)

    row = lambda g: g.reshape(1, -1)
    layers = []
    for l in range(DEPTH):
        layers.append(dict(
            wq=_pad_heads(wq_g[l], 96), wkv=_pad_w_kv(wkv_g[l]),
            g_pre=row(g_mix_pre[l]), g_post=row(g_mix_post[l]), g_fpre=row(g_ffn_pre[l]), g_fpost=row(g_ffn_post[l]),
            g_out=_pad_gain_out(g_mix_out[l]), g_q=row(g_q_lora[l]), g_kv=row(g_kv_lora[l]),
            b_pad=jnp.pad(b_forget[l], (0, LANES - HEADS)).reshape(1, LANES)))
    layers[0]["win"] = _pad_w_in(first[0].reshape(D_MODEL, D_IN))

    saved = []
    xin = x0
    h = _rms_fwd(xin, layers[0]["g_pre"] + rest_token[0:1, 0:1], "rms_pre_0")
    loss_row = dx = None
    for l, p in enumerate(layers):
        s = dict(x=xin, h=h)
        proj = _mm(h, p["win"], name=f"in_proj_{l}", tm=512, tn=NP_IN)
        cum = _cumsum(_forget_lsf(proj, p["b_pad"], f"forget_lsf_{l}"), False, f"forget_cumsum_{l}")
        fox, ret, sb, mla, fox_t, sb_t, mla_t, ret_t = _prep_fwd(proj, cum, pos, invf, lg_lanes, p["g_q"],
                                                          p["g_kv"], p["wq"], p["wkv"], f"prep_fwd_{l}")
        oa, lse_a = _softmax_fwd(fox, fox_t, chunk_mask=False, scale=1.0, name=f"fox_fwd_{l}")
        ob, lse_b = _softmax_fwd(mla, mla_t, chunk_mask=True, scale=96.0 ** -0.5, name=f"mla_fwd_{l}")
        oc, ret_st = _ret_fwd(ret, ret_t, lg_heads, f"ret_fwd_{l}")
        od, sb_tot = _sb_fwd(sb, sb_t, f"sb_fwd_{l}")
        if l == 0:
            got = _exchange_wait(l0_state, od, True, "weights_gather_layer0_wait")
            p.update(wout=got[0].reshape(D_MODEL, D_MODEL), wup=got[1].transpose(1, 0, 2).reshape(D_MODEL, D_FF),
                     wdn=got[2].reshape(D_FF, D_MODEL))
        mixed = _post_fwd(oa, ob, oc, od, proj, p["g_out"], f"post_fwd_{l}")
        mix = _mm(mixed, p["wout"], name=f"out_proj_{l}")
        x1, h2 = _add_rms_fwd(xin, mix, p["g_post"], p["g_fpre"], f"mix_residual_{l}")
        a = _mm(h2, p["wup"], name=f"ffn_up_{l}", out_dtype=BF16)
        y = _mm(a, p["wdn"], name=f"ffn_down_{l}", a_fn=_relu2)
        s.update(proj=proj, fox=fox, ret=ret, sb=sb, mla=mla, fox_t=fox_t, sb_t=sb_t, mla_t=mla_t, ret_st=ret_st, oa=oa, ob=ob, oc=oc,
                 od=od, sb_tot=sb_tot, lse_a=lse_a, lse_b=lse_b, mixed=mixed, mix=mix, x1=x1, h2=h2, a=a, y=y)
        saved.append(s)
        if l == 0:
            got = _exchange_wait(l1_state, y, True, "weights_gather_layer1_wait")
            layers[1].update(win=_pad_w_in(got[0].reshape(D_MODEL, D_IN)), wout=got[1].reshape(D_MODEL, D_MODEL),
                             wup=got[2].transpose(1, 0, 2).reshape(D_MODEL, D_FF), wdn=got[3].reshape(D_FF, D_MODEL))
        if l + 1 < DEPTH:
            xin, h = _add_rms_fwd(x1, y, p["g_fpost"], layers[l + 1]["g_pre"], f"ffn_residual_{l}")
        else:
            loss_row, dx = _final_loss(x1, y, p["g_fpost"], tgt, "loss")

    small_g = {n: [None] * DEPTH for n, _ in _SMALL}
    big_g = [[None] * DEPTH for _ in range(6)]
    to_send = [
        lambda g: g.reshape(N_DEV, 1, D_MODEL // N_DEV, D_IN),
        lambda g: g.reshape(Q_RANK, N_DEV, 384 // N_DEV).transpose(1, 0, 2)[:, None],
        lambda g: g.reshape(KV_RANK, N_DEV, 512 // N_DEV).transpose(1, 0, 2)[:, None],
        lambda g: g.reshape(N_DEV, 1, D_MODEL // N_DEV, D_MODEL),
        lambda g: g.reshape(D_MODEL, N_DEV, D_FF // N_DEV).transpose(1, 0, 2)[:, None],
        lambda g: g.reshape(N_DEV, 1, D_FF // N_DEV, D_MODEL),
    ]
    send_of = lambda ks, l: [to_send[k](big_g[k][l]).astype(BF16) for k in ks]
    late_state = early_state = None
    order_token = jnp.zeros((1, 1), F32)
    for l in reversed(range(DEPTH)):
        p, s = layers[l], saved[l]
        dy, dg = _norm_bwd(dx, s["y"], p["g_fpost"] + order_token, None, BF16, f"ffn_post_bwd_{l}")
        small_g["g_ffn_post"][l] = dg
        da = _mm(dy, p["wdn"], name=f"ffn_down_dx_{l}", tb=True, out_dtype=BF16, epi=_drelu2, epi_in=s["a"])
        big_g[5][l] = _mm(s["a"], dy, name=f"ffn_down_dw_{l}", ta=True, a_fn=_relu2, tk=2048)
        big_g[4][l] = _mm(s["h2"], da, name=f"ffn_up_dw_{l}", ta=True, tk=2048)
        dh2 = _mm(da, p["wup"], name=f"ffn_up_dx_{l}", tb=True)
        dx1, dg = _norm_bwd(dh2, s["x1"], p["g_fpre"], dx, F32, f"ffn_pre_bwd_{l}")
        small_g["g_ffn_pre"][l] = dg
        dmix, dg = _norm_bwd(dx1, s["mix"], p["g_post"], None, BF16, f"mix_post_bwd_{l}")
        small_g["g_mix_post"][l] = dg
        dmixed = _mm(dmix, p["wout"], name=f"out_proj_dx_{l}", tb=True)
        big_g[3][l] = _mm(s["mixed"], dmix, name=f"out_proj_dw_{l}", ta=True, tk=2048)
        g_out = p["g_out"]
        if l == 0:
            early_state, early_token = _exchange_start(send_of((3, 4, 5), 0), False, "grads_layer0_early_start")
            g_out = g_out + early_token[0:1, 0:1]
        doa, dob, doc, dod, dla, dlb, drg, dgo = _post_bwd(dmixed, s["oa"], s["ob"], s["oc"], s["od"], s["proj"],
                                                           g_out, f"post_bwd_{l}")
        small_g["g_mix_out"][l] = _unpad_gain_out(dgo).reshape(1, D_MODEL)
        as_rows = lambda a: a[:, :HEADS].T.reshape(HEADS, t // BQA, 1, BQA)
        dfq, dfk, dfv, dcum_k, dcum_q = _softmax_bwd(s["fox"], s["fox_t"], doa, s["lse_a"], as_rows(dla), bias=True,
                                                     chunk_mask=False, scale=1.0, name=f"fox_bwd_{l}")
        dmq, dmk, dmv = _softmax_bwd(s["mla"], s["mla_t"], dob, s["lse_b"], as_rows(dlb), bias=False, chunk_mask=True,
                                     scale=96.0 ** -0.5, name=f"mla_bwd_{l}")
        drq, drk, drv = _ret_bwd(s["ret"], s["ret_st"], lg_heads, doc, f"ret_bwd_{l}")
        dsq, dsk, dsv = _sb_bwd(s["sb"], s["sb_t"], dod, s["sb_tot"], f"sb_bwd_{l}")
        dcum_q = jnp.pad(dcum_q.reshape(HEADS, t).T, ((0, 0), (0, LANES - HEADS)))
        dlsf = _cumsum(dcum_q, True, f"forget_cumsum_bwd_{l}", partials=dcum_k)
        dproj, dwq, dwkv, dgq, dgkv, dbf = _prep_bwd(
            (dfq, dfk, dfv), (drq, drk, drv), (dsq, dsk, dsv), (dmq, dmk, dmv), drg, dlsf, s["proj"], pos, invf,
            lg_lanes, p["b_pad"], p["g_q"], p["g_kv"], p["wq"], p["wkv"], f"prep_bwd_{l}")
        small_g["g_q_lora"][l] = dgq
        small_g["g_kv_lora"][l] = dgkv
        small_g["b_forget"][l] = dbf[:, :HEADS]
        big_g[1][l] = _unpad_heads(dwq, 96)
        big_g[2][l] = _unpad_w_kv(dwkv)
        dh = _mm(dproj, p["win"], name=f"in_proj_dx_{l}", tb=True, tk=NP_IN // 2)
        big_g[0][l] = _unpad_w_in(_mm(s["h"], dproj, name=f"in_proj_dw_{l}", ta=True, tn=NP_IN // 2))
        g_pre = p["g_pre"]
        if l == 0:
            last_state, last_token = _exchange_start(send_of((0, 1, 2), 0), False, "grads_layer0_rest_start")
            g_pre = g_pre + last_token[0:1, 0:1]
        dx, dg = _norm_bwd(dh, s["x"], g_pre, dx1, F32, f"mix_pre_bwd_{l}")
        small_g["g_mix_pre"][l] = dg
        if l == DEPTH - 1:
            late_state, late_token = _exchange_start(send_of(range(6), l), False, "grads_layer1_start")
            order_token = late_token[0:1, 0:1]
    grad_x = dx.reshape(1, t, D_MODEL)

    res = {}
    small_w = dict(g_mix_pre=g_mix_pre, g_mix_post=g_mix_post, g_ffn_pre=g_ffn_pre, g_ffn_post=g_ffn_post,
                   g_mix_out=g_mix_out, g_q_lora=g_q_lora, g_kv_lora=g_kv_lora, b_forget=b_forget)
    small_m = dict(g_mix_pre=m_g_mix_pre, g_mix_post=m_g_mix_post, g_ffn_pre=m_g_ffn_pre, g_ffn_post=m_g_ffn_post,
                   g_mix_out=m_g_mix_out, g_q_lora=m_g_q_lora, g_kv_lora=m_g_kv_lora, b_forget=m_b_forget)
    small_v = dict(g_mix_pre=v_g_mix_pre, g_mix_post=v_g_mix_post, g_ffn_pre=v_g_ffn_pre, g_ffn_post=v_g_ffn_post,
                   g_mix_out=v_g_mix_out, g_q_lora=v_g_q_lora, g_kv_lora=v_g_kv_lora, b_forget=v_b_forget)
    n_small = DEPTH * SMALL_ROWS
    extra = lambda a: jnp.concatenate([a, jnp.zeros((8, LANES), F32)], axis=0)
    part = jnp.concatenate([_pack_small({n: jnp.concatenate(small_g[n], axis=0) for n, _ in _SMALL}),
                            jnp.broadcast_to(loss_row, (8, LANES))], axis=0)
    sres = _small_allreduce_adam(part, extra(_pack_small(small_w)), extra(_pack_small(small_m)),
                                 extra(_pack_small(small_v)), "small_allreduce_adamw")
    loss = sres[0][n_small, 0]
    sg, sd, sm, sv = [_unpack_small(a[:n_small]) for a in sres]
    for n, _ in _SMALL:
        res[n] = [sg[n], sd[n], sm[n], sv[n]]

    late = _exchange_wait(late_state, dx, False, "grads_layer1_wait")
    early = _exchange_wait(early_state, dx, False, "grads_layer0_early_wait")
    ms = [m_w_in, m_w_q_up, m_w_kv_up, m_w_out, m_w_ffn_up, m_w_ffn_down]
    vs = [v_w_in, v_w_q_up, v_w_kv_up, v_w_out, v_w_ffn_up, v_w_ffn_down]
    names = ["w_in", "w_q_up", "w_kv_up", "w_out", "w_ffn_up", "w_ffn_down"]
    for k in (3, 4, 5):
        recv = jnp.concatenate([early[k - 3], late[k]], axis=1)
        res[names[k]] = _reduce_adam(recv, big[k], ms[k], vs[k], f"adamw_{names[k]}")
    done = [sres[0]] + [res[names[k]][0] for k in (3, 4, 5)]
    last = _exchange_wait(last_state, done, False, "grads_layer0_rest_wait")
    for k in (0, 1, 2):
        recv = jnp.concatenate([last[k], late[k]], axis=1)
        res[names[k]] = _reduce_adam(recv, big[k], ms[k], vs[k], f"adamw_{names[k]}")

    order = ["g_mix_pre", "w_in", "b_forget", "g_q_lora", "w_q_up", "g_kv_lora", "w_kv_up", "g_mix_out", "w_out",
             "g_mix_post", "g_ffn_pre", "w_ffn_up", "w_ffn_down", "g_ffn_post"]
    outs = [loss, grad_x]
    for idx in range(4):
        outs += [res[n][idx] for n in order]
    return tuple(outs)
```

```python
import functools
import math

import numpy as np
import jax
import jax.numpy as jnp
from jax import lax
from jax.experimental import pallas as pl
from jax.experimental.pallas import tpu as pltpu

F32 = jnp.float32
BF16 = jnp.bfloat16

D_MODEL = 1024
DEPTH = 2
N_DEV = 8
GROUP = 256
HEADS = 4
HEAD_DIM = 64
LANES = 128
HP = HEADS * LANES
QKV = 3 * HP
Q_RANK = 256
KV_RANK = 128
ROPE_MLA = 32
D_FF = 4096
D_IN = 2980
CHUNK_SHIFT = 6
EPS = 1e-6
ROPE_BASE = 10000.0
NEG = -1e30

QKV_IN = 3 * GROUP
OFF_FOX, OFF_RET, OFF_SB = 0, QKV_IN, 2 * QKV_IN
OFF_RG = 3 * QKV_IN
OFF_CQ = OFF_RG + GROUP
OFF_CKV = OFF_CQ + Q_RANK
OFF_KR = OFF_CKV + LANES
OFF_FF = OFF_KR + LANES
NP_IN = 3328

BQ = 256
TKS = 128
TM = 256
VMEM_LIMIT = 58 * 1024 * 1024

ADAM_LR, ADAM_B1, ADAM_B2, ADAM_EPS, ADAM_WD, ADAM_STEP = 0.001, 0.9, 0.999, 1e-08, 0.01, 10
ADAM_C1 = 1.0 - ADAM_B1 ** ADAM_STEP
ADAM_C2 = 1.0 - ADAM_B2 ** ADAM_STEP

SMALL_ROWS = 44

NT = (((1,), (1,)), ((), ()))
TN = (((0,), (0,)), ((), ()))


def _cp(*sem):
    return pltpu.CompilerParams(dimension_semantics=sem if sem else None, vmem_limit_bytes=VMEM_LIMIT)


def _bdot(a, b, dn=None):
    if dn is None:
        return jnp.dot(a, b, preferred_element_type=F32)
    return lax.dot_general(a, b, dn, preferred_element_type=F32)


def _split2(x):
    hi = x.astype(BF16)
    lo = (x - hi.astype(F32)).astype(BF16)
    return hi, lo


def _mm(a, b, *, name, ta=False, tb=False, out_dtype=F32, a_fn=None, epi=None, epi_in=None,
        tm=1024, tn=1024, tk=1024):
    m, k = (a.shape[1], a.shape[0]) if ta else a.shape
    n = b.shape[0] if tb else b.shape[1]
    tm, tn, tk = min(tm, m), min(tn, n), min(tk, k)
    assert m % tm == 0 and n % tn == 0 and k % tk == 0, (name, m, n, k)
    nk = k // tk
    dn = (((0 if ta else 1,), (1 if tb else 0,)), ((), ()))

    def body(*refs):
        if epi is None:
            a_ref, b_ref, o_ref = refs[:3]
            e_ref = None
            rest = refs[3:]
        else:
            a_ref, b_ref, e_ref, o_ref = refs[:4]
            rest = refs[4:]
        av = a_ref[...]
        if a_fn is not None:
            av = a_fn(av)
        part = lax.dot_general(av.astype(BF16), b_ref[...].astype(BF16), dn, preferred_element_type=F32)

        def finish(r):
            if epi is not None:
                r = epi(r, e_ref[...])
            o_ref[...] = r.astype(out_dtype)

        if nk == 1:
            finish(part)
        else:
            acc_ref = rest[0]
            kk = pl.program_id(2)

            @pl.when(kk == 0)
            def _():
                acc_ref[...] = part

            @pl.when(kk > 0)
            def _():
                acc_ref[...] += part

            @pl.when(kk == nk - 1)
            def _():
                finish(acc_ref[...])

    a_spec = pl.BlockSpec((tk, tm), lambda i, j, kk: (kk, i)) if ta else pl.BlockSpec((tm, tk), lambda i, j, kk: (i, kk))
    b_spec = pl.BlockSpec((tn, tk), lambda i, j, kk: (j, kk)) if tb else pl.BlockSpec((tk, tn), lambda i, j, kk: (kk, j))
    o_spec = pl.BlockSpec((tm, tn), lambda i, j, kk: (i, j))
    in_specs = [a_spec, b_spec]
    args = [a, b]
    if epi is not None:
        in_specs.append(o_spec)
        args.append(epi_in)
    return pl.pallas_call(
        body, name=name, grid=(m // tm, n // tn, nk),
        in_specs=in_specs, out_specs=o_spec,
        out_shape=jax.ShapeDtypeStruct((m, n), out_dtype),
        scratch_shapes=[pltpu.VMEM((tm, tn), F32)] if nk > 1 else [],
        compiler_params=_cp("parallel", "parallel", "arbitrary"),
    )(*args)


def _relu2(v):
    r = jnp.maximum(v, 0.0)
    return r * r


def _drelu2(du, av):
    return du * (2.0 * jnp.maximum(av, 0.0))


def _rms(v, g):
    r = lax.rsqrt(jnp.mean(v * v, axis=-1, keepdims=True) + EPS)
    return v * r * g


def _row_spec(w):
    return pl.BlockSpec((TM, w), lambda i: (i, 0))


def _vec_spec(w):
    return pl.BlockSpec((1, w), lambda i: (0, 0))


def _rms_fwd(x, g, name):
    t, d = x.shape

    def body(x_ref, g_ref, h_ref):
        h_ref[...] = _rms(x_ref[...], g_ref[...]).astype(BF16)

    return pl.pallas_call(
        body, name=name, grid=(t // TM,), in_specs=[_row_spec(d), _vec_spec(d)], out_specs=_row_spec(d),
        out_shape=jax.ShapeDtypeStruct((t, d), BF16), compiler_params=_cp("parallel"))(x, g)


def _add_rms_fwd(x, y, g1, g2, name):
    t, d = x.shape

    def body(x_ref, y_ref, g1_ref, g2_ref, xn_ref, h_ref):
        xn = x_ref[...] + _rms(y_ref[...], g1_ref[...])
        xn_ref[...] = xn
        h_ref[...] = _rms(xn, g2_ref[...]).astype(BF16)

    return pl.pallas_call(
        body, name=name, grid=(t // TM,),
        in_specs=[_row_spec(d), _row_spec(d), _vec_spec(d), _vec_spec(d)],
        out_specs=[_row_spec(d), _row_spec(d)],
        out_shape=[jax.ShapeDtypeStruct((t, d), F32), jax.ShapeDtypeStruct((t, d), BF16)],
        compiler_params=_cp("parallel"))(x, y, g1, g2)


def _final_loss(x, y, g, tgt, name):
    t, d = x.shape

    def body(x_ref, y_ref, g_ref, t_ref, l_ref, dx_ref):
        @pl.when(pl.program_id(0) == 0)
        def _():
            l_ref[...] = jnp.zeros_like(l_ref)

        err = x_ref[...] + _rms(y_ref[...], g_ref[...]) - t_ref[...]
        dx_ref[...] = err * (1.0 / d)
        l_ref[...] += jnp.sum(jnp.sum(err * err, axis=1, keepdims=True), axis=0, keepdims=True) * (0.5 / d)

    return pl.pallas_call(
        body, name=name, grid=(t // TM,),
        in_specs=[_row_spec(d), _row_spec(d), _vec_spec(d), _row_spec(d)],
        out_specs=[pl.BlockSpec((1, LANES), lambda i: (0, 0)), _row_spec(d)],
        out_shape=[jax.ShapeDtypeStruct((1, LANES), F32), jax.ShapeDtypeStruct((t, d), F32)],
        compiler_params=_cp("arbitrary"))(x, y, g, tgt)


def _rms_bwd_vals(dn, v, g):
    w = v.shape[-1]
    r = lax.rsqrt(jnp.mean(v * v, axis=-1, keepdims=True) + EPS)
    vh = v * r
    dgp = jnp.sum(dn * vh, axis=0, keepdims=True)
    dvh = dn * g
    dv = r * (dvh - vh * (jnp.sum(dvh * vh, axis=-1, keepdims=True) * (1.0 / w)))
    return dv, dgp


def _norm_bwd(dn, v, g, resid, out_dtype, name):
    t, d = v.shape
    has_res = resid is not None

    def body(*refs):
        if has_res:
            dn_ref, v_ref, g_ref, r_ref, dv_ref, dg_ref = refs
        else:
            dn_ref, v_ref, g_ref, dv_ref, dg_ref = refs

        @pl.when(pl.program_id(0) == 0)
        def _():
            dg_ref[...] = jnp.zeros_like(dg_ref)

        dv, dgp = _rms_bwd_vals(dn_ref[...].astype(F32), v_ref[...], g_ref[...])
        if has_res:
            dv = dv + r_ref[...]
        dv_ref[...] = dv.astype(out_dtype)
        dg_ref[...] += dgp

    in_specs = [_row_spec(d), _row_spec(d), _vec_spec(d)] + ([_row_spec(d)] if has_res else [])
    args = [dn, v, g] + ([resid] if has_res else [])
    return pl.pallas_call(
        body, name=name, grid=(t // TM,), in_specs=in_specs,
        out_specs=[_row_spec(d), _vec_spec(d)],
        out_shape=[jax.ShapeDtypeStruct((t, d), out_dtype), jax.ShapeDtypeStruct((1, d), F32)],
        compiler_params=_cp("arbitrary"))(*args)


def _rope_trig(pos, invf):
    ang = pos * invf
    return jnp.cos(ang), jnp.sin(ang)


def _rope_tables(trig, w, lo, half):
    c, s = trig
    lane = lax.broadcasted_iota(jnp.int32, c.shape, 1)
    active = (lane >= lo) & (lane < lo + 2 * half)
    cos = jnp.concatenate([jnp.where(active, c, 1.0)] * (w // LANES), axis=1)
    sin = jnp.concatenate([jnp.where(active, s, 0.0)] * (w // LANES), axis=1)
    lanes = lax.broadcasted_iota(jnp.int32, (c.shape[0], w), 1) & (LANES - 1)
    first = (lanes >= lo) & (lanes < lo + half)
    second = (lanes >= lo + half) & (lanes < lo + 2 * half)
    return cos, sin, first, second


def _rope_apply(v, cos, sin, first, second, half, sign):
    w = v.shape[-1]
    up = pltpu.roll(v, w - half, 1)
    dn = pltpu.roll(v, half, 1)
    rot = jnp.where(first, -up, jnp.where(second, dn, 0.0))
    return v * cos + rot * (sin * sign)


def _forget_lsf(proj, b_pad, name):
    t = proj.shape[0]

    def body(ff_ref, b_ref, o_ref):
        f = ff_ref[...] + b_ref[...]
        o_ref[...] = -(jnp.maximum(-f, 0.0) + jnp.log(1.0 + jnp.exp(-jnp.abs(f))))

    return pl.pallas_call(
        body, name=name, grid=(t // TM,),
        in_specs=[pl.BlockSpec((TM, LANES), lambda i: (i, OFF_FF // LANES)), _vec_spec(LANES)],
        out_specs=_row_spec(LANES), out_shape=jax.ShapeDtypeStruct((t, LANES), F32),
        compiler_params=_cp("parallel"))(proj, b_pad)


def _split3(c):
    hi = c.astype(BF16).astype(F32)
    mid = (c - hi).astype(BF16).astype(F32)
    return hi, mid, (c - hi) - mid


def _spread_heads(x):
    low = lax.broadcasted_iota(jnp.int32, (x.shape[0], LANES), 1) < HEAD_DIM
    out = []
    for c in range(x.shape[1] // LANES):
        blk = x[:, c * LANES:(c + 1) * LANES]
        out.append(jnp.where(low, blk, 0.0))
        out.append(jnp.where(low, pltpu.roll(blk, HEAD_DIM, 1), 0.0))
    return jnp.concatenate(out, axis=1)


def _gather_heads(y):
    low = lax.broadcasted_iota(jnp.int32, (y.shape[0], LANES), 1) < HEAD_DIM
    out = []
    for c in range(y.shape[1] // (2 * LANES)):
        a = y[:, 2 * c * LANES:(2 * c + 1) * LANES]
        b = y[:, (2 * c + 1) * LANES:(2 * c + 2) * LANES]
        out.append(jnp.where(low, a, pltpu.roll(b, HEAD_DIM, 1)))
    return jnp.concatenate(out, axis=1)


def _transposed_tiles(dst, row0, blocks, width):
    for b, blk in enumerate(blocks):
        bt = blk.T.astype(BF16)
        rows = bt.shape[0]
        for w in range(TM // width):
            dst[w, row0 + b * rows:row0 + (b + 1) * rows, :] = bt[:, w * width:(w + 1) * width]


def _prep_fwd(proj, cum, pos, invf, lg_lanes, g_q, g_kv, wq_pad, wkv_pad, name):
    t = proj.shape[0]

    def body(fox_ref, ret_ref, sb_ref, cq_ref, ckv_ref, kr_ref, cum_ref, pos_ref, invf_ref, lg_ref,
             gq_ref, gkv_ref, wq_ref, wkv_ref,
             ofox_ref, oret_ref, osb_ref, omla_ref, ofoxt_ref, osbt_ref, omlat_ref, orett_ref):
        pos_v = pos_ref[...]
        foxv, retv, sbv = [[_spread_heads(ref[:, g * GROUP:(g + 1) * GROUP]) for g in range(3)]
                           for ref in (fox_ref, ret_ref, sb_ref)]
        osb_ref[:, 0:HP] = (sbv[0] * 0.125).astype(BF16)
        osb_ref[:, HP:2 * HP] = sbv[1].astype(BF16)
        osb_ref[:, 2 * HP:QKV] = sbv[2].astype(BF16)
        _transposed_tiles(osbt_ref, 0, [sbv[1], sbv[2]], TKS)
        lane = lax.broadcasted_iota(jnp.int32, (TM, LANES), 1)
        cumv = cum_ref[...]
        fq, fk = [], []
        for hb in range(HEADS):
            hi, mid, lo = _split3(cumv[:, hb:hb + 1])
            q = foxv[0][:, hb * LANES:(hb + 1) * LANES] * 0.125
            k = foxv[1][:, hb * LANES:(hb + 1) * LANES]
            ones_q = (lane >= HEAD_DIM) & (lane < HEAD_DIM + 3)
            ones_k = (lane >= HEAD_DIM + 3) & (lane < HEAD_DIM + 6)
            q = jnp.where(ones_q, 1.0, q)
            k = jnp.where(ones_k, 1.0, k)
            for n, part in enumerate((hi, mid, lo)):
                q = jnp.where(lane == HEAD_DIM + 3 + n, part, q)
                k = jnp.where(lane == HEAD_DIM + n, -part, k)
            fq.append(q)
            fk.append(k)
        fk = jnp.concatenate(fk, axis=1)
        ofox_ref[:, 0:HP] = jnp.concatenate(fq, axis=1).astype(BF16)
        ofox_ref[:, HP:2 * HP] = fk.astype(BF16)
        ofox_ref[:, 2 * HP:QKV] = foxv[2].astype(BF16)
        _transposed_tiles(ofoxt_ref, 0, [fk, foxv[2]], BQ)
        trig = _rope_trig(pos_v, invf_ref[...])
        cos, sin, first, second = _rope_tables(trig, HP, 0, HEAD_DIM // 2)
        nloc = lax.broadcasted_iota(jnp.int32, (TM, 1), 0).astype(F32)
        dec = lg_ref[...] * nloc
        rq = _rope_apply(retv[0], cos, sin, first, second, HEAD_DIM // 2, 1.0)
        rk = _rope_apply(retv[1], cos, sin, first, second, HEAD_DIM // 2, 1.0)
        oret_ref[:, 0:HP] = (rq * jnp.exp(dec)).astype(BF16)
        rk = rk * 0.125 * jnp.exp(-dec)
        oret_ref[:, HP:2 * HP] = rk.astype(BF16)
        _transposed_tiles(orett_ref, 0, [rk], BQ)
        oret_ref[:, 2 * HP:QKV] = retv[2].astype(BF16)
        cosm, sinm, firstm, secondm = _rope_tables(trig, HP, HEAD_DIM, ROPE_MLA // 2)
        cqn = _rms(cq_ref[...], gq_ref[...]).astype(BF16)
        qm = _bdot(cqn, wq_ref[...])
        omla_ref[:, 0:HP] = _rope_apply(qm, cosm, sinm, firstm, secondm, ROPE_MLA // 2, 1.0).astype(BF16)
        ckvn = _rms(ckv_ref[...], gkv_ref[...]).astype(BF16)
        kv = _bdot(ckvn, wkv_ref[...])
        krr = _rope_apply(kr_ref[...], cosm[:, 0:LANES], sinm[:, 0:LANES], firstm[:, 0:LANES],
                          secondm[:, 0:LANES], ROPE_MLA // 2, 1.0)
        mk = kv[:, 0:HP] + jnp.concatenate([krr] * HEADS, axis=1)
        omla_ref[:, HP:2 * HP] = mk.astype(BF16)
        omla_ref[:, 2 * HP:QKV] = kv[:, HP:2 * HP].astype(BF16)
        _transposed_tiles(omlat_ref, 0, [mk, kv[:, HP:2 * HP]], BQ)

    def seg(off, w):
        return pl.BlockSpec((TM, w), lambda i, o=off // w: (i, o))

    def full(shape):
        return pl.BlockSpec(shape, lambda i: (0,) * len(shape))

    def tiles(width):
        return pl.BlockSpec((TM // width, 2 * HP, width), lambda i: (i, 0, 0))

    in_specs = [seg(OFF_FOX, QKV_IN), seg(OFF_RET, QKV_IN), seg(OFF_SB, QKV_IN), seg(OFF_CQ, Q_RANK),
                seg(OFF_CKV, LANES), seg(OFF_KR, LANES), _row_spec(LANES), pl.BlockSpec((TM, 1), lambda i: (i, 0)),
                full((1, LANES)), full((1, HP)), full((1, Q_RANK)), full((1, KV_RANK)),
                full((Q_RANK, HP)), full((KV_RANK, 2 * HP))]
    out_specs = [_row_spec(QKV)] * 4 + [tiles(BQ), tiles(TKS), tiles(BQ),
                                        pl.BlockSpec((1, HP, BQ), lambda i: (i, 0, 0))]
    out_shape = [jax.ShapeDtypeStruct((t, QKV), BF16)] * 4 + [
        jax.ShapeDtypeStruct((t // BQ, 2 * HP, BQ), BF16), jax.ShapeDtypeStruct((t // TKS, 2 * HP, TKS), BF16),
        jax.ShapeDtypeStruct((t // BQ, 2 * HP, BQ), BF16), jax.ShapeDtypeStruct((t // BQ, HP, BQ), BF16)]
    return pl.pallas_call(
        body, name=name, grid=(t // TM,), in_specs=in_specs, out_specs=out_specs, out_shape=out_shape,
        compiler_params=_cp("parallel"))(proj, proj, proj, proj, proj, proj, cum, pos, invf, lg_lanes,
                                         g_q, g_kv, wq_pad, wkv_pad)


def _prep_bwd(dfox, dret, dsb, dmla, drg, dlsf, proj, pos, invf, lg_lanes, b_pad, g_q, g_kv,
              wq_pad, wkv_pad, name):
    t = proj.shape[0]

    def body(dfq, dfk, dfv, drq, drk, drv, dsq, dsk, dsv, dmq, dmk, dmv, drg_ref, dlsf_ref,
             cq_ref, ckv_ref, ff_ref, pos_ref, invf_ref, lg_ref, b_ref, gq_ref, gkv_ref, wq_ref, wkv_ref,
             dp_ref, dwq_ref, dwkv_ref, dgq_ref, dgkv_ref, dbf_ref):
        @pl.when(pl.program_id(0) == 0)
        def _():
            dwq_ref[...] = jnp.zeros_like(dwq_ref)
            dwkv_ref[...] = jnp.zeros_like(dwkv_ref)
            dgq_ref[...] = jnp.zeros_like(dgq_ref)
            dgkv_ref[...] = jnp.zeros_like(dgkv_ref)
            dbf_ref[...] = jnp.zeros_like(dbf_ref)

        pos_v = pos_ref[...]
        def put(off, val):
            dp_ref[:, off:off + GROUP] = _gather_heads(val).astype(BF16)

        for off, (dq, dk, dv) in ((OFF_FOX, (dfq, dfk, dfv)), (OFF_SB, (dsq, dsk, dsv))):
            put(off, dq[...] * 0.125)
            put(off + GROUP, dk[...])
            put(off + 2 * GROUP, dv[...])
        trig = _rope_trig(pos_v, invf_ref[...])
        cos, sin, first, second = _rope_tables(trig, HP, 0, HEAD_DIM // 2)
        nloc = lax.broadcasted_iota(jnp.int32, (TM, 1), 0).astype(F32)
        dec = lg_ref[...] * nloc
        dq = _rope_apply(drq[...] * jnp.exp(dec), cos, sin, first, second, HEAD_DIM // 2, -1.0)
        dk = _rope_apply(drk[...] * (0.125 * jnp.exp(-dec)), cos, sin, first, second, HEAD_DIM // 2, -1.0)
        put(OFF_RET, dq)
        put(OFF_RET + GROUP, dk)
        put(OFF_RET + 2 * GROUP, drv[...])
        put(OFF_RG, drg_ref[...])
        cosm, sinm, firstm, secondm = _rope_tables(trig, HP, HEAD_DIM, ROPE_MLA // 2)
        dql = _rope_apply(dmq[...], cosm, sinm, firstm, secondm, ROPE_MLA // 2, -1.0).astype(BF16)
        cq = cq_ref[...]
        cqn = _rms(cq, gq_ref[...]).astype(BF16)
        dwq_ref[...] += _bdot(cqn, dql, TN)
        dcqn = _bdot(dql, wq_ref[...], NT)
        dcq, dgq = _rms_bwd_vals(dcqn, cq, gq_ref[...])
        dgq_ref[...] += dgq
        dp_ref[:, OFF_CQ:OFF_CQ + Q_RANK] = dcq.astype(BF16)
        dkm = dmk[...]
        dkv = jnp.concatenate([dkm, dmv[...]], axis=1).astype(BF16)
        ckv = ckv_ref[...]
        ckvn = _rms(ckv, gkv_ref[...]).astype(BF16)
        dwkv_ref[...] += _bdot(ckvn, dkv, TN)
        dckvn = _bdot(dkv, wkv_ref[...], NT)
        dckv, dgkv = _rms_bwd_vals(dckvn, ckv, gkv_ref[...])
        dgkv_ref[...] += dgkv
        dp_ref[:, OFF_CKV:OFF_CKV + LANES] = dckv.astype(BF16)
        dkr = dkm[:, 0:LANES] + dkm[:, LANES:2 * LANES] + dkm[:, 2 * LANES:3 * LANES] + dkm[:, 3 * LANES:HP]
        act = firstm[:, 0:LANES] | secondm[:, 0:LANES]
        dkr = jnp.where(act, dkr, 0.0)
        dkr = _rope_apply(dkr, cosm[:, 0:LANES], sinm[:, 0:LANES], firstm[:, 0:LANES], secondm[:, 0:LANES],
                          ROPE_MLA // 2, -1.0)
        dp_ref[:, OFF_KR:OFF_KR + LANES] = dkr.astype(BF16)
        f = ff_ref[...] + b_ref[...]
        dff = dlsf_ref[...] / (1.0 + jnp.exp(f))
        dbf_ref[...] += jnp.sum(dff, axis=0, keepdims=True)
        dp_ref[:, OFF_FF:OFF_FF + LANES] = dff.astype(BF16)
        dp_ref[:, OFF_FF + LANES:NP_IN] = jnp.zeros((TM, NP_IN - OFF_FF - LANES), BF16)

    def seg(off, w):
        return pl.BlockSpec((TM, w), lambda i, o=off // w: (i, o))

    def full(shape):
        return pl.BlockSpec(shape, lambda i: (0,) * len(shape))

    hp_spec = _row_spec(HP)
    in_specs = [hp_spec] * 13 + [_row_spec(LANES), seg(OFF_CQ, Q_RANK), seg(OFF_CKV, LANES), seg(OFF_FF, LANES),
                                 pl.BlockSpec((TM, 1), lambda i: (i, 0)),
                                 full((1, LANES)), full((1, HP)), full((1, LANES)), full((1, Q_RANK)),
                                 full((1, KV_RANK)), full((Q_RANK, HP)), full((KV_RANK, 2 * HP))]
    out_specs = [_row_spec(NP_IN), full((Q_RANK, HP)), full((KV_RANK, 2 * HP)), full((1, Q_RANK)),
                 full((1, KV_RANK)), full((1, LANES))]
    out_shape = [jax.ShapeDtypeStruct((t, NP_IN), BF16), jax.ShapeDtypeStruct((Q_RANK, HP), F32),
                 jax.ShapeDtypeStruct((KV_RANK, 2 * HP), F32), jax.ShapeDtypeStruct((1, Q_RANK), F32),
                 jax.ShapeDtypeStruct((1, KV_RANK), F32), jax.ShapeDtypeStruct((1, LANES), F32)]
    return pl.pallas_call(
        body, name=name, grid=(t // TM,), in_specs=in_specs, out_specs=out_specs, out_shape=out_shape,
        compiler_params=_cp("arbitrary"))(*dfox, *dret, *dsb, *dmla, drg, dlsf, proj, proj, proj, pos, invf,
                                          lg_lanes, b_pad, g_q, g_kv, wq_pad, wkv_pad)


def _cumsum(x, reverse, name, partials=None):
    t, w = x.shape
    n = t // TM
    xs = [x] if partials is None else [x, partials]

    def body(*refs):
        x_refs, o_ref, carry = refs[:len(xs)], refs[len(xs)], refs[len(xs) + 1]

        @pl.when(pl.program_id(0) == 0)
        def _():
            carry[...] = jnp.zeros_like(carry)

        r = lax.broadcasted_iota(jnp.int32, (TM, TM), 0)
        c = lax.broadcasted_iota(jnp.int32, (TM, TM), 1)
        tri = jnp.where((r <= c) if reverse else (r >= c), 1.0, 0.0).astype(BF16)
        v = x_refs[0][...]
        if partials is not None:
            lane = lax.broadcasted_iota(jnp.int32, (TM, LANES), 1)
            for hb in range(HEADS):
                v = v + jnp.where(lane == hb, jnp.sum(x_refs[1][:, _hs(hb)], axis=1, keepdims=True), 0.0)
        hi = v.astype(BF16)
        r1 = v - hi.astype(F32)
        mid = r1.astype(BF16)
        lo = (r1 - mid.astype(F32)).astype(BF16)
        cs = _bdot(tri, hi) + _bdot(tri, mid) + _bdot(tri, lo) + carry[...]
        o_ref[...] = cs
        carry[...] = cs[0:1, :] if reverse else cs[TM - 1:TM, :]

    imap = (lambda i: (n - 1 - i, 0)) if reverse else (lambda i: (i, 0))
    return pl.pallas_call(
        body, name=name, grid=(n,), in_specs=[pl.BlockSpec((TM, a.shape[1]), imap) for a in xs],
        out_specs=pl.BlockSpec((TM, w), imap),
        out_shape=jax.ShapeDtypeStruct((t, w), F32), scratch_shapes=[pltpu.VMEM((1, w), F32)],
        compiler_params=_cp("arbitrary"))(*xs)


HB_FWD = 4
HB_BWD = 4
HB_SB_FWD = 4
BQS = 512
BQA = 512


def _q_spec(hb, bq=BQ):
    return pl.BlockSpec((bq, hb * LANES), lambda g, i: (i, g))


ONE_BUFFER = pl.Buffered(1)


def _kv_spec(t, which, hb):
    return pl.BlockSpec((t, hb * LANES), lambda g, i, w=which: (0, w * (HEADS // hb) + g), pipeline_mode=ONE_BUFFER)


def _acc_spec(t, hb):
    return pl.BlockSpec((t, hb * LANES), lambda g, i: (0, g), pipeline_mode=ONE_BUFFER)


def _hs(hh):
    return slice(hh * LANES, (hh + 1) * LANES)


def _tile_iota(rows, cols):
    return (lax.broadcasted_iota(jnp.int32, (rows, cols), 0), lax.broadcasted_iota(jnp.int32, (rows, cols), 1))


def _kvt_spec(nkv, width, which, hb):
    return pl.BlockSpec((nkv, hb * LANES, width), lambda g, i, w=which: (0, w * (HEADS // hb) + g, 0),
                        pipeline_mode=ONE_BUFFER)


def _qrow_spec(hb, bq=BQ):
    return pl.BlockSpec((hb, 1, 1, bq), lambda g, i: (g, i, 0, 0))


def _vis(key0, query0, rows, cols, kind):
    r, c = _tile_iota(rows, cols)
    k, q = key0 + r, query0 + c
    if kind == "chunk":
        return (k >> CHUNK_SHIFT) <= (q >> CHUNK_SHIFT)
    return (k < q) if kind == "strict" else (k <= q)


def _softmax_fwd(qkv, kvt, *, chunk_mask, scale, name):
    t = qkv.shape[0]
    nq = t // BQA
    per = BQA // BQ
    hb = HB_FWD
    kind = "chunk" if chunk_mask else "causal"

    def body(q_ref, k_ref, vt_ref, o_ref, lse_ref, m_sc, l_sc, acc_sc):
        i = pl.program_id(1)
        m_sc[...] = jnp.full((hb, 1, BQA), NEG, F32)
        l_sc[...] = jnp.zeros((hb, 1, BQA), F32)
        acc_sc[...] = jnp.zeros((hb, LANES, BQA), F32)

        def tile(j, qoff):
            off = pl.multiple_of(j * BQ, BQ)
            lo = 0 if qoff is None else qoff
            qs = slice(lo, BQA)
            vis = None if qoff is None else _vis(off, i * BQA + lo, BQ, BQA - lo, kind)
            ss = [_bdot(k_ref[pl.ds(off, BQ), _hs(hh)], q_ref[qs, _hs(hh)], NT) for hh in range(hb)]
            ps, alphas = [], []
            for hh in range(hb):
                s = ss[hh]
                if scale != 1.0:
                    s = s * scale
                if vis is not None:
                    s = jnp.where(vis, s, NEG)
                m_old = m_sc[hh, :, qs]
                m_new = jnp.maximum(m_old, jnp.max(s, axis=0, keepdims=True))
                alpha = jnp.exp(m_old - m_new)
                p = jnp.exp(s - m_new)
                l_sc[hh, :, qs] = alpha * l_sc[hh, :, qs] + jnp.sum(p, axis=0, keepdims=True)
                m_sc[hh, :, qs] = m_new
                ps.append(p.astype(BF16))
                alphas.append(alpha)
            for hh in range(hb):
                acc_sc[hh, :, qs] = alphas[hh] * acc_sc[hh, :, qs] + _bdot(vt_ref[j, _hs(hh), :], ps[hh])

        def loop(j, carry):
            tile(j, None)
            return carry

        lax.fori_loop(0, per * i, loop, 0)
        for d in range(per):
            tile(per * i + d, d * BQ)
        for hh in range(hb):
            l = l_sc[hh]
            o_ref[:, _hs(hh)] = (acc_sc[hh] / l).T
            lse_ref[hh, 0] = m_sc[hh] + jnp.log(l)

    return pl.pallas_call(
        body, name=name, grid=(HEADS // hb, nq),
        in_specs=[_q_spec(hb, BQA), _kv_spec(t, 1, hb), _kvt_spec(t // BQ, BQ, 1, hb)],
        out_specs=[_q_spec(hb, BQA), _qrow_spec(hb, BQA)],
        out_shape=[jax.ShapeDtypeStruct((t, HP), F32), jax.ShapeDtypeStruct((HEADS, nq, 1, BQA), F32)],
        scratch_shapes=[pltpu.VMEM((hb, 1, BQA), F32), pltpu.VMEM((hb, 1, BQA), F32),
                        pltpu.VMEM((hb, LANES, BQA), F32)],
        compiler_params=_cp("parallel", "arbitrary"))(qkv, qkv, kvt)


def _softmax_bwd(qkv, kvt, do, lse, delta, *, bias, chunk_mask, scale, name):
    t = qkv.shape[0]
    nq = t // BQA
    per = BQA // BQ
    hb = HB_BWD
    kind = "chunk" if chunk_mask else "causal"

    def body(*refs):
        if bias:
            (q_ref, k_ref, v_ref, kt_ref, do_ref, lse_ref, dl_ref, dq_ref, dk_ref, dv_ref, dck_ref, dcq_ref,
             dq_sc, dcq_sc) = refs
            dcq_sc[...] = jnp.zeros((hb, 1, BQA), F32)
        else:
            q_ref, k_ref, v_ref, kt_ref, do_ref, lse_ref, dl_ref, dq_ref, dk_ref, dv_ref, dq_sc = refs
        i = pl.program_id(1)

        @pl.when(i == 0)
        def _():
            dk_ref[...] = jnp.zeros_like(dk_ref)
            dv_ref[...] = jnp.zeros_like(dv_ref)
            if bias:
                dck_ref[...] = jnp.zeros_like(dck_ref)

        dq_sc[...] = jnp.zeros((hb, LANES, BQA), F32)

        def tile(j, qoff):
            off = pl.multiple_of(j * BQ, BQ)
            lo = 0 if qoff is None else qoff
            qsl = slice(lo, BQA)
            vis = None if qoff is None else _vis(off, i * BQA + lo, BQ, BQA - lo, kind)
            qs = [q_ref[qsl, _hs(hh)] for hh in range(hb)]
            dobs = [do_ref[qsl, _hs(hh)].astype(BF16) for hh in range(hb)]
            ss = [_bdot(k_ref[pl.ds(off, BQ), _hs(hh)], qs[hh], NT) for hh in range(hb)]
            dps = [_bdot(v_ref[pl.ds(off, BQ), _hs(hh)], dobs[hh], NT) for hh in range(hb)]
            pbs, dsbs = [], []
            for hh in range(hb):
                s = ss[hh]
                if scale != 1.0:
                    s = s * scale
                p = jnp.exp(s - lse_ref[hh, 0, :, qsl])
                if vis is not None:
                    p = jnp.where(vis, p, 0.0)
                ds = p * (dps[hh] - dl_ref[hh, 0, :, qsl])
                if bias:
                    part = ds[:, 0:LANES]
                    for b in range(1, (BQA - lo) // LANES):
                        part = part + ds[:, b * LANES:(b + 1) * LANES]
                    dck_ref[pl.ds(off, BQ), _hs(hh)] -= part
                    dcq_sc[hh, :, qsl] += jnp.sum(ds, axis=0, keepdims=True)
                if scale != 1.0:
                    ds = ds * scale
                pbs.append(p.astype(BF16))
                dsbs.append(ds.astype(BF16))
            for hh in range(hb):
                sl = _hs(hh)
                dv_ref[pl.ds(off, BQ), sl] += _bdot(pbs[hh], dobs[hh])
                dk_ref[pl.ds(off, BQ), sl] += _bdot(dsbs[hh], qs[hh])
                dq_sc[hh, :, qsl] += _bdot(kt_ref[j, sl, :], dsbs[hh])

        def loop(j, carry):
            tile(j, None)
            return carry

        lax.fori_loop(0, per * i, loop, 0)
        for d in range(per):
            tile(per * i + d, d * BQ)
        for hh in range(hb):
            dq_ref[:, _hs(hh)] = dq_sc[hh].T
            if bias:
                dcq_ref[hh, 0] = dcq_sc[hh]

    in_specs = [_q_spec(hb, BQA), _kv_spec(t, 1, hb), _kv_spec(t, 2, hb), _kvt_spec(t // BQ, BQ, 0, hb),
                _q_spec(hb, BQA), _qrow_spec(hb, BQA), _qrow_spec(hb, BQA)]
    out_specs = [_q_spec(hb, BQA), _acc_spec(t, hb), _acc_spec(t, hb)]
    out_shape = [jax.ShapeDtypeStruct((t, HP), F32)] * 3
    scratch = [pltpu.VMEM((hb, LANES, BQA), F32)]
    if bias:
        out_specs += [_acc_spec(t, hb), _qrow_spec(hb, BQA)]
        out_shape += [jax.ShapeDtypeStruct((t, HP), F32), jax.ShapeDtypeStruct((HEADS, nq, 1, BQA), F32)]
        scratch.append(pltpu.VMEM((hb, 1, BQA), F32))
    return pl.pallas_call(
        body, name=name, grid=(HEADS // hb, nq), in_specs=in_specs, out_specs=out_specs, out_shape=out_shape,
        scratch_shapes=scratch,
        compiler_params=_cp("parallel", "arbitrary"))(qkv, qkv, qkv, kvt, do, lse, delta)


def _ret_diag_decay(lg1, keys_on_rows=False):
    r, c = _tile_iota(BQ, BQ)
    qn, km = (c, r) if keys_on_rows else (r, c)
    dd = jnp.where(km > qn, jnp.exp((2.0 * lg1) * (km - qn).astype(F32)), 1.0)
    return jnp.where((km >> CHUNK_SHIFT) <= (qn >> CHUNK_SHIFT), dd, 0.0)


def _lg_spec(hb):
    return pl.BlockSpec((hb, 1, LANES), lambda g, i: (g, 0, 0))


def _ret_specs(nq, hb, reverse):
    tile = (lambda i: nq - 1 - i) if reverse else (lambda i: i)
    qkv = [pl.BlockSpec((BQ, hb * LANES), lambda g, i, w=w: (tile(i), w * (HEADS // hb) + g)) for w in range(3)]
    kt = pl.BlockSpec((1, hb * LANES, BQ), lambda g, i: (tile(i), g, 0))
    st = pl.BlockSpec((1, hb * LANES, LANES), lambda g, i: (tile(i), g, 0))
    return qkv, kt, st


def _ret_fwd(qkv, kt, lg_heads, name):
    t = qkv.shape[0]
    nq = t // BQ
    hb = HB_FWD

    def body(lg_ref, q_ref, k_ref, v_ref, kt_ref, o_ref, st_ref, s_sc):
        @pl.when(pl.program_id(1) == 0)
        def _():
            s_sc[...] = jnp.zeros_like(s_sc)

        qs = [q_ref[:, _hs(hh)] for hh in range(hb)]
        vs = [v_ref[:, _hs(hh)] for hh in range(hb)]
        aa = [_bdot(qs[hh], k_ref[:, _hs(hh)], NT) for hh in range(hb)]
        kv = [_bdot(kt_ref[0, _hs(hh), :], vs[hh]) for hh in range(hb)]
        for hh in range(hb):
            sl = _hs(hh)
            lg1 = lg_ref[hh][:, 0:1]
            s = s_sc[hh]
            st_ref[0, sl, :] = s
            shi, slo = _split2(s)
            a = (aa[hh] * _ret_diag_decay(lg1)).astype(BF16)
            o_ref[:, sl] = _bdot(a, vs[hh]) + _bdot(qs[hh], shi) + _bdot(qs[hh], slo)
            s_sc[hh] = jnp.exp(lg1 * float(BQ)) * (s + kv[hh])

    qkv_specs, kt_spec, st_spec = _ret_specs(nq, hb, False)
    return pl.pallas_call(
        body, name=name, grid=(HEADS // hb, nq), in_specs=[_lg_spec(hb)] + qkv_specs + [kt_spec],
        out_specs=[_q_spec(hb), st_spec],
        out_shape=[jax.ShapeDtypeStruct((t, HP), F32), jax.ShapeDtypeStruct((nq, HP, LANES), F32)],
        scratch_shapes=[pltpu.VMEM((hb, LANES, LANES), F32)],
        compiler_params=_cp("parallel", "arbitrary"))(lg_heads, qkv, qkv, qkv, kt)


def _ret_bwd(qkv, states, lg_heads, do, name):
    t = qkv.shape[0]
    nq = t // BQ
    hb = HB_FWD

    def body(lg_ref, q_ref, k_ref, v_ref, st_ref, do_ref, dq_ref, dk_ref, dv_ref, g_sc):
        @pl.when(pl.program_id(1) == 0)
        def _():
            g_sc[...] = jnp.zeros_like(g_sc)

        qs = [q_ref[:, _hs(hh)] for hh in range(hb)]
        ks = [k_ref[:, _hs(hh)] for hh in range(hb)]
        vs = [v_ref[:, _hs(hh)] for hh in range(hb)]
        dobs = [do_ref[:, _hs(hh)].astype(BF16) for hh in range(hb)]
        aa = [_bdot(ks[hh], qs[hh], NT) for hh in range(hb)]
        das = [_bdot(vs[hh], dobs[hh], NT) for hh in range(hb)]
        qdo = [_bdot(qs[hh], dobs[hh], TN) for hh in range(hb)]
        for hh in range(hb):
            sl = _hs(hh)
            lg1 = lg_ref[hh][:, 0:1]
            dd = _ret_diag_decay(lg1, keys_on_rows=True)
            at = (aa[hh] * dd).astype(BF16)
            dat = (das[hh] * dd).astype(BF16)
            h = jnp.exp(lg1 * float(BQ)) * g_sc[hh]
            hhi, hlo = _split2(h)
            shi, slo = _split2(st_ref[0, sl, :])
            dv_ref[:, sl] = _bdot(at, dobs[hh]) + _bdot(ks[hh], hhi) + _bdot(ks[hh], hlo)
            dk_ref[:, sl] = _bdot(dat, qs[hh]) + _bdot(vs[hh], hhi, NT) + _bdot(vs[hh], hlo, NT)
            dq_ref[:, sl] = _bdot(dat, ks[hh], TN) + _bdot(dobs[hh], shi, NT) + _bdot(dobs[hh], slo, NT)
            g_sc[hh] = qdo[hh] + h

    qkv_specs, _, st_spec = _ret_specs(nq, hb, True)
    tile_spec = pl.BlockSpec((BQ, hb * LANES), lambda g, i: (nq - 1 - i, g))
    return pl.pallas_call(
        body, name=name, grid=(HEADS // hb, nq), in_specs=[_lg_spec(hb)] + qkv_specs + [st_spec, tile_spec],
        out_specs=[tile_spec] * 3, out_shape=[jax.ShapeDtypeStruct((t, HP), F32)] * 3,
        scratch_shapes=[pltpu.VMEM((hb, LANES, LANES), F32)],
        compiler_params=_cp("parallel", "arbitrary"))(lg_heads, qkv, qkv, qkv, states, do)


def _sb_tile_logs(q, kb, vis):
    z = _bdot(kb, q, NT)
    ls = -(jnp.maximum(z, 0.0) + jnp.log(1.0 + jnp.exp(-jnp.abs(z))))
    if vis is not None:
        ls = jnp.where(vis, ls, 0.0)
    return z, ls


def _sb_later(ls, after):
    hi, lo = _split2(ls)
    return _bdot(after, hi) + _bdot(after, lo)


def _sb_fwd(qkv, kvt, name):
    t = qkv.shape[0]
    nq = t // BQS
    per = BQS // TKS
    hb = HB_SB_FWD

    def body(q_ref, k_ref, vt_ref, o_ref, tot_ref, acc_sc, r_sc):
        i = pl.program_id(1)
        acc_sc[...] = jnp.zeros((hb, LANES, BQS), F32)
        r_sc[...] = jnp.zeros((hb, 1, BQS), F32)
        mr, mc = _tile_iota(TKS, TKS)
        after = jnp.where(mc > mr, 1.0, 0.0).astype(BF16)

        def tile(j, qoff):
            off = pl.multiple_of(j * TKS, TKS)
            lo = 0 if qoff is None else qoff
            qsl = slice(lo, BQS)
            vis = None if qoff is None else _vis(off, i * BQS + lo, TKS, BQS - lo, "strict")
            zl = [_sb_tile_logs(q_ref[qsl, _hs(hh)], k_ref[pl.ds(off, TKS), _hs(hh)], vis) for hh in range(hb)]
            laters = [_sb_later(zl[hh][1], after) for hh in range(hb)]
            ws = []
            for hh in range(hb):
                z, ls = zl[hh]
                w = jnp.exp(z + ls + laters[hh] + r_sc[hh, :, qsl])
                if vis is not None:
                    w = jnp.where(vis, w, 0.0)
                ws.append(w.astype(BF16))
                r_sc[hh, :, qsl] += jnp.sum(ls, axis=0, keepdims=True)
            for hh in range(hb):
                acc_sc[hh, :, qsl] += _bdot(vt_ref[j, _hs(hh), :], ws[hh])

        for d in reversed(range(per)):
            tile(per * i + d, d * TKS)

        def loop(jj, carry):
            tile(per * i - 1 - jj, None)
            return carry

        lax.fori_loop(0, per * i, loop, 0)
        for hh in range(hb):
            o_ref[:, _hs(hh)] = acc_sc[hh].T
            tot_ref[hh, 0] = r_sc[hh]

    return pl.pallas_call(
        body, name=name, grid=(HEADS // hb, nq),
        in_specs=[_q_spec(hb, BQS), _kv_spec(t, 1, hb), _kvt_spec(t // TKS, TKS, 1, hb)],
        out_specs=[_q_spec(hb, BQS), _qrow_spec(hb, BQS)],
        out_shape=[jax.ShapeDtypeStruct((t, HP), F32), jax.ShapeDtypeStruct((HEADS, nq, 1, BQS), F32)],
        scratch_shapes=[pltpu.VMEM((hb, LANES, BQS), F32), pltpu.VMEM((hb, 1, BQS), F32)],
        compiler_params=_cp("parallel", "arbitrary"))(qkv, qkv, kvt)


def _sb_bwd(qkv, kvt, do, tot, name):
    t = qkv.shape[0]
    nq = t // BQS
    per = BQS // TKS
    hb = HB_BWD

    def body(q_ref, k_ref, v_ref, kt_ref, do_ref, tot_ref, dq_ref, dk_ref, dv_ref, dq_sc, p_sc, g_sc):
        i = pl.program_id(1)

        @pl.when(i == 0)
        def _():
            dk_ref[...] = jnp.zeros_like(dk_ref)
            dv_ref[...] = jnp.zeros_like(dv_ref)

        dq_sc[...] = jnp.zeros((hb, LANES, BQS), F32)
        p_sc[...] = jnp.zeros((hb, 1, BQS), F32)
        g_sc[...] = jnp.zeros((hb, 1, BQS), F32)
        mr, mc = _tile_iota(TKS, TKS)
        after = jnp.where(mc > mr, 1.0, 0.0).astype(BF16)
        before = jnp.where(mc < mr, 1.0, 0.0).astype(BF16)

        def tile(j, qoff):
            off = pl.multiple_of(j * TKS, TKS)
            lo = 0 if qoff is None else qoff
            qsl = slice(lo, BQS)
            vis = None if qoff is None else _vis(off, i * BQS + lo, TKS, BQS - lo, "strict")
            qs = [q_ref[qsl, _hs(hh)] for hh in range(hb)]
            dobs = [do_ref[qsl, _hs(hh)].astype(BF16) for hh in range(hb)]
            zl = [_sb_tile_logs(qs[hh], k_ref[pl.ds(off, TKS), _hs(hh)], vis) for hh in range(hb)]
            dws = [_bdot(v_ref[pl.ds(off, TKS), _hs(hh)], dobs[hh], NT) for hh in range(hb)]
            laters = [_sb_later(zl[hh][1], after) for hh in range(hb)]
            ws, gs = [], []
            for hh in range(hb):
                z, ls = zl[hh]
                own = jnp.sum(ls, axis=0, keepdims=True)
                rest = tot_ref[hh, 0, :, qsl] - p_sc[hh, :, qsl] - own
                w = jnp.exp(z + ls + laters[hh] + rest)
                if vis is not None:
                    w = jnp.where(vis, w, 0.0)
                p_sc[hh, :, qsl] += own
                ws.append(w.astype(BF16))
                gs.append(dws[hh] * w)
            gins = [_bdot(before, gs[hh].astype(BF16)) for hh in range(hb)]
            dzbs = []
            for hh in range(hb):
                g = gs[hh]
                stay = jnp.exp(zl[hh][1])
                dz = g * stay - (1.0 - stay) * (gins[hh] + g_sc[hh, :, qsl])
                if vis is not None:
                    dz = jnp.where(vis, dz, 0.0)
                g_sc[hh, :, qsl] += jnp.sum(g, axis=0, keepdims=True)
                dzbs.append(dz.astype(BF16))
            for hh in range(hb):
                sl = _hs(hh)
                dv_ref[pl.ds(off, TKS), sl] += _bdot(ws[hh], dobs[hh])
                dk_ref[pl.ds(off, TKS), sl] += _bdot(dzbs[hh], qs[hh])
                dq_sc[hh, :, qsl] += _bdot(kt_ref[j, sl, :], dzbs[hh])

        def loop(j, carry):
            tile(j, None)
            return carry

        lax.fori_loop(0, per * i, loop, 0)
        for d in range(per):
            tile(per * i + d, d * TKS)
        for hh in range(hb):
            dq_ref[:, _hs(hh)] = dq_sc[hh].T

    return pl.pallas_call(
        body, name=name, grid=(HEADS // hb, nq),
        in_specs=[_q_spec(hb, BQS), _kv_spec(t, 1, hb), _kv_spec(t, 2, hb), _kvt_spec(t // TKS, TKS, 0, hb),
                  _q_spec(hb, BQS), _qrow_spec(hb, BQS)],
        out_specs=[_q_spec(hb, BQS), _acc_spec(t, hb), _acc_spec(t, hb)],
        out_shape=[jax.ShapeDtypeStruct((t, HP), F32)] * 3,
        scratch_shapes=[pltpu.VMEM((hb, LANES, BQS), F32), pltpu.VMEM((hb, 1, BQS), F32),
                        pltpu.VMEM((hb, 1, BQS), F32)],
        compiler_params=_cp("parallel", "arbitrary"))(qkv, qkv, qkv, kvt, do, tot)


def _sigmoid(v):
    return 1.0 / (1.0 + jnp.exp(-v))


def _post_fwd(oa, ob, oc, od, proj, g_pad, name):
    t = oa.shape[0]

    def body(oa_ref, ob_ref, oc_ref, od_ref, rg_ref, g_ref, mx_ref):
        g = g_ref[...]

        def group(o, gg):
            r = lax.rsqrt(jnp.sum(o * o, axis=-1, keepdims=True) * (1.0 / GROUP) + EPS)
            return _gather_heads(o * r * gg).astype(BF16)

        mx_ref[:, 0:GROUP] = group(oa_ref[...], g[:, 0:HP])
        mx_ref[:, GROUP:2 * GROUP] = group(ob_ref[...], g[:, HP:2 * HP])
        mx_ref[:, 3 * GROUP:4 * GROUP] = group(od_ref[...], g[:, 3 * HP:4 * HP])
        real = lax.broadcasted_iota(jnp.int32, (TM, LANES), 1) < HEAD_DIM
        rg = _spread_heads(rg_ref[...])
        gated = []
        for hb in range(HEADS):
            sl = slice(hb * LANES, (hb + 1) * LANES)
            o = oc_ref[:, sl]
            mu = jnp.sum(o, axis=-1, keepdims=True) * (1.0 / HEAD_DIM)
            dlt = jnp.where(real, o - mu, 0.0)
            var = jnp.sum(dlt * dlt, axis=-1, keepdims=True) * (1.0 / HEAD_DIM)
            yn = dlt * lax.rsqrt(var + EPS) * g[:, 2 * HP + hb * LANES:2 * HP + (hb + 1) * LANES]
            x = rg[:, sl]
            gated.append(yn * (x * _sigmoid(x)))
        mx_ref[:, 2 * GROUP:3 * GROUP] = _gather_heads(jnp.concatenate(gated, axis=1)).astype(BF16)

    rg_spec = pl.BlockSpec((TM, GROUP), lambda i: (i, OFF_RG // GROUP))
    return pl.pallas_call(
        body, name=name, grid=(t // TM,),
        in_specs=[_row_spec(HP)] * 4 + [rg_spec, _vec_spec(4 * HP)], out_specs=_row_spec(D_MODEL),
        out_shape=jax.ShapeDtypeStruct((t, D_MODEL), BF16), compiler_params=_cp("parallel"))(oa, ob, oc, od, proj, g_pad)


def _post_bwd(dmx, oa, ob, oc, od, proj, g_pad, name):
    t = oa.shape[0]

    def body(dm_ref, oa_ref, ob_ref, oc_ref, od_ref, rg_ref, g_ref,
             doa_ref, dob_ref, doc_ref, dod_ref, dla_ref, dlb_ref, drg_ref, dg_ref):
        @pl.when(pl.program_id(0) == 0)
        def _():
            dg_ref[...] = jnp.zeros_like(dg_ref)

        g = g_ref[...]

        def group_bwd(dm, o, gg):
            r = lax.rsqrt(jnp.sum(o * o, axis=-1, keepdims=True) * (1.0 / GROUP) + EPS)
            oh = o * r
            dgp = jnp.sum(dm * oh, axis=0, keepdims=True)
            dyh = dm * gg
            do = r * (dyh - oh * (jnp.sum(dyh * oh, axis=-1, keepdims=True) * (1.0 / GROUP)))
            return do, dgp

        def delta_bc(do, o):
            prod = do * o
            lane = lax.broadcasted_iota(jnp.int32, (TM, LANES), 1)
            out = jnp.zeros((TM, LANES), F32)
            for hb in range(HEADS):
                out = jnp.where(lane == hb, jnp.sum(prod[:, hb * LANES:(hb + 1) * LANES], axis=-1, keepdims=True), out)
            return out

        dmp = [_spread_heads(dm_ref[:, gi * GROUP:(gi + 1) * GROUP]) for gi in range(4)]
        rg = _spread_heads(rg_ref[...])
        oa = oa_ref[...]
        do_a, dga = group_bwd(dmp[0], oa, g[:, 0:HP])
        doa_ref[...] = do_a
        dla_ref[...] = delta_bc(do_a, oa)
        dg_ref[:, 0:HP] += dga
        ob = ob_ref[...]
        do_b, dgb = group_bwd(dmp[1], ob, g[:, HP:2 * HP])
        dob_ref[...] = do_b
        dlb_ref[...] = delta_bc(do_b, ob)
        dg_ref[:, HP:2 * HP] += dgb
        do_d, dgd = group_bwd(dmp[3], od_ref[...], g[:, 3 * HP:4 * HP])
        dod_ref[...] = do_d
        dg_ref[:, 3 * HP:4 * HP] += dgd
        real = lax.broadcasted_iota(jnp.int32, (TM, LANES), 1) < HEAD_DIM
        for hb in range(HEADS):
            sl = slice(hb * LANES, (hb + 1) * LANES)
            gsl = slice(2 * HP + hb * LANES, 2 * HP + (hb + 1) * LANES)
            o = oc_ref[:, sl]
            mu = jnp.sum(o, axis=-1, keepdims=True) * (1.0 / HEAD_DIM)
            dlt = jnp.where(real, o - mu, 0.0)
            var = jnp.sum(dlt * dlt, axis=-1, keepdims=True) * (1.0 / HEAD_DIM)
            rstd = lax.rsqrt(var + EPS)
            dhat = dlt * rstd
            gc = g[:, gsl]
            x = rg[:, sl]
            sg = _sigmoid(x)
            dm = dmp[2][:, sl]
            drg_ref[:, sl] = dm * (dhat * gc) * (sg * (1.0 + x * (1.0 - sg)))
            dyn = dm * (x * sg)
            dg_ref[:, gsl] += jnp.sum(dyn * dhat, axis=0, keepdims=True)
            ddh = dyn * gc
            m1 = jnp.sum(ddh, axis=-1, keepdims=True) * (1.0 / HEAD_DIM)
            m2 = jnp.sum(ddh * dhat, axis=-1, keepdims=True) * (1.0 / HEAD_DIM)
            doc_ref[:, sl] = jnp.where(real, rstd * (ddh - m1 - dhat * m2), 0.0)

    rg_spec = pl.BlockSpec((TM, GROUP), lambda i: (i, OFF_RG // GROUP))
    hp = _row_spec(HP)
    return pl.pallas_call(
        body, name=name, grid=(t // TM,),
        in_specs=[_row_spec(D_MODEL), hp, hp, hp, hp, rg_spec, _vec_spec(4 * HP)],
        out_specs=[hp] * 4 + [_row_spec(LANES)] * 2 + [hp, _vec_spec(4 * HP)],
        out_shape=[jax.ShapeDtypeStruct((t, HP), F32)] * 4 + [jax.ShapeDtypeStruct((t, LANES), F32)] * 2
        + [jax.ShapeDtypeStruct((t, HP), F32), jax.ShapeDtypeStruct((1, 4 * HP), F32)],
        compiler_params=_cp("arbitrary"))(dmx, oa, ob, oc, od, proj, g_pad)


def _mesh_pos():
    return lax.axis_index("x"), lax.axis_index("y"), lax.axis_index("c")


def _peer(pos, k):
    x, y, c = pos
    px = 1 - x if (k >> 2) & 1 else x
    py = 1 - y if (k >> 1) & 1 else y
    pc = 1 - c if k & 1 else c
    return (px, py, pc), 4 * px + 2 * py + pc


def _exchange(arrs, gather, name):
    n = len(arrs)

    def body(*refs):
        ins, outs = refs[:n], refs[n:2 * n]
        send_sems, recv_sems, loc_sems = refs[2 * n:]
        pos = _mesh_pos()
        me = 4 * pos[0] + 2 * pos[1] + pos[2]
        local = []
        for a in range(n):
            src = ins[a] if gather else ins[a].at[me]
            cp = pltpu.make_async_copy(src, outs[a].at[me], loc_sems.at[a])
            cp.start()
            local.append(cp)
        sends, recvs = [], []
        for k in range(1, N_DEV):
            peer, pid = _peer(pos, k)
            for a in range(n):
                s = a * (N_DEV - 1) + k - 1
                src = ins[a] if gather else ins[a].at[pid]
                cp = pltpu.make_async_remote_copy(
                    src_ref=src, dst_ref=outs[a].at[me], send_sem=send_sems.at[s], recv_sem=recv_sems.at[s],
                    device_id=peer, device_id_type=pl.DeviceIdType.MESH)
                cp.start()
                sends.append(cp)
                recvs.append(pltpu.make_async_remote_copy(
                    src_ref=src, dst_ref=outs[a].at[pid], send_sem=send_sems.at[s], recv_sem=recv_sems.at[s],
                    device_id=peer, device_id_type=pl.DeviceIdType.MESH))
        for cp in recvs:
            cp.wait_recv()
        for cp in sends:
            cp.wait_send()
        for cp in local:
            cp.wait()

    any_spec = pl.BlockSpec(memory_space=pl.ANY)
    out_shape = [jax.ShapeDtypeStruct((N_DEV,) + tuple(a.shape) if gather else tuple(a.shape), a.dtype) for a in arrs]
    return pl.pallas_call(
        body, name=name, in_specs=[any_spec] * n, out_specs=[any_spec] * n, out_shape=out_shape,
        scratch_shapes=[pltpu.SemaphoreType.DMA((n * (N_DEV - 1),)), pltpu.SemaphoreType.DMA((n * (N_DEV - 1),)),
                        pltpu.SemaphoreType.DMA((n,))],
        compiler_params=pltpu.CompilerParams(has_side_effects=True))(*arrs)


def _device_index():
    x, y, c = _mesh_pos()
    return 4 * x + 2 * y + c


def _landing(srcs, gather):
    me = _device_index()
    lands = []
    for a in srcs:
        own = a[None] if gather else lax.dynamic_slice_in_dim(a, me, 1, axis=0)
        shape = (N_DEV,) + tuple(a.shape) if gather else tuple(a.shape)
        lands.append(lax.dynamic_update_slice_in_dim(lax.empty(shape, a.dtype), own, me, axis=0))
    return lands


def _exchange_copies(ins, lands, send_sems, recv_sems, gather):
    pos = _mesh_pos()
    me = 4 * pos[0] + 2 * pos[1] + pos[2]
    sends, recvs = [], []
    for k in range(1, N_DEV):
        peer, pid = _peer(pos, k)
        for a in range(len(ins)):
            s = a * (N_DEV - 1) + k - 1
            src = ins[a] if gather else ins[a].at[pid]
            sends.append(pltpu.make_async_remote_copy(
                src_ref=src, dst_ref=lands[a].at[me], send_sem=send_sems.at[s], recv_sem=recv_sems.at[s],
                device_id=peer, device_id_type=pl.DeviceIdType.MESH))
            recvs.append(pltpu.make_async_remote_copy(
                src_ref=src, dst_ref=lands[a].at[pid], send_sem=send_sems.at[s], recv_sem=recv_sems.at[s],
                device_id=peer, device_id_type=pl.DeviceIdType.MESH))
    return sends, recvs


def _exchange_start(srcs, gather, name, after=None):
    n = len(srcs)
    lands = _landing(srcs, gather)
    nsem = n * (N_DEV - 1)
    extra = [] if after is None else [after]

    def body(*refs):
        ins, lnd = refs[:n], refs[n:2 * n]
        send_sems, recv_sems = refs[2 * n + len(extra)], refs[2 * n + len(extra) + 1]
        token = refs[-1]
        sends, _ = _exchange_copies(ins, lnd, send_sems, recv_sems, gather)
        for cp in sends:
            cp.start()
        token[...] = jnp.zeros_like(token)

    hbm = pl.BlockSpec(memory_space=pltpu.HBM)
    sem = pl.BlockSpec(memory_space=pltpu.SEMAPHORE)
    bufs = list(srcs) + lands
    out_shape = ([pltpu.SemaphoreType.DMA((nsem,)), pltpu.SemaphoreType.DMA((nsem,))]
                 + [pltpu.HBM(b.shape, b.dtype) for b in bufs] + [jax.ShapeDtypeStruct((8, LANES), F32)])
    outs = pl.pallas_call(
        body, name=name, in_specs=[hbm] * (2 * n) + [pl.BlockSpec(memory_space=pl.ANY)] * len(extra),
        out_specs=[sem, sem] + [hbm] * (2 * n) + [pl.BlockSpec(memory_space=pltpu.VMEM)], out_shape=out_shape,
        input_output_aliases={i: 2 + i for i in range(2 * n)},
        compiler_params=pltpu.CompilerParams(has_side_effects=pltpu.SideEffectType.DATAFLOW_SIDE_EFFECTING),
    )(*[pltpu.with_memory_space_constraint(b, pltpu.HBM) for b in bufs], *extra)
    return (outs[0], outs[1], outs[2:2 + n], outs[2 + n:2 + 2 * n]), outs[-1]


def _exchange_wait(state, after, gather, name):
    send_sems, recv_sems, srcs, lands = state
    n = len(srcs)
    after = list(after) if isinstance(after, (list, tuple)) else [after]

    def body(*refs):
        ins, lnd = refs[:n], refs[n:2 * n]
        s_sems, r_sems = refs[2 * n], refs[2 * n + 1]
        sends, recvs = _exchange_copies(ins, lnd, s_sems, r_sems, gather)
        for cp in sends:
            cp.wait_send()
        for cp in recvs:
            cp.wait_recv()

    hbm = pl.BlockSpec(memory_space=pltpu.HBM)
    sem = pl.BlockSpec(memory_space=pltpu.SEMAPHORE)
    bufs = list(srcs) + list(lands)
    outs = pl.pallas_call(
        body, name=name, in_specs=[hbm] * (2 * n) + [sem, sem] + [pl.BlockSpec(memory_space=pl.ANY)] * len(after),
        out_specs=[hbm] * (2 * n), out_shape=[pltpu.HBM(b.shape, b.dtype) for b in bufs],
        input_output_aliases={i: i for i in range(2 * n)},
        compiler_params=pltpu.CompilerParams(has_side_effects=pltpu.SideEffectType.DATAFLOW_SIDE_EFFECTING),
    )(*bufs, send_sems, recv_sems, *after)
    return outs[n:]


def _adam_vals(w, g, m, v):
    m = ADAM_B1 * m + (1.0 - ADAM_B1) * g
    v = ADAM_B2 * v + (1.0 - ADAM_B2) * (g * g)
    m_hat = m / ADAM_C1
    v_hat = v / ADAM_C2
    delta = -ADAM_LR * (m_hat / (jnp.sqrt(v_hat) + ADAM_EPS) + ADAM_WD * w)
    return delta, m, v


def _small_allreduce_adam(part, w, m, v, name):
    rows = part.shape[0]

    def body(p_ref, w_ref, m_ref, v_ref, g_ref, d_ref, nm_ref, nv_ref, gath, send_sems, recv_sems):
        pos = _mesh_pos()
        me = 4 * pos[0] + 2 * pos[1] + pos[2]
        gath[me] = p_ref[...]
        sends, recvs = [], []
        for k in range(1, N_DEV):
            peer, pid = _peer(pos, k)
            cp = pltpu.make_async_remote_copy(
                src_ref=p_ref, dst_ref=gath.at[me], send_sem=send_sems.at[k - 1], recv_sem=recv_sems.at[k - 1],
                device_id=peer, device_id_type=pl.DeviceIdType.MESH)
            cp.start()
            sends.append(cp)
            recvs.append(pltpu.make_async_remote_copy(
                src_ref=p_ref, dst_ref=gath.at[pid], send_sem=send_sems.at[k - 1], recv_sem=recv_sems.at[k - 1],
                device_id=peer, device_id_type=pl.DeviceIdType.MESH))
        for cp in recvs:
            cp.wait_recv()
        for cp in sends:
            cp.wait_send()
        g = gath[0]
        for p in range(1, N_DEV):
            g = g + gath[p]
        g_ref[...] = g
        d, nm, nv = _adam_vals(w_ref[...], g, m_ref[...], v_ref[...])
        d_ref[...] = d
        nm_ref[...] = nm
        nv_ref[...] = nv

    vm = pl.BlockSpec(memory_space=pltpu.VMEM)
    sds = jax.ShapeDtypeStruct((rows, LANES), F32)
    return pl.pallas_call(
        body, name=name, in_specs=[vm] * 4, out_specs=[vm] * 4, out_shape=[sds] * 4,
        scratch_shapes=[pltpu.VMEM((N_DEV, rows, LANES), F32), pltpu.SemaphoreType.DMA((N_DEV - 1,)),
                        pltpu.SemaphoreType.DMA((N_DEV - 1,))],
        compiler_params=pltpu.CompilerParams(has_side_effects=True))(part, w, m, v)


def _reduce_adam(recv, w, m, v, name):
    shape = w.shape
    c = shape[-1]
    r = int(np.prod(shape[:-1]))
    recv2, w2, m2, v2 = recv.reshape(N_DEV, r, c), w.reshape(r, c), m.reshape(r, c), v.reshape(r, c)
    tr = r
    while tr * c * 4 > (1 << 20) and tr % 16 == 0:
        tr //= 2

    def body(r_ref, w_ref, m_ref, v_ref, g_ref, d_ref, nm_ref, nv_ref):
        g = r_ref[0].astype(F32)
        for p in range(1, N_DEV):
            g = g + r_ref[p].astype(F32)
        g_ref[...] = g
        d, nm, nv = _adam_vals(w_ref[...], g, m_ref[...], v_ref[...])
        d_ref[...] = d
        nm_ref[...] = nm
        nv_ref[...] = nv

    spec = pl.BlockSpec((tr, c), lambda i: (i, 0))
    sds = jax.ShapeDtypeStruct((r, c), F32)
    outs = pl.pallas_call(
        body, name=name, grid=(r // tr,),
        in_specs=[pl.BlockSpec((N_DEV, tr, c), lambda i: (0, i, 0)), spec, spec, spec],
        out_specs=[spec] * 4, out_shape=[sds] * 4, compiler_params=_cp("parallel"))(recv2, w2, m2, v2)
    return [o.reshape(shape) for o in outs]


def _pad_heads(w, real=HEAD_DIM):
    lead = w.shape[:-1]
    w = w.reshape(lead + (HEADS, real))
    w = jnp.pad(w, [(0, 0)] * len(lead) + [(0, 0), (0, LANES - real)])
    return w.reshape(lead + (HP,))


def _unpad_heads(w, real=HEAD_DIM):
    lead = w.shape[:-1]
    return w.reshape(lead + (HEADS, LANES))[..., :real].reshape(lead + (HEADS * real,))


_IN_SEGS = (("fq", 0, 256), ("fk", 256, 512), ("fv", 512, 768), ("ff", 768, 772), ("cq", 772, 1028),
            ("ckv", 1028, 1156), ("kr", 1156, 1188), ("rq", 1188, 1444), ("rk", 1444, 1700), ("rv", 1700, 1956),
            ("rg", 1956, 2212), ("sq", 2212, 2468), ("sk", 2468, 2724), ("sv", 2724, 2980))


def _pad_w_in(w):
    s = {n: w[:, a:b] for n, a, b in _IN_SEGS}
    rows = w.shape[0]
    z = lambda n: jnp.zeros((rows, n), w.dtype)
    parts = [s[n] for n in ("fq", "fk", "fv", "rq", "rk", "rv", "sq", "sk", "sv", "rg", "cq", "ckv")]
    parts += [z(HEAD_DIM), s["kr"], z(LANES - HEAD_DIM - ROPE_MLA), s["ff"], z(LANES - HEADS),
              z(NP_IN - OFF_FF - LANES)]
    return jnp.concatenate(parts, axis=1)


def _unpad_w_in(wp):
    seg = lambda off, n=GROUP: wp[:, off:off + n]
    parts = [seg(OFF_FOX), seg(OFF_FOX + GROUP), seg(OFF_FOX + 2 * GROUP), seg(OFF_FF, HEADS),
             seg(OFF_CQ, Q_RANK), seg(OFF_CKV, KV_RANK), seg(OFF_KR + HEAD_DIM, ROPE_MLA),
             seg(OFF_RET), seg(OFF_RET + GROUP), seg(OFF_RET + 2 * GROUP), seg(OFF_RG),
             seg(OFF_SB), seg(OFF_SB + GROUP), seg(OFF_SB + 2 * GROUP)]
    return jnp.concatenate(parts, axis=1)


def _pad_w_kv(w):
    w4 = w.reshape(KV_RANK, HEADS, 2 * HEAD_DIM)
    k = w4[:, :, :HEAD_DIM].reshape(KV_RANK, GROUP)
    v = w4[:, :, HEAD_DIM:].reshape(KV_RANK, GROUP)
    return jnp.concatenate([_pad_heads(k), _pad_heads(v)], axis=1)


def _unpad_w_kv(wp):
    k = _unpad_heads(wp[:, :HP]).reshape(KV_RANK, HEADS, HEAD_DIM)
    v = _unpad_heads(wp[:, HP:]).reshape(KV_RANK, HEADS, HEAD_DIM)
    return jnp.concatenate([k, v], axis=-1).reshape(KV_RANK, HEADS * 2 * HEAD_DIM)


def _pad_gain_out(g):
    g = jnp.pad(g.reshape(4 * HEADS, HEAD_DIM), ((0, 0), (0, LANES - HEAD_DIM)))
    return g.reshape(1, 4 * HP)


def _unpad_gain_out(gp):
    return gp.reshape(4 * HEADS, LANES)[:, :HEAD_DIM].reshape(D_MODEL)


_SMALL = (("g_mix_pre", 1024), ("g_mix_post", 1024), ("g_ffn_pre", 1024), ("g_ffn_post", 1024), ("g_mix_out", 1024),
          ("g_q_lora", 256), ("g_kv_lora", 128), ("b_forget", 4))


def _pack_small(vals):
    parts = []
    for name, n in _SMALL:
        a = vals[name].astype(F32)
        if n < LANES:
            a = jnp.pad(a, ((0, 0), (0, LANES - n)))
        parts.append(a)
    return jnp.concatenate(parts, axis=1).reshape(DEPTH * SMALL_ROWS, LANES)


def _unpack_small(packed):
    flat = packed.reshape(DEPTH, SMALL_ROWS * LANES)
    out, off = {}, 0
    for name, n in _SMALL:
        out[name] = flat[:, off:off + n]
        off += max(n, LANES)
    return out


def kernel(x, positions, g_mix_pre, w_in, b_forget, g_q_lora, w_q_up, g_kv_lora, w_kv_up, g_mix_out, w_out, g_mix_post, g_ffn_pre, w_ffn_up, w_ffn_down, g_ffn_post, loss_target, m_g_mix_pre, m_w_in, m_b_forget, m_g_q_lora, m_w_q_up, m_g_kv_lora, m_w_kv_up, m_g_mix_out, m_w_out, m_g_mix_post, m_g_ffn_pre, m_w_ffn_up, m_w_ffn_down, m_g_ffn_post, v_g_mix_pre, v_w_in, v_b_forget, v_g_q_lora, v_w_q_up, v_g_kv_lora, v_w_kv_up, v_g_mix_out, v_w_out, v_g_mix_post, v_g_ffn_pre, v_w_ffn_up, v_w_ffn_down, v_g_ffn_post):
    t = x.shape[1]
    nq = t // BQ
    x0 = x[0]
    tgt = loss_target[0]
    pos = positions[0].astype(F32).reshape(t, 1)

    half_r, half_m = HEAD_DIM // 2, ROPE_MLA // 2
    invf_r = ROPE_BASE ** (-jnp.arange(half_r, dtype=F32) / half_r)
    invf_m = ROPE_BASE ** (-jnp.arange(half_m, dtype=F32) / half_m)
    invf = jnp.concatenate([invf_r, invf_r, invf_m, invf_m,
                            jnp.zeros((LANES - HEAD_DIM - ROPE_MLA,), F32)]).reshape(1, LANES)
    log_gamma = jnp.log1p(-jnp.power(2.0, -5.0 - jnp.arange(HEADS, dtype=F32)))
    lg_lanes = jnp.repeat(log_gamma, LANES).reshape(1, HP)
    lg_heads = jnp.broadcast_to(log_gamma[:, None, None], (HEADS, 1, LANES))

    big = [w_in, w_q_up, w_kv_up, w_out, w_ffn_up, w_ffn_down]
    bf = lambda w: w.astype(BF16)
    first = _exchange([bf(w_in[0]), bf(w_q_up), bf(w_kv_up)], True, "weights_gather_first")
    l0_state, l0_token = _exchange_start([bf(w[0]) for w in (w_out, w_ffn_up, w_ffn_down)], True,
                                         "weights_gather_layer0_start")
    l1_state, rest_token = _exchange_start([bf(w[1]) for w in (w_in, w_out, w_ffn_up, w_ffn_down)], True,
                                           "weights_gather_layer1_start", after=l0_token)
    wq_g = first[1].transpose(1, 2, 0, 3).reshape(DEPTH, Q_RANK, 384)
    wkv_g = first[2].transpose(1, 2, 0, 3).reshape(DEPTH, KV_RANK, 512)

    row = lambda g: g.reshape(1, -1)
    layers = []
    for l in range(DEPTH):
        layers.append(dict(
            wq=_pad_heads(wq_g[l], 96), wkv=_pad_w_kv(wkv_g[l]),
            g_pre=row(g_mix_pre[l]), g_post=row(g_mix_post[l]), g_fpre=row(g_ffn_pre[l]), g_fpost=row(g_ffn_post[l]),
            g_out=_pad_gain_out(g_mix_out[l]), g_q=row(g_q_lora[l]), g_kv=row(g_kv_lora[l]),
            b_pad=jnp.pad(b_forget[l], (0, LANES - HEADS)).reshape(1, LANES)))
    layers[0]["win"] = _pad_w_in(first[0].reshape(D_MODEL, D_IN))

    saved = []
    xin = x0
    h = _rms_fwd(xin, layers[0]["g_pre"] + rest_token[0:1, 0:1], "rms_pre_0")
    loss_row = dx = None
    for l, p in enumerate(layers):
        s = dict(x=xin, h=h)
        proj = _mm(h, p["win"], name=f"in_proj_{l}", tm=512, tn=NP_IN)
        cum = _cumsum(_forget_lsf(proj, p["b_pad"], f"forget_lsf_{l}"), False, f"forget_cumsum_{l}")
        fox, ret, sb, mla, fox_t, sb_t, mla_t, ret_t = _prep_fwd(proj, cum, pos, invf, lg_lanes, p["g_q"],
                                                          p["g_kv"], p["wq"], p["wkv"], f"prep_fwd_{l}")
        oa, lse_a = _softmax_fwd(fox, fox_t, chunk_mask=False, scale=1.0, name=f"fox_fwd_{l}")
        ob, lse_b = _softmax_fwd(mla, mla_t, chunk_mask=True, scale=96.0 ** -0.5, name=f"mla_fwd_{l}")
        oc, ret_st = _ret_fwd(ret, ret_t, lg_heads, f"ret_fwd_{l}")
        od, sb_tot = _sb_fwd(sb, sb_t, f"sb_fwd_{l}")
        if l == 0:
            got = _exchange_wait(l0_state, od, True, "weights_gather_layer0_wait")
            p.update(wout=got[0].reshape(D_MODEL, D_MODEL), wup=got[1].transpose(1, 0, 2).reshape(D_MODEL, D_FF),
                     wdn=got[2].reshape(D_FF, D_MODEL))
        mixed = _post_fwd(oa, ob, oc, od, proj, p["g_out"], f"post_fwd_{l}")
        mix = _mm(mixed, p["wout"], name=f"out_proj_{l}", tm=2048)
        x1, h2 = _add_rms_fwd(xin, mix, p["g_post"], p["g_fpre"], f"mix_residual_{l}")
        a = _mm(h2, p["wup"], name=f"ffn_up_{l}", out_dtype=BF16, tm=2048)
        y = _mm(a, p["wdn"], name=f"ffn_down_{l}", a_fn=_relu2, tm=2048)
        s.update(proj=proj, fox=fox, ret=ret, sb=sb, mla=mla, fox_t=fox_t, sb_t=sb_t, mla_t=mla_t, ret_st=ret_st, oa=oa, ob=ob, oc=oc,
                 od=od, sb_tot=sb_tot, lse_a=lse_a, lse_b=lse_b, mixed=mixed, mix=mix, x1=x1, h2=h2, a=a, y=y)
        saved.append(s)
        if l == 0:
            got = _exchange_wait(l1_state, y, True, "weights_gather_layer1_wait")
            layers[1].update(win=_pad_w_in(got[0].reshape(D_MODEL, D_IN)), wout=got[1].reshape(D_MODEL, D_MODEL),
                             wup=got[2].transpose(1, 0, 2).reshape(D_MODEL, D_FF), wdn=got[3].reshape(D_FF, D_MODEL))
        if l + 1 < DEPTH:
            xin, h = _add_rms_fwd(x1, y, p["g_fpost"], layers[l + 1]["g_pre"], f"ffn_residual_{l}")
        else:
            loss_row, dx = _final_loss(x1, y, p["g_fpost"], tgt, "loss")

    small_g = {n: [None] * DEPTH for n, _ in _SMALL}
    big_g = [[None] * DEPTH for _ in range(6)]
    to_send = [
        lambda g: g.reshape(N_DEV, 1, D_MODEL // N_DEV, D_IN),
        lambda g: g.reshape(Q_RANK, N_DEV, 384 // N_DEV).transpose(1, 0, 2)[:, None],
        lambda g: g.reshape(KV_RANK, N_DEV, 512 // N_DEV).transpose(1, 0, 2)[:, None],
        lambda g: g.reshape(N_DEV, 1, D_MODEL // N_DEV, D_MODEL),
        lambda g: g.reshape(D_MODEL, N_DEV, D_FF // N_DEV).transpose(1, 0, 2)[:, None],
        lambda g: g.reshape(N_DEV, 1, D_FF // N_DEV, D_MODEL),
    ]
    send_of = lambda ks, l: [to_send[k](big_g[k][l]).astype(BF16) for k in ks]
    late_state = early_state = None
    order_token = jnp.zeros((1, 1), F32)
    for l in reversed(range(DEPTH)):
        p, s = layers[l], saved[l]
        dy, dg = _norm_bwd(dx, s["y"], p["g_fpost"] + order_token, None, BF16, f"ffn_post_bwd_{l}")
        small_g["g_ffn_post"][l] = dg
        da = _mm(dy, p["wdn"], name=f"ffn_down_dx_{l}", tb=True, out_dtype=BF16, epi=_drelu2, epi_in=s["a"],
                 tm=2048)
        big_g[5][l] = _mm(s["a"], dy, name=f"ffn_down_dw_{l}", ta=True, a_fn=_relu2, tk=2048, out_dtype=BF16)
        big_g[4][l] = _mm(s["h2"], da, name=f"ffn_up_dw_{l}", ta=True, tk=2048, out_dtype=BF16)
        dh2 = _mm(da, p["wup"], name=f"ffn_up_dx_{l}", tb=True, tm=2048)
        dx1, dg = _norm_bwd(dh2, s["x1"], p["g_fpre"], dx, F32, f"ffn_pre_bwd_{l}")
        small_g["g_ffn_pre"][l] = dg
        dmix, dg = _norm_bwd(dx1, s["mix"], p["g_post"], None, BF16, f"mix_post_bwd_{l}")
        small_g["g_mix_post"][l] = dg
        dmixed = _mm(dmix, p["wout"], name=f"out_proj_dx_{l}", tb=True, tm=2048)
        big_g[3][l] = _mm(s["mixed"], dmix, name=f"out_proj_dw_{l}", ta=True, tk=2048, out_dtype=BF16)
        g_out = p["g_out"]
        if l == 0:
            early_state, early_token = _exchange_start(send_of((3, 4, 5), 0), False, "grads_layer0_early_start")
            g_out = g_out + early_token[0:1, 0:1]
        doa, dob, doc, dod, dla, dlb, drg, dgo = _post_bwd(dmixed, s["oa"], s["ob"], s["oc"], s["od"], s["proj"],
                                                           g_out, f"post_bwd_{l}")
        small_g["g_mix_out"][l] = _unpad_gain_out(dgo).reshape(1, D_MODEL)
        as_rows = lambda a: a[:, :HEADS].T.reshape(HEADS, t // BQA, 1, BQA)
        dfq, dfk, dfv, dcum_k, dcum_q = _softmax_bwd(s["fox"], s["fox_t"], doa, s["lse_a"], as_rows(dla), bias=True,
                                                     chunk_mask=False, scale=1.0, name=f"fox_bwd_{l}")
        dmq, dmk, dmv = _softmax_bwd(s["mla"], s["mla_t"], dob, s["lse_b"], as_rows(dlb), bias=False, chunk_mask=True,
                                     scale=96.0 ** -0.5, name=f"mla_bwd_{l}")
        drq, drk, drv = _ret_bwd(s["ret"], s["ret_st"], lg_heads, doc, f"ret_bwd_{l}")
        dsq, dsk, dsv = _sb_bwd(s["sb"], s["sb_t"], dod, s["sb_tot"], f"sb_bwd_{l}")
        dcum_q = jnp.pad(dcum_q.reshape(HEADS, t).T, ((0, 0), (0, LANES - HEADS)))
        dlsf = _cumsum(dcum_q, True, f"forget_cumsum_bwd_{l}", partials=dcum_k)
        dproj, dwq, dwkv, dgq, dgkv, dbf = _prep_bwd(
            (dfq, dfk, dfv), (drq, drk, drv), (dsq, dsk, dsv), (dmq, dmk, dmv), drg, dlsf, s["proj"], pos, invf,
            lg_lanes, p["b_pad"], p["g_q"], p["g_kv"], p["wq"], p["wkv"], f"prep_bwd_{l}")
        small_g["g_q_lora"][l] = dgq
        small_g["g_kv_lora"][l] = dgkv
        small_g["b_forget"][l] = dbf[:, :HEADS]
        big_g[1][l] = _unpad_heads(dwq, 96)
        big_g[2][l] = _unpad_w_kv(dwkv)
        dh = _mm(dproj, p["win"], name=f"in_proj_dx_{l}", tb=True, tk=NP_IN // 2, tm=2048)
        big_g[0][l] = _unpad_w_in(_mm(s["h"], dproj, name=f"in_proj_dw_{l}", ta=True, tn=NP_IN // 2, out_dtype=BF16))
        g_pre = p["g_pre"]
        if l == 0:
            last_state, last_token = _exchange_start(send_of((0, 1, 2), 0), False, "grads_layer0_rest_start")
            g_pre = g_pre + last_token[0:1, 0:1]
        dx, dg = _norm_bwd(dh, s["x"], g_pre, dx1, F32, f"mix_pre_bwd_{l}")
        small_g["g_mix_pre"][l] = dg
        if l == DEPTH - 1:
            late_state, late_token = _exchange_start(send_of(range(6), l), False, "grads_layer1_start")
            order_token = late_token[0:1, 0:1]
    grad_x = dx.reshape(1, t, D_MODEL)

    res = {}
    small_w = dict(g_mix_pre=g_mix_pre, g_mix_post=g_mix_post, g_ffn_pre=g_ffn_pre, g_ffn_post=g_ffn_post,
                   g_mix_out=g_mix_out, g_q_lora=g_q_lora, g_kv_lora=g_kv_lora, b_forget=b_forget)
    small_m = dict(g_mix_pre=m_g_mix_pre, g_mix_post=m_g_mix_post, g_ffn_pre=m_g_ffn_pre, g_ffn_post=m_g_ffn_post,
                   g_mix_out=m_g_mix_out, g_q_lora=m_g_q_lora, g_kv_lora=m_g_kv_lora, b_forget=m_b_forget)
    small_v = dict(g_mix_pre=v_g_mix_pre, g_mix_post=v_g_mix_post, g_ffn_pre=v_g_ffn_pre, g_ffn_post=v_g_ffn_post,
                   g_mix_out=v_g_mix_out, g_q_lora=v_g_q_lora, g_kv_lora=v_g_kv_lora, b_forget=v_b_forget)
    n_small = DEPTH * SMALL_ROWS
    extra = lambda a: jnp.concatenate([a, jnp.zeros((8, LANES), F32)], axis=0)
    part = jnp.concatenate([_pack_small({n: jnp.concatenate(small_g[n], axis=0) for n, _ in _SMALL}),
                            jnp.broadcast_to(loss_row, (8, LANES))], axis=0)
    sres = _small_allreduce_adam(part, extra(_pack_small(small_w)), extra(_pack_small(small_m)),
                                 extra(_pack_small(small_v)), "small_allreduce_adamw")
    loss = sres[0][n_small, 0]
    sg, sd, sm, sv = [_unpack_small(a[:n_small]) for a in sres]
    for n, _ in _SMALL:
        res[n] = [sg[n], sd[n], sm[n], sv[n]]

    late = _exchange_wait(late_state, dx, False, "grads_layer1_wait")
    early = _exchange_wait(early_state, dx, False, "grads_layer0_early_wait")
    ms = [m_w_in, m_w_q_up, m_w_kv_up, m_w_out, m_w_ffn_up, m_w_ffn_down]
    vs = [v_w_in, v_w_q_up, v_w_kv_up, v_w_out, v_w_ffn_up, v_w_ffn_down]
    names = ["w_in", "w_q_up", "w_kv_up", "w_out", "w_ffn_up", "w_ffn_down"]
    for k in (3, 4, 5):
        recv = jnp.concatenate([early[k - 3], late[k]], axis=1)
        res[names[k]] = _reduce_adam(recv, big[k], ms[k], vs[k], f"adamw_{names[k]}")
    done = [sres[0]] + [res[names[k]][0] for k in (3, 4, 5)]
    last = _exchange_wait(last_state, done, False, "grads_layer0_rest_wait")
    for k in (0, 1, 2):
        recv = jnp.concatenate([last[k], late[k]], axis=1)
        res[names[k]] = _reduce_adam(recv, big[k], ms[k], vs[k], f"adamw_{names[k]}")

    order = ["g_mix_pre", "w_in", "b_forget", "g_q_lora", "w_q_up", "g_kv_lora", "w_kv_up", "g_mix_out", "w_out",
             "g_mix_post", "g_ffn_pre", "w_ffn_up", "w_ffn_down", "g_ffn_post"]
    outs = [loss, grad_x]
    for idx in range(4):
        outs += [res[n][idx] for n in order]
    return tuple(outs)
```

```python
import functools
import math

import numpy as np
import jax
import jax.numpy as jnp
from jax import lax
from jax.experimental import pallas as pl
from jax.experimental.pallas import tpu as pltpu

F32 = jnp.float32
BF16 = jnp.bfloat16

D_MODEL = 1024
DEPTH = 2
N_DEV = 8
GROUP = 256
HEADS = 4
HEAD_DIM = 64
LANES = 128
HP = HEADS * LANES
QKV = 3 * HP
Q_RANK = 256
KV_RANK = 128
ROPE_MLA = 32
D_FF = 4096
D_IN = 2980
CHUNK_SHIFT = 6
EPS = 1e-6
ROPE_BASE = 10000.0
NEG = -1e30

QKV_IN = 3 * GROUP
OFF_FOX, OFF_RET, OFF_SB = 0, QKV_IN, 2 * QKV_IN
OFF_RG = 3 * QKV_IN
OFF_CQ = OFF_RG + GROUP
OFF_CKV = OFF_CQ + Q_RANK
OFF_KR = OFF_CKV + LANES
OFF_FF = OFF_KR + LANES
NP_IN = 3328

BQ = 256
TKS = 128
TM = 256
VMEM_LIMIT = 58 * 1024 * 1024

ADAM_LR, ADAM_B1, ADAM_B2, ADAM_EPS, ADAM_WD, ADAM_STEP = 0.001, 0.9, 0.999, 1e-08, 0.01, 10
ADAM_C1 = 1.0 - ADAM_B1 ** ADAM_STEP
ADAM_C2 = 1.0 - ADAM_B2 ** ADAM_STEP

SMALL_ROWS = 44

NT = (((1,), (1,)), ((), ()))
TN = (((0,), (0,)), ((), ()))


def _cp(*sem):
    return pltpu.CompilerParams(dimension_semantics=sem if sem else None, vmem_limit_bytes=VMEM_LIMIT)


def _bdot(a, b, dn=None):
    if dn is None:
        return jnp.dot(a, b, preferred_element_type=F32)
    return lax.dot_general(a, b, dn, preferred_element_type=F32)


def _split2(x):
    hi = x.astype(BF16)
    lo = (x - hi.astype(F32)).astype(BF16)
    return hi, lo


def _mm(a, b, *, name, ta=False, tb=False, out_dtype=F32, a_fn=None, epi=None, epi_in=None,
        tm=1024, tn=1024, tk=1024):
    m, k = (a.shape[1], a.shape[0]) if ta else a.shape
    n = b.shape[0] if tb else b.shape[1]
    tm, tn, tk = min(tm, m), min(tn, n), min(tk, k)
    assert m % tm == 0 and n % tn == 0 and k % tk == 0, (name, m, n, k)
    nk = k // tk
    dn = (((0 if ta else 1,), (1 if tb else 0,)), ((), ()))

    def body(*refs):
        if epi is None:
            a_ref, b_ref, o_ref = refs[:3]
            e_ref = None
            rest = refs[3:]
        else:
            a_ref, b_ref, e_ref, o_ref = refs[:4]
            rest = refs[4:]
        av = a_ref[...]
        if a_fn is not None:
            av = a_fn(av)
        part = lax.dot_general(av.astype(BF16), b_ref[...].astype(BF16), dn, preferred_element_type=F32)

        def finish(r):
            if epi is not None:
                r = epi(r, e_ref[...])
            o_ref[...] = r.astype(out_dtype)

        if nk == 1:
            finish(part)
        else:
            acc_ref = rest[0]
            kk = pl.program_id(2)

            @pl.when(kk == 0)
            def _():
                acc_ref[...] = part

            @pl.when(kk > 0)
            def _():
                acc_ref[...] += part

            @pl.when(kk == nk - 1)
            def _():
                finish(acc_ref[...])

    a_spec = pl.BlockSpec((tk, tm), lambda i, j, kk: (kk, i)) if ta else pl.BlockSpec((tm, tk), lambda i, j, kk: (i, kk))
    b_spec = pl.BlockSpec((tn, tk), lambda i, j, kk: (j, kk)) if tb else pl.BlockSpec((tk, tn), lambda i, j, kk: (kk, j))
    o_spec = pl.BlockSpec((tm, tn), lambda i, j, kk: (i, j))
    in_specs = [a_spec, b_spec]
    args = [a, b]
    if epi is not None:
        in_specs.append(o_spec)
        args.append(epi_in)
    return pl.pallas_call(
        body, name=name, grid=(m // tm, n // tn, nk),
        in_specs=in_specs, out_specs=o_spec,
        out_shape=jax.ShapeDtypeStruct((m, n), out_dtype),
        scratch_shapes=[pltpu.VMEM((tm, tn), F32)] if nk > 1 else [],
        compiler_params=_cp("parallel", "parallel", "arbitrary"),
    )(*args)


def _relu2(v):
    r = jnp.maximum(v, 0.0)
    return r * r


def _drelu2(du, av):
    return du * (2.0 * jnp.maximum(av, 0.0))


def _rms(v, g):
    r = lax.rsqrt(jnp.mean(v * v, axis=-1, keepdims=True) + EPS)
    return v * r * g


def _row_spec(w):
    return pl.BlockSpec((TM, w), lambda i: (i, 0))


def _vec_spec(w):
    return pl.BlockSpec((1, w), lambda i: (0, 0))


def _rms_fwd(x, g, name):
    t, d = x.shape

    def body(x_ref, g_ref, h_ref):
        h_ref[...] = _rms(x_ref[...], g_ref[...]).astype(BF16)

    return pl.pallas_call(
        body, name=name, grid=(t // TM,), in_specs=[_row_spec(d), _vec_spec(d)], out_specs=_row_spec(d),
        out_shape=jax.ShapeDtypeStruct((t, d), BF16), compiler_params=_cp("parallel"))(x, g)


def _add_rms_fwd(x, y, g1, g2, name):
    t, d = x.shape

    def body(x_ref, y_ref, g1_ref, g2_ref, xn_ref, h_ref):
        xn = x_ref[...] + _rms(y_ref[...], g1_ref[...])
        xn_ref[...] = xn
        h_ref[...] = _rms(xn, g2_ref[...]).astype(BF16)

    return pl.pallas_call(
        body, name=name, grid=(t // TM,),
        in_specs=[_row_spec(d), _row_spec(d), _vec_spec(d), _vec_spec(d)],
        out_specs=[_row_spec(d), _row_spec(d)],
        out_shape=[jax.ShapeDtypeStruct((t, d), F32), jax.ShapeDtypeStruct((t, d), BF16)],
        compiler_params=_cp("parallel"))(x, y, g1, g2)


def _final_loss(x, y, g, tgt, name):
    t, d = x.shape

    def body(x_ref, y_ref, g_ref, t_ref, l_ref, dx_ref):
        @pl.when(pl.program_id(0) == 0)
        def _():
            l_ref[...] = jnp.zeros_like(l_ref)

        err = x_ref[...] + _rms(y_ref[...], g_ref[...]) - t_ref[...]
        dx_ref[...] = err * (1.0 / d)
        l_ref[...] += jnp.sum(jnp.sum(err * err, axis=1, keepdims=True), axis=0, keepdims=True) * (0.5 / d)

    return pl.pallas_call(
        body, name=name, grid=(t // TM,),
        in_specs=[_row_spec(d), _row_spec(d), _vec_spec(d), _row_spec(d)],
        out_specs=[pl.BlockSpec((1, LANES), lambda i: (0, 0)), _row_spec(d)],
        out_shape=[jax.ShapeDtypeStruct((1, LANES), F32), jax.ShapeDtypeStruct((t, d), F32)],
        compiler_params=_cp("arbitrary"))(x, y, g, tgt)


def _rms_bwd_vals(dn, v, g):
    w = v.shape[-1]
    r = lax.rsqrt(jnp.mean(v * v, axis=-1, keepdims=True) + EPS)
    vh = v * r
    dgp = jnp.sum(dn * vh, axis=0, keepdims=True)
    dvh = dn * g
    dv = r * (dvh - vh * (jnp.sum(dvh * vh, axis=-1, keepdims=True) * (1.0 / w)))
    return dv, dgp


def _norm_bwd(dn, v, g, resid, out_dtype, name):
    t, d = v.shape
    has_res = resid is not None

    def body(*refs):
        if has_res:
            dn_ref, v_ref, g_ref, r_ref, dv_ref, dg_ref = refs
        else:
            dn_ref, v_ref, g_ref, dv_ref, dg_ref = refs

        @pl.when(pl.program_id(0) == 0)
        def _():
            dg_ref[...] = jnp.zeros_like(dg_ref)

        dv, dgp = _rms_bwd_vals(dn_ref[...].astype(F32), v_ref[...], g_ref[...])
        if has_res:
            dv = dv + r_ref[...]
        dv_ref[...] = dv.astype(out_dtype)
        dg_ref[...] += dgp

    in_specs = [_row_spec(d), _row_spec(d), _vec_spec(d)] + ([_row_spec(d)] if has_res else [])
    args = [dn, v, g] + ([resid] if has_res else [])
    return pl.pallas_call(
        body, name=name, grid=(t // TM,), in_specs=in_specs,
        out_specs=[_row_spec(d), _vec_spec(d)],
        out_shape=[jax.ShapeDtypeStruct((t, d), out_dtype), jax.ShapeDtypeStruct((1, d), F32)],
        compiler_params=_cp("arbitrary"))(*args)


def _rope_trig(pos, invf):
    ang = pos * invf
    return jnp.cos(ang), jnp.sin(ang)


def _rope_tables(trig, w, lo, half):
    c, s = trig
    lane = lax.broadcasted_iota(jnp.int32, c.shape, 1)
    active = (lane >= lo) & (lane < lo + 2 * half)
    cos = jnp.concatenate([jnp.where(active, c, 1.0)] * (w // LANES), axis=1)
    sin = jnp.concatenate([jnp.where(active, s, 0.0)] * (w // LANES), axis=1)
    lanes = lax.broadcasted_iota(jnp.int32, (c.shape[0], w), 1) & (LANES - 1)
    first = (lanes >= lo) & (lanes < lo + half)
    second = (lanes >= lo + half) & (lanes < lo + 2 * half)
    return cos, sin, first, second


def _rope_apply(v, cos, sin, first, second, half, sign):
    w = v.shape[-1]
    up = pltpu.roll(v, w - half, 1)
    dn = pltpu.roll(v, half, 1)
    rot = jnp.where(first, -up, jnp.where(second, dn, 0.0))
    return v * cos + rot * (sin * sign)


def _forget_lsf(proj, b_pad, name):
    t = proj.shape[0]

    def body(ff_ref, b_ref, o_ref):
        f = ff_ref[...] + b_ref[...]
        o_ref[...] = -(jnp.maximum(-f, 0.0) + jnp.log(1.0 + jnp.exp(-jnp.abs(f))))

    return pl.pallas_call(
        body, name=name, grid=(t // TM,),
        in_specs=[pl.BlockSpec((TM, LANES), lambda i: (i, OFF_FF // LANES)), _vec_spec(LANES)],
        out_specs=_row_spec(LANES), out_shape=jax.ShapeDtypeStruct((t, LANES), F32),
        compiler_params=_cp("parallel"))(proj, b_pad)


def _split3(c):
    hi = c.astype(BF16).astype(F32)
    mid = (c - hi).astype(BF16).astype(F32)
    return hi, mid, (c - hi) - mid


def _spread_heads(x):
    low = lax.broadcasted_iota(jnp.int32, (x.shape[0], LANES), 1) < HEAD_DIM
    out = []
    for c in range(x.shape[1] // LANES):
        blk = x[:, c * LANES:(c + 1) * LANES]
        out.append(jnp.where(low, blk, 0.0))
        out.append(jnp.where(low, pltpu.roll(blk, HEAD_DIM, 1), 0.0))
    return jnp.concatenate(out, axis=1)


def _gather_heads(y):
    low = lax.broadcasted_iota(jnp.int32, (y.shape[0], LANES), 1) < HEAD_DIM
    out = []
    for c in range(y.shape[1] // (2 * LANES)):
        a = y[:, 2 * c * LANES:(2 * c + 1) * LANES]
        b = y[:, (2 * c + 1) * LANES:(2 * c + 2) * LANES]
        out.append(jnp.where(low, a, pltpu.roll(b, HEAD_DIM, 1)))
    return jnp.concatenate(out, axis=1)


def _transposed_tiles(dst, row0, blocks, width):
    for b, blk in enumerate(blocks):
        bt = blk.T.astype(BF16)
        rows = bt.shape[0]
        for w in range(TM // width):
            dst[w, row0 + b * rows:row0 + (b + 1) * rows, :] = bt[:, w * width:(w + 1) * width]


def _prep_fwd(proj, cum, pos, invf, lg_lanes, g_q, g_kv, wq_pad, wkv_pad, name):
    t = proj.shape[0]

    def body(fox_ref, ret_ref, sb_ref, cq_ref, ckv_ref, kr_ref, cum_ref, pos_ref, invf_ref, lg_ref,
             gq_ref, gkv_ref, wq_ref, wkv_ref,
             ofox_ref, oret_ref, osb_ref, omla_ref, ofoxt_ref, osbt_ref, omlat_ref, orett_ref):
        pos_v = pos_ref[...]
        foxv, retv, sbv = [[_spread_heads(ref[:, g * GROUP:(g + 1) * GROUP]) for g in range(3)]
                           for ref in (fox_ref, ret_ref, sb_ref)]
        osb_ref[:, 0:HP] = (sbv[0] * 0.125).astype(BF16)
        osb_ref[:, HP:2 * HP] = sbv[1].astype(BF16)
        osb_ref[:, 2 * HP:QKV] = sbv[2].astype(BF16)
        _transposed_tiles(osbt_ref, 0, [sbv[1], sbv[2]], TKS)
        lane = lax.broadcasted_iota(jnp.int32, (TM, LANES), 1)
        cumv = cum_ref[...]
        fq, fk = [], []
        for hb in range(HEADS):
            hi, mid, lo = _split3(cumv[:, hb:hb + 1])
            q = foxv[0][:, hb * LANES:(hb + 1) * LANES] * 0.125
            k = foxv[1][:, hb * LANES:(hb + 1) * LANES]
            ones_q = (lane >= HEAD_DIM) & (lane < HEAD_DIM + 3)
            ones_k = (lane >= HEAD_DIM + 3) & (lane < HEAD_DIM + 6)
            q = jnp.where(ones_q, 1.0, q)
            k = jnp.where(ones_k, 1.0, k)
            for n, part in enumerate((hi, mid, lo)):
                q = jnp.where(lane == HEAD_DIM + 3 + n, part, q)
                k = jnp.where(lane == HEAD_DIM + n, -part, k)
            fq.append(q)
            fk.append(k)
        fk = jnp.concatenate(fk, axis=1)
        ofox_ref[:, 0:HP] = jnp.concatenate(fq, axis=1).astype(BF16)
        ofox_ref[:, HP:2 * HP] = fk.astype(BF16)
        ofox_ref[:, 2 * HP:QKV] = foxv[2].astype(BF16)
        _transposed_tiles(ofoxt_ref, 0, [fk, foxv[2]], BQ)
        trig = _rope_trig(pos_v, invf_ref[...])
        cos, sin, first, second = _rope_tables(trig, HP, 0, HEAD_DIM // 2)
        nloc = lax.broadcasted_iota(jnp.int32, (TM, 1), 0).astype(F32)
        dec = lg_ref[...] * nloc
        rq = _rope_apply(retv[0], cos, sin, first, second, HEAD_DIM // 2, 1.0)
        rk = _rope_apply(retv[1], cos, sin, first, second, HEAD_DIM // 2, 1.0)
        oret_ref[:, 0:HP] = (rq * jnp.exp(dec)).astype(BF16)
        rk = rk * 0.125 * jnp.exp(-dec)
        oret_ref[:, HP:2 * HP] = rk.astype(BF16)
        _transposed_tiles(orett_ref, 0, [rk], BQ)
        oret_ref[:, 2 * HP:QKV] = retv[2].astype(BF16)
        cosm, sinm, firstm, secondm = _rope_tables(trig, HP, HEAD_DIM, ROPE_MLA // 2)
        cqn = _rms(cq_ref[...], gq_ref[...]).astype(BF16)
        qm = _bdot(cqn, wq_ref[...])
        omla_ref[:, 0:HP] = _rope_apply(qm, cosm, sinm, firstm, secondm, ROPE_MLA // 2, 1.0).astype(BF16)
        ckvn = _rms(ckv_ref[...], gkv_ref[...]).astype(BF16)
        kv = _bdot(ckvn, wkv_ref[...])
        krr = _rope_apply(kr_ref[...], cosm[:, 0:LANES], sinm[:, 0:LANES], firstm[:, 0:LANES],
                          secondm[:, 0:LANES], ROPE_MLA // 2, 1.0)
        mk = kv[:, 0:HP] + jnp.concatenate([krr] * HEADS, axis=1)
        omla_ref[:, HP:2 * HP] = mk.astype(BF16)
        omla_ref[:, 2 * HP:QKV] = kv[:, HP:2 * HP].astype(BF16)
        _transposed_tiles(omlat_ref, 0, [mk, kv[:, HP:2 * HP]], BQ)

    def seg(off, w):
        return pl.BlockSpec((TM, w), lambda i, o=off // w: (i, o))

    def full(shape):
        return pl.BlockSpec(shape, lambda i: (0,) * len(shape))

    def tiles(width):
        return pl.BlockSpec((TM // width, 2 * HP, width), lambda i: (i, 0, 0))

    in_specs = [seg(OFF_FOX, QKV_IN), seg(OFF_RET, QKV_IN), seg(OFF_SB, QKV_IN), seg(OFF_CQ, Q_RANK),
                seg(OFF_CKV, LANES), seg(OFF_KR, LANES), _row_spec(LANES), pl.BlockSpec((TM, 1), lambda i: (i, 0)),
                full((1, LANES)), full((1, HP)), full((1, Q_RANK)), full((1, KV_RANK)),
                full((Q_RANK, HP)), full((KV_RANK, 2 * HP))]
    out_specs = [_row_spec(QKV)] * 4 + [tiles(BQ), tiles(TKS), tiles(BQ),
                                        pl.BlockSpec((1, HP, BQ), lambda i: (i, 0, 0))]
    out_shape = [jax.ShapeDtypeStruct((t, QKV), BF16)] * 4 + [
        jax.ShapeDtypeStruct((t // BQ, 2 * HP, BQ), BF16), jax.ShapeDtypeStruct((t // TKS, 2 * HP, TKS), BF16),
        jax.ShapeDtypeStruct((t // BQ, 2 * HP, BQ), BF16), jax.ShapeDtypeStruct((t // BQ, HP, BQ), BF16)]
    return pl.pallas_call(
        body, name=name, grid=(t // TM,), in_specs=in_specs, out_specs=out_specs, out_shape=out_shape,
        compiler_params=_cp("parallel"))(proj, proj, proj, proj, proj, proj, cum, pos, invf, lg_lanes,
                                         g_q, g_kv, wq_pad, wkv_pad)


def _prep_bwd(dfox, dret, dsb, dmla, drg, dlsf, proj, pos, invf, lg_lanes, b_pad, g_q, g_kv,
              wq_pad, wkv_pad, name):
    t = proj.shape[0]

    def body(dfq, dfk, dfv, drq, drk, drv, dsq, dsk, dsv, dmq, dmk, dmv, drg_ref, dlsf_ref,
             cq_ref, ckv_ref, ff_ref, pos_ref, invf_ref, lg_ref, b_ref, gq_ref, gkv_ref, wq_ref, wkv_ref,
             dp_ref, dwq_ref, dwkv_ref, dgq_ref, dgkv_ref, dbf_ref):
        @pl.when(pl.program_id(0) == 0)
        def _():
            dwq_ref[...] = jnp.zeros_like(dwq_ref)
            dwkv_ref[...] = jnp.zeros_like(dwkv_ref)
            dgq_ref[...] = jnp.zeros_like(dgq_ref)
            dgkv_ref[...] = jnp.zeros_like(dgkv_ref)
            dbf_ref[...] = jnp.zeros_like(dbf_ref)

        pos_v = pos_ref[...]
        def put(off, val):
            dp_ref[:, off:off + GROUP] = _gather_heads(val).astype(BF16)

        for off, (dq, dk, dv) in ((OFF_FOX, (dfq, dfk, dfv)), (OFF_SB, (dsq, dsk, dsv))):
            put(off, dq[...] * 0.125)
            put(off + GROUP, dk[...])
            put(off + 2 * GROUP, dv[...])
        trig = _rope_trig(pos_v, invf_ref[...])
        cos, sin, first, second = _rope_tables(trig, HP, 0, HEAD_DIM // 2)
        nloc = lax.broadcasted_iota(jnp.int32, (TM, 1), 0).astype(F32)
        dec = lg_ref[...] * nloc
        dq = _rope_apply(drq[...] * jnp.exp(dec), cos, sin, first, second, HEAD_DIM // 2, -1.0)
        dk = _rope_apply(drk[...] * (0.125 * jnp.exp(-dec)), cos, sin, first, second, HEAD_DIM // 2, -1.0)
        put(OFF_RET, dq)
        put(OFF_RET + GROUP, dk)
        put(OFF_RET + 2 * GROUP, drv[...])
        put(OFF_RG, drg_ref[...])
        cosm, sinm, firstm, secondm = _rope_tables(trig, HP, HEAD_DIM, ROPE_MLA // 2)
        dql = _rope_apply(dmq[...], cosm, sinm, firstm, secondm, ROPE_MLA // 2, -1.0).astype(BF16)
        cq = cq_ref[...]
        cqn = _rms(cq, gq_ref[...]).astype(BF16)
        dwq_ref[...] += _bdot(cqn, dql, TN)
        dcqn = _bdot(dql, wq_ref[...], NT)
        dcq, dgq = _rms_bwd_vals(dcqn, cq, gq_ref[...])
        dgq_ref[...] += dgq
        dp_ref[:, OFF_CQ:OFF_CQ + Q_RANK] = dcq.astype(BF16)
        dkm = dmk[...]
        dkv = jnp.concatenate([dkm, dmv[...]], axis=1).astype(BF16)
        ckv = ckv_ref[...]
        ckvn = _rms(ckv, gkv_ref[...]).astype(BF16)
        dwkv_ref[...] += _bdot(ckvn, dkv, TN)
        dckvn = _bdot(dkv, wkv_ref[...], NT)
        dckv, dgkv = _rms_bwd_vals(dckvn, ckv, gkv_ref[...])
        dgkv_ref[...] += dgkv
        dp_ref[:, OFF_CKV:OFF_CKV + LANES] = dckv.astype(BF16)
        dkr = dkm[:, 0:LANES] + dkm[:, LANES:2 * LANES] + dkm[:, 2 * LANES:3 * LANES] + dkm[:, 3 * LANES:HP]
        act = firstm[:, 0:LANES] | secondm[:, 0:LANES]
        dkr = jnp.where(act, dkr, 0.0)
        dkr = _rope_apply(dkr, cosm[:, 0:LANES], sinm[:, 0:LANES], firstm[:, 0:LANES], secondm[:, 0:LANES],
                          ROPE_MLA // 2, -1.0)
        dp_ref[:, OFF_KR:OFF_KR + LANES] = dkr.astype(BF16)
        f = ff_ref[...] + b_ref[...]
        dff = dlsf_ref[...] / (1.0 + jnp.exp(f))
        dbf_ref[...] += jnp.sum(dff, axis=0, keepdims=True)
        dp_ref[:, OFF_FF:OFF_FF + LANES] = dff.astype(BF16)
        dp_ref[:, OFF_FF + LANES:NP_IN] = jnp.zeros((TM, NP_IN - OFF_FF - LANES), BF16)

    def seg(off, w):
        return pl.BlockSpec((TM, w), lambda i, o=off // w: (i, o))

    def full(shape):
        return pl.BlockSpec(shape, lambda i: (0,) * len(shape))

    hp_spec = _row_spec(HP)
    in_specs = [hp_spec] * 13 + [_row_spec(LANES), seg(OFF_CQ, Q_RANK), seg(OFF_CKV, LANES), seg(OFF_FF, LANES),
                                 pl.BlockSpec((TM, 1), lambda i: (i, 0)),
                                 full((1, LANES)), full((1, HP)), full((1, LANES)), full((1, Q_RANK)),
                                 full((1, KV_RANK)), full((Q_RANK, HP)), full((KV_RANK, 2 * HP))]
    out_specs = [_row_spec(NP_IN), full((Q_RANK, HP)), full((KV_RANK, 2 * HP)), full((1, Q_RANK)),
                 full((1, KV_RANK)), full((1, LANES))]
    out_shape = [jax.ShapeDtypeStruct((t, NP_IN), BF16), jax.ShapeDtypeStruct((Q_RANK, HP), F32),
                 jax.ShapeDtypeStruct((KV_RANK, 2 * HP), F32), jax.ShapeDtypeStruct((1, Q_RANK), F32),
                 jax.ShapeDtypeStruct((1, KV_RANK), F32), jax.ShapeDtypeStruct((1, LANES), F32)]
    return pl.pallas_call(
        body, name=name, grid=(t // TM,), in_specs=in_specs, out_specs=out_specs, out_shape=out_shape,
        compiler_params=_cp("arbitrary"))(*dfox, *dret, *dsb, *dmla, drg, dlsf, proj, proj, proj, pos, invf,
                                          lg_lanes, b_pad, g_q, g_kv, wq_pad, wkv_pad)


def _cumsum(x, reverse, name, partials=None):
    t, w = x.shape
    n = t // TM
    xs = [x] if partials is None else [x, partials]

    def body(*refs):
        x_refs, o_ref, carry = refs[:len(xs)], refs[len(xs)], refs[len(xs) + 1]

        @pl.when(pl.program_id(0) == 0)
        def _():
            carry[...] = jnp.zeros_like(carry)

        r = lax.broadcasted_iota(jnp.int32, (TM, TM), 0)
        c = lax.broadcasted_iota(jnp.int32, (TM, TM), 1)
        tri = jnp.where((r <= c) if reverse else (r >= c), 1.0, 0.0).astype(BF16)
        v = x_refs[0][...]
        if partials is not None:
            lane = lax.broadcasted_iota(jnp.int32, (TM, LANES), 1)
            for hb in range(HEADS):
                v = v + jnp.where(lane == hb, jnp.sum(x_refs[1][:, _hs(hb)], axis=1, keepdims=True), 0.0)
        hi = v.astype(BF16)
        r1 = v - hi.astype(F32)
        mid = r1.astype(BF16)
        lo = (r1 - mid.astype(F32)).astype(BF16)
        cs = _bdot(tri, hi) + _bdot(tri, mid) + _bdot(tri, lo) + carry[...]
        o_ref[...] = cs
        carry[...] = cs[0:1, :] if reverse else cs[TM - 1:TM, :]

    imap = (lambda i: (n - 1 - i, 0)) if reverse else (lambda i: (i, 0))
    return pl.pallas_call(
        body, name=name, grid=(n,), in_specs=[pl.BlockSpec((TM, a.shape[1]), imap) for a in xs],
        out_specs=pl.BlockSpec((TM, w), imap),
        out_shape=jax.ShapeDtypeStruct((t, w), F32), scratch_shapes=[pltpu.VMEM((1, w), F32)],
        compiler_params=_cp("arbitrary"))(*xs)


HB_FWD = 4
HB_BWD = 4
HB_SB_FWD = 4
BQS = 512
BQA = 512


def _q_spec(hb, bq=BQ):
    return pl.BlockSpec((bq, hb * LANES), lambda g, i: (i, g))


ONE_BUFFER = pl.Buffered(1)


def _kv_spec(t, which, hb):
    return pl.BlockSpec((t, hb * LANES), lambda g, i, w=which: (0, w * (HEADS // hb) + g), pipeline_mode=ONE_BUFFER)


def _acc_spec(t, hb):
    return pl.BlockSpec((t, hb * LANES), lambda g, i: (0, g), pipeline_mode=ONE_BUFFER)


def _hs(hh):
    return slice(hh * LANES, (hh + 1) * LANES)


def _tile_iota(rows, cols):
    return (lax.broadcasted_iota(jnp.int32, (rows, cols), 0), lax.broadcasted_iota(jnp.int32, (rows, cols), 1))


def _kvt_spec(nkv, width, which, hb):
    return pl.BlockSpec((nkv, hb * LANES, width), lambda g, i, w=which: (0, w * (HEADS // hb) + g, 0),
                        pipeline_mode=ONE_BUFFER)


def _qrow_spec(hb, bq=BQ):
    return pl.BlockSpec((hb, 1, 1, bq), lambda g, i: (g, i, 0, 0))


def _vis(key0, query0, rows, cols, kind):
    r, c = _tile_iota(rows, cols)
    k, q = key0 + r, query0 + c
    if kind == "chunk":
        return (k >> CHUNK_SHIFT) <= (q >> CHUNK_SHIFT)
    return (k < q) if kind == "strict" else (k <= q)


def _softmax_fwd(qkv, kvt, *, chunk_mask, scale, name):
    t = qkv.shape[0]
    nq = t // BQA
    per = BQA // BQ
    hb = HB_FWD
    kind = "chunk" if chunk_mask else "causal"

    def body(q_ref, k_ref, vt_ref, o_ref, lse_ref, m_sc, l_sc, acc_sc):
        i = pl.program_id(1)
        m_sc[...] = jnp.full((hb, 1, BQA), NEG, F32)
        l_sc[...] = jnp.zeros((hb, 1, BQA), F32)
        acc_sc[...] = jnp.zeros((hb, LANES, BQA), F32)

        def tile(j, qoff):
            off = pl.multiple_of(j * BQ, BQ)
            lo = 0 if qoff is None else qoff
            qs = slice(lo, BQA)
            vis = None if qoff is None else _vis(off, i * BQA + lo, BQ, BQA - lo, kind)
            ss = [_bdot(k_ref[pl.ds(off, BQ), _hs(hh)], q_ref[qs, _hs(hh)], NT) for hh in range(hb)]
            ps, alphas = [], []
            for hh in range(hb):
                s = ss[hh]
                if scale != 1.0:
                    s = s * scale
                if vis is not None:
                    s = jnp.where(vis, s, NEG)
                m_old = m_sc[hh, :, qs]
                m_new = jnp.maximum(m_old, jnp.max(s, axis=0, keepdims=True))
                alpha = jnp.exp(m_old - m_new)
                p = jnp.exp(s - m_new)
                l_sc[hh, :, qs] = alpha * l_sc[hh, :, qs] + jnp.sum(p, axis=0, keepdims=True)
                m_sc[hh, :, qs] = m_new
                ps.append(p.astype(BF16))
                alphas.append(alpha)
            for hh in range(hb):
                acc_sc[hh, :, qs] = alphas[hh] * acc_sc[hh, :, qs] + _bdot(vt_ref[j, _hs(hh), :], ps[hh])

        def loop(j, carry):
            tile(j, None)
            return carry

        lax.fori_loop(0, per * i, loop, 0)
        for d in range(per):
            tile(per * i + d, d * BQ)
        for hh in range(hb):
            l = l_sc[hh]
            o_ref[:, _hs(hh)] = (acc_sc[hh] / l).T
            lse_ref[hh, 0] = m_sc[hh] + jnp.log(l)

    return pl.pallas_call(
        body, name=name, grid=(HEADS // hb, nq),
        in_specs=[_q_spec(hb, BQA), _kv_spec(t, 1, hb), _kvt_spec(t // BQ, BQ, 1, hb)],
        out_specs=[_q_spec(hb, BQA), _qrow_spec(hb, BQA)],
        out_shape=[jax.ShapeDtypeStruct((t, HP), F32), jax.ShapeDtypeStruct((HEADS, nq, 1, BQA), F32)],
        scratch_shapes=[pltpu.VMEM((hb, 1, BQA), F32), pltpu.VMEM((hb, 1, BQA), F32),
                        pltpu.VMEM((hb, LANES, BQA), F32)],
        compiler_params=_cp("parallel", "arbitrary"))(qkv, qkv, kvt)


def _softmax_bwd(qkv, kvt, do, lse, delta, *, bias, chunk_mask, scale, name):
    t = qkv.shape[0]
    nq = t // BQA
    per = BQA // BQ
    hb = HB_BWD
    kind = "chunk" if chunk_mask else "causal"

    def body(*refs):
        if bias:
            (q_ref, k_ref, v_ref, kt_ref, do_ref, lse_ref, dl_ref, dq_ref, dk_ref, dv_ref, dck_ref, dcq_ref,
             dq_sc, dcq_sc) = refs
            dcq_sc[...] = jnp.zeros((hb, 1, BQA), F32)
        else:
            q_ref, k_ref, v_ref, kt_ref, do_ref, lse_ref, dl_ref, dq_ref, dk_ref, dv_ref, dq_sc = refs
        i = pl.program_id(1)

        @pl.when(i == 0)
        def _():
            dk_ref[...] = jnp.zeros_like(dk_ref)
            dv_ref[...] = jnp.zeros_like(dv_ref)
            if bias:
                dck_ref[...] = jnp.zeros_like(dck_ref)

        dq_sc[...] = jnp.zeros((hb, LANES, BQA), F32)

        def tile(j, qoff):
            off = pl.multiple_of(j * BQ, BQ)
            lo = 0 if qoff is None else qoff
            qsl = slice(lo, BQA)
            vis = None if qoff is None else _vis(off, i * BQA + lo, BQ, BQA - lo, kind)
            qs = [q_ref[qsl, _hs(hh)] for hh in range(hb)]
            dobs = [do_ref[qsl, _hs(hh)].astype(BF16) for hh in range(hb)]
            ss = [_bdot(k_ref[pl.ds(off, BQ), _hs(hh)], qs[hh], NT) for hh in range(hb)]
            dps = [_bdot(v_ref[pl.ds(off, BQ), _hs(hh)], dobs[hh], NT) for hh in range(hb)]
            pbs, dsbs = [], []
            for hh in range(hb):
                s = ss[hh]
                if scale != 1.0:
                    s = s * scale
                p = jnp.exp(s - lse_ref[hh, 0, :, qsl])
                if vis is not None:
                    p = jnp.where(vis, p, 0.0)
                ds = p * (dps[hh] - dl_ref[hh, 0, :, qsl])
                if bias:
                    part = ds[:, 0:LANES]
                    for b in range(1, (BQA - lo) // LANES):
                        part = part + ds[:, b * LANES:(b + 1) * LANES]
                    dck_ref[pl.ds(off, BQ), _hs(hh)] -= part
                    dcq_sc[hh, :, qsl] += jnp.sum(ds, axis=0, keepdims=True)
                if scale != 1.0:
                    ds = ds * scale
                pbs.append(p.astype(BF16))
                dsbs.append(ds.astype(BF16))
            for hh in range(hb):
                sl = _hs(hh)
                dv_ref[pl.ds(off, BQ), sl] += _bdot(pbs[hh], dobs[hh])
                dk_ref[pl.ds(off, BQ), sl] += _bdot(dsbs[hh], qs[hh])
                dq_sc[hh, :, qsl] += _bdot(kt_ref[j, sl, :], dsbs[hh])

        def loop(j, carry):
            tile(j, None)
            return carry

        lax.fori_loop(0, per * i, loop, 0)
        for d in range(per):
            tile(per * i + d, d * BQ)
        for hh in range(hb):
            dq_ref[:, _hs(hh)] = dq_sc[hh].T
            if bias:
                dcq_ref[hh, 0] = dcq_sc[hh]

    in_specs = [_q_spec(hb, BQA), _kv_spec(t, 1, hb), _kv_spec(t, 2, hb), _kvt_spec(t // BQ, BQ, 0, hb),
                _q_spec(hb, BQA), _qrow_spec(hb, BQA), _qrow_spec(hb, BQA)]
    out_specs = [_q_spec(hb, BQA), _acc_spec(t, hb), _acc_spec(t, hb)]
    out_shape = [jax.ShapeDtypeStruct((t, HP), F32)] * 3
    scratch = [pltpu.VMEM((hb, LANES, BQA), F32)]
    if bias:
        out_specs += [_acc_spec(t, hb), _qrow_spec(hb, BQA)]
        out_shape += [jax.ShapeDtypeStruct((t, HP), F32), jax.ShapeDtypeStruct((HEADS, nq, 1, BQA), F32)]
        scratch.append(pltpu.VMEM((hb, 1, BQA), F32))
    return pl.pallas_call(
        body, name=name, grid=(HEADS // hb, nq), in_specs=in_specs, out_specs=out_specs, out_shape=out_shape,
        scratch_shapes=scratch,
        compiler_params=_cp("parallel", "arbitrary"))(qkv, qkv, qkv, kvt, do, lse, delta)


def _ret_diag_decay(lg1, keys_on_rows=False):
    r, c = _tile_iota(BQ, BQ)
    qn, km = (c, r) if keys_on_rows else (r, c)
    dd = jnp.where(km > qn, jnp.exp((2.0 * lg1) * (km - qn).astype(F32)), 1.0)
    return jnp.where((km >> CHUNK_SHIFT) <= (qn >> CHUNK_SHIFT), dd, 0.0)


def _lg_spec(hb):
    return pl.BlockSpec((hb, 1, LANES), lambda g, i: (g, 0, 0))


def _ret_specs(nq, hb, reverse):
    tile = (lambda i: nq - 1 - i) if reverse else (lambda i: i)
    qkv = [pl.BlockSpec((BQ, hb * LANES), lambda g, i, w=w: (tile(i), w * (HEADS // hb) + g)) for w in range(3)]
    kt = pl.BlockSpec((1, hb * LANES, BQ), lambda g, i: (tile(i), g, 0))
    st = pl.BlockSpec((1, hb * LANES, LANES), lambda g, i: (tile(i), g, 0))
    return qkv, kt, st


def _ret_fwd(qkv, kt, lg_heads, name):
    t = qkv.shape[0]
    nq = t // BQ
    hb = HB_FWD

    def body(lg_ref, q_ref, k_ref, v_ref, kt_ref, o_ref, st_ref, s_sc):
        @pl.when(pl.program_id(1) == 0)
        def _():
            s_sc[...] = jnp.zeros_like(s_sc)

        qs = [q_ref[:, _hs(hh)] for hh in range(hb)]
        vs = [v_ref[:, _hs(hh)] for hh in range(hb)]
        aa = [_bdot(qs[hh], k_ref[:, _hs(hh)], NT) for hh in range(hb)]
        kv = [_bdot(kt_ref[0, _hs(hh), :], vs[hh]) for hh in range(hb)]
        for hh in range(hb):
            sl = _hs(hh)
            lg1 = lg_ref[hh][:, 0:1]
            s = s_sc[hh]
            st_ref[0, sl, :] = s
            shi, slo = _split2(s)
            a = (aa[hh] * _ret_diag_decay(lg1)).astype(BF16)
            o_ref[:, sl] = _bdot(a, vs[hh]) + _bdot(qs[hh], shi) + _bdot(qs[hh], slo)
            s_sc[hh] = jnp.exp(lg1 * float(BQ)) * (s + kv[hh])

    qkv_specs, kt_spec, st_spec = _ret_specs(nq, hb, False)
    return pl.pallas_call(
        body, name=name, grid=(HEADS // hb, nq), in_specs=[_lg_spec(hb)] + qkv_specs + [kt_spec],
        out_specs=[_q_spec(hb), st_spec],
        out_shape=[jax.ShapeDtypeStruct((t, HP), F32), jax.ShapeDtypeStruct((nq, HP, LANES), F32)],
        scratch_shapes=[pltpu.VMEM((hb, LANES, LANES), F32)],
        compiler_params=_cp("parallel", "arbitrary"))(lg_heads, qkv, qkv, qkv, kt)


def _ret_bwd(qkv, states, lg_heads, do, name):
    t = qkv.shape[0]
    nq = t // BQ
    hb = HB_FWD

    def body(lg_ref, q_ref, k_ref, v_ref, st_ref, do_ref, dq_ref, dk_ref, dv_ref, g_sc):
        @pl.when(pl.program_id(1) == 0)
        def _():
            g_sc[...] = jnp.zeros_like(g_sc)

        qs = [q_ref[:, _hs(hh)] for hh in range(hb)]
        ks = [k_ref[:, _hs(hh)] for hh in range(hb)]
        vs = [v_ref[:, _hs(hh)] for hh in range(hb)]
        dobs = [do_ref[:, _hs(hh)].astype(BF16) for hh in range(hb)]
        aa = [_bdot(ks[hh], qs[hh], NT) for hh in range(hb)]
        das = [_bdot(vs[hh], dobs[hh], NT) for hh in range(hb)]
        qdo = [_bdot(qs[hh], dobs[hh], TN) for hh in range(hb)]
        for hh in range(hb):
            sl = _hs(hh)
            lg1 = lg_ref[hh][:, 0:1]
            dd = _ret_diag_decay(lg1, keys_on_rows=True)
            at = (aa[hh] * dd).astype(BF16)
            dat = (das[hh] * dd).astype(BF16)
            h = jnp.exp(lg1 * float(BQ)) * g_sc[hh]
            hhi, hlo = _split2(h)
            shi, slo = _split2(st_ref[0, sl, :])
            dv_ref[:, sl] = _bdot(at, dobs[hh]) + _bdot(ks[hh], hhi) + _bdot(ks[hh], hlo)
            dk_ref[:, sl] = _bdot(dat, qs[hh]) + _bdot(vs[hh], hhi, NT) + _bdot(vs[hh], hlo, NT)
            dq_ref[:, sl] = _bdot(dat, ks[hh], TN) + _bdot(dobs[hh], shi, NT) + _bdot(dobs[hh], slo, NT)
            g_sc[hh] = qdo[hh] + h

    qkv_specs, _, st_spec = _ret_specs(nq, hb, True)
    tile_spec = pl.BlockSpec((BQ, hb * LANES), lambda g, i: (nq - 1 - i, g))
    return pl.pallas_call(
        body, name=name, grid=(HEADS // hb, nq), in_specs=[_lg_spec(hb)] + qkv_specs + [st_spec, tile_spec],
        out_specs=[tile_spec] * 3, out_shape=[jax.ShapeDtypeStruct((t, HP), F32)] * 3,
        scratch_shapes=[pltpu.VMEM((hb, LANES, LANES), F32)],
        compiler_params=_cp("parallel", "arbitrary"))(lg_heads, qkv, qkv, qkv, states, do)


def _sb_tile_logs(q, kb, vis):
    z = _bdot(kb, q, NT)
    ls = -(jnp.maximum(z, 0.0) + jnp.log(1.0 + jnp.exp(-jnp.abs(z))))
    if vis is not None:
        ls = jnp.where(vis, ls, 0.0)
    return z, ls


def _sb_later(ls, after):
    hi, lo = _split2(ls)
    return _bdot(after, hi) + _bdot(after, lo)


def _sb_fwd(qkv, kvt, name):
    t = qkv.shape[0]
    nq = t // BQS
    per = BQS // TKS
    hb = HB_SB_FWD

    def body(q_ref, k_ref, vt_ref, o_ref, tot_ref, acc_sc, r_sc):
        i = pl.program_id(1)
        acc_sc[...] = jnp.zeros((hb, LANES, BQS), F32)
        r_sc[...] = jnp.zeros((hb, 1, BQS), F32)
        mr, mc = _tile_iota(TKS, TKS)
        after = jnp.where(mc > mr, 1.0, 0.0).astype(BF16)

        def tile(j, qoff):
            off = pl.multiple_of(j * TKS, TKS)
            lo = 0 if qoff is None else qoff
            qsl = slice(lo, BQS)
            vis = None if qoff is None else _vis(off, i * BQS + lo, TKS, BQS - lo, "strict")
            zl = [_sb_tile_logs(q_ref[qsl, _hs(hh)], k_ref[pl.ds(off, TKS), _hs(hh)], vis) for hh in range(hb)]
            laters = [_sb_later(zl[hh][1], after) for hh in range(hb)]
            ws = []
            for hh in range(hb):
                z, ls = zl[hh]
                w = jnp.exp(z + ls + laters[hh] + r_sc[hh, :, qsl])
                if vis is not None:
                    w = jnp.where(vis, w, 0.0)
                ws.append(w.astype(BF16))
                r_sc[hh, :, qsl] += jnp.sum(ls, axis=0, keepdims=True)
            for hh in range(hb):
                acc_sc[hh, :, qsl] += _bdot(vt_ref[j, _hs(hh), :], ws[hh])

        for d in reversed(range(per)):
            tile(per * i + d, d * TKS)

        def loop(jj, carry):
            tile(per * i - 1 - jj, None)
            return carry

        lax.fori_loop(0, per * i, loop, 0)
        for hh in range(hb):
            o_ref[:, _hs(hh)] = acc_sc[hh].T
            tot_ref[hh, 0] = r_sc[hh]

    return pl.pallas_call(
        body, name=name, grid=(HEADS // hb, nq),
        in_specs=[_q_spec(hb, BQS), _kv_spec(t, 1, hb), _kvt_spec(t // TKS, TKS, 1, hb)],
        out_specs=[_q_spec(hb, BQS), _qrow_spec(hb, BQS)],
        out_shape=[jax.ShapeDtypeStruct((t, HP), F32), jax.ShapeDtypeStruct((HEADS, nq, 1, BQS), F32)],
        scratch_shapes=[pltpu.VMEM((hb, LANES, BQS), F32), pltpu.VMEM((hb, 1, BQS), F32)],
        compiler_params=_cp("parallel", "arbitrary"))(qkv, qkv, kvt)


def _sb_bwd(qkv, kvt, do, tot, name):
    t = qkv.shape[0]
    nq = t // BQS
    per = BQS // TKS
    hb = HB_BWD

    def body(q_ref, k_ref, v_ref, kt_ref, do_ref, tot_ref, dq_ref, dk_ref, dv_ref, dq_sc, p_sc, g_sc):
        i = pl.program_id(1)

        @pl.when(i == 0)
        def _():
            dk_ref[...] = jnp.zeros_like(dk_ref)
            dv_ref[...] = jnp.zeros_like(dv_ref)

        dq_sc[...] = jnp.zeros((hb, LANES, BQS), F32)
        p_sc[...] = jnp.zeros((hb, 1, BQS), F32)
        g_sc[...] = jnp.zeros((hb, 1, BQS), F32)
        mr, mc = _tile_iota(TKS, TKS)
        after = jnp.where(mc > mr, 1.0, 0.0).astype(BF16)
        before = jnp.where(mc < mr, 1.0, 0.0).astype(BF16)

        def tile(j, qoff):
            off = pl.multiple_of(j * TKS, TKS)
            lo = 0 if qoff is None else qoff
            qsl = slice(lo, BQS)
            vis = None if qoff is None else _vis(off, i * BQS + lo, TKS, BQS - lo, "strict")
            qs = [q_ref[qsl, _hs(hh)] for hh in range(hb)]
            dobs = [do_ref[qsl, _hs(hh)].astype(BF16) for hh in range(hb)]
            zl = [_sb_tile_logs(qs[hh], k_ref[pl.ds(off, TKS), _hs(hh)], vis) for hh in range(hb)]
            dws = [_bdot(v_ref[pl.ds(off, TKS), _hs(hh)], dobs[hh], NT) for hh in range(hb)]
            laters = [_sb_later(zl[hh][1], after) for hh in range(hb)]
            ws, gs = [], []
            for hh in range(hb):
                z, ls = zl[hh]
                own = jnp.sum(ls, axis=0, keepdims=True)
                rest = tot_ref[hh, 0, :, qsl] - p_sc[hh, :, qsl] - own
                w = jnp.exp(z + ls + laters[hh] + rest)
                if vis is not None:
                    w = jnp.where(vis, w, 0.0)
                p_sc[hh, :, qsl] += own
                ws.append(w.astype(BF16))
                gs.append(dws[hh] * w)
            gins = [_bdot(before, gs[hh].astype(BF16)) for hh in range(hb)]
            dzbs = []
            for hh in range(hb):
                g = gs[hh]
                stay = jnp.exp(zl[hh][1])
                dz = g * stay - (1.0 - stay) * (gins[hh] + g_sc[hh, :, qsl])
                if vis is not None:
                    dz = jnp.where(vis, dz, 0.0)
                g_sc[hh, :, qsl] += jnp.sum(g, axis=0, keepdims=True)
                dzbs.append(dz.astype(BF16))
            for hh in range(hb):
                sl = _hs(hh)
                dv_ref[pl.ds(off, TKS), sl] += _bdot(ws[hh], dobs[hh])
                dk_ref[pl.ds(off, TKS), sl] += _bdot(dzbs[hh], qs[hh])
                dq_sc[hh, :, qsl] += _bdot(kt_ref[j, sl, :], dzbs[hh])

        def loop(j, carry):
            tile(j, None)
            return carry

        lax.fori_loop(0, per * i, loop, 0)
        for d in range(per):
            tile(per * i + d, d * TKS)
        for hh in range(hb):
            dq_ref[:, _hs(hh)] = dq_sc[hh].T

    return pl.pallas_call(
        body, name=name, grid=(HEADS // hb, nq),
        in_specs=[_q_spec(hb, BQS), _kv_spec(t, 1, hb), _kv_spec(t, 2, hb), _kvt_spec(t // TKS, TKS, 0, hb),
                  _q_spec(hb, BQS), _qrow_spec(hb, BQS)],
        out_specs=[_q_spec(hb, BQS), _acc_spec(t, hb), _acc_spec(t, hb)],
        out_shape=[jax.ShapeDtypeStruct((t, HP), F32)] * 3,
        scratch_shapes=[pltpu.VMEM((hb, LANES, BQS), F32), pltpu.VMEM((hb, 1, BQS), F32),
                        pltpu.VMEM((hb, 1, BQS), F32)],
        compiler_params=_cp("parallel", "arbitrary"))(qkv, qkv, qkv, kvt, do, tot)


def _sigmoid(v):
    return 1.0 / (1.0 + jnp.exp(-v))


def _post_fwd(oa, ob, oc, od, proj, g_pad, name):
    t = oa.shape[0]

    def body(oa_ref, ob_ref, oc_ref, od_ref, rg_ref, g_ref, mx_ref):
        g = g_ref[...]

        def group(o, gg):
            r = lax.rsqrt(jnp.sum(o * o, axis=-1, keepdims=True) * (1.0 / GROUP) + EPS)
            return _gather_heads(o * r * gg).astype(BF16)

        mx_ref[:, 0:GROUP] = group(oa_ref[...], g[:, 0:HP])
        mx_ref[:, GROUP:2 * GROUP] = group(ob_ref[...], g[:, HP:2 * HP])
        mx_ref[:, 3 * GROUP:4 * GROUP] = group(od_ref[...], g[:, 3 * HP:4 * HP])
        real = lax.broadcasted_iota(jnp.int32, (TM, LANES), 1) < HEAD_DIM
        rg = _spread_heads(rg_ref[...])
        gated = []
        for hb in range(HEADS):
            sl = slice(hb * LANES, (hb + 1) * LANES)
            o = oc_ref[:, sl]
            mu = jnp.sum(o, axis=-1, keepdims=True) * (1.0 / HEAD_DIM)
            dlt = jnp.where(real, o - mu, 0.0)
            var = jnp.sum(dlt * dlt, axis=-1, keepdims=True) * (1.0 / HEAD_DIM)
            yn = dlt * lax.rsqrt(var + EPS) * g[:, 2 * HP + hb * LANES:2 * HP + (hb + 1) * LANES]
            x = rg[:, sl]
            gated.append(yn * (x * _sigmoid(x)))
        mx_ref[:, 2 * GROUP:3 * GROUP] = _gather_heads(jnp.concatenate(gated, axis=1)).astype(BF16)

    rg_spec = pl.BlockSpec((TM, GROUP), lambda i: (i, OFF_RG // GROUP))
    return pl.pallas_call(
        body, name=name, grid=(t // TM,),
        in_specs=[_row_spec(HP)] * 4 + [rg_spec, _vec_spec(4 * HP)], out_specs=_row_spec(D_MODEL),
        out_shape=jax.ShapeDtypeStruct((t, D_MODEL), BF16), compiler_params=_cp("parallel"))(oa, ob, oc, od, proj, g_pad)


def _post_bwd(dmx, oa, ob, oc, od, proj, g_pad, name):
    t = oa.shape[0]

    def body(dm_ref, oa_ref, ob_ref, oc_ref, od_ref, rg_ref, g_ref,
             doa_ref, dob_ref, doc_ref, dod_ref, dla_ref, dlb_ref, drg_ref, dg_ref):
        @pl.when(pl.program_id(0) == 0)
        def _():
            dg_ref[...] = jnp.zeros_like(dg_ref)

        g = g_ref[...]

        def group_bwd(dm, o, gg):
            r = lax.rsqrt(jnp.sum(o * o, axis=-1, keepdims=True) * (1.0 / GROUP) + EPS)
            oh = o * r
            dgp = jnp.sum(dm * oh, axis=0, keepdims=True)
            dyh = dm * gg
            do = r * (dyh - oh * (jnp.sum(dyh * oh, axis=-1, keepdims=True) * (1.0 / GROUP)))
            return do, dgp

        def delta_bc(do, o):
            prod = do * o
            lane = lax.broadcasted_iota(jnp.int32, (TM, LANES), 1)
            out = jnp.zeros((TM, LANES), F32)
            for hb in range(HEADS):
                out = jnp.where(lane == hb, jnp.sum(prod[:, hb * LANES:(hb + 1) * LANES], axis=-1, keepdims=True), out)
            return out

        dmp = [_spread_heads(dm_ref[:, gi * GROUP:(gi + 1) * GROUP]) for gi in range(4)]
        rg = _spread_heads(rg_ref[...])
        oa = oa_ref[...]
        do_a, dga = group_bwd(dmp[0], oa, g[:, 0:HP])
        doa_ref[...] = do_a
        dla_ref[...] = delta_bc(do_a, oa)
        dg_ref[:, 0:HP] += dga
        ob = ob_ref[...]
        do_b, dgb = group_bwd(dmp[1], ob, g[:, HP:2 * HP])
        dob_ref[...] = do_b
        dlb_ref[...] = delta_bc(do_b, ob)
        dg_ref[:, HP:2 * HP] += dgb
        do_d, dgd = group_bwd(dmp[3], od_ref[...], g[:, 3 * HP:4 * HP])
        dod_ref[...] = do_d
        dg_ref[:, 3 * HP:4 * HP] += dgd
        real = lax.broadcasted_iota(jnp.int32, (TM, LANES), 1) < HEAD_DIM
        for hb in range(HEADS):
            sl = slice(hb * LANES, (hb + 1) * LANES)
            gsl = slice(2 * HP + hb * LANES, 2 * HP + (hb + 1) * LANES)
            o = oc_ref[:, sl]
            mu = jnp.sum(o, axis=-1, keepdims=True) * (1.0 / HEAD_DIM)
            dlt = jnp.where(real, o - mu, 0.0)
            var = jnp.sum(dlt * dlt, axis=-1, keepdims=True) * (1.0 / HEAD_DIM)
            rstd = lax.rsqrt(var + EPS)
            dhat = dlt * rstd
            gc = g[:, gsl]
            x = rg[:, sl]
            sg = _sigmoid(x)
            dm = dmp[2][:, sl]
            drg_ref[:, sl] = dm * (dhat * gc) * (sg * (1.0 + x * (1.0 - sg)))
            dyn = dm * (x * sg)
            dg_ref[:, gsl] += jnp.sum(dyn * dhat, axis=0, keepdims=True)
            ddh = dyn * gc
            m1 = jnp.sum(ddh, axis=-1, keepdims=True) * (1.0 / HEAD_DIM)
            m2 = jnp.sum(ddh * dhat, axis=-1, keepdims=True) * (1.0 / HEAD_DIM)
            doc_ref[:, sl] = jnp.where(real, rstd * (ddh - m1 - dhat * m2), 0.0)

    rg_spec = pl.BlockSpec((TM, GROUP), lambda i: (i, OFF_RG // GROUP))
    hp = _row_spec(HP)
    return pl.pallas_call(
        body, name=name, grid=(t // TM,),
        in_specs=[_row_spec(D_MODEL), hp, hp, hp, hp, rg_spec, _vec_spec(4 * HP)],
        out_specs=[hp] * 4 + [_row_spec(LANES)] * 2 + [hp, _vec_spec(4 * HP)],
        out_shape=[jax.ShapeDtypeStruct((t, HP), F32)] * 4 + [jax.ShapeDtypeStruct((t, LANES), F32)] * 2
        + [jax.ShapeDtypeStruct((t, HP), F32), jax.ShapeDtypeStruct((1, 4 * HP), F32)],
        compiler_params=_cp("arbitrary"))(dmx, oa, ob, oc, od, proj, g_pad)


def _mesh_pos():
    return lax.axis_index("x"), lax.axis_index("y"), lax.axis_index("c")


def _peer(pos, k):
    x, y, c = pos
    px = 1 - x if (k >> 2) & 1 else x
    py = 1 - y if (k >> 1) & 1 else y
    pc = 1 - c if k & 1 else c
    return (px, py, pc), 4 * px + 2 * py + pc


def _exchange(arrs, gather, name):
    n = len(arrs)

    def body(*refs):
        ins, outs = refs[:n], refs[n:2 * n]
        send_sems, recv_sems, loc_sems = refs[2 * n:]
        pos = _mesh_pos()
        me = 4 * pos[0] + 2 * pos[1] + pos[2]
        local = []
        for a in range(n):
            src = ins[a] if gather else ins[a].at[me]
            cp = pltpu.make_async_copy(src, outs[a].at[me], loc_sems.at[a])
            cp.start()
            local.append(cp)
        sends, recvs = [], []
        for k in range(1, N_DEV):
            peer, pid = _peer(pos, k)
            for a in range(n):
                s = a * (N_DEV - 1) + k - 1
                src = ins[a] if gather else ins[a].at[pid]
                cp = pltpu.make_async_remote_copy(
                    src_ref=src, dst_ref=outs[a].at[me], send_sem=send_sems.at[s], recv_sem=recv_sems.at[s],
                    device_id=peer, device_id_type=pl.DeviceIdType.MESH)
                cp.start()
                sends.append(cp)
                recvs.append(pltpu.make_async_remote_copy(
                    src_ref=src, dst_ref=outs[a].at[pid], send_sem=send_sems.at[s], recv_sem=recv_sems.at[s],
                    device_id=peer, device_id_type=pl.DeviceIdType.MESH))
        for cp in recvs:
            cp.wait_recv()
        for cp in sends:
            cp.wait_send()
        for cp in local:
            cp.wait()

    any_spec = pl.BlockSpec(memory_space=pl.ANY)
    out_shape = [jax.ShapeDtypeStruct((N_DEV,) + tuple(a.shape) if gather else tuple(a.shape), a.dtype) for a in arrs]
    return pl.pallas_call(
        body, name=name, in_specs=[any_spec] * n, out_specs=[any_spec] * n, out_shape=out_shape,
        scratch_shapes=[pltpu.SemaphoreType.DMA((n * (N_DEV - 1),)), pltpu.SemaphoreType.DMA((n * (N_DEV - 1),)),
                        pltpu.SemaphoreType.DMA((n,))],
        compiler_params=pltpu.CompilerParams(has_side_effects=True))(*arrs)


def _device_index():
    x, y, c = _mesh_pos()
    return 4 * x + 2 * y + c


def _landing(srcs, gather):
    me = _device_index()
    lands = []
    for a in srcs:
        own = a[None] if gather else lax.dynamic_slice_in_dim(a, me, 1, axis=0)
        shape = (N_DEV,) + tuple(a.shape) if gather else tuple(a.shape)
        lands.append(lax.dynamic_update_slice_in_dim(lax.empty(shape, a.dtype), own, me, axis=0))
    return lands


def _exchange_copies(ins, lands, send_sems, recv_sems, gather):
    pos = _mesh_pos()
    me = 4 * pos[0] + 2 * pos[1] + pos[2]
    sends, recvs = [], []
    for k in range(1, N_DEV):
        peer, pid = _peer(pos, k)
        for a in range(len(ins)):
            s = a * (N_DEV - 1) + k - 1
            src = ins[a] if gather else ins[a].at[pid]
            sends.append(pltpu.make_async_remote_copy(
                src_ref=src, dst_ref=lands[a].at[me], send_sem=send_sems.at[s], recv_sem=recv_sems.at[s],
                device_id=peer, device_id_type=pl.DeviceIdType.MESH))
            recvs.append(pltpu.make_async_remote_copy(
                src_ref=src, dst_ref=lands[a].at[pid], send_sem=send_sems.at[s], recv_sem=recv_sems.at[s],
                device_id=peer, device_id_type=pl.DeviceIdType.MESH))
    return sends, recvs


def _exchange_start(srcs, gather, name, after=None):
    n = len(srcs)
    lands = _landing(srcs, gather)
    nsem = n * (N_DEV - 1)
    extra = [] if after is None else [after]

    def body(*refs):
        ins, lnd = refs[:n], refs[n:2 * n]
        send_sems, recv_sems = refs[2 * n + len(extra)], refs[2 * n + len(extra) + 1]
        token = refs[-1]
        sends, _ = _exchange_copies(ins, lnd, send_sems, recv_sems, gather)
        for cp in sends:
            cp.start()
        token[...] = jnp.zeros_like(token)

    hbm = pl.BlockSpec(memory_space=pltpu.HBM)
    sem = pl.BlockSpec(memory_space=pltpu.SEMAPHORE)
    bufs = list(srcs) + lands
    out_shape = ([pltpu.SemaphoreType.DMA((nsem,)), pltpu.SemaphoreType.DMA((nsem,))]
                 + [pltpu.HBM(b.shape, b.dtype) for b in bufs] + [jax.ShapeDtypeStruct((8, LANES), F32)])
    outs = pl.pallas_call(
        body, name=name, in_specs=[hbm] * (2 * n) + [pl.BlockSpec(memory_space=pl.ANY)] * len(extra),
        out_specs=[sem, sem] + [hbm] * (2 * n) + [pl.BlockSpec(memory_space=pltpu.VMEM)], out_shape=out_shape,
        input_output_aliases={i: 2 + i for i in range(2 * n)},
        compiler_params=pltpu.CompilerParams(has_side_effects=pltpu.SideEffectType.DATAFLOW_SIDE_EFFECTING),
    )(*[pltpu.with_memory_space_constraint(b, pltpu.HBM) for b in bufs], *extra)
    return (outs[0], outs[1], outs[2:2 + n], outs[2 + n:2 + 2 * n]), outs[-1]


def _exchange_wait(state, after, gather, name):
    send_sems, recv_sems, srcs, lands = state
    n = len(srcs)
    after = list(after) if isinstance(after, (list, tuple)) else [after]

    def body(*refs):
        ins, lnd = refs[:n], refs[n:2 * n]
        s_sems, r_sems = refs[2 * n], refs[2 * n + 1]
        sends, recvs = _exchange_copies(ins, lnd, s_sems, r_sems, gather)
        for cp in sends:
            cp.wait_send()
        for cp in recvs:
            cp.wait_recv()

    hbm = pl.BlockSpec(memory_space=pltpu.HBM)
    sem = pl.BlockSpec(memory_space=pltpu.SEMAPHORE)
    bufs = list(srcs) + list(lands)
    outs = pl.pallas_call(
        body, name=name, in_specs=[hbm] * (2 * n) + [sem, sem] + [pl.BlockSpec(memory_space=pl.ANY)] * len(after),
        out_specs=[hbm] * (2 * n), out_shape=[pltpu.HBM(b.shape, b.dtype) for b in bufs],
        input_output_aliases={i: i for i in range(2 * n)},
        compiler_params=pltpu.CompilerParams(has_side_effects=pltpu.SideEffectType.DATAFLOW_SIDE_EFFECTING),
    )(*bufs, send_sems, recv_sems, *after)
    return outs[n:]


def _adam_vals(w, g, m, v):
    m = ADAM_B1 * m + (1.0 - ADAM_B1) * g
    v = ADAM_B2 * v + (1.0 - ADAM_B2) * (g * g)
    m_hat = m / ADAM_C1
    v_hat = v / ADAM_C2
    delta = -ADAM_LR * (m_hat / (jnp.sqrt(v_hat) + ADAM_EPS) + ADAM_WD * w)
    return delta, m, v


def _small_allreduce_adam(part, w, m, v, name):
    rows = part.shape[0]

    def body(p_ref, w_ref, m_ref, v_ref, g_ref, d_ref, nm_ref, nv_ref, gath, send_sems, recv_sems):
        pos = _mesh_pos()
        me = 4 * pos[0] + 2 * pos[1] + pos[2]
        gath[me] = p_ref[...]
        sends, recvs = [], []
        for k in range(1, N_DEV):
            peer, pid = _peer(pos, k)
            cp = pltpu.make_async_remote_copy(
                src_ref=p_ref, dst_ref=gath.at[me], send_sem=send_sems.at[k - 1], recv_sem=recv_sems.at[k - 1],
                device_id=peer, device_id_type=pl.DeviceIdType.MESH)
            cp.start()
            sends.append(cp)
            recvs.append(pltpu.make_async_remote_copy(
                src_ref=p_ref, dst_ref=gath.at[pid], send_sem=send_sems.at[k - 1], recv_sem=recv_sems.at[k - 1],
                device_id=peer, device_id_type=pl.DeviceIdType.MESH))
        for cp in recvs:
            cp.wait_recv()
        for cp in sends:
            cp.wait_send()
        g = gath[0]
        for p in range(1, N_DEV):
            g = g + gath[p]
        g_ref[...] = g
        d, nm, nv = _adam_vals(w_ref[...], g, m_ref[...], v_ref[...])
        d_ref[...] = d
        nm_ref[...] = nm
        nv_ref[...] = nv

    vm = pl.BlockSpec(memory_space=pltpu.VMEM)
    sds = jax.ShapeDtypeStruct((rows, LANES), F32)
    return pl.pallas_call(
        body, name=name, in_specs=[vm] * 4, out_specs=[vm] * 4, out_shape=[sds] * 4,
        scratch_shapes=[pltpu.VMEM((N_DEV, rows, LANES), F32), pltpu.SemaphoreType.DMA((N_DEV - 1,)),
                        pltpu.SemaphoreType.DMA((N_DEV - 1,))],
        compiler_params=pltpu.CompilerParams(has_side_effects=True))(part, w, m, v)


def _reduce_adam(recv, w, m, v, name):
    shape = w.shape
    c = shape[-1]
    r = int(np.prod(shape[:-1]))
    recv2, w2, m2, v2 = recv.reshape(N_DEV, r, c), w.reshape(r, c), m.reshape(r, c), v.reshape(r, c)
    tr = r
    while tr * c * 4 > (1 << 20) and tr % 16 == 0:
        tr //= 2

    def body(r_ref, w_ref, m_ref, v_ref, g_ref, d_ref, nm_ref, nv_ref):
        g = r_ref[0].astype(F32)
        for p in range(1, N_DEV):
            g = g + r_ref[p].astype(F32)
        g_ref[...] = g
        d, nm, nv = _adam_vals(w_ref[...], g, m_ref[...], v_ref[...])
        d_ref[...] = d
        nm_ref[...] = nm
        nv_ref[...] = nv

    spec = pl.BlockSpec((tr, c), lambda i: (i, 0))
    sds = jax.ShapeDtypeStruct((r, c), F32)
    outs = pl.pallas_call(
        body, name=name, grid=(r // tr,),
        in_specs=[pl.BlockSpec((N_DEV, tr, c), lambda i: (0, i, 0)), spec, spec, spec],
        out_specs=[spec] * 4, out_shape=[sds] * 4, compiler_params=_cp("parallel"))(recv2, w2, m2, v2)
    return [o.reshape(shape) for o in outs]


def _pad_heads(w, real=HEAD_DIM):
    lead = w.shape[:-1]
    w = w.reshape(lead + (HEADS, real))
    w = jnp.pad(w, [(0, 0)] * len(lead) + [(0, 0), (0, LANES - real)])
    return w.reshape(lead + (HP,))


def _unpad_heads(w, real=HEAD_DIM):
    lead = w.shape[:-1]
    return w.reshape(lead + (HEADS, LANES))[..., :real].reshape(lead + (HEADS * real,))


_IN_SEGS = (("fq", 0, 256), ("fk", 256, 512), ("fv", 512, 768), ("ff", 768, 772), ("cq", 772, 1028),
            ("ckv", 1028, 1156), ("kr", 1156, 1188), ("rq", 1188, 1444), ("rk", 1444, 1700), ("rv", 1700, 1956),
            ("rg", 1956, 2212), ("sq", 2212, 2468), ("sk", 2468, 2724), ("sv", 2724, 2980))


def _pad_w_in(w):
    s = {n: w[:, a:b] for n, a, b in _IN_SEGS}
    rows = w.shape[0]
    z = lambda n: jnp.zeros((rows, n), w.dtype)
    parts = [s[n] for n in ("fq", "fk", "fv", "rq", "rk", "rv", "sq", "sk", "sv", "rg", "cq", "ckv")]
    parts += [z(HEAD_DIM), s["kr"], z(LANES - HEAD_DIM - ROPE_MLA), s["ff"], z(LANES - HEADS),
              z(NP_IN - OFF_FF - LANES)]
    return jnp.concatenate(parts, axis=1)


def _unpad_w_in(wp):
    seg = lambda off, n=GROUP: wp[:, off:off + n]
    parts = [seg(OFF_FOX), seg(OFF_FOX + GROUP), seg(OFF_FOX + 2 * GROUP), seg(OFF_FF, HEADS),
             seg(OFF_CQ, Q_RANK), seg(OFF_CKV, KV_RANK), seg(OFF_KR + HEAD_DIM, ROPE_MLA),
             seg(OFF_RET), seg(OFF_RET + GROUP), seg(OFF_RET + 2 * GROUP), seg(OFF_RG),
             seg(OFF_SB), seg(OFF_SB + GROUP), seg(OFF_SB + 2 * GROUP)]
    return jnp.concatenate(parts, axis=1)


def _pad_w_kv(w):
    w4 = w.reshape(KV_RANK, HEADS, 2 * HEAD_DIM)
    k = w4[:, :, :HEAD_DIM].reshape(KV_RANK, GROUP)
    v = w4[:, :, HEAD_DIM:].reshape(KV_RANK, GROUP)
    return jnp.concatenate([_pad_heads(k), _pad_heads(v)], axis=1)


def _unpad_w_kv(wp):
    k = _unpad_heads(wp[:, :HP]).reshape(KV_RANK, HEADS, HEAD_DIM)
    v = _unpad_heads(wp[:, HP:]).reshape(KV_RANK, HEADS, HEAD_DIM)
    return jnp.concatenate([k, v], axis=-1).reshape(KV_RANK, HEADS * 2 * HEAD_DIM)


def _pad_gain_out(g):
    g = jnp.pad(g.reshape(4 * HEADS, HEAD_DIM), ((0, 0), (0, LANES - HEAD_DIM)))
    return g.reshape(1, 4 * HP)


def _unpad_gain_out(gp):
    return gp.reshape(4 * HEADS, LANES)[:, :HEAD_DIM].reshape(D_MODEL)


_SMALL = (("g_mix_pre", 1024), ("g_mix_post", 1024), ("g_ffn_pre", 1024), ("g_ffn_post", 1024), ("g_mix_out", 1024),
          ("g_q_lora", 256), ("g_kv_lora", 128), ("b_forget", 4))


def _pack_small(vals):
    parts = []
    for name, n in _SMALL:
        a = vals[name].astype(F32)
        if n < LANES:
            a = jnp.pad(a, ((0, 0), (0, LANES - n)))
        parts.append(a)
    return jnp.concatenate(parts, axis=1).reshape(DEPTH * SMALL_ROWS, LANES)


def _unpack_small(packed):
    flat = packed.reshape(DEPTH, SMALL_ROWS * LANES)
    out, off = {}, 0
    for name, n in _SMALL:
        out[name] = flat[:, off:off + n]
        off += max(n, LANES)
    return out


def kernel(x, positions, g_mix_pre, w_in, b_forget, g_q_lora, w_q_up, g_kv_lora, w_kv_up, g_mix_out, w_out, g_mix_post, g_ffn_pre, w_ffn_up, w_ffn_down, g_ffn_post, loss_target, m_g_mix_pre, m_w_in, m_b_forget, m_g_q_lora, m_w_q_up, m_g_kv_lora, m_w_kv_up, m_g_mix_out, m_w_out, m_g_mix_post, m_g_ffn_pre, m_w_ffn_up, m_w_ffn_down, m_g_ffn_post, v_g_mix_pre, v_w_in, v_b_forget, v_g_q_lora, v_w_q_up, v_g_kv_lora, v_w_kv_up, v_g_mix_out, v_w_out, v_g_mix_post, v_g_ffn_pre, v_w_ffn_up, v_w_ffn_down, v_g_ffn_post):
    t = x.shape[1]
    nq = t // BQ
    x0 = x[0]
    tgt = loss_target[0]
    pos = positions[0].astype(F32).reshape(t, 1)

    half_r, half_m = HEAD_DIM // 2, ROPE_MLA // 2
    invf_r = ROPE_BASE ** (-jnp.arange(half_r, dtype=F32) / half_r)
    invf_m = ROPE_BASE ** (-jnp.arange(half_m, dtype=F32) / half_m)
    invf = jnp.concatenate([invf_r, invf_r, invf_m, invf_m,
                            jnp.zeros((LANES - HEAD_DIM - ROPE_MLA,), F32)]).reshape(1, LANES)
    log_gamma = jnp.log1p(-jnp.power(2.0, -5.0 - jnp.arange(HEADS, dtype=F32)))
    lg_lanes = jnp.repeat(log_gamma, LANES).reshape(1, HP)
    lg_heads = jnp.broadcast_to(log_gamma[:, None, None], (HEADS, 1, LANES))

    big = [w_in, w_q_up, w_kv_up, w_out, w_ffn_up, w_ffn_down]
    bf = lambda w: w.astype(BF16)
    first = _exchange([bf(w_in[0]), bf(w_q_up), bf(w_kv_up)], True, "weights_gather_first")
    l0_state, l0_token = _exchange_start([bf(w[0]) for w in (w_out, w_ffn_up, w_ffn_down)], True,
                                         "weights_gather_layer0_start")
    l1_state, rest_token = _exchange_start([bf(w[1]) for w in (w_in, w_out, w_ffn_up, w_ffn_down)], True,
                                           "weights_gather_layer1_start", after=l0_token)
    wq_g = first[1].transpose(1, 2, 0, 3).reshape(DEPTH, Q_RANK, 384)
    wkv_g = first[2].transpose(1, 2, 0, 3).reshape(DEPTH, KV_RANK, 512)

    row = lambda g: g.reshape(1, -1)
    layers = []
    for l in range(DEPTH):
        layers.append(dict(
            wq=_pad_heads(wq_g[l], 96), wkv=_pad_w_kv(wkv_g[l]),
            g_pre=row(g_mix_pre[l]), g_post=row(g_mix_post[l]), g_fpre=row(g_ffn_pre[l]), g_fpost=row(g_ffn_post[l]),
            g_out=_pad_gain_out(g_mix_out[l]), g_q=row(g_q_lora[l]), g_kv=row(g_kv_lora[l]),
            b_pad=jnp.pad(b_forget[l], (0, LANES - HEADS)).reshape(1, LANES)))
    layers[0]["win"] = _pad_w_in(first[0].reshape(D_MODEL, D_IN))

    saved = []
    xin = x0
    h = _rms_fwd(xin, layers[0]["g_pre"] + rest_token[0:1, 0:1], "rms_pre_0")
    loss_row = dx = None
    for l, p in enumerate(layers):
        s = dict(x=xin, h=h)
        proj = _mm(h, p["win"], name=f"in_proj_{l}", tm=512, tn=NP_IN)
        cum = _cumsum(_forget_lsf(proj, p["b_pad"], f"forget_lsf_{l}"), False, f"forget_cumsum_{l}")
        fox, ret, sb, mla, fox_t, sb_t, mla_t, ret_t = _prep_fwd(proj, cum, pos, invf, lg_lanes, p["g_q"],
                                                          p["g_kv"], p["wq"], p["wkv"], f"prep_fwd_{l}")
        oa, lse_a = _softmax_fwd(fox, fox_t, chunk_mask=False, scale=1.0, name=f"fox_fwd_{l}")
        ob, lse_b = _softmax_fwd(mla, mla_t, chunk_mask=True, scale=96.0 ** -0.5, name=f"mla_fwd_{l}")
        oc, ret_st = _ret_fwd(ret, ret_t, lg_heads, f"ret_fwd_{l}")
        od, sb_tot = _sb_fwd(sb, sb_t, f"sb_fwd_{l}")
        if l == 0:
            got = _exchange_wait(l0_state, od, True, "weights_gather_layer0_wait")
            p.update(wout=got[0].reshape(D_MODEL, D_MODEL), wup=got[1].transpose(1, 0, 2).reshape(D_MODEL, D_FF),
                     wdn=got[2].reshape(D_FF, D_MODEL))
        mixed = _post_fwd(oa, ob, oc, od, proj, p["g_out"], f"post_fwd_{l}")
        mix = _mm(mixed, p["wout"], name=f"out_proj_{l}", tm=2048)
        x1, h2 = _add_rms_fwd(xin, mix, p["g_post"], p["g_fpre"], f"mix_residual_{l}")
        a = _mm(h2, p["wup"], name=f"ffn_up_{l}", out_dtype=BF16, tm=2048)
        y = _mm(a, p["wdn"], name=f"ffn_down_{l}", a_fn=_relu2, tk=D_FF)
        s.update(proj=proj, fox=fox, ret=ret, sb=sb, mla=mla, fox_t=fox_t, sb_t=sb_t, mla_t=mla_t, ret_st=ret_st, oa=oa, ob=ob, oc=oc,
                 od=od, sb_tot=sb_tot, lse_a=lse_a, lse_b=lse_b, mixed=mixed, mix=mix, x1=x1, h2=h2, a=a, y=y)
        saved.append(s)
        if l == 0:
            got = _exchange_wait(l1_state, y, True, "weights_gather_layer1_wait")
            layers[1].update(win=_pad_w_in(got[0].reshape(D_MODEL, D_IN)), wout=got[1].reshape(D_MODEL, D_MODEL),
                             wup=got[2].transpose(1, 0, 2).reshape(D_MODEL, D_FF), wdn=got[3].reshape(D_FF, D_MODEL))
        if l + 1 < DEPTH:
            xin, h = _add_rms_fwd(x1, y, p["g_fpost"], layers[l + 1]["g_pre"], f"ffn_residual_{l}")
        else:
            loss_row, dx = _final_loss(x1, y, p["g_fpost"], tgt, "loss")

    small_g = {n: [None] * DEPTH for n, _ in _SMALL}
    big_g = [[None] * DEPTH for _ in range(6)]
    to_send = [
        lambda g: g.reshape(N_DEV, 1, D_MODEL // N_DEV, D_IN),
        lambda g: g.reshape(Q_RANK, N_DEV, 384 // N_DEV).transpose(1, 0, 2)[:, None],
        lambda g: g.reshape(KV_RANK, N_DEV, 512 // N_DEV).transpose(1, 0, 2)[:, None],
        lambda g: g.reshape(N_DEV, 1, D_MODEL // N_DEV, D_MODEL),
        lambda g: g.reshape(D_MODEL, N_DEV, D_FF // N_DEV).transpose(1, 0, 2)[:, None],
        lambda g: g.reshape(N_DEV, 1, D_FF // N_DEV, D_MODEL),
    ]
    send_of = lambda ks, l: [to_send[k](big_g[k][l]).astype(BF16) for k in ks]
    late_state = early_state = None
    order_token = jnp.zeros((1, 1), F32)
    for l in reversed(range(DEPTH)):
        p, s = layers[l], saved[l]
        dy, dg = _norm_bwd(dx, s["y"], p["g_fpost"] + order_token, None, BF16, f"ffn_post_bwd_{l}")
        small_g["g_ffn_post"][l] = dg
        da = _mm(dy, p["wdn"], name=f"ffn_down_dx_{l}", tb=True, out_dtype=BF16, epi=_drelu2, epi_in=s["a"],
                 tm=2048)
        big_g[5][l] = _mm(s["a"], dy, name=f"ffn_down_dw_{l}", ta=True, a_fn=_relu2, tk=t, out_dtype=BF16)
        big_g[4][l] = _mm(s["h2"], da, name=f"ffn_up_dw_{l}", ta=True, tk=t, out_dtype=BF16)
        dh2 = _mm(da, p["wup"], name=f"ffn_up_dx_{l}", tb=True, tk=D_FF)
        dx1, dg = _norm_bwd(dh2, s["x1"], p["g_fpre"], dx, F32, f"ffn_pre_bwd_{l}")
        small_g["g_ffn_pre"][l] = dg
        dmix, dg = _norm_bwd(dx1, s["mix"], p["g_post"], None, BF16, f"mix_post_bwd_{l}")
        small_g["g_mix_post"][l] = dg
        dmixed = _mm(dmix, p["wout"], name=f"out_proj_dx_{l}", tb=True, tm=2048)
        big_g[3][l] = _mm(s["mixed"], dmix, name=f"out_proj_dw_{l}", ta=True, tk=t, out_dtype=BF16)
        g_out = p["g_out"]
        if l == 0:
            early_state, early_token = _exchange_start(send_of((3, 4, 5), 0), False, "grads_layer0_early_start")
            g_out = g_out + early_token[0:1, 0:1]
        doa, dob, doc, dod, dla, dlb, drg, dgo = _post_bwd(dmixed, s["oa"], s["ob"], s["oc"], s["od"], s["proj"],
                                                           g_out, f"post_bwd_{l}")
        small_g["g_mix_out"][l] = _unpad_gain_out(dgo).reshape(1, D_MODEL)
        as_rows = lambda a: a[:, :HEADS].T.reshape(HEADS, t // BQA, 1, BQA)
        dfq, dfk, dfv, dcum_k, dcum_q = _softmax_bwd(s["fox"], s["fox_t"], doa, s["lse_a"], as_rows(dla), bias=True,
                                                     chunk_mask=False, scale=1.0, name=f"fox_bwd_{l}")
        dmq, dmk, dmv = _softmax_bwd(s["mla"], s["mla_t"], dob, s["lse_b"], as_rows(dlb), bias=False, chunk_mask=True,
                                     scale=96.0 ** -0.5, name=f"mla_bwd_{l}")
        drq, drk, drv = _ret_bwd(s["ret"], s["ret_st"], lg_heads, doc, f"ret_bwd_{l}")
        dsq, dsk, dsv = _sb_bwd(s["sb"], s["sb_t"], dod, s["sb_tot"], f"sb_bwd_{l}")
        dcum_q = jnp.pad(dcum_q.reshape(HEADS, t).T, ((0, 0), (0, LANES - HEADS)))
        dlsf = _cumsum(dcum_q, True, f"forget_cumsum_bwd_{l}", partials=dcum_k)
        dproj, dwq, dwkv, dgq, dgkv, dbf = _prep_bwd(
            (dfq, dfk, dfv), (drq, drk, drv), (dsq, dsk, dsv), (dmq, dmk, dmv), drg, dlsf, s["proj"], pos, invf,
            lg_lanes, p["b_pad"], p["g_q"], p["g_kv"], p["wq"], p["wkv"], f"prep_bwd_{l}")
        small_g["g_q_lora"][l] = dgq
        small_g["g_kv_lora"][l] = dgkv
        small_g["b_forget"][l] = dbf[:, :HEADS]
        big_g[1][l] = _unpad_heads(dwq, 96)
        big_g[2][l] = _unpad_w_kv(dwkv)
        dh = _mm(dproj, p["win"], name=f"in_proj_dx_{l}", tb=True, tk=NP_IN)
        big_g[0][l] = _unpad_w_in(_mm(s["h"], dproj, name=f"in_proj_dw_{l}", ta=True, tn=NP_IN // 2, tk=2048,
                                          out_dtype=BF16))
        g_pre = p["g_pre"]
        if l == 0:
            last_state, last_token = _exchange_start(send_of((0, 1, 2), 0), False, "grads_layer0_rest_start")
            g_pre = g_pre + last_token[0:1, 0:1]
        dx, dg = _norm_bwd(dh, s["x"], g_pre, dx1, F32, f"mix_pre_bwd_{l}")
        small_g["g_mix_pre"][l] = dg
        if l == DEPTH - 1:
            late_state, late_token = _exchange_start(send_of(range(6), l), False, "grads_layer1_start")
            order_token = late_token[0:1, 0:1]
    grad_x = dx.reshape(1, t, D_MODEL)

    res = {}
    small_w = dict(g_mix_pre=g_mix_pre, g_mix_post=g_mix_post, g_ffn_pre=g_ffn_pre, g_ffn_post=g_ffn_post,
                   g_mix_out=g_mix_out, g_q_lora=g_q_lora, g_kv_lora=g_kv_lora, b_forget=b_forget)
    small_m = dict(g_mix_pre=m_g_mix_pre, g_mix_post=m_g_mix_post, g_ffn_pre=m_g_ffn_pre, g_ffn_post=m_g_ffn_post,
                   g_mix_out=m_g_mix_out, g_q_lora=m_g_q_lora, g_kv_lora=m_g_kv_lora, b_forget=m_b_forget)
    small_v = dict(g_mix_pre=v_g_mix_pre, g_mix_post=v_g_mix_post, g_ffn_pre=v_g_ffn_pre, g_ffn_post=v_g_ffn_post,
                   g_mix_out=v_g_mix_out, g_q_lora=v_g_q_lora, g_kv_lora=v_g_kv_lora, b_forget=v_b_forget)
    n_small = DEPTH * SMALL_ROWS
    extra = lambda a: jnp.concatenate([a, jnp.zeros((8, LANES), F32)], axis=0)
    part = jnp.concatenate([_pack_small({n: jnp.concatenate(small_g[n], axis=0) for n, _ in _SMALL}),
                            jnp.broadcast_to(loss_row, (8, LANES))], axis=0)
    sres = _small_allreduce_adam(part, extra(_pack_small(small_w)), extra(_pack_small(small_m)),
                                 extra(_pack_small(small_v)), "small_allreduce_adamw")
    loss = sres[0][n_small, 0]
    sg, sd, sm, sv = [_unpack_small(a[:n_small]) for a in sres]
    for n, _ in _SMALL:
        res[n] = [sg[n], sd[n], sm[n], sv[n]]

    late = _exchange_wait(late_state, dx, False, "grads_layer1_wait")
    early = _exchange_wait(early_state, dx, False, "grads_layer0_early_wait")
    ms = [m_w_in, m_w_q_up, m_w_kv_up, m_w_out, m_w_ffn_up, m_w_ffn_down]
    vs = [v_w_in, v_w_q_up, v_w_kv_up, v_w_out, v_w_ffn_up, v_w_ffn_down]
    names = ["w_in", "w_q_up", "w_kv_up", "w_out", "w_ffn_up", "w_ffn_down"]
    for k in (3, 4, 5):
        recv = jnp.concatenate([early[k - 3], late[k]], axis=1)
        res[names[k]] = _reduce_adam(recv, big[k], ms[k], vs[k], f"adamw_{names[k]}")
    done = [sres[0]] + [res[names[k]][0] for k in (3, 4, 5)]
    last = _exchange_wait(last_state, done, False, "grads_layer0_rest_wait")
    for k in (0, 1, 2):
        recv = jnp.concatenate([last[k], late[k]], axis=1)
        res[names[k]] = _reduce_adam(recv, big[k], ms[k], vs[k], f"adamw_{names[k]}")

    order = ["g_mix_pre", "w_in", "b_forget", "g_q_lora", "w_q_up", "g_kv_lora", "w_kv_up", "g_mix_out", "w_out",
             "g_mix_post", "g_ffn_pre", "w_ffn_up", "w_ffn_down", "g_ffn_post"]
    outs = [loss, grad_x]
    for idx in range(4):
        outs += [res[n][idx] for n in order]
    return tuple(outs)
```

```python
import functools
import math

import numpy as np
import jax
import jax.numpy as jnp
from jax import lax
from jax.experimental import pallas as pl
from jax.experimental.pallas import tpu as pltpu

F32 = jnp.float32
BF16 = jnp.bfloat16

D_MODEL = 1024
DEPTH = 2
N_DEV = 8
GROUP = 256
HEADS = 4
HEAD_DIM = 64
LANES = 128
HP = HEADS * LANES
QKV = 3 * HP
Q_RANK = 256
KV_RANK = 128
ROPE_MLA = 32
D_FF = 4096
D_IN = 2980
CHUNK_SHIFT = 6
EPS = 1e-6
ROPE_BASE = 10000.0
NEG = -1e30

QKV_IN = 3 * GROUP
OFF_FOX, OFF_RET, OFF_SB = 0, QKV_IN, 2 * QKV_IN
OFF_RG = 3 * QKV_IN
OFF_CQ = OFF_RG + GROUP
OFF_CKV = OFF_CQ + Q_RANK
OFF_KR = OFF_CKV + LANES
OFF_FF = OFF_KR + LANES
NP_IN = 3328

BQ = 256
TKS = 128
TM = 256
VMEM_LIMIT = 58 * 1024 * 1024

ADAM_LR, ADAM_B1, ADAM_B2, ADAM_EPS, ADAM_WD, ADAM_STEP = 0.001, 0.9, 0.999, 1e-08, 0.01, 10
ADAM_C1 = 1.0 - ADAM_B1 ** ADAM_STEP
ADAM_C2 = 1.0 - ADAM_B2 ** ADAM_STEP

SMALL_ROWS = 44

NT = (((1,), (1,)), ((), ()))
TN = (((0,), (0,)), ((), ()))


def _cp(*sem):
    return pltpu.CompilerParams(dimension_semantics=sem if sem else None, vmem_limit_bytes=VMEM_LIMIT)


def _bdot(a, b, dn=None):
    if dn is None:
        return jnp.dot(a, b, preferred_element_type=F32)
    return lax.dot_general(a, b, dn, preferred_element_type=F32)


def _split2(x):
    hi = x.astype(BF16)
    lo = (x - hi.astype(F32)).astype(BF16)
    return hi, lo


def _mm(a, b, *, name, ta=False, tb=False, out_dtype=F32, a_fn=None, epi=None, epi_in=None,
        tm=1024, tn=1024, tk=1024):
    m, k = (a.shape[1], a.shape[0]) if ta else a.shape
    n = b.shape[0] if tb else b.shape[1]
    tm, tn, tk = min(tm, m), min(tn, n), min(tk, k)
    assert m % tm == 0 and n % tn == 0 and k % tk == 0, (name, m, n, k)
    nk = k // tk
    dn = (((0 if ta else 1,), (1 if tb else 0,)), ((), ()))

    def body(*refs):
        if epi is None:
            a_ref, b_ref, o_ref = refs[:3]
            e_ref = None
            rest = refs[3:]
        else:
            a_ref, b_ref, e_ref, o_ref = refs[:4]
            rest = refs[4:]
        av = a_ref[...]
        if a_fn is not None:
            av = a_fn(av)
        part = lax.dot_general(av.astype(BF16), b_ref[...].astype(BF16), dn, preferred_element_type=F32)

        def finish(r):
            if epi is not None:
                r = epi(r, e_ref[...])
            o_ref[...] = r.astype(out_dtype)

        if nk == 1:
            finish(part)
        else:
            acc_ref = rest[0]
            kk = pl.program_id(2)

            @pl.when(kk == 0)
            def _():
                acc_ref[...] = part

            @pl.when(kk > 0)
            def _():
                acc_ref[...] += part

            @pl.when(kk == nk - 1)
            def _():
                finish(acc_ref[...])

    a_spec = pl.BlockSpec((tk, tm), lambda i, j, kk: (kk, i)) if ta else pl.BlockSpec((tm, tk), lambda i, j, kk: (i, kk))
    b_spec = pl.BlockSpec((tn, tk), lambda i, j, kk: (j, kk)) if tb else pl.BlockSpec((tk, tn), lambda i, j, kk: (kk, j))
    o_spec = pl.BlockSpec((tm, tn), lambda i, j, kk: (i, j))
    in_specs = [a_spec, b_spec]
    args = [a, b]
    if epi is not None:
        in_specs.append(o_spec)
        args.append(epi_in)
    return pl.pallas_call(
        body, name=name, grid=(m // tm, n // tn, nk),
        in_specs=in_specs, out_specs=o_spec,
        out_shape=jax.ShapeDtypeStruct((m, n), out_dtype),
        scratch_shapes=[pltpu.VMEM((tm, tn), F32)] if nk > 1 else [],
        compiler_params=_cp("parallel", "parallel", "arbitrary"),
    )(*args)


def _relu2(v):
    r = jnp.maximum(v, 0.0)
    return r * r


def _drelu2(du, av):
    return du * (2.0 * jnp.maximum(av, 0.0))


def _rms(v, g):
    r = lax.rsqrt(jnp.mean(v * v, axis=-1, keepdims=True) + EPS)
    return v * r * g


def _row_spec(w):
    return pl.BlockSpec((TM, w), lambda i: (i, 0))


def _vec_spec(w):
    return pl.BlockSpec((1, w), lambda i: (0, 0))


def _rms_fwd(x, g, name):
    t, d = x.shape

    def body(x_ref, g_ref, h_ref):
        h_ref[...] = _rms(x_ref[...], g_ref[...]).astype(BF16)

    return pl.pallas_call(
        body, name=name, grid=(t // TM,), in_specs=[_row_spec(d), _vec_spec(d)], out_specs=_row_spec(d),
        out_shape=jax.ShapeDtypeStruct((t, d), BF16), compiler_params=_cp("parallel"))(x, g)


def _add_rms_fwd(x, y, g1, g2, name):
    t, d = x.shape

    def body(x_ref, y_ref, g1_ref, g2_ref, xn_ref, h_ref):
        xn = x_ref[...] + _rms(y_ref[...], g1_ref[...])
        xn_ref[...] = xn
        h_ref[...] = _rms(xn, g2_ref[...]).astype(BF16)

    return pl.pallas_call(
        body, name=name, grid=(t // TM,),
        in_specs=[_row_spec(d), _row_spec(d), _vec_spec(d), _vec_spec(d)],
        out_specs=[_row_spec(d), _row_spec(d)],
        out_shape=[jax.ShapeDtypeStruct((t, d), F32), jax.ShapeDtypeStruct((t, d), BF16)],
        compiler_params=_cp("parallel"))(x, y, g1, g2)


def _final_loss(x, y, g, tgt, name):
    t, d = x.shape

    def body(x_ref, y_ref, g_ref, t_ref, l_ref, dx_ref):
        @pl.when(pl.program_id(0) == 0)
        def _():
            l_ref[...] = jnp.zeros_like(l_ref)

        err = x_ref[...] + _rms(y_ref[...], g_ref[...]) - t_ref[...]
        dx_ref[...] = err * (1.0 / d)
        l_ref[...] += jnp.sum(jnp.sum(err * err, axis=1, keepdims=True), axis=0, keepdims=True) * (0.5 / d)

    return pl.pallas_call(
        body, name=name, grid=(t // TM,),
        in_specs=[_row_spec(d), _row_spec(d), _vec_spec(d), _row_spec(d)],
        out_specs=[pl.BlockSpec((1, LANES), lambda i: (0, 0)), _row_spec(d)],
        out_shape=[jax.ShapeDtypeStruct((1, LANES), F32), jax.ShapeDtypeStruct((t, d), F32)],
        compiler_params=_cp("arbitrary"))(x, y, g, tgt)


def _rms_bwd_vals(dn, v, g):
    w = v.shape[-1]
    r = lax.rsqrt(jnp.mean(v * v, axis=-1, keepdims=True) + EPS)
    vh = v * r
    dgp = jnp.sum(dn * vh, axis=0, keepdims=True)
    dvh = dn * g
    dv = r * (dvh - vh * (jnp.sum(dvh * vh, axis=-1, keepdims=True) * (1.0 / w)))
    return dv, dgp


def _norm_bwd(dn, v, g, resid, out_dtype, name):
    t, d = v.shape
    has_res = resid is not None

    def body(*refs):
        if has_res:
            dn_ref, v_ref, g_ref, r_ref, dv_ref, dg_ref = refs
        else:
            dn_ref, v_ref, g_ref, dv_ref, dg_ref = refs

        @pl.when(pl.program_id(0) == 0)
        def _():
            dg_ref[...] = jnp.zeros_like(dg_ref)

        dv, dgp = _rms_bwd_vals(dn_ref[...].astype(F32), v_ref[...], g_ref[...])
        if has_res:
            dv = dv + r_ref[...]
        dv_ref[...] = dv.astype(out_dtype)
        dg_ref[...] += dgp

    in_specs = [_row_spec(d), _row_spec(d), _vec_spec(d)] + ([_row_spec(d)] if has_res else [])
    args = [dn, v, g] + ([resid] if has_res else [])
    return pl.pallas_call(
        body, name=name, grid=(t // TM,), in_specs=in_specs,
        out_specs=[_row_spec(d), _vec_spec(d)],
        out_shape=[jax.ShapeDtypeStruct((t, d), out_dtype), jax.ShapeDtypeStruct((1, d), F32)],
        compiler_params=_cp("arbitrary"))(*args)


def _rope_trig(pos, invf):
    ang = pos * invf
    return jnp.cos(ang), jnp.sin(ang)


def _rope_tables(trig, w, lo, half):
    c, s = trig
    lane = lax.broadcasted_iota(jnp.int32, c.shape, 1)
    active = (lane >= lo) & (lane < lo + 2 * half)
    cos = jnp.concatenate([jnp.where(active, c, 1.0)] * (w // LANES), axis=1)
    sin = jnp.concatenate([jnp.where(active, s, 0.0)] * (w // LANES), axis=1)
    lanes = lax.broadcasted_iota(jnp.int32, (c.shape[0], w), 1) & (LANES - 1)
    first = (lanes >= lo) & (lanes < lo + half)
    second = (lanes >= lo + half) & (lanes < lo + 2 * half)
    return cos, sin, first, second


def _rope_apply(v, cos, sin, first, second, half, sign):
    w = v.shape[-1]
    up = pltpu.roll(v, w - half, 1)
    dn = pltpu.roll(v, half, 1)
    rot = jnp.where(first, -up, jnp.where(second, dn, 0.0))
    return v * cos + rot * (sin * sign)


def _split3(c):
    hi = c.astype(BF16).astype(F32)
    mid = (c - hi).astype(BF16).astype(F32)
    return hi, mid, (c - hi) - mid


def _spread_heads(x):
    low = lax.broadcasted_iota(jnp.int32, (x.shape[0], LANES), 1) < HEAD_DIM
    out = []
    for c in range(x.shape[1] // LANES):
        blk = x[:, c * LANES:(c + 1) * LANES]
        out.append(jnp.where(low, blk, 0.0))
        out.append(jnp.where(low, pltpu.roll(blk, HEAD_DIM, 1), 0.0))
    return jnp.concatenate(out, axis=1)


def _gather_heads(y):
    low = lax.broadcasted_iota(jnp.int32, (y.shape[0], LANES), 1) < HEAD_DIM
    out = []
    for c in range(y.shape[1] // (2 * LANES)):
        a = y[:, 2 * c * LANES:(2 * c + 1) * LANES]
        b = y[:, (2 * c + 1) * LANES:(2 * c + 2) * LANES]
        out.append(jnp.where(low, a, pltpu.roll(b, HEAD_DIM, 1)))
    return jnp.concatenate(out, axis=1)


def _transposed_tiles(dst, row0, blocks, width):
    for b, blk in enumerate(blocks):
        bt = blk.T.astype(BF16)
        rows = bt.shape[0]
        for w in range(TM // width):
            dst[w, row0 + b * rows:row0 + (b + 1) * rows, :] = bt[:, w * width:(w + 1) * width]


def _prep_fwd(proj, cum, pos, invf, lg_lanes, g_q, g_kv, wq_pad, wkv_pad, name):
    t = proj.shape[0]

    def body(fox_ref, ret_ref, sb_ref, cq_ref, ckv_ref, kr_ref, cum_ref, pos_ref, invf_ref, lg_ref,
             gq_ref, gkv_ref, wq_ref, wkv_ref,
             ofox_ref, oret_ref, osb_ref, omla_ref, ofoxt_ref, osbt_ref, omlat_ref, orett_ref):
        pos_v = pos_ref[...]
        foxv, retv, sbv = [[_spread_heads(ref[:, g * GROUP:(g + 1) * GROUP]) for g in range(3)]
                           for ref in (fox_ref, ret_ref, sb_ref)]
        osb_ref[:, 0:HP] = (sbv[0] * 0.125).astype(BF16)
        osb_ref[:, HP:2 * HP] = sbv[1].astype(BF16)
        osb_ref[:, 2 * HP:QKV] = sbv[2].astype(BF16)
        _transposed_tiles(osbt_ref, 0, [sbv[1], sbv[2]], TKS)
        lane = lax.broadcasted_iota(jnp.int32, (TM, LANES), 1)
        cumv = cum_ref[...]
        fq, fk = [], []
        for hb in range(HEADS):
            hi, mid, lo = _split3(cumv[:, hb:hb + 1])
            q = foxv[0][:, hb * LANES:(hb + 1) * LANES] * 0.125
            k = foxv[1][:, hb * LANES:(hb + 1) * LANES]
            ones_q = (lane >= HEAD_DIM) & (lane < HEAD_DIM + 3)
            ones_k = (lane >= HEAD_DIM + 3) & (lane < HEAD_DIM + 6)
            q = jnp.where(ones_q, 1.0, q)
            k = jnp.where(ones_k, 1.0, k)
            for n, part in enumerate((hi, mid, lo)):
                q = jnp.where(lane == HEAD_DIM + 3 + n, part, q)
                k = jnp.where(lane == HEAD_DIM + n, -part, k)
            fq.append(q)
            fk.append(k)
        fk = jnp.concatenate(fk, axis=1)
        ofox_ref[:, 0:HP] = jnp.concatenate(fq, axis=1).astype(BF16)
        ofox_ref[:, HP:2 * HP] = fk.astype(BF16)
        ofox_ref[:, 2 * HP:QKV] = foxv[2].astype(BF16)
        _transposed_tiles(ofoxt_ref, 0, [fk, foxv[2]], BQ)
        trig = _rope_trig(pos_v, invf_ref[...])
        cos, sin, first, second = _rope_tables(trig, HP, 0, HEAD_DIM // 2)
        nloc = lax.broadcasted_iota(jnp.int32, (TM, 1), 0).astype(F32)
        dec = lg_ref[...] * nloc
        rq = _rope_apply(retv[0], cos, sin, first, second, HEAD_DIM // 2, 1.0)
        rk = _rope_apply(retv[1], cos, sin, first, second, HEAD_DIM // 2, 1.0)
        oret_ref[:, 0:HP] = (rq * jnp.exp(dec)).astype(BF16)
        rk = rk * 0.125 * jnp.exp(-dec)
        oret_ref[:, HP:2 * HP] = rk.astype(BF16)
        _transposed_tiles(orett_ref, 0, [rk], BQ)
        oret_ref[:, 2 * HP:QKV] = retv[2].astype(BF16)
        cosm, sinm, firstm, secondm = _rope_tables(trig, HP, HEAD_DIM, ROPE_MLA // 2)
        cqn = _rms(cq_ref[...], gq_ref[...]).astype(BF16)
        qm = _bdot(cqn, wq_ref[...])
        omla_ref[:, 0:HP] = _rope_apply(qm, cosm, sinm, firstm, secondm, ROPE_MLA // 2, 1.0).astype(BF16)
        ckvn = _rms(ckv_ref[...], gkv_ref[...]).astype(BF16)
        kv = _bdot(ckvn, wkv_ref[...])
        krr = _rope_apply(kr_ref[...], cosm[:, 0:LANES], sinm[:, 0:LANES], firstm[:, 0:LANES],
                          secondm[:, 0:LANES], ROPE_MLA // 2, 1.0)
        mk = kv[:, 0:HP] + jnp.concatenate([krr] * HEADS, axis=1)
        omla_ref[:, HP:2 * HP] = mk.astype(BF16)
        omla_ref[:, 2 * HP:QKV] = kv[:, HP:2 * HP].astype(BF16)
        _transposed_tiles(omlat_ref, 0, [mk, kv[:, HP:2 * HP]], BQ)

    def seg(off, w):
        return pl.BlockSpec((TM, w), lambda i, o=off // w: (i, o))

    def full(shape):
        return pl.BlockSpec(shape, lambda i: (0,) * len(shape))

    def tiles(width):
        return pl.BlockSpec((TM // width, 2 * HP, width), lambda i: (i, 0, 0))

    in_specs = [seg(OFF_FOX, QKV_IN), seg(OFF_RET, QKV_IN), seg(OFF_SB, QKV_IN), seg(OFF_CQ, Q_RANK),
                seg(OFF_CKV, LANES), seg(OFF_KR, LANES), _row_spec(LANES), pl.BlockSpec((TM, 1), lambda i: (i, 0)),
                full((1, LANES)), full((1, HP)), full((1, Q_RANK)), full((1, KV_RANK)),
                full((Q_RANK, HP)), full((KV_RANK, 2 * HP))]
    out_specs = [_row_spec(QKV)] * 4 + [tiles(BQ), tiles(TKS), tiles(BQ),
                                        pl.BlockSpec((1, HP, BQ), lambda i: (i, 0, 0))]
    out_shape = [jax.ShapeDtypeStruct((t, QKV), BF16)] * 4 + [
        jax.ShapeDtypeStruct((t // BQ, 2 * HP, BQ), BF16), jax.ShapeDtypeStruct((t // TKS, 2 * HP, TKS), BF16),
        jax.ShapeDtypeStruct((t // BQ, 2 * HP, BQ), BF16), jax.ShapeDtypeStruct((t // BQ, HP, BQ), BF16)]
    return pl.pallas_call(
        body, name=name, grid=(t // TM,), in_specs=in_specs, out_specs=out_specs, out_shape=out_shape,
        compiler_params=_cp("parallel"))(proj, proj, proj, proj, proj, proj, cum, pos, invf, lg_lanes,
                                         g_q, g_kv, wq_pad, wkv_pad)


def _prep_bwd(dfox, dret, dsb, dmla, drg, dlsf, proj, pos, invf, lg_lanes, b_pad, g_q, g_kv,
              wq_pad, wkv_pad, name):
    t = proj.shape[0]

    def body(dfq, dfk, dfv, drq, drk, drv, dsq, dsk, dsv, dmq, dmk, dmv, drg_ref, dlsf_ref,
             cq_ref, ckv_ref, ff_ref, pos_ref, invf_ref, lg_ref, b_ref, gq_ref, gkv_ref, wq_ref, wkv_ref,
             dp_ref, dwq_ref, dwkv_ref, dgq_ref, dgkv_ref, dbf_ref):
        @pl.when(pl.program_id(0) == 0)
        def _():
            dwq_ref[...] = jnp.zeros_like(dwq_ref)
            dwkv_ref[...] = jnp.zeros_like(dwkv_ref)
            dgq_ref[...] = jnp.zeros_like(dgq_ref)
            dgkv_ref[...] = jnp.zeros_like(dgkv_ref)
            dbf_ref[...] = jnp.zeros_like(dbf_ref)

        pos_v = pos_ref[...]
        def put(off, val):
            dp_ref[:, off:off + GROUP] = _gather_heads(val).astype(BF16)

        for off, (dq, dk, dv) in ((OFF_FOX, (dfq, dfk, dfv)), (OFF_SB, (dsq, dsk, dsv))):
            put(off, dq[...] * 0.125)
            put(off + GROUP, dk[...])
            put(off + 2 * GROUP, dv[...])
        trig = _rope_trig(pos_v, invf_ref[...])
        cos, sin, first, second = _rope_tables(trig, HP, 0, HEAD_DIM // 2)
        nloc = lax.broadcasted_iota(jnp.int32, (TM, 1), 0).astype(F32)
        dec = lg_ref[...] * nloc
        dq = _rope_apply(drq[...] * jnp.exp(dec), cos, sin, first, second, HEAD_DIM // 2, -1.0)
        dk = _rope_apply(drk[...] * (0.125 * jnp.exp(-dec)), cos, sin, first, second, HEAD_DIM // 2, -1.0)
        put(OFF_RET, dq)
        put(OFF_RET + GROUP, dk)
        put(OFF_RET + 2 * GROUP, drv[...])
        put(OFF_RG, drg_ref[...])
        cosm, sinm, firstm, secondm = _rope_tables(trig, HP, HEAD_DIM, ROPE_MLA // 2)
        dql = _rope_apply(dmq[...], cosm, sinm, firstm, secondm, ROPE_MLA // 2, -1.0).astype(BF16)
        cq = cq_ref[...]
        cqn = _rms(cq, gq_ref[...]).astype(BF16)
        dwq_ref[...] += _bdot(cqn, dql, TN)
        dcqn = _bdot(dql, wq_ref[...], NT)
        dcq, dgq = _rms_bwd_vals(dcqn, cq, gq_ref[...])
        dgq_ref[...] += dgq
        dp_ref[:, OFF_CQ:OFF_CQ + Q_RANK] = dcq.astype(BF16)
        dkm = dmk[...]
        dkv = jnp.concatenate([dkm, dmv[...]], axis=1).astype(BF16)
        ckv = ckv_ref[...]
        ckvn = _rms(ckv, gkv_ref[...]).astype(BF16)
        dwkv_ref[...] += _bdot(ckvn, dkv, TN)
        dckvn = _bdot(dkv, wkv_ref[...], NT)
        dckv, dgkv = _rms_bwd_vals(dckvn, ckv, gkv_ref[...])
        dgkv_ref[...] += dgkv
        dp_ref[:, OFF_CKV:OFF_CKV + LANES] = dckv.astype(BF16)
        dkr = dkm[:, 0:LANES] + dkm[:, LANES:2 * LANES] + dkm[:, 2 * LANES:3 * LANES] + dkm[:, 3 * LANES:HP]
        act = firstm[:, 0:LANES] | secondm[:, 0:LANES]
        dkr = jnp.where(act, dkr, 0.0)
        dkr = _rope_apply(dkr, cosm[:, 0:LANES], sinm[:, 0:LANES], firstm[:, 0:LANES], secondm[:, 0:LANES],
                          ROPE_MLA // 2, -1.0)
        dp_ref[:, OFF_KR:OFF_KR + LANES] = dkr.astype(BF16)
        f = ff_ref[...] + b_ref[...]
        dff = dlsf_ref[...] / (1.0 + jnp.exp(f))
        dbf_ref[...] += jnp.sum(dff, axis=0, keepdims=True)
        dp_ref[:, OFF_FF:OFF_FF + LANES] = dff.astype(BF16)
        dp_ref[:, OFF_FF + LANES:NP_IN] = jnp.zeros((TM, NP_IN - OFF_FF - LANES), BF16)

    def seg(off, w):
        return pl.BlockSpec((TM, w), lambda i, o=off // w: (i, o))

    def full(shape):
        return pl.BlockSpec(shape, lambda i: (0,) * len(shape))

    hp_spec = _row_spec(HP)
    in_specs = [hp_spec] * 13 + [_row_spec(LANES), seg(OFF_CQ, Q_RANK), seg(OFF_CKV, LANES), seg(OFF_FF, LANES),
                                 pl.BlockSpec((TM, 1), lambda i: (i, 0)),
                                 full((1, LANES)), full((1, HP)), full((1, LANES)), full((1, Q_RANK)),
                                 full((1, KV_RANK)), full((Q_RANK, HP)), full((KV_RANK, 2 * HP))]
    out_specs = [_row_spec(NP_IN), full((Q_RANK, HP)), full((KV_RANK, 2 * HP)), full((1, Q_RANK)),
                 full((1, KV_RANK)), full((1, LANES))]
    out_shape = [jax.ShapeDtypeStruct((t, NP_IN), BF16), jax.ShapeDtypeStruct((Q_RANK, HP), F32),
                 jax.ShapeDtypeStruct((KV_RANK, 2 * HP), F32), jax.ShapeDtypeStruct((1, Q_RANK), F32),
                 jax.ShapeDtypeStruct((1, KV_RANK), F32), jax.ShapeDtypeStruct((1, LANES), F32)]
    return pl.pallas_call(
        body, name=name, grid=(t // TM,), in_specs=in_specs, out_specs=out_specs, out_shape=out_shape,
        compiler_params=_cp("arbitrary"))(*dfox, *dret, *dsb, *dmla, drg, dlsf, proj, proj, proj, pos, invf,
                                          lg_lanes, b_pad, g_q, g_kv, wq_pad, wkv_pad)


TC = 1024


def _cumsum(x, reverse, name, partials=None, forget_bias=None):
    t = x.shape[0]
    w = LANES
    tc = min(TC, t)
    n = t // tc
    xs = [x] + ([] if partials is None else [partials]) + ([] if forget_bias is None else [forget_bias])

    def body(*refs):
        x_refs, o_ref, carry = refs[:len(xs)], refs[len(xs)], refs[len(xs) + 1]

        @pl.when(pl.program_id(0) == 0)
        def _():
            carry[...] = jnp.zeros_like(carry)

        r = lax.broadcasted_iota(jnp.int32, (tc, tc), 0)
        c = lax.broadcasted_iota(jnp.int32, (tc, tc), 1)
        tri = jnp.where((r <= c) if reverse else (r >= c), 1.0, 0.0).astype(BF16)
        v = x_refs[0][...]
        if forget_bias is not None:
            f = v + x_refs[-1][...]
            v = -(jnp.maximum(-f, 0.0) + jnp.log(1.0 + jnp.exp(-jnp.abs(f))))
        if partials is not None:
            lane = lax.broadcasted_iota(jnp.int32, (tc, LANES), 1)
            for hb in range(HEADS):
                v = v + jnp.where(lane == hb, jnp.sum(x_refs[1][:, _hs(hb)], axis=1, keepdims=True), 0.0)
        hi = v.astype(BF16)
        r1 = v - hi.astype(F32)
        mid = r1.astype(BF16)
        lo = (r1 - mid.astype(F32)).astype(BF16)
        cs = _bdot(tri, hi) + _bdot(tri, mid) + _bdot(tri, lo) + carry[...]
        o_ref[...] = cs
        carry[...] = cs[0:1, :] if reverse else cs[tc - 1:tc, :]

    step = (lambda i: n - 1 - i) if reverse else (lambda i: i)
    x_col = OFF_FF // LANES if forget_bias is not None else 0
    in_specs = [pl.BlockSpec((tc, LANES), lambda i: (step(i), x_col))]
    if partials is not None:
        in_specs.append(pl.BlockSpec((tc, HP), lambda i: (step(i), 0)))
    if forget_bias is not None:
        in_specs.append(_vec_spec(LANES))
    return pl.pallas_call(
        body, name=name, grid=(n,), in_specs=in_specs, out_specs=pl.BlockSpec((tc, w), lambda i: (step(i), 0)),
        out_shape=jax.ShapeDtypeStruct((t, w), F32), scratch_shapes=[pltpu.VMEM((1, w), F32)],
        compiler_params=_cp("arbitrary"))(*xs)


HB_FWD = 4
HB_BWD = 4
HB_SB_FWD = 4
BQS = 512
BQA = 512


def _q_spec(hb, bq=BQ):
    return pl.BlockSpec((bq, hb * LANES), lambda g, i: (i, g))


ONE_BUFFER = pl.Buffered(1)


def _kv_spec(t, which, hb):
    return pl.BlockSpec((t, hb * LANES), lambda g, i, w=which: (0, w * (HEADS // hb) + g), pipeline_mode=ONE_BUFFER)


def _acc_spec(t, hb):
    return pl.BlockSpec((t, hb * LANES), lambda g, i: (0, g), pipeline_mode=ONE_BUFFER)


def _hs(hh):
    return slice(hh * LANES, (hh + 1) * LANES)


def _tile_iota(rows, cols):
    return (lax.broadcasted_iota(jnp.int32, (rows, cols), 0), lax.broadcasted_iota(jnp.int32, (rows, cols), 1))


def _kvt_spec(nkv, width, which, hb):
    return pl.BlockSpec((nkv, hb * LANES, width), lambda g, i, w=which: (0, w * (HEADS // hb) + g, 0),
                        pipeline_mode=ONE_BUFFER)


def _qrow_spec(hb, bq=BQ):
    return pl.BlockSpec((hb, 1, 1, bq), lambda g, i: (g, i, 0, 0))


def _vis(key0, query0, rows, cols, kind):
    r, c = _tile_iota(rows, cols)
    k, q = key0 + r, query0 + c
    if kind == "chunk":
        return (k >> CHUNK_SHIFT) <= (q >> CHUNK_SHIFT)
    return (k < q) if kind == "strict" else (k <= q)


def _softmax_fwd(qkv, kvt, *, chunk_mask, scale, name):
    t = qkv.shape[0]
    nq = t // BQA
    per = BQA // BQ
    hb = HB_FWD
    kind = "chunk" if chunk_mask else "causal"

    def body(q_ref, k_ref, vt_ref, o_ref, lse_ref, m_sc, l_sc, acc_sc):
        i = pl.program_id(1)
        m_sc[...] = jnp.full((hb, 1, BQA), NEG, F32)
        l_sc[...] = jnp.zeros((hb, 1, BQA), F32)
        acc_sc[...] = jnp.zeros((hb, LANES, BQA), F32)

        def tile(j, qoff):
            off = pl.multiple_of(j * BQ, BQ)
            lo = 0 if qoff is None else qoff
            qs = slice(lo, BQA)
            vis = None if qoff is None else _vis(off, i * BQA + lo, BQ, BQA - lo, kind)
            ss = [_bdot(k_ref[pl.ds(off, BQ), _hs(hh)], q_ref[qs, _hs(hh)], NT) for hh in range(hb)]
            for hh in range(hb):
                s = ss[hh]
                if scale != 1.0:
                    s = s * scale
                if vis is not None:
                    s = jnp.where(vis, s, NEG)
                m_old = m_sc[hh, :, qs]
                m_new = jnp.maximum(m_old, jnp.max(s, axis=0, keepdims=True))
                alpha = jnp.exp(m_old - m_new)
                p = jnp.exp(s - m_new)
                l_sc[hh, :, qs] = alpha * l_sc[hh, :, qs] + jnp.sum(p, axis=0, keepdims=True)
                m_sc[hh, :, qs] = m_new
                acc_sc[hh, :, qs] = alpha * acc_sc[hh, :, qs] + _bdot(vt_ref[j, _hs(hh), :], p.astype(BF16))

        def loop(j, carry):
            tile(j, None)
            return carry

        lax.fori_loop(0, per * i, loop, 0)
        for d in range(per):
            tile(per * i + d, d * BQ)
        for hh in range(hb):
            l = l_sc[hh]
            o_ref[:, _hs(hh)] = (acc_sc[hh] / l).T
            lse_ref[hh, 0] = m_sc[hh] + jnp.log(l)

    return pl.pallas_call(
        body, name=name, grid=(HEADS // hb, nq),
        in_specs=[_q_spec(hb, BQA), _kv_spec(t, 1, hb), _kvt_spec(t // BQ, BQ, 1, hb)],
        out_specs=[_q_spec(hb, BQA), _qrow_spec(hb, BQA)],
        out_shape=[jax.ShapeDtypeStruct((t, HP), F32), jax.ShapeDtypeStruct((HEADS, nq, 1, BQA), F32)],
        scratch_shapes=[pltpu.VMEM((hb, 1, BQA), F32), pltpu.VMEM((hb, 1, BQA), F32),
                        pltpu.VMEM((hb, LANES, BQA), F32)],
        compiler_params=_cp("parallel", "arbitrary"))(qkv, qkv, kvt)


def _softmax_bwd(qkv, kvt, do, lse, delta, *, bias, chunk_mask, scale, name):
    t = qkv.shape[0]
    nq = t // BQA
    per = BQA // BQ
    hb = HB_BWD
    kind = "chunk" if chunk_mask else "causal"

    def body(*refs):
        if bias:
            (q_ref, k_ref, v_ref, kt_ref, do_ref, lse_ref, dl_ref, dq_ref, dk_ref, dv_ref, dck_ref, dcq_ref,
             dq_sc, dcq_sc) = refs
            dcq_sc[...] = jnp.zeros((hb, 1, BQA), F32)
        else:
            q_ref, k_ref, v_ref, kt_ref, do_ref, lse_ref, dl_ref, dq_ref, dk_ref, dv_ref, dq_sc = refs
        i = pl.program_id(1)

        @pl.when(i == 0)
        def _():
            dk_ref[...] = jnp.zeros_like(dk_ref)
            dv_ref[...] = jnp.zeros_like(dv_ref)
            if bias:
                dck_ref[...] = jnp.zeros_like(dck_ref)

        dq_sc[...] = jnp.zeros((hb, LANES, BQA), F32)

        def tile(j, qoff):
            off = pl.multiple_of(j * BQ, BQ)
            lo = 0 if qoff is None else qoff
            qsl = slice(lo, BQA)
            vis = None if qoff is None else _vis(off, i * BQA + lo, BQ, BQA - lo, kind)
            qs = [q_ref[qsl, _hs(hh)] for hh in range(hb)]
            dobs = [do_ref[qsl, _hs(hh)].astype(BF16) for hh in range(hb)]
            ss = [_bdot(k_ref[pl.ds(off, BQ), _hs(hh)], qs[hh], NT) for hh in range(hb)]
            dps = [_bdot(v_ref[pl.ds(off, BQ), _hs(hh)], dobs[hh], NT) for hh in range(hb)]
            pbs, dsbs = [], []
            for hh in range(hb):
                s = ss[hh]
                if scale != 1.0:
                    s = s * scale
                p = jnp.exp(s - lse_ref[hh, 0, :, qsl])
                if vis is not None:
                    p = jnp.where(vis, p, 0.0)
                ds = p * (dps[hh] - dl_ref[hh, 0, :, qsl])
                if bias:
                    part = ds[:, 0:LANES]
                    for b in range(1, (BQA - lo) // LANES):
                        part = part + ds[:, b * LANES:(b + 1) * LANES]
                    dck_ref[pl.ds(off, BQ), _hs(hh)] -= part
                    dcq_sc[hh, :, qsl] += jnp.sum(ds, axis=0, keepdims=True)
                if scale != 1.0:
                    ds = ds * scale
                pbs.append(p.astype(BF16))
                dsbs.append(ds.astype(BF16))
            for hh in range(hb):
                sl = _hs(hh)
                dv_ref[pl.ds(off, BQ), sl] += _bdot(pbs[hh], dobs[hh])
                dk_ref[pl.ds(off, BQ), sl] += _bdot(dsbs[hh], qs[hh])
                dq_sc[hh, :, qsl] += _bdot(kt_ref[j, sl, :], dsbs[hh])

        def loop(j, carry):
            tile(j, None)
            return carry

        lax.fori_loop(0, per * i, loop, 0)
        for d in range(per):
            tile(per * i + d, d * BQ)
        for hh in range(hb):
            dq_ref[:, _hs(hh)] = dq_sc[hh].T
            if bias:
                dcq_ref[hh, 0] = dcq_sc[hh]

    in_specs = [_q_spec(hb, BQA), _kv_spec(t, 1, hb), _kv_spec(t, 2, hb), _kvt_spec(t // BQ, BQ, 0, hb),
                _q_spec(hb, BQA), _qrow_spec(hb, BQA), _qrow_spec(hb, BQA)]
    out_specs = [_q_spec(hb, BQA), _acc_spec(t, hb), _acc_spec(t, hb)]
    out_shape = [jax.ShapeDtypeStruct((t, HP), F32)] * 3
    scratch = [pltpu.VMEM((hb, LANES, BQA), F32)]
    if bias:
        out_specs += [_acc_spec(t, hb), _qrow_spec(hb, BQA)]
        out_shape += [jax.ShapeDtypeStruct((t, HP), F32), jax.ShapeDtypeStruct((HEADS, nq, 1, BQA), F32)]
        scratch.append(pltpu.VMEM((hb, 1, BQA), F32))
    return pl.pallas_call(
        body, name=name, grid=(HEADS // hb, nq), in_specs=in_specs, out_specs=out_specs, out_shape=out_shape,
        scratch_shapes=scratch,
        compiler_params=_cp("parallel", "arbitrary"))(qkv, qkv, qkv, kvt, do, lse, delta)


def _ret_diag_decay(lg1, keys_on_rows=False):
    r, c = _tile_iota(BQ, BQ)
    qn, km = (c, r) if keys_on_rows else (r, c)
    dd = jnp.where(km > qn, jnp.exp((2.0 * lg1) * (km - qn).astype(F32)), 1.0)
    return jnp.where((km >> CHUNK_SHIFT) <= (qn >> CHUNK_SHIFT), dd, 0.0)


def _lg_spec(hb):
    return pl.BlockSpec((hb, 1, LANES), lambda g, i: (g, 0, 0))


def _ret_specs(nq, hb, reverse):
    tile = (lambda i: nq - 1 - i) if reverse else (lambda i: i)
    qkv = [pl.BlockSpec((BQ, hb * LANES), lambda g, i, w=w: (tile(i), w * (HEADS // hb) + g)) for w in range(3)]
    kt = pl.BlockSpec((1, hb * LANES, BQ), lambda g, i: (tile(i), g, 0))
    st = pl.BlockSpec((1, hb * LANES, LANES), lambda g, i: (tile(i), g, 0))
    return qkv, kt, st


def _ret_fwd(qkv, kt, lg_heads, name):
    t = qkv.shape[0]
    nq = t // BQ
    hb = HB_FWD

    def body(lg_ref, q_ref, k_ref, v_ref, kt_ref, o_ref, st_ref, s_sc):
        @pl.when(pl.program_id(1) == 0)
        def _():
            s_sc[...] = jnp.zeros_like(s_sc)

        qs = [q_ref[:, _hs(hh)] for hh in range(hb)]
        vs = [v_ref[:, _hs(hh)] for hh in range(hb)]
        aa = [_bdot(qs[hh], k_ref[:, _hs(hh)], NT) for hh in range(hb)]
        kv = [_bdot(kt_ref[0, _hs(hh), :], vs[hh]) for hh in range(hb)]
        for hh in range(hb):
            sl = _hs(hh)
            lg1 = lg_ref[hh][:, 0:1]
            s = s_sc[hh]
            st_ref[0, sl, :] = s
            shi, slo = _split2(s)
            a = (aa[hh] * _ret_diag_decay(lg1)).astype(BF16)
            o_ref[:, sl] = _bdot(a, vs[hh]) + _bdot(qs[hh], shi) + _bdot(qs[hh], slo)
            s_sc[hh] = jnp.exp(lg1 * float(BQ)) * (s + kv[hh])

    qkv_specs, kt_spec, st_spec = _ret_specs(nq, hb, False)
    return pl.pallas_call(
        body, name=name, grid=(HEADS // hb, nq), in_specs=[_lg_spec(hb)] + qkv_specs + [kt_spec],
        out_specs=[_q_spec(hb), st_spec],
        out_shape=[jax.ShapeDtypeStruct((t, HP), F32), jax.ShapeDtypeStruct((nq, HP, LANES), F32)],
        scratch_shapes=[pltpu.VMEM((hb, LANES, LANES), F32)],
        compiler_params=_cp("parallel", "arbitrary"))(lg_heads, qkv, qkv, qkv, kt)


def _ret_bwd(qkv, states, lg_heads, do, name):
    t = qkv.shape[0]
    nq = t // BQ
    hb = HB_FWD

    def body(lg_ref, q_ref, k_ref, v_ref, st_ref, do_ref, dq_ref, dk_ref, dv_ref, g_sc):
        @pl.when(pl.program_id(1) == 0)
        def _():
            g_sc[...] = jnp.zeros_like(g_sc)

        qs = [q_ref[:, _hs(hh)] for hh in range(hb)]
        ks = [k_ref[:, _hs(hh)] for hh in range(hb)]
        vs = [v_ref[:, _hs(hh)] for hh in range(hb)]
        dobs = [do_ref[:, _hs(hh)].astype(BF16) for hh in range(hb)]
        aa = [_bdot(ks[hh], qs[hh], NT) for hh in range(hb)]
        das = [_bdot(vs[hh], dobs[hh], NT) for hh in range(hb)]
        qdo = [_bdot(qs[hh], dobs[hh], TN) for hh in range(hb)]
        for hh in range(hb):
            sl = _hs(hh)
            lg1 = lg_ref[hh][:, 0:1]
            dd = _ret_diag_decay(lg1, keys_on_rows=True)
            at = (aa[hh] * dd).astype(BF16)
            dat = (das[hh] * dd).astype(BF16)
            h = jnp.exp(lg1 * float(BQ)) * g_sc[hh]
            hhi, hlo = _split2(h)
            shi, slo = _split2(st_ref[0, sl, :])
            dv_ref[:, sl] = _bdot(at, dobs[hh]) + _bdot(ks[hh], hhi) + _bdot(ks[hh], hlo)
            dk_ref[:, sl] = _bdot(dat, qs[hh]) + _bdot(vs[hh], hhi, NT) + _bdot(vs[hh], hlo, NT)
            dq_ref[:, sl] = _bdot(dat, ks[hh], TN) + _bdot(dobs[hh], shi, NT) + _bdot(dobs[hh], slo, NT)
            g_sc[hh] = qdo[hh] + h

    qkv_specs, _, st_spec = _ret_specs(nq, hb, True)
    tile_spec = pl.BlockSpec((BQ, hb * LANES), lambda g, i: (nq - 1 - i, g))
    return pl.pallas_call(
        body, name=name, grid=(HEADS // hb, nq), in_specs=[_lg_spec(hb)] + qkv_specs + [st_spec, tile_spec],
        out_specs=[tile_spec] * 3, out_shape=[jax.ShapeDtypeStruct((t, HP), F32)] * 3,
        scratch_shapes=[pltpu.VMEM((hb, LANES, LANES), F32)],
        compiler_params=_cp("parallel", "arbitrary"))(lg_heads, qkv, qkv, qkv, states, do)


def _sb_tile_logs(q, kb, vis):
    z = _bdot(kb, q, NT)
    ls = -(jnp.maximum(z, 0.0) + jnp.log(1.0 + jnp.exp(-jnp.abs(z))))
    if vis is not None:
        ls = jnp.where(vis, ls, 0.0)
    return z, ls


def _sb_later(ls, after):
    hi, lo = _split2(ls)
    return _bdot(after, hi) + _bdot(after, lo)


def _sb_fwd(qkv, kvt, name):
    t = qkv.shape[0]
    nq = t // BQS
    per = BQS // TKS
    hb = HB_SB_FWD

    def body(q_ref, k_ref, vt_ref, o_ref, tot_ref, acc_sc, r_sc):
        i = pl.program_id(1)
        acc_sc[...] = jnp.zeros((hb, LANES, BQS), F32)
        r_sc[...] = jnp.zeros((hb, 1, BQS), F32)
        mr, mc = _tile_iota(TKS, TKS)
        after = jnp.where(mc > mr, 1.0, 0.0).astype(BF16)

        def tile(j, qoff):
            off = pl.multiple_of(j * TKS, TKS)
            lo = 0 if qoff is None else qoff
            qsl = slice(lo, BQS)
            vis = None if qoff is None else _vis(off, i * BQS + lo, TKS, BQS - lo, "strict")
            zl = [_sb_tile_logs(q_ref[qsl, _hs(hh)], k_ref[pl.ds(off, TKS), _hs(hh)], vis) for hh in range(hb)]
            laters = [_sb_later(zl[hh][1], after) for hh in range(hb)]
            ws = []
            for hh in range(hb):
                z, ls = zl[hh]
                w = jnp.exp(z + ls + laters[hh] + r_sc[hh, :, qsl])
                if vis is not None:
                    w = jnp.where(vis, w, 0.0)
                ws.append(w.astype(BF16))
                r_sc[hh, :, qsl] += jnp.sum(ls, axis=0, keepdims=True)
            for hh in range(hb):
                acc_sc[hh, :, qsl] += _bdot(vt_ref[j, _hs(hh), :], ws[hh])

        for d in reversed(range(per)):
            tile(per * i + d, d * TKS)

        def loop(jj, carry):
            tile(per * i - 1 - jj, None)
            return carry

        lax.fori_loop(0, per * i, loop, 0)
        for hh in range(hb):
            o_ref[:, _hs(hh)] = acc_sc[hh].T
            tot_ref[hh, 0] = r_sc[hh]

    return pl.pallas_call(
        body, name=name, grid=(HEADS // hb, nq),
        in_specs=[_q_spec(hb, BQS), _kv_spec(t, 1, hb), _kvt_spec(t // TKS, TKS, 1, hb)],
        out_specs=[_q_spec(hb, BQS), _qrow_spec(hb, BQS)],
        out_shape=[jax.ShapeDtypeStruct((t, HP), F32), jax.ShapeDtypeStruct((HEADS, nq, 1, BQS), F32)],
        scratch_shapes=[pltpu.VMEM((hb, LANES, BQS), F32), pltpu.VMEM((hb, 1, BQS), F32)],
        compiler_params=_cp("parallel", "arbitrary"))(qkv, qkv, kvt)


def _sb_bwd(qkv, kvt, do, tot, name):
    t = qkv.shape[0]
    nq = t // BQS
    per = BQS // TKS
    hb = HB_BWD

    def body(q_ref, k_ref, v_ref, kt_ref, do_ref, tot_ref, dq_ref, dk_ref, dv_ref, dq_sc, p_sc, g_sc):
        i = pl.program_id(1)

        @pl.when(i == 0)
        def _():
            dk_ref[...] = jnp.zeros_like(dk_ref)
            dv_ref[...] = jnp.zeros_like(dv_ref)

        dq_sc[...] = jnp.zeros((hb, LANES, BQS), F32)
        p_sc[...] = jnp.zeros((hb, 1, BQS), F32)
        g_sc[...] = jnp.zeros((hb, 1, BQS), F32)
        mr, mc = _tile_iota(TKS, TKS)
        after = jnp.where(mc > mr, 1.0, 0.0).astype(BF16)
        before = jnp.where(mc < mr, 1.0, 0.0).astype(BF16)

        def tile(j, qoff):
            off = pl.multiple_of(j * TKS, TKS)
            lo = 0 if qoff is None else qoff
            qsl = slice(lo, BQS)
            vis = None if qoff is None else _vis(off, i * BQS + lo, TKS, BQS - lo, "strict")
            qs = [q_ref[qsl, _hs(hh)] for hh in range(hb)]
            dobs = [do_ref[qsl, _hs(hh)].astype(BF16) for hh in range(hb)]
            zl = [_sb_tile_logs(qs[hh], k_ref[pl.ds(off, TKS), _hs(hh)], vis) for hh in range(hb)]
            dws = [_bdot(v_ref[pl.ds(off, TKS), _hs(hh)], dobs[hh], NT) for hh in range(hb)]
            laters = [_sb_later(zl[hh][1], after) for hh in range(hb)]
            ws, gs = [], []
            for hh in range(hb):
                z, ls = zl[hh]
                own = jnp.sum(ls, axis=0, keepdims=True)
                rest = tot_ref[hh, 0, :, qsl] - p_sc[hh, :, qsl] - own
                w = jnp.exp(z + ls + laters[hh] + rest)
                if vis is not None:
                    w = jnp.where(vis, w, 0.0)
                p_sc[hh, :, qsl] += own
                ws.append(w.astype(BF16))
                gs.append(dws[hh] * w)
            gins = [_bdot(before, gs[hh].astype(BF16)) for hh in range(hb)]
            dzbs = []
            for hh in range(hb):
                g = gs[hh]
                stay = jnp.exp(zl[hh][1])
                dz = g * stay - (1.0 - stay) * (gins[hh] + g_sc[hh, :, qsl])
                if vis is not None:
                    dz = jnp.where(vis, dz, 0.0)
                g_sc[hh, :, qsl] += jnp.sum(g, axis=0, keepdims=True)
                dzbs.append(dz.astype(BF16))
            for hh in range(hb):
                sl = _hs(hh)
                dv_ref[pl.ds(off, TKS), sl] += _bdot(ws[hh], dobs[hh])
                dk_ref[pl.ds(off, TKS), sl] += _bdot(dzbs[hh], qs[hh])
                dq_sc[hh, :, qsl] += _bdot(kt_ref[j, sl, :], dzbs[hh])

        def loop(j, carry):
            tile(j, None)
            return carry

        lax.fori_loop(0, per * i, loop, 0)
        for d in range(per):
            tile(per * i + d, d * TKS)
        for hh in range(hb):
            dq_ref[:, _hs(hh)] = dq_sc[hh].T

    return pl.pallas_call(
        body, name=name, grid=(HEADS // hb, nq),
        in_specs=[_q_spec(hb, BQS), _kv_spec(t, 1, hb), _kv_spec(t, 2, hb), _kvt_spec(t // TKS, TKS, 0, hb),
                  _q_spec(hb, BQS), _qrow_spec(hb, BQS)],
        out_specs=[_q_spec(hb, BQS), _acc_spec(t, hb), _acc_spec(t, hb)],
        out_shape=[jax.ShapeDtypeStruct((t, HP), F32)] * 3,
        scratch_shapes=[pltpu.VMEM((hb, LANES, BQS), F32), pltpu.VMEM((hb, 1, BQS), F32),
                        pltpu.VMEM((hb, 1, BQS), F32)],
        compiler_params=_cp("parallel", "arbitrary"))(qkv, qkv, qkv, kvt, do, tot)


def _sigmoid(v):
    return 1.0 / (1.0 + jnp.exp(-v))


def _post_fwd(oa, ob, oc, od, proj, g_pad, name):
    t = oa.shape[0]

    def body(oa_ref, ob_ref, oc_ref, od_ref, rg_ref, g_ref, mx_ref):
        g = g_ref[...]

        def group(o, gg):
            r = lax.rsqrt(jnp.sum(o * o, axis=-1, keepdims=True) * (1.0 / GROUP) + EPS)
            return _gather_heads(o * r * gg).astype(BF16)

        mx_ref[:, 0:GROUP] = group(oa_ref[...], g[:, 0:HP])
        mx_ref[:, GROUP:2 * GROUP] = group(ob_ref[...], g[:, HP:2 * HP])
        mx_ref[:, 3 * GROUP:4 * GROUP] = group(od_ref[...], g[:, 3 * HP:4 * HP])
        real = lax.broadcasted_iota(jnp.int32, (TM, LANES), 1) < HEAD_DIM
        rg = _spread_heads(rg_ref[...])
        gated = []
        for hb in range(HEADS):
            sl = slice(hb * LANES, (hb + 1) * LANES)
            o = oc_ref[:, sl]
            mu = jnp.sum(o, axis=-1, keepdims=True) * (1.0 / HEAD_DIM)
            dlt = jnp.where(real, o - mu, 0.0)
            var = jnp.sum(dlt * dlt, axis=-1, keepdims=True) * (1.0 / HEAD_DIM)
            yn = dlt * lax.rsqrt(var + EPS) * g[:, 2 * HP + hb * LANES:2 * HP + (hb + 1) * LANES]
            x = rg[:, sl]
            gated.append(yn * (x * _sigmoid(x)))
        mx_ref[:, 2 * GROUP:3 * GROUP] = _gather_heads(jnp.concatenate(gated, axis=1)).astype(BF16)

    rg_spec = pl.BlockSpec((TM, GROUP), lambda i: (i, OFF_RG // GROUP))
    return pl.pallas_call(
        body, name=name, grid=(t // TM,),
        in_specs=[_row_spec(HP)] * 4 + [rg_spec, _vec_spec(4 * HP)], out_specs=_row_spec(D_MODEL),
        out_shape=jax.ShapeDtypeStruct((t, D_MODEL), BF16), compiler_params=_cp("parallel"))(oa, ob, oc, od, proj, g_pad)


def _post_bwd(dmx, oa, ob, oc, od, proj, g_pad, name):
    t = oa.shape[0]

    def body(dm_ref, oa_ref, ob_ref, oc_ref, od_ref, rg_ref, g_ref,
             doa_ref, dob_ref, doc_ref, dod_ref, dla_ref, dlb_ref, drg_ref, dg_ref):
        @pl.when(pl.program_id(0) == 0)
        def _():
            dg_ref[...] = jnp.zeros_like(dg_ref)

        g = g_ref[...]

        def group_bwd(dm, o, gg):
            r = lax.rsqrt(jnp.sum(o * o, axis=-1, keepdims=True) * (1.0 / GROUP) + EPS)
            oh = o * r
            dgp = jnp.sum(dm * oh, axis=0, keepdims=True)
            dyh = dm * gg
            do = r * (dyh - oh * (jnp.sum(dyh * oh, axis=-1, keepdims=True) * (1.0 / GROUP)))
            return do, dgp

        def delta_bc(do, o):
            prod = do * o
            lane = lax.broadcasted_iota(jnp.int32, (TM, LANES), 1)
            out = jnp.zeros((TM, LANES), F32)
            for hb in range(HEADS):
                out = jnp.where(lane == hb, jnp.sum(prod[:, hb * LANES:(hb + 1) * LANES], axis=-1, keepdims=True), out)
            return out

        dmp = [_spread_heads(dm_ref[:, gi * GROUP:(gi + 1) * GROUP]) for gi in range(4)]
        rg = _spread_heads(rg_ref[...])
        oa = oa_ref[...]
        do_a, dga = group_bwd(dmp[0], oa, g[:, 0:HP])
        doa_ref[...] = do_a.astype(BF16)
        dla_ref[...] = delta_bc(do_a, oa)
        dg_ref[:, 0:HP] += dga
        ob = ob_ref[...]
        do_b, dgb = group_bwd(dmp[1], ob, g[:, HP:2 * HP])
        dob_ref[...] = do_b.astype(BF16)
        dlb_ref[...] = delta_bc(do_b, ob)
        dg_ref[:, HP:2 * HP] += dgb
        do_d, dgd = group_bwd(dmp[3], od_ref[...], g[:, 3 * HP:4 * HP])
        dod_ref[...] = do_d.astype(BF16)
        dg_ref[:, 3 * HP:4 * HP] += dgd
        real = lax.broadcasted_iota(jnp.int32, (TM, LANES), 1) < HEAD_DIM
        for hb in range(HEADS):
            sl = slice(hb * LANES, (hb + 1) * LANES)
            gsl = slice(2 * HP + hb * LANES, 2 * HP + (hb + 1) * LANES)
            o = oc_ref[:, sl]
            mu = jnp.sum(o, axis=-1, keepdims=True) * (1.0 / HEAD_DIM)
            dlt = jnp.where(real, o - mu, 0.0)
            var = jnp.sum(dlt * dlt, axis=-1, keepdims=True) * (1.0 / HEAD_DIM)
            rstd = lax.rsqrt(var + EPS)
            dhat = dlt * rstd
            gc = g[:, gsl]
            x = rg[:, sl]
            sg = _sigmoid(x)
            dm = dmp[2][:, sl]
            drg_ref[:, sl] = dm * (dhat * gc) * (sg * (1.0 + x * (1.0 - sg)))
            dyn = dm * (x * sg)
            dg_ref[:, gsl] += jnp.sum(dyn * dhat, axis=0, keepdims=True)
            ddh = dyn * gc
            m1 = jnp.sum(ddh, axis=-1, keepdims=True) * (1.0 / HEAD_DIM)
            m2 = jnp.sum(ddh * dhat, axis=-1, keepdims=True) * (1.0 / HEAD_DIM)
            doc_ref[:, sl] = jnp.where(real, rstd * (ddh - m1 - dhat * m2), 0.0).astype(BF16)

    rg_spec = pl.BlockSpec((TM, GROUP), lambda i: (i, OFF_RG // GROUP))
    hp = _row_spec(HP)
    return pl.pallas_call(
        body, name=name, grid=(t // TM,),
        in_specs=[_row_spec(D_MODEL), hp, hp, hp, hp, rg_spec, _vec_spec(4 * HP)],
        out_specs=[hp] * 4 + [_row_spec(LANES)] * 2 + [hp, _vec_spec(4 * HP)],
        out_shape=[jax.ShapeDtypeStruct((t, HP), BF16)] * 4 + [jax.ShapeDtypeStruct((t, LANES), F32)] * 2
        + [jax.ShapeDtypeStruct((t, HP), F32), jax.ShapeDtypeStruct((1, 4 * HP), F32)],
        compiler_params=_cp("arbitrary"))(dmx, oa, ob, oc, od, proj, g_pad)


def _mesh_pos():
    return lax.axis_index("x"), lax.axis_index("y"), lax.axis_index("c")


def _peer(pos, k):
    x, y, c = pos
    px = 1 - x if (k >> 2) & 1 else x
    py = 1 - y if (k >> 1) & 1 else y
    pc = 1 - c if k & 1 else c
    return (px, py, pc), 4 * px + 2 * py + pc


def _exchange(arrs, gather, name):
    n = len(arrs)

    def body(*refs):
        ins, outs = refs[:n], refs[n:2 * n]
        send_sems, recv_sems, loc_sems = refs[2 * n:]
        pos = _mesh_pos()
        me = 4 * pos[0] + 2 * pos[1] + pos[2]
        local = []
        for a in range(n):
            src = ins[a] if gather else ins[a].at[me]
            cp = pltpu.make_async_copy(src, outs[a].at[me], loc_sems.at[a])
            cp.start()
            local.append(cp)
        sends, recvs = [], []
        for k in range(1, N_DEV):
            peer, pid = _peer(pos, k)
            for a in range(n):
                s = a * (N_DEV - 1) + k - 1
                src = ins[a] if gather else ins[a].at[pid]
                cp = pltpu.make_async_remote_copy(
                    src_ref=src, dst_ref=outs[a].at[me], send_sem=send_sems.at[s], recv_sem=recv_sems.at[s],
                    device_id=peer, device_id_type=pl.DeviceIdType.MESH)
                cp.start()
                sends.append(cp)
                recvs.append(pltpu.make_async_remote_copy(
                    src_ref=src, dst_ref=outs[a].at[pid], send_sem=send_sems.at[s], recv_sem=recv_sems.at[s],
                    device_id=peer, device_id_type=pl.DeviceIdType.MESH))
        for cp in recvs:
            cp.wait_recv()
        for cp in sends:
            cp.wait_send()
        for cp in local:
            cp.wait()

    any_spec = pl.BlockSpec(memory_space=pl.ANY)
    out_shape = [jax.ShapeDtypeStruct((N_DEV,) + tuple(a.shape) if gather else tuple(a.shape), a.dtype) for a in arrs]
    return pl.pallas_call(
        body, name=name, in_specs=[any_spec] * n, out_specs=[any_spec] * n, out_shape=out_shape,
        scratch_shapes=[pltpu.SemaphoreType.DMA((n * (N_DEV - 1),)), pltpu.SemaphoreType.DMA((n * (N_DEV - 1),)),
                        pltpu.SemaphoreType.DMA((n,))],
        compiler_params=pltpu.CompilerParams(has_side_effects=True))(*arrs)


def _device_index():
    x, y, c = _mesh_pos()
    return 4 * x + 2 * y + c


def _landing(srcs, gather):
    me = _device_index()
    lands = []
    for a in srcs:
        own = a[None] if gather else lax.dynamic_slice_in_dim(a, me, 1, axis=0)
        shape = (N_DEV,) + tuple(a.shape) if gather else tuple(a.shape)
        lands.append(lax.dynamic_update_slice_in_dim(lax.empty(shape, a.dtype), own, me, axis=0))
    return lands


def _exchange_copies(ins, lands, send_sems, recv_sems, gather):
    pos = _mesh_pos()
    me = 4 * pos[0] + 2 * pos[1] + pos[2]
    sends, recvs = [], []
    for k in range(1, N_DEV):
        peer, pid = _peer(pos, k)
        for a in range(len(ins)):
            s = a * (N_DEV - 1) + k - 1
            src = ins[a] if gather else ins[a].at[pid]
            sends.append(pltpu.make_async_remote_copy(
                src_ref=src, dst_ref=lands[a].at[me], send_sem=send_sems.at[s], recv_sem=recv_sems.at[s],
                device_id=peer, device_id_type=pl.DeviceIdType.MESH))
            recvs.append(pltpu.make_async_remote_copy(
                src_ref=src, dst_ref=lands[a].at[pid], send_sem=send_sems.at[s], recv_sem=recv_sems.at[s],
                device_id=peer, device_id_type=pl.DeviceIdType.MESH))
    return sends, recvs


def _exchange_start(srcs, gather, name, after=None):
    n = len(srcs)
    lands = _landing(srcs, gather)
    nsem = n * (N_DEV - 1)
    extra = [] if after is None else [after]

    def body(*refs):
        ins, lnd = refs[:n], refs[n:2 * n]
        send_sems, recv_sems = refs[2 * n + len(extra)], refs[2 * n + len(extra) + 1]
        token = refs[-1]
        sends, _ = _exchange_copies(ins, lnd, send_sems, recv_sems, gather)
        for cp in sends:
            cp.start()
        token[...] = jnp.zeros_like(token)

    hbm = pl.BlockSpec(memory_space=pltpu.HBM)
    sem = pl.BlockSpec(memory_space=pltpu.SEMAPHORE)
    bufs = list(srcs) + lands
    out_shape = ([pltpu.SemaphoreType.DMA((nsem,)), pltpu.SemaphoreType.DMA((nsem,))]
                 + [pltpu.HBM(b.shape, b.dtype) for b in bufs] + [jax.ShapeDtypeStruct((8, LANES), F32)])
    outs = pl.pallas_call(
        body, name=name, in_specs=[hbm] * (2 * n) + [pl.BlockSpec(memory_space=pl.ANY)] * len(extra),
        out_specs=[sem, sem] + [hbm] * (2 * n) + [pl.BlockSpec(memory_space=pltpu.VMEM)], out_shape=out_shape,
        input_output_aliases={i: 2 + i for i in range(2 * n)},
        compiler_params=pltpu.CompilerParams(has_side_effects=pltpu.SideEffectType.DATAFLOW_SIDE_EFFECTING),
    )(*[pltpu.with_memory_space_constraint(b, pltpu.HBM) for b in bufs], *extra)
    return (outs[0], outs[1], outs[2:2 + n], outs[2 + n:2 + 2 * n]), outs[-1]


def _exchange_wait(state, after, gather, name):
    send_sems, recv_sems, srcs, lands = state
    n = len(srcs)
    after = list(after) if isinstance(after, (list, tuple)) else [after]

    def body(*refs):
        ins, lnd = refs[:n], refs[n:2 * n]
        s_sems, r_sems = refs[2 * n], refs[2 * n + 1]
        sends, recvs = _exchange_copies(ins, lnd, s_sems, r_sems, gather)
        for cp in sends:
            cp.wait_send()
        for cp in recvs:
            cp.wait_recv()

    hbm = pl.BlockSpec(memory_space=pltpu.HBM)
    sem = pl.BlockSpec(memory_space=pltpu.SEMAPHORE)
    bufs = list(srcs) + list(lands)
    outs = pl.pallas_call(
        body, name=name, in_specs=[hbm] * (2 * n) + [sem, sem] + [pl.BlockSpec(memory_space=pl.ANY)] * len(after),
        out_specs=[hbm] * (2 * n), out_shape=[pltpu.HBM(b.shape, b.dtype) for b in bufs],
        input_output_aliases={i: i for i in range(2 * n)},
        compiler_params=pltpu.CompilerParams(has_side_effects=pltpu.SideEffectType.DATAFLOW_SIDE_EFFECTING),
    )(*bufs, send_sems, recv_sems, *after)
    return outs[n:]


def _adam_vals(w, g, m, v):
    m = ADAM_B1 * m + (1.0 - ADAM_B1) * g
    v = ADAM_B2 * v + (1.0 - ADAM_B2) * (g * g)
    m_hat = m / ADAM_C1
    v_hat = v / ADAM_C2
    delta = -ADAM_LR * (m_hat / (jnp.sqrt(v_hat) + ADAM_EPS) + ADAM_WD * w)
    return delta, m, v


def _small_allreduce_adam(part, w, m, v, name):
    rows = part.shape[0]

    def body(p_ref, w_ref, m_ref, v_ref, g_ref, d_ref, nm_ref, nv_ref, gath, send_sems, recv_sems):
        pos = _mesh_pos()
        me = 4 * pos[0] + 2 * pos[1] + pos[2]
        gath[me] = p_ref[...]
        sends, recvs = [], []
        for k in range(1, N_DEV):
            peer, pid = _peer(pos, k)
            cp = pltpu.make_async_remote_copy(
                src_ref=p_ref, dst_ref=gath.at[me], send_sem=send_sems.at[k - 1], recv_sem=recv_sems.at[k - 1],
                device_id=peer, device_id_type=pl.DeviceIdType.MESH)
            cp.start()
            sends.append(cp)
            recvs.append(pltpu.make_async_remote_copy(
                src_ref=p_ref, dst_ref=gath.at[pid], send_sem=send_sems.at[k - 1], recv_sem=recv_sems.at[k - 1],
                device_id=peer, device_id_type=pl.DeviceIdType.MESH))
        for cp in recvs:
            cp.wait_recv()
        for cp in sends:
            cp.wait_send()
        g = gath[0]
        for p in range(1, N_DEV):
            g = g + gath[p]
        g_ref[...] = g
        d, nm, nv = _adam_vals(w_ref[...], g, m_ref[...], v_ref[...])
        d_ref[...] = d
        nm_ref[...] = nm
        nv_ref[...] = nv

    vm = pl.BlockSpec(memory_space=pltpu.VMEM)
    sds = jax.ShapeDtypeStruct((rows, LANES), F32)
    return pl.pallas_call(
        body, name=name, in_specs=[vm] * 4, out_specs=[vm] * 4, out_shape=[sds] * 4,
        scratch_shapes=[pltpu.VMEM((N_DEV, rows, LANES), F32), pltpu.SemaphoreType.DMA((N_DEV - 1,)),
                        pltpu.SemaphoreType.DMA((N_DEV - 1,))],
        compiler_params=pltpu.CompilerParams(has_side_effects=True))(part, w, m, v)


def _reduce_adam(recv, w, m, v, name):
    shape = w.shape
    c = shape[-1]
    r = int(np.prod(shape[:-1]))
    recv2, w2, m2, v2 = recv.reshape(N_DEV, r, c), w.reshape(r, c), m.reshape(r, c), v.reshape(r, c)
    tr = r
    while tr * c * 4 > (1 << 20) and tr % 16 == 0:
        tr //= 2

    def body(r_ref, w_ref, m_ref, v_ref, g_ref, d_ref, nm_ref, nv_ref):
        g = r_ref[0].astype(F32)
        for p in range(1, N_DEV):
            g = g + r_ref[p].astype(F32)
        g_ref[...] = g
        d, nm, nv = _adam_vals(w_ref[...], g, m_ref[...], v_ref[...])
        d_ref[...] = d
        nm_ref[...] = nm
        nv_ref[...] = nv

    spec = pl.BlockSpec((tr, c), lambda i: (i, 0))
    sds = jax.ShapeDtypeStruct((r, c), F32)
    outs = pl.pallas_call(
        body, name=name, grid=(r // tr,),
        in_specs=[pl.BlockSpec((N_DEV, tr, c), lambda i: (0, i, 0)), spec, spec, spec],
        out_specs=[spec] * 4, out_shape=[sds] * 4, compiler_params=_cp("parallel"))(recv2, w2, m2, v2)
    return [o.reshape(shape) for o in outs]


def _pad_heads(w, real=HEAD_DIM):
    lead = w.shape[:-1]
    w = w.reshape(lead + (HEADS, real))
    w = jnp.pad(w, [(0, 0)] * len(lead) + [(0, 0), (0, LANES - real)])
    return w.reshape(lead + (HP,))


def _unpad_heads(w, real=HEAD_DIM):
    lead = w.shape[:-1]
    return w.reshape(lead + (HEADS, LANES))[..., :real].reshape(lead + (HEADS * real,))


_IN_SEGS = (("fq", 0, 256), ("fk", 256, 512), ("fv", 512, 768), ("ff", 768, 772), ("cq", 772, 1028),
            ("ckv", 1028, 1156), ("kr", 1156, 1188), ("rq", 1188, 1444), ("rk", 1444, 1700), ("rv", 1700, 1956),
            ("rg", 1956, 2212), ("sq", 2212, 2468), ("sk", 2468, 2724), ("sv", 2724, 2980))


def _pad_w_in(w):
    s = {n: w[:, a:b] for n, a, b in _IN_SEGS}
    rows = w.shape[0]
    z = lambda n: jnp.zeros((rows, n), w.dtype)
    parts = [s[n] for n in ("fq", "fk", "fv", "rq", "rk", "rv", "sq", "sk", "sv", "rg", "cq", "ckv")]
    parts += [z(HEAD_DIM), s["kr"], z(LANES - HEAD_DIM - ROPE_MLA), s["ff"], z(LANES - HEADS),
              z(NP_IN - OFF_FF - LANES)]
    return jnp.concatenate(parts, axis=1)


def _unpad_w_in(wp):
    seg = lambda off, n=GROUP: wp[:, off:off + n]
    parts = [seg(OFF_FOX), seg(OFF_FOX + GROUP), seg(OFF_FOX + 2 * GROUP), seg(OFF_FF, HEADS),
             seg(OFF_CQ, Q_RANK), seg(OFF_CKV, KV_RANK), seg(OFF_KR + HEAD_DIM, ROPE_MLA),
             seg(OFF_RET), seg(OFF_RET + GROUP), seg(OFF_RET + 2 * GROUP), seg(OFF_RG),
             seg(OFF_SB), seg(OFF_SB + GROUP), seg(OFF_SB + 2 * GROUP)]
    return jnp.concatenate(parts, axis=1)


def _pad_w_kv(w):
    w4 = w.reshape(KV_RANK, HEADS, 2 * HEAD_DIM)
    k = w4[:, :, :HEAD_DIM].reshape(KV_RANK, GROUP)
    v = w4[:, :, HEAD_DIM:].reshape(KV_RANK, GROUP)
    return jnp.concatenate([_pad_heads(k), _pad_heads(v)], axis=1)


def _unpad_w_kv(wp):
    k = _unpad_heads(wp[:, :HP]).reshape(KV_RANK, HEADS, HEAD_DIM)
    v = _unpad_heads(wp[:, HP:]).reshape(KV_RANK, HEADS, HEAD_DIM)
    return jnp.concatenate([k, v], axis=-1).reshape(KV_RANK, HEADS * 2 * HEAD_DIM)


def _pad_gain_out(g):
    g = jnp.pad(g.reshape(4 * HEADS, HEAD_DIM), ((0, 0), (0, LANES - HEAD_DIM)))
    return g.reshape(1, 4 * HP)


def _unpad_gain_out(gp):
    return gp.reshape(4 * HEADS, LANES)[:, :HEAD_DIM].reshape(D_MODEL)


_SMALL = (("g_mix_pre", 1024), ("g_mix_post", 1024), ("g_ffn_pre", 1024), ("g_ffn_post", 1024), ("g_mix_out", 1024),
          ("g_q_lora", 256), ("g_kv_lora", 128), ("b_forget", 4))


def _pack_small(vals):
    parts = []
    for name, n in _SMALL:
        a = vals[name].astype(F32)
        if n < LANES:
            a = jnp.pad(a, ((0, 0), (0, LANES - n)))
        parts.append(a)
    return jnp.concatenate(parts, axis=1).reshape(DEPTH * SMALL_ROWS, LANES)


def _unpack_small(packed):
    flat = packed.reshape(DEPTH, SMALL_ROWS * LANES)
    out, off = {}, 0
    for name, n in _SMALL:
        out[name] = flat[:, off:off + n]
        off += max(n, LANES)
    return out


def kernel(x, positions, g_mix_pre, w_in, b_forget, g_q_lora, w_q_up, g_kv_lora, w_kv_up, g_mix_out, w_out, g_mix_post, g_ffn_pre, w_ffn_up, w_ffn_down, g_ffn_post, loss_target, m_g_mix_pre, m_w_in, m_b_forget, m_g_q_lora, m_w_q_up, m_g_kv_lora, m_w_kv_up, m_g_mix_out, m_w_out, m_g_mix_post, m_g_ffn_pre, m_w_ffn_up, m_w_ffn_down, m_g_ffn_post, v_g_mix_pre, v_w_in, v_b_forget, v_g_q_lora, v_w_q_up, v_g_kv_lora, v_w_kv_up, v_g_mix_out, v_w_out, v_g_mix_post, v_g_ffn_pre, v_w_ffn_up, v_w_ffn_down, v_g_ffn_post):
    t = x.shape[1]
    nq = t // BQ
    x0 = x[0]
    tgt = loss_target[0]
    pos = positions[0].astype(F32).reshape(t, 1)

    half_r, half_m = HEAD_DIM // 2, ROPE_MLA // 2
    invf_r = ROPE_BASE ** (-jnp.arange(half_r, dtype=F32) / half_r)
    invf_m = ROPE_BASE ** (-jnp.arange(half_m, dtype=F32) / half_m)
    invf = jnp.concatenate([invf_r, invf_r, invf_m, invf_m,
                            jnp.zeros((LANES - HEAD_DIM - ROPE_MLA,), F32)]).reshape(1, LANES)
    log_gamma = jnp.log1p(-jnp.power(2.0, -5.0 - jnp.arange(HEADS, dtype=F32)))
    lg_lanes = jnp.repeat(log_gamma, LANES).reshape(1, HP)
    lg_heads = jnp.broadcast_to(log_gamma[:, None, None], (HEADS, 1, LANES))

    big = [w_in, w_q_up, w_kv_up, w_out, w_ffn_up, w_ffn_down]
    bf = lambda w: w.astype(BF16)
    first = _exchange([bf(w_in[0]), bf(w_q_up), bf(w_kv_up)], True, "weights_gather_first")
    l0_state, l0_token = _exchange_start([bf(w[0]) for w in (w_out, w_ffn_up, w_ffn_down)], True,
                                         "weights_gather_layer0_start")
    l1_state, rest_token = _exchange_start([bf(w[1]) for w in (w_in, w_out, w_ffn_up, w_ffn_down)], True,
                                           "weights_gather_layer1_start", after=l0_token)
    wq_g = first[1].transpose(1, 2, 0, 3).reshape(DEPTH, Q_RANK, 384)
    wkv_g = first[2].transpose(1, 2, 0, 3).reshape(DEPTH, KV_RANK, 512)

    row = lambda g: g.reshape(1, -1)
    layers = []
    for l in range(DEPTH):
        layers.append(dict(
            wq=_pad_heads(wq_g[l], 96), wkv=_pad_w_kv(wkv_g[l]),
            g_pre=row(g_mix_pre[l]), g_post=row(g_mix_post[l]), g_fpre=row(g_ffn_pre[l]), g_fpost=row(g_ffn_post[l]),
            g_out=_pad_gain_out(g_mix_out[l]), g_q=row(g_q_lora[l]), g_kv=row(g_kv_lora[l]),
            b_pad=jnp.pad(b_forget[l], (0, LANES - HEADS)).reshape(1, LANES)))
    layers[0]["win"] = _pad_w_in(first[0].reshape(D_MODEL, D_IN))

    saved = []
    xin = x0
    h = _rms_fwd(xin, layers[0]["g_pre"] + rest_token[0:1, 0:1], "rms_pre_0")
    loss_row = dx = None
    for l, p in enumerate(layers):
        s = dict(x=xin, h=h)
        proj = _mm(h, p["win"], name=f"in_proj_{l}", tm=512, tn=NP_IN)
        cum = _cumsum(proj, False, f"forget_cumsum_{l}", forget_bias=p["b_pad"])
        fox, ret, sb, mla, fox_t, sb_t, mla_t, ret_t = _prep_fwd(proj, cum, pos, invf, lg_lanes, p["g_q"],
                                                          p["g_kv"], p["wq"], p["wkv"], f"prep_fwd_{l}")
        oa, lse_a = _softmax_fwd(fox, fox_t, chunk_mask=False, scale=1.0, name=f"fox_fwd_{l}")
        ob, lse_b = _softmax_fwd(mla, mla_t, chunk_mask=True, scale=96.0 ** -0.5, name=f"mla_fwd_{l}")
        oc, ret_st = _ret_fwd(ret, ret_t, lg_heads, f"ret_fwd_{l}")
        od, sb_tot = _sb_fwd(sb, sb_t, f"sb_fwd_{l}")
        if l == 0:
            got = _exchange_wait(l0_state, od, True, "weights_gather_layer0_wait")
            p.update(wout=got[0].reshape(D_MODEL, D_MODEL), wup=got[1].transpose(1, 0, 2).reshape(D_MODEL, D_FF),
                     wdn=got[2].reshape(D_FF, D_MODEL))
        mixed = _post_fwd(oa, ob, oc, od, proj, p["g_out"], f"post_fwd_{l}")
        mix = _mm(mixed, p["wout"], name=f"out_proj_{l}", tm=2048)
        x1, h2 = _add_rms_fwd(xin, mix, p["g_post"], p["g_fpre"], f"mix_residual_{l}")
        a = _mm(h2, p["wup"], name=f"ffn_up_{l}", out_dtype=BF16, tm=2048)
        y = _mm(a, p["wdn"], name=f"ffn_down_{l}", a_fn=_relu2, tk=D_FF)
        s.update(proj=proj, fox=fox, ret=ret, sb=sb, mla=mla, fox_t=fox_t, sb_t=sb_t, mla_t=mla_t, ret_st=ret_st, oa=oa, ob=ob, oc=oc,
                 od=od, sb_tot=sb_tot, lse_a=lse_a, lse_b=lse_b, mixed=mixed, mix=mix, x1=x1, h2=h2, a=a, y=y)
        saved.append(s)
        if l == 0:
            got = _exchange_wait(l1_state, y, True, "weights_gather_layer1_wait")
            layers[1].update(win=_pad_w_in(got[0].reshape(D_MODEL, D_IN)), wout=got[1].reshape(D_MODEL, D_MODEL),
                             wup=got[2].transpose(1, 0, 2).reshape(D_MODEL, D_FF), wdn=got[3].reshape(D_FF, D_MODEL))
        if l + 1 < DEPTH:
            xin, h = _add_rms_fwd(x1, y, p["g_fpost"], layers[l + 1]["g_pre"], f"ffn_residual_{l}")
        else:
            loss_row, dx = _final_loss(x1, y, p["g_fpost"], tgt, "loss")

    small_g = {n: [None] * DEPTH for n, _ in _SMALL}
    big_g = [[None] * DEPTH for _ in range(6)]
    to_send = [
        lambda g: g.reshape(N_DEV, 1, D_MODEL // N_DEV, D_IN),
        lambda g: g.reshape(Q_RANK, N_DEV, 384 // N_DEV).transpose(1, 0, 2)[:, None],
        lambda g: g.reshape(KV_RANK, N_DEV, 512 // N_DEV).transpose(1, 0, 2)[:, None],
        lambda g: g.reshape(N_DEV, 1, D_MODEL // N_DEV, D_MODEL),
        lambda g: g.reshape(D_MODEL, N_DEV, D_FF // N_DEV).transpose(1, 0, 2)[:, None],
        lambda g: g.reshape(N_DEV, 1, D_FF // N_DEV, D_MODEL),
    ]
    send_of = lambda ks, l: [to_send[k](big_g[k][l]).astype(BF16) for k in ks]
    late_state = early_state = None
    order_token = jnp.zeros((1, 1), F32)
    for l in reversed(range(DEPTH)):
        p, s = layers[l], saved[l]
        dy, dg = _norm_bwd(dx, s["y"], p["g_fpost"] + order_token, None, BF16, f"ffn_post_bwd_{l}")
        small_g["g_ffn_post"][l] = dg
        da = _mm(dy, p["wdn"], name=f"ffn_down_dx_{l}", tb=True, out_dtype=BF16, epi=_drelu2, epi_in=s["a"],
                 tm=2048)
        big_g[5][l] = _mm(s["a"], dy, name=f"ffn_down_dw_{l}", ta=True, a_fn=_relu2, tk=t, out_dtype=BF16)
        big_g[4][l] = _mm(s["h2"], da, name=f"ffn_up_dw_{l}", ta=True, tk=t, out_dtype=BF16)
        dh2 = _mm(da, p["wup"], name=f"ffn_up_dx_{l}", tb=True, tk=D_FF)
        dx1, dg = _norm_bwd(dh2, s["x1"], p["g_fpre"], dx, F32, f"ffn_pre_bwd_{l}")
        small_g["g_ffn_pre"][l] = dg
        dmix, dg = _norm_bwd(dx1, s["mix"], p["g_post"], None, BF16, f"mix_post_bwd_{l}")
        small_g["g_mix_post"][l] = dg
        dmixed = _mm(dmix, p["wout"], name=f"out_proj_dx_{l}", tb=True, tm=2048)
        big_g[3][l] = _mm(s["mixed"], dmix, name=f"out_proj_dw_{l}", ta=True, tk=t, out_dtype=BF16)
        g_out = p["g_out"]
        if l == 0:
            early_state, early_token = _exchange_start(send_of((3, 4, 5), 0), False, "grads_layer0_early_start")
            g_out = g_out + early_token[0:1, 0:1]
        doa, dob, doc, dod, dla, dlb, drg, dgo = _post_bwd(dmixed, s["oa"], s["ob"], s["oc"], s["od"], s["proj"],
                                                           g_out, f"post_bwd_{l}")
        small_g["g_mix_out"][l] = _unpad_gain_out(dgo).reshape(1, D_MODEL)
        as_rows = lambda a: a[:, :HEADS].T.reshape(HEADS, t // BQA, 1, BQA)
        dfq, dfk, dfv, dcum_k, dcum_q = _softmax_bwd(s["fox"], s["fox_t"], doa, s["lse_a"], as_rows(dla), bias=True,
                                                     chunk_mask=False, scale=1.0, name=f"fox_bwd_{l}")
        dmq, dmk, dmv = _softmax_bwd(s["mla"], s["mla_t"], dob, s["lse_b"], as_rows(dlb), bias=False, chunk_mask=True,
                                     scale=96.0 ** -0.5, name=f"mla_bwd_{l}")
        drq, drk, drv = _ret_bwd(s["ret"], s["ret_st"], lg_heads, doc, f"ret_bwd_{l}")
        dsq, dsk, dsv = _sb_bwd(s["sb"], s["sb_t"], dod, s["sb_tot"], f"sb_bwd_{l}")
        dcum_q = jnp.pad(dcum_q.reshape(HEADS, t).T, ((0, 0), (0, LANES - HEADS)))
        dlsf = _cumsum(dcum_q, True, f"forget_cumsum_bwd_{l}", partials=dcum_k)
        dproj, dwq, dwkv, dgq, dgkv, dbf = _prep_bwd(
            (dfq, dfk, dfv), (drq, drk, drv), (dsq, dsk, dsv), (dmq, dmk, dmv), drg, dlsf, s["proj"], pos, invf,
            lg_lanes, p["b_pad"], p["g_q"], p["g_kv"], p["wq"], p["wkv"], f"prep_bwd_{l}")
        small_g["g_q_lora"][l] = dgq
        small_g["g_kv_lora"][l] = dgkv
        small_g["b_forget"][l] = dbf[:, :HEADS]
        big_g[1][l] = _unpad_heads(dwq, 96)
        big_g[2][l] = _unpad_w_kv(dwkv)
        dh = _mm(dproj, p["win"], name=f"in_proj_dx_{l}", tb=True, tk=NP_IN)
        big_g[0][l] = _unpad_w_in(_mm(s["h"], dproj, name=f"in_proj_dw_{l}", ta=True, tn=NP_IN // 2, tk=2048,
                                          out_dtype=BF16))
        g_pre = p["g_pre"]
        if l == 0:
            last_state, last_token = _exchange_start(send_of((0, 1, 2), 0), False, "grads_layer0_rest_start")
            g_pre = g_pre + last_token[0:1, 0:1]
        dx, dg = _norm_bwd(dh, s["x"], g_pre, dx1, F32, f"mix_pre_bwd_{l}")
        small_g["g_mix_pre"][l] = dg
        if l == DEPTH - 1:
            late_state, late_token = _exchange_start(send_of(range(6), l), False, "grads_layer1_start")
            order_token = late_token[0:1, 0:1]
    grad_x = dx.reshape(1, t, D_MODEL)

    res = {}
    small_w = dict(g_mix_pre=g_mix_pre, g_mix_post=g_mix_post, g_ffn_pre=g_ffn_pre, g_ffn_post=g_ffn_post,
                   g_mix_out=g_mix_out, g_q_lora=g_q_lora, g_kv_lora=g_kv_lora, b_forget=b_forget)
    small_m = dict(g_mix_pre=m_g_mix_pre, g_mix_post=m_g_mix_post, g_ffn_pre=m_g_ffn_pre, g_ffn_post=m_g_ffn_post,
                   g_mix_out=m_g_mix_out, g_q_lora=m_g_q_lora, g_kv_lora=m_g_kv_lora, b_forget=m_b_forget)
    small_v = dict(g_mix_pre=v_g_mix_pre, g_mix_post=v_g_mix_post, g_ffn_pre=v_g_ffn_pre, g_ffn_post=v_g_ffn_post,
                   g_mix_out=v_g_mix_out, g_q_lora=v_g_q_lora, g_kv_lora=v_g_kv_lora, b_forget=v_b_forget)
    n_small = DEPTH * SMALL_ROWS
    extra = lambda a: jnp.concatenate([a, jnp.zeros((8, LANES), F32)], axis=0)
    part = jnp.concatenate([_pack_small({n: jnp.concatenate(small_g[n], axis=0) for n, _ in _SMALL}),
                            jnp.broadcast_to(loss_row, (8, LANES))], axis=0)
    sres = _small_allreduce_adam(part, extra(_pack_small(small_w)), extra(_pack_small(small_m)),
                                 extra(_pack_small(small_v)), "small_allreduce_adamw")
    loss = sres[0][n_small, 0]
    sg, sd, sm, sv = [_unpack_small(a[:n_small]) for a in sres]
    for n, _ in _SMALL:
        res[n] = [sg[n], sd[n], sm[n], sv[n]]

    late = _exchange_wait(late_state, dx, False, "grads_layer1_wait")
    early = _exchange_wait(early_state, dx, False, "grads_layer0_early_wait")
    ms = [m_w_in, m_w_q_up, m_w_kv_up, m_w_out, m_w_ffn_up, m_w_ffn_down]
    vs = [v_w_in, v_w_q_up, v_w_kv_up, v_w_out, v_w_ffn_up, v_w_ffn_down]
    names = ["w_in", "w_q_up", "w_kv_up", "w_out", "w_ffn_up", "w_ffn_down"]
    for k in (3, 4, 5):
        recv = jnp.concatenate([early[k - 3], late[k]], axis=1)
        res[names[k]] = _reduce_adam(recv, big[k], ms[k], vs[k], f"adamw_{names[k]}")
    done = [sres[0]] + [res[names[k]][0] for k in (3, 4, 5)]
    last = _exchange_wait(last_state, done, False, "grads_layer0_rest_wait")
    for k in (0, 1, 2):
        recv = jnp.concatenate([last[k], late[k]], axis=1)
        res[names[k]] = _reduce_adam(recv, big[k], ms[k], vs[k], f"adamw_{names[k]}")

    order = ["g_mix_pre", "w_in", "b_forget", "g_q_lora", "w_q_up", "g_kv_lora", "w_kv_up", "g_mix_out", "w_out",
             "g_mix_post", "g_ffn_pre", "w_ffn_up", "w_ffn_down", "g_ffn_post"]
    outs = [loss, grad_x]
    for idx in range(4):
        outs += [res[n][idx] for n in order]
    return tuple(outs)
```

```python
import functools
import math

import numpy as np
import jax
import jax.numpy as jnp
from jax import lax
from jax.experimental import pallas as pl
from jax.experimental.pallas import tpu as pltpu

F32 = jnp.float32
BF16 = jnp.bfloat16

D_MODEL = 1024
DEPTH = 2
N_DEV = 8
GROUP = 256
HEADS = 4
HEAD_DIM = 64
LANES = 128
HP = HEADS * LANES
QKV = 3 * HP
Q_RANK = 256
KV_RANK = 128
ROPE_MLA = 32
MLA_SCALE = (HEAD_DIM + ROPE_MLA) ** -0.5
D_FF = 4096
D_IN = 2980
CHUNK_SHIFT = 6
EPS = 1e-6
ROPE_BASE = 10000.0
NEG = -1e30

QKV_IN = 3 * GROUP
OFF_FOX, OFF_RET, OFF_SB = 0, QKV_IN, 2 * QKV_IN
OFF_RG = 3 * QKV_IN
OFF_CQ = OFF_RG + GROUP
OFF_CKV = OFF_CQ + Q_RANK
OFF_KR = OFF_CKV + LANES
OFF_FF = OFF_KR + LANES
NP_IN = 3328

BQ = 256
TKS = 128
TM = 256
VMEM_LIMIT = 58 * 1024 * 1024

ADAM_LR, ADAM_B1, ADAM_B2, ADAM_EPS, ADAM_WD, ADAM_STEP = 0.001, 0.9, 0.999, 1e-08, 0.01, 10
ADAM_C1 = 1.0 - ADAM_B1 ** ADAM_STEP
ADAM_C2 = 1.0 - ADAM_B2 ** ADAM_STEP

SMALL_ROWS = 44

NT = (((1,), (1,)), ((), ()))
TN = (((0,), (0,)), ((), ()))


def _cp(*sem):
    return pltpu.CompilerParams(dimension_semantics=sem if sem else None, vmem_limit_bytes=VMEM_LIMIT)


def _bdot(a, b, dn=None):
    if dn is None:
        return jnp.dot(a, b, preferred_element_type=F32)
    return lax.dot_general(a, b, dn, preferred_element_type=F32)


def _split2(x):
    hi = x.astype(BF16)
    lo = (x - hi.astype(F32)).astype(BF16)
    return hi, lo


def _mm(a, b, *, name, ta=False, tb=False, out_dtype=F32, a_fn=None, epi=None, epi_in=None,
        tm=1024, tn=1024, tk=1024):
    m, k = (a.shape[1], a.shape[0]) if ta else a.shape
    n = b.shape[0] if tb else b.shape[1]
    tm, tn, tk = min(tm, m), min(tn, n), min(tk, k)
    assert m % tm == 0 and n % tn == 0 and k % tk == 0, (name, m, n, k)
    nk = k // tk
    dn = (((0 if ta else 1,), (1 if tb else 0,)), ((), ()))

    def body(*refs):
        if epi is None:
            a_ref, b_ref, o_ref = refs[:3]
            e_ref = None
            rest = refs[3:]
        else:
            a_ref, b_ref, e_ref, o_ref = refs[:4]
            rest = refs[4:]
        av = a_ref[...]
        if a_fn is not None:
            av = a_fn(av)
        part = lax.dot_general(av.astype(BF16), b_ref[...].astype(BF16), dn, preferred_element_type=F32)

        def finish(r):
            if epi is not None:
                r = epi(r, e_ref[...])
            o_ref[...] = r.astype(out_dtype)

        if nk == 1:
            finish(part)
        else:
            acc_ref = rest[0]
            kk = pl.program_id(2)

            @pl.when(kk == 0)
            def _():
                acc_ref[...] = part

            @pl.when(kk > 0)
            def _():
                acc_ref[...] += part

            @pl.when(kk == nk - 1)
            def _():
                finish(acc_ref[...])

    a_spec = pl.BlockSpec((tk, tm), lambda i, j, kk: (kk, i)) if ta else pl.BlockSpec((tm, tk), lambda i, j, kk: (i, kk))
    b_spec = pl.BlockSpec((tn, tk), lambda i, j, kk: (j, kk)) if tb else pl.BlockSpec((tk, tn), lambda i, j, kk: (kk, j))
    o_spec = pl.BlockSpec((tm, tn), lambda i, j, kk: (i, j))
    in_specs = [a_spec, b_spec]
    args = [a, b]
    if epi is not None:
        in_specs.append(o_spec)
        args.append(epi_in)
    return pl.pallas_call(
        body, name=name, grid=(m // tm, n // tn, nk),
        in_specs=in_specs, out_specs=o_spec,
        out_shape=jax.ShapeDtypeStruct((m, n), out_dtype),
        scratch_shapes=[pltpu.VMEM((tm, tn), F32)] if nk > 1 else [],
        compiler_params=_cp("parallel", "parallel", "arbitrary"),
    )(*args)


def _relu2(v):
    r = jnp.maximum(v, 0.0)
    return r * r


def _drelu2(du, av):
    return du * (2.0 * jnp.maximum(av, 0.0))


def _rms(v, g):
    r = lax.rsqrt(jnp.mean(v * v, axis=-1, keepdims=True) + EPS)
    return v * r * g


def _row_spec(w):
    return pl.BlockSpec((TM, w), lambda i: (i, 0))


def _vec_spec(w):
    return pl.BlockSpec((1, w), lambda i: (0, 0))


def _rms_fwd(x, g, name):
    t, d = x.shape

    def body(x_ref, g_ref, h_ref):
        h_ref[...] = _rms(x_ref[...], g_ref[...]).astype(BF16)

    return pl.pallas_call(
        body, name=name, grid=(t // TM,), in_specs=[_row_spec(d), _vec_spec(d)], out_specs=_row_spec(d),
        out_shape=jax.ShapeDtypeStruct((t, d), BF16), compiler_params=_cp("parallel"))(x, g)


def _add_rms_fwd(x, y, g1, g2, name):
    t, d = x.shape

    def body(x_ref, y_ref, g1_ref, g2_ref, xn_ref, h_ref):
        xn = x_ref[...] + _rms(y_ref[...], g1_ref[...])
        xn_ref[...] = xn
        h_ref[...] = _rms(xn, g2_ref[...]).astype(BF16)

    return pl.pallas_call(
        body, name=name, grid=(t // TM,),
        in_specs=[_row_spec(d), _row_spec(d), _vec_spec(d), _vec_spec(d)],
        out_specs=[_row_spec(d), _row_spec(d)],
        out_shape=[jax.ShapeDtypeStruct((t, d), F32), jax.ShapeDtypeStruct((t, d), BF16)],
        compiler_params=_cp("parallel"))(x, y, g1, g2)


def _final_loss(x, y, g, tgt, name):
    t, d = x.shape

    def body(x_ref, y_ref, g_ref, t_ref, l_ref, dx_ref):
        @pl.when(pl.program_id(0) == 0)
        def _():
            l_ref[...] = jnp.zeros_like(l_ref)

        err = x_ref[...] + _rms(y_ref[...], g_ref[...]) - t_ref[...]
        dx_ref[...] = err * (1.0 / d)
        l_ref[...] += jnp.sum(jnp.sum(err * err, axis=1, keepdims=True), axis=0, keepdims=True) * (0.5 / d)

    return pl.pallas_call(
        body, name=name, grid=(t // TM,),
        in_specs=[_row_spec(d), _row_spec(d), _vec_spec(d), _row_spec(d)],
        out_specs=[pl.BlockSpec((1, LANES), lambda i: (0, 0)), _row_spec(d)],
        out_shape=[jax.ShapeDtypeStruct((1, LANES), F32), jax.ShapeDtypeStruct((t, d), F32)],
        compiler_params=_cp("arbitrary"))(x, y, g, tgt)


def _rms_bwd_vals(dn, v, g):
    w = v.shape[-1]
    r = lax.rsqrt(jnp.mean(v * v, axis=-1, keepdims=True) + EPS)
    vh = v * r
    dgp = jnp.sum(dn * vh, axis=0, keepdims=True)
    dvh = dn * g
    dv = r * (dvh - vh * (jnp.sum(dvh * vh, axis=-1, keepdims=True) * (1.0 / w)))
    return dv, dgp


def _norm_bwd(dn, v, g, resid, out_dtype, name):
    t, d = v.shape
    has_res = resid is not None

    def body(*refs):
        if has_res:
            dn_ref, v_ref, g_ref, r_ref, dv_ref, dg_ref = refs
        else:
            dn_ref, v_ref, g_ref, dv_ref, dg_ref = refs

        @pl.when(pl.program_id(0) == 0)
        def _():
            dg_ref[...] = jnp.zeros_like(dg_ref)

        dv, dgp = _rms_bwd_vals(dn_ref[...].astype(F32), v_ref[...], g_ref[...])
        if has_res:
            dv = dv + r_ref[...]
        dv_ref[...] = dv.astype(out_dtype)
        dg_ref[...] += dgp

    in_specs = [_row_spec(d), _row_spec(d), _vec_spec(d)] + ([_row_spec(d)] if has_res else [])
    args = [dn, v, g] + ([resid] if has_res else [])
    return pl.pallas_call(
        body, name=name, grid=(t // TM,), in_specs=in_specs,
        out_specs=[_row_spec(d), _vec_spec(d)],
        out_shape=[jax.ShapeDtypeStruct((t, d), out_dtype), jax.ShapeDtypeStruct((1, d), F32)],
        compiler_params=_cp("arbitrary"))(*args)


def _rope_trig(pos, invf):
    ang = pos * invf
    return jnp.cos(ang), jnp.sin(ang)


def _rope_tables(trig, w, lo, half):
    c, s = trig
    lane = lax.broadcasted_iota(jnp.int32, c.shape, 1)
    active = (lane >= lo) & (lane < lo + 2 * half)
    cos = jnp.concatenate([jnp.where(active, c, 1.0)] * (w // LANES), axis=1)
    sin = jnp.concatenate([jnp.where(active, s, 0.0)] * (w // LANES), axis=1)
    lanes = lax.broadcasted_iota(jnp.int32, (c.shape[0], w), 1) & (LANES - 1)
    first = (lanes >= lo) & (lanes < lo + half)
    second = (lanes >= lo + half) & (lanes < lo + 2 * half)
    return cos, sin, first, second


def _rope_apply(v, cos, sin, first, second, half, sign):
    w = v.shape[-1]
    up = pltpu.roll(v, w - half, 1)
    dn = pltpu.roll(v, half, 1)
    rot = jnp.where(first, -up, jnp.where(second, dn, 0.0))
    return v * cos + rot * (sin * sign)


def _split3(c):
    hi = c.astype(BF16).astype(F32)
    mid = (c - hi).astype(BF16).astype(F32)
    return hi, mid, (c - hi) - mid


def _spread_heads(x):
    low = lax.broadcasted_iota(jnp.int32, (x.shape[0], LANES), 1) < HEAD_DIM
    out = []
    for c in range(x.shape[1] // LANES):
        blk = x[:, c * LANES:(c + 1) * LANES]
        out.append(jnp.where(low, blk, 0.0))
        out.append(jnp.where(low, pltpu.roll(blk, HEAD_DIM, 1), 0.0))
    return jnp.concatenate(out, axis=1)


def _gather_heads(y):
    low = lax.broadcasted_iota(jnp.int32, (y.shape[0], LANES), 1) < HEAD_DIM
    out = []
    for c in range(y.shape[1] // (2 * LANES)):
        a = y[:, 2 * c * LANES:(2 * c + 1) * LANES]
        b = y[:, (2 * c + 1) * LANES:(2 * c + 2) * LANES]
        out.append(jnp.where(low, a, pltpu.roll(b, HEAD_DIM, 1)))
    return jnp.concatenate(out, axis=1)


def _transposed_tiles(dst, row0, blocks, width):
    for b, blk in enumerate(blocks):
        bt = blk.T.astype(BF16)
        rows = bt.shape[0]
        for w in range(TM // width):
            dst[w, row0 + b * rows:row0 + (b + 1) * rows, :] = bt[:, w * width:(w + 1) * width]


def _prep_fwd(proj, cum, pos, invf, lg_lanes, g_q, g_kv, wq_pad, wkv_pad, name):
    t = proj.shape[0]

    def body(fox_ref, ret_ref, sb_ref, cq_ref, ckv_ref, kr_ref, cum_ref, pos_ref, invf_ref, lg_ref,
             gq_ref, gkv_ref, wq_ref, wkv_ref,
             ofox_ref, oret_ref, osb_ref, omla_ref, ofoxt_ref, osbt_ref, omlat_ref, orett_ref):
        pos_v = pos_ref[...]
        foxv, retv, sbv = [[_spread_heads(ref[:, g * GROUP:(g + 1) * GROUP]) for g in range(3)]
                           for ref in (fox_ref, ret_ref, sb_ref)]
        osb_ref[:, 0:HP] = (sbv[0] * 0.125).astype(BF16)
        osb_ref[:, HP:2 * HP] = sbv[1].astype(BF16)
        osb_ref[:, 2 * HP:QKV] = sbv[2].astype(BF16)
        _transposed_tiles(osbt_ref, 0, [sbv[1], sbv[2]], TKS)
        lane = lax.broadcasted_iota(jnp.int32, (TM, LANES), 1)
        cumv = cum_ref[...]
        fq, fk = [], []
        for hb in range(HEADS):
            hi, mid, lo = _split3(cumv[:, hb:hb + 1])
            q = foxv[0][:, hb * LANES:(hb + 1) * LANES] * 0.125
            k = foxv[1][:, hb * LANES:(hb + 1) * LANES]
            ones_q = (lane >= HEAD_DIM) & (lane < HEAD_DIM + 3)
            ones_k = (lane >= HEAD_DIM + 3) & (lane < HEAD_DIM + 6)
            q = jnp.where(ones_q, 1.0, q)
            k = jnp.where(ones_k, 1.0, k)
            for n, part in enumerate((hi, mid, lo)):
                q = jnp.where(lane == HEAD_DIM + 3 + n, part, q)
                k = jnp.where(lane == HEAD_DIM + n, -part, k)
            fq.append(q)
            fk.append(k)
        fk = jnp.concatenate(fk, axis=1)
        ofox_ref[:, 0:HP] = jnp.concatenate(fq, axis=1).astype(BF16)
        ofox_ref[:, HP:2 * HP] = fk.astype(BF16)
        ofox_ref[:, 2 * HP:QKV] = foxv[2].astype(BF16)
        _transposed_tiles(ofoxt_ref, 0, [fk, foxv[2]], BQ)
        trig = _rope_trig(pos_v, invf_ref[...])
        cos, sin, first, second = _rope_tables(trig, HP, 0, HEAD_DIM // 2)
        nloc = lax.broadcasted_iota(jnp.int32, (TM, 1), 0).astype(F32)
        dec = lg_ref[...] * nloc
        rq = _rope_apply(retv[0], cos, sin, first, second, HEAD_DIM // 2, 1.0)
        rk = _rope_apply(retv[1], cos, sin, first, second, HEAD_DIM // 2, 1.0)
        oret_ref[:, 0:HP] = (rq * jnp.exp(dec)).astype(BF16)
        rk = rk * 0.125 * jnp.exp(-dec)
        oret_ref[:, HP:2 * HP] = rk.astype(BF16)
        _transposed_tiles(orett_ref, 0, [rk], BQ)
        oret_ref[:, 2 * HP:QKV] = retv[2].astype(BF16)
        cosm, sinm, firstm, secondm = _rope_tables(trig, HP, HEAD_DIM, ROPE_MLA // 2)
        cqn = _rms(cq_ref[...], gq_ref[...]).astype(BF16)
        qm = _bdot(cqn, wq_ref[...])
        qm = _rope_apply(qm, cosm, sinm, firstm, secondm, ROPE_MLA // 2, 1.0)
        omla_ref[:, 0:HP] = (qm * MLA_SCALE).astype(BF16)
        ckvn = _rms(ckv_ref[...], gkv_ref[...]).astype(BF16)
        kv = _bdot(ckvn, wkv_ref[...])
        krr = _rope_apply(kr_ref[...], cosm[:, 0:LANES], sinm[:, 0:LANES], firstm[:, 0:LANES],
                          secondm[:, 0:LANES], ROPE_MLA // 2, 1.0)
        mk = kv[:, 0:HP] + jnp.concatenate([krr] * HEADS, axis=1)
        omla_ref[:, HP:2 * HP] = mk.astype(BF16)
        omla_ref[:, 2 * HP:QKV] = kv[:, HP:2 * HP].astype(BF16)
        _transposed_tiles(omlat_ref, 0, [mk, kv[:, HP:2 * HP]], BQ)

    def seg(off, w):
        return pl.BlockSpec((TM, w), lambda i, o=off // w: (i, o))

    def full(shape):
        return pl.BlockSpec(shape, lambda i: (0,) * len(shape))

    def tiles(width):
        return pl.BlockSpec((TM // width, 2 * HP, width), lambda i: (i, 0, 0))

    in_specs = [seg(OFF_FOX, QKV_IN), seg(OFF_RET, QKV_IN), seg(OFF_SB, QKV_IN), seg(OFF_CQ, Q_RANK),
                seg(OFF_CKV, LANES), seg(OFF_KR, LANES), _row_spec(LANES), pl.BlockSpec((TM, 1), lambda i: (i, 0)),
                full((1, LANES)), full((1, HP)), full((1, Q_RANK)), full((1, KV_RANK)),
                full((Q_RANK, HP)), full((KV_RANK, 2 * HP))]
    out_specs = [_row_spec(QKV)] * 4 + [tiles(BQ), tiles(TKS), tiles(BQ),
                                        pl.BlockSpec((1, HP, BQ), lambda i: (i, 0, 0))]
    out_shape = [jax.ShapeDtypeStruct((t, QKV), BF16)] * 4 + [
        jax.ShapeDtypeStruct((t // BQ, 2 * HP, BQ), BF16), jax.ShapeDtypeStruct((t // TKS, 2 * HP, TKS), BF16),
        jax.ShapeDtypeStruct((t // BQ, 2 * HP, BQ), BF16), jax.ShapeDtypeStruct((t // BQ, HP, BQ), BF16)]
    return pl.pallas_call(
        body, name=name, grid=(t // TM,), in_specs=in_specs, out_specs=out_specs, out_shape=out_shape,
        compiler_params=_cp("parallel"))(proj, proj, proj, proj, proj, proj, cum, pos, invf, lg_lanes,
                                         g_q, g_kv, wq_pad, wkv_pad)


def _prep_bwd(dfox, dret, dsb, dmla, drg, dlsf, proj, pos, invf, lg_lanes, b_pad, g_q, g_kv,
              wq_pad, wkv_pad, name):
    t = proj.shape[0]

    def body(dfq, dfk, dfv, drq, drk, drv, dsq, dsk, dsv, dmq, dmk, dmv, drg_ref, dlsf_ref,
             cq_ref, ckv_ref, ff_ref, pos_ref, invf_ref, lg_ref, b_ref, gq_ref, gkv_ref, wq_ref, wkv_ref,
             dp_ref, dwq_ref, dwkv_ref, dgq_ref, dgkv_ref, dbf_ref):
        @pl.when(pl.program_id(0) == 0)
        def _():
            dwq_ref[...] = jnp.zeros_like(dwq_ref)
            dwkv_ref[...] = jnp.zeros_like(dwkv_ref)
            dgq_ref[...] = jnp.zeros_like(dgq_ref)
            dgkv_ref[...] = jnp.zeros_like(dgkv_ref)
            dbf_ref[...] = jnp.zeros_like(dbf_ref)

        pos_v = pos_ref[...]
        def put(off, val):
            dp_ref[:, off:off + GROUP] = _gather_heads(val).astype(BF16)

        for off, (dq, dk, dv) in ((OFF_FOX, (dfq, dfk, dfv)), (OFF_SB, (dsq, dsk, dsv))):
            put(off, dq[...] * 0.125)
            put(off + GROUP, dk[...])
            put(off + 2 * GROUP, dv[...])
        trig = _rope_trig(pos_v, invf_ref[...])
        cos, sin, first, second = _rope_tables(trig, HP, 0, HEAD_DIM // 2)
        nloc = lax.broadcasted_iota(jnp.int32, (TM, 1), 0).astype(F32)
        dec = lg_ref[...] * nloc
        dq = _rope_apply(drq[...] * jnp.exp(dec), cos, sin, first, second, HEAD_DIM // 2, -1.0)
        dk = _rope_apply(drk[...] * (0.125 * jnp.exp(-dec)), cos, sin, first, second, HEAD_DIM // 2, -1.0)
        put(OFF_RET, dq)
        put(OFF_RET + GROUP, dk)
        put(OFF_RET + 2 * GROUP, drv[...])
        put(OFF_RG, drg_ref[...])
        cosm, sinm, firstm, secondm = _rope_tables(trig, HP, HEAD_DIM, ROPE_MLA // 2)
        dql = _rope_apply(dmq[...] * MLA_SCALE, cosm, sinm, firstm, secondm, ROPE_MLA // 2, -1.0).astype(BF16)
        cq = cq_ref[...]
        cqn = _rms(cq, gq_ref[...]).astype(BF16)
        dwq_ref[...] += _bdot(cqn, dql, TN)
        dcqn = _bdot(dql, wq_ref[...], NT)
        dcq, dgq = _rms_bwd_vals(dcqn, cq, gq_ref[...])
        dgq_ref[...] += dgq
        dp_ref[:, OFF_CQ:OFF_CQ + Q_RANK] = dcq.astype(BF16)
        dkm = dmk[...]
        dkv = jnp.concatenate([dkm, dmv[...]], axis=1).astype(BF16)
        ckv = ckv_ref[...]
        ckvn = _rms(ckv, gkv_ref[...]).astype(BF16)
        dwkv_ref[...] += _bdot(ckvn, dkv, TN)
        dckvn = _bdot(dkv, wkv_ref[...], NT)
        dckv, dgkv = _rms_bwd_vals(dckvn, ckv, gkv_ref[...])
        dgkv_ref[...] += dgkv
        dp_ref[:, OFF_CKV:OFF_CKV + LANES] = dckv.astype(BF16)
        dkr = dkm[:, 0:LANES] + dkm[:, LANES:2 * LANES] + dkm[:, 2 * LANES:3 * LANES] + dkm[:, 3 * LANES:HP]
        act = firstm[:, 0:LANES] | secondm[:, 0:LANES]
        dkr = jnp.where(act, dkr, 0.0)
        dkr = _rope_apply(dkr, cosm[:, 0:LANES], sinm[:, 0:LANES], firstm[:, 0:LANES], secondm[:, 0:LANES],
                          ROPE_MLA // 2, -1.0)
        dp_ref[:, OFF_KR:OFF_KR + LANES] = dkr.astype(BF16)
        f = ff_ref[...] + b_ref[...]
        dff = dlsf_ref[...] / (1.0 + jnp.exp(f))
        dbf_ref[...] += jnp.sum(dff, axis=0, keepdims=True)
        dp_ref[:, OFF_FF:OFF_FF + LANES] = dff.astype(BF16)
        dp_ref[:, OFF_FF + LANES:NP_IN] = jnp.zeros((TM, NP_IN - OFF_FF - LANES), BF16)

    def seg(off, w):
        return pl.BlockSpec((TM, w), lambda i, o=off // w: (i, o))

    def full(shape):
        return pl.BlockSpec(shape, lambda i: (0,) * len(shape))

    hp_spec = _row_spec(HP)
    in_specs = [hp_spec] * 13 + [_row_spec(LANES), seg(OFF_CQ, Q_RANK), seg(OFF_CKV, LANES), seg(OFF_FF, LANES),
                                 pl.BlockSpec((TM, 1), lambda i: (i, 0)),
                                 full((1, LANES)), full((1, HP)), full((1, LANES)), full((1, Q_RANK)),
                                 full((1, KV_RANK)), full((Q_RANK, HP)), full((KV_RANK, 2 * HP))]
    out_specs = [_row_spec(NP_IN), full((Q_RANK, HP)), full((KV_RANK, 2 * HP)), full((1, Q_RANK)),
                 full((1, KV_RANK)), full((1, LANES))]
    out_shape = [jax.ShapeDtypeStruct((t, NP_IN), BF16), jax.ShapeDtypeStruct((Q_RANK, HP), F32),
                 jax.ShapeDtypeStruct((KV_RANK, 2 * HP), F32), jax.ShapeDtypeStruct((1, Q_RANK), F32),
                 jax.ShapeDtypeStruct((1, KV_RANK), F32), jax.ShapeDtypeStruct((1, LANES), F32)]
    return pl.pallas_call(
        body, name=name, grid=(t // TM,), in_specs=in_specs, out_specs=out_specs, out_shape=out_shape,
        compiler_params=_cp("arbitrary"))(*dfox, *dret, *dsb, *dmla, drg, dlsf, proj, proj, proj, pos, invf,
                                          lg_lanes, b_pad, g_q, g_kv, wq_pad, wkv_pad)


TC = 1024


def _cumsum(x, reverse, name, partials=None, forget_bias=None):
    t = x.shape[0]
    w = LANES
    tc = min(TC, t)
    n = t // tc
    xs = [x] + ([] if partials is None else [partials]) + ([] if forget_bias is None else [forget_bias])

    def body(*refs):
        x_refs, o_ref, carry = refs[:len(xs)], refs[len(xs)], refs[len(xs) + 1]

        @pl.when(pl.program_id(0) == 0)
        def _():
            carry[...] = jnp.zeros_like(carry)

        r = lax.broadcasted_iota(jnp.int32, (tc, tc), 0)
        c = lax.broadcasted_iota(jnp.int32, (tc, tc), 1)
        tri = jnp.where((r <= c) if reverse else (r >= c), 1.0, 0.0).astype(BF16)
        v = x_refs[0][...]
        if forget_bias is not None:
            f = v + x_refs[-1][...]
            v = -(jnp.maximum(-f, 0.0) + jnp.log(1.0 + jnp.exp(-jnp.abs(f))))
        if partials is not None:
            lane = lax.broadcasted_iota(jnp.int32, (tc, LANES), 1)
            for hb in range(HEADS):
                v = v + jnp.where(lane == hb, jnp.sum(x_refs[1][:, _hs(hb)], axis=1, keepdims=True), 0.0)
        hi = v.astype(BF16)
        r1 = v - hi.astype(F32)
        mid = r1.astype(BF16)
        lo = (r1 - mid.astype(F32)).astype(BF16)
        cs = _bdot(tri, hi) + _bdot(tri, mid) + _bdot(tri, lo) + carry[...]
        o_ref[...] = cs
        carry[...] = cs[0:1, :] if reverse else cs[tc - 1:tc, :]

    step = (lambda i: n - 1 - i) if reverse else (lambda i: i)
    x_col = OFF_FF // LANES if forget_bias is not None else 0
    in_specs = [pl.BlockSpec((tc, LANES), lambda i: (step(i), x_col))]
    if partials is not None:
        in_specs.append(pl.BlockSpec((tc, HP), lambda i: (step(i), 0)))
    if forget_bias is not None:
        in_specs.append(_vec_spec(LANES))
    return pl.pallas_call(
        body, name=name, grid=(n,), in_specs=in_specs, out_specs=pl.BlockSpec((tc, w), lambda i: (step(i), 0)),
        out_shape=jax.ShapeDtypeStruct((t, w), F32), scratch_shapes=[pltpu.VMEM((1, w), F32)],
        compiler_params=_cp("arbitrary"))(*xs)


HB_FWD = 4
HB_BWD = 4
HB_SB_FWD = 4
BQS = 512
BQA = 512


def _q_spec(hb, bq=BQ):
    return pl.BlockSpec((bq, hb * LANES), lambda g, i: (i, g))


ONE_BUFFER = pl.Buffered(1)


def _kv_spec(t, which, hb):
    return pl.BlockSpec((t, hb * LANES), lambda g, i, w=which: (0, w * (HEADS // hb) + g), pipeline_mode=ONE_BUFFER)


def _acc_spec(t, hb):
    return pl.BlockSpec((t, hb * LANES), lambda g, i: (0, g), pipeline_mode=ONE_BUFFER)


def _hs(hh):
    return slice(hh * LANES, (hh + 1) * LANES)


def _tile_iota(rows, cols):
    return (lax.broadcasted_iota(jnp.int32, (rows, cols), 0), lax.broadcasted_iota(jnp.int32, (rows, cols), 1))


def _kvt_spec(nkv, width, which, hb):
    return pl.BlockSpec((nkv, hb * LANES, width), lambda g, i, w=which: (0, w * (HEADS // hb) + g, 0),
                        pipeline_mode=ONE_BUFFER)


def _qrow_spec(hb, bq=BQ):
    return pl.BlockSpec((hb, 1, 1, bq), lambda g, i: (g, i, 0, 0))


def _vis(key0, query0, rows, cols, kind):
    r, c = _tile_iota(rows, cols)
    k, q = key0 + r, query0 + c
    if kind == "chunk":
        return (k >> CHUNK_SHIFT) <= (q >> CHUNK_SHIFT)
    return (k < q) if kind == "strict" else (k <= q)


def _softmax_fwd(qkv, kvt, *, chunk_mask, scale, name):
    t = qkv.shape[0]
    nq = t // BQA
    per = BQA // BQ
    hb = HB_FWD
    kind = "chunk" if chunk_mask else "causal"

    def body(q_ref, k_ref, vt_ref, o_ref, lse_ref, m_sc, l_sc, acc_sc):
        i = pl.program_id(1)
        m_sc[...] = jnp.full((hb, 1, BQA), NEG, F32)
        l_sc[...] = jnp.zeros((hb, 1, BQA), F32)
        acc_sc[...] = jnp.zeros((hb, LANES, BQA), F32)

        def tile(j, qoff):
            off = pl.multiple_of(j * BQ, BQ)
            lo = 0 if qoff is None else qoff
            qs = slice(lo, BQA)
            vis = None if qoff is None else _vis(off, i * BQA + lo, BQ, BQA - lo, kind)
            ss = [_bdot(k_ref[pl.ds(off, BQ), _hs(hh)], q_ref[qs, _hs(hh)], NT) for hh in range(hb)]
            for hh in range(hb):
                s = ss[hh]
                if scale != 1.0:
                    s = s * scale
                if vis is not None:
                    s = jnp.where(vis, s, NEG)
                m_old = m_sc[hh, :, qs]
                m_new = jnp.maximum(m_old, jnp.max(s, axis=0, keepdims=True))
                alpha = jnp.exp(m_old - m_new)
                p = jnp.exp(s - m_new)
                l_sc[hh, :, qs] = alpha * l_sc[hh, :, qs] + jnp.sum(p, axis=0, keepdims=True)
                m_sc[hh, :, qs] = m_new
                acc_sc[hh, :, qs] = alpha * acc_sc[hh, :, qs] + _bdot(vt_ref[j, _hs(hh), :], p.astype(BF16))

        def loop(j, carry):
            tile(j, None)
            return carry

        lax.fori_loop(0, per * i, loop, 0)
        for d in range(per):
            tile(per * i + d, d * BQ)
        for hh in range(hb):
            l = l_sc[hh]
            o_ref[:, _hs(hh)] = (acc_sc[hh] / l).T
            lse_ref[hh, 0] = m_sc[hh] + jnp.log(l)

    return pl.pallas_call(
        body, name=name, grid=(HEADS // hb, nq),
        in_specs=[_q_spec(hb, BQA), _kv_spec(t, 1, hb), _kvt_spec(t // BQ, BQ, 1, hb)],
        out_specs=[_q_spec(hb, BQA), _qrow_spec(hb, BQA)],
        out_shape=[jax.ShapeDtypeStruct((t, HP), F32), jax.ShapeDtypeStruct((HEADS, nq, 1, BQA), F32)],
        scratch_shapes=[pltpu.VMEM((hb, 1, BQA), F32), pltpu.VMEM((hb, 1, BQA), F32),
                        pltpu.VMEM((hb, LANES, BQA), F32)],
        compiler_params=_cp("parallel", "arbitrary"))(qkv, qkv, kvt)


def _softmax_bwd(qkv, kvt, do, lse, delta, *, bias, chunk_mask, scale, name):
    t = qkv.shape[0]
    nq = t // BQA
    per = BQA // BQ
    hb = HB_BWD
    kind = "chunk" if chunk_mask else "causal"

    def body(*refs):
        if bias:
            (q_ref, k_ref, v_ref, kt_ref, do_ref, lse_ref, dl_ref, dq_ref, dk_ref, dv_ref, dck_ref, dcq_ref,
             dq_sc, dcq_sc) = refs
            dcq_sc[...] = jnp.zeros((hb, 1, BQA), F32)
        else:
            q_ref, k_ref, v_ref, kt_ref, do_ref, lse_ref, dl_ref, dq_ref, dk_ref, dv_ref, dq_sc = refs
        i = pl.program_id(1)

        @pl.when(i == 0)
        def _():
            dk_ref[...] = jnp.zeros_like(dk_ref)
            dv_ref[...] = jnp.zeros_like(dv_ref)
            if bias:
                dck_ref[...] = jnp.zeros_like(dck_ref)

        dq_sc[...] = jnp.zeros((hb, LANES, BQA), F32)

        def tile(j, qoff):
            off = pl.multiple_of(j * BQ, BQ)
            lo = 0 if qoff is None else qoff
            qsl = slice(lo, BQA)
            vis = None if qoff is None else _vis(off, i * BQA + lo, BQ, BQA - lo, kind)
            qs = [q_ref[qsl, _hs(hh)] for hh in range(hb)]
            dobs = [do_ref[qsl, _hs(hh)].astype(BF16) for hh in range(hb)]
            ss = [_bdot(k_ref[pl.ds(off, BQ), _hs(hh)], qs[hh], NT) for hh in range(hb)]
            dps = [_bdot(v_ref[pl.ds(off, BQ), _hs(hh)], dobs[hh], NT) for hh in range(hb)]
            pbs, dsbs = [], []
            for hh in range(hb):
                s = ss[hh]
                if scale != 1.0:
                    s = s * scale
                p = jnp.exp(s - lse_ref[hh, 0, :, qsl])
                if vis is not None:
                    p = jnp.where(vis, p, 0.0)
                ds = p * (dps[hh] - dl_ref[hh, 0, :, qsl])
                if bias:
                    part = ds[:, 0:LANES]
                    for b in range(1, (BQA - lo) // LANES):
                        part = part + ds[:, b * LANES:(b + 1) * LANES]
                    dck_ref[pl.ds(off, BQ), _hs(hh)] -= part
                    dcq_sc[hh, :, qsl] += jnp.sum(ds, axis=0, keepdims=True)
                if scale != 1.0:
                    ds = ds * scale
                pbs.append(p.astype(BF16))
                dsbs.append(ds.astype(BF16))
            for hh in range(hb):
                sl = _hs(hh)
                dv_ref[pl.ds(off, BQ), sl] += _bdot(pbs[hh], dobs[hh])
                dk_ref[pl.ds(off, BQ), sl] += _bdot(dsbs[hh], qs[hh])
                dq_sc[hh, :, qsl] += _bdot(kt_ref[j, sl, :], dsbs[hh])

        def loop(j, carry):
            tile(j, None)
            return carry

        lax.fori_loop(0, per * i, loop, 0)
        for d in range(per):
            tile(per * i + d, d * BQ)
        for hh in range(hb):
            dq_ref[:, _hs(hh)] = dq_sc[hh].T
            if bias:
                dcq_ref[hh, 0] = dcq_sc[hh]

    in_specs = [_q_spec(hb, BQA), _kv_spec(t, 1, hb), _kv_spec(t, 2, hb), _kvt_spec(t // BQ, BQ, 0, hb),
                _q_spec(hb, BQA), _qrow_spec(hb, BQA), _qrow_spec(hb, BQA)]
    out_specs = [_q_spec(hb, BQA), _acc_spec(t, hb), _acc_spec(t, hb)]
    out_shape = [jax.ShapeDtypeStruct((t, HP), F32)] * 3
    scratch = [pltpu.VMEM((hb, LANES, BQA), F32)]
    if bias:
        out_specs += [_acc_spec(t, hb), _qrow_spec(hb, BQA)]
        out_shape += [jax.ShapeDtypeStruct((t, HP), F32), jax.ShapeDtypeStruct((HEADS, nq, 1, BQA), F32)]
        scratch.append(pltpu.VMEM((hb, 1, BQA), F32))
    return pl.pallas_call(
        body, name=name, grid=(HEADS // hb, nq), in_specs=in_specs, out_specs=out_specs, out_shape=out_shape,
        scratch_shapes=scratch,
        compiler_params=_cp("parallel", "arbitrary"))(qkv, qkv, qkv, kvt, do, lse, delta)


def _ret_diag_decay(lg1, keys_on_rows=False):
    r, c = _tile_iota(BQ, BQ)
    qn, km = (c, r) if keys_on_rows else (r, c)
    dd = jnp.where(km > qn, jnp.exp((2.0 * lg1) * (km - qn).astype(F32)), 1.0)
    return jnp.where((km >> CHUNK_SHIFT) <= (qn >> CHUNK_SHIFT), dd, 0.0)


def _lg_spec(hb):
    return pl.BlockSpec((hb, 1, LANES), lambda g, i: (g, 0, 0))


def _ret_specs(nq, hb, reverse):
    tile = (lambda i: nq - 1 - i) if reverse else (lambda i: i)
    qkv = [pl.BlockSpec((BQ, hb * LANES), lambda g, i, w=w: (tile(i), w * (HEADS // hb) + g)) for w in range(3)]
    kt = pl.BlockSpec((1, hb * LANES, BQ), lambda g, i: (tile(i), g, 0))
    st = pl.BlockSpec((1, hb * LANES, LANES), lambda g, i: (tile(i), g, 0))
    return qkv, kt, st


def _ret_fwd(qkv, kt, lg_heads, name):
    t = qkv.shape[0]
    nq = t // BQ
    hb = HB_FWD

    def body(lg_ref, q_ref, k_ref, v_ref, kt_ref, o_ref, st_ref, s_sc):
        @pl.when(pl.program_id(1) == 0)
        def _():
            s_sc[...] = jnp.zeros_like(s_sc)

        qs = [q_ref[:, _hs(hh)] for hh in range(hb)]
        vs = [v_ref[:, _hs(hh)] for hh in range(hb)]
        aa = [_bdot(qs[hh], k_ref[:, _hs(hh)], NT) for hh in range(hb)]
        kv = [_bdot(kt_ref[0, _hs(hh), :], vs[hh]) for hh in range(hb)]
        for hh in range(hb):
            sl = _hs(hh)
            lg1 = lg_ref[hh][:, 0:1]
            s = s_sc[hh]
            st_ref[0, sl, :] = s
            shi, slo = _split2(s)
            a = (aa[hh] * _ret_diag_decay(lg1)).astype(BF16)
            o_ref[:, sl] = _bdot(a, vs[hh]) + _bdot(qs[hh], shi) + _bdot(qs[hh], slo)
            s_sc[hh] = jnp.exp(lg1 * float(BQ)) * (s + kv[hh])

    qkv_specs, kt_spec, st_spec = _ret_specs(nq, hb, False)
    return pl.pallas_call(
        body, name=name, grid=(HEADS // hb, nq), in_specs=[_lg_spec(hb)] + qkv_specs + [kt_spec],
        out_specs=[_q_spec(hb), st_spec],
        out_shape=[jax.ShapeDtypeStruct((t, HP), F32), jax.ShapeDtypeStruct((nq, HP, LANES), F32)],
        scratch_shapes=[pltpu.VMEM((hb, LANES, LANES), F32)],
        compiler_params=_cp("parallel", "arbitrary"))(lg_heads, qkv, qkv, qkv, kt)


def _ret_bwd(qkv, states, lg_heads, do, name):
    t = qkv.shape[0]
    nq = t // BQ
    hb = HB_FWD

    def body(lg_ref, q_ref, k_ref, v_ref, st_ref, do_ref, dq_ref, dk_ref, dv_ref, g_sc):
        @pl.when(pl.program_id(1) == 0)
        def _():
            g_sc[...] = jnp.zeros_like(g_sc)

        qs = [q_ref[:, _hs(hh)] for hh in range(hb)]
        ks = [k_ref[:, _hs(hh)] for hh in range(hb)]
        vs = [v_ref[:, _hs(hh)] for hh in range(hb)]
        dobs = [do_ref[:, _hs(hh)].astype(BF16) for hh in range(hb)]
        aa = [_bdot(ks[hh], qs[hh], NT) for hh in range(hb)]
        das = [_bdot(vs[hh], dobs[hh], NT) for hh in range(hb)]
        qdo = [_bdot(qs[hh], dobs[hh], TN) for hh in range(hb)]
        for hh in range(hb):
            sl = _hs(hh)
            lg1 = lg_ref[hh][:, 0:1]
            dd = _ret_diag_decay(lg1, keys_on_rows=True)
            at = (aa[hh] * dd).astype(BF16)
            dat = (das[hh] * dd).astype(BF16)
            h = jnp.exp(lg1 * float(BQ)) * g_sc[hh]
            hhi, hlo = _split2(h)
            shi, slo = _split2(st_ref[0, sl, :])
            dv_ref[:, sl] = _bdot(at, dobs[hh]) + _bdot(ks[hh], hhi) + _bdot(ks[hh], hlo)
            dk_ref[:, sl] = _bdot(dat, qs[hh]) + _bdot(vs[hh], hhi, NT) + _bdot(vs[hh], hlo, NT)
            dq_ref[:, sl] = _bdot(dat, ks[hh], TN) + _bdot(dobs[hh], shi, NT) + _bdot(dobs[hh], slo, NT)
            g_sc[hh] = qdo[hh] + h

    qkv_specs, _, st_spec = _ret_specs(nq, hb, True)
    tile_spec = pl.BlockSpec((BQ, hb * LANES), lambda g, i: (nq - 1 - i, g))
    return pl.pallas_call(
        body, name=name, grid=(HEADS // hb, nq), in_specs=[_lg_spec(hb)] + qkv_specs + [st_spec, tile_spec],
        out_specs=[tile_spec] * 3, out_shape=[jax.ShapeDtypeStruct((t, HP), F32)] * 3,
        scratch_shapes=[pltpu.VMEM((hb, LANES, LANES), F32)],
        compiler_params=_cp("parallel", "arbitrary"))(lg_heads, qkv, qkv, qkv, states, do)


def _sb_tile_logs(q, kb, vis):
    z = _bdot(kb, q, NT)
    ls = -(jnp.maximum(z, 0.0) + jnp.log(1.0 + jnp.exp(-jnp.abs(z))))
    if vis is not None:
        ls = jnp.where(vis, ls, 0.0)
    return z, ls


def _sb_later(ls, after):
    hi, lo = _split2(ls)
    return _bdot(after, hi) + _bdot(after, lo)


def _sb_fwd(qkv, kvt, name):
    t = qkv.shape[0]
    nq = t // BQS
    per = BQS // TKS
    hb = HB_SB_FWD

    def body(q_ref, k_ref, vt_ref, o_ref, tot_ref, acc_sc, r_sc):
        i = pl.program_id(1)
        acc_sc[...] = jnp.zeros((hb, LANES, BQS), F32)
        r_sc[...] = jnp.zeros((hb, 1, BQS), F32)
        mr, mc = _tile_iota(TKS, TKS)
        after = jnp.where(mc > mr, 1.0, 0.0).astype(BF16)

        def tile(j, qoff):
            off = pl.multiple_of(j * TKS, TKS)
            lo = 0 if qoff is None else qoff
            qsl = slice(lo, BQS)
            vis = None if qoff is None else _vis(off, i * BQS + lo, TKS, BQS - lo, "strict")
            zl = [_sb_tile_logs(q_ref[qsl, _hs(hh)], k_ref[pl.ds(off, TKS), _hs(hh)], vis) for hh in range(hb)]
            laters = [_sb_later(zl[hh][1], after) for hh in range(hb)]
            ws = []
            for hh in range(hb):
                z, ls = zl[hh]
                w = jnp.exp(z + ls + laters[hh] + r_sc[hh, :, qsl])
                if vis is not None:
                    w = jnp.where(vis, w, 0.0)
                ws.append(w.astype(BF16))
                r_sc[hh, :, qsl] += jnp.sum(ls, axis=0, keepdims=True)
            for hh in range(hb):
                acc_sc[hh, :, qsl] += _bdot(vt_ref[j, _hs(hh), :], ws[hh])

        for d in reversed(range(per)):
            tile(per * i + d, d * TKS)

        def loop(jj, carry):
            tile(per * i - 1 - jj, None)
            return carry

        lax.fori_loop(0, per * i, loop, 0)
        for hh in range(hb):
            o_ref[:, _hs(hh)] = acc_sc[hh].T
            tot_ref[hh, 0] = r_sc[hh]

    return pl.pallas_call(
        body, name=name, grid=(HEADS // hb, nq),
        in_specs=[_q_spec(hb, BQS), _kv_spec(t, 1, hb), _kvt_spec(t // TKS, TKS, 1, hb)],
        out_specs=[_q_spec(hb, BQS), _qrow_spec(hb, BQS)],
        out_shape=[jax.ShapeDtypeStruct((t, HP), F32), jax.ShapeDtypeStruct((HEADS, nq, 1, BQS), F32)],
        scratch_shapes=[pltpu.VMEM((hb, LANES, BQS), F32), pltpu.VMEM((hb, 1, BQS), F32)],
        compiler_params=_cp("parallel", "arbitrary"))(qkv, qkv, kvt)


def _sb_bwd(qkv, kvt, do, tot, name):
    t = qkv.shape[0]
    nq = t // BQS
    per = BQS // TKS
    hb = HB_BWD

    def body(q_ref, k_ref, v_ref, kt_ref, do_ref, tot_ref, dq_ref, dk_ref, dv_ref, dq_sc, p_sc, g_sc):
        i = pl.program_id(1)

        @pl.when(i == 0)
        def _():
            dk_ref[...] = jnp.zeros_like(dk_ref)
            dv_ref[...] = jnp.zeros_like(dv_ref)

        dq_sc[...] = jnp.zeros((hb, LANES, BQS), F32)
        p_sc[...] = jnp.zeros((hb, 1, BQS), F32)
        g_sc[...] = jnp.zeros((hb, 1, BQS), F32)
        mr, mc = _tile_iota(TKS, TKS)
        after = jnp.where(mc > mr, 1.0, 0.0).astype(BF16)
        before = jnp.where(mc < mr, 1.0, 0.0).astype(BF16)

        def tile(j, qoff):
            off = pl.multiple_of(j * TKS, TKS)
            lo = 0 if qoff is None else qoff
            qsl = slice(lo, BQS)
            vis = None if qoff is None else _vis(off, i * BQS + lo, TKS, BQS - lo, "strict")
            qs = [q_ref[qsl, _hs(hh)] for hh in range(hb)]
            dobs = [do_ref[qsl, _hs(hh)].astype(BF16) for hh in range(hb)]
            zl = [_sb_tile_logs(qs[hh], k_ref[pl.ds(off, TKS), _hs(hh)], vis) for hh in range(hb)]
            dws = [_bdot(v_ref[pl.ds(off, TKS), _hs(hh)], dobs[hh], NT) for hh in range(hb)]
            laters = [_sb_later(zl[hh][1], after) for hh in range(hb)]
            ws, gs = [], []
            for hh in range(hb):
                z, ls = zl[hh]
                own = jnp.sum(ls, axis=0, keepdims=True)
                rest = tot_ref[hh, 0, :, qsl] - p_sc[hh, :, qsl] - own
                w = jnp.exp(z + ls + laters[hh] + rest)
                if vis is not None:
                    w = jnp.where(vis, w, 0.0)
                p_sc[hh, :, qsl] += own
                ws.append(w.astype(BF16))
                gs.append(dws[hh] * w)
            gins = [_bdot(before, gs[hh].astype(BF16)) for hh in range(hb)]
            dzbs = []
            for hh in range(hb):
                g = gs[hh]
                stay = jnp.exp(zl[hh][1])
                dz = g * stay - (1.0 - stay) * (gins[hh] + g_sc[hh, :, qsl])
                if vis is not None:
                    dz = jnp.where(vis, dz, 0.0)
                g_sc[hh, :, qsl] += jnp.sum(g, axis=0, keepdims=True)
                dzbs.append(dz.astype(BF16))
            for hh in range(hb):
                sl = _hs(hh)
                dv_ref[pl.ds(off, TKS), sl] += _bdot(ws[hh], dobs[hh])
                dk_ref[pl.ds(off, TKS), sl] += _bdot(dzbs[hh], qs[hh])
                dq_sc[hh, :, qsl] += _bdot(kt_ref[j, sl, :], dzbs[hh])

        def loop(j, carry):
            tile(j, None)
            return carry

        lax.fori_loop(0, per * i, loop, 0)
        for d in range(per):
            tile(per * i + d, d * TKS)
        for hh in range(hb):
            dq_ref[:, _hs(hh)] = dq_sc[hh].T

    return pl.pallas_call(
        body, name=name, grid=(HEADS // hb, nq),
        in_specs=[_q_spec(hb, BQS), _kv_spec(t, 1, hb), _kv_spec(t, 2, hb), _kvt_spec(t // TKS, TKS, 0, hb),
                  _q_spec(hb, BQS), _qrow_spec(hb, BQS)],
        out_specs=[_q_spec(hb, BQS), _acc_spec(t, hb), _acc_spec(t, hb)],
        out_shape=[jax.ShapeDtypeStruct((t, HP), F32)] * 3,
        scratch_shapes=[pltpu.VMEM((hb, LANES, BQS), F32), pltpu.VMEM((hb, 1, BQS), F32),
                        pltpu.VMEM((hb, 1, BQS), F32)],
        compiler_params=_cp("parallel", "arbitrary"))(qkv, qkv, qkv, kvt, do, tot)


def _sigmoid(v):
    return 1.0 / (1.0 + jnp.exp(-v))


def _post_fwd(oa, ob, oc, od, proj, g_pad, name):
    t = oa.shape[0]

    def body(oa_ref, ob_ref, oc_ref, od_ref, rg_ref, g_ref, mx_ref):
        g = g_ref[...]

        def group(o, gg):
            r = lax.rsqrt(jnp.sum(o * o, axis=-1, keepdims=True) * (1.0 / GROUP) + EPS)
            return _gather_heads(o * r * gg).astype(BF16)

        mx_ref[:, 0:GROUP] = group(oa_ref[...], g[:, 0:HP])
        mx_ref[:, GROUP:2 * GROUP] = group(ob_ref[...], g[:, HP:2 * HP])
        mx_ref[:, 3 * GROUP:4 * GROUP] = group(od_ref[...], g[:, 3 * HP:4 * HP])
        real = lax.broadcasted_iota(jnp.int32, (TM, LANES), 1) < HEAD_DIM
        rg = _spread_heads(rg_ref[...])
        gated = []
        for hb in range(HEADS):
            sl = slice(hb * LANES, (hb + 1) * LANES)
            o = oc_ref[:, sl]
            mu = jnp.sum(o, axis=-1, keepdims=True) * (1.0 / HEAD_DIM)
            dlt = jnp.where(real, o - mu, 0.0)
            var = jnp.sum(dlt * dlt, axis=-1, keepdims=True) * (1.0 / HEAD_DIM)
            yn = dlt * lax.rsqrt(var + EPS) * g[:, 2 * HP + hb * LANES:2 * HP + (hb + 1) * LANES]
            x = rg[:, sl]
            gated.append(yn * (x * _sigmoid(x)))
        mx_ref[:, 2 * GROUP:3 * GROUP] = _gather_heads(jnp.concatenate(gated, axis=1)).astype(BF16)

    rg_spec = pl.BlockSpec((TM, GROUP), lambda i: (i, OFF_RG // GROUP))
    return pl.pallas_call(
        body, name=name, grid=(t // TM,),
        in_specs=[_row_spec(HP)] * 4 + [rg_spec, _vec_spec(4 * HP)], out_specs=_row_spec(D_MODEL),
        out_shape=jax.ShapeDtypeStruct((t, D_MODEL), BF16), compiler_params=_cp("parallel"))(oa, ob, oc, od, proj, g_pad)


def _post_bwd(dmx, oa, ob, oc, od, proj, g_pad, name):
    t = oa.shape[0]

    def body(dm_ref, oa_ref, ob_ref, oc_ref, od_ref, rg_ref, g_ref,
             doa_ref, dob_ref, doc_ref, dod_ref, dla_ref, dlb_ref, drg_ref, dg_ref):
        @pl.when(pl.program_id(0) == 0)
        def _():
            dg_ref[...] = jnp.zeros_like(dg_ref)

        g = g_ref[...]

        def group_bwd(dm, o, gg):
            r = lax.rsqrt(jnp.sum(o * o, axis=-1, keepdims=True) * (1.0 / GROUP) + EPS)
            oh = o * r
            dgp = jnp.sum(dm * oh, axis=0, keepdims=True)
            dyh = dm * gg
            do = r * (dyh - oh * (jnp.sum(dyh * oh, axis=-1, keepdims=True) * (1.0 / GROUP)))
            return do, dgp

        def delta_bc(do, o):
            prod = do * o
            lane = lax.broadcasted_iota(jnp.int32, (TM, LANES), 1)
            out = jnp.zeros((TM, LANES), F32)
            for hb in range(HEADS):
                out = jnp.where(lane == hb, jnp.sum(prod[:, hb * LANES:(hb + 1) * LANES], axis=-1, keepdims=True), out)
            return out

        dmp = [_spread_heads(dm_ref[:, gi * GROUP:(gi + 1) * GROUP]) for gi in range(4)]
        rg = _spread_heads(rg_ref[...])
        oa = oa_ref[...]
        do_a, dga = group_bwd(dmp[0], oa, g[:, 0:HP])
        doa_ref[...] = do_a.astype(BF16)
        dla_ref[...] = delta_bc(do_a, oa)
        dg_ref[:, 0:HP] += dga
        ob = ob_ref[...]
        do_b, dgb = group_bwd(dmp[1], ob, g[:, HP:2 * HP])
        dob_ref[...] = do_b.astype(BF16)
        dlb_ref[...] = delta_bc(do_b, ob)
        dg_ref[:, HP:2 * HP] += dgb
        do_d, dgd = group_bwd(dmp[3], od_ref[...], g[:, 3 * HP:4 * HP])
        dod_ref[...] = do_d.astype(BF16)
        dg_ref[:, 3 * HP:4 * HP] += dgd
        real = lax.broadcasted_iota(jnp.int32, (TM, LANES), 1) < HEAD_DIM
        for hb in range(HEADS):
            sl = slice(hb * LANES, (hb + 1) * LANES)
            gsl = slice(2 * HP + hb * LANES, 2 * HP + (hb + 1) * LANES)
            o = oc_ref[:, sl]
            mu = jnp.sum(o, axis=-1, keepdims=True) * (1.0 / HEAD_DIM)
            dlt = jnp.where(real, o - mu, 0.0)
            var = jnp.sum(dlt * dlt, axis=-1, keepdims=True) * (1.0 / HEAD_DIM)
            rstd = lax.rsqrt(var + EPS)
            dhat = dlt * rstd
            gc = g[:, gsl]
            x = rg[:, sl]
            sg = _sigmoid(x)
            dm = dmp[2][:, sl]
            drg_ref[:, sl] = dm * (dhat * gc) * (sg * (1.0 + x * (1.0 - sg)))
            dyn = dm * (x * sg)
            dg_ref[:, gsl] += jnp.sum(dyn * dhat, axis=0, keepdims=True)
            ddh = dyn * gc
            m1 = jnp.sum(ddh, axis=-1, keepdims=True) * (1.0 / HEAD_DIM)
            m2 = jnp.sum(ddh * dhat, axis=-1, keepdims=True) * (1.0 / HEAD_DIM)
            doc_ref[:, sl] = jnp.where(real, rstd * (ddh - m1 - dhat * m2), 0.0).astype(BF16)

    rg_spec = pl.BlockSpec((TM, GROUP), lambda i: (i, OFF_RG // GROUP))
    hp = _row_spec(HP)
    return pl.pallas_call(
        body, name=name, grid=(t // TM,),
        in_specs=[_row_spec(D_MODEL), hp, hp, hp, hp, rg_spec, _vec_spec(4 * HP)],
        out_specs=[hp] * 4 + [_row_spec(LANES)] * 2 + [hp, _vec_spec(4 * HP)],
        out_shape=[jax.ShapeDtypeStruct((t, HP), BF16)] * 4 + [jax.ShapeDtypeStruct((t, LANES), F32)] * 2
        + [jax.ShapeDtypeStruct((t, HP), F32), jax.ShapeDtypeStruct((1, 4 * HP), F32)],
        compiler_params=_cp("arbitrary"))(dmx, oa, ob, oc, od, proj, g_pad)


def _mesh_pos():
    return lax.axis_index("x"), lax.axis_index("y"), lax.axis_index("c")


def _peer(pos, k):
    x, y, c = pos
    px = 1 - x if (k >> 2) & 1 else x
    py = 1 - y if (k >> 1) & 1 else y
    pc = 1 - c if k & 1 else c
    return (px, py, pc), 4 * px + 2 * py + pc


def _exchange(arrs, gather, name):
    n = len(arrs)

    def body(*refs):
        ins, outs = refs[:n], refs[n:2 * n]
        send_sems, recv_sems, loc_sems = refs[2 * n:]
        pos = _mesh_pos()
        me = 4 * pos[0] + 2 * pos[1] + pos[2]
        local = []
        for a in range(n):
            src = ins[a] if gather else ins[a].at[me]
            cp = pltpu.make_async_copy(src, outs[a].at[me], loc_sems.at[a])
            cp.start()
            local.append(cp)
        sends, recvs = [], []
        for k in range(1, N_DEV):
            peer, pid = _peer(pos, k)
            for a in range(n):
                s = a * (N_DEV - 1) + k - 1
                src = ins[a] if gather else ins[a].at[pid]
                cp = pltpu.make_async_remote_copy(
                    src_ref=src, dst_ref=outs[a].at[me], send_sem=send_sems.at[s], recv_sem=recv_sems.at[s],
                    device_id=peer, device_id_type=pl.DeviceIdType.MESH)
                cp.start()
                sends.append(cp)
                recvs.append(pltpu.make_async_remote_copy(
                    src_ref=src, dst_ref=outs[a].at[pid], send_sem=send_sems.at[s], recv_sem=recv_sems.at[s],
                    device_id=peer, device_id_type=pl.DeviceIdType.MESH))
        for cp in recvs:
            cp.wait_recv()
        for cp in sends:
            cp.wait_send()
        for cp in local:
            cp.wait()

    any_spec = pl.BlockSpec(memory_space=pl.ANY)
    out_shape = [jax.ShapeDtypeStruct((N_DEV,) + tuple(a.shape) if gather else tuple(a.shape), a.dtype) for a in arrs]
    return pl.pallas_call(
        body, name=name, in_specs=[any_spec] * n, out_specs=[any_spec] * n, out_shape=out_shape,
        scratch_shapes=[pltpu.SemaphoreType.DMA((n * (N_DEV - 1),)), pltpu.SemaphoreType.DMA((n * (N_DEV - 1),)),
                        pltpu.SemaphoreType.DMA((n,))],
        compiler_params=pltpu.CompilerParams(has_side_effects=True))(*arrs)


def _device_index():
    x, y, c = _mesh_pos()
    return 4 * x + 2 * y + c


def _landing(srcs, gather):
    me = _device_index()
    lands = []
    for a in srcs:
        own = a[None] if gather else lax.dynamic_slice_in_dim(a, me, 1, axis=0)
        shape = (N_DEV,) + tuple(a.shape) if gather else tuple(a.shape)
        lands.append(lax.dynamic_update_slice_in_dim(lax.empty(shape, a.dtype), own, me, axis=0))
    return lands


def _exchange_copies(ins, lands, send_sems, recv_sems, gather):
    pos = _mesh_pos()
    me = 4 * pos[0] + 2 * pos[1] + pos[2]
    sends, recvs = [], []
    for k in range(1, N_DEV):
        peer, pid = _peer(pos, k)
        for a in range(len(ins)):
            s = a * (N_DEV - 1) + k - 1
            src = ins[a] if gather else ins[a].at[pid]
            sends.append(pltpu.make_async_remote_copy(
                src_ref=src, dst_ref=lands[a].at[me], send_sem=send_sems.at[s], recv_sem=recv_sems.at[s],
                device_id=peer, device_id_type=pl.DeviceIdType.MESH))
            recvs.append(pltpu.make_async_remote_copy(
                src_ref=src, dst_ref=lands[a].at[pid], send_sem=send_sems.at[s], recv_sem=recv_sems.at[s],
                device_id=peer, device_id_type=pl.DeviceIdType.MESH))
    return sends, recvs


def _exchange_start(srcs, gather, name, after=None):
    n = len(srcs)
    lands = _landing(srcs, gather)
    nsem = n * (N_DEV - 1)
    extra = [] if after is None else [after]

    def body(*refs):
        ins, lnd = refs[:n], refs[n:2 * n]
        send_sems, recv_sems = refs[2 * n + len(extra)], refs[2 * n + len(extra) + 1]
        token = refs[-1]
        sends, _ = _exchange_copies(ins, lnd, send_sems, recv_sems, gather)
        for cp in sends:
            cp.start()
        token[...] = jnp.zeros_like(token)

    hbm = pl.BlockSpec(memory_space=pltpu.HBM)
    sem = pl.BlockSpec(memory_space=pltpu.SEMAPHORE)
    bufs = list(srcs) + lands
    out_shape = ([pltpu.SemaphoreType.DMA((nsem,)), pltpu.SemaphoreType.DMA((nsem,))]
                 + [pltpu.HBM(b.shape, b.dtype) for b in bufs] + [jax.ShapeDtypeStruct((8, LANES), F32)])
    outs = pl.pallas_call(
        body, name=name, in_specs=[hbm] * (2 * n) + [pl.BlockSpec(memory_space=pl.ANY)] * len(extra),
        out_specs=[sem, sem] + [hbm] * (2 * n) + [pl.BlockSpec(memory_space=pltpu.VMEM)], out_shape=out_shape,
        input_output_aliases={i: 2 + i for i in range(2 * n)},
        compiler_params=pltpu.CompilerParams(has_side_effects=pltpu.SideEffectType.DATAFLOW_SIDE_EFFECTING),
    )(*[pltpu.with_memory_space_constraint(b, pltpu.HBM) for b in bufs], *extra)
    return (outs[0], outs[1], outs[2:2 + n], outs[2 + n:2 + 2 * n]), outs[-1]


def _exchange_wait(state, after, gather, name):
    send_sems, recv_sems, srcs, lands = state
    n = len(srcs)
    after = list(after) if isinstance(after, (list, tuple)) else [after]

    def body(*refs):
        ins, lnd = refs[:n], refs[n:2 * n]
        s_sems, r_sems = refs[2 * n], refs[2 * n + 1]
        sends, recvs = _exchange_copies(ins, lnd, s_sems, r_sems, gather)
        for cp in sends:
            cp.wait_send()
        for cp in recvs:
            cp.wait_recv()

    hbm = pl.BlockSpec(memory_space=pltpu.HBM)
    sem = pl.BlockSpec(memory_space=pltpu.SEMAPHORE)
    bufs = list(srcs) + list(lands)
    outs = pl.pallas_call(
        body, name=name, in_specs=[hbm] * (2 * n) + [sem, sem] + [pl.BlockSpec(memory_space=pl.ANY)] * len(after),
        out_specs=[hbm] * (2 * n), out_shape=[pltpu.HBM(b.shape, b.dtype) for b in bufs],
        input_output_aliases={i: i for i in range(2 * n)},
        compiler_params=pltpu.CompilerParams(has_side_effects=pltpu.SideEffectType.DATAFLOW_SIDE_EFFECTING),
    )(*bufs, send_sems, recv_sems, *after)
    return outs[n:]


def _adam_vals(w, g, m, v):
    m = ADAM_B1 * m + (1.0 - ADAM_B1) * g
    v = ADAM_B2 * v + (1.0 - ADAM_B2) * (g * g)
    m_hat = m / ADAM_C1
    v_hat = v / ADAM_C2
    delta = -ADAM_LR * (m_hat / (jnp.sqrt(v_hat) + ADAM_EPS) + ADAM_WD * w)
    return delta, m, v


def _small_allreduce_adam(part, w, m, v, name):
    rows = part.shape[0]

    def body(p_ref, w_ref, m_ref, v_ref, g_ref, d_ref, nm_ref, nv_ref, gath, send_sems, recv_sems):
        pos = _mesh_pos()
        me = 4 * pos[0] + 2 * pos[1] + pos[2]
        gath[me] = p_ref[...]
        sends, recvs = [], []
        for k in range(1, N_DEV):
            peer, pid = _peer(pos, k)
            cp = pltpu.make_async_remote_copy(
                src_ref=p_ref, dst_ref=gath.at[me], send_sem=send_sems.at[k - 1], recv_sem=recv_sems.at[k - 1],
                device_id=peer, device_id_type=pl.DeviceIdType.MESH)
            cp.start()
            sends.append(cp)
            recvs.append(pltpu.make_async_remote_copy(
                src_ref=p_ref, dst_ref=gath.at[pid], send_sem=send_sems.at[k - 1], recv_sem=recv_sems.at[k - 1],
                device_id=peer, device_id_type=pl.DeviceIdType.MESH))
        for cp in recvs:
            cp.wait_recv()
        for cp in sends:
            cp.wait_send()
        g = gath[0]
        for p in range(1, N_DEV):
            g = g + gath[p]
        g_ref[...] = g
        d, nm, nv = _adam_vals(w_ref[...], g, m_ref[...], v_ref[...])
        d_ref[...] = d
        nm_ref[...] = nm
        nv_ref[...] = nv

    vm = pl.BlockSpec(memory_space=pltpu.VMEM)
    sds = jax.ShapeDtypeStruct((rows, LANES), F32)
    return pl.pallas_call(
        body, name=name, in_specs=[vm] * 4, out_specs=[vm] * 4, out_shape=[sds] * 4,
        scratch_shapes=[pltpu.VMEM((N_DEV, rows, LANES), F32), pltpu.SemaphoreType.DMA((N_DEV - 1,)),
                        pltpu.SemaphoreType.DMA((N_DEV - 1,))],
        compiler_params=pltpu.CompilerParams(has_side_effects=True))(part, w, m, v)


def _reduce_adam(recvs, w, m, v, name):
    shape = w.shape
    c = shape[-1]
    r = int(np.prod(shape[1:-1]))
    rs = [a.reshape(N_DEV, r, c) for a in recvs]
    w2, m2, v2 = w.reshape(DEPTH * r, c), m.reshape(DEPTH * r, c), v.reshape(DEPTH * r, c)
    tr = r
    while tr * c * 4 > (1 << 20) and tr % 16 == 0:
        tr //= 2
    nt = r // tr

    def body(*refs):
        r_refs = refs[:DEPTH]
        w_ref, m_ref, v_ref, g_ref, d_ref, nm_ref, nv_ref = refs[DEPTH:]
        for l, r_ref in enumerate(r_refs):
            @pl.when(pl.program_id(0) // nt == l)
            def _(r_ref=r_ref):
                g = r_ref[0].astype(F32)
                for p in range(1, N_DEV):
                    g = g + r_ref[p].astype(F32)
                g_ref[...] = g
                d, nm, nv = _adam_vals(w_ref[...], g, m_ref[...], v_ref[...])
                d_ref[...] = d
                nm_ref[...] = nm
                nv_ref[...] = nv

    recv_specs = [pl.BlockSpec((N_DEV, tr, c), lambda i, l=l: (0, jnp.clip(i - l * nt, 0, nt - 1), 0))
                  for l in range(DEPTH)]
    spec = pl.BlockSpec((tr, c), lambda i: (i, 0))
    sds = jax.ShapeDtypeStruct((DEPTH * r, c), F32)
    outs = pl.pallas_call(
        body, name=name, grid=(DEPTH * nt,), in_specs=recv_specs + [spec, spec, spec],
        out_specs=[spec] * 4, out_shape=[sds] * 4, compiler_params=_cp("arbitrary"))(*rs, w2, m2, v2)
    return [o.reshape(shape) for o in outs]


def _pad_heads(w, real=HEAD_DIM):
    lead = w.shape[:-1]
    w = w.reshape(lead + (HEADS, real))
    w = jnp.pad(w, [(0, 0)] * len(lead) + [(0, 0), (0, LANES - real)])
    return w.reshape(lead + (HP,))


def _unpad_heads(w, real=HEAD_DIM):
    lead = w.shape[:-1]
    return w.reshape(lead + (HEADS, LANES))[..., :real].reshape(lead + (HEADS * real,))


_IN_SEGS = (("fq", 0, 256), ("fk", 256, 512), ("fv", 512, 768), ("ff", 768, 772), ("cq", 772, 1028),
            ("ckv", 1028, 1156), ("kr", 1156, 1188), ("rq", 1188, 1444), ("rk", 1444, 1700), ("rv", 1700, 1956),
            ("rg", 1956, 2212), ("sq", 2212, 2468), ("sk", 2468, 2724), ("sv", 2724, 2980))


def _pad_w_in(w):
    s = {n: w[:, a:b] for n, a, b in _IN_SEGS}
    rows = w.shape[0]
    z = lambda n: jnp.zeros((rows, n), w.dtype)
    parts = [s[n] for n in ("fq", "fk", "fv", "rq", "rk", "rv", "sq", "sk", "sv", "rg", "cq", "ckv")]
    parts += [z(HEAD_DIM), s["kr"], z(LANES - HEAD_DIM - ROPE_MLA), s["ff"], z(LANES - HEADS),
              z(NP_IN - OFF_FF - LANES)]
    return jnp.concatenate(parts, axis=1)


def _unpad_w_in(wp):
    seg = lambda off, n=GROUP: wp[:, off:off + n]
    parts = [seg(OFF_FOX), seg(OFF_FOX + GROUP), seg(OFF_FOX + 2 * GROUP), seg(OFF_FF, HEADS),
             seg(OFF_CQ, Q_RANK), seg(OFF_CKV, KV_RANK), seg(OFF_KR + HEAD_DIM, ROPE_MLA),
             seg(OFF_RET), seg(OFF_RET + GROUP), seg(OFF_RET + 2 * GROUP), seg(OFF_RG),
             seg(OFF_SB), seg(OFF_SB + GROUP), seg(OFF_SB + 2 * GROUP)]
    return jnp.concatenate(parts, axis=1)


def _pad_w_kv(w):
    w4 = w.reshape(KV_RANK, HEADS, 2 * HEAD_DIM)
    k = w4[:, :, :HEAD_DIM].reshape(KV_RANK, GROUP)
    v = w4[:, :, HEAD_DIM:].reshape(KV_RANK, GROUP)
    return jnp.concatenate([_pad_heads(k), _pad_heads(v)], axis=1)


def _unpad_w_kv(wp):
    k = _unpad_heads(wp[:, :HP]).reshape(KV_RANK, HEADS, HEAD_DIM)
    v = _unpad_heads(wp[:, HP:]).reshape(KV_RANK, HEADS, HEAD_DIM)
    return jnp.concatenate([k, v], axis=-1).reshape(KV_RANK, HEADS * 2 * HEAD_DIM)


def _pad_gain_out(g):
    g = jnp.pad(g.reshape(4 * HEADS, HEAD_DIM), ((0, 0), (0, LANES - HEAD_DIM)))
    return g.reshape(1, 4 * HP)


def _unpad_gain_out(gp):
    return gp.reshape(4 * HEADS, LANES)[:, :HEAD_DIM].reshape(D_MODEL)


_SMALL = (("g_mix_pre", 1024), ("g_mix_post", 1024), ("g_ffn_pre", 1024), ("g_ffn_post", 1024), ("g_mix_out", 1024),
          ("g_q_lora", 256), ("g_kv_lora", 128), ("b_forget", 4))


def _pack_small(vals):
    parts = []
    for name, n in _SMALL:
        a = vals[name].astype(F32)
        if n < LANES:
            a = jnp.pad(a, ((0, 0), (0, LANES - n)))
        parts.append(a)
    return jnp.concatenate(parts, axis=1).reshape(DEPTH * SMALL_ROWS, LANES)


def _unpack_small(packed):
    flat = packed.reshape(DEPTH, SMALL_ROWS * LANES)
    out, off = {}, 0
    for name, n in _SMALL:
        out[name] = flat[:, off:off + n]
        off += max(n, LANES)
    return out


def kernel(x, positions, g_mix_pre, w_in, b_forget, g_q_lora, w_q_up, g_kv_lora, w_kv_up, g_mix_out, w_out, g_mix_post, g_ffn_pre, w_ffn_up, w_ffn_down, g_ffn_post, loss_target, m_g_mix_pre, m_w_in, m_b_forget, m_g_q_lora, m_w_q_up, m_g_kv_lora, m_w_kv_up, m_g_mix_out, m_w_out, m_g_mix_post, m_g_ffn_pre, m_w_ffn_up, m_w_ffn_down, m_g_ffn_post, v_g_mix_pre, v_w_in, v_b_forget, v_g_q_lora, v_w_q_up, v_g_kv_lora, v_w_kv_up, v_g_mix_out, v_w_out, v_g_mix_post, v_g_ffn_pre, v_w_ffn_up, v_w_ffn_down, v_g_ffn_post):
    t = x.shape[1]
    nq = t // BQ
    x0 = x[0]
    tgt = loss_target[0]
    pos = positions[0].astype(F32).reshape(t, 1)

    half_r, half_m = HEAD_DIM // 2, ROPE_MLA // 2
    invf_r = ROPE_BASE ** (-jnp.arange(half_r, dtype=F32) / half_r)
    invf_m = ROPE_BASE ** (-jnp.arange(half_m, dtype=F32) / half_m)
    invf = jnp.concatenate([invf_r, invf_r, invf_m, invf_m,
                            jnp.zeros((LANES - HEAD_DIM - ROPE_MLA,), F32)]).reshape(1, LANES)
    log_gamma = jnp.log1p(-jnp.power(2.0, -5.0 - jnp.arange(HEADS, dtype=F32)))
    lg_lanes = jnp.repeat(log_gamma, LANES).reshape(1, HP)
    lg_heads = jnp.broadcast_to(log_gamma[:, None, None], (HEADS, 1, LANES))

    big = [w_in, w_q_up, w_kv_up, w_out, w_ffn_up, w_ffn_down]
    bf = lambda w: w.astype(BF16)
    first = _exchange([bf(w_in[0]), bf(w_q_up), bf(w_kv_up)], True, "weights_gather_first")
    l0_state, l0_token = _exchange_start([bf(w[0]) for w in (w_out, w_ffn_up, w_ffn_down)], True,
                                         "weights_gather_layer0_start")
    l1_state, rest_token = _exchange_start([bf(w[1]) for w in (w_in, w_out, w_ffn_up, w_ffn_down)], True,
                                           "weights_gather_layer1_start", after=l0_token)
    wq_g = first[1].transpose(1, 2, 0, 3).reshape(DEPTH, Q_RANK, 384)
    wkv_g = first[2].transpose(1, 2, 0, 3).reshape(DEPTH, KV_RANK, 512)

    row = lambda g: g.reshape(1, -1)
    layers = []
    for l in range(DEPTH):
        layers.append(dict(
            wq=_pad_heads(wq_g[l], 96), wkv=_pad_w_kv(wkv_g[l]),
            g_pre=row(g_mix_pre[l]), g_post=row(g_mix_post[l]), g_fpre=row(g_ffn_pre[l]), g_fpost=row(g_ffn_post[l]),
            g_out=_pad_gain_out(g_mix_out[l]), g_q=row(g_q_lora[l]), g_kv=row(g_kv_lora[l]),
            b_pad=jnp.pad(b_forget[l], (0, LANES - HEADS)).reshape(1, LANES)))
    layers[0]["win"] = _pad_w_in(first[0].reshape(D_MODEL, D_IN))

    saved = []
    xin = x0
    h = _rms_fwd(xin, layers[0]["g_pre"] + rest_token[0:1, 0:1], "rms_pre_0")
    loss_row = dx = None
    for l, p in enumerate(layers):
        s = dict(x=xin, h=h)
        proj = _mm(h, p["win"], name=f"in_proj_{l}", tm=512, tn=NP_IN)
        cum = _cumsum(proj, False, f"forget_cumsum_{l}", forget_bias=p["b_pad"])
        fox, ret, sb, mla, fox_t, sb_t, mla_t, ret_t = _prep_fwd(proj, cum, pos, invf, lg_lanes, p["g_q"],
                                                          p["g_kv"], p["wq"], p["wkv"], f"prep_fwd_{l}")
        oa, lse_a = _softmax_fwd(fox, fox_t, chunk_mask=False, scale=1.0, name=f"fox_fwd_{l}")
        ob, lse_b = _softmax_fwd(mla, mla_t, chunk_mask=True, scale=1.0, name=f"mla_fwd_{l}")
        oc, ret_st = _ret_fwd(ret, ret_t, lg_heads, f"ret_fwd_{l}")
        od, sb_tot = _sb_fwd(sb, sb_t, f"sb_fwd_{l}")
        if l == 0:
            got = _exchange_wait(l0_state, od, True, "weights_gather_layer0_wait")
            p.update(wout=got[0].reshape(D_MODEL, D_MODEL), wup=got[1].transpose(1, 0, 2).reshape(D_MODEL, D_FF),
                     wdn=got[2].reshape(D_FF, D_MODEL))
        mixed = _post_fwd(oa, ob, oc, od, proj, p["g_out"], f"post_fwd_{l}")
        mix = _mm(mixed, p["wout"], name=f"out_proj_{l}", tm=2048)
        x1, h2 = _add_rms_fwd(xin, mix, p["g_post"], p["g_fpre"], f"mix_residual_{l}")
        a = _mm(h2, p["wup"], name=f"ffn_up_{l}", out_dtype=BF16, tm=2048)
        y = _mm(a, p["wdn"], name=f"ffn_down_{l}", a_fn=_relu2, tk=D_FF)
        s.update(proj=proj, fox=fox, ret=ret, sb=sb, mla=mla, fox_t=fox_t, sb_t=sb_t, mla_t=mla_t, ret_st=ret_st, oa=oa, ob=ob, oc=oc,
                 od=od, sb_tot=sb_tot, lse_a=lse_a, lse_b=lse_b, mixed=mixed, mix=mix, x1=x1, h2=h2, a=a, y=y)
        saved.append(s)
        if l == 0:
            got = _exchange_wait(l1_state, y, True, "weights_gather_layer1_wait")
            layers[1].update(win=_pad_w_in(got[0].reshape(D_MODEL, D_IN)), wout=got[1].reshape(D_MODEL, D_MODEL),
                             wup=got[2].transpose(1, 0, 2).reshape(D_MODEL, D_FF), wdn=got[3].reshape(D_FF, D_MODEL))
        if l + 1 < DEPTH:
            xin, h = _add_rms_fwd(x1, y, p["g_fpost"], layers[l + 1]["g_pre"], f"ffn_residual_{l}")
        else:
            loss_row, dx = _final_loss(x1, y, p["g_fpost"], tgt, "loss")

    small_g = {n: [None] * DEPTH for n, _ in _SMALL}
    big_g = [[None] * DEPTH for _ in range(6)]
    to_send = [
        lambda g: g.reshape(N_DEV, 1, D_MODEL // N_DEV, D_IN),
        lambda g: g.reshape(Q_RANK, N_DEV, 384 // N_DEV).transpose(1, 0, 2)[:, None],
        lambda g: g.reshape(KV_RANK, N_DEV, 512 // N_DEV).transpose(1, 0, 2)[:, None],
        lambda g: g.reshape(N_DEV, 1, D_MODEL // N_DEV, D_MODEL),
        lambda g: g.reshape(D_MODEL, N_DEV, D_FF // N_DEV).transpose(1, 0, 2)[:, None],
        lambda g: g.reshape(N_DEV, 1, D_FF // N_DEV, D_MODEL),
    ]
    send_of = lambda ks, l: [to_send[k](big_g[k][l]).astype(BF16) for k in ks]
    late_state = early_state = None
    order_token = jnp.zeros((1, 1), F32)
    for l in reversed(range(DEPTH)):
        p, s = layers[l], saved[l]
        dy, dg = _norm_bwd(dx, s["y"], p["g_fpost"] + order_token, None, BF16, f"ffn_post_bwd_{l}")
        small_g["g_ffn_post"][l] = dg
        da = _mm(dy, p["wdn"], name=f"ffn_down_dx_{l}", tb=True, out_dtype=BF16, epi=_drelu2, epi_in=s["a"],
                 tm=2048)
        big_g[5][l] = _mm(s["a"], dy, name=f"ffn_down_dw_{l}", ta=True, a_fn=_relu2, tk=t, out_dtype=BF16)
        big_g[4][l] = _mm(s["h2"], da, name=f"ffn_up_dw_{l}", ta=True, tk=t, out_dtype=BF16)
        dh2 = _mm(da, p["wup"], name=f"ffn_up_dx_{l}", tb=True, tk=D_FF)
        dx1, dg = _norm_bwd(dh2, s["x1"], p["g_fpre"], dx, F32, f"ffn_pre_bwd_{l}")
        small_g["g_ffn_pre"][l] = dg
        dmix, dg = _norm_bwd(dx1, s["mix"], p["g_post"], None, BF16, f"mix_post_bwd_{l}")
        small_g["g_mix_post"][l] = dg
        dmixed = _mm(dmix, p["wout"], name=f"out_proj_dx_{l}", tb=True, tm=2048)
        big_g[3][l] = _mm(s["mixed"], dmix, name=f"out_proj_dw_{l}", ta=True, tk=t, out_dtype=BF16)
        g_out = p["g_out"]
        if l == 0:
            early_state, early_token = _exchange_start(send_of((3, 4, 5), 0), False, "grads_layer0_early_start")
            g_out = g_out + early_token[0:1, 0:1]
        doa, dob, doc, dod, dla, dlb, drg, dgo = _post_bwd(dmixed, s["oa"], s["ob"], s["oc"], s["od"], s["proj"],
                                                           g_out, f"post_bwd_{l}")
        small_g["g_mix_out"][l] = _unpad_gain_out(dgo).reshape(1, D_MODEL)
        as_rows = lambda a: a[:, :HEADS].T.reshape(HEADS, t // BQA, 1, BQA)
        dfq, dfk, dfv, dcum_k, dcum_q = _softmax_bwd(s["fox"], s["fox_t"], doa, s["lse_a"], as_rows(dla), bias=True,
                                                     chunk_mask=False, scale=1.0, name=f"fox_bwd_{l}")
        dmq, dmk, dmv = _softmax_bwd(s["mla"], s["mla_t"], dob, s["lse_b"], as_rows(dlb), bias=False, chunk_mask=True,
                                     scale=1.0, name=f"mla_bwd_{l}")
        drq, drk, drv = _ret_bwd(s["ret"], s["ret_st"], lg_heads, doc, f"ret_bwd_{l}")
        dsq, dsk, dsv = _sb_bwd(s["sb"], s["sb_t"], dod, s["sb_tot"], f"sb_bwd_{l}")
        dcum_q = jnp.pad(dcum_q.reshape(HEADS, t).T, ((0, 0), (0, LANES - HEADS)))
        dlsf = _cumsum(dcum_q, True, f"forget_cumsum_bwd_{l}", partials=dcum_k)
        dproj, dwq, dwkv, dgq, dgkv, dbf = _prep_bwd(
            (dfq, dfk, dfv), (drq, drk, drv), (dsq, dsk, dsv), (dmq, dmk, dmv), drg, dlsf, s["proj"], pos, invf,
            lg_lanes, p["b_pad"], p["g_q"], p["g_kv"], p["wq"], p["wkv"], f"prep_bwd_{l}")
        small_g["g_q_lora"][l] = dgq
        small_g["g_kv_lora"][l] = dgkv
        small_g["b_forget"][l] = dbf[:, :HEADS]
        big_g[1][l] = _unpad_heads(dwq, 96)
        big_g[2][l] = _unpad_w_kv(dwkv)
        dh = _mm(dproj, p["win"], name=f"in_proj_dx_{l}", tb=True, tk=NP_IN)
        big_g[0][l] = _unpad_w_in(_mm(s["h"], dproj, name=f"in_proj_dw_{l}", ta=True, tn=NP_IN // 2, tk=2048,
                                          out_dtype=BF16))
        g_pre = p["g_pre"]
        if l == 0:
            last_state, last_token = _exchange_start(send_of((0, 1, 2), 0), False, "grads_layer0_rest_start")
            g_pre = g_pre + last_token[0:1, 0:1]
        dx, dg = _norm_bwd(dh, s["x"], g_pre, dx1, F32, f"mix_pre_bwd_{l}")
        small_g["g_mix_pre"][l] = dg
        if l == DEPTH - 1:
            late_state, late_token = _exchange_start(send_of(range(6), l), False, "grads_layer1_start")
            order_token = late_token[0:1, 0:1]
    grad_x = dx.reshape(1, t, D_MODEL)

    res = {}
    small_w = dict(g_mix_pre=g_mix_pre, g_mix_post=g_mix_post, g_ffn_pre=g_ffn_pre, g_ffn_post=g_ffn_post,
                   g_mix_out=g_mix_out, g_q_lora=g_q_lora, g_kv_lora=g_kv_lora, b_forget=b_forget)
    small_m = dict(g_mix_pre=m_g_mix_pre, g_mix_post=m_g_mix_post, g_ffn_pre=m_g_ffn_pre, g_ffn_post=m_g_ffn_post,
                   g_mix_out=m_g_mix_out, g_q_lora=m_g_q_lora, g_kv_lora=m_g_kv_lora, b_forget=m_b_forget)
    small_v = dict(g_mix_pre=v_g_mix_pre, g_mix_post=v_g_mix_post, g_ffn_pre=v_g_ffn_pre, g_ffn_post=v_g_ffn_post,
                   g_mix_out=v_g_mix_out, g_q_lora=v_g_q_lora, g_kv_lora=v_g_kv_lora, b_forget=v_b_forget)
    n_small = DEPTH * SMALL_ROWS
    extra = lambda a: jnp.concatenate([a, jnp.zeros((8, LANES), F32)], axis=0)
    part = jnp.concatenate([_pack_small({n: jnp.concatenate(small_g[n], axis=0) for n, _ in _SMALL}),
                            jnp.broadcast_to(loss_row, (8, LANES))], axis=0)
    sres = _small_allreduce_adam(part, extra(_pack_small(small_w)), extra(_pack_small(small_m)),
                                 extra(_pack_small(small_v)), "small_allreduce_adamw")
    loss = sres[0][n_small, 0]
    sg, sd, sm, sv = [_unpack_small(a[:n_small]) for a in sres]
    for n, _ in _SMALL:
        res[n] = [sg[n], sd[n], sm[n], sv[n]]

    late = _exchange_wait(late_state, dx, False, "grads_layer1_wait")
    early = _exchange_wait(early_state, dx, False, "grads_layer0_early_wait")
    ms = [m_w_in, m_w_q_up, m_w_kv_up, m_w_out, m_w_ffn_up, m_w_ffn_down]
    vs = [v_w_in, v_w_q_up, v_w_kv_up, v_w_out, v_w_ffn_up, v_w_ffn_down]
    names = ["w_in", "w_q_up", "w_kv_up", "w_out", "w_ffn_up", "w_ffn_down"]
    for k in (3, 4, 5):
        res[names[k]] = _reduce_adam([early[k - 3], late[k]], big[k], ms[k], vs[k], f"adamw_{names[k]}")
    done = [sres[0]] + [res[names[k]][0] for k in (3, 4, 5)]
    last = _exchange_wait(last_state, done, False, "grads_layer0_rest_wait")
    for k in (0, 1, 2):
        res[names[k]] = _reduce_adam([last[k], late[k]], big[k], ms[k], vs[k], f"adamw_{names[k]}")

    order = ["g_mix_pre", "w_in", "b_forget", "g_q_lora", "w_q_up", "g_kv_lora", "w_kv_up", "g_mix_out", "w_out",
             "g_mix_post", "g_ffn_pre", "w_ffn_up", "w_ffn_down", "g_ffn_post"]
    outs = [loss, grad_x]
    for idx in range(4):
        outs += [res[n][idx] for n in order]
    return tuple(outs)
```

```python
import functools
import math

import numpy as np
import jax
import jax.numpy as jnp
from jax import lax
from jax.experimental import pallas as pl
from jax.experimental.pallas import tpu as pltpu

F32 = jnp.float32
BF16 = jnp.bfloat16

D_MODEL = 1024
DEPTH = 2
N_DEV = 8
GROUP = 256
HEADS = 4
HEAD_DIM = 64
LANES = 128
HP = HEADS * LANES
QKV = 3 * HP
Q_RANK = 256
KV_RANK = 128
ROPE_MLA = 32
MLA_SCALE = (HEAD_DIM + ROPE_MLA) ** -0.5
D_FF = 4096
D_IN = 2980
CHUNK_SHIFT = 6
EPS = 1e-6
ROPE_BASE = 10000.0
NEG = -1e30

QKV_IN = 3 * GROUP
OFF_FOX, OFF_RET, OFF_SB = 0, QKV_IN, 2 * QKV_IN
OFF_RG = 3 * QKV_IN
OFF_CQ = OFF_RG + GROUP
OFF_CKV = OFF_CQ + Q_RANK
OFF_KR = OFF_CKV + LANES
OFF_FF = OFF_KR + LANES
NP_IN = 3328

BQ = 256
TKS = 128
TM = 256
VMEM_LIMIT = 58 * 1024 * 1024

ADAM_LR, ADAM_B1, ADAM_B2, ADAM_EPS, ADAM_WD, ADAM_STEP = 0.001, 0.9, 0.999, 1e-08, 0.01, 10
ADAM_C1 = 1.0 - ADAM_B1 ** ADAM_STEP
ADAM_C2 = 1.0 - ADAM_B2 ** ADAM_STEP

SMALL_ROWS = 44

NT = (((1,), (1,)), ((), ()))
TN = (((0,), (0,)), ((), ()))


def _cp(*sem):
    return pltpu.CompilerParams(dimension_semantics=sem if sem else None, vmem_limit_bytes=VMEM_LIMIT)


def _bdot(a, b, dn=None):
    if dn is None:
        return jnp.dot(a, b, preferred_element_type=F32)
    return lax.dot_general(a, b, dn, preferred_element_type=F32)


def _split2(x):
    hi = x.astype(BF16)
    lo = (x - hi.astype(F32)).astype(BF16)
    return hi, lo


def _mm(a, b, *, name, ta=False, tb=False, out_dtype=F32, a_fn=None, epi=None, epi_in=None, column_blocks=False,
        tm=1024, tn=1024, tk=1024):
    m, k = (a.shape[1], a.shape[0]) if ta else a.shape
    n = b.shape[0] if tb else b.shape[1]
    tm, tn, tk = min(tm, m), min(tn, n), min(tk, k)
    assert m % tm == 0 and n % tn == 0 and k % tk == 0, (name, m, n, k)
    nk = k // tk
    dn = (((0 if ta else 1,), (1 if tb else 0,)), ((), ()))

    def body(*refs):
        if epi is None:
            a_ref, b_ref, o_ref = refs[:3]
            e_ref = None
            rest = refs[3:]
        else:
            a_ref, b_ref, e_ref, o_ref = refs[:4]
            rest = refs[4:]
        av = a_ref[...]
        if a_fn is not None:
            av = a_fn(av)
        part = lax.dot_general(av.astype(BF16), b_ref[...].astype(BF16), dn, preferred_element_type=F32)

        def finish(r):
            if epi is not None:
                r = epi(r, e_ref[...])
            o_ref[...] = r.astype(out_dtype)

        if nk == 1:
            finish(part)
        else:
            acc_ref = rest[0]
            kk = pl.program_id(2)

            @pl.when(kk == 0)
            def _():
                acc_ref[...] = part

            @pl.when(kk > 0)
            def _():
                acc_ref[...] += part

            @pl.when(kk == nk - 1)
            def _():
                finish(acc_ref[...])

    a_spec = pl.BlockSpec((tk, tm), lambda i, j, kk: (kk, i)) if ta else pl.BlockSpec((tm, tk), lambda i, j, kk: (i, kk))
    b_spec = pl.BlockSpec((tn, tk), lambda i, j, kk: (j, kk)) if tb else pl.BlockSpec((tk, tn), lambda i, j, kk: (kk, j))
    o_spec = pl.BlockSpec((tm, tn), lambda i, j, kk: (i, j))
    in_specs = [a_spec, b_spec]
    args = [a, b]
    if epi is not None:
        in_specs.append(o_spec)
        args.append(epi_in)
    out_shape = jax.ShapeDtypeStruct((m, n), out_dtype)
    if column_blocks:
        assert epi is None
        o_spec = pl.BlockSpec((None, tm, tn), lambda i, j, kk: (j, i, 0))
        out_shape = jax.ShapeDtypeStruct((n // tn, m, tn), out_dtype)
    return pl.pallas_call(
        body, name=name, grid=(m // tm, n // tn, nk),
        in_specs=in_specs, out_specs=o_spec,
        out_shape=out_shape,
        scratch_shapes=[pltpu.VMEM((tm, tn), F32)] if nk > 1 else [],
        compiler_params=_cp("parallel", "parallel", "arbitrary"),
    )(*args)


def _relu2(v):
    r = jnp.maximum(v, 0.0)
    return r * r


def _drelu2(du, av):
    return du * (2.0 * jnp.maximum(av, 0.0))


def _rms(v, g):
    r = lax.rsqrt(jnp.mean(v * v, axis=-1, keepdims=True) + EPS)
    return v * r * g


def _row_spec(w):
    return pl.BlockSpec((TM, w), lambda i: (i, 0))


def _vec_spec(w):
    return pl.BlockSpec((1, w), lambda i: (0, 0))


def _rms_fwd(x, g, name):
    t, d = x.shape

    def body(x_ref, g_ref, h_ref):
        h_ref[...] = _rms(x_ref[...], g_ref[...]).astype(BF16)

    return pl.pallas_call(
        body, name=name, grid=(t // TM,), in_specs=[_row_spec(d), _vec_spec(d)], out_specs=_row_spec(d),
        out_shape=jax.ShapeDtypeStruct((t, d), BF16), compiler_params=_cp("parallel"))(x, g)


def _add_rms_fwd(x, y, g1, g2, name):
    t, d = x.shape

    def body(x_ref, y_ref, g1_ref, g2_ref, xn_ref, h_ref):
        xn = x_ref[...] + _rms(y_ref[...], g1_ref[...])
        xn_ref[...] = xn
        h_ref[...] = _rms(xn, g2_ref[...]).astype(BF16)

    return pl.pallas_call(
        body, name=name, grid=(t // TM,),
        in_specs=[_row_spec(d), _row_spec(d), _vec_spec(d), _vec_spec(d)],
        out_specs=[_row_spec(d), _row_spec(d)],
        out_shape=[jax.ShapeDtypeStruct((t, d), F32), jax.ShapeDtypeStruct((t, d), BF16)],
        compiler_params=_cp("parallel"))(x, y, g1, g2)


def _final_loss(x, y, g, tgt, name):
    t, d = x.shape

    def body(x_ref, y_ref, g_ref, t_ref, l_ref, dx_ref):
        @pl.when(pl.program_id(0) == 0)
        def _():
            l_ref[...] = jnp.zeros_like(l_ref)

        err = x_ref[...] + _rms(y_ref[...], g_ref[...]) - t_ref[...]
        dx_ref[...] = err * (1.0 / d)
        l_ref[...] += jnp.sum(jnp.sum(err * err, axis=1, keepdims=True), axis=0, keepdims=True) * (0.5 / d)

    return pl.pallas_call(
        body, name=name, grid=(t // TM,),
        in_specs=[_row_spec(d), _row_spec(d), _vec_spec(d), _row_spec(d)],
        out_specs=[pl.BlockSpec((1, LANES), lambda i: (0, 0)), _row_spec(d)],
        out_shape=[jax.ShapeDtypeStruct((1, LANES), F32), jax.ShapeDtypeStruct((t, d), F32)],
        compiler_params=_cp("arbitrary"))(x, y, g, tgt)


def _rms_bwd_vals(dn, v, g):
    w = v.shape[-1]
    r = lax.rsqrt(jnp.mean(v * v, axis=-1, keepdims=True) + EPS)
    vh = v * r
    dgp = jnp.sum(dn * vh, axis=0, keepdims=True)
    dvh = dn * g
    dv = r * (dvh - vh * (jnp.sum(dvh * vh, axis=-1, keepdims=True) * (1.0 / w)))
    return dv, dgp


def _norm_bwd(dn, v, g, resid, out_dtype, name):
    t, d = v.shape
    has_res = resid is not None

    def body(*refs):
        if has_res:
            dn_ref, v_ref, g_ref, r_ref, dv_ref, dg_ref = refs
        else:
            dn_ref, v_ref, g_ref, dv_ref, dg_ref = refs

        @pl.when(pl.program_id(0) == 0)
        def _():
            dg_ref[...] = jnp.zeros_like(dg_ref)

        dv, dgp = _rms_bwd_vals(dn_ref[...].astype(F32), v_ref[...], g_ref[...])
        if has_res:
            dv = dv + r_ref[...]
        dv_ref[...] = dv.astype(out_dtype)
        dg_ref[...] += dgp

    in_specs = [_row_spec(d), _row_spec(d), _vec_spec(d)] + ([_row_spec(d)] if has_res else [])
    args = [dn, v, g] + ([resid] if has_res else [])
    return pl.pallas_call(
        body, name=name, grid=(t // TM,), in_specs=in_specs,
        out_specs=[_row_spec(d), _vec_spec(d)],
        out_shape=[jax.ShapeDtypeStruct((t, d), out_dtype), jax.ShapeDtypeStruct((1, d), F32)],
        compiler_params=_cp("arbitrary"))(*args)


def _norm_bwd_pair(dn, v1, g1, resid, v2, g2, name):
    t, d = v1.shape

    def body(dn_ref, v1_ref, g1_ref, r_ref, v2_ref, g2_ref, d1_ref, d2_ref, dg1_ref, dg2_ref):
        @pl.when(pl.program_id(0) == 0)
        def _():
            dg1_ref[...] = jnp.zeros_like(dg1_ref)
            dg2_ref[...] = jnp.zeros_like(dg2_ref)

        d1, dgp1 = _rms_bwd_vals(dn_ref[...].astype(F32), v1_ref[...], g1_ref[...])
        d1 = d1 + r_ref[...]
        d1_ref[...] = d1
        d2, dgp2 = _rms_bwd_vals(d1, v2_ref[...], g2_ref[...])
        d2_ref[...] = d2.astype(BF16)
        dg1_ref[...] += dgp1
        dg2_ref[...] += dgp2

    row, vec = _row_spec(d), _vec_spec(d)
    return pl.pallas_call(
        body, name=name, grid=(t // TM,), in_specs=[row, row, vec, row, row, vec],
        out_specs=[row, row, vec, vec],
        out_shape=[jax.ShapeDtypeStruct((t, d), F32), jax.ShapeDtypeStruct((t, d), BF16),
                   jax.ShapeDtypeStruct((1, d), F32), jax.ShapeDtypeStruct((1, d), F32)],
        compiler_params=_cp("arbitrary"))(dn, v1, g1, resid, v2, g2)


def _rope_trig(pos, invf):
    ang = pos * invf
    return jnp.cos(ang), jnp.sin(ang)


def _rope_tables(trig, w, lo, half):
    c, s = trig
    lane = lax.broadcasted_iota(jnp.int32, c.shape, 1)
    active = (lane >= lo) & (lane < lo + 2 * half)
    cos = jnp.concatenate([jnp.where(active, c, 1.0)] * (w // LANES), axis=1)
    sin = jnp.concatenate([jnp.where(active, s, 0.0)] * (w // LANES), axis=1)
    lanes = lax.broadcasted_iota(jnp.int32, (c.shape[0], w), 1) & (LANES - 1)
    first = (lanes >= lo) & (lanes < lo + half)
    second = (lanes >= lo + half) & (lanes < lo + 2 * half)
    return cos, sin, first, second


def _rope_apply(v, cos, sin, first, second, half, sign):
    w = v.shape[-1]
    up = pltpu.roll(v, w - half, 1)
    dn = pltpu.roll(v, half, 1)
    rot = jnp.where(first, -up, jnp.where(second, dn, 0.0))
    return v * cos + rot * (sin * sign)


def _split3(c):
    hi = c.astype(BF16).astype(F32)
    mid = (c - hi).astype(BF16).astype(F32)
    return hi, mid, (c - hi) - mid


def _spread_heads(x):
    low = lax.broadcasted_iota(jnp.int32, (x.shape[0], LANES), 1) < HEAD_DIM
    out = []
    for c in range(x.shape[1] // LANES):
        blk = x[:, c * LANES:(c + 1) * LANES]
        out.append(jnp.where(low, blk, 0.0))
        out.append(jnp.where(low, pltpu.roll(blk, HEAD_DIM, 1), 0.0))
    return jnp.concatenate(out, axis=1)


def _gather_heads(y):
    low = lax.broadcasted_iota(jnp.int32, (y.shape[0], LANES), 1) < HEAD_DIM
    out = []
    for c in range(y.shape[1] // (2 * LANES)):
        a = y[:, 2 * c * LANES:(2 * c + 1) * LANES]
        b = y[:, (2 * c + 1) * LANES:(2 * c + 2) * LANES]
        out.append(jnp.where(low, a, pltpu.roll(b, HEAD_DIM, 1)))
    return jnp.concatenate(out, axis=1)


def _transposed_tiles(dst, row0, blocks, width):
    for b, blk in enumerate(blocks):
        bt = blk.T.astype(BF16)
        rows = bt.shape[0]
        for w in range(TM // width):
            dst[w, row0 + b * rows:row0 + (b + 1) * rows, :] = bt[:, w * width:(w + 1) * width]


def _prep_fwd(proj, cum, pos, invf, lg_lanes, g_q, g_kv, wq_pad, wkv_pad, name):
    t = proj.shape[0]

    def body(fox_ref, ret_ref, sb_ref, cq_ref, ckv_ref, kr_ref, cum_ref, pos_ref, invf_ref, lg_ref,
             gq_ref, gkv_ref, wq_ref, wkv_ref,
             ofox_ref, oret_ref, osb_ref, omla_ref, ofoxt_ref, osbt_ref, omlat_ref, orett_ref):
        pos_v = pos_ref[...]
        foxv, retv, sbv = [[_spread_heads(ref[:, g * GROUP:(g + 1) * GROUP]) for g in range(3)]
                           for ref in (fox_ref, ret_ref, sb_ref)]
        osb_ref[:, 0:HP] = (sbv[0] * 0.125).astype(BF16)
        osb_ref[:, HP:2 * HP] = sbv[1].astype(BF16)
        osb_ref[:, 2 * HP:QKV] = sbv[2].astype(BF16)
        _transposed_tiles(osbt_ref, 0, [sbv[1], sbv[2]], TKS)
        lane = lax.broadcasted_iota(jnp.int32, (TM, LANES), 1)
        cumv = cum_ref[...]
        fq, fk = [], []
        for hb in range(HEADS):
            hi, mid, lo = _split3(cumv[:, hb:hb + 1])
            q = foxv[0][:, hb * LANES:(hb + 1) * LANES] * 0.125
            k = foxv[1][:, hb * LANES:(hb + 1) * LANES]
            ones_q = (lane >= HEAD_DIM) & (lane < HEAD_DIM + 3)
            ones_k = (lane >= HEAD_DIM + 3) & (lane < HEAD_DIM + 6)
            q = jnp.where(ones_q, 1.0, q)
            k = jnp.where(ones_k, 1.0, k)
            for n, part in enumerate((hi, mid, lo)):
                q = jnp.where(lane == HEAD_DIM + 3 + n, part, q)
                k = jnp.where(lane == HEAD_DIM + n, -part, k)
            fq.append(q)
            fk.append(k)
        fk = jnp.concatenate(fk, axis=1)
        ofox_ref[:, 0:HP] = jnp.concatenate(fq, axis=1).astype(BF16)
        ofox_ref[:, HP:2 * HP] = fk.astype(BF16)
        ofox_ref[:, 2 * HP:QKV] = foxv[2].astype(BF16)
        _transposed_tiles(ofoxt_ref, 0, [fk, foxv[2]], BQ)
        trig = _rope_trig(pos_v, invf_ref[...])
        cos, sin, first, second = _rope_tables(trig, HP, 0, HEAD_DIM // 2)
        nloc = lax.broadcasted_iota(jnp.int32, (TM, 1), 0).astype(F32)
        dec = lg_ref[...] * nloc
        rq = _rope_apply(retv[0], cos, sin, first, second, HEAD_DIM // 2, 1.0)
        rk = _rope_apply(retv[1], cos, sin, first, second, HEAD_DIM // 2, 1.0)
        oret_ref[:, 0:HP] = (rq * jnp.exp(dec)).astype(BF16)
        rk = rk * 0.125 * jnp.exp(-dec)
        oret_ref[:, HP:2 * HP] = rk.astype(BF16)
        _transposed_tiles(orett_ref, 0, [rk], BQ)
        oret_ref[:, 2 * HP:QKV] = retv[2].astype(BF16)
        cosm, sinm, firstm, secondm = _rope_tables(trig, HP, HEAD_DIM, ROPE_MLA // 2)
        cqn = _rms(cq_ref[...], gq_ref[...]).astype(BF16)
        qm = _bdot(cqn, wq_ref[...])
        qm = _rope_apply(qm, cosm, sinm, firstm, secondm, ROPE_MLA // 2, 1.0)
        omla_ref[:, 0:HP] = (qm * MLA_SCALE).astype(BF16)
        ckvn = _rms(ckv_ref[...], gkv_ref[...]).astype(BF16)
        kv = _bdot(ckvn, wkv_ref[...])
        krr = _rope_apply(kr_ref[...], cosm[:, 0:LANES], sinm[:, 0:LANES], firstm[:, 0:LANES],
                          secondm[:, 0:LANES], ROPE_MLA // 2, 1.0)
        mk = kv[:, 0:HP] + jnp.concatenate([krr] * HEADS, axis=1)
        omla_ref[:, HP:2 * HP] = mk.astype(BF16)
        omla_ref[:, 2 * HP:QKV] = kv[:, HP:2 * HP].astype(BF16)
        _transposed_tiles(omlat_ref, 0, [mk, kv[:, HP:2 * HP]], BQ)

    def seg(off, w):
        return pl.BlockSpec((TM, w), lambda i, o=off // w: (i, o))

    def full(shape):
        return pl.BlockSpec(shape, lambda i: (0,) * len(shape))

    def tiles(width):
        return pl.BlockSpec((TM // width, 2 * HP, width), lambda i: (i, 0, 0))

    in_specs = [seg(OFF_FOX, QKV_IN), seg(OFF_RET, QKV_IN), seg(OFF_SB, QKV_IN), seg(OFF_CQ, Q_RANK),
                seg(OFF_CKV, LANES), seg(OFF_KR, LANES), _row_spec(LANES), pl.BlockSpec((TM, 1), lambda i: (i, 0)),
                full((1, LANES)), full((1, HP)), full((1, Q_RANK)), full((1, KV_RANK)),
                full((Q_RANK, HP)), full((KV_RANK, 2 * HP))]
    out_specs = [_row_spec(QKV)] * 4 + [tiles(BQ), tiles(TKS), tiles(BQ),
                                        pl.BlockSpec((1, HP, BQ), lambda i: (i, 0, 0))]
    out_shape = [jax.ShapeDtypeStruct((t, QKV), BF16)] * 4 + [
        jax.ShapeDtypeStruct((t // BQ, 2 * HP, BQ), BF16), jax.ShapeDtypeStruct((t // TKS, 2 * HP, TKS), BF16),
        jax.ShapeDtypeStruct((t // BQ, 2 * HP, BQ), BF16), jax.ShapeDtypeStruct((t // BQ, HP, BQ), BF16)]
    return pl.pallas_call(
        body, name=name, grid=(t // TM,), in_specs=in_specs, out_specs=out_specs, out_shape=out_shape,
        compiler_params=_cp("parallel"))(proj, proj, proj, proj, proj, proj, cum, pos, invf, lg_lanes,
                                         g_q, g_kv, wq_pad, wkv_pad)


def _prep_bwd(dfox, dret, dsb, dmla, drg, dlsf, proj, pos, invf, lg_lanes, b_pad, g_q, g_kv,
              wq_pad, wkv_pad, name):
    t = proj.shape[0]

    def body(dfq, dfk, dfv, drq, drk, drv, dsq, dsk, dsv, dmq, dmk, dmv, drg_ref, dlsf_ref,
             cq_ref, ckv_ref, ff_ref, pos_ref, invf_ref, lg_ref, b_ref, gq_ref, gkv_ref, wq_ref, wkv_ref,
             dp_ref, dwq_ref, dwkv_ref, dgq_ref, dgkv_ref, dbf_ref):
        @pl.when(pl.program_id(0) == 0)
        def _():
            dwq_ref[...] = jnp.zeros_like(dwq_ref)
            dwkv_ref[...] = jnp.zeros_like(dwkv_ref)
            dgq_ref[...] = jnp.zeros_like(dgq_ref)
            dgkv_ref[...] = jnp.zeros_like(dgkv_ref)
            dbf_ref[...] = jnp.zeros_like(dbf_ref)

        pos_v = pos_ref[...]
        def put(off, val):
            dp_ref[:, off:off + GROUP] = _gather_heads(val).astype(BF16)

        for off, (dq, dk, dv) in ((OFF_FOX, (dfq, dfk, dfv)), (OFF_SB, (dsq, dsk, dsv))):
            put(off, dq[...] * 0.125)
            put(off + GROUP, dk[...])
            put(off + 2 * GROUP, dv[...])
        trig = _rope_trig(pos_v, invf_ref[...])
        cos, sin, first, second = _rope_tables(trig, HP, 0, HEAD_DIM // 2)
        nloc = lax.broadcasted_iota(jnp.int32, (TM, 1), 0).astype(F32)
        dec = lg_ref[...] * nloc
        dq = _rope_apply(drq[...] * jnp.exp(dec), cos, sin, first, second, HEAD_DIM // 2, -1.0)
        dk = _rope_apply(drk[...] * (0.125 * jnp.exp(-dec)), cos, sin, first, second, HEAD_DIM // 2, -1.0)
        put(OFF_RET, dq)
        put(OFF_RET + GROUP, dk)
        put(OFF_RET + 2 * GROUP, drv[...])
        put(OFF_RG, drg_ref[...])
        cosm, sinm, firstm, secondm = _rope_tables(trig, HP, HEAD_DIM, ROPE_MLA // 2)
        dql = _rope_apply(dmq[...] * MLA_SCALE, cosm, sinm, firstm, secondm, ROPE_MLA // 2, -1.0).astype(BF16)
        cq = cq_ref[...]
        cqn = _rms(cq, gq_ref[...]).astype(BF16)
        dwq_ref[...] += _bdot(cqn, dql, TN)
        dcqn = _bdot(dql, wq_ref[...], NT)
        dcq, dgq = _rms_bwd_vals(dcqn, cq, gq_ref[...])
        dgq_ref[...] += dgq
        dp_ref[:, OFF_CQ:OFF_CQ + Q_RANK] = dcq.astype(BF16)
        dkm = dmk[...]
        dkv = jnp.concatenate([dkm, dmv[...]], axis=1).astype(BF16)
        ckv = ckv_ref[...]
        ckvn = _rms(ckv, gkv_ref[...]).astype(BF16)
        dwkv_ref[...] += _bdot(ckvn, dkv, TN)
        dckvn = _bdot(dkv, wkv_ref[...], NT)
        dckv, dgkv = _rms_bwd_vals(dckvn, ckv, gkv_ref[...])
        dgkv_ref[...] += dgkv
        dp_ref[:, OFF_CKV:OFF_CKV + LANES] = dckv.astype(BF16)
        dkr = dkm[:, 0:LANES] + dkm[:, LANES:2 * LANES] + dkm[:, 2 * LANES:3 * LANES] + dkm[:, 3 * LANES:HP]
        act = firstm[:, 0:LANES] | secondm[:, 0:LANES]
        dkr = jnp.where(act, dkr, 0.0)
        dkr = _rope_apply(dkr, cosm[:, 0:LANES], sinm[:, 0:LANES], firstm[:, 0:LANES], secondm[:, 0:LANES],
                          ROPE_MLA // 2, -1.0)
        dp_ref[:, OFF_KR:OFF_KR + LANES] = dkr.astype(BF16)
        f = ff_ref[...] + b_ref[...]
        dff = dlsf_ref[...] / (1.0 + jnp.exp(f))
        dbf_ref[...] += jnp.sum(dff, axis=0, keepdims=True)
        dp_ref[:, OFF_FF:OFF_FF + LANES] = dff.astype(BF16)
        dp_ref[:, OFF_FF + LANES:NP_IN] = jnp.zeros((TM, NP_IN - OFF_FF - LANES), BF16)

    def seg(off, w):
        return pl.BlockSpec((TM, w), lambda i, o=off // w: (i, o))

    def full(shape):
        return pl.BlockSpec(shape, lambda i: (0,) * len(shape))

    hp_spec = _row_spec(HP)
    in_specs = [hp_spec] * 13 + [_row_spec(LANES), seg(OFF_CQ, Q_RANK), seg(OFF_CKV, LANES), seg(OFF_FF, LANES),
                                 pl.BlockSpec((TM, 1), lambda i: (i, 0)),
                                 full((1, LANES)), full((1, HP)), full((1, LANES)), full((1, Q_RANK)),
                                 full((1, KV_RANK)), full((Q_RANK, HP)), full((KV_RANK, 2 * HP))]
    out_specs = [_row_spec(NP_IN), full((Q_RANK, HP)), full((KV_RANK, 2 * HP)), full((1, Q_RANK)),
                 full((1, KV_RANK)), full((1, LANES))]
    out_shape = [jax.ShapeDtypeStruct((t, NP_IN), BF16), jax.ShapeDtypeStruct((Q_RANK, HP), F32),
                 jax.ShapeDtypeStruct((KV_RANK, 2 * HP), F32), jax.ShapeDtypeStruct((1, Q_RANK), F32),
                 jax.ShapeDtypeStruct((1, KV_RANK), F32), jax.ShapeDtypeStruct((1, LANES), F32)]
    return pl.pallas_call(
        body, name=name, grid=(t // TM,), in_specs=in_specs, out_specs=out_specs, out_shape=out_shape,
        compiler_params=_cp("arbitrary"))(*dfox, *dret, *dsb, *dmla, drg, dlsf, proj, proj, proj, pos, invf,
                                          lg_lanes, b_pad, g_q, g_kv, wq_pad, wkv_pad)


TC = 1024


def _cumsum(x, reverse, name, partials=None, forget_bias=None):
    t = x.shape[0]
    w = LANES
    tc = min(TC, t)
    n = t // tc
    xs = [x] + ([] if partials is None else [partials]) + ([] if forget_bias is None else [forget_bias])

    def body(*refs):
        x_refs, o_ref, carry = refs[:len(xs)], refs[len(xs)], refs[len(xs) + 1]

        @pl.when(pl.program_id(0) == 0)
        def _():
            carry[...] = jnp.zeros_like(carry)

        r = lax.broadcasted_iota(jnp.int32, (tc, tc), 0)
        c = lax.broadcasted_iota(jnp.int32, (tc, tc), 1)
        tri = jnp.where((r <= c) if reverse else (r >= c), 1.0, 0.0).astype(BF16)
        v = x_refs[0][...]
        if forget_bias is not None:
            f = v + x_refs[-1][...]
            v = -(jnp.maximum(-f, 0.0) + jnp.log(1.0 + jnp.exp(-jnp.abs(f))))
        if partials is not None:
            lane = lax.broadcasted_iota(jnp.int32, (tc, LANES), 1)
            for hb in range(HEADS):
                v = v + jnp.where(lane == hb, jnp.sum(x_refs[1][:, _hs(hb)], axis=1, keepdims=True), 0.0)
        hi = v.astype(BF16)
        r1 = v - hi.astype(F32)
        mid = r1.astype(BF16)
        lo = (r1 - mid.astype(F32)).astype(BF16)
        cs = _bdot(tri, hi) + _bdot(tri, mid) + _bdot(tri, lo) + carry[...]
        o_ref[...] = cs
        carry[...] = cs[0:1, :] if reverse else cs[tc - 1:tc, :]

    step = (lambda i: n - 1 - i) if reverse else (lambda i: i)
    x_col = OFF_FF // LANES if forget_bias is not None else 0
    in_specs = [pl.BlockSpec((tc, LANES), lambda i: (step(i), x_col))]
    if partials is not None:
        in_specs.append(pl.BlockSpec((tc, HP), lambda i: (step(i), 0)))
    if forget_bias is not None:
        in_specs.append(_vec_spec(LANES))
    return pl.pallas_call(
        body, name=name, grid=(n,), in_specs=in_specs, out_specs=pl.BlockSpec((tc, w), lambda i: (step(i), 0)),
        out_shape=jax.ShapeDtypeStruct((t, w), F32), scratch_shapes=[pltpu.VMEM((1, w), F32)],
        compiler_params=_cp("arbitrary"))(*xs)


HB_FWD = 4
HB_BWD = 4
HB_SB_FWD = 4
BQS = 512
BQA = 512


def _q_spec(hb, bq=BQ):
    return pl.BlockSpec((bq, hb * LANES), lambda g, i: (i, g))


ONE_BUFFER = pl.Buffered(1)


def _kv_spec(t, which, hb):
    return pl.BlockSpec((t, hb * LANES), lambda g, i, w=which: (0, w * (HEADS // hb) + g), pipeline_mode=ONE_BUFFER)


def _acc_spec(t, hb):
    return pl.BlockSpec((t, hb * LANES), lambda g, i: (0, g), pipeline_mode=ONE_BUFFER)


def _hs(hh):
    return slice(hh * LANES, (hh + 1) * LANES)


def _tile_iota(rows, cols):
    return (lax.broadcasted_iota(jnp.int32, (rows, cols), 0), lax.broadcasted_iota(jnp.int32, (rows, cols), 1))


def _kvt_spec(nkv, width, which, hb):
    return pl.BlockSpec((nkv, hb * LANES, width), lambda g, i, w=which: (0, w * (HEADS // hb) + g, 0),
                        pipeline_mode=ONE_BUFFER)


def _qrow_spec(hb, bq=BQ):
    return pl.BlockSpec((hb, 1, 1, bq), lambda g, i: (g, i, 0, 0))


def _vis(key0, query0, rows, cols, kind):
    r, c = _tile_iota(rows, cols)
    k, q = key0 + r, query0 + c
    if kind == "chunk":
        return (k >> CHUNK_SHIFT) <= (q >> CHUNK_SHIFT)
    return (k < q) if kind == "strict" else (k <= q)


def _softmax_fwd(qkv, kvt, *, chunk_mask, scale, name):
    t = qkv.shape[0]
    nq = t // BQA
    per = BQA // BQ
    hb = HB_FWD
    kind = "chunk" if chunk_mask else "causal"

    def body(q_ref, k_ref, vt_ref, o_ref, lse_ref, m_sc, l_sc, acc_sc):
        i = pl.program_id(1)
        m_sc[...] = jnp.full((hb, 1, BQA), NEG, F32)
        l_sc[...] = jnp.zeros((hb, 1, BQA), F32)
        acc_sc[...] = jnp.zeros((hb, LANES, BQA), F32)

        def tile(j, qoff):
            off = pl.multiple_of(j * BQ, BQ)
            lo = 0 if qoff is None else qoff
            qs = slice(lo, BQA)
            vis = None if qoff is None else _vis(off, i * BQA + lo, BQ, BQA - lo, kind)
            ss = [_bdot(k_ref[pl.ds(off, BQ), _hs(hh)], q_ref[qs, _hs(hh)], NT) for hh in range(hb)]
            for hh in range(hb):
                s = ss[hh]
                if scale != 1.0:
                    s = s * scale
                if vis is not None:
                    s = jnp.where(vis, s, NEG)
                m_old = m_sc[hh, :, qs]
                m_new = jnp.maximum(m_old, jnp.max(s, axis=0, keepdims=True))
                alpha = jnp.exp(m_old - m_new)
                p = jnp.exp(s - m_new)
                l_sc[hh, :, qs] = alpha * l_sc[hh, :, qs] + jnp.sum(p, axis=0, keepdims=True)
                m_sc[hh, :, qs] = m_new
                acc_sc[hh, :, qs] = alpha * acc_sc[hh, :, qs] + _bdot(vt_ref[j, _hs(hh), :], p.astype(BF16))

        def loop(j, carry):
            tile(j, None)
            return carry

        lax.fori_loop(0, per * i, loop, 0)
        for d in range(per):
            tile(per * i + d, d * BQ)
        for hh in range(hb):
            l = l_sc[hh]
            o_ref[:, _hs(hh)] = (acc_sc[hh] / l).T
            lse_ref[hh, 0] = m_sc[hh] + jnp.log(l)

    return pl.pallas_call(
        body, name=name, grid=(HEADS // hb, nq),
        in_specs=[_q_spec(hb, BQA), _kv_spec(t, 1, hb), _kvt_spec(t // BQ, BQ, 1, hb)],
        out_specs=[_q_spec(hb, BQA), _qrow_spec(hb, BQA)],
        out_shape=[jax.ShapeDtypeStruct((t, HP), F32), jax.ShapeDtypeStruct((HEADS, nq, 1, BQA), F32)],
        scratch_shapes=[pltpu.VMEM((hb, 1, BQA), F32), pltpu.VMEM((hb, 1, BQA), F32),
                        pltpu.VMEM((hb, LANES, BQA), F32)],
        compiler_params=_cp("parallel", "arbitrary"))(qkv, qkv, kvt)


def _softmax_bwd(qkv, kvt, do, lse, delta, *, bias, chunk_mask, scale, name):
    t = qkv.shape[0]
    nq = t // BQA
    per = BQA // BQ
    hb = HB_BWD
    kind = "chunk" if chunk_mask else "causal"

    def body(*refs):
        if bias:
            (q_ref, k_ref, v_ref, kt_ref, do_ref, lse_ref, dl_ref, dq_ref, dk_ref, dv_ref, dck_ref, dcq_ref,
             dq_sc, dcq_sc) = refs
            dcq_sc[...] = jnp.zeros((hb, 1, BQA), F32)
        else:
            q_ref, k_ref, v_ref, kt_ref, do_ref, lse_ref, dl_ref, dq_ref, dk_ref, dv_ref, dq_sc = refs
        i = pl.program_id(1)

        @pl.when(i == 0)
        def _():
            dk_ref[...] = jnp.zeros_like(dk_ref)
            dv_ref[...] = jnp.zeros_like(dv_ref)
            if bias:
                dck_ref[...] = jnp.zeros_like(dck_ref)

        dq_sc[...] = jnp.zeros((hb, LANES, BQA), F32)

        def tile(j, qoff):
            off = pl.multiple_of(j * BQ, BQ)
            lo = 0 if qoff is None else qoff
            qsl = slice(lo, BQA)
            vis = None if qoff is None else _vis(off, i * BQA + lo, BQ, BQA - lo, kind)
            qs = [q_ref[qsl, _hs(hh)] for hh in range(hb)]
            dobs = [do_ref[qsl, _hs(hh)].astype(BF16) for hh in range(hb)]
            ss = [_bdot(k_ref[pl.ds(off, BQ), _hs(hh)], qs[hh], NT) for hh in range(hb)]
            dps = [_bdot(v_ref[pl.ds(off, BQ), _hs(hh)], dobs[hh], NT) for hh in range(hb)]
            pbs, dsbs = [], []
            for hh in range(hb):
                s = ss[hh]
                if scale != 1.0:
                    s = s * scale
                p = jnp.exp(s - lse_ref[hh, 0, :, qsl])
                if vis is not None:
                    p = jnp.where(vis, p, 0.0)
                ds = p * (dps[hh] - dl_ref[hh, 0, :, qsl])
                if bias:
                    part = ds[:, 0:LANES]
                    for b in range(1, (BQA - lo) // LANES):
                        part = part + ds[:, b * LANES:(b + 1) * LANES]
                    dck_ref[pl.ds(off, BQ), _hs(hh)] -= part
                    dcq_sc[hh, :, qsl] += jnp.sum(ds, axis=0, keepdims=True)
                if scale != 1.0:
                    ds = ds * scale
                pbs.append(p.astype(BF16))
                dsbs.append(ds.astype(BF16))
            for hh in range(hb):
                sl = _hs(hh)
                dv_ref[pl.ds(off, BQ), sl] += _bdot(pbs[hh], dobs[hh])
                dk_ref[pl.ds(off, BQ), sl] += _bdot(dsbs[hh], qs[hh])
                dq_sc[hh, :, qsl] += _bdot(kt_ref[j, sl, :], dsbs[hh])

        def loop(j, carry):
            tile(j, None)
            return carry

        lax.fori_loop(0, per * i, loop, 0)
        for d in range(per):
            tile(per * i + d, d * BQ)
        for hh in range(hb):
            dq_ref[:, _hs(hh)] = dq_sc[hh].T
            if bias:
                dcq_ref[hh, 0] = dcq_sc[hh]

    in_specs = [_q_spec(hb, BQA), _kv_spec(t, 1, hb), _kv_spec(t, 2, hb), _kvt_spec(t // BQ, BQ, 0, hb),
                _q_spec(hb, BQA), _qrow_spec(hb, BQA), _qrow_spec(hb, BQA)]
    out_specs = [_q_spec(hb, BQA), _acc_spec(t, hb), _acc_spec(t, hb)]
    out_shape = [jax.ShapeDtypeStruct((t, HP), F32)] * 3
    scratch = [pltpu.VMEM((hb, LANES, BQA), F32)]
    if bias:
        out_specs += [_acc_spec(t, hb), _qrow_spec(hb, BQA)]
        out_shape += [jax.ShapeDtypeStruct((t, HP), F32), jax.ShapeDtypeStruct((HEADS, nq, 1, BQA), F32)]
        scratch.append(pltpu.VMEM((hb, 1, BQA), F32))
    return pl.pallas_call(
        body, name=name, grid=(HEADS // hb, nq), in_specs=in_specs, out_specs=out_specs, out_shape=out_shape,
        scratch_shapes=scratch,
        compiler_params=_cp("parallel", "arbitrary"))(qkv, qkv, qkv, kvt, do, lse, delta)


def _ret_diag_decay(lg1, keys_on_rows=False):
    r, c = _tile_iota(BQ, BQ)
    qn, km = (c, r) if keys_on_rows else (r, c)
    dd = jnp.where(km > qn, jnp.exp((2.0 * lg1) * (km - qn).astype(F32)), 1.0)
    return jnp.where((km >> CHUNK_SHIFT) <= (qn >> CHUNK_SHIFT), dd, 0.0)


def _lg_spec(hb):
    return pl.BlockSpec((hb, 1, LANES), lambda g, i: (g, 0, 0))


def _ret_specs(nq, hb, reverse):
    tile = (lambda i: nq - 1 - i) if reverse else (lambda i: i)
    qkv = [pl.BlockSpec((BQ, hb * LANES), lambda g, i, w=w: (tile(i), w * (HEADS // hb) + g)) for w in range(3)]
    kt = pl.BlockSpec((1, hb * LANES, BQ), lambda g, i: (tile(i), g, 0))
    st = pl.BlockSpec((1, hb * LANES, LANES), lambda g, i: (tile(i), g, 0))
    return qkv, kt, st


def _ret_fwd(qkv, kt, lg_heads, name):
    t = qkv.shape[0]
    nq = t // BQ
    hb = HB_FWD

    def body(lg_ref, q_ref, k_ref, v_ref, kt_ref, o_ref, st_ref, s_sc):
        @pl.when(pl.program_id(1) == 0)
        def _():
            s_sc[...] = jnp.zeros_like(s_sc)

        qs = [q_ref[:, _hs(hh)] for hh in range(hb)]
        vs = [v_ref[:, _hs(hh)] for hh in range(hb)]
        aa = [_bdot(qs[hh], k_ref[:, _hs(hh)], NT) for hh in range(hb)]
        kv = [_bdot(kt_ref[0, _hs(hh), :], vs[hh]) for hh in range(hb)]
        for hh in range(hb):
            sl = _hs(hh)
            lg1 = lg_ref[hh][:, 0:1]
            s = s_sc[hh]
            st_ref[0, sl, :] = s
            shi, slo = _split2(s)
            a = (aa[hh] * _ret_diag_decay(lg1)).astype(BF16)
            o_ref[:, sl] = _bdot(a, vs[hh]) + _bdot(qs[hh], shi) + _bdot(qs[hh], slo)
            s_sc[hh] = jnp.exp(lg1 * float(BQ)) * (s + kv[hh])

    qkv_specs, kt_spec, st_spec = _ret_specs(nq, hb, False)
    return pl.pallas_call(
        body, name=name, grid=(HEADS // hb, nq), in_specs=[_lg_spec(hb)] + qkv_specs + [kt_spec],
        out_specs=[_q_spec(hb), st_spec],
        out_shape=[jax.ShapeDtypeStruct((t, HP), F32), jax.ShapeDtypeStruct((nq, HP, LANES), F32)],
        scratch_shapes=[pltpu.VMEM((hb, LANES, LANES), F32)],
        compiler_params=_cp("parallel", "arbitrary"))(lg_heads, qkv, qkv, qkv, kt)


def _ret_bwd(qkv, states, lg_heads, do, name):
    t = qkv.shape[0]
    nq = t // BQ
    hb = HB_FWD

    def body(lg_ref, q_ref, k_ref, v_ref, st_ref, do_ref, dq_ref, dk_ref, dv_ref, g_sc):
        @pl.when(pl.program_id(1) == 0)
        def _():
            g_sc[...] = jnp.zeros_like(g_sc)

        qs = [q_ref[:, _hs(hh)] for hh in range(hb)]
        ks = [k_ref[:, _hs(hh)] for hh in range(hb)]
        vs = [v_ref[:, _hs(hh)] for hh in range(hb)]
        dobs = [do_ref[:, _hs(hh)].astype(BF16) for hh in range(hb)]
        aa = [_bdot(ks[hh], qs[hh], NT) for hh in range(hb)]
        das = [_bdot(vs[hh], dobs[hh], NT) for hh in range(hb)]
        qdo = [_bdot(qs[hh], dobs[hh], TN) for hh in range(hb)]
        for hh in range(hb):
            sl = _hs(hh)
            lg1 = lg_ref[hh][:, 0:1]
            dd = _ret_diag_decay(lg1, keys_on_rows=True)
            at = (aa[hh] * dd).astype(BF16)
            dat = (das[hh] * dd).astype(BF16)
            h = jnp.exp(lg1 * float(BQ)) * g_sc[hh]
            hhi, hlo = _split2(h)
            shi, slo = _split2(st_ref[0, sl, :])
            dv_ref[:, sl] = _bdot(at, dobs[hh]) + _bdot(ks[hh], hhi) + _bdot(ks[hh], hlo)
            dk_ref[:, sl] = _bdot(dat, qs[hh]) + _bdot(vs[hh], hhi, NT) + _bdot(vs[hh], hlo, NT)
            dq_ref[:, sl] = _bdot(dat, ks[hh], TN) + _bdot(dobs[hh], shi, NT) + _bdot(dobs[hh], slo, NT)
            g_sc[hh] = qdo[hh] + h

    qkv_specs, _, st_spec = _ret_specs(nq, hb, True)
    tile_spec = pl.BlockSpec((BQ, hb * LANES), lambda g, i: (nq - 1 - i, g))
    return pl.pallas_call(
        body, name=name, grid=(HEADS // hb, nq), in_specs=[_lg_spec(hb)] + qkv_specs + [st_spec, tile_spec],
        out_specs=[tile_spec] * 3, out_shape=[jax.ShapeDtypeStruct((t, HP), F32)] * 3,
        scratch_shapes=[pltpu.VMEM((hb, LANES, LANES), F32)],
        compiler_params=_cp("parallel", "arbitrary"))(lg_heads, qkv, qkv, qkv, states, do)


def _sb_tile_logs(q, kb, vis):
    z = _bdot(kb, q, NT)
    ls = -(jnp.maximum(z, 0.0) + jnp.log(1.0 + jnp.exp(-jnp.abs(z))))
    if vis is not None:
        ls = jnp.where(vis, ls, 0.0)
    return z, ls


def _sb_later(ls, after):
    hi, lo = _split2(ls)
    return _bdot(after, hi) + _bdot(after, lo)


def _sb_fwd(qkv, kvt, name):
    t = qkv.shape[0]
    nq = t // BQS
    per = BQS // TKS
    hb = HB_SB_FWD

    def body(q_ref, k_ref, vt_ref, o_ref, tot_ref, acc_sc, r_sc):
        i = pl.program_id(1)
        acc_sc[...] = jnp.zeros((hb, LANES, BQS), F32)
        r_sc[...] = jnp.zeros((hb, 1, BQS), F32)
        mr, mc = _tile_iota(TKS, TKS)
        after = jnp.where(mc > mr, 1.0, 0.0).astype(BF16)

        def tile(j, qoff):
            off = pl.multiple_of(j * TKS, TKS)
            lo = 0 if qoff is None else qoff
            qsl = slice(lo, BQS)
            vis = None if qoff is None else _vis(off, i * BQS + lo, TKS, BQS - lo, "strict")
            zl = [_sb_tile_logs(q_ref[qsl, _hs(hh)], k_ref[pl.ds(off, TKS), _hs(hh)], vis) for hh in range(hb)]
            laters = [_sb_later(zl[hh][1], after) for hh in range(hb)]
            ws = []
            for hh in range(hb):
                z, ls = zl[hh]
                w = jnp.exp(z + ls + laters[hh] + r_sc[hh, :, qsl])
                if vis is not None:
                    w = jnp.where(vis, w, 0.0)
                ws.append(w.astype(BF16))
                r_sc[hh, :, qsl] += jnp.sum(ls, axis=0, keepdims=True)
            for hh in range(hb):
                acc_sc[hh, :, qsl] += _bdot(vt_ref[j, _hs(hh), :], ws[hh])

        for d in reversed(range(per)):
            tile(per * i + d, d * TKS)

        def loop(jj, carry):
            tile(per * i - 1 - jj, None)
            return carry

        lax.fori_loop(0, per * i, loop, 0)
        for hh in range(hb):
            o_ref[:, _hs(hh)] = acc_sc[hh].T
            tot_ref[hh, 0] = r_sc[hh]

    return pl.pallas_call(
        body, name=name, grid=(HEADS // hb, nq),
        in_specs=[_q_spec(hb, BQS), _kv_spec(t, 1, hb), _kvt_spec(t // TKS, TKS, 1, hb)],
        out_specs=[_q_spec(hb, BQS), _qrow_spec(hb, BQS)],
        out_shape=[jax.ShapeDtypeStruct((t, HP), F32), jax.ShapeDtypeStruct((HEADS, nq, 1, BQS), F32)],
        scratch_shapes=[pltpu.VMEM((hb, LANES, BQS), F32), pltpu.VMEM((hb, 1, BQS), F32)],
        compiler_params=_cp("parallel", "arbitrary"))(qkv, qkv, kvt)


def _sb_bwd(qkv, kvt, do, tot, name):
    t = qkv.shape[0]
    nq = t // BQS
    per = BQS // TKS
    hb = HB_BWD

    def body(q_ref, k_ref, v_ref, kt_ref, do_ref, tot_ref, dq_ref, dk_ref, dv_ref, dq_sc, p_sc, g_sc):
        i = pl.program_id(1)

        @pl.when(i == 0)
        def _():
            dk_ref[...] = jnp.zeros_like(dk_ref)
            dv_ref[...] = jnp.zeros_like(dv_ref)

        dq_sc[...] = jnp.zeros((hb, LANES, BQS), F32)
        p_sc[...] = jnp.zeros((hb, 1, BQS), F32)
        g_sc[...] = jnp.zeros((hb, 1, BQS), F32)
        mr, mc = _tile_iota(TKS, TKS)
        after = jnp.where(mc > mr, 1.0, 0.0).astype(BF16)
        before = jnp.where(mc < mr, 1.0, 0.0).astype(BF16)

        def tile(j, qoff):
            off = pl.multiple_of(j * TKS, TKS)
            lo = 0 if qoff is None else qoff
            qsl = slice(lo, BQS)
            vis = None if qoff is None else _vis(off, i * BQS + lo, TKS, BQS - lo, "strict")
            qs = [q_ref[qsl, _hs(hh)] for hh in range(hb)]
            dobs = [do_ref[qsl, _hs(hh)].astype(BF16) for hh in range(hb)]
            zl = [_sb_tile_logs(qs[hh], k_ref[pl.ds(off, TKS), _hs(hh)], vis) for hh in range(hb)]
            dws = [_bdot(v_ref[pl.ds(off, TKS), _hs(hh)], dobs[hh], NT) for hh in range(hb)]
            laters = [_sb_later(zl[hh][1], after) for hh in range(hb)]
            ws, gs = [], []
            for hh in range(hb):
                z, ls = zl[hh]
                own = jnp.sum(ls, axis=0, keepdims=True)
                rest = tot_ref[hh, 0, :, qsl] - p_sc[hh, :, qsl] - own
                w = jnp.exp(z + ls + laters[hh] + rest)
                if vis is not None:
                    w = jnp.where(vis, w, 0.0)
                p_sc[hh, :, qsl] += own
                ws.append(w.astype(BF16))
                gs.append(dws[hh] * w)
            gins = [_bdot(before, gs[hh].astype(BF16)) for hh in range(hb)]
            dzbs = []
            for hh in range(hb):
                g = gs[hh]
                stay = jnp.exp(zl[hh][1])
                dz = g * stay - (1.0 - stay) * (gins[hh] + g_sc[hh, :, qsl])
                if vis is not None:
                    dz = jnp.where(vis, dz, 0.0)
                g_sc[hh, :, qsl] += jnp.sum(g, axis=0, keepdims=True)
                dzbs.append(dz.astype(BF16))
            for hh in range(hb):
                sl = _hs(hh)
                dv_ref[pl.ds(off, TKS), sl] += _bdot(ws[hh], dobs[hh])
                dk_ref[pl.ds(off, TKS), sl] += _bdot(dzbs[hh], qs[hh])
                dq_sc[hh, :, qsl] += _bdot(kt_ref[j, sl, :], dzbs[hh])

        def loop(j, carry):
            tile(j, None)
            return carry

        lax.fori_loop(0, per * i, loop, 0)
        for d in range(per):
            tile(per * i + d, d * TKS)
        for hh in range(hb):
            dq_ref[:, _hs(hh)] = dq_sc[hh].T

    return pl.pallas_call(
        body, name=name, grid=(HEADS // hb, nq),
        in_specs=[_q_spec(hb, BQS), _kv_spec(t, 1, hb), _kv_spec(t, 2, hb), _kvt_spec(t // TKS, TKS, 0, hb),
                  _q_spec(hb, BQS), _qrow_spec(hb, BQS)],
        out_specs=[_q_spec(hb, BQS), _acc_spec(t, hb), _acc_spec(t, hb)],
        out_shape=[jax.ShapeDtypeStruct((t, HP), F32)] * 3,
        scratch_shapes=[pltpu.VMEM((hb, LANES, BQS), F32), pltpu.VMEM((hb, 1, BQS), F32),
                        pltpu.VMEM((hb, 1, BQS), F32)],
        compiler_params=_cp("parallel", "arbitrary"))(qkv, qkv, qkv, kvt, do, tot)


def _sigmoid(v):
    return 1.0 / (1.0 + jnp.exp(-v))


def _post_fwd(oa, ob, oc, od, proj, g_pad, name):
    t = oa.shape[0]

    def body(oa_ref, ob_ref, oc_ref, od_ref, rg_ref, g_ref, mx_ref):
        g = g_ref[...]

        def group(o, gg):
            r = lax.rsqrt(jnp.sum(o * o, axis=-1, keepdims=True) * (1.0 / GROUP) + EPS)
            return _gather_heads(o * r * gg).astype(BF16)

        mx_ref[:, 0:GROUP] = group(oa_ref[...], g[:, 0:HP])
        mx_ref[:, GROUP:2 * GROUP] = group(ob_ref[...], g[:, HP:2 * HP])
        mx_ref[:, 3 * GROUP:4 * GROUP] = group(od_ref[...], g[:, 3 * HP:4 * HP])
        real = lax.broadcasted_iota(jnp.int32, (TM, LANES), 1) < HEAD_DIM
        rg = _spread_heads(rg_ref[...])
        gated = []
        for hb in range(HEADS):
            sl = slice(hb * LANES, (hb + 1) * LANES)
            o = oc_ref[:, sl]
            mu = jnp.sum(o, axis=-1, keepdims=True) * (1.0 / HEAD_DIM)
            dlt = jnp.where(real, o - mu, 0.0)
            var = jnp.sum(dlt * dlt, axis=-1, keepdims=True) * (1.0 / HEAD_DIM)
            yn = dlt * lax.rsqrt(var + EPS) * g[:, 2 * HP + hb * LANES:2 * HP + (hb + 1) * LANES]
            x = rg[:, sl]
            gated.append(yn * (x * _sigmoid(x)))
        mx_ref[:, 2 * GROUP:3 * GROUP] = _gather_heads(jnp.concatenate(gated, axis=1)).astype(BF16)

    rg_spec = pl.BlockSpec((TM, GROUP), lambda i: (i, OFF_RG // GROUP))
    return pl.pallas_call(
        body, name=name, grid=(t // TM,),
        in_specs=[_row_spec(HP)] * 4 + [rg_spec, _vec_spec(4 * HP)], out_specs=_row_spec(D_MODEL),
        out_shape=jax.ShapeDtypeStruct((t, D_MODEL), BF16), compiler_params=_cp("parallel"))(oa, ob, oc, od, proj, g_pad)


def _post_bwd(dmx, oa, ob, oc, od, proj, g_pad, name):
    t = oa.shape[0]

    def body(dm_ref, oa_ref, ob_ref, oc_ref, od_ref, rg_ref, g_ref,
             doa_ref, dob_ref, doc_ref, dod_ref, dla_ref, dlb_ref, drg_ref, dg_ref):
        @pl.when(pl.program_id(0) == 0)
        def _():
            dg_ref[...] = jnp.zeros_like(dg_ref)

        g = g_ref[...]

        def group_bwd(dm, o, gg):
            r = lax.rsqrt(jnp.sum(o * o, axis=-1, keepdims=True) * (1.0 / GROUP) + EPS)
            oh = o * r
            dgp = jnp.sum(dm * oh, axis=0, keepdims=True)
            dyh = dm * gg
            do = r * (dyh - oh * (jnp.sum(dyh * oh, axis=-1, keepdims=True) * (1.0 / GROUP)))
            return do, dgp

        def delta_bc(do, o):
            prod = do * o
            lane = lax.broadcasted_iota(jnp.int32, (TM, LANES), 1)
            out = jnp.zeros((TM, LANES), F32)
            for hb in range(HEADS):
                out = jnp.where(lane == hb, jnp.sum(prod[:, hb * LANES:(hb + 1) * LANES], axis=-1, keepdims=True), out)
            return out

        dmp = [_spread_heads(dm_ref[:, gi * GROUP:(gi + 1) * GROUP]) for gi in range(4)]
        rg = _spread_heads(rg_ref[...])
        oa = oa_ref[...]
        do_a, dga = group_bwd(dmp[0], oa, g[:, 0:HP])
        doa_ref[...] = do_a.astype(BF16)
        dla_ref[...] = delta_bc(do_a, oa)
        dg_ref[:, 0:HP] += dga
        ob = ob_ref[...]
        do_b, dgb = group_bwd(dmp[1], ob, g[:, HP:2 * HP])
        dob_ref[...] = do_b.astype(BF16)
        dlb_ref[...] = delta_bc(do_b, ob)
        dg_ref[:, HP:2 * HP] += dgb
        do_d, dgd = group_bwd(dmp[3], od_ref[...], g[:, 3 * HP:4 * HP])
        dod_ref[...] = do_d.astype(BF16)
        dg_ref[:, 3 * HP:4 * HP] += dgd
        real = lax.broadcasted_iota(jnp.int32, (TM, LANES), 1) < HEAD_DIM
        for hb in range(HEADS):
            sl = slice(hb * LANES, (hb + 1) * LANES)
            gsl = slice(2 * HP + hb * LANES, 2 * HP + (hb + 1) * LANES)
            o = oc_ref[:, sl]
            mu = jnp.sum(o, axis=-1, keepdims=True) * (1.0 / HEAD_DIM)
            dlt = jnp.where(real, o - mu, 0.0)
            var = jnp.sum(dlt * dlt, axis=-1, keepdims=True) * (1.0 / HEAD_DIM)
            rstd = lax.rsqrt(var + EPS)
            dhat = dlt * rstd
            gc = g[:, gsl]
            x = rg[:, sl]
            sg = _sigmoid(x)
            dm = dmp[2][:, sl]
            drg_ref[:, sl] = dm * (dhat * gc) * (sg * (1.0 + x * (1.0 - sg)))
            dyn = dm * (x * sg)
            dg_ref[:, gsl] += jnp.sum(dyn * dhat, axis=0, keepdims=True)
            ddh = dyn * gc
            m1 = jnp.sum(ddh, axis=-1, keepdims=True) * (1.0 / HEAD_DIM)
            m2 = jnp.sum(ddh * dhat, axis=-1, keepdims=True) * (1.0 / HEAD_DIM)
            doc_ref[:, sl] = jnp.where(real, rstd * (ddh - m1 - dhat * m2), 0.0).astype(BF16)

    rg_spec = pl.BlockSpec((TM, GROUP), lambda i: (i, OFF_RG // GROUP))
    hp = _row_spec(HP)
    return pl.pallas_call(
        body, name=name, grid=(t // TM,),
        in_specs=[_row_spec(D_MODEL), hp, hp, hp, hp, rg_spec, _vec_spec(4 * HP)],
        out_specs=[hp] * 4 + [_row_spec(LANES)] * 2 + [hp, _vec_spec(4 * HP)],
        out_shape=[jax.ShapeDtypeStruct((t, HP), BF16)] * 4 + [jax.ShapeDtypeStruct((t, LANES), F32)] * 2
        + [jax.ShapeDtypeStruct((t, HP), F32), jax.ShapeDtypeStruct((1, 4 * HP), F32)],
        compiler_params=_cp("arbitrary"))(dmx, oa, ob, oc, od, proj, g_pad)


def _mesh_pos():
    return lax.axis_index("x"), lax.axis_index("y"), lax.axis_index("c")


def _peer(pos, k):
    x, y, c = pos
    px = 1 - x if (k >> 2) & 1 else x
    py = 1 - y if (k >> 1) & 1 else y
    pc = 1 - c if k & 1 else c
    return (px, py, pc), 4 * px + 2 * py + pc


def _exchange(arrs, gather, name):
    n = len(arrs)

    def body(*refs):
        ins, outs = refs[:n], refs[n:2 * n]
        send_sems, recv_sems, loc_sems = refs[2 * n:]
        pos = _mesh_pos()
        me = 4 * pos[0] + 2 * pos[1] + pos[2]
        local = []
        for a in range(n):
            src = ins[a] if gather else ins[a].at[me]
            cp = pltpu.make_async_copy(src, outs[a].at[me], loc_sems.at[a])
            cp.start()
            local.append(cp)
        sends, recvs = [], []
        for k in range(1, N_DEV):
            peer, pid = _peer(pos, k)
            for a in range(n):
                s = a * (N_DEV - 1) + k - 1
                src = ins[a] if gather else ins[a].at[pid]
                cp = pltpu.make_async_remote_copy(
                    src_ref=src, dst_ref=outs[a].at[me], send_sem=send_sems.at[s], recv_sem=recv_sems.at[s],
                    device_id=peer, device_id_type=pl.DeviceIdType.MESH)
                cp.start()
                sends.append(cp)
                recvs.append(pltpu.make_async_remote_copy(
                    src_ref=src, dst_ref=outs[a].at[pid], send_sem=send_sems.at[s], recv_sem=recv_sems.at[s],
                    device_id=peer, device_id_type=pl.DeviceIdType.MESH))
        for cp in recvs:
            cp.wait_recv()
        for cp in sends:
            cp.wait_send()
        for cp in local:
            cp.wait()

    any_spec = pl.BlockSpec(memory_space=pl.ANY)
    out_shape = [jax.ShapeDtypeStruct((N_DEV,) + tuple(a.shape) if gather else tuple(a.shape), a.dtype) for a in arrs]
    return pl.pallas_call(
        body, name=name, in_specs=[any_spec] * n, out_specs=[any_spec] * n, out_shape=out_shape,
        scratch_shapes=[pltpu.SemaphoreType.DMA((n * (N_DEV - 1),)), pltpu.SemaphoreType.DMA((n * (N_DEV - 1),)),
                        pltpu.SemaphoreType.DMA((n,))],
        compiler_params=pltpu.CompilerParams(has_side_effects=True))(*arrs)


def _device_index():
    x, y, c = _mesh_pos()
    return 4 * x + 2 * y + c


def _landing(srcs, gather):
    me = _device_index()
    lands = []
    for a in srcs:
        own = a[None] if gather else lax.dynamic_slice_in_dim(a, me, 1, axis=0)
        shape = (N_DEV,) + tuple(a.shape) if gather else tuple(a.shape)
        lands.append(lax.dynamic_update_slice_in_dim(lax.empty(shape, a.dtype), own, me, axis=0))
    return lands


def _exchange_copies(ins, lands, send_sems, recv_sems, gather):
    pos = _mesh_pos()
    me = 4 * pos[0] + 2 * pos[1] + pos[2]
    sends, recvs = [], []
    for k in range(1, N_DEV):
        peer, pid = _peer(pos, k)
        for a in range(len(ins)):
            s = a * (N_DEV - 1) + k - 1
            src = ins[a] if gather else ins[a].at[pid]
            sends.append(pltpu.make_async_remote_copy(
                src_ref=src, dst_ref=lands[a].at[me], send_sem=send_sems.at[s], recv_sem=recv_sems.at[s],
                device_id=peer, device_id_type=pl.DeviceIdType.MESH))
            recvs.append(pltpu.make_async_remote_copy(
                src_ref=src, dst_ref=lands[a].at[pid], send_sem=send_sems.at[s], recv_sem=recv_sems.at[s],
                device_id=peer, device_id_type=pl.DeviceIdType.MESH))
    return sends, recvs


def _exchange_start(srcs, gather, name, after=None):
    n = len(srcs)
    lands = _landing(srcs, gather)
    nsem = n * (N_DEV - 1)
    extra = [] if after is None else [after]

    def body(*refs):
        ins, lnd = refs[:n], refs[n:2 * n]
        send_sems, recv_sems = refs[2 * n + len(extra)], refs[2 * n + len(extra) + 1]
        token = refs[-1]
        sends, _ = _exchange_copies(ins, lnd, send_sems, recv_sems, gather)
        for cp in sends:
            cp.start()
        token[...] = jnp.zeros_like(token)

    hbm = pl.BlockSpec(memory_space=pltpu.HBM)
    sem = pl.BlockSpec(memory_space=pltpu.SEMAPHORE)
    bufs = list(srcs) + lands
    out_shape = ([pltpu.SemaphoreType.DMA((nsem,)), pltpu.SemaphoreType.DMA((nsem,))]
                 + [pltpu.HBM(b.shape, b.dtype) for b in bufs] + [jax.ShapeDtypeStruct((8, LANES), F32)])
    outs = pl.pallas_call(
        body, name=name, in_specs=[hbm] * (2 * n) + [pl.BlockSpec(memory_space=pl.ANY)] * len(extra),
        out_specs=[sem, sem] + [hbm] * (2 * n) + [pl.BlockSpec(memory_space=pltpu.VMEM)], out_shape=out_shape,
        input_output_aliases={i: 2 + i for i in range(2 * n)},
        compiler_params=pltpu.CompilerParams(has_side_effects=pltpu.SideEffectType.DATAFLOW_SIDE_EFFECTING),
    )(*[pltpu.with_memory_space_constraint(b, pltpu.HBM) for b in bufs], *extra)
    return (outs[0], outs[1], outs[2:2 + n], outs[2 + n:2 + 2 * n]), outs[-1]


def _exchange_wait(state, after, gather, name):
    send_sems, recv_sems, srcs, lands = state
    n = len(srcs)
    after = list(after) if isinstance(after, (list, tuple)) else [after]

    def body(*refs):
        ins, lnd = refs[:n], refs[n:2 * n]
        s_sems, r_sems = refs[2 * n], refs[2 * n + 1]
        sends, recvs = _exchange_copies(ins, lnd, s_sems, r_sems, gather)
        for cp in sends:
            cp.wait_send()
        for cp in recvs:
            cp.wait_recv()

    hbm = pl.BlockSpec(memory_space=pltpu.HBM)
    sem = pl.BlockSpec(memory_space=pltpu.SEMAPHORE)
    bufs = list(srcs) + list(lands)
    outs = pl.pallas_call(
        body, name=name, in_specs=[hbm] * (2 * n) + [sem, sem] + [pl.BlockSpec(memory_space=pl.ANY)] * len(after),
        out_specs=[hbm] * (2 * n), out_shape=[pltpu.HBM(b.shape, b.dtype) for b in bufs],
        input_output_aliases={i: i for i in range(2 * n)},
        compiler_params=pltpu.CompilerParams(has_side_effects=pltpu.SideEffectType.DATAFLOW_SIDE_EFFECTING),
    )(*bufs, send_sems, recv_sems, *after)
    return outs[n:]


def _adam_vals(w, g, m, v):
    m = ADAM_B1 * m + (1.0 - ADAM_B1) * g
    v = ADAM_B2 * v + (1.0 - ADAM_B2) * (g * g)
    m_hat = m / ADAM_C1
    v_hat = v / ADAM_C2
    delta = -ADAM_LR * (m_hat / (jnp.sqrt(v_hat) + ADAM_EPS) + ADAM_WD * w)
    return delta, m, v


def _small_allreduce_adam(part, w, m, v, name):
    rows = part.shape[0]

    def body(p_ref, w_ref, m_ref, v_ref, g_ref, d_ref, nm_ref, nv_ref, gath, send_sems, recv_sems):
        pos = _mesh_pos()
        me = 4 * pos[0] + 2 * pos[1] + pos[2]
        gath[me] = p_ref[...]
        sends, recvs = [], []
        for k in range(1, N_DEV):
            peer, pid = _peer(pos, k)
            cp = pltpu.make_async_remote_copy(
                src_ref=p_ref, dst_ref=gath.at[me], send_sem=send_sems.at[k - 1], recv_sem=recv_sems.at[k - 1],
                device_id=peer, device_id_type=pl.DeviceIdType.MESH)
            cp.start()
            sends.append(cp)
            recvs.append(pltpu.make_async_remote_copy(
                src_ref=p_ref, dst_ref=gath.at[pid], send_sem=send_sems.at[k - 1], recv_sem=recv_sems.at[k - 1],
                device_id=peer, device_id_type=pl.DeviceIdType.MESH))
        for cp in recvs:
            cp.wait_recv()
        for cp in sends:
            cp.wait_send()
        g = gath[0]
        for p in range(1, N_DEV):
            g = g + gath[p]
        g_ref[...] = g
        d, nm, nv = _adam_vals(w_ref[...], g, m_ref[...], v_ref[...])
        d_ref[...] = d
        nm_ref[...] = nm
        nv_ref[...] = nv

    vm = pl.BlockSpec(memory_space=pltpu.VMEM)
    sds = jax.ShapeDtypeStruct((rows, LANES), F32)
    return pl.pallas_call(
        body, name=name, in_specs=[vm] * 4, out_specs=[vm] * 4, out_shape=[sds] * 4,
        scratch_shapes=[pltpu.VMEM((N_DEV, rows, LANES), F32), pltpu.SemaphoreType.DMA((N_DEV - 1,)),
                        pltpu.SemaphoreType.DMA((N_DEV - 1,))],
        compiler_params=pltpu.CompilerParams(has_side_effects=True))(part, w, m, v)


def _reduce_adam(recvs, w, m, v, name):
    shape = w.shape
    c = shape[-1]
    r = int(np.prod(shape[1:-1]))
    rs = [a.reshape(N_DEV, r, c) for a in recvs]
    w2, m2, v2 = w.reshape(DEPTH * r, c), m.reshape(DEPTH * r, c), v.reshape(DEPTH * r, c)
    tr = r
    while tr * c * 4 > (1 << 20) and tr % 16 == 0:
        tr //= 2
    nt = r // tr

    def body(*refs):
        r_refs = refs[:DEPTH]
        w_ref, m_ref, v_ref, g_ref, d_ref, nm_ref, nv_ref = refs[DEPTH:]
        for l, r_ref in enumerate(r_refs):
            @pl.when(pl.program_id(0) // nt == l)
            def _(r_ref=r_ref):
                g = r_ref[0].astype(F32)
                for p in range(1, N_DEV):
                    g = g + r_ref[p].astype(F32)
                g_ref[...] = g
                d, nm, nv = _adam_vals(w_ref[...], g, m_ref[...], v_ref[...])
                d_ref[...] = d
                nm_ref[...] = nm
                nv_ref[...] = nv

    recv_specs = [pl.BlockSpec((N_DEV, tr, c), lambda i, l=l: (0, jnp.clip(i - l * nt, 0, nt - 1), 0))
                  for l in range(DEPTH)]
    spec = pl.BlockSpec((tr, c), lambda i: (i, 0))
    sds = jax.ShapeDtypeStruct((DEPTH * r, c), F32)
    outs = pl.pallas_call(
        body, name=name, grid=(DEPTH * nt,), in_specs=recv_specs + [spec, spec, spec],
        out_specs=[spec] * 4, out_shape=[sds] * 4, compiler_params=_cp("arbitrary"))(*rs, w2, m2, v2)
    return [o.reshape(shape) for o in outs]


def _pad_heads(w, real=HEAD_DIM):
    lead = w.shape[:-1]
    w = w.reshape(lead + (HEADS, real))
    w = jnp.pad(w, [(0, 0)] * len(lead) + [(0, 0), (0, LANES - real)])
    return w.reshape(lead + (HP,))


def _unpad_heads(w, real=HEAD_DIM):
    lead = w.shape[:-1]
    return w.reshape(lead + (HEADS, LANES))[..., :real].reshape(lead + (HEADS * real,))


_IN_SEGS = (("fq", 0, 256), ("fk", 256, 512), ("fv", 512, 768), ("ff", 768, 772), ("cq", 772, 1028),
            ("ckv", 1028, 1156), ("kr", 1156, 1188), ("rq", 1188, 1444), ("rk", 1444, 1700), ("rv", 1700, 1956),
            ("rg", 1956, 2212), ("sq", 2212, 2468), ("sk", 2468, 2724), ("sv", 2724, 2980))


def _pad_w_in(w):
    s = {n: w[:, a:b] for n, a, b in _IN_SEGS}
    rows = w.shape[0]
    z = lambda n: jnp.zeros((rows, n), w.dtype)
    parts = [s[n] for n in ("fq", "fk", "fv", "rq", "rk", "rv", "sq", "sk", "sv", "rg", "cq", "ckv")]
    parts += [z(HEAD_DIM), s["kr"], z(LANES - HEAD_DIM - ROPE_MLA), s["ff"], z(LANES - HEADS),
              z(NP_IN - OFF_FF - LANES)]
    return jnp.concatenate(parts, axis=1)


def _unpad_w_in(wp):
    seg = lambda off, n=GROUP: wp[:, off:off + n]
    parts = [seg(OFF_FOX), seg(OFF_FOX + GROUP), seg(OFF_FOX + 2 * GROUP), seg(OFF_FF, HEADS),
             seg(OFF_CQ, Q_RANK), seg(OFF_CKV, KV_RANK), seg(OFF_KR + HEAD_DIM, ROPE_MLA),
             seg(OFF_RET), seg(OFF_RET + GROUP), seg(OFF_RET + 2 * GROUP), seg(OFF_RG),
             seg(OFF_SB), seg(OFF_SB + GROUP), seg(OFF_SB + 2 * GROUP)]
    return jnp.concatenate(parts, axis=1)


def _pad_w_kv(w):
    w4 = w.reshape(KV_RANK, HEADS, 2 * HEAD_DIM)
    k = w4[:, :, :HEAD_DIM].reshape(KV_RANK, GROUP)
    v = w4[:, :, HEAD_DIM:].reshape(KV_RANK, GROUP)
    return jnp.concatenate([_pad_heads(k), _pad_heads(v)], axis=1)


def _unpad_w_kv(wp):
    k = _unpad_heads(wp[:, :HP]).reshape(KV_RANK, HEADS, HEAD_DIM)
    v = _unpad_heads(wp[:, HP:]).reshape(KV_RANK, HEADS, HEAD_DIM)
    return jnp.concatenate([k, v], axis=-1).reshape(KV_RANK, HEADS * 2 * HEAD_DIM)


def _pad_gain_out(g):
    g = jnp.pad(g.reshape(4 * HEADS, HEAD_DIM), ((0, 0), (0, LANES - HEAD_DIM)))
    return g.reshape(1, 4 * HP)


def _unpad_gain_out(gp):
    return gp.reshape(4 * HEADS, LANES)[:, :HEAD_DIM].reshape(D_MODEL)


_SMALL = (("g_mix_pre", 1024), ("g_mix_post", 1024), ("g_ffn_pre", 1024), ("g_ffn_post", 1024), ("g_mix_out", 1024),
          ("g_q_lora", 256), ("g_kv_lora", 128), ("b_forget", 4))


def _pack_small(vals):
    parts = []
    for name, n in _SMALL:
        a = vals[name].astype(F32)
        if n < LANES:
            a = jnp.pad(a, ((0, 0), (0, LANES - n)))
        parts.append(a)
    return jnp.concatenate(parts, axis=1).reshape(DEPTH * SMALL_ROWS, LANES)


def _unpack_small(packed):
    flat = packed.reshape(DEPTH, SMALL_ROWS * LANES)
    out, off = {}, 0
    for name, n in _SMALL:
        out[name] = flat[:, off:off + n]
        off += max(n, LANES)
    return out


def kernel(x, positions, g_mix_pre, w_in, b_forget, g_q_lora, w_q_up, g_kv_lora, w_kv_up, g_mix_out, w_out, g_mix_post, g_ffn_pre, w_ffn_up, w_ffn_down, g_ffn_post, loss_target, m_g_mix_pre, m_w_in, m_b_forget, m_g_q_lora, m_w_q_up, m_g_kv_lora, m_w_kv_up, m_g_mix_out, m_w_out, m_g_mix_post, m_g_ffn_pre, m_w_ffn_up, m_w_ffn_down, m_g_ffn_post, v_g_mix_pre, v_w_in, v_b_forget, v_g_q_lora, v_w_q_up, v_g_kv_lora, v_w_kv_up, v_g_mix_out, v_w_out, v_g_mix_post, v_g_ffn_pre, v_w_ffn_up, v_w_ffn_down, v_g_ffn_post):
    t = x.shape[1]
    nq = t // BQ
    x0 = x[0]
    tgt = loss_target[0]
    pos = positions[0].astype(F32).reshape(t, 1)

    half_r, half_m = HEAD_DIM // 2, ROPE_MLA // 2
    invf_r = ROPE_BASE ** (-jnp.arange(half_r, dtype=F32) / half_r)
    invf_m = ROPE_BASE ** (-jnp.arange(half_m, dtype=F32) / half_m)
    invf = jnp.concatenate([invf_r, invf_r, invf_m, invf_m,
                            jnp.zeros((LANES - HEAD_DIM - ROPE_MLA,), F32)]).reshape(1, LANES)
    log_gamma = jnp.log1p(-jnp.power(2.0, -5.0 - jnp.arange(HEADS, dtype=F32)))
    lg_lanes = jnp.repeat(log_gamma, LANES).reshape(1, HP)
    lg_heads = jnp.broadcast_to(log_gamma[:, None, None], (HEADS, 1, LANES))

    big = [w_in, w_q_up, w_kv_up, w_out, w_ffn_up, w_ffn_down]
    bf = lambda w: w.astype(BF16)
    first = _exchange([bf(w_in[0]), bf(w_q_up), bf(w_kv_up)], True, "weights_gather_first")
    l0_state, l0_token = _exchange_start([bf(w[0]) for w in (w_out, w_ffn_up, w_ffn_down)], True,
                                         "weights_gather_layer0_start")
    l1_state, rest_token = _exchange_start([bf(w[1]) for w in (w_in, w_out, w_ffn_up, w_ffn_down)], True,
                                           "weights_gather_layer1_start", after=l0_token)
    wq_g = first[1].transpose(1, 2, 0, 3).reshape(DEPTH, Q_RANK, 384)
    wkv_g = first[2].transpose(1, 2, 0, 3).reshape(DEPTH, KV_RANK, 512)

    row = lambda g: g.reshape(1, -1)
    layers = []
    for l in range(DEPTH):
        layers.append(dict(
            wq=_pad_heads(wq_g[l], 96), wkv=_pad_w_kv(wkv_g[l]),
            g_pre=row(g_mix_pre[l]), g_post=row(g_mix_post[l]), g_fpre=row(g_ffn_pre[l]), g_fpost=row(g_ffn_post[l]),
            g_out=_pad_gain_out(g_mix_out[l]), g_q=row(g_q_lora[l]), g_kv=row(g_kv_lora[l]),
            b_pad=jnp.pad(b_forget[l], (0, LANES - HEADS)).reshape(1, LANES)))
    layers[0]["win"] = _pad_w_in(first[0].reshape(D_MODEL, D_IN))

    saved = []
    xin = x0
    h = _rms_fwd(xin, layers[0]["g_pre"] + rest_token[0:1, 0:1], "rms_pre_0")
    loss_row = dx = None
    for l, p in enumerate(layers):
        s = dict(x=xin, h=h)
        proj = _mm(h, p["win"], name=f"in_proj_{l}", tm=512, tn=NP_IN)
        cum = _cumsum(proj, False, f"forget_cumsum_{l}", forget_bias=p["b_pad"])
        fox, ret, sb, mla, fox_t, sb_t, mla_t, ret_t = _prep_fwd(proj, cum, pos, invf, lg_lanes, p["g_q"],
                                                          p["g_kv"], p["wq"], p["wkv"], f"prep_fwd_{l}")
        oa, lse_a = _softmax_fwd(fox, fox_t, chunk_mask=False, scale=1.0, name=f"fox_fwd_{l}")
        ob, lse_b = _softmax_fwd(mla, mla_t, chunk_mask=True, scale=1.0, name=f"mla_fwd_{l}")
        oc, ret_st = _ret_fwd(ret, ret_t, lg_heads, f"ret_fwd_{l}")
        od, sb_tot = _sb_fwd(sb, sb_t, f"sb_fwd_{l}")
        if l == 0:
            got = _exchange_wait(l0_state, od, True, "weights_gather_layer0_wait")
            p.update(wout=got[0].reshape(D_MODEL, D_MODEL), wup=got[1].transpose(1, 0, 2).reshape(D_MODEL, D_FF),
                     wdn=got[2].reshape(D_FF, D_MODEL))
        mixed = _post_fwd(oa, ob, oc, od, proj, p["g_out"], f"post_fwd_{l}")
        mix = _mm(mixed, p["wout"], name=f"out_proj_{l}", tm=2048)
        x1, h2 = _add_rms_fwd(xin, mix, p["g_post"], p["g_fpre"], f"mix_residual_{l}")
        a = _mm(h2, p["wup"], name=f"ffn_up_{l}", out_dtype=BF16, tm=2048)
        y = _mm(a, p["wdn"], name=f"ffn_down_{l}", a_fn=_relu2, tk=D_FF)
        s.update(proj=proj, fox=fox, ret=ret, sb=sb, mla=mla, fox_t=fox_t, sb_t=sb_t, mla_t=mla_t, ret_st=ret_st, oa=oa, ob=ob, oc=oc,
                 od=od, sb_tot=sb_tot, lse_a=lse_a, lse_b=lse_b, mixed=mixed, mix=mix, x1=x1, h2=h2, a=a, y=y)
        saved.append(s)
        if l == 0:
            got = _exchange_wait(l1_state, y, True, "weights_gather_layer1_wait")
            layers[1].update(win=_pad_w_in(got[0].reshape(D_MODEL, D_IN)), wout=got[1].reshape(D_MODEL, D_MODEL),
                             wup=got[2].transpose(1, 0, 2).reshape(D_MODEL, D_FF), wdn=got[3].reshape(D_FF, D_MODEL))
        if l + 1 < DEPTH:
            xin, h = _add_rms_fwd(x1, y, p["g_fpost"], layers[l + 1]["g_pre"], f"ffn_residual_{l}")
        else:
            loss_row, dx = _final_loss(x1, y, p["g_fpost"], tgt, "loss")

    small_g = {n: [None] * DEPTH for n, _ in _SMALL}
    big_g = [[None] * DEPTH for _ in range(6)]
    to_send = [
        lambda g: g.reshape(N_DEV, 1, D_MODEL // N_DEV, D_IN),
        lambda g: g.reshape(Q_RANK, N_DEV, 384 // N_DEV).transpose(1, 0, 2)[:, None],
        lambda g: g.reshape(KV_RANK, N_DEV, 512 // N_DEV).transpose(1, 0, 2)[:, None],
        lambda g: g.reshape(N_DEV, 1, D_MODEL // N_DEV, D_MODEL),
        lambda g: g[:, None],
        lambda g: g.reshape(N_DEV, 1, D_FF // N_DEV, D_MODEL),
    ]
    send_of = lambda ks, l: [to_send[k](big_g[k][l]).astype(BF16) for k in ks]
    late_state = early_state = None
    order_token = jnp.zeros((1, 1), F32)
    for l in reversed(range(DEPTH)):
        p, s = layers[l], saved[l]
        dy, dg = _norm_bwd(dx, s["y"], p["g_fpost"] + order_token, None, BF16, f"ffn_post_bwd_{l}")
        small_g["g_ffn_post"][l] = dg
        da = _mm(dy, p["wdn"], name=f"ffn_down_dx_{l}", tb=True, out_dtype=BF16, epi=_drelu2, epi_in=s["a"],
                 tm=2048)
        big_g[5][l] = _mm(s["a"], dy, name=f"ffn_down_dw_{l}", ta=True, a_fn=_relu2, tk=t, out_dtype=BF16)
        big_g[4][l] = _mm(s["h2"], da, name=f"ffn_up_dw_{l}", ta=True, tk=t, tn=D_FF // N_DEV, out_dtype=BF16,
                          column_blocks=True)
        dh2 = _mm(da, p["wup"], name=f"ffn_up_dx_{l}", tb=True, tk=D_FF)
        dx1, dmix, small_g["g_ffn_pre"][l], small_g["g_mix_post"][l] = _norm_bwd_pair(
            dh2, s["x1"], p["g_fpre"], dx, s["mix"], p["g_post"], f"ffn_pre_mix_post_bwd_{l}")
        dmixed = _mm(dmix, p["wout"], name=f"out_proj_dx_{l}", tb=True, tm=2048)
        big_g[3][l] = _mm(s["mixed"], dmix, name=f"out_proj_dw_{l}", ta=True, tk=t, out_dtype=BF16)
        g_out = p["g_out"]
        if l == 0:
            early_state, early_token = _exchange_start(send_of((3, 4, 5), 0), False, "grads_layer0_early_start")
            g_out = g_out + early_token[0:1, 0:1]
        doa, dob, doc, dod, dla, dlb, drg, dgo = _post_bwd(dmixed, s["oa"], s["ob"], s["oc"], s["od"], s["proj"],
                                                           g_out, f"post_bwd_{l}")
        small_g["g_mix_out"][l] = _unpad_gain_out(dgo).reshape(1, D_MODEL)
        as_rows = lambda a: a[:, :HEADS].T.reshape(HEADS, t // BQA, 1, BQA)
        dfq, dfk, dfv, dcum_k, dcum_q = _softmax_bwd(s["fox"], s["fox_t"], doa, s["lse_a"], as_rows(dla), bias=True,
                                                     chunk_mask=False, scale=1.0, name=f"fox_bwd_{l}")
        dmq, dmk, dmv = _softmax_bwd(s["mla"], s["mla_t"], dob, s["lse_b"], as_rows(dlb), bias=False, chunk_mask=True,
                                     scale=1.0, name=f"mla_bwd_{l}")
        drq, drk, drv = _ret_bwd(s["ret"], s["ret_st"], lg_heads, doc, f"ret_bwd_{l}")
        dsq, dsk, dsv = _sb_bwd(s["sb"], s["sb_t"], dod, s["sb_tot"], f"sb_bwd_{l}")
        dcum_q = jnp.pad(dcum_q.reshape(HEADS, t).T, ((0, 0), (0, LANES - HEADS)))
        dlsf = _cumsum(dcum_q, True, f"forget_cumsum_bwd_{l}", partials=dcum_k)
        dproj, dwq, dwkv, dgq, dgkv, dbf = _prep_bwd(
            (dfq, dfk, dfv), (drq, drk, drv), (dsq, dsk, dsv), (dmq, dmk, dmv), drg, dlsf, s["proj"], pos, invf,
            lg_lanes, p["b_pad"], p["g_q"], p["g_kv"], p["wq"], p["wkv"], f"prep_bwd_{l}")
        small_g["g_q_lora"][l] = dgq
        small_g["g_kv_lora"][l] = dgkv
        small_g["b_forget"][l] = dbf[:, :HEADS]
        big_g[1][l] = _unpad_heads(dwq, 96)
        big_g[2][l] = _unpad_w_kv(dwkv)
        dh = _mm(dproj, p["win"], name=f"in_proj_dx_{l}", tb=True, tk=NP_IN)
        big_g[0][l] = _unpad_w_in(_mm(s["h"], dproj, name=f"in_proj_dw_{l}", ta=True, tn=NP_IN // 2, tk=2048,
                                          out_dtype=BF16))
        g_pre = p["g_pre"]
        if l == 0:
            last_state, last_token = _exchange_start(send_of((0, 1, 2), 0), False, "grads_layer0_rest_start")
            g_pre = g_pre + last_token[0:1, 0:1]
        dx, dg = _norm_bwd(dh, s["x"], g_pre, dx1, F32, f"mix_pre_bwd_{l}")
        small_g["g_mix_pre"][l] = dg
        if l == DEPTH - 1:
            late_state, late_token = _exchange_start(send_of(range(6), l), False, "grads_layer1_start")
            order_token = late_token[0:1, 0:1]
    grad_x = dx.reshape(1, t, D_MODEL)

    res = {}
    small_w = dict(g_mix_pre=g_mix_pre, g_mix_post=g_mix_post, g_ffn_pre=g_ffn_pre, g_ffn_post=g_ffn_post,
                   g_mix_out=g_mix_out, g_q_lora=g_q_lora, g_kv_lora=g_kv_lora, b_forget=b_forget)
    small_m = dict(g_mix_pre=m_g_mix_pre, g_mix_post=m_g_mix_post, g_ffn_pre=m_g_ffn_pre, g_ffn_post=m_g_ffn_post,
                   g_mix_out=m_g_mix_out, g_q_lora=m_g_q_lora, g_kv_lora=m_g_kv_lora, b_forget=m_b_forget)
    small_v = dict(g_mix_pre=v_g_mix_pre, g_mix_post=v_g_mix_post, g_ffn_pre=v_g_ffn_pre, g_ffn_post=v_g_ffn_post,
                   g_mix_out=v_g_mix_out, g_q_lora=v_g_q_lora, g_kv_lora=v_g_kv_lora, b_forget=v_b_forget)
    n_small = DEPTH * SMALL_ROWS
    extra = lambda a: jnp.concatenate([a, jnp.zeros((8, LANES), F32)], axis=0)
    part = jnp.concatenate([_pack_small({n: jnp.concatenate(small_g[n], axis=0) for n, _ in _SMALL}),
                            jnp.broadcast_to(loss_row, (8, LANES))], axis=0)
    sres = _small_allreduce_adam(part, extra(_pack_small(small_w)), extra(_pack_small(small_m)),
                                 extra(_pack_small(small_v)), "small_allreduce_adamw")
    loss = sres[0][n_small, 0]
    sg, sd, sm, sv = [_unpack_small(a[:n_small]) for a in sres]
    for n, _ in _SMALL:
        res[n] = [sg[n], sd[n], sm[n], sv[n]]

    late = _exchange_wait(late_state, dx, False, "grads_layer1_wait")
    early = _exchange_wait(early_state, dx, False, "grads_layer0_early_wait")
    ms = [m_w_in, m_w_q_up, m_w_kv_up, m_w_out, m_w_ffn_up, m_w_ffn_down]
    vs = [v_w_in, v_w_q_up, v_w_kv_up, v_w_out, v_w_ffn_up, v_w_ffn_down]
    names = ["w_in", "w_q_up", "w_kv_up", "w_out", "w_ffn_up", "w_ffn_down"]
    for k in (3, 4, 5):
        res[names[k]] = _reduce_adam([early[k - 3], late[k]], big[k], ms[k], vs[k], f"adamw_{names[k]}")
    done = [sres[0]] + [res[names[k]][0] for k in (3, 4, 5)]
    last = _exchange_wait(last_state, done, False, "grads_layer0_rest_wait")
    for k in (0, 1, 2):
        res[names[k]] = _reduce_adam([last[k], late[k]], big[k], ms[k], vs[k], f"adamw_{names[k]}")

    order = ["g_mix_pre", "w_in", "b_forget", "g_q_lora", "w_q_up", "g_kv_lora", "w_kv_up", "g_mix_out", "w_out",
             "g_mix_post", "g_ffn_pre", "w_ffn_up", "w_ffn_down", "g_ffn_post"]
    outs = [loss, grad_x]
    for idx in range(4):
        outs += [res[n][idx] for n in order]
    return tuple(outs)
```

```python
import functools
import math

import numpy as np
import jax
import jax.numpy as jnp
from jax import lax
from jax.experimental import pallas as pl
from jax.experimental.pallas import tpu as pltpu

F32 = jnp.float32
BF16 = jnp.bfloat16

D_MODEL = 1024
DEPTH = 2
N_DEV = 8
GROUP = 256
HEADS = 4
HEAD_DIM = 64
LANES = 128
HP = HEADS * LANES
QKV = 3 * HP
Q_RANK = 256
KV_RANK = 128
ROPE_MLA = 32
MLA_SCALE = (HEAD_DIM + ROPE_MLA) ** -0.5
D_FF = 4096
D_IN = 2980
CHUNK_SHIFT = 6
EPS = 1e-6
ROPE_BASE = 10000.0
NEG = -1e30

QKV_IN = 3 * GROUP
OFF_FOX, OFF_RET, OFF_SB = 0, QKV_IN, 2 * QKV_IN
OFF_RG = 3 * QKV_IN
OFF_CQ = OFF_RG + GROUP
OFF_CKV = OFF_CQ + Q_RANK
OFF_KR = OFF_CKV + LANES
OFF_FF = OFF_KR + LANES
NP_IN = 3328

BQ = 256
TKS = 128
TM = 512
VMEM_LIMIT = 58 * 1024 * 1024

ADAM_LR, ADAM_B1, ADAM_B2, ADAM_EPS, ADAM_WD, ADAM_STEP = 0.001, 0.9, 0.999, 1e-08, 0.01, 10
ADAM_C1 = 1.0 - ADAM_B1 ** ADAM_STEP
ADAM_C2 = 1.0 - ADAM_B2 ** ADAM_STEP

SMALL_ROWS = 44

NT = (((1,), (1,)), ((), ()))
TN = (((0,), (0,)), ((), ()))


def _cp(*sem):
    return pltpu.CompilerParams(dimension_semantics=sem if sem else None, vmem_limit_bytes=VMEM_LIMIT)


def _bdot(a, b, dn=None):
    if dn is None:
        return jnp.dot(a, b, preferred_element_type=F32)
    return lax.dot_general(a, b, dn, preferred_element_type=F32)


def _split2(x):
    hi = x.astype(BF16)
    lo = (x - hi.astype(F32)).astype(BF16)
    return hi, lo


def _mm(a, b, *, name, ta=False, tb=False, out_dtype=F32, a_fn=None, epi=None, epi_in=None, column_blocks=False,
        tm=1024, tn=1024, tk=1024):
    m, k = (a.shape[1], a.shape[0]) if ta else a.shape
    n = b.shape[0] if tb else b.shape[1]
    tm, tn, tk = min(tm, m), min(tn, n), min(tk, k)
    assert m % tm == 0 and n % tn == 0 and k % tk == 0, (name, m, n, k)
    nk = k // tk
    dn = (((0 if ta else 1,), (1 if tb else 0,)), ((), ()))

    def body(*refs):
        if epi is None:
            a_ref, b_ref, o_ref = refs[:3]
            e_ref = None
            rest = refs[3:]
        else:
            a_ref, b_ref, e_ref, o_ref = refs[:4]
            rest = refs[4:]
        av = a_ref[...]
        if a_fn is not None:
            av = a_fn(av)
        part = lax.dot_general(av.astype(BF16), b_ref[...].astype(BF16), dn, preferred_element_type=F32)

        def finish(r):
            if epi is not None:
                r = epi(r, e_ref[...])
            o_ref[...] = r.astype(out_dtype)

        if nk == 1:
            finish(part)
        else:
            acc_ref = rest[0]
            kk = pl.program_id(2)

            @pl.when(kk == 0)
            def _():
                acc_ref[...] = part

            @pl.when(kk > 0)
            def _():
                acc_ref[...] += part

            @pl.when(kk == nk - 1)
            def _():
                finish(acc_ref[...])

    a_spec = pl.BlockSpec((tk, tm), lambda i, j, kk: (kk, i)) if ta else pl.BlockSpec((tm, tk), lambda i, j, kk: (i, kk))
    b_spec = pl.BlockSpec((tn, tk), lambda i, j, kk: (j, kk)) if tb else pl.BlockSpec((tk, tn), lambda i, j, kk: (kk, j))
    o_spec = pl.BlockSpec((tm, tn), lambda i, j, kk: (i, j))
    in_specs = [a_spec, b_spec]
    args = [a, b]
    if epi is not None:
        in_specs.append(o_spec)
        args.append(epi_in)
    out_shape = jax.ShapeDtypeStruct((m, n), out_dtype)
    if column_blocks:
        assert epi is None
        o_spec = pl.BlockSpec((None, tm, tn), lambda i, j, kk: (j, i, 0))
        out_shape = jax.ShapeDtypeStruct((n // tn, m, tn), out_dtype)
    return pl.pallas_call(
        body, name=name, grid=(m // tm, n // tn, nk),
        in_specs=in_specs, out_specs=o_spec,
        out_shape=out_shape,
        scratch_shapes=[pltpu.VMEM((tm, tn), F32)] if nk > 1 else [],
        compiler_params=_cp("parallel", "parallel", "arbitrary"),
    )(*args)


def _relu2(v):
    r = jnp.maximum(v, 0.0)
    return r * r


def _drelu2(du, av):
    return du * (2.0 * jnp.maximum(av, 0.0))


def _rms(v, g):
    r = lax.rsqrt(jnp.mean(v * v, axis=-1, keepdims=True) + EPS)
    return v * r * g


def _row_spec(w):
    return pl.BlockSpec((TM, w), lambda i: (i, 0))


def _vec_spec(w):
    return pl.BlockSpec((1, w), lambda i: (0, 0))


def _rms_fwd(x, g, name):
    t, d = x.shape

    def body(x_ref, g_ref, h_ref):
        h_ref[...] = _rms(x_ref[...], g_ref[...]).astype(BF16)

    return pl.pallas_call(
        body, name=name, grid=(t // TM,), in_specs=[_row_spec(d), _vec_spec(d)], out_specs=_row_spec(d),
        out_shape=jax.ShapeDtypeStruct((t, d), BF16), compiler_params=_cp("parallel"))(x, g)


def _add_rms_fwd(x, y, g1, g2, name):
    t, d = x.shape

    def body(x_ref, y_ref, g1_ref, g2_ref, xn_ref, h_ref):
        xn = x_ref[...] + _rms(y_ref[...], g1_ref[...])
        xn_ref[...] = xn
        h_ref[...] = _rms(xn, g2_ref[...]).astype(BF16)

    return pl.pallas_call(
        body, name=name, grid=(t // TM,),
        in_specs=[_row_spec(d), _row_spec(d), _vec_spec(d), _vec_spec(d)],
        out_specs=[_row_spec(d), _row_spec(d)],
        out_shape=[jax.ShapeDtypeStruct((t, d), F32), jax.ShapeDtypeStruct((t, d), BF16)],
        compiler_params=_cp("parallel"))(x, y, g1, g2)


def _final_loss(x, y, g, tgt, name):
    t, d = x.shape

    def body(x_ref, y_ref, g_ref, t_ref, l_ref, dx_ref):
        @pl.when(pl.program_id(0) == 0)
        def _():
            l_ref[...] = jnp.zeros_like(l_ref)

        err = x_ref[...] + _rms(y_ref[...], g_ref[...]) - t_ref[...]
        dx_ref[...] = err * (1.0 / d)
        l_ref[...] += jnp.sum(jnp.sum(err * err, axis=1, keepdims=True), axis=0, keepdims=True) * (0.5 / d)

    return pl.pallas_call(
        body, name=name, grid=(t // TM,),
        in_specs=[_row_spec(d), _row_spec(d), _vec_spec(d), _row_spec(d)],
        out_specs=[pl.BlockSpec((1, LANES), lambda i: (0, 0)), _row_spec(d)],
        out_shape=[jax.ShapeDtypeStruct((1, LANES), F32), jax.ShapeDtypeStruct((t, d), F32)],
        compiler_params=_cp("arbitrary"))(x, y, g, tgt)


def _rms_bwd_vals(dn, v, g):
    w = v.shape[-1]
    r = lax.rsqrt(jnp.mean(v * v, axis=-1, keepdims=True) + EPS)
    vh = v * r
    dgp = jnp.sum(dn * vh, axis=0, keepdims=True)
    dvh = dn * g
    dv = r * (dvh - vh * (jnp.sum(dvh * vh, axis=-1, keepdims=True) * (1.0 / w)))
    return dv, dgp


def _norm_bwd(dn, v, g, resid, out_dtype, name):
    t, d = v.shape
    has_res = resid is not None

    def body(*refs):
        if has_res:
            dn_ref, v_ref, g_ref, r_ref, dv_ref, dg_ref = refs
        else:
            dn_ref, v_ref, g_ref, dv_ref, dg_ref = refs

        @pl.when(pl.program_id(0) == 0)
        def _():
            dg_ref[...] = jnp.zeros_like(dg_ref)

        dv, dgp = _rms_bwd_vals(dn_ref[...].astype(F32), v_ref[...], g_ref[...])
        if has_res:
            dv = dv + r_ref[...]
        dv_ref[...] = dv.astype(out_dtype)
        dg_ref[...] += dgp

    in_specs = [_row_spec(d), _row_spec(d), _vec_spec(d)] + ([_row_spec(d)] if has_res else [])
    args = [dn, v, g] + ([resid] if has_res else [])
    return pl.pallas_call(
        body, name=name, grid=(t // TM,), in_specs=in_specs,
        out_specs=[_row_spec(d), _vec_spec(d)],
        out_shape=[jax.ShapeDtypeStruct((t, d), out_dtype), jax.ShapeDtypeStruct((1, d), F32)],
        compiler_params=_cp("arbitrary"))(*args)


def _norm_bwd_pair(dn, v1, g1, resid, v2, g2, name):
    t, d = v1.shape

    def body(dn_ref, v1_ref, g1_ref, r_ref, v2_ref, g2_ref, d1_ref, d2_ref, dg1_ref, dg2_ref):
        @pl.when(pl.program_id(0) == 0)
        def _():
            dg1_ref[...] = jnp.zeros_like(dg1_ref)
            dg2_ref[...] = jnp.zeros_like(dg2_ref)

        d1, dgp1 = _rms_bwd_vals(dn_ref[...].astype(F32), v1_ref[...], g1_ref[...])
        d1 = d1 + r_ref[...]
        d1_ref[...] = d1
        d2, dgp2 = _rms_bwd_vals(d1, v2_ref[...], g2_ref[...])
        d2_ref[...] = d2.astype(BF16)
        dg1_ref[...] += dgp1
        dg2_ref[...] += dgp2

    row, vec = _row_spec(d), _vec_spec(d)
    return pl.pallas_call(
        body, name=name, grid=(t // TM,), in_specs=[row, row, vec, row, row, vec],
        out_specs=[row, row, vec, vec],
        out_shape=[jax.ShapeDtypeStruct((t, d), F32), jax.ShapeDtypeStruct((t, d), BF16),
                   jax.ShapeDtypeStruct((1, d), F32), jax.ShapeDtypeStruct((1, d), F32)],
        compiler_params=_cp("arbitrary"))(dn, v1, g1, resid, v2, g2)


def _rope_trig(pos, invf):
    ang = pos * invf
    return jnp.cos(ang), jnp.sin(ang)


def _rope_tables(trig, w, lo, half):
    c, s = trig
    lane = lax.broadcasted_iota(jnp.int32, c.shape, 1)
    active = (lane >= lo) & (lane < lo + 2 * half)
    cos = jnp.concatenate([jnp.where(active, c, 1.0)] * (w // LANES), axis=1)
    sin = jnp.concatenate([jnp.where(active, s, 0.0)] * (w // LANES), axis=1)
    lanes = lax.broadcasted_iota(jnp.int32, (c.shape[0], w), 1) & (LANES - 1)
    first = (lanes >= lo) & (lanes < lo + half)
    second = (lanes >= lo + half) & (lanes < lo + 2 * half)
    return cos, sin, first, second


def _rope_apply(v, cos, sin, first, second, half, sign):
    w = v.shape[-1]
    up = pltpu.roll(v, w - half, 1)
    dn = pltpu.roll(v, half, 1)
    rot = jnp.where(first, -up, jnp.where(second, dn, 0.0))
    return v * cos + rot * (sin * sign)


def _split3(c):
    hi = c.astype(BF16).astype(F32)
    mid = (c - hi).astype(BF16).astype(F32)
    return hi, mid, (c - hi) - mid


def _spread_heads(x):
    low = lax.broadcasted_iota(jnp.int32, (x.shape[0], LANES), 1) < HEAD_DIM
    out = []
    for c in range(x.shape[1] // LANES):
        blk = x[:, c * LANES:(c + 1) * LANES]
        out.append(jnp.where(low, blk, 0.0))
        out.append(jnp.where(low, pltpu.roll(blk, HEAD_DIM, 1), 0.0))
    return jnp.concatenate(out, axis=1)


def _gather_heads(y):
    low = lax.broadcasted_iota(jnp.int32, (y.shape[0], LANES), 1) < HEAD_DIM
    out = []
    for c in range(y.shape[1] // (2 * LANES)):
        a = y[:, 2 * c * LANES:(2 * c + 1) * LANES]
        b = y[:, (2 * c + 1) * LANES:(2 * c + 2) * LANES]
        out.append(jnp.where(low, a, pltpu.roll(b, HEAD_DIM, 1)))
    return jnp.concatenate(out, axis=1)


def _transposed_tiles(dst, row0, blocks, width):
    for b, blk in enumerate(blocks):
        bt = blk.T.astype(BF16)
        rows = bt.shape[0]
        for w in range(TM // width):
            dst[w, row0 + b * rows:row0 + (b + 1) * rows, :] = bt[:, w * width:(w + 1) * width]


def _prep_fwd(proj, cum, pos, invf, lg_lanes, g_q, g_kv, wq_pad, wkv_pad, name):
    t = proj.shape[0]

    def body(fox_ref, ret_ref, sb_ref, cq_ref, ckv_ref, kr_ref, cum_ref, pos_ref, invf_ref, lg_ref,
             gq_ref, gkv_ref, wq_ref, wkv_ref,
             ofox_ref, oret_ref, osb_ref, omla_ref, ofoxt_ref, osbt_ref, omlat_ref, orett_ref):
        pos_v = pos_ref[...]
        foxv, retv, sbv = [[_spread_heads(ref[:, g * GROUP:(g + 1) * GROUP]) for g in range(3)]
                           for ref in (fox_ref, ret_ref, sb_ref)]
        osb_ref[:, 0:HP] = (sbv[0] * 0.125).astype(BF16)
        osb_ref[:, HP:2 * HP] = sbv[1].astype(BF16)
        osb_ref[:, 2 * HP:QKV] = sbv[2].astype(BF16)
        _transposed_tiles(osbt_ref, 0, [sbv[1], sbv[2]], TKS)
        lane = lax.broadcasted_iota(jnp.int32, (TM, LANES), 1)
        cumv = cum_ref[...]
        fq, fk = [], []
        for hb in range(HEADS):
            hi, mid, lo = _split3(cumv[:, hb:hb + 1])
            q = foxv[0][:, hb * LANES:(hb + 1) * LANES] * 0.125
            k = foxv[1][:, hb * LANES:(hb + 1) * LANES]
            ones_q = (lane >= HEAD_DIM) & (lane < HEAD_DIM + 3)
            ones_k = (lane >= HEAD_DIM + 3) & (lane < HEAD_DIM + 6)
            q = jnp.where(ones_q, 1.0, q)
            k = jnp.where(ones_k, 1.0, k)
            for n, part in enumerate((hi, mid, lo)):
                q = jnp.where(lane == HEAD_DIM + 3 + n, part, q)
                k = jnp.where(lane == HEAD_DIM + n, -part, k)
            fq.append(q)
            fk.append(k)
        fk = jnp.concatenate(fk, axis=1)
        ofox_ref[:, 0:HP] = jnp.concatenate(fq, axis=1).astype(BF16)
        ofox_ref[:, HP:2 * HP] = fk.astype(BF16)
        ofox_ref[:, 2 * HP:QKV] = foxv[2].astype(BF16)
        _transposed_tiles(ofoxt_ref, 0, [fk, foxv[2]], BQ)
        trig = _rope_trig(pos_v, invf_ref[...])
        cos, sin, first, second = _rope_tables(trig, HP, 0, HEAD_DIM // 2)
        nloc = (lax.broadcasted_iota(jnp.int32, (TM, 1), 0) & (BQ - 1)).astype(F32)
        dec = lg_ref[...] * nloc
        rq = _rope_apply(retv[0], cos, sin, first, second, HEAD_DIM // 2, 1.0)
        rk = _rope_apply(retv[1], cos, sin, first, second, HEAD_DIM // 2, 1.0)
        oret_ref[:, 0:HP] = (rq * jnp.exp(dec)).astype(BF16)
        rk = rk * 0.125 * jnp.exp(-dec)
        oret_ref[:, HP:2 * HP] = rk.astype(BF16)
        _transposed_tiles(orett_ref, 0, [rk], BQ)
        oret_ref[:, 2 * HP:QKV] = retv[2].astype(BF16)
        cosm, sinm, firstm, secondm = _rope_tables(trig, HP, HEAD_DIM, ROPE_MLA // 2)
        cqn = _rms(cq_ref[...], gq_ref[...]).astype(BF16)
        qm = _bdot(cqn, wq_ref[...])
        qm = _rope_apply(qm, cosm, sinm, firstm, secondm, ROPE_MLA // 2, 1.0)
        omla_ref[:, 0:HP] = (qm * MLA_SCALE).astype(BF16)
        ckvn = _rms(ckv_ref[...], gkv_ref[...]).astype(BF16)
        kv = _bdot(ckvn, wkv_ref[...])
        krr = _rope_apply(kr_ref[...], cosm[:, 0:LANES], sinm[:, 0:LANES], firstm[:, 0:LANES],
                          secondm[:, 0:LANES], ROPE_MLA // 2, 1.0)
        mk = kv[:, 0:HP] + jnp.concatenate([krr] * HEADS, axis=1)
        omla_ref[:, HP:2 * HP] = mk.astype(BF16)
        omla_ref[:, 2 * HP:QKV] = kv[:, HP:2 * HP].astype(BF16)
        _transposed_tiles(omlat_ref, 0, [mk, kv[:, HP:2 * HP]], BQ)

    def seg(off, w):
        return pl.BlockSpec((TM, w), lambda i, o=off // w: (i, o))

    def full(shape):
        return pl.BlockSpec(shape, lambda i: (0,) * len(shape))

    def tiles(width):
        return pl.BlockSpec((TM // width, 2 * HP, width), lambda i: (i, 0, 0))

    in_specs = [seg(OFF_FOX, QKV_IN), seg(OFF_RET, QKV_IN), seg(OFF_SB, QKV_IN), seg(OFF_CQ, Q_RANK),
                seg(OFF_CKV, LANES), seg(OFF_KR, LANES), _row_spec(LANES), pl.BlockSpec((TM, 1), lambda i: (i, 0)),
                full((1, LANES)), full((1, HP)), full((1, Q_RANK)), full((1, KV_RANK)),
                full((Q_RANK, HP)), full((KV_RANK, 2 * HP))]
    out_specs = [_row_spec(QKV)] * 4 + [tiles(BQ), tiles(TKS), tiles(BQ),
                                        pl.BlockSpec((TM // BQ, HP, BQ), lambda i: (i, 0, 0))]
    out_shape = [jax.ShapeDtypeStruct((t, QKV), BF16)] * 4 + [
        jax.ShapeDtypeStruct((t // BQ, 2 * HP, BQ), BF16), jax.ShapeDtypeStruct((t // TKS, 2 * HP, TKS), BF16),
        jax.ShapeDtypeStruct((t // BQ, 2 * HP, BQ), BF16), jax.ShapeDtypeStruct((t // BQ, HP, BQ), BF16)]
    return pl.pallas_call(
        body, name=name, grid=(t // TM,), in_specs=in_specs, out_specs=out_specs, out_shape=out_shape,
        compiler_params=_cp("parallel"))(proj, proj, proj, proj, proj, proj, cum, pos, invf, lg_lanes,
                                         g_q, g_kv, wq_pad, wkv_pad)


def _prep_bwd(dfox, dret, dsb, dmla, drg, dlsf, proj, pos, invf, lg_lanes, b_pad, g_q, g_kv,
              wq_pad, wkv_pad, name):
    t = proj.shape[0]

    def body(dfq, dfk, dfv, drq, drk, drv, dsq, dsk, dsv, dmq, dmk, dmv, drg_ref, dlsf_ref,
             cq_ref, ckv_ref, ff_ref, pos_ref, invf_ref, lg_ref, b_ref, gq_ref, gkv_ref, wq_ref, wkv_ref,
             dp_ref, dwq_ref, dwkv_ref, dgq_ref, dgkv_ref, dbf_ref):
        @pl.when(pl.program_id(0) == 0)
        def _():
            dwq_ref[...] = jnp.zeros_like(dwq_ref)
            dwkv_ref[...] = jnp.zeros_like(dwkv_ref)
            dgq_ref[...] = jnp.zeros_like(dgq_ref)
            dgkv_ref[...] = jnp.zeros_like(dgkv_ref)
            dbf_ref[...] = jnp.zeros_like(dbf_ref)

        pos_v = pos_ref[...]
        def put(off, val):
            dp_ref[:, off:off + GROUP] = _gather_heads(val).astype(BF16)

        for off, (dq, dk, dv) in ((OFF_FOX, (dfq, dfk, dfv)), (OFF_SB, (dsq, dsk, dsv))):
            put(off, dq[...] * 0.125)
            put(off + GROUP, dk[...])
            put(off + 2 * GROUP, dv[...])
        trig = _rope_trig(pos_v, invf_ref[...])
        cos, sin, first, second = _rope_tables(trig, HP, 0, HEAD_DIM // 2)
        nloc = (lax.broadcasted_iota(jnp.int32, (TM, 1), 0) & (BQ - 1)).astype(F32)
        dec = lg_ref[...] * nloc
        dq = _rope_apply(drq[...] * jnp.exp(dec), cos, sin, first, second, HEAD_DIM // 2, -1.0)
        dk = _rope_apply(drk[...] * (0.125 * jnp.exp(-dec)), cos, sin, first, second, HEAD_DIM // 2, -1.0)
        put(OFF_RET, dq)
        put(OFF_RET + GROUP, dk)
        put(OFF_RET + 2 * GROUP, drv[...])
        put(OFF_RG, drg_ref[...])
        cosm, sinm, firstm, secondm = _rope_tables(trig, HP, HEAD_DIM, ROPE_MLA // 2)
        dql = _rope_apply(dmq[...] * MLA_SCALE, cosm, sinm, firstm, secondm, ROPE_MLA // 2, -1.0).astype(BF16)
        cq = cq_ref[...]
        cqn = _rms(cq, gq_ref[...]).astype(BF16)
        dwq_ref[...] += _bdot(cqn, dql, TN)
        dcqn = _bdot(dql, wq_ref[...], NT)
        dcq, dgq = _rms_bwd_vals(dcqn, cq, gq_ref[...])
        dgq_ref[...] += dgq
        dp_ref[:, OFF_CQ:OFF_CQ + Q_RANK] = dcq.astype(BF16)
        dkm = dmk[...]
        dkv = jnp.concatenate([dkm, dmv[...]], axis=1).astype(BF16)
        ckv = ckv_ref[...]
        ckvn = _rms(ckv, gkv_ref[...]).astype(BF16)
        dwkv_ref[...] += _bdot(ckvn, dkv, TN)
        dckvn = _bdot(dkv, wkv_ref[...], NT)
        dckv, dgkv = _rms_bwd_vals(dckvn, ckv, gkv_ref[...])
        dgkv_ref[...] += dgkv
        dp_ref[:, OFF_CKV:OFF_CKV + LANES] = dckv.astype(BF16)
        dkr = dkm[:, 0:LANES] + dkm[:, LANES:2 * LANES] + dkm[:, 2 * LANES:3 * LANES] + dkm[:, 3 * LANES:HP]
        act = firstm[:, 0:LANES] | secondm[:, 0:LANES]
        dkr = jnp.where(act, dkr, 0.0)
        dkr = _rope_apply(dkr, cosm[:, 0:LANES], sinm[:, 0:LANES], firstm[:, 0:LANES], secondm[:, 0:LANES],
                          ROPE_MLA // 2, -1.0)
        dp_ref[:, OFF_KR:OFF_KR + LANES] = dkr.astype(BF16)
        f = ff_ref[...] + b_ref[...]
        dff = dlsf_ref[...] / (1.0 + jnp.exp(f))
        dbf_ref[...] += jnp.sum(dff, axis=0, keepdims=True)
        dp_ref[:, OFF_FF:OFF_FF + LANES] = dff.astype(BF16)
        dp_ref[:, OFF_FF + LANES:NP_IN] = jnp.zeros((TM, NP_IN - OFF_FF - LANES), BF16)

    def seg(off, w):
        return pl.BlockSpec((TM, w), lambda i, o=off // w: (i, o))

    def full(shape):
        return pl.BlockSpec(shape, lambda i: (0,) * len(shape))

    hp_spec = _row_spec(HP)
    in_specs = [hp_spec] * 13 + [_row_spec(LANES), seg(OFF_CQ, Q_RANK), seg(OFF_CKV, LANES), seg(OFF_FF, LANES),
                                 pl.BlockSpec((TM, 1), lambda i: (i, 0)),
                                 full((1, LANES)), full((1, HP)), full((1, LANES)), full((1, Q_RANK)),
                                 full((1, KV_RANK)), full((Q_RANK, HP)), full((KV_RANK, 2 * HP))]
    out_specs = [_row_spec(NP_IN), full((Q_RANK, HP)), full((KV_RANK, 2 * HP)), full((1, Q_RANK)),
                 full((1, KV_RANK)), full((1, LANES))]
    out_shape = [jax.ShapeDtypeStruct((t, NP_IN), BF16), jax.ShapeDtypeStruct((Q_RANK, HP), F32),
                 jax.ShapeDtypeStruct((KV_RANK, 2 * HP), F32), jax.ShapeDtypeStruct((1, Q_RANK), F32),
                 jax.ShapeDtypeStruct((1, KV_RANK), F32), jax.ShapeDtypeStruct((1, LANES), F32)]
    return pl.pallas_call(
        body, name=name, grid=(t // TM,), in_specs=in_specs, out_specs=out_specs, out_shape=out_shape,
        compiler_params=_cp("arbitrary"))(*dfox, *dret, *dsb, *dmla, drg, dlsf, proj, proj, proj, pos, invf,
                                          lg_lanes, b_pad, g_q, g_kv, wq_pad, wkv_pad)


TC = 1024


def _cumsum(x, reverse, name, partials=None, forget_bias=None):
    t = x.shape[0]
    w = LANES
    tc = min(TC, t)
    n = t // tc
    xs = [x] + ([] if partials is None else [partials]) + ([] if forget_bias is None else [forget_bias])

    def body(*refs):
        x_refs, o_ref, carry = refs[:len(xs)], refs[len(xs)], refs[len(xs) + 1]

        @pl.when(pl.program_id(0) == 0)
        def _():
            carry[...] = jnp.zeros_like(carry)

        r = lax.broadcasted_iota(jnp.int32, (tc, tc), 0)
        c = lax.broadcasted_iota(jnp.int32, (tc, tc), 1)
        tri = jnp.where((r <= c) if reverse else (r >= c), 1.0, 0.0).astype(BF16)
        v = x_refs[0][...]
        if forget_bias is not None:
            f = v + x_refs[-1][...]
            v = -(jnp.maximum(-f, 0.0) + jnp.log(1.0 + jnp.exp(-jnp.abs(f))))
        if partials is not None:
            lane = lax.broadcasted_iota(jnp.int32, (tc, LANES), 1)
            for hb in range(HEADS):
                v = v + jnp.where(lane == hb, jnp.sum(x_refs[1][:, _hs(hb)], axis=1, keepdims=True), 0.0)
        hi = v.astype(BF16)
        r1 = v - hi.astype(F32)
        mid = r1.astype(BF16)
        lo = (r1 - mid.astype(F32)).astype(BF16)
        cs = _bdot(tri, hi) + _bdot(tri, mid) + _bdot(tri, lo) + carry[...]
        o_ref[...] = cs
        carry[...] = cs[0:1, :] if reverse else cs[tc - 1:tc, :]

    step = (lambda i: n - 1 - i) if reverse else (lambda i: i)
    x_col = OFF_FF // LANES if forget_bias is not None else 0
    in_specs = [pl.BlockSpec((tc, LANES), lambda i: (step(i), x_col))]
    if partials is not None:
        in_specs.append(pl.BlockSpec((tc, HP), lambda i: (step(i), 0)))
    if forget_bias is not None:
        in_specs.append(_vec_spec(LANES))
    return pl.pallas_call(
        body, name=name, grid=(n,), in_specs=in_specs, out_specs=pl.BlockSpec((tc, w), lambda i: (step(i), 0)),
        out_shape=jax.ShapeDtypeStruct((t, w), F32), scratch_shapes=[pltpu.VMEM((1, w), F32)],
        compiler_params=_cp("arbitrary"))(*xs)


HB_FWD = 4
HB_BWD = 4
HB_SB_FWD = 4
BQS = 512
BQA = 512


def _q_spec(hb, bq=BQ):
    return pl.BlockSpec((bq, hb * LANES), lambda g, i: (i, g))


ONE_BUFFER = pl.Buffered(1)


def _kv_spec(t, which, hb):
    return pl.BlockSpec((t, hb * LANES), lambda g, i, w=which: (0, w * (HEADS // hb) + g), pipeline_mode=ONE_BUFFER)


def _acc_spec(t, hb):
    return pl.BlockSpec((t, hb * LANES), lambda g, i: (0, g), pipeline_mode=ONE_BUFFER)


def _hs(hh):
    return slice(hh * LANES, (hh + 1) * LANES)


def _tile_iota(rows, cols):
    return (lax.broadcasted_iota(jnp.int32, (rows, cols), 0), lax.broadcasted_iota(jnp.int32, (rows, cols), 1))


def _kvt_spec(nkv, width, which, hb):
    return pl.BlockSpec((nkv, hb * LANES, width), lambda g, i, w=which: (0, w * (HEADS // hb) + g, 0),
                        pipeline_mode=ONE_BUFFER)


def _qrow_spec(hb, bq=BQ):
    return pl.BlockSpec((hb, 1, 1, bq), lambda g, i: (g, i, 0, 0))


def _vis(key0, query0, rows, cols, kind):
    r, c = _tile_iota(rows, cols)
    k, q = key0 + r, query0 + c
    if kind == "chunk":
        return (k >> CHUNK_SHIFT) <= (q >> CHUNK_SHIFT)
    return (k < q) if kind == "strict" else (k <= q)


def _softmax_fwd(qkv, kvt, *, chunk_mask, scale, name):
    t = qkv.shape[0]
    nq = t // BQA
    per = BQA // BQ
    hb = HB_FWD
    kind = "chunk" if chunk_mask else "causal"

    def body(q_ref, k_ref, vt_ref, o_ref, lse_ref, m_sc, l_sc, acc_sc):
        i = pl.program_id(1)
        m_sc[...] = jnp.full((hb, 1, BQA), NEG, F32)
        l_sc[...] = jnp.zeros((hb, 1, BQA), F32)
        acc_sc[...] = jnp.zeros((hb, LANES, BQA), F32)

        def tile(j, qoff):
            off = pl.multiple_of(j * BQ, BQ)
            lo = 0 if qoff is None else qoff
            qs = slice(lo, BQA)
            vis = None if qoff is None else _vis(off, i * BQA + lo, BQ, BQA - lo, kind)
            ss = [_bdot(k_ref[pl.ds(off, BQ), _hs(hh)], q_ref[qs, _hs(hh)], NT) for hh in range(hb)]
            for hh in range(hb):
                s = ss[hh]
                if scale != 1.0:
                    s = s * scale
                if vis is not None:
                    s = jnp.where(vis, s, NEG)
                m_old = m_sc[hh, :, qs]
                m_new = jnp.maximum(m_old, jnp.max(s, axis=0, keepdims=True))
                alpha = jnp.exp(m_old - m_new)
                p = jnp.exp(s - m_new)
                l_sc[hh, :, qs] = alpha * l_sc[hh, :, qs] + jnp.sum(p, axis=0, keepdims=True)
                m_sc[hh, :, qs] = m_new
                acc_sc[hh, :, qs] = alpha * acc_sc[hh, :, qs] + _bdot(vt_ref[j, _hs(hh), :], p.astype(BF16))

        def loop(j, carry):
            tile(j, None)
            return carry

        lax.fori_loop(0, per * i, loop, 0)
        for d in range(per):
            tile(per * i + d, d * BQ)
        for hh in range(hb):
            l = l_sc[hh]
            o_ref[:, _hs(hh)] = (acc_sc[hh] / l).T
            lse_ref[hh, 0] = m_sc[hh] + jnp.log(l)

    return pl.pallas_call(
        body, name=name, grid=(HEADS // hb, nq),
        in_specs=[_q_spec(hb, BQA), _kv_spec(t, 1, hb), _kvt_spec(t // BQ, BQ, 1, hb)],
        out_specs=[_q_spec(hb, BQA), _qrow_spec(hb, BQA)],
        out_shape=[jax.ShapeDtypeStruct((t, HP), F32), jax.ShapeDtypeStruct((HEADS, nq, 1, BQA), F32)],
        scratch_shapes=[pltpu.VMEM((hb, 1, BQA), F32), pltpu.VMEM((hb, 1, BQA), F32),
                        pltpu.VMEM((hb, LANES, BQA), F32)],
        compiler_params=_cp("parallel", "arbitrary"))(qkv, qkv, kvt)


def _softmax_bwd(qkv, kvt, do, lse, delta, *, bias, chunk_mask, scale, name):
    t = qkv.shape[0]
    nq = t // BQA
    per = BQA // BQ
    hb = HB_BWD
    kind = "chunk" if chunk_mask else "causal"

    def body(*refs):
        if bias:
            (q_ref, k_ref, v_ref, kt_ref, do_ref, lse_ref, dl_ref, dq_ref, dk_ref, dv_ref, dck_ref, dcq_ref,
             dq_sc, dcq_sc) = refs
            dcq_sc[...] = jnp.zeros((hb, 1, BQA), F32)
        else:
            q_ref, k_ref, v_ref, kt_ref, do_ref, lse_ref, dl_ref, dq_ref, dk_ref, dv_ref, dq_sc = refs
        i = pl.program_id(1)

        @pl.when(i == 0)
        def _():
            dk_ref[...] = jnp.zeros_like(dk_ref)
            dv_ref[...] = jnp.zeros_like(dv_ref)
            if bias:
                dck_ref[...] = jnp.zeros_like(dck_ref)

        dq_sc[...] = jnp.zeros((hb, LANES, BQA), F32)

        def tile(j, qoff):
            off = pl.multiple_of(j * BQ, BQ)
            lo = 0 if qoff is None else qoff
            qsl = slice(lo, BQA)
            vis = None if qoff is None else _vis(off, i * BQA + lo, BQ, BQA - lo, kind)
            qs = [q_ref[qsl, _hs(hh)] for hh in range(hb)]
            dobs = [do_ref[qsl, _hs(hh)].astype(BF16) for hh in range(hb)]
            ss = [_bdot(k_ref[pl.ds(off, BQ), _hs(hh)], qs[hh], NT) for hh in range(hb)]
            dps = [_bdot(v_ref[pl.ds(off, BQ), _hs(hh)], dobs[hh], NT) for hh in range(hb)]
            pbs, dsbs = [], []
            for hh in range(hb):
                s = ss[hh]
                if scale != 1.0:
                    s = s * scale
                p = jnp.exp(s - lse_ref[hh, 0, :, qsl])
                if vis is not None:
                    p = jnp.where(vis, p, 0.0)
                ds = p * (dps[hh] - dl_ref[hh, 0, :, qsl])
                if bias:
                    part = ds[:, 0:LANES]
                    for b in range(1, (BQA - lo) // LANES):
                        part = part + ds[:, b * LANES:(b + 1) * LANES]
                    dck_ref[pl.ds(off, BQ), _hs(hh)] -= part
                    dcq_sc[hh, :, qsl] += jnp.sum(ds, axis=0, keepdims=True)
                if scale != 1.0:
                    ds = ds * scale
                pbs.append(p.astype(BF16))
                dsbs.append(ds.astype(BF16))
            for hh in range(hb):
                sl = _hs(hh)
                dv_ref[pl.ds(off, BQ), sl] += _bdot(pbs[hh], dobs[hh])
                dk_ref[pl.ds(off, BQ), sl] += _bdot(dsbs[hh], qs[hh])
                dq_sc[hh, :, qsl] += _bdot(kt_ref[j, sl, :], dsbs[hh])

        def loop(j, carry):
            tile(j, None)
            return carry

        lax.fori_loop(0, per * i, loop, 0)
        for d in range(per):
            tile(per * i + d, d * BQ)
        for hh in range(hb):
            dq_ref[:, _hs(hh)] = dq_sc[hh].T
            if bias:
                dcq_ref[hh, 0] = dcq_sc[hh]

    in_specs = [_q_spec(hb, BQA), _kv_spec(t, 1, hb), _kv_spec(t, 2, hb), _kvt_spec(t // BQ, BQ, 0, hb),
                _q_spec(hb, BQA), _qrow_spec(hb, BQA), _qrow_spec(hb, BQA)]
    out_specs = [_q_spec(hb, BQA), _acc_spec(t, hb), _acc_spec(t, hb)]
    out_shape = [jax.ShapeDtypeStruct((t, HP), F32)] * 3
    scratch = [pltpu.VMEM((hb, LANES, BQA), F32)]
    if bias:
        out_specs += [_acc_spec(t, hb), _qrow_spec(hb, BQA)]
        out_shape += [jax.ShapeDtypeStruct((t, HP), F32), jax.ShapeDtypeStruct((HEADS, nq, 1, BQA), F32)]
        scratch.append(pltpu.VMEM((hb, 1, BQA), F32))
    return pl.pallas_call(
        body, name=name, grid=(HEADS // hb, nq), in_specs=in_specs, out_specs=out_specs, out_shape=out_shape,
        scratch_shapes=scratch,
        compiler_params=_cp("parallel", "arbitrary"))(qkv, qkv, qkv, kvt, do, lse, delta)


def _ret_diag_decay(lg1, keys_on_rows=False):
    r, c = _tile_iota(BQ, BQ)
    qn, km = (c, r) if keys_on_rows else (r, c)
    dd = jnp.where(km > qn, jnp.exp((2.0 * lg1) * (km - qn).astype(F32)), 1.0)
    return jnp.where((km >> CHUNK_SHIFT) <= (qn >> CHUNK_SHIFT), dd, 0.0)


def _lg_spec(hb):
    return pl.BlockSpec((hb, 1, LANES), lambda g, i: (g, 0, 0))


def _ret_specs(nq, hb, reverse):
    tile = (lambda i: nq - 1 - i) if reverse else (lambda i: i)
    qkv = [pl.BlockSpec((BQ, hb * LANES), lambda g, i, w=w: (tile(i), w * (HEADS // hb) + g)) for w in range(3)]
    kt = pl.BlockSpec((1, hb * LANES, BQ), lambda g, i: (tile(i), g, 0))
    st = pl.BlockSpec((1, hb * LANES, LANES), lambda g, i: (tile(i), g, 0))
    return qkv, kt, st


def _ret_fwd(qkv, kt, lg_heads, name):
    t = qkv.shape[0]
    nq = t // BQ
    hb = HB_FWD

    def body(lg_ref, q_ref, k_ref, v_ref, kt_ref, o_ref, st_ref, s_sc):
        @pl.when(pl.program_id(1) == 0)
        def _():
            s_sc[...] = jnp.zeros_like(s_sc)

        qs = [q_ref[:, _hs(hh)] for hh in range(hb)]
        vs = [v_ref[:, _hs(hh)] for hh in range(hb)]
        aa = [_bdot(qs[hh], k_ref[:, _hs(hh)], NT) for hh in range(hb)]
        kv = [_bdot(kt_ref[0, _hs(hh), :], vs[hh]) for hh in range(hb)]
        for hh in range(hb):
            sl = _hs(hh)
            lg1 = lg_ref[hh][:, 0:1]
            s = s_sc[hh]
            st_ref[0, sl, :] = s
            shi, slo = _split2(s)
            a = (aa[hh] * _ret_diag_decay(lg1)).astype(BF16)
            o_ref[:, sl] = _bdot(a, vs[hh]) + _bdot(qs[hh], shi) + _bdot(qs[hh], slo)
            s_sc[hh] = jnp.exp(lg1 * float(BQ)) * (s + kv[hh])

    qkv_specs, kt_spec, st_spec = _ret_specs(nq, hb, False)
    return pl.pallas_call(
        body, name=name, grid=(HEADS // hb, nq), in_specs=[_lg_spec(hb)] + qkv_specs + [kt_spec],
        out_specs=[_q_spec(hb), st_spec],
        out_shape=[jax.ShapeDtypeStruct((t, HP), F32), jax.ShapeDtypeStruct((nq, HP, LANES), F32)],
        scratch_shapes=[pltpu.VMEM((hb, LANES, LANES), F32)],
        compiler_params=_cp("parallel", "arbitrary"))(lg_heads, qkv, qkv, qkv, kt)


def _ret_bwd(qkv, states, lg_heads, do, name):
    t = qkv.shape[0]
    nq = t // BQ
    hb = HB_FWD

    def body(lg_ref, q_ref, k_ref, v_ref, st_ref, do_ref, dq_ref, dk_ref, dv_ref, g_sc):
        @pl.when(pl.program_id(1) == 0)
        def _():
            g_sc[...] = jnp.zeros_like(g_sc)

        qs = [q_ref[:, _hs(hh)] for hh in range(hb)]
        ks = [k_ref[:, _hs(hh)] for hh in range(hb)]
        vs = [v_ref[:, _hs(hh)] for hh in range(hb)]
        dobs = [do_ref[:, _hs(hh)].astype(BF16) for hh in range(hb)]
        aa = [_bdot(ks[hh], qs[hh], NT) for hh in range(hb)]
        das = [_bdot(vs[hh], dobs[hh], NT) for hh in range(hb)]
        qdo = [_bdot(qs[hh], dobs[hh], TN) for hh in range(hb)]
        for hh in range(hb):
            sl = _hs(hh)
            lg1 = lg_ref[hh][:, 0:1]
            dd = _ret_diag_decay(lg1, keys_on_rows=True)
            at = (aa[hh] * dd).astype(BF16)
            dat = (das[hh] * dd).astype(BF16)
            h = jnp.exp(lg1 * float(BQ)) * g_sc[hh]
            hhi, hlo = _split2(h)
            shi, slo = _split2(st_ref[0, sl, :])
            dv_ref[:, sl] = _bdot(at, dobs[hh]) + _bdot(ks[hh], hhi) + _bdot(ks[hh], hlo)
            dk_ref[:, sl] = _bdot(dat, qs[hh]) + _bdot(vs[hh], hhi, NT) + _bdot(vs[hh], hlo, NT)
            dq_ref[:, sl] = _bdot(dat, ks[hh], TN) + _bdot(dobs[hh], shi, NT) + _bdot(dobs[hh], slo, NT)
            g_sc[hh] = qdo[hh] + h

    qkv_specs, _, st_spec = _ret_specs(nq, hb, True)
    tile_spec = pl.BlockSpec((BQ, hb * LANES), lambda g, i: (nq - 1 - i, g))
    return pl.pallas_call(
        body, name=name, grid=(HEADS // hb, nq), in_specs=[_lg_spec(hb)] + qkv_specs + [st_spec, tile_spec],
        out_specs=[tile_spec] * 3, out_shape=[jax.ShapeDtypeStruct((t, HP), F32)] * 3,
        scratch_shapes=[pltpu.VMEM((hb, LANES, LANES), F32)],
        compiler_params=_cp("parallel", "arbitrary"))(lg_heads, qkv, qkv, qkv, states, do)


def _sb_tile_logs(q, kb, vis):
    z = _bdot(kb, q, NT)
    ls = -(jnp.maximum(z, 0.0) + jnp.log(1.0 + jnp.exp(-jnp.abs(z))))
    if vis is not None:
        ls = jnp.where(vis, ls, 0.0)
    return z, ls


def _sb_later(ls, after):
    hi, lo = _split2(ls)
    return _bdot(after, hi) + _bdot(after, lo)


def _sb_fwd(qkv, kvt, name):
    t = qkv.shape[0]
    nq = t // BQS
    per = BQS // TKS
    hb = HB_SB_FWD

    def body(q_ref, k_ref, vt_ref, o_ref, tot_ref, acc_sc, r_sc):
        i = pl.program_id(1)
        acc_sc[...] = jnp.zeros((hb, LANES, BQS), F32)
        r_sc[...] = jnp.zeros((hb, 1, BQS), F32)
        mr, mc = _tile_iota(TKS, TKS)
        after = jnp.where(mc > mr, 1.0, 0.0).astype(BF16)

        def tile(j, qoff):
            off = pl.multiple_of(j * TKS, TKS)
            lo = 0 if qoff is None else qoff
            qsl = slice(lo, BQS)
            vis = None if qoff is None else _vis(off, i * BQS + lo, TKS, BQS - lo, "strict")
            zl = [_sb_tile_logs(q_ref[qsl, _hs(hh)], k_ref[pl.ds(off, TKS), _hs(hh)], vis) for hh in range(hb)]
            laters = [_sb_later(zl[hh][1], after) for hh in range(hb)]
            ws = []
            for hh in range(hb):
                z, ls = zl[hh]
                w = jnp.exp(z + ls + laters[hh] + r_sc[hh, :, qsl])
                if vis is not None:
                    w = jnp.where(vis, w, 0.0)
                ws.append(w.astype(BF16))
                r_sc[hh, :, qsl] += jnp.sum(ls, axis=0, keepdims=True)
            for hh in range(hb):
                acc_sc[hh, :, qsl] += _bdot(vt_ref[j, _hs(hh), :], ws[hh])

        for d in reversed(range(per)):
            tile(per * i + d, d * TKS)

        def loop(jj, carry):
            tile(per * i - 1 - jj, None)
            return carry

        lax.fori_loop(0, per * i, loop, 0)
        for hh in range(hb):
            o_ref[:, _hs(hh)] = acc_sc[hh].T
            tot_ref[hh, 0] = r_sc[hh]

    return pl.pallas_call(
        body, name=name, grid=(HEADS // hb, nq),
        in_specs=[_q_spec(hb, BQS), _kv_spec(t, 1, hb), _kvt_spec(t // TKS, TKS, 1, hb)],
        out_specs=[_q_spec(hb, BQS), _qrow_spec(hb, BQS)],
        out_shape=[jax.ShapeDtypeStruct((t, HP), F32), jax.ShapeDtypeStruct((HEADS, nq, 1, BQS), F32)],
        scratch_shapes=[pltpu.VMEM((hb, LANES, BQS), F32), pltpu.VMEM((hb, 1, BQS), F32)],
        compiler_params=_cp("parallel", "arbitrary"))(qkv, qkv, kvt)


def _sb_bwd(qkv, kvt, do, tot, name):
    t = qkv.shape[0]
    nq = t // BQS
    per = BQS // TKS
    hb = HB_BWD

    def body(q_ref, k_ref, v_ref, kt_ref, do_ref, tot_ref, dq_ref, dk_ref, dv_ref, dq_sc, p_sc, g_sc):
        i = pl.program_id(1)

        @pl.when(i == 0)
        def _():
            dk_ref[...] = jnp.zeros_like(dk_ref)
            dv_ref[...] = jnp.zeros_like(dv_ref)

        dq_sc[...] = jnp.zeros((hb, LANES, BQS), F32)
        p_sc[...] = jnp.zeros((hb, 1, BQS), F32)
        g_sc[...] = jnp.zeros((hb, 1, BQS), F32)
        mr, mc = _tile_iota(TKS, TKS)
        after = jnp.where(mc > mr, 1.0, 0.0).astype(BF16)
        before = jnp.where(mc < mr, 1.0, 0.0).astype(BF16)

        def tile(j, qoff):
            off = pl.multiple_of(j * TKS, TKS)
            lo = 0 if qoff is None else qoff
            qsl = slice(lo, BQS)
            vis = None if qoff is None else _vis(off, i * BQS + lo, TKS, BQS - lo, "strict")
            qs = [q_ref[qsl, _hs(hh)] for hh in range(hb)]
            dobs = [do_ref[qsl, _hs(hh)].astype(BF16) for hh in range(hb)]
            zl = [_sb_tile_logs(qs[hh], k_ref[pl.ds(off, TKS), _hs(hh)], vis) for hh in range(hb)]
            dws = [_bdot(v_ref[pl.ds(off, TKS), _hs(hh)], dobs[hh], NT) for hh in range(hb)]
            laters = [_sb_later(zl[hh][1], after) for hh in range(hb)]
            ws, gs = [], []
            for hh in range(hb):
                z, ls = zl[hh]
                own = jnp.sum(ls, axis=0, keepdims=True)
                rest = tot_ref[hh, 0, :, qsl] - p_sc[hh, :, qsl] - own
                w = jnp.exp(z + ls + laters[hh] + rest)
                if vis is not None:
                    w = jnp.where(vis, w, 0.0)
                p_sc[hh, :, qsl] += own
                ws.append(w.astype(BF16))
                gs.append(dws[hh] * w)
            gins = [_bdot(before, gs[hh].astype(BF16)) for hh in range(hb)]
            dzbs = []
            for hh in range(hb):
                g = gs[hh]
                stay = jnp.exp(zl[hh][1])
                dz = g * stay - (1.0 - stay) * (gins[hh] + g_sc[hh, :, qsl])
                if vis is not None:
                    dz = jnp.where(vis, dz, 0.0)
                g_sc[hh, :, qsl] += jnp.sum(g, axis=0, keepdims=True)
                dzbs.append(dz.astype(BF16))
            for hh in range(hb):
                sl = _hs(hh)
                dv_ref[pl.ds(off, TKS), sl] += _bdot(ws[hh], dobs[hh])
                dk_ref[pl.ds(off, TKS), sl] += _bdot(dzbs[hh], qs[hh])
                dq_sc[hh, :, qsl] += _bdot(kt_ref[j, sl, :], dzbs[hh])

        def loop(j, carry):
            tile(j, None)
            return carry

        lax.fori_loop(0, per * i, loop, 0)
        for d in range(per):
            tile(per * i + d, d * TKS)
        for hh in range(hb):
            dq_ref[:, _hs(hh)] = dq_sc[hh].T

    return pl.pallas_call(
        body, name=name, grid=(HEADS // hb, nq),
        in_specs=[_q_spec(hb, BQS), _kv_spec(t, 1, hb), _kv_spec(t, 2, hb), _kvt_spec(t // TKS, TKS, 0, hb),
                  _q_spec(hb, BQS), _qrow_spec(hb, BQS)],
        out_specs=[_q_spec(hb, BQS), _acc_spec(t, hb), _acc_spec(t, hb)],
        out_shape=[jax.ShapeDtypeStruct((t, HP), F32)] * 3,
        scratch_shapes=[pltpu.VMEM((hb, LANES, BQS), F32), pltpu.VMEM((hb, 1, BQS), F32),
                        pltpu.VMEM((hb, 1, BQS), F32)],
        compiler_params=_cp("parallel", "arbitrary"))(qkv, qkv, qkv, kvt, do, tot)


def _sigmoid(v):
    return 1.0 / (1.0 + jnp.exp(-v))


def _post_fwd(oa, ob, oc, od, proj, g_pad, name):
    t = oa.shape[0]

    def body(oa_ref, ob_ref, oc_ref, od_ref, rg_ref, g_ref, mx_ref):
        g = g_ref[...]

        def group(o, gg):
            r = lax.rsqrt(jnp.sum(o * o, axis=-1, keepdims=True) * (1.0 / GROUP) + EPS)
            return _gather_heads(o * r * gg).astype(BF16)

        mx_ref[:, 0:GROUP] = group(oa_ref[...], g[:, 0:HP])
        mx_ref[:, GROUP:2 * GROUP] = group(ob_ref[...], g[:, HP:2 * HP])
        mx_ref[:, 3 * GROUP:4 * GROUP] = group(od_ref[...], g[:, 3 * HP:4 * HP])
        real = lax.broadcasted_iota(jnp.int32, (TM, LANES), 1) < HEAD_DIM
        rg = _spread_heads(rg_ref[...])
        gated = []
        for hb in range(HEADS):
            sl = slice(hb * LANES, (hb + 1) * LANES)
            o = oc_ref[:, sl]
            mu = jnp.sum(o, axis=-1, keepdims=True) * (1.0 / HEAD_DIM)
            dlt = jnp.where(real, o - mu, 0.0)
            var = jnp.sum(dlt * dlt, axis=-1, keepdims=True) * (1.0 / HEAD_DIM)
            yn = dlt * lax.rsqrt(var + EPS) * g[:, 2 * HP + hb * LANES:2 * HP + (hb + 1) * LANES]
            x = rg[:, sl]
            gated.append(yn * (x * _sigmoid(x)))
        mx_ref[:, 2 * GROUP:3 * GROUP] = _gather_heads(jnp.concatenate(gated, axis=1)).astype(BF16)

    rg_spec = pl.BlockSpec((TM, GROUP), lambda i: (i, OFF_RG // GROUP))
    return pl.pallas_call(
        body, name=name, grid=(t // TM,),
        in_specs=[_row_spec(HP)] * 4 + [rg_spec, _vec_spec(4 * HP)], out_specs=_row_spec(D_MODEL),
        out_shape=jax.ShapeDtypeStruct((t, D_MODEL), BF16), compiler_params=_cp("parallel"))(oa, ob, oc, od, proj, g_pad)


def _post_bwd(dmx, oa, ob, oc, od, proj, g_pad, name):
    t = oa.shape[0]

    def body(dm_ref, oa_ref, ob_ref, oc_ref, od_ref, rg_ref, g_ref,
             doa_ref, dob_ref, doc_ref, dod_ref, dla_ref, dlb_ref, drg_ref, dg_ref):
        @pl.when(pl.program_id(0) == 0)
        def _():
            dg_ref[...] = jnp.zeros_like(dg_ref)

        g = g_ref[...]

        def group_bwd(dm, o, gg):
            r = lax.rsqrt(jnp.sum(o * o, axis=-1, keepdims=True) * (1.0 / GROUP) + EPS)
            oh = o * r
            dgp = jnp.sum(dm * oh, axis=0, keepdims=True)
            dyh = dm * gg
            do = r * (dyh - oh * (jnp.sum(dyh * oh, axis=-1, keepdims=True) * (1.0 / GROUP)))
            return do, dgp

        def delta_bc(do, o):
            prod = do * o
            lane = lax.broadcasted_iota(jnp.int32, (TM, LANES), 1)
            out = jnp.zeros((TM, LANES), F32)
            for hb in range(HEADS):
                out = jnp.where(lane == hb, jnp.sum(prod[:, hb * LANES:(hb + 1) * LANES], axis=-1, keepdims=True), out)
            return out

        dmp = [_spread_heads(dm_ref[:, gi * GROUP:(gi + 1) * GROUP]) for gi in range(4)]
        rg = _spread_heads(rg_ref[...])
        oa = oa_ref[...]
        do_a, dga = group_bwd(dmp[0], oa, g[:, 0:HP])
        doa_ref[...] = do_a.astype(BF16)
        dla_ref[...] = delta_bc(do_a, oa)
        dg_ref[:, 0:HP] += dga
        ob = ob_ref[...]
        do_b, dgb = group_bwd(dmp[1], ob, g[:, HP:2 * HP])
        dob_ref[...] = do_b.astype(BF16)
        dlb_ref[...] = delta_bc(do_b, ob)
        dg_ref[:, HP:2 * HP] += dgb
        do_d, dgd = group_bwd(dmp[3], od_ref[...], g[:, 3 * HP:4 * HP])
        dod_ref[...] = do_d.astype(BF16)
        dg_ref[:, 3 * HP:4 * HP] += dgd
        real = lax.broadcasted_iota(jnp.int32, (TM, LANES), 1) < HEAD_DIM
        for hb in range(HEADS):
            sl = slice(hb * LANES, (hb + 1) * LANES)
            gsl = slice(2 * HP + hb * LANES, 2 * HP + (hb + 1) * LANES)
            o = oc_ref[:, sl]
            mu = jnp.sum(o, axis=-1, keepdims=True) * (1.0 / HEAD_DIM)
            dlt = jnp.where(real, o - mu, 0.0)
            var = jnp.sum(dlt * dlt, axis=-1, keepdims=True) * (1.0 / HEAD_DIM)
            rstd = lax.rsqrt(var + EPS)
            dhat = dlt * rstd
            gc = g[:, gsl]
            x = rg[:, sl]
            sg = _sigmoid(x)
            dm = dmp[2][:, sl]
            drg_ref[:, sl] = dm * (dhat * gc) * (sg * (1.0 + x * (1.0 - sg)))
            dyn = dm * (x * sg)
            dg_ref[:, gsl] += jnp.sum(dyn * dhat, axis=0, keepdims=True)
            ddh = dyn * gc
            m1 = jnp.sum(ddh, axis=-1, keepdims=True) * (1.0 / HEAD_DIM)
            m2 = jnp.sum(ddh * dhat, axis=-1, keepdims=True) * (1.0 / HEAD_DIM)
            doc_ref[:, sl] = jnp.where(real, rstd * (ddh - m1 - dhat * m2), 0.0).astype(BF16)

    rg_spec = pl.BlockSpec((TM, GROUP), lambda i: (i, OFF_RG // GROUP))
    hp = _row_spec(HP)
    return pl.pallas_call(
        body, name=name, grid=(t // TM,),
        in_specs=[_row_spec(D_MODEL), hp, hp, hp, hp, rg_spec, _vec_spec(4 * HP)],
        out_specs=[hp] * 4 + [_row_spec(LANES)] * 2 + [hp, _vec_spec(4 * HP)],
        out_shape=[jax.ShapeDtypeStruct((t, HP), BF16)] * 4 + [jax.ShapeDtypeStruct((t, LANES), F32)] * 2
        + [jax.ShapeDtypeStruct((t, HP), F32), jax.ShapeDtypeStruct((1, 4 * HP), F32)],
        compiler_params=_cp("arbitrary"))(dmx, oa, ob, oc, od, proj, g_pad)


def _mesh_pos():
    return lax.axis_index("x"), lax.axis_index("y"), lax.axis_index("c")


def _peer(pos, k):
    x, y, c = pos
    px = 1 - x if (k >> 2) & 1 else x
    py = 1 - y if (k >> 1) & 1 else y
    pc = 1 - c if k & 1 else c
    return (px, py, pc), 4 * px + 2 * py + pc


def _exchange(arrs, gather, name):
    n = len(arrs)

    def body(*refs):
        ins, outs = refs[:n], refs[n:2 * n]
        send_sems, recv_sems, loc_sems = refs[2 * n:]
        pos = _mesh_pos()
        me = 4 * pos[0] + 2 * pos[1] + pos[2]
        local = []
        for a in range(n):
            src = ins[a] if gather else ins[a].at[me]
            cp = pltpu.make_async_copy(src, outs[a].at[me], loc_sems.at[a])
            cp.start()
            local.append(cp)
        sends, recvs = [], []
        for k in range(1, N_DEV):
            peer, pid = _peer(pos, k)
            for a in range(n):
                s = a * (N_DEV - 1) + k - 1
                src = ins[a] if gather else ins[a].at[pid]
                cp = pltpu.make_async_remote_copy(
                    src_ref=src, dst_ref=outs[a].at[me], send_sem=send_sems.at[s], recv_sem=recv_sems.at[s],
                    device_id=peer, device_id_type=pl.DeviceIdType.MESH)
                cp.start()
                sends.append(cp)
                recvs.append(pltpu.make_async_remote_copy(
                    src_ref=src, dst_ref=outs[a].at[pid], send_sem=send_sems.at[s], recv_sem=recv_sems.at[s],
                    device_id=peer, device_id_type=pl.DeviceIdType.MESH))
        for cp in recvs:
            cp.wait_recv()
        for cp in sends:
            cp.wait_send()
        for cp in local:
            cp.wait()

    any_spec = pl.BlockSpec(memory_space=pl.ANY)
    out_shape = [jax.ShapeDtypeStruct((N_DEV,) + tuple(a.shape) if gather else tuple(a.shape), a.dtype) for a in arrs]
    return pl.pallas_call(
        body, name=name, in_specs=[any_spec] * n, out_specs=[any_spec] * n, out_shape=out_shape,
        scratch_shapes=[pltpu.SemaphoreType.DMA((n * (N_DEV - 1),)), pltpu.SemaphoreType.DMA((n * (N_DEV - 1),)),
                        pltpu.SemaphoreType.DMA((n,))],
        compiler_params=pltpu.CompilerParams(has_side_effects=True))(*arrs)


def _device_index():
    x, y, c = _mesh_pos()
    return 4 * x + 2 * y + c


def _landing(srcs, gather):
    me = _device_index()
    lands = []
    for a in srcs:
        own = a[None] if gather else lax.dynamic_slice_in_dim(a, me, 1, axis=0)
        shape = (N_DEV,) + tuple(a.shape) if gather else tuple(a.shape)
        lands.append(lax.dynamic_update_slice_in_dim(lax.empty(shape, a.dtype), own, me, axis=0))
    return lands


def _exchange_copies(ins, lands, send_sems, recv_sems, gather):
    pos = _mesh_pos()
    me = 4 * pos[0] + 2 * pos[1] + pos[2]
    sends, recvs = [], []
    for k in range(1, N_DEV):
        peer, pid = _peer(pos, k)
        for a in range(len(ins)):
            s = a * (N_DEV - 1) + k - 1
            src = ins[a] if gather else ins[a].at[pid]
            sends.append(pltpu.make_async_remote_copy(
                src_ref=src, dst_ref=lands[a].at[me], send_sem=send_sems.at[s], recv_sem=recv_sems.at[s],
                device_id=peer, device_id_type=pl.DeviceIdType.MESH))
            recvs.append(pltpu.make_async_remote_copy(
                src_ref=src, dst_ref=lands[a].at[pid], send_sem=send_sems.at[s], recv_sem=recv_sems.at[s],
                device_id=peer, device_id_type=pl.DeviceIdType.MESH))
    return sends, recvs


def _exchange_start(srcs, gather, name, after=None):
    n = len(srcs)
    lands = _landing(srcs, gather)
    nsem = n * (N_DEV - 1)
    extra = [] if after is None else [after]

    def body(*refs):
        ins, lnd = refs[:n], refs[n:2 * n]
        send_sems, recv_sems = refs[2 * n + len(extra)], refs[2 * n + len(extra) + 1]
        token = refs[-1]
        sends, _ = _exchange_copies(ins, lnd, send_sems, recv_sems, gather)
        for cp in sends:
            cp.start()
        token[...] = jnp.zeros_like(token)

    hbm = pl.BlockSpec(memory_space=pltpu.HBM)
    sem = pl.BlockSpec(memory_space=pltpu.SEMAPHORE)
    bufs = list(srcs) + lands
    out_shape = ([pltpu.SemaphoreType.DMA((nsem,)), pltpu.SemaphoreType.DMA((nsem,))]
                 + [pltpu.HBM(b.shape, b.dtype) for b in bufs] + [jax.ShapeDtypeStruct((8, LANES), F32)])
    outs = pl.pallas_call(
        body, name=name, in_specs=[hbm] * (2 * n) + [pl.BlockSpec(memory_space=pl.ANY)] * len(extra),
        out_specs=[sem, sem] + [hbm] * (2 * n) + [pl.BlockSpec(memory_space=pltpu.VMEM)], out_shape=out_shape,
        input_output_aliases={i: 2 + i for i in range(2 * n)},
        compiler_params=pltpu.CompilerParams(has_side_effects=pltpu.SideEffectType.DATAFLOW_SIDE_EFFECTING),
    )(*[pltpu.with_memory_space_constraint(b, pltpu.HBM) for b in bufs], *extra)
    return (outs[0], outs[1], outs[2:2 + n], outs[2 + n:2 + 2 * n]), outs[-1]


def _exchange_wait(state, after, gather, name):
    send_sems, recv_sems, srcs, lands = state
    n = len(srcs)
    after = list(after) if isinstance(after, (list, tuple)) else [after]

    def body(*refs):
        ins, lnd = refs[:n], refs[n:2 * n]
        s_sems, r_sems = refs[2 * n], refs[2 * n + 1]
        sends, recvs = _exchange_copies(ins, lnd, s_sems, r_sems, gather)
        for cp in sends:
            cp.wait_send()
        for cp in recvs:
            cp.wait_recv()

    hbm = pl.BlockSpec(memory_space=pltpu.HBM)
    sem = pl.BlockSpec(memory_space=pltpu.SEMAPHORE)
    bufs = list(srcs) + list(lands)
    outs = pl.pallas_call(
        body, name=name, in_specs=[hbm] * (2 * n) + [sem, sem] + [pl.BlockSpec(memory_space=pl.ANY)] * len(after),
        out_specs=[hbm] * (2 * n), out_shape=[pltpu.HBM(b.shape, b.dtype) for b in bufs],
        input_output_aliases={i: i for i in range(2 * n)},
        compiler_params=pltpu.CompilerParams(has_side_effects=pltpu.SideEffectType.DATAFLOW_SIDE_EFFECTING),
    )(*bufs, send_sems, recv_sems, *after)
    return outs[n:]


def _adam_vals(w, g, m, v):
    m = ADAM_B1 * m + (1.0 - ADAM_B1) * g
    v = ADAM_B2 * v + (1.0 - ADAM_B2) * (g * g)
    m_hat = m / ADAM_C1
    v_hat = v / ADAM_C2
    delta = -ADAM_LR * (m_hat / (jnp.sqrt(v_hat) + ADAM_EPS) + ADAM_WD * w)
    return delta, m, v


def _small_allreduce_adam(part, w, m, v, name):
    rows = part.shape[0]

    def body(p_ref, w_ref, m_ref, v_ref, g_ref, d_ref, nm_ref, nv_ref, gath, send_sems, recv_sems):
        pos = _mesh_pos()
        me = 4 * pos[0] + 2 * pos[1] + pos[2]
        gath[me] = p_ref[...]
        sends, recvs = [], []
        for k in range(1, N_DEV):
            peer, pid = _peer(pos, k)
            cp = pltpu.make_async_remote_copy(
                src_ref=p_ref, dst_ref=gath.at[me], send_sem=send_sems.at[k - 1], recv_sem=recv_sems.at[k - 1],
                device_id=peer, device_id_type=pl.DeviceIdType.MESH)
            cp.start()
            sends.append(cp)
            recvs.append(pltpu.make_async_remote_copy(
                src_ref=p_ref, dst_ref=gath.at[pid], send_sem=send_sems.at[k - 1], recv_sem=recv_sems.at[k - 1],
                device_id=peer, device_id_type=pl.DeviceIdType.MESH))
        for cp in recvs:
            cp.wait_recv()
        for cp in sends:
            cp.wait_send()
        g = gath[0]
        for p in range(1, N_DEV):
            g = g + gath[p]
        g_ref[...] = g
        d, nm, nv = _adam_vals(w_ref[...], g, m_ref[...], v_ref[...])
        d_ref[...] = d
        nm_ref[...] = nm
        nv_ref[...] = nv

    vm = pl.BlockSpec(memory_space=pltpu.VMEM)
    sds = jax.ShapeDtypeStruct((rows, LANES), F32)
    return pl.pallas_call(
        body, name=name, in_specs=[vm] * 4, out_specs=[vm] * 4, out_shape=[sds] * 4,
        scratch_shapes=[pltpu.VMEM((N_DEV, rows, LANES), F32), pltpu.SemaphoreType.DMA((N_DEV - 1,)),
                        pltpu.SemaphoreType.DMA((N_DEV - 1,))],
        compiler_params=pltpu.CompilerParams(has_side_effects=True))(part, w, m, v)


def _reduce_adam(recvs, w, m, v, name):
    shape = w.shape
    c = shape[-1]
    r = int(np.prod(shape[1:-1]))
    rs = [a.reshape(N_DEV, r, c) for a in recvs]
    w2, m2, v2 = w.reshape(DEPTH * r, c), m.reshape(DEPTH * r, c), v.reshape(DEPTH * r, c)
    tr = r
    while tr * c * 4 > (1 << 20) and tr % 16 == 0:
        tr //= 2
    nt = r // tr

    def body(*refs):
        r_refs = refs[:DEPTH]
        w_ref, m_ref, v_ref, g_ref, d_ref, nm_ref, nv_ref = refs[DEPTH:]
        for l, r_ref in enumerate(r_refs):
            @pl.when(pl.program_id(0) // nt == l)
            def _(r_ref=r_ref):
                g = r_ref[0].astype(F32)
                for p in range(1, N_DEV):
                    g = g + r_ref[p].astype(F32)
                g_ref[...] = g
                d, nm, nv = _adam_vals(w_ref[...], g, m_ref[...], v_ref[...])
                d_ref[...] = d
                nm_ref[...] = nm
                nv_ref[...] = nv

    recv_specs = [pl.BlockSpec((N_DEV, tr, c), lambda i, l=l: (0, jnp.clip(i - l * nt, 0, nt - 1), 0))
                  for l in range(DEPTH)]
    spec = pl.BlockSpec((tr, c), lambda i: (i, 0))
    sds = jax.ShapeDtypeStruct((DEPTH * r, c), F32)
    outs = pl.pallas_call(
        body, name=name, grid=(DEPTH * nt,), in_specs=recv_specs + [spec, spec, spec],
        out_specs=[spec] * 4, out_shape=[sds] * 4, compiler_params=_cp("arbitrary"))(*rs, w2, m2, v2)
    return [o.reshape(shape) for o in outs]


def _pad_heads(w, real=HEAD_DIM):
    lead = w.shape[:-1]
    w = w.reshape(lead + (HEADS, real))
    w = jnp.pad(w, [(0, 0)] * len(lead) + [(0, 0), (0, LANES - real)])
    return w.reshape(lead + (HP,))


def _unpad_heads(w, real=HEAD_DIM):
    lead = w.shape[:-1]
    return w.reshape(lead + (HEADS, LANES))[..., :real].reshape(lead + (HEADS * real,))


_IN_SEGS = (("fq", 0, 256), ("fk", 256, 512), ("fv", 512, 768), ("ff", 768, 772), ("cq", 772, 1028),
            ("ckv", 1028, 1156), ("kr", 1156, 1188), ("rq", 1188, 1444), ("rk", 1444, 1700), ("rv", 1700, 1956),
            ("rg", 1956, 2212), ("sq", 2212, 2468), ("sk", 2468, 2724), ("sv", 2724, 2980))


def _pad_w_in(w):
    s = {n: w[:, a:b] for n, a, b in _IN_SEGS}
    rows = w.shape[0]
    z = lambda n: jnp.zeros((rows, n), w.dtype)
    parts = [s[n] for n in ("fq", "fk", "fv", "rq", "rk", "rv", "sq", "sk", "sv", "rg", "cq", "ckv")]
    parts += [z(HEAD_DIM), s["kr"], z(LANES - HEAD_DIM - ROPE_MLA), s["ff"], z(LANES - HEADS),
              z(NP_IN - OFF_FF - LANES)]
    return jnp.concatenate(parts, axis=1)


def _unpad_w_in(wp):
    seg = lambda off, n=GROUP: wp[:, off:off + n]
    parts = [seg(OFF_FOX), seg(OFF_FOX + GROUP), seg(OFF_FOX + 2 * GROUP), seg(OFF_FF, HEADS),
             seg(OFF_CQ, Q_RANK), seg(OFF_CKV, KV_RANK), seg(OFF_KR + HEAD_DIM, ROPE_MLA),
             seg(OFF_RET), seg(OFF_RET + GROUP), seg(OFF_RET + 2 * GROUP), seg(OFF_RG),
             seg(OFF_SB), seg(OFF_SB + GROUP), seg(OFF_SB + 2 * GROUP)]
    return jnp.concatenate(parts, axis=1)


def _pad_w_kv(w):
    w4 = w.reshape(KV_RANK, HEADS, 2 * HEAD_DIM)
    k = w4[:, :, :HEAD_DIM].reshape(KV_RANK, GROUP)
    v = w4[:, :, HEAD_DIM:].reshape(KV_RANK, GROUP)
    return jnp.concatenate([_pad_heads(k), _pad_heads(v)], axis=1)


def _unpad_w_kv(wp):
    k = _unpad_heads(wp[:, :HP]).reshape(KV_RANK, HEADS, HEAD_DIM)
    v = _unpad_heads(wp[:, HP:]).reshape(KV_RANK, HEADS, HEAD_DIM)
    return jnp.concatenate([k, v], axis=-1).reshape(KV_RANK, HEADS * 2 * HEAD_DIM)


def _pad_gain_out(g):
    g = jnp.pad(g.reshape(4 * HEADS, HEAD_DIM), ((0, 0), (0, LANES - HEAD_DIM)))
    return g.reshape(1, 4 * HP)


def _unpad_gain_out(gp):
    return gp.reshape(4 * HEADS, LANES)[:, :HEAD_DIM].reshape(D_MODEL)


_SMALL = (("g_mix_pre", 1024), ("g_mix_post", 1024), ("g_ffn_pre", 1024), ("g_ffn_post", 1024), ("g_mix_out", 1024),
          ("g_q_lora", 256), ("g_kv_lora", 128), ("b_forget", 4))


def _pack_small(vals):
    parts = []
    for name, n in _SMALL:
        a = vals[name].astype(F32)
        if n < LANES:
            a = jnp.pad(a, ((0, 0), (0, LANES - n)))
        parts.append(a)
    return jnp.concatenate(parts, axis=1).reshape(DEPTH * SMALL_ROWS, LANES)


def _unpack_small(packed):
    flat = packed.reshape(DEPTH, SMALL_ROWS * LANES)
    out, off = {}, 0
    for name, n in _SMALL:
        out[name] = flat[:, off:off + n]
        off += max(n, LANES)
    return out


def kernel(x, positions, g_mix_pre, w_in, b_forget, g_q_lora, w_q_up, g_kv_lora, w_kv_up, g_mix_out, w_out, g_mix_post, g_ffn_pre, w_ffn_up, w_ffn_down, g_ffn_post, loss_target, m_g_mix_pre, m_w_in, m_b_forget, m_g_q_lora, m_w_q_up, m_g_kv_lora, m_w_kv_up, m_g_mix_out, m_w_out, m_g_mix_post, m_g_ffn_pre, m_w_ffn_up, m_w_ffn_down, m_g_ffn_post, v_g_mix_pre, v_w_in, v_b_forget, v_g_q_lora, v_w_q_up, v_g_kv_lora, v_w_kv_up, v_g_mix_out, v_w_out, v_g_mix_post, v_g_ffn_pre, v_w_ffn_up, v_w_ffn_down, v_g_ffn_post):
    t = x.shape[1]
    nq = t // BQ
    x0 = x[0]
    tgt = loss_target[0]
    pos = positions[0].astype(F32).reshape(t, 1)

    half_r, half_m = HEAD_DIM // 2, ROPE_MLA // 2
    invf_r = ROPE_BASE ** (-jnp.arange(half_r, dtype=F32) / half_r)
    invf_m = ROPE_BASE ** (-jnp.arange(half_m, dtype=F32) / half_m)
    invf = jnp.concatenate([invf_r, invf_r, invf_m, invf_m,
                            jnp.zeros((LANES - HEAD_DIM - ROPE_MLA,), F32)]).reshape(1, LANES)
    log_gamma = jnp.log1p(-jnp.power(2.0, -5.0 - jnp.arange(HEADS, dtype=F32)))
    lg_lanes = jnp.repeat(log_gamma, LANES).reshape(1, HP)
    lg_heads = jnp.broadcast_to(log_gamma[:, None, None], (HEADS, 1, LANES))

    big = [w_in, w_q_up, w_kv_up, w_out, w_ffn_up, w_ffn_down]
    bf = lambda w: w.astype(BF16)
    first = _exchange([bf(w_in[0]), bf(w_q_up), bf(w_kv_up)], True, "weights_gather_first")
    l0_state, l0_token = _exchange_start([bf(w[0]) for w in (w_out, w_ffn_up, w_ffn_down)], True,
                                         "weights_gather_layer0_start")
    l1_state, rest_token = _exchange_start([bf(w[1]) for w in (w_in, w_out, w_ffn_up, w_ffn_down)], True,
                                           "weights_gather_layer1_start", after=l0_token)
    wq_g = first[1].transpose(1, 2, 0, 3).reshape(DEPTH, Q_RANK, 384)
    wkv_g = first[2].transpose(1, 2, 0, 3).reshape(DEPTH, KV_RANK, 512)

    row = lambda g: g.reshape(1, -1)
    layers = []
    for l in range(DEPTH):
        layers.append(dict(
            wq=_pad_heads(wq_g[l], 96), wkv=_pad_w_kv(wkv_g[l]),
            g_pre=row(g_mix_pre[l]), g_post=row(g_mix_post[l]), g_fpre=row(g_ffn_pre[l]), g_fpost=row(g_ffn_post[l]),
            g_out=_pad_gain_out(g_mix_out[l]), g_q=row(g_q_lora[l]), g_kv=row(g_kv_lora[l]),
            b_pad=jnp.pad(b_forget[l], (0, LANES - HEADS)).reshape(1, LANES)))
    layers[0]["win"] = _pad_w_in(first[0].reshape(D_MODEL, D_IN))

    saved = []
    xin = x0
    h = _rms_fwd(xin, layers[0]["g_pre"] + rest_token[0:1, 0:1], "rms_pre_0")
    loss_row = dx = None
    for l, p in enumerate(layers):
        s = dict(x=xin, h=h)
        proj = _mm(h, p["win"], name=f"in_proj_{l}", tm=512, tn=NP_IN)
        cum = _cumsum(proj, False, f"forget_cumsum_{l}", forget_bias=p["b_pad"])
        fox, ret, sb, mla, fox_t, sb_t, mla_t, ret_t = _prep_fwd(proj, cum, pos, invf, lg_lanes, p["g_q"],
                                                          p["g_kv"], p["wq"], p["wkv"], f"prep_fwd_{l}")
        oa, lse_a = _softmax_fwd(fox, fox_t, chunk_mask=False, scale=1.0, name=f"fox_fwd_{l}")
        ob, lse_b = _softmax_fwd(mla, mla_t, chunk_mask=True, scale=1.0, name=f"mla_fwd_{l}")
        oc, ret_st = _ret_fwd(ret, ret_t, lg_heads, f"ret_fwd_{l}")
        od, sb_tot = _sb_fwd(sb, sb_t, f"sb_fwd_{l}")
        if l == 0:
            got = _exchange_wait(l0_state, od, True, "weights_gather_layer0_wait")
            p.update(wout=got[0].reshape(D_MODEL, D_MODEL), wup=got[1].transpose(1, 0, 2).reshape(D_MODEL, D_FF),
                     wdn=got[2].reshape(D_FF, D_MODEL))
        mixed = _post_fwd(oa, ob, oc, od, proj, p["g_out"], f"post_fwd_{l}")
        mix = _mm(mixed, p["wout"], name=f"out_proj_{l}", tm=2048)
        x1, h2 = _add_rms_fwd(xin, mix, p["g_post"], p["g_fpre"], f"mix_residual_{l}")
        a = _mm(h2, p["wup"], name=f"ffn_up_{l}", out_dtype=BF16, tm=2048)
        y = _mm(a, p["wdn"], name=f"ffn_down_{l}", a_fn=_relu2, tk=D_FF)
        s.update(proj=proj, fox=fox, ret=ret, sb=sb, mla=mla, fox_t=fox_t, sb_t=sb_t, mla_t=mla_t, ret_st=ret_st, oa=oa, ob=ob, oc=oc,
                 od=od, sb_tot=sb_tot, lse_a=lse_a, lse_b=lse_b, mixed=mixed, mix=mix, x1=x1, h2=h2, a=a, y=y)
        saved.append(s)
        if l == 0:
            got = _exchange_wait(l1_state, y, True, "weights_gather_layer1_wait")
            layers[1].update(win=_pad_w_in(got[0].reshape(D_MODEL, D_IN)), wout=got[1].reshape(D_MODEL, D_MODEL),
                             wup=got[2].transpose(1, 0, 2).reshape(D_MODEL, D_FF), wdn=got[3].reshape(D_FF, D_MODEL))
        if l + 1 < DEPTH:
            xin, h = _add_rms_fwd(x1, y, p["g_fpost"], layers[l + 1]["g_pre"], f"ffn_residual_{l}")
        else:
            loss_row, dx = _final_loss(x1, y, p["g_fpost"], tgt, "loss")

    small_g = {n: [None] * DEPTH for n, _ in _SMALL}
    big_g = [[None] * DEPTH for _ in range(6)]
    to_send = [
        lambda g: g.reshape(N_DEV, 1, D_MODEL // N_DEV, D_IN),
        lambda g: g.reshape(Q_RANK, N_DEV, 384 // N_DEV).transpose(1, 0, 2)[:, None],
        lambda g: g.reshape(KV_RANK, N_DEV, 512 // N_DEV).transpose(1, 0, 2)[:, None],
        lambda g: g.reshape(N_DEV, 1, D_MODEL // N_DEV, D_MODEL),
        lambda g: g[:, None],
        lambda g: g.reshape(N_DEV, 1, D_FF // N_DEV, D_MODEL),
    ]
    send_of = lambda ks, l: [to_send[k](big_g[k][l]).astype(BF16) for k in ks]
    late_state = early_state = None
    order_token = jnp.zeros((1, 1), F32)
    for l in reversed(range(DEPTH)):
        p, s = layers[l], saved[l]
        dy, dg = _norm_bwd(dx, s["y"], p["g_fpost"] + order_token, None, BF16, f"ffn_post_bwd_{l}")
        small_g["g_ffn_post"][l] = dg
        da = _mm(dy, p["wdn"], name=f"ffn_down_dx_{l}", tb=True, out_dtype=BF16, epi=_drelu2, epi_in=s["a"],
                 tm=2048)
        big_g[5][l] = _mm(s["a"], dy, name=f"ffn_down_dw_{l}", ta=True, a_fn=_relu2, tk=t, out_dtype=BF16)
        big_g[4][l] = _mm(s["h2"], da, name=f"ffn_up_dw_{l}", ta=True, tk=t, tn=D_FF // N_DEV, out_dtype=BF16,
                          column_blocks=True)
        dh2 = _mm(da, p["wup"], name=f"ffn_up_dx_{l}", tb=True, tk=D_FF)
        dx1, dmix, small_g["g_ffn_pre"][l], small_g["g_mix_post"][l] = _norm_bwd_pair(
            dh2, s["x1"], p["g_fpre"], dx, s["mix"], p["g_post"], f"ffn_pre_mix_post_bwd_{l}")
        dmixed = _mm(dmix, p["wout"], name=f"out_proj_dx_{l}", tb=True, tm=2048)
        big_g[3][l] = _mm(s["mixed"], dmix, name=f"out_proj_dw_{l}", ta=True, tk=t, out_dtype=BF16)
        g_out = p["g_out"]
        if l == 0:
            early_state, early_token = _exchange_start(send_of((3, 4, 5), 0), False, "grads_layer0_early_start")
            g_out = g_out + early_token[0:1, 0:1]
        doa, dob, doc, dod, dla, dlb, drg, dgo = _post_bwd(dmixed, s["oa"], s["ob"], s["oc"], s["od"], s["proj"],
                                                           g_out, f"post_bwd_{l}")
        small_g["g_mix_out"][l] = _unpad_gain_out(dgo).reshape(1, D_MODEL)
        as_rows = lambda a: a[:, :HEADS].T.reshape(HEADS, t // BQA, 1, BQA)
        dfq, dfk, dfv, dcum_k, dcum_q = _softmax_bwd(s["fox"], s["fox_t"], doa, s["lse_a"], as_rows(dla), bias=True,
                                                     chunk_mask=False, scale=1.0, name=f"fox_bwd_{l}")
        dmq, dmk, dmv = _softmax_bwd(s["mla"], s["mla_t"], dob, s["lse_b"], as_rows(dlb), bias=False, chunk_mask=True,
                                     scale=1.0, name=f"mla_bwd_{l}")
        drq, drk, drv = _ret_bwd(s["ret"], s["ret_st"], lg_heads, doc, f"ret_bwd_{l}")
        dsq, dsk, dsv = _sb_bwd(s["sb"], s["sb_t"], dod, s["sb_tot"], f"sb_bwd_{l}")
        dcum_q = jnp.pad(dcum_q.reshape(HEADS, t).T, ((0, 0), (0, LANES - HEADS)))
        dlsf = _cumsum(dcum_q, True, f"forget_cumsum_bwd_{l}", partials=dcum_k)
        dproj, dwq, dwkv, dgq, dgkv, dbf = _prep_bwd(
            (dfq, dfk, dfv), (drq, drk, drv), (dsq, dsk, dsv), (dmq, dmk, dmv), drg, dlsf, s["proj"], pos, invf,
            lg_lanes, p["b_pad"], p["g_q"], p["g_kv"], p["wq"], p["wkv"], f"prep_bwd_{l}")
        small_g["g_q_lora"][l] = dgq
        small_g["g_kv_lora"][l] = dgkv
        small_g["b_forget"][l] = dbf[:, :HEADS]
        big_g[1][l] = _unpad_heads(dwq, 96)
        big_g[2][l] = _unpad_w_kv(dwkv)
        dh = _mm(dproj, p["win"], name=f"in_proj_dx_{l}", tb=True, tk=NP_IN)
        big_g[0][l] = _unpad_w_in(_mm(s["h"], dproj, name=f"in_proj_dw_{l}", ta=True, tn=NP_IN // 2, tk=2048,
                                          out_dtype=BF16))
        g_pre = p["g_pre"]
        if l == 0:
            last_state, last_token = _exchange_start(send_of((0, 1, 2), 0), False, "grads_layer0_rest_start")
            g_pre = g_pre + last_token[0:1, 0:1]
        dx, dg = _norm_bwd(dh, s["x"], g_pre, dx1, F32, f"mix_pre_bwd_{l}")
        small_g["g_mix_pre"][l] = dg
        if l == DEPTH - 1:
            late_state, late_token = _exchange_start(send_of(range(6), l), False, "grads_layer1_start")
            order_token = late_token[0:1, 0:1]
    grad_x = dx.reshape(1, t, D_MODEL)

    res = {}
    small_w = dict(g_mix_pre=g_mix_pre, g_mix_post=g_mix_post, g_ffn_pre=g_ffn_pre, g_ffn_post=g_ffn_post,
                   g_mix_out=g_mix_out, g_q_lora=g_q_lora, g_kv_lora=g_kv_lora, b_forget=b_forget)
    small_m = dict(g_mix_pre=m_g_mix_pre, g_mix_post=m_g_mix_post, g_ffn_pre=m_g_ffn_pre, g_ffn_post=m_g_ffn_post,
                   g_mix_out=m_g_mix_out, g_q_lora=m_g_q_lora, g_kv_lora=m_g_kv_lora, b_forget=m_b_forget)
    small_v = dict(g_mix_pre=v_g_mix_pre, g_mix_post=v_g_mix_post, g_ffn_pre=v_g_ffn_pre, g_ffn_post=v_g_ffn_post,
                   g_mix_out=v_g_mix_out, g_q_lora=v_g_q_lora, g_kv_lora=v_g_kv_lora, b_forget=v_b_forget)
    n_small = DEPTH * SMALL_ROWS
    extra = lambda a: jnp.concatenate([a, jnp.zeros((8, LANES), F32)], axis=0)
    part = jnp.concatenate([_pack_small({n: jnp.concatenate(small_g[n], axis=0) for n, _ in _SMALL}),
                            jnp.broadcast_to(loss_row, (8, LANES))], axis=0)
    sres = _small_allreduce_adam(part, extra(_pack_small(small_w)), extra(_pack_small(small_m)),
                                 extra(_pack_small(small_v)), "small_allreduce_adamw")
    loss = sres[0][n_small, 0]
    sg, sd, sm, sv = [_unpack_small(a[:n_small]) for a in sres]
    for n, _ in _SMALL:
        res[n] = [sg[n], sd[n], sm[n], sv[n]]

    late = _exchange_wait(late_state, dx, False, "grads_layer1_wait")
    early = _exchange_wait(early_state, dx, False, "grads_layer0_early_wait")
    ms = [m_w_in, m_w_q_up, m_w_kv_up, m_w_out, m_w_ffn_up, m_w_ffn_down]
    vs = [v_w_in, v_w_q_up, v_w_kv_up, v_w_out, v_w_ffn_up, v_w_ffn_down]
    names = ["w_in", "w_q_up", "w_kv_up", "w_out", "w_ffn_up", "w_ffn_down"]
    for k in (3, 4, 5):
        res[names[k]] = _reduce_adam([early[k - 3], late[k]], big[k], ms[k], vs[k], f"adamw_{names[k]}")
    done = [sres[0]] + [res[names[k]][0] for k in (3, 4, 5)]
    last = _exchange_wait(last_state, done, False, "grads_layer0_rest_wait")
    for k in (0, 1, 2):
        res[names[k]] = _reduce_adam([last[k], late[k]], big[k], ms[k], vs[k], f"adamw_{names[k]}")

    order = ["g_mix_pre", "w_in", "b_forget", "g_q_lora", "w_q_up", "g_kv_lora", "w_kv_up", "g_mix_out", "w_out",
             "g_mix_post", "g_ffn_pre", "w_ffn_up", "w_ffn_down", "g_ffn_post"]
    outs = [loss, grad_x]
    for idx in range(4):
        outs += [res[n][idx] for n in order]
    return tuple(outs)
```

```python
import functools
import math

import numpy as np
import jax
import jax.numpy as jnp
from jax import lax
from jax.experimental import pallas as pl
from jax.experimental.pallas import tpu as pltpu

F32 = jnp.float32
BF16 = jnp.bfloat16

D_MODEL = 1024
DEPTH = 2
N_DEV = 8
GROUP = 256
HEADS = 4
HEAD_DIM = 64
LANES = 128
HP = HEADS * LANES
QKV = 3 * HP
Q_RANK = 256
KV_RANK = 128
ROPE_MLA = 32
MLA_SCALE = (HEAD_DIM + ROPE_MLA) ** -0.5
D_FF = 4096
D_IN = 2980
CHUNK_SHIFT = 6
EPS = 1e-6
ROPE_BASE = 10000.0
NEG = -1e30

QKV_IN = 3 * GROUP
OFF_FOX, OFF_RET, OFF_SB = 0, QKV_IN, 2 * QKV_IN
OFF_RG = 3 * QKV_IN
OFF_CQ = OFF_RG + GROUP
OFF_CKV = OFF_CQ + Q_RANK
OFF_KR = OFF_CKV + LANES
OFF_FF = OFF_KR + LANES
NP_IN = 3328

BQ = 256
TKS = 128
TM = 512
VMEM_LIMIT = 58 * 1024 * 1024

ADAM_LR, ADAM_B1, ADAM_B2, ADAM_EPS, ADAM_WD, ADAM_STEP = 0.001, 0.9, 0.999, 1e-08, 0.01, 10
ADAM_C1 = 1.0 - ADAM_B1 ** ADAM_STEP
ADAM_C2 = 1.0 - ADAM_B2 ** ADAM_STEP

SMALL_ROWS = 44

NT = (((1,), (1,)), ((), ()))
TN = (((0,), (0,)), ((), ()))


def _cp(*sem):
    return pltpu.CompilerParams(dimension_semantics=sem if sem else None, vmem_limit_bytes=VMEM_LIMIT)


def _bdot(a, b, dn=None):
    if dn is None:
        return jnp.dot(a, b, preferred_element_type=F32)
    return lax.dot_general(a, b, dn, preferred_element_type=F32)


def _split2(x):
    hi = x.astype(BF16)
    lo = (x - hi.astype(F32)).astype(BF16)
    return hi, lo


def _mm(a, b, *, name, ta=False, tb=False, out_dtype=F32, a_fn=None, epi=None, epi_in=None, column_blocks=False,
        tm=1024, tn=1024, tk=1024):
    m, k = (a.shape[1], a.shape[0]) if ta else a.shape
    n = b.shape[0] if tb else b.shape[1]
    tm, tn, tk = min(tm, m), min(tn, n), min(tk, k)
    assert m % tm == 0 and n % tn == 0 and k % tk == 0, (name, m, n, k)
    nk = k // tk
    dn = (((0 if ta else 1,), (1 if tb else 0,)), ((), ()))

    def body(*refs):
        if epi is None:
            a_ref, b_ref, o_ref = refs[:3]
            e_ref = None
            rest = refs[3:]
        else:
            a_ref, b_ref, e_ref, o_ref = refs[:4]
            rest = refs[4:]
        av = a_ref[...]
        if a_fn is not None:
            av = a_fn(av)
        part = lax.dot_general(av.astype(BF16), b_ref[...].astype(BF16), dn, preferred_element_type=F32)

        def finish(r):
            if epi is not None:
                r = epi(r, e_ref[...])
            o_ref[...] = r.astype(out_dtype)

        if nk == 1:
            finish(part)
        else:
            acc_ref = rest[0]
            kk = pl.program_id(2)

            @pl.when(kk == 0)
            def _():
                acc_ref[...] = part

            @pl.when(kk > 0)
            def _():
                acc_ref[...] += part

            @pl.when(kk == nk - 1)
            def _():
                finish(acc_ref[...])

    a_spec = pl.BlockSpec((tk, tm), lambda i, j, kk: (kk, i)) if ta else pl.BlockSpec((tm, tk), lambda i, j, kk: (i, kk))
    b_spec = pl.BlockSpec((tn, tk), lambda i, j, kk: (j, kk)) if tb else pl.BlockSpec((tk, tn), lambda i, j, kk: (kk, j))
    o_spec = pl.BlockSpec((tm, tn), lambda i, j, kk: (i, j))
    in_specs = [a_spec, b_spec]
    args = [a, b]
    if epi is not None:
        in_specs.append(o_spec)
        args.append(epi_in)
    out_shape = jax.ShapeDtypeStruct((m, n), out_dtype)
    if column_blocks:
        assert epi is None
        o_spec = pl.BlockSpec((None, tm, tn), lambda i, j, kk: (j, i, 0))
        out_shape = jax.ShapeDtypeStruct((n // tn, m, tn), out_dtype)
    return pl.pallas_call(
        body, name=name, grid=(m // tm, n // tn, nk),
        in_specs=in_specs, out_specs=o_spec,
        out_shape=out_shape,
        scratch_shapes=[pltpu.VMEM((tm, tn), F32)] if nk > 1 else [],
        compiler_params=_cp("parallel", "parallel", "arbitrary"),
    )(*args)


def _relu2(v):
    r = jnp.maximum(v, 0.0)
    return r * r


def _drelu2(du, av):
    return du * (2.0 * jnp.maximum(av, 0.0))


def _rms(v, g):
    r = lax.rsqrt(jnp.mean(v * v, axis=-1, keepdims=True) + EPS)
    return v * r * g


def _row_spec(w):
    return pl.BlockSpec((TM, w), lambda i: (i, 0))


TMN = 1024


def _nrow_spec(w, t):
    return pl.BlockSpec((min(TMN, t), w), lambda i: (i, 0))


def _ngrid(t):
    return (t // min(TMN, t),)


def _vec_spec(w):
    return pl.BlockSpec((1, w), lambda i: (0, 0))


def _rms_fwd(x, g, name):
    t, d = x.shape

    def body(x_ref, g_ref, h_ref):
        h_ref[...] = _rms(x_ref[...], g_ref[...]).astype(BF16)

    return pl.pallas_call(
        body, name=name, grid=_ngrid(t), in_specs=[_nrow_spec(d, t), _vec_spec(d)], out_specs=_nrow_spec(d, t),
        out_shape=jax.ShapeDtypeStruct((t, d), BF16), compiler_params=_cp("parallel"))(x, g)


def _add_rms_fwd(x, y, g1, g2, name):
    t, d = x.shape

    def body(x_ref, y_ref, g1_ref, g2_ref, xn_ref, h_ref):
        xn = x_ref[...] + _rms(y_ref[...], g1_ref[...])
        xn_ref[...] = xn
        h_ref[...] = _rms(xn, g2_ref[...]).astype(BF16)

    return pl.pallas_call(
        body, name=name, grid=_ngrid(t),
        in_specs=[_nrow_spec(d, t), _nrow_spec(d, t), _vec_spec(d), _vec_spec(d)],
        out_specs=[_nrow_spec(d, t), _nrow_spec(d, t)],
        out_shape=[jax.ShapeDtypeStruct((t, d), F32), jax.ShapeDtypeStruct((t, d), BF16)],
        compiler_params=_cp("parallel"))(x, y, g1, g2)


def _final_loss(x, y, g, tgt, name):
    t, d = x.shape

    def body(x_ref, y_ref, g_ref, t_ref, l_ref, dx_ref):
        @pl.when(pl.program_id(0) == 0)
        def _():
            l_ref[...] = jnp.zeros_like(l_ref)

        err = x_ref[...] + _rms(y_ref[...], g_ref[...]) - t_ref[...]
        dx_ref[...] = err * (1.0 / d)
        l_ref[...] += jnp.sum(jnp.sum(err * err, axis=1, keepdims=True), axis=0, keepdims=True) * (0.5 / d)

    return pl.pallas_call(
        body, name=name, grid=_ngrid(t),
        in_specs=[_nrow_spec(d, t), _nrow_spec(d, t), _vec_spec(d), _nrow_spec(d, t)],
        out_specs=[pl.BlockSpec((1, LANES), lambda i: (0, 0)), _nrow_spec(d, t)],
        out_shape=[jax.ShapeDtypeStruct((1, LANES), F32), jax.ShapeDtypeStruct((t, d), F32)],
        compiler_params=_cp("arbitrary"))(x, y, g, tgt)


def _rms_bwd_vals(dn, v, g):
    w = v.shape[-1]
    r = lax.rsqrt(jnp.mean(v * v, axis=-1, keepdims=True) + EPS)
    vh = v * r
    dgp = jnp.sum(dn * vh, axis=0, keepdims=True)
    dvh = dn * g
    dv = r * (dvh - vh * (jnp.sum(dvh * vh, axis=-1, keepdims=True) * (1.0 / w)))
    return dv, dgp


def _norm_bwd(dn, v, g, resid, out_dtype, name):
    t, d = v.shape
    has_res = resid is not None

    def body(*refs):
        if has_res:
            dn_ref, v_ref, g_ref, r_ref, dv_ref, dg_ref = refs
        else:
            dn_ref, v_ref, g_ref, dv_ref, dg_ref = refs

        @pl.when(pl.program_id(0) == 0)
        def _():
            dg_ref[...] = jnp.zeros_like(dg_ref)

        dv, dgp = _rms_bwd_vals(dn_ref[...].astype(F32), v_ref[...], g_ref[...])
        if has_res:
            dv = dv + r_ref[...]
        dv_ref[...] = dv.astype(out_dtype)
        dg_ref[...] += dgp

    in_specs = [_nrow_spec(d, t), _nrow_spec(d, t), _vec_spec(d)] + ([_nrow_spec(d, t)] if has_res else [])
    args = [dn, v, g] + ([resid] if has_res else [])
    return pl.pallas_call(
        body, name=name, grid=_ngrid(t), in_specs=in_specs,
        out_specs=[_nrow_spec(d, t), _vec_spec(d)],
        out_shape=[jax.ShapeDtypeStruct((t, d), out_dtype), jax.ShapeDtypeStruct((1, d), F32)],
        compiler_params=_cp("arbitrary"))(*args)


def _norm_bwd_pair(dn, v1, g1, resid, v2, g2, name):
    t, d = v1.shape

    def body(dn_ref, v1_ref, g1_ref, r_ref, v2_ref, g2_ref, d1_ref, d2_ref, dg1_ref, dg2_ref):
        @pl.when(pl.program_id(0) == 0)
        def _():
            dg1_ref[...] = jnp.zeros_like(dg1_ref)
            dg2_ref[...] = jnp.zeros_like(dg2_ref)

        d1, dgp1 = _rms_bwd_vals(dn_ref[...].astype(F32), v1_ref[...], g1_ref[...])
        d1 = d1 + r_ref[...]
        d1_ref[...] = d1
        d2, dgp2 = _rms_bwd_vals(d1, v2_ref[...], g2_ref[...])
        d2_ref[...] = d2.astype(BF16)
        dg1_ref[...] += dgp1
        dg2_ref[...] += dgp2

    row, vec = _nrow_spec(d, t), _vec_spec(d)
    return pl.pallas_call(
        body, name=name, grid=_ngrid(t), in_specs=[row, row, vec, row, row, vec],
        out_specs=[row, row, vec, vec],
        out_shape=[jax.ShapeDtypeStruct((t, d), F32), jax.ShapeDtypeStruct((t, d), BF16),
                   jax.ShapeDtypeStruct((1, d), F32), jax.ShapeDtypeStruct((1, d), F32)],
        compiler_params=_cp("arbitrary"))(dn, v1, g1, resid, v2, g2)


def _rope_trig(pos, invf):
    ang = pos * invf
    return jnp.cos(ang), jnp.sin(ang)


def _rope_tables(trig, w, lo, half):
    c, s = trig
    lane = lax.broadcasted_iota(jnp.int32, c.shape, 1)
    active = (lane >= lo) & (lane < lo + 2 * half)
    cos = jnp.concatenate([jnp.where(active, c, 1.0)] * (w // LANES), axis=1)
    sin = jnp.concatenate([jnp.where(active, s, 0.0)] * (w // LANES), axis=1)
    lanes = lax.broadcasted_iota(jnp.int32, (c.shape[0], w), 1) & (LANES - 1)
    first = (lanes >= lo) & (lanes < lo + half)
    second = (lanes >= lo + half) & (lanes < lo + 2 * half)
    return cos, sin, first, second


def _rope_apply(v, cos, sin, first, second, half, sign):
    w = v.shape[-1]
    up = pltpu.roll(v, w - half, 1)
    dn = pltpu.roll(v, half, 1)
    rot = jnp.where(first, -up, jnp.where(second, dn, 0.0))
    return v * cos + rot * (sin * sign)


def _split3(c):
    hi = c.astype(BF16).astype(F32)
    mid = (c - hi).astype(BF16).astype(F32)
    return hi, mid, (c - hi) - mid


def _spread_heads(x):
    low = lax.broadcasted_iota(jnp.int32, (x.shape[0], LANES), 1) < HEAD_DIM
    out = []
    for c in range(x.shape[1] // LANES):
        blk = x[:, c * LANES:(c + 1) * LANES]
        out.append(jnp.where(low, blk, 0.0))
        out.append(jnp.where(low, pltpu.roll(blk, HEAD_DIM, 1), 0.0))
    return jnp.concatenate(out, axis=1)


def _gather_heads(y):
    low = lax.broadcasted_iota(jnp.int32, (y.shape[0], LANES), 1) < HEAD_DIM
    out = []
    for c in range(y.shape[1] // (2 * LANES)):
        a = y[:, 2 * c * LANES:(2 * c + 1) * LANES]
        b = y[:, (2 * c + 1) * LANES:(2 * c + 2) * LANES]
        out.append(jnp.where(low, a, pltpu.roll(b, HEAD_DIM, 1)))
    return jnp.concatenate(out, axis=1)


def _transposed_tiles(dst, row0, blocks, width):
    for b, blk in enumerate(blocks):
        bt = blk.T.astype(BF16)
        rows = bt.shape[0]
        for w in range(TM // width):
            dst[w, row0 + b * rows:row0 + (b + 1) * rows, :] = bt[:, w * width:(w + 1) * width]


def _prep_fwd(proj, cum, pos, invf, lg_lanes, g_q, g_kv, wq_pad, wkv_pad, name):
    t = proj.shape[0]

    def body(fox_ref, ret_ref, sb_ref, cq_ref, ckv_ref, kr_ref, cum_ref, pos_ref, invf_ref, lg_ref,
             gq_ref, gkv_ref, wq_ref, wkv_ref,
             ofox_ref, oret_ref, osb_ref, omla_ref, ofoxt_ref, osbt_ref, omlat_ref, orett_ref):
        pos_v = pos_ref[...]
        foxv, retv, sbv = [[_spread_heads(ref[:, g * GROUP:(g + 1) * GROUP]) for g in range(3)]
                           for ref in (fox_ref, ret_ref, sb_ref)]
        osb_ref[:, 0:HP] = (sbv[0] * 0.125).astype(BF16)
        osb_ref[:, HP:2 * HP] = sbv[1].astype(BF16)
        osb_ref[:, 2 * HP:QKV] = sbv[2].astype(BF16)
        _transposed_tiles(osbt_ref, 0, [sbv[1], sbv[2]], TKS)
        lane = lax.broadcasted_iota(jnp.int32, (TM, LANES), 1)
        cumv = cum_ref[...]
        fq, fk = [], []
        for hb in range(HEADS):
            hi, mid, lo = _split3(cumv[:, hb:hb + 1])
            q = foxv[0][:, hb * LANES:(hb + 1) * LANES] * 0.125
            k = foxv[1][:, hb * LANES:(hb + 1) * LANES]
            ones_q = (lane >= HEAD_DIM) & (lane < HEAD_DIM + 3)
            ones_k = (lane >= HEAD_DIM + 3) & (lane < HEAD_DIM + 6)
            q = jnp.where(ones_q, 1.0, q)
            k = jnp.where(ones_k, 1.0, k)
            for n, part in enumerate((hi, mid, lo)):
                q = jnp.where(lane == HEAD_DIM + 3 + n, part, q)
                k = jnp.where(lane == HEAD_DIM + n, -part, k)
            fq.append(q)
            fk.append(k)
        fk = jnp.concatenate(fk, axis=1)
        ofox_ref[:, 0:HP] = jnp.concatenate(fq, axis=1).astype(BF16)
        ofox_ref[:, HP:2 * HP] = fk.astype(BF16)
        ofox_ref[:, 2 * HP:QKV] = foxv[2].astype(BF16)
        _transposed_tiles(ofoxt_ref, 0, [fk, foxv[2]], BQ)
        trig = _rope_trig(pos_v, invf_ref[...])
        cos, sin, first, second = _rope_tables(trig, HP, 0, HEAD_DIM // 2)
        nloc = (lax.broadcasted_iota(jnp.int32, (TM, 1), 0) & (BQ - 1)).astype(F32)
        dec = lg_ref[...] * nloc
        rq = _rope_apply(retv[0], cos, sin, first, second, HEAD_DIM // 2, 1.0)
        rk = _rope_apply(retv[1], cos, sin, first, second, HEAD_DIM // 2, 1.0)
        oret_ref[:, 0:HP] = (rq * jnp.exp(dec)).astype(BF16)
        rk = rk * 0.125 * jnp.exp(-dec)
        oret_ref[:, HP:2 * HP] = rk.astype(BF16)
        _transposed_tiles(orett_ref, 0, [rk], BQ)
        oret_ref[:, 2 * HP:QKV] = retv[2].astype(BF16)
        cosm, sinm, firstm, secondm = _rope_tables(trig, HP, HEAD_DIM, ROPE_MLA // 2)
        cqn = _rms(cq_ref[...], gq_ref[...]).astype(BF16)
        qm = _bdot(cqn, wq_ref[...])
        qm = _rope_apply(qm, cosm, sinm, firstm, secondm, ROPE_MLA // 2, 1.0)
        omla_ref[:, 0:HP] = (qm * MLA_SCALE).astype(BF16)
        ckvn = _rms(ckv_ref[...], gkv_ref[...]).astype(BF16)
        kv = _bdot(ckvn, wkv_ref[...])
        krr = _rope_apply(kr_ref[...], cosm[:, 0:LANES], sinm[:, 0:LANES], firstm[:, 0:LANES],
                          secondm[:, 0:LANES], ROPE_MLA // 2, 1.0)
        mk = kv[:, 0:HP] + jnp.concatenate([krr] * HEADS, axis=1)
        omla_ref[:, HP:2 * HP] = mk.astype(BF16)
        omla_ref[:, 2 * HP:QKV] = kv[:, HP:2 * HP].astype(BF16)
        _transposed_tiles(omlat_ref, 0, [mk, kv[:, HP:2 * HP]], BQ)

    def seg(off, w):
        return pl.BlockSpec((TM, w), lambda i, o=off // w: (i, o))

    def full(shape):
        return pl.BlockSpec(shape, lambda i: (0,) * len(shape))

    def tiles(width):
        return pl.BlockSpec((TM // width, 2 * HP, width), lambda i: (i, 0, 0))

    in_specs = [seg(OFF_FOX, QKV_IN), seg(OFF_RET, QKV_IN), seg(OFF_SB, QKV_IN), seg(OFF_CQ, Q_RANK),
                seg(OFF_CKV, LANES), seg(OFF_KR, LANES), _row_spec(LANES), pl.BlockSpec((TM, 1), lambda i: (i, 0)),
                full((1, LANES)), full((1, HP)), full((1, Q_RANK)), full((1, KV_RANK)),
                full((Q_RANK, HP)), full((KV_RANK, 2 * HP))]
    out_specs = [_row_spec(QKV)] * 4 + [tiles(BQ), tiles(TKS), tiles(BQ),
                                        pl.BlockSpec((TM // BQ, HP, BQ), lambda i: (i, 0, 0))]
    out_shape = [jax.ShapeDtypeStruct((t, QKV), BF16)] * 4 + [
        jax.ShapeDtypeStruct((t // BQ, 2 * HP, BQ), BF16), jax.ShapeDtypeStruct((t // TKS, 2 * HP, TKS), BF16),
        jax.ShapeDtypeStruct((t // BQ, 2 * HP, BQ), BF16), jax.ShapeDtypeStruct((t // BQ, HP, BQ), BF16)]
    return pl.pallas_call(
        body, name=name, grid=(t // TM,), in_specs=in_specs, out_specs=out_specs, out_shape=out_shape,
        compiler_params=_cp("parallel"))(proj, proj, proj, proj, proj, proj, cum, pos, invf, lg_lanes,
                                         g_q, g_kv, wq_pad, wkv_pad)


def _prep_bwd(dfox, dret, dsb, dmla, drg, dlsf, proj, pos, invf, lg_lanes, b_pad, g_q, g_kv,
              wq_pad, wkv_pad, name):
    t = proj.shape[0]

    def body(dfq, dfk, dfv, drq, drk, drv, dsq, dsk, dsv, dmq, dmk, dmv, drg_ref, dlsf_ref,
             cq_ref, ckv_ref, ff_ref, pos_ref, invf_ref, lg_ref, b_ref, gq_ref, gkv_ref, wq_ref, wkv_ref,
             dp_ref, dwq_ref, dwkv_ref, dgq_ref, dgkv_ref, dbf_ref):
        @pl.when(pl.program_id(0) == 0)
        def _():
            dwq_ref[...] = jnp.zeros_like(dwq_ref)
            dwkv_ref[...] = jnp.zeros_like(dwkv_ref)
            dgq_ref[...] = jnp.zeros_like(dgq_ref)
            dgkv_ref[...] = jnp.zeros_like(dgkv_ref)
            dbf_ref[...] = jnp.zeros_like(dbf_ref)

        pos_v = pos_ref[...]
        def put(off, val):
            dp_ref[:, off:off + GROUP] = _gather_heads(val).astype(BF16)

        for off, (dq, dk, dv) in ((OFF_FOX, (dfq, dfk, dfv)), (OFF_SB, (dsq, dsk, dsv))):
            put(off, dq[...] * 0.125)
            put(off + GROUP, dk[...])
            put(off + 2 * GROUP, dv[...])
        trig = _rope_trig(pos_v, invf_ref[...])
        cos, sin, first, second = _rope_tables(trig, HP, 0, HEAD_DIM // 2)
        nloc = (lax.broadcasted_iota(jnp.int32, (TM, 1), 0) & (BQ - 1)).astype(F32)
        dec = lg_ref[...] * nloc
        dq = _rope_apply(drq[...] * jnp.exp(dec), cos, sin, first, second, HEAD_DIM // 2, -1.0)
        dk = _rope_apply(drk[...] * (0.125 * jnp.exp(-dec)), cos, sin, first, second, HEAD_DIM // 2, -1.0)
        put(OFF_RET, dq)
        put(OFF_RET + GROUP, dk)
        put(OFF_RET + 2 * GROUP, drv[...])
        put(OFF_RG, drg_ref[...])
        cosm, sinm, firstm, secondm = _rope_tables(trig, HP, HEAD_DIM, ROPE_MLA // 2)
        dql = _rope_apply(dmq[...] * MLA_SCALE, cosm, sinm, firstm, secondm, ROPE_MLA // 2, -1.0).astype(BF16)
        cq = cq_ref[...]
        cqn = _rms(cq, gq_ref[...]).astype(BF16)
        dwq_ref[...] += _bdot(cqn, dql, TN)
        dcqn = _bdot(dql, wq_ref[...], NT)
        dcq, dgq = _rms_bwd_vals(dcqn, cq, gq_ref[...])
        dgq_ref[...] += dgq
        dp_ref[:, OFF_CQ:OFF_CQ + Q_RANK] = dcq.astype(BF16)
        dkm = dmk[...]
        dkv = jnp.concatenate([dkm, dmv[...]], axis=1).astype(BF16)
        ckv = ckv_ref[...]
        ckvn = _rms(ckv, gkv_ref[...]).astype(BF16)
        dwkv_ref[...] += _bdot(ckvn, dkv, TN)
        dckvn = _bdot(dkv, wkv_ref[...], NT)
        dckv, dgkv = _rms_bwd_vals(dckvn, ckv, gkv_ref[...])
        dgkv_ref[...] += dgkv
        dp_ref[:, OFF_CKV:OFF_CKV + LANES] = dckv.astype(BF16)
        dkr = dkm[:, 0:LANES] + dkm[:, LANES:2 * LANES] + dkm[:, 2 * LANES:3 * LANES] + dkm[:, 3 * LANES:HP]
        act = firstm[:, 0:LANES] | secondm[:, 0:LANES]
        dkr = jnp.where(act, dkr, 0.0)
        dkr = _rope_apply(dkr, cosm[:, 0:LANES], sinm[:, 0:LANES], firstm[:, 0:LANES], secondm[:, 0:LANES],
                          ROPE_MLA // 2, -1.0)
        dp_ref[:, OFF_KR:OFF_KR + LANES] = dkr.astype(BF16)
        f = ff_ref[...] + b_ref[...]
        dff = dlsf_ref[...] / (1.0 + jnp.exp(f))
        dbf_ref[...] += jnp.sum(dff, axis=0, keepdims=True)
        dp_ref[:, OFF_FF:OFF_FF + LANES] = dff.astype(BF16)
        dp_ref[:, OFF_FF + LANES:NP_IN] = jnp.zeros((TM, NP_IN - OFF_FF - LANES), BF16)

    def seg(off, w):
        return pl.BlockSpec((TM, w), lambda i, o=off // w: (i, o))

    def full(shape):
        return pl.BlockSpec(shape, lambda i: (0,) * len(shape))

    hp_spec = _row_spec(HP)
    in_specs = [hp_spec] * 13 + [_row_spec(LANES), seg(OFF_CQ, Q_RANK), seg(OFF_CKV, LANES), seg(OFF_FF, LANES),
                                 pl.BlockSpec((TM, 1), lambda i: (i, 0)),
                                 full((1, LANES)), full((1, HP)), full((1, LANES)), full((1, Q_RANK)),
                                 full((1, KV_RANK)), full((Q_RANK, HP)), full((KV_RANK, 2 * HP))]
    out_specs = [_row_spec(NP_IN), full((Q_RANK, HP)), full((KV_RANK, 2 * HP)), full((1, Q_RANK)),
                 full((1, KV_RANK)), full((1, LANES))]
    out_shape = [jax.ShapeDtypeStruct((t, NP_IN), BF16), jax.ShapeDtypeStruct((Q_RANK, HP), F32),
                 jax.ShapeDtypeStruct((KV_RANK, 2 * HP), F32), jax.ShapeDtypeStruct((1, Q_RANK), F32),
                 jax.ShapeDtypeStruct((1, KV_RANK), F32), jax.ShapeDtypeStruct((1, LANES), F32)]
    return pl.pallas_call(
        body, name=name, grid=(t // TM,), in_specs=in_specs, out_specs=out_specs, out_shape=out_shape,
        compiler_params=_cp("arbitrary"))(*dfox, *dret, *dsb, *dmla, drg, dlsf, proj, proj, proj, pos, invf,
                                          lg_lanes, b_pad, g_q, g_kv, wq_pad, wkv_pad)


TC = 1024


def _cumsum(x, reverse, name, partials=None, forget_bias=None):
    t = x.shape[0]
    w = LANES
    tc = min(TC, t)
    n = t // tc
    xs = [x] + ([] if partials is None else [partials]) + ([] if forget_bias is None else [forget_bias])

    def body(*refs):
        x_refs, o_ref, carry = refs[:len(xs)], refs[len(xs)], refs[len(xs) + 1]

        @pl.when(pl.program_id(0) == 0)
        def _():
            carry[...] = jnp.zeros_like(carry)

        r = lax.broadcasted_iota(jnp.int32, (tc, tc), 0)
        c = lax.broadcasted_iota(jnp.int32, (tc, tc), 1)
        tri = jnp.where((r <= c) if reverse else (r >= c), 1.0, 0.0).astype(BF16)
        v = x_refs[0][...]
        if forget_bias is not None:
            f = v + x_refs[-1][...]
            v = -(jnp.maximum(-f, 0.0) + jnp.log(1.0 + jnp.exp(-jnp.abs(f))))
        if partials is not None:
            lane = lax.broadcasted_iota(jnp.int32, (tc, LANES), 1)
            for hb in range(HEADS):
                v = v + jnp.where(lane == hb, jnp.sum(x_refs[1][:, _hs(hb)], axis=1, keepdims=True), 0.0)
        hi = v.astype(BF16)
        r1 = v - hi.astype(F32)
        mid = r1.astype(BF16)
        lo = (r1 - mid.astype(F32)).astype(BF16)
        cs = _bdot(tri, hi) + _bdot(tri, mid) + _bdot(tri, lo) + carry[...]
        o_ref[...] = cs
        carry[...] = cs[0:1, :] if reverse else cs[tc - 1:tc, :]

    step = (lambda i: n - 1 - i) if reverse else (lambda i: i)
    x_col = OFF_FF // LANES if forget_bias is not None else 0
    in_specs = [pl.BlockSpec((tc, LANES), lambda i: (step(i), x_col))]
    if partials is not None:
        in_specs.append(pl.BlockSpec((tc, HP), lambda i: (step(i), 0)))
    if forget_bias is not None:
        in_specs.append(_vec_spec(LANES))
    return pl.pallas_call(
        body, name=name, grid=(n,), in_specs=in_specs, out_specs=pl.BlockSpec((tc, w), lambda i: (step(i), 0)),
        out_shape=jax.ShapeDtypeStruct((t, w), F32), scratch_shapes=[pltpu.VMEM((1, w), F32)],
        compiler_params=_cp("arbitrary"))(*xs)


HB_FWD = 4
HB_BWD = 4
HB_SB_FWD = 4
BQS = 512
BQA = 512


def _q_spec(hb, bq=BQ):
    return pl.BlockSpec((bq, hb * LANES), lambda g, i: (i, g))


ONE_BUFFER = pl.Buffered(1)


def _kv_spec(t, which, hb):
    return pl.BlockSpec((t, hb * LANES), lambda g, i, w=which: (0, w * (HEADS // hb) + g), pipeline_mode=ONE_BUFFER)


def _acc_spec(t, hb):
    return pl.BlockSpec((t, hb * LANES), lambda g, i: (0, g), pipeline_mode=ONE_BUFFER)


def _hs(hh):
    return slice(hh * LANES, (hh + 1) * LANES)


def _tile_iota(rows, cols):
    return (lax.broadcasted_iota(jnp.int32, (rows, cols), 0), lax.broadcasted_iota(jnp.int32, (rows, cols), 1))


def _kvt_spec(nkv, width, which, hb):
    return pl.BlockSpec((nkv, hb * LANES, width), lambda g, i, w=which: (0, w * (HEADS // hb) + g, 0),
                        pipeline_mode=ONE_BUFFER)


def _qrow_spec(hb, bq=BQ):
    return pl.BlockSpec((hb, 1, 1, bq), lambda g, i: (g, i, 0, 0))


def _vis(key0, query0, rows, cols, kind):
    r, c = _tile_iota(rows, cols)
    k, q = key0 + r, query0 + c
    if kind == "chunk":
        return (k >> CHUNK_SHIFT) <= (q >> CHUNK_SHIFT)
    return (k < q) if kind == "strict" else (k <= q)


def _softmax_fwd(qkv, kvt, *, chunk_mask, scale, name):
    t = qkv.shape[0]
    nq = t // BQA
    per = BQA // BQ
    hb = HB_FWD
    kind = "chunk" if chunk_mask else "causal"

    def body(q_ref, k_ref, vt_ref, o_ref, lse_ref, m_sc, l_sc, acc_sc):
        i = pl.program_id(1)
        m_sc[...] = jnp.full((hb, 1, BQA), NEG, F32)
        l_sc[...] = jnp.zeros((hb, 1, BQA), F32)
        acc_sc[...] = jnp.zeros((hb, LANES, BQA), F32)

        def tile(j, qoff):
            off = pl.multiple_of(j * BQ, BQ)
            lo = 0 if qoff is None else qoff
            qs = slice(lo, BQA)
            vis = None if qoff is None else _vis(off, i * BQA + lo, BQ, BQA - lo, kind)
            ss = [_bdot(k_ref[pl.ds(off, BQ), _hs(hh)], q_ref[qs, _hs(hh)], NT) for hh in range(hb)]
            for hh in range(hb):
                s = ss[hh]
                if scale != 1.0:
                    s = s * scale
                if vis is not None:
                    s = jnp.where(vis, s, NEG)
                m_old = m_sc[hh, :, qs]
                m_new = jnp.maximum(m_old, jnp.max(s, axis=0, keepdims=True))
                alpha = jnp.exp(m_old - m_new)
                p = jnp.exp(s - m_new)
                l_sc[hh, :, qs] = alpha * l_sc[hh, :, qs] + jnp.sum(p, axis=0, keepdims=True)
                m_sc[hh, :, qs] = m_new
                acc_sc[hh, :, qs] = alpha * acc_sc[hh, :, qs] + _bdot(vt_ref[j, _hs(hh), :], p.astype(BF16))

        def loop(j, carry):
            tile(j, None)
            return carry

        lax.fori_loop(0, per * i, loop, 0)
        for d in range(per):
            tile(per * i + d, d * BQ)
        for hh in range(hb):
            l = l_sc[hh]
            o_ref[:, _hs(hh)] = (acc_sc[hh] / l).T
            lse_ref[hh, 0] = m_sc[hh] + jnp.log(l)

    return pl.pallas_call(
        body, name=name, grid=(HEADS // hb, nq),
        in_specs=[_q_spec(hb, BQA), _kv_spec(t, 1, hb), _kvt_spec(t // BQ, BQ, 1, hb)],
        out_specs=[_q_spec(hb, BQA), _qrow_spec(hb, BQA)],
        out_shape=[jax.ShapeDtypeStruct((t, HP), F32), jax.ShapeDtypeStruct((HEADS, nq, 1, BQA), F32)],
        scratch_shapes=[pltpu.VMEM((hb, 1, BQA), F32), pltpu.VMEM((hb, 1, BQA), F32),
                        pltpu.VMEM((hb, LANES, BQA), F32)],
        compiler_params=_cp("parallel", "arbitrary"))(qkv, qkv, kvt)


def _softmax_bwd(qkv, kvt, do, lse, delta, *, bias, chunk_mask, scale, name):
    t = qkv.shape[0]
    nq = t // BQA
    per = BQA // BQ
    hb = HB_BWD
    kind = "chunk" if chunk_mask else "causal"

    def body(*refs):
        if bias:
            (q_ref, k_ref, v_ref, kt_ref, do_ref, lse_ref, dl_ref, dq_ref, dk_ref, dv_ref, dck_ref, dcq_ref,
             dq_sc, dcq_sc) = refs
            dcq_sc[...] = jnp.zeros((hb, 1, BQA), F32)
        else:
            q_ref, k_ref, v_ref, kt_ref, do_ref, lse_ref, dl_ref, dq_ref, dk_ref, dv_ref, dq_sc = refs
        i = pl.program_id(1)

        @pl.when(i == 0)
        def _():
            dk_ref[...] = jnp.zeros_like(dk_ref)
            dv_ref[...] = jnp.zeros_like(dv_ref)
            if bias:
                dck_ref[...] = jnp.zeros_like(dck_ref)

        dq_sc[...] = jnp.zeros((hb, LANES, BQA), F32)

        def tile(j, qoff):
            off = pl.multiple_of(j * BQ, BQ)
            lo = 0 if qoff is None else qoff
            qsl = slice(lo, BQA)
            vis = None if qoff is None else _vis(off, i * BQA + lo, BQ, BQA - lo, kind)
            qs = [q_ref[qsl, _hs(hh)] for hh in range(hb)]
            dobs = [do_ref[qsl, _hs(hh)].astype(BF16) for hh in range(hb)]
            ss = [_bdot(k_ref[pl.ds(off, BQ), _hs(hh)], qs[hh], NT) for hh in range(hb)]
            dps = [_bdot(v_ref[pl.ds(off, BQ), _hs(hh)], dobs[hh], NT) for hh in range(hb)]
            pbs, dsbs = [], []
            for hh in range(hb):
                s = ss[hh]
                if scale != 1.0:
                    s = s * scale
                p = jnp.exp(s - lse_ref[hh, 0, :, qsl])
                if vis is not None:
                    p = jnp.where(vis, p, 0.0)
                ds = p * (dps[hh] - dl_ref[hh, 0, :, qsl])
                if bias:
                    part = ds[:, 0:LANES]
                    for b in range(1, (BQA - lo) // LANES):
                        part = part + ds[:, b * LANES:(b + 1) * LANES]
                    dck_ref[pl.ds(off, BQ), _hs(hh)] -= part
                    dcq_sc[hh, :, qsl] += jnp.sum(ds, axis=0, keepdims=True)
                if scale != 1.0:
                    ds = ds * scale
                pbs.append(p.astype(BF16))
                dsbs.append(ds.astype(BF16))
            for hh in range(hb):
                sl = _hs(hh)
                dv_ref[pl.ds(off, BQ), sl] += _bdot(pbs[hh], dobs[hh])
                dk_ref[pl.ds(off, BQ), sl] += _bdot(dsbs[hh], qs[hh])
                dq_sc[hh, :, qsl] += _bdot(kt_ref[j, sl, :], dsbs[hh])

        def loop(j, carry):
            tile(j, None)
            return carry

        lax.fori_loop(0, per * i, loop, 0)
        for d in range(per):
            tile(per * i + d, d * BQ)
        for hh in range(hb):
            dq_ref[:, _hs(hh)] = dq_sc[hh].T
            if bias:
                dcq_ref[hh, 0] = dcq_sc[hh]

    in_specs = [_q_spec(hb, BQA), _kv_spec(t, 1, hb), _kv_spec(t, 2, hb), _kvt_spec(t // BQ, BQ, 0, hb),
                _q_spec(hb, BQA), _qrow_spec(hb, BQA), _qrow_spec(hb, BQA)]
    out_specs = [_q_spec(hb, BQA), _acc_spec(t, hb), _acc_spec(t, hb)]
    out_shape = [jax.ShapeDtypeStruct((t, HP), F32)] * 3
    scratch = [pltpu.VMEM((hb, LANES, BQA), F32)]
    if bias:
        out_specs += [_acc_spec(t, hb), _qrow_spec(hb, BQA)]
        out_shape += [jax.ShapeDtypeStruct((t, HP), F32), jax.ShapeDtypeStruct((HEADS, nq, 1, BQA), F32)]
        scratch.append(pltpu.VMEM((hb, 1, BQA), F32))
    return pl.pallas_call(
        body, name=name, grid=(HEADS // hb, nq), in_specs=in_specs, out_specs=out_specs, out_shape=out_shape,
        scratch_shapes=scratch,
        compiler_params=_cp("parallel", "arbitrary"))(qkv, qkv, qkv, kvt, do, lse, delta)


def _ret_diag_decay(lg1, keys_on_rows=False):
    r, c = _tile_iota(BQ, BQ)
    qn, km = (c, r) if keys_on_rows else (r, c)
    dd = jnp.where(km > qn, jnp.exp((2.0 * lg1) * (km - qn).astype(F32)), 1.0)
    return jnp.where((km >> CHUNK_SHIFT) <= (qn >> CHUNK_SHIFT), dd, 0.0)


def _lg_spec(hb):
    return pl.BlockSpec((hb, 1, LANES), lambda g, i: (g, 0, 0))


def _ret_specs(nq, hb, reverse):
    tile = (lambda i: nq - 1 - i) if reverse else (lambda i: i)
    qkv = [pl.BlockSpec((BQ, hb * LANES), lambda g, i, w=w: (tile(i), w * (HEADS // hb) + g)) for w in range(3)]
    kt = pl.BlockSpec((1, hb * LANES, BQ), lambda g, i: (tile(i), g, 0))
    st = pl.BlockSpec((1, hb * LANES, LANES), lambda g, i: (tile(i), g, 0))
    return qkv, kt, st


def _ret_fwd(qkv, kt, lg_heads, name):
    t = qkv.shape[0]
    nq = t // BQ
    hb = HB_FWD

    def body(lg_ref, q_ref, k_ref, v_ref, kt_ref, o_ref, st_ref, s_sc):
        @pl.when(pl.program_id(1) == 0)
        def _():
            s_sc[...] = jnp.zeros_like(s_sc)

        qs = [q_ref[:, _hs(hh)] for hh in range(hb)]
        vs = [v_ref[:, _hs(hh)] for hh in range(hb)]
        aa = [_bdot(qs[hh], k_ref[:, _hs(hh)], NT) for hh in range(hb)]
        kv = [_bdot(kt_ref[0, _hs(hh), :], vs[hh]) for hh in range(hb)]
        for hh in range(hb):
            sl = _hs(hh)
            lg1 = lg_ref[hh][:, 0:1]
            s = s_sc[hh]
            st_ref[0, sl, :] = s
            shi, slo = _split2(s)
            a = (aa[hh] * _ret_diag_decay(lg1)).astype(BF16)
            o_ref[:, sl] = _bdot(a, vs[hh]) + _bdot(qs[hh], shi) + _bdot(qs[hh], slo)
            s_sc[hh] = jnp.exp(lg1 * float(BQ)) * (s + kv[hh])

    qkv_specs, kt_spec, st_spec = _ret_specs(nq, hb, False)
    return pl.pallas_call(
        body, name=name, grid=(HEADS // hb, nq), in_specs=[_lg_spec(hb)] + qkv_specs + [kt_spec],
        out_specs=[_q_spec(hb), st_spec],
        out_shape=[jax.ShapeDtypeStruct((t, HP), F32), jax.ShapeDtypeStruct((nq, HP, LANES), F32)],
        scratch_shapes=[pltpu.VMEM((hb, LANES, LANES), F32)],
        compiler_params=_cp("parallel", "arbitrary"))(lg_heads, qkv, qkv, qkv, kt)


def _ret_bwd(qkv, states, lg_heads, do, name):
    t = qkv.shape[0]
    nq = t // BQ
    hb = HB_FWD

    def body(lg_ref, q_ref, k_ref, v_ref, st_ref, do_ref, dq_ref, dk_ref, dv_ref, g_sc):
        @pl.when(pl.program_id(1) == 0)
        def _():
            g_sc[...] = jnp.zeros_like(g_sc)

        qs = [q_ref[:, _hs(hh)] for hh in range(hb)]
        ks = [k_ref[:, _hs(hh)] for hh in range(hb)]
        vs = [v_ref[:, _hs(hh)] for hh in range(hb)]
        dobs = [do_ref[:, _hs(hh)].astype(BF16) for hh in range(hb)]
        aa = [_bdot(ks[hh], qs[hh], NT) for hh in range(hb)]
        das = [_bdot(vs[hh], dobs[hh], NT) for hh in range(hb)]
        qdo = [_bdot(qs[hh], dobs[hh], TN) for hh in range(hb)]
        for hh in range(hb):
            sl = _hs(hh)
            lg1 = lg_ref[hh][:, 0:1]
            dd = _ret_diag_decay(lg1, keys_on_rows=True)
            at = (aa[hh] * dd).astype(BF16)
            dat = (das[hh] * dd).astype(BF16)
            h = jnp.exp(lg1 * float(BQ)) * g_sc[hh]
            hhi, hlo = _split2(h)
            shi, slo = _split2(st_ref[0, sl, :])
            dv_ref[:, sl] = _bdot(at, dobs[hh]) + _bdot(ks[hh], hhi) + _bdot(ks[hh], hlo)
            dk_ref[:, sl] = _bdot(dat, qs[hh]) + _bdot(vs[hh], hhi, NT) + _bdot(vs[hh], hlo, NT)
            dq_ref[:, sl] = _bdot(dat, ks[hh], TN) + _bdot(dobs[hh], shi, NT) + _bdot(dobs[hh], slo, NT)
            g_sc[hh] = qdo[hh] + h

    qkv_specs, _, st_spec = _ret_specs(nq, hb, True)
    tile_spec = pl.BlockSpec((BQ, hb * LANES), lambda g, i: (nq - 1 - i, g))
    return pl.pallas_call(
        body, name=name, grid=(HEADS // hb, nq), in_specs=[_lg_spec(hb)] + qkv_specs + [st_spec, tile_spec],
        out_specs=[tile_spec] * 3, out_shape=[jax.ShapeDtypeStruct((t, HP), F32)] * 3,
        scratch_shapes=[pltpu.VMEM((hb, LANES, LANES), F32)],
        compiler_params=_cp("parallel", "arbitrary"))(lg_heads, qkv, qkv, qkv, states, do)


def _sb_tile_logs(q, kb, vis):
    z = _bdot(kb, q, NT)
    ls = -(jnp.maximum(z, 0.0) + jnp.log(1.0 + jnp.exp(-jnp.abs(z))))
    if vis is not None:
        ls = jnp.where(vis, ls, 0.0)
    return z, ls


def _sb_later(ls, after):
    hi, lo = _split2(ls)
    return _bdot(after, hi) + _bdot(after, lo)


def _sb_fwd(qkv, kvt, name):
    t = qkv.shape[0]
    nq = t // BQS
    per = BQS // TKS
    hb = HB_SB_FWD

    def body(q_ref, k_ref, vt_ref, o_ref, tot_ref, acc_sc, r_sc):
        i = pl.program_id(1)
        acc_sc[...] = jnp.zeros((hb, LANES, BQS), F32)
        r_sc[...] = jnp.zeros((hb, 1, BQS), F32)
        mr, mc = _tile_iota(TKS, TKS)
        after = jnp.where(mc > mr, 1.0, 0.0).astype(BF16)

        def tile(j, qoff):
            off = pl.multiple_of(j * TKS, TKS)
            lo = 0 if qoff is None else qoff
            qsl = slice(lo, BQS)
            vis = None if qoff is None else _vis(off, i * BQS + lo, TKS, BQS - lo, "strict")
            zl = [_sb_tile_logs(q_ref[qsl, _hs(hh)], k_ref[pl.ds(off, TKS), _hs(hh)], vis) for hh in range(hb)]
            laters = [_sb_later(zl[hh][1], after) for hh in range(hb)]
            ws = []
            for hh in range(hb):
                z, ls = zl[hh]
                w = jnp.exp(z + ls + laters[hh] + r_sc[hh, :, qsl])
                if vis is not None:
                    w = jnp.where(vis, w, 0.0)
                ws.append(w.astype(BF16))
                r_sc[hh, :, qsl] += jnp.sum(ls, axis=0, keepdims=True)
            for hh in range(hb):
                acc_sc[hh, :, qsl] += _bdot(vt_ref[j, _hs(hh), :], ws[hh])

        for d in reversed(range(per)):
            tile(per * i + d, d * TKS)

        def loop(jj, carry):
            tile(per * i - 1 - jj, None)
            return carry

        lax.fori_loop(0, per * i, loop, 0)
        for hh in range(hb):
            o_ref[:, _hs(hh)] = acc_sc[hh].T
            tot_ref[hh, 0] = r_sc[hh]

    return pl.pallas_call(
        body, name=name, grid=(HEADS // hb, nq),
        in_specs=[_q_spec(hb, BQS), _kv_spec(t, 1, hb), _kvt_spec(t // TKS, TKS, 1, hb)],
        out_specs=[_q_spec(hb, BQS), _qrow_spec(hb, BQS)],
        out_shape=[jax.ShapeDtypeStruct((t, HP), F32), jax.ShapeDtypeStruct((HEADS, nq, 1, BQS), F32)],
        scratch_shapes=[pltpu.VMEM((hb, LANES, BQS), F32), pltpu.VMEM((hb, 1, BQS), F32)],
        compiler_params=_cp("parallel", "arbitrary"))(qkv, qkv, kvt)


def _sb_bwd(qkv, kvt, do, tot, name):
    t = qkv.shape[0]
    nq = t // BQS
    per = BQS // TKS
    hb = HB_BWD

    def body(q_ref, k_ref, v_ref, kt_ref, do_ref, tot_ref, dq_ref, dk_ref, dv_ref, dq_sc, p_sc, g_sc):
        i = pl.program_id(1)

        @pl.when(i == 0)
        def _():
            dk_ref[...] = jnp.zeros_like(dk_ref)
            dv_ref[...] = jnp.zeros_like(dv_ref)

        dq_sc[...] = jnp.zeros((hb, LANES, BQS), F32)
        p_sc[...] = jnp.zeros((hb, 1, BQS), F32)
        g_sc[...] = jnp.zeros((hb, 1, BQS), F32)
        mr, mc = _tile_iota(TKS, TKS)
        after = jnp.where(mc > mr, 1.0, 0.0).astype(BF16)
        before = jnp.where(mc < mr, 1.0, 0.0).astype(BF16)

        def tile(j, qoff):
            off = pl.multiple_of(j * TKS, TKS)
            lo = 0 if qoff is None else qoff
            qsl = slice(lo, BQS)
            vis = None if qoff is None else _vis(off, i * BQS + lo, TKS, BQS - lo, "strict")
            qs = [q_ref[qsl, _hs(hh)] for hh in range(hb)]
            dobs = [do_ref[qsl, _hs(hh)].astype(BF16) for hh in range(hb)]
            zl = [_sb_tile_logs(qs[hh], k_ref[pl.ds(off, TKS), _hs(hh)], vis) for hh in range(hb)]
            dws = [_bdot(v_ref[pl.ds(off, TKS), _hs(hh)], dobs[hh], NT) for hh in range(hb)]
            laters = [_sb_later(zl[hh][1], after) for hh in range(hb)]
            ws, gs = [], []
            for hh in range(hb):
                z, ls = zl[hh]
                own = jnp.sum(ls, axis=0, keepdims=True)
                rest = tot_ref[hh, 0, :, qsl] - p_sc[hh, :, qsl] - own
                w = jnp.exp(z + ls + laters[hh] + rest)
                if vis is not None:
                    w = jnp.where(vis, w, 0.0)
                p_sc[hh, :, qsl] += own
                ws.append(w.astype(BF16))
                gs.append(dws[hh] * w)
            gins = [_bdot(before, gs[hh].astype(BF16)) for hh in range(hb)]
            dzbs = []
            for hh in range(hb):
                g = gs[hh]
                stay = jnp.exp(zl[hh][1])
                dz = g * stay - (1.0 - stay) * (gins[hh] + g_sc[hh, :, qsl])
                if vis is not None:
                    dz = jnp.where(vis, dz, 0.0)
                g_sc[hh, :, qsl] += jnp.sum(g, axis=0, keepdims=True)
                dzbs.append(dz.astype(BF16))
            for hh in range(hb):
                sl = _hs(hh)
                dv_ref[pl.ds(off, TKS), sl] += _bdot(ws[hh], dobs[hh])
                dk_ref[pl.ds(off, TKS), sl] += _bdot(dzbs[hh], qs[hh])
                dq_sc[hh, :, qsl] += _bdot(kt_ref[j, sl, :], dzbs[hh])

        def loop(j, carry):
            tile(j, None)
            return carry

        lax.fori_loop(0, per * i, loop, 0)
        for d in range(per):
            tile(per * i + d, d * TKS)
        for hh in range(hb):
            dq_ref[:, _hs(hh)] = dq_sc[hh].T

    return pl.pallas_call(
        body, name=name, grid=(HEADS // hb, nq),
        in_specs=[_q_spec(hb, BQS), _kv_spec(t, 1, hb), _kv_spec(t, 2, hb), _kvt_spec(t // TKS, TKS, 0, hb),
                  _q_spec(hb, BQS), _qrow_spec(hb, BQS)],
        out_specs=[_q_spec(hb, BQS), _acc_spec(t, hb), _acc_spec(t, hb)],
        out_shape=[jax.ShapeDtypeStruct((t, HP), F32)] * 3,
        scratch_shapes=[pltpu.VMEM((hb, LANES, BQS), F32), pltpu.VMEM((hb, 1, BQS), F32),
                        pltpu.VMEM((hb, 1, BQS), F32)],
        compiler_params=_cp("parallel", "arbitrary"))(qkv, qkv, qkv, kvt, do, tot)


def _sigmoid(v):
    return 1.0 / (1.0 + jnp.exp(-v))


def _post_fwd(oa, ob, oc, od, proj, g_pad, name):
    t = oa.shape[0]

    def body(oa_ref, ob_ref, oc_ref, od_ref, rg_ref, g_ref, mx_ref):
        g = g_ref[...]

        def group(o, gg):
            r = lax.rsqrt(jnp.sum(o * o, axis=-1, keepdims=True) * (1.0 / GROUP) + EPS)
            return _gather_heads(o * r * gg).astype(BF16)

        mx_ref[:, 0:GROUP] = group(oa_ref[...], g[:, 0:HP])
        mx_ref[:, GROUP:2 * GROUP] = group(ob_ref[...], g[:, HP:2 * HP])
        mx_ref[:, 3 * GROUP:4 * GROUP] = group(od_ref[...], g[:, 3 * HP:4 * HP])
        real = lax.broadcasted_iota(jnp.int32, (TM, LANES), 1) < HEAD_DIM
        rg = _spread_heads(rg_ref[...])
        gated = []
        for hb in range(HEADS):
            sl = slice(hb * LANES, (hb + 1) * LANES)
            o = oc_ref[:, sl]
            mu = jnp.sum(o, axis=-1, keepdims=True) * (1.0 / HEAD_DIM)
            dlt = jnp.where(real, o - mu, 0.0)
            var = jnp.sum(dlt * dlt, axis=-1, keepdims=True) * (1.0 / HEAD_DIM)
            yn = dlt * lax.rsqrt(var + EPS) * g[:, 2 * HP + hb * LANES:2 * HP + (hb + 1) * LANES]
            x = rg[:, sl]
            gated.append(yn * (x * _sigmoid(x)))
        mx_ref[:, 2 * GROUP:3 * GROUP] = _gather_heads(jnp.concatenate(gated, axis=1)).astype(BF16)

    rg_spec = pl.BlockSpec((TM, GROUP), lambda i: (i, OFF_RG // GROUP))
    return pl.pallas_call(
        body, name=name, grid=(t // TM,),
        in_specs=[_row_spec(HP)] * 4 + [rg_spec, _vec_spec(4 * HP)], out_specs=_row_spec(D_MODEL),
        out_shape=jax.ShapeDtypeStruct((t, D_MODEL), BF16), compiler_params=_cp("parallel"))(oa, ob, oc, od, proj, g_pad)


def _post_bwd(dmx, oa, ob, oc, od, proj, g_pad, name):
    t = oa.shape[0]

    def body(dm_ref, oa_ref, ob_ref, oc_ref, od_ref, rg_ref, g_ref,
             doa_ref, dob_ref, doc_ref, dod_ref, dla_ref, dlb_ref, drg_ref, dg_ref):
        @pl.when(pl.program_id(0) == 0)
        def _():
            dg_ref[...] = jnp.zeros_like(dg_ref)

        g = g_ref[...]

        def group_bwd(dm, o, gg):
            r = lax.rsqrt(jnp.sum(o * o, axis=-1, keepdims=True) * (1.0 / GROUP) + EPS)
            oh = o * r
            dgp = jnp.sum(dm * oh, axis=0, keepdims=True)
            dyh = dm * gg
            do = r * (dyh - oh * (jnp.sum(dyh * oh, axis=-1, keepdims=True) * (1.0 / GROUP)))
            return do, dgp

        def delta_bc(do, o):
            prod = do * o
            lane = lax.broadcasted_iota(jnp.int32, (TM, LANES), 1)
            out = jnp.zeros((TM, LANES), F32)
            for hb in range(HEADS):
                out = jnp.where(lane == hb, jnp.sum(prod[:, hb * LANES:(hb + 1) * LANES], axis=-1, keepdims=True), out)
            return out

        dmp = [_spread_heads(dm_ref[:, gi * GROUP:(gi + 1) * GROUP]) for gi in range(4)]
        rg = _spread_heads(rg_ref[...])
        oa = oa_ref[...]
        do_a, dga = group_bwd(dmp[0], oa, g[:, 0:HP])
        doa_ref[...] = do_a.astype(BF16)
        dla_ref[...] = delta_bc(do_a, oa)
        dg_ref[:, 0:HP] += dga
        ob = ob_ref[...]
        do_b, dgb = group_bwd(dmp[1], ob, g[:, HP:2 * HP])
        dob_ref[...] = do_b.astype(BF16)
        dlb_ref[...] = delta_bc(do_b, ob)
        dg_ref[:, HP:2 * HP] += dgb
        do_d, dgd = group_bwd(dmp[3], od_ref[...], g[:, 3 * HP:4 * HP])
        dod_ref[...] = do_d.astype(BF16)
        dg_ref[:, 3 * HP:4 * HP] += dgd
        real = lax.broadcasted_iota(jnp.int32, (TM, LANES), 1) < HEAD_DIM
        for hb in range(HEADS):
            sl = slice(hb * LANES, (hb + 1) * LANES)
            gsl = slice(2 * HP + hb * LANES, 2 * HP + (hb + 1) * LANES)
            o = oc_ref[:, sl]
            mu = jnp.sum(o, axis=-1, keepdims=True) * (1.0 / HEAD_DIM)
            dlt = jnp.where(real, o - mu, 0.0)
            var = jnp.sum(dlt * dlt, axis=-1, keepdims=True) * (1.0 / HEAD_DIM)
            rstd = lax.rsqrt(var + EPS)
            dhat = dlt * rstd
            gc = g[:, gsl]
            x = rg[:, sl]
            sg = _sigmoid(x)
            dm = dmp[2][:, sl]
            drg_ref[:, sl] = dm * (dhat * gc) * (sg * (1.0 + x * (1.0 - sg)))
            dyn = dm * (x * sg)
            dg_ref[:, gsl] += jnp.sum(dyn * dhat, axis=0, keepdims=True)
            ddh = dyn * gc
            m1 = jnp.sum(ddh, axis=-1, keepdims=True) * (1.0 / HEAD_DIM)
            m2 = jnp.sum(ddh * dhat, axis=-1, keepdims=True) * (1.0 / HEAD_DIM)
            doc_ref[:, sl] = jnp.where(real, rstd * (ddh - m1 - dhat * m2), 0.0).astype(BF16)

    rg_spec = pl.BlockSpec((TM, GROUP), lambda i: (i, OFF_RG // GROUP))
    hp = _row_spec(HP)
    return pl.pallas_call(
        body, name=name, grid=(t // TM,),
        in_specs=[_row_spec(D_MODEL), hp, hp, hp, hp, rg_spec, _vec_spec(4 * HP)],
        out_specs=[hp] * 4 + [_row_spec(LANES)] * 2 + [hp, _vec_spec(4 * HP)],
        out_shape=[jax.ShapeDtypeStruct((t, HP), BF16)] * 4 + [jax.ShapeDtypeStruct((t, LANES), F32)] * 2
        + [jax.ShapeDtypeStruct((t, HP), F32), jax.ShapeDtypeStruct((1, 4 * HP), F32)],
        compiler_params=_cp("arbitrary"))(dmx, oa, ob, oc, od, proj, g_pad)


def _mesh_pos():
    return lax.axis_index("x"), lax.axis_index("y"), lax.axis_index("c")


def _peer(pos, k):
    x, y, c = pos
    px = 1 - x if (k >> 2) & 1 else x
    py = 1 - y if (k >> 1) & 1 else y
    pc = 1 - c if k & 1 else c
    return (px, py, pc), 4 * px + 2 * py + pc


def _exchange(arrs, gather, name):
    n = len(arrs)

    def body(*refs):
        ins, outs = refs[:n], refs[n:2 * n]
        send_sems, recv_sems, loc_sems = refs[2 * n:]
        pos = _mesh_pos()
        me = 4 * pos[0] + 2 * pos[1] + pos[2]
        local = []
        for a in range(n):
            src = ins[a] if gather else ins[a].at[me]
            cp = pltpu.make_async_copy(src, outs[a].at[me], loc_sems.at[a])
            cp.start()
            local.append(cp)
        sends, recvs = [], []
        for k in range(1, N_DEV):
            peer, pid = _peer(pos, k)
            for a in range(n):
                s = a * (N_DEV - 1) + k - 1
                src = ins[a] if gather else ins[a].at[pid]
                cp = pltpu.make_async_remote_copy(
                    src_ref=src, dst_ref=outs[a].at[me], send_sem=send_sems.at[s], recv_sem=recv_sems.at[s],
                    device_id=peer, device_id_type=pl.DeviceIdType.MESH)
                cp.start()
                sends.append(cp)
                recvs.append(pltpu.make_async_remote_copy(
                    src_ref=src, dst_ref=outs[a].at[pid], send_sem=send_sems.at[s], recv_sem=recv_sems.at[s],
                    device_id=peer, device_id_type=pl.DeviceIdType.MESH))
        for cp in recvs:
            cp.wait_recv()
        for cp in sends:
            cp.wait_send()
        for cp in local:
            cp.wait()

    any_spec = pl.BlockSpec(memory_space=pl.ANY)
    out_shape = [jax.ShapeDtypeStruct((N_DEV,) + tuple(a.shape) if gather else tuple(a.shape), a.dtype) for a in arrs]
    return pl.pallas_call(
        body, name=name, in_specs=[any_spec] * n, out_specs=[any_spec] * n, out_shape=out_shape,
        scratch_shapes=[pltpu.SemaphoreType.DMA((n * (N_DEV - 1),)), pltpu.SemaphoreType.DMA((n * (N_DEV - 1),)),
                        pltpu.SemaphoreType.DMA((n,))],
        compiler_params=pltpu.CompilerParams(has_side_effects=True))(*arrs)


def _device_index():
    x, y, c = _mesh_pos()
    return 4 * x + 2 * y + c


def _landing(srcs, gather):
    me = _device_index()
    lands = []
    for a in srcs:
        own = a[None] if gather else lax.dynamic_slice_in_dim(a, me, 1, axis=0)
        shape = (N_DEV,) + tuple(a.shape) if gather else tuple(a.shape)
        lands.append(lax.dynamic_update_slice_in_dim(lax.empty(shape, a.dtype), own, me, axis=0))
    return lands


def _exchange_copies(ins, lands, send_sems, recv_sems, gather):
    pos = _mesh_pos()
    me = 4 * pos[0] + 2 * pos[1] + pos[2]
    sends, recvs = [], []
    for k in range(1, N_DEV):
        peer, pid = _peer(pos, k)
        for a in range(len(ins)):
            s = a * (N_DEV - 1) + k - 1
            src = ins[a] if gather else ins[a].at[pid]
            sends.append(pltpu.make_async_remote_copy(
                src_ref=src, dst_ref=lands[a].at[me], send_sem=send_sems.at[s], recv_sem=recv_sems.at[s],
                device_id=peer, device_id_type=pl.DeviceIdType.MESH))
            recvs.append(pltpu.make_async_remote_copy(
                src_ref=src, dst_ref=lands[a].at[pid], send_sem=send_sems.at[s], recv_sem=recv_sems.at[s],
                device_id=peer, device_id_type=pl.DeviceIdType.MESH))
    return sends, recvs


def _exchange_start(srcs, gather, name, after=None):
    n = len(srcs)
    lands = _landing(srcs, gather)
    nsem = n * (N_DEV - 1)
    extra = [] if after is None else [after]

    def body(*refs):
        ins, lnd = refs[:n], refs[n:2 * n]
        send_sems, recv_sems = refs[2 * n + len(extra)], refs[2 * n + len(extra) + 1]
        token = refs[-1]
        sends, _ = _exchange_copies(ins, lnd, send_sems, recv_sems, gather)
        for cp in sends:
            cp.start()
        token[...] = jnp.zeros_like(token)

    hbm = pl.BlockSpec(memory_space=pltpu.HBM)
    sem = pl.BlockSpec(memory_space=pltpu.SEMAPHORE)
    bufs = list(srcs) + lands
    out_shape = ([pltpu.SemaphoreType.DMA((nsem,)), pltpu.SemaphoreType.DMA((nsem,))]
                 + [pltpu.HBM(b.shape, b.dtype) for b in bufs] + [jax.ShapeDtypeStruct((8, LANES), F32)])
    outs = pl.pallas_call(
        body, name=name, in_specs=[hbm] * (2 * n) + [pl.BlockSpec(memory_space=pl.ANY)] * len(extra),
        out_specs=[sem, sem] + [hbm] * (2 * n) + [pl.BlockSpec(memory_space=pltpu.VMEM)], out_shape=out_shape,
        input_output_aliases={i: 2 + i for i in range(2 * n)},
        compiler_params=pltpu.CompilerParams(has_side_effects=pltpu.SideEffectType.DATAFLOW_SIDE_EFFECTING),
    )(*[pltpu.with_memory_space_constraint(b, pltpu.HBM) for b in bufs], *extra)
    return (outs[0], outs[1], outs[2:2 + n], outs[2 + n:2 + 2 * n]), outs[-1]


def _exchange_wait(state, after, gather, name):
    send_sems, recv_sems, srcs, lands = state
    n = len(srcs)
    after = list(after) if isinstance(after, (list, tuple)) else [after]

    def body(*refs):
        ins, lnd = refs[:n], refs[n:2 * n]
        s_sems, r_sems = refs[2 * n], refs[2 * n + 1]
        sends, recvs = _exchange_copies(ins, lnd, s_sems, r_sems, gather)
        for cp in sends:
            cp.wait_send()
        for cp in recvs:
            cp.wait_recv()

    hbm = pl.BlockSpec(memory_space=pltpu.HBM)
    sem = pl.BlockSpec(memory_space=pltpu.SEMAPHORE)
    bufs = list(srcs) + list(lands)
    outs = pl.pallas_call(
        body, name=name, in_specs=[hbm] * (2 * n) + [sem, sem] + [pl.BlockSpec(memory_space=pl.ANY)] * len(after),
        out_specs=[hbm] * (2 * n), out_shape=[pltpu.HBM(b.shape, b.dtype) for b in bufs],
        input_output_aliases={i: i for i in range(2 * n)},
        compiler_params=pltpu.CompilerParams(has_side_effects=pltpu.SideEffectType.DATAFLOW_SIDE_EFFECTING),
    )(*bufs, send_sems, recv_sems, *after)
    return outs[n:]


def _adam_vals(w, g, m, v):
    m = ADAM_B1 * m + (1.0 - ADAM_B1) * g
    v = ADAM_B2 * v + (1.0 - ADAM_B2) * (g * g)
    m_hat = m / ADAM_C1
    v_hat = v / ADAM_C2
    delta = -ADAM_LR * (m_hat / (jnp.sqrt(v_hat) + ADAM_EPS) + ADAM_WD * w)
    return delta, m, v


def _small_allreduce_adam(part, w, m, v, name):
    rows = part.shape[0]

    def body(p_ref, w_ref, m_ref, v_ref, g_ref, d_ref, nm_ref, nv_ref, gath, send_sems, recv_sems):
        pos = _mesh_pos()
        me = 4 * pos[0] + 2 * pos[1] + pos[2]
        gath[me] = p_ref[...]
        sends, recvs = [], []
        for k in range(1, N_DEV):
            peer, pid = _peer(pos, k)
            cp = pltpu.make_async_remote_copy(
                src_ref=p_ref, dst_ref=gath.at[me], send_sem=send_sems.at[k - 1], recv_sem=recv_sems.at[k - 1],
                device_id=peer, device_id_type=pl.DeviceIdType.MESH)
            cp.start()
            sends.append(cp)
            recvs.append(pltpu.make_async_remote_copy(
                src_ref=p_ref, dst_ref=gath.at[pid], send_sem=send_sems.at[k - 1], recv_sem=recv_sems.at[k - 1],
                device_id=peer, device_id_type=pl.DeviceIdType.MESH))
        for cp in recvs:
            cp.wait_recv()
        for cp in sends:
            cp.wait_send()
        g = gath[0]
        for p in range(1, N_DEV):
            g = g + gath[p]
        g_ref[...] = g
        d, nm, nv = _adam_vals(w_ref[...], g, m_ref[...], v_ref[...])
        d_ref[...] = d
        nm_ref[...] = nm
        nv_ref[...] = nv

    vm = pl.BlockSpec(memory_space=pltpu.VMEM)
    sds = jax.ShapeDtypeStruct((rows, LANES), F32)
    return pl.pallas_call(
        body, name=name, in_specs=[vm] * 4, out_specs=[vm] * 4, out_shape=[sds] * 4,
        scratch_shapes=[pltpu.VMEM((N_DEV, rows, LANES), F32), pltpu.SemaphoreType.DMA((N_DEV - 1,)),
                        pltpu.SemaphoreType.DMA((N_DEV - 1,))],
        compiler_params=pltpu.CompilerParams(has_side_effects=True))(part, w, m, v)


def _reduce_adam(recvs, w, m, v, name):
    shape = w.shape
    c = shape[-1]
    r = int(np.prod(shape[1:-1]))
    rs = [a.reshape(N_DEV, r, c) for a in recvs]
    w2, m2, v2 = w.reshape(DEPTH * r, c), m.reshape(DEPTH * r, c), v.reshape(DEPTH * r, c)
    tr = r
    while tr * c * 4 > (1 << 20) and tr % 16 == 0:
        tr //= 2
    nt = r // tr

    def body(*refs):
        r_refs = refs[:DEPTH]
        w_ref, m_ref, v_ref, g_ref, d_ref, nm_ref, nv_ref = refs[DEPTH:]
        for l, r_ref in enumerate(r_refs):
            @pl.when(pl.program_id(0) // nt == l)
            def _(r_ref=r_ref):
                g = r_ref[0].astype(F32)
                for p in range(1, N_DEV):
                    g = g + r_ref[p].astype(F32)
                g_ref[...] = g
                d, nm, nv = _adam_vals(w_ref[...], g, m_ref[...], v_ref[...])
                d_ref[...] = d
                nm_ref[...] = nm
                nv_ref[...] = nv

    recv_specs = [pl.BlockSpec((N_DEV, tr, c), lambda i, l=l: (0, jnp.clip(i - l * nt, 0, nt - 1), 0))
                  for l in range(DEPTH)]
    spec = pl.BlockSpec((tr, c), lambda i: (i, 0))
    sds = jax.ShapeDtypeStruct((DEPTH * r, c), F32)
    outs = pl.pallas_call(
        body, name=name, grid=(DEPTH * nt,), in_specs=recv_specs + [spec, spec, spec],
        out_specs=[spec] * 4, out_shape=[sds] * 4, compiler_params=_cp("arbitrary"))(*rs, w2, m2, v2)
    return [o.reshape(shape) for o in outs]


def _pad_heads(w, real=HEAD_DIM):
    lead = w.shape[:-1]
    w = w.reshape(lead + (HEADS, real))
    w = jnp.pad(w, [(0, 0)] * len(lead) + [(0, 0), (0, LANES - real)])
    return w.reshape(lead + (HP,))


def _unpad_heads(w, real=HEAD_DIM):
    lead = w.shape[:-1]
    return w.reshape(lead + (HEADS, LANES))[..., :real].reshape(lead + (HEADS * real,))


_IN_SEGS = (("fq", 0, 256), ("fk", 256, 512), ("fv", 512, 768), ("ff", 768, 772), ("cq", 772, 1028),
            ("ckv", 1028, 1156), ("kr", 1156, 1188), ("rq", 1188, 1444), ("rk", 1444, 1700), ("rv", 1700, 1956),
            ("rg", 1956, 2212), ("sq", 2212, 2468), ("sk", 2468, 2724), ("sv", 2724, 2980))


def _pad_w_in(w):
    s = {n: w[:, a:b] for n, a, b in _IN_SEGS}
    rows = w.shape[0]
    z = lambda n: jnp.zeros((rows, n), w.dtype)
    parts = [s[n] for n in ("fq", "fk", "fv", "rq", "rk", "rv", "sq", "sk", "sv", "rg", "cq", "ckv")]
    parts += [z(HEAD_DIM), s["kr"], z(LANES - HEAD_DIM - ROPE_MLA), s["ff"], z(LANES - HEADS),
              z(NP_IN - OFF_FF - LANES)]
    return jnp.concatenate(parts, axis=1)


def _unpad_w_in(wp):
    seg = lambda off, n=GROUP: wp[:, off:off + n]
    parts = [seg(OFF_FOX), seg(OFF_FOX + GROUP), seg(OFF_FOX + 2 * GROUP), seg(OFF_FF, HEADS),
             seg(OFF_CQ, Q_RANK), seg(OFF_CKV, KV_RANK), seg(OFF_KR + HEAD_DIM, ROPE_MLA),
             seg(OFF_RET), seg(OFF_RET + GROUP), seg(OFF_RET + 2 * GROUP), seg(OFF_RG),
             seg(OFF_SB), seg(OFF_SB + GROUP), seg(OFF_SB + 2 * GROUP)]
    return jnp.concatenate(parts, axis=1)


def _pad_w_kv(w):
    w4 = w.reshape(KV_RANK, HEADS, 2 * HEAD_DIM)
    k = w4[:, :, :HEAD_DIM].reshape(KV_RANK, GROUP)
    v = w4[:, :, HEAD_DIM:].reshape(KV_RANK, GROUP)
    return jnp.concatenate([_pad_heads(k), _pad_heads(v)], axis=1)


def _unpad_w_kv(wp):
    k = _unpad_heads(wp[:, :HP]).reshape(KV_RANK, HEADS, HEAD_DIM)
    v = _unpad_heads(wp[:, HP:]).reshape(KV_RANK, HEADS, HEAD_DIM)
    return jnp.concatenate([k, v], axis=-1).reshape(KV_RANK, HEADS * 2 * HEAD_DIM)


def _pad_gain_out(g):
    g = jnp.pad(g.reshape(4 * HEADS, HEAD_DIM), ((0, 0), (0, LANES - HEAD_DIM)))
    return g.reshape(1, 4 * HP)


def _unpad_gain_out(gp):
    return gp.reshape(4 * HEADS, LANES)[:, :HEAD_DIM].reshape(D_MODEL)


_SMALL = (("g_mix_pre", 1024), ("g_mix_post", 1024), ("g_ffn_pre", 1024), ("g_ffn_post", 1024), ("g_mix_out", 1024),
          ("g_q_lora", 256), ("g_kv_lora", 128), ("b_forget", 4))


def _pack_small(vals):
    parts = []
    for name, n in _SMALL:
        a = vals[name].astype(F32)
        if n < LANES:
            a = jnp.pad(a, ((0, 0), (0, LANES - n)))
        parts.append(a)
    return jnp.concatenate(parts, axis=1).reshape(DEPTH * SMALL_ROWS, LANES)


def _unpack_small(packed):
    flat = packed.reshape(DEPTH, SMALL_ROWS * LANES)
    out, off = {}, 0
    for name, n in _SMALL:
        out[name] = flat[:, off:off + n]
        off += max(n, LANES)
    return out


def kernel(x, positions, g_mix_pre, w_in, b_forget, g_q_lora, w_q_up, g_kv_lora, w_kv_up, g_mix_out, w_out, g_mix_post, g_ffn_pre, w_ffn_up, w_ffn_down, g_ffn_post, loss_target, m_g_mix_pre, m_w_in, m_b_forget, m_g_q_lora, m_w_q_up, m_g_kv_lora, m_w_kv_up, m_g_mix_out, m_w_out, m_g_mix_post, m_g_ffn_pre, m_w_ffn_up, m_w_ffn_down, m_g_ffn_post, v_g_mix_pre, v_w_in, v_b_forget, v_g_q_lora, v_w_q_up, v_g_kv_lora, v_w_kv_up, v_g_mix_out, v_w_out, v_g_mix_post, v_g_ffn_pre, v_w_ffn_up, v_w_ffn_down, v_g_ffn_post):
    t = x.shape[1]
    nq = t // BQ
    x0 = x[0]
    tgt = loss_target[0]
    pos = positions[0].astype(F32).reshape(t, 1)

    half_r, half_m = HEAD_DIM // 2, ROPE_MLA // 2
    invf_r = ROPE_BASE ** (-jnp.arange(half_r, dtype=F32) / half_r)
    invf_m = ROPE_BASE ** (-jnp.arange(half_m, dtype=F32) / half_m)
    invf = jnp.concatenate([invf_r, invf_r, invf_m, invf_m,
                            jnp.zeros((LANES - HEAD_DIM - ROPE_MLA,), F32)]).reshape(1, LANES)
    log_gamma = jnp.log1p(-jnp.power(2.0, -5.0 - jnp.arange(HEADS, dtype=F32)))
    lg_lanes = jnp.repeat(log_gamma, LANES).reshape(1, HP)
    lg_heads = jnp.broadcast_to(log_gamma[:, None, None], (HEADS, 1, LANES))

    big = [w_in, w_q_up, w_kv_up, w_out, w_ffn_up, w_ffn_down]
    bf = lambda w: w.astype(BF16)
    first = _exchange([bf(w_in[0]), bf(w_q_up), bf(w_kv_up)], True, "weights_gather_first")
    l0_state, l0_token = _exchange_start([bf(w[0]) for w in (w_out, w_ffn_up, w_ffn_down)], True,
                                         "weights_gather_layer0_start")
    l1_state, rest_token = _exchange_start([bf(w[1]) for w in (w_in, w_out, w_ffn_up, w_ffn_down)], True,
                                           "weights_gather_layer1_start", after=l0_token)
    wq_g = first[1].transpose(1, 2, 0, 3).reshape(DEPTH, Q_RANK, 384)
    wkv_g = first[2].transpose(1, 2, 0, 3).reshape(DEPTH, KV_RANK, 512)

    row = lambda g: g.reshape(1, -1)
    layers = []
    for l in range(DEPTH):
        layers.append(dict(
            wq=_pad_heads(wq_g[l], 96), wkv=_pad_w_kv(wkv_g[l]),
            g_pre=row(g_mix_pre[l]), g_post=row(g_mix_post[l]), g_fpre=row(g_ffn_pre[l]), g_fpost=row(g_ffn_post[l]),
            g_out=_pad_gain_out(g_mix_out[l]), g_q=row(g_q_lora[l]), g_kv=row(g_kv_lora[l]),
            b_pad=jnp.pad(b_forget[l], (0, LANES - HEADS)).reshape(1, LANES)))
    layers[0]["win"] = _pad_w_in(first[0].reshape(D_MODEL, D_IN))

    saved = []
    xin = x0
    h = _rms_fwd(xin, layers[0]["g_pre"] + rest_token[0:1, 0:1], "rms_pre_0")
    loss_row = dx = None
    for l, p in enumerate(layers):
        s = dict(x=xin, h=h)
        proj = _mm(h, p["win"], name=f"in_proj_{l}", tm=512, tn=NP_IN)
        cum = _cumsum(proj, False, f"forget_cumsum_{l}", forget_bias=p["b_pad"])
        fox, ret, sb, mla, fox_t, sb_t, mla_t, ret_t = _prep_fwd(proj, cum, pos, invf, lg_lanes, p["g_q"],
                                                          p["g_kv"], p["wq"], p["wkv"], f"prep_fwd_{l}")
        oa, lse_a = _softmax_fwd(fox, fox_t, chunk_mask=False, scale=1.0, name=f"fox_fwd_{l}")
        ob, lse_b = _softmax_fwd(mla, mla_t, chunk_mask=True, scale=1.0, name=f"mla_fwd_{l}")
        oc, ret_st = _ret_fwd(ret, ret_t, lg_heads, f"ret_fwd_{l}")
        od, sb_tot = _sb_fwd(sb, sb_t, f"sb_fwd_{l}")
        if l == 0:
            got = _exchange_wait(l0_state, od, True, "weights_gather_layer0_wait")
            p.update(wout=got[0].reshape(D_MODEL, D_MODEL), wup=got[1].transpose(1, 0, 2).reshape(D_MODEL, D_FF),
                     wdn=got[2].reshape(D_FF, D_MODEL))
        mixed = _post_fwd(oa, ob, oc, od, proj, p["g_out"], f"post_fwd_{l}")
        mix = _mm(mixed, p["wout"], name=f"out_proj_{l}", tm=2048)
        x1, h2 = _add_rms_fwd(xin, mix, p["g_post"], p["g_fpre"], f"mix_residual_{l}")
        a = _mm(h2, p["wup"], name=f"ffn_up_{l}", out_dtype=BF16, tm=2048)
        y = _mm(a, p["wdn"], name=f"ffn_down_{l}", a_fn=_relu2, tk=D_FF)
        s.update(proj=proj, fox=fox, ret=ret, sb=sb, mla=mla, fox_t=fox_t, sb_t=sb_t, mla_t=mla_t, ret_st=ret_st, oa=oa, ob=ob, oc=oc,
                 od=od, sb_tot=sb_tot, lse_a=lse_a, lse_b=lse_b, mixed=mixed, mix=mix, x1=x1, h2=h2, a=a, y=y)
        saved.append(s)
        if l == 0:
            got = _exchange_wait(l1_state, y, True, "weights_gather_layer1_wait")
            layers[1].update(win=_pad_w_in(got[0].reshape(D_MODEL, D_IN)), wout=got[1].reshape(D_MODEL, D_MODEL),
                             wup=got[2].transpose(1, 0, 2).reshape(D_MODEL, D_FF), wdn=got[3].reshape(D_FF, D_MODEL))
        if l + 1 < DEPTH:
            xin, h = _add_rms_fwd(x1, y, p["g_fpost"], layers[l + 1]["g_pre"], f"ffn_residual_{l}")
        else:
            loss_row, dx = _final_loss(x1, y, p["g_fpost"], tgt, "loss")

    small_g = {n: [None] * DEPTH for n, _ in _SMALL}
    big_g = [[None] * DEPTH for _ in range(6)]
    to_send = [
        lambda g: g.reshape(N_DEV, 1, D_MODEL // N_DEV, D_IN),
        lambda g: g.reshape(Q_RANK, N_DEV, 384 // N_DEV).transpose(1, 0, 2)[:, None],
        lambda g: g.reshape(KV_RANK, N_DEV, 512 // N_DEV).transpose(1, 0, 2)[:, None],
        lambda g: g.reshape(N_DEV, 1, D_MODEL // N_DEV, D_MODEL),
        lambda g: g[:, None],
        lambda g: g.reshape(N_DEV, 1, D_FF // N_DEV, D_MODEL),
    ]
    send_of = lambda ks, l: [to_send[k](big_g[k][l]).astype(BF16) for k in ks]
    late_state = early_state = None
    order_token = jnp.zeros((1, 1), F32)
    for l in reversed(range(DEPTH)):
        p, s = layers[l], saved[l]
        dy, dg = _norm_bwd(dx, s["y"], p["g_fpost"] + order_token, None, BF16, f"ffn_post_bwd_{l}")
        small_g["g_ffn_post"][l] = dg
        da = _mm(dy, p["wdn"], name=f"ffn_down_dx_{l}", tb=True, out_dtype=BF16, epi=_drelu2, epi_in=s["a"],
                 tm=2048)
        big_g[5][l] = _mm(s["a"], dy, name=f"ffn_down_dw_{l}", ta=True, a_fn=_relu2, tk=t, out_dtype=BF16)
        big_g[4][l] = _mm(s["h2"], da, name=f"ffn_up_dw_{l}", ta=True, tk=t, tn=D_FF // N_DEV, out_dtype=BF16,
                          column_blocks=True)
        dh2 = _mm(da, p["wup"], name=f"ffn_up_dx_{l}", tb=True, tk=D_FF)
        dx1, dmix, small_g["g_ffn_pre"][l], small_g["g_mix_post"][l] = _norm_bwd_pair(
            dh2, s["x1"], p["g_fpre"], dx, s["mix"], p["g_post"], f"ffn_pre_mix_post_bwd_{l}")
        dmixed = _mm(dmix, p["wout"], name=f"out_proj_dx_{l}", tb=True, tm=2048)
        big_g[3][l] = _mm(s["mixed"], dmix, name=f"out_proj_dw_{l}", ta=True, tk=t, out_dtype=BF16)
        g_out = p["g_out"]
        if l == 0:
            early_state, early_token = _exchange_start(send_of((3, 4, 5), 0), False, "grads_layer0_early_start")
            g_out = g_out + early_token[0:1, 0:1]
        doa, dob, doc, dod, dla, dlb, drg, dgo = _post_bwd(dmixed, s["oa"], s["ob"], s["oc"], s["od"], s["proj"],
                                                           g_out, f"post_bwd_{l}")
        small_g["g_mix_out"][l] = _unpad_gain_out(dgo).reshape(1, D_MODEL)
        as_rows = lambda a: a[:, :HEADS].T.reshape(HEADS, t // BQA, 1, BQA)
        dfq, dfk, dfv, dcum_k, dcum_q = _softmax_bwd(s["fox"], s["fox_t"], doa, s["lse_a"], as_rows(dla), bias=True,
                                                     chunk_mask=False, scale=1.0, name=f"fox_bwd_{l}")
        dmq, dmk, dmv = _softmax_bwd(s["mla"], s["mla_t"], dob, s["lse_b"], as_rows(dlb), bias=False, chunk_mask=True,
                                     scale=1.0, name=f"mla_bwd_{l}")
        drq, drk, drv = _ret_bwd(s["ret"], s["ret_st"], lg_heads, doc, f"ret_bwd_{l}")
        dsq, dsk, dsv = _sb_bwd(s["sb"], s["sb_t"], dod, s["sb_tot"], f"sb_bwd_{l}")
        dcum_q = jnp.pad(dcum_q.reshape(HEADS, t).T, ((0, 0), (0, LANES - HEADS)))
        dlsf = _cumsum(dcum_q, True, f"forget_cumsum_bwd_{l}", partials=dcum_k)
        dproj, dwq, dwkv, dgq, dgkv, dbf = _prep_bwd(
            (dfq, dfk, dfv), (drq, drk, drv), (dsq, dsk, dsv), (dmq, dmk, dmv), drg, dlsf, s["proj"], pos, invf,
            lg_lanes, p["b_pad"], p["g_q"], p["g_kv"], p["wq"], p["wkv"], f"prep_bwd_{l}")
        small_g["g_q_lora"][l] = dgq
        small_g["g_kv_lora"][l] = dgkv
        small_g["b_forget"][l] = dbf[:, :HEADS]
        big_g[1][l] = _unpad_heads(dwq, 96)
        big_g[2][l] = _unpad_w_kv(dwkv)
        dh = _mm(dproj, p["win"], name=f"in_proj_dx_{l}", tb=True, tk=NP_IN)
        big_g[0][l] = _unpad_w_in(_mm(s["h"], dproj, name=f"in_proj_dw_{l}", ta=True, tn=NP_IN // 2, tk=2048,
                                          out_dtype=BF16))
        g_pre = p["g_pre"]
        if l == 0:
            last_state, last_token = _exchange_start(send_of((0, 1, 2), 0), False, "grads_layer0_rest_start")
            g_pre = g_pre + last_token[0:1, 0:1]
        dx, dg = _norm_bwd(dh, s["x"], g_pre, dx1, F32, f"mix_pre_bwd_{l}")
        small_g["g_mix_pre"][l] = dg
        if l == DEPTH - 1:
            late_state, late_token = _exchange_start(send_of(range(6), l), False, "grads_layer1_start")
            order_token = late_token[0:1, 0:1]
    grad_x = dx.reshape(1, t, D_MODEL)

    res = {}
    small_w = dict(g_mix_pre=g_mix_pre, g_mix_post=g_mix_post, g_ffn_pre=g_ffn_pre, g_ffn_post=g_ffn_post,
                   g_mix_out=g_mix_out, g_q_lora=g_q_lora, g_kv_lora=g_kv_lora, b_forget=b_forget)
    small_m = dict(g_mix_pre=m_g_mix_pre, g_mix_post=m_g_mix_post, g_ffn_pre=m_g_ffn_pre, g_ffn_post=m_g_ffn_post,
                   g_mix_out=m_g_mix_out, g_q_lora=m_g_q_lora, g_kv_lora=m_g_kv_lora, b_forget=m_b_forget)
    small_v = dict(g_mix_pre=v_g_mix_pre, g_mix_post=v_g_mix_post, g_ffn_pre=v_g_ffn_pre, g_ffn_post=v_g_ffn_post,
                   g_mix_out=v_g_mix_out, g_q_lora=v_g_q_lora, g_kv_lora=v_g_kv_lora, b_forget=v_b_forget)
    n_small = DEPTH * SMALL_ROWS
    extra = lambda a: jnp.concatenate([a, jnp.zeros((8, LANES), F32)], axis=0)
    part = jnp.concatenate([_pack_small({n: jnp.concatenate(small_g[n], axis=0) for n, _ in _SMALL}),
                            jnp.broadcast_to(loss_row, (8, LANES))], axis=0)
    sres = _small_allreduce_adam(part, extra(_pack_small(small_w)), extra(_pack_small(small_m)),
                                 extra(_pack_small(small_v)), "small_allreduce_adamw")
    loss = sres[0][n_small, 0]
    sg, sd, sm, sv = [_unpack_small(a[:n_small]) for a in sres]
    for n, _ in _SMALL:
        res[n] = [sg[n], sd[n], sm[n], sv[n]]

    late = _exchange_wait(late_state, dx, False, "grads_layer1_wait")
    early = _exchange_wait(early_state, dx, False, "grads_layer0_early_wait")
    ms = [m_w_in, m_w_q_up, m_w_kv_up, m_w_out, m_w_ffn_up, m_w_ffn_down]
    vs = [v_w_in, v_w_q_up, v_w_kv_up, v_w_out, v_w_ffn_up, v_w_ffn_down]
    names = ["w_in", "w_q_up", "w_kv_up", "w_out", "w_ffn_up", "w_ffn_down"]
    for k in (3, 4, 5):
        res[names[k]] = _reduce_adam([early[k - 3], late[k]], big[k], ms[k], vs[k], f"adamw_{names[k]}")
    done = [sres[0]] + [res[names[k]][0] for k in (3, 4, 5)]
    last = _exchange_wait(last_state, done, False, "grads_layer0_rest_wait")
    for k in (0, 1, 2):
        res[names[k]] = _reduce_adam([last[k], late[k]], big[k], ms[k], vs[k], f"adamw_{names[k]}")

    order = ["g_mix_pre", "w_in", "b_forget", "g_q_lora", "w_q_up", "g_kv_lora", "w_kv_up", "g_mix_out", "w_out",
             "g_mix_post", "g_ffn_pre", "w_ffn_up", "w_ffn_down", "g_ffn_post"]
    outs = [loss, grad_x]
    for idx in range(4):
        outs += [res[n][idx] for n in order]
    return tuple(outs)
```

```python
import functools
import math

import numpy as np
import jax
import jax.numpy as jnp
from jax import lax
from jax.experimental import pallas as pl
from jax.experimental.pallas import tpu as pltpu

F32 = jnp.float32
BF16 = jnp.bfloat16

D_MODEL = 1024
DEPTH = 2
N_DEV = 8
GROUP = 256
HEADS = 4
HEAD_DIM = 64
LANES = 128
HP = HEADS * LANES
QKV = 3 * HP
Q_RANK = 256
KV_RANK = 128
ROPE_MLA = 32
MLA_SCALE = (HEAD_DIM + ROPE_MLA) ** -0.5
D_FF = 4096
D_IN = 2980
CHUNK_SHIFT = 6
EPS = 1e-6
ROPE_BASE = 10000.0
NEG = -1e30

QKV_IN = 3 * GROUP
OFF_FOX, OFF_RET, OFF_SB = 0, QKV_IN, 2 * QKV_IN
OFF_RG = 3 * QKV_IN
OFF_CQ = OFF_RG + GROUP
OFF_CKV = OFF_CQ + Q_RANK
OFF_KR = OFF_CKV + LANES
OFF_FF = OFF_KR + LANES
NP_IN = 3328

BQ = 256
TKS = 128
TM = 512
VMEM_LIMIT = 58 * 1024 * 1024

ADAM_LR, ADAM_B1, ADAM_B2, ADAM_EPS, ADAM_WD, ADAM_STEP = 0.001, 0.9, 0.999, 1e-08, 0.01, 10
ADAM_C1 = 1.0 - ADAM_B1 ** ADAM_STEP
ADAM_C2 = 1.0 - ADAM_B2 ** ADAM_STEP

SMALL_ROWS = 44

NT = (((1,), (1,)), ((), ()))
TN = (((0,), (0,)), ((), ()))


def _cp(*sem):
    return pltpu.CompilerParams(dimension_semantics=sem if sem else None, vmem_limit_bytes=VMEM_LIMIT)


def _bdot(a, b, dn=None):
    if dn is None:
        return jnp.dot(a, b, preferred_element_type=F32)
    return lax.dot_general(a, b, dn, preferred_element_type=F32)


def _split2(x):
    hi = x.astype(BF16)
    lo = (x - hi.astype(F32)).astype(BF16)
    return hi, lo


def _mm(a, b, *, name, ta=False, tb=False, out_dtype=F32, a_fn=None, epi=None, epi_in=None, column_blocks=False,
        tm=1024, tn=1024, tk=1024):
    m, k = (a.shape[1], a.shape[0]) if ta else a.shape
    n = b.shape[0] if tb else b.shape[1]
    tm, tn, tk = min(tm, m), min(tn, n), min(tk, k)
    assert m % tm == 0 and n % tn == 0 and k % tk == 0, (name, m, n, k)
    nk = k // tk
    dn = (((0 if ta else 1,), (1 if tb else 0,)), ((), ()))

    def body(*refs):
        if epi is None:
            a_ref, b_ref, o_ref = refs[:3]
            e_ref = None
            rest = refs[3:]
        else:
            a_ref, b_ref, e_ref, o_ref = refs[:4]
            rest = refs[4:]
        av = a_ref[...]
        if a_fn is not None:
            av = a_fn(av)
        part = lax.dot_general(av.astype(BF16), b_ref[...].astype(BF16), dn, preferred_element_type=F32)

        def finish(r):
            if epi is not None:
                r = epi(r, e_ref[...])
            o_ref[...] = r.astype(out_dtype)

        if nk == 1:
            finish(part)
        else:
            acc_ref = rest[0]
            kk = pl.program_id(2)

            @pl.when(kk == 0)
            def _():
                acc_ref[...] = part

            @pl.when(kk > 0)
            def _():
                acc_ref[...] += part

            @pl.when(kk == nk - 1)
            def _():
                finish(acc_ref[...])

    a_spec = pl.BlockSpec((tk, tm), lambda i, j, kk: (kk, i)) if ta else pl.BlockSpec((tm, tk), lambda i, j, kk: (i, kk))
    b_spec = pl.BlockSpec((tn, tk), lambda i, j, kk: (j, kk)) if tb else pl.BlockSpec((tk, tn), lambda i, j, kk: (kk, j))
    o_spec = pl.BlockSpec((tm, tn), lambda i, j, kk: (i, j))
    in_specs = [a_spec, b_spec]
    args = [a, b]
    if epi is not None:
        in_specs.append(o_spec)
        args.append(epi_in)
    out_shape = jax.ShapeDtypeStruct((m, n), out_dtype)
    if column_blocks:
        assert epi is None
        o_spec = pl.BlockSpec((None, tm, tn), lambda i, j, kk: (j, i, 0))
        out_shape = jax.ShapeDtypeStruct((n // tn, m, tn), out_dtype)
    return pl.pallas_call(
        body, name=name, grid=(m // tm, n // tn, nk),
        in_specs=in_specs, out_specs=o_spec,
        out_shape=out_shape,
        scratch_shapes=[pltpu.VMEM((tm, tn), F32)] if nk > 1 else [],
        compiler_params=_cp("parallel", "parallel", "arbitrary"),
    )(*args)


def _relu2(v):
    r = jnp.maximum(v, 0.0)
    return r * r


def _drelu2(du, av):
    return du * (2.0 * jnp.maximum(av, 0.0))


def _rms(v, g):
    r = lax.rsqrt(jnp.mean(v * v, axis=-1, keepdims=True) + EPS)
    return v * r * g


def _row_spec(w):
    return pl.BlockSpec((TM, w), lambda i: (i, 0))


TMN = 1024


def _nrow_spec(w, t):
    return pl.BlockSpec((min(TMN, t), w), lambda i: (i, 0))


def _ngrid(t):
    return (t // min(TMN, t),)


def _vec_spec(w):
    return pl.BlockSpec((1, w), lambda i: (0, 0))


def _rms_fwd(x, g, name):
    t, d = x.shape

    def body(x_ref, g_ref, h_ref):
        h_ref[...] = _rms(x_ref[...], g_ref[...]).astype(BF16)

    return pl.pallas_call(
        body, name=name, grid=_ngrid(t), in_specs=[_nrow_spec(d, t), _vec_spec(d)], out_specs=_nrow_spec(d, t),
        out_shape=jax.ShapeDtypeStruct((t, d), BF16), compiler_params=_cp("parallel"))(x, g)


def _add_rms_fwd(x, y, g1, g2, name):
    t, d = x.shape

    def body(x_ref, y_ref, g1_ref, g2_ref, xn_ref, h_ref):
        xn = x_ref[...] + _rms(y_ref[...], g1_ref[...])
        xn_ref[...] = xn
        h_ref[...] = _rms(xn, g2_ref[...]).astype(BF16)

    return pl.pallas_call(
        body, name=name, grid=_ngrid(t),
        in_specs=[_nrow_spec(d, t), _nrow_spec(d, t), _vec_spec(d), _vec_spec(d)],
        out_specs=[_nrow_spec(d, t), _nrow_spec(d, t)],
        out_shape=[jax.ShapeDtypeStruct((t, d), F32), jax.ShapeDtypeStruct((t, d), BF16)],
        compiler_params=_cp("parallel"))(x, y, g1, g2)


def _final_loss(x, y, g, tgt, name):
    t, d = x.shape

    def body(x_ref, y_ref, g_ref, t_ref, l_ref, dx_ref):
        @pl.when(pl.program_id(0) == 0)
        def _():
            l_ref[...] = jnp.zeros_like(l_ref)

        err = x_ref[...] + _rms(y_ref[...], g_ref[...]) - t_ref[...]
        dx_ref[...] = err * (1.0 / d)
        l_ref[...] += jnp.sum(jnp.sum(err * err, axis=1, keepdims=True), axis=0, keepdims=True) * (0.5 / d)

    return pl.pallas_call(
        body, name=name, grid=_ngrid(t),
        in_specs=[_nrow_spec(d, t), _nrow_spec(d, t), _vec_spec(d), _nrow_spec(d, t)],
        out_specs=[pl.BlockSpec((1, LANES), lambda i: (0, 0)), _nrow_spec(d, t)],
        out_shape=[jax.ShapeDtypeStruct((1, LANES), F32), jax.ShapeDtypeStruct((t, d), F32)],
        compiler_params=_cp("arbitrary"))(x, y, g, tgt)


def _rms_bwd_vals(dn, v, g):
    w = v.shape[-1]
    r = lax.rsqrt(jnp.mean(v * v, axis=-1, keepdims=True) + EPS)
    vh = v * r
    dgp = jnp.sum(dn * vh, axis=0, keepdims=True)
    dvh = dn * g
    dv = r * (dvh - vh * (jnp.sum(dvh * vh, axis=-1, keepdims=True) * (1.0 / w)))
    return dv, dgp


def _norm_bwd(dn, v, g, resid, out_dtype, name):
    t, d = v.shape
    has_res = resid is not None

    def body(*refs):
        if has_res:
            dn_ref, v_ref, g_ref, r_ref, dv_ref, dg_ref = refs
        else:
            dn_ref, v_ref, g_ref, dv_ref, dg_ref = refs

        @pl.when(pl.program_id(0) == 0)
        def _():
            dg_ref[...] = jnp.zeros_like(dg_ref)

        dv, dgp = _rms_bwd_vals(dn_ref[...].astype(F32), v_ref[...], g_ref[...])
        if has_res:
            dv = dv + r_ref[...]
        dv_ref[...] = dv.astype(out_dtype)
        dg_ref[...] += dgp

    in_specs = [_nrow_spec(d, t), _nrow_spec(d, t), _vec_spec(d)] + ([_nrow_spec(d, t)] if has_res else [])
    args = [dn, v, g] + ([resid] if has_res else [])
    return pl.pallas_call(
        body, name=name, grid=_ngrid(t), in_specs=in_specs,
        out_specs=[_nrow_spec(d, t), _vec_spec(d)],
        out_shape=[jax.ShapeDtypeStruct((t, d), out_dtype), jax.ShapeDtypeStruct((1, d), F32)],
        compiler_params=_cp("arbitrary"))(*args)


def _norm_bwd_pair(dn, v1, g1, resid, v2, g2, name):
    t, d = v1.shape

    def body(dn_ref, v1_ref, g1_ref, r_ref, v2_ref, g2_ref, d1_ref, d2_ref, dg1_ref, dg2_ref):
        @pl.when(pl.program_id(0) == 0)
        def _():
            dg1_ref[...] = jnp.zeros_like(dg1_ref)
            dg2_ref[...] = jnp.zeros_like(dg2_ref)

        d1, dgp1 = _rms_bwd_vals(dn_ref[...].astype(F32), v1_ref[...], g1_ref[...])
        d1 = d1 + r_ref[...]
        d1_ref[...] = d1
        d2, dgp2 = _rms_bwd_vals(d1, v2_ref[...], g2_ref[...])
        d2_ref[...] = d2.astype(BF16)
        dg1_ref[...] += dgp1
        dg2_ref[...] += dgp2

    row, vec = _nrow_spec(d, t), _vec_spec(d)
    return pl.pallas_call(
        body, name=name, grid=_ngrid(t), in_specs=[row, row, vec, row, row, vec],
        out_specs=[row, row, vec, vec],
        out_shape=[jax.ShapeDtypeStruct((t, d), F32), jax.ShapeDtypeStruct((t, d), BF16),
                   jax.ShapeDtypeStruct((1, d), F32), jax.ShapeDtypeStruct((1, d), F32)],
        compiler_params=_cp("arbitrary"))(dn, v1, g1, resid, v2, g2)


def _rope_trig(pos, invf):
    ang = pos * invf
    return jnp.cos(ang), jnp.sin(ang)


def _rope_tables(trig, w, lo, half):
    c, s = trig
    lane = lax.broadcasted_iota(jnp.int32, c.shape, 1)
    active = (lane >= lo) & (lane < lo + 2 * half)
    cos = jnp.concatenate([jnp.where(active, c, 1.0)] * (w // LANES), axis=1)
    sin = jnp.concatenate([jnp.where(active, s, 0.0)] * (w // LANES), axis=1)
    lanes = lax.broadcasted_iota(jnp.int32, (c.shape[0], w), 1) & (LANES - 1)
    first = (lanes >= lo) & (lanes < lo + half)
    second = (lanes >= lo + half) & (lanes < lo + 2 * half)
    return cos, sin, first, second


def _rope_apply(v, cos, sin, first, second, half, sign):
    w = v.shape[-1]
    up = pltpu.roll(v, w - half, 1)
    dn = pltpu.roll(v, half, 1)
    rot = jnp.where(first, -up, jnp.where(second, dn, 0.0))
    return v * cos + rot * (sin * sign)


def _split3(c):
    hi = c.astype(BF16).astype(F32)
    mid = (c - hi).astype(BF16).astype(F32)
    return hi, mid, (c - hi) - mid


def _spread_heads(x):
    low = lax.broadcasted_iota(jnp.int32, (x.shape[0], LANES), 1) < HEAD_DIM
    out = []
    for c in range(x.shape[1] // LANES):
        blk = x[:, c * LANES:(c + 1) * LANES]
        out.append(jnp.where(low, blk, 0.0))
        out.append(jnp.where(low, pltpu.roll(blk, HEAD_DIM, 1), 0.0))
    return jnp.concatenate(out, axis=1)


def _gather_heads(y):
    low = lax.broadcasted_iota(jnp.int32, (y.shape[0], LANES), 1) < HEAD_DIM
    out = []
    for c in range(y.shape[1] // (2 * LANES)):
        a = y[:, 2 * c * LANES:(2 * c + 1) * LANES]
        b = y[:, (2 * c + 1) * LANES:(2 * c + 2) * LANES]
        out.append(jnp.where(low, a, pltpu.roll(b, HEAD_DIM, 1)))
    return jnp.concatenate(out, axis=1)


def _transposed_tiles(dst, row0, blocks, width):
    for b, blk in enumerate(blocks):
        bt = blk.T.astype(BF16)
        rows = bt.shape[0]
        for w in range(TM // width):
            dst[w, row0 + b * rows:row0 + (b + 1) * rows, :] = bt[:, w * width:(w + 1) * width]


def _prep_fwd(proj, cum, pos, invf, lg_lanes, g_q, g_kv, wq_pad, wkv_pad, name):
    t = proj.shape[0]

    def body(fox_ref, ret_ref, sb_ref, cq_ref, ckv_ref, kr_ref, cum_ref, pos_ref, invf_ref, lg_ref,
             gq_ref, gkv_ref, wq_ref, wkv_ref,
             ofox_ref, oret_ref, osb_ref, omla_ref, ofoxt_ref, osbt_ref, omlat_ref, orett_ref):
        pos_v = pos_ref[...]
        foxv, retv, sbv = [[_spread_heads(ref[:, g * GROUP:(g + 1) * GROUP]) for g in range(3)]
                           for ref in (fox_ref, ret_ref, sb_ref)]
        osb_ref[:, 0:HP] = (sbv[0] * 0.125).astype(BF16)
        osb_ref[:, HP:2 * HP] = sbv[1].astype(BF16)
        osb_ref[:, 2 * HP:QKV] = sbv[2].astype(BF16)
        _transposed_tiles(osbt_ref, 0, [sbv[1], sbv[2]], TKS)
        lane = lax.broadcasted_iota(jnp.int32, (TM, LANES), 1)
        cumv = cum_ref[...]
        fq, fk = [], []
        for hb in range(HEADS):
            hi, mid, lo = _split3(cumv[:, hb:hb + 1])
            q = foxv[0][:, hb * LANES:(hb + 1) * LANES] * 0.125
            k = foxv[1][:, hb * LANES:(hb + 1) * LANES]
            ones_q = (lane >= HEAD_DIM) & (lane < HEAD_DIM + 3)
            ones_k = (lane >= HEAD_DIM + 3) & (lane < HEAD_DIM + 6)
            q = jnp.where(ones_q, 1.0, q)
            k = jnp.where(ones_k, 1.0, k)
            for n, part in enumerate((hi, mid, lo)):
                q = jnp.where(lane == HEAD_DIM + 3 + n, part, q)
                k = jnp.where(lane == HEAD_DIM + n, -part, k)
            fq.append(q)
            fk.append(k)
        fk = jnp.concatenate(fk, axis=1)
        ofox_ref[:, 0:HP] = jnp.concatenate(fq, axis=1).astype(BF16)
        ofox_ref[:, HP:2 * HP] = fk.astype(BF16)
        ofox_ref[:, 2 * HP:QKV] = foxv[2].astype(BF16)
        _transposed_tiles(ofoxt_ref, 0, [fk, foxv[2]], BQ)
        trig = _rope_trig(pos_v, invf_ref[...])
        cos, sin, first, second = _rope_tables(trig, HP, 0, HEAD_DIM // 2)
        nloc = (lax.broadcasted_iota(jnp.int32, (TM, 1), 0) & (BQ - 1)).astype(F32)
        dec = lg_ref[...] * nloc
        rq = _rope_apply(retv[0], cos, sin, first, second, HEAD_DIM // 2, 1.0)
        rk = _rope_apply(retv[1], cos, sin, first, second, HEAD_DIM // 2, 1.0)
        oret_ref[:, 0:HP] = (rq * jnp.exp(dec)).astype(BF16)
        rk = rk * 0.125 * jnp.exp(-dec)
        oret_ref[:, HP:2 * HP] = rk.astype(BF16)
        _transposed_tiles(orett_ref, 0, [rk], BQ)
        oret_ref[:, 2 * HP:QKV] = retv[2].astype(BF16)
        cosm, sinm, firstm, secondm = _rope_tables(trig, HP, HEAD_DIM, ROPE_MLA // 2)
        cqn = _rms(cq_ref[...], gq_ref[...]).astype(BF16)
        qm = _bdot(cqn, wq_ref[...])
        qm = _rope_apply(qm, cosm, sinm, firstm, secondm, ROPE_MLA // 2, 1.0)
        omla_ref[:, 0:HP] = (qm * MLA_SCALE).astype(BF16)
        ckvn = _rms(ckv_ref[...], gkv_ref[...]).astype(BF16)
        kv = _bdot(ckvn, wkv_ref[...])
        krr = _rope_apply(kr_ref[...], cosm[:, 0:LANES], sinm[:, 0:LANES], firstm[:, 0:LANES],
                          secondm[:, 0:LANES], ROPE_MLA // 2, 1.0)
        mk = kv[:, 0:HP] + jnp.concatenate([krr] * HEADS, axis=1)
        omla_ref[:, HP:2 * HP] = mk.astype(BF16)
        omla_ref[:, 2 * HP:QKV] = kv[:, HP:2 * HP].astype(BF16)
        _transposed_tiles(omlat_ref, 0, [mk, kv[:, HP:2 * HP]], BQ)

    def seg(off, w):
        return pl.BlockSpec((TM, w), lambda i, o=off // w: (i, o))

    def full(shape):
        return pl.BlockSpec(shape, lambda i: (0,) * len(shape))

    def tiles(width):
        return pl.BlockSpec((TM // width, 2 * HP, width), lambda i: (i, 0, 0))

    in_specs = [seg(OFF_FOX, QKV_IN), seg(OFF_RET, QKV_IN), seg(OFF_SB, QKV_IN), seg(OFF_CQ, Q_RANK),
                seg(OFF_CKV, LANES), seg(OFF_KR, LANES), _row_spec(LANES), pl.BlockSpec((TM, 1), lambda i: (i, 0)),
                full((1, LANES)), full((1, HP)), full((1, Q_RANK)), full((1, KV_RANK)),
                full((Q_RANK, HP)), full((KV_RANK, 2 * HP))]
    out_specs = [_row_spec(QKV)] * 4 + [tiles(BQ), tiles(TKS), tiles(BQ),
                                        pl.BlockSpec((TM // BQ, HP, BQ), lambda i: (i, 0, 0))]
    out_shape = [jax.ShapeDtypeStruct((t, QKV), BF16)] * 4 + [
        jax.ShapeDtypeStruct((t // BQ, 2 * HP, BQ), BF16), jax.ShapeDtypeStruct((t // TKS, 2 * HP, TKS), BF16),
        jax.ShapeDtypeStruct((t // BQ, 2 * HP, BQ), BF16), jax.ShapeDtypeStruct((t // BQ, HP, BQ), BF16)]
    return pl.pallas_call(
        body, name=name, grid=(t // TM,), in_specs=in_specs, out_specs=out_specs, out_shape=out_shape,
        compiler_params=_cp("parallel"))(proj, proj, proj, proj, proj, proj, cum, pos, invf, lg_lanes,
                                         g_q, g_kv, wq_pad, wkv_pad)


def _prep_bwd(dfox, dret, dsb, dmla, drg, dlsf, proj, pos, invf, lg_lanes, b_pad, g_q, g_kv,
              wq_pad, wkv_pad, name):
    t = proj.shape[0]

    def body(dfq, dfk, dfv, drq, drk, drv, dsq, dsk, dsv, dmq, dmk, dmv, drg_ref, dlsf_ref,
             cq_ref, ckv_ref, ff_ref, pos_ref, invf_ref, lg_ref, b_ref, gq_ref, gkv_ref, wq_ref, wkv_ref,
             dp_ref, dwq_ref, dwkv_ref, dgq_ref, dgkv_ref, dbf_ref):
        @pl.when(pl.program_id(0) == 0)
        def _():
            dwq_ref[...] = jnp.zeros_like(dwq_ref)
            dwkv_ref[...] = jnp.zeros_like(dwkv_ref)
            dgq_ref[...] = jnp.zeros_like(dgq_ref)
            dgkv_ref[...] = jnp.zeros_like(dgkv_ref)
            dbf_ref[...] = jnp.zeros_like(dbf_ref)

        pos_v = pos_ref[...]
        def put(off, val):
            dp_ref[:, off:off + GROUP] = _gather_heads(val).astype(BF16)

        for off, (dq, dk, dv) in ((OFF_FOX, (dfq, dfk, dfv)), (OFF_SB, (dsq, dsk, dsv))):
            put(off, dq[...] * 0.125)
            put(off + GROUP, dk[...])
            put(off + 2 * GROUP, dv[...])
        trig = _rope_trig(pos_v, invf_ref[...])
        cos, sin, first, second = _rope_tables(trig, HP, 0, HEAD_DIM // 2)
        nloc = (lax.broadcasted_iota(jnp.int32, (TM, 1), 0) & (BQ - 1)).astype(F32)
        dec = lg_ref[...] * nloc
        dq = _rope_apply(drq[...] * jnp.exp(dec), cos, sin, first, second, HEAD_DIM // 2, -1.0)
        dk = _rope_apply(drk[...] * (0.125 * jnp.exp(-dec)), cos, sin, first, second, HEAD_DIM // 2, -1.0)
        put(OFF_RET, dq)
        put(OFF_RET + GROUP, dk)
        put(OFF_RET + 2 * GROUP, drv[...])
        put(OFF_RG, drg_ref[...])
        cosm, sinm, firstm, secondm = _rope_tables(trig, HP, HEAD_DIM, ROPE_MLA // 2)
        dql = _rope_apply(dmq[...] * MLA_SCALE, cosm, sinm, firstm, secondm, ROPE_MLA // 2, -1.0).astype(BF16)
        cq = cq_ref[...]
        cqn = _rms(cq, gq_ref[...]).astype(BF16)
        dwq_ref[...] += _bdot(cqn, dql, TN)
        dcqn = _bdot(dql, wq_ref[...], NT)
        dcq, dgq = _rms_bwd_vals(dcqn, cq, gq_ref[...])
        dgq_ref[...] += dgq
        dp_ref[:, OFF_CQ:OFF_CQ + Q_RANK] = dcq.astype(BF16)
        dkm = dmk[...]
        dkv = jnp.concatenate([dkm, dmv[...]], axis=1).astype(BF16)
        ckv = ckv_ref[...]
        ckvn = _rms(ckv, gkv_ref[...]).astype(BF16)
        dwkv_ref[...] += _bdot(ckvn, dkv, TN)
        dckvn = _bdot(dkv, wkv_ref[...], NT)
        dckv, dgkv = _rms_bwd_vals(dckvn, ckv, gkv_ref[...])
        dgkv_ref[...] += dgkv
        dp_ref[:, OFF_CKV:OFF_CKV + LANES] = dckv.astype(BF16)
        dkr = dkm[:, 0:LANES] + dkm[:, LANES:2 * LANES] + dkm[:, 2 * LANES:3 * LANES] + dkm[:, 3 * LANES:HP]
        act = firstm[:, 0:LANES] | secondm[:, 0:LANES]
        dkr = jnp.where(act, dkr, 0.0)
        dkr = _rope_apply(dkr, cosm[:, 0:LANES], sinm[:, 0:LANES], firstm[:, 0:LANES], secondm[:, 0:LANES],
                          ROPE_MLA // 2, -1.0)
        dp_ref[:, OFF_KR:OFF_KR + LANES] = dkr.astype(BF16)
        f = ff_ref[...] + b_ref[...]
        dff = dlsf_ref[...] / (1.0 + jnp.exp(f))
        dbf_ref[...] += jnp.sum(dff, axis=0, keepdims=True)
        dp_ref[:, OFF_FF:OFF_FF + LANES] = dff.astype(BF16)
        dp_ref[:, OFF_FF + LANES:NP_IN] = jnp.zeros((TM, NP_IN - OFF_FF - LANES), BF16)

    def seg(off, w):
        return pl.BlockSpec((TM, w), lambda i, o=off // w: (i, o))

    def full(shape):
        return pl.BlockSpec(shape, lambda i: (0,) * len(shape))

    hp_spec = _row_spec(HP)
    in_specs = [hp_spec] * 13 + [_row_spec(LANES), seg(OFF_CQ, Q_RANK), seg(OFF_CKV, LANES), seg(OFF_FF, LANES),
                                 pl.BlockSpec((TM, 1), lambda i: (i, 0)),
                                 full((1, LANES)), full((1, HP)), full((1, LANES)), full((1, Q_RANK)),
                                 full((1, KV_RANK)), full((Q_RANK, HP)), full((KV_RANK, 2 * HP))]
    out_specs = [_row_spec(NP_IN), full((Q_RANK, HP)), full((KV_RANK, 2 * HP)), full((1, Q_RANK)),
                 full((1, KV_RANK)), full((1, LANES))]
    out_shape = [jax.ShapeDtypeStruct((t, NP_IN), BF16), jax.ShapeDtypeStruct((Q_RANK, HP), F32),
                 jax.ShapeDtypeStruct((KV_RANK, 2 * HP), F32), jax.ShapeDtypeStruct((1, Q_RANK), F32),
                 jax.ShapeDtypeStruct((1, KV_RANK), F32), jax.ShapeDtypeStruct((1, LANES), F32)]
    return pl.pallas_call(
        body, name=name, grid=(t // TM,), in_specs=in_specs, out_specs=out_specs, out_shape=out_shape,
        compiler_params=_cp("arbitrary"))(*dfox, *dret, *dsb, *dmla, drg, dlsf, proj, proj, proj, pos, invf,
                                          lg_lanes, b_pad, g_q, g_kv, wq_pad, wkv_pad)


TC = 1024


def _cumsum(x, reverse, name, partials=None, forget_bias=None):
    t = x.shape[0]
    w = LANES
    tc = min(TC, t)
    n = t // tc
    xs = [x] + ([] if partials is None else [partials]) + ([] if forget_bias is None else [forget_bias])

    def body(*refs):
        x_refs, o_ref, carry = refs[:len(xs)], refs[len(xs)], refs[len(xs) + 1]

        @pl.when(pl.program_id(0) == 0)
        def _():
            carry[...] = jnp.zeros_like(carry)

        r = lax.broadcasted_iota(jnp.int32, (tc, tc), 0)
        c = lax.broadcasted_iota(jnp.int32, (tc, tc), 1)
        tri = jnp.where((r <= c) if reverse else (r >= c), 1.0, 0.0).astype(BF16)
        v = x_refs[0][...]
        if forget_bias is not None:
            f = v + x_refs[-1][...]
            v = -(jnp.maximum(-f, 0.0) + jnp.log(1.0 + jnp.exp(-jnp.abs(f))))
        if partials is not None:
            lane = lax.broadcasted_iota(jnp.int32, (tc, LANES), 1)
            for hb in range(HEADS):
                v = v + jnp.where(lane == hb, jnp.sum(x_refs[1][:, _hs(hb)], axis=1, keepdims=True), 0.0)
        hi = v.astype(BF16)
        r1 = v - hi.astype(F32)
        mid = r1.astype(BF16)
        lo = (r1 - mid.astype(F32)).astype(BF16)
        cs = _bdot(tri, hi) + _bdot(tri, mid) + _bdot(tri, lo) + carry[...]
        o_ref[...] = cs
        carry[...] = cs[0:1, :] if reverse else cs[tc - 1:tc, :]

    step = (lambda i: n - 1 - i) if reverse else (lambda i: i)
    x_col = OFF_FF // LANES if forget_bias is not None else 0
    in_specs = [pl.BlockSpec((tc, LANES), lambda i: (step(i), x_col))]
    if partials is not None:
        in_specs.append(pl.BlockSpec((tc, HP), lambda i: (step(i), 0)))
    if forget_bias is not None:
        in_specs.append(_vec_spec(LANES))
    return pl.pallas_call(
        body, name=name, grid=(n,), in_specs=in_specs, out_specs=pl.BlockSpec((tc, w), lambda i: (step(i), 0)),
        out_shape=jax.ShapeDtypeStruct((t, w), F32), scratch_shapes=[pltpu.VMEM((1, w), F32)],
        compiler_params=_cp("arbitrary"))(*xs)


HB_FWD = 4
HB_BWD = 4
HB_SB_FWD = 4
BQS = 512
BQA = 512


def _q_spec(hb, bq=BQ):
    return pl.BlockSpec((bq, hb * LANES), lambda g, i: (i, g))


ONE_BUFFER = pl.Buffered(1)


def _kv_spec(t, which, hb):
    return pl.BlockSpec((t, hb * LANES), lambda g, i, w=which: (0, w * (HEADS // hb) + g), pipeline_mode=ONE_BUFFER)


def _acc_spec(t, hb):
    return pl.BlockSpec((t, hb * LANES), lambda g, i: (0, g), pipeline_mode=ONE_BUFFER)


def _hs(hh):
    return slice(hh * LANES, (hh + 1) * LANES)


def _tile_iota(rows, cols):
    return (lax.broadcasted_iota(jnp.int32, (rows, cols), 0), lax.broadcasted_iota(jnp.int32, (rows, cols), 1))


def _kvt_spec(nkv, width, which, hb):
    return pl.BlockSpec((nkv, hb * LANES, width), lambda g, i, w=which: (0, w * (HEADS // hb) + g, 0),
                        pipeline_mode=ONE_BUFFER)


def _qrow_spec(hb, bq=BQ):
    return pl.BlockSpec((hb, 1, 1, bq), lambda g, i: (g, i, 0, 0))


def _vis(key0, query0, rows, cols, kind):
    r, c = _tile_iota(rows, cols)
    k, q = key0 + r, query0 + c
    if kind == "chunk":
        return (k >> CHUNK_SHIFT) <= (q >> CHUNK_SHIFT)
    return (k < q) if kind == "strict" else (k <= q)


def _softmax_fwd(qkv, kvt, *, chunk_mask, scale, name):
    t = qkv.shape[0]
    nq = t // BQA
    per = BQA // BQ
    hb = HB_FWD
    kind = "chunk" if chunk_mask else "causal"

    def body(q_ref, k_ref, vt_ref, o_ref, lse_ref, m_sc, l_sc, acc_sc):
        i = pl.program_id(1)
        m_sc[...] = jnp.full((hb, 1, BQA), NEG, F32)
        l_sc[...] = jnp.zeros((hb, 1, BQA), F32)
        acc_sc[...] = jnp.zeros((hb, LANES, BQA), F32)

        def tile(j, qoff):
            off = pl.multiple_of(j * BQ, BQ)
            lo = 0 if qoff is None else qoff
            qs = slice(lo, BQA)
            vis = None if qoff is None else _vis(off, i * BQA + lo, BQ, BQA - lo, kind)
            ss = [_bdot(k_ref[pl.ds(off, BQ), _hs(hh)], q_ref[qs, _hs(hh)], NT) for hh in range(hb)]
            for hh in range(hb):
                s = ss[hh]
                if scale != 1.0:
                    s = s * scale
                if vis is not None:
                    s = jnp.where(vis, s, NEG)
                m_old = m_sc[hh, :, qs]
                m_new = jnp.maximum(m_old, jnp.max(s, axis=0, keepdims=True))
                alpha = jnp.exp(m_old - m_new)
                p = jnp.exp(s - m_new)
                l_sc[hh, :, qs] = alpha * l_sc[hh, :, qs] + jnp.sum(p, axis=0, keepdims=True)
                m_sc[hh, :, qs] = m_new
                acc_sc[hh, :, qs] = alpha * acc_sc[hh, :, qs] + _bdot(vt_ref[j, _hs(hh), :], p.astype(BF16))

        def loop(j, carry):
            tile(j, None)
            return carry

        lax.fori_loop(0, per * i, loop, 0)
        for d in range(per):
            tile(per * i + d, d * BQ)
        for hh in range(hb):
            l = l_sc[hh]
            o_ref[:, _hs(hh)] = (acc_sc[hh] / l).T
            lse_ref[hh, 0] = m_sc[hh] + jnp.log(l)

    return pl.pallas_call(
        body, name=name, grid=(HEADS // hb, nq),
        in_specs=[_q_spec(hb, BQA), _kv_spec(t, 1, hb), _kvt_spec(t // BQ, BQ, 1, hb)],
        out_specs=[_q_spec(hb, BQA), _qrow_spec(hb, BQA)],
        out_shape=[jax.ShapeDtypeStruct((t, HP), F32), jax.ShapeDtypeStruct((HEADS, nq, 1, BQA), F32)],
        scratch_shapes=[pltpu.VMEM((hb, 1, BQA), F32), pltpu.VMEM((hb, 1, BQA), F32),
                        pltpu.VMEM((hb, LANES, BQA), F32)],
        compiler_params=_cp("parallel", "arbitrary"))(qkv, qkv, kvt)


def _softmax_bwd(qkv, kvt, do, lse, delta, *, bias, chunk_mask, scale, name):
    t = qkv.shape[0]
    nq = t // BQA
    per = BQA // BQ
    hb = HB_BWD
    kind = "chunk" if chunk_mask else "causal"

    def body(*refs):
        if bias:
            (q_ref, k_ref, v_ref, kt_ref, do_ref, lse_ref, dl_ref, dq_ref, dk_ref, dv_ref, dck_ref, dcq_ref,
             dq_sc, dcq_sc) = refs
            dcq_sc[...] = jnp.zeros((hb, 1, BQA), F32)
        else:
            q_ref, k_ref, v_ref, kt_ref, do_ref, lse_ref, dl_ref, dq_ref, dk_ref, dv_ref, dq_sc = refs
        i = pl.program_id(1)

        @pl.when(i == 0)
        def _():
            dk_ref[...] = jnp.zeros_like(dk_ref)
            dv_ref[...] = jnp.zeros_like(dv_ref)
            if bias:
                dck_ref[...] = jnp.zeros_like(dck_ref)

        dq_sc[...] = jnp.zeros((hb, LANES, BQA), F32)

        def tile(j, qoff):
            off = pl.multiple_of(j * BQ, BQ)
            lo = 0 if qoff is None else qoff
            qsl = slice(lo, BQA)
            vis = None if qoff is None else _vis(off, i * BQA + lo, BQ, BQA - lo, kind)
            qs = [q_ref[qsl, _hs(hh)] for hh in range(hb)]
            dobs = [do_ref[qsl, _hs(hh)].astype(BF16) for hh in range(hb)]
            ss = [_bdot(k_ref[pl.ds(off, BQ), _hs(hh)], qs[hh], NT) for hh in range(hb)]
            dps = [_bdot(v_ref[pl.ds(off, BQ), _hs(hh)], dobs[hh], NT) for hh in range(hb)]
            pbs, dsbs = [], []
            for hh in range(hb):
                s = ss[hh]
                if scale != 1.0:
                    s = s * scale
                p = jnp.exp(s - lse_ref[hh, 0, :, qsl])
                if vis is not None:
                    p = jnp.where(vis, p, 0.0)
                ds = p * (dps[hh] - dl_ref[hh, 0, :, qsl])
                if bias:
                    part = ds[:, 0:LANES]
                    for b in range(1, (BQA - lo) // LANES):
                        part = part + ds[:, b * LANES:(b + 1) * LANES]
                    dck_ref[pl.ds(off, BQ), _hs(hh)] -= part
                    dcq_sc[hh, :, qsl] += jnp.sum(ds, axis=0, keepdims=True)
                if scale != 1.0:
                    ds = ds * scale
                pbs.append(p.astype(BF16))
                dsbs.append(ds.astype(BF16))
            for hh in range(hb):
                sl = _hs(hh)
                dv_ref[pl.ds(off, BQ), sl] += _bdot(pbs[hh], dobs[hh])
                dk_ref[pl.ds(off, BQ), sl] += _bdot(dsbs[hh], qs[hh])
                dq_sc[hh, :, qsl] += _bdot(kt_ref[j, sl, :], dsbs[hh])

        def loop(j, carry):
            tile(j, None)
            return carry

        lax.fori_loop(0, per * i, loop, 0)
        for d in range(per):
            tile(per * i + d, d * BQ)
        for hh in range(hb):
            dq_ref[:, _hs(hh)] = dq_sc[hh].T
            if bias:
                dcq_ref[hh, 0] = dcq_sc[hh]

    in_specs = [_q_spec(hb, BQA), _kv_spec(t, 1, hb), _kv_spec(t, 2, hb), _kvt_spec(t // BQ, BQ, 0, hb),
                _q_spec(hb, BQA), _qrow_spec(hb, BQA), _qrow_spec(hb, BQA)]
    out_specs = [_q_spec(hb, BQA), _acc_spec(t, hb), _acc_spec(t, hb)]
    out_shape = [jax.ShapeDtypeStruct((t, HP), F32)] * 3
    scratch = [pltpu.VMEM((hb, LANES, BQA), F32)]
    if bias:
        out_specs += [_acc_spec(t, hb), _qrow_spec(hb, BQA)]
        out_shape += [jax.ShapeDtypeStruct((t, HP), F32), jax.ShapeDtypeStruct((HEADS, nq, 1, BQA), F32)]
        scratch.append(pltpu.VMEM((hb, 1, BQA), F32))
    return pl.pallas_call(
        body, name=name, grid=(HEADS // hb, nq), in_specs=in_specs, out_specs=out_specs, out_shape=out_shape,
        scratch_shapes=scratch,
        compiler_params=_cp("parallel", "arbitrary"))(qkv, qkv, qkv, kvt, do, lse, delta)


def _ret_diag_decay(lg1, keys_on_rows=False):
    r, c = _tile_iota(BQ, BQ)
    qn, km = (c, r) if keys_on_rows else (r, c)
    dd = jnp.where(km > qn, jnp.exp((2.0 * lg1) * (km - qn).astype(F32)), 1.0)
    return jnp.where((km >> CHUNK_SHIFT) <= (qn >> CHUNK_SHIFT), dd, 0.0)


def _lg_spec(hb):
    return pl.BlockSpec((hb, 1, LANES), lambda g, i: (g, 0, 0))


def _ret_specs(nq, hb, reverse):
    tile = (lambda i: nq - 1 - i) if reverse else (lambda i: i)
    qkv = [pl.BlockSpec((BQ, hb * LANES), lambda g, i, w=w: (tile(i), w * (HEADS // hb) + g)) for w in range(3)]
    kt = pl.BlockSpec((1, hb * LANES, BQ), lambda g, i: (tile(i), g, 0))
    st = pl.BlockSpec((1, hb * LANES, LANES), lambda g, i: (tile(i), g, 0))
    return qkv, kt, st


def _ret_fwd(qkv, kt, lg_heads, name):
    t = qkv.shape[0]
    nq = t // BQ
    hb = HB_FWD

    def body(lg_ref, q_ref, k_ref, v_ref, kt_ref, o_ref, st_ref, s_sc):
        @pl.when(pl.program_id(1) == 0)
        def _():
            s_sc[...] = jnp.zeros_like(s_sc)

        qs = [q_ref[:, _hs(hh)] for hh in range(hb)]
        vs = [v_ref[:, _hs(hh)] for hh in range(hb)]
        aa = [_bdot(qs[hh], k_ref[:, _hs(hh)], NT) for hh in range(hb)]
        kv = [_bdot(kt_ref[0, _hs(hh), :], vs[hh]) for hh in range(hb)]
        for hh in range(hb):
            sl = _hs(hh)
            lg1 = lg_ref[hh][:, 0:1]
            s = s_sc[hh]
            st_ref[0, sl, :] = s
            shi, slo = _split2(s)
            a = (aa[hh] * _ret_diag_decay(lg1)).astype(BF16)
            o_ref[:, sl] = _bdot(a, vs[hh]) + _bdot(qs[hh], shi) + _bdot(qs[hh], slo)
            s_sc[hh] = jnp.exp(lg1 * float(BQ)) * (s + kv[hh])

    qkv_specs, kt_spec, st_spec = _ret_specs(nq, hb, False)
    return pl.pallas_call(
        body, name=name, grid=(HEADS // hb, nq), in_specs=[_lg_spec(hb)] + qkv_specs + [kt_spec],
        out_specs=[_q_spec(hb), st_spec],
        out_shape=[jax.ShapeDtypeStruct((t, HP), F32), jax.ShapeDtypeStruct((nq, HP, LANES), F32)],
        scratch_shapes=[pltpu.VMEM((hb, LANES, LANES), F32)],
        compiler_params=_cp("parallel", "arbitrary"))(lg_heads, qkv, qkv, qkv, kt)


def _ret_bwd(qkv, states, lg_heads, do, name):
    t = qkv.shape[0]
    nq = t // BQ
    hb = HB_FWD

    def body(lg_ref, q_ref, k_ref, v_ref, st_ref, do_ref, dq_ref, dk_ref, dv_ref, g_sc):
        @pl.when(pl.program_id(1) == 0)
        def _():
            g_sc[...] = jnp.zeros_like(g_sc)

        qs = [q_ref[:, _hs(hh)] for hh in range(hb)]
        ks = [k_ref[:, _hs(hh)] for hh in range(hb)]
        vs = [v_ref[:, _hs(hh)] for hh in range(hb)]
        dobs = [do_ref[:, _hs(hh)].astype(BF16) for hh in range(hb)]
        aa = [_bdot(ks[hh], qs[hh], NT) for hh in range(hb)]
        das = [_bdot(vs[hh], dobs[hh], NT) for hh in range(hb)]
        qdo = [_bdot(qs[hh], dobs[hh], TN) for hh in range(hb)]
        for hh in range(hb):
            sl = _hs(hh)
            lg1 = lg_ref[hh][:, 0:1]
            dd = _ret_diag_decay(lg1, keys_on_rows=True)
            at = (aa[hh] * dd).astype(BF16)
            dat = (das[hh] * dd).astype(BF16)
            h = jnp.exp(lg1 * float(BQ)) * g_sc[hh]
            hhi, hlo = _split2(h)
            shi, slo = _split2(st_ref[0, sl, :])
            dv_ref[:, sl] = _bdot(at, dobs[hh]) + _bdot(ks[hh], hhi) + _bdot(ks[hh], hlo)
            dk_ref[:, sl] = _bdot(dat, qs[hh]) + _bdot(vs[hh], hhi, NT) + _bdot(vs[hh], hlo, NT)
            dq_ref[:, sl] = _bdot(dat, ks[hh], TN) + _bdot(dobs[hh], shi, NT) + _bdot(dobs[hh], slo, NT)
            g_sc[hh] = qdo[hh] + h

    qkv_specs, _, st_spec = _ret_specs(nq, hb, True)
    tile_spec = pl.BlockSpec((BQ, hb * LANES), lambda g, i: (nq - 1 - i, g))
    return pl.pallas_call(
        body, name=name, grid=(HEADS // hb, nq), in_specs=[_lg_spec(hb)] + qkv_specs + [st_spec, tile_spec],
        out_specs=[tile_spec] * 3, out_shape=[jax.ShapeDtypeStruct((t, HP), F32)] * 3,
        scratch_shapes=[pltpu.VMEM((hb, LANES, LANES), F32)],
        compiler_params=_cp("parallel", "arbitrary"))(lg_heads, qkv, qkv, qkv, states, do)


def _sb_tile_logs(q, kb, vis):
    z = _bdot(kb, q, NT)
    ls = -(jnp.maximum(z, 0.0) + jnp.log(1.0 + jnp.exp(-jnp.abs(z))))
    if vis is not None:
        ls = jnp.where(vis, ls, 0.0)
    return z, ls


def _sb_later(ls, after):
    hi, lo = _split2(ls)
    return _bdot(after, hi) + _bdot(after, lo)


def _sb_fwd(qkv, kvt, name):
    t = qkv.shape[0]
    nq = t // BQS
    per = BQS // TKS
    hb = HB_SB_FWD

    def body(q_ref, k_ref, vt_ref, o_ref, tot_ref, acc_sc, r_sc):
        i = pl.program_id(1)
        acc_sc[...] = jnp.zeros((hb, LANES, BQS), F32)
        r_sc[...] = jnp.zeros((hb, 1, BQS), F32)
        mr, mc = _tile_iota(TKS, TKS)
        after = jnp.where(mc > mr, 1.0, 0.0).astype(BF16)

        def tile(j, qoff):
            off = pl.multiple_of(j * TKS, TKS)
            lo = 0 if qoff is None else qoff
            qsl = slice(lo, BQS)
            vis = None if qoff is None else _vis(off, i * BQS + lo, TKS, BQS - lo, "strict")
            zl = [_sb_tile_logs(q_ref[qsl, _hs(hh)], k_ref[pl.ds(off, TKS), _hs(hh)], vis) for hh in range(hb)]
            laters = [_sb_later(zl[hh][1], after) for hh in range(hb)]
            ws = []
            for hh in range(hb):
                z, ls = zl[hh]
                w = jnp.exp(z + ls + laters[hh] + r_sc[hh, :, qsl])
                if vis is not None:
                    w = jnp.where(vis, w, 0.0)
                ws.append(w.astype(BF16))
                r_sc[hh, :, qsl] += jnp.sum(ls, axis=0, keepdims=True)
            for hh in range(hb):
                acc_sc[hh, :, qsl] += _bdot(vt_ref[j, _hs(hh), :], ws[hh])

        for d in reversed(range(per)):
            tile(per * i + d, d * TKS)

        def loop(jj, carry):
            tile(per * i - 1 - jj, None)
            return carry

        lax.fori_loop(0, per * i, loop, 0)
        for hh in range(hb):
            o_ref[:, _hs(hh)] = acc_sc[hh].T
            tot_ref[hh, 0] = r_sc[hh]

    return pl.pallas_call(
        body, name=name, grid=(HEADS // hb, nq),
        in_specs=[_q_spec(hb, BQS), _kv_spec(t, 1, hb), _kvt_spec(t // TKS, TKS, 1, hb)],
        out_specs=[_q_spec(hb, BQS), _qrow_spec(hb, BQS)],
        out_shape=[jax.ShapeDtypeStruct((t, HP), F32), jax.ShapeDtypeStruct((HEADS, nq, 1, BQS), F32)],
        scratch_shapes=[pltpu.VMEM((hb, LANES, BQS), F32), pltpu.VMEM((hb, 1, BQS), F32)],
        compiler_params=_cp("parallel", "arbitrary"))(qkv, qkv, kvt)


def _sb_bwd(qkv, kvt, do, tot, name):
    t = qkv.shape[0]
    nq = t // BQS
    per = BQS // TKS
    hb = HB_BWD

    def body(q_ref, k_ref, v_ref, kt_ref, do_ref, tot_ref, dq_ref, dk_ref, dv_ref, dq_sc, p_sc, g_sc):
        i = pl.program_id(1)

        @pl.when(i == 0)
        def _():
            dk_ref[...] = jnp.zeros_like(dk_ref)
            dv_ref[...] = jnp.zeros_like(dv_ref)

        dq_sc[...] = jnp.zeros((hb, LANES, BQS), F32)
        p_sc[...] = jnp.zeros((hb, 1, BQS), F32)
        g_sc[...] = jnp.zeros((hb, 1, BQS), F32)
        mr, mc = _tile_iota(TKS, TKS)
        after = jnp.where(mc > mr, 1.0, 0.0).astype(BF16)
        before = jnp.where(mc < mr, 1.0, 0.0).astype(BF16)

        def tile(j, qoff):
            off = pl.multiple_of(j * TKS, TKS)
            lo = 0 if qoff is None else qoff
            qsl = slice(lo, BQS)
            vis = None if qoff is None else _vis(off, i * BQS + lo, TKS, BQS - lo, "strict")
            qs = [q_ref[qsl, _hs(hh)] for hh in range(hb)]
            dobs = [do_ref[qsl, _hs(hh)].astype(BF16) for hh in range(hb)]
            zl = [_sb_tile_logs(qs[hh], k_ref[pl.ds(off, TKS), _hs(hh)], vis) for hh in range(hb)]
            dws = [_bdot(v_ref[pl.ds(off, TKS), _hs(hh)], dobs[hh], NT) for hh in range(hb)]
            laters = [_sb_later(zl[hh][1], after) for hh in range(hb)]
            ws, gs = [], []
            for hh in range(hb):
                z, ls = zl[hh]
                own = jnp.sum(ls, axis=0, keepdims=True)
                rest = tot_ref[hh, 0, :, qsl] - p_sc[hh, :, qsl] - own
                w = jnp.exp(z + ls + laters[hh] + rest)
                if vis is not None:
                    w = jnp.where(vis, w, 0.0)
                p_sc[hh, :, qsl] += own
                ws.append(w.astype(BF16))
                gs.append(dws[hh] * w)
            gins = [_bdot(before, gs[hh].astype(BF16)) for hh in range(hb)]
            dzbs = []
            for hh in range(hb):
                g = gs[hh]
                stay = jnp.exp(zl[hh][1])
                dz = g * stay - (1.0 - stay) * (gins[hh] + g_sc[hh, :, qsl])
                if vis is not None:
                    dz = jnp.where(vis, dz, 0.0)
                g_sc[hh, :, qsl] += jnp.sum(g, axis=0, keepdims=True)
                dzbs.append(dz.astype(BF16))
            for hh in range(hb):
                sl = _hs(hh)
                dv_ref[pl.ds(off, TKS), sl] += _bdot(ws[hh], dobs[hh])
                dk_ref[pl.ds(off, TKS), sl] += _bdot(dzbs[hh], qs[hh])
                dq_sc[hh, :, qsl] += _bdot(kt_ref[j, sl, :], dzbs[hh])

        def loop(j, carry):
            tile(j, None)
            return carry

        lax.fori_loop(0, per * i, loop, 0)
        for d in range(per):
            tile(per * i + d, d * TKS)
        for hh in range(hb):
            dq_ref[:, _hs(hh)] = dq_sc[hh].T

    return pl.pallas_call(
        body, name=name, grid=(HEADS // hb, nq),
        in_specs=[_q_spec(hb, BQS), _kv_spec(t, 1, hb), _kv_spec(t, 2, hb), _kvt_spec(t // TKS, TKS, 0, hb),
                  _q_spec(hb, BQS), _qrow_spec(hb, BQS)],
        out_specs=[_q_spec(hb, BQS), _acc_spec(t, hb), _acc_spec(t, hb)],
        out_shape=[jax.ShapeDtypeStruct((t, HP), F32)] * 3,
        scratch_shapes=[pltpu.VMEM((hb, LANES, BQS), F32), pltpu.VMEM((hb, 1, BQS), F32),
                        pltpu.VMEM((hb, 1, BQS), F32)],
        compiler_params=_cp("parallel", "arbitrary"))(qkv, qkv, qkv, kvt, do, tot)


def _sigmoid(v):
    return 1.0 / (1.0 + jnp.exp(-v))


def _post_fwd(oa, ob, oc, od, proj, g_pad, name):
    t = oa.shape[0]

    def body(oa_ref, ob_ref, oc_ref, od_ref, rg_ref, g_ref, mx_ref):
        g = g_ref[...]

        def group(o, gg):
            r = lax.rsqrt(jnp.sum(o * o, axis=-1, keepdims=True) * (1.0 / GROUP) + EPS)
            return _gather_heads(o * r * gg).astype(BF16)

        mx_ref[:, 0:GROUP] = group(oa_ref[...], g[:, 0:HP])
        mx_ref[:, GROUP:2 * GROUP] = group(ob_ref[...], g[:, HP:2 * HP])
        mx_ref[:, 3 * GROUP:4 * GROUP] = group(od_ref[...], g[:, 3 * HP:4 * HP])
        real = lax.broadcasted_iota(jnp.int32, (TM, LANES), 1) < HEAD_DIM
        rg = _spread_heads(rg_ref[...])
        gated = []
        for hb in range(HEADS):
            sl = slice(hb * LANES, (hb + 1) * LANES)
            o = oc_ref[:, sl]
            mu = jnp.sum(o, axis=-1, keepdims=True) * (1.0 / HEAD_DIM)
            dlt = jnp.where(real, o - mu, 0.0)
            var = jnp.sum(dlt * dlt, axis=-1, keepdims=True) * (1.0 / HEAD_DIM)
            yn = dlt * lax.rsqrt(var + EPS) * g[:, 2 * HP + hb * LANES:2 * HP + (hb + 1) * LANES]
            x = rg[:, sl]
            gated.append(yn * (x * _sigmoid(x)))
        mx_ref[:, 2 * GROUP:3 * GROUP] = _gather_heads(jnp.concatenate(gated, axis=1)).astype(BF16)

    rg_spec = pl.BlockSpec((TM, GROUP), lambda i: (i, OFF_RG // GROUP))
    return pl.pallas_call(
        body, name=name, grid=(t // TM,),
        in_specs=[_row_spec(HP)] * 4 + [rg_spec, _vec_spec(4 * HP)], out_specs=_row_spec(D_MODEL),
        out_shape=jax.ShapeDtypeStruct((t, D_MODEL), BF16), compiler_params=_cp("parallel"))(oa, ob, oc, od, proj, g_pad)


def _post_bwd(dmx, oa, ob, oc, od, proj, g_pad, name):
    t = oa.shape[0]

    def body(dm_ref, oa_ref, ob_ref, oc_ref, od_ref, rg_ref, g_ref,
             doa_ref, dob_ref, doc_ref, dod_ref, dla_ref, dlb_ref, drg_ref, dg_ref):
        @pl.when(pl.program_id(0) == 0)
        def _():
            dg_ref[...] = jnp.zeros_like(dg_ref)

        g = g_ref[...]

        def group_bwd(dm, o, gg):
            r = lax.rsqrt(jnp.sum(o * o, axis=-1, keepdims=True) * (1.0 / GROUP) + EPS)
            oh = o * r
            dgp = jnp.sum(dm * oh, axis=0, keepdims=True)
            dyh = dm * gg
            do = r * (dyh - oh * (jnp.sum(dyh * oh, axis=-1, keepdims=True) * (1.0 / GROUP)))
            return do, dgp

        def delta_bc(do, o):
            prod = do * o
            lane = lax.broadcasted_iota(jnp.int32, (TM, LANES), 1)
            out = jnp.zeros((TM, LANES), F32)
            for hb in range(HEADS):
                out = jnp.where(lane == hb, jnp.sum(prod[:, hb * LANES:(hb + 1) * LANES], axis=-1, keepdims=True), out)
            return out

        dmp = [_spread_heads(dm_ref[:, gi * GROUP:(gi + 1) * GROUP]) for gi in range(4)]
        rg = _spread_heads(rg_ref[...])
        oa = oa_ref[...]
        do_a, dga = group_bwd(dmp[0], oa, g[:, 0:HP])
        doa_ref[...] = do_a.astype(BF16)
        dla_ref[...] = delta_bc(do_a, oa)
        dg_ref[:, 0:HP] += dga
        ob = ob_ref[...]
        do_b, dgb = group_bwd(dmp[1], ob, g[:, HP:2 * HP])
        dob_ref[...] = do_b.astype(BF16)
        dlb_ref[...] = delta_bc(do_b, ob)
        dg_ref[:, HP:2 * HP] += dgb
        do_d, dgd = group_bwd(dmp[3], od_ref[...], g[:, 3 * HP:4 * HP])
        dod_ref[...] = do_d.astype(BF16)
        dg_ref[:, 3 * HP:4 * HP] += dgd
        real = lax.broadcasted_iota(jnp.int32, (TM, LANES), 1) < HEAD_DIM
        for hb in range(HEADS):
            sl = slice(hb * LANES, (hb + 1) * LANES)
            gsl = slice(2 * HP + hb * LANES, 2 * HP + (hb + 1) * LANES)
            o = oc_ref[:, sl]
            mu = jnp.sum(o, axis=-1, keepdims=True) * (1.0 / HEAD_DIM)
            dlt = jnp.where(real, o - mu, 0.0)
            var = jnp.sum(dlt * dlt, axis=-1, keepdims=True) * (1.0 / HEAD_DIM)
            rstd = lax.rsqrt(var + EPS)
            dhat = dlt * rstd
            gc = g[:, gsl]
            x = rg[:, sl]
            sg = _sigmoid(x)
            dm = dmp[2][:, sl]
            drg_ref[:, sl] = dm * (dhat * gc) * (sg * (1.0 + x * (1.0 - sg)))
            dyn = dm * (x * sg)
            dg_ref[:, gsl] += jnp.sum(dyn * dhat, axis=0, keepdims=True)
            ddh = dyn * gc
            m1 = jnp.sum(ddh, axis=-1, keepdims=True) * (1.0 / HEAD_DIM)
            m2 = jnp.sum(ddh * dhat, axis=-1, keepdims=True) * (1.0 / HEAD_DIM)
            doc_ref[:, sl] = jnp.where(real, rstd * (ddh - m1 - dhat * m2), 0.0).astype(BF16)

    rg_spec = pl.BlockSpec((TM, GROUP), lambda i: (i, OFF_RG // GROUP))
    hp = _row_spec(HP)
    return pl.pallas_call(
        body, name=name, grid=(t // TM,),
        in_specs=[_row_spec(D_MODEL), hp, hp, hp, hp, rg_spec, _vec_spec(4 * HP)],
        out_specs=[hp] * 4 + [_row_spec(LANES)] * 2 + [hp, _vec_spec(4 * HP)],
        out_shape=[jax.ShapeDtypeStruct((t, HP), BF16)] * 4 + [jax.ShapeDtypeStruct((t, LANES), F32)] * 2
        + [jax.ShapeDtypeStruct((t, HP), F32), jax.ShapeDtypeStruct((1, 4 * HP), F32)],
        compiler_params=_cp("arbitrary"))(dmx, oa, ob, oc, od, proj, g_pad)


def _mesh_pos():
    return lax.axis_index("x"), lax.axis_index("y"), lax.axis_index("c")


def _peer(pos, k):
    x, y, c = pos
    px = 1 - x if (k >> 2) & 1 else x
    py = 1 - y if (k >> 1) & 1 else y
    pc = 1 - c if k & 1 else c
    return (px, py, pc), 4 * px + 2 * py + pc


def _exchange(arrs, gather, name):
    n = len(arrs)

    def body(*refs):
        ins, outs = refs[:n], refs[n:2 * n]
        send_sems, recv_sems, loc_sems = refs[2 * n:]
        pos = _mesh_pos()
        me = 4 * pos[0] + 2 * pos[1] + pos[2]
        local = []
        for a in range(n):
            src = ins[a] if gather else ins[a].at[me]
            cp = pltpu.make_async_copy(src, outs[a].at[me], loc_sems.at[a])
            cp.start()
            local.append(cp)
        sends, recvs = [], []
        for k in range(1, N_DEV):
            peer, pid = _peer(pos, k)
            for a in range(n):
                s = a * (N_DEV - 1) + k - 1
                src = ins[a] if gather else ins[a].at[pid]
                cp = pltpu.make_async_remote_copy(
                    src_ref=src, dst_ref=outs[a].at[me], send_sem=send_sems.at[s], recv_sem=recv_sems.at[s],
                    device_id=peer, device_id_type=pl.DeviceIdType.MESH)
                cp.start()
                sends.append(cp)
                recvs.append(pltpu.make_async_remote_copy(
                    src_ref=src, dst_ref=outs[a].at[pid], send_sem=send_sems.at[s], recv_sem=recv_sems.at[s],
                    device_id=peer, device_id_type=pl.DeviceIdType.MESH))
        for cp in recvs:
            cp.wait_recv()
        for cp in sends:
            cp.wait_send()
        for cp in local:
            cp.wait()

    any_spec = pl.BlockSpec(memory_space=pl.ANY)
    out_shape = [jax.ShapeDtypeStruct((N_DEV,) + tuple(a.shape) if gather else tuple(a.shape), a.dtype) for a in arrs]
    return pl.pallas_call(
        body, name=name, in_specs=[any_spec] * n, out_specs=[any_spec] * n, out_shape=out_shape,
        scratch_shapes=[pltpu.SemaphoreType.DMA((n * (N_DEV - 1),)), pltpu.SemaphoreType.DMA((n * (N_DEV - 1),)),
                        pltpu.SemaphoreType.DMA((n,))],
        compiler_params=pltpu.CompilerParams(has_side_effects=True))(*arrs)


def _device_index():
    x, y, c = _mesh_pos()
    return 4 * x + 2 * y + c


def _landing(srcs, gather):
    me = _device_index()
    lands = []
    for a in srcs:
        own = a[None] if gather else lax.dynamic_slice_in_dim(a, me, 1, axis=0)
        shape = (N_DEV,) + tuple(a.shape) if gather else tuple(a.shape)
        lands.append(lax.dynamic_update_slice_in_dim(lax.empty(shape, a.dtype), own, me, axis=0))
    return lands


def _exchange_copies(ins, lands, send_sems, recv_sems, gather):
    pos = _mesh_pos()
    me = 4 * pos[0] + 2 * pos[1] + pos[2]
    sends, recvs = [], []
    for k in range(1, N_DEV):
        peer, pid = _peer(pos, k)
        for a in range(len(ins)):
            s = a * (N_DEV - 1) + k - 1
            src = ins[a] if gather else ins[a].at[pid]
            sends.append(pltpu.make_async_remote_copy(
                src_ref=src, dst_ref=lands[a].at[me], send_sem=send_sems.at[s], recv_sem=recv_sems.at[s],
                device_id=peer, device_id_type=pl.DeviceIdType.MESH))
            recvs.append(pltpu.make_async_remote_copy(
                src_ref=src, dst_ref=lands[a].at[pid], send_sem=send_sems.at[s], recv_sem=recv_sems.at[s],
                device_id=peer, device_id_type=pl.DeviceIdType.MESH))
    return sends, recvs


def _exchange_start(srcs, gather, name, after=None):
    n = len(srcs)
    lands = _landing(srcs, gather)
    nsem = n * (N_DEV - 1)
    extra = [] if after is None else [after]

    def body(*refs):
        ins, lnd = refs[:n], refs[n:2 * n]
        send_sems, recv_sems = refs[2 * n + len(extra)], refs[2 * n + len(extra) + 1]
        token = refs[-1]
        sends, _ = _exchange_copies(ins, lnd, send_sems, recv_sems, gather)
        for cp in sends:
            cp.start()
        token[...] = jnp.zeros_like(token)

    hbm = pl.BlockSpec(memory_space=pltpu.HBM)
    sem = pl.BlockSpec(memory_space=pltpu.SEMAPHORE)
    bufs = list(srcs) + lands
    out_shape = ([pltpu.SemaphoreType.DMA((nsem,)), pltpu.SemaphoreType.DMA((nsem,))]
                 + [pltpu.HBM(b.shape, b.dtype) for b in bufs] + [jax.ShapeDtypeStruct((8, LANES), F32)])
    outs = pl.pallas_call(
        body, name=name, in_specs=[hbm] * (2 * n) + [pl.BlockSpec(memory_space=pl.ANY)] * len(extra),
        out_specs=[sem, sem] + [hbm] * (2 * n) + [pl.BlockSpec(memory_space=pltpu.VMEM)], out_shape=out_shape,
        input_output_aliases={i: 2 + i for i in range(2 * n)},
        compiler_params=pltpu.CompilerParams(has_side_effects=pltpu.SideEffectType.DATAFLOW_SIDE_EFFECTING),
    )(*[pltpu.with_memory_space_constraint(b, pltpu.HBM) for b in bufs], *extra)
    return (outs[0], outs[1], outs[2:2 + n], outs[2 + n:2 + 2 * n]), outs[-1]


def _exchange_wait(state, after, gather, name):
    send_sems, recv_sems, srcs, lands = state
    n = len(srcs)
    after = list(after) if isinstance(after, (list, tuple)) else [after]

    def body(*refs):
        ins, lnd = refs[:n], refs[n:2 * n]
        s_sems, r_sems = refs[2 * n], refs[2 * n + 1]
        sends, recvs = _exchange_copies(ins, lnd, s_sems, r_sems, gather)
        for cp in sends:
            cp.wait_send()
        for cp in recvs:
            cp.wait_recv()

    hbm = pl.BlockSpec(memory_space=pltpu.HBM)
    sem = pl.BlockSpec(memory_space=pltpu.SEMAPHORE)
    bufs = list(srcs) + list(lands)
    outs = pl.pallas_call(
        body, name=name, in_specs=[hbm] * (2 * n) + [sem, sem] + [pl.BlockSpec(memory_space=pl.ANY)] * len(after),
        out_specs=[hbm] * (2 * n), out_shape=[pltpu.HBM(b.shape, b.dtype) for b in bufs],
        input_output_aliases={i: i for i in range(2 * n)},
        compiler_params=pltpu.CompilerParams(has_side_effects=pltpu.SideEffectType.DATAFLOW_SIDE_EFFECTING),
    )(*bufs, send_sems, recv_sems, *after)
    return outs[n:]


def _adam_vals(w, g, m, v):
    m = ADAM_B1 * m + (1.0 - ADAM_B1) * g
    v = ADAM_B2 * v + (1.0 - ADAM_B2) * (g * g)
    m_hat = m / ADAM_C1
    v_hat = v / ADAM_C2
    delta = -ADAM_LR * (m_hat / (jnp.sqrt(v_hat) + ADAM_EPS) + ADAM_WD * w)
    return delta, m, v


def _small_allreduce_adam(part, w, m, v, name):
    rows = part.shape[0]

    def body(p_ref, w_ref, m_ref, v_ref, g_ref, d_ref, nm_ref, nv_ref, gath, send_sems, recv_sems):
        pos = _mesh_pos()
        me = 4 * pos[0] + 2 * pos[1] + pos[2]
        gath[me] = p_ref[...]
        sends, recvs = [], []
        for k in range(1, N_DEV):
            peer, pid = _peer(pos, k)
            cp = pltpu.make_async_remote_copy(
                src_ref=p_ref, dst_ref=gath.at[me], send_sem=send_sems.at[k - 1], recv_sem=recv_sems.at[k - 1],
                device_id=peer, device_id_type=pl.DeviceIdType.MESH)
            cp.start()
            sends.append(cp)
            recvs.append(pltpu.make_async_remote_copy(
                src_ref=p_ref, dst_ref=gath.at[pid], send_sem=send_sems.at[k - 1], recv_sem=recv_sems.at[k - 1],
                device_id=peer, device_id_type=pl.DeviceIdType.MESH))
        for cp in recvs:
            cp.wait_recv()
        for cp in sends:
            cp.wait_send()
        g = gath[0]
        for p in range(1, N_DEV):
            g = g + gath[p]
        g_ref[...] = g
        d, nm, nv = _adam_vals(w_ref[...], g, m_ref[...], v_ref[...])
        d_ref[...] = d
        nm_ref[...] = nm
        nv_ref[...] = nv

    vm = pl.BlockSpec(memory_space=pltpu.VMEM)
    sds = jax.ShapeDtypeStruct((rows, LANES), F32)
    return pl.pallas_call(
        body, name=name, in_specs=[vm] * 4, out_specs=[vm] * 4, out_shape=[sds] * 4,
        scratch_shapes=[pltpu.VMEM((N_DEV, rows, LANES), F32), pltpu.SemaphoreType.DMA((N_DEV - 1,)),
                        pltpu.SemaphoreType.DMA((N_DEV - 1,))],
        compiler_params=pltpu.CompilerParams(has_side_effects=True))(part, w, m, v)


def _reduce_adam(recvs, w, m, v, name):
    shape = w.shape
    c = shape[-1]
    r = int(np.prod(shape[1:-1]))
    rs = [a.reshape(N_DEV, r, c) for a in recvs]
    w2, m2, v2 = w.reshape(DEPTH * r, c), m.reshape(DEPTH * r, c), v.reshape(DEPTH * r, c)
    tr = r
    while tr * c * 4 > (1 << 20) and tr % 16 == 0:
        tr //= 2
    nt = r // tr

    def body(*refs):
        r_refs = refs[:DEPTH]
        w_ref, m_ref, v_ref, g_ref, d_ref, nm_ref, nv_ref = refs[DEPTH:]
        for l, r_ref in enumerate(r_refs):
            @pl.when(pl.program_id(0) // nt == l)
            def _(r_ref=r_ref):
                g = r_ref[0].astype(F32)
                for p in range(1, N_DEV):
                    g = g + r_ref[p].astype(F32)
                g_ref[...] = g
                d, nm, nv = _adam_vals(w_ref[...], g, m_ref[...], v_ref[...])
                d_ref[...] = d
                nm_ref[...] = nm
                nv_ref[...] = nv

    recv_specs = [pl.BlockSpec((N_DEV, tr, c), lambda i, l=l: (0, jnp.clip(i - l * nt, 0, nt - 1), 0))
                  for l in range(DEPTH)]
    spec = pl.BlockSpec((tr, c), lambda i: (i, 0))
    sds = jax.ShapeDtypeStruct((DEPTH * r, c), F32)
    outs = pl.pallas_call(
        body, name=name, grid=(DEPTH * nt,), in_specs=recv_specs + [spec, spec, spec],
        out_specs=[spec] * 4, out_shape=[sds] * 4, compiler_params=_cp("arbitrary"))(*rs, w2, m2, v2)
    return [o.reshape(shape) for o in outs]


def _pad_heads(w, real=HEAD_DIM):
    lead = w.shape[:-1]
    w = w.reshape(lead + (HEADS, real))
    w = jnp.pad(w, [(0, 0)] * len(lead) + [(0, 0), (0, LANES - real)])
    return w.reshape(lead + (HP,))


def _unpad_heads(w, real=HEAD_DIM):
    lead = w.shape[:-1]
    return w.reshape(lead + (HEADS, LANES))[..., :real].reshape(lead + (HEADS * real,))


_IN_SEGS = (("fq", 0, 256), ("fk", 256, 512), ("fv", 512, 768), ("ff", 768, 772), ("cq", 772, 1028),
            ("ckv", 1028, 1156), ("kr", 1156, 1188), ("rq", 1188, 1444), ("rk", 1444, 1700), ("rv", 1700, 1956),
            ("rg", 1956, 2212), ("sq", 2212, 2468), ("sk", 2468, 2724), ("sv", 2724, 2980))


def _pad_w_in(w):
    s = {n: w[:, a:b] for n, a, b in _IN_SEGS}
    rows = w.shape[0]
    z = lambda n: jnp.zeros((rows, n), w.dtype)
    parts = [s[n] for n in ("fq", "fk", "fv", "rq", "rk", "rv", "sq", "sk", "sv", "rg", "cq", "ckv")]
    parts += [z(HEAD_DIM), s["kr"], z(LANES - HEAD_DIM - ROPE_MLA), s["ff"], z(LANES - HEADS),
              z(NP_IN - OFF_FF - LANES)]
    return jnp.concatenate(parts, axis=1)


def _unpad_w_in(wp):
    seg = lambda off, n=GROUP: wp[:, off:off + n]
    parts = [seg(OFF_FOX), seg(OFF_FOX + GROUP), seg(OFF_FOX + 2 * GROUP), seg(OFF_FF, HEADS),
             seg(OFF_CQ, Q_RANK), seg(OFF_CKV, KV_RANK), seg(OFF_KR + HEAD_DIM, ROPE_MLA),
             seg(OFF_RET), seg(OFF_RET + GROUP), seg(OFF_RET + 2 * GROUP), seg(OFF_RG),
             seg(OFF_SB), seg(OFF_SB + GROUP), seg(OFF_SB + 2 * GROUP)]
    return jnp.concatenate(parts, axis=1)


def _pad_w_kv(w):
    w4 = w.reshape(KV_RANK, HEADS, 2 * HEAD_DIM)
    k = w4[:, :, :HEAD_DIM].reshape(KV_RANK, GROUP)
    v = w4[:, :, HEAD_DIM:].reshape(KV_RANK, GROUP)
    return jnp.concatenate([_pad_heads(k), _pad_heads(v)], axis=1)


def _unpad_w_kv(wp):
    k = _unpad_heads(wp[:, :HP]).reshape(KV_RANK, HEADS, HEAD_DIM)
    v = _unpad_heads(wp[:, HP:]).reshape(KV_RANK, HEADS, HEAD_DIM)
    return jnp.concatenate([k, v], axis=-1).reshape(KV_RANK, HEADS * 2 * HEAD_DIM)


def _pad_gain_out(g):
    g = jnp.pad(g.reshape(4 * HEADS, HEAD_DIM), ((0, 0), (0, LANES - HEAD_DIM)))
    return g.reshape(1, 4 * HP)


def _unpad_gain_out(gp):
    return gp.reshape(4 * HEADS, LANES)[:, :HEAD_DIM].reshape(D_MODEL)


_SMALL = (("g_mix_pre", 1024), ("g_mix_post", 1024), ("g_ffn_pre", 1024), ("g_ffn_post", 1024), ("g_mix_out", 1024),
          ("g_q_lora", 256), ("g_kv_lora", 128), ("b_forget", 4))


def _pack_small(vals):
    parts = []
    for name, n in _SMALL:
        a = vals[name].astype(F32)
        if n < LANES:
            a = jnp.pad(a, ((0, 0), (0, LANES - n)))
        parts.append(a)
    return jnp.concatenate(parts, axis=1).reshape(DEPTH * SMALL_ROWS, LANES)


def _unpack_small(packed):
    flat = packed.reshape(DEPTH, SMALL_ROWS * LANES)
    out, off = {}, 0
    for name, n in _SMALL:
        out[name] = flat[:, off:off + n]
        off += max(n, LANES)
    return out


def kernel(x, positions, g_mix_pre, w_in, b_forget, g_q_lora, w_q_up, g_kv_lora, w_kv_up, g_mix_out, w_out, g_mix_post, g_ffn_pre, w_ffn_up, w_ffn_down, g_ffn_post, loss_target, m_g_mix_pre, m_w_in, m_b_forget, m_g_q_lora, m_w_q_up, m_g_kv_lora, m_w_kv_up, m_g_mix_out, m_w_out, m_g_mix_post, m_g_ffn_pre, m_w_ffn_up, m_w_ffn_down, m_g_ffn_post, v_g_mix_pre, v_w_in, v_b_forget, v_g_q_lora, v_w_q_up, v_g_kv_lora, v_w_kv_up, v_g_mix_out, v_w_out, v_g_mix_post, v_g_ffn_pre, v_w_ffn_up, v_w_ffn_down, v_g_ffn_post):
    t = x.shape[1]
    nq = t // BQ
    x0 = x[0]
    tgt = loss_target[0]
    pos = positions[0].astype(F32).reshape(t, 1)

    half_r, half_m = HEAD_DIM // 2, ROPE_MLA // 2
    invf_r = ROPE_BASE ** (-jnp.arange(half_r, dtype=F32) / half_r)
    invf_m = ROPE_BASE ** (-jnp.arange(half_m, dtype=F32) / half_m)
    invf = jnp.concatenate([invf_r, invf_r, invf_m, invf_m,
                            jnp.zeros((LANES - HEAD_DIM - ROPE_MLA,), F32)]).reshape(1, LANES)
    log_gamma = jnp.log1p(-jnp.power(2.0, -5.0 - jnp.arange(HEADS, dtype=F32)))
    lg_lanes = jnp.repeat(log_gamma, LANES).reshape(1, HP)
    lg_heads = jnp.broadcast_to(log_gamma[:, None, None], (HEADS, 1, LANES))

    big = [w_in, w_q_up, w_kv_up, w_out, w_ffn_up, w_ffn_down]
    bf = lambda w: w.astype(BF16)
    first = _exchange([bf(w_in[0]), bf(w_q_up), bf(w_kv_up)], True, "weights_gather_first")
    l0_state, l0_token = _exchange_start([bf(w[0]) for w in (w_out, w_ffn_up, w_ffn_down)], True,
                                         "weights_gather_layer0_start")
    l1_state, rest_token = _exchange_start([bf(w[1]) for w in (w_in, w_out, w_ffn_up, w_ffn_down)], True,
                                           "weights_gather_layer1_start", after=l0_token)
    wq_g = first[1].transpose(1, 2, 0, 3).reshape(DEPTH, Q_RANK, 384)
    wkv_g = first[2].transpose(1, 2, 0, 3).reshape(DEPTH, KV_RANK, 512)

    row = lambda g: g.reshape(1, -1)
    layers = []
    for l in range(DEPTH):
        layers.append(dict(
            wq=_pad_heads(wq_g[l], 96), wkv=_pad_w_kv(wkv_g[l]),
            g_pre=row(g_mix_pre[l]), g_post=row(g_mix_post[l]), g_fpre=row(g_ffn_pre[l]), g_fpost=row(g_ffn_post[l]),
            g_out=_pad_gain_out(g_mix_out[l]), g_q=row(g_q_lora[l]), g_kv=row(g_kv_lora[l]),
            b_pad=jnp.pad(b_forget[l], (0, LANES - HEADS)).reshape(1, LANES)))
    layers[0]["win"] = _pad_w_in(first[0].reshape(D_MODEL, D_IN))

    saved = []
    xin = x0
    h = _rms_fwd(xin, layers[0]["g_pre"] + rest_token[0:1, 0:1], "rms_pre_0")
    loss_row = dx = None
    for l, p in enumerate(layers):
        s = dict(x=xin, h=h)
        proj = _mm(h, p["win"], name=f"in_proj_{l}", tm=1024, tn=NP_IN)
        cum = _cumsum(proj, False, f"forget_cumsum_{l}", forget_bias=p["b_pad"])
        fox, ret, sb, mla, fox_t, sb_t, mla_t, ret_t = _prep_fwd(proj, cum, pos, invf, lg_lanes, p["g_q"],
                                                          p["g_kv"], p["wq"], p["wkv"], f"prep_fwd_{l}")
        oa, lse_a = _softmax_fwd(fox, fox_t, chunk_mask=False, scale=1.0, name=f"fox_fwd_{l}")
        ob, lse_b = _softmax_fwd(mla, mla_t, chunk_mask=True, scale=1.0, name=f"mla_fwd_{l}")
        oc, ret_st = _ret_fwd(ret, ret_t, lg_heads, f"ret_fwd_{l}")
        od, sb_tot = _sb_fwd(sb, sb_t, f"sb_fwd_{l}")
        if l == 0:
            got = _exchange_wait(l0_state, od, True, "weights_gather_layer0_wait")
            p.update(wout=got[0].reshape(D_MODEL, D_MODEL), wup=got[1].transpose(1, 0, 2).reshape(D_MODEL, D_FF),
                     wdn=got[2].reshape(D_FF, D_MODEL))
        mixed = _post_fwd(oa, ob, oc, od, proj, p["g_out"], f"post_fwd_{l}")
        mix = _mm(mixed, p["wout"], name=f"out_proj_{l}", tm=2048)
        x1, h2 = _add_rms_fwd(xin, mix, p["g_post"], p["g_fpre"], f"mix_residual_{l}")
        a = _mm(h2, p["wup"], name=f"ffn_up_{l}", out_dtype=BF16, tm=2048)
        y = _mm(a, p["wdn"], name=f"ffn_down_{l}", a_fn=_relu2, tk=D_FF)
        s.update(proj=proj, fox=fox, ret=ret, sb=sb, mla=mla, fox_t=fox_t, sb_t=sb_t, mla_t=mla_t, ret_st=ret_st, oa=oa, ob=ob, oc=oc,
                 od=od, sb_tot=sb_tot, lse_a=lse_a, lse_b=lse_b, mixed=mixed, mix=mix, x1=x1, h2=h2, a=a, y=y)
        saved.append(s)
        if l == 0:
            got = _exchange_wait(l1_state, y, True, "weights_gather_layer1_wait")
            layers[1].update(win=_pad_w_in(got[0].reshape(D_MODEL, D_IN)), wout=got[1].reshape(D_MODEL, D_MODEL),
                             wup=got[2].transpose(1, 0, 2).reshape(D_MODEL, D_FF), wdn=got[3].reshape(D_FF, D_MODEL))
        if l + 1 < DEPTH:
            xin, h = _add_rms_fwd(x1, y, p["g_fpost"], layers[l + 1]["g_pre"], f"ffn_residual_{l}")
        else:
            loss_row, dx = _final_loss(x1, y, p["g_fpost"], tgt, "loss")

    small_g = {n: [None] * DEPTH for n, _ in _SMALL}
    big_g = [[None] * DEPTH for _ in range(6)]
    to_send = [
        lambda g: g.reshape(N_DEV, 1, D_MODEL // N_DEV, D_IN),
        lambda g: g.reshape(Q_RANK, N_DEV, 384 // N_DEV).transpose(1, 0, 2)[:, None],
        lambda g: g.reshape(KV_RANK, N_DEV, 512 // N_DEV).transpose(1, 0, 2)[:, None],
        lambda g: g.reshape(N_DEV, 1, D_MODEL // N_DEV, D_MODEL),
        lambda g: g[:, None],
        lambda g: g.reshape(N_DEV, 1, D_FF // N_DEV, D_MODEL),
    ]
    send_of = lambda ks, l: [to_send[k](big_g[k][l]).astype(BF16) for k in ks]
    late_state = early_state = None
    order_token = jnp.zeros((1, 1), F32)
    for l in reversed(range(DEPTH)):
        p, s = layers[l], saved[l]
        dy, dg = _norm_bwd(dx, s["y"], p["g_fpost"] + order_token, None, BF16, f"ffn_post_bwd_{l}")
        small_g["g_ffn_post"][l] = dg
        da = _mm(dy, p["wdn"], name=f"ffn_down_dx_{l}", tb=True, out_dtype=BF16, epi=_drelu2, epi_in=s["a"],
                 tm=2048)
        big_g[5][l] = _mm(s["a"], dy, name=f"ffn_down_dw_{l}", ta=True, a_fn=_relu2, tk=t, out_dtype=BF16)
        big_g[4][l] = _mm(s["h2"], da, name=f"ffn_up_dw_{l}", ta=True, tk=t, tn=D_FF // N_DEV, out_dtype=BF16,
                          column_blocks=True)
        dh2 = _mm(da, p["wup"], name=f"ffn_up_dx_{l}", tb=True, tk=D_FF)
        dx1, dmix, small_g["g_ffn_pre"][l], small_g["g_mix_post"][l] = _norm_bwd_pair(
            dh2, s["x1"], p["g_fpre"], dx, s["mix"], p["g_post"], f"ffn_pre_mix_post_bwd_{l}")
        dmixed = _mm(dmix, p["wout"], name=f"out_proj_dx_{l}", tb=True, tm=2048)
        big_g[3][l] = _mm(s["mixed"], dmix, name=f"out_proj_dw_{l}", ta=True, tk=t, out_dtype=BF16)
        g_out = p["g_out"]
        if l == 0:
            early_state, early_token = _exchange_start(send_of((3, 4, 5), 0), False, "grads_layer0_early_start")
            g_out = g_out + early_token[0:1, 0:1]
        doa, dob, doc, dod, dla, dlb, drg, dgo = _post_bwd(dmixed, s["oa"], s["ob"], s["oc"], s["od"], s["proj"],
                                                           g_out, f"post_bwd_{l}")
        small_g["g_mix_out"][l] = _unpad_gain_out(dgo).reshape(1, D_MODEL)
        as_rows = lambda a: a[:, :HEADS].T.reshape(HEADS, t // BQA, 1, BQA)
        dfq, dfk, dfv, dcum_k, dcum_q = _softmax_bwd(s["fox"], s["fox_t"], doa, s["lse_a"], as_rows(dla), bias=True,
                                                     chunk_mask=False, scale=1.0, name=f"fox_bwd_{l}")
        dmq, dmk, dmv = _softmax_bwd(s["mla"], s["mla_t"], dob, s["lse_b"], as_rows(dlb), bias=False, chunk_mask=True,
                                     scale=1.0, name=f"mla_bwd_{l}")
        drq, drk, drv = _ret_bwd(s["ret"], s["ret_st"], lg_heads, doc, f"ret_bwd_{l}")
        dsq, dsk, dsv = _sb_bwd(s["sb"], s["sb_t"], dod, s["sb_tot"], f"sb_bwd_{l}")
        dcum_q = jnp.pad(dcum_q.reshape(HEADS, t).T, ((0, 0), (0, LANES - HEADS)))
        dlsf = _cumsum(dcum_q, True, f"forget_cumsum_bwd_{l}", partials=dcum_k)
        dproj, dwq, dwkv, dgq, dgkv, dbf = _prep_bwd(
            (dfq, dfk, dfv), (drq, drk, drv), (dsq, dsk, dsv), (dmq, dmk, dmv), drg, dlsf, s["proj"], pos, invf,
            lg_lanes, p["b_pad"], p["g_q"], p["g_kv"], p["wq"], p["wkv"], f"prep_bwd_{l}")
        small_g["g_q_lora"][l] = dgq
        small_g["g_kv_lora"][l] = dgkv
        small_g["b_forget"][l] = dbf[:, :HEADS]
        big_g[1][l] = _unpad_heads(dwq, 96)
        big_g[2][l] = _unpad_w_kv(dwkv)
        dh = _mm(dproj, p["win"], name=f"in_proj_dx_{l}", tb=True, tk=NP_IN)
        big_g[0][l] = _unpad_w_in(_mm(s["h"], dproj, name=f"in_proj_dw_{l}", ta=True, tn=NP_IN // 2, tk=2048,
                                          out_dtype=BF16))
        g_pre = p["g_pre"]
        if l == 0:
            last_state, last_token = _exchange_start(send_of((0, 1, 2), 0), False, "grads_layer0_rest_start")
            g_pre = g_pre + last_token[0:1, 0:1]
        dx, dg = _norm_bwd(dh, s["x"], g_pre, dx1, F32, f"mix_pre_bwd_{l}")
        small_g["g_mix_pre"][l] = dg
        if l == DEPTH - 1:
            late_state, late_token = _exchange_start(send_of(range(6), l), False, "grads_layer1_start")
            order_token = late_token[0:1, 0:1]
    grad_x = dx.reshape(1, t, D_MODEL)

    res = {}
    small_w = dict(g_mix_pre=g_mix_pre, g_mix_post=g_mix_post, g_ffn_pre=g_ffn_pre, g_ffn_post=g_ffn_post,
                   g_mix_out=g_mix_out, g_q_lora=g_q_lora, g_kv_lora=g_kv_lora, b_forget=b_forget)
    small_m = dict(g_mix_pre=m_g_mix_pre, g_mix_post=m_g_mix_post, g_ffn_pre=m_g_ffn_pre, g_ffn_post=m_g_ffn_post,
                   g_mix_out=m_g_mix_out, g_q_lora=m_g_q_lora, g_kv_lora=m_g_kv_lora, b_forget=m_b_forget)
    small_v = dict(g_mix_pre=v_g_mix_pre, g_mix_post=v_g_mix_post, g_ffn_pre=v_g_ffn_pre, g_ffn_post=v_g_ffn_post,
                   g_mix_out=v_g_mix_out, g_q_lora=v_g_q_lora, g_kv_lora=v_g_kv_lora, b_forget=v_b_forget)
    n_small = DEPTH * SMALL_ROWS
    extra = lambda a: jnp.concatenate([a, jnp.zeros((8, LANES), F32)], axis=0)
    part = jnp.concatenate([_pack_small({n: jnp.concatenate(small_g[n], axis=0) for n, _ in _SMALL}),
                            jnp.broadcast_to(loss_row, (8, LANES))], axis=0)
    sres = _small_allreduce_adam(part, extra(_pack_small(small_w)), extra(_pack_small(small_m)),
                                 extra(_pack_small(small_v)), "small_allreduce_adamw")
    loss = sres[0][n_small, 0]
    sg, sd, sm, sv = [_unpack_small(a[:n_small]) for a in sres]
    for n, _ in _SMALL:
        res[n] = [sg[n], sd[n], sm[n], sv[n]]

    late = _exchange_wait(late_state, dx, False, "grads_layer1_wait")
    early = _exchange_wait(early_state, dx, False, "grads_layer0_early_wait")
    ms = [m_w_in, m_w_q_up, m_w_kv_up, m_w_out, m_w_ffn_up, m_w_ffn_down]
    vs = [v_w_in, v_w_q_up, v_w_kv_up, v_w_out, v_w_ffn_up, v_w_ffn_down]
    names = ["w_in", "w_q_up", "w_kv_up", "w_out", "w_ffn_up", "w_ffn_down"]
    for k in (3, 4, 5):
        res[names[k]] = _reduce_adam([early[k - 3], late[k]], big[k], ms[k], vs[k], f"adamw_{names[k]}")
    done = [sres[0]] + [res[names[k]][0] for k in (3, 4, 5)]
    last = _exchange_wait(last_state, done, False, "grads_layer0_rest_wait")
    for k in (0, 1, 2):
        res[names[k]] = _reduce_adam([last[k], late[k]], big[k], ms[k], vs[k], f"adamw_{names[k]}")

    order = ["g_mix_pre", "w_in", "b_forget", "g_q_lora", "w_q_up", "g_kv_lora", "w_kv_up", "g_mix_out", "w_out",
             "g_mix_post", "g_ffn_pre", "w_ffn_up", "w_ffn_down", "g_ffn_post"]
    outs = [loss, grad_x]
    for idx in range(4):
        outs += [res[n][idx] for n in order]
    return tuple(outs)
```

```python
import functools
import math

import numpy as np
import jax
import jax.numpy as jnp
from jax import lax
from jax.experimental import pallas as pl
from jax.experimental.pallas import tpu as pltpu

F32 = jnp.float32
BF16 = jnp.bfloat16

D_MODEL = 1024
DEPTH = 2
N_DEV = 8
GROUP = 256
HEADS = 4
HEAD_DIM = 64
LANES = 128
HP = HEADS * LANES
QKV = 3 * HP
Q_RANK = 256
KV_RANK = 128
ROPE_MLA = 32
MLA_SCALE = (HEAD_DIM + ROPE_MLA) ** -0.5
D_FF = 4096
D_IN = 2980
CHUNK_SHIFT = 6
EPS = 1e-6
ROPE_BASE = 10000.0
NEG = -1e30

QKV_IN = 3 * GROUP
OFF_FOX, OFF_RET, OFF_SB = 0, QKV_IN, 2 * QKV_IN
OFF_RG = 3 * QKV_IN
OFF_CQ = OFF_RG + GROUP
OFF_CKV = OFF_CQ + Q_RANK
OFF_KR = OFF_CKV + LANES
OFF_FF = OFF_KR + LANES
NP_IN = 3328

BQ = 256
TKS = 128
TM = 512
VMEM_LIMIT = 58 * 1024 * 1024

ADAM_LR, ADAM_B1, ADAM_B2, ADAM_EPS, ADAM_WD, ADAM_STEP = 0.001, 0.9, 0.999, 1e-08, 0.01, 10
ADAM_C1 = 1.0 - ADAM_B1 ** ADAM_STEP
ADAM_C2 = 1.0 - ADAM_B2 ** ADAM_STEP

SMALL_ROWS = 44

NT = (((1,), (1,)), ((), ()))
TN = (((0,), (0,)), ((), ()))


def _cp(*sem):
    return pltpu.CompilerParams(dimension_semantics=sem if sem else None, vmem_limit_bytes=VMEM_LIMIT)


def _bdot(a, b, dn=None):
    if dn is None:
        return jnp.dot(a, b, preferred_element_type=F32)
    return lax.dot_general(a, b, dn, preferred_element_type=F32)


def _split2(x):
    hi = x.astype(BF16)
    lo = (x - hi.astype(F32)).astype(BF16)
    return hi, lo


def _mm(a, b, *, name, ta=False, tb=False, out_dtype=F32, a_fn=None, epi=None, epi_in=None, column_blocks=False,
        tm=1024, tn=1024, tk=1024):
    m, k = (a.shape[1], a.shape[0]) if ta else a.shape
    n = b.shape[0] if tb else b.shape[1]
    tm, tn, tk = min(tm, m), min(tn, n), min(tk, k)
    assert m % tm == 0 and n % tn == 0 and k % tk == 0, (name, m, n, k)
    nk = k // tk
    dn = (((0 if ta else 1,), (1 if tb else 0,)), ((), ()))

    def body(*refs):
        if epi is None:
            a_ref, b_ref, o_ref = refs[:3]
            e_ref = None
            rest = refs[3:]
        else:
            a_ref, b_ref, e_ref, o_ref = refs[:4]
            rest = refs[4:]
        av = a_ref[...]
        if a_fn is not None:
            av = a_fn(av)
        part = lax.dot_general(av.astype(BF16), b_ref[...].astype(BF16), dn, preferred_element_type=F32)

        def finish(r):
            if epi is not None:
                r = epi(r, e_ref[...])
            o_ref[...] = r.astype(out_dtype)

        if nk == 1:
            finish(part)
        else:
            acc_ref = rest[0]
            kk = pl.program_id(2)

            @pl.when(kk == 0)
            def _():
                acc_ref[...] = part

            @pl.when(kk > 0)
            def _():
                acc_ref[...] += part

            @pl.when(kk == nk - 1)
            def _():
                finish(acc_ref[...])

    a_spec = pl.BlockSpec((tk, tm), lambda i, j, kk: (kk, i)) if ta else pl.BlockSpec((tm, tk), lambda i, j, kk: (i, kk))
    b_spec = pl.BlockSpec((tn, tk), lambda i, j, kk: (j, kk)) if tb else pl.BlockSpec((tk, tn), lambda i, j, kk: (kk, j))
    o_spec = pl.BlockSpec((tm, tn), lambda i, j, kk: (i, j))
    in_specs = [a_spec, b_spec]
    args = [a, b]
    if epi is not None:
        in_specs.append(o_spec)
        args.append(epi_in)
    out_shape = jax.ShapeDtypeStruct((m, n), out_dtype)
    if column_blocks:
        assert epi is None
        o_spec = pl.BlockSpec((None, tm, tn), lambda i, j, kk: (j, i, 0))
        out_shape = jax.ShapeDtypeStruct((n // tn, m, tn), out_dtype)
    return pl.pallas_call(
        body, name=name, grid=(m // tm, n // tn, nk),
        in_specs=in_specs, out_specs=o_spec,
        out_shape=out_shape,
        scratch_shapes=[pltpu.VMEM((tm, tn), F32)] if nk > 1 else [],
        compiler_params=_cp("parallel", "parallel", "arbitrary"),
    )(*args)


def _relu2(v):
    r = jnp.maximum(v, 0.0)
    return r * r


def _drelu2(du, av):
    return du * (2.0 * jnp.maximum(av, 0.0))


def _rms(v, g):
    r = lax.rsqrt(jnp.mean(v * v, axis=-1, keepdims=True) + EPS)
    return v * r * g


def _row_spec(w):
    return pl.BlockSpec((TM, w), lambda i: (i, 0))


TMN = 1024


def _nrow_spec(w, t):
    return pl.BlockSpec((min(TMN, t), w), lambda i: (i, 0))


def _ngrid(t):
    return (t // min(TMN, t),)


def _vec_spec(w):
    return pl.BlockSpec((1, w), lambda i: (0, 0))


def _rms_fwd(x, g, name):
    t, d = x.shape

    def body(x_ref, g_ref, h_ref):
        h_ref[...] = _rms(x_ref[...], g_ref[...]).astype(BF16)

    return pl.pallas_call(
        body, name=name, grid=_ngrid(t), in_specs=[_nrow_spec(d, t), _vec_spec(d)], out_specs=_nrow_spec(d, t),
        out_shape=jax.ShapeDtypeStruct((t, d), BF16), compiler_params=_cp("parallel"))(x, g)


def _add_rms_fwd(x, y, g1, g2, name):
    t, d = x.shape

    def body(x_ref, y_ref, g1_ref, g2_ref, xn_ref, h_ref):
        xn = x_ref[...] + _rms(y_ref[...], g1_ref[...])
        xn_ref[...] = xn
        h_ref[...] = _rms(xn, g2_ref[...]).astype(BF16)

    return pl.pallas_call(
        body, name=name, grid=_ngrid(t),
        in_specs=[_nrow_spec(d, t), _nrow_spec(d, t), _vec_spec(d), _vec_spec(d)],
        out_specs=[_nrow_spec(d, t), _nrow_spec(d, t)],
        out_shape=[jax.ShapeDtypeStruct((t, d), F32), jax.ShapeDtypeStruct((t, d), BF16)],
        compiler_params=_cp("parallel"))(x, y, g1, g2)


def _final_loss(x, y, g, tgt, name):
    t, d = x.shape

    def body(x_ref, y_ref, g_ref, t_ref, l_ref, dx_ref):
        @pl.when(pl.program_id(0) == 0)
        def _():
            l_ref[...] = jnp.zeros_like(l_ref)

        err = x_ref[...] + _rms(y_ref[...], g_ref[...]) - t_ref[...]
        dx_ref[...] = err * (1.0 / d)
        l_ref[...] += jnp.sum(jnp.sum(err * err, axis=1, keepdims=True), axis=0, keepdims=True) * (0.5 / d)

    return pl.pallas_call(
        body, name=name, grid=_ngrid(t),
        in_specs=[_nrow_spec(d, t), _nrow_spec(d, t), _vec_spec(d), _nrow_spec(d, t)],
        out_specs=[pl.BlockSpec((1, LANES), lambda i: (0, 0)), _nrow_spec(d, t)],
        out_shape=[jax.ShapeDtypeStruct((1, LANES), F32), jax.ShapeDtypeStruct((t, d), F32)],
        compiler_params=_cp("arbitrary"))(x, y, g, tgt)


def _rms_bwd_vals(dn, v, g):
    w = v.shape[-1]
    r = lax.rsqrt(jnp.mean(v * v, axis=-1, keepdims=True) + EPS)
    vh = v * r
    dgp = jnp.sum(dn * vh, axis=0, keepdims=True)
    dvh = dn * g
    dv = r * (dvh - vh * (jnp.sum(dvh * vh, axis=-1, keepdims=True) * (1.0 / w)))
    return dv, dgp


def _norm_bwd(dn, v, g, resid, out_dtype, name):
    t, d = v.shape
    has_res = resid is not None

    def body(*refs):
        if has_res:
            dn_ref, v_ref, g_ref, r_ref, dv_ref, dg_ref = refs
        else:
            dn_ref, v_ref, g_ref, dv_ref, dg_ref = refs

        @pl.when(pl.program_id(0) == 0)
        def _():
            dg_ref[...] = jnp.zeros_like(dg_ref)

        dv, dgp = _rms_bwd_vals(dn_ref[...].astype(F32), v_ref[...], g_ref[...])
        if has_res:
            dv = dv + r_ref[...]
        dv_ref[...] = dv.astype(out_dtype)
        dg_ref[...] += dgp

    in_specs = [_nrow_spec(d, t), _nrow_spec(d, t), _vec_spec(d)] + ([_nrow_spec(d, t)] if has_res else [])
    args = [dn, v, g] + ([resid] if has_res else [])
    return pl.pallas_call(
        body, name=name, grid=_ngrid(t), in_specs=in_specs,
        out_specs=[_nrow_spec(d, t), _vec_spec(d)],
        out_shape=[jax.ShapeDtypeStruct((t, d), out_dtype), jax.ShapeDtypeStruct((1, d), F32)],
        compiler_params=_cp("arbitrary"))(*args)


def _norm_bwd_pair(dn, v1, g1, resid, v2, g2, name):
    t, d = v1.shape

    def body(dn_ref, v1_ref, g1_ref, r_ref, v2_ref, g2_ref, d1_ref, d2_ref, dg1_ref, dg2_ref):
        @pl.when(pl.program_id(0) == 0)
        def _():
            dg1_ref[...] = jnp.zeros_like(dg1_ref)
            dg2_ref[...] = jnp.zeros_like(dg2_ref)

        d1, dgp1 = _rms_bwd_vals(dn_ref[...].astype(F32), v1_ref[...], g1_ref[...])
        d1 = d1 + r_ref[...]
        d1_ref[...] = d1
        d2, dgp2 = _rms_bwd_vals(d1, v2_ref[...], g2_ref[...])
        d2_ref[...] = d2.astype(BF16)
        dg1_ref[...] += dgp1
        dg2_ref[...] += dgp2

    row, vec = _nrow_spec(d, t), _vec_spec(d)
    return pl.pallas_call(
        body, name=name, grid=_ngrid(t), in_specs=[row, row, vec, row, row, vec],
        out_specs=[row, row, vec, vec],
        out_shape=[jax.ShapeDtypeStruct((t, d), F32), jax.ShapeDtypeStruct((t, d), BF16),
                   jax.ShapeDtypeStruct((1, d), F32), jax.ShapeDtypeStruct((1, d), F32)],
        compiler_params=_cp("arbitrary"))(dn, v1, g1, resid, v2, g2)


def _rope_trig(pos, invf):
    ang = pos * invf
    return jnp.cos(ang), jnp.sin(ang)


def _rope_tables(trig, w, lo, half):
    c, s = trig
    lane = lax.broadcasted_iota(jnp.int32, c.shape, 1)
    active = (lane >= lo) & (lane < lo + 2 * half)
    cos = jnp.concatenate([jnp.where(active, c, 1.0)] * (w // LANES), axis=1)
    sin = jnp.concatenate([jnp.where(active, s, 0.0)] * (w // LANES), axis=1)
    lanes = lax.broadcasted_iota(jnp.int32, (c.shape[0], w), 1) & (LANES - 1)
    first = (lanes >= lo) & (lanes < lo + half)
    second = (lanes >= lo + half) & (lanes < lo + 2 * half)
    return cos, sin, first, second


def _rope_apply(v, cos, sin, first, second, half, sign):
    w = v.shape[-1]
    up = pltpu.roll(v, w - half, 1)
    dn = pltpu.roll(v, half, 1)
    rot = jnp.where(first, -up, jnp.where(second, dn, 0.0))
    return v * cos + rot * (sin * sign)


def _split3(c):
    hi = c.astype(BF16).astype(F32)
    mid = (c - hi).astype(BF16).astype(F32)
    return hi, mid, (c - hi) - mid


def _spread_heads(x):
    low = lax.broadcasted_iota(jnp.int32, (x.shape[0], LANES), 1) < HEAD_DIM
    out = []
    for c in range(x.shape[1] // LANES):
        blk = x[:, c * LANES:(c + 1) * LANES]
        out.append(jnp.where(low, blk, 0.0))
        out.append(jnp.where(low, pltpu.roll(blk, HEAD_DIM, 1), 0.0))
    return jnp.concatenate(out, axis=1)


def _gather_heads(y):
    low = lax.broadcasted_iota(jnp.int32, (y.shape[0], LANES), 1) < HEAD_DIM
    out = []
    for c in range(y.shape[1] // (2 * LANES)):
        a = y[:, 2 * c * LANES:(2 * c + 1) * LANES]
        b = y[:, (2 * c + 1) * LANES:(2 * c + 2) * LANES]
        out.append(jnp.where(low, a, pltpu.roll(b, HEAD_DIM, 1)))
    return jnp.concatenate(out, axis=1)


def _transposed_tiles(dst, row0, blocks, width):
    for b, blk in enumerate(blocks):
        bt = blk.T.astype(BF16)
        rows = bt.shape[0]
        for w in range(TM // width):
            dst[w, row0 + b * rows:row0 + (b + 1) * rows, :] = bt[:, w * width:(w + 1) * width]


def _prep_fwd(proj, cum, pos, invf, lg_lanes, g_q, g_kv, wq_pad, wkv_pad, name):
    t = proj.shape[0]

    def body(fox_ref, ret_ref, sb_ref, cq_ref, ckv_ref, kr_ref, cum_ref, pos_ref, invf_ref, lg_ref,
             gq_ref, gkv_ref, wq_ref, wkv_ref,
             ofox_ref, oret_ref, osb_ref, omla_ref, ofoxt_ref, osbt_ref, omlat_ref, orett_ref):
        pos_v = pos_ref[...]
        foxv, retv, sbv = [[_spread_heads(ref[:, g * GROUP:(g + 1) * GROUP]) for g in range(3)]
                           for ref in (fox_ref, ret_ref, sb_ref)]
        osb_ref[:, 0:HP] = (sbv[0] * 0.125).astype(BF16)
        osb_ref[:, HP:2 * HP] = sbv[1].astype(BF16)
        osb_ref[:, 2 * HP:QKV] = sbv[2].astype(BF16)
        _transposed_tiles(osbt_ref, 0, [sbv[1], sbv[2]], TKS)
        lane = lax.broadcasted_iota(jnp.int32, (TM, LANES), 1)
        cumv = cum_ref[...]
        fq, fk = [], []
        for hb in range(HEADS):
            hi, mid, lo = _split3(cumv[:, hb:hb + 1])
            q = foxv[0][:, hb * LANES:(hb + 1) * LANES] * 0.125
            k = foxv[1][:, hb * LANES:(hb + 1) * LANES]
            ones_q = (lane >= HEAD_DIM) & (lane < HEAD_DIM + 3)
            ones_k = (lane >= HEAD_DIM + 3) & (lane < HEAD_DIM + 6)
            q = jnp.where(ones_q, 1.0, q)
            k = jnp.where(ones_k, 1.0, k)
            for n, part in enumerate((hi, mid, lo)):
                q = jnp.where(lane == HEAD_DIM + 3 + n, part, q)
                k = jnp.where(lane == HEAD_DIM + n, -part, k)
            fq.append(q)
            fk.append(k)
        fk = jnp.concatenate(fk, axis=1)
        ofox_ref[:, 0:HP] = jnp.concatenate(fq, axis=1).astype(BF16)
        ofox_ref[:, HP:2 * HP] = fk.astype(BF16)
        ofox_ref[:, 2 * HP:QKV] = foxv[2].astype(BF16)
        _transposed_tiles(ofoxt_ref, 0, [fk, foxv[2]], BQ)
        trig = _rope_trig(pos_v, invf_ref[...])
        cos, sin, first, second = _rope_tables(trig, HP, 0, HEAD_DIM // 2)
        nloc = (lax.broadcasted_iota(jnp.int32, (TM, 1), 0) & (BQ - 1)).astype(F32)
        dec = lg_ref[...] * nloc
        rq = _rope_apply(retv[0], cos, sin, first, second, HEAD_DIM // 2, 1.0)
        rk = _rope_apply(retv[1], cos, sin, first, second, HEAD_DIM // 2, 1.0)
        oret_ref[:, 0:HP] = (rq * jnp.exp(dec)).astype(BF16)
        rk = rk * 0.125 * jnp.exp(-dec)
        oret_ref[:, HP:2 * HP] = rk.astype(BF16)
        _transposed_tiles(orett_ref, 0, [rk], BQ)
        oret_ref[:, 2 * HP:QKV] = retv[2].astype(BF16)
        cosm, sinm, firstm, secondm = _rope_tables(trig, HP, HEAD_DIM, ROPE_MLA // 2)
        cqn = _rms(cq_ref[...], gq_ref[...]).astype(BF16)
        qm = _bdot(cqn, wq_ref[...])
        qm = _rope_apply(qm, cosm, sinm, firstm, secondm, ROPE_MLA // 2, 1.0)
        omla_ref[:, 0:HP] = (qm * MLA_SCALE).astype(BF16)
        ckvn = _rms(ckv_ref[...], gkv_ref[...]).astype(BF16)
        kv = _bdot(ckvn, wkv_ref[...])
        krr = _rope_apply(kr_ref[...], cosm[:, 0:LANES], sinm[:, 0:LANES], firstm[:, 0:LANES],
                          secondm[:, 0:LANES], ROPE_MLA // 2, 1.0)
        mk = kv[:, 0:HP] + jnp.concatenate([krr] * HEADS, axis=1)
        omla_ref[:, HP:2 * HP] = mk.astype(BF16)
        omla_ref[:, 2 * HP:QKV] = kv[:, HP:2 * HP].astype(BF16)
        _transposed_tiles(omlat_ref, 0, [mk, kv[:, HP:2 * HP]], BQ)

    def seg(off, w):
        return pl.BlockSpec((TM, w), lambda i, o=off // w: (i, o))

    def full(shape):
        return pl.BlockSpec(shape, lambda i: (0,) * len(shape))

    def tiles(width):
        return pl.BlockSpec((TM // width, 2 * HP, width), lambda i: (i, 0, 0))

    in_specs = [seg(OFF_FOX, QKV_IN), seg(OFF_RET, QKV_IN), seg(OFF_SB, QKV_IN), seg(OFF_CQ, Q_RANK),
                seg(OFF_CKV, LANES), seg(OFF_KR, LANES), _row_spec(LANES), pl.BlockSpec((TM, 1), lambda i: (i, 0)),
                full((1, LANES)), full((1, HP)), full((1, Q_RANK)), full((1, KV_RANK)),
                full((Q_RANK, HP)), full((KV_RANK, 2 * HP))]
    out_specs = [_row_spec(QKV)] * 4 + [tiles(BQ), tiles(TKS), tiles(BQ),
                                        pl.BlockSpec((TM // BQ, HP, BQ), lambda i: (i, 0, 0))]
    out_shape = [jax.ShapeDtypeStruct((t, QKV), BF16)] * 4 + [
        jax.ShapeDtypeStruct((t // BQ, 2 * HP, BQ), BF16), jax.ShapeDtypeStruct((t // TKS, 2 * HP, TKS), BF16),
        jax.ShapeDtypeStruct((t // BQ, 2 * HP, BQ), BF16), jax.ShapeDtypeStruct((t // BQ, HP, BQ), BF16)]
    return pl.pallas_call(
        body, name=name, grid=(t // TM,), in_specs=in_specs, out_specs=out_specs, out_shape=out_shape,
        compiler_params=_cp("parallel"))(proj, proj, proj, proj, proj, proj, cum, pos, invf, lg_lanes,
                                         g_q, g_kv, wq_pad, wkv_pad)


def _prep_bwd(dfox, dret, dsb, dmla, drg, dlsf, proj, pos, invf, lg_lanes, b_pad, g_q, g_kv,
              wq_pad, wkv_pad, name):
    t = proj.shape[0]

    def body(dfq, dfk, dfv, drq, drk, drv, dsq, dsk, dsv, dmq, dmk, dmv, drg_ref, dlsf_ref,
             cq_ref, ckv_ref, ff_ref, pos_ref, invf_ref, lg_ref, b_ref, gq_ref, gkv_ref, wq_ref, wkv_ref,
             dp_ref, dwq_ref, dwkv_ref, dgq_ref, dgkv_ref, dbf_ref):
        @pl.when(pl.program_id(0) == 0)
        def _():
            dwq_ref[...] = jnp.zeros_like(dwq_ref)
            dwkv_ref[...] = jnp.zeros_like(dwkv_ref)
            dgq_ref[...] = jnp.zeros_like(dgq_ref)
            dgkv_ref[...] = jnp.zeros_like(dgkv_ref)
            dbf_ref[...] = jnp.zeros_like(dbf_ref)

        pos_v = pos_ref[...]
        def put(off, val):
            dp_ref[:, off:off + GROUP] = _gather_heads(val).astype(BF16)

        for off, (dq, dk, dv) in ((OFF_FOX, (dfq, dfk, dfv)), (OFF_SB, (dsq, dsk, dsv))):
            put(off, dq[...] * 0.125)
            put(off + GROUP, dk[...])
            put(off + 2 * GROUP, dv[...])
        trig = _rope_trig(pos_v, invf_ref[...])
        cos, sin, first, second = _rope_tables(trig, HP, 0, HEAD_DIM // 2)
        nloc = (lax.broadcasted_iota(jnp.int32, (TM, 1), 0) & (BQ - 1)).astype(F32)
        dec = lg_ref[...] * nloc
        dq = _rope_apply(drq[...] * jnp.exp(dec), cos, sin, first, second, HEAD_DIM // 2, -1.0)
        dk = _rope_apply(drk[...] * (0.125 * jnp.exp(-dec)), cos, sin, first, second, HEAD_DIM // 2, -1.0)
        put(OFF_RET, dq)
        put(OFF_RET + GROUP, dk)
        put(OFF_RET + 2 * GROUP, drv[...])
        put(OFF_RG, drg_ref[...])
        cosm, sinm, firstm, secondm = _rope_tables(trig, HP, HEAD_DIM, ROPE_MLA // 2)
        dql = _rope_apply(dmq[...] * MLA_SCALE, cosm, sinm, firstm, secondm, ROPE_MLA // 2, -1.0).astype(BF16)
        cq = cq_ref[...]
        cqn = _rms(cq, gq_ref[...]).astype(BF16)
        dwq_ref[...] += _bdot(cqn, dql, TN)
        dcqn = _bdot(dql, wq_ref[...], NT)
        dcq, dgq = _rms_bwd_vals(dcqn, cq, gq_ref[...])
        dgq_ref[...] += dgq
        dp_ref[:, OFF_CQ:OFF_CQ + Q_RANK] = dcq.astype(BF16)
        dkm = dmk[...]
        dkv = jnp.concatenate([dkm, dmv[...]], axis=1).astype(BF16)
        ckv = ckv_ref[...]
        ckvn = _rms(ckv, gkv_ref[...]).astype(BF16)
        dwkv_ref[...] += _bdot(ckvn, dkv, TN)
        dckvn = _bdot(dkv, wkv_ref[...], NT)
        dckv, dgkv = _rms_bwd_vals(dckvn, ckv, gkv_ref[...])
        dgkv_ref[...] += dgkv
        dp_ref[:, OFF_CKV:OFF_CKV + LANES] = dckv.astype(BF16)
        dkr = dkm[:, 0:LANES] + dkm[:, LANES:2 * LANES] + dkm[:, 2 * LANES:3 * LANES] + dkm[:, 3 * LANES:HP]
        act = firstm[:, 0:LANES] | secondm[:, 0:LANES]
        dkr = jnp.where(act, dkr, 0.0)
        dkr = _rope_apply(dkr, cosm[:, 0:LANES], sinm[:, 0:LANES], firstm[:, 0:LANES], secondm[:, 0:LANES],
                          ROPE_MLA // 2, -1.0)
        dp_ref[:, OFF_KR:OFF_KR + LANES] = dkr.astype(BF16)
        f = ff_ref[...] + b_ref[...]
        dff = dlsf_ref[...] / (1.0 + jnp.exp(f))
        dbf_ref[...] += jnp.sum(dff, axis=0, keepdims=True)
        dp_ref[:, OFF_FF:OFF_FF + LANES] = dff.astype(BF16)
        dp_ref[:, OFF_FF + LANES:NP_IN] = jnp.zeros((TM, NP_IN - OFF_FF - LANES), BF16)

    def seg(off, w):
        return pl.BlockSpec((TM, w), lambda i, o=off // w: (i, o))

    def full(shape):
        return pl.BlockSpec(shape, lambda i: (0,) * len(shape))

    hp_spec = _row_spec(HP)
    in_specs = [hp_spec] * 13 + [_row_spec(LANES), seg(OFF_CQ, Q_RANK), seg(OFF_CKV, LANES), seg(OFF_FF, LANES),
                                 pl.BlockSpec((TM, 1), lambda i: (i, 0)),
                                 full((1, LANES)), full((1, HP)), full((1, LANES)), full((1, Q_RANK)),
                                 full((1, KV_RANK)), full((Q_RANK, HP)), full((KV_RANK, 2 * HP))]
    out_specs = [_row_spec(NP_IN), full((Q_RANK, HP)), full((KV_RANK, 2 * HP)), full((1, Q_RANK)),
                 full((1, KV_RANK)), full((1, LANES))]
    out_shape = [jax.ShapeDtypeStruct((t, NP_IN), BF16), jax.ShapeDtypeStruct((Q_RANK, HP), F32),
                 jax.ShapeDtypeStruct((KV_RANK, 2 * HP), F32), jax.ShapeDtypeStruct((1, Q_RANK), F32),
                 jax.ShapeDtypeStruct((1, KV_RANK), F32), jax.ShapeDtypeStruct((1, LANES), F32)]
    return pl.pallas_call(
        body, name=name, grid=(t // TM,), in_specs=in_specs, out_specs=out_specs, out_shape=out_shape,
        compiler_params=_cp("arbitrary"))(*dfox, *dret, *dsb, *dmla, drg, dlsf, proj, proj, proj, pos, invf,
                                          lg_lanes, b_pad, g_q, g_kv, wq_pad, wkv_pad)


TC = 1024


def _cumsum(x, reverse, name, partials=None, forget_bias=None):
    t = x.shape[0]
    w = LANES
    tc = min(TC, t)
    n = t // tc
    xs = [x] + ([] if partials is None else [partials]) + ([] if forget_bias is None else [forget_bias])

    def body(*refs):
        x_refs, o_ref, carry = refs[:len(xs)], refs[len(xs)], refs[len(xs) + 1]

        @pl.when(pl.program_id(0) == 0)
        def _():
            carry[...] = jnp.zeros_like(carry)

        r = lax.broadcasted_iota(jnp.int32, (tc, tc), 0)
        c = lax.broadcasted_iota(jnp.int32, (tc, tc), 1)
        tri = jnp.where((r <= c) if reverse else (r >= c), 1.0, 0.0).astype(BF16)
        v = x_refs[0][...]
        if forget_bias is not None:
            f = v + x_refs[-1][...]
            v = -(jnp.maximum(-f, 0.0) + jnp.log(1.0 + jnp.exp(-jnp.abs(f))))
        if partials is not None:
            lane = lax.broadcasted_iota(jnp.int32, (tc, LANES), 1)
            for hb in range(HEADS):
                v = v + jnp.where(lane == hb, jnp.sum(x_refs[1][:, _hs(hb)], axis=1, keepdims=True), 0.0)
        hi = v.astype(BF16)
        r1 = v - hi.astype(F32)
        mid = r1.astype(BF16)
        lo = (r1 - mid.astype(F32)).astype(BF16)
        cs = _bdot(tri, hi) + _bdot(tri, mid) + _bdot(tri, lo) + carry[...]
        o_ref[...] = cs
        carry[...] = cs[0:1, :] if reverse else cs[tc - 1:tc, :]

    step = (lambda i: n - 1 - i) if reverse else (lambda i: i)
    x_col = OFF_FF // LANES if forget_bias is not None else 0
    in_specs = [pl.BlockSpec((tc, LANES), lambda i: (step(i), x_col))]
    if partials is not None:
        in_specs.append(pl.BlockSpec((tc, HP), lambda i: (step(i), 0)))
    if forget_bias is not None:
        in_specs.append(_vec_spec(LANES))
    return pl.pallas_call(
        body, name=name, grid=(n,), in_specs=in_specs, out_specs=pl.BlockSpec((tc, w), lambda i: (step(i), 0)),
        out_shape=jax.ShapeDtypeStruct((t, w), F32), scratch_shapes=[pltpu.VMEM((1, w), F32)],
        compiler_params=_cp("arbitrary"))(*xs)


HB_FWD = 4
HB_BWD = 4
HB_SB_FWD = 4
BQS = 512
BQA = 512


def _q_spec(hb, bq=BQ):
    return pl.BlockSpec((bq, hb * LANES), lambda g, i: (i, g))


ONE_BUFFER = pl.Buffered(1)


def _kv_spec(t, which, hb):
    return pl.BlockSpec((t, hb * LANES), lambda g, i, w=which: (0, w * (HEADS // hb) + g), pipeline_mode=ONE_BUFFER)


def _acc_spec(t, hb):
    return pl.BlockSpec((t, hb * LANES), lambda g, i: (0, g), pipeline_mode=ONE_BUFFER)


def _hs(hh):
    return slice(hh * LANES, (hh + 1) * LANES)


def _tile_iota(rows, cols):
    return (lax.broadcasted_iota(jnp.int32, (rows, cols), 0), lax.broadcasted_iota(jnp.int32, (rows, cols), 1))


def _kvt_spec(nkv, width, which, hb):
    return pl.BlockSpec((nkv, hb * LANES, width), lambda g, i, w=which: (0, w * (HEADS // hb) + g, 0),
                        pipeline_mode=ONE_BUFFER)


def _qrow_spec(hb, bq=BQ):
    return pl.BlockSpec((hb, 1, 1, bq), lambda g, i: (g, i, 0, 0))


def _vis(key0, query0, rows, cols, kind):
    r, c = _tile_iota(rows, cols)
    k, q = key0 + r, query0 + c
    if kind == "chunk":
        return (k >> CHUNK_SHIFT) <= (q >> CHUNK_SHIFT)
    return (k < q) if kind == "strict" else (k <= q)


def _softmax_fwd(qkv, kvt, *, chunk_mask, scale, name):
    t = qkv.shape[0]
    nq = t // BQA
    per = BQA // BQ
    hb = HB_FWD
    kind = "chunk" if chunk_mask else "causal"

    def body(q_ref, k_ref, vt_ref, o_ref, lse_ref, m_sc, l_sc, acc_sc):
        i = pl.program_id(1)
        m_sc[...] = jnp.full((hb, 1, BQA), NEG, F32)
        l_sc[...] = jnp.zeros((hb, 1, BQA), F32)
        acc_sc[...] = jnp.zeros((hb, LANES, BQA), F32)

        def tile(j, qoff):
            off = pl.multiple_of(j * BQ, BQ)
            lo = 0 if qoff is None else qoff
            qs = slice(lo, BQA)
            vis = None if qoff is None else _vis(off, i * BQA + lo, BQ, BQA - lo, kind)
            ss = [_bdot(k_ref[pl.ds(off, BQ), _hs(hh)], q_ref[qs, _hs(hh)], NT) for hh in range(hb)]
            for hh in range(hb):
                s = ss[hh]
                if scale != 1.0:
                    s = s * scale
                if vis is not None:
                    s = jnp.where(vis, s, NEG)
                m_old = m_sc[hh, :, qs]
                m_new = jnp.maximum(m_old, jnp.max(s, axis=0, keepdims=True))
                alpha = jnp.exp(m_old - m_new)
                p = jnp.exp(s - m_new)
                l_sc[hh, :, qs] = alpha * l_sc[hh, :, qs] + jnp.sum(p, axis=0, keepdims=True)
                m_sc[hh, :, qs] = m_new
                acc_sc[hh, :, qs] = alpha * acc_sc[hh, :, qs] + _bdot(vt_ref[j, _hs(hh), :], p.astype(BF16))

        def loop(j, carry):
            tile(j, None)
            return carry

        lax.fori_loop(0, per * i, loop, 0)
        for d in range(per):
            tile(per * i + d, d * BQ)
        for hh in range(hb):
            l = l_sc[hh]
            o_ref[:, _hs(hh)] = (acc_sc[hh] / l).T
            lse_ref[hh, 0] = m_sc[hh] + jnp.log(l)

    return pl.pallas_call(
        body, name=name, grid=(HEADS // hb, nq),
        in_specs=[_q_spec(hb, BQA), _kv_spec(t, 1, hb), _kvt_spec(t // BQ, BQ, 1, hb)],
        out_specs=[_q_spec(hb, BQA), _qrow_spec(hb, BQA)],
        out_shape=[jax.ShapeDtypeStruct((t, HP), F32), jax.ShapeDtypeStruct((HEADS, nq, 1, BQA), F32)],
        scratch_shapes=[pltpu.VMEM((hb, 1, BQA), F32), pltpu.VMEM((hb, 1, BQA), F32),
                        pltpu.VMEM((hb, LANES, BQA), F32)],
        compiler_params=_cp("parallel", "arbitrary"))(qkv, qkv, kvt)


def _softmax_bwd(qkv, kvt, do, lse, delta, *, bias, chunk_mask, scale, name):
    t = qkv.shape[0]
    nq = t // BQA
    per = BQA // BQ
    hb = HB_BWD
    kind = "chunk" if chunk_mask else "causal"

    def body(*refs):
        if bias:
            (q_ref, k_ref, v_ref, kt_ref, do_ref, lse_ref, dl_ref, dq_ref, dk_ref, dv_ref, dck_ref, dcq_ref,
             dq_sc, dcq_sc) = refs
            dcq_sc[...] = jnp.zeros((hb, 1, BQA), F32)
        else:
            q_ref, k_ref, v_ref, kt_ref, do_ref, lse_ref, dl_ref, dq_ref, dk_ref, dv_ref, dq_sc = refs
        i = pl.program_id(1)

        @pl.when(i == 0)
        def _():
            dk_ref[...] = jnp.zeros_like(dk_ref)
            dv_ref[...] = jnp.zeros_like(dv_ref)
            if bias:
                dck_ref[...] = jnp.zeros_like(dck_ref)

        dq_sc[...] = jnp.zeros((hb, LANES, BQA), F32)

        def tile(j, qoff):
            off = pl.multiple_of(j * BQ, BQ)
            lo = 0 if qoff is None else qoff
            qsl = slice(lo, BQA)
            vis = None if qoff is None else _vis(off, i * BQA + lo, BQ, BQA - lo, kind)
            qs = [q_ref[qsl, _hs(hh)] for hh in range(hb)]
            dobs = [do_ref[qsl, _hs(hh)].astype(BF16) for hh in range(hb)]
            ss = [_bdot(k_ref[pl.ds(off, BQ), _hs(hh)], qs[hh], NT) for hh in range(hb)]
            dps = [_bdot(v_ref[pl.ds(off, BQ), _hs(hh)], dobs[hh], NT) for hh in range(hb)]
            pbs, dsbs = [], []
            for hh in range(hb):
                s = ss[hh]
                if scale != 1.0:
                    s = s * scale
                p = jnp.exp(s - lse_ref[hh, 0, :, qsl])
                if vis is not None:
                    p = jnp.where(vis, p, 0.0)
                ds = p * (dps[hh] - dl_ref[hh, 0, :, qsl])
                if bias:
                    part = ds[:, 0:LANES]
                    for b in range(1, (BQA - lo) // LANES):
                        part = part + ds[:, b * LANES:(b + 1) * LANES]
                    dck_ref[pl.ds(off, BQ), _hs(hh)] -= part
                    dcq_sc[hh, :, qsl] += jnp.sum(ds, axis=0, keepdims=True)
                if scale != 1.0:
                    ds = ds * scale
                pbs.append(p.astype(BF16))
                dsbs.append(ds.astype(BF16))
            for hh in range(hb):
                sl = _hs(hh)
                dv_ref[pl.ds(off, BQ), sl] += _bdot(pbs[hh], dobs[hh])
                dk_ref[pl.ds(off, BQ), sl] += _bdot(dsbs[hh], qs[hh])
                dq_sc[hh, :, qsl] += _bdot(kt_ref[j, sl, :], dsbs[hh])

        def loop(j, carry):
            tile(j, None)
            return carry

        lax.fori_loop(0, per * i, loop, 0)
        for d in range(per):
            tile(per * i + d, d * BQ)
        for hh in range(hb):
            dq_ref[:, _hs(hh)] = dq_sc[hh].T
            if bias:
                dcq_ref[hh, 0] = dcq_sc[hh]

    in_specs = [_q_spec(hb, BQA), _kv_spec(t, 1, hb), _kv_spec(t, 2, hb), _kvt_spec(t // BQ, BQ, 0, hb),
                _q_spec(hb, BQA), _qrow_spec(hb, BQA), _qrow_spec(hb, BQA)]
    out_specs = [_q_spec(hb, BQA), _acc_spec(t, hb), _acc_spec(t, hb)]
    out_shape = [jax.ShapeDtypeStruct((t, HP), F32)] * 3
    scratch = [pltpu.VMEM((hb, LANES, BQA), F32)]
    if bias:
        out_specs += [_acc_spec(t, hb), _qrow_spec(hb, BQA)]
        out_shape += [jax.ShapeDtypeStruct((t, HP), F32), jax.ShapeDtypeStruct((HEADS, nq, 1, BQA), F32)]
        scratch.append(pltpu.VMEM((hb, 1, BQA), F32))
    return pl.pallas_call(
        body, name=name, grid=(HEADS // hb, nq), in_specs=in_specs, out_specs=out_specs, out_shape=out_shape,
        scratch_shapes=scratch,
        compiler_params=_cp("parallel", "arbitrary"))(qkv, qkv, qkv, kvt, do, lse, delta)


def _ret_diag_decay(lg1, keys_on_rows=False):
    r, c = _tile_iota(BQ, BQ)
    qn, km = (c, r) if keys_on_rows else (r, c)
    dd = jnp.where(km > qn, jnp.exp((2.0 * lg1) * (km - qn).astype(F32)), 1.0)
    return jnp.where((km >> CHUNK_SHIFT) <= (qn >> CHUNK_SHIFT), dd, 0.0)


def _lg_spec(hb):
    return pl.BlockSpec((hb, 1, LANES), lambda g, i: (g, 0, 0))


def _ret_specs(nq, hb, reverse):
    tile = (lambda i: nq - 1 - i) if reverse else (lambda i: i)
    qkv = [pl.BlockSpec((BQ, hb * LANES), lambda g, i, w=w: (tile(i), w * (HEADS // hb) + g)) for w in range(3)]
    kt = pl.BlockSpec((1, hb * LANES, BQ), lambda g, i: (tile(i), g, 0))
    st = pl.BlockSpec((1, hb * LANES, LANES), lambda g, i: (tile(i), g, 0))
    return qkv, kt, st


def _ret_fwd(qkv, kt, lg_heads, name):
    t = qkv.shape[0]
    nq = t // BQ
    hb = HB_FWD

    def body(lg_ref, q_ref, k_ref, v_ref, kt_ref, o_ref, st_ref, s_sc):
        @pl.when(pl.program_id(1) == 0)
        def _():
            s_sc[...] = jnp.zeros_like(s_sc)

        qs = [q_ref[:, _hs(hh)] for hh in range(hb)]
        vs = [v_ref[:, _hs(hh)] for hh in range(hb)]
        aa = [_bdot(qs[hh], k_ref[:, _hs(hh)], NT) for hh in range(hb)]
        kv = [_bdot(kt_ref[0, _hs(hh), :], vs[hh]) for hh in range(hb)]
        for hh in range(hb):
            sl = _hs(hh)
            lg1 = lg_ref[hh][:, 0:1]
            s = s_sc[hh]
            st_ref[0, sl, :] = s
            shi, slo = _split2(s)
            a = (aa[hh] * _ret_diag_decay(lg1)).astype(BF16)
            o_ref[:, sl] = _bdot(a, vs[hh]) + _bdot(qs[hh], shi) + _bdot(qs[hh], slo)
            s_sc[hh] = jnp.exp(lg1 * float(BQ)) * (s + kv[hh])

    qkv_specs, kt_spec, st_spec = _ret_specs(nq, hb, False)
    return pl.pallas_call(
        body, name=name, grid=(HEADS // hb, nq), in_specs=[_lg_spec(hb)] + qkv_specs + [kt_spec],
        out_specs=[_q_spec(hb), st_spec],
        out_shape=[jax.ShapeDtypeStruct((t, HP), F32), jax.ShapeDtypeStruct((nq, HP, LANES), F32)],
        scratch_shapes=[pltpu.VMEM((hb, LANES, LANES), F32)],
        compiler_params=_cp("parallel", "arbitrary"))(lg_heads, qkv, qkv, qkv, kt)


def _ret_bwd(qkv, states, lg_heads, do, name):
    t = qkv.shape[0]
    nq = t // BQ
    hb = HB_FWD

    def body(lg_ref, q_ref, k_ref, v_ref, st_ref, do_ref, dq_ref, dk_ref, dv_ref, g_sc):
        @pl.when(pl.program_id(1) == 0)
        def _():
            g_sc[...] = jnp.zeros_like(g_sc)

        qs = [q_ref[:, _hs(hh)] for hh in range(hb)]
        ks = [k_ref[:, _hs(hh)] for hh in range(hb)]
        vs = [v_ref[:, _hs(hh)] for hh in range(hb)]
        dobs = [do_ref[:, _hs(hh)].astype(BF16) for hh in range(hb)]
        aa = [_bdot(ks[hh], qs[hh], NT) for hh in range(hb)]
        das = [_bdot(vs[hh], dobs[hh], NT) for hh in range(hb)]
        qdo = [_bdot(qs[hh], dobs[hh], TN) for hh in range(hb)]
        for hh in range(hb):
            sl = _hs(hh)
            lg1 = lg_ref[hh][:, 0:1]
            dd = _ret_diag_decay(lg1, keys_on_rows=True)
            at = (aa[hh] * dd).astype(BF16)
            dat = (das[hh] * dd).astype(BF16)
            h = jnp.exp(lg1 * float(BQ)) * g_sc[hh]
            hhi, hlo = _split2(h)
            shi, slo = _split2(st_ref[0, sl, :])
            dv_ref[:, sl] = _bdot(at, dobs[hh]) + _bdot(ks[hh], hhi) + _bdot(ks[hh], hlo)
            dk_ref[:, sl] = _bdot(dat, qs[hh]) + _bdot(vs[hh], hhi, NT) + _bdot(vs[hh], hlo, NT)
            dq_ref[:, sl] = _bdot(dat, ks[hh], TN) + _bdot(dobs[hh], shi, NT) + _bdot(dobs[hh], slo, NT)
            g_sc[hh] = qdo[hh] + h

    qkv_specs, _, st_spec = _ret_specs(nq, hb, True)
    tile_spec = pl.BlockSpec((BQ, hb * LANES), lambda g, i: (nq - 1 - i, g))
    return pl.pallas_call(
        body, name=name, grid=(HEADS // hb, nq), in_specs=[_lg_spec(hb)] + qkv_specs + [st_spec, tile_spec],
        out_specs=[tile_spec] * 3, out_shape=[jax.ShapeDtypeStruct((t, HP), F32)] * 3,
        scratch_shapes=[pltpu.VMEM((hb, LANES, LANES), F32)],
        compiler_params=_cp("parallel", "arbitrary"))(lg_heads, qkv, qkv, qkv, states, do)


def _sb_tile_logs(q, kb, vis):
    z = _bdot(kb, q, NT)
    ls = -(jnp.maximum(z, 0.0) + jnp.log(1.0 + jnp.exp(-jnp.abs(z))))
    if vis is not None:
        ls = jnp.where(vis, ls, 0.0)
    return z, ls


def _sb_later(ls, after):
    hi, lo = _split2(ls)
    return _bdot(after, hi) + _bdot(after, lo)


def _sb_fwd(qkv, kvt, name):
    t = qkv.shape[0]
    nq = t // BQS
    per = BQS // TKS
    hb = HB_SB_FWD

    def body(q_ref, k_ref, vt_ref, o_ref, tot_ref, acc_sc, r_sc):
        i = pl.program_id(1)
        acc_sc[...] = jnp.zeros((hb, LANES, BQS), F32)
        r_sc[...] = jnp.zeros((hb, 1, BQS), F32)
        mr, mc = _tile_iota(TKS, TKS)
        after = jnp.where(mc > mr, 1.0, 0.0).astype(BF16)

        def tile(j, qoff):
            off = pl.multiple_of(j * TKS, TKS)
            lo = 0 if qoff is None else qoff
            qsl = slice(lo, BQS)
            vis = None if qoff is None else _vis(off, i * BQS + lo, TKS, BQS - lo, "strict")
            zl = [_sb_tile_logs(q_ref[qsl, _hs(hh)], k_ref[pl.ds(off, TKS), _hs(hh)], vis) for hh in range(hb)]
            laters = [_sb_later(zl[hh][1], after) for hh in range(hb)]
            ws = []
            for hh in range(hb):
                z, ls = zl[hh]
                w = jnp.exp(z + ls + laters[hh] + r_sc[hh, :, qsl])
                if vis is not None:
                    w = jnp.where(vis, w, 0.0)
                ws.append(w.astype(BF16))
                r_sc[hh, :, qsl] += jnp.sum(ls, axis=0, keepdims=True)
            for hh in range(hb):
                acc_sc[hh, :, qsl] += _bdot(vt_ref[j, _hs(hh), :], ws[hh])

        for d in reversed(range(per)):
            tile(per * i + d, d * TKS)

        def loop(jj, carry):
            tile(per * i - 1 - jj, None)
            return carry

        lax.fori_loop(0, per * i, loop, 0)
        for hh in range(hb):
            o_ref[:, _hs(hh)] = acc_sc[hh].T
            tot_ref[hh, 0] = r_sc[hh]

    return pl.pallas_call(
        body, name=name, grid=(HEADS // hb, nq),
        in_specs=[_q_spec(hb, BQS), _kv_spec(t, 1, hb), _kvt_spec(t // TKS, TKS, 1, hb)],
        out_specs=[_q_spec(hb, BQS), _qrow_spec(hb, BQS)],
        out_shape=[jax.ShapeDtypeStruct((t, HP), F32), jax.ShapeDtypeStruct((HEADS, nq, 1, BQS), F32)],
        scratch_shapes=[pltpu.VMEM((hb, LANES, BQS), F32), pltpu.VMEM((hb, 1, BQS), F32)],
        compiler_params=_cp("parallel", "arbitrary"))(qkv, qkv, kvt)


def _sb_bwd(qkv, kvt, do, tot, name):
    t = qkv.shape[0]
    nq = t // BQS
    per = BQS // TKS
    hb = HB_BWD

    def body(q_ref, k_ref, v_ref, kt_ref, do_ref, tot_ref, dq_ref, dk_ref, dv_ref, dq_sc, p_sc, g_sc):
        i = pl.program_id(1)

        @pl.when(i == 0)
        def _():
            dk_ref[...] = jnp.zeros_like(dk_ref)
            dv_ref[...] = jnp.zeros_like(dv_ref)

        dq_sc[...] = jnp.zeros((hb, LANES, BQS), F32)
        p_sc[...] = jnp.zeros((hb, 1, BQS), F32)
        g_sc[...] = jnp.zeros((hb, 1, BQS), F32)
        mr, mc = _tile_iota(TKS, TKS)
        after = jnp.where(mc > mr, 1.0, 0.0).astype(BF16)
        before = jnp.where(mc < mr, 1.0, 0.0).astype(BF16)

        def tile(j, qoff):
            off = pl.multiple_of(j * TKS, TKS)
            lo = 0 if qoff is None else qoff
            qsl = slice(lo, BQS)
            vis = None if qoff is None else _vis(off, i * BQS + lo, TKS, BQS - lo, "strict")
            qs = [q_ref[qsl, _hs(hh)] for hh in range(hb)]
            dobs = [do_ref[qsl, _hs(hh)].astype(BF16) for hh in range(hb)]
            zl = [_sb_tile_logs(qs[hh], k_ref[pl.ds(off, TKS), _hs(hh)], vis) for hh in range(hb)]
            dws = [_bdot(v_ref[pl.ds(off, TKS), _hs(hh)], dobs[hh], NT) for hh in range(hb)]
            laters = [_sb_later(zl[hh][1], after) for hh in range(hb)]
            ws, gs = [], []
            for hh in range(hb):
                z, ls = zl[hh]
                own = jnp.sum(ls, axis=0, keepdims=True)
                rest = tot_ref[hh, 0, :, qsl] - p_sc[hh, :, qsl] - own
                w = jnp.exp(z + ls + laters[hh] + rest)
                if vis is not None:
                    w = jnp.where(vis, w, 0.0)
                p_sc[hh, :, qsl] += own
                ws.append(w.astype(BF16))
                gs.append(dws[hh] * w)
            gins = [_bdot(before, gs[hh].astype(BF16)) for hh in range(hb)]
            dzbs = []
            for hh in range(hb):
                g = gs[hh]
                stay = jnp.exp(zl[hh][1])
                dz = g * stay - (1.0 - stay) * (gins[hh] + g_sc[hh, :, qsl])
                if vis is not None:
                    dz = jnp.where(vis, dz, 0.0)
                g_sc[hh, :, qsl] += jnp.sum(g, axis=0, keepdims=True)
                dzbs.append(dz.astype(BF16))
            for hh in range(hb):
                sl = _hs(hh)
                dv_ref[pl.ds(off, TKS), sl] += _bdot(ws[hh], dobs[hh])
                dk_ref[pl.ds(off, TKS), sl] += _bdot(dzbs[hh], qs[hh])
                dq_sc[hh, :, qsl] += _bdot(kt_ref[j, sl, :], dzbs[hh])

        def loop(j, carry):
            tile(j, None)
            return carry

        lax.fori_loop(0, per * i, loop, 0)
        for d in range(per):
            tile(per * i + d, d * TKS)
        for hh in range(hb):
            dq_ref[:, _hs(hh)] = dq_sc[hh].T

    return pl.pallas_call(
        body, name=name, grid=(HEADS // hb, nq),
        in_specs=[_q_spec(hb, BQS), _kv_spec(t, 1, hb), _kv_spec(t, 2, hb), _kvt_spec(t // TKS, TKS, 0, hb),
                  _q_spec(hb, BQS), _qrow_spec(hb, BQS)],
        out_specs=[_q_spec(hb, BQS), _acc_spec(t, hb), _acc_spec(t, hb)],
        out_shape=[jax.ShapeDtypeStruct((t, HP), F32)] * 3,
        scratch_shapes=[pltpu.VMEM((hb, LANES, BQS), F32), pltpu.VMEM((hb, 1, BQS), F32),
                        pltpu.VMEM((hb, 1, BQS), F32)],
        compiler_params=_cp("parallel", "arbitrary"))(qkv, qkv, qkv, kvt, do, tot)


def _sigmoid(v):
    return 1.0 / (1.0 + jnp.exp(-v))


def _post_fwd(oa, ob, oc, od, proj, g_pad, name):
    t = oa.shape[0]

    def body(oa_ref, ob_ref, oc_ref, od_ref, rg_ref, g_ref, mx_ref):
        g = g_ref[...]

        def group(o, gg):
            r = lax.rsqrt(jnp.sum(o * o, axis=-1, keepdims=True) * (1.0 / GROUP) + EPS)
            return _gather_heads(o * r * gg).astype(BF16)

        mx_ref[:, 0:GROUP] = group(oa_ref[...], g[:, 0:HP])
        mx_ref[:, GROUP:2 * GROUP] = group(ob_ref[...], g[:, HP:2 * HP])
        mx_ref[:, 3 * GROUP:4 * GROUP] = group(od_ref[...], g[:, 3 * HP:4 * HP])
        real = lax.broadcasted_iota(jnp.int32, (TM, LANES), 1) < HEAD_DIM
        rg = _spread_heads(rg_ref[...])
        gated = []
        for hb in range(HEADS):
            sl = slice(hb * LANES, (hb + 1) * LANES)
            o = oc_ref[:, sl]
            mu = jnp.sum(o, axis=-1, keepdims=True) * (1.0 / HEAD_DIM)
            dlt = jnp.where(real, o - mu, 0.0)
            var = jnp.sum(dlt * dlt, axis=-1, keepdims=True) * (1.0 / HEAD_DIM)
            yn = dlt * lax.rsqrt(var + EPS) * g[:, 2 * HP + hb * LANES:2 * HP + (hb + 1) * LANES]
            x = rg[:, sl]
            gated.append(yn * (x * _sigmoid(x)))
        mx_ref[:, 2 * GROUP:3 * GROUP] = _gather_heads(jnp.concatenate(gated, axis=1)).astype(BF16)

    rg_spec = pl.BlockSpec((TM, GROUP), lambda i: (i, OFF_RG // GROUP))
    return pl.pallas_call(
        body, name=name, grid=(t // TM,),
        in_specs=[_row_spec(HP)] * 4 + [rg_spec, _vec_spec(4 * HP)], out_specs=_row_spec(D_MODEL),
        out_shape=jax.ShapeDtypeStruct((t, D_MODEL), BF16), compiler_params=_cp("parallel"))(oa, ob, oc, od, proj, g_pad)


def _post_bwd(dmx, oa, ob, oc, od, proj, g_pad, name):
    t = oa.shape[0]

    def body(dm_ref, oa_ref, ob_ref, oc_ref, od_ref, rg_ref, g_ref,
             doa_ref, dob_ref, doc_ref, dod_ref, dla_ref, dlb_ref, drg_ref, dg_ref):
        @pl.when(pl.program_id(0) == 0)
        def _():
            dg_ref[...] = jnp.zeros_like(dg_ref)

        g = g_ref[...]

        def group_bwd(dm, o, gg):
            r = lax.rsqrt(jnp.sum(o * o, axis=-1, keepdims=True) * (1.0 / GROUP) + EPS)
            oh = o * r
            dgp = jnp.sum(dm * oh, axis=0, keepdims=True)
            dyh = dm * gg
            do = r * (dyh - oh * (jnp.sum(dyh * oh, axis=-1, keepdims=True) * (1.0 / GROUP)))
            return do, dgp

        def delta_bc(do, o):
            prod = do * o
            lane = lax.broadcasted_iota(jnp.int32, (TM, LANES), 1)
            out = jnp.zeros((TM, LANES), F32)
            for hb in range(HEADS):
                out = jnp.where(lane == hb, jnp.sum(prod[:, hb * LANES:(hb + 1) * LANES], axis=-1, keepdims=True), out)
            return out

        dmp = [_spread_heads(dm_ref[:, gi * GROUP:(gi + 1) * GROUP]) for gi in range(4)]
        rg = _spread_heads(rg_ref[...])
        oa = oa_ref[...]
        do_a, dga = group_bwd(dmp[0], oa, g[:, 0:HP])
        doa_ref[...] = do_a.astype(BF16)
        dla_ref[...] = delta_bc(do_a, oa)
        dg_ref[:, 0:HP] += dga
        ob = ob_ref[...]
        do_b, dgb = group_bwd(dmp[1], ob, g[:, HP:2 * HP])
        dob_ref[...] = do_b.astype(BF16)
        dlb_ref[...] = delta_bc(do_b, ob)
        dg_ref[:, HP:2 * HP] += dgb
        do_d, dgd = group_bwd(dmp[3], od_ref[...], g[:, 3 * HP:4 * HP])
        dod_ref[...] = do_d.astype(BF16)
        dg_ref[:, 3 * HP:4 * HP] += dgd
        real = lax.broadcasted_iota(jnp.int32, (TM, LANES), 1) < HEAD_DIM
        for hb in range(HEADS):
            sl = slice(hb * LANES, (hb + 1) * LANES)
            gsl = slice(2 * HP + hb * LANES, 2 * HP + (hb + 1) * LANES)
            o = oc_ref[:, sl]
            mu = jnp.sum(o, axis=-1, keepdims=True) * (1.0 / HEAD_DIM)
            dlt = jnp.where(real, o - mu, 0.0)
            var = jnp.sum(dlt * dlt, axis=-1, keepdims=True) * (1.0 / HEAD_DIM)
            rstd = lax.rsqrt(var + EPS)
            dhat = dlt * rstd
            gc = g[:, gsl]
            x = rg[:, sl]
            sg = _sigmoid(x)
            dm = dmp[2][:, sl]
            drg_ref[:, sl] = dm * (dhat * gc) * (sg * (1.0 + x * (1.0 - sg)))
            dyn = dm * (x * sg)
            dg_ref[:, gsl] += jnp.sum(dyn * dhat, axis=0, keepdims=True)
            ddh = dyn * gc
            m1 = jnp.sum(ddh, axis=-1, keepdims=True) * (1.0 / HEAD_DIM)
            m2 = jnp.sum(ddh * dhat, axis=-1, keepdims=True) * (1.0 / HEAD_DIM)
            doc_ref[:, sl] = jnp.where(real, rstd * (ddh - m1 - dhat * m2), 0.0).astype(BF16)

    rg_spec = pl.BlockSpec((TM, GROUP), lambda i: (i, OFF_RG // GROUP))
    hp = _row_spec(HP)
    return pl.pallas_call(
        body, name=name, grid=(t // TM,),
        in_specs=[_row_spec(D_MODEL), hp, hp, hp, hp, rg_spec, _vec_spec(4 * HP)],
        out_specs=[hp] * 4 + [_row_spec(LANES)] * 2 + [hp, _vec_spec(4 * HP)],
        out_shape=[jax.ShapeDtypeStruct((t, HP), BF16)] * 4 + [jax.ShapeDtypeStruct((t, LANES), F32)] * 2
        + [jax.ShapeDtypeStruct((t, HP), F32), jax.ShapeDtypeStruct((1, 4 * HP), F32)],
        compiler_params=_cp("arbitrary"))(dmx, oa, ob, oc, od, proj, g_pad)


def _mesh_pos():
    return lax.axis_index("x"), lax.axis_index("y"), lax.axis_index("c")


def _peer(pos, k):
    x, y, c = pos
    px = 1 - x if (k >> 2) & 1 else x
    py = 1 - y if (k >> 1) & 1 else y
    pc = 1 - c if k & 1 else c
    return (px, py, pc), 4 * px + 2 * py + pc


def _exchange(arrs, gather, name):
    n = len(arrs)

    def body(*refs):
        ins, outs = refs[:n], refs[n:2 * n]
        send_sems, recv_sems, loc_sems = refs[2 * n:]
        pos = _mesh_pos()
        me = 4 * pos[0] + 2 * pos[1] + pos[2]
        local = []
        for a in range(n):
            src = ins[a] if gather else ins[a].at[me]
            cp = pltpu.make_async_copy(src, outs[a].at[me], loc_sems.at[a])
            cp.start()
            local.append(cp)
        sends, recvs = [], []
        for k in range(1, N_DEV):
            peer, pid = _peer(pos, k)
            for a in range(n):
                s = a * (N_DEV - 1) + k - 1
                src = ins[a] if gather else ins[a].at[pid]
                cp = pltpu.make_async_remote_copy(
                    src_ref=src, dst_ref=outs[a].at[me], send_sem=send_sems.at[s], recv_sem=recv_sems.at[s],
                    device_id=peer, device_id_type=pl.DeviceIdType.MESH)
                cp.start()
                sends.append(cp)
                recvs.append(pltpu.make_async_remote_copy(
                    src_ref=src, dst_ref=outs[a].at[pid], send_sem=send_sems.at[s], recv_sem=recv_sems.at[s],
                    device_id=peer, device_id_type=pl.DeviceIdType.MESH))
        for cp in recvs:
            cp.wait_recv()
        for cp in sends:
            cp.wait_send()
        for cp in local:
            cp.wait()

    any_spec = pl.BlockSpec(memory_space=pl.ANY)
    out_shape = [jax.ShapeDtypeStruct((N_DEV,) + tuple(a.shape) if gather else tuple(a.shape), a.dtype) for a in arrs]
    return pl.pallas_call(
        body, name=name, in_specs=[any_spec] * n, out_specs=[any_spec] * n, out_shape=out_shape,
        scratch_shapes=[pltpu.SemaphoreType.DMA((n * (N_DEV - 1),)), pltpu.SemaphoreType.DMA((n * (N_DEV - 1),)),
                        pltpu.SemaphoreType.DMA((n,))],
        compiler_params=pltpu.CompilerParams(has_side_effects=True))(*arrs)


def _device_index():
    x, y, c = _mesh_pos()
    return 4 * x + 2 * y + c


def _landing(srcs, gather):
    me = _device_index()
    lands = []
    for a in srcs:
        own = a[None] if gather else lax.dynamic_slice_in_dim(a, me, 1, axis=0)
        shape = (N_DEV,) + tuple(a.shape) if gather else tuple(a.shape)
        lands.append(lax.dynamic_update_slice_in_dim(lax.empty(shape, a.dtype), own, me, axis=0))
    return lands


def _exchange_copies(ins, lands, send_sems, recv_sems, gather):
    pos = _mesh_pos()
    me = 4 * pos[0] + 2 * pos[1] + pos[2]
    sends, recvs = [], []
    for k in range(1, N_DEV):
        peer, pid = _peer(pos, k)
        for a in range(len(ins)):
            s = a * (N_DEV - 1) + k - 1
            src = ins[a] if gather else ins[a].at[pid]
            sends.append(pltpu.make_async_remote_copy(
                src_ref=src, dst_ref=lands[a].at[me], send_sem=send_sems.at[s], recv_sem=recv_sems.at[s],
                device_id=peer, device_id_type=pl.DeviceIdType.MESH))
            recvs.append(pltpu.make_async_remote_copy(
                src_ref=src, dst_ref=lands[a].at[pid], send_sem=send_sems.at[s], recv_sem=recv_sems.at[s],
                device_id=peer, device_id_type=pl.DeviceIdType.MESH))
    return sends, recvs


def _exchange_start(srcs, gather, name, after=None):
    n = len(srcs)
    lands = _landing(srcs, gather)
    nsem = n * (N_DEV - 1)
    extra = [] if after is None else [after]

    def body(*refs):
        ins, lnd = refs[:n], refs[n:2 * n]
        send_sems, recv_sems = refs[2 * n + len(extra)], refs[2 * n + len(extra) + 1]
        token = refs[-1]
        sends, _ = _exchange_copies(ins, lnd, send_sems, recv_sems, gather)
        for cp in sends:
            cp.start()
        token[...] = jnp.zeros_like(token)

    hbm = pl.BlockSpec(memory_space=pltpu.HBM)
    sem = pl.BlockSpec(memory_space=pltpu.SEMAPHORE)
    bufs = list(srcs) + lands
    out_shape = ([pltpu.SemaphoreType.DMA((nsem,)), pltpu.SemaphoreType.DMA((nsem,))]
                 + [pltpu.HBM(b.shape, b.dtype) for b in bufs] + [jax.ShapeDtypeStruct((8, LANES), F32)])
    outs = pl.pallas_call(
        body, name=name, in_specs=[hbm] * (2 * n) + [pl.BlockSpec(memory_space=pl.ANY)] * len(extra),
        out_specs=[sem, sem] + [hbm] * (2 * n) + [pl.BlockSpec(memory_space=pltpu.VMEM)], out_shape=out_shape,
        input_output_aliases={i: 2 + i for i in range(2 * n)},
        compiler_params=pltpu.CompilerParams(has_side_effects=pltpu.SideEffectType.DATAFLOW_SIDE_EFFECTING),
    )(*[pltpu.with_memory_space_constraint(b, pltpu.HBM) for b in bufs], *extra)
    return (outs[0], outs[1], outs[2:2 + n], outs[2 + n:2 + 2 * n]), outs[-1]


def _exchange_wait(state, after, gather, name):
    send_sems, recv_sems, srcs, lands = state
    n = len(srcs)
    after = list(after) if isinstance(after, (list, tuple)) else [after]

    def body(*refs):
        ins, lnd = refs[:n], refs[n:2 * n]
        s_sems, r_sems = refs[2 * n], refs[2 * n + 1]
        sends, recvs = _exchange_copies(ins, lnd, s_sems, r_sems, gather)
        for cp in sends:
            cp.wait_send()
        for cp in recvs:
            cp.wait_recv()

    hbm = pl.BlockSpec(memory_space=pltpu.HBM)
    sem = pl.BlockSpec(memory_space=pltpu.SEMAPHORE)
    bufs = list(srcs) + list(lands)
    outs = pl.pallas_call(
        body, name=name, in_specs=[hbm] * (2 * n) + [sem, sem] + [pl.BlockSpec(memory_space=pl.ANY)] * len(after),
        out_specs=[hbm] * (2 * n), out_shape=[pltpu.HBM(b.shape, b.dtype) for b in bufs],
        input_output_aliases={i: i for i in range(2 * n)},
        compiler_params=pltpu.CompilerParams(has_side_effects=pltpu.SideEffectType.DATAFLOW_SIDE_EFFECTING),
    )(*bufs, send_sems, recv_sems, *after)
    return outs[n:]


def _adam_vals(w, g, m, v):
    m = ADAM_B1 * m + (1.0 - ADAM_B1) * g
    v = ADAM_B2 * v + (1.0 - ADAM_B2) * (g * g)
    m_hat = m / ADAM_C1
    v_hat = v / ADAM_C2
    delta = -ADAM_LR * (m_hat / (jnp.sqrt(v_hat) + ADAM_EPS) + ADAM_WD * w)
    return delta, m, v


def _small_allreduce_adam(part, w, m, v, name):
    rows = part.shape[0]

    def body(p_ref, w_ref, m_ref, v_ref, g_ref, d_ref, nm_ref, nv_ref, gath, send_sems, recv_sems):
        pos = _mesh_pos()
        me = 4 * pos[0] + 2 * pos[1] + pos[2]
        gath[me] = p_ref[...]
        sends, recvs = [], []
        for k in range(1, N_DEV):
            peer, pid = _peer(pos, k)
            cp = pltpu.make_async_remote_copy(
                src_ref=p_ref, dst_ref=gath.at[me], send_sem=send_sems.at[k - 1], recv_sem=recv_sems.at[k - 1],
                device_id=peer, device_id_type=pl.DeviceIdType.MESH)
            cp.start()
            sends.append(cp)
            recvs.append(pltpu.make_async_remote_copy(
                src_ref=p_ref, dst_ref=gath.at[pid], send_sem=send_sems.at[k - 1], recv_sem=recv_sems.at[k - 1],
                device_id=peer, device_id_type=pl.DeviceIdType.MESH))
        for cp in recvs:
            cp.wait_recv()
        for cp in sends:
            cp.wait_send()
        g = gath[0]
        for p in range(1, N_DEV):
            g = g + gath[p]
        g_ref[...] = g
        d, nm, nv = _adam_vals(w_ref[...], g, m_ref[...], v_ref[...])
        d_ref[...] = d
        nm_ref[...] = nm
        nv_ref[...] = nv

    vm = pl.BlockSpec(memory_space=pltpu.VMEM)
    sds = jax.ShapeDtypeStruct((rows, LANES), F32)
    return pl.pallas_call(
        body, name=name, in_specs=[vm] * 4, out_specs=[vm] * 4, out_shape=[sds] * 4,
        scratch_shapes=[pltpu.VMEM((N_DEV, rows, LANES), F32), pltpu.SemaphoreType.DMA((N_DEV - 1,)),
                        pltpu.SemaphoreType.DMA((N_DEV - 1,))],
        compiler_params=pltpu.CompilerParams(has_side_effects=True))(part, w, m, v)


def _reduce_adam(recvs, w, m, v, name):
    shape = w.shape
    c = shape[-1]
    r = int(np.prod(shape[1:-1]))
    rs = [a.reshape(N_DEV, r, c) for a in recvs]
    w2, m2, v2 = w.reshape(DEPTH * r, c), m.reshape(DEPTH * r, c), v.reshape(DEPTH * r, c)
    tr = r
    while tr * c * 4 > (1 << 20) and tr % 16 == 0:
        tr //= 2
    nt = r // tr

    def body(*refs):
        r_refs = refs[:DEPTH]
        w_ref, m_ref, v_ref, g_ref, d_ref, nm_ref, nv_ref = refs[DEPTH:]
        for l, r_ref in enumerate(r_refs):
            @pl.when(pl.program_id(0) // nt == l)
            def _(r_ref=r_ref):
                g = r_ref[0].astype(F32)
                for p in range(1, N_DEV):
                    g = g + r_ref[p].astype(F32)
                g_ref[...] = g
                d, nm, nv = _adam_vals(w_ref[...], g, m_ref[...], v_ref[...])
                d_ref[...] = d
                nm_ref[...] = nm
                nv_ref[...] = nv

    recv_specs = [pl.BlockSpec((N_DEV, tr, c), lambda i, l=l: (0, jnp.clip(i - l * nt, 0, nt - 1), 0))
                  for l in range(DEPTH)]
    spec = pl.BlockSpec((tr, c), lambda i: (i, 0))
    sds = jax.ShapeDtypeStruct((DEPTH * r, c), F32)
    outs = pl.pallas_call(
        body, name=name, grid=(DEPTH * nt,), in_specs=recv_specs + [spec, spec, spec],
        out_specs=[spec] * 4, out_shape=[sds] * 4, compiler_params=_cp("arbitrary"))(*rs, w2, m2, v2)
    return [o.reshape(shape) for o in outs]


def _pad_heads(w, real=HEAD_DIM):
    lead = w.shape[:-1]
    w = w.reshape(lead + (HEADS, real))
    w = jnp.pad(w, [(0, 0)] * len(lead) + [(0, 0), (0, LANES - real)])
    return w.reshape(lead + (HP,))


def _unpad_heads(w, real=HEAD_DIM):
    lead = w.shape[:-1]
    return w.reshape(lead + (HEADS, LANES))[..., :real].reshape(lead + (HEADS * real,))


_IN_SEGS = (("fq", 0, 256), ("fk", 256, 512), ("fv", 512, 768), ("ff", 768, 772), ("cq", 772, 1028),
            ("ckv", 1028, 1156), ("kr", 1156, 1188), ("rq", 1188, 1444), ("rk", 1444, 1700), ("rv", 1700, 1956),
            ("rg", 1956, 2212), ("sq", 2212, 2468), ("sk", 2468, 2724), ("sv", 2724, 2980))


def _pad_w_in(w):
    s = {n: w[:, a:b] for n, a, b in _IN_SEGS}
    rows = w.shape[0]
    z = lambda n: jnp.zeros((rows, n), w.dtype)
    parts = [s[n] for n in ("fq", "fk", "fv", "rq", "rk", "rv", "sq", "sk", "sv", "rg", "cq", "ckv")]
    parts += [z(HEAD_DIM), s["kr"], z(LANES - HEAD_DIM - ROPE_MLA), s["ff"], z(LANES - HEADS),
              z(NP_IN - OFF_FF - LANES)]
    return jnp.concatenate(parts, axis=1)


def _unpad_w_in(wp):
    seg = lambda off, n=GROUP: wp[:, off:off + n]
    parts = [seg(OFF_FOX), seg(OFF_FOX + GROUP), seg(OFF_FOX + 2 * GROUP), seg(OFF_FF, HEADS),
             seg(OFF_CQ, Q_RANK), seg(OFF_CKV, KV_RANK), seg(OFF_KR + HEAD_DIM, ROPE_MLA),
             seg(OFF_RET), seg(OFF_RET + GROUP), seg(OFF_RET + 2 * GROUP), seg(OFF_RG),
             seg(OFF_SB), seg(OFF_SB + GROUP), seg(OFF_SB + 2 * GROUP)]
    return jnp.concatenate(parts, axis=1)


def _pad_w_kv(w):
    w4 = w.reshape(KV_RANK, HEADS, 2 * HEAD_DIM)
    k = w4[:, :, :HEAD_DIM].reshape(KV_RANK, GROUP)
    v = w4[:, :, HEAD_DIM:].reshape(KV_RANK, GROUP)
    return jnp.concatenate([_pad_heads(k), _pad_heads(v)], axis=1)


def _unpad_w_kv(wp):
    k = _unpad_heads(wp[:, :HP]).reshape(KV_RANK, HEADS, HEAD_DIM)
    v = _unpad_heads(wp[:, HP:]).reshape(KV_RANK, HEADS, HEAD_DIM)
    return jnp.concatenate([k, v], axis=-1).reshape(KV_RANK, HEADS * 2 * HEAD_DIM)


def _pad_gain_out(g):
    g = jnp.pad(g.reshape(4 * HEADS, HEAD_DIM), ((0, 0), (0, LANES - HEAD_DIM)))
    return g.reshape(1, 4 * HP)


def _unpad_gain_out(gp):
    return gp.reshape(4 * HEADS, LANES)[:, :HEAD_DIM].reshape(D_MODEL)


_SMALL = (("g_mix_pre", 1024), ("g_mix_post", 1024), ("g_ffn_pre", 1024), ("g_ffn_post", 1024), ("g_mix_out", 1024),
          ("g_q_lora", 256), ("g_kv_lora", 128), ("b_forget", 4))


def _pack_small(vals):
    parts = []
    for name, n in _SMALL:
        a = vals[name].astype(F32)
        if n < LANES:
            a = jnp.pad(a, ((0, 0), (0, LANES - n)))
        parts.append(a)
    return jnp.concatenate(parts, axis=1).reshape(DEPTH * SMALL_ROWS, LANES)


def _unpack_small(packed):
    flat = packed.reshape(DEPTH, SMALL_ROWS * LANES)
    out, off = {}, 0
    for name, n in _SMALL:
        out[name] = flat[:, off:off + n]
        off += max(n, LANES)
    return out


def kernel(x, positions, g_mix_pre, w_in, b_forget, g_q_lora, w_q_up, g_kv_lora, w_kv_up, g_mix_out, w_out, g_mix_post, g_ffn_pre, w_ffn_up, w_ffn_down, g_ffn_post, loss_target, m_g_mix_pre, m_w_in, m_b_forget, m_g_q_lora, m_w_q_up, m_g_kv_lora, m_w_kv_up, m_g_mix_out, m_w_out, m_g_mix_post, m_g_ffn_pre, m_w_ffn_up, m_w_ffn_down, m_g_ffn_post, v_g_mix_pre, v_w_in, v_b_forget, v_g_q_lora, v_w_q_up, v_g_kv_lora, v_w_kv_up, v_g_mix_out, v_w_out, v_g_mix_post, v_g_ffn_pre, v_w_ffn_up, v_w_ffn_down, v_g_ffn_post):
    t = x.shape[1]
    nq = t // BQ
    x0 = x[0]
    tgt = loss_target[0]
    pos = positions[0].astype(F32).reshape(t, 1)

    half_r, half_m = HEAD_DIM // 2, ROPE_MLA // 2
    invf_r = ROPE_BASE ** (-jnp.arange(half_r, dtype=F32) / half_r)
    invf_m = ROPE_BASE ** (-jnp.arange(half_m, dtype=F32) / half_m)
    invf = jnp.concatenate([invf_r, invf_r, invf_m, invf_m,
                            jnp.zeros((LANES - HEAD_DIM - ROPE_MLA,), F32)]).reshape(1, LANES)
    log_gamma = jnp.log1p(-jnp.power(2.0, -5.0 - jnp.arange(HEADS, dtype=F32)))
    lg_lanes = jnp.repeat(log_gamma, LANES).reshape(1, HP)
    lg_heads = jnp.broadcast_to(log_gamma[:, None, None], (HEADS, 1, LANES))

    big = [w_in, w_q_up, w_kv_up, w_out, w_ffn_up, w_ffn_down]
    bf = lambda w: w.astype(BF16)
    first = _exchange([bf(w_in[0])], True, "weights_gather_first")
    lora_state, lora_token = _exchange_start([bf(w_q_up), bf(w_kv_up)], True, "weights_gather_lora_start")
    l0_state, l0_token = _exchange_start([bf(w[0]) for w in (w_out, w_ffn_up, w_ffn_down)], True,
                                         "weights_gather_layer0_start", after=lora_token)
    l1_state, rest_token = _exchange_start([bf(w[1]) for w in (w_in, w_out, w_ffn_up, w_ffn_down)], True,
                                           "weights_gather_layer1_start", after=l0_token)
    row = lambda g: g.reshape(1, -1)
    layers = []
    for l in range(DEPTH):
        layers.append(dict(
            g_pre=row(g_mix_pre[l]), g_post=row(g_mix_post[l]), g_fpre=row(g_ffn_pre[l]), g_fpost=row(g_ffn_post[l]),
            g_out=_pad_gain_out(g_mix_out[l]), g_q=row(g_q_lora[l]), g_kv=row(g_kv_lora[l]),
            b_pad=jnp.pad(b_forget[l], (0, LANES - HEADS)).reshape(1, LANES)))
    layers[0]["win"] = _pad_w_in(first[0].reshape(D_MODEL, D_IN))

    saved = []
    xin = x0
    h = _rms_fwd(xin, layers[0]["g_pre"] + rest_token[0:1, 0:1], "rms_pre_0")
    loss_row = dx = None
    for l, p in enumerate(layers):
        s = dict(x=xin, h=h)
        proj = _mm(h, p["win"], name=f"in_proj_{l}", tm=512, tn=NP_IN)
        if l == 0:
            got = _exchange_wait(lora_state, proj, True, "weights_gather_lora_wait")
            wq_g = got[0].transpose(1, 2, 0, 3).reshape(DEPTH, Q_RANK, 384)
            wkv_g = got[1].transpose(1, 2, 0, 3).reshape(DEPTH, KV_RANK, 512)
            for ll in range(DEPTH):
                layers[ll].update(wq=_pad_heads(wq_g[ll], 96), wkv=_pad_w_kv(wkv_g[ll]))
        cum = _cumsum(proj, False, f"forget_cumsum_{l}", forget_bias=p["b_pad"])
        fox, ret, sb, mla, fox_t, sb_t, mla_t, ret_t = _prep_fwd(proj, cum, pos, invf, lg_lanes, p["g_q"],
                                                          p["g_kv"], p["wq"], p["wkv"], f"prep_fwd_{l}")
        oa, lse_a = _softmax_fwd(fox, fox_t, chunk_mask=False, scale=1.0, name=f"fox_fwd_{l}")
        ob, lse_b = _softmax_fwd(mla, mla_t, chunk_mask=True, scale=1.0, name=f"mla_fwd_{l}")
        oc, ret_st = _ret_fwd(ret, ret_t, lg_heads, f"ret_fwd_{l}")
        od, sb_tot = _sb_fwd(sb, sb_t, f"sb_fwd_{l}")
        if l == 0:
            got = _exchange_wait(l0_state, od, True, "weights_gather_layer0_wait")
            p.update(wout=got[0].reshape(D_MODEL, D_MODEL), wup=got[1].transpose(1, 0, 2).reshape(D_MODEL, D_FF),
                     wdn=got[2].reshape(D_FF, D_MODEL))
        mixed = _post_fwd(oa, ob, oc, od, proj, p["g_out"], f"post_fwd_{l}")
        mix = _mm(mixed, p["wout"], name=f"out_proj_{l}", tm=2048)
        x1, h2 = _add_rms_fwd(xin, mix, p["g_post"], p["g_fpre"], f"mix_residual_{l}")
        a = _mm(h2, p["wup"], name=f"ffn_up_{l}", out_dtype=BF16, tm=2048)
        y = _mm(a, p["wdn"], name=f"ffn_down_{l}", a_fn=_relu2, tk=D_FF)
        s.update(proj=proj, fox=fox, ret=ret, sb=sb, mla=mla, fox_t=fox_t, sb_t=sb_t, mla_t=mla_t, ret_st=ret_st, oa=oa, ob=ob, oc=oc,
                 od=od, sb_tot=sb_tot, lse_a=lse_a, lse_b=lse_b, mixed=mixed, mix=mix, x1=x1, h2=h2, a=a, y=y)
        saved.append(s)
        if l == 0:
            got = _exchange_wait(l1_state, y, True, "weights_gather_layer1_wait")
            layers[1].update(win=_pad_w_in(got[0].reshape(D_MODEL, D_IN)), wout=got[1].reshape(D_MODEL, D_MODEL),
                             wup=got[2].transpose(1, 0, 2).reshape(D_MODEL, D_FF), wdn=got[3].reshape(D_FF, D_MODEL))
        if l + 1 < DEPTH:
            xin, h = _add_rms_fwd(x1, y, p["g_fpost"], layers[l + 1]["g_pre"], f"ffn_residual_{l}")
        else:
            loss_row, dx = _final_loss(x1, y, p["g_fpost"], tgt, "loss")

    small_g = {n: [None] * DEPTH for n, _ in _SMALL}
    big_g = [[None] * DEPTH for _ in range(6)]
    to_send = [
        lambda g: g.reshape(N_DEV, 1, D_MODEL // N_DEV, D_IN),
        lambda g: g.reshape(Q_RANK, N_DEV, 384 // N_DEV).transpose(1, 0, 2)[:, None],
        lambda g: g.reshape(KV_RANK, N_DEV, 512 // N_DEV).transpose(1, 0, 2)[:, None],
        lambda g: g.reshape(N_DEV, 1, D_MODEL // N_DEV, D_MODEL),
        lambda g: g[:, None],
        lambda g: g.reshape(N_DEV, 1, D_FF // N_DEV, D_MODEL),
    ]
    send_of = lambda ks, l: [to_send[k](big_g[k][l]).astype(BF16) for k in ks]
    late_state = early_state = None
    order_token = jnp.zeros((1, 1), F32)
    for l in reversed(range(DEPTH)):
        p, s = layers[l], saved[l]
        dy, dg = _norm_bwd(dx, s["y"], p["g_fpost"] + order_token, None, BF16, f"ffn_post_bwd_{l}")
        small_g["g_ffn_post"][l] = dg
        da = _mm(dy, p["wdn"], name=f"ffn_down_dx_{l}", tb=True, out_dtype=BF16, epi=_drelu2, epi_in=s["a"],
                 tm=2048)
        big_g[5][l] = _mm(s["a"], dy, name=f"ffn_down_dw_{l}", ta=True, a_fn=_relu2, tk=t, out_dtype=BF16)
        big_g[4][l] = _mm(s["h2"], da, name=f"ffn_up_dw_{l}", ta=True, tk=t, tn=D_FF // N_DEV, out_dtype=BF16,
                          column_blocks=True)
        dh2 = _mm(da, p["wup"], name=f"ffn_up_dx_{l}", tb=True, tk=D_FF)
        dx1, dmix, small_g["g_ffn_pre"][l], small_g["g_mix_post"][l] = _norm_bwd_pair(
            dh2, s["x1"], p["g_fpre"], dx, s["mix"], p["g_post"], f"ffn_pre_mix_post_bwd_{l}")
        dmixed = _mm(dmix, p["wout"], name=f"out_proj_dx_{l}", tb=True, tm=2048)
        big_g[3][l] = _mm(s["mixed"], dmix, name=f"out_proj_dw_{l}", ta=True, tk=t, out_dtype=BF16)
        g_out = p["g_out"]
        if l == 0:
            early_state, early_token = _exchange_start(send_of((3, 4, 5), 0), False, "grads_layer0_early_start")
            g_out = g_out + early_token[0:1, 0:1]
        doa, dob, doc, dod, dla, dlb, drg, dgo = _post_bwd(dmixed, s["oa"], s["ob"], s["oc"], s["od"], s["proj"],
                                                           g_out, f"post_bwd_{l}")
        small_g["g_mix_out"][l] = _unpad_gain_out(dgo).reshape(1, D_MODEL)
        as_rows = lambda a: a[:, :HEADS].T.reshape(HEADS, t // BQA, 1, BQA)
        dfq, dfk, dfv, dcum_k, dcum_q = _softmax_bwd(s["fox"], s["fox_t"], doa, s["lse_a"], as_rows(dla), bias=True,
                                                     chunk_mask=False, scale=1.0, name=f"fox_bwd_{l}")
        dmq, dmk, dmv = _softmax_bwd(s["mla"], s["mla_t"], dob, s["lse_b"], as_rows(dlb), bias=False, chunk_mask=True,
                                     scale=1.0, name=f"mla_bwd_{l}")
        drq, drk, drv = _ret_bwd(s["ret"], s["ret_st"], lg_heads, doc, f"ret_bwd_{l}")
        dsq, dsk, dsv = _sb_bwd(s["sb"], s["sb_t"], dod, s["sb_tot"], f"sb_bwd_{l}")
        dcum_q = jnp.pad(dcum_q.reshape(HEADS, t).T, ((0, 0), (0, LANES - HEADS)))
        dlsf = _cumsum(dcum_q, True, f"forget_cumsum_bwd_{l}", partials=dcum_k)
        dproj, dwq, dwkv, dgq, dgkv, dbf = _prep_bwd(
            (dfq, dfk, dfv), (drq, drk, drv), (dsq, dsk, dsv), (dmq, dmk, dmv), drg, dlsf, s["proj"], pos, invf,
            lg_lanes, p["b_pad"], p["g_q"], p["g_kv"], p["wq"], p["wkv"], f"prep_bwd_{l}")
        small_g["g_q_lora"][l] = dgq
        small_g["g_kv_lora"][l] = dgkv
        small_g["b_forget"][l] = dbf[:, :HEADS]
        big_g[1][l] = _unpad_heads(dwq, 96)
        big_g[2][l] = _unpad_w_kv(dwkv)
        dh = _mm(dproj, p["win"], name=f"in_proj_dx_{l}", tb=True, tk=NP_IN)
        big_g[0][l] = _unpad_w_in(_mm(s["h"], dproj, name=f"in_proj_dw_{l}", ta=True, tn=NP_IN // 2, tk=2048,
                                          out_dtype=BF16))
        g_pre = p["g_pre"]
        if l == 0:
            last_state, last_token = _exchange_start(send_of((0, 1, 2), 0), False, "grads_layer0_rest_start")
            g_pre = g_pre + last_token[0:1, 0:1]
        dx, dg = _norm_bwd(dh, s["x"], g_pre, dx1, F32, f"mix_pre_bwd_{l}")
        small_g["g_mix_pre"][l] = dg
        if l == DEPTH - 1:
            late_state, late_token = _exchange_start(send_of(range(6), l), False, "grads_layer1_start")
            order_token = late_token[0:1, 0:1]
    grad_x = dx.reshape(1, t, D_MODEL)

    res = {}
    small_w = dict(g_mix_pre=g_mix_pre, g_mix_post=g_mix_post, g_ffn_pre=g_ffn_pre, g_ffn_post=g_ffn_post,
                   g_mix_out=g_mix_out, g_q_lora=g_q_lora, g_kv_lora=g_kv_lora, b_forget=b_forget)
    small_m = dict(g_mix_pre=m_g_mix_pre, g_mix_post=m_g_mix_post, g_ffn_pre=m_g_ffn_pre, g_ffn_post=m_g_ffn_post,
                   g_mix_out=m_g_mix_out, g_q_lora=m_g_q_lora, g_kv_lora=m_g_kv_lora, b_forget=m_b_forget)
    small_v = dict(g_mix_pre=v_g_mix_pre, g_mix_post=v_g_mix_post, g_ffn_pre=v_g_ffn_pre, g_ffn_post=v_g_ffn_post,
                   g_mix_out=v_g_mix_out, g_q_lora=v_g_q_lora, g_kv_lora=v_g_kv_lora, b_forget=v_b_forget)
    n_small = DEPTH * SMALL_ROWS
    extra = lambda a: jnp.concatenate([a, jnp.zeros((8, LANES), F32)], axis=0)
    part = jnp.concatenate([_pack_small({n: jnp.concatenate(small_g[n], axis=0) for n, _ in _SMALL}),
                            jnp.broadcast_to(loss_row, (8, LANES))], axis=0)
    sres = _small_allreduce_adam(part, extra(_pack_small(small_w)), extra(_pack_small(small_m)),
                                 extra(_pack_small(small_v)), "small_allreduce_adamw")
    loss = sres[0][n_small, 0]
    sg, sd, sm, sv = [_unpack_small(a[:n_small]) for a in sres]
    for n, _ in _SMALL:
        res[n] = [sg[n], sd[n], sm[n], sv[n]]

    late = _exchange_wait(late_state, dx, False, "grads_layer1_wait")
    early = _exchange_wait(early_state, dx, False, "grads_layer0_early_wait")
    ms = [m_w_in, m_w_q_up, m_w_kv_up, m_w_out, m_w_ffn_up, m_w_ffn_down]
    vs = [v_w_in, v_w_q_up, v_w_kv_up, v_w_out, v_w_ffn_up, v_w_ffn_down]
    names = ["w_in", "w_q_up", "w_kv_up", "w_out", "w_ffn_up", "w_ffn_down"]
    for k in (3, 4, 5):
        res[names[k]] = _reduce_adam([early[k - 3], late[k]], big[k], ms[k], vs[k], f"adamw_{names[k]}")
    done = [sres[0]] + [res[names[k]][0] for k in (3, 4, 5)]
    last = _exchange_wait(last_state, done, False, "grads_layer0_rest_wait")
    for k in (0, 1, 2):
        res[names[k]] = _reduce_adam([last[k], late[k]], big[k], ms[k], vs[k], f"adamw_{names[k]}")

    order = ["g_mix_pre", "w_in", "b_forget", "g_q_lora", "w_q_up", "g_kv_lora", "w_kv_up", "g_mix_out", "w_out",
             "g_mix_post", "g_ffn_pre", "w_ffn_up", "w_ffn_down", "g_ffn_post"]
    outs = [loss, grad_x]
    for idx in range(4):
        outs += [res[n][idx] for n in order]
    return tuple(outs)
```
